```python
import jax, jax.numpy as jnp
from jax import lax
import numpy as np

D_MODEL = 1024
BATCH = 8
SEQ = 4096
DEPTH = 1

CHUNK = 64
RET_HEADS = 4
RET_QK_DIM = 256
RET_V_DIM = 256
RET_QK = RET_HEADS * RET_QK_DIM
RET_V = RET_HEADS * RET_V_DIM
POOL_WINDOWS = (2, 4, 8, 16)
POOL_GROUPS = 4
POOL_GROUP_DIM = 256
POOL_WIDTH = POOL_GROUPS * POOL_GROUP_DIM
N_BRANCH = 2
IN_WIDTH = 2 * RET_QK + 2 * RET_V + POOL_WIDTH + N_BRANCH * D_MODEL
D_FF = 2816
ROPE_BASE = 10000.0
NORM_EPS = 1e-6
FFN_RES_WEIGHT = 0.5

kernel_name = "hybrid_retention_pool_macaron"


def rmsnorm(x, g):
    xf = x.astype(jnp.float32)
    y = xf * lax.rsqrt(jnp.mean(xf * xf, axis=-1, keepdims=True) + NORM_EPS)
    return (y * g.astype(jnp.float32)).astype(x.dtype)


def swiglu_ffn(x, w_in, w_out):
    gate, up = jnp.split(x @ w_in, 2, axis=-1)
    return (jax.nn.silu(gate) * up) @ w_out


def rotary(x):
    s, d = x.shape[1], x.shape[-1]
    half = d // 2
    inv_freq = ROPE_BASE ** (-jnp.arange(half, dtype=jnp.float32) / half)
    ang = jnp.arange(s, dtype=jnp.float32)[:, None] * inv_freq[None, :]
    cos = jnp.cos(ang)[None, :, None, :].astype(x.dtype)
    sin = jnp.sin(ang)[None, :, None, :].astype(x.dtype)
    x1, x2 = x[..., :half], x[..., half:]
    return jnp.concatenate([x1 * cos - x2 * sin, x1 * sin + x2 * cos], axis=-1)


def retention(q, k, v):
    b, s, h, dk = q.shape
    dv = v.shape[-1]
    nc = s // CHUNK
    log_gamma = jnp.log(1.0 - 2.0 ** (-5.0 - jnp.arange(h, dtype=jnp.float32)))
    idx = jnp.arange(CHUNK, dtype=jnp.float32)
    inner_decay = jnp.exp(log_gamma[:, None, None] * jnp.abs(idx[:, None] - idx[None, :]))
    q_decay = jnp.exp(log_gamma[None, :] * (idx[:, None] + 1.0))
    k_decay = jnp.exp(log_gamma[None, :] * (CHUNK - 1.0 - idx[:, None]))
    chunk_decay = jnp.exp(log_gamma * CHUNK)

    qc = q.reshape(b, nc, CHUNK, h, dk)
    kc = k.reshape(b, nc, CHUNK, h, dk)
    vc = v.reshape(b, nc, CHUNK, h, dv)

    scores = jnp.einsum("bnchd,bnshd->bnhcs", qc, kc) * inner_decay[None, None]
    inner = jnp.einsum("bnhcs,bnshe->bnche", scores, vc)

    def step(state, inp):
        q_i, k_i, v_i = inp
        cross = jnp.einsum("bchd,bhde->bche", q_i * q_decay[None, :, :, None], state)
        new_state = state * chunk_decay[None, :, None, None] + jnp.einsum(
            "bchd,bche->bhde", k_i * k_decay[None, :, :, None], v_i)
        return new_state, cross

    state0 = jnp.zeros((b, h, dk, dv), jnp.float32)
    xs = (qc.transpose(1, 0, 2, 3, 4), kc.transpose(1, 0, 2, 3, 4), vc.transpose(1, 0, 2, 3, 4))
    _, cross = lax.scan(step, state0, xs)
    out = inner + cross.transpose(1, 0, 2, 3, 4)
    return out.reshape(b, s, h, dv)


def multiscale_pool(p, w_group, scale):
    b, s, _ = p.shape
    pf = p.astype(jnp.float32).reshape(b, s, POOL_GROUPS, POOL_GROUP_DIM)
    cs = jnp.concatenate([jnp.zeros((b, 1, POOL_GROUPS, POOL_GROUP_DIM), jnp.float32),
                          jnp.cumsum(pf, axis=1)], axis=1)
    t = jnp.arange(s, dtype=jnp.float32)
    outs = []
    for g, w in enumerate(POOL_WINDOWS):
        cs_g = cs[:, :, g]
        shifted = jnp.pad(cs_g[:, : s + 1 - w], ((0, 0), (w - 1, 0), (0, 0)))
        count = jnp.minimum(t + 1.0, float(w))[None, :, None]
        outs.append((cs_g[:, 1:] - shifted) / count - pf[:, :, g])
    pooled = jnp.stack(outs, axis=2)
    mixed = jnp.einsum("bsgc,gcd->bsgd", pooled, w_group.astype(jnp.float32))
    return (mixed.reshape(b, s, POOL_WIDTH) * scale.astype(jnp.float32)).astype(p.dtype)


def _fwd_setup_inputs(seed: int = 0) -> dict:
    key = jax.random.key(seed)
    ks = jax.random.split(key, 20)
    f32 = jnp.float32

    def nrm(k, shape, fan_in):
        return jax.random.normal(k, shape, f32) * (fan_in ** -0.5)

    def gain(k, shape):
        return 1.0 + 0.02 * jax.random.normal(k, shape, f32)

    return {
        "x": jax.random.normal(ks[0], (BATCH, SEQ, D_MODEL), f32),
        "norm_ffn1": gain(ks[1], (DEPTH, D_MODEL)),
        "ffn1_w_in": nrm(ks[2], (DEPTH, D_MODEL, 2 * D_FF), D_MODEL),
        "ffn1_w_out": nrm(ks[3], (DEPTH, D_FF, D_MODEL), D_FF),
        "norm_mix": gain(ks[4], (DEPTH, D_MODEL)),
        "w_in": nrm(ks[5], (DEPTH, D_MODEL, IN_WIDTH), D_MODEL),
        "gate_bias": 0.02 * jax.random.normal(ks[6], (DEPTH, N_BRANCH, D_MODEL), f32),
        "pool_w": nrm(ks[7], (DEPTH, POOL_GROUPS, POOL_GROUP_DIM, POOL_GROUP_DIM), POOL_GROUP_DIM),
        "pool_scale": gain(ks[8], (DEPTH, POOL_WIDTH)),
        "w_ret_up": nrm(ks[9], (DEPTH, RET_V, D_MODEL), RET_V),
        "w_pool_up": nrm(ks[10], (DEPTH, POOL_WIDTH, D_MODEL), POOL_WIDTH),
        "w_out": nrm(ks[11], (DEPTH, D_MODEL, D_MODEL), D_MODEL),
        "norm_ffn2": gain(ks[12], (DEPTH, D_MODEL)),
        "ffn2_w_in": nrm(ks[13], (DEPTH, D_MODEL, 2 * D_FF), D_MODEL),
        "ffn2_w_out": nrm(ks[14], (DEPTH, D_FF, D_MODEL), D_FF),
        "norm_final": gain(ks[15], (D_MODEL,)),
    }


def _fwd_reference(x, norm_ffn1, ffn1_w_in, ffn1_w_out, norm_mix, w_in, gate_bias, pool_w,
              pool_scale, w_ret_up, w_pool_up, w_out, norm_ffn2, ffn2_w_in, ffn2_w_out,
              norm_final):
    b, s, _ = x.shape
    split_points = [RET_QK, 2 * RET_QK, 2 * RET_QK + RET_V, 2 * RET_QK + 2 * RET_V,
                    2 * RET_QK + 2 * RET_V + POOL_WIDTH]
    h = x
    for l in range(DEPTH):
        h = h + FFN_RES_WEIGHT * swiglu_ffn(rmsnorm(h, norm_ffn1[l]), ffn1_w_in[l], ffn1_w_out[l])

        u = rmsnorm(h, norm_mix[l])
        proj = u @ w_in[l]
        q, k, v, g_ret, p, gates = jnp.split(proj, split_points, axis=-1)
        q = rotary(q.reshape(b, s, RET_HEADS, RET_QK_DIM))
        k = rotary(k.reshape(b, s, RET_HEADS, RET_QK_DIM)) * (RET_QK_DIM ** -0.5)
        v = v.reshape(b, s, RET_HEADS, RET_V_DIM)
        ret = retention(q.astype(jnp.float32), k.astype(jnp.float32), v.astype(jnp.float32))
        ret = ret * lax.rsqrt(jnp.mean(ret * ret, axis=-1, keepdims=True) + NORM_EPS)
        ret = ret.reshape(b, s, RET_V).astype(proj.dtype) * jax.nn.silu(g_ret)

        pool_out = multiscale_pool(p, pool_w[l], pool_scale[l])

        gate = jax.nn.sigmoid(gates.reshape(b, s, N_BRANCH, D_MODEL) + gate_bias[l])
        merged = gate[:, :, 0] * (ret @ w_ret_up[l]) + gate[:, :, 1] * (pool_out @ w_pool_up[l])
        h = h + merged @ w_out[l]

        h = h + FFN_RES_WEIGHT * swiglu_ffn(rmsnorm(h, norm_ffn2[l]), ffn2_w_in[l], ffn2_w_out[l])
    return rmsnorm(h, norm_final)


import jax as _jax
import jax.numpy as _jnp

TWIN_FORMAT = 'train_step'
FWD_PARAMS = ['x', 'norm_ffn1', 'ffn1_w_in', 'ffn1_w_out', 'norm_mix', 'w_in', 'gate_bias', 'pool_w', 'pool_scale', 'w_ret_up', 'w_pool_up', 'w_out', 'norm_ffn2', 'ffn2_w_in', 'ffn2_w_out', 'norm_final']
TWIN_WEIGHTS = ['norm_ffn1', 'ffn1_w_in', 'ffn1_w_out', 'norm_mix', 'w_in', 'gate_bias', 'pool_w', 'pool_scale', 'w_ret_up', 'w_pool_up', 'w_out', 'norm_ffn2', 'ffn2_w_in', 'ffn2_w_out', 'norm_final']
TWIN_DIFF_INPUT = 'x'
TWIN_INPUTS = ['x', 'norm_ffn1', 'ffn1_w_in', 'ffn1_w_out', 'norm_mix', 'w_in', 'gate_bias', 'pool_w', 'pool_scale', 'w_ret_up', 'w_pool_up', 'w_out', 'norm_ffn2', 'ffn2_w_in', 'ffn2_w_out', 'norm_final', 'loss_target', 'm_norm_ffn1', 'm_ffn1_w_in', 'm_ffn1_w_out', 'm_norm_mix', 'm_w_in', 'm_gate_bias', 'm_pool_w', 'm_pool_scale', 'm_w_ret_up', 'm_w_pool_up', 'm_w_out', 'm_norm_ffn2', 'm_ffn2_w_in', 'm_ffn2_w_out', 'm_norm_final', 'v_norm_ffn1', 'v_ffn1_w_in', 'v_ffn1_w_out', 'v_norm_mix', 'v_w_in', 'v_gate_bias', 'v_pool_w', 'v_pool_scale', 'v_w_ret_up', 'v_w_pool_up', 'v_w_out', 'v_norm_ffn2', 'v_ffn2_w_in', 'v_ffn2_w_out', 'v_norm_final']
TWIN_OUTPUTS = ['loss', 'grad_x', 'grad_norm_ffn1', 'grad_ffn1_w_in', 'grad_ffn1_w_out', 'grad_norm_mix', 'grad_w_in', 'grad_gate_bias', 'grad_pool_w', 'grad_pool_scale', 'grad_w_ret_up', 'grad_w_pool_up', 'grad_w_out', 'grad_norm_ffn2', 'grad_ffn2_w_in', 'grad_ffn2_w_out', 'grad_norm_final', 'delta_norm_ffn1', 'delta_ffn1_w_in', 'delta_ffn1_w_out', 'delta_norm_mix', 'delta_w_in', 'delta_gate_bias', 'delta_pool_w', 'delta_pool_scale', 'delta_w_ret_up', 'delta_w_pool_up', 'delta_w_out', 'delta_norm_ffn2', 'delta_ffn2_w_in', 'delta_ffn2_w_out', 'delta_norm_final', 'new_m_norm_ffn1', 'new_m_ffn1_w_in', 'new_m_ffn1_w_out', 'new_m_norm_mix', 'new_m_w_in', 'new_m_gate_bias', 'new_m_pool_w', 'new_m_pool_scale', 'new_m_w_ret_up', 'new_m_w_pool_up', 'new_m_w_out', 'new_m_norm_ffn2', 'new_m_ffn2_w_in', 'new_m_ffn2_w_out', 'new_m_norm_final', 'new_v_norm_ffn1', 'new_v_ffn1_w_in', 'new_v_ffn1_w_out', 'new_v_norm_mix', 'new_v_w_in', 'new_v_gate_bias', 'new_v_pool_w', 'new_v_pool_scale', 'new_v_w_ret_up', 'new_v_w_pool_up', 'new_v_w_out', 'new_v_norm_ffn2', 'new_v_ffn2_w_in', 'new_v_ffn2_w_out', 'new_v_norm_final']
TWIN_LEAF_KINDS = {'loss': 'loss', 'grad_x': 'grad_x', 'grad_norm_ffn1': 'grad_w', 'grad_ffn1_w_in': 'grad_w', 'grad_ffn1_w_out': 'grad_w', 'grad_norm_mix': 'grad_w', 'grad_w_in': 'grad_w', 'grad_gate_bias': 'grad_w', 'grad_pool_w': 'grad_w', 'grad_pool_scale': 'grad_w', 'grad_w_ret_up': 'grad_w', 'grad_w_pool_up': 'grad_w', 'grad_w_out': 'grad_w', 'grad_norm_ffn2': 'grad_w', 'grad_ffn2_w_in': 'grad_w', 'grad_ffn2_w_out': 'grad_w', 'grad_norm_final': 'grad_w', 'delta_norm_ffn1': 'delta_w', 'delta_ffn1_w_in': 'delta_w', 'delta_ffn1_w_out': 'delta_w', 'delta_norm_mix': 'delta_w', 'delta_w_in': 'delta_w', 'delta_gate_bias': 'delta_w', 'delta_pool_w': 'delta_w', 'delta_pool_scale': 'delta_w', 'delta_w_ret_up': 'delta_w', 'delta_w_pool_up': 'delta_w', 'delta_w_out': 'delta_w', 'delta_norm_ffn2': 'delta_w', 'delta_ffn2_w_in': 'delta_w', 'delta_ffn2_w_out': 'delta_w', 'delta_norm_final': 'delta_w', 'new_m_norm_ffn1': 'new_m', 'new_m_ffn1_w_in': 'new_m', 'new_m_ffn1_w_out': 'new_m', 'new_m_norm_mix': 'new_m', 'new_m_w_in': 'new_m', 'new_m_gate_bias': 'new_m', 'new_m_pool_w': 'new_m', 'new_m_pool_scale': 'new_m', 'new_m_w_ret_up': 'new_m', 'new_m_w_pool_up': 'new_m', 'new_m_w_out': 'new_m', 'new_m_norm_ffn2': 'new_m', 'new_m_ffn2_w_in': 'new_m', 'new_m_ffn2_w_out': 'new_m', 'new_m_norm_final': 'new_m', 'new_v_norm_ffn1': 'new_v', 'new_v_ffn1_w_in': 'new_v', 'new_v_ffn1_w_out': 'new_v', 'new_v_norm_mix': 'new_v', 'new_v_w_in': 'new_v', 'new_v_gate_bias': 'new_v', 'new_v_pool_w': 'new_v', 'new_v_pool_scale': 'new_v', 'new_v_w_ret_up': 'new_v', 'new_v_w_pool_up': 'new_v', 'new_v_w_out': 'new_v', 'new_v_norm_ffn2': 'new_v', 'new_v_ffn2_w_in': 'new_v', 'new_v_ffn2_w_out': 'new_v', 'new_v_norm_final': 'new_v'}


def _forward(args):
    return _fwd_reference(*[args[k] for k in FWD_PARAMS])


def _output_shape():
    out = _jax.eval_shape(lambda: _forward(_fwd_setup_inputs(0)))
    return out.shape, out.dtype

N_MICROBATCH = 1
ADAM_LR = 0.001
ADAM_B1 = 0.9
ADAM_B2 = 0.999
ADAM_EPS = 1e-08
ADAM_WD = 0.01
ADAM_STEP = 10
PER_EXAMPLE_BATCH_AXIS = {'x': 0, 'loss_target': 0}
SHARED_INPUTS = []
_WEIGHT_DTYPES = {'norm_ffn1': _jnp.float32, 'ffn1_w_in': _jnp.float32, 'ffn1_w_out': _jnp.float32, 'norm_mix': _jnp.float32, 'w_in': _jnp.float32, 'gate_bias': _jnp.float32, 'pool_w': _jnp.float32, 'pool_scale': _jnp.float32, 'w_ret_up': _jnp.float32, 'w_pool_up': _jnp.float32, 'w_out': _jnp.float32, 'norm_ffn2': _jnp.float32, 'ffn2_w_in': _jnp.float32, 'ffn2_w_out': _jnp.float32, 'norm_final': _jnp.float32}
MOMENT_SCALE = {'norm_ffn1': 8.712842e-02, 'ffn1_w_in': 3.590202e-02, 'ffn1_w_out': 5.852983e-02, 'norm_mix': 1.261111e-01, 'w_in': 4.871221e-02, 'gate_bias': 2.511580e-02, 'pool_w': 7.317282e-02, 'pool_scale': 7.526921e-02, 'w_ret_up': 4.944220e-02, 'w_pool_up': 7.312245e-02, 'w_out': 8.887394e-02, 'norm_ffn2': 6.505897e-02, 'ffn2_w_in': 2.649699e-02, 'ffn2_w_out': 4.319177e-02, 'norm_final': 3.194538e+01}


def _to_microbatches(a, axis):
    t = _jnp.moveaxis(a, axis, 0)
    t = t.reshape((N_MICROBATCH, t.shape[0] // N_MICROBATCH) + t.shape[1:])
    return _jnp.moveaxis(t, 1, axis + 1)


def setup_inputs(seed: int = 0) -> dict:
    inp = _fwd_setup_inputs(seed)
    key = _jax.random.fold_in(_jax.random.key(seed), 7919)
    shape, _ = _output_shape()
    out = dict(inp)
    out["loss_target"] = _jax.random.normal(_jax.random.fold_in(key, 0), shape, _jnp.float32)
    for i, name in enumerate(TWIN_WEIGHTS):
        w = inp[name].astype(_jnp.float32)
        if MOMENT_SCALE is None:
            s = _jnp.sqrt(_jnp.mean(_jnp.square(w)) + 1e-30)
        else:
            s = MOMENT_SCALE[name]
        km, kv = _jax.random.split(_jax.random.fold_in(key, i + 1))
        out[name] = w
        out["m_" + name] = s * _jax.random.normal(km, w.shape, _jnp.float32)
        out["v_" + name] = (s * s) * _jax.random.uniform(kv, w.shape, _jnp.float32, 0.5, 1.5)
    if N_MICROBATCH > 1:
        for name, axis in PER_EXAMPLE_BATCH_AXIS.items():
            out[name] = _to_microbatches(out[name], axis)
    return {'x': out['x'], 'norm_ffn1': out['norm_ffn1'], 'ffn1_w_in': out['ffn1_w_in'], 'ffn1_w_out': out['ffn1_w_out'], 'norm_mix': out['norm_mix'], 'w_in': out['w_in'], 'gate_bias': out['gate_bias'], 'pool_w': out['pool_w'], 'pool_scale': out['pool_scale'], 'w_ret_up': out['w_ret_up'], 'w_pool_up': out['w_pool_up'], 'w_out': out['w_out'], 'norm_ffn2': out['norm_ffn2'], 'ffn2_w_in': out['ffn2_w_in'], 'ffn2_w_out': out['ffn2_w_out'], 'norm_final': out['norm_final'], 'loss_target': out['loss_target'], 'm_norm_ffn1': out['m_norm_ffn1'], 'm_ffn1_w_in': out['m_ffn1_w_in'], 'm_ffn1_w_out': out['m_ffn1_w_out'], 'm_norm_mix': out['m_norm_mix'], 'm_w_in': out['m_w_in'], 'm_gate_bias': out['m_gate_bias'], 'm_pool_w': out['m_pool_w'], 'm_pool_scale': out['m_pool_scale'], 'm_w_ret_up': out['m_w_ret_up'], 'm_w_pool_up': out['m_w_pool_up'], 'm_w_out': out['m_w_out'], 'm_norm_ffn2': out['m_norm_ffn2'], 'm_ffn2_w_in': out['m_ffn2_w_in'], 'm_ffn2_w_out': out['m_ffn2_w_out'], 'm_norm_final': out['m_norm_final'], 'v_norm_ffn1': out['v_norm_ffn1'], 'v_ffn1_w_in': out['v_ffn1_w_in'], 'v_ffn1_w_out': out['v_ffn1_w_out'], 'v_norm_mix': out['v_norm_mix'], 'v_w_in': out['v_w_in'], 'v_gate_bias': out['v_gate_bias'], 'v_pool_w': out['v_pool_w'], 'v_pool_scale': out['v_pool_scale'], 'v_w_ret_up': out['v_w_ret_up'], 'v_w_pool_up': out['v_w_pool_up'], 'v_w_out': out['v_w_out'], 'v_norm_ffn2': out['v_norm_ffn2'], 'v_ffn2_w_in': out['v_ffn2_w_in'], 'v_ffn2_w_out': out['v_ffn2_w_out'], 'v_norm_final': out['v_norm_final']}


def _loss(weights, diff, rest, loss_target):
    with _jax.named_scope("forward"):
        args = {**rest, TWIN_DIFF_INPUT: diff, **{k: w.astype(_WEIGHT_DTYPES[k]) for k, w in weights.items()}}
        y = _forward(args)
    with _jax.named_scope("loss_head"):
        err = _jnp.square(y.astype(_jnp.float32) - loss_target)
        return 0.5 * _jnp.sum(_jnp.mean(err, axis=-1)) if err.ndim else 0.5 * err


def _adamw(w, g, m, v):
    m = ADAM_B1 * m + (1.0 - ADAM_B1) * g
    v = ADAM_B2 * v + (1.0 - ADAM_B2) * _jnp.square(g)
    m_hat = m / (1.0 - ADAM_B1 ** ADAM_STEP)
    v_hat = v / (1.0 - ADAM_B2 ** ADAM_STEP)
    delta = -ADAM_LR * (m_hat / (_jnp.sqrt(v_hat) + ADAM_EPS) + ADAM_WD * w)
    return delta, m, v


def reference(x, norm_ffn1, ffn1_w_in, ffn1_w_out, norm_mix, w_in, gate_bias, pool_w, pool_scale, w_ret_up, w_pool_up, w_out, norm_ffn2, ffn2_w_in, ffn2_w_out, norm_final, loss_target, m_norm_ffn1, m_ffn1_w_in, m_ffn1_w_out, m_norm_mix, m_w_in, m_gate_bias, m_pool_w, m_pool_scale, m_w_ret_up, m_w_pool_up, m_w_out, m_norm_ffn2, m_ffn2_w_in, m_ffn2_w_out, m_norm_final, v_norm_ffn1, v_ffn1_w_in, v_ffn1_w_out, v_norm_mix, v_w_in, v_gate_bias, v_pool_w, v_pool_scale, v_w_ret_up, v_w_pool_up, v_w_out, v_norm_ffn2, v_ffn2_w_in, v_ffn2_w_out, v_norm_final):
    given = dict(x=x, norm_ffn1=norm_ffn1, ffn1_w_in=ffn1_w_in, ffn1_w_out=ffn1_w_out, norm_mix=norm_mix, w_in=w_in, gate_bias=gate_bias, pool_w=pool_w, pool_scale=pool_scale, w_ret_up=w_ret_up, w_pool_up=w_pool_up, w_out=w_out, norm_ffn2=norm_ffn2, ffn2_w_in=ffn2_w_in, ffn2_w_out=ffn2_w_out, norm_final=norm_final, loss_target=loss_target, m_norm_ffn1=m_norm_ffn1, m_ffn1_w_in=m_ffn1_w_in, m_ffn1_w_out=m_ffn1_w_out, m_norm_mix=m_norm_mix, m_w_in=m_w_in, m_gate_bias=m_gate_bias, m_pool_w=m_pool_w, m_pool_scale=m_pool_scale, m_w_ret_up=m_w_ret_up, m_w_pool_up=m_w_pool_up, m_w_out=m_w_out, m_norm_ffn2=m_norm_ffn2, m_ffn2_w_in=m_ffn2_w_in, m_ffn2_w_out=m_ffn2_w_out, m_norm_final=m_norm_final, v_norm_ffn1=v_norm_ffn1, v_ffn1_w_in=v_ffn1_w_in, v_ffn1_w_out=v_ffn1_w_out, v_norm_mix=v_norm_mix, v_w_in=v_w_in, v_gate_bias=v_gate_bias, v_pool_w=v_pool_w, v_pool_scale=v_pool_scale, v_w_ret_up=v_w_ret_up, v_w_pool_up=v_w_pool_up, v_w_out=v_w_out, v_norm_ffn2=v_norm_ffn2, v_ffn2_w_in=v_ffn2_w_in, v_ffn2_w_out=v_ffn2_w_out, v_norm_final=v_norm_final)
    weights = {n: given[n] for n in TWIN_WEIGHTS}
    shared = {n: given[n] for n in SHARED_INPUTS}
    per_example = {n: given[n] for n in ['x']}
    grad_fn = _jax.value_and_grad(_loss, argnums=(0, 1))

    def one_microbatch(ex, loss_target):
        ex = dict(ex)
        diff = ex.pop(TWIN_DIFF_INPUT)
        return grad_fn(weights, diff, {**shared, **ex}, loss_target)

    if N_MICROBATCH == 1:
        loss, (grad_w, grad_x) = one_microbatch(per_example, given["loss_target"])
    else:
        def body(carry, xs):
            loss_sum, grad_sum = carry
            l_k, (gw_k, gx_k) = one_microbatch(xs[0], xs[1])
            with _jax.named_scope("update"):
                return (loss_sum + l_k, _jax.tree.map(_jnp.add, grad_sum, gw_k)), gx_k

        init = (_jnp.zeros((), _jnp.float32), _jax.tree.map(_jnp.zeros_like, weights))
        (loss, grad_w), grad_x = _jax.lax.scan(body, init, (per_example, given["loss_target"]))
    with _jax.named_scope("update"):
        delta_w, new_m, new_v = {}, {}, {}
        for n in TWIN_WEIGHTS:
            delta_w[n], new_m[n], new_v[n] = _adamw(weights[n], grad_w[n], given["m_" + n], given["v_" + n])
    return (loss, grad_x, *[grad_w[n] for n in TWIN_WEIGHTS], *[delta_w[n] for n in TWIN_WEIGHTS],
            *[new_m[n] for n in TWIN_WEIGHTS], *[new_v[n] for n in TWIN_WEIGHTS])
```

```python
import functools

import numpy as np
import jax
import jax.numpy as jnp
from jax import lax
from jax.experimental import pallas as pl
from jax.experimental.pallas import tpu as pltpu

F32 = jnp.float32
BF16 = jnp.bfloat16
MESH = pl.DeviceIdType.MESH

D_MODEL = 1024
D_FF = 2816
HEADS = 4
HEAD_DIM = 256
GROUPS = 4
GROUP_DIM = 256
POOL_WINDOWS = (2, 4, 8, 16)
IN_WIDTH = 7 * D_MODEL
ROPE_BASE = 10000.0
NORM_EPS = 1e-6
FFN_RES_WEIGHT = 0.5
ADAM_LR, ADAM_B1, ADAM_B2, ADAM_EPS, ADAM_WD, ADAM_STEP = 0.001, 0.9, 0.999, 1e-08, 0.01, 10

N_CHIPS = 4
RET_BLOCK = 256
V7X_VMEM_LIMIT = 48 * 1024 * 1024


def _cparams(sem):
    return pltpu.CompilerParams(dimension_semantics=sem, vmem_limit_bytes=V7X_VMEM_LIMIT)


def _sigmoid(x):
    return jax.nn.sigmoid(x)


_DIMS = {"nn": (((1,), (0,)), ((), ())), "nt": (((1,), (1,)), ((), ())), "tn": (((0,), (0,)), ((), ()))}


def _matmul(name, a, b, mode, m, n, k, tm, tn, tk, out_dtypes, a_spec=None, b_spec=None, extras=(), epilogue=None):
    tm, tn, tk = min(tm, m), min(tn, n), min(tk, k)
    gi, gj, gk = m // tm, n // tn, k // tk
    assert gi * tm == m and gj * tn == n and gk * tk == k, (name, m, n, k, tm, tn, tk)
    if a_spec is None:
        a_spec = pl.BlockSpec((tk, tm), lambda i, j, kk: (kk, i)) if mode == "tn" else pl.BlockSpec((tm, tk), lambda i, j, kk: (i, kk))
    if b_spec is None:
        b_spec = pl.BlockSpec((tn, tk), lambda i, j, kk: (j, kk)) if mode == "nt" else pl.BlockSpec((tk, tn), lambda i, j, kk: (kk, j))
    n_ex, n_out = len(extras), len(out_dtypes)
    dims = _DIMS[mode]

    def body(a_ref, b_ref, *rest):
        ex_refs, out_refs = rest[:n_ex], rest[n_ex:n_ex + n_out]

        def finish(acc):
            outs = (acc,) if epilogue is None else epilogue(acc, *[e[...] for e in ex_refs])
            for o_ref, o in zip(out_refs, outs):
                o_ref[...] = o.astype(o_ref.dtype)

        prod = lax.dot_general(a_ref[...], b_ref[...], dims, preferred_element_type=F32)
        if gk == 1:
            finish(prod)
        else:
            acc_ref = rest[n_ex + n_out]
            kk = pl.program_id(2)

            @pl.when(kk == 0)
            def _():
                acc_ref[...] = prod

            @pl.when(kk > 0)
            def _():
                acc_ref[...] += prod

            @pl.when(kk == gk - 1)
            def _():
                finish(acc_ref[...])

    o_spec = pl.BlockSpec((tm, tn), lambda i, j, kk: (i, j))
    outs = pl.pallas_call(
        body, name=name, grid=(gi, gj, gk),
        in_specs=[a_spec, b_spec] + [o_spec] * n_ex,
        out_specs=[o_spec] * n_out,
        out_shape=[jax.ShapeDtypeStruct((m, n), dt) for dt in out_dtypes],
        scratch_shapes=[pltpu.VMEM((tm, tn), F32)] if gk > 1 else [],
        compiler_params=_cparams(("parallel", "parallel", "arbitrary")),
    )(a, b, *extras)
    return outs[0] if n_out == 1 else outs


def _row_spec(tm, width, col_block=0):
    return pl.BlockSpec((tm, width), lambda i: (i, col_block))


def _full_spec(shape):
    return pl.BlockSpec(shape, lambda *_: (0,) * len(shape))


def _rmsnorm_fwd(name, h, g, tm=512):
    t = h.shape[0]

    def body(h_ref, g_ref, o_ref):
        x = h_ref[...]
        r = lax.rsqrt(jnp.mean(x * x, axis=-1, keepdims=True) + NORM_EPS)
        o_ref[...] = (x * r * g_ref[...]).astype(BF16)

    return pl.pallas_call(
        body, name=name, grid=(t // tm,),
        in_specs=[_row_spec(tm, D_MODEL), _full_spec((1, D_MODEL))],
        out_specs=_row_spec(tm, D_MODEL),
        out_shape=jax.ShapeDtypeStruct((t, D_MODEL), BF16),
        compiler_params=_cparams(("parallel",)),
    )(h, g)


def _rmsnorm_bwd(name, h, g, dn, dres, tm=512):
    t = h.shape[0]

    def body(h_ref, g_ref, dn_ref, dres_ref, dh_ref, dg_ref):
        i = pl.program_id(0)
        x = h_ref[...]
        r = lax.rsqrt(jnp.mean(x * x, axis=-1, keepdims=True) + NORM_EPS)
        xh = x * r
        dn_v = dn_ref[...]
        dxh = dn_v * g_ref[...]
        dh_ref[...] = dres_ref[...] + r * (dxh - xh * jnp.mean(dxh * xh, axis=-1, keepdims=True))
        part = jnp.sum(dn_v * xh, axis=0, keepdims=True)

        @pl.when(i == 0)
        def _():
            dg_ref[...] = part

        @pl.when(i > 0)
        def _():
            dg_ref[...] += part

    return pl.pallas_call(
        body, name=name, grid=(t // tm,),
        in_specs=[_row_spec(tm, D_MODEL), _full_spec((1, D_MODEL)), _row_spec(tm, D_MODEL), _row_spec(tm, D_MODEL)],
        out_specs=[_row_spec(tm, D_MODEL), _full_spec((1, D_MODEL))],
        out_shape=[jax.ShapeDtypeStruct((t, D_MODEL), F32), jax.ShapeDtypeStruct((1, D_MODEL), F32)],
        compiler_params=_cparams(("arbitrary",)),
    )(h, g, dn, dres)


def _loss_and_grad(name, h, g, target, tm=512):
    t = h.shape[0]

    def body(h_ref, g_ref, t_ref, dh_ref, dg_ref, loss_ref):
        i = pl.program_id(0)
        x = h_ref[...]
        gv = g_ref[...]
        r = lax.rsqrt(jnp.mean(x * x, axis=-1, keepdims=True) + NORM_EPS)
        xh = x * r
        err = xh * gv - t_ref[...]
        row = jnp.mean(err * err, axis=-1, keepdims=True)
        part_loss = 0.5 * jnp.sum(row, axis=0, keepdims=True)
        dy = err * (1.0 / D_MODEL)
        dxh = dy * gv
        dh_ref[...] = r * (dxh - xh * jnp.mean(dxh * xh, axis=-1, keepdims=True))
        part = jnp.sum(dy * xh, axis=0, keepdims=True)

        @pl.when(i == 0)
        def _():
            dg_ref[...] = part
            loss_ref[...] = jnp.zeros(loss_ref.shape, F32) + part_loss

        @pl.when(i > 0)
        def _():
            dg_ref[...] += part
            loss_ref[...] += part_loss

    return pl.pallas_call(
        body, name=name, grid=(t // tm,),
        in_specs=[_row_spec(tm, D_MODEL), _full_spec((1, D_MODEL)), _row_spec(tm, D_MODEL)],
        out_specs=[_row_spec(tm, D_MODEL), _full_spec((1, D_MODEL)), _full_spec((8, 128))],
        out_shape=[jax.ShapeDtypeStruct((t, D_MODEL), F32), jax.ShapeDtypeStruct((1, D_MODEL), F32),
                   jax.ShapeDtypeStruct((8, 128), F32)],
        compiler_params=_cparams(("arbitrary",)),
    )(h, g, target)


def _swiglu_fwd(name, a, tm=256):
    t = a.shape[0]

    def body(a_ref, o_ref):
        gate = a_ref[:, :D_FF]
        up = a_ref[:, D_FF:]
        o_ref[...] = (gate * _sigmoid(gate) * up).astype(BF16)

    return pl.pallas_call(
        body, name=name, grid=(t // tm,),
        in_specs=[_row_spec(tm, 2 * D_FF)], out_specs=_row_spec(tm, D_FF),
        out_shape=jax.ShapeDtypeStruct((t, D_FF), BF16),
        compiler_params=_cparams(("parallel",)),
    )(a)


def _swiglu_bwd(name, a, dmid, tm=256):
    t = a.shape[0]

    def body(a_ref, d_ref, o_ref):
        gate = a_ref[:, :D_FF]
        up = a_ref[:, D_FF:]
        dm = d_ref[...]
        s = _sigmoid(gate)
        o_ref[:, :D_FF] = (dm * up * (s * (1.0 + gate * (1.0 - s)))).astype(BF16)
        o_ref[:, D_FF:] = (dm * (gate * s)).astype(BF16)

    return pl.pallas_call(
        body, name=name, grid=(t // tm,),
        in_specs=[_row_spec(tm, 2 * D_FF), _row_spec(tm, D_FF)], out_specs=_row_spec(tm, 2 * D_FF),
        out_shape=jax.ShapeDtypeStruct((t, 2 * D_FF), BF16),
        compiler_params=_cparams(("parallel",)),
    )(a, dmid)


def _rope_tables(t):
    half = HEAD_DIM // 2
    inv_freq = ROPE_BASE ** (-jnp.arange(half, dtype=F32) / half)
    ang = jnp.arange(t, dtype=F32)[:, None] * inv_freq[None, :]
    return jnp.cos(ang), jnp.sin(ang)


def _rotary_fwd(name, proj, cos, sin, tm=512):
    t = proj.shape[0]
    half = HEAD_DIM // 2
    k_scale = HEAD_DIM ** -0.5

    def body(q_ref, k_ref, v_ref, c_ref, s_ref, qo_ref, ko_ref, vo_ref):
        c, s = c_ref[...], s_ref[...]
        for hh in range(HEADS):
            lo, mid, hi = hh * HEAD_DIM, hh * HEAD_DIM + half, (hh + 1) * HEAD_DIM
            x1, x2 = q_ref[:, lo:mid], q_ref[:, mid:hi]
            qo_ref[:, lo:mid] = x1 * c - x2 * s
            qo_ref[:, mid:hi] = x1 * s + x2 * c
            x1, x2 = k_ref[:, lo:mid], k_ref[:, mid:hi]
            ko_ref[:, lo:mid] = (x1 * c - x2 * s) * k_scale
            ko_ref[:, mid:hi] = (x1 * s + x2 * c) * k_scale
        vo_ref[...] = v_ref[...].astype(BF16)

    return pl.pallas_call(
        body, name=name, grid=(t // tm,),
        in_specs=[_row_spec(tm, D_MODEL, 0), _row_spec(tm, D_MODEL, 1), _row_spec(tm, D_MODEL, 2),
                  _row_spec(tm, half), _row_spec(tm, half)],
        out_specs=[_row_spec(tm, D_MODEL)] * 3,
        out_shape=[jax.ShapeDtypeStruct((t, D_MODEL), F32), jax.ShapeDtypeStruct((t, D_MODEL), F32),
                   jax.ShapeDtypeStruct((t, D_MODEL), BF16)],
        compiler_params=_cparams(("parallel",)),
    )(proj, proj, proj, cos, sin)


def _rotary_bwd(name, dqr, dkr, cos, sin, tm=512):
    t = dqr.shape[0]
    half = HEAD_DIM // 2
    k_scale = HEAD_DIM ** -0.5

    def body(q_ref, k_ref, c_ref, s_ref, qo_ref, ko_ref):
        c, s = c_ref[...], s_ref[...]
        for hh in range(HEADS):
            lo, mid, hi = hh * HEAD_DIM, hh * HEAD_DIM + half, (hh + 1) * HEAD_DIM
            y1, y2 = q_ref[:, lo:mid], q_ref[:, mid:hi]
            qo_ref[:, lo:mid] = (y1 * c + y2 * s).astype(BF16)
            qo_ref[:, mid:hi] = (y2 * c - y1 * s).astype(BF16)
            y1, y2 = k_ref[:, lo:mid], k_ref[:, mid:hi]
            ko_ref[:, lo:mid] = ((y1 * c + y2 * s) * k_scale).astype(BF16)
            ko_ref[:, mid:hi] = ((y2 * c - y1 * s) * k_scale).astype(BF16)

    return pl.pallas_call(
        body, name=name, grid=(t // tm,),
        in_specs=[_row_spec(tm, D_MODEL), _row_spec(tm, D_MODEL), _row_spec(tm, half), _row_spec(tm, half)],
        out_specs=[_row_spec(tm, D_MODEL)] * 2,
        out_shape=[jax.ShapeDtypeStruct((t, D_MODEL), BF16)] * 2,
        compiler_params=_cparams(("parallel",)),
    )(dqr, dkr, cos, sin)


def _retention_tables():
    b, chunk = RET_BLOCK, 64
    gamma = 1.0 - 2.0 ** (-5.0 - np.arange(HEADS, dtype=np.float64))
    log_g = np.log(gamma)[:, None, None]
    i = np.arange(b)[:, None]
    j = np.arange(b)[None, :]
    same = (i // chunk) == (j // chunk)
    earlier = (j // chunk) < (i // chunk)
    expo = np.where(same, np.abs(i - j), np.where(earlier, i - j, 0)).astype(np.float64)
    mask = np.where(same | earlier, 1.0, 0.0)
    dmat = np.exp(log_g * expo[None]) * mask[None]
    qd = np.exp(log_g[:, :, 0] * (np.arange(b)[None, :] + 1.0))
    kd = np.exp(log_g[:, :, 0] * (b - 1.0 - np.arange(b)[None, :]))
    cd = np.exp(log_g[:, :, 0] * b) * np.ones((1, HEAD_DIM))
    as32 = lambda v: jnp.asarray(v.astype(np.float32))
    return (as32(dmat), as32(np.swapaxes(dmat, 1, 2)), as32(qd[:, :, None]), as32(kd[:, :, None]), as32(cd[:, None, :]))


def _dot(a, b, mode="nn"):
    return lax.dot_general(a, b, _DIMS[mode], preferred_element_type=F32)


def _head_specs(nb, rev=False):
    blk = (RET_BLOCK, HEAD_DIM)
    pos = (lambda h, n: (nb - 1 - n, h)) if rev else (lambda h, n: (n, h))
    tok = pl.BlockSpec(blk, pos)
    gr = pl.BlockSpec(blk, (lambda h, n: (nb - 1 - n, 3 * HEADS + h)) if rev else (lambda h, n: (n, 3 * HEADS + h)))
    tab = pl.BlockSpec((None, RET_BLOCK, RET_BLOCK), lambda h, n: (h, 0, 0))
    col = pl.BlockSpec((None, RET_BLOCK, 1), lambda h, n: (h, 0, 0))
    rowv = pl.BlockSpec((None, 1, HEAD_DIM), lambda h, n: (h, 0, 0))
    st = pl.BlockSpec((None, None, HEAD_DIM, HEAD_DIM), (lambda h, n: (h, nb - 1 - n, 0, 0)) if rev else (lambda h, n: (h, n, 0, 0)))
    return tok, gr, tab, col, rowv, st


def _retention_fwd(name, qr, kr, vb, proj, tables):
    t = qr.shape[0]
    nb = t // RET_BLOCK
    dmat, _, qd, kd, cd = tables
    tok, gr, tab, col, rowv, st = _head_specs(nb)

    def body(q_ref, k_ref, v_ref, g_ref, d_ref, qd_ref, kd_ref, cd_ref, o_ref, ret_ref, st_ref, state):
        n = pl.program_id(1)

        @pl.when(n == 0)
        def _():
            state[...] = jnp.zeros(state.shape, F32)

        q, k, v = q_ref[...], k_ref[...], v_ref[...]
        s = _dot(q.astype(BF16), k.astype(BF16), "nt") * d_ref[...]
        stb = state[...].astype(BF16)
        st_ref[...] = stb
        o = _dot(s.astype(BF16), v) + _dot((q * qd_ref[...]).astype(BF16), stb)
        o_ref[...] = o
        rn = o * lax.rsqrt(jnp.mean(o * o, axis=-1, keepdims=True) + NORM_EPS)
        g = g_ref[...]
        ret_ref[...] = (rn * (g * _sigmoid(g))).astype(BF16)
        state[...] = state[...] * cd_ref[...] + _dot((k * kd_ref[...]).astype(BF16), v, "tn")

    return pl.pallas_call(
        body, name=name, grid=(HEADS, nb),
        in_specs=[tok, tok, tok, gr, tab, col, col, rowv],
        out_specs=[tok, tok, st],
        out_shape=[jax.ShapeDtypeStruct((t, D_MODEL), F32), jax.ShapeDtypeStruct((t, D_MODEL), BF16),
                   jax.ShapeDtypeStruct((HEADS, nb, HEAD_DIM, HEAD_DIM), BF16)],
        scratch_shapes=[pltpu.VMEM((HEAD_DIM, HEAD_DIM), F32)],
        compiler_params=_cparams(("parallel", "arbitrary")),
    )(qr, kr, vb, proj, dmat, qd, kd, cd)


def _retention_bwd(name, dret, o, qr, kr, vb, proj, states, tables):
    t = qr.shape[0]
    nb = t // RET_BLOCK
    dmat, dmat_t, qd, kd, cd = tables
    tok, gr, tab, col, rowv, st = _head_specs(nb, rev=True)

    def body(dr_ref, o_ref, q_ref, k_ref, v_ref, g_ref, st_ref, d_ref, dt_ref, qd_ref, kd_ref, cd_ref,
             dq_ref, dk_ref, dv_ref, dg_ref, gstate):
        n = pl.program_id(1)

        @pl.when(n == 0)
        def _():
            gstate[...] = jnp.zeros(gstate.shape, F32)

        o_v, g, dr = o_ref[...], g_ref[...], dr_ref[...]
        sg = _sigmoid(g)
        r = lax.rsqrt(jnp.mean(o_v * o_v, axis=-1, keepdims=True) + NORM_EPS)
        rn = o_v * r
        d_rn = dr * (g * sg)
        dg_ref[...] = (dr * rn * (sg * (1.0 + g * (1.0 - sg)))).astype(BF16)
        d_o = r * (d_rn - rn * jnp.mean(d_rn * rn, axis=-1, keepdims=True))
        dob = d_o.astype(BF16)

        q, k, v = q_ref[...], k_ref[...], v_ref[...]
        qb, kb = q.astype(BF16), k.astype(BF16)
        qdv, kdv = qd_ref[...], kd_ref[...]
        s_t = (_dot(kb, qb, "nt") * dt_ref[...]).astype(BF16)
        p_t = (_dot(v, dob, "nt") * dt_ref[...]).astype(BF16)
        p = (_dot(dob, v, "nt") * d_ref[...]).astype(BF16)
        stb = st_ref[...]
        gb = gstate[...].astype(BF16)
        dq_ref[...] = _dot(p, kb) + _dot(dob, stb, "nt") * qdv
        dk_ref[...] = _dot(p_t, qb) + _dot(v, gb, "nt") * kdv
        dv_ref[...] = (_dot(s_t, dob) + _dot((k * kdv).astype(BF16), gb)).astype(BF16)
        gstate[...] = gstate[...] * cd_ref[...] + _dot((q * qdv).astype(BF16), dob, "tn")

    return pl.pallas_call(
        body, name=name, grid=(HEADS, nb),
        in_specs=[tok, tok, tok, tok, tok, gr, st, tab, tab, col, col, rowv],
        out_specs=[tok, tok, tok, tok],
        out_shape=[jax.ShapeDtypeStruct((t, D_MODEL), F32), jax.ShapeDtypeStruct((t, D_MODEL), F32),
                   jax.ShapeDtypeStruct((t, D_MODEL), BF16), jax.ShapeDtypeStruct((t, D_MODEL), BF16)],
        scratch_shapes=[pltpu.VMEM((HEAD_DIM, HEAD_DIM), F32)],
        compiler_params=_cparams(("parallel", "arbitrary")),
    )(dret, o, qr, kr, vb, proj, states, dmat, dmat_t, qd, kd, cd)


POOL_TILE = 256


def _pool_tables():
    b = POOL_TILE
    tt = np.arange(b)[:, None]
    jj = np.arange(b)[None, :]
    cur, prev = [], []
    for w in POOL_WINDOWS:
        cur.append(((tt - jj >= 0) & (tt - jj <= w - 1)).astype(np.float32))
        prev.append((tt - (jj - b) <= w - 1).astype(np.float32))
    cur, prev = np.stack(cur), np.stack(prev)
    as16 = lambda v: jnp.asarray(v, dtype=BF16)
    return as16(cur), as16(prev), as16(np.swapaxes(cur, 1, 2)), as16(np.swapaxes(prev, 1, 2))


def _split2(x):
    hi = x.astype(BF16)
    return hi, (x - hi.astype(F32)).astype(BF16)


def _pool_count(n, g):
    tpos = n * POOL_TILE + lax.broadcasted_iota(jnp.int32, (POOL_TILE, 1), 0)
    return jnp.minimum(tpos + 1, jnp.left_shift(2, g)).astype(F32)


def _pool_fwd(name, proj, pool_w, scale, tables):
    t = proj.shape[0]
    nb = t // POOL_TILE
    mc, mp, _, _ = tables
    p_block0 = 4 * D_MODEL // GROUP_DIM
    blk = (POOL_TILE, GROUP_DIM)
    tab = pl.BlockSpec((None, POOL_TILE, POOL_TILE), lambda g, n: (g, 0, 0))

    def body(pc_ref, pp_ref, mc_ref, mp_ref, w_ref, sc_ref, pm_ref, mix_ref, po_ref):
        g, n = pl.program_id(0), pl.program_id(1)
        p = pc_ref[...]
        c_hi, c_lo = _split2(p)
        p_hi, p_lo = _split2(pp_ref[...])
        mcv, mpv = mc_ref[...], mp_ref[...]
        win = _dot(mcv, c_hi) + _dot(mcv, c_lo)
        before = _dot(mpv, p_hi) + _dot(mpv, p_lo)
        win = win + jnp.where(n > 0, before, 0.0)
        pm = (win / _pool_count(n, g) - p).astype(BF16)
        pm_ref[...] = pm
        mixed = _dot(pm, w_ref[...])
        mix_ref[...] = mixed
        po_ref[...] = (mixed * sc_ref[...]).astype(BF16)

    return pl.pallas_call(
        body, name=name, grid=(GROUPS, nb),
        in_specs=[pl.BlockSpec(blk, lambda g, n: (n, p_block0 + g)),
                  pl.BlockSpec(blk, lambda g, n: (jnp.maximum(n - 1, 0), p_block0 + g)),
                  tab, tab,
                  pl.BlockSpec((None, GROUP_DIM, GROUP_DIM), lambda g, n: (g, 0, 0)),
                  pl.BlockSpec((1, GROUP_DIM), lambda g, n: (0, g))],
        out_specs=[pl.BlockSpec(blk, lambda g, n: (n, g))] * 3,
        out_shape=[jax.ShapeDtypeStruct((t, D_MODEL), BF16), jax.ShapeDtypeStruct((t, D_MODEL), F32),
                   jax.ShapeDtypeStruct((t, D_MODEL), BF16)],
        compiler_params=_cparams(("parallel", "parallel")),
    )(proj, proj, mc, mp, pool_w, scale)


def _pool_bwd(name, dpo, pm, mixed, pool_w, scale, tables):
    t = dpo.shape[0]
    nb = t // POOL_TILE
    _, _, mct, mpt = tables
    blk = (POOL_TILE, GROUP_DIM)
    cur = pl.BlockSpec(blk, lambda g, n: (n, g))
    nxt = pl.BlockSpec(blk, lambda g, n: (jnp.minimum(n + 1, nb - 1), g))
    tab = pl.BlockSpec((None, POOL_TILE, POOL_TILE), lambda g, n: (g, 0, 0))
    wspec = pl.BlockSpec((None, GROUP_DIM, GROUP_DIM), lambda g, n: (g, 0, 0))
    sspec = pl.BlockSpec((1, GROUP_DIM), lambda g, n: (0, g))

    def body(dc_ref, dn_ref, pm_ref, mix_ref, mct_ref, mpt_ref, w_ref, sc_ref, dp_ref, dw_ref, ds_ref):
        g, n = pl.program_id(0), pl.program_id(1)
        dc, sc, w = dc_ref[...], sc_ref[...], w_ref[...]
        dmix_c = (dc * sc).astype(BF16)
        dmix_n = (dn_ref[...] * sc).astype(BF16)
        dpm_c = _dot(dmix_c, w, "nt")
        dpm_n = _dot(dmix_n, w, "nt")
        e_hi, e_lo = _split2(dpm_c / _pool_count(n, g))
        f_hi, f_lo = _split2(dpm_n / _pool_count(n + 1, g))
        mctv, mptv = mct_ref[...], mpt_ref[...]
        back = _dot(mctv, e_hi) + _dot(mctv, e_lo)
        after = _dot(mptv, f_hi) + _dot(mptv, f_lo)
        dp_ref[...] = (back + jnp.where(n < nb - 1, after, 0.0) - dpm_c).astype(BF16)
        dw_part = _dot(pm_ref[...], dmix_c, "tn")
        ds_part = jnp.sum(dc * mix_ref[...], axis=0, keepdims=True)

        @pl.when(n == 0)
        def _():
            dw_ref[...] = dw_part
            ds_ref[...] = ds_part

        @pl.when(n > 0)
        def _():
            dw_ref[...] += dw_part
            ds_ref[...] += ds_part

    return pl.pallas_call(
        body, name=name, grid=(GROUPS, nb),
        in_specs=[cur, nxt, cur, cur, tab, tab, wspec, sspec],
        out_specs=[cur, wspec, sspec],
        out_shape=[jax.ShapeDtypeStruct((t, D_MODEL), BF16), jax.ShapeDtypeStruct((GROUPS, GROUP_DIM, GROUP_DIM), F32),
                   jax.ShapeDtypeStruct((1, D_MODEL), F32)],
        compiler_params=_cparams(("parallel", "arbitrary")),
    )(dpo, dpo, pm, mixed, mct, mpt, pool_w, scale)


GATE0_BLOCK, GATE1_BLOCK = 5, 6


def _merge_fwd(name, ret, po, w_ru, w_pu, proj, bias, tm=512):
    t = ret.shape[0]

    def body(r_ref, p_ref, wr_ref, wp_ref, g0_ref, g1_ref, b_ref, m_ref, ru_ref, pu_ref):
        ru = _dot(r_ref[...], wr_ref[...])
        pu = _dot(p_ref[...], wp_ref[...])
        ru_ref[...] = ru
        pu_ref[...] = pu
        m_ref[...] = (_sigmoid(g0_ref[...] + b_ref[0:1, :]) * ru + _sigmoid(g1_ref[...] + b_ref[1:2, :]) * pu).astype(BF16)

    row = _row_spec(tm, D_MODEL)
    wspec = _full_spec((D_MODEL, D_MODEL))
    return pl.pallas_call(
        body, name=name, grid=(t // tm,),
        in_specs=[row, row, wspec, wspec, _row_spec(tm, D_MODEL, GATE0_BLOCK), _row_spec(tm, D_MODEL, GATE1_BLOCK),
                  _full_spec((2, D_MODEL))],
        out_specs=[row, row, row],
        out_shape=[jax.ShapeDtypeStruct((t, D_MODEL), BF16), jax.ShapeDtypeStruct((t, D_MODEL), F32),
                   jax.ShapeDtypeStruct((t, D_MODEL), F32)],
        compiler_params=_cparams(("parallel",)),
    )(ret, po, w_ru, w_pu, proj, proj, bias)


def _merge_bwd(name, dm, ru, pu, proj, bias, tm=512):
    t = dm.shape[0]

    def body(dm_ref, ru_ref, pu_ref, g0_ref, g1_ref, b_ref, dru_ref, dpu_ref, dg0_ref, dg1_ref, db_ref):
        i = pl.program_id(0)
        d = dm_ref[...]
        s0 = _sigmoid(g0_ref[...] + b_ref[0:1, :])
        s1 = _sigmoid(g1_ref[...] + b_ref[1:2, :])
        dru_ref[...] = (d * s0).astype(BF16)
        dpu_ref[...] = (d * s1).astype(BF16)
        dg0 = d * ru_ref[...] * (s0 * (1.0 - s0))
        dg1 = d * pu_ref[...] * (s1 * (1.0 - s1))
        dg0_ref[...] = dg0.astype(BF16)
        dg1_ref[...] = dg1.astype(BF16)
        part0 = jnp.sum(dg0, axis=0, keepdims=True)
        part1 = jnp.sum(dg1, axis=0, keepdims=True)

        @pl.when(i == 0)
        def _():
            db_ref[0:1, :] = part0
            db_ref[1:2, :] = part1

        @pl.when(i > 0)
        def _():
            db_ref[0:1, :] += part0
            db_ref[1:2, :] += part1

    row = _row_spec(tm, D_MODEL)
    return pl.pallas_call(
        body, name=name, grid=(t // tm,),
        in_specs=[row, row, row, _row_spec(tm, D_MODEL, GATE0_BLOCK), _row_spec(tm, D_MODEL, GATE1_BLOCK),
                  _full_spec((2, D_MODEL))],
        out_specs=[row, row, row, row, _full_spec((2, D_MODEL))],
        out_shape=[jax.ShapeDtypeStruct((t, D_MODEL), BF16)] * 4 + [jax.ShapeDtypeStruct((2, D_MODEL), F32)],
        compiler_params=_cparams(("arbitrary",)),
    )(dm, ru, pu, proj, proj, bias)


def _half_scale(acc):
    return (FFN_RES_WEIGHT * acc,)


def _residual_half(acc, res):
    return (res + FFN_RES_WEIGHT * acc,)


def _residual(acc, res):
    return (res + acc,)


def _ffn_fwd(tag, h, g, w_in, w_out):
    t = h.shape[0]
    nrm = _rmsnorm_fwd(f"{tag}_norm", h, g)
    a = _matmul(f"{tag}_in", nrm, w_in, "nn", t, 2 * D_FF, D_MODEL, 1024, 1408, D_MODEL, [F32])
    mid = _swiglu_fwd(f"{tag}_act", a)
    out = _matmul(f"{tag}_out", mid, w_out, "nn", t, D_MODEL, D_FF, 512, D_MODEL, D_FF, [F32],
                  extras=(h,), epilogue=_residual_half)
    return out, (nrm, a, mid)


def _ffn_bwd(tag, h, g, w_in, w_out, saved, dout):
    t = h.shape[0]
    nrm, a, mid = saved
    dout_b = dout.astype(BF16)
    dmid = _matmul(f"{tag}_dmid", dout_b, w_out, "nt", t, D_FF, D_MODEL, 1024, 1408, D_MODEL, [F32], epilogue=_half_scale)
    d_w_out = _matmul(f"{tag}_dwout", mid, dout_b, "tn", D_FF, D_MODEL, t, 1408, D_MODEL, 1024, [BF16], epilogue=_half_scale)
    da = _swiglu_bwd(f"{tag}_dact", a, dmid)
    d_w_in = _matmul(f"{tag}_dwin", nrm, da, "tn", D_MODEL, 2 * D_FF, t, D_MODEL, 1408, 1024, [BF16])
    dn = _matmul(f"{tag}_dn", da, w_in, "nt", t, D_MODEL, 2 * D_FF, 1024, D_MODEL, 1408, [F32])
    dh, dg = _rmsnorm_bwd(f"{tag}_dnorm", h, g, dn, dout)
    return dh, dg, d_w_in, d_w_out


def _local_step(x, target, w):
    t = x.shape[0]
    cos, sin = _rope_tables(t)
    rtab = _retention_tables()
    ptab = _pool_tables()

    h1, s1 = _ffn_fwd("ffn1", x, w["norm_ffn1"], w["ffn1_w_in"], w["ffn1_w_out"])
    u = _rmsnorm_fwd("mix_norm", h1, w["norm_mix"])
    proj = _matmul("mix_in", u, w["w_in"], "nn", t, IN_WIDTH, D_MODEL, 1024, 1024, D_MODEL, [F32])
    qr, kr, vb = _rotary_fwd("rotary", proj, cos, sin)
    o, ret, states = _retention_fwd("retention", qr, kr, vb, proj, rtab)
    pm, mixed, po = _pool_fwd("pool", proj, w["pool_w"], w["pool_scale"], ptab)
    merged, ru, pu = _merge_fwd("merge", ret, po, w["w_ret_up"], w["w_pool_up"], proj, w["gate_bias"])
    h2 = _matmul("mix_out", merged, w["w_out"], "nn", t, D_MODEL, D_MODEL, 512, D_MODEL, D_MODEL, [F32],
                 extras=(h1,), epilogue=_residual)
    h3, s2 = _ffn_fwd("ffn2", h2, w["norm_ffn2"], w["ffn2_w_in"], w["ffn2_w_out"])
    dh3, dg_final, loss = _loss_and_grad("loss", h3, w["norm_final"], target)

    dh2, dg_ffn2, d_ffn2_in, d_ffn2_out = _ffn_bwd("ffn2", h2, w["norm_ffn2"], w["ffn2_w_in"], w["ffn2_w_out"], s2, dh3)
    dh2_b = dh2.astype(BF16)
    dm = _matmul("mix_dmerged", dh2_b, w["w_out"], "nt", t, D_MODEL, D_MODEL, 1024, D_MODEL, D_MODEL, [F32])
    d_w_out = _matmul("mix_dwout", merged, dh2_b, "tn", D_MODEL, D_MODEL, t, D_MODEL, D_MODEL, 1024, [BF16])
    dru, dpu, dg0, dg1, d_bias = _merge_bwd("merge_bwd", dm, ru, pu, proj, w["gate_bias"])
    dret = _matmul("mix_dret", dru, w["w_ret_up"], "nt", t, D_MODEL, D_MODEL, 1024, D_MODEL, D_MODEL, [F32])
    d_w_ru = _matmul("mix_dwru", ret, dru, "tn", D_MODEL, D_MODEL, t, D_MODEL, D_MODEL, 1024, [BF16])
    dpo = _matmul("mix_dpool", dpu, w["w_pool_up"], "nt", t, D_MODEL, D_MODEL, 1024, D_MODEL, D_MODEL, [F32])
    d_w_pu = _matmul("mix_dwpu", po, dpu, "tn", D_MODEL, D_MODEL, t, D_MODEL, D_MODEL, 1024, [BF16])
    dp, d_pool_w, d_scale = _pool_bwd("pool_bwd", dpo, pm, mixed, w["pool_w"], w["pool_scale"], ptab)
    dqr, dkr, dv, dgr = _retention_bwd("retention_bwd", dret, o, qr, kr, vb, proj, states, rtab)
    dq, dk = _rotary_bwd("rotary_bwd", dqr, dkr, cos, sin)
    dproj = jnp.concatenate([dq, dk, dv, dgr, dp, dg0, dg1], axis=1)
    d_w_in = _matmul("mix_dwin", u, dproj, "tn", D_MODEL, IN_WIDTH, t, D_MODEL, 1024, 1024, [BF16])
    du = _matmul("mix_du", dproj, w["w_in"], "nt", t, D_MODEL, IN_WIDTH, 1024, D_MODEL, 1024, [F32])
    dh1, dg_mix = _rmsnorm_bwd("mix_dnorm", h1, w["norm_mix"], du, dh2)
    dx, dg_ffn1, d_ffn1_in, d_ffn1_out = _ffn_bwd("ffn1", x, w["norm_ffn1"], w["ffn1_w_in"], w["ffn1_w_out"], s1, dh1)

    big = dict(ffn1_w_in=d_ffn1_in, ffn1_w_out=d_ffn1_out, w_in=d_w_in, pool_w=d_pool_w.astype(BF16), w_ret_up=d_w_ru,
               w_pool_up=d_w_pu, w_out=d_w_out, ffn2_w_in=d_ffn2_in, ffn2_w_out=d_ffn2_out)
    small = dict(norm_ffn1=dg_ffn1, norm_mix=dg_mix, gate_bias=d_bias, pool_scale=d_scale, norm_ffn2=dg_ffn2,
                 norm_final=dg_final)
    return loss[0, 0], dx, big, small


BIG = ("ffn1_w_in", "ffn1_w_out", "w_in", "pool_w", "w_ret_up", "w_pool_up", "w_out", "ffn2_w_in", "ffn2_w_out")
KIND = dict(ffn1_w_in="col", ffn1_w_out="row", w_in="col", pool_w="pool", w_ret_up="row", w_pool_up="row", w_out="row",
            ffn2_w_in="col", ffn2_w_out="row")
ANY = pl.BlockSpec(memory_space=pl.ANY)


def _place():
    x, y, c = lax.axis_index("x"), lax.axis_index("y"), lax.axis_index("c")
    chips = [(1 - x, y), (x, 1 - y), (1 - x, 1 - y)]
    return x, y, c, chips


def _full_view_shape(kind, local_shape):
    if kind == "col":
        return (2, local_shape[0] // 2, N_CHIPS * local_shape[1])
    if kind == "row":
        return (N_CHIPS, 2, local_shape[0] // 2, local_shape[1])
    return (GROUPS, N_CHIPS, 2, local_shape[1] // 2, local_shape[2])


def _local_view(kind, arr):
    if kind == "pool":
        return arr.reshape(GROUPS, 2, arr.shape[1] // 2, arr.shape[2])
    return arr.reshape(2, arr.shape[0] // 2, arr.shape[1])


def _blk(kind, ref, s, c):
    if kind == "col":
        cs = ref.shape[2] // N_CHIPS
        return ref.at[c, :, pl.ds(pl.multiple_of(s * cs, 128), cs)]
    if kind == "row":
        return ref.at[s, c]
    return ref.at[:, s, c]


def _half(kind, ref, c):
    return ref.at[:, c] if kind == "pool" else ref.at[c]


def _gather_weights(shards):
    names = list(BIG)
    kinds = [KIND[nm] for nm in names]
    locs = [_local_view(KIND[nm], shards[nm]) for nm in names]
    n = len(names)

    def body(*refs):
        loc, full = refs[:n], refs[n:2 * n]
        send_sems, recv_sems, local_sems = refs[2 * n:]
        x, y, c, chips = _place()
        s = 2 * x + y
        sib = (x, y, 1 - c)

        def remote(a, k, src, dst, to):
            return pltpu.make_async_remote_copy(src_ref=src, dst_ref=dst, send_sem=send_sems.at[a * 6 + k],
                                                recv_sem=recv_sems.at[a * 6 + k], device_id=to, device_id_type=MESH)

        local, sends = [], []
        for a in range(n):
            for cc in range(2):
                cp = pltpu.make_async_copy(_half(kinds[a], loc[a], cc), _blk(kinds[a], full[a], s, cc), local_sems.at[2 * a + cc])
                cp.start()
                local.append(cp)
            for j, chip in enumerate(chips):
                cp = remote(a, j, _half(kinds[a], loc[a], c), _blk(kinds[a], full[a], s, c), (*chip, c))
                cp.start()
                sends.append(cp)
        for a in range(n):
            for j, (px, py) in enumerate(chips):
                theirs = _blk(kinds[a], full[a], 2 * px + py, c)
                remote(a, j, theirs, theirs, sib).wait_recv()
                cp = remote(a, 3 + j, theirs, theirs, sib)
                cp.start()
                sends.append(cp)
        for a in range(n):
            for j, (px, py) in enumerate(chips):
                from_sib = _blk(kinds[a], full[a], 2 * px + py, 1 - c)
                remote(a, 3 + j, from_sib, from_sib, sib).wait_recv()
        for cp in sends:
            cp.wait_send()
        for cp in local:
            cp.wait()

    outs = pl.pallas_call(
        body, name="gather_weights",
        in_specs=[ANY] * n, out_specs=[ANY] * n,
        out_shape=[jax.ShapeDtypeStruct(_full_view_shape(k, shards[nm].shape), BF16) for nm, k in zip(names, kinds)],
        scratch_shapes=[pltpu.SemaphoreType.DMA((6 * n,)), pltpu.SemaphoreType.DMA((6 * n,)), pltpu.SemaphoreType.DMA((2 * n,))],
    )(*locs)
    full = {}
    for nm, k, o in zip(names, kinds, outs):
        if k == "col":
            full[nm] = o.reshape(o.shape[0] * o.shape[1], o.shape[2])
        elif k == "row":
            full[nm] = o.reshape(-1, o.shape[3])
        else:
            full[nm] = o.reshape(GROUPS, -1, o.shape[4])
    return full


def _grad_view(kind, g):
    if kind == "col":
        return g.reshape(2, g.shape[0] // 2, g.shape[1])
    if kind == "row":
        return g.reshape(N_CHIPS, 2, g.shape[0] // (2 * N_CHIPS), g.shape[1])
    return g.reshape(GROUPS, N_CHIPS, 2, g.shape[1] // (2 * N_CHIPS), g.shape[2])


def _pair_exchange(views):
    names = list(BIG)
    kinds = [KIND[nm] for nm in names]
    n = len(names)

    def other_half(kind, ref, c):
        if kind == "col":
            return ref.at[c]
        if kind == "row":
            return ref.at[:, c]
        return ref.at[:, :, c]

    def body(*refs):
        g, got = refs[:n], refs[n:2 * n]
        send_sems, recv_sems = refs[2 * n:]
        x, y, c, _ = _place()
        sib = (x, y, 1 - c)
        cps = []
        for a in range(n):
            cp = pltpu.make_async_remote_copy(src_ref=other_half(kinds[a], g[a], 1 - c), dst_ref=got[a], send_sem=send_sems.at[a],
                                              recv_sem=recv_sems.at[a], device_id=sib, device_id_type=MESH)
            cp.start()
            cps.append(cp)
        for cp in cps:
            cp.wait()

    def got_shape(kind, v):
        if kind == "col":
            return v.shape[1:]
        if kind == "row":
            return (v.shape[0],) + v.shape[2:]
        return v.shape[:2] + v.shape[3:]

    outs = pl.pallas_call(
        body, name="grad_pair_exchange",
        in_specs=[ANY] * n, out_specs=[ANY] * n,
        out_shape=[jax.ShapeDtypeStruct(got_shape(k, views[nm]), BF16) for nm, k in zip(names, kinds)],
        scratch_shapes=[pltpu.SemaphoreType.DMA((n,)), pltpu.SemaphoreType.DMA((n,))],
    )(*[views[nm] for nm in names])
    return dict(zip(names, outs))


def _pair_sum(name, kind, view, got, c_arr):
    if kind == "col":
        _, rows, cols = view.shape
        tr = 128
        grid = (rows // tr,)
        v_spec = pl.BlockSpec((None, tr, cols), lambda i, c: (c[0], i, 0))
        g_spec = pl.BlockSpec((tr, cols), lambda i, c: (i, 0))
    elif kind == "row":
        _, _, rows, cols = view.shape
        grid = (N_CHIPS,)
        v_spec = pl.BlockSpec((None, None, rows, cols), lambda i, c: (i, c[0], 0, 0))
        g_spec = pl.BlockSpec((None, rows, cols), lambda i, c: (i, 0, 0))
    else:
        _, _, _, rows, cols = view.shape
        grid = (GROUPS,)
        v_spec = pl.BlockSpec((None, N_CHIPS, None, rows, cols), lambda i, c: (i, 0, c[0], 0, 0))
        g_spec = pl.BlockSpec((None, N_CHIPS, rows, cols), lambda i, c: (i, 0, 0, 0))

    def body(c_ref, v_ref, g_ref, o_ref):
        o_ref[...] = (v_ref[...].astype(F32) + g_ref[...].astype(F32)).astype(BF16)

    return pl.pallas_call(
        body, name=name,
        grid_spec=pltpu.PrefetchScalarGridSpec(num_scalar_prefetch=1, grid=grid, in_specs=[v_spec, g_spec], out_specs=g_spec),
        out_shape=jax.ShapeDtypeStruct(got.shape, BF16),
        compiler_params=_cparams(("parallel",)),
    )(c_arr, view, got)


def _piece(kind, ref, s):
    if kind == "col":
        cs = ref.shape[1] // N_CHIPS
        return ref.at[:, pl.ds(pl.multiple_of(s * cs, 128), cs)]
    if kind == "row":
        return ref.at[s]
    return ref.at[:, s]


def _piece_shape(kind, shape):
    if kind == "col":
        return (shape[0], shape[1] // N_CHIPS)
    if kind == "row":
        return shape[1:]
    return (shape[0],) + shape[2:]


def _shard_exchange(psums):
    names = list(BIG)
    kinds = [KIND[nm] for nm in names]
    n = len(names)

    def body(*refs):
        p, got = refs[:n], refs[n:2 * n]
        send_sems, recv_sems = refs[2 * n:]
        x, y, c, chips = _place()
        cps = []
        for a in range(n):
            for j, (px, py) in enumerate(chips):
                cp = pltpu.make_async_remote_copy(src_ref=_piece(kinds[a], p[a], 2 * px + py), dst_ref=got[a].at[j],
                                                  send_sem=send_sems.at[3 * a + j], recv_sem=recv_sems.at[3 * a + j],
                                                  device_id=(px, py, c), device_id_type=MESH)
                cp.start()
                cps.append(cp)
        for cp in cps:
            cp.wait()

    outs = pl.pallas_call(
        body, name="grad_shard_exchange",
        in_specs=[ANY] * n, out_specs=[ANY] * n,
        out_shape=[jax.ShapeDtypeStruct((3,) + _piece_shape(k, psums[nm].shape), BF16) for nm, k in zip(names, kinds)],
        scratch_shapes=[pltpu.SemaphoreType.DMA((3 * n,)), pltpu.SemaphoreType.DMA((3 * n,))],
    )(*[psums[nm] for nm in names])
    return dict(zip(names, outs))


def _shard_sum(name, kind, psum, got, s_arr):
    if kind == "col":
        rows, cols = psum.shape
        cs = cols // N_CHIPS
        tr = 128
        grid = (rows // tr,)
        p_spec = pl.BlockSpec((tr, cs), lambda i, s: (i, s[0]))
        g_spec = pl.BlockSpec((3, tr, cs), lambda i, s: (0, i, 0))
        o_spec = pl.BlockSpec((tr, cs), lambda i, s: (i, 0))
        out_shape = (rows, cs)
    elif kind == "row":
        _, rows, cols = psum.shape
        grid = (1,)
        p_spec = pl.BlockSpec((None, rows, cols), lambda i, s: (s[0], 0, 0))
        g_spec = pl.BlockSpec((3, rows, cols), lambda i, s: (0, 0, 0))
        o_spec = pl.BlockSpec((rows, cols), lambda i, s: (0, 0))
        out_shape = (rows, cols)
    else:
        _, _, rows, cols = psum.shape
        grid = (1,)
        p_spec = pl.BlockSpec((GROUPS, None, rows, cols), lambda i, s: (0, s[0], 0, 0))
        g_spec = pl.BlockSpec((3, GROUPS, rows, cols), lambda i, s: (0, 0, 0, 0))
        o_spec = pl.BlockSpec((GROUPS, rows, cols), lambda i, s: (0, 0, 0))
        out_shape = (GROUPS, rows, cols)

    def body(s_ref, p_ref, g_ref, o_ref):
        o_ref[...] = ((p_ref[...].astype(F32) + g_ref[0].astype(F32)) + g_ref[1].astype(F32)) + g_ref[2].astype(F32)

    return pl.pallas_call(
        body, name=name,
        grid_spec=pltpu.PrefetchScalarGridSpec(num_scalar_prefetch=1, grid=grid, in_specs=[p_spec, g_spec], out_specs=o_spec),
        out_shape=jax.ShapeDtypeStruct(out_shape, F32),
        compiler_params=_cparams(("parallel",)),
    )(s_arr, psum, got)


def _half_exchange(halves):
    names = list(BIG)
    kinds = [KIND[nm] for nm in names]
    n = len(names)

    def body(*refs):
        h, out = refs[:n], refs[n:2 * n]
        send_sems, recv_sems, local_sems = refs[2 * n:]
        x, y, c, _ = _place()
        sib = (x, y, 1 - c)
        cps = []
        for a in range(n):
            mine = _half(kinds[a], out[a], c)
            loc = pltpu.make_async_copy(h[a], mine, local_sems.at[a])
            loc.start()
            cp = pltpu.make_async_remote_copy(src_ref=h[a], dst_ref=mine, send_sem=send_sems.at[a], recv_sem=recv_sems.at[a],
                                              device_id=sib, device_id_type=MESH)
            cp.start()
            cps.append((loc, cp, _half(kinds[a], out[a], 1 - c)))
        for a, (loc, cp, theirs) in enumerate(cps):
            loc.wait()
            cp.wait_send()
            pltpu.make_async_remote_copy(src_ref=h[a], dst_ref=theirs, send_sem=send_sems.at[a], recv_sem=recv_sems.at[a],
                                         device_id=sib, device_id_type=MESH).wait_recv()

    def out_shape(kind, shape):
        return (shape[0], 2) + shape[1:] if kind == "pool" else (2,) + shape

    outs = pl.pallas_call(
        body, name="grad_half_exchange",
        in_specs=[ANY] * n, out_specs=[ANY] * n,
        out_shape=[jax.ShapeDtypeStruct(out_shape(k, halves[nm].shape), F32) for nm, k in zip(names, kinds)],
        scratch_shapes=[pltpu.SemaphoreType.DMA((n,)), pltpu.SemaphoreType.DMA((n,)), pltpu.SemaphoreType.DMA((n,))],
    )(*[halves[nm] for nm in names])
    return dict(zip(names, outs))


N_DEV = 8
SMALL_ROWS = 8


def _all_reduce_small(name, v):
    def body(v_ref, o_ref, buf, send_sems, recv_sems):
        x, y, c, _ = _place()
        me = 4 * x + 2 * y + c
        buf[me] = v_ref[...]
        cps = []
        for r in range(1, N_DEV):
            to = (x ^ (r >> 2), y ^ ((r >> 1) & 1), c ^ (r & 1))
            cp = pltpu.make_async_remote_copy(src_ref=v_ref, dst_ref=buf.at[me], send_sem=send_sems.at[r - 1],
                                              recv_sem=recv_sems.at[r - 1], device_id=to, device_id_type=MESH)
            cp.start()
            cps.append(cp)
        for r in range(1, N_DEV):
            pltpu.make_async_remote_copy(src_ref=v_ref, dst_ref=buf.at[me ^ r], send_sem=send_sems.at[r - 1],
                                         recv_sem=recv_sems.at[r - 1], device_id=(x, y, c), device_id_type=MESH).wait_recv()
        for cp in cps:
            cp.wait_send()
        acc = buf[0]
        for d in range(1, N_DEV):
            acc = acc + buf[d]
        o_ref[...] = acc

    vm = pl.BlockSpec(memory_space=pltpu.VMEM)
    return pl.pallas_call(
        body, name=name, in_specs=[vm], out_specs=vm,
        out_shape=jax.ShapeDtypeStruct((SMALL_ROWS, D_MODEL), F32),
        scratch_shapes=[pltpu.VMEM((N_DEV, SMALL_ROWS, D_MODEL), F32), pltpu.SemaphoreType.DMA((N_DEV - 1,)),
                        pltpu.SemaphoreType.DMA((N_DEV - 1,))],
    )(v)


def _adamw(name, w, g, m, v):
    rows, cols = w.shape
    tr = next((c for c in (256, 176, 128, 64, 32, 8) if rows % c == 0), rows)
    spec = pl.BlockSpec((tr, cols), lambda i: (i, 0))

    def body(w_ref, g_ref, m_ref, v_ref, d_ref, mo_ref, vo_ref):
        gv = g_ref[...]
        m_new = ADAM_B1 * m_ref[...] + (1.0 - ADAM_B1) * gv
        v_new = ADAM_B2 * v_ref[...] + (1.0 - ADAM_B2) * jnp.square(gv)
        m_hat = m_new / (1.0 - ADAM_B1 ** ADAM_STEP)
        v_hat = v_new / (1.0 - ADAM_B2 ** ADAM_STEP)
        d_ref[...] = -ADAM_LR * (m_hat / (jnp.sqrt(v_hat) + ADAM_EPS) + ADAM_WD * w_ref[...])
        mo_ref[...] = m_new
        vo_ref[...] = v_new

    return pl.pallas_call(
        body, name=name, grid=(rows // tr,),
        in_specs=[spec] * 4, out_specs=[spec] * 3,
        out_shape=[jax.ShapeDtypeStruct((rows, cols), F32)] * 3,
        compiler_params=_cparams(("parallel",)),
    )(w, g, m, v)


WEIGHTS = ("norm_ffn1", "ffn1_w_in", "ffn1_w_out", "norm_mix", "w_in", "gate_bias", "pool_w", "pool_scale", "w_ret_up",
           "w_pool_up", "w_out", "norm_ffn2", "ffn2_w_in", "ffn2_w_out", "norm_final")
SMALL_ROW = dict(norm_ffn1=0, norm_mix=1, gate_bias=2, pool_scale=4, norm_ffn2=5, norm_final=6)


def _as2d(a):
    return a.reshape(-1, a.shape[-1])


def kernel(x, norm_ffn1, ffn1_w_in, ffn1_w_out, norm_mix, w_in, gate_bias, pool_w, pool_scale, w_ret_up, w_pool_up, w_out, norm_ffn2, ffn2_w_in, ffn2_w_out, norm_final, loss_target, m_norm_ffn1, m_ffn1_w_in, m_ffn1_w_out, m_norm_mix, m_w_in, m_gate_bias, m_pool_w, m_pool_scale, m_w_ret_up, m_w_pool_up, m_w_out, m_norm_ffn2, m_ffn2_w_in, m_ffn2_w_out, m_norm_final, v_norm_ffn1, v_ffn1_w_in, v_ffn1_w_out, v_norm_mix, v_w_in, v_gate_bias, v_pool_w, v_pool_scale, v_w_ret_up, v_w_pool_up, v_w_out, v_norm_ffn2, v_ffn2_w_in, v_ffn2_w_out, v_norm_final):
    wt = dict(norm_ffn1=norm_ffn1, ffn1_w_in=ffn1_w_in, ffn1_w_out=ffn1_w_out, norm_mix=norm_mix, w_in=w_in, gate_bias=gate_bias,
              pool_w=pool_w, pool_scale=pool_scale, w_ret_up=w_ret_up, w_pool_up=w_pool_up, w_out=w_out, norm_ffn2=norm_ffn2,
              ffn2_w_in=ffn2_w_in, ffn2_w_out=ffn2_w_out, norm_final=norm_final)
    mom = dict(norm_ffn1=m_norm_ffn1, ffn1_w_in=m_ffn1_w_in, ffn1_w_out=m_ffn1_w_out, norm_mix=m_norm_mix, w_in=m_w_in,
               gate_bias=m_gate_bias, pool_w=m_pool_w, pool_scale=m_pool_scale, w_ret_up=m_w_ret_up, w_pool_up=m_w_pool_up,
               w_out=m_w_out, norm_ffn2=m_norm_ffn2, ffn2_w_in=m_ffn2_w_in, ffn2_w_out=m_ffn2_w_out, norm_final=m_norm_final)
    var = dict(norm_ffn1=v_norm_ffn1, ffn1_w_in=v_ffn1_w_in, ffn1_w_out=v_ffn1_w_out, norm_mix=v_norm_mix, w_in=v_w_in,
               gate_bias=v_gate_bias, pool_w=v_pool_w, pool_scale=v_pool_scale, w_ret_up=v_w_ret_up, w_pool_up=v_w_pool_up,
               w_out=v_w_out, norm_ffn2=v_norm_ffn2, ffn2_w_in=v_ffn2_w_in, ffn2_w_out=v_ffn2_w_out, norm_final=v_norm_final)

    ax, ay, ac = lax.axis_index("x"), lax.axis_index("y"), lax.axis_index("c")
    chip = 2 * ax + ay
    c_arr = jnp.reshape(ac, (1,)).astype(jnp.int32)
    s_arr = jnp.reshape(chip, (1,)).astype(jnp.int32)
    bias_cols = gate_bias.shape[-1]

    shards = {nm: wt[nm][0].astype(BF16) for nm in BIG}
    full = _gather_weights(shards)
    placed = lax.dynamic_update_slice(jnp.zeros((SMALL_ROWS, D_MODEL), F32), gate_bias[0], (0, chip * bias_cols))
    bias_full = _all_reduce_small("gather_gate_bias", jnp.where(ac == 0, placed, 0.0))[:2]
    full.update(norm_ffn1=norm_ffn1, norm_mix=norm_mix, norm_ffn2=norm_ffn2, pool_scale=pool_scale,
                norm_final=norm_final.reshape(1, D_MODEL), gate_bias=bias_full)

    loss_local, dx, big, small = _local_step(x[0], loss_target[0], full)
    loss = lax.psum(loss_local, ("x", "y", "c"))

    packed = jnp.concatenate([small["norm_ffn1"], small["norm_mix"], small["gate_bias"], small["pool_scale"],
                              small["norm_ffn2"], small["norm_final"], jnp.zeros((1, D_MODEL), F32)], axis=0)
    small_sum = _all_reduce_small("reduce_small_grads", packed)
    views = {nm: _grad_view(KIND[nm], big[nm]) for nm in BIG}
    from_sib = _pair_exchange(views)
    psums = {nm: _pair_sum(f"pair_sum_{nm}", KIND[nm], views[nm], from_sib[nm], c_arr) for nm in BIG}
    from_chips = _shard_exchange(psums)
    halves = {nm: _shard_sum(f"shard_sum_{nm}", KIND[nm], psums[nm], from_chips[nm], s_arr) for nm in BIG}
    reduced = _half_exchange(halves)

    grads = {}
    for nm in BIG:
        grads[nm] = reduced[nm].reshape(wt[nm].shape)
    for nm in ("norm_ffn1", "norm_mix", "pool_scale", "norm_ffn2"):
        grads[nm] = small_sum[SMALL_ROW[nm]][None, :]
    grads["norm_final"] = small_sum[SMALL_ROW["norm_final"]]
    grads["gate_bias"] = lax.dynamic_slice(small_sum, (SMALL_ROW["gate_bias"], chip * bias_cols), (2, bias_cols))[None]

    delta, new_m, new_v = {}, {}, {}
    for nm in WEIGHTS:
        shape = wt[nm].shape
        d, m2, v2 = _adamw(f"adamw_{nm}", _as2d(wt[nm]), _as2d(grads[nm]), _as2d(mom[nm]), _as2d(var[nm]))
        delta[nm], new_m[nm], new_v[nm] = d.reshape(shape), m2.reshape(shape), v2.reshape(shape)

    return (loss, dx[None], *[grads[nm] for nm in WEIGHTS], *[delta[nm] for nm in WEIGHTS],
            *[new_m[nm] for nm in WEIGHTS], *[new_v[nm] for nm in WEIGHTS])
```

```python
import functools

import numpy as np
import jax
import jax.numpy as jnp
from jax import lax
from jax.experimental import pallas as pl
from jax.experimental.pallas import tpu as pltpu

F32 = jnp.float32
BF16 = jnp.bfloat16
MESH = pl.DeviceIdType.MESH

D_MODEL = 1024
D_FF = 2816
HEADS = 4
HEAD_DIM = 256
GROUPS = 4
GROUP_DIM = 256
POOL_WINDOWS = (2, 4, 8, 16)
IN_WIDTH = 7 * D_MODEL
ROPE_BASE = 10000.0
NORM_EPS = 1e-6
FFN_RES_WEIGHT = 0.5
ADAM_LR, ADAM_B1, ADAM_B2, ADAM_EPS, ADAM_WD, ADAM_STEP = 0.001, 0.9, 0.999, 1e-08, 0.01, 10

N_CHIPS = 4
RET_BLOCK = 256
V7X_VMEM_LIMIT = 48 * 1024 * 1024


def _cparams(sem):
    return pltpu.CompilerParams(dimension_semantics=sem, vmem_limit_bytes=V7X_VMEM_LIMIT)


def _sigmoid(x):
    return jax.nn.sigmoid(x)


_DIMS = {"nn": (((1,), (0,)), ((), ())), "nt": (((1,), (1,)), ((), ())), "tn": (((0,), (0,)), ((), ()))}


def _matmul(name, a, b, mode, m, n, k, tm, tn, tk, out_dtypes, a_spec=None, b_spec=None, extras=(), epilogue=None):
    tm, tn, tk = min(tm, m), min(tn, n), min(tk, k)
    gi, gj, gk = m // tm, n // tn, k // tk
    assert gi * tm == m and gj * tn == n and gk * tk == k, (name, m, n, k, tm, tn, tk)
    if a_spec is None:
        a_spec = pl.BlockSpec((tk, tm), lambda i, j, kk: (kk, i)) if mode == "tn" else pl.BlockSpec((tm, tk), lambda i, j, kk: (i, kk))
    if b_spec is None:
        b_spec = pl.BlockSpec((tn, tk), lambda i, j, kk: (j, kk)) if mode == "nt" else pl.BlockSpec((tk, tn), lambda i, j, kk: (kk, j))
    n_ex, n_out = len(extras), len(out_dtypes)
    dims = _DIMS[mode]

    def body(a_ref, b_ref, *rest):
        ex_refs, out_refs = rest[:n_ex], rest[n_ex:n_ex + n_out]

        def finish(acc):
            outs = (acc,) if epilogue is None else epilogue(acc, *[e[...] for e in ex_refs])
            for o_ref, o in zip(out_refs, outs):
                o_ref[...] = o.astype(o_ref.dtype)

        prod = lax.dot_general(a_ref[...], b_ref[...], dims, preferred_element_type=F32)
        if gk == 1:
            finish(prod)
        else:
            acc_ref = rest[n_ex + n_out]
            kk = pl.program_id(2)

            @pl.when(kk == 0)
            def _():
                acc_ref[...] = prod

            @pl.when(kk > 0)
            def _():
                acc_ref[...] += prod

            @pl.when(kk == gk - 1)
            def _():
                finish(acc_ref[...])

    o_spec = pl.BlockSpec((tm, tn), lambda i, j, kk: (i, j))
    outs = pl.pallas_call(
        body, name=name, grid=(gi, gj, gk),
        in_specs=[a_spec, b_spec] + [o_spec] * n_ex,
        out_specs=[o_spec] * n_out,
        out_shape=[jax.ShapeDtypeStruct((m, n), dt) for dt in out_dtypes],
        scratch_shapes=[pltpu.VMEM((tm, tn), F32)] if gk > 1 else [],
        compiler_params=_cparams(("parallel", "parallel", "arbitrary")),
    )(a, b, *extras)
    return outs[0] if n_out == 1 else outs


def _row_spec(tm, width, col_block=0):
    return pl.BlockSpec((tm, width), lambda i: (i, col_block))


def _full_spec(shape):
    return pl.BlockSpec(shape, lambda *_: (0,) * len(shape))


def _rmsnorm_fwd(name, h, g, tm=512):
    t = h.shape[0]

    def body(h_ref, g_ref, o_ref):
        x = h_ref[...]
        r = lax.rsqrt(jnp.mean(x * x, axis=-1, keepdims=True) + NORM_EPS)
        o_ref[...] = (x * r * g_ref[...]).astype(BF16)

    return pl.pallas_call(
        body, name=name, grid=(t // tm,),
        in_specs=[_row_spec(tm, D_MODEL), _full_spec((1, D_MODEL))],
        out_specs=_row_spec(tm, D_MODEL),
        out_shape=jax.ShapeDtypeStruct((t, D_MODEL), BF16),
        compiler_params=_cparams(("parallel",)),
    )(h, g)


def _rmsnorm_bwd(name, h, g, dn, dres, tm=512):
    t = h.shape[0]

    def body(h_ref, g_ref, dn_ref, dres_ref, dh_ref, dhb_ref, dg_ref):
        i = pl.program_id(0)
        x = h_ref[...]
        r = lax.rsqrt(jnp.mean(x * x, axis=-1, keepdims=True) + NORM_EPS)
        xh = x * r
        dn_v = dn_ref[...]
        dxh = dn_v * g_ref[...]
        dh = dres_ref[...] + r * (dxh - xh * jnp.mean(dxh * xh, axis=-1, keepdims=True))
        dh_ref[...] = dh
        dhb_ref[...] = dh.astype(BF16)
        part = jnp.sum(dn_v * xh, axis=0, keepdims=True)

        @pl.when(i == 0)
        def _():
            dg_ref[...] = part

        @pl.when(i > 0)
        def _():
            dg_ref[...] += part

    return pl.pallas_call(
        body, name=name, grid=(t // tm,),
        in_specs=[_row_spec(tm, D_MODEL), _full_spec((1, D_MODEL)), _row_spec(tm, D_MODEL), _row_spec(tm, D_MODEL)],
        out_specs=[_row_spec(tm, D_MODEL), _row_spec(tm, D_MODEL), _full_spec((1, D_MODEL))],
        out_shape=[jax.ShapeDtypeStruct((t, D_MODEL), F32), jax.ShapeDtypeStruct((t, D_MODEL), BF16),
                   jax.ShapeDtypeStruct((1, D_MODEL), F32)],
        compiler_params=_cparams(("arbitrary",)),
    )(h, g, dn, dres)


def _loss_and_grad(name, h, g, target, tm=512):
    t = h.shape[0]

    def body(h_ref, g_ref, t_ref, dh_ref, dhb_ref, dg_ref, loss_ref):
        i = pl.program_id(0)
        x = h_ref[...]
        gv = g_ref[...]
        r = lax.rsqrt(jnp.mean(x * x, axis=-1, keepdims=True) + NORM_EPS)
        xh = x * r
        err = xh * gv - t_ref[...]
        row = jnp.mean(err * err, axis=-1, keepdims=True)
        part_loss = 0.5 * jnp.sum(row, axis=0, keepdims=True)
        dy = err * (1.0 / D_MODEL)
        dxh = dy * gv
        dh = r * (dxh - xh * jnp.mean(dxh * xh, axis=-1, keepdims=True))
        dh_ref[...] = dh
        dhb_ref[...] = dh.astype(BF16)
        part = jnp.sum(dy * xh, axis=0, keepdims=True)

        @pl.when(i == 0)
        def _():
            dg_ref[...] = part
            loss_ref[...] = jnp.zeros(loss_ref.shape, F32) + part_loss

        @pl.when(i > 0)
        def _():
            dg_ref[...] += part
            loss_ref[...] += part_loss

    return pl.pallas_call(
        body, name=name, grid=(t // tm,),
        in_specs=[_row_spec(tm, D_MODEL), _full_spec((1, D_MODEL)), _row_spec(tm, D_MODEL)],
        out_specs=[_row_spec(tm, D_MODEL), _row_spec(tm, D_MODEL), _full_spec((1, D_MODEL)), _full_spec((8, 128))],
        out_shape=[jax.ShapeDtypeStruct((t, D_MODEL), F32), jax.ShapeDtypeStruct((t, D_MODEL), BF16),
                   jax.ShapeDtypeStruct((1, D_MODEL), F32), jax.ShapeDtypeStruct((8, 128), F32)],
        compiler_params=_cparams(("arbitrary",)),
    )(h, g, target)


def _rope_tables(t):
    half = HEAD_DIM // 2
    inv_freq = ROPE_BASE ** (-jnp.arange(half, dtype=F32) / half)
    ang = jnp.arange(t, dtype=F32)[:, None] * inv_freq[None, :]
    return jnp.cos(ang), jnp.sin(ang)


def _rotary_fwd(name, proj, cos, sin, tm=512):
    t = proj.shape[0]
    half = HEAD_DIM // 2
    k_scale = HEAD_DIM ** -0.5

    def body(q_ref, k_ref, v_ref, c_ref, s_ref, qo_ref, ko_ref, vo_ref):
        c, s = c_ref[...], s_ref[...]
        for hh in range(HEADS):
            lo, mid, hi = hh * HEAD_DIM, hh * HEAD_DIM + half, (hh + 1) * HEAD_DIM
            x1, x2 = q_ref[:, lo:mid], q_ref[:, mid:hi]
            qo_ref[:, lo:mid] = x1 * c - x2 * s
            qo_ref[:, mid:hi] = x1 * s + x2 * c
            x1, x2 = k_ref[:, lo:mid], k_ref[:, mid:hi]
            ko_ref[:, lo:mid] = (x1 * c - x2 * s) * k_scale
            ko_ref[:, mid:hi] = (x1 * s + x2 * c) * k_scale
        vo_ref[...] = v_ref[...].astype(BF16)

    return pl.pallas_call(
        body, name=name, grid=(t // tm,),
        in_specs=[_row_spec(tm, D_MODEL, 0), _row_spec(tm, D_MODEL, 1), _row_spec(tm, D_MODEL, 2),
                  _row_spec(tm, half), _row_spec(tm, half)],
        out_specs=[_row_spec(tm, D_MODEL)] * 3,
        out_shape=[jax.ShapeDtypeStruct((t, D_MODEL), F32), jax.ShapeDtypeStruct((t, D_MODEL), F32),
                   jax.ShapeDtypeStruct((t, D_MODEL), BF16)],
        compiler_params=_cparams(("parallel",)),
    )(proj, proj, proj, cos, sin)


def _rotary_bwd(name, dqr, dkr, cos, sin, tm=512):
    t = dqr.shape[0]
    half = HEAD_DIM // 2
    k_scale = HEAD_DIM ** -0.5

    def body(q_ref, k_ref, c_ref, s_ref, qo_ref, ko_ref):
        c, s = c_ref[...], s_ref[...]
        for hh in range(HEADS):
            lo, mid, hi = hh * HEAD_DIM, hh * HEAD_DIM + half, (hh + 1) * HEAD_DIM
            y1, y2 = q_ref[:, lo:mid], q_ref[:, mid:hi]
            qo_ref[:, lo:mid] = (y1 * c + y2 * s).astype(BF16)
            qo_ref[:, mid:hi] = (y2 * c - y1 * s).astype(BF16)
            y1, y2 = k_ref[:, lo:mid], k_ref[:, mid:hi]
            ko_ref[:, lo:mid] = ((y1 * c + y2 * s) * k_scale).astype(BF16)
            ko_ref[:, mid:hi] = ((y2 * c - y1 * s) * k_scale).astype(BF16)

    return pl.pallas_call(
        body, name=name, grid=(t // tm,),
        in_specs=[_row_spec(tm, D_MODEL), _row_spec(tm, D_MODEL), _row_spec(tm, half), _row_spec(tm, half)],
        out_specs=[_row_spec(tm, D_MODEL)] * 2,
        out_shape=[jax.ShapeDtypeStruct((t, D_MODEL), BF16)] * 2,
        compiler_params=_cparams(("parallel",)),
    )(dqr, dkr, cos, sin)


def _retention_tables():
    b, chunk = RET_BLOCK, 64
    gamma = 1.0 - 2.0 ** (-5.0 - np.arange(HEADS, dtype=np.float64))
    log_g = np.log(gamma)[:, None, None]
    i = np.arange(b)[:, None]
    j = np.arange(b)[None, :]
    same = (i // chunk) == (j // chunk)
    earlier = (j // chunk) < (i // chunk)
    expo = np.where(same, np.abs(i - j), np.where(earlier, i - j, 0)).astype(np.float64)
    mask = np.where(same | earlier, 1.0, 0.0)
    dmat = np.exp(log_g * expo[None]) * mask[None]
    qd = np.exp(log_g[:, :, 0] * (np.arange(b)[None, :] + 1.0))
    kd = np.exp(log_g[:, :, 0] * (b - 1.0 - np.arange(b)[None, :]))
    cd = np.exp(log_g[:, :, 0] * b) * np.ones((1, HEAD_DIM))
    as32 = lambda v: jnp.asarray(v.astype(np.float32))
    return (as32(dmat), as32(np.swapaxes(dmat, 1, 2)), as32(qd[:, :, None]), as32(kd[:, :, None]), as32(cd[:, None, :]))


def _dot(a, b, mode="nn"):
    return lax.dot_general(a, b, _DIMS[mode], preferred_element_type=F32)


def _head_specs(nb, rev=False):
    blk = (RET_BLOCK, HEAD_DIM)
    pos = (lambda h, n: (nb - 1 - n, h)) if rev else (lambda h, n: (n, h))
    tok = pl.BlockSpec(blk, pos)
    gr = pl.BlockSpec(blk, (lambda h, n: (nb - 1 - n, 3 * HEADS + h)) if rev else (lambda h, n: (n, 3 * HEADS + h)))
    tab = pl.BlockSpec((None, RET_BLOCK, RET_BLOCK), lambda h, n: (h, 0, 0))
    col = pl.BlockSpec((None, RET_BLOCK, 1), lambda h, n: (h, 0, 0))
    rowv = pl.BlockSpec((None, 1, HEAD_DIM), lambda h, n: (h, 0, 0))
    st = pl.BlockSpec((None, None, HEAD_DIM, HEAD_DIM), (lambda h, n: (h, nb - 1 - n, 0, 0)) if rev else (lambda h, n: (h, n, 0, 0)))
    return tok, gr, tab, col, rowv, st


def _retention_fwd(name, qr, kr, vb, proj, tables):
    t = qr.shape[0]
    nb = t // RET_BLOCK
    dmat, _, qd, kd, cd = tables
    tok, gr, tab, col, rowv, st = _head_specs(nb)

    def body(q_ref, k_ref, v_ref, g_ref, d_ref, qd_ref, kd_ref, cd_ref, o_ref, ret_ref, st_ref, state):
        n = pl.program_id(1)

        @pl.when(n == 0)
        def _():
            state[...] = jnp.zeros(state.shape, F32)

        q, k, v = q_ref[...], k_ref[...], v_ref[...]
        s = _dot(q.astype(BF16), k.astype(BF16), "nt") * d_ref[...]
        stb = state[...].astype(BF16)
        st_ref[...] = stb
        o = _dot(s.astype(BF16), v) + _dot((q * qd_ref[...]).astype(BF16), stb)
        o_ref[...] = o
        rn = o * lax.rsqrt(jnp.mean(o * o, axis=-1, keepdims=True) + NORM_EPS)
        g = g_ref[...]
        ret_ref[...] = (rn * (g * _sigmoid(g))).astype(BF16)
        state[...] = state[...] * cd_ref[...] + _dot((k * kd_ref[...]).astype(BF16), v, "tn")

    return pl.pallas_call(
        body, name=name, grid=(HEADS, nb),
        in_specs=[tok, tok, tok, gr, tab, col, col, rowv],
        out_specs=[tok, tok, st],
        out_shape=[jax.ShapeDtypeStruct((t, D_MODEL), F32), jax.ShapeDtypeStruct((t, D_MODEL), BF16),
                   jax.ShapeDtypeStruct((HEADS, nb, HEAD_DIM, HEAD_DIM), BF16)],
        scratch_shapes=[pltpu.VMEM((HEAD_DIM, HEAD_DIM), F32)],
        compiler_params=_cparams(("parallel", "arbitrary")),
    )(qr, kr, vb, proj, dmat, qd, kd, cd)


def _retention_bwd(name, dret, o, qr, kr, vb, proj, states, tables):
    t = qr.shape[0]
    nb = t // RET_BLOCK
    dmat, dmat_t, qd, kd, cd = tables
    tok, gr, tab, col, rowv, st = _head_specs(nb, rev=True)

    def body(dr_ref, o_ref, q_ref, k_ref, v_ref, g_ref, st_ref, d_ref, dt_ref, qd_ref, kd_ref, cd_ref,
             dq_ref, dk_ref, dv_ref, dg_ref, gstate):
        n = pl.program_id(1)

        @pl.when(n == 0)
        def _():
            gstate[...] = jnp.zeros(gstate.shape, F32)

        o_v, g, dr = o_ref[...], g_ref[...], dr_ref[...]
        sg = _sigmoid(g)
        r = lax.rsqrt(jnp.mean(o_v * o_v, axis=-1, keepdims=True) + NORM_EPS)
        rn = o_v * r
        d_rn = dr * (g * sg)
        dg_ref[...] = (dr * rn * (sg * (1.0 + g * (1.0 - sg)))).astype(BF16)
        d_o = r * (d_rn - rn * jnp.mean(d_rn * rn, axis=-1, keepdims=True))
        dob = d_o.astype(BF16)

        q, k, v = q_ref[...], k_ref[...], v_ref[...]
        qb, kb = q.astype(BF16), k.astype(BF16)
        qdv, kdv = qd_ref[...], kd_ref[...]
        s_t = (_dot(kb, qb, "nt") * dt_ref[...]).astype(BF16)
        p_t = (_dot(v, dob, "nt") * dt_ref[...]).astype(BF16)
        p = (_dot(dob, v, "nt") * d_ref[...]).astype(BF16)
        stb = st_ref[...]
        gb = gstate[...].astype(BF16)
        dq_ref[...] = _dot(p, kb) + _dot(dob, stb, "nt") * qdv
        dk_ref[...] = _dot(p_t, qb) + _dot(v, gb, "nt") * kdv
        dv_ref[...] = (_dot(s_t, dob) + _dot((k * kdv).astype(BF16), gb)).astype(BF16)
        gstate[...] = gstate[...] * cd_ref[...] + _dot((q * qdv).astype(BF16), dob, "tn")

    return pl.pallas_call(
        body, name=name, grid=(HEADS, nb),
        in_specs=[tok, tok, tok, tok, tok, gr, st, tab, tab, col, col, rowv],
        out_specs=[tok, tok, tok, tok],
        out_shape=[jax.ShapeDtypeStruct((t, D_MODEL), F32), jax.ShapeDtypeStruct((t, D_MODEL), F32),
                   jax.ShapeDtypeStruct((t, D_MODEL), BF16), jax.ShapeDtypeStruct((t, D_MODEL), BF16)],
        scratch_shapes=[pltpu.VMEM((HEAD_DIM, HEAD_DIM), F32)],
        compiler_params=_cparams(("parallel", "arbitrary")),
    )(dret, o, qr, kr, vb, proj, states, dmat, dmat_t, qd, kd, cd)


POOL_TILE = 256


def _pool_tables():
    b = POOL_TILE
    tt = np.arange(b)[:, None]
    jj = np.arange(b)[None, :]
    cur, prev = [], []
    for w in POOL_WINDOWS:
        cur.append(((tt - jj >= 0) & (tt - jj <= w - 1)).astype(np.float32))
        prev.append((tt - (jj - b) <= w - 1).astype(np.float32))
    cur, prev = np.stack(cur), np.stack(prev)
    as16 = lambda v: jnp.asarray(v, dtype=BF16)
    return as16(cur), as16(prev), as16(np.swapaxes(cur, 1, 2)), as16(np.swapaxes(prev, 1, 2))


def _split2(x):
    hi = x.astype(BF16)
    return hi, (x - hi.astype(F32)).astype(BF16)


def _pool_count(n, g):
    tpos = n * POOL_TILE + lax.broadcasted_iota(jnp.int32, (POOL_TILE, 1), 0)
    return jnp.minimum(tpos + 1, jnp.left_shift(2, g)).astype(F32)


def _pool_fwd(name, proj, pool_w, scale, tables):
    t = proj.shape[0]
    nb = t // POOL_TILE
    mc, mp, _, _ = tables
    p_block0 = 4 * D_MODEL // GROUP_DIM
    blk = (POOL_TILE, GROUP_DIM)
    tab = pl.BlockSpec((None, POOL_TILE, POOL_TILE), lambda g, n: (g, 0, 0))

    def body(pc_ref, pp_ref, mc_ref, mp_ref, w_ref, sc_ref, pm_ref, mix_ref, po_ref):
        g, n = pl.program_id(0), pl.program_id(1)
        p = pc_ref[...]
        c_hi, c_lo = _split2(p)
        p_hi, p_lo = _split2(pp_ref[...])
        mcv, mpv = mc_ref[...], mp_ref[...]
        win = _dot(mcv, c_hi) + _dot(mcv, c_lo)
        before = _dot(mpv, p_hi) + _dot(mpv, p_lo)
        win = win + jnp.where(n > 0, before, 0.0)
        pm = (win / _pool_count(n, g) - p).astype(BF16)
        pm_ref[...] = pm
        mixed = _dot(pm, w_ref[...])
        mix_ref[...] = mixed
        po_ref[...] = (mixed * sc_ref[...]).astype(BF16)

    return pl.pallas_call(
        body, name=name, grid=(GROUPS, nb),
        in_specs=[pl.BlockSpec(blk, lambda g, n: (n, p_block0 + g)),
                  pl.BlockSpec(blk, lambda g, n: (jnp.maximum(n - 1, 0), p_block0 + g)),
                  tab, tab,
                  pl.BlockSpec((None, GROUP_DIM, GROUP_DIM), lambda g, n: (g, 0, 0)),
                  pl.BlockSpec((1, GROUP_DIM), lambda g, n: (0, g))],
        out_specs=[pl.BlockSpec(blk, lambda g, n: (n, g))] * 3,
        out_shape=[jax.ShapeDtypeStruct((t, D_MODEL), BF16), jax.ShapeDtypeStruct((t, D_MODEL), F32),
                   jax.ShapeDtypeStruct((t, D_MODEL), BF16)],
        compiler_params=_cparams(("parallel", "parallel")),
    )(proj, proj, mc, mp, pool_w, scale)


def _pool_bwd(name, dpo, pm, mixed, pool_w, scale, tables):
    t = dpo.shape[0]
    nb = t // POOL_TILE
    _, _, mct, mpt = tables
    blk = (POOL_TILE, GROUP_DIM)
    cur = pl.BlockSpec(blk, lambda g, n: (n, g))
    nxt = pl.BlockSpec(blk, lambda g, n: (jnp.minimum(n + 1, nb - 1), g))
    tab = pl.BlockSpec((None, POOL_TILE, POOL_TILE), lambda g, n: (g, 0, 0))
    wspec = pl.BlockSpec((None, GROUP_DIM, GROUP_DIM), lambda g, n: (g, 0, 0))
    sspec = pl.BlockSpec((1, GROUP_DIM), lambda g, n: (0, g))

    def body(dc_ref, dn_ref, pm_ref, mix_ref, mct_ref, mpt_ref, w_ref, sc_ref, dp_ref, dw_ref, ds_ref):
        g, n = pl.program_id(0), pl.program_id(1)
        dc, sc, w = dc_ref[...], sc_ref[...], w_ref[...]
        dmix_c = (dc * sc).astype(BF16)
        dmix_n = (dn_ref[...] * sc).astype(BF16)
        dpm_c = _dot(dmix_c, w, "nt")
        dpm_n = _dot(dmix_n, w, "nt")
        e_hi, e_lo = _split2(dpm_c / _pool_count(n, g))
        f_hi, f_lo = _split2(dpm_n / _pool_count(n + 1, g))
        mctv, mptv = mct_ref[...], mpt_ref[...]
        back = _dot(mctv, e_hi) + _dot(mctv, e_lo)
        after = _dot(mptv, f_hi) + _dot(mptv, f_lo)
        dp_ref[...] = (back + jnp.where(n < nb - 1, after, 0.0) - dpm_c).astype(BF16)
        dw_part = _dot(pm_ref[...], dmix_c, "tn")
        ds_part = jnp.sum(dc * mix_ref[...], axis=0, keepdims=True)

        @pl.when(n == 0)
        def _():
            dw_ref[...] = dw_part
            ds_ref[...] = ds_part

        @pl.when(n > 0)
        def _():
            dw_ref[...] += dw_part
            ds_ref[...] += ds_part

    return pl.pallas_call(
        body, name=name, grid=(GROUPS, nb),
        in_specs=[cur, nxt, cur, cur, tab, tab, wspec, sspec],
        out_specs=[cur, wspec, sspec],
        out_shape=[jax.ShapeDtypeStruct((t, D_MODEL), BF16), jax.ShapeDtypeStruct((GROUPS, GROUP_DIM, GROUP_DIM), F32),
                   jax.ShapeDtypeStruct((1, D_MODEL), F32)],
        compiler_params=_cparams(("parallel", "arbitrary")),
    )(dpo, dpo, pm, mixed, mct, mpt, pool_w, scale)


GATE0_BLOCK, GATE1_BLOCK = 5, 6


def _merge_fwd(name, ret, po, w_ru, w_pu, proj, bias, tm=512):
    t = ret.shape[0]

    def body(r_ref, p_ref, wr_ref, wp_ref, g0_ref, g1_ref, b_ref, m_ref, ru_ref, pu_ref):
        ru = _dot(r_ref[...], wr_ref[...])
        pu = _dot(p_ref[...], wp_ref[...])
        ru_ref[...] = ru
        pu_ref[...] = pu
        m_ref[...] = (_sigmoid(g0_ref[...] + b_ref[0:1, :]) * ru + _sigmoid(g1_ref[...] + b_ref[1:2, :]) * pu).astype(BF16)

    row = _row_spec(tm, D_MODEL)
    wspec = _full_spec((D_MODEL, D_MODEL))
    return pl.pallas_call(
        body, name=name, grid=(t // tm,),
        in_specs=[row, row, wspec, wspec, _row_spec(tm, D_MODEL, GATE0_BLOCK), _row_spec(tm, D_MODEL, GATE1_BLOCK),
                  _full_spec((2, D_MODEL))],
        out_specs=[row, row, row],
        out_shape=[jax.ShapeDtypeStruct((t, D_MODEL), BF16), jax.ShapeDtypeStruct((t, D_MODEL), F32),
                   jax.ShapeDtypeStruct((t, D_MODEL), F32)],
        compiler_params=_cparams(("parallel",)),
    )(ret, po, w_ru, w_pu, proj, proj, bias)


def _merge_bwd(name, dm, ru, pu, proj, bias, tm=512):
    t = dm.shape[0]

    def body(dm_ref, ru_ref, pu_ref, g0_ref, g1_ref, b_ref, dru_ref, dpu_ref, dg0_ref, dg1_ref, db_ref):
        i = pl.program_id(0)
        d = dm_ref[...]
        s0 = _sigmoid(g0_ref[...] + b_ref[0:1, :])
        s1 = _sigmoid(g1_ref[...] + b_ref[1:2, :])
        dru_ref[...] = (d * s0).astype(BF16)
        dpu_ref[...] = (d * s1).astype(BF16)
        dg0 = d * ru_ref[...] * (s0 * (1.0 - s0))
        dg1 = d * pu_ref[...] * (s1 * (1.0 - s1))
        dg0_ref[...] = dg0.astype(BF16)
        dg1_ref[...] = dg1.astype(BF16)
        part0 = jnp.sum(dg0, axis=0, keepdims=True)
        part1 = jnp.sum(dg1, axis=0, keepdims=True)

        @pl.when(i == 0)
        def _():
            db_ref[0:1, :] = part0
            db_ref[1:2, :] = part1

        @pl.when(i > 0)
        def _():
            db_ref[0:1, :] += part0
            db_ref[1:2, :] += part1

    row = _row_spec(tm, D_MODEL)
    return pl.pallas_call(
        body, name=name, grid=(t // tm,),
        in_specs=[row, row, row, _row_spec(tm, D_MODEL, GATE0_BLOCK), _row_spec(tm, D_MODEL, GATE1_BLOCK),
                  _full_spec((2, D_MODEL))],
        out_specs=[row, row, row, row, _full_spec((2, D_MODEL))],
        out_shape=[jax.ShapeDtypeStruct((t, D_MODEL), BF16)] * 4 + [jax.ShapeDtypeStruct((2, D_MODEL), F32)],
        compiler_params=_cparams(("arbitrary",)),
    )(dm, ru, pu, proj, proj, bias)


def _half_scale(acc):
    return (FFN_RES_WEIGHT * acc,)


def _residual_half(acc, res):
    return (res + FFN_RES_WEIGHT * acc,)


def _residual(acc, res):
    return (res + acc,)


FF_TILE = D_FF // 2


def _ffn_in(name, nrm, w_in, tm=512):
    t = nrm.shape[0]
    nj = D_FF // FF_TILE

    def body(n_ref, wg_ref, wu_ref, a_ref, mid_ref):
        nv = n_ref[...]
        gate = _dot(nv, wg_ref[...])
        up = _dot(nv, wu_ref[...])
        a_ref[0] = gate
        a_ref[1] = up
        mid_ref[...] = (gate * _sigmoid(gate) * up).astype(BF16)

    return pl.pallas_call(
        body, name=name, grid=(t // tm, nj),
        in_specs=[pl.BlockSpec((tm, D_MODEL), lambda i, j: (i, 0)),
                  pl.BlockSpec((D_MODEL, FF_TILE), lambda i, j: (0, j)),
                  pl.BlockSpec((D_MODEL, FF_TILE), lambda i, j: (0, j + nj))],
        out_specs=[pl.BlockSpec((2, tm, FF_TILE), lambda i, j: (0, i, j)), pl.BlockSpec((tm, FF_TILE), lambda i, j: (i, j))],
        out_shape=[jax.ShapeDtypeStruct((2, t, D_FF), F32), jax.ShapeDtypeStruct((t, D_FF), BF16)],
        compiler_params=_cparams(("parallel", "parallel")),
    )(nrm, w_in, w_in)


def _ffn_dact(name, dout_b, w_out, a, tm=512):
    t = dout_b.shape[0]

    def body(d_ref, w_ref, a_ref, da_ref):
        dm = FFN_RES_WEIGHT * _dot(d_ref[...], w_ref[...], "nt")
        gate, up = a_ref[0], a_ref[1]
        s = _sigmoid(gate)
        da_ref[0] = (dm * up * (s * (1.0 + gate * (1.0 - s)))).astype(BF16)
        da_ref[1] = (dm * (gate * s)).astype(BF16)

    blk = pl.BlockSpec((2, tm, FF_TILE), lambda i, j: (0, i, j))
    return pl.pallas_call(
        body, name=name, grid=(t // tm, D_FF // FF_TILE),
        in_specs=[pl.BlockSpec((tm, D_MODEL), lambda i, j: (i, 0)), pl.BlockSpec((FF_TILE, D_MODEL), lambda i, j: (j, 0)), blk],
        out_specs=blk,
        out_shape=jax.ShapeDtypeStruct((2, t, D_FF), BF16),
        compiler_params=_cparams(("parallel", "parallel")),
    )(dout_b, w_out, a)


def _ffn_fwd(tag, h, g, w_in, w_out):
    t = h.shape[0]
    nrm = _rmsnorm_fwd(f"{tag}_norm", h, g)
    a, mid = _ffn_in(f"{tag}_in", nrm, w_in, tm=min(512, t))
    out = _matmul(f"{tag}_out", mid, w_out, "nn", t, D_MODEL, D_FF, 512, D_MODEL, D_FF, [F32],
                  extras=(h,), epilogue=_residual_half)
    return out, (nrm, a, mid)


def _ffn_bwd(tag, h, g, w_in, w_out, saved, dout, dout_b):
    t = h.shape[0]
    nrm, a, mid = saved
    nj = D_FF // FF_TILE
    tk = min(1024, t)
    d_w_out = _matmul(f"{tag}_dwout", mid, dout_b, "tn", D_FF, D_MODEL, t, 1408, D_MODEL, 1024, [BF16], epilogue=_half_scale)
    da = _ffn_dact(f"{tag}_dact", dout_b, w_out, a, tm=min(512, t))
    d_w_in = _matmul(f"{tag}_dwin", nrm, da, "tn", D_MODEL, 2 * D_FF, t, D_MODEL, FF_TILE, tk, [BF16],
                     b_spec=pl.BlockSpec((None, tk, FF_TILE), lambda i, j, kk: (j // nj, kk, j % nj)))
    tm = min(1024, t)
    dn = _matmul(f"{tag}_dn", da, w_in, "nt", t, D_MODEL, 2 * D_FF, tm, D_MODEL, FF_TILE, [F32],
                 a_spec=pl.BlockSpec((None, tm, FF_TILE), lambda i, j, kk: (kk // nj, i, kk % nj)))
    dh, dh_b, dg = _rmsnorm_bwd(f"{tag}_dnorm", h, g, dn, dout)
    return dh, dh_b, dg, d_w_in, d_w_out


def _local_step(x, target, w):
    t = x.shape[0]
    cos, sin = _rope_tables(t)
    rtab = _retention_tables()
    ptab = _pool_tables()

    h1, s1 = _ffn_fwd("ffn1", x, w["norm_ffn1"], w["ffn1_w_in"], w["ffn1_w_out"])
    u = _rmsnorm_fwd("mix_norm", h1, w["norm_mix"])
    proj = _matmul("mix_in", u, w["w_in"], "nn", t, IN_WIDTH, D_MODEL, 1024, 1024, D_MODEL, [F32])
    qr, kr, vb = _rotary_fwd("rotary", proj, cos, sin)
    o, ret, states = _retention_fwd("retention", qr, kr, vb, proj, rtab)
    pm, mixed, po = _pool_fwd("pool", proj, w["pool_w"], w["pool_scale"], ptab)
    merged, ru, pu = _merge_fwd("merge", ret, po, w["w_ret_up"], w["w_pool_up"], proj, w["gate_bias"])
    h2 = _matmul("mix_out", merged, w["w_out"], "nn", t, D_MODEL, D_MODEL, 512, D_MODEL, D_MODEL, [F32],
                 extras=(h1,), epilogue=_residual)
    h3, s2 = _ffn_fwd("ffn2", h2, w["norm_ffn2"], w["ffn2_w_in"], w["ffn2_w_out"])
    dh3, dh3_b, dg_final, loss = _loss_and_grad("loss", h3, w["norm_final"], target)

    dh2, dh2_b, dg_ffn2, d_ffn2_in, d_ffn2_out = _ffn_bwd("ffn2", h2, w["norm_ffn2"], w["ffn2_w_in"], w["ffn2_w_out"], s2,
                                                          dh3, dh3_b)
    dm = _matmul("mix_dmerged", dh2_b, w["w_out"], "nt", t, D_MODEL, D_MODEL, 1024, D_MODEL, D_MODEL, [F32])
    d_w_out = _matmul("mix_dwout", merged, dh2_b, "tn", D_MODEL, D_MODEL, t, D_MODEL, D_MODEL, 1024, [BF16])
    dru, dpu, dg0, dg1, d_bias = _merge_bwd("merge_bwd", dm, ru, pu, proj, w["gate_bias"])
    dret = _matmul("mix_dret", dru, w["w_ret_up"], "nt", t, D_MODEL, D_MODEL, 1024, D_MODEL, D_MODEL, [F32])
    d_w_ru = _matmul("mix_dwru", ret, dru, "tn", D_MODEL, D_MODEL, t, D_MODEL, D_MODEL, 1024, [BF16])
    dpo = _matmul("mix_dpool", dpu, w["w_pool_up"], "nt", t, D_MODEL, D_MODEL, 1024, D_MODEL, D_MODEL, [F32])
    d_w_pu = _matmul("mix_dwpu", po, dpu, "tn", D_MODEL, D_MODEL, t, D_MODEL, D_MODEL, 1024, [BF16])
    dp, d_pool_w, d_scale = _pool_bwd("pool_bwd", dpo, pm, mixed, w["pool_w"], w["pool_scale"], ptab)
    dqr, dkr, dv, dgr = _retention_bwd("retention_bwd", dret, o, qr, kr, vb, proj, states, rtab)
    dq, dk = _rotary_bwd("rotary_bwd", dqr, dkr, cos, sin)
    dproj = jnp.concatenate([dq, dk, dv, dgr, dp, dg0, dg1], axis=1)
    d_w_in = _matmul("mix_dwin", u, dproj, "tn", D_MODEL, IN_WIDTH, t, D_MODEL, 1024, 1024, [BF16])
    du = _matmul("mix_du", dproj, w["w_in"], "nt", t, D_MODEL, IN_WIDTH, 1024, D_MODEL, 1024, [F32])
    dh1, dh1_b, dg_mix = _rmsnorm_bwd("mix_dnorm", h1, w["norm_mix"], du, dh2)
    dx, _, dg_ffn1, d_ffn1_in, d_ffn1_out = _ffn_bwd("ffn1", x, w["norm_ffn1"], w["ffn1_w_in"], w["ffn1_w_out"], s1, dh1, dh1_b)

    big = dict(ffn1_w_in=d_ffn1_in, ffn1_w_out=d_ffn1_out, w_in=d_w_in, pool_w=d_pool_w.astype(BF16), w_ret_up=d_w_ru,
               w_pool_up=d_w_pu, w_out=d_w_out, ffn2_w_in=d_ffn2_in, ffn2_w_out=d_ffn2_out)
    small = dict(norm_ffn1=dg_ffn1, norm_mix=dg_mix, gate_bias=d_bias, pool_scale=d_scale, norm_ffn2=dg_ffn2,
                 norm_final=dg_final)
    return loss[0, 0], dx, big, small


BIG = ("ffn1_w_in", "ffn1_w_out", "w_in", "pool_w", "w_ret_up", "w_pool_up", "w_out", "ffn2_w_in", "ffn2_w_out")
KIND = dict(ffn1_w_in="col", ffn1_w_out="row", w_in="col", pool_w="pool", w_ret_up="row", w_pool_up="row", w_out="row",
            ffn2_w_in="col", ffn2_w_out="row")
ANY = pl.BlockSpec(memory_space=pl.ANY)


def _place():
    x, y, c = lax.axis_index("x"), lax.axis_index("y"), lax.axis_index("c")
    chips = [(1 - x, y), (x, 1 - y), (1 - x, 1 - y)]
    return x, y, c, chips


def _full_view_shape(kind, local_shape):
    if kind == "col":
        return (2, local_shape[0] // 2, N_CHIPS * local_shape[1])
    if kind == "row":
        return (N_CHIPS, 2, local_shape[0] // 2, local_shape[1])
    return (GROUPS, N_CHIPS, 2, local_shape[1] // 2, local_shape[2])


def _local_view(kind, arr):
    if kind == "pool":
        return arr.reshape(GROUPS, 2, arr.shape[1] // 2, arr.shape[2])
    return arr.reshape(2, arr.shape[0] // 2, arr.shape[1])


def _blk(kind, ref, s, c):
    if kind == "col":
        cs = ref.shape[2] // N_CHIPS
        return ref.at[c, :, pl.ds(pl.multiple_of(s * cs, 128), cs)]
    if kind == "row":
        return ref.at[s, c]
    return ref.at[:, s, c]


def _half(kind, ref, c):
    return ref.at[:, c] if kind == "pool" else ref.at[c]


def _shard(kind, ref, s):
    if kind == "col":
        cs = ref.shape[2] // N_CHIPS
        return ref.at[:, :, pl.ds(pl.multiple_of(s * cs, 128), cs)]
    if kind == "row":
        return ref.at[s]
    return ref.at[:, s]


def _gather_weights(shards):
    names = list(BIG)
    kinds = [KIND[nm] for nm in names]
    locs = [_local_view(KIND[nm], shards[nm]) for nm in names]
    n = len(names)

    def body(*refs):
        loc, full = refs[:n], refs[n:2 * n]
        send_sems, recv_sems = refs[2 * n:]
        x, y, c, chips = _place()
        s = 2 * x + y
        sib = (x, y, 1 - c)

        def remote(a, k, src, dst, to):
            return pltpu.make_async_remote_copy(src_ref=src, dst_ref=dst, send_sem=send_sems.at[a * 7 + k],
                                                recv_sem=recv_sems.at[a * 7 + k], device_id=to, device_id_type=MESH)

        sends = []
        for a in range(n):
            cp = remote(a, 6, loc[a], _shard(kinds[a], full[a], s), sib)
            cp.start()
            sends.append(cp)
            for j, chip in enumerate(chips):
                cp = remote(a, j, _half(kinds[a], loc[a], c), _blk(kinds[a], full[a], s, c), (*chip, c))
                cp.start()
                sends.append(cp)
        for a in range(n):
            for j, (px, py) in enumerate(chips):
                theirs = _blk(kinds[a], full[a], 2 * px + py, c)
                remote(a, j, theirs, theirs, sib).wait_recv()
                cp = remote(a, 3 + j, theirs, theirs, sib)
                cp.start()
                sends.append(cp)
        for a in range(n):
            for j, (px, py) in enumerate(chips):
                from_sib = _blk(kinds[a], full[a], 2 * px + py, 1 - c)
                remote(a, 3 + j, from_sib, from_sib, sib).wait_recv()
            own = _shard(kinds[a], full[a], s)
            remote(a, 6, own, own, sib).wait_recv()
        for cp in sends:
            cp.wait_send()

    outs = pl.pallas_call(
        body, name="gather_weights",
        in_specs=[ANY] * n, out_specs=[ANY] * n,
        out_shape=[jax.ShapeDtypeStruct(_full_view_shape(k, shards[nm].shape), BF16) for nm, k in zip(names, kinds)],
        scratch_shapes=[pltpu.SemaphoreType.DMA((7 * n,)), pltpu.SemaphoreType.DMA((7 * n,))],
    )(*locs)
    full = {}
    for nm, k, o in zip(names, kinds, outs):
        if k == "col":
            full[nm] = o.reshape(o.shape[0] * o.shape[1], o.shape[2])
        elif k == "row":
            full[nm] = o.reshape(-1, o.shape[3])
        else:
            full[nm] = o.reshape(GROUPS, -1, o.shape[4])
    return full


def _grad_view(kind, g):
    if kind == "col":
        return g.reshape(2, g.shape[0] // 2, g.shape[1])
    if kind == "row":
        return g.reshape(N_CHIPS, 2, g.shape[0] // (2 * N_CHIPS), g.shape[1])
    return g.reshape(GROUPS, N_CHIPS, 2, g.shape[1] // (2 * N_CHIPS), g.shape[2])


def _pair_exchange(views):
    names = list(BIG)
    kinds = [KIND[nm] for nm in names]
    n = len(names)

    def other_half(kind, ref, c):
        if kind == "col":
            return ref.at[c]
        if kind == "row":
            return ref.at[:, c]
        return ref.at[:, :, c]

    def body(*refs):
        g, got = refs[:n], refs[n:2 * n]
        send_sems, recv_sems = refs[2 * n:]
        x, y, c, _ = _place()
        sib = (x, y, 1 - c)
        cps = []
        for a in range(n):
            cp = pltpu.make_async_remote_copy(src_ref=other_half(kinds[a], g[a], 1 - c), dst_ref=got[a], send_sem=send_sems.at[a],
                                              recv_sem=recv_sems.at[a], device_id=sib, device_id_type=MESH)
            cp.start()
            cps.append(cp)
        for cp in cps:
            cp.wait()

    def got_shape(kind, v):
        if kind == "col":
            return v.shape[1:]
        if kind == "row":
            return (v.shape[0],) + v.shape[2:]
        return v.shape[:2] + v.shape[3:]

    outs = pl.pallas_call(
        body, name="grad_pair_exchange",
        in_specs=[ANY] * n, out_specs=[ANY] * n,
        out_shape=[jax.ShapeDtypeStruct(got_shape(k, views[nm]), BF16) for nm, k in zip(names, kinds)],
        scratch_shapes=[pltpu.SemaphoreType.DMA((n,)), pltpu.SemaphoreType.DMA((n,))],
    )(*[views[nm] for nm in names])
    return dict(zip(names, outs))


def _pair_sum(name, kind, view, got, c_arr):
    if kind == "col":
        _, rows, cols = view.shape
        tr = 128
        grid = (rows // tr,)
        v_spec = pl.BlockSpec((None, tr, cols), lambda i, c: (c[0], i, 0))
        g_spec = pl.BlockSpec((tr, cols), lambda i, c: (i, 0))
    elif kind == "row":
        _, _, rows, cols = view.shape
        grid = (N_CHIPS,)
        v_spec = pl.BlockSpec((None, None, rows, cols), lambda i, c: (i, c[0], 0, 0))
        g_spec = pl.BlockSpec((None, rows, cols), lambda i, c: (i, 0, 0))
    else:
        _, _, _, rows, cols = view.shape
        grid = (GROUPS,)
        v_spec = pl.BlockSpec((None, N_CHIPS, None, rows, cols), lambda i, c: (i, 0, c[0], 0, 0))
        g_spec = pl.BlockSpec((None, N_CHIPS, rows, cols), lambda i, c: (i, 0, 0, 0))

    def body(c_ref, v_ref, g_ref, o_ref):
        o_ref[...] = (v_ref[...].astype(F32) + g_ref[...].astype(F32)).astype(BF16)

    return pl.pallas_call(
        body, name=name,
        grid_spec=pltpu.PrefetchScalarGridSpec(num_scalar_prefetch=1, grid=grid, in_specs=[v_spec, g_spec], out_specs=g_spec),
        out_shape=jax.ShapeDtypeStruct(got.shape, BF16),
        compiler_params=_cparams(("parallel",)),
    )(c_arr, view, got)


def _piece(kind, ref, s):
    if kind == "col":
        cs = ref.shape[1] // N_CHIPS
        return ref.at[:, pl.ds(pl.multiple_of(s * cs, 128), cs)]
    if kind == "row":
        return ref.at[s]
    return ref.at[:, s]


def _piece_shape(kind, shape):
    if kind == "col":
        return (shape[0], shape[1] // N_CHIPS)
    if kind == "row":
        return shape[1:]
    return (shape[0],) + shape[2:]


def _shard_exchange(psums):
    names = list(BIG)
    kinds = [KIND[nm] for nm in names]
    n = len(names)

    def body(*refs):
        p, got = refs[:n], refs[n:2 * n]
        send_sems, recv_sems = refs[2 * n:]
        x, y, c, chips = _place()
        cps = []
        for a in range(n):
            for j, (px, py) in enumerate(chips):
                cp = pltpu.make_async_remote_copy(src_ref=_piece(kinds[a], p[a], 2 * px + py), dst_ref=got[a].at[j],
                                                  send_sem=send_sems.at[3 * a + j], recv_sem=recv_sems.at[3 * a + j],
                                                  device_id=(px, py, c), device_id_type=MESH)
                cp.start()
                cps.append(cp)
        for cp in cps:
            cp.wait()

    outs = pl.pallas_call(
        body, name="grad_shard_exchange",
        in_specs=[ANY] * n, out_specs=[ANY] * n,
        out_shape=[jax.ShapeDtypeStruct((3,) + _piece_shape(k, psums[nm].shape), BF16) for nm, k in zip(names, kinds)],
        scratch_shapes=[pltpu.SemaphoreType.DMA((3 * n,)), pltpu.SemaphoreType.DMA((3 * n,))],
    )(*[psums[nm] for nm in names])
    return dict(zip(names, outs))


def _shard_sum(name, kind, psum, got, sc_arr):
    if kind == "col":
        rows, cols = psum.shape
        cs = cols // N_CHIPS
        tr = 128
        grid = (rows // tr,)
        p_spec = pl.BlockSpec((tr, cs), lambda i, sc: (i, sc[0]))
        g_spec = pl.BlockSpec((3, tr, cs), lambda i, sc: (0, i, 0))
        o_spec = pl.BlockSpec((None, tr, cs), lambda i, sc: (sc[1], i, 0))
        out_shape = (2, rows, cs)
    elif kind == "row":
        _, rows, cols = psum.shape
        grid = (1,)
        p_spec = pl.BlockSpec((None, rows, cols), lambda i, sc: (sc[0], 0, 0))
        g_spec = pl.BlockSpec((3, rows, cols), lambda i, sc: (0, 0, 0))
        o_spec = pl.BlockSpec((None, rows, cols), lambda i, sc: (sc[1], 0, 0))
        out_shape = (2, rows, cols)
    else:
        _, _, rows, cols = psum.shape
        grid = (1,)
        p_spec = pl.BlockSpec((GROUPS, None, rows, cols), lambda i, sc: (0, sc[0], 0, 0))
        g_spec = pl.BlockSpec((3, GROUPS, rows, cols), lambda i, sc: (0, 0, 0, 0))
        o_spec = pl.BlockSpec((GROUPS, None, rows, cols), lambda i, sc: (0, sc[1], 0, 0))
        out_shape = (GROUPS, 2, rows, cols)

    def body(sc_ref, p_ref, g_ref, o_ref):
        o_ref[...] = ((p_ref[...].astype(F32) + g_ref[0].astype(F32)) + g_ref[1].astype(F32)) + g_ref[2].astype(F32)

    return pl.pallas_call(
        body, name=name,
        grid_spec=pltpu.PrefetchScalarGridSpec(num_scalar_prefetch=1, grid=grid, in_specs=[p_spec, g_spec], out_specs=o_spec),
        out_shape=jax.ShapeDtypeStruct(out_shape, F32),
        compiler_params=_cparams(("parallel",)),
    )(sc_arr, psum, got)


def _half_exchange(bufs):
    names = list(BIG)
    kinds = [KIND[nm] for nm in names]
    n = len(names)

    def body(*refs):
        out = refs[n:2 * n]
        send_sems, recv_sems = refs[2 * n:]
        x, y, c, _ = _place()
        sib = (x, y, 1 - c)
        cps = []
        for a in range(n):
            mine = _half(kinds[a], out[a], c)
            cp = pltpu.make_async_remote_copy(src_ref=mine, dst_ref=mine, send_sem=send_sems.at[a], recv_sem=recv_sems.at[a],
                                              device_id=sib, device_id_type=MESH)
            cp.start()
            cps.append(cp)
        for a, cp in enumerate(cps):
            cp.wait_send()
            theirs = _half(kinds[a], out[a], 1 - c)
            pltpu.make_async_remote_copy(src_ref=theirs, dst_ref=theirs, send_sem=send_sems.at[a], recv_sem=recv_sems.at[a],
                                         device_id=sib, device_id_type=MESH).wait_recv()

    outs = pl.pallas_call(
        body, name="grad_half_exchange",
        in_specs=[ANY] * n, out_specs=[ANY] * n,
        out_shape=[jax.ShapeDtypeStruct(bufs[nm].shape, F32) for nm in names],
        input_output_aliases={a: a for a in range(n)},
        scratch_shapes=[pltpu.SemaphoreType.DMA((n,)), pltpu.SemaphoreType.DMA((n,))],
    )(*[bufs[nm] for nm in names])
    return dict(zip(names, outs))


N_DEV = 8
SMALL_ROWS = 8


def _all_reduce_small(name, v):
    def body(v_ref, o_ref, buf, send_sems, recv_sems):
        x, y, c, _ = _place()
        me = 4 * x + 2 * y + c
        buf[me] = v_ref[...]
        cps = []
        for r in range(1, N_DEV):
            to = (x ^ (r >> 2), y ^ ((r >> 1) & 1), c ^ (r & 1))
            cp = pltpu.make_async_remote_copy(src_ref=v_ref, dst_ref=buf.at[me], send_sem=send_sems.at[r - 1],
                                              recv_sem=recv_sems.at[r - 1], device_id=to, device_id_type=MESH)
            cp.start()
            cps.append(cp)
        for r in range(1, N_DEV):
            pltpu.make_async_remote_copy(src_ref=v_ref, dst_ref=buf.at[me ^ r], send_sem=send_sems.at[r - 1],
                                         recv_sem=recv_sems.at[r - 1], device_id=(x, y, c), device_id_type=MESH).wait_recv()
        for cp in cps:
            cp.wait_send()
        acc = buf[0]
        for d in range(1, N_DEV):
            acc = acc + buf[d]
        o_ref[...] = acc

    vm = pl.BlockSpec(memory_space=pltpu.VMEM)
    return pl.pallas_call(
        body, name=name, in_specs=[vm], out_specs=vm,
        out_shape=jax.ShapeDtypeStruct((SMALL_ROWS, D_MODEL), F32),
        scratch_shapes=[pltpu.VMEM((N_DEV, SMALL_ROWS, D_MODEL), F32), pltpu.SemaphoreType.DMA((N_DEV - 1,)),
                        pltpu.SemaphoreType.DMA((N_DEV - 1,))],
    )(v)


def _adamw(name, w, g, m, v):
    rows, cols = w.shape
    tr = next((c for c in (256, 176, 128, 64, 32, 8) if rows % c == 0), rows)
    spec = pl.BlockSpec((tr, cols), lambda i: (i, 0))

    def body(w_ref, g_ref, m_ref, v_ref, d_ref, mo_ref, vo_ref):
        gv = g_ref[...]
        m_new = ADAM_B1 * m_ref[...] + (1.0 - ADAM_B1) * gv
        v_new = ADAM_B2 * v_ref[...] + (1.0 - ADAM_B2) * jnp.square(gv)
        m_hat = m_new / (1.0 - ADAM_B1 ** ADAM_STEP)
        v_hat = v_new / (1.0 - ADAM_B2 ** ADAM_STEP)
        d_ref[...] = -ADAM_LR * (m_hat / (jnp.sqrt(v_hat) + ADAM_EPS) + ADAM_WD * w_ref[...])
        mo_ref[...] = m_new
        vo_ref[...] = v_new

    return pl.pallas_call(
        body, name=name, grid=(rows // tr,),
        in_specs=[spec] * 4, out_specs=[spec] * 3,
        out_shape=[jax.ShapeDtypeStruct((rows, cols), F32)] * 3,
        compiler_params=_cparams(("parallel",)),
    )(w, g, m, v)


WEIGHTS = ("norm_ffn1", "ffn1_w_in", "ffn1_w_out", "norm_mix", "w_in", "gate_bias", "pool_w", "pool_scale", "w_ret_up",
           "w_pool_up", "w_out", "norm_ffn2", "ffn2_w_in", "ffn2_w_out", "norm_final")
SMALL_ROW = dict(norm_ffn1=0, norm_mix=1, gate_bias=2, pool_scale=4, norm_ffn2=5, norm_final=6)


def _as2d(a):
    return a.reshape(-1, a.shape[-1])


def kernel(x, norm_ffn1, ffn1_w_in, ffn1_w_out, norm_mix, w_in, gate_bias, pool_w, pool_scale, w_ret_up, w_pool_up, w_out, norm_ffn2, ffn2_w_in, ffn2_w_out, norm_final, loss_target, m_norm_ffn1, m_ffn1_w_in, m_ffn1_w_out, m_norm_mix, m_w_in, m_gate_bias, m_pool_w, m_pool_scale, m_w_ret_up, m_w_pool_up, m_w_out, m_norm_ffn2, m_ffn2_w_in, m_ffn2_w_out, m_norm_final, v_norm_ffn1, v_ffn1_w_in, v_ffn1_w_out, v_norm_mix, v_w_in, v_gate_bias, v_pool_w, v_pool_scale, v_w_ret_up, v_w_pool_up, v_w_out, v_norm_ffn2, v_ffn2_w_in, v_ffn2_w_out, v_norm_final):
    wt = dict(norm_ffn1=norm_ffn1, ffn1_w_in=ffn1_w_in, ffn1_w_out=ffn1_w_out, norm_mix=norm_mix, w_in=w_in, gate_bias=gate_bias,
              pool_w=pool_w, pool_scale=pool_scale, w_ret_up=w_ret_up, w_pool_up=w_pool_up, w_out=w_out, norm_ffn2=norm_ffn2,
              ffn2_w_in=ffn2_w_in, ffn2_w_out=ffn2_w_out, norm_final=norm_final)
    mom = dict(norm_ffn1=m_norm_ffn1, ffn1_w_in=m_ffn1_w_in, ffn1_w_out=m_ffn1_w_out, norm_mix=m_norm_mix, w_in=m_w_in,
               gate_bias=m_gate_bias, pool_w=m_pool_w, pool_scale=m_pool_scale, w_ret_up=m_w_ret_up, w_pool_up=m_w_pool_up,
               w_out=m_w_out, norm_ffn2=m_norm_ffn2, ffn2_w_in=m_ffn2_w_in, ffn2_w_out=m_ffn2_w_out, norm_final=m_norm_final)
    var = dict(norm_ffn1=v_norm_ffn1, ffn1_w_in=v_ffn1_w_in, ffn1_w_out=v_ffn1_w_out, norm_mix=v_norm_mix, w_in=v_w_in,
               gate_bias=v_gate_bias, pool_w=v_pool_w, pool_scale=v_pool_scale, w_ret_up=v_w_ret_up, w_pool_up=v_w_pool_up,
               w_out=v_w_out, norm_ffn2=v_norm_ffn2, ffn2_w_in=v_ffn2_w_in, ffn2_w_out=v_ffn2_w_out, norm_final=v_norm_final)

    ax, ay, ac = lax.axis_index("x"), lax.axis_index("y"), lax.axis_index("c")
    chip = 2 * ax + ay
    c_arr = jnp.reshape(ac, (1,)).astype(jnp.int32)
    sc_arr = jnp.stack([chip, ac]).astype(jnp.int32)
    bias_cols = gate_bias.shape[-1]

    shards = {nm: wt[nm][0].astype(BF16) for nm in BIG}
    full = _gather_weights(shards)
    placed = lax.dynamic_update_slice(jnp.zeros((SMALL_ROWS, D_MODEL), F32), gate_bias[0], (0, chip * bias_cols))
    bias_full = _all_reduce_small("gather_gate_bias", jnp.where(ac == 0, placed, 0.0))[:2]
    full.update(norm_ffn1=norm_ffn1, norm_mix=norm_mix, norm_ffn2=norm_ffn2, pool_scale=pool_scale,
                norm_final=norm_final.reshape(1, D_MODEL), gate_bias=bias_full)

    loss_local, dx, big, small = _local_step(x[0], loss_target[0], full)
    loss = lax.psum(loss_local, ("x", "y", "c"))

    packed = jnp.concatenate([small["norm_ffn1"], small["norm_mix"], small["gate_bias"], small["pool_scale"],
                              small["norm_ffn2"], small["norm_final"], jnp.zeros((1, D_MODEL), F32)], axis=0)
    small_sum = _all_reduce_small("reduce_small_grads", packed)
    views = {nm: _grad_view(KIND[nm], big[nm]) for nm in BIG}
    from_sib = _pair_exchange(views)
    psums = {nm: _pair_sum(f"pair_sum_{nm}", KIND[nm], views[nm], from_sib[nm], c_arr) for nm in BIG}
    from_chips = _shard_exchange(psums)
    halves = {nm: _shard_sum(f"shard_sum_{nm}", KIND[nm], psums[nm], from_chips[nm], sc_arr) for nm in BIG}
    reduced = _half_exchange(halves)

    grads = {}
    for nm in BIG:
        grads[nm] = reduced[nm].reshape(wt[nm].shape)
    for nm in ("norm_ffn1", "norm_mix", "pool_scale", "norm_ffn2"):
        grads[nm] = small_sum[SMALL_ROW[nm]][None, :]
    grads["norm_final"] = small_sum[SMALL_ROW["norm_final"]]
    grads["gate_bias"] = lax.dynamic_slice(small_sum, (SMALL_ROW["gate_bias"], chip * bias_cols), (2, bias_cols))[None]

    delta, new_m, new_v = {}, {}, {}
    for nm in WEIGHTS:
        shape = wt[nm].shape
        d, m2, v2 = _adamw(f"adamw_{nm}", _as2d(wt[nm]), _as2d(grads[nm]), _as2d(mom[nm]), _as2d(var[nm]))
        delta[nm], new_m[nm], new_v[nm] = d.reshape(shape), m2.reshape(shape), v2.reshape(shape)

    return (loss, dx[None], *[grads[nm] for nm in WEIGHTS], *[delta[nm] for nm in WEIGHTS],
            *[new_m[nm] for nm in WEIGHTS], *[new_v[nm] for nm in WEIGHTS])
```

```python
import functools

import numpy as np
import jax
import jax.numpy as jnp
from jax import lax
from jax.experimental import pallas as pl
from jax.experimental.pallas import tpu as pltpu

F32 = jnp.float32
BF16 = jnp.bfloat16
MESH = pl.DeviceIdType.MESH

D_MODEL = 1024
D_FF = 2816
HEADS = 4
HEAD_DIM = 256
GROUPS = 4
GROUP_DIM = 256
POOL_WINDOWS = (2, 4, 8, 16)
IN_WIDTH = 7 * D_MODEL
ROPE_BASE = 10000.0
NORM_EPS = 1e-6
FFN_RES_WEIGHT = 0.5
ADAM_LR, ADAM_B1, ADAM_B2, ADAM_EPS, ADAM_WD, ADAM_STEP = 0.001, 0.9, 0.999, 1e-08, 0.01, 10

N_CHIPS = 4
RET_BLOCK = 256
V7X_VMEM_LIMIT = 48 * 1024 * 1024


def _cparams(sem):
    return pltpu.CompilerParams(dimension_semantics=sem, vmem_limit_bytes=V7X_VMEM_LIMIT)


def _sigmoid(x):
    return jax.nn.sigmoid(x)


_DIMS = {"nn": (((1,), (0,)), ((), ())), "nt": (((1,), (1,)), ((), ())), "tn": (((0,), (0,)), ((), ()))}


def _matmul(name, a, b, mode, m, n, k, tm, tn, tk, out_dtypes, a_spec=None, b_spec=None, extras=(), epilogue=None):
    tm, tn, tk = min(tm, m), min(tn, n), min(tk, k)
    gi, gj, gk = m // tm, n // tn, k // tk
    assert gi * tm == m and gj * tn == n and gk * tk == k, (name, m, n, k, tm, tn, tk)
    if a_spec is None:
        a_spec = pl.BlockSpec((tk, tm), lambda i, j, kk: (kk, i)) if mode == "tn" else pl.BlockSpec((tm, tk), lambda i, j, kk: (i, kk))
    if b_spec is None:
        b_spec = pl.BlockSpec((tn, tk), lambda i, j, kk: (j, kk)) if mode == "nt" else pl.BlockSpec((tk, tn), lambda i, j, kk: (kk, j))
    n_ex, n_out = len(extras), len(out_dtypes)
    dims = _DIMS[mode]

    def body(a_ref, b_ref, *rest):
        ex_refs, out_refs = rest[:n_ex], rest[n_ex:n_ex + n_out]

        def finish(acc):
            outs = (acc,) if epilogue is None else epilogue(acc, *[e[...] for e in ex_refs])
            for o_ref, o in zip(out_refs, outs):
                o_ref[...] = o.astype(o_ref.dtype)

        prod = lax.dot_general(a_ref[...], b_ref[...], dims, preferred_element_type=F32)
        if gk == 1:
            finish(prod)
        else:
            acc_ref = rest[n_ex + n_out]
            kk = pl.program_id(2)

            @pl.when(kk == 0)
            def _():
                acc_ref[...] = prod

            @pl.when(kk > 0)
            def _():
                acc_ref[...] += prod

            @pl.when(kk == gk - 1)
            def _():
                finish(acc_ref[...])

    o_spec = pl.BlockSpec((tm, tn), lambda i, j, kk: (i, j))
    outs = pl.pallas_call(
        body, name=name, grid=(gi, gj, gk),
        in_specs=[a_spec, b_spec] + [o_spec] * n_ex,
        out_specs=[o_spec] * n_out,
        out_shape=[jax.ShapeDtypeStruct((m, n), dt) for dt in out_dtypes],
        scratch_shapes=[pltpu.VMEM((tm, tn), F32)] if gk > 1 else [],
        compiler_params=_cparams(("parallel", "parallel", "arbitrary")),
    )(a, b, *extras)
    return outs[0] if n_out == 1 else outs


def _row_spec(tm, width, col_block=0):
    return pl.BlockSpec((tm, width), lambda i: (i, col_block))


def _full_spec(shape):
    return pl.BlockSpec(shape, lambda *_: (0,) * len(shape))


def _rmsnorm_fwd(name, h, g, tm=512):
    t = h.shape[0]

    def body(h_ref, g_ref, o_ref):
        x = h_ref[...]
        r = lax.rsqrt(jnp.mean(x * x, axis=-1, keepdims=True) + NORM_EPS)
        o_ref[...] = (x * r * g_ref[...]).astype(BF16)

    return pl.pallas_call(
        body, name=name, grid=(t // tm,),
        in_specs=[_row_spec(tm, D_MODEL), _full_spec((1, D_MODEL))],
        out_specs=_row_spec(tm, D_MODEL),
        out_shape=jax.ShapeDtypeStruct((t, D_MODEL), BF16),
        compiler_params=_cparams(("parallel",)),
    )(h, g)


def _rmsnorm_bwd(name, h, g, dn, dres, tm=512):
    t = h.shape[0]

    def body(h_ref, g_ref, dn_ref, dres_ref, dh_ref, dhb_ref, dg_ref):
        i = pl.program_id(0)
        x = h_ref[...]
        r = lax.rsqrt(jnp.mean(x * x, axis=-1, keepdims=True) + NORM_EPS)
        xh = x * r
        dn_v = dn_ref[...]
        dxh = dn_v * g_ref[...]
        dh = dres_ref[...] + r * (dxh - xh * jnp.mean(dxh * xh, axis=-1, keepdims=True))
        dh_ref[...] = dh
        dhb_ref[...] = dh.astype(BF16)
        part = jnp.sum(dn_v * xh, axis=0, keepdims=True)

        @pl.when(i == 0)
        def _():
            dg_ref[...] = part

        @pl.when(i > 0)
        def _():
            dg_ref[...] += part

    return pl.pallas_call(
        body, name=name, grid=(t // tm,),
        in_specs=[_row_spec(tm, D_MODEL), _full_spec((1, D_MODEL)), _row_spec(tm, D_MODEL), _row_spec(tm, D_MODEL)],
        out_specs=[_row_spec(tm, D_MODEL), _row_spec(tm, D_MODEL), _full_spec((1, D_MODEL))],
        out_shape=[jax.ShapeDtypeStruct((t, D_MODEL), F32), jax.ShapeDtypeStruct((t, D_MODEL), BF16),
                   jax.ShapeDtypeStruct((1, D_MODEL), F32)],
        compiler_params=_cparams(("arbitrary",)),
    )(h, g, dn, dres)


def _loss_and_grad(name, h, g, target, tm=512):
    t = h.shape[0]

    def body(h_ref, g_ref, t_ref, dh_ref, dhb_ref, dg_ref, loss_ref):
        i = pl.program_id(0)
        x = h_ref[...]
        gv = g_ref[...]
        r = lax.rsqrt(jnp.mean(x * x, axis=-1, keepdims=True) + NORM_EPS)
        xh = x * r
        err = xh * gv - t_ref[...]
        row = jnp.mean(err * err, axis=-1, keepdims=True)
        part_loss = 0.5 * jnp.sum(row, axis=0, keepdims=True)
        dy = err * (1.0 / D_MODEL)
        dxh = dy * gv
        dh = r * (dxh - xh * jnp.mean(dxh * xh, axis=-1, keepdims=True))
        dh_ref[...] = dh
        dhb_ref[...] = dh.astype(BF16)
        part = jnp.sum(dy * xh, axis=0, keepdims=True)

        @pl.when(i == 0)
        def _():
            dg_ref[...] = part
            loss_ref[...] = jnp.zeros(loss_ref.shape, F32) + part_loss

        @pl.when(i > 0)
        def _():
            dg_ref[...] += part
            loss_ref[...] += part_loss

    return pl.pallas_call(
        body, name=name, grid=(t // tm,),
        in_specs=[_row_spec(tm, D_MODEL), _full_spec((1, D_MODEL)), _row_spec(tm, D_MODEL)],
        out_specs=[_row_spec(tm, D_MODEL), _row_spec(tm, D_MODEL), _full_spec((1, D_MODEL)), _full_spec((8, 128))],
        out_shape=[jax.ShapeDtypeStruct((t, D_MODEL), F32), jax.ShapeDtypeStruct((t, D_MODEL), BF16),
                   jax.ShapeDtypeStruct((1, D_MODEL), F32), jax.ShapeDtypeStruct((8, 128), F32)],
        compiler_params=_cparams(("arbitrary",)),
    )(h, g, target)


def _rope_tables(t):
    half = HEAD_DIM // 2
    inv_freq = ROPE_BASE ** (-jnp.arange(half, dtype=F32) / half)
    ang = jnp.arange(t, dtype=F32)[:, None] * inv_freq[None, :]
    return jnp.cos(ang), jnp.sin(ang)


def _rotary_fwd(name, proj, cos, sin, tm=512):
    t = proj.shape[0]
    half = HEAD_DIM // 2
    k_scale = HEAD_DIM ** -0.5

    def body(q_ref, k_ref, v_ref, c_ref, s_ref, qo_ref, ko_ref, vo_ref):
        c, s = c_ref[...], s_ref[...]
        for hh in range(HEADS):
            lo, mid, hi = hh * HEAD_DIM, hh * HEAD_DIM + half, (hh + 1) * HEAD_DIM
            x1, x2 = q_ref[:, lo:mid], q_ref[:, mid:hi]
            qo_ref[:, lo:mid] = x1 * c - x2 * s
            qo_ref[:, mid:hi] = x1 * s + x2 * c
            x1, x2 = k_ref[:, lo:mid], k_ref[:, mid:hi]
            ko_ref[:, lo:mid] = (x1 * c - x2 * s) * k_scale
            ko_ref[:, mid:hi] = (x1 * s + x2 * c) * k_scale
        vo_ref[...] = v_ref[...].astype(BF16)

    return pl.pallas_call(
        body, name=name, grid=(t // tm,),
        in_specs=[_row_spec(tm, D_MODEL, 0), _row_spec(tm, D_MODEL, 1), _row_spec(tm, D_MODEL, 2),
                  _row_spec(tm, half), _row_spec(tm, half)],
        out_specs=[_row_spec(tm, D_MODEL)] * 3,
        out_shape=[jax.ShapeDtypeStruct((t, D_MODEL), F32), jax.ShapeDtypeStruct((t, D_MODEL), F32),
                   jax.ShapeDtypeStruct((t, D_MODEL), BF16)],
        compiler_params=_cparams(("parallel",)),
    )(proj, proj, proj, cos, sin)


def _rotary_bwd(name, dqr, dkr, cos, sin, tm=512):
    t = dqr.shape[0]
    half = HEAD_DIM // 2
    k_scale = HEAD_DIM ** -0.5

    def body(q_ref, k_ref, c_ref, s_ref, qo_ref, ko_ref):
        c, s = c_ref[...], s_ref[...]
        for hh in range(HEADS):
            lo, mid, hi = hh * HEAD_DIM, hh * HEAD_DIM + half, (hh + 1) * HEAD_DIM
            y1, y2 = q_ref[:, lo:mid], q_ref[:, mid:hi]
            qo_ref[:, lo:mid] = (y1 * c + y2 * s).astype(BF16)
            qo_ref[:, mid:hi] = (y2 * c - y1 * s).astype(BF16)
            y1, y2 = k_ref[:, lo:mid], k_ref[:, mid:hi]
            ko_ref[:, lo:mid] = ((y1 * c + y2 * s) * k_scale).astype(BF16)
            ko_ref[:, mid:hi] = ((y2 * c - y1 * s) * k_scale).astype(BF16)

    return pl.pallas_call(
        body, name=name, grid=(t // tm,),
        in_specs=[_row_spec(tm, D_MODEL), _row_spec(tm, D_MODEL), _row_spec(tm, half), _row_spec(tm, half)],
        out_specs=[_row_spec(tm, D_MODEL)] * 2,
        out_shape=[jax.ShapeDtypeStruct((t, D_MODEL), BF16)] * 2,
        compiler_params=_cparams(("parallel",)),
    )(dqr, dkr, cos, sin)


def _retention_tables():
    b, chunk = RET_BLOCK, 64
    gamma = 1.0 - 2.0 ** (-5.0 - np.arange(HEADS, dtype=np.float64))
    log_g = np.log(gamma)[:, None, None]
    i = np.arange(b)[:, None]
    j = np.arange(b)[None, :]
    same = (i // chunk) == (j // chunk)
    earlier = (j // chunk) < (i // chunk)
    expo = np.where(same, np.abs(i - j), np.where(earlier, i - j, 0)).astype(np.float64)
    mask = np.where(same | earlier, 1.0, 0.0)
    dmat = np.exp(log_g * expo[None]) * mask[None]
    qd = np.exp(log_g[:, :, 0] * (np.arange(b)[None, :] + 1.0))
    kd = np.exp(log_g[:, :, 0] * (b - 1.0 - np.arange(b)[None, :]))
    cd = np.exp(log_g[:, :, 0] * b) * np.ones((1, HEAD_DIM))
    as32 = lambda v: jnp.asarray(v.astype(np.float32))
    return (as32(dmat), as32(np.swapaxes(dmat, 1, 2)), as32(qd[:, :, None]), as32(kd[:, :, None]), as32(cd[:, None, :]))


def _dot(a, b, mode="nn"):
    return lax.dot_general(a, b, _DIMS[mode], preferred_element_type=F32)


def _head_specs(nb, rev=False):
    blk = (RET_BLOCK, HEAD_DIM)
    pos = (lambda h, n: (nb - 1 - n, h)) if rev else (lambda h, n: (n, h))
    tok = pl.BlockSpec(blk, pos)
    gr = pl.BlockSpec(blk, (lambda h, n: (nb - 1 - n, 3 * HEADS + h)) if rev else (lambda h, n: (n, 3 * HEADS + h)))
    tab = pl.BlockSpec((None, RET_BLOCK, RET_BLOCK), lambda h, n: (h, 0, 0))
    col = pl.BlockSpec((None, RET_BLOCK, 1), lambda h, n: (h, 0, 0))
    rowv = pl.BlockSpec((None, 1, HEAD_DIM), lambda h, n: (h, 0, 0))
    st = pl.BlockSpec((None, None, HEAD_DIM, HEAD_DIM), (lambda h, n: (h, nb - 1 - n, 0, 0)) if rev else (lambda h, n: (h, n, 0, 0)))
    return tok, gr, tab, col, rowv, st


def _retention_fwd(name, qr, kr, vb, proj, tables):
    t = qr.shape[0]
    nb = t // RET_BLOCK
    dmat, _, qd, kd, cd = tables
    tok, gr, tab, col, rowv, st = _head_specs(nb)

    def body(q_ref, k_ref, v_ref, g_ref, d_ref, qd_ref, kd_ref, cd_ref, o_ref, ret_ref, st_ref, state):
        n = pl.program_id(1)

        @pl.when(n == 0)
        def _():
            state[...] = jnp.zeros(state.shape, F32)

        q, k, v = q_ref[...], k_ref[...], v_ref[...]
        s = _dot(q.astype(BF16), k.astype(BF16), "nt") * d_ref[...]
        stb = state[...].astype(BF16)
        st_ref[...] = stb
        o = _dot(s.astype(BF16), v) + _dot((q * qd_ref[...]).astype(BF16), stb)
        o_ref[...] = o
        rn = o * lax.rsqrt(jnp.mean(o * o, axis=-1, keepdims=True) + NORM_EPS)
        g = g_ref[...]
        ret_ref[...] = (rn * (g * _sigmoid(g))).astype(BF16)
        state[...] = state[...] * cd_ref[...] + _dot((k * kd_ref[...]).astype(BF16), v, "tn")

    return pl.pallas_call(
        body, name=name, grid=(HEADS, nb),
        in_specs=[tok, tok, tok, gr, tab, col, col, rowv],
        out_specs=[tok, tok, st],
        out_shape=[jax.ShapeDtypeStruct((t, D_MODEL), F32), jax.ShapeDtypeStruct((t, D_MODEL), BF16),
                   jax.ShapeDtypeStruct((HEADS, nb, HEAD_DIM, HEAD_DIM), BF16)],
        scratch_shapes=[pltpu.VMEM((HEAD_DIM, HEAD_DIM), F32)],
        compiler_params=_cparams(("parallel", "arbitrary")),
    )(qr, kr, vb, proj, dmat, qd, kd, cd)


def _retention_bwd(name, dret, o, qr, kr, vb, proj, states, tables):
    t = qr.shape[0]
    nb = t // RET_BLOCK
    dmat, dmat_t, qd, kd, cd = tables
    tok, gr, tab, col, rowv, st = _head_specs(nb, rev=True)

    def body(dr_ref, o_ref, q_ref, k_ref, v_ref, g_ref, st_ref, d_ref, dt_ref, qd_ref, kd_ref, cd_ref,
             dq_ref, dk_ref, dv_ref, dg_ref, gstate):
        n = pl.program_id(1)

        @pl.when(n == 0)
        def _():
            gstate[...] = jnp.zeros(gstate.shape, F32)

        o_v, g, dr = o_ref[...], g_ref[...], dr_ref[...]
        sg = _sigmoid(g)
        r = lax.rsqrt(jnp.mean(o_v * o_v, axis=-1, keepdims=True) + NORM_EPS)
        rn = o_v * r
        d_rn = dr * (g * sg)
        dg_ref[...] = (dr * rn * (sg * (1.0 + g * (1.0 - sg)))).astype(BF16)
        d_o = r * (d_rn - rn * jnp.mean(d_rn * rn, axis=-1, keepdims=True))
        dob = d_o.astype(BF16)

        q, k, v = q_ref[...], k_ref[...], v_ref[...]
        qb, kb = q.astype(BF16), k.astype(BF16)
        qdv, kdv = qd_ref[...], kd_ref[...]
        s_t = (_dot(kb, qb, "nt") * dt_ref[...]).astype(BF16)
        p_t = (_dot(v, dob, "nt") * dt_ref[...]).astype(BF16)
        p = (_dot(dob, v, "nt") * d_ref[...]).astype(BF16)
        stb = st_ref[...]
        gb = gstate[...].astype(BF16)
        dq_ref[...] = _dot(p, kb) + _dot(dob, stb, "nt") * qdv
        dk_ref[...] = _dot(p_t, qb) + _dot(v, gb, "nt") * kdv
        dv_ref[...] = (_dot(s_t, dob) + _dot((k * kdv).astype(BF16), gb)).astype(BF16)
        gstate[...] = gstate[...] * cd_ref[...] + _dot((q * qdv).astype(BF16), dob, "tn")

    return pl.pallas_call(
        body, name=name, grid=(HEADS, nb),
        in_specs=[tok, tok, tok, tok, tok, gr, st, tab, tab, col, col, rowv],
        out_specs=[tok, tok, tok, tok],
        out_shape=[jax.ShapeDtypeStruct((t, D_MODEL), F32), jax.ShapeDtypeStruct((t, D_MODEL), F32),
                   jax.ShapeDtypeStruct((t, D_MODEL), BF16), jax.ShapeDtypeStruct((t, D_MODEL), BF16)],
        scratch_shapes=[pltpu.VMEM((HEAD_DIM, HEAD_DIM), F32)],
        compiler_params=_cparams(("parallel", "arbitrary")),
    )(dret, o, qr, kr, vb, proj, states, dmat, dmat_t, qd, kd, cd)


POOL_TILE = 256


def _pool_tables():
    b = POOL_TILE
    tt = np.arange(b)[:, None]
    jj = np.arange(b)[None, :]
    cur, prev = [], []
    for w in POOL_WINDOWS:
        cur.append(((tt - jj >= 0) & (tt - jj <= w - 1)).astype(np.float32))
        prev.append((tt - (jj - b) <= w - 1).astype(np.float32))
    cur, prev = np.stack(cur), np.stack(prev)
    as16 = lambda v: jnp.asarray(v, dtype=BF16)
    return as16(cur), as16(prev), as16(np.swapaxes(cur, 1, 2)), as16(np.swapaxes(prev, 1, 2))


def _split2(x):
    hi = x.astype(BF16)
    return hi, (x - hi.astype(F32)).astype(BF16)


def _pool_count(n, g):
    tpos = n * POOL_TILE + lax.broadcasted_iota(jnp.int32, (POOL_TILE, 1), 0)
    return jnp.minimum(tpos + 1, jnp.left_shift(2, g)).astype(F32)


def _pool_fwd(name, proj, pool_w, scale, tables):
    t = proj.shape[0]
    nb = t // POOL_TILE
    mc, mp, _, _ = tables
    p_block0 = 4 * D_MODEL // GROUP_DIM
    blk = (POOL_TILE, GROUP_DIM)
    tab = pl.BlockSpec((None, POOL_TILE, POOL_TILE), lambda g, n: (g, 0, 0))

    def body(pc_ref, pp_ref, mc_ref, mp_ref, w_ref, sc_ref, pm_ref, mix_ref, po_ref):
        g, n = pl.program_id(0), pl.program_id(1)
        p = pc_ref[...]
        c_hi, c_lo = _split2(p)
        p_hi, p_lo = _split2(pp_ref[...])
        mcv, mpv = mc_ref[...], mp_ref[...]
        win = _dot(mcv, c_hi) + _dot(mcv, c_lo)
        before = _dot(mpv, p_hi) + _dot(mpv, p_lo)
        win = win + jnp.where(n > 0, before, 0.0)
        pm = (win / _pool_count(n, g) - p).astype(BF16)
        pm_ref[...] = pm
        mixed = _dot(pm, w_ref[...])
        mix_ref[...] = mixed
        po_ref[...] = (mixed * sc_ref[...]).astype(BF16)

    return pl.pallas_call(
        body, name=name, grid=(GROUPS, nb),
        in_specs=[pl.BlockSpec(blk, lambda g, n: (n, p_block0 + g)),
                  pl.BlockSpec(blk, lambda g, n: (jnp.maximum(n - 1, 0), p_block0 + g)),
                  tab, tab,
                  pl.BlockSpec((None, GROUP_DIM, GROUP_DIM), lambda g, n: (g, 0, 0)),
                  pl.BlockSpec((1, GROUP_DIM), lambda g, n: (0, g))],
        out_specs=[pl.BlockSpec(blk, lambda g, n: (n, g))] * 3,
        out_shape=[jax.ShapeDtypeStruct((t, D_MODEL), BF16), jax.ShapeDtypeStruct((t, D_MODEL), F32),
                   jax.ShapeDtypeStruct((t, D_MODEL), BF16)],
        compiler_params=_cparams(("parallel", "parallel")),
    )(proj, proj, mc, mp, pool_w, scale)


def _pool_bwd(name, dpo, pm, mixed, pool_w, scale, tables):
    t = dpo.shape[0]
    nb = t // POOL_TILE
    _, _, mct, mpt = tables
    blk = (POOL_TILE, GROUP_DIM)
    cur = pl.BlockSpec(blk, lambda g, n: (n, g))
    nxt = pl.BlockSpec(blk, lambda g, n: (jnp.minimum(n + 1, nb - 1), g))
    tab = pl.BlockSpec((None, POOL_TILE, POOL_TILE), lambda g, n: (g, 0, 0))
    wspec = pl.BlockSpec((None, GROUP_DIM, GROUP_DIM), lambda g, n: (g, 0, 0))
    sspec = pl.BlockSpec((1, GROUP_DIM), lambda g, n: (0, g))

    def body(dc_ref, dn_ref, pm_ref, mix_ref, mct_ref, mpt_ref, w_ref, sc_ref, dp_ref, dw_ref, ds_ref):
        g, n = pl.program_id(0), pl.program_id(1)
        dc, sc, w = dc_ref[...], sc_ref[...], w_ref[...]
        dmix_c = (dc * sc).astype(BF16)
        dmix_n = (dn_ref[...] * sc).astype(BF16)
        dpm_c = _dot(dmix_c, w, "nt")
        dpm_n = _dot(dmix_n, w, "nt")
        e_hi, e_lo = _split2(dpm_c / _pool_count(n, g))
        f_hi, f_lo = _split2(dpm_n / _pool_count(n + 1, g))
        mctv, mptv = mct_ref[...], mpt_ref[...]
        back = _dot(mctv, e_hi) + _dot(mctv, e_lo)
        after = _dot(mptv, f_hi) + _dot(mptv, f_lo)
        dp_ref[...] = (back + jnp.where(n < nb - 1, after, 0.0) - dpm_c).astype(BF16)
        dw_part = _dot(pm_ref[...], dmix_c, "tn")
        ds_part = jnp.sum(dc * mix_ref[...], axis=0, keepdims=True)

        @pl.when(n == 0)
        def _():
            dw_ref[...] = dw_part
            ds_ref[...] = ds_part

        @pl.when(n > 0)
        def _():
            dw_ref[...] += dw_part
            ds_ref[...] += ds_part

    return pl.pallas_call(
        body, name=name, grid=(GROUPS, nb),
        in_specs=[cur, nxt, cur, cur, tab, tab, wspec, sspec],
        out_specs=[cur, wspec, sspec],
        out_shape=[jax.ShapeDtypeStruct((t, D_MODEL), BF16), jax.ShapeDtypeStruct((GROUPS, GROUP_DIM, GROUP_DIM), F32),
                   jax.ShapeDtypeStruct((1, D_MODEL), F32)],
        compiler_params=_cparams(("parallel", "arbitrary")),
    )(dpo, dpo, pm, mixed, mct, mpt, pool_w, scale)


GATE0_BLOCK, GATE1_BLOCK = 5, 6


def _merge_fwd(name, ret, po, w_ru, w_pu, proj, bias, tm=512):
    t = ret.shape[0]

    def body(r_ref, p_ref, wr_ref, wp_ref, g0_ref, g1_ref, b_ref, m_ref, ru_ref, pu_ref):
        ru = _dot(r_ref[...], wr_ref[...])
        pu = _dot(p_ref[...], wp_ref[...])
        ru_ref[...] = ru
        pu_ref[...] = pu
        m_ref[...] = (_sigmoid(g0_ref[...] + b_ref[0:1, :]) * ru + _sigmoid(g1_ref[...] + b_ref[1:2, :]) * pu).astype(BF16)

    row = _row_spec(tm, D_MODEL)
    wspec = _full_spec((D_MODEL, D_MODEL))
    return pl.pallas_call(
        body, name=name, grid=(t // tm,),
        in_specs=[row, row, wspec, wspec, _row_spec(tm, D_MODEL, GATE0_BLOCK), _row_spec(tm, D_MODEL, GATE1_BLOCK),
                  _full_spec((2, D_MODEL))],
        out_specs=[row, row, row],
        out_shape=[jax.ShapeDtypeStruct((t, D_MODEL), BF16), jax.ShapeDtypeStruct((t, D_MODEL), F32),
                   jax.ShapeDtypeStruct((t, D_MODEL), F32)],
        compiler_params=_cparams(("parallel",)),
    )(ret, po, w_ru, w_pu, proj, proj, bias)


def _merge_bwd(name, dm, ru, pu, proj, bias, tm=512):
    t = dm.shape[0]

    def body(dm_ref, ru_ref, pu_ref, g0_ref, g1_ref, b_ref, dru_ref, dpu_ref, dg0_ref, dg1_ref, db_ref):
        i = pl.program_id(0)
        d = dm_ref[...]
        s0 = _sigmoid(g0_ref[...] + b_ref[0:1, :])
        s1 = _sigmoid(g1_ref[...] + b_ref[1:2, :])
        dru_ref[...] = (d * s0).astype(BF16)
        dpu_ref[...] = (d * s1).astype(BF16)
        dg0 = d * ru_ref[...] * (s0 * (1.0 - s0))
        dg1 = d * pu_ref[...] * (s1 * (1.0 - s1))
        dg0_ref[...] = dg0.astype(BF16)
        dg1_ref[...] = dg1.astype(BF16)
        part0 = jnp.sum(dg0, axis=0, keepdims=True)
        part1 = jnp.sum(dg1, axis=0, keepdims=True)

        @pl.when(i == 0)
        def _():
            db_ref[0:1, :] = part0
            db_ref[1:2, :] = part1

        @pl.when(i > 0)
        def _():
            db_ref[0:1, :] += part0
            db_ref[1:2, :] += part1

    row = _row_spec(tm, D_MODEL)
    return pl.pallas_call(
        body, name=name, grid=(t // tm,),
        in_specs=[row, row, row, _row_spec(tm, D_MODEL, GATE0_BLOCK), _row_spec(tm, D_MODEL, GATE1_BLOCK),
                  _full_spec((2, D_MODEL))],
        out_specs=[row, row, row, row, _full_spec((2, D_MODEL))],
        out_shape=[jax.ShapeDtypeStruct((t, D_MODEL), BF16)] * 4 + [jax.ShapeDtypeStruct((2, D_MODEL), F32)],
        compiler_params=_cparams(("arbitrary",)),
    )(dm, ru, pu, proj, proj, bias)


def _half_scale(acc):
    return (FFN_RES_WEIGHT * acc,)


def _residual_half(acc, res):
    return (res + FFN_RES_WEIGHT * acc,)


def _residual(acc, res):
    return (res + acc,)


FF_TILE = D_FF // 2


def _ffn_in(name, nrm, w_in, tm=512):
    t = nrm.shape[0]
    nj = D_FF // FF_TILE

    def body(n_ref, wg_ref, wu_ref, a_ref, mid_ref):
        nv = n_ref[...]
        gate = _dot(nv, wg_ref[...])
        up = _dot(nv, wu_ref[...])
        a_ref[0] = gate
        a_ref[1] = up
        mid_ref[...] = (gate * _sigmoid(gate) * up).astype(BF16)

    return pl.pallas_call(
        body, name=name, grid=(t // tm, nj),
        in_specs=[pl.BlockSpec((tm, D_MODEL), lambda i, j: (i, 0)),
                  pl.BlockSpec((D_MODEL, FF_TILE), lambda i, j: (0, j)),
                  pl.BlockSpec((D_MODEL, FF_TILE), lambda i, j: (0, j + nj))],
        out_specs=[pl.BlockSpec((2, tm, FF_TILE), lambda i, j: (0, i, j)), pl.BlockSpec((tm, FF_TILE), lambda i, j: (i, j))],
        out_shape=[jax.ShapeDtypeStruct((2, t, D_FF), F32), jax.ShapeDtypeStruct((t, D_FF), BF16)],
        compiler_params=_cparams(("parallel", "parallel")),
    )(nrm, w_in, w_in)


def _ffn_dact(name, dout_b, w_out, a, tm=512):
    t = dout_b.shape[0]

    def body(d_ref, w_ref, a_ref, da_ref):
        dm = FFN_RES_WEIGHT * _dot(d_ref[...], w_ref[...], "nt")
        gate, up = a_ref[0], a_ref[1]
        s = _sigmoid(gate)
        da_ref[0] = (dm * up * (s * (1.0 + gate * (1.0 - s)))).astype(BF16)
        da_ref[1] = (dm * (gate * s)).astype(BF16)

    blk = pl.BlockSpec((2, tm, FF_TILE), lambda i, j: (0, i, j))
    return pl.pallas_call(
        body, name=name, grid=(t // tm, D_FF // FF_TILE),
        in_specs=[pl.BlockSpec((tm, D_MODEL), lambda i, j: (i, 0)), pl.BlockSpec((FF_TILE, D_MODEL), lambda i, j: (j, 0)), blk],
        out_specs=blk,
        out_shape=jax.ShapeDtypeStruct((2, t, D_FF), BF16),
        compiler_params=_cparams(("parallel", "parallel")),
    )(dout_b, w_out, a)


def _ffn_fwd(tag, h, g, get_w_in, get_w_out):
    t = h.shape[0]
    nrm = _rmsnorm_fwd(f"{tag}_norm", h, g)
    w_in = get_w_in(nrm)
    a, mid = _ffn_in(f"{tag}_in", nrm, w_in, tm=min(512, t))
    w_out = get_w_out(mid)
    out = _matmul(f"{tag}_out", mid, w_out, "nn", t, D_MODEL, D_FF, 512, D_MODEL, D_FF, [F32],
                  extras=(h,), epilogue=_residual_half)
    return out, (nrm, a, mid, w_in, w_out)


def _ffn_bwd(tag, h, g, saved, dout, dout_b, on_grads):
    t = h.shape[0]
    nrm, a, mid, w_in, w_out = saved
    nj = D_FF // FF_TILE
    tk = min(1024, t)
    d_w_out = _matmul(f"{tag}_dwout", mid, dout_b, "tn", D_FF, D_MODEL, t, 1408, D_MODEL, 1024, [BF16], epilogue=_half_scale)
    da = _ffn_dact(f"{tag}_dact", dout_b, w_out, a, tm=min(512, t))
    d_w_in = _matmul(f"{tag}_dwin", nrm, da, "tn", D_MODEL, 2 * D_FF, t, D_MODEL, FF_TILE, tk, [BF16],
                     b_spec=pl.BlockSpec((None, tk, FF_TILE), lambda i, j, kk: (j // nj, kk, j % nj)))
    tie = on_grads({f"{tag}_w_in": d_w_in, f"{tag}_w_out": d_w_out})
    tm = min(1024, t)
    dn = _matmul(f"{tag}_dn", da, w_in, "nt", t, D_MODEL, 2 * D_FF, tm, D_MODEL, FF_TILE, [F32],
                 a_spec=pl.BlockSpec((None, tm, FF_TILE), lambda i, j, kk: (kk // nj, i, kk % nj)))
    dh, dh_b, dg = _rmsnorm_bwd(f"{tag}_dnorm", h, g if tie is None else g + tie, dn, dout)
    return dh, dh_b, dg


def _local_step(x, target, vec, get_w, on_grads):
    t = x.shape[0]
    cos, sin = _rope_tables(t)
    rtab = _retention_tables()
    ptab = _pool_tables()
    w = {}

    def getter(group, name):
        def get(after):
            if name not in w:
                w.update(get_w(group, after))
            return w[name]
        return get

    h1, s1 = _ffn_fwd("ffn1", x, vec["norm_ffn1"], getter(0, "ffn1_w_in"), getter(1, "ffn1_w_out"))
    u = _rmsnorm_fwd("mix_norm", h1, vec["norm_mix"])
    w.update(get_w(2, u))
    proj = _matmul("mix_in", u, w["w_in"], "nn", t, IN_WIDTH, D_MODEL, 1024, 1024, D_MODEL, [F32])
    qr, kr, vb = _rotary_fwd("rotary", proj, cos, sin)
    o, ret, states = _retention_fwd("retention", qr, kr, vb, proj, rtab)
    pm, mixed, po = _pool_fwd("pool", proj, w["pool_w"], vec["pool_scale"], ptab)
    merged, ru, pu = _merge_fwd("merge", ret, po, w["w_ret_up"], w["w_pool_up"], proj, vec["gate_bias"])
    h2 = _matmul("mix_out", merged, w["w_out"], "nn", t, D_MODEL, D_MODEL, 512, D_MODEL, D_MODEL, [F32],
                 extras=(h1,), epilogue=_residual)
    h3, s2 = _ffn_fwd("ffn2", h2, vec["norm_ffn2"], getter(3, "ffn2_w_in"), getter(3, "ffn2_w_out"))
    dh3, dh3_b, dg_final, loss = _loss_and_grad("loss", h3, vec["norm_final"], target)

    dh2, dh2_b, dg_ffn2 = _ffn_bwd("ffn2", h2, vec["norm_ffn2"], s2, dh3, dh3_b, on_grads)
    dm = _matmul("mix_dmerged", dh2_b, w["w_out"], "nt", t, D_MODEL, D_MODEL, 1024, D_MODEL, D_MODEL, [F32])
    d_w_out = _matmul("mix_dwout", merged, dh2_b, "tn", D_MODEL, D_MODEL, t, D_MODEL, D_MODEL, 1024, [BF16])
    dru, dpu, dg0, dg1, d_bias = _merge_bwd("merge_bwd", dm, ru, pu, proj, vec["gate_bias"])
    dret = _matmul("mix_dret", dru, w["w_ret_up"], "nt", t, D_MODEL, D_MODEL, 1024, D_MODEL, D_MODEL, [F32])
    d_w_ru = _matmul("mix_dwru", ret, dru, "tn", D_MODEL, D_MODEL, t, D_MODEL, D_MODEL, 1024, [BF16])
    dpo = _matmul("mix_dpool", dpu, w["w_pool_up"], "nt", t, D_MODEL, D_MODEL, 1024, D_MODEL, D_MODEL, [F32])
    d_w_pu = _matmul("mix_dwpu", po, dpu, "tn", D_MODEL, D_MODEL, t, D_MODEL, D_MODEL, 1024, [BF16])
    dp, d_pool_w, d_scale = _pool_bwd("pool_bwd", dpo, pm, mixed, w["pool_w"], vec["pool_scale"], ptab)
    dqr, dkr, dv, dgr = _retention_bwd("retention_bwd", dret, o, qr, kr, vb, proj, states, rtab)
    dq, dk = _rotary_bwd("rotary_bwd", dqr, dkr, cos, sin)
    dproj = jnp.concatenate([dq, dk, dv, dgr, dp, dg0, dg1], axis=1)
    d_w_in = _matmul("mix_dwin", u, dproj, "tn", D_MODEL, IN_WIDTH, t, D_MODEL, 1024, 1024, [BF16])
    tie = on_grads(dict(w_in=d_w_in, pool_w=d_pool_w.astype(BF16), w_ret_up=d_w_ru, w_pool_up=d_w_pu, w_out=d_w_out))
    du = _matmul("mix_du", dproj, w["w_in"], "nt", t, D_MODEL, IN_WIDTH, 1024, D_MODEL, 1024, [F32])
    g_mix = vec["norm_mix"] if tie is None else vec["norm_mix"] + tie
    dh1, dh1_b, dg_mix = _rmsnorm_bwd("mix_dnorm", h1, g_mix, du, dh2)
    dx, _, dg_ffn1 = _ffn_bwd("ffn1", x, vec["norm_ffn1"], s1, dh1, dh1_b, on_grads)

    small = dict(norm_ffn1=dg_ffn1, norm_mix=dg_mix, gate_bias=d_bias, pool_scale=d_scale, norm_ffn2=dg_ffn2,
                 norm_final=dg_final)
    return loss[0, 0], dx, small


BIG = ("ffn1_w_in", "ffn1_w_out", "w_in", "pool_w", "w_ret_up", "w_pool_up", "w_out", "ffn2_w_in", "ffn2_w_out")
KIND = dict(ffn1_w_in="col", ffn1_w_out="row", w_in="col", pool_w="pool", w_ret_up="row", w_pool_up="row", w_out="row",
            ffn2_w_in="col", ffn2_w_out="row")
ANY = pl.BlockSpec(memory_space=pl.ANY)


def _place():
    x, y, c = lax.axis_index("x"), lax.axis_index("y"), lax.axis_index("c")
    chips = [(1 - x, y), (x, 1 - y), (1 - x, 1 - y)]
    return x, y, c, chips


def _full_view_shape(kind, local_shape):
    if kind == "col":
        return (2, local_shape[0] // 2, N_CHIPS * local_shape[1])
    if kind == "row":
        return (N_CHIPS, 2, local_shape[0] // 2, local_shape[1])
    return (GROUPS, N_CHIPS, 2, local_shape[1] // 2, local_shape[2])


def _local_view(kind, arr):
    if kind == "pool":
        return arr.reshape(GROUPS, 2, arr.shape[1] // 2, arr.shape[2])
    return arr.reshape(2, arr.shape[0] // 2, arr.shape[1])


def _blk(kind, ref, s, c):
    if kind == "col":
        cs = ref.shape[2] // N_CHIPS
        return ref.at[c, :, pl.ds(pl.multiple_of(s * cs, 128), cs)]
    if kind == "row":
        return ref.at[s, c]
    return ref.at[:, s, c]


def _half(kind, ref, c):
    return ref.at[:, c] if kind == "pool" else ref.at[c]


def _shard(kind, ref, s):
    if kind == "col":
        cs = ref.shape[2] // N_CHIPS
        return ref.at[:, :, pl.ds(pl.multiple_of(s * cs, 128), cs)]
    if kind == "row":
        return ref.at[s]
    return ref.at[:, s]


HBM = pl.BlockSpec(memory_space=pltpu.HBM)
SEM = pl.BlockSpec(memory_space=pltpu.SEMAPHORE)
EFFECT = pltpu.SideEffectType.DATAFLOW_SIDE_EFFECTING
WEIGHT_GROUPS = (("ffn1_w_in",), ("ffn1_w_out",), ("w_in", "pool_w", "w_ret_up", "w_pool_up", "w_out"), ("ffn2_w_in", "ffn2_w_out"))
GRAD_GROUPS = (("ffn2_w_in", "ffn2_w_out"), ("w_in", "pool_w", "w_ret_up", "w_pool_up", "w_out"), ("ffn1_w_in", "ffn1_w_out"))


def _hbm(a):
    return pltpu.with_memory_space_constraint(a, pltpu.HBM)


def _natural(kind, o):
    if kind == "col":
        return o.reshape(o.shape[0] * o.shape[1], o.shape[2])
    if kind == "row":
        return o.reshape(-1, o.shape[3])
    return o.reshape(GROUPS, -1, o.shape[4])


def _ici_copy(kind, loc, full, j, chips, s, c, send_sem, recv_sem):
    px, py = chips[j]
    return (pltpu.make_async_remote_copy(src_ref=_half(kind, loc, c), dst_ref=_blk(kind, full, s, c), send_sem=send_sem,
                                         recv_sem=recv_sem, device_id=(px, py, c), device_id_type=MESH),
            pltpu.make_async_remote_copy(src_ref=_half(kind, loc, c), dst_ref=_blk(kind, full, 2 * px + py, c), send_sem=send_sem,
                                         recv_sem=recv_sem, device_id=(px, py, c), device_id_type=MESH))


def _gather_start(shards):
    names = [nm for grp in WEIGHT_GROUPS for nm in grp]
    kinds = [KIND[nm] for nm in names]
    n, ng = len(names), len(WEIGHT_GROUPS)
    locs = [_hbm(_local_view(KIND[nm], shards[nm])) for nm in names]
    lands = [_hbm(lax.empty(_full_view_shape(KIND[nm], shards[nm].shape), BF16)) for nm in names]
    first = np.cumsum([0] + [len(grp) for grp in WEIGHT_GROUPS])

    def body(*refs):
        loc, full = refs[:n], refs[n:2 * n]
        send_sems, recv_sems = refs[2 * n:2 * n + ng], refs[2 * n + ng:2 * n + 2 * ng]
        token = refs[-1]
        x, y, c, chips = _place()
        s = 2 * x + y
        for g in range(ng):
            for a in range(first[g], first[g + 1]):
                for j in range(3):
                    k = 3 * (a - first[g]) + j
                    _ici_copy(kinds[a], loc[a], full[a], j, chips, s, c, send_sems[g].at[k], recv_sems[g].at[k])[0].start()
        token[...] = jnp.zeros(token.shape, F32)

    sem_shapes = [pltpu.SemaphoreType.DMA((3 * len(grp),)) for grp in WEIGHT_GROUPS]
    outs = pl.pallas_call(
        body, name="gather_start",
        in_specs=[HBM] * (2 * n),
        out_specs=[SEM] * (2 * ng) + [HBM] * (2 * n) + [pl.BlockSpec(memory_space=pltpu.VMEM)],
        out_shape=sem_shapes + sem_shapes + [pltpu.HBM(a.shape, a.dtype) for a in locs + lands] + [jax.ShapeDtypeStruct((8, 128), F32)],
        input_output_aliases={i: 2 * ng + i for i in range(2 * n)},
        compiler_params=pltpu.CompilerParams(has_side_effects=EFFECT),
    )(*locs, *lands)
    send_sems, recv_sems = outs[:ng], outs[ng:2 * ng]
    locs_t, lands_t = outs[2 * ng:2 * ng + n], outs[2 * ng + n:2 * ng + 2 * n]
    groups = []
    for g in range(ng):
        sl = slice(first[g], first[g + 1])
        groups.append((send_sems[g], recv_sems[g], list(locs_t[sl]), list(lands_t[sl])))
    return groups, outs[-1]


def _gather_finish(g, group, after):
    names = WEIGHT_GROUPS[g]
    kinds = [KIND[nm] for nm in names]
    m = len(names)
    send_sem, recv_sem, locs, lands = group

    def wait_body(*refs):
        loc, full = refs[:m], refs[m:2 * m]
        send_sems, recv_sems = refs[2 * m], refs[2 * m + 1]
        x, y, c, chips = _place()
        s = 2 * x + y
        for a in range(m):
            for j in range(3):
                k = 3 * a + j
                sent, landed = _ici_copy(kinds[a], loc[a], full[a], j, chips, s, c, send_sems.at[k], recv_sems.at[k])
                sent.wait_send()
                landed.wait_recv()

    outs = pl.pallas_call(
        wait_body, name=f"gather_wait_{g}",
        in_specs=[HBM] * (2 * m) + [SEM, SEM, ANY], out_specs=[HBM] * (2 * m),
        out_shape=[pltpu.HBM(a.shape, a.dtype) for a in locs + lands],
        input_output_aliases={i: i for i in range(2 * m)},
        compiler_params=pltpu.CompilerParams(has_side_effects=EFFECT),
    )(*locs, *lands, send_sem, recv_sem, after)
    locs, lands = outs[:m], outs[m:]

    def forward_body(*refs):
        loc, full = refs[:m], refs[2 * m:3 * m]
        send_sems, recv_sems = refs[3 * m:]
        x, y, c, chips = _place()
        s = 2 * x + y
        sib = (x, y, 1 - c)

        def remote(a, k, src, dst):
            return pltpu.make_async_remote_copy(src_ref=src, dst_ref=dst, send_sem=send_sems.at[4 * a + k],
                                                recv_sem=recv_sems.at[4 * a + k], device_id=sib, device_id_type=MESH)

        sends = []
        for a in range(m):
            for j, (px, py) in enumerate(chips):
                theirs = _blk(kinds[a], full[a], 2 * px + py, c)
                sends.append(remote(a, j, theirs, theirs))
            sends.append(remote(a, 3, loc[a], _shard(kinds[a], full[a], s)))
        for cp in sends:
            cp.start()
        for a in range(m):
            for j, (px, py) in enumerate(chips):
                from_sib = _blk(kinds[a], full[a], 2 * px + py, 1 - c)
                remote(a, j, from_sib, from_sib).wait_recv()
            own = _shard(kinds[a], full[a], s)
            remote(a, 3, own, own).wait_recv()
        for cp in sends:
            cp.wait_send()

    outs = pl.pallas_call(
        forward_body, name=f"gather_forward_{g}",
        in_specs=[ANY] * (2 * m), out_specs=[ANY] * m,
        out_shape=[jax.ShapeDtypeStruct(a.shape, a.dtype) for a in lands],
        input_output_aliases={m + i: i for i in range(m)},
        scratch_shapes=[pltpu.SemaphoreType.DMA((4 * m,)), pltpu.SemaphoreType.DMA((4 * m,))],
    )(*locs, *lands)
    return {nm: _natural(k, o) for nm, k, o in zip(names, kinds, outs)}


def _grad_view(kind, g):
    if kind == "col":
        return g.reshape(2, g.shape[0] // 2, g.shape[1])
    if kind == "row":
        return g.reshape(N_CHIPS, 2, g.shape[0] // (2 * N_CHIPS), g.shape[1])
    return g.reshape(GROUPS, N_CHIPS, 2, g.shape[1] // (2 * N_CHIPS), g.shape[2])


def _pair_exchange(tag, names, views):
    kinds = [KIND[nm] for nm in names]
    n = len(names)

    def other_half(kind, ref, c):
        if kind == "col":
            return ref.at[c]
        if kind == "row":
            return ref.at[:, c]
        return ref.at[:, :, c]

    def body(*refs):
        g, got = refs[:n], refs[n:2 * n]
        send_sems, recv_sems = refs[2 * n:]
        x, y, c, _ = _place()
        sib = (x, y, 1 - c)
        cps = []
        for a in range(n):
            cp = pltpu.make_async_remote_copy(src_ref=other_half(kinds[a], g[a], 1 - c), dst_ref=got[a], send_sem=send_sems.at[a],
                                              recv_sem=recv_sems.at[a], device_id=sib, device_id_type=MESH)
            cp.start()
            cps.append(cp)
        for cp in cps:
            cp.wait()

    def got_shape(kind, v):
        if kind == "col":
            return v.shape[1:]
        if kind == "row":
            return (v.shape[0],) + v.shape[2:]
        return v.shape[:2] + v.shape[3:]

    outs = pl.pallas_call(
        body, name=f"grad_pair_exchange_{tag}",
        in_specs=[ANY] * n, out_specs=[ANY] * n,
        out_shape=[jax.ShapeDtypeStruct(got_shape(k, views[nm]), BF16) for nm, k in zip(names, kinds)],
        scratch_shapes=[pltpu.SemaphoreType.DMA((n,)), pltpu.SemaphoreType.DMA((n,))],
    )(*[views[nm] for nm in names])
    return dict(zip(names, outs))


def _pair_sum(name, kind, view, got, c_arr):
    if kind == "col":
        _, rows, cols = view.shape
        tr = 128
        grid = (rows // tr,)
        v_spec = pl.BlockSpec((None, tr, cols), lambda i, c: (c[0], i, 0))
        g_spec = pl.BlockSpec((tr, cols), lambda i, c: (i, 0))
    elif kind == "row":
        _, _, rows, cols = view.shape
        grid = (N_CHIPS,)
        v_spec = pl.BlockSpec((None, None, rows, cols), lambda i, c: (i, c[0], 0, 0))
        g_spec = pl.BlockSpec((None, rows, cols), lambda i, c: (i, 0, 0))
    else:
        _, _, _, rows, cols = view.shape
        grid = (GROUPS,)
        v_spec = pl.BlockSpec((None, N_CHIPS, None, rows, cols), lambda i, c: (i, 0, c[0], 0, 0))
        g_spec = pl.BlockSpec((None, N_CHIPS, rows, cols), lambda i, c: (i, 0, 0, 0))

    def body(c_ref, v_ref, g_ref, o_ref):
        o_ref[...] = (v_ref[...].astype(F32) + g_ref[...].astype(F32)).astype(BF16)

    return pl.pallas_call(
        body, name=name,
        grid_spec=pltpu.PrefetchScalarGridSpec(num_scalar_prefetch=1, grid=grid, in_specs=[v_spec, g_spec], out_specs=g_spec),
        out_shape=jax.ShapeDtypeStruct(got.shape, BF16),
        compiler_params=_cparams(("parallel",)),
    )(c_arr, view, got)


def _piece(kind, ref, s):
    if kind == "col":
        cs = ref.shape[1] // N_CHIPS
        return ref.at[:, pl.ds(pl.multiple_of(s * cs, 128), cs)]
    if kind == "row":
        return ref.at[s]
    return ref.at[:, s]


def _piece_shape(kind, shape):
    if kind == "col":
        return (shape[0], shape[1] // N_CHIPS)
    if kind == "row":
        return shape[1:]
    return (shape[0],) + shape[2:]


def _shard_copies(kinds, p, got, send_sems, recv_sems):
    x, y, c, chips = _place()
    return [pltpu.make_async_remote_copy(src_ref=_piece(kinds[a], p[a], 2 * px + py), dst_ref=got[a].at[j],
                                         send_sem=send_sems.at[3 * a + j], recv_sem=recv_sems.at[3 * a + j],
                                         device_id=(px, py, c), device_id_type=MESH)
            for a in range(len(kinds)) for j, (px, py) in enumerate(chips)]


def _shard_exchange_start(g, names, psums):
    kinds = [KIND[nm] for nm in names]
    n = len(names)
    srcs = [_hbm(psums[nm]) for nm in names]
    lands = [_hbm(lax.empty((3,) + _piece_shape(k, psums[nm].shape), BF16)) for nm, k in zip(names, kinds)]

    def body(*refs):
        p, got = refs[:n], refs[n:2 * n]
        send_sems, recv_sems = refs[2 * n], refs[2 * n + 1]
        token = refs[-1]
        for cp in _shard_copies(kinds, p, got, send_sems, recv_sems):
            cp.start()
        token[...] = jnp.zeros(token.shape, F32)

    sem_shape = pltpu.SemaphoreType.DMA((3 * n,))
    outs = pl.pallas_call(
        body, name=f"grad_shard_exchange_start_{g}",
        in_specs=[HBM] * (2 * n),
        out_specs=[SEM, SEM] + [HBM] * (2 * n) + [pl.BlockSpec(memory_space=pltpu.VMEM)],
        out_shape=[sem_shape, sem_shape] + [pltpu.HBM(a.shape, a.dtype) for a in srcs + lands] + [jax.ShapeDtypeStruct((8, 128), F32)],
        input_output_aliases={i: 2 + i for i in range(2 * n)},
        compiler_params=pltpu.CompilerParams(has_side_effects=EFFECT),
    )(*srcs, *lands)
    return (outs[0], outs[1], list(outs[2:2 + n]), list(outs[2 + n:2 + 2 * n])), outs[-1]


def _shard_exchange_wait(g, names, state, after):
    kinds = [KIND[nm] for nm in names]
    n = len(names)
    send_sem, recv_sem, srcs, lands = state

    def body(*refs):
        p, got = refs[:n], refs[n:2 * n]
        for cp in _shard_copies(kinds, p, got, refs[2 * n], refs[2 * n + 1]):
            cp.wait_send()
            cp.wait_recv()

    outs = pl.pallas_call(
        body, name=f"grad_shard_exchange_wait_{g}",
        in_specs=[HBM] * (2 * n) + [SEM, SEM, ANY], out_specs=[HBM] * (2 * n),
        out_shape=[pltpu.HBM(a.shape, a.dtype) for a in srcs + lands],
        input_output_aliases={i: i for i in range(2 * n)},
        compiler_params=pltpu.CompilerParams(has_side_effects=EFFECT),
    )(*srcs, *lands, send_sem, recv_sem, after)
    return dict(zip(names, outs[:n])), dict(zip(names, outs[n:]))


def _shard_sum(name, kind, psum, got, sc_arr):
    if kind == "col":
        rows, cols = psum.shape
        cs = cols // N_CHIPS
        tr = 128
        grid = (rows // tr,)
        p_spec = pl.BlockSpec((tr, cs), lambda i, sc: (i, sc[0]))
        g_spec = pl.BlockSpec((3, tr, cs), lambda i, sc: (0, i, 0))
        o_spec = pl.BlockSpec((None, tr, cs), lambda i, sc: (sc[1], i, 0))
        out_shape = (2, rows, cs)
    elif kind == "row":
        _, rows, cols = psum.shape
        grid = (1,)
        p_spec = pl.BlockSpec((None, rows, cols), lambda i, sc: (sc[0], 0, 0))
        g_spec = pl.BlockSpec((3, rows, cols), lambda i, sc: (0, 0, 0))
        o_spec = pl.BlockSpec((None, rows, cols), lambda i, sc: (sc[1], 0, 0))
        out_shape = (2, rows, cols)
    else:
        _, _, rows, cols = psum.shape
        grid = (1,)
        p_spec = pl.BlockSpec((GROUPS, None, rows, cols), lambda i, sc: (0, sc[0], 0, 0))
        g_spec = pl.BlockSpec((3, GROUPS, rows, cols), lambda i, sc: (0, 0, 0, 0))
        o_spec = pl.BlockSpec((GROUPS, None, rows, cols), lambda i, sc: (0, sc[1], 0, 0))
        out_shape = (GROUPS, 2, rows, cols)

    def body(sc_ref, p_ref, g_ref, o_ref):
        o_ref[...] = ((p_ref[...].astype(F32) + g_ref[0].astype(F32)) + g_ref[1].astype(F32)) + g_ref[2].astype(F32)

    return pl.pallas_call(
        body, name=name,
        grid_spec=pltpu.PrefetchScalarGridSpec(num_scalar_prefetch=1, grid=grid, in_specs=[p_spec, g_spec], out_specs=o_spec),
        out_shape=jax.ShapeDtypeStruct(out_shape, F32),
        compiler_params=_cparams(("parallel",)),
    )(sc_arr, psum, got)


def _half_exchange(tag, names, bufs):
    kinds = [KIND[nm] for nm in names]
    n = len(names)

    def body(*refs):
        out = refs[n:2 * n]
        send_sems, recv_sems = refs[2 * n:]
        x, y, c, _ = _place()
        sib = (x, y, 1 - c)
        cps = []
        for a in range(n):
            mine = _half(kinds[a], out[a], c)
            cp = pltpu.make_async_remote_copy(src_ref=mine, dst_ref=mine, send_sem=send_sems.at[a], recv_sem=recv_sems.at[a],
                                              device_id=sib, device_id_type=MESH)
            cp.start()
            cps.append(cp)
        for a, cp in enumerate(cps):
            cp.wait_send()
            theirs = _half(kinds[a], out[a], 1 - c)
            pltpu.make_async_remote_copy(src_ref=theirs, dst_ref=theirs, send_sem=send_sems.at[a], recv_sem=recv_sems.at[a],
                                         device_id=sib, device_id_type=MESH).wait_recv()

    outs = pl.pallas_call(
        body, name=f"grad_half_exchange_{tag}",
        in_specs=[ANY] * n, out_specs=[ANY] * n,
        out_shape=[jax.ShapeDtypeStruct(bufs[nm].shape, F32) for nm in names],
        input_output_aliases={a: a for a in range(n)},
        scratch_shapes=[pltpu.SemaphoreType.DMA((n,)), pltpu.SemaphoreType.DMA((n,))],
    )(*[bufs[nm] for nm in names])
    return dict(zip(names, outs))


N_DEV = 8
SMALL_ROWS = 8


def _all_reduce_small(name, v):
    def body(v_ref, o_ref, buf, send_sems, recv_sems):
        x, y, c, _ = _place()
        me = 4 * x + 2 * y + c
        buf[me] = v_ref[...]
        cps = []
        for r in range(1, N_DEV):
            to = (x ^ (r >> 2), y ^ ((r >> 1) & 1), c ^ (r & 1))
            cp = pltpu.make_async_remote_copy(src_ref=v_ref, dst_ref=buf.at[me], send_sem=send_sems.at[r - 1],
                                              recv_sem=recv_sems.at[r - 1], device_id=to, device_id_type=MESH)
            cp.start()
            cps.append(cp)
        for r in range(1, N_DEV):
            pltpu.make_async_remote_copy(src_ref=v_ref, dst_ref=buf.at[me ^ r], send_sem=send_sems.at[r - 1],
                                         recv_sem=recv_sems.at[r - 1], device_id=(x, y, c), device_id_type=MESH).wait_recv()
        for cp in cps:
            cp.wait_send()
        acc = buf[0]
        for d in range(1, N_DEV):
            acc = acc + buf[d]
        o_ref[...] = acc

    vm = pl.BlockSpec(memory_space=pltpu.VMEM)
    return pl.pallas_call(
        body, name=name, in_specs=[vm], out_specs=vm,
        out_shape=jax.ShapeDtypeStruct((SMALL_ROWS, D_MODEL), F32),
        scratch_shapes=[pltpu.VMEM((N_DEV, SMALL_ROWS, D_MODEL), F32), pltpu.SemaphoreType.DMA((N_DEV - 1,)),
                        pltpu.SemaphoreType.DMA((N_DEV - 1,))],
    )(v)


def _adamw(name, w, g, m, v):
    rows, cols = w.shape
    tr = next((c for c in (256, 176, 128, 64, 32, 8) if rows % c == 0), rows)
    spec = pl.BlockSpec((tr, cols), lambda i: (i, 0))

    def body(w_ref, g_ref, m_ref, v_ref, d_ref, mo_ref, vo_ref):
        gv = g_ref[...]
        m_new = ADAM_B1 * m_ref[...] + (1.0 - ADAM_B1) * gv
        v_new = ADAM_B2 * v_ref[...] + (1.0 - ADAM_B2) * jnp.square(gv)
        m_hat = m_new / (1.0 - ADAM_B1 ** ADAM_STEP)
        v_hat = v_new / (1.0 - ADAM_B2 ** ADAM_STEP)
        d_ref[...] = -ADAM_LR * (m_hat / (jnp.sqrt(v_hat) + ADAM_EPS) + ADAM_WD * w_ref[...])
        mo_ref[...] = m_new
        vo_ref[...] = v_new

    return pl.pallas_call(
        body, name=name, grid=(rows // tr,),
        in_specs=[spec] * 4, out_specs=[spec] * 3,
        out_shape=[jax.ShapeDtypeStruct((rows, cols), F32)] * 3,
        compiler_params=_cparams(("parallel",)),
    )(w, g, m, v)


WEIGHTS = ("norm_ffn1", "ffn1_w_in", "ffn1_w_out", "norm_mix", "w_in", "gate_bias", "pool_w", "pool_scale", "w_ret_up",
           "w_pool_up", "w_out", "norm_ffn2", "ffn2_w_in", "ffn2_w_out", "norm_final")
SMALL_ROW = dict(norm_ffn1=0, norm_mix=1, gate_bias=2, pool_scale=4, norm_ffn2=5, norm_final=6)


def _as2d(a):
    return a.reshape(-1, a.shape[-1])


def kernel(x, norm_ffn1, ffn1_w_in, ffn1_w_out, norm_mix, w_in, gate_bias, pool_w, pool_scale, w_ret_up, w_pool_up, w_out, norm_ffn2, ffn2_w_in, ffn2_w_out, norm_final, loss_target, m_norm_ffn1, m_ffn1_w_in, m_ffn1_w_out, m_norm_mix, m_w_in, m_gate_bias, m_pool_w, m_pool_scale, m_w_ret_up, m_w_pool_up, m_w_out, m_norm_ffn2, m_ffn2_w_in, m_ffn2_w_out, m_norm_final, v_norm_ffn1, v_ffn1_w_in, v_ffn1_w_out, v_norm_mix, v_w_in, v_gate_bias, v_pool_w, v_pool_scale, v_w_ret_up, v_w_pool_up, v_w_out, v_norm_ffn2, v_ffn2_w_in, v_ffn2_w_out, v_norm_final):
    wt = dict(norm_ffn1=norm_ffn1, ffn1_w_in=ffn1_w_in, ffn1_w_out=ffn1_w_out, norm_mix=norm_mix, w_in=w_in, gate_bias=gate_bias,
              pool_w=pool_w, pool_scale=pool_scale, w_ret_up=w_ret_up, w_pool_up=w_pool_up, w_out=w_out, norm_ffn2=norm_ffn2,
              ffn2_w_in=ffn2_w_in, ffn2_w_out=ffn2_w_out, norm_final=norm_final)
    mom = dict(norm_ffn1=m_norm_ffn1, ffn1_w_in=m_ffn1_w_in, ffn1_w_out=m_ffn1_w_out, norm_mix=m_norm_mix, w_in=m_w_in,
               gate_bias=m_gate_bias, pool_w=m_pool_w, pool_scale=m_pool_scale, w_ret_up=m_w_ret_up, w_pool_up=m_w_pool_up,
               w_out=m_w_out, norm_ffn2=m_norm_ffn2, ffn2_w_in=m_ffn2_w_in, ffn2_w_out=m_ffn2_w_out, norm_final=m_norm_final)
    var = dict(norm_ffn1=v_norm_ffn1, ffn1_w_in=v_ffn1_w_in, ffn1_w_out=v_ffn1_w_out, norm_mix=v_norm_mix, w_in=v_w_in,
               gate_bias=v_gate_bias, pool_w=v_pool_w, pool_scale=v_pool_scale, w_ret_up=v_w_ret_up, w_pool_up=v_w_pool_up,
               w_out=v_w_out, norm_ffn2=v_norm_ffn2, ffn2_w_in=v_ffn2_w_in, ffn2_w_out=v_ffn2_w_out, norm_final=v_norm_final)

    ax, ay, ac = lax.axis_index("x"), lax.axis_index("y"), lax.axis_index("c")
    chip = 2 * ax + ay
    c_arr = jnp.reshape(ac, (1,)).astype(jnp.int32)
    sc_arr = jnp.stack([chip, ac]).astype(jnp.int32)
    bias_cols = gate_bias.shape[-1]

    shards = {nm: wt[nm][0].astype(BF16) for nm in BIG}
    gather_groups, gather_token = _gather_start(shards)
    placed = lax.dynamic_update_slice(jnp.zeros((SMALL_ROWS, D_MODEL), F32), gate_bias[0], (0, chip * bias_cols))
    bias_full = _all_reduce_small("gather_gate_bias", jnp.where(ac == 0, placed, 0.0))[:2]
    vec = dict(norm_ffn1=norm_ffn1, norm_mix=norm_mix, norm_ffn2=norm_ffn2, pool_scale=pool_scale,
               norm_final=norm_final.reshape(1, D_MODEL), gate_bias=bias_full)

    def get_w(g, after):
        return _gather_finish(g, gather_groups[g], gather_token if after is None else after)

    pending = []

    def on_grads(gr):
        g = len(pending)
        names = GRAD_GROUPS[g]
        assert set(names) == set(gr), (names, list(gr))
        views = {nm: _grad_view(KIND[nm], gr[nm]) for nm in names}
        from_sib = _pair_exchange(g, names, views)
        psums = {nm: _pair_sum(f"pair_sum_{nm}", KIND[nm], views[nm], from_sib[nm], c_arr) for nm in names}
        state, token = _shard_exchange_start(g, names, psums)
        pending.append(state)
        return token[0:1, 0:1]

    loss_local, dx, small = _local_step(x[0], loss_target[0], vec, get_w, on_grads)
    loss = lax.psum(loss_local, ("x", "y", "c"))

    packed = jnp.concatenate([small["norm_ffn1"], small["norm_mix"], small["gate_bias"], small["pool_scale"],
                              small["norm_ffn2"], small["norm_final"], jnp.zeros((1, D_MODEL), F32)], axis=0)
    small_sum = _all_reduce_small("reduce_small_grads", packed)
    grads, delta, new_m, new_v = {}, {}, {}, {}

    def adamw(nm):
        shape = wt[nm].shape
        d, m2, v2 = _adamw(f"adamw_{nm}", _as2d(wt[nm]), _as2d(grads[nm]), _as2d(mom[nm]), _as2d(var[nm]))
        delta[nm], new_m[nm], new_v[nm] = d.reshape(shape), m2.reshape(shape), v2.reshape(shape)
        return d

    for nm in ("norm_ffn1", "norm_mix", "pool_scale", "norm_ffn2"):
        grads[nm] = small_sum[SMALL_ROW[nm]][None, :]
    grads["norm_final"] = small_sum[SMALL_ROW["norm_final"]]
    grads["gate_bias"] = lax.dynamic_slice(small_sum, (SMALL_ROW["gate_bias"], chip * bias_cols), (2, bias_cols))[None]
    after = dx
    for g, names in enumerate(GRAD_GROUPS):
        psums, from_chips = _shard_exchange_wait(g, names, pending[g], after)
        bufs = {nm: _shard_sum(f"shard_sum_{nm}", KIND[nm], psums[nm], from_chips[nm], sc_arr) for nm in names}
        reduced = _half_exchange(g, names, bufs)
        for nm in names:
            grads[nm] = reduced[nm].reshape(wt[nm].shape)
            after = adamw(nm)
    for nm in WEIGHTS:
        if nm not in delta:
            adamw(nm)

    return (loss, dx[None], *[grads[nm] for nm in WEIGHTS], *[delta[nm] for nm in WEIGHTS],
            *[new_m[nm] for nm in WEIGHTS], *[new_v[nm] for nm in WEIGHTS])
```

```python
import functools

import numpy as np
import jax
import jax.numpy as jnp
from jax import lax
from jax.experimental import pallas as pl
from jax.experimental.pallas import tpu as pltpu

F32 = jnp.float32
BF16 = jnp.bfloat16
MESH = pl.DeviceIdType.MESH

D_MODEL = 1024
D_FF = 2816
HEADS = 4
HEAD_DIM = 256
GROUPS = 4
GROUP_DIM = 256
POOL_WINDOWS = (2, 4, 8, 16)
IN_WIDTH = 7 * D_MODEL
ROPE_BASE = 10000.0
NORM_EPS = 1e-6
FFN_RES_WEIGHT = 0.5
ADAM_LR, ADAM_B1, ADAM_B2, ADAM_EPS, ADAM_WD, ADAM_STEP = 0.001, 0.9, 0.999, 1e-08, 0.01, 10

N_CHIPS = 4
RET_BLOCK = 256
V7X_VMEM_LIMIT = 48 * 1024 * 1024


def _cparams(sem):
    return pltpu.CompilerParams(dimension_semantics=sem, vmem_limit_bytes=V7X_VMEM_LIMIT)


def _sigmoid(x):
    return jax.nn.sigmoid(x)


_DIMS = {"nn": (((1,), (0,)), ((), ())), "nt": (((1,), (1,)), ((), ())), "tn": (((0,), (0,)), ((), ()))}


def _matmul(name, a, b, mode, m, n, k, tm, tn, tk, out_dtypes, a_spec=None, b_spec=None, extras=(), consts=(), epilogue=None,
            resident=None):
    tm, tn, tk = min(tm, m), min(tn, n), min(tk, k)
    gi, gj, gk = m // tm, n // tn, k // tk
    assert gi * tm == m and gj * tn == n and gk * tk == k, (name, m, n, k, tm, tn, tk)
    once = dict(pipeline_mode=pl.Buffered(1))
    if a_spec is None:
        kw = once if resident == "a" else {}
        a_spec = (pl.BlockSpec((tk, tm), lambda i, j, kk: (kk, i), **kw) if mode == "tn"
                  else pl.BlockSpec((tm, tk), lambda i, j, kk: (i, kk), **kw))
    if b_spec is None:
        kw = once if resident == "b" else {}
        b_spec = (pl.BlockSpec((tn, tk), lambda i, j, kk: (j, kk), **kw) if mode == "nt"
                  else pl.BlockSpec((tk, tn), lambda i, j, kk: (kk, j), **kw))
    n_ex, n_out = len(extras) + len(consts), len(out_dtypes)
    dims = _DIMS[mode]

    def body(a_ref, b_ref, *rest):
        ex_refs, out_refs = rest[:n_ex], rest[n_ex:n_ex + n_out]

        def finish(acc):
            outs = (acc,) if epilogue is None else epilogue(acc, *[e[...] for e in ex_refs])
            for o_ref, o in zip(out_refs, outs):
                o_ref[...] = o.astype(o_ref.dtype)

        prod = lax.dot_general(a_ref[...], b_ref[...], dims, preferred_element_type=F32)
        if gk == 1:
            finish(prod)
        else:
            acc_ref = rest[n_ex + n_out]
            kk = pl.program_id(2)

            @pl.when(kk == 0)
            def _():
                acc_ref[...] = prod

            @pl.when(kk > 0)
            def _():
                acc_ref[...] += prod

            @pl.when(kk == gk - 1)
            def _():
                finish(acc_ref[...])

    o_spec = pl.BlockSpec((tm, tn), lambda i, j, kk: (i, j))
    outs = pl.pallas_call(
        body, name=name, grid=(gi, gj, gk),
        in_specs=[a_spec, b_spec] + [o_spec] * len(extras) + [pl.BlockSpec((1, tn), lambda i, j, kk: (0, j))] * len(consts),
        out_specs=[o_spec] * n_out,
        out_shape=[jax.ShapeDtypeStruct((m, n), dt) for dt in out_dtypes],
        scratch_shapes=[pltpu.VMEM((tm, tn), F32)] if gk > 1 else [],
        compiler_params=_cparams(("parallel", "parallel", "arbitrary")),
    )(a, b, *extras, *consts)
    return outs[0] if n_out == 1 else outs


def _row_spec(tm, width, col_block=0):
    return pl.BlockSpec((tm, width), lambda i: (i, col_block))


def _full_spec(shape):
    return pl.BlockSpec(shape, lambda *_: (0,) * len(shape))


def _rmsnorm_fwd(name, h, g, tm=512):
    t = h.shape[0]

    def body(h_ref, g_ref, o_ref):
        x = h_ref[...]
        r = lax.rsqrt(jnp.mean(x * x, axis=-1, keepdims=True) + NORM_EPS)
        o_ref[...] = (x * r * g_ref[...]).astype(BF16)

    return pl.pallas_call(
        body, name=name, grid=(t // tm,),
        in_specs=[_row_spec(tm, D_MODEL), _full_spec((1, D_MODEL))],
        out_specs=_row_spec(tm, D_MODEL),
        out_shape=jax.ShapeDtypeStruct((t, D_MODEL), BF16),
        compiler_params=_cparams(("parallel",)),
    )(h, g)


def _proj_norm_bwd(name, a, a_spec, parts, w, h, g, dres, tm):
    t = h.shape[0]

    def body(a_ref, w_ref, h_ref, g_ref, dres_ref, dh_ref, dhb_ref, dg_ref):
        i = pl.program_id(0)
        dn_v = None
        for lead, k0, k1 in parts:
            term = _dot(a_ref[...] if lead is None else a_ref[lead], w_ref[:, k0:k1], "nt")
            dn_v = term if dn_v is None else dn_v + term
        x = h_ref[...]
        r = lax.rsqrt(jnp.mean(x * x, axis=-1, keepdims=True) + NORM_EPS)
        xh = x * r
        dxh = dn_v * g_ref[...]
        dh = dres_ref[...] + r * (dxh - xh * jnp.mean(dxh * xh, axis=-1, keepdims=True))
        dh_ref[...] = dh
        dhb_ref[...] = dh.astype(BF16)
        part = jnp.sum(dn_v * xh, axis=0, keepdims=True)

        @pl.when(i == 0)
        def _():
            dg_ref[...] = part

        @pl.when(i > 0)
        def _():
            dg_ref[...] += part

    row = _row_spec(tm, D_MODEL)
    return pl.pallas_call(
        body, name=name, grid=(t // tm,),
        in_specs=[a_spec, pl.BlockSpec(w.shape, lambda i: (0, 0), pipeline_mode=pl.Buffered(1)), row, _full_spec((1, D_MODEL)), row],
        out_specs=[row, row, _full_spec((1, D_MODEL))],
        out_shape=[jax.ShapeDtypeStruct((t, D_MODEL), F32), jax.ShapeDtypeStruct((t, D_MODEL), BF16),
                   jax.ShapeDtypeStruct((1, D_MODEL), F32)],
        compiler_params=_cparams(("arbitrary",)),
    )(a, w, h, g, dres)


def _loss_and_grad(name, h, g, target, tm=512):
    t = h.shape[0]

    def body(h_ref, g_ref, t_ref, dh_ref, dhb_ref, dg_ref, loss_ref):
        i = pl.program_id(0)
        x = h_ref[...]
        gv = g_ref[...]
        r = lax.rsqrt(jnp.mean(x * x, axis=-1, keepdims=True) + NORM_EPS)
        xh = x * r
        err = xh * gv - t_ref[...]
        row = jnp.mean(err * err, axis=-1, keepdims=True)
        part_loss = 0.5 * jnp.sum(row, axis=0, keepdims=True)
        dy = err * (1.0 / D_MODEL)
        dxh = dy * gv
        dh = r * (dxh - xh * jnp.mean(dxh * xh, axis=-1, keepdims=True))
        dh_ref[...] = dh
        dhb_ref[...] = dh.astype(BF16)
        part = jnp.sum(dy * xh, axis=0, keepdims=True)

        @pl.when(i == 0)
        def _():
            dg_ref[...] = part
            loss_ref[...] = jnp.zeros(loss_ref.shape, F32) + part_loss

        @pl.when(i > 0)
        def _():
            dg_ref[...] += part
            loss_ref[...] += part_loss

    return pl.pallas_call(
        body, name=name, grid=(t // tm,),
        in_specs=[_row_spec(tm, D_MODEL), _full_spec((1, D_MODEL)), _row_spec(tm, D_MODEL)],
        out_specs=[_row_spec(tm, D_MODEL), _row_spec(tm, D_MODEL), _full_spec((1, D_MODEL)), _full_spec((8, 128))],
        out_shape=[jax.ShapeDtypeStruct((t, D_MODEL), F32), jax.ShapeDtypeStruct((t, D_MODEL), BF16),
                   jax.ShapeDtypeStruct((1, D_MODEL), F32), jax.ShapeDtypeStruct((8, 128), F32)],
        compiler_params=_cparams(("arbitrary",)),
    )(h, g, target)


def _rope_tables(t):
    half = HEAD_DIM // 2
    inv_freq = ROPE_BASE ** (-jnp.arange(half, dtype=F32) / half)
    ang = jnp.arange(t, dtype=F32)[:, None] * inv_freq[None, :]
    return jnp.cos(ang), jnp.sin(ang)


def _rotary_fwd(name, proj, cos, sin, tm=512):
    t = proj.shape[0]
    half = HEAD_DIM // 2
    k_scale = HEAD_DIM ** -0.5

    def body(q_ref, k_ref, v_ref, c_ref, s_ref, qo_ref, ko_ref, vo_ref):
        c, s = c_ref[...], s_ref[...]
        for hh in range(HEADS):
            lo, mid, hi = hh * HEAD_DIM, hh * HEAD_DIM + half, (hh + 1) * HEAD_DIM
            x1, x2 = q_ref[:, lo:mid], q_ref[:, mid:hi]
            qo_ref[:, lo:mid] = x1 * c - x2 * s
            qo_ref[:, mid:hi] = x1 * s + x2 * c
            x1, x2 = k_ref[:, lo:mid], k_ref[:, mid:hi]
            ko_ref[:, lo:mid] = (x1 * c - x2 * s) * k_scale
            ko_ref[:, mid:hi] = (x1 * s + x2 * c) * k_scale
        vo_ref[...] = v_ref[...].astype(BF16)

    return pl.pallas_call(
        body, name=name, grid=(t // tm,),
        in_specs=[_row_spec(tm, D_MODEL, 0), _row_spec(tm, D_MODEL, 1), _row_spec(tm, D_MODEL, 2),
                  _row_spec(tm, half), _row_spec(tm, half)],
        out_specs=[_row_spec(tm, D_MODEL)] * 3,
        out_shape=[jax.ShapeDtypeStruct((t, D_MODEL), F32), jax.ShapeDtypeStruct((t, D_MODEL), F32),
                   jax.ShapeDtypeStruct((t, D_MODEL), BF16)],
        compiler_params=_cparams(("parallel",)),
    )(proj, proj, proj, cos, sin)


def _rotary_bwd(name, dqr, dkr, cos, sin, tm=512):
    t = dqr.shape[0]
    half = HEAD_DIM // 2
    k_scale = HEAD_DIM ** -0.5

    def body(q_ref, k_ref, c_ref, s_ref, qo_ref, ko_ref):
        c, s = c_ref[...], s_ref[...]
        for hh in range(HEADS):
            lo, mid, hi = hh * HEAD_DIM, hh * HEAD_DIM + half, (hh + 1) * HEAD_DIM
            y1, y2 = q_ref[:, lo:mid], q_ref[:, mid:hi]
            qo_ref[:, lo:mid] = (y1 * c + y2 * s).astype(BF16)
            qo_ref[:, mid:hi] = (y2 * c - y1 * s).astype(BF16)
            y1, y2 = k_ref[:, lo:mid], k_ref[:, mid:hi]
            ko_ref[:, lo:mid] = ((y1 * c + y2 * s) * k_scale).astype(BF16)
            ko_ref[:, mid:hi] = ((y2 * c - y1 * s) * k_scale).astype(BF16)

    return pl.pallas_call(
        body, name=name, grid=(t // tm,),
        in_specs=[_row_spec(tm, D_MODEL), _row_spec(tm, D_MODEL), _row_spec(tm, half), _row_spec(tm, half)],
        out_specs=[_row_spec(tm, D_MODEL)] * 2,
        out_shape=[jax.ShapeDtypeStruct((t, D_MODEL), BF16)] * 2,
        compiler_params=_cparams(("parallel",)),
    )(dqr, dkr, cos, sin)


def _retention_tables():
    b, chunk = RET_BLOCK, 64
    gamma = 1.0 - 2.0 ** (-5.0 - np.arange(HEADS, dtype=np.float64))
    log_g = np.log(gamma)[:, None, None]
    i = np.arange(b)[:, None]
    j = np.arange(b)[None, :]
    same = (i // chunk) == (j // chunk)
    earlier = (j // chunk) < (i // chunk)
    expo = np.where(same, np.abs(i - j), np.where(earlier, i - j, 0)).astype(np.float64)
    mask = np.where(same | earlier, 1.0, 0.0)
    dmat = np.exp(log_g * expo[None]) * mask[None]
    qd = np.exp(log_g[:, :, 0] * (np.arange(b)[None, :] + 1.0))
    kd = np.exp(log_g[:, :, 0] * (b - 1.0 - np.arange(b)[None, :]))
    cd = np.exp(log_g[:, :, 0] * b) * np.ones((1, HEAD_DIM))
    as32 = lambda v: jnp.asarray(v.astype(np.float32))
    return (as32(dmat), as32(np.swapaxes(dmat, 1, 2)), as32(qd[:, :, None]), as32(kd[:, :, None]), as32(cd[:, None, :]))


def _dot(a, b, mode="nn"):
    return lax.dot_general(a, b, _DIMS[mode], preferred_element_type=F32)


def _head_specs(nb, rev=False):
    blk = (RET_BLOCK, HEAD_DIM)
    pos = (lambda h, n: (nb - 1 - n, h)) if rev else (lambda h, n: (n, h))
    tok = pl.BlockSpec(blk, pos)
    gr = pl.BlockSpec(blk, (lambda h, n: (nb - 1 - n, 3 * HEADS + h)) if rev else (lambda h, n: (n, 3 * HEADS + h)))
    tab = pl.BlockSpec((None, RET_BLOCK, RET_BLOCK), lambda h, n: (h, 0, 0))
    col = pl.BlockSpec((None, RET_BLOCK, 1), lambda h, n: (h, 0, 0))
    rowv = pl.BlockSpec((None, 1, HEAD_DIM), lambda h, n: (h, 0, 0))
    st = pl.BlockSpec((None, None, HEAD_DIM, HEAD_DIM), (lambda h, n: (h, nb - 1 - n, 0, 0)) if rev else (lambda h, n: (h, n, 0, 0)))
    return tok, gr, tab, col, rowv, st


def _retention_fwd(name, qr, kr, vb, proj, tables):
    t = qr.shape[0]
    nb = t // RET_BLOCK
    dmat, _, qd, kd, cd = tables
    tok, gr, tab, col, rowv, st = _head_specs(nb)

    def body(q_ref, k_ref, v_ref, g_ref, d_ref, qd_ref, kd_ref, cd_ref, o_ref, ret_ref, st_ref, state):
        n = pl.program_id(1)

        @pl.when(n == 0)
        def _():
            state[...] = jnp.zeros(state.shape, F32)

        q, k, v = q_ref[...], k_ref[...], v_ref[...]
        s = _dot(q.astype(BF16), k.astype(BF16), "nt") * d_ref[...]
        stb = state[...].astype(BF16)
        st_ref[...] = stb
        o = _dot(s.astype(BF16), v) + _dot((q * qd_ref[...]).astype(BF16), stb)
        o_ref[...] = o
        rn = o * lax.rsqrt(jnp.mean(o * o, axis=-1, keepdims=True) + NORM_EPS)
        g = g_ref[...]
        ret_ref[...] = (rn * (g * _sigmoid(g))).astype(BF16)
        state[...] = state[...] * cd_ref[...] + _dot((k * kd_ref[...]).astype(BF16), v, "tn")

    return pl.pallas_call(
        body, name=name, grid=(HEADS, nb),
        in_specs=[tok, tok, tok, gr, tab, col, col, rowv],
        out_specs=[tok, tok, st],
        out_shape=[jax.ShapeDtypeStruct((t, D_MODEL), F32), jax.ShapeDtypeStruct((t, D_MODEL), BF16),
                   jax.ShapeDtypeStruct((HEADS, nb, HEAD_DIM, HEAD_DIM), BF16)],
        scratch_shapes=[pltpu.VMEM((HEAD_DIM, HEAD_DIM), F32)],
        compiler_params=_cparams(("parallel", "arbitrary")),
    )(qr, kr, vb, proj, dmat, qd, kd, cd)


def _retention_bwd(name, dret, o, qr, kr, vb, proj, states, tables):
    t = qr.shape[0]
    nb = t // RET_BLOCK
    dmat, dmat_t, qd, kd, cd = tables
    tok, gr, tab, col, rowv, st = _head_specs(nb, rev=True)

    def body(dr_ref, o_ref, q_ref, k_ref, v_ref, g_ref, st_ref, d_ref, dt_ref, qd_ref, kd_ref, cd_ref,
             dq_ref, dk_ref, dv_ref, dg_ref, gstate):
        n = pl.program_id(1)

        @pl.when(n == 0)
        def _():
            gstate[...] = jnp.zeros(gstate.shape, F32)

        o_v, g, dr = o_ref[...], g_ref[...], dr_ref[...]
        sg = _sigmoid(g)
        r = lax.rsqrt(jnp.mean(o_v * o_v, axis=-1, keepdims=True) + NORM_EPS)
        rn = o_v * r
        d_rn = dr * (g * sg)
        dg_ref[...] = (dr * rn * (sg * (1.0 + g * (1.0 - sg)))).astype(BF16)
        d_o = r * (d_rn - rn * jnp.mean(d_rn * rn, axis=-1, keepdims=True))
        dob = d_o.astype(BF16)

        q, k, v = q_ref[...], k_ref[...], v_ref[...]
        qb, kb = q.astype(BF16), k.astype(BF16)
        qdv, kdv = qd_ref[...], kd_ref[...]
        s_t = (_dot(kb, qb, "nt") * dt_ref[...]).astype(BF16)
        p_t = (_dot(v, dob, "nt") * dt_ref[...]).astype(BF16)
        p = (_dot(dob, v, "nt") * d_ref[...]).astype(BF16)
        stb = st_ref[...]
        gb = gstate[...].astype(BF16)
        dq_ref[...] = _dot(p, kb) + _dot(dob, stb, "nt") * qdv
        dk_ref[...] = _dot(p_t, qb) + _dot(v, gb, "nt") * kdv
        dv_ref[...] = (_dot(s_t, dob) + _dot((k * kdv).astype(BF16), gb)).astype(BF16)
        gstate[...] = gstate[...] * cd_ref[...] + _dot((q * qdv).astype(BF16), dob, "tn")

    return pl.pallas_call(
        body, name=name, grid=(HEADS, nb),
        in_specs=[tok, tok, tok, tok, tok, gr, st, tab, tab, col, col, rowv],
        out_specs=[tok, tok, tok, tok],
        out_shape=[jax.ShapeDtypeStruct((t, D_MODEL), F32), jax.ShapeDtypeStruct((t, D_MODEL), F32),
                   jax.ShapeDtypeStruct((t, D_MODEL), BF16), jax.ShapeDtypeStruct((t, D_MODEL), BF16)],
        scratch_shapes=[pltpu.VMEM((HEAD_DIM, HEAD_DIM), F32)],
        compiler_params=_cparams(("parallel", "arbitrary")),
    )(dret, o, qr, kr, vb, proj, states, dmat, dmat_t, qd, kd, cd)


POOL_TILE = 256


def _pool_tables():
    b = POOL_TILE
    tt = np.arange(b)[:, None]
    jj = np.arange(b)[None, :]
    cur, prev = [], []
    for w in POOL_WINDOWS:
        cur.append(((tt - jj >= 0) & (tt - jj <= w - 1)).astype(np.float32))
        prev.append((tt - (jj - b) <= w - 1).astype(np.float32))
    cur, prev = np.stack(cur), np.stack(prev)
    as16 = lambda v: jnp.asarray(v, dtype=BF16)
    return as16(cur), as16(prev), as16(np.swapaxes(cur, 1, 2)), as16(np.swapaxes(prev, 1, 2))


def _split2(x):
    hi = x.astype(BF16)
    return hi, (x - hi.astype(F32)).astype(BF16)


def _pool_count(n, g):
    tpos = n * POOL_TILE + lax.broadcasted_iota(jnp.int32, (POOL_TILE, 1), 0)
    return jnp.minimum(tpos + 1, jnp.left_shift(2, g)).astype(F32)


def _pool_fwd(name, proj, pool_w, scale, tables):
    t = proj.shape[0]
    nb = t // POOL_TILE
    mc, mp, _, _ = tables
    p_block0 = 4 * D_MODEL // GROUP_DIM
    blk = (POOL_TILE, GROUP_DIM)
    tab = pl.BlockSpec((None, POOL_TILE, POOL_TILE), lambda g, n: (g, 0, 0))

    def body(pc_ref, pp_ref, mc_ref, mp_ref, w_ref, sc_ref, pm_ref, mix_ref, po_ref):
        g, n = pl.program_id(0), pl.program_id(1)
        p = pc_ref[...]
        c_hi, c_lo = _split2(p)
        p_hi, p_lo = _split2(pp_ref[...])
        mcv, mpv = mc_ref[...], mp_ref[...]
        win = _dot(mcv, c_hi) + _dot(mcv, c_lo)
        before = _dot(mpv, p_hi) + _dot(mpv, p_lo)
        win = win + jnp.where(n > 0, before, 0.0)
        pm = (win / _pool_count(n, g) - p).astype(BF16)
        pm_ref[...] = pm
        mixed = _dot(pm, w_ref[...])
        mix_ref[...] = mixed
        po_ref[...] = (mixed * sc_ref[...]).astype(BF16)

    return pl.pallas_call(
        body, name=name, grid=(GROUPS, nb),
        in_specs=[pl.BlockSpec(blk, lambda g, n: (n, p_block0 + g)),
                  pl.BlockSpec(blk, lambda g, n: (jnp.maximum(n - 1, 0), p_block0 + g)),
                  tab, tab,
                  pl.BlockSpec((None, GROUP_DIM, GROUP_DIM), lambda g, n: (g, 0, 0)),
                  pl.BlockSpec((1, GROUP_DIM), lambda g, n: (0, g))],
        out_specs=[pl.BlockSpec(blk, lambda g, n: (n, g))] * 3,
        out_shape=[jax.ShapeDtypeStruct((t, D_MODEL), BF16), jax.ShapeDtypeStruct((t, D_MODEL), F32),
                   jax.ShapeDtypeStruct((t, D_MODEL), BF16)],
        compiler_params=_cparams(("parallel", "parallel")),
    )(proj, proj, mc, mp, pool_w, scale)


def _pool_bwd(name, dpo, pm, mixed, pool_w, scale, tables):
    t = dpo.shape[0]
    nb = t // POOL_TILE
    _, _, mct, mpt = tables
    blk = (POOL_TILE, GROUP_DIM)
    cur = pl.BlockSpec(blk, lambda g, n: (n, g))
    nxt = pl.BlockSpec(blk, lambda g, n: (jnp.minimum(n + 1, nb - 1), g))
    tab = pl.BlockSpec((None, POOL_TILE, POOL_TILE), lambda g, n: (g, 0, 0))
    wspec = pl.BlockSpec((None, GROUP_DIM, GROUP_DIM), lambda g, n: (g, 0, 0))
    sspec = pl.BlockSpec((1, GROUP_DIM), lambda g, n: (0, g))

    def body(dc_ref, dn_ref, pm_ref, mix_ref, mct_ref, mpt_ref, w_ref, sc_ref, dp_ref, dw_ref, ds_ref):
        g, n = pl.program_id(0), pl.program_id(1)
        dc, sc, w = dc_ref[...], sc_ref[...], w_ref[...]
        dmix_c = (dc * sc).astype(BF16)
        dmix_n = (dn_ref[...] * sc).astype(BF16)
        dpm_c = _dot(dmix_c, w, "nt")
        dpm_n = _dot(dmix_n, w, "nt")
        e_hi, e_lo = _split2(dpm_c / _pool_count(n, g))
        f_hi, f_lo = _split2(dpm_n / _pool_count(n + 1, g))
        mctv, mptv = mct_ref[...], mpt_ref[...]
        back = _dot(mctv, e_hi) + _dot(mctv, e_lo)
        after = _dot(mptv, f_hi) + _dot(mptv, f_lo)
        dp_ref[...] = (back + jnp.where(n < nb - 1, after, 0.0) - dpm_c).astype(BF16)
        dw_part = _dot(pm_ref[...], dmix_c, "tn")
        ds_part = jnp.sum(dc * mix_ref[...], axis=0, keepdims=True)

        @pl.when(n == 0)
        def _():
            dw_ref[...] = dw_part
            ds_ref[...] = ds_part

        @pl.when(n > 0)
        def _():
            dw_ref[...] += dw_part
            ds_ref[...] += ds_part

    return pl.pallas_call(
        body, name=name, grid=(GROUPS, nb),
        in_specs=[cur, nxt, cur, cur, tab, tab, wspec, sspec],
        out_specs=[cur, wspec, sspec],
        out_shape=[jax.ShapeDtypeStruct((t, D_MODEL), BF16), jax.ShapeDtypeStruct((GROUPS, GROUP_DIM, GROUP_DIM), F32),
                   jax.ShapeDtypeStruct((1, D_MODEL), F32)],
        compiler_params=_cparams(("parallel", "arbitrary")),
    )(dpo, dpo, pm, mixed, mct, mpt, pool_w, scale)


GATE0_BLOCK, GATE1_BLOCK = 5, 6


def _merge_fwd(name, ret, po, w_ru, w_pu, proj, bias, tm=512):
    t = ret.shape[0]

    def body(r_ref, p_ref, wr_ref, wp_ref, g0_ref, g1_ref, b_ref, m_ref, ru_ref, pu_ref):
        ru = _dot(r_ref[...], wr_ref[...])
        pu = _dot(p_ref[...], wp_ref[...])
        ru_ref[...] = ru
        pu_ref[...] = pu
        m_ref[...] = (_sigmoid(g0_ref[...] + b_ref[0:1, :]) * ru + _sigmoid(g1_ref[...] + b_ref[1:2, :]) * pu).astype(BF16)

    row = _row_spec(tm, D_MODEL)
    wspec = _full_spec((D_MODEL, D_MODEL))
    return pl.pallas_call(
        body, name=name, grid=(t // tm,),
        in_specs=[row, row, wspec, wspec, _row_spec(tm, D_MODEL, GATE0_BLOCK), _row_spec(tm, D_MODEL, GATE1_BLOCK),
                  _full_spec((2, D_MODEL))],
        out_specs=[row, row, row],
        out_shape=[jax.ShapeDtypeStruct((t, D_MODEL), BF16), jax.ShapeDtypeStruct((t, D_MODEL), F32),
                   jax.ShapeDtypeStruct((t, D_MODEL), F32)],
        compiler_params=_cparams(("parallel",)),
    )(ret, po, w_ru, w_pu, proj, proj, bias)


def _merge_bwd(name, dm, ru, pu, proj, bias, tm=512):
    t = dm.shape[0]

    def body(dm_ref, ru_ref, pu_ref, g0_ref, g1_ref, b_ref, dru_ref, dpu_ref, dg0_ref, dg1_ref, db_ref):
        i = pl.program_id(0)
        d = dm_ref[...]
        s0 = _sigmoid(g0_ref[...] + b_ref[0:1, :])
        s1 = _sigmoid(g1_ref[...] + b_ref[1:2, :])
        dru_ref[...] = (d * s0).astype(BF16)
        dpu_ref[...] = (d * s1).astype(BF16)
        dg0 = d * ru_ref[...] * (s0 * (1.0 - s0))
        dg1 = d * pu_ref[...] * (s1 * (1.0 - s1))
        dg0_ref[...] = dg0.astype(BF16)
        dg1_ref[...] = dg1.astype(BF16)
        part0 = jnp.sum(dg0, axis=0, keepdims=True)
        part1 = jnp.sum(dg1, axis=0, keepdims=True)

        @pl.when(i == 0)
        def _():
            db_ref[0:1, :] = part0
            db_ref[1:2, :] = part1

        @pl.when(i > 0)
        def _():
            db_ref[0:1, :] += part0
            db_ref[1:2, :] += part1

    row = _row_spec(tm, D_MODEL)
    return pl.pallas_call(
        body, name=name, grid=(t // tm,),
        in_specs=[row, row, row, _row_spec(tm, D_MODEL, GATE0_BLOCK), _row_spec(tm, D_MODEL, GATE1_BLOCK),
                  _full_spec((2, D_MODEL))],
        out_specs=[row, row, row, row, _full_spec((2, D_MODEL))],
        out_shape=[jax.ShapeDtypeStruct((t, D_MODEL), BF16)] * 4 + [jax.ShapeDtypeStruct((2, D_MODEL), F32)],
        compiler_params=_cparams(("arbitrary",)),
    )(dm, ru, pu, proj, proj, bias)


def _half_scale(acc):
    return (FFN_RES_WEIGHT * acc,)


def _residual_half(acc, res):
    return (res + FFN_RES_WEIGHT * acc,)


def _normed(h, g):
    return h * lax.rsqrt(jnp.mean(h * h, axis=-1, keepdims=True) + NORM_EPS) * g


def _residual_half_norm(acc, res, g):
    h = res + FFN_RES_WEIGHT * acc
    return h, _normed(h, g)


def _residual_norm(acc, res, g):
    h = res + acc
    return h, _normed(h, g)


FF_TILE = D_FF // 2
DW_TILE = 256


def _ffn_in(name, nrm, w_in, tm=512):
    t = nrm.shape[0]
    nj = D_FF // FF_TILE

    def body(n_ref, wg_ref, wu_ref, a_ref, mid_ref):
        nv = n_ref[...]
        gate = _dot(nv, wg_ref[...])
        up = _dot(nv, wu_ref[...])
        a_ref[0] = gate
        a_ref[1] = up
        mid_ref[...] = (gate * _sigmoid(gate) * up).astype(BF16)

    return pl.pallas_call(
        body, name=name, grid=(t // tm, nj),
        in_specs=[pl.BlockSpec((tm, D_MODEL), lambda i, j: (i, 0)),
                  pl.BlockSpec((D_MODEL, FF_TILE), lambda i, j: (0, j)),
                  pl.BlockSpec((D_MODEL, FF_TILE), lambda i, j: (0, j + nj))],
        out_specs=[pl.BlockSpec((2, tm, FF_TILE), lambda i, j: (0, i, j)), pl.BlockSpec((tm, FF_TILE), lambda i, j: (i, j))],
        out_shape=[jax.ShapeDtypeStruct((2, t, D_FF), F32), jax.ShapeDtypeStruct((t, D_FF), BF16)],
        compiler_params=_cparams(("parallel", "parallel")),
    )(nrm, w_in, w_in)


def _ffn_dact(name, dout_b, w_out, a, tm=512):
    t = dout_b.shape[0]

    def body(d_ref, w_ref, a_ref, da_ref):
        dm = FFN_RES_WEIGHT * _dot(d_ref[...], w_ref[...], "nt")
        gate, up = a_ref[0], a_ref[1]
        s = _sigmoid(gate)
        da_ref[0] = (dm * up * (s * (1.0 + gate * (1.0 - s)))).astype(BF16)
        da_ref[1] = (dm * (gate * s)).astype(BF16)

    blk = pl.BlockSpec((2, tm, FF_TILE), lambda i, j: (0, i, j))
    return pl.pallas_call(
        body, name=name, grid=(t // tm, D_FF // FF_TILE),
        in_specs=[pl.BlockSpec((tm, D_MODEL), lambda i, j: (i, 0)), pl.BlockSpec((FF_TILE, D_MODEL), lambda i, j: (j, 0)), blk],
        out_specs=blk,
        out_shape=jax.ShapeDtypeStruct((2, t, D_FF), BF16),
        compiler_params=_cparams(("parallel", "parallel")),
    )(dout_b, w_out, a)


def _ffn_fwd(tag, h, nrm, get_w_in, get_w_out, next_g=None):
    t = h.shape[0]
    w_in = get_w_in(nrm)
    a, mid = _ffn_in(f"{tag}_in", nrm, w_in, tm=min(512, t))
    w_out = get_w_out(mid)
    if next_g is None:
        out = _matmul(f"{tag}_out", mid, w_out, "nn", t, D_MODEL, D_FF, 512, D_MODEL, D_FF, [F32],
                      extras=(h,), epilogue=_residual_half)
        nxt = None
    else:
        out, nxt = _matmul(f"{tag}_out", mid, w_out, "nn", t, D_MODEL, D_FF, 512, D_MODEL, D_FF, [F32, BF16],
                           extras=(h,), consts=(next_g,), epilogue=_residual_half_norm)
    return out, nxt, (nrm, a, mid, w_in, w_out)


def _ffn_bwd(tag, h, g, saved, dout, dout_b, on_grads):
    t = h.shape[0]
    nrm, a, mid, w_in, w_out = saved
    d_w_out = _matmul(f"{tag}_dwout", mid, dout_b, "tn", D_FF, D_MODEL, t, DW_TILE, D_MODEL, t, [BF16], epilogue=_half_scale,
                      resident="b")
    da = _ffn_dact(f"{tag}_dact", dout_b, w_out, a, tm=min(512, t))
    nj = D_FF // DW_TILE
    d_w_in = _matmul(f"{tag}_dwin", nrm, da, "tn", D_MODEL, 2 * D_FF, t, D_MODEL, DW_TILE, t, [BF16], resident="a",
                     b_spec=pl.BlockSpec((None, t, DW_TILE), lambda i, j, kk: (j // nj, 0, j % nj)))
    tie = on_grads({f"{tag}_w_in": d_w_in, f"{tag}_w_out": d_w_out})
    tm = min(256, t)
    dh, dh_b, dg = _proj_norm_bwd(f"{tag}_dn", da, pl.BlockSpec((2, tm, D_FF), lambda i: (0, i, 0)),
                                  ((0, 0, D_FF), (1, D_FF, 2 * D_FF)), w_in, h, g if tie is None else g + tie, dout, tm)
    return dh, dh_b, dg


def _local_step(x, target, vec, get_w, on_grads):
    t = x.shape[0]
    cos, sin = _rope_tables(t)
    rtab = _retention_tables()
    ptab = _pool_tables()
    w = {}

    def getter(group, name):
        def get(after):
            if name not in w:
                w.update(get_w(group, after))
            return w[name]
        return get

    nrm1 = _rmsnorm_fwd("ffn1_norm", x, vec["norm_ffn1"])
    h1, u, s1 = _ffn_fwd("ffn1", x, nrm1, getter(0, "ffn1_w_in"), getter(1, "ffn1_w_out"), vec["norm_mix"])
    w.update(get_w(2, u))
    proj = _matmul("mix_in", u, w["w_in"], "nn", t, IN_WIDTH, D_MODEL, 1024, 1024, D_MODEL, [F32])
    qr, kr, vb = _rotary_fwd("rotary", proj, cos, sin)
    o, ret, states = _retention_fwd("retention", qr, kr, vb, proj, rtab)
    pm, mixed, po = _pool_fwd("pool", proj, w["pool_w"], vec["pool_scale"], ptab)
    merged, ru, pu = _merge_fwd("merge", ret, po, w["w_ret_up"], w["w_pool_up"], proj, vec["gate_bias"])
    h2, nrm2 = _matmul("mix_out", merged, w["w_out"], "nn", t, D_MODEL, D_MODEL, 512, D_MODEL, D_MODEL, [F32, BF16],
                       extras=(h1,), consts=(vec["norm_ffn2"],), epilogue=_residual_norm)
    h3, _, s2 = _ffn_fwd("ffn2", h2, nrm2, getter(3, "ffn2_w_in"), getter(3, "ffn2_w_out"))
    dh3, dh3_b, dg_final, loss = _loss_and_grad("loss", h3, vec["norm_final"], target)

    dh2, dh2_b, dg_ffn2 = _ffn_bwd("ffn2", h2, vec["norm_ffn2"], s2, dh3, dh3_b, on_grads)
    dm = _matmul("mix_dmerged", dh2_b, w["w_out"], "nt", t, D_MODEL, D_MODEL, 1024, D_MODEL, D_MODEL, [F32])
    d_w_out = _matmul("mix_dwout", merged, dh2_b, "tn", D_MODEL, D_MODEL, t, D_MODEL, D_MODEL, 1024, [BF16])
    dru, dpu, dg0, dg1, d_bias = _merge_bwd("merge_bwd", dm, ru, pu, proj, vec["gate_bias"])
    dret = _matmul("mix_dret", dru, w["w_ret_up"], "nt", t, D_MODEL, D_MODEL, 1024, D_MODEL, D_MODEL, [F32])
    d_w_ru = _matmul("mix_dwru", ret, dru, "tn", D_MODEL, D_MODEL, t, D_MODEL, D_MODEL, 1024, [BF16])
    dpo = _matmul("mix_dpool", dpu, w["w_pool_up"], "nt", t, D_MODEL, D_MODEL, 1024, D_MODEL, D_MODEL, [F32])
    d_w_pu = _matmul("mix_dwpu", po, dpu, "tn", D_MODEL, D_MODEL, t, D_MODEL, D_MODEL, 1024, [BF16])
    dp, d_pool_w, d_scale = _pool_bwd("pool_bwd", dpo, pm, mixed, w["pool_w"], vec["pool_scale"], ptab)
    dqr, dkr, dv, dgr = _retention_bwd("retention_bwd", dret, o, qr, kr, vb, proj, states, rtab)
    dq, dk = _rotary_bwd("rotary_bwd", dqr, dkr, cos, sin)
    dproj = jnp.concatenate([dq, dk, dv, dgr, dp, dg0, dg1], axis=1)
    d_w_in = _matmul("mix_dwin", u, dproj, "tn", D_MODEL, IN_WIDTH, t, D_MODEL, 512, t, [BF16], resident="a")
    tie = on_grads(dict(w_in=d_w_in, pool_w=d_pool_w.astype(BF16), w_ret_up=d_w_ru, w_pool_up=d_w_pu, w_out=d_w_out))
    g_mix = vec["norm_mix"] if tie is None else vec["norm_mix"] + tie
    tm = min(256, t)
    dh1, dh1_b, dg_mix = _proj_norm_bwd("mix_du", dproj, pl.BlockSpec((tm, IN_WIDTH), lambda i: (i, 0)), ((None, 0, IN_WIDTH),),
                                        w["w_in"], h1, g_mix, dh2, tm)
    dx, _, dg_ffn1 = _ffn_bwd("ffn1", x, vec["norm_ffn1"], s1, dh1, dh1_b, on_grads)

    small = dict(norm_ffn1=dg_ffn1, norm_mix=dg_mix, gate_bias=d_bias, pool_scale=d_scale, norm_ffn2=dg_ffn2,
                 norm_final=dg_final)
    return loss[0, 0], dx, small


BIG = ("ffn1_w_in", "ffn1_w_out", "w_in", "pool_w", "w_ret_up", "w_pool_up", "w_out", "ffn2_w_in", "ffn2_w_out")
KIND = dict(ffn1_w_in="col", ffn1_w_out="row", w_in="col", pool_w="pool", w_ret_up="row", w_pool_up="row", w_out="row",
            ffn2_w_in="col", ffn2_w_out="row")
ANY = pl.BlockSpec(memory_space=pl.ANY)


def _place():
    x, y, c = lax.axis_index("x"), lax.axis_index("y"), lax.axis_index("c")
    chips = [(1 - x, y), (x, 1 - y), (1 - x, 1 - y)]
    return x, y, c, chips


def _full_view_shape(kind, local_shape):
    if kind == "col":
        return (2, local_shape[0] // 2, N_CHIPS * local_shape[1])
    if kind == "row":
        return (N_CHIPS, 2, local_shape[0] // 2, local_shape[1])
    return (GROUPS, N_CHIPS, 2, local_shape[1] // 2, local_shape[2])


def _local_view(kind, arr):
    if kind == "pool":
        return arr.reshape(GROUPS, 2, arr.shape[1] // 2, arr.shape[2])
    return arr.reshape(2, arr.shape[0] // 2, arr.shape[1])


def _blk(kind, ref, s, c):
    if kind == "col":
        cs = ref.shape[2] // N_CHIPS
        return ref.at[c, :, pl.ds(pl.multiple_of(s * cs, 128), cs)]
    if kind == "row":
        return ref.at[s, c]
    return ref.at[:, s, c]


def _half(kind, ref, c):
    return ref.at[:, c] if kind == "pool" else ref.at[c]


def _shard(kind, ref, s):
    if kind == "col":
        cs = ref.shape[2] // N_CHIPS
        return ref.at[:, :, pl.ds(pl.multiple_of(s * cs, 128), cs)]
    if kind == "row":
        return ref.at[s]
    return ref.at[:, s]


HBM = pl.BlockSpec(memory_space=pltpu.HBM)
SEM = pl.BlockSpec(memory_space=pltpu.SEMAPHORE)
EFFECT = pltpu.SideEffectType.DATAFLOW_SIDE_EFFECTING
WEIGHT_GROUPS = (("ffn1_w_in",), ("ffn1_w_out",), ("w_in", "pool_w", "w_ret_up", "w_pool_up", "w_out"), ("ffn2_w_in", "ffn2_w_out"))
GRAD_GROUPS = (("ffn2_w_in", "ffn2_w_out"), ("w_in", "pool_w", "w_ret_up", "w_pool_up", "w_out"), ("ffn1_w_in", "ffn1_w_out"))


def _hbm(a):
    return pltpu.with_memory_space_constraint(a, pltpu.HBM)


def _natural(kind, o):
    if kind == "col":
        return o.reshape(o.shape[0] * o.shape[1], o.shape[2])
    if kind == "row":
        return o.reshape(-1, o.shape[3])
    return o.reshape(GROUPS, -1, o.shape[4])


def _ici_copy(kind, loc, full, j, chips, s, c, send_sem, recv_sem):
    px, py = chips[j]
    return (pltpu.make_async_remote_copy(src_ref=_half(kind, loc, c), dst_ref=_blk(kind, full, s, c), send_sem=send_sem,
                                         recv_sem=recv_sem, device_id=(px, py, c), device_id_type=MESH),
            pltpu.make_async_remote_copy(src_ref=_half(kind, loc, c), dst_ref=_blk(kind, full, 2 * px + py, c), send_sem=send_sem,
                                         recv_sem=recv_sem, device_id=(px, py, c), device_id_type=MESH))


def _gather_start(shards):
    names = [nm for grp in WEIGHT_GROUPS for nm in grp]
    kinds = [KIND[nm] for nm in names]
    n, ng = len(names), len(WEIGHT_GROUPS)
    locs = [_hbm(_local_view(KIND[nm], shards[nm])) for nm in names]
    lands = [_hbm(lax.empty(_full_view_shape(KIND[nm], shards[nm].shape), BF16)) for nm in names]
    first = np.cumsum([0] + [len(grp) for grp in WEIGHT_GROUPS])

    def body(*refs):
        loc, full = refs[:n], refs[n:2 * n]
        send_sems, recv_sems = refs[2 * n:2 * n + ng], refs[2 * n + ng:2 * n + 2 * ng]
        token = refs[-1]
        x, y, c, chips = _place()
        s = 2 * x + y
        for g in range(ng):
            for a in range(first[g], first[g + 1]):
                for j in range(3):
                    k = 3 * (a - first[g]) + j
                    _ici_copy(kinds[a], loc[a], full[a], j, chips, s, c, send_sems[g].at[k], recv_sems[g].at[k])[0].start()
        token[...] = jnp.zeros(token.shape, F32)

    sem_shapes = [pltpu.SemaphoreType.DMA((3 * len(grp),)) for grp in WEIGHT_GROUPS]
    outs = pl.pallas_call(
        body, name="gather_start",
        in_specs=[HBM] * (2 * n),
        out_specs=[SEM] * (2 * ng) + [HBM] * (2 * n) + [pl.BlockSpec(memory_space=pltpu.VMEM)],
        out_shape=sem_shapes + sem_shapes + [pltpu.HBM(a.shape, a.dtype) for a in locs + lands] + [jax.ShapeDtypeStruct((8, 128), F32)],
        input_output_aliases={i: 2 * ng + i for i in range(2 * n)},
        compiler_params=pltpu.CompilerParams(has_side_effects=EFFECT),
    )(*locs, *lands)
    send_sems, recv_sems = outs[:ng], outs[ng:2 * ng]
    locs_t, lands_t = outs[2 * ng:2 * ng + n], outs[2 * ng + n:2 * ng + 2 * n]
    groups = []
    for g in range(ng):
        sl = slice(first[g], first[g + 1])
        groups.append((send_sems[g], recv_sems[g], list(locs_t[sl]), list(lands_t[sl])))
    return groups, outs[-1]


def _gather_finish(g, group, after):
    names = WEIGHT_GROUPS[g]
    kinds = [KIND[nm] for nm in names]
    m = len(names)
    send_sem, recv_sem, locs, lands = group

    def wait_body(*refs):
        loc, full = refs[:m], refs[m:2 * m]
        send_sems, recv_sems = refs[2 * m], refs[2 * m + 1]
        x, y, c, chips = _place()
        s = 2 * x + y
        for a in range(m):
            for j in range(3):
                k = 3 * a + j
                sent, landed = _ici_copy(kinds[a], loc[a], full[a], j, chips, s, c, send_sems.at[k], recv_sems.at[k])
                sent.wait_send()
                landed.wait_recv()

    outs = pl.pallas_call(
        wait_body, name=f"gather_wait_{g}",
        in_specs=[HBM] * (2 * m) + [SEM, SEM, ANY], out_specs=[HBM] * (2 * m),
        out_shape=[pltpu.HBM(a.shape, a.dtype) for a in locs + lands],
        input_output_aliases={i: i for i in range(2 * m)},
        compiler_params=pltpu.CompilerParams(has_side_effects=EFFECT),
    )(*locs, *lands, send_sem, recv_sem, after)
    locs, lands = outs[:m], outs[m:]

    def forward_body(*refs):
        loc, full = refs[:m], refs[2 * m:3 * m]
        send_sems, recv_sems = refs[3 * m:]
        x, y, c, chips = _place()
        s = 2 * x + y
        sib = (x, y, 1 - c)

        def remote(a, k, src, dst):
            return pltpu.make_async_remote_copy(src_ref=src, dst_ref=dst, send_sem=send_sems.at[4 * a + k],
                                                recv_sem=recv_sems.at[4 * a + k], device_id=sib, device_id_type=MESH)

        sends = []
        for a in range(m):
            for j, (px, py) in enumerate(chips):
                theirs = _blk(kinds[a], full[a], 2 * px + py, c)
                sends.append(remote(a, j, theirs, theirs))
            sends.append(remote(a, 3, loc[a], _shard(kinds[a], full[a], s)))
        for cp in sends:
            cp.start()
        for a in range(m):
            for j, (px, py) in enumerate(chips):
                from_sib = _blk(kinds[a], full[a], 2 * px + py, 1 - c)
                remote(a, j, from_sib, from_sib).wait_recv()
            own = _shard(kinds[a], full[a], s)
            remote(a, 3, own, own).wait_recv()
        for cp in sends:
            cp.wait_send()

    outs = pl.pallas_call(
        forward_body, name=f"gather_forward_{g}",
        in_specs=[ANY] * (2 * m), out_specs=[ANY] * m,
        out_shape=[jax.ShapeDtypeStruct(a.shape, a.dtype) for a in lands],
        input_output_aliases={m + i: i for i in range(m)},
        scratch_shapes=[pltpu.SemaphoreType.DMA((4 * m,)), pltpu.SemaphoreType.DMA((4 * m,))],
    )(*locs, *lands)
    return {nm: _natural(k, o) for nm, k, o in zip(names, kinds, outs)}


def _grad_view(kind, g):
    if kind == "col":
        return g.reshape(2, g.shape[0] // 2, g.shape[1])
    if kind == "row":
        return g.reshape(N_CHIPS, 2, g.shape[0] // (2 * N_CHIPS), g.shape[1])
    return g.reshape(GROUPS, N_CHIPS, 2, g.shape[1] // (2 * N_CHIPS), g.shape[2])


def _pair_exchange(tag, names, views):
    kinds = [KIND[nm] for nm in names]
    n = len(names)

    def other_half(kind, ref, c):
        if kind == "col":
            return ref.at[c]
        if kind == "row":
            return ref.at[:, c]
        return ref.at[:, :, c]

    def body(*refs):
        g, got = refs[:n], refs[n:2 * n]
        send_sems, recv_sems = refs[2 * n:]
        x, y, c, _ = _place()
        sib = (x, y, 1 - c)
        cps = []
        for a in range(n):
            cp = pltpu.make_async_remote_copy(src_ref=other_half(kinds[a], g[a], 1 - c), dst_ref=got[a], send_sem=send_sems.at[a],
                                              recv_sem=recv_sems.at[a], device_id=sib, device_id_type=MESH)
            cp.start()
            cps.append(cp)
        for cp in cps:
            cp.wait()

    def got_shape(kind, v):
        if kind == "col":
            return v.shape[1:]
        if kind == "row":
            return (v.shape[0],) + v.shape[2:]
        return v.shape[:2] + v.shape[3:]

    outs = pl.pallas_call(
        body, name=f"grad_pair_exchange_{tag}",
        in_specs=[ANY] * n, out_specs=[ANY] * n,
        out_shape=[jax.ShapeDtypeStruct(got_shape(k, views[nm]), BF16) for nm, k in zip(names, kinds)],
        scratch_shapes=[pltpu.SemaphoreType.DMA((n,)), pltpu.SemaphoreType.DMA((n,))],
    )(*[views[nm] for nm in names])
    return dict(zip(names, outs))


def _pair_sum(name, kind, view, got, c_arr):
    if kind == "col":
        _, rows, cols = view.shape
        tr = 128
        grid = (rows // tr,)
        v_spec = pl.BlockSpec((None, tr, cols), lambda i, c: (c[0], i, 0))
        g_spec = pl.BlockSpec((tr, cols), lambda i, c: (i, 0))
    elif kind == "row":
        _, _, rows, cols = view.shape
        grid = (N_CHIPS,)
        v_spec = pl.BlockSpec((None, None, rows, cols), lambda i, c: (i, c[0], 0, 0))
        g_spec = pl.BlockSpec((None, rows, cols), lambda i, c: (i, 0, 0))
    else:
        _, _, _, rows, cols = view.shape
        grid = (GROUPS,)
        v_spec = pl.BlockSpec((None, N_CHIPS, None, rows, cols), lambda i, c: (i, 0, c[0], 0, 0))
        g_spec = pl.BlockSpec((None, N_CHIPS, rows, cols), lambda i, c: (i, 0, 0, 0))

    def body(c_ref, v_ref, g_ref, o_ref):
        o_ref[...] = (v_ref[...].astype(F32) + g_ref[...].astype(F32)).astype(BF16)

    return pl.pallas_call(
        body, name=name,
        grid_spec=pltpu.PrefetchScalarGridSpec(num_scalar_prefetch=1, grid=grid, in_specs=[v_spec, g_spec], out_specs=g_spec),
        out_shape=jax.ShapeDtypeStruct(got.shape, BF16),
        compiler_params=_cparams(("parallel",)),
    )(c_arr, view, got)


def _piece(kind, ref, s):
    if kind == "col":
        cs = ref.shape[1] // N_CHIPS
        return ref.at[:, pl.ds(pl.multiple_of(s * cs, 128), cs)]
    if kind == "row":
        return ref.at[s]
    return ref.at[:, s]


def _piece_shape(kind, shape):
    if kind == "col":
        return (shape[0], shape[1] // N_CHIPS)
    if kind == "row":
        return shape[1:]
    return (shape[0],) + shape[2:]


def _shard_copies(kinds, p, got, send_sems, recv_sems):
    x, y, c, chips = _place()
    return [pltpu.make_async_remote_copy(src_ref=_piece(kinds[a], p[a], 2 * px + py), dst_ref=got[a].at[j],
                                         send_sem=send_sems.at[3 * a + j], recv_sem=recv_sems.at[3 * a + j],
                                         device_id=(px, py, c), device_id_type=MESH)
            for a in range(len(kinds)) for j, (px, py) in enumerate(chips)]


def _shard_exchange_start(g, names, psums):
    kinds = [KIND[nm] for nm in names]
    n = len(names)
    srcs = [_hbm(psums[nm]) for nm in names]
    lands = [_hbm(lax.empty((3,) + _piece_shape(k, psums[nm].shape), BF16)) for nm, k in zip(names, kinds)]

    def body(*refs):
        p, got = refs[:n], refs[n:2 * n]
        send_sems, recv_sems = refs[2 * n], refs[2 * n + 1]
        token = refs[-1]
        for cp in _shard_copies(kinds, p, got, send_sems, recv_sems):
            cp.start()
        token[...] = jnp.zeros(token.shape, F32)

    sem_shape = pltpu.SemaphoreType.DMA((3 * n,))
    outs = pl.pallas_call(
        body, name=f"grad_shard_exchange_start_{g}",
        in_specs=[HBM] * (2 * n),
        out_specs=[SEM, SEM] + [HBM] * (2 * n) + [pl.BlockSpec(memory_space=pltpu.VMEM)],
        out_shape=[sem_shape, sem_shape] + [pltpu.HBM(a.shape, a.dtype) for a in srcs + lands] + [jax.ShapeDtypeStruct((8, 128), F32)],
        input_output_aliases={i: 2 + i for i in range(2 * n)},
        compiler_params=pltpu.CompilerParams(has_side_effects=EFFECT),
    )(*srcs, *lands)
    return (outs[0], outs[1], list(outs[2:2 + n]), list(outs[2 + n:2 + 2 * n])), outs[-1]


def _shard_exchange_wait(g, names, state, after):
    kinds = [KIND[nm] for nm in names]
    n = len(names)
    send_sem, recv_sem, srcs, lands = state

    def body(*refs):
        p, got = refs[:n], refs[n:2 * n]
        for cp in _shard_copies(kinds, p, got, refs[2 * n], refs[2 * n + 1]):
            cp.wait_send()
            cp.wait_recv()

    outs = pl.pallas_call(
        body, name=f"grad_shard_exchange_wait_{g}",
        in_specs=[HBM] * (2 * n) + [SEM, SEM, ANY], out_specs=[HBM] * (2 * n),
        out_shape=[pltpu.HBM(a.shape, a.dtype) for a in srcs + lands],
        input_output_aliases={i: i for i in range(2 * n)},
        compiler_params=pltpu.CompilerParams(has_side_effects=EFFECT),
    )(*srcs, *lands, send_sem, recv_sem, after)
    return dict(zip(names, outs[:n])), dict(zip(names, outs[n:]))


def _shard_sum(name, kind, psum, got, sc_arr):
    if kind == "col":
        rows, cols = psum.shape
        cs = cols // N_CHIPS
        tr = 128
        grid = (rows // tr,)
        p_spec = pl.BlockSpec((tr, cs), lambda i, sc: (i, sc[0]))
        g_spec = pl.BlockSpec((3, tr, cs), lambda i, sc: (0, i, 0))
        o_spec = pl.BlockSpec((None, tr, cs), lambda i, sc: (sc[1], i, 0))
        out_shape = (2, rows, cs)
    elif kind == "row":
        _, rows, cols = psum.shape
        grid = (1,)
        p_spec = pl.BlockSpec((None, rows, cols), lambda i, sc: (sc[0], 0, 0))
        g_spec = pl.BlockSpec((3, rows, cols), lambda i, sc: (0, 0, 0))
        o_spec = pl.BlockSpec((None, rows, cols), lambda i, sc: (sc[1], 0, 0))
        out_shape = (2, rows, cols)
    else:
        _, _, rows, cols = psum.shape
        grid = (1,)
        p_spec = pl.BlockSpec((GROUPS, None, rows, cols), lambda i, sc: (0, sc[0], 0, 0))
        g_spec = pl.BlockSpec((3, GROUPS, rows, cols), lambda i, sc: (0, 0, 0, 0))
        o_spec = pl.BlockSpec((GROUPS, None, rows, cols), lambda i, sc: (0, sc[1], 0, 0))
        out_shape = (GROUPS, 2, rows, cols)

    def body(sc_ref, p_ref, g_ref, o_ref):
        o_ref[...] = ((p_ref[...].astype(F32) + g_ref[0].astype(F32)) + g_ref[1].astype(F32)) + g_ref[2].astype(F32)

    return pl.pallas_call(
        body, name=name,
        grid_spec=pltpu.PrefetchScalarGridSpec(num_scalar_prefetch=1, grid=grid, in_specs=[p_spec, g_spec], out_specs=o_spec),
        out_shape=jax.ShapeDtypeStruct(out_shape, F32),
        compiler_params=_cparams(("parallel",)),
    )(sc_arr, psum, got)


def _half_exchange(tag, names, bufs):
    kinds = [KIND[nm] for nm in names]
    n = len(names)

    def body(*refs):
        out = refs[n:2 * n]
        send_sems, recv_sems = refs[2 * n:]
        x, y, c, _ = _place()
        sib = (x, y, 1 - c)
        cps = []
        for a in range(n):
            mine = _half(kinds[a], out[a], c)
            cp = pltpu.make_async_remote_copy(src_ref=mine, dst_ref=mine, send_sem=send_sems.at[a], recv_sem=recv_sems.at[a],
                                              device_id=sib, device_id_type=MESH)
            cp.start()
            cps.append(cp)
        for a, cp in enumerate(cps):
            cp.wait_send()
            theirs = _half(kinds[a], out[a], 1 - c)
            pltpu.make_async_remote_copy(src_ref=theirs, dst_ref=theirs, send_sem=send_sems.at[a], recv_sem=recv_sems.at[a],
                                         device_id=sib, device_id_type=MESH).wait_recv()

    outs = pl.pallas_call(
        body, name=f"grad_half_exchange_{tag}",
        in_specs=[ANY] * n, out_specs=[ANY] * n,
        out_shape=[jax.ShapeDtypeStruct(bufs[nm].shape, F32) for nm in names],
        input_output_aliases={a: a for a in range(n)},
        scratch_shapes=[pltpu.SemaphoreType.DMA((n,)), pltpu.SemaphoreType.DMA((n,))],
    )(*[bufs[nm] for nm in names])
    return dict(zip(names, outs))


N_DEV = 8
SMALL_ROWS = 8


def _all_reduce_small(name, v):
    def body(v_ref, o_ref, buf, send_sems, recv_sems):
        x, y, c, _ = _place()
        me = 4 * x + 2 * y + c
        buf[me] = v_ref[...]
        cps = []
        for r in range(1, N_DEV):
            to = (x ^ (r >> 2), y ^ ((r >> 1) & 1), c ^ (r & 1))
            cp = pltpu.make_async_remote_copy(src_ref=v_ref, dst_ref=buf.at[me], send_sem=send_sems.at[r - 1],
                                              recv_sem=recv_sems.at[r - 1], device_id=to, device_id_type=MESH)
            cp.start()
            cps.append(cp)
        for r in range(1, N_DEV):
            pltpu.make_async_remote_copy(src_ref=v_ref, dst_ref=buf.at[me ^ r], send_sem=send_sems.at[r - 1],
                                         recv_sem=recv_sems.at[r - 1], device_id=(x, y, c), device_id_type=MESH).wait_recv()
        for cp in cps:
            cp.wait_send()
        acc = buf[0]
        for d in range(1, N_DEV):
            acc = acc + buf[d]
        o_ref[...] = acc

    vm = pl.BlockSpec(memory_space=pltpu.VMEM)
    return pl.pallas_call(
        body, name=name, in_specs=[vm], out_specs=vm,
        out_shape=jax.ShapeDtypeStruct((SMALL_ROWS, D_MODEL), F32),
        scratch_shapes=[pltpu.VMEM((N_DEV, SMALL_ROWS, D_MODEL), F32), pltpu.SemaphoreType.DMA((N_DEV - 1,)),
                        pltpu.SemaphoreType.DMA((N_DEV - 1,))],
    )(v)


def _adamw(name, w, g, m, v):
    rows, cols = w.shape
    tr = next((c for c in (256, 176, 128, 64, 32, 8) if rows % c == 0), rows)
    spec = pl.BlockSpec((tr, cols), lambda i: (i, 0))

    def body(w_ref, g_ref, m_ref, v_ref, d_ref, mo_ref, vo_ref):
        gv = g_ref[...]
        m_new = ADAM_B1 * m_ref[...] + (1.0 - ADAM_B1) * gv
        v_new = ADAM_B2 * v_ref[...] + (1.0 - ADAM_B2) * jnp.square(gv)
        m_hat = m_new / (1.0 - ADAM_B1 ** ADAM_STEP)
        v_hat = v_new / (1.0 - ADAM_B2 ** ADAM_STEP)
        d_ref[...] = -ADAM_LR * (m_hat / (jnp.sqrt(v_hat) + ADAM_EPS) + ADAM_WD * w_ref[...])
        mo_ref[...] = m_new
        vo_ref[...] = v_new

    return pl.pallas_call(
        body, name=name, grid=(rows // tr,),
        in_specs=[spec] * 4, out_specs=[spec] * 3,
        out_shape=[jax.ShapeDtypeStruct((rows, cols), F32)] * 3,
        compiler_params=_cparams(("parallel",)),
    )(w, g, m, v)


WEIGHTS = ("norm_ffn1", "ffn1_w_in", "ffn1_w_out", "norm_mix", "w_in", "gate_bias", "pool_w", "pool_scale", "w_ret_up",
           "w_pool_up", "w_out", "norm_ffn2", "ffn2_w_in", "ffn2_w_out", "norm_final")
SMALL_ROW = dict(norm_ffn1=0, norm_mix=1, gate_bias=2, pool_scale=4, norm_ffn2=5, norm_final=6)


def _as2d(a):
    return a.reshape(-1, a.shape[-1])


def kernel(x, norm_ffn1, ffn1_w_in, ffn1_w_out, norm_mix, w_in, gate_bias, pool_w, pool_scale, w_ret_up, w_pool_up, w_out, norm_ffn2, ffn2_w_in, ffn2_w_out, norm_final, loss_target, m_norm_ffn1, m_ffn1_w_in, m_ffn1_w_out, m_norm_mix, m_w_in, m_gate_bias, m_pool_w, m_pool_scale, m_w_ret_up, m_w_pool_up, m_w_out, m_norm_ffn2, m_ffn2_w_in, m_ffn2_w_out, m_norm_final, v_norm_ffn1, v_ffn1_w_in, v_ffn1_w_out, v_norm_mix, v_w_in, v_gate_bias, v_pool_w, v_pool_scale, v_w_ret_up, v_w_pool_up, v_w_out, v_norm_ffn2, v_ffn2_w_in, v_ffn2_w_out, v_norm_final):
    wt = dict(norm_ffn1=norm_ffn1, ffn1_w_in=ffn1_w_in, ffn1_w_out=ffn1_w_out, norm_mix=norm_mix, w_in=w_in, gate_bias=gate_bias,
              pool_w=pool_w, pool_scale=pool_scale, w_ret_up=w_ret_up, w_pool_up=w_pool_up, w_out=w_out, norm_ffn2=norm_ffn2,
              ffn2_w_in=ffn2_w_in, ffn2_w_out=ffn2_w_out, norm_final=norm_final)
    mom = dict(norm_ffn1=m_norm_ffn1, ffn1_w_in=m_ffn1_w_in, ffn1_w_out=m_ffn1_w_out, norm_mix=m_norm_mix, w_in=m_w_in,
               gate_bias=m_gate_bias, pool_w=m_pool_w, pool_scale=m_pool_scale, w_ret_up=m_w_ret_up, w_pool_up=m_w_pool_up,
               w_out=m_w_out, norm_ffn2=m_norm_ffn2, ffn2_w_in=m_ffn2_w_in, ffn2_w_out=m_ffn2_w_out, norm_final=m_norm_final)
    var = dict(norm_ffn1=v_norm_ffn1, ffn1_w_in=v_ffn1_w_in, ffn1_w_out=v_ffn1_w_out, norm_mix=v_norm_mix, w_in=v_w_in,
               gate_bias=v_gate_bias, pool_w=v_pool_w, pool_scale=v_pool_scale, w_ret_up=v_w_ret_up, w_pool_up=v_w_pool_up,
               w_out=v_w_out, norm_ffn2=v_norm_ffn2, ffn2_w_in=v_ffn2_w_in, ffn2_w_out=v_ffn2_w_out, norm_final=v_norm_final)

    ax, ay, ac = lax.axis_index("x"), lax.axis_index("y"), lax.axis_index("c")
    chip = 2 * ax + ay
    c_arr = jnp.reshape(ac, (1,)).astype(jnp.int32)
    sc_arr = jnp.stack([chip, ac]).astype(jnp.int32)
    bias_cols = gate_bias.shape[-1]

    shards = {nm: wt[nm][0].astype(BF16) for nm in BIG}
    gather_groups, gather_token = _gather_start(shards)
    placed = lax.dynamic_update_slice(jnp.zeros((SMALL_ROWS, D_MODEL), F32), gate_bias[0], (0, chip * bias_cols))
    bias_full = _all_reduce_small("gather_gate_bias", jnp.where(ac == 0, placed, 0.0))[:2]
    vec = dict(norm_ffn1=norm_ffn1, norm_mix=norm_mix, norm_ffn2=norm_ffn2, pool_scale=pool_scale,
               norm_final=norm_final.reshape(1, D_MODEL), gate_bias=bias_full)

    def get_w(g, after):
        return _gather_finish(g, gather_groups[g], gather_token if after is None else after)

    pending = []

    def on_grads(gr):
        g = len(pending)
        names = GRAD_GROUPS[g]
        assert set(names) == set(gr), (names, list(gr))
        views = {nm: _grad_view(KIND[nm], gr[nm]) for nm in names}
        from_sib = _pair_exchange(g, names, views)
        psums = {nm: _pair_sum(f"pair_sum_{nm}", KIND[nm], views[nm], from_sib[nm], c_arr) for nm in names}
        state, token = _shard_exchange_start(g, names, psums)
        pending.append(state)
        return token[0:1, 0:1]

    loss_local, dx, small = _local_step(x[0], loss_target[0], vec, get_w, on_grads)

    packed = jnp.concatenate([small["norm_ffn1"], small["norm_mix"], small["gate_bias"], small["pool_scale"],
                              small["norm_ffn2"], small["norm_final"], jnp.broadcast_to(loss_local, (1, D_MODEL))], axis=0)
    small_sum = _all_reduce_small("reduce_small_grads", packed)
    loss = small_sum[SMALL_ROWS - 1, 0]
    grads, delta, new_m, new_v = {}, {}, {}, {}

    def adamw(nm):
        shape = wt[nm].shape
        d, m2, v2 = _adamw(f"adamw_{nm}", _as2d(wt[nm]), _as2d(grads[nm]), _as2d(mom[nm]), _as2d(var[nm]))
        delta[nm], new_m[nm], new_v[nm] = d.reshape(shape), m2.reshape(shape), v2.reshape(shape)
        return d

    for nm in ("norm_ffn1", "norm_mix", "pool_scale", "norm_ffn2"):
        grads[nm] = small_sum[SMALL_ROW[nm]][None, :]
    grads["norm_final"] = small_sum[SMALL_ROW["norm_final"]]
    grads["gate_bias"] = lax.dynamic_slice(small_sum, (SMALL_ROW["gate_bias"], chip * bias_cols), (2, bias_cols))[None]
    after = dx
    for g, names in enumerate(GRAD_GROUPS):
        psums, from_chips = _shard_exchange_wait(g, names, pending[g], after)
        bufs = {nm: _shard_sum(f"shard_sum_{nm}", KIND[nm], psums[nm], from_chips[nm], sc_arr) for nm in names}
        reduced = _half_exchange(g, names, bufs)
        for nm in names:
            grads[nm] = reduced[nm].reshape(wt[nm].shape)
            after = adamw(nm)
    for nm in WEIGHTS:
        if nm not in delta:
            adamw(nm)

    return (loss, dx[None], *[grads[nm] for nm in WEIGHTS], *[delta[nm] for nm in WEIGHTS],
            *[new_m[nm] for nm in WEIGHTS], *[new_v[nm] for nm in WEIGHTS])
```

```python
import functools

import numpy as np
import jax
import jax.numpy as jnp
from jax import lax
from jax.experimental import pallas as pl
from jax.experimental.pallas import tpu as pltpu

F32 = jnp.float32
BF16 = jnp.bfloat16
MESH = pl.DeviceIdType.MESH

D_MODEL = 1024
D_FF = 2816
HEADS = 4
HEAD_DIM = 256
GROUPS = 4
GROUP_DIM = 256
POOL_WINDOWS = (2, 4, 8, 16)
IN_WIDTH = 7 * D_MODEL
ROPE_BASE = 10000.0
NORM_EPS = 1e-6
FFN_RES_WEIGHT = 0.5
ADAM_LR, ADAM_B1, ADAM_B2, ADAM_EPS, ADAM_WD, ADAM_STEP = 0.001, 0.9, 0.999, 1e-08, 0.01, 10

N_CHIPS = 4
RET_BLOCK = 256
V7X_VMEM_LIMIT = 48 * 1024 * 1024


def _cparams(sem):
    return pltpu.CompilerParams(dimension_semantics=sem, vmem_limit_bytes=V7X_VMEM_LIMIT)


def _sigmoid(x):
    return jax.nn.sigmoid(x)


_DIMS = {"nn": (((1,), (0,)), ((), ())), "nt": (((1,), (1,)), ((), ())), "tn": (((0,), (0,)), ((), ()))}


def _matmul(name, a, b, mode, m, n, k, tm, tn, tk, out_dtypes, a_spec=None, b_spec=None, extras=(), consts=(), epilogue=None,
            resident=None, n_outer=False):
    tm, tn, tk = min(tm, m), min(tn, n), min(tk, k)
    gi, gj, gk = m // tm, n // tn, k // tk
    assert gi * tm == m and gj * tn == n and gk * tk == k, (name, m, n, k, tm, tn, tk)
    assert not (n_outer and (a_spec is not None or b_spec is not None)), name
    once = dict(pipeline_mode=pl.Buffered(1))

    def spec(shape, index, **kw):
        return pl.BlockSpec(shape, (lambda j, i, kk: index(i, j, kk)) if n_outer else index, **kw)

    if a_spec is None:
        kw = once if resident == "a" else {}
        a_spec = (spec((tk, tm), lambda i, j, kk: (kk, i), **kw) if mode == "tn"
                  else spec((tm, tk), lambda i, j, kk: (i, kk), **kw))
    if b_spec is None:
        kw = once if resident == "b" else {}
        b_spec = (spec((tn, tk), lambda i, j, kk: (j, kk), **kw) if mode == "nt"
                  else spec((tk, tn), lambda i, j, kk: (kk, j), **kw))
    n_ex, n_out = len(extras) + len(consts), len(out_dtypes)
    dims = _DIMS[mode]

    def body(a_ref, b_ref, *rest):
        ex_refs, out_refs = rest[:n_ex], rest[n_ex:n_ex + n_out]

        def finish(acc):
            outs = (acc,) if epilogue is None else epilogue(acc, *[e[...] for e in ex_refs])
            for o_ref, o in zip(out_refs, outs):
                o_ref[...] = o.astype(o_ref.dtype)

        prod = lax.dot_general(a_ref[...], b_ref[...], dims, preferred_element_type=F32)
        if gk == 1:
            finish(prod)
        else:
            acc_ref = rest[n_ex + n_out]
            kk = pl.program_id(2)

            @pl.when(kk == 0)
            def _():
                acc_ref[...] = prod

            @pl.when(kk > 0)
            def _():
                acc_ref[...] += prod

            @pl.when(kk == gk - 1)
            def _():
                finish(acc_ref[...])

    o_spec = spec((tm, tn), lambda i, j, kk: (i, j))
    outs = pl.pallas_call(
        body, name=name, grid=(gj, gi, gk) if n_outer else (gi, gj, gk),
        in_specs=[a_spec, b_spec] + [o_spec] * len(extras) + [spec((1, tn), lambda i, j, kk: (0, j))] * len(consts),
        out_specs=[o_spec] * n_out,
        out_shape=[jax.ShapeDtypeStruct((m, n), dt) for dt in out_dtypes],
        scratch_shapes=[pltpu.VMEM((tm, tn), F32)] if gk > 1 else [],
        compiler_params=_cparams(("parallel", "parallel", "arbitrary")),
    )(a, b, *extras, *consts)
    return outs[0] if n_out == 1 else outs


def _row_spec(tm, width, col_block=0):
    return pl.BlockSpec((tm, width), lambda i: (i, col_block))


def _full_spec(shape):
    return pl.BlockSpec(shape, lambda *_: (0,) * len(shape))


def _rmsnorm_fwd(name, h, g, tm=512):
    t = h.shape[0]

    def body(h_ref, g_ref, o_ref):
        x = h_ref[...]
        r = lax.rsqrt(jnp.mean(x * x, axis=-1, keepdims=True) + NORM_EPS)
        o_ref[...] = (x * r * g_ref[...]).astype(BF16)

    return pl.pallas_call(
        body, name=name, grid=(t // tm,),
        in_specs=[_row_spec(tm, D_MODEL), _full_spec((1, D_MODEL))],
        out_specs=_row_spec(tm, D_MODEL),
        out_shape=jax.ShapeDtypeStruct((t, D_MODEL), BF16),
        compiler_params=_cparams(("parallel",)),
    )(h, g)


def _proj_norm_bwd(name, a, a_spec, parts, w, h, g, dres, tm):
    t = h.shape[0]

    def body(a_ref, w_ref, h_ref, g_ref, dres_ref, dh_ref, dhb_ref, dg_ref):
        i = pl.program_id(0)
        dn_v = None
        for lead, k0, k1 in parts:
            term = _dot(a_ref[...] if lead is None else a_ref[lead], w_ref[:, k0:k1], "nt")
            dn_v = term if dn_v is None else dn_v + term
        x = h_ref[...]
        r = lax.rsqrt(jnp.mean(x * x, axis=-1, keepdims=True) + NORM_EPS)
        xh = x * r
        dxh = dn_v * g_ref[...]
        dh = dres_ref[...] + r * (dxh - xh * jnp.mean(dxh * xh, axis=-1, keepdims=True))
        dh_ref[...] = dh
        dhb_ref[...] = dh.astype(BF16)
        part = jnp.sum(dn_v * xh, axis=0, keepdims=True)

        @pl.when(i == 0)
        def _():
            dg_ref[...] = part

        @pl.when(i > 0)
        def _():
            dg_ref[...] += part

    row = _row_spec(tm, D_MODEL)
    return pl.pallas_call(
        body, name=name, grid=(t // tm,),
        in_specs=[a_spec, pl.BlockSpec(w.shape, lambda i: (0, 0), pipeline_mode=pl.Buffered(1)), row, _full_spec((1, D_MODEL)), row],
        out_specs=[row, row, _full_spec((1, D_MODEL))],
        out_shape=[jax.ShapeDtypeStruct((t, D_MODEL), F32), jax.ShapeDtypeStruct((t, D_MODEL), BF16),
                   jax.ShapeDtypeStruct((1, D_MODEL), F32)],
        compiler_params=_cparams(("arbitrary",)),
    )(a, w, h, g, dres)


def _loss_and_grad(name, h, g, target, tm=512):
    t = h.shape[0]

    def body(h_ref, g_ref, t_ref, dh_ref, dhb_ref, dg_ref, loss_ref):
        i = pl.program_id(0)
        x = h_ref[...]
        gv = g_ref[...]
        r = lax.rsqrt(jnp.mean(x * x, axis=-1, keepdims=True) + NORM_EPS)
        xh = x * r
        err = xh * gv - t_ref[...]
        row = jnp.mean(err * err, axis=-1, keepdims=True)
        part_loss = 0.5 * jnp.sum(row, axis=0, keepdims=True)
        dy = err * (1.0 / D_MODEL)
        dxh = dy * gv
        dh = r * (dxh - xh * jnp.mean(dxh * xh, axis=-1, keepdims=True))
        dh_ref[...] = dh
        dhb_ref[...] = dh.astype(BF16)
        part = jnp.sum(dy * xh, axis=0, keepdims=True)

        @pl.when(i == 0)
        def _():
            dg_ref[...] = part
            loss_ref[...] = jnp.zeros(loss_ref.shape, F32) + part_loss

        @pl.when(i > 0)
        def _():
            dg_ref[...] += part
            loss_ref[...] += part_loss

    return pl.pallas_call(
        body, name=name, grid=(t // tm,),
        in_specs=[_row_spec(tm, D_MODEL), _full_spec((1, D_MODEL)), _row_spec(tm, D_MODEL)],
        out_specs=[_row_spec(tm, D_MODEL), _row_spec(tm, D_MODEL), _full_spec((1, D_MODEL)), _full_spec((8, 128))],
        out_shape=[jax.ShapeDtypeStruct((t, D_MODEL), F32), jax.ShapeDtypeStruct((t, D_MODEL), BF16),
                   jax.ShapeDtypeStruct((1, D_MODEL), F32), jax.ShapeDtypeStruct((8, 128), F32)],
        compiler_params=_cparams(("arbitrary",)),
    )(h, g, target)


def _rope_tables(t):
    half = HEAD_DIM // 2
    inv_freq = ROPE_BASE ** (-jnp.arange(half, dtype=F32) / half)
    ang = jnp.arange(t, dtype=F32)[:, None] * inv_freq[None, :]
    return jnp.cos(ang), jnp.sin(ang)


def _rotary_fwd(name, proj, cos, sin, tm=512):
    t = proj.shape[0]
    half = HEAD_DIM // 2
    k_scale = HEAD_DIM ** -0.5

    def body(q_ref, k_ref, v_ref, c_ref, s_ref, qo_ref, ko_ref, vo_ref):
        c, s = c_ref[...], s_ref[...]
        for hh in range(HEADS):
            lo, mid, hi = hh * HEAD_DIM, hh * HEAD_DIM + half, (hh + 1) * HEAD_DIM
            x1, x2 = q_ref[:, lo:mid], q_ref[:, mid:hi]
            qo_ref[:, lo:mid] = x1 * c - x2 * s
            qo_ref[:, mid:hi] = x1 * s + x2 * c
            x1, x2 = k_ref[:, lo:mid], k_ref[:, mid:hi]
            ko_ref[:, lo:mid] = (x1 * c - x2 * s) * k_scale
            ko_ref[:, mid:hi] = (x1 * s + x2 * c) * k_scale
        vo_ref[...] = v_ref[...].astype(BF16)

    return pl.pallas_call(
        body, name=name, grid=(t // tm,),
        in_specs=[_row_spec(tm, D_MODEL, 0), _row_spec(tm, D_MODEL, 1), _row_spec(tm, D_MODEL, 2),
                  _row_spec(tm, half), _row_spec(tm, half)],
        out_specs=[_row_spec(tm, D_MODEL)] * 3,
        out_shape=[jax.ShapeDtypeStruct((t, D_MODEL), F32), jax.ShapeDtypeStruct((t, D_MODEL), F32),
                   jax.ShapeDtypeStruct((t, D_MODEL), BF16)],
        compiler_params=_cparams(("parallel",)),
    )(proj, proj, proj, cos, sin)


def _rotary_bwd(name, dqr, dkr, cos, sin, tm=512):
    t = dqr.shape[0]
    half = HEAD_DIM // 2
    k_scale = HEAD_DIM ** -0.5

    def body(q_ref, k_ref, c_ref, s_ref, qo_ref, ko_ref):
        c, s = c_ref[...], s_ref[...]
        for hh in range(HEADS):
            lo, mid, hi = hh * HEAD_DIM, hh * HEAD_DIM + half, (hh + 1) * HEAD_DIM
            y1, y2 = q_ref[:, lo:mid], q_ref[:, mid:hi]
            qo_ref[:, lo:mid] = (y1 * c + y2 * s).astype(BF16)
            qo_ref[:, mid:hi] = (y2 * c - y1 * s).astype(BF16)
            y1, y2 = k_ref[:, lo:mid], k_ref[:, mid:hi]
            ko_ref[:, lo:mid] = ((y1 * c + y2 * s) * k_scale).astype(BF16)
            ko_ref[:, mid:hi] = ((y2 * c - y1 * s) * k_scale).astype(BF16)

    return pl.pallas_call(
        body, name=name, grid=(t // tm,),
        in_specs=[_row_spec(tm, D_MODEL), _row_spec(tm, D_MODEL), _row_spec(tm, half), _row_spec(tm, half)],
        out_specs=[_row_spec(tm, D_MODEL)] * 2,
        out_shape=[jax.ShapeDtypeStruct((t, D_MODEL), BF16)] * 2,
        compiler_params=_cparams(("parallel",)),
    )(dqr, dkr, cos, sin)


def _retention_tables():
    b, chunk = RET_BLOCK, 64
    gamma = 1.0 - 2.0 ** (-5.0 - np.arange(HEADS, dtype=np.float64))
    log_g = np.log(gamma)[:, None, None]
    i = np.arange(b)[:, None]
    j = np.arange(b)[None, :]
    same = (i // chunk) == (j // chunk)
    earlier = (j // chunk) < (i // chunk)
    expo = np.where(same, np.abs(i - j), np.where(earlier, i - j, 0)).astype(np.float64)
    mask = np.where(same | earlier, 1.0, 0.0)
    dmat = np.exp(log_g * expo[None]) * mask[None]
    qd = np.exp(log_g[:, :, 0] * (np.arange(b)[None, :] + 1.0))
    kd = np.exp(log_g[:, :, 0] * (b - 1.0 - np.arange(b)[None, :]))
    cd = np.exp(log_g[:, :, 0] * b) * np.ones((1, HEAD_DIM))
    as32 = lambda v: jnp.asarray(v.astype(np.float32))
    return (as32(dmat), as32(np.swapaxes(dmat, 1, 2)), as32(qd[:, :, None]), as32(kd[:, :, None]), as32(cd[:, None, :]))


def _dot(a, b, mode="nn"):
    return lax.dot_general(a, b, _DIMS[mode], preferred_element_type=F32)


GRET_BLOCK = 3


def _head_specs(nb, rev=False):
    pos = (lambda n: nb - 1 - n) if rev else (lambda n: n)
    tok = pl.BlockSpec((RET_BLOCK, D_MODEL), lambda n: (pos(n), 0))
    gr = pl.BlockSpec((RET_BLOCK, D_MODEL), lambda n: (pos(n), GRET_BLOCK))
    tab = _full_spec((HEADS, RET_BLOCK, RET_BLOCK))
    col = _full_spec((HEADS, RET_BLOCK, 1))
    rowv = _full_spec((HEADS, 1, HEAD_DIM))
    st = pl.BlockSpec((HEADS, None, HEAD_DIM, HEAD_DIM), lambda n: (0, pos(n), 0, 0))
    return tok, gr, tab, col, rowv, st


def _retention_fwd(name, qr, kr, vb, proj, tables):
    t = qr.shape[0]
    nb = t // RET_BLOCK
    dmat, _, qd, kd, cd = tables
    tok, gr, tab, col, rowv, st = _head_specs(nb)

    def body(q_ref, k_ref, v_ref, g_ref, d_ref, qd_ref, kd_ref, cd_ref, o_ref, ret_ref, st_ref, state):
        n = pl.program_id(0)

        @pl.when(n == 0)
        def _():
            state[...] = jnp.zeros(state.shape, F32)

        for hh in range(HEADS):
            sl = slice(hh * HEAD_DIM, (hh + 1) * HEAD_DIM)
            q, k, v = q_ref[:, sl], k_ref[:, sl], v_ref[:, sl]
            s = _dot(q.astype(BF16), k.astype(BF16), "nt") * d_ref[hh]
            stb = state[hh].astype(BF16)
            st_ref[hh] = stb
            o = _dot(s.astype(BF16), v) + _dot((q * qd_ref[hh]).astype(BF16), stb)
            o_ref[:, sl] = o
            rn = o * lax.rsqrt(jnp.mean(o * o, axis=-1, keepdims=True) + NORM_EPS)
            g = g_ref[:, sl]
            ret_ref[:, sl] = (rn * (g * _sigmoid(g))).astype(BF16)
            state[hh] = state[hh] * cd_ref[hh] + _dot((k * kd_ref[hh]).astype(BF16), v, "tn")

    return pl.pallas_call(
        body, name=name, grid=(nb,),
        in_specs=[tok, tok, tok, gr, tab, col, col, rowv],
        out_specs=[tok, tok, st],
        out_shape=[jax.ShapeDtypeStruct((t, D_MODEL), F32), jax.ShapeDtypeStruct((t, D_MODEL), BF16),
                   jax.ShapeDtypeStruct((HEADS, nb, HEAD_DIM, HEAD_DIM), BF16)],
        scratch_shapes=[pltpu.VMEM((HEADS, HEAD_DIM, HEAD_DIM), F32)],
        compiler_params=_cparams(("arbitrary",)),
    )(qr, kr, vb, proj, dmat, qd, kd, cd)


def _retention_bwd(name, dret, o, qr, kr, vb, proj, states, tables):
    t = qr.shape[0]
    nb = t // RET_BLOCK
    dmat, dmat_t, qd, kd, cd = tables
    tok, gr, tab, col, rowv, st = _head_specs(nb, rev=True)

    def body(dr_ref, o_ref, q_ref, k_ref, v_ref, g_ref, st_ref, d_ref, dt_ref, qd_ref, kd_ref, cd_ref,
             dq_ref, dk_ref, dv_ref, dg_ref, gstate):
        n = pl.program_id(0)

        @pl.when(n == 0)
        def _():
            gstate[...] = jnp.zeros(gstate.shape, F32)

        for hh in range(HEADS):
            sl = slice(hh * HEAD_DIM, (hh + 1) * HEAD_DIM)
            o_v, g, dr = o_ref[:, sl], g_ref[:, sl], dr_ref[:, sl]
            sg = _sigmoid(g)
            r = lax.rsqrt(jnp.mean(o_v * o_v, axis=-1, keepdims=True) + NORM_EPS)
            rn = o_v * r
            d_rn = dr * (g * sg)
            dg_ref[:, sl] = (dr * rn * (sg * (1.0 + g * (1.0 - sg)))).astype(BF16)
            d_o = r * (d_rn - rn * jnp.mean(d_rn * rn, axis=-1, keepdims=True))
            dob = d_o.astype(BF16)

            q, k, v = q_ref[:, sl], k_ref[:, sl], v_ref[:, sl]
            qb, kb = q.astype(BF16), k.astype(BF16)
            qdv, kdv = qd_ref[hh], kd_ref[hh]
            s_t = (_dot(kb, qb, "nt") * dt_ref[hh]).astype(BF16)
            p_t = (_dot(v, dob, "nt") * dt_ref[hh]).astype(BF16)
            p = (_dot(dob, v, "nt") * d_ref[hh]).astype(BF16)
            stb = st_ref[hh]
            gb = gstate[hh].astype(BF16)
            dq_ref[:, sl] = _dot(p, kb) + _dot(dob, stb, "nt") * qdv
            dk_ref[:, sl] = _dot(p_t, qb) + _dot(v, gb, "nt") * kdv
            dv_ref[:, sl] = (_dot(s_t, dob) + _dot((k * kdv).astype(BF16), gb)).astype(BF16)
            gstate[hh] = gstate[hh] * cd_ref[hh] + _dot((q * qdv).astype(BF16), dob, "tn")

    return pl.pallas_call(
        body, name=name, grid=(nb,),
        in_specs=[tok, tok, tok, tok, tok, gr, st, tab, tab, col, col, rowv],
        out_specs=[tok, tok, tok, tok],
        out_shape=[jax.ShapeDtypeStruct((t, D_MODEL), F32), jax.ShapeDtypeStruct((t, D_MODEL), F32),
                   jax.ShapeDtypeStruct((t, D_MODEL), BF16), jax.ShapeDtypeStruct((t, D_MODEL), BF16)],
        scratch_shapes=[pltpu.VMEM((HEADS, HEAD_DIM, HEAD_DIM), F32)],
        compiler_params=_cparams(("arbitrary",)),
    )(dret, o, qr, kr, vb, proj, states, dmat, dmat_t, qd, kd, cd)


POOL_TILE = 256


def _pool_tables():
    b = POOL_TILE
    tt = np.arange(b)[:, None]
    jj = np.arange(b)[None, :]
    cur, prev = [], []
    for w in POOL_WINDOWS:
        cur.append(((tt - jj >= 0) & (tt - jj <= w - 1)).astype(np.float32))
        prev.append((tt - (jj - b) <= w - 1).astype(np.float32))
    cur, prev = np.stack(cur), np.stack(prev)
    as16 = lambda v: jnp.asarray(v, dtype=BF16)
    return as16(cur), as16(prev), as16(np.swapaxes(cur, 1, 2)), as16(np.swapaxes(prev, 1, 2))


def _split2(x):
    hi = x.astype(BF16)
    return hi, (x - hi.astype(F32)).astype(BF16)


POOL_BLOCK = 4


def _pool_count(n, window):
    tpos = n * POOL_TILE + lax.broadcasted_iota(jnp.int32, (POOL_TILE, 1), 0)
    return jnp.minimum(tpos + 1, window).astype(F32)


def _pool_fwd(name, proj, pool_w, scale, tables):
    t = proj.shape[0]
    nb = t // POOL_TILE
    mc, mp, _, _ = tables
    tab = _full_spec((GROUPS, POOL_TILE, POOL_TILE))
    row = _row_spec(POOL_TILE, D_MODEL)

    def body(pc_ref, pp_ref, mc_ref, mp_ref, w_ref, sc_ref, pm_ref, mix_ref, po_ref):
        n = pl.program_id(0)
        for g, window in enumerate(POOL_WINDOWS):
            sl = slice(g * GROUP_DIM, (g + 1) * GROUP_DIM)
            p = pc_ref[:, sl]
            c_hi, c_lo = _split2(p)
            p_hi, p_lo = _split2(pp_ref[:, sl])
            mcv, mpv = mc_ref[g], mp_ref[g]
            win = _dot(mcv, c_hi) + _dot(mcv, c_lo)
            before = _dot(mpv, p_hi) + _dot(mpv, p_lo)
            win = win + jnp.where(n > 0, before, 0.0)
            pm = (win / _pool_count(n, window) - p).astype(BF16)
            pm_ref[:, sl] = pm
            mixed = _dot(pm, w_ref[g])
            mix_ref[:, sl] = mixed
            po_ref[:, sl] = (mixed * sc_ref[:, sl]).astype(BF16)

    return pl.pallas_call(
        body, name=name, grid=(nb,),
        in_specs=[_row_spec(POOL_TILE, D_MODEL, POOL_BLOCK),
                  pl.BlockSpec((POOL_TILE, D_MODEL), lambda n: (jnp.maximum(n - 1, 0), POOL_BLOCK)),
                  tab, tab, _full_spec((GROUPS, GROUP_DIM, GROUP_DIM)), _full_spec((1, D_MODEL))],
        out_specs=[row] * 3,
        out_shape=[jax.ShapeDtypeStruct((t, D_MODEL), BF16), jax.ShapeDtypeStruct((t, D_MODEL), F32),
                   jax.ShapeDtypeStruct((t, D_MODEL), BF16)],
        compiler_params=_cparams(("parallel",)),
    )(proj, proj, mc, mp, pool_w, scale)


def _pool_bwd(name, dpo, pm, mixed, pool_w, scale, tables):
    t = dpo.shape[0]
    nb = t // POOL_TILE
    _, _, mct, mpt = tables
    cur = _row_spec(POOL_TILE, D_MODEL)
    nxt = pl.BlockSpec((POOL_TILE, D_MODEL), lambda n: (jnp.minimum(n + 1, nb - 1), 0))
    tab = _full_spec((GROUPS, POOL_TILE, POOL_TILE))
    wspec = _full_spec((GROUPS, GROUP_DIM, GROUP_DIM))
    sspec = _full_spec((1, D_MODEL))

    def body(dc_ref, dn_ref, pm_ref, mix_ref, mct_ref, mpt_ref, w_ref, sc_ref, dp_ref, dw_ref, ds_ref):
        n = pl.program_id(0)

        @pl.when(n == 0)
        def _():
            dw_ref[...] = jnp.zeros(dw_ref.shape, F32)
            ds_ref[...] = jnp.zeros(ds_ref.shape, F32)

        for g, window in enumerate(POOL_WINDOWS):
            sl = slice(g * GROUP_DIM, (g + 1) * GROUP_DIM)
            dc, sc, w = dc_ref[:, sl], sc_ref[:, sl], w_ref[g]
            dmix_c = (dc * sc).astype(BF16)
            dmix_n = (dn_ref[:, sl] * sc).astype(BF16)
            dpm_c = _dot(dmix_c, w, "nt")
            dpm_n = _dot(dmix_n, w, "nt")
            e_hi, e_lo = _split2(dpm_c / _pool_count(n, window))
            f_hi, f_lo = _split2(dpm_n / _pool_count(n + 1, window))
            mctv, mptv = mct_ref[g], mpt_ref[g]
            back = _dot(mctv, e_hi) + _dot(mctv, e_lo)
            after = _dot(mptv, f_hi) + _dot(mptv, f_lo)
            dp_ref[:, sl] = (back + jnp.where(n < nb - 1, after, 0.0) - dpm_c).astype(BF16)
            dw_ref[g] += _dot(pm_ref[:, sl], dmix_c, "tn")
            ds_ref[:, sl] += jnp.sum(dc * mix_ref[:, sl], axis=0, keepdims=True)

    return pl.pallas_call(
        body, name=name, grid=(nb,),
        in_specs=[cur, nxt, cur, cur, tab, tab, wspec, sspec],
        out_specs=[cur, wspec, sspec],
        out_shape=[jax.ShapeDtypeStruct((t, D_MODEL), BF16), jax.ShapeDtypeStruct((GROUPS, GROUP_DIM, GROUP_DIM), F32),
                   jax.ShapeDtypeStruct((1, D_MODEL), F32)],
        compiler_params=_cparams(("arbitrary",)),
    )(dpo, dpo, pm, mixed, mct, mpt, pool_w, scale)


GATE0_BLOCK, GATE1_BLOCK = 5, 6


def _merge_fwd(name, ret, po, w_ru, w_pu, proj, bias, tm=512):
    t = ret.shape[0]

    def body(r_ref, p_ref, wr_ref, wp_ref, g0_ref, g1_ref, b_ref, m_ref, ru_ref, pu_ref):
        ru = _dot(r_ref[...], wr_ref[...])
        pu = _dot(p_ref[...], wp_ref[...])
        ru_ref[...] = ru
        pu_ref[...] = pu
        m_ref[...] = (_sigmoid(g0_ref[...] + b_ref[0:1, :]) * ru + _sigmoid(g1_ref[...] + b_ref[1:2, :]) * pu).astype(BF16)

    row = _row_spec(tm, D_MODEL)
    wspec = _full_spec((D_MODEL, D_MODEL))
    return pl.pallas_call(
        body, name=name, grid=(t // tm,),
        in_specs=[row, row, wspec, wspec, _row_spec(tm, D_MODEL, GATE0_BLOCK), _row_spec(tm, D_MODEL, GATE1_BLOCK),
                  _full_spec((2, D_MODEL))],
        out_specs=[row, row, row],
        out_shape=[jax.ShapeDtypeStruct((t, D_MODEL), BF16), jax.ShapeDtypeStruct((t, D_MODEL), F32),
                   jax.ShapeDtypeStruct((t, D_MODEL), F32)],
        compiler_params=_cparams(("parallel",)),
    )(ret, po, w_ru, w_pu, proj, proj, bias)


def _merge_bwd(name, dm, ru, pu, proj, bias, tm=512):
    t = dm.shape[0]

    def body(dm_ref, ru_ref, pu_ref, g0_ref, g1_ref, b_ref, dru_ref, dpu_ref, dg0_ref, dg1_ref, db_ref):
        i = pl.program_id(0)
        d = dm_ref[...]
        s0 = _sigmoid(g0_ref[...] + b_ref[0:1, :])
        s1 = _sigmoid(g1_ref[...] + b_ref[1:2, :])
        dru_ref[...] = (d * s0).astype(BF16)
        dpu_ref[...] = (d * s1).astype(BF16)
        dg0 = d * ru_ref[...] * (s0 * (1.0 - s0))
        dg1 = d * pu_ref[...] * (s1 * (1.0 - s1))
        dg0_ref[...] = dg0.astype(BF16)
        dg1_ref[...] = dg1.astype(BF16)
        part0 = jnp.sum(dg0, axis=0, keepdims=True)
        part1 = jnp.sum(dg1, axis=0, keepdims=True)

        @pl.when(i == 0)
        def _():
            db_ref[0:1, :] = part0
            db_ref[1:2, :] = part1

        @pl.when(i > 0)
        def _():
            db_ref[0:1, :] += part0
            db_ref[1:2, :] += part1

    row = _row_spec(tm, D_MODEL)
    return pl.pallas_call(
        body, name=name, grid=(t // tm,),
        in_specs=[row, row, row, _row_spec(tm, D_MODEL, GATE0_BLOCK), _row_spec(tm, D_MODEL, GATE1_BLOCK),
                  _full_spec((2, D_MODEL))],
        out_specs=[row, row, row, row, _full_spec((2, D_MODEL))],
        out_shape=[jax.ShapeDtypeStruct((t, D_MODEL), BF16)] * 4 + [jax.ShapeDtypeStruct((2, D_MODEL), F32)],
        compiler_params=_cparams(("arbitrary",)),
    )(dm, ru, pu, proj, proj, bias)


def _half_scale(acc):
    return (FFN_RES_WEIGHT * acc,)


def _residual_half(acc, res):
    return (res + FFN_RES_WEIGHT * acc,)


def _normed(h, g):
    return h * lax.rsqrt(jnp.mean(h * h, axis=-1, keepdims=True) + NORM_EPS) * g


def _residual_half_norm(acc, res, g):
    h = res + FFN_RES_WEIGHT * acc
    return h, _normed(h, g)


def _residual_norm(acc, res, g):
    h = res + acc
    return h, _normed(h, g)


FF_TILE = D_FF // 2
DW_TILE = 256
FF_CHUNKS = ((0, 512), (512, 1024), (1024, FF_TILE))


def _ffn_in(name, nrm, w_in, tm=512):
    t = nrm.shape[0]
    nj = D_FF // FF_TILE

    def body(n_ref, wg_ref, wu_ref, a_ref, mid_ref):
        nv = n_ref[...]
        for c0, c1 in FF_CHUNKS:
            gate = _dot(nv, wg_ref[:, c0:c1])
            up = _dot(nv, wu_ref[:, c0:c1])
            a_ref[0, :, c0:c1] = gate
            a_ref[1, :, c0:c1] = up
            mid_ref[:, c0:c1] = (gate * _sigmoid(gate) * up).astype(BF16)

    return pl.pallas_call(
        body, name=name, grid=(nj, t // tm),
        in_specs=[pl.BlockSpec((tm, D_MODEL), lambda j, i: (i, 0)),
                  pl.BlockSpec((D_MODEL, FF_TILE), lambda j, i: (0, j)),
                  pl.BlockSpec((D_MODEL, FF_TILE), lambda j, i: (0, j + nj))],
        out_specs=[pl.BlockSpec((2, tm, FF_TILE), lambda j, i: (0, i, j)), pl.BlockSpec((tm, FF_TILE), lambda j, i: (i, j))],
        out_shape=[jax.ShapeDtypeStruct((2, t, D_FF), F32), jax.ShapeDtypeStruct((t, D_FF), BF16)],
        compiler_params=_cparams(("parallel", "parallel")),
    )(nrm, w_in, w_in)


def _ffn_dact(name, dout_b, w_out, a, tm=512):
    t = dout_b.shape[0]

    def body(d_ref, w_ref, a_ref, da_ref):
        dv = d_ref[...]
        for c0, c1 in FF_CHUNKS:
            dm = FFN_RES_WEIGHT * _dot(dv, w_ref[c0:c1, :], "nt")
            gate, up = a_ref[0, :, c0:c1], a_ref[1, :, c0:c1]
            s = _sigmoid(gate)
            da_ref[0, :, c0:c1] = (dm * up * (s * (1.0 + gate * (1.0 - s)))).astype(BF16)
            da_ref[1, :, c0:c1] = (dm * (gate * s)).astype(BF16)

    blk = pl.BlockSpec((2, tm, FF_TILE), lambda j, i: (0, i, j))
    return pl.pallas_call(
        body, name=name, grid=(D_FF // FF_TILE, t // tm),
        in_specs=[pl.BlockSpec((tm, D_MODEL), lambda j, i: (i, 0)), pl.BlockSpec((FF_TILE, D_MODEL), lambda j, i: (j, 0)), blk],
        out_specs=blk,
        out_shape=jax.ShapeDtypeStruct((2, t, D_FF), BF16),
        compiler_params=_cparams(("parallel", "parallel")),
    )(dout_b, w_out, a)


def _ffn_fwd(tag, h, nrm, get_w_in, get_w_out, next_g=None):
    t = h.shape[0]
    w_in = get_w_in(nrm)
    a, mid = _ffn_in(f"{tag}_in", nrm, w_in, tm=min(512, t))
    w_out = get_w_out(mid)
    if next_g is None:
        out = _matmul(f"{tag}_out", mid, w_out, "nn", t, D_MODEL, D_FF, 512, D_MODEL, D_FF, [F32],
                      extras=(h,), epilogue=_residual_half)
        nxt = None
    else:
        out, nxt = _matmul(f"{tag}_out", mid, w_out, "nn", t, D_MODEL, D_FF, 512, D_MODEL, D_FF, [F32, BF16],
                           extras=(h,), consts=(next_g,), epilogue=_residual_half_norm)
    return out, nxt, (nrm, a, mid, w_in, w_out)


def _ffn_bwd(tag, h, g, saved, dout, dout_b, on_grads):
    t = h.shape[0]
    nrm, a, mid, w_in, w_out = saved
    d_w_out = _matmul(f"{tag}_dwout", mid, dout_b, "tn", D_FF, D_MODEL, t, DW_TILE, D_MODEL, t, [BF16], epilogue=_half_scale,
                      resident="b")
    da = _ffn_dact(f"{tag}_dact", dout_b, w_out, a, tm=min(512, t))
    nj = D_FF // DW_TILE
    d_w_in = _matmul(f"{tag}_dwin", nrm, da, "tn", D_MODEL, 2 * D_FF, t, D_MODEL, DW_TILE, t, [BF16], resident="a",
                     b_spec=pl.BlockSpec((None, t, DW_TILE), lambda i, j, kk: (j // nj, 0, j % nj)))
    tie = on_grads({f"{tag}_w_in": d_w_in, f"{tag}_w_out": d_w_out})
    tm = min(256, t)
    dh, dh_b, dg = _proj_norm_bwd(f"{tag}_dn", da, pl.BlockSpec((2, tm, D_FF), lambda i: (0, i, 0)),
                                  ((0, 0, D_FF), (1, D_FF, 2 * D_FF)), w_in, h, g if tie is None else g + tie, dout, tm)
    return dh, dh_b, dg


def _local_step(x, target, vec, get_w, on_grads):
    t = x.shape[0]
    cos, sin = _rope_tables(t)
    rtab = _retention_tables()
    ptab = _pool_tables()
    w = {}

    def getter(group, name):
        def get(after):
            if name not in w:
                w.update(get_w(group, after))
            return w[name]
        return get

    nrm1 = _rmsnorm_fwd("ffn1_norm", x, vec["norm_ffn1"])
    h1, u, s1 = _ffn_fwd("ffn1", x, nrm1, getter(0, "ffn1_w_in"), getter(1, "ffn1_w_out"), vec["norm_mix"])
    w.update(get_w(2, u))
    proj = _matmul("mix_in", u, w["w_in"], "nn", t, IN_WIDTH, D_MODEL, 1024, 1024, D_MODEL, [F32], n_outer=True)
    qr, kr, vb = _rotary_fwd("rotary", proj, cos, sin)
    o, ret, states = _retention_fwd("retention", qr, kr, vb, proj, rtab)
    pm, mixed, po = _pool_fwd("pool", proj, w["pool_w"], vec["pool_scale"], ptab)
    merged, ru, pu = _merge_fwd("merge", ret, po, w["w_ret_up"], w["w_pool_up"], proj, vec["gate_bias"])
    h2, nrm2 = _matmul("mix_out", merged, w["w_out"], "nn", t, D_MODEL, D_MODEL, 512, D_MODEL, D_MODEL, [F32, BF16],
                       extras=(h1,), consts=(vec["norm_ffn2"],), epilogue=_residual_norm)
    h3, _, s2 = _ffn_fwd("ffn2", h2, nrm2, getter(3, "ffn2_w_in"), getter(3, "ffn2_w_out"))
    dh3, dh3_b, dg_final, loss = _loss_and_grad("loss", h3, vec["norm_final"], target)

    dh2, dh2_b, dg_ffn2 = _ffn_bwd("ffn2", h2, vec["norm_ffn2"], s2, dh3, dh3_b, on_grads)
    dm = _matmul("mix_dmerged", dh2_b, w["w_out"], "nt", t, D_MODEL, D_MODEL, 1024, D_MODEL, D_MODEL, [F32])
    d_w_out = _matmul("mix_dwout", merged, dh2_b, "tn", D_MODEL, D_MODEL, t, D_MODEL, D_MODEL, 1024, [BF16])
    dru, dpu, dg0, dg1, d_bias = _merge_bwd("merge_bwd", dm, ru, pu, proj, vec["gate_bias"])
    dret = _matmul("mix_dret", dru, w["w_ret_up"], "nt", t, D_MODEL, D_MODEL, 1024, D_MODEL, D_MODEL, [F32])
    d_w_ru = _matmul("mix_dwru", ret, dru, "tn", D_MODEL, D_MODEL, t, D_MODEL, D_MODEL, 1024, [BF16])
    dpo = _matmul("mix_dpool", dpu, w["w_pool_up"], "nt", t, D_MODEL, D_MODEL, 1024, D_MODEL, D_MODEL, [F32])
    d_w_pu = _matmul("mix_dwpu", po, dpu, "tn", D_MODEL, D_MODEL, t, D_MODEL, D_MODEL, 1024, [BF16])
    dp, d_pool_w, d_scale = _pool_bwd("pool_bwd", dpo, pm, mixed, w["pool_w"], vec["pool_scale"], ptab)
    dqr, dkr, dv, dgr = _retention_bwd("retention_bwd", dret, o, qr, kr, vb, proj, states, rtab)
    dq, dk = _rotary_bwd("rotary_bwd", dqr, dkr, cos, sin)
    dproj = jnp.concatenate([dq, dk, dv, dgr, dp, dg0, dg1], axis=1)
    d_w_in = _matmul("mix_dwin", u, dproj, "tn", D_MODEL, IN_WIDTH, t, D_MODEL, 512, t, [BF16], resident="a")
    tie = on_grads(dict(w_in=d_w_in, pool_w=d_pool_w.astype(BF16), w_ret_up=d_w_ru, w_pool_up=d_w_pu, w_out=d_w_out))
    g_mix = vec["norm_mix"] if tie is None else vec["norm_mix"] + tie
    tm = min(256, t)
    dh1, dh1_b, dg_mix = _proj_norm_bwd("mix_du", dproj, pl.BlockSpec((tm, IN_WIDTH), lambda i: (i, 0)), ((None, 0, IN_WIDTH),),
                                        w["w_in"], h1, g_mix, dh2, tm)
    dx, _, dg_ffn1 = _ffn_bwd("ffn1", x, vec["norm_ffn1"], s1, dh1, dh1_b, on_grads)

    small = dict(norm_ffn1=dg_ffn1, norm_mix=dg_mix, gate_bias=d_bias, pool_scale=d_scale, norm_ffn2=dg_ffn2,
                 norm_final=dg_final)
    return loss[0, 0], dx, small


BIG = ("ffn1_w_in", "ffn1_w_out", "w_in", "pool_w", "w_ret_up", "w_pool_up", "w_out", "ffn2_w_in", "ffn2_w_out")
KIND = dict(ffn1_w_in="col", ffn1_w_out="row", w_in="col", pool_w="pool", w_ret_up="row", w_pool_up="row", w_out="row",
            ffn2_w_in="col", ffn2_w_out="row")
ANY = pl.BlockSpec(memory_space=pl.ANY)


def _place():
    x, y, c = lax.axis_index("x"), lax.axis_index("y"), lax.axis_index("c")
    chips = [(1 - x, y), (x, 1 - y), (1 - x, 1 - y)]
    return x, y, c, chips


def _full_view_shape(kind, local_shape):
    if kind == "col":
        return (2, local_shape[0] // 2, N_CHIPS * local_shape[1])
    if kind == "row":
        return (N_CHIPS, 2, local_shape[0] // 2, local_shape[1])
    return (GROUPS, N_CHIPS, 2, local_shape[1] // 2, local_shape[2])


def _local_view(kind, arr):
    if kind == "pool":
        return arr.reshape(GROUPS, 2, arr.shape[1] // 2, arr.shape[2])
    return arr.reshape(2, arr.shape[0] // 2, arr.shape[1])


def _blk(kind, ref, s, c):
    if kind == "col":
        cs = ref.shape[2] // N_CHIPS
        return ref.at[c, :, pl.ds(pl.multiple_of(s * cs, 128), cs)]
    if kind == "row":
        return ref.at[s, c]
    return ref.at[:, s, c]


def _half(kind, ref, c):
    return ref.at[:, c] if kind == "pool" else ref.at[c]


def _shard(kind, ref, s):
    if kind == "col":
        cs = ref.shape[2] // N_CHIPS
        return ref.at[:, :, pl.ds(pl.multiple_of(s * cs, 128), cs)]
    if kind == "row":
        return ref.at[s]
    return ref.at[:, s]


HBM = pl.BlockSpec(memory_space=pltpu.HBM)
SEM = pl.BlockSpec(memory_space=pltpu.SEMAPHORE)
EFFECT = pltpu.SideEffectType.DATAFLOW_SIDE_EFFECTING
WEIGHT_GROUPS = (("ffn1_w_in",), ("ffn1_w_out",), ("w_in", "pool_w", "w_ret_up", "w_pool_up", "w_out"), ("ffn2_w_in", "ffn2_w_out"))
GRAD_GROUPS = (("ffn2_w_in", "ffn2_w_out"), ("w_in", "pool_w", "w_ret_up", "w_pool_up", "w_out"), ("ffn1_w_in", "ffn1_w_out"))


def _hbm(a):
    return pltpu.with_memory_space_constraint(a, pltpu.HBM)


def _natural(kind, o):
    if kind == "col":
        return o.reshape(o.shape[0] * o.shape[1], o.shape[2])
    if kind == "row":
        return o.reshape(-1, o.shape[3])
    return o.reshape(GROUPS, -1, o.shape[4])


def _ici_copy(kind, loc, full, j, chips, s, c, send_sem, recv_sem):
    px, py = chips[j]
    return (pltpu.make_async_remote_copy(src_ref=_half(kind, loc, c), dst_ref=_blk(kind, full, s, c), send_sem=send_sem,
                                         recv_sem=recv_sem, device_id=(px, py, c), device_id_type=MESH),
            pltpu.make_async_remote_copy(src_ref=_half(kind, loc, c), dst_ref=_blk(kind, full, 2 * px + py, c), send_sem=send_sem,
                                         recv_sem=recv_sem, device_id=(px, py, c), device_id_type=MESH))


def _gather_start(shards):
    names = [nm for grp in WEIGHT_GROUPS for nm in grp]
    kinds = [KIND[nm] for nm in names]
    n, ng = len(names), len(WEIGHT_GROUPS)
    locs = [_hbm(_local_view(KIND[nm], shards[nm])) for nm in names]
    lands = [_hbm(lax.empty(_full_view_shape(KIND[nm], shards[nm].shape), BF16)) for nm in names]
    first = np.cumsum([0] + [len(grp) for grp in WEIGHT_GROUPS])

    def body(*refs):
        loc, full = refs[:n], refs[n:2 * n]
        send_sems, recv_sems = refs[2 * n:2 * n + ng], refs[2 * n + ng:2 * n + 2 * ng]
        token = refs[-1]
        x, y, c, chips = _place()
        s = 2 * x + y
        for g in range(ng):
            for a in range(first[g], first[g + 1]):
                for j in range(3):
                    k = 3 * (a - first[g]) + j
                    _ici_copy(kinds[a], loc[a], full[a], j, chips, s, c, send_sems[g].at[k], recv_sems[g].at[k])[0].start()
        token[...] = jnp.zeros(token.shape, F32)

    sem_shapes = [pltpu.SemaphoreType.DMA((3 * len(grp),)) for grp in WEIGHT_GROUPS]
    outs = pl.pallas_call(
        body, name="gather_start",
        in_specs=[HBM] * (2 * n),
        out_specs=[SEM] * (2 * ng) + [HBM] * (2 * n) + [pl.BlockSpec(memory_space=pltpu.VMEM)],
        out_shape=sem_shapes + sem_shapes + [pltpu.HBM(a.shape, a.dtype) for a in locs + lands] + [jax.ShapeDtypeStruct((8, 128), F32)],
        input_output_aliases={i: 2 * ng + i for i in range(2 * n)},
        compiler_params=pltpu.CompilerParams(has_side_effects=EFFECT),
    )(*locs, *lands)
    send_sems, recv_sems = outs[:ng], outs[ng:2 * ng]
    locs_t, lands_t = outs[2 * ng:2 * ng + n], outs[2 * ng + n:2 * ng + 2 * n]
    groups = []
    for g in range(ng):
        sl = slice(first[g], first[g + 1])
        groups.append((send_sems[g], recv_sems[g], list(locs_t[sl]), list(lands_t[sl])))
    return groups, outs[-1]


def _gather_finish(g, group, after):
    names = WEIGHT_GROUPS[g]
    kinds = [KIND[nm] for nm in names]
    m = len(names)
    send_sem, recv_sem, locs, lands = group

    def wait_body(*refs):
        loc, full = refs[:m], refs[m:2 * m]
        send_sems, recv_sems = refs[2 * m], refs[2 * m + 1]
        x, y, c, chips = _place()
        s = 2 * x + y
        for a in range(m):
            for j in range(3):
                k = 3 * a + j
                sent, landed = _ici_copy(kinds[a], loc[a], full[a], j, chips, s, c, send_sems.at[k], recv_sems.at[k])
                sent.wait_send()
                landed.wait_recv()

    outs = pl.pallas_call(
        wait_body, name=f"gather_wait_{g}",
        in_specs=[HBM] * (2 * m) + [SEM, SEM, ANY], out_specs=[HBM] * (2 * m),
        out_shape=[pltpu.HBM(a.shape, a.dtype) for a in locs + lands],
        input_output_aliases={i: i for i in range(2 * m)},
        compiler_params=pltpu.CompilerParams(has_side_effects=EFFECT),
    )(*locs, *lands, send_sem, recv_sem, after)
    locs, lands = outs[:m], outs[m:]

    def forward_body(*refs):
        loc, full = refs[:m], refs[2 * m:3 * m]
        send_sems, recv_sems = refs[3 * m:]
        x, y, c, chips = _place()
        s = 2 * x + y
        sib = (x, y, 1 - c)

        def remote(a, k, src, dst):
            return pltpu.make_async_remote_copy(src_ref=src, dst_ref=dst, send_sem=send_sems.at[4 * a + k],
                                                recv_sem=recv_sems.at[4 * a + k], device_id=sib, device_id_type=MESH)

        sends = []
        for a in range(m):
            for j, (px, py) in enumerate(chips):
                theirs = _blk(kinds[a], full[a], 2 * px + py, c)
                sends.append(remote(a, j, theirs, theirs))
            sends.append(remote(a, 3, loc[a], _shard(kinds[a], full[a], s)))
        for cp in sends:
            cp.start()
        for a in range(m):
            for j, (px, py) in enumerate(chips):
                from_sib = _blk(kinds[a], full[a], 2 * px + py, 1 - c)
                remote(a, j, from_sib, from_sib).wait_recv()
            own = _shard(kinds[a], full[a], s)
            remote(a, 3, own, own).wait_recv()
        for cp in sends:
            cp.wait_send()

    outs = pl.pallas_call(
        forward_body, name=f"gather_forward_{g}",
        in_specs=[ANY] * (2 * m), out_specs=[ANY] * m,
        out_shape=[jax.ShapeDtypeStruct(a.shape, a.dtype) for a in lands],
        input_output_aliases={m + i: i for i in range(m)},
        scratch_shapes=[pltpu.SemaphoreType.DMA((4 * m,)), pltpu.SemaphoreType.DMA((4 * m,))],
    )(*locs, *lands)
    return {nm: _natural(k, o) for nm, k, o in zip(names, kinds, outs)}


def _grad_view(kind, g):
    if kind == "col":
        return g.reshape(2, g.shape[0] // 2, g.shape[1])
    if kind == "row":
        return g.reshape(N_CHIPS, 2, g.shape[0] // (2 * N_CHIPS), g.shape[1])
    return g.reshape(GROUPS, N_CHIPS, 2, g.shape[1] // (2 * N_CHIPS), g.shape[2])


def _pair_exchange(tag, names, views):
    kinds = [KIND[nm] for nm in names]
    n = len(names)

    def other_half(kind, ref, c):
        if kind == "col":
            return ref.at[c]
        if kind == "row":
            return ref.at[:, c]
        return ref.at[:, :, c]

    def body(*refs):
        g, got = refs[:n], refs[n:2 * n]
        send_sems, recv_sems = refs[2 * n:]
        x, y, c, _ = _place()
        sib = (x, y, 1 - c)
        cps = []
        for a in range(n):
            cp = pltpu.make_async_remote_copy(src_ref=other_half(kinds[a], g[a], 1 - c), dst_ref=got[a], send_sem=send_sems.at[a],
                                              recv_sem=recv_sems.at[a], device_id=sib, device_id_type=MESH)
            cp.start()
            cps.append(cp)
        for cp in cps:
            cp.wait()

    def got_shape(kind, v):
        if kind == "col":
            return v.shape[1:]
        if kind == "row":
            return (v.shape[0],) + v.shape[2:]
        return v.shape[:2] + v.shape[3:]

    outs = pl.pallas_call(
        body, name=f"grad_pair_exchange_{tag}",
        in_specs=[ANY] * n, out_specs=[ANY] * n,
        out_shape=[jax.ShapeDtypeStruct(got_shape(k, views[nm]), BF16) for nm, k in zip(names, kinds)],
        scratch_shapes=[pltpu.SemaphoreType.DMA((n,)), pltpu.SemaphoreType.DMA((n,))],
    )(*[views[nm] for nm in names])
    return dict(zip(names, outs))


def _pair_sum(name, kind, view, got, c_arr):
    if kind == "col":
        _, rows, cols = view.shape
        tr = 128
        grid = (rows // tr,)
        v_spec = pl.BlockSpec((None, tr, cols), lambda i, c: (c[0], i, 0))
        g_spec = pl.BlockSpec((tr, cols), lambda i, c: (i, 0))
    elif kind == "row":
        _, _, rows, cols = view.shape
        grid = (N_CHIPS,)
        v_spec = pl.BlockSpec((None, None, rows, cols), lambda i, c: (i, c[0], 0, 0))
        g_spec = pl.BlockSpec((None, rows, cols), lambda i, c: (i, 0, 0))
    else:
        _, _, _, rows, cols = view.shape
        grid = (GROUPS,)
        v_spec = pl.BlockSpec((None, N_CHIPS, None, rows, cols), lambda i, c: (i, 0, c[0], 0, 0))
        g_spec = pl.BlockSpec((None, N_CHIPS, rows, cols), lambda i, c: (i, 0, 0, 0))

    def body(c_ref, v_ref, g_ref, o_ref):
        o_ref[...] = (v_ref[...].astype(F32) + g_ref[...].astype(F32)).astype(BF16)

    return pl.pallas_call(
        body, name=name,
        grid_spec=pltpu.PrefetchScalarGridSpec(num_scalar_prefetch=1, grid=grid, in_specs=[v_spec, g_spec], out_specs=g_spec),
        out_shape=jax.ShapeDtypeStruct(got.shape, BF16),
        compiler_params=_cparams(("parallel",)),
    )(c_arr, view, got)


def _piece(kind, ref, s):
    if kind == "col":
        cs = ref.shape[1] // N_CHIPS
        return ref.at[:, pl.ds(pl.multiple_of(s * cs, 128), cs)]
    if kind == "row":
        return ref.at[s]
    return ref.at[:, s]


def _piece_shape(kind, shape):
    if kind == "col":
        return (shape[0], shape[1] // N_CHIPS)
    if kind == "row":
        return shape[1:]
    return (shape[0],) + shape[2:]


def _shard_copies(kinds, p, got, send_sems, recv_sems):
    x, y, c, chips = _place()
    return [pltpu.make_async_remote_copy(src_ref=_piece(kinds[a], p[a], 2 * px + py), dst_ref=got[a].at[j],
                                         send_sem=send_sems.at[3 * a + j], recv_sem=recv_sems.at[3 * a + j],
                                         device_id=(px, py, c), device_id_type=MESH)
            for a in range(len(kinds)) for j, (px, py) in enumerate(chips)]


def _shard_exchange_start(g, names, psums):
    kinds = [KIND[nm] for nm in names]
    n = len(names)
    srcs = [_hbm(psums[nm]) for nm in names]
    lands = [_hbm(lax.empty((3,) + _piece_shape(k, psums[nm].shape), BF16)) for nm, k in zip(names, kinds)]

    def body(*refs):
        p, got = refs[:n], refs[n:2 * n]
        send_sems, recv_sems = refs[2 * n], refs[2 * n + 1]
        token = refs[-1]
        for cp in _shard_copies(kinds, p, got, send_sems, recv_sems):
            cp.start()
        token[...] = jnp.zeros(token.shape, F32)

    sem_shape = pltpu.SemaphoreType.DMA((3 * n,))
    outs = pl.pallas_call(
        body, name=f"grad_shard_exchange_start_{g}",
        in_specs=[HBM] * (2 * n),
        out_specs=[SEM, SEM] + [HBM] * (2 * n) + [pl.BlockSpec(memory_space=pltpu.VMEM)],
        out_shape=[sem_shape, sem_shape] + [pltpu.HBM(a.shape, a.dtype) for a in srcs + lands] + [jax.ShapeDtypeStruct((8, 128), F32)],
        input_output_aliases={i: 2 + i for i in range(2 * n)},
        compiler_params=pltpu.CompilerParams(has_side_effects=EFFECT),
    )(*srcs, *lands)
    return (outs[0], outs[1], list(outs[2:2 + n]), list(outs[2 + n:2 + 2 * n])), outs[-1]


def _shard_exchange_wait(g, names, state, after):
    kinds = [KIND[nm] for nm in names]
    n = len(names)
    send_sem, recv_sem, srcs, lands = state

    def body(*refs):
        p, got = refs[:n], refs[n:2 * n]
        for cp in _shard_copies(kinds, p, got, refs[2 * n], refs[2 * n + 1]):
            cp.wait_send()
            cp.wait_recv()

    outs = pl.pallas_call(
        body, name=f"grad_shard_exchange_wait_{g}",
        in_specs=[HBM] * (2 * n) + [SEM, SEM, ANY], out_specs=[HBM] * (2 * n),
        out_shape=[pltpu.HBM(a.shape, a.dtype) for a in srcs + lands],
        input_output_aliases={i: i for i in range(2 * n)},
        compiler_params=pltpu.CompilerParams(has_side_effects=EFFECT),
    )(*srcs, *lands, send_sem, recv_sem, after)
    return dict(zip(names, outs[:n])), dict(zip(names, outs[n:]))


def _shard_sum(name, kind, psum, got, sc_arr):
    if kind == "col":
        rows, cols = psum.shape
        cs = cols // N_CHIPS
        tr = 128
        grid = (rows // tr,)
        p_spec = pl.BlockSpec((tr, cs), lambda i, sc: (i, sc[0]))
        g_spec = pl.BlockSpec((3, tr, cs), lambda i, sc: (0, i, 0))
        o_spec = pl.BlockSpec((None, tr, cs), lambda i, sc: (sc[1], i, 0))
        out_shape = (2, rows, cs)
    elif kind == "row":
        _, rows, cols = psum.shape
        grid = (1,)
        p_spec = pl.BlockSpec((None, rows, cols), lambda i, sc: (sc[0], 0, 0))
        g_spec = pl.BlockSpec((3, rows, cols), lambda i, sc: (0, 0, 0))
        o_spec = pl.BlockSpec((None, rows, cols), lambda i, sc: (sc[1], 0, 0))
        out_shape = (2, rows, cols)
    else:
        _, _, rows, cols = psum.shape
        grid = (1,)
        p_spec = pl.BlockSpec((GROUPS, None, rows, cols), lambda i, sc: (0, sc[0], 0, 0))
        g_spec = pl.BlockSpec((3, GROUPS, rows, cols), lambda i, sc: (0, 0, 0, 0))
        o_spec = pl.BlockSpec((GROUPS, None, rows, cols), lambda i, sc: (0, sc[1], 0, 0))
        out_shape = (GROUPS, 2, rows, cols)

    def body(sc_ref, p_ref, g_ref, o_ref):
        o_ref[...] = ((p_ref[...].astype(F32) + g_ref[0].astype(F32)) + g_ref[1].astype(F32)) + g_ref[2].astype(F32)

    return pl.pallas_call(
        body, name=name,
        grid_spec=pltpu.PrefetchScalarGridSpec(num_scalar_prefetch=1, grid=grid, in_specs=[p_spec, g_spec], out_specs=o_spec),
        out_shape=jax.ShapeDtypeStruct(out_shape, F32),
        compiler_params=_cparams(("parallel",)),
    )(sc_arr, psum, got)


def _half_exchange(tag, names, bufs):
    kinds = [KIND[nm] for nm in names]
    n = len(names)

    def body(*refs):
        out = refs[n:2 * n]
        send_sems, recv_sems = refs[2 * n:]
        x, y, c, _ = _place()
        sib = (x, y, 1 - c)
        cps = []
        for a in range(n):
            mine = _half(kinds[a], out[a], c)
            cp = pltpu.make_async_remote_copy(src_ref=mine, dst_ref=mine, send_sem=send_sems.at[a], recv_sem=recv_sems.at[a],
                                              device_id=sib, device_id_type=MESH)
            cp.start()
            cps.append(cp)
        for a, cp in enumerate(cps):
            cp.wait_send()
            theirs = _half(kinds[a], out[a], 1 - c)
            pltpu.make_async_remote_copy(src_ref=theirs, dst_ref=theirs, send_sem=send_sems.at[a], recv_sem=recv_sems.at[a],
                                         device_id=sib, device_id_type=MESH).wait_recv()

    outs = pl.pallas_call(
        body, name=f"grad_half_exchange_{tag}",
        in_specs=[ANY] * n, out_specs=[ANY] * n,
        out_shape=[jax.ShapeDtypeStruct(bufs[nm].shape, F32) for nm in names],
        input_output_aliases={a: a for a in range(n)},
        scratch_shapes=[pltpu.SemaphoreType.DMA((n,)), pltpu.SemaphoreType.DMA((n,))],
    )(*[bufs[nm] for nm in names])
    return dict(zip(names, outs))


N_DEV = 8
SMALL_ROWS = 8


def _all_reduce_small(name, v):
    def body(v_ref, o_ref, buf, send_sems, recv_sems):
        x, y, c, _ = _place()
        me = 4 * x + 2 * y + c
        buf[me] = v_ref[...]
        cps = []
        for r in range(1, N_DEV):
            to = (x ^ (r >> 2), y ^ ((r >> 1) & 1), c ^ (r & 1))
            cp = pltpu.make_async_remote_copy(src_ref=v_ref, dst_ref=buf.at[me], send_sem=send_sems.at[r - 1],
                                              recv_sem=recv_sems.at[r - 1], device_id=to, device_id_type=MESH)
            cp.start()
            cps.append(cp)
        for r in range(1, N_DEV):
            pltpu.make_async_remote_copy(src_ref=v_ref, dst_ref=buf.at[me ^ r], send_sem=send_sems.at[r - 1],
                                         recv_sem=recv_sems.at[r - 1], device_id=(x, y, c), device_id_type=MESH).wait_recv()
        for cp in cps:
            cp.wait_send()
        acc = buf[0]
        for d in range(1, N_DEV):
            acc = acc + buf[d]
        o_ref[...] = acc

    vm = pl.BlockSpec(memory_space=pltpu.VMEM)
    return pl.pallas_call(
        body, name=name, in_specs=[vm], out_specs=vm,
        out_shape=jax.ShapeDtypeStruct((SMALL_ROWS, D_MODEL), F32),
        scratch_shapes=[pltpu.VMEM((N_DEV, SMALL_ROWS, D_MODEL), F32), pltpu.SemaphoreType.DMA((N_DEV - 1,)),
                        pltpu.SemaphoreType.DMA((N_DEV - 1,))],
    )(v)


def _adamw(name, w, g, m, v):
    rows, cols = w.shape
    tr = next((c for c in (256, 176, 128, 64, 32, 8) if rows % c == 0), rows)
    spec = pl.BlockSpec((tr, cols), lambda i: (i, 0))

    def body(w_ref, g_ref, m_ref, v_ref, d_ref, mo_ref, vo_ref):
        gv = g_ref[...]
        m_new = ADAM_B1 * m_ref[...] + (1.0 - ADAM_B1) * gv
        v_new = ADAM_B2 * v_ref[...] + (1.0 - ADAM_B2) * jnp.square(gv)
        m_hat = m_new / (1.0 - ADAM_B1 ** ADAM_STEP)
        v_hat = v_new / (1.0 - ADAM_B2 ** ADAM_STEP)
        d_ref[...] = -ADAM_LR * (m_hat / (jnp.sqrt(v_hat) + ADAM_EPS) + ADAM_WD * w_ref[...])
        mo_ref[...] = m_new
        vo_ref[...] = v_new

    return pl.pallas_call(
        body, name=name, grid=(rows // tr,),
        in_specs=[spec] * 4, out_specs=[spec] * 3,
        out_shape=[jax.ShapeDtypeStruct((rows, cols), F32)] * 3,
        compiler_params=_cparams(("parallel",)),
    )(w, g, m, v)


WEIGHTS = ("norm_ffn1", "ffn1_w_in", "ffn1_w_out", "norm_mix", "w_in", "gate_bias", "pool_w", "pool_scale", "w_ret_up",
           "w_pool_up", "w_out", "norm_ffn2", "ffn2_w_in", "ffn2_w_out", "norm_final")
SMALL_ROW = dict(norm_ffn1=0, norm_mix=1, gate_bias=2, pool_scale=4, norm_ffn2=5, norm_final=6)


def _as2d(a):
    return a.reshape(-1, a.shape[-1])


def kernel(x, norm_ffn1, ffn1_w_in, ffn1_w_out, norm_mix, w_in, gate_bias, pool_w, pool_scale, w_ret_up, w_pool_up, w_out, norm_ffn2, ffn2_w_in, ffn2_w_out, norm_final, loss_target, m_norm_ffn1, m_ffn1_w_in, m_ffn1_w_out, m_norm_mix, m_w_in, m_gate_bias, m_pool_w, m_pool_scale, m_w_ret_up, m_w_pool_up, m_w_out, m_norm_ffn2, m_ffn2_w_in, m_ffn2_w_out, m_norm_final, v_norm_ffn1, v_ffn1_w_in, v_ffn1_w_out, v_norm_mix, v_w_in, v_gate_bias, v_pool_w, v_pool_scale, v_w_ret_up, v_w_pool_up, v_w_out, v_norm_ffn2, v_ffn2_w_in, v_ffn2_w_out, v_norm_final):
    wt = dict(norm_ffn1=norm_ffn1, ffn1_w_in=ffn1_w_in, ffn1_w_out=ffn1_w_out, norm_mix=norm_mix, w_in=w_in, gate_bias=gate_bias,
              pool_w=pool_w, pool_scale=pool_scale, w_ret_up=w_ret_up, w_pool_up=w_pool_up, w_out=w_out, norm_ffn2=norm_ffn2,
              ffn2_w_in=ffn2_w_in, ffn2_w_out=ffn2_w_out, norm_final=norm_final)
    mom = dict(norm_ffn1=m_norm_ffn1, ffn1_w_in=m_ffn1_w_in, ffn1_w_out=m_ffn1_w_out, norm_mix=m_norm_mix, w_in=m_w_in,
               gate_bias=m_gate_bias, pool_w=m_pool_w, pool_scale=m_pool_scale, w_ret_up=m_w_ret_up, w_pool_up=m_w_pool_up,
               w_out=m_w_out, norm_ffn2=m_norm_ffn2, ffn2_w_in=m_ffn2_w_in, ffn2_w_out=m_ffn2_w_out, norm_final=m_norm_final)
    var = dict(norm_ffn1=v_norm_ffn1, ffn1_w_in=v_ffn1_w_in, ffn1_w_out=v_ffn1_w_out, norm_mix=v_norm_mix, w_in=v_w_in,
               gate_bias=v_gate_bias, pool_w=v_pool_w, pool_scale=v_pool_scale, w_ret_up=v_w_ret_up, w_pool_up=v_w_pool_up,
               w_out=v_w_out, norm_ffn2=v_norm_ffn2, ffn2_w_in=v_ffn2_w_in, ffn2_w_out=v_ffn2_w_out, norm_final=v_norm_final)

    ax, ay, ac = lax.axis_index("x"), lax.axis_index("y"), lax.axis_index("c")
    chip = 2 * ax + ay
    c_arr = jnp.reshape(ac, (1,)).astype(jnp.int32)
    sc_arr = jnp.stack([chip, ac]).astype(jnp.int32)
    bias_cols = gate_bias.shape[-1]

    shards = {nm: wt[nm][0].astype(BF16) for nm in BIG}
    gather_groups, gather_token = _gather_start(shards)
    placed = lax.dynamic_update_slice(jnp.zeros((SMALL_ROWS, D_MODEL), F32), gate_bias[0], (0, chip * bias_cols))
    bias_full = _all_reduce_small("gather_gate_bias", jnp.where(ac == 0, placed, 0.0))[:2]
    vec = dict(norm_ffn1=norm_ffn1, norm_mix=norm_mix, norm_ffn2=norm_ffn2, pool_scale=pool_scale,
               norm_final=norm_final.reshape(1, D_MODEL), gate_bias=bias_full)

    def get_w(g, after):
        return _gather_finish(g, gather_groups[g], gather_token if after is None else after)

    pending = []

    def on_grads(gr):
        g = len(pending)
        names = GRAD_GROUPS[g]
        assert set(names) == set(gr), (names, list(gr))
        views = {nm: _grad_view(KIND[nm], gr[nm]) for nm in names}
        from_sib = _pair_exchange(g, names, views)
        psums = {nm: _pair_sum(f"pair_sum_{nm}", KIND[nm], views[nm], from_sib[nm], c_arr) for nm in names}
        state, token = _shard_exchange_start(g, names, psums)
        pending.append(state)
        return token[0:1, 0:1]

    loss_local, dx, small = _local_step(x[0], loss_target[0], vec, get_w, on_grads)

    packed = jnp.concatenate([small["norm_ffn1"], small["norm_mix"], small["gate_bias"], small["pool_scale"],
                              small["norm_ffn2"], small["norm_final"], jnp.broadcast_to(loss_local, (1, D_MODEL))], axis=0)
    small_sum = _all_reduce_small("reduce_small_grads", packed)
    loss = small_sum[SMALL_ROWS - 1, 0]
    grads, delta, new_m, new_v = {}, {}, {}, {}

    def adamw(nm):
        shape = wt[nm].shape
        d, m2, v2 = _adamw(f"adamw_{nm}", _as2d(wt[nm]), _as2d(grads[nm]), _as2d(mom[nm]), _as2d(var[nm]))
        delta[nm], new_m[nm], new_v[nm] = d.reshape(shape), m2.reshape(shape), v2.reshape(shape)
        return d

    for nm in ("norm_ffn1", "norm_mix", "pool_scale", "norm_ffn2"):
        grads[nm] = small_sum[SMALL_ROW[nm]][None, :]
    grads["norm_final"] = small_sum[SMALL_ROW["norm_final"]]
    grads["gate_bias"] = lax.dynamic_slice(small_sum, (SMALL_ROW["gate_bias"], chip * bias_cols), (2, bias_cols))[None]
    after = dx
    for g, names in enumerate(GRAD_GROUPS):
        psums, from_chips = _shard_exchange_wait(g, names, pending[g], after)
        bufs = {nm: _shard_sum(f"shard_sum_{nm}", KIND[nm], psums[nm], from_chips[nm], sc_arr) for nm in names}
        reduced = _half_exchange(g, names, bufs)
        for nm in names:
            grads[nm] = reduced[nm].reshape(wt[nm].shape)
            after = adamw(nm)
    for nm in WEIGHTS:
        if nm not in delta:
            adamw(nm)

    return (loss, dx[None], *[grads[nm] for nm in WEIGHTS], *[delta[nm] for nm in WEIGHTS],
            *[new_m[nm] for nm in WEIGHTS], *[new_v[nm] for nm in WEIGHTS])
```

```python
import functools

import numpy as np
import jax
import jax.numpy as jnp
from jax import lax
from jax.experimental import pallas as pl
from jax.experimental.pallas import tpu as pltpu

F32 = jnp.float32
BF16 = jnp.bfloat16
MESH = pl.DeviceIdType.MESH

D_MODEL = 1024
D_FF = 2816
HEADS = 4
HEAD_DIM = 256
GROUPS = 4
GROUP_DIM = 256
POOL_WINDOWS = (2, 4, 8, 16)
IN_WIDTH = 7 * D_MODEL
ROPE_BASE = 10000.0
NORM_EPS = 1e-6
FFN_RES_WEIGHT = 0.5
ADAM_LR, ADAM_B1, ADAM_B2, ADAM_EPS, ADAM_WD, ADAM_STEP = 0.001, 0.9, 0.999, 1e-08, 0.01, 10

N_CHIPS = 4
RET_BLOCK = 256
V7X_VMEM_LIMIT = 48 * 1024 * 1024


def _cparams(sem):
    return pltpu.CompilerParams(dimension_semantics=sem, vmem_limit_bytes=V7X_VMEM_LIMIT)


def _sigmoid(x):
    return jax.nn.sigmoid(x)


_DIMS = {"nn": (((1,), (0,)), ((), ())), "nt": (((1,), (1,)), ((), ())), "tn": (((0,), (0,)), ((), ()))}


def _matmul(name, a, b, mode, m, n, k, tm, tn, tk, out_dtypes, a_spec=None, b_spec=None, extras=(), consts=(), epilogue=None,
            resident=None, n_outer=False):
    tm, tn, tk = min(tm, m), min(tn, n), min(tk, k)
    gi, gj, gk = m // tm, n // tn, k // tk
    assert gi * tm == m and gj * tn == n and gk * tk == k, (name, m, n, k, tm, tn, tk)
    assert not (n_outer and (a_spec is not None or b_spec is not None)), name
    once = dict(pipeline_mode=pl.Buffered(1))

    def spec(shape, index, **kw):
        return pl.BlockSpec(shape, (lambda j, i, kk: index(i, j, kk)) if n_outer else index, **kw)

    if a_spec is None:
        kw = once if resident == "a" else {}
        a_spec = (spec((tk, tm), lambda i, j, kk: (kk, i), **kw) if mode == "tn"
                  else spec((tm, tk), lambda i, j, kk: (i, kk), **kw))
    if b_spec is None:
        kw = once if resident == "b" else {}
        b_spec = (spec((tn, tk), lambda i, j, kk: (j, kk), **kw) if mode == "nt"
                  else spec((tk, tn), lambda i, j, kk: (kk, j), **kw))
    n_ex, n_out = len(extras) + len(consts), len(out_dtypes)
    dims = _DIMS[mode]

    def body(a_ref, b_ref, *rest):
        ex_refs, out_refs = rest[:n_ex], rest[n_ex:n_ex + n_out]

        def finish(acc):
            outs = (acc,) if epilogue is None else epilogue(acc, *[e[...] for e in ex_refs])
            for o_ref, o in zip(out_refs, outs):
                o_ref[...] = o.astype(o_ref.dtype)

        prod = lax.dot_general(a_ref[...], b_ref[...], dims, preferred_element_type=F32)
        if gk == 1:
            finish(prod)
        else:
            acc_ref = rest[n_ex + n_out]
            kk = pl.program_id(2)

            @pl.when(kk == 0)
            def _():
                acc_ref[...] = prod

            @pl.when(kk > 0)
            def _():
                acc_ref[...] += prod

            @pl.when(kk == gk - 1)
            def _():
                finish(acc_ref[...])

    o_spec = spec((tm, tn), lambda i, j, kk: (i, j))
    outs = pl.pallas_call(
        body, name=name, grid=(gj, gi, gk) if n_outer else (gi, gj, gk),
        in_specs=[a_spec, b_spec] + [o_spec] * len(extras) + [spec((1, tn), lambda i, j, kk: (0, j))] * len(consts),
        out_specs=[o_spec] * n_out,
        out_shape=[jax.ShapeDtypeStruct((m, n), dt) for dt in out_dtypes],
        scratch_shapes=[pltpu.VMEM((tm, tn), F32)] if gk > 1 else [],
        compiler_params=_cparams(("parallel", "parallel", "arbitrary")),
    )(a, b, *extras, *consts)
    return outs[0] if n_out == 1 else outs


def _row_spec(tm, width, col_block=0):
    return pl.BlockSpec((tm, width), lambda i: (i, col_block))


def _full_spec(shape):
    return pl.BlockSpec(shape, lambda *_: (0,) * len(shape))


def _rmsnorm_fwd(name, h, g, tm=512):
    t = h.shape[0]

    def body(h_ref, g_ref, o_ref):
        x = h_ref[...]
        r = lax.rsqrt(jnp.mean(x * x, axis=-1, keepdims=True) + NORM_EPS)
        o_ref[...] = (x * r * g_ref[...]).astype(BF16)

    return pl.pallas_call(
        body, name=name, grid=(t // tm,),
        in_specs=[_row_spec(tm, D_MODEL), _full_spec((1, D_MODEL))],
        out_specs=_row_spec(tm, D_MODEL),
        out_shape=jax.ShapeDtypeStruct((t, D_MODEL), BF16),
        compiler_params=_cparams(("parallel",)),
    )(h, g)


def _proj_norm_bwd(name, a_list, a_specs, parts, w, h, g, dres, tm):
    t = h.shape[0]
    na = len(a_list)

    def body(*refs):
        a_refs = refs[:na]
        w_ref, h_ref, g_ref, dres_ref, dh_ref, dhb_ref, dg_ref = refs[na:]
        i = pl.program_id(0)
        dn_v = None
        for which, lead, k0, k1 in parts:
            a_ref = a_refs[which]
            term = _dot(a_ref[...] if lead is None else a_ref[lead], w_ref[:, k0:k1], "nt")
            dn_v = term if dn_v is None else dn_v + term
        x = h_ref[...]
        r = lax.rsqrt(jnp.mean(x * x, axis=-1, keepdims=True) + NORM_EPS)
        xh = x * r
        dxh = dn_v * g_ref[...]
        dh = dres_ref[...] + r * (dxh - xh * jnp.mean(dxh * xh, axis=-1, keepdims=True))
        dh_ref[...] = dh
        dhb_ref[...] = dh.astype(BF16)
        part = jnp.sum(dn_v * xh, axis=0, keepdims=True)

        @pl.when(i == 0)
        def _():
            dg_ref[...] = part

        @pl.when(i > 0)
        def _():
            dg_ref[...] += part

    row = _row_spec(tm, D_MODEL)
    return pl.pallas_call(
        body, name=name, grid=(t // tm,),
        in_specs=list(a_specs) + [pl.BlockSpec(w.shape, lambda i: (0, 0), pipeline_mode=pl.Buffered(1)), row,
                                  _full_spec((1, D_MODEL)), row],
        out_specs=[row, row, _full_spec((1, D_MODEL))],
        out_shape=[jax.ShapeDtypeStruct((t, D_MODEL), F32), jax.ShapeDtypeStruct((t, D_MODEL), BF16),
                   jax.ShapeDtypeStruct((1, D_MODEL), F32)],
        compiler_params=_cparams(("arbitrary",)),
    )(*a_list, w, h, g, dres)


def _loss_and_grad(name, h, g, target, tm=512):
    t = h.shape[0]

    def body(h_ref, g_ref, t_ref, dh_ref, dhb_ref, dg_ref, loss_ref):
        i = pl.program_id(0)
        x = h_ref[...]
        gv = g_ref[...]
        r = lax.rsqrt(jnp.mean(x * x, axis=-1, keepdims=True) + NORM_EPS)
        xh = x * r
        err = xh * gv - t_ref[...]
        row = jnp.mean(err * err, axis=-1, keepdims=True)
        part_loss = 0.5 * jnp.sum(row, axis=0, keepdims=True)
        dy = err * (1.0 / D_MODEL)
        dxh = dy * gv
        dh = r * (dxh - xh * jnp.mean(dxh * xh, axis=-1, keepdims=True))
        dh_ref[...] = dh
        dhb_ref[...] = dh.astype(BF16)
        part = jnp.sum(dy * xh, axis=0, keepdims=True)

        @pl.when(i == 0)
        def _():
            dg_ref[...] = part
            loss_ref[...] = jnp.zeros(loss_ref.shape, F32) + part_loss

        @pl.when(i > 0)
        def _():
            dg_ref[...] += part
            loss_ref[...] += part_loss

    return pl.pallas_call(
        body, name=name, grid=(t // tm,),
        in_specs=[_row_spec(tm, D_MODEL), _full_spec((1, D_MODEL)), _row_spec(tm, D_MODEL)],
        out_specs=[_row_spec(tm, D_MODEL), _row_spec(tm, D_MODEL), _full_spec((1, D_MODEL)), _full_spec((8, 128))],
        out_shape=[jax.ShapeDtypeStruct((t, D_MODEL), F32), jax.ShapeDtypeStruct((t, D_MODEL), BF16),
                   jax.ShapeDtypeStruct((1, D_MODEL), F32), jax.ShapeDtypeStruct((8, 128), F32)],
        compiler_params=_cparams(("arbitrary",)),
    )(h, g, target)


def _rope_tables(t):
    half = HEAD_DIM // 2
    inv_freq = ROPE_BASE ** (-jnp.arange(half, dtype=F32) / half)
    ang = jnp.arange(t, dtype=F32)[:, None] * inv_freq[None, :]
    return jnp.cos(ang), jnp.sin(ang)


def _rotary_fwd(name, proj, cos, sin, tm=512):
    t = proj.shape[0]
    half = HEAD_DIM // 2
    k_scale = HEAD_DIM ** -0.5

    def body(q_ref, k_ref, v_ref, c_ref, s_ref, qo_ref, ko_ref, vo_ref):
        c, s = c_ref[...], s_ref[...]
        for hh in range(HEADS):
            lo, mid, hi = hh * HEAD_DIM, hh * HEAD_DIM + half, (hh + 1) * HEAD_DIM
            x1, x2 = q_ref[:, lo:mid], q_ref[:, mid:hi]
            qo_ref[:, lo:mid] = x1 * c - x2 * s
            qo_ref[:, mid:hi] = x1 * s + x2 * c
            x1, x2 = k_ref[:, lo:mid], k_ref[:, mid:hi]
            ko_ref[:, lo:mid] = (x1 * c - x2 * s) * k_scale
            ko_ref[:, mid:hi] = (x1 * s + x2 * c) * k_scale
        vo_ref[...] = v_ref[...].astype(BF16)

    return pl.pallas_call(
        body, name=name, grid=(t // tm,),
        in_specs=[_row_spec(tm, D_MODEL, 0), _row_spec(tm, D_MODEL, 1), _row_spec(tm, D_MODEL, 2),
                  _row_spec(tm, half), _row_spec(tm, half)],
        out_specs=[_row_spec(tm, D_MODEL)] * 3,
        out_shape=[jax.ShapeDtypeStruct((t, D_MODEL), F32), jax.ShapeDtypeStruct((t, D_MODEL), F32),
                   jax.ShapeDtypeStruct((t, D_MODEL), BF16)],
        compiler_params=_cparams(("parallel",)),
    )(proj, proj, proj, cos, sin)


def _rotary_bwd(name, dqr, dkr, cos, sin, tm=512):
    t = dqr.shape[0]
    half = HEAD_DIM // 2
    k_scale = HEAD_DIM ** -0.5

    def body(q_ref, k_ref, c_ref, s_ref, qo_ref, ko_ref):
        c, s = c_ref[...], s_ref[...]
        for hh in range(HEADS):
            lo, mid, hi = hh * HEAD_DIM, hh * HEAD_DIM + half, (hh + 1) * HEAD_DIM
            y1, y2 = q_ref[:, lo:mid], q_ref[:, mid:hi]
            qo_ref[:, lo:mid] = (y1 * c + y2 * s).astype(BF16)
            qo_ref[:, mid:hi] = (y2 * c - y1 * s).astype(BF16)
            y1, y2 = k_ref[:, lo:mid], k_ref[:, mid:hi]
            ko_ref[:, lo:mid] = ((y1 * c + y2 * s) * k_scale).astype(BF16)
            ko_ref[:, mid:hi] = ((y2 * c - y1 * s) * k_scale).astype(BF16)

    return pl.pallas_call(
        body, name=name, grid=(t // tm,),
        in_specs=[_row_spec(tm, D_MODEL), _row_spec(tm, D_MODEL), _row_spec(tm, half), _row_spec(tm, half)],
        out_specs=[_row_spec(tm, D_MODEL)] * 2,
        out_shape=[jax.ShapeDtypeStruct((t, D_MODEL), BF16)] * 2,
        compiler_params=_cparams(("parallel",)),
    )(dqr, dkr, cos, sin)


def _retention_tables():
    b, chunk = RET_BLOCK, 64
    gamma = 1.0 - 2.0 ** (-5.0 - np.arange(HEADS, dtype=np.float64))
    log_g = np.log(gamma)[:, None, None]
    i = np.arange(b)[:, None]
    j = np.arange(b)[None, :]
    same = (i // chunk) == (j // chunk)
    earlier = (j // chunk) < (i // chunk)
    expo = np.where(same, np.abs(i - j), np.where(earlier, i - j, 0)).astype(np.float64)
    mask = np.where(same | earlier, 1.0, 0.0)
    dmat = np.exp(log_g * expo[None]) * mask[None]
    qd = np.exp(log_g[:, :, 0] * (np.arange(b)[None, :] + 1.0))
    kd = np.exp(log_g[:, :, 0] * (b - 1.0 - np.arange(b)[None, :]))
    cd = np.exp(log_g[:, :, 0] * b) * np.ones((1, HEAD_DIM))
    as32 = lambda v: jnp.asarray(v.astype(np.float32))
    return (as32(dmat), as32(np.swapaxes(dmat, 1, 2)), as32(qd[:, :, None]), as32(kd[:, :, None]), as32(cd[:, None, :]))


def _dot(a, b, mode="nn"):
    return lax.dot_general(a, b, _DIMS[mode], preferred_element_type=F32)


GRET_BLOCK = 3


def _head_specs(nb, rev=False):
    pos = (lambda n: nb - 1 - n) if rev else (lambda n: n)
    tok = pl.BlockSpec((RET_BLOCK, D_MODEL), lambda n: (pos(n), 0))
    gr = pl.BlockSpec((RET_BLOCK, D_MODEL), lambda n: (pos(n), GRET_BLOCK))
    tab = _full_spec((HEADS, RET_BLOCK, RET_BLOCK))
    col = _full_spec((HEADS, RET_BLOCK, 1))
    rowv = _full_spec((HEADS, 1, HEAD_DIM))
    st = pl.BlockSpec((HEADS, None, HEAD_DIM, HEAD_DIM), lambda n: (0, pos(n), 0, 0))
    return tok, gr, tab, col, rowv, st


def _retention_fwd(name, qr, kr, vb, proj, tables):
    t = qr.shape[0]
    nb = t // RET_BLOCK
    dmat, _, qd, kd, cd = tables
    tok, gr, tab, col, rowv, st = _head_specs(nb)

    def body(q_ref, k_ref, v_ref, g_ref, d_ref, qd_ref, kd_ref, cd_ref, o_ref, ret_ref, st_ref, state):
        n = pl.program_id(0)

        @pl.when(n == 0)
        def _():
            state[...] = jnp.zeros(state.shape, F32)

        for hh in range(HEADS):
            sl = slice(hh * HEAD_DIM, (hh + 1) * HEAD_DIM)
            q, k, v = q_ref[:, sl], k_ref[:, sl], v_ref[:, sl]
            s = _dot(q.astype(BF16), k.astype(BF16), "nt") * d_ref[hh]
            stb = state[hh].astype(BF16)
            st_ref[hh] = stb
            o = _dot(s.astype(BF16), v) + _dot((q * qd_ref[hh]).astype(BF16), stb)
            o_ref[:, sl] = o
            rn = o * lax.rsqrt(jnp.mean(o * o, axis=-1, keepdims=True) + NORM_EPS)
            g = g_ref[:, sl]
            ret_ref[:, sl] = (rn * (g * _sigmoid(g))).astype(BF16)
            state[hh] = state[hh] * cd_ref[hh] + _dot((k * kd_ref[hh]).astype(BF16), v, "tn")

    return pl.pallas_call(
        body, name=name, grid=(nb,),
        in_specs=[tok, tok, tok, gr, tab, col, col, rowv],
        out_specs=[tok, tok, st],
        out_shape=[jax.ShapeDtypeStruct((t, D_MODEL), F32), jax.ShapeDtypeStruct((t, D_MODEL), BF16),
                   jax.ShapeDtypeStruct((HEADS, nb, HEAD_DIM, HEAD_DIM), BF16)],
        scratch_shapes=[pltpu.VMEM((HEADS, HEAD_DIM, HEAD_DIM), F32)],
        compiler_params=_cparams(("arbitrary",)),
    )(qr, kr, vb, proj, dmat, qd, kd, cd)


def _retention_bwd(name, dret, o, qr, kr, vb, proj, states, tables):
    t = qr.shape[0]
    nb = t // RET_BLOCK
    dmat, dmat_t, qd, kd, cd = tables
    tok, gr, tab, col, rowv, st = _head_specs(nb, rev=True)

    def body(dr_ref, o_ref, q_ref, k_ref, v_ref, g_ref, st_ref, d_ref, dt_ref, qd_ref, kd_ref, cd_ref,
             dq_ref, dk_ref, dv_ref, dg_ref, gstate):
        n = pl.program_id(0)

        @pl.when(n == 0)
        def _():
            gstate[...] = jnp.zeros(gstate.shape, F32)

        for hh in range(HEADS):
            sl = slice(hh * HEAD_DIM, (hh + 1) * HEAD_DIM)
            o_v, g, dr = o_ref[:, sl], g_ref[:, sl], dr_ref[:, sl]
            sg = _sigmoid(g)
            r = lax.rsqrt(jnp.mean(o_v * o_v, axis=-1, keepdims=True) + NORM_EPS)
            rn = o_v * r
            d_rn = dr * (g * sg)
            dg_ref[:, sl] = (dr * rn * (sg * (1.0 + g * (1.0 - sg)))).astype(BF16)
            d_o = r * (d_rn - rn * jnp.mean(d_rn * rn, axis=-1, keepdims=True))
            dob = d_o.astype(BF16)

            q, k, v = q_ref[:, sl], k_ref[:, sl], v_ref[:, sl]
            qb, kb = q.astype(BF16), k.astype(BF16)
            qdv, kdv = qd_ref[hh], kd_ref[hh]
            s_t = (_dot(kb, qb, "nt") * dt_ref[hh]).astype(BF16)
            p_t = (_dot(v, dob, "nt") * dt_ref[hh]).astype(BF16)
            p = (_dot(dob, v, "nt") * d_ref[hh]).astype(BF16)
            stb = st_ref[hh]
            gb = gstate[hh].astype(BF16)
            dq_ref[:, sl] = _dot(p, kb) + _dot(dob, stb, "nt") * qdv
            dk_ref[:, sl] = _dot(p_t, qb) + _dot(v, gb, "nt") * kdv
            dv_ref[:, sl] = (_dot(s_t, dob) + _dot((k * kdv).astype(BF16), gb)).astype(BF16)
            gstate[hh] = gstate[hh] * cd_ref[hh] + _dot((q * qdv).astype(BF16), dob, "tn")

    return pl.pallas_call(
        body, name=name, grid=(nb,),
        in_specs=[tok, tok, tok, tok, tok, gr, st, tab, tab, col, col, rowv],
        out_specs=[tok, tok, tok, tok],
        out_shape=[jax.ShapeDtypeStruct((t, D_MODEL), F32), jax.ShapeDtypeStruct((t, D_MODEL), F32),
                   jax.ShapeDtypeStruct((t, D_MODEL), BF16), jax.ShapeDtypeStruct((t, D_MODEL), BF16)],
        scratch_shapes=[pltpu.VMEM((HEADS, HEAD_DIM, HEAD_DIM), F32)],
        compiler_params=_cparams(("arbitrary",)),
    )(dret, o, qr, kr, vb, proj, states, dmat, dmat_t, qd, kd, cd)


POOL_TILE = 256


def _pool_tables():
    b = POOL_TILE
    tt = np.arange(b)[:, None]
    jj = np.arange(b)[None, :]
    cur, prev = [], []
    for w in POOL_WINDOWS:
        cur.append(((tt - jj >= 0) & (tt - jj <= w - 1)).astype(np.float32))
        prev.append((tt - (jj - b) <= w - 1).astype(np.float32))
    cur, prev = np.stack(cur), np.stack(prev)
    as16 = lambda v: jnp.asarray(v, dtype=BF16)
    return as16(cur), as16(prev), as16(np.swapaxes(cur, 1, 2)), as16(np.swapaxes(prev, 1, 2))


def _split2(x):
    hi = x.astype(BF16)
    return hi, (x - hi.astype(F32)).astype(BF16)


POOL_BLOCK = 4


def _pool_count(n, window):
    tpos = n * POOL_TILE + lax.broadcasted_iota(jnp.int32, (POOL_TILE, 1), 0)
    return jnp.minimum(tpos + 1, window).astype(F32)


def _pool_fwd(name, proj, pool_w, scale, tables):
    t = proj.shape[0]
    nb = t // POOL_TILE
    mc, mp, _, _ = tables
    tab = _full_spec((GROUPS, POOL_TILE, POOL_TILE))
    row = _row_spec(POOL_TILE, D_MODEL)

    def body(pc_ref, pp_ref, mc_ref, mp_ref, w_ref, sc_ref, pm_ref, mix_ref, po_ref):
        n = pl.program_id(0)
        for g, window in enumerate(POOL_WINDOWS):
            sl = slice(g * GROUP_DIM, (g + 1) * GROUP_DIM)
            p = pc_ref[:, sl]
            c_hi, c_lo = _split2(p)
            p_hi, p_lo = _split2(pp_ref[:, sl])
            mcv, mpv = mc_ref[g], mp_ref[g]
            win = _dot(mcv, c_hi) + _dot(mcv, c_lo)
            before = _dot(mpv, p_hi) + _dot(mpv, p_lo)
            win = win + jnp.where(n > 0, before, 0.0)
            pm = (win / _pool_count(n, window) - p).astype(BF16)
            pm_ref[:, sl] = pm
            mixed = _dot(pm, w_ref[g])
            mix_ref[:, sl] = mixed
            po_ref[:, sl] = (mixed * sc_ref[:, sl]).astype(BF16)

    return pl.pallas_call(
        body, name=name, grid=(nb,),
        in_specs=[_row_spec(POOL_TILE, D_MODEL, POOL_BLOCK),
                  pl.BlockSpec((POOL_TILE, D_MODEL), lambda n: (jnp.maximum(n - 1, 0), POOL_BLOCK)),
                  tab, tab, _full_spec((GROUPS, GROUP_DIM, GROUP_DIM)), _full_spec((1, D_MODEL))],
        out_specs=[row] * 3,
        out_shape=[jax.ShapeDtypeStruct((t, D_MODEL), BF16), jax.ShapeDtypeStruct((t, D_MODEL), F32),
                   jax.ShapeDtypeStruct((t, D_MODEL), BF16)],
        compiler_params=_cparams(("parallel",)),
    )(proj, proj, mc, mp, pool_w, scale)


def _pool_bwd(name, dpo, pm, mixed, pool_w, scale, tables):
    t = dpo.shape[0]
    nb = t // POOL_TILE
    _, _, mct, mpt = tables
    cur = _row_spec(POOL_TILE, D_MODEL)
    nxt = pl.BlockSpec((POOL_TILE, D_MODEL), lambda n: (jnp.minimum(n + 1, nb - 1), 0))
    tab = _full_spec((GROUPS, POOL_TILE, POOL_TILE))
    wspec = _full_spec((GROUPS, GROUP_DIM, GROUP_DIM))
    sspec = _full_spec((1, D_MODEL))

    def body(dc_ref, dn_ref, pm_ref, mix_ref, mct_ref, mpt_ref, w_ref, sc_ref, dp_ref, dw_ref, ds_ref):
        n = pl.program_id(0)

        @pl.when(n == 0)
        def _():
            dw_ref[...] = jnp.zeros(dw_ref.shape, F32)
            ds_ref[...] = jnp.zeros(ds_ref.shape, F32)

        for g, window in enumerate(POOL_WINDOWS):
            sl = slice(g * GROUP_DIM, (g + 1) * GROUP_DIM)
            dc, sc, w = dc_ref[:, sl], sc_ref[:, sl], w_ref[g]
            dmix_c = (dc * sc).astype(BF16)
            dmix_n = (dn_ref[:, sl] * sc).astype(BF16)
            dpm_c = _dot(dmix_c, w, "nt")
            dpm_n = _dot(dmix_n, w, "nt")
            e_hi, e_lo = _split2(dpm_c / _pool_count(n, window))
            f_hi, f_lo = _split2(dpm_n / _pool_count(n + 1, window))
            mctv, mptv = mct_ref[g], mpt_ref[g]
            back = _dot(mctv, e_hi) + _dot(mctv, e_lo)
            after = _dot(mptv, f_hi) + _dot(mptv, f_lo)
            dp_ref[:, sl] = (back + jnp.where(n < nb - 1, after, 0.0) - dpm_c).astype(BF16)
            dw_ref[g] += _dot(pm_ref[:, sl], dmix_c, "tn")
            ds_ref[:, sl] += jnp.sum(dc * mix_ref[:, sl], axis=0, keepdims=True)

    return pl.pallas_call(
        body, name=name, grid=(nb,),
        in_specs=[cur, nxt, cur, cur, tab, tab, wspec, sspec],
        out_specs=[cur, wspec, sspec],
        out_shape=[jax.ShapeDtypeStruct((t, D_MODEL), BF16), jax.ShapeDtypeStruct((GROUPS, GROUP_DIM, GROUP_DIM), F32),
                   jax.ShapeDtypeStruct((1, D_MODEL), F32)],
        compiler_params=_cparams(("arbitrary",)),
    )(dpo, dpo, pm, mixed, mct, mpt, pool_w, scale)


GATE0_BLOCK, GATE1_BLOCK = 5, 6


def _merge_fwd(name, ret, po, w_ru, w_pu, proj, bias, tm=512):
    t = ret.shape[0]

    def body(r_ref, p_ref, wr_ref, wp_ref, g0_ref, g1_ref, b_ref, m_ref, ru_ref, pu_ref):
        ru = _dot(r_ref[...], wr_ref[...])
        pu = _dot(p_ref[...], wp_ref[...])
        ru_ref[...] = ru
        pu_ref[...] = pu
        m_ref[...] = (_sigmoid(g0_ref[...] + b_ref[0:1, :]) * ru + _sigmoid(g1_ref[...] + b_ref[1:2, :]) * pu).astype(BF16)

    row = _row_spec(tm, D_MODEL)
    wspec = _full_spec((D_MODEL, D_MODEL))
    return pl.pallas_call(
        body, name=name, grid=(t // tm,),
        in_specs=[row, row, wspec, wspec, _row_spec(tm, D_MODEL, GATE0_BLOCK), _row_spec(tm, D_MODEL, GATE1_BLOCK),
                  _full_spec((2, D_MODEL))],
        out_specs=[row, row, row],
        out_shape=[jax.ShapeDtypeStruct((t, D_MODEL), BF16), jax.ShapeDtypeStruct((t, D_MODEL), F32),
                   jax.ShapeDtypeStruct((t, D_MODEL), F32)],
        compiler_params=_cparams(("parallel",)),
    )(ret, po, w_ru, w_pu, proj, proj, bias)


def _merge_bwd(name, dm, ru, pu, proj, bias, tm=512):
    t = dm.shape[0]

    def body(dm_ref, ru_ref, pu_ref, g0_ref, g1_ref, b_ref, dru_ref, dpu_ref, dg0_ref, dg1_ref, db_ref):
        i = pl.program_id(0)
        d = dm_ref[...]
        s0 = _sigmoid(g0_ref[...] + b_ref[0:1, :])
        s1 = _sigmoid(g1_ref[...] + b_ref[1:2, :])
        dru_ref[...] = (d * s0).astype(BF16)
        dpu_ref[...] = (d * s1).astype(BF16)
        dg0 = d * ru_ref[...] * (s0 * (1.0 - s0))
        dg1 = d * pu_ref[...] * (s1 * (1.0 - s1))
        dg0_ref[...] = dg0.astype(BF16)
        dg1_ref[...] = dg1.astype(BF16)
        part0 = jnp.sum(dg0, axis=0, keepdims=True)
        part1 = jnp.sum(dg1, axis=0, keepdims=True)

        @pl.when(i == 0)
        def _():
            db_ref[0:1, :] = part0
            db_ref[1:2, :] = part1

        @pl.when(i > 0)
        def _():
            db_ref[0:1, :] += part0
            db_ref[1:2, :] += part1

    row = _row_spec(tm, D_MODEL)
    return pl.pallas_call(
        body, name=name, grid=(t // tm,),
        in_specs=[row, row, row, _row_spec(tm, D_MODEL, GATE0_BLOCK), _row_spec(tm, D_MODEL, GATE1_BLOCK),
                  _full_spec((2, D_MODEL))],
        out_specs=[row, row, row, row, _full_spec((2, D_MODEL))],
        out_shape=[jax.ShapeDtypeStruct((t, D_MODEL), BF16)] * 4 + [jax.ShapeDtypeStruct((2, D_MODEL), F32)],
        compiler_params=_cparams(("arbitrary",)),
    )(dm, ru, pu, proj, proj, bias)


def _half_scale(acc):
    return (FFN_RES_WEIGHT * acc,)


def _residual_half(acc, res):
    return (res + FFN_RES_WEIGHT * acc,)


def _normed(h, g):
    return h * lax.rsqrt(jnp.mean(h * h, axis=-1, keepdims=True) + NORM_EPS) * g


def _residual_half_norm(acc, res, g):
    h = res + FFN_RES_WEIGHT * acc
    return h, _normed(h, g)


def _residual_norm(acc, res, g):
    h = res + acc
    return h, _normed(h, g)


FF_TILE = D_FF // 2
DW_TILE = 256
FF_CHUNKS = ((0, 512), (512, 1024), (1024, FF_TILE))


def _ffn_in(name, nrm, w_in, tm=512):
    t = nrm.shape[0]
    nj = D_FF // FF_TILE

    def body(n_ref, wg_ref, wu_ref, a_ref, mid_ref):
        nv = n_ref[...]
        for c0, c1 in FF_CHUNKS:
            gate = _dot(nv, wg_ref[:, c0:c1])
            up = _dot(nv, wu_ref[:, c0:c1])
            a_ref[0, :, c0:c1] = gate
            a_ref[1, :, c0:c1] = up
            mid_ref[:, c0:c1] = (gate * _sigmoid(gate) * up).astype(BF16)

    return pl.pallas_call(
        body, name=name, grid=(nj, t // tm),
        in_specs=[pl.BlockSpec((tm, D_MODEL), lambda j, i: (i, 0)),
                  pl.BlockSpec((D_MODEL, FF_TILE), lambda j, i: (0, j)),
                  pl.BlockSpec((D_MODEL, FF_TILE), lambda j, i: (0, j + nj))],
        out_specs=[pl.BlockSpec((2, tm, FF_TILE), lambda j, i: (0, i, j)), pl.BlockSpec((tm, FF_TILE), lambda j, i: (i, j))],
        out_shape=[jax.ShapeDtypeStruct((2, t, D_FF), F32), jax.ShapeDtypeStruct((t, D_FF), BF16)],
        compiler_params=_cparams(("parallel", "parallel")),
    )(nrm, w_in, w_in)


def _ffn_dact(name, dout_b, w_out, a, tm=512):
    t = dout_b.shape[0]

    def body(d_ref, w_ref, a_ref, da_ref):
        dv = d_ref[...]
        for c0, c1 in FF_CHUNKS:
            dm = FFN_RES_WEIGHT * _dot(dv, w_ref[c0:c1, :], "nt")
            gate, up = a_ref[0, :, c0:c1], a_ref[1, :, c0:c1]
            s = _sigmoid(gate)
            da_ref[0, :, c0:c1] = (dm * up * (s * (1.0 + gate * (1.0 - s)))).astype(BF16)
            da_ref[1, :, c0:c1] = (dm * (gate * s)).astype(BF16)

    blk = pl.BlockSpec((2, tm, FF_TILE), lambda j, i: (0, i, j))
    return pl.pallas_call(
        body, name=name, grid=(D_FF // FF_TILE, t // tm),
        in_specs=[pl.BlockSpec((tm, D_MODEL), lambda j, i: (i, 0)), pl.BlockSpec((FF_TILE, D_MODEL), lambda j, i: (j, 0)), blk],
        out_specs=blk,
        out_shape=jax.ShapeDtypeStruct((2, t, D_FF), BF16),
        compiler_params=_cparams(("parallel", "parallel")),
    )(dout_b, w_out, a)


def _ffn_fwd(tag, h, nrm, get_w_in, get_w_out, next_g=None):
    t = h.shape[0]
    w_in = get_w_in(nrm)
    a, mid = _ffn_in(f"{tag}_in", nrm, w_in, tm=min(512, t))
    w_out = get_w_out(mid)
    if next_g is None:
        out = _matmul(f"{tag}_out", mid, w_out, "nn", t, D_MODEL, D_FF, 512, D_MODEL, D_FF, [F32],
                      extras=(h,), epilogue=_residual_half)
        nxt = None
    else:
        out, nxt = _matmul(f"{tag}_out", mid, w_out, "nn", t, D_MODEL, D_FF, 512, D_MODEL, D_FF, [F32, BF16],
                           extras=(h,), consts=(next_g,), epilogue=_residual_half_norm)
    return out, nxt, (nrm, a, mid, w_in, w_out)


def _ffn_bwd(tag, h, g, saved, dout, dout_b, on_grads, flush):
    t = h.shape[0]
    nrm, a, mid, w_in, w_out = saved
    d_w_out = _matmul(f"{tag}_dwout", mid, dout_b, "tn", D_FF, D_MODEL, t, DW_TILE, D_MODEL, t, [BF16], epilogue=_half_scale,
                      resident="b")
    da = _ffn_dact(f"{tag}_dact", dout_b, w_out, a, tm=min(512, t))
    nj = D_FF // DW_TILE
    d_w_in = _matmul(f"{tag}_dwin", nrm, da, "tn", D_MODEL, 2 * D_FF, t, D_MODEL, DW_TILE, t, [BF16], resident="a",
                     b_spec=pl.BlockSpec((None, t, DW_TILE), lambda i, j, kk: (j // nj, 0, j % nj)))
    tie = on_grads({f"{tag}_w_in": d_w_in, f"{tag}_w_out": d_w_out})
    tm = min(256, t)
    dh, dh_b, dg = _proj_norm_bwd(f"{tag}_dn", [da], [pl.BlockSpec((2, tm, D_FF), lambda i: (0, i, 0))],
                                  ((0, 0, 0, D_FF), (0, 1, D_FF, 2 * D_FF)), w_in, h, g if tie is None else g + tie, dout, tm)
    return dh, dh_b, dg, flush(dh)


def _mix_dwin(name, u, pieces, tn=256):
    t = u.shape[0]
    nj = D_MODEL // tn
    npc = len(pieces)

    def body(*refs):
        u_ref, p_refs, o_ref = refs[0], refs[1:1 + npc], refs[1 + npc]
        s = pl.program_id(0)
        for which in range(npc):
            @pl.when(s // nj == which)
            def _(which=which):
                o_ref[...] = _dot(u_ref[...], p_refs[which][...], "tn").astype(BF16)

    def piece_spec(which):
        return pl.BlockSpec((t, tn), lambda s: (0, jnp.clip(s - which * nj, 0, nj - 1)))

    return pl.pallas_call(
        body, name=name, grid=(npc * nj,),
        in_specs=[pl.BlockSpec((t, D_MODEL), lambda s: (0, 0), pipeline_mode=pl.Buffered(1))] + [piece_spec(k) for k in range(npc)],
        out_specs=pl.BlockSpec((D_MODEL, tn), lambda s: (0, s)),
        out_shape=jax.ShapeDtypeStruct((D_MODEL, npc * D_MODEL), BF16),
        compiler_params=_cparams(("parallel",)),
    )(u, *pieces)


def _local_step(x, target, vec, get_w, on_grads, flush):
    t = x.shape[0]
    cos, sin = _rope_tables(t)
    rtab = _retention_tables()
    ptab = _pool_tables()
    w = {}

    def getter(group, name):
        def get(after):
            if name not in w:
                w.update(get_w(group, after))
            return w[name]
        return get

    nrm1 = _rmsnorm_fwd("ffn1_norm", x, vec["norm_ffn1"])
    h1, u, s1 = _ffn_fwd("ffn1", x, nrm1, getter(0, "ffn1_w_in"), getter(1, "ffn1_w_out"), vec["norm_mix"])
    w.update(get_w(2, u))
    proj = _matmul("mix_in", u, w["w_in"], "nn", t, IN_WIDTH, D_MODEL, 1024, 1024, D_MODEL, [F32], n_outer=True)
    qr, kr, vb = _rotary_fwd("rotary", proj, cos, sin)
    o, ret, states = _retention_fwd("retention", qr, kr, vb, proj, rtab)
    pm, mixed, po = _pool_fwd("pool", proj, w["pool_w"], vec["pool_scale"], ptab)
    merged, ru, pu = _merge_fwd("merge", ret, po, w["w_ret_up"], w["w_pool_up"], proj, vec["gate_bias"])
    h2, nrm2 = _matmul("mix_out", merged, w["w_out"], "nn", t, D_MODEL, D_MODEL, 512, D_MODEL, D_MODEL, [F32, BF16],
                       extras=(h1,), consts=(vec["norm_ffn2"],), epilogue=_residual_norm)
    h3, _, s2 = _ffn_fwd("ffn2", h2, nrm2, getter(3, "ffn2_w_in"), getter(3, "ffn2_w_out"))
    dh3, dh3_b, dg_final, loss = _loss_and_grad("loss", h3, vec["norm_final"], target)

    def tied(v, tie):
        return v if tie is None else v + tie

    dh2, dh2_b, dg_ffn2, tie = _ffn_bwd("ffn2", h2, vec["norm_ffn2"], s2, dh3, dh3_b, on_grads, flush)
    dm = _matmul("mix_dmerged", dh2_b, w["w_out"], "nt", t, D_MODEL, D_MODEL, 1024, D_MODEL, D_MODEL, [F32])
    d_w_out = _matmul("mix_dwout", merged, dh2_b, "tn", D_MODEL, D_MODEL, t, D_MODEL, D_MODEL, 1024, [BF16])
    dru, dpu, dg0, dg1, d_bias = _merge_bwd("merge_bwd", dm, ru, pu, proj, tied(vec["gate_bias"], tie))
    dret = _matmul("mix_dret", dru, w["w_ret_up"], "nt", t, D_MODEL, D_MODEL, 1024, D_MODEL, D_MODEL, [F32])
    d_w_ru = _matmul("mix_dwru", ret, dru, "tn", D_MODEL, D_MODEL, t, D_MODEL, D_MODEL, 1024, [BF16])
    dpo = _matmul("mix_dpool", dpu, w["w_pool_up"], "nt", t, D_MODEL, D_MODEL, 1024, D_MODEL, D_MODEL, [F32])
    d_w_pu = _matmul("mix_dwpu", po, dpu, "tn", D_MODEL, D_MODEL, t, D_MODEL, D_MODEL, 1024, [BF16])
    dp, d_pool_w, d_scale = _pool_bwd("pool_bwd", dpo, pm, mixed, w["pool_w"], vec["pool_scale"], ptab)
    dqr, dkr, dv, dgr = _retention_bwd("retention_bwd", dret, o, qr, kr, vb, proj, states, rtab)
    dq, dk = _rotary_bwd("rotary_bwd", dqr, dkr, cos, sin)
    dproj = [dq, dk, dv, dgr, dp, dg0, dg1]
    d_w_in = _mix_dwin("mix_dwin", u, dproj)
    tie = on_grads(dict(w_in=d_w_in, pool_w=d_pool_w.astype(BF16), w_ret_up=d_w_ru, w_pool_up=d_w_pu, w_out=d_w_out))
    tm = min(256, t)
    dh1, dh1_b, dg_mix = _proj_norm_bwd("mix_du", dproj, [_row_spec(tm, D_MODEL)] * len(dproj),
                                        [(k, None, k * D_MODEL, (k + 1) * D_MODEL) for k in range(len(dproj))],
                                        w["w_in"], h1, tied(vec["norm_mix"], tie), dh2, tm)
    tie = flush(dh1)
    dx, _, dg_ffn1, _ = _ffn_bwd("ffn1", x, tied(vec["norm_ffn1"], tie), s1, dh1, dh1_b, on_grads, flush)

    small = dict(norm_ffn1=dg_ffn1, norm_mix=dg_mix, gate_bias=d_bias, pool_scale=d_scale, norm_ffn2=dg_ffn2,
                 norm_final=dg_final)
    return loss[0, 0], dx, small


BIG = ("ffn1_w_in", "ffn1_w_out", "w_in", "pool_w", "w_ret_up", "w_pool_up", "w_out", "ffn2_w_in", "ffn2_w_out")
KIND = dict(ffn1_w_in="col", ffn1_w_out="row", w_in="col", pool_w="pool", w_ret_up="row", w_pool_up="row", w_out="row",
            ffn2_w_in="col", ffn2_w_out="row")
ANY = pl.BlockSpec(memory_space=pl.ANY)


def _place():
    x, y, c = lax.axis_index("x"), lax.axis_index("y"), lax.axis_index("c")
    chips = [(1 - x, y), (x, 1 - y), (1 - x, 1 - y)]
    return x, y, c, chips


def _full_view_shape(kind, local_shape):
    if kind == "col":
        return (2, local_shape[0] // 2, N_CHIPS * local_shape[1])
    if kind == "row":
        return (N_CHIPS, 2, local_shape[0] // 2, local_shape[1])
    return (GROUPS, N_CHIPS, 2, local_shape[1] // 2, local_shape[2])


def _local_view(kind, arr):
    if kind == "pool":
        return arr.reshape(GROUPS, 2, arr.shape[1] // 2, arr.shape[2])
    return arr.reshape(2, arr.shape[0] // 2, arr.shape[1])


def _blk(kind, ref, s, c):
    if kind == "col":
        cs = ref.shape[2] // N_CHIPS
        return ref.at[c, :, pl.ds(pl.multiple_of(s * cs, 128), cs)]
    if kind == "row":
        return ref.at[s, c]
    return ref.at[:, s, c]


def _half(kind, ref, c):
    return ref.at[:, c] if kind == "pool" else ref.at[c]


def _shard(kind, ref, s):
    if kind == "col":
        cs = ref.shape[2] // N_CHIPS
        return ref.at[:, :, pl.ds(pl.multiple_of(s * cs, 128), cs)]
    if kind == "row":
        return ref.at[s]
    return ref.at[:, s]


HBM = pl.BlockSpec(memory_space=pltpu.HBM)
SEM = pl.BlockSpec(memory_space=pltpu.SEMAPHORE)
EFFECT = pltpu.SideEffectType.DATAFLOW_SIDE_EFFECTING
WEIGHT_GROUPS = (("ffn1_w_in",), ("ffn1_w_out",), ("w_in", "pool_w", "w_ret_up", "w_pool_up", "w_out"), ("ffn2_w_in", "ffn2_w_out"))
GRAD_GROUPS = (("ffn2_w_in", "ffn2_w_out"), ("w_in", "pool_w", "w_ret_up", "w_pool_up", "w_out"), ("ffn1_w_in", "ffn1_w_out"))


def _hbm(a):
    return pltpu.with_memory_space_constraint(a, pltpu.HBM)


def _natural(kind, o):
    if kind == "col":
        return o.reshape(o.shape[0] * o.shape[1], o.shape[2])
    if kind == "row":
        return o.reshape(-1, o.shape[3])
    return o.reshape(GROUPS, -1, o.shape[4])


def _ici_copy(kind, loc, full, j, chips, s, c, send_sem, recv_sem):
    px, py = chips[j]
    return (pltpu.make_async_remote_copy(src_ref=_half(kind, loc, c), dst_ref=_blk(kind, full, s, c), send_sem=send_sem,
                                         recv_sem=recv_sem, device_id=(px, py, c), device_id_type=MESH),
            pltpu.make_async_remote_copy(src_ref=_half(kind, loc, c), dst_ref=_blk(kind, full, 2 * px + py, c), send_sem=send_sem,
                                         recv_sem=recv_sem, device_id=(px, py, c), device_id_type=MESH))


def _gather_start(tag, group_ids, shards):
    grps = [WEIGHT_GROUPS[g] for g in group_ids]
    names = [nm for grp in grps for nm in grp]
    kinds = [KIND[nm] for nm in names]
    n, ng = len(names), len(grps)
    locs = [_hbm(_local_view(KIND[nm], shards[nm])) for nm in names]
    lands = [_hbm(lax.empty(_full_view_shape(KIND[nm], shards[nm].shape), BF16)) for nm in names]
    first = np.cumsum([0] + [len(grp) for grp in grps])

    def body(*refs):
        loc, full = refs[:n], refs[n:2 * n]
        send_sems, recv_sems = refs[2 * n:2 * n + ng], refs[2 * n + ng:2 * n + 2 * ng]
        token = refs[-1]
        x, y, c, chips = _place()
        s = 2 * x + y
        for g in range(ng):
            for a in range(first[g], first[g + 1]):
                for j in range(3):
                    k = 3 * (a - first[g]) + j
                    _ici_copy(kinds[a], loc[a], full[a], j, chips, s, c, send_sems[g].at[k], recv_sems[g].at[k])[0].start()
        token[...] = jnp.zeros(token.shape, F32)

    sem_shapes = [pltpu.SemaphoreType.DMA((3 * len(grp),)) for grp in grps]
    outs = pl.pallas_call(
        body, name=f"gather_start_{tag}",
        in_specs=[HBM] * (2 * n),
        out_specs=[SEM] * (2 * ng) + [HBM] * (2 * n) + [pl.BlockSpec(memory_space=pltpu.VMEM)],
        out_shape=sem_shapes + sem_shapes + [pltpu.HBM(a.shape, a.dtype) for a in locs + lands] + [jax.ShapeDtypeStruct((8, 128), F32)],
        input_output_aliases={i: 2 * ng + i for i in range(2 * n)},
        compiler_params=pltpu.CompilerParams(has_side_effects=EFFECT),
    )(*locs, *lands)
    send_sems, recv_sems = outs[:ng], outs[ng:2 * ng]
    locs_t, lands_t = outs[2 * ng:2 * ng + n], outs[2 * ng + n:2 * ng + 2 * n]
    groups = {}
    for k, g in enumerate(group_ids):
        sl = slice(first[k], first[k + 1])
        groups[g] = (send_sems[k], recv_sems[k], list(locs_t[sl]), list(lands_t[sl]))
    return groups, outs[-1]


def _gather_finish(g, group, after):
    names = WEIGHT_GROUPS[g]
    kinds = [KIND[nm] for nm in names]
    m = len(names)
    send_sem, recv_sem, locs, lands = group

    def wait_body(*refs):
        loc, full = refs[:m], refs[m:2 * m]
        send_sems, recv_sems = refs[2 * m], refs[2 * m + 1]
        x, y, c, chips = _place()
        s = 2 * x + y
        for a in range(m):
            for j in range(3):
                k = 3 * a + j
                sent, landed = _ici_copy(kinds[a], loc[a], full[a], j, chips, s, c, send_sems.at[k], recv_sems.at[k])
                sent.wait_send()
                landed.wait_recv()

    outs = pl.pallas_call(
        wait_body, name=f"gather_wait_{g}",
        in_specs=[HBM] * (2 * m) + [SEM, SEM] + [ANY] * len(after), out_specs=[HBM] * (2 * m),
        out_shape=[pltpu.HBM(a.shape, a.dtype) for a in locs + lands],
        input_output_aliases={i: i for i in range(2 * m)},
        compiler_params=pltpu.CompilerParams(has_side_effects=EFFECT),
    )(*locs, *lands, send_sem, recv_sem, *after)
    locs, lands = outs[:m], outs[m:]

    def forward_body(*refs):
        loc, full = refs[:m], refs[2 * m:3 * m]
        send_sems, recv_sems = refs[3 * m:]
        x, y, c, chips = _place()
        s = 2 * x + y
        sib = (x, y, 1 - c)

        def remote(a, k, src, dst):
            return pltpu.make_async_remote_copy(src_ref=src, dst_ref=dst, send_sem=send_sems.at[4 * a + k],
                                                recv_sem=recv_sems.at[4 * a + k], device_id=sib, device_id_type=MESH)

        sends = []
        for a in range(m):
            for j, (px, py) in enumerate(chips):
                theirs = _blk(kinds[a], full[a], 2 * px + py, c)
                sends.append(remote(a, j, theirs, theirs))
            sends.append(remote(a, 3, loc[a], _shard(kinds[a], full[a], s)))
        for cp in sends:
            cp.start()
        for a in range(m):
            for j, (px, py) in enumerate(chips):
                from_sib = _blk(kinds[a], full[a], 2 * px + py, 1 - c)
                remote(a, j, from_sib, from_sib).wait_recv()
            own = _shard(kinds[a], full[a], s)
            remote(a, 3, own, own).wait_recv()
        for cp in sends:
            cp.wait_send()

    outs = pl.pallas_call(
        forward_body, name=f"gather_forward_{g}",
        in_specs=[ANY] * (2 * m), out_specs=[ANY] * m,
        out_shape=[jax.ShapeDtypeStruct(a.shape, a.dtype) for a in lands],
        input_output_aliases={m + i: i for i in range(m)},
        scratch_shapes=[pltpu.SemaphoreType.DMA((4 * m,)), pltpu.SemaphoreType.DMA((4 * m,))],
    )(*locs, *lands)
    return {nm: _natural(k, o) for nm, k, o in zip(names, kinds, outs)}


def _grad_view(kind, g):
    if kind == "col":
        return g.reshape(2, g.shape[0] // 2, g.shape[1])
    if kind == "row":
        return g.reshape(N_CHIPS, 2, g.shape[0] // (2 * N_CHIPS), g.shape[1])
    return g.reshape(GROUPS, N_CHIPS, 2, g.shape[1] // (2 * N_CHIPS), g.shape[2])


def _pair_copies(kinds, g, got, send_sems, recv_sems):
    x, y, c, _ = _place()

    def other_half(kind, ref):
        if kind == "col":
            return ref.at[1 - c]
        if kind == "row":
            return ref.at[:, 1 - c]
        return ref.at[:, :, 1 - c]

    return [pltpu.make_async_remote_copy(src_ref=other_half(kinds[a], g[a]), dst_ref=got[a], send_sem=send_sems.at[a],
                                         recv_sem=recv_sems.at[a], device_id=(x, y, 1 - c), device_id_type=MESH)
            for a in range(len(kinds))]


def _pair_exchange_start(tag, names, views):
    kinds = [KIND[nm] for nm in names]
    n = len(names)

    def got_shape(kind, v):
        if kind == "col":
            return v.shape[1:]
        if kind == "row":
            return (v.shape[0],) + v.shape[2:]
        return v.shape[:2] + v.shape[3:]

    srcs = [_hbm(views[nm]) for nm in names]
    lands = [_hbm(lax.empty(got_shape(k, views[nm]), BF16)) for nm, k in zip(names, kinds)]

    def body(*refs):
        g, got = refs[:n], refs[n:2 * n]
        for cp in _pair_copies(kinds, g, got, refs[2 * n], refs[2 * n + 1]):
            cp.start()
        refs[-1][...] = jnp.zeros(refs[-1].shape, F32)

    sem_shape = pltpu.SemaphoreType.DMA((n,))
    outs = pl.pallas_call(
        body, name=f"grad_pair_exchange_start_{tag}",
        in_specs=[HBM] * (2 * n),
        out_specs=[SEM, SEM] + [HBM] * (2 * n) + [pl.BlockSpec(memory_space=pltpu.VMEM)],
        out_shape=[sem_shape, sem_shape] + [pltpu.HBM(a.shape, a.dtype) for a in srcs + lands] + [jax.ShapeDtypeStruct((8, 128), F32)],
        input_output_aliases={i: 2 + i for i in range(2 * n)},
        compiler_params=pltpu.CompilerParams(has_side_effects=EFFECT),
    )(*srcs, *lands)
    return (outs[0], outs[1], list(outs[2:2 + n]), list(outs[2 + n:2 + 2 * n])), outs[-1]


def _pair_exchange_wait(tag, names, state, after):
    kinds = [KIND[nm] for nm in names]
    n = len(names)
    send_sem, recv_sem, srcs, lands = state

    def body(*refs):
        g, got = refs[:n], refs[n:2 * n]
        for cp in _pair_copies(kinds, g, got, refs[2 * n], refs[2 * n + 1]):
            cp.wait_send()
            cp.wait_recv()

    outs = pl.pallas_call(
        body, name=f"grad_pair_exchange_wait_{tag}",
        in_specs=[HBM] * (2 * n) + [SEM, SEM, ANY], out_specs=[HBM] * (2 * n),
        out_shape=[pltpu.HBM(a.shape, a.dtype) for a in srcs + lands],
        input_output_aliases={i: i for i in range(2 * n)},
        compiler_params=pltpu.CompilerParams(has_side_effects=EFFECT),
    )(*srcs, *lands, send_sem, recv_sem, after)
    return dict(zip(names, outs[:n])), dict(zip(names, outs[n:]))


def _pair_sum(name, kind, view, got, c_arr):
    if kind == "col":
        _, rows, cols = view.shape
        tr = 128
        grid = (rows // tr,)
        v_spec = pl.BlockSpec((None, tr, cols), lambda i, c: (c[0], i, 0))
        g_spec = pl.BlockSpec((tr, cols), lambda i, c: (i, 0))
    elif kind == "row":
        _, _, rows, cols = view.shape
        grid = (N_CHIPS,)
        v_spec = pl.BlockSpec((None, None, rows, cols), lambda i, c: (i, c[0], 0, 0))
        g_spec = pl.BlockSpec((None, rows, cols), lambda i, c: (i, 0, 0))
    else:
        _, _, _, rows, cols = view.shape
        grid = (GROUPS,)
        v_spec = pl.BlockSpec((None, N_CHIPS, None, rows, cols), lambda i, c: (i, 0, c[0], 0, 0))
        g_spec = pl.BlockSpec((None, N_CHIPS, rows, cols), lambda i, c: (i, 0, 0, 0))

    def body(c_ref, v_ref, g_ref, o_ref):
        o_ref[...] = (v_ref[...].astype(F32) + g_ref[...].astype(F32)).astype(BF16)

    return pl.pallas_call(
        body, name=name,
        grid_spec=pltpu.PrefetchScalarGridSpec(num_scalar_prefetch=1, grid=grid, in_specs=[v_spec, g_spec], out_specs=g_spec),
        out_shape=jax.ShapeDtypeStruct(got.shape, BF16),
        compiler_params=_cparams(("parallel",)),
    )(c_arr, view, got)


def _piece(kind, ref, s):
    if kind == "col":
        cs = ref.shape[1] // N_CHIPS
        return ref.at[:, pl.ds(pl.multiple_of(s * cs, 128), cs)]
    if kind == "row":
        return ref.at[s]
    return ref.at[:, s]


def _piece_shape(kind, shape):
    if kind == "col":
        return (shape[0], shape[1] // N_CHIPS)
    if kind == "row":
        return shape[1:]
    return (shape[0],) + shape[2:]


def _shard_copies(kinds, p, got, send_sems, recv_sems):
    x, y, c, chips = _place()
    return [pltpu.make_async_remote_copy(src_ref=_piece(kinds[a], p[a], 2 * px + py), dst_ref=got[a].at[j],
                                         send_sem=send_sems.at[3 * a + j], recv_sem=recv_sems.at[3 * a + j],
                                         device_id=(px, py, c), device_id_type=MESH)
            for a in range(len(kinds)) for j, (px, py) in enumerate(chips)]


def _shard_exchange_start(g, names, psums):
    kinds = [KIND[nm] for nm in names]
    n = len(names)
    srcs = [_hbm(psums[nm]) for nm in names]
    lands = [_hbm(lax.empty((3,) + _piece_shape(k, psums[nm].shape), BF16)) for nm, k in zip(names, kinds)]

    def body(*refs):
        p, got = refs[:n], refs[n:2 * n]
        send_sems, recv_sems = refs[2 * n], refs[2 * n + 1]
        token = refs[-1]
        for cp in _shard_copies(kinds, p, got, send_sems, recv_sems):
            cp.start()
        token[...] = jnp.zeros(token.shape, F32)

    sem_shape = pltpu.SemaphoreType.DMA((3 * n,))
    outs = pl.pallas_call(
        body, name=f"grad_shard_exchange_start_{g}",
        in_specs=[HBM] * (2 * n),
        out_specs=[SEM, SEM] + [HBM] * (2 * n) + [pl.BlockSpec(memory_space=pltpu.VMEM)],
        out_shape=[sem_shape, sem_shape] + [pltpu.HBM(a.shape, a.dtype) for a in srcs + lands] + [jax.ShapeDtypeStruct((8, 128), F32)],
        input_output_aliases={i: 2 + i for i in range(2 * n)},
        compiler_params=pltpu.CompilerParams(has_side_effects=EFFECT),
    )(*srcs, *lands)
    return (outs[0], outs[1], list(outs[2:2 + n]), list(outs[2 + n:2 + 2 * n])), outs[-1]


def _shard_exchange_wait(g, names, state, after):
    kinds = [KIND[nm] for nm in names]
    n = len(names)
    send_sem, recv_sem, srcs, lands = state

    def body(*refs):
        p, got = refs[:n], refs[n:2 * n]
        for cp in _shard_copies(kinds, p, got, refs[2 * n], refs[2 * n + 1]):
            cp.wait_send()
            cp.wait_recv()

    outs = pl.pallas_call(
        body, name=f"grad_shard_exchange_wait_{g}",
        in_specs=[HBM] * (2 * n) + [SEM, SEM, ANY], out_specs=[HBM] * (2 * n),
        out_shape=[pltpu.HBM(a.shape, a.dtype) for a in srcs + lands],
        input_output_aliases={i: i for i in range(2 * n)},
        compiler_params=pltpu.CompilerParams(has_side_effects=EFFECT),
    )(*srcs, *lands, send_sem, recv_sem, after)
    return dict(zip(names, outs[:n])), dict(zip(names, outs[n:]))


def _shard_sum(name, kind, psum, got, sc_arr):
    if kind == "col":
        rows, cols = psum.shape
        cs = cols // N_CHIPS
        tr = 128
        grid = (rows // tr,)
        p_spec = pl.BlockSpec((tr, cs), lambda i, sc: (i, sc[0]))
        g_spec = pl.BlockSpec((3, tr, cs), lambda i, sc: (0, i, 0))
        o_spec = pl.BlockSpec((None, tr, cs), lambda i, sc: (sc[1], i, 0))
        out_shape = (2, rows, cs)
    elif kind == "row":
        _, rows, cols = psum.shape
        grid = (1,)
        p_spec = pl.BlockSpec((None, rows, cols), lambda i, sc: (sc[0], 0, 0))
        g_spec = pl.BlockSpec((3, rows, cols), lambda i, sc: (0, 0, 0))
        o_spec = pl.BlockSpec((None, rows, cols), lambda i, sc: (sc[1], 0, 0))
        out_shape = (2, rows, cols)
    else:
        _, _, rows, cols = psum.shape
        grid = (1,)
        p_spec = pl.BlockSpec((GROUPS, None, rows, cols), lambda i, sc: (0, sc[0], 0, 0))
        g_spec = pl.BlockSpec((3, GROUPS, rows, cols), lambda i, sc: (0, 0, 0, 0))
        o_spec = pl.BlockSpec((GROUPS, None, rows, cols), lambda i, sc: (0, sc[1], 0, 0))
        out_shape = (GROUPS, 2, rows, cols)

    def body(sc_ref, p_ref, g_ref, o_ref):
        o_ref[...] = ((p_ref[...].astype(F32) + g_ref[0].astype(F32)) + g_ref[1].astype(F32)) + g_ref[2].astype(F32)

    return pl.pallas_call(
        body, name=name,
        grid_spec=pltpu.PrefetchScalarGridSpec(num_scalar_prefetch=1, grid=grid, in_specs=[p_spec, g_spec], out_specs=o_spec),
        out_shape=jax.ShapeDtypeStruct(out_shape, F32),
        compiler_params=_cparams(("parallel",)),
    )(sc_arr, psum, got)


def _half_exchange(tag, names, bufs):
    kinds = [KIND[nm] for nm in names]
    n = len(names)

    def body(*refs):
        out = refs[n:2 * n]
        send_sems, recv_sems = refs[2 * n:]
        x, y, c, _ = _place()
        sib = (x, y, 1 - c)
        cps = []
        for a in range(n):
            mine = _half(kinds[a], out[a], c)
            cp = pltpu.make_async_remote_copy(src_ref=mine, dst_ref=mine, send_sem=send_sems.at[a], recv_sem=recv_sems.at[a],
                                              device_id=sib, device_id_type=MESH)
            cp.start()
            cps.append(cp)
        for a, cp in enumerate(cps):
            cp.wait_send()
            theirs = _half(kinds[a], out[a], 1 - c)
            pltpu.make_async_remote_copy(src_ref=theirs, dst_ref=theirs, send_sem=send_sems.at[a], recv_sem=recv_sems.at[a],
                                         device_id=sib, device_id_type=MESH).wait_recv()

    outs = pl.pallas_call(
        body, name=f"grad_half_exchange_{tag}",
        in_specs=[ANY] * n, out_specs=[ANY] * n,
        out_shape=[jax.ShapeDtypeStruct(bufs[nm].shape, F32) for nm in names],
        input_output_aliases={a: a for a in range(n)},
        scratch_shapes=[pltpu.SemaphoreType.DMA((n,)), pltpu.SemaphoreType.DMA((n,))],
    )(*[bufs[nm] for nm in names])
    return dict(zip(names, outs))


N_DEV = 8
SMALL_ROWS = 8


def _all_reduce_small(name, v):
    def body(v_ref, o_ref, buf, send_sems, recv_sems):
        x, y, c, _ = _place()
        me = 4 * x + 2 * y + c
        buf[me] = v_ref[...]
        cps = []
        for r in range(1, N_DEV):
            to = (x ^ (r >> 2), y ^ ((r >> 1) & 1), c ^ (r & 1))
            cp = pltpu.make_async_remote_copy(src_ref=v_ref, dst_ref=buf.at[me], send_sem=send_sems.at[r - 1],
                                              recv_sem=recv_sems.at[r - 1], device_id=to, device_id_type=MESH)
            cp.start()
            cps.append(cp)
        for r in range(1, N_DEV):
            pltpu.make_async_remote_copy(src_ref=v_ref, dst_ref=buf.at[me ^ r], send_sem=send_sems.at[r - 1],
                                         recv_sem=recv_sems.at[r - 1], device_id=(x, y, c), device_id_type=MESH).wait_recv()
        for cp in cps:
            cp.wait_send()
        acc = buf[0]
        for d in range(1, N_DEV):
            acc = acc + buf[d]
        o_ref[...] = acc

    vm = pl.BlockSpec(memory_space=pltpu.VMEM)
    return pl.pallas_call(
        body, name=name, in_specs=[vm], out_specs=vm,
        out_shape=jax.ShapeDtypeStruct((SMALL_ROWS, D_MODEL), F32),
        scratch_shapes=[pltpu.VMEM((N_DEV, SMALL_ROWS, D_MODEL), F32), pltpu.SemaphoreType.DMA((N_DEV - 1,)),
                        pltpu.SemaphoreType.DMA((N_DEV - 1,))],
    )(v)


def _adamw(name, w, g, m, v):
    rows, cols = w.shape
    tr = next((c for c in (256, 176, 128, 64, 32, 8) if rows % c == 0), rows)
    spec = pl.BlockSpec((tr, cols), lambda i: (i, 0))

    def body(w_ref, g_ref, m_ref, v_ref, d_ref, mo_ref, vo_ref):
        gv = g_ref[...]
        m_new = ADAM_B1 * m_ref[...] + (1.0 - ADAM_B1) * gv
        v_new = ADAM_B2 * v_ref[...] + (1.0 - ADAM_B2) * jnp.square(gv)
        m_hat = m_new / (1.0 - ADAM_B1 ** ADAM_STEP)
        v_hat = v_new / (1.0 - ADAM_B2 ** ADAM_STEP)
        d_ref[...] = -ADAM_LR * (m_hat / (jnp.sqrt(v_hat) + ADAM_EPS) + ADAM_WD * w_ref[...])
        mo_ref[...] = m_new
        vo_ref[...] = v_new

    return pl.pallas_call(
        body, name=name, grid=(rows // tr,),
        in_specs=[spec] * 4, out_specs=[spec] * 3,
        out_shape=[jax.ShapeDtypeStruct((rows, cols), F32)] * 3,
        compiler_params=_cparams(("parallel",)),
    )(w, g, m, v)


WEIGHTS = ("norm_ffn1", "ffn1_w_in", "ffn1_w_out", "norm_mix", "w_in", "gate_bias", "pool_w", "pool_scale", "w_ret_up",
           "w_pool_up", "w_out", "norm_ffn2", "ffn2_w_in", "ffn2_w_out", "norm_final")
SMALL_ROW = dict(norm_ffn1=0, norm_mix=1, gate_bias=2, pool_scale=4, norm_ffn2=5, norm_final=6)


def _as2d(a):
    return a.reshape(-1, a.shape[-1])


def kernel(x, norm_ffn1, ffn1_w_in, ffn1_w_out, norm_mix, w_in, gate_bias, pool_w, pool_scale, w_ret_up, w_pool_up, w_out, norm_ffn2, ffn2_w_in, ffn2_w_out, norm_final, loss_target, m_norm_ffn1, m_ffn1_w_in, m_ffn1_w_out, m_norm_mix, m_w_in, m_gate_bias, m_pool_w, m_pool_scale, m_w_ret_up, m_w_pool_up, m_w_out, m_norm_ffn2, m_ffn2_w_in, m_ffn2_w_out, m_norm_final, v_norm_ffn1, v_ffn1_w_in, v_ffn1_w_out, v_norm_mix, v_w_in, v_gate_bias, v_pool_w, v_pool_scale, v_w_ret_up, v_w_pool_up, v_w_out, v_norm_ffn2, v_ffn2_w_in, v_ffn2_w_out, v_norm_final):
    wt = dict(norm_ffn1=norm_ffn1, ffn1_w_in=ffn1_w_in, ffn1_w_out=ffn1_w_out, norm_mix=norm_mix, w_in=w_in, gate_bias=gate_bias,
              pool_w=pool_w, pool_scale=pool_scale, w_ret_up=w_ret_up, w_pool_up=w_pool_up, w_out=w_out, norm_ffn2=norm_ffn2,
              ffn2_w_in=ffn2_w_in, ffn2_w_out=ffn2_w_out, norm_final=norm_final)
    mom = dict(norm_ffn1=m_norm_ffn1, ffn1_w_in=m_ffn1_w_in, ffn1_w_out=m_ffn1_w_out, norm_mix=m_norm_mix, w_in=m_w_in,
               gate_bias=m_gate_bias, pool_w=m_pool_w, pool_scale=m_pool_scale, w_ret_up=m_w_ret_up, w_pool_up=m_w_pool_up,
               w_out=m_w_out, norm_ffn2=m_norm_ffn2, ffn2_w_in=m_ffn2_w_in, ffn2_w_out=m_ffn2_w_out, norm_final=m_norm_final)
    var = dict(norm_ffn1=v_norm_ffn1, ffn1_w_in=v_ffn1_w_in, ffn1_w_out=v_ffn1_w_out, norm_mix=v_norm_mix, w_in=v_w_in,
               gate_bias=v_gate_bias, pool_w=v_pool_w, pool_scale=v_pool_scale, w_ret_up=v_w_ret_up, w_pool_up=v_w_pool_up,
               w_out=v_w_out, norm_ffn2=v_norm_ffn2, ffn2_w_in=v_ffn2_w_in, ffn2_w_out=v_ffn2_w_out, norm_final=v_norm_final)

    ax, ay, ac = lax.axis_index("x"), lax.axis_index("y"), lax.axis_index("c")
    chip = 2 * ax + ay
    c_arr = jnp.reshape(ac, (1,)).astype(jnp.int32)
    sc_arr = jnp.stack([chip, ac]).astype(jnp.int32)
    bias_cols = gate_bias.shape[-1]

    first = WEIGHT_GROUPS[0]
    gather_groups, token = _gather_start("first", [0], {nm: wt[nm][0].astype(BF16) for nm in first})
    rest, _ = _gather_start("rest", [1, 2, 3], {nm: wt[nm][0].astype(BF16) + token[0, 0].astype(BF16) for nm in BIG if nm not in first})
    gather_groups.update(rest)
    placed = lax.dynamic_update_slice(jnp.zeros((SMALL_ROWS, D_MODEL), F32), gate_bias[0], (0, chip * bias_cols)) + token[0, 0]
    bias_full = _all_reduce_small("gather_gate_bias", jnp.where(ac == 0, placed, 0.0))[:2]
    vec = dict(norm_ffn1=norm_ffn1, norm_mix=norm_mix, norm_ffn2=norm_ffn2, pool_scale=pool_scale,
               norm_final=norm_final.reshape(1, D_MODEL), gate_bias=bias_full)

    def get_w(g, after):
        return _gather_finish(g, gather_groups[g], (after, bias_full) if g == 0 else (after,))

    pairs, pending = [], []

    def on_grads(gr):
        g = len(pairs)
        names = GRAD_GROUPS[g]
        assert set(names) == set(gr), (names, list(gr))
        state, token = _pair_exchange_start(g, names, {nm: _grad_view(KIND[nm], gr[nm]) for nm in names})
        pairs.append(state)
        return token[0:1, 0:1]

    def flush(after):
        g = len(pending)
        names = GRAD_GROUPS[g]
        views, from_sib = _pair_exchange_wait(g, names, pairs[g], after)
        psums = {nm: _pair_sum(f"pair_sum_{nm}", KIND[nm], views[nm], from_sib[nm], c_arr) for nm in names}
        state, token = _shard_exchange_start(g, names, psums)
        pending.append(state)
        return token[0:1, 0:1]

    loss_local, dx, small = _local_step(x[0], loss_target[0], vec, get_w, on_grads, flush)

    packed = jnp.concatenate([small["norm_ffn1"], small["norm_mix"], small["gate_bias"], small["pool_scale"],
                              small["norm_ffn2"], small["norm_final"], jnp.broadcast_to(loss_local, (1, D_MODEL))], axis=0)
    small_sum = _all_reduce_small("reduce_small_grads", packed)
    loss = small_sum[SMALL_ROWS - 1, 0]
    grads, delta, new_m, new_v = {}, {}, {}, {}

    def adamw(nm):
        shape = wt[nm].shape
        d, m2, v2 = _adamw(f"adamw_{nm}", _as2d(wt[nm]), _as2d(grads[nm]), _as2d(mom[nm]), _as2d(var[nm]))
        delta[nm], new_m[nm], new_v[nm] = d.reshape(shape), m2.reshape(shape), v2.reshape(shape)
        return d

    for nm in ("norm_ffn1", "norm_mix", "pool_scale", "norm_ffn2"):
        grads[nm] = small_sum[SMALL_ROW[nm]][None, :]
    grads["norm_final"] = small_sum[SMALL_ROW["norm_final"]]
    grads["gate_bias"] = lax.dynamic_slice(small_sum, (SMALL_ROW["gate_bias"], chip * bias_cols), (2, bias_cols))[None]
    after = dx
    for g, names in enumerate(GRAD_GROUPS):
        psums, from_chips = _shard_exchange_wait(g, names, pending[g], after)
        bufs = {nm: _shard_sum(f"shard_sum_{nm}", KIND[nm], psums[nm], from_chips[nm], sc_arr) for nm in names}
        reduced = _half_exchange(g, names, bufs)
        for nm in names:
            grads[nm] = reduced[nm].reshape(wt[nm].shape)
            after = adamw(nm)
    for nm in WEIGHTS:
        if nm not in delta:
            adamw(nm)

    return (loss, dx[None], *[grads[nm] for nm in WEIGHTS], *[delta[nm] for nm in WEIGHTS],
            *[new_m[nm] for nm in WEIGHTS], *[new_v[nm] for nm in WEIGHTS])
```

```python
import functools

import numpy as np
import jax
import jax.numpy as jnp
from jax import lax
from jax.experimental import pallas as pl
from jax.experimental.pallas import tpu as pltpu

F32 = jnp.float32
BF16 = jnp.bfloat16
MESH = pl.DeviceIdType.MESH

D_MODEL = 1024
D_FF = 2816
HEADS = 4
HEAD_DIM = 256
GROUPS = 4
GROUP_DIM = 256
POOL_WINDOWS = (2, 4, 8, 16)
IN_WIDTH = 7 * D_MODEL
ROPE_BASE = 10000.0
NORM_EPS = 1e-6
FFN_RES_WEIGHT = 0.5
ADAM_LR, ADAM_B1, ADAM_B2, ADAM_EPS, ADAM_WD, ADAM_STEP = 0.001, 0.9, 0.999, 1e-08, 0.01, 10

N_CHIPS = 4
RET_BLOCK = 256
V7X_VMEM_LIMIT = 48 * 1024 * 1024


def _cparams(sem):
    return pltpu.CompilerParams(dimension_semantics=sem, vmem_limit_bytes=V7X_VMEM_LIMIT)


def _sigmoid(x):
    return jax.nn.sigmoid(x)


_DIMS = {"nn": (((1,), (0,)), ((), ())), "nt": (((1,), (1,)), ((), ())), "tn": (((0,), (0,)), ((), ()))}


def _matmul(name, a, b, mode, m, n, k, tm, tn, tk, out_dtypes, a_spec=None, b_spec=None, extras=(), consts=(), epilogue=None,
            resident=None, n_outer=False):
    tm, tn, tk = min(tm, m), min(tn, n), min(tk, k)
    gi, gj, gk = m // tm, n // tn, k // tk
    assert gi * tm == m and gj * tn == n and gk * tk == k, (name, m, n, k, tm, tn, tk)
    assert not (n_outer and (a_spec is not None or b_spec is not None)), name
    once = dict(pipeline_mode=pl.Buffered(1))

    def spec(shape, index, **kw):
        return pl.BlockSpec(shape, (lambda j, i, kk: index(i, j, kk)) if n_outer else index, **kw)

    if a_spec is None:
        kw = once if resident == "a" else {}
        a_spec = (spec((tk, tm), lambda i, j, kk: (kk, i), **kw) if mode == "tn"
                  else spec((tm, tk), lambda i, j, kk: (i, kk), **kw))
    if b_spec is None:
        kw = once if resident == "b" else {}
        b_spec = (spec((tn, tk), lambda i, j, kk: (j, kk), **kw) if mode == "nt"
                  else spec((tk, tn), lambda i, j, kk: (kk, j), **kw))
    n_ex, n_out = len(extras) + len(consts), len(out_dtypes)
    dims = _DIMS[mode]

    def body(a_ref, b_ref, *rest):
        ex_refs, out_refs = rest[:n_ex], rest[n_ex:n_ex + n_out]

        def finish(acc):
            outs = (acc,) if epilogue is None else epilogue(acc, *[e[...] for e in ex_refs])
            for o_ref, o in zip(out_refs, outs):
                o_ref[...] = o.astype(o_ref.dtype)

        prod = lax.dot_general(a_ref[...], b_ref[...], dims, preferred_element_type=F32)
        if gk == 1:
            finish(prod)
        else:
            acc_ref = rest[n_ex + n_out]
            kk = pl.program_id(2)

            @pl.when(kk == 0)
            def _():
                acc_ref[...] = prod

            @pl.when(kk > 0)
            def _():
                acc_ref[...] += prod

            @pl.when(kk == gk - 1)
            def _():
                finish(acc_ref[...])

    o_spec = spec((tm, tn), lambda i, j, kk: (i, j))
    outs = pl.pallas_call(
        body, name=name, grid=(gj, gi, gk) if n_outer else (gi, gj, gk),
        in_specs=[a_spec, b_spec] + [o_spec] * len(extras) + [spec((1, tn), lambda i, j, kk: (0, j))] * len(consts),
        out_specs=[o_spec] * n_out,
        out_shape=[jax.ShapeDtypeStruct((m, n), dt) for dt in out_dtypes],
        scratch_shapes=[pltpu.VMEM((tm, tn), F32)] if gk > 1 else [],
        compiler_params=_cparams(("parallel", "parallel", "arbitrary")),
    )(a, b, *extras, *consts)
    return outs[0] if n_out == 1 else outs


def _row_spec(tm, width, col_block=0):
    return pl.BlockSpec((tm, width), lambda i: (i, col_block))


def _full_spec(shape):
    return pl.BlockSpec(shape, lambda *_: (0,) * len(shape))


def _rmsnorm_fwd(name, h, g, tm=512):
    t = h.shape[0]

    def body(h_ref, g_ref, o_ref):
        x = h_ref[...]
        r = lax.rsqrt(jnp.mean(x * x, axis=-1, keepdims=True) + NORM_EPS)
        o_ref[...] = (x * r * g_ref[...]).astype(BF16)

    return pl.pallas_call(
        body, name=name, grid=(t // tm,),
        in_specs=[_row_spec(tm, D_MODEL), _full_spec((1, D_MODEL))],
        out_specs=_row_spec(tm, D_MODEL),
        out_shape=jax.ShapeDtypeStruct((t, D_MODEL), BF16),
        compiler_params=_cparams(("parallel",)),
    )(h, g)


def _proj_norm_bwd(name, a_list, a_specs, parts, w, h, g, dres, tm):
    t = h.shape[0]
    na = len(a_list)

    def body(*refs):
        a_refs = refs[:na]
        w_ref, h_ref, g_ref, dres_ref, dh_ref, dhb_ref, dg_ref = refs[na:]
        i = pl.program_id(0)
        dn_v = None
        for which, lead, k0, k1 in parts:
            a_ref = a_refs[which]
            term = _dot(a_ref[...] if lead is None else a_ref[lead], w_ref[:, k0:k1], "nt")
            dn_v = term if dn_v is None else dn_v + term
        x = h_ref[...]
        r = lax.rsqrt(jnp.mean(x * x, axis=-1, keepdims=True) + NORM_EPS)
        xh = x * r
        dxh = dn_v * g_ref[...]
        dh = dres_ref[...] + r * (dxh - xh * jnp.mean(dxh * xh, axis=-1, keepdims=True))
        dh_ref[...] = dh
        dhb_ref[...] = dh.astype(BF16)
        part = jnp.sum(dn_v * xh, axis=0, keepdims=True)

        @pl.when(i == 0)
        def _():
            dg_ref[...] = part

        @pl.when(i > 0)
        def _():
            dg_ref[...] += part

    row = _row_spec(tm, D_MODEL)
    return pl.pallas_call(
        body, name=name, grid=(t // tm,),
        in_specs=list(a_specs) + [pl.BlockSpec(w.shape, lambda i: (0, 0), pipeline_mode=pl.Buffered(1)), row,
                                  _full_spec((1, D_MODEL)), row],
        out_specs=[row, row, _full_spec((1, D_MODEL))],
        out_shape=[jax.ShapeDtypeStruct((t, D_MODEL), F32), jax.ShapeDtypeStruct((t, D_MODEL), BF16),
                   jax.ShapeDtypeStruct((1, D_MODEL), F32)],
        compiler_params=_cparams(("arbitrary",)),
    )(*a_list, w, h, g, dres)


def _loss_and_grad(name, h, g, target, tm=512):
    t = h.shape[0]

    def body(h_ref, g_ref, t_ref, dh_ref, dhb_ref, dg_ref, loss_ref):
        i = pl.program_id(0)
        x = h_ref[...]
        gv = g_ref[...]
        r = lax.rsqrt(jnp.mean(x * x, axis=-1, keepdims=True) + NORM_EPS)
        xh = x * r
        err = xh * gv - t_ref[...]
        row = jnp.mean(err * err, axis=-1, keepdims=True)
        part_loss = 0.5 * jnp.sum(row, axis=0, keepdims=True)
        dy = err * (1.0 / D_MODEL)
        dxh = dy * gv
        dh = r * (dxh - xh * jnp.mean(dxh * xh, axis=-1, keepdims=True))
        dh_ref[...] = dh
        dhb_ref[...] = dh.astype(BF16)
        part = jnp.sum(dy * xh, axis=0, keepdims=True)

        @pl.when(i == 0)
        def _():
            dg_ref[...] = part
            loss_ref[...] = jnp.zeros(loss_ref.shape, F32) + part_loss

        @pl.when(i > 0)
        def _():
            dg_ref[...] += part
            loss_ref[...] += part_loss

    return pl.pallas_call(
        body, name=name, grid=(t // tm,),
        in_specs=[_row_spec(tm, D_MODEL), _full_spec((1, D_MODEL)), _row_spec(tm, D_MODEL)],
        out_specs=[_row_spec(tm, D_MODEL), _row_spec(tm, D_MODEL), _full_spec((1, D_MODEL)), _full_spec((8, 128))],
        out_shape=[jax.ShapeDtypeStruct((t, D_MODEL), F32), jax.ShapeDtypeStruct((t, D_MODEL), BF16),
                   jax.ShapeDtypeStruct((1, D_MODEL), F32), jax.ShapeDtypeStruct((8, 128), F32)],
        compiler_params=_cparams(("arbitrary",)),
    )(h, g, target)


def _rope_tables(t):
    half = HEAD_DIM // 2
    inv_freq = ROPE_BASE ** (-jnp.arange(half, dtype=F32) / half)
    ang = jnp.arange(t, dtype=F32)[:, None] * inv_freq[None, :]
    return jnp.cos(ang), jnp.sin(ang)


def _rotary_fwd(name, proj, cos, sin, tm=512):
    t = proj.shape[0]
    half = HEAD_DIM // 2
    k_scale = HEAD_DIM ** -0.5

    def body(q_ref, k_ref, v_ref, c_ref, s_ref, qo_ref, ko_ref, vo_ref):
        c, s = c_ref[...], s_ref[...]
        for hh in range(HEADS):
            lo, mid, hi = hh * HEAD_DIM, hh * HEAD_DIM + half, (hh + 1) * HEAD_DIM
            x1, x2 = q_ref[:, lo:mid], q_ref[:, mid:hi]
            qo_ref[:, lo:mid] = x1 * c - x2 * s
            qo_ref[:, mid:hi] = x1 * s + x2 * c
            x1, x2 = k_ref[:, lo:mid], k_ref[:, mid:hi]
            ko_ref[:, lo:mid] = (x1 * c - x2 * s) * k_scale
            ko_ref[:, mid:hi] = (x1 * s + x2 * c) * k_scale
        vo_ref[...] = v_ref[...].astype(BF16)

    return pl.pallas_call(
        body, name=name, grid=(t // tm,),
        in_specs=[_row_spec(tm, D_MODEL, 0), _row_spec(tm, D_MODEL, 1), _row_spec(tm, D_MODEL, 2),
                  _row_spec(tm, half), _row_spec(tm, half)],
        out_specs=[_row_spec(tm, D_MODEL)] * 3,
        out_shape=[jax.ShapeDtypeStruct((t, D_MODEL), F32), jax.ShapeDtypeStruct((t, D_MODEL), F32),
                   jax.ShapeDtypeStruct((t, D_MODEL), BF16)],
        compiler_params=_cparams(("parallel",)),
    )(proj, proj, proj, cos, sin)


def _rotary_bwd(name, dqr, dkr, cos, sin, tm=512):
    t = dqr.shape[0]
    half = HEAD_DIM // 2
    k_scale = HEAD_DIM ** -0.5

    def body(q_ref, k_ref, c_ref, s_ref, qo_ref, ko_ref):
        c, s = c_ref[...], s_ref[...]
        for hh in range(HEADS):
            lo, mid, hi = hh * HEAD_DIM, hh * HEAD_DIM + half, (hh + 1) * HEAD_DIM
            y1, y2 = q_ref[:, lo:mid], q_ref[:, mid:hi]
            qo_ref[:, lo:mid] = (y1 * c + y2 * s).astype(BF16)
            qo_ref[:, mid:hi] = (y2 * c - y1 * s).astype(BF16)
            y1, y2 = k_ref[:, lo:mid], k_ref[:, mid:hi]
            ko_ref[:, lo:mid] = ((y1 * c + y2 * s) * k_scale).astype(BF16)
            ko_ref[:, mid:hi] = ((y2 * c - y1 * s) * k_scale).astype(BF16)

    return pl.pallas_call(
        body, name=name, grid=(t // tm,),
        in_specs=[_row_spec(tm, D_MODEL), _row_spec(tm, D_MODEL), _row_spec(tm, half), _row_spec(tm, half)],
        out_specs=[_row_spec(tm, D_MODEL)] * 2,
        out_shape=[jax.ShapeDtypeStruct((t, D_MODEL), BF16)] * 2,
        compiler_params=_cparams(("parallel",)),
    )(dqr, dkr, cos, sin)


def _retention_tables():
    b, chunk = RET_BLOCK, 64
    gamma = 1.0 - 2.0 ** (-5.0 - np.arange(HEADS, dtype=np.float64))
    log_g = np.log(gamma)[:, None, None]
    i = np.arange(b)[:, None]
    j = np.arange(b)[None, :]
    same = (i // chunk) == (j // chunk)
    earlier = (j // chunk) < (i // chunk)
    expo = np.where(same, np.abs(i - j), np.where(earlier, i - j, 0)).astype(np.float64)
    mask = np.where(same | earlier, 1.0, 0.0)
    dmat = np.exp(log_g * expo[None]) * mask[None]
    qd = np.exp(log_g[:, :, 0] * (np.arange(b)[None, :] + 1.0))
    kd = np.exp(log_g[:, :, 0] * (b - 1.0 - np.arange(b)[None, :]))
    cd = np.exp(log_g[:, :, 0] * b) * np.ones((1, HEAD_DIM))
    as32 = lambda v: jnp.asarray(v.astype(np.float32))
    return (as32(dmat), as32(np.swapaxes(dmat, 1, 2)), as32(qd[:, :, None]), as32(kd[:, :, None]), as32(cd[:, None, :]))


def _dot(a, b, mode="nn"):
    return lax.dot_general(a, b, _DIMS[mode], preferred_element_type=F32)


GRET_BLOCK = 3


def _head_specs(nb, rev=False):
    pos = (lambda n: nb - 1 - n) if rev else (lambda n: n)
    tok = pl.BlockSpec((RET_BLOCK, D_MODEL), lambda n: (pos(n), 0))
    gr = pl.BlockSpec((RET_BLOCK, D_MODEL), lambda n: (pos(n), GRET_BLOCK))
    tab = _full_spec((HEADS, RET_BLOCK, RET_BLOCK))
    col = _full_spec((HEADS, RET_BLOCK, 1))
    rowv = _full_spec((HEADS, 1, HEAD_DIM))
    st = pl.BlockSpec((HEADS, None, HEAD_DIM, HEAD_DIM), lambda n: (0, pos(n), 0, 0))
    return tok, gr, tab, col, rowv, st


def _retention_fwd(name, qr, kr, vb, proj, tables):
    t = qr.shape[0]
    nb = t // RET_BLOCK
    dmat, _, qd, kd, cd = tables
    tok, gr, tab, col, rowv, st = _head_specs(nb)

    def body(q_ref, k_ref, v_ref, g_ref, d_ref, qd_ref, kd_ref, cd_ref, o_ref, ret_ref, st_ref, state):
        n = pl.program_id(0)

        @pl.when(n == 0)
        def _():
            state[...] = jnp.zeros(state.shape, F32)

        for hh in range(HEADS):
            sl = slice(hh * HEAD_DIM, (hh + 1) * HEAD_DIM)
            q, k, v = q_ref[:, sl], k_ref[:, sl], v_ref[:, sl]
            s = _dot(q.astype(BF16), k.astype(BF16), "nt") * d_ref[hh]
            stb = state[hh].astype(BF16)
            st_ref[hh] = stb
            o = _dot(s.astype(BF16), v) + _dot((q * qd_ref[hh]).astype(BF16), stb)
            o_ref[:, sl] = o
            rn = o * lax.rsqrt(jnp.mean(o * o, axis=-1, keepdims=True) + NORM_EPS)
            g = g_ref[:, sl]
            ret_ref[:, sl] = (rn * (g * _sigmoid(g))).astype(BF16)
            state[hh] = state[hh] * cd_ref[hh] + _dot((k * kd_ref[hh]).astype(BF16), v, "tn")

    return pl.pallas_call(
        body, name=name, grid=(nb,),
        in_specs=[tok, tok, tok, gr, tab, col, col, rowv],
        out_specs=[tok, tok, st],
        out_shape=[jax.ShapeDtypeStruct((t, D_MODEL), F32), jax.ShapeDtypeStruct((t, D_MODEL), BF16),
                   jax.ShapeDtypeStruct((HEADS, nb, HEAD_DIM, HEAD_DIM), BF16)],
        scratch_shapes=[pltpu.VMEM((HEADS, HEAD_DIM, HEAD_DIM), F32)],
        compiler_params=_cparams(("arbitrary",)),
    )(qr, kr, vb, proj, dmat, qd, kd, cd)


def _retention_bwd(name, dret, o, qr, kr, vb, proj, states, tables):
    t = qr.shape[0]
    nb = t // RET_BLOCK
    dmat, dmat_t, qd, kd, cd = tables
    tok, gr, tab, col, rowv, st = _head_specs(nb, rev=True)

    def body(dr_ref, o_ref, q_ref, k_ref, v_ref, g_ref, st_ref, d_ref, dt_ref, qd_ref, kd_ref, cd_ref,
             dq_ref, dk_ref, dv_ref, dg_ref, gstate):
        n = pl.program_id(0)

        @pl.when(n == 0)
        def _():
            gstate[...] = jnp.zeros(gstate.shape, F32)

        for hh in range(HEADS):
            sl = slice(hh * HEAD_DIM, (hh + 1) * HEAD_DIM)
            o_v, g, dr = o_ref[:, sl], g_ref[:, sl], dr_ref[:, sl]
            sg = _sigmoid(g)
            r = lax.rsqrt(jnp.mean(o_v * o_v, axis=-1, keepdims=True) + NORM_EPS)
            rn = o_v * r
            d_rn = dr * (g * sg)
            dg_ref[:, sl] = (dr * rn * (sg * (1.0 + g * (1.0 - sg)))).astype(BF16)
            d_o = r * (d_rn - rn * jnp.mean(d_rn * rn, axis=-1, keepdims=True))
            dob = d_o.astype(BF16)

            q, k, v = q_ref[:, sl], k_ref[:, sl], v_ref[:, sl]
            qb, kb = q.astype(BF16), k.astype(BF16)
            qdv, kdv = qd_ref[hh], kd_ref[hh]
            s_t = (_dot(kb, qb, "nt") * dt_ref[hh]).astype(BF16)
            p_t = (_dot(v, dob, "nt") * dt_ref[hh]).astype(BF16)
            p = (_dot(dob, v, "nt") * d_ref[hh]).astype(BF16)
            stb = st_ref[hh]
            gb = gstate[hh].astype(BF16)
            dq_ref[:, sl] = _dot(p, kb) + _dot(dob, stb, "nt") * qdv
            dk_ref[:, sl] = _dot(p_t, qb) + _dot(v, gb, "nt") * kdv
            dv_ref[:, sl] = (_dot(s_t, dob) + _dot((k * kdv).astype(BF16), gb)).astype(BF16)
            gstate[hh] = gstate[hh] * cd_ref[hh] + _dot((q * qdv).astype(BF16), dob, "tn")

    return pl.pallas_call(
        body, name=name, grid=(nb,),
        in_specs=[tok, tok, tok, tok, tok, gr, st, tab, tab, col, col, rowv],
        out_specs=[tok, tok, tok, tok],
        out_shape=[jax.ShapeDtypeStruct((t, D_MODEL), F32), jax.ShapeDtypeStruct((t, D_MODEL), F32),
                   jax.ShapeDtypeStruct((t, D_MODEL), BF16), jax.ShapeDtypeStruct((t, D_MODEL), BF16)],
        scratch_shapes=[pltpu.VMEM((HEADS, HEAD_DIM, HEAD_DIM), F32)],
        compiler_params=_cparams(("arbitrary",)),
    )(dret, o, qr, kr, vb, proj, states, dmat, dmat_t, qd, kd, cd)


POOL_TILE = 256


def _pool_tables():
    b = POOL_TILE
    tt = np.arange(b)[:, None]
    jj = np.arange(b)[None, :]
    cur, prev = [], []
    for w in POOL_WINDOWS:
        cur.append(((tt - jj >= 0) & (tt - jj <= w - 1)).astype(np.float32))
        prev.append((tt - (jj - b) <= w - 1).astype(np.float32))
    cur, prev = np.stack(cur), np.stack(prev)
    as16 = lambda v: jnp.asarray(v, dtype=BF16)
    return as16(cur), as16(prev), as16(np.swapaxes(cur, 1, 2)), as16(np.swapaxes(prev, 1, 2))


def _split2(x):
    hi = x.astype(BF16)
    return hi, (x - hi.astype(F32)).astype(BF16)


POOL_BLOCK = 4


def _pool_count(n, window):
    tpos = n * POOL_TILE + lax.broadcasted_iota(jnp.int32, (POOL_TILE, 1), 0)
    return jnp.minimum(tpos + 1, window).astype(F32)


def _pool_fwd(name, proj, pool_w, scale, tables):
    t = proj.shape[0]
    nb = t // POOL_TILE
    mc, mp, _, _ = tables
    tab = _full_spec((GROUPS, POOL_TILE, POOL_TILE))
    row = _row_spec(POOL_TILE, D_MODEL)

    def body(pc_ref, pp_ref, mc_ref, mp_ref, w_ref, sc_ref, pm_ref, mix_ref, po_ref):
        n = pl.program_id(0)
        for g, window in enumerate(POOL_WINDOWS):
            sl = slice(g * GROUP_DIM, (g + 1) * GROUP_DIM)
            p = pc_ref[:, sl]
            c_hi, c_lo = _split2(p)
            p_hi, p_lo = _split2(pp_ref[:, sl])
            mcv, mpv = mc_ref[g], mp_ref[g]
            win = _dot(mcv, c_hi) + _dot(mcv, c_lo)
            before = _dot(mpv, p_hi) + _dot(mpv, p_lo)
            win = win + jnp.where(n > 0, before, 0.0)
            pm = (win / _pool_count(n, window) - p).astype(BF16)
            pm_ref[:, sl] = pm
            mixed = _dot(pm, w_ref[g])
            mix_ref[:, sl] = mixed
            po_ref[:, sl] = (mixed * sc_ref[:, sl]).astype(BF16)

    return pl.pallas_call(
        body, name=name, grid=(nb,),
        in_specs=[_row_spec(POOL_TILE, D_MODEL, POOL_BLOCK),
                  pl.BlockSpec((POOL_TILE, D_MODEL), lambda n: (jnp.maximum(n - 1, 0), POOL_BLOCK)),
                  tab, tab, _full_spec((GROUPS, GROUP_DIM, GROUP_DIM)), _full_spec((1, D_MODEL))],
        out_specs=[row] * 3,
        out_shape=[jax.ShapeDtypeStruct((t, D_MODEL), BF16), jax.ShapeDtypeStruct((t, D_MODEL), F32),
                   jax.ShapeDtypeStruct((t, D_MODEL), BF16)],
        compiler_params=_cparams(("parallel",)),
    )(proj, proj, mc, mp, pool_w, scale)


def _pool_bwd(name, dpo, pm, mixed, pool_w, scale, tables):
    t = dpo.shape[0]
    nb = t // POOL_TILE
    _, _, mct, mpt = tables
    cur = _row_spec(POOL_TILE, D_MODEL)
    nxt = pl.BlockSpec((POOL_TILE, D_MODEL), lambda n: (jnp.minimum(n + 1, nb - 1), 0))
    tab = _full_spec((GROUPS, POOL_TILE, POOL_TILE))
    wspec = _full_spec((GROUPS, GROUP_DIM, GROUP_DIM))
    sspec = _full_spec((1, D_MODEL))

    def body(dc_ref, dn_ref, pm_ref, mix_ref, mct_ref, mpt_ref, w_ref, sc_ref, dp_ref, dw_ref, ds_ref):
        n = pl.program_id(0)

        @pl.when(n == 0)
        def _():
            dw_ref[...] = jnp.zeros(dw_ref.shape, F32)
            ds_ref[...] = jnp.zeros(ds_ref.shape, F32)

        for g, window in enumerate(POOL_WINDOWS):
            sl = slice(g * GROUP_DIM, (g + 1) * GROUP_DIM)
            dc, sc, w = dc_ref[:, sl], sc_ref[:, sl], w_ref[g]
            dmix_c = (dc * sc).astype(BF16)
            dmix_n = (dn_ref[:, sl] * sc).astype(BF16)
            dpm_c = _dot(dmix_c, w, "nt")
            dpm_n = _dot(dmix_n, w, "nt")
            e_hi, e_lo = _split2(dpm_c / _pool_count(n, window))
            f_hi, f_lo = _split2(dpm_n / _pool_count(n + 1, window))
            mctv, mptv = mct_ref[g], mpt_ref[g]
            back = _dot(mctv, e_hi) + _dot(mctv, e_lo)
            after = _dot(mptv, f_hi) + _dot(mptv, f_lo)
            dp_ref[:, sl] = (back + jnp.where(n < nb - 1, after, 0.0) - dpm_c).astype(BF16)
            dw_ref[g] += _dot(pm_ref[:, sl], dmix_c, "tn")
            ds_ref[:, sl] += jnp.sum(dc * mix_ref[:, sl], axis=0, keepdims=True)

    return pl.pallas_call(
        body, name=name, grid=(nb,),
        in_specs=[cur, nxt, cur, cur, tab, tab, wspec, sspec],
        out_specs=[cur, wspec, sspec],
        out_shape=[jax.ShapeDtypeStruct((t, D_MODEL), BF16), jax.ShapeDtypeStruct((GROUPS, GROUP_DIM, GROUP_DIM), F32),
                   jax.ShapeDtypeStruct((1, D_MODEL), F32)],
        compiler_params=_cparams(("arbitrary",)),
    )(dpo, dpo, pm, mixed, mct, mpt, pool_w, scale)


GATE0_BLOCK, GATE1_BLOCK = 5, 6


def _merge_fwd(name, ret, po, w_ru, w_pu, proj, bias, tm=512):
    t = ret.shape[0]

    def body(r_ref, p_ref, wr_ref, wp_ref, g0_ref, g1_ref, b_ref, m_ref, ru_ref, pu_ref):
        ru = _dot(r_ref[...], wr_ref[...])
        pu = _dot(p_ref[...], wp_ref[...])
        ru_ref[...] = ru
        pu_ref[...] = pu
        m_ref[...] = (_sigmoid(g0_ref[...] + b_ref[0:1, :]) * ru + _sigmoid(g1_ref[...] + b_ref[1:2, :]) * pu).astype(BF16)

    row = _row_spec(tm, D_MODEL)
    wspec = _full_spec((D_MODEL, D_MODEL))
    return pl.pallas_call(
        body, name=name, grid=(t // tm,),
        in_specs=[row, row, wspec, wspec, _row_spec(tm, D_MODEL, GATE0_BLOCK), _row_spec(tm, D_MODEL, GATE1_BLOCK),
                  _full_spec((2, D_MODEL))],
        out_specs=[row, row, row],
        out_shape=[jax.ShapeDtypeStruct((t, D_MODEL), BF16), jax.ShapeDtypeStruct((t, D_MODEL), F32),
                   jax.ShapeDtypeStruct((t, D_MODEL), F32)],
        compiler_params=_cparams(("parallel",)),
    )(ret, po, w_ru, w_pu, proj, proj, bias)


def _merge_bwd(name, dm, ru, pu, proj, bias, tm=512):
    t = dm.shape[0]

    def body(dm_ref, ru_ref, pu_ref, g0_ref, g1_ref, b_ref, dru_ref, dpu_ref, dg0_ref, dg1_ref, db_ref):
        i = pl.program_id(0)
        d = dm_ref[...]
        s0 = _sigmoid(g0_ref[...] + b_ref[0:1, :])
        s1 = _sigmoid(g1_ref[...] + b_ref[1:2, :])
        dru_ref[...] = (d * s0).astype(BF16)
        dpu_ref[...] = (d * s1).astype(BF16)
        dg0 = d * ru_ref[...] * (s0 * (1.0 - s0))
        dg1 = d * pu_ref[...] * (s1 * (1.0 - s1))
        dg0_ref[...] = dg0.astype(BF16)
        dg1_ref[...] = dg1.astype(BF16)
        part0 = jnp.sum(dg0, axis=0, keepdims=True)
        part1 = jnp.sum(dg1, axis=0, keepdims=True)

        @pl.when(i == 0)
        def _():
            db_ref[0:1, :] = part0
            db_ref[1:2, :] = part1

        @pl.when(i > 0)
        def _():
            db_ref[0:1, :] += part0
            db_ref[1:2, :] += part1

    row = _row_spec(tm, D_MODEL)
    return pl.pallas_call(
        body, name=name, grid=(t // tm,),
        in_specs=[row, row, row, _row_spec(tm, D_MODEL, GATE0_BLOCK), _row_spec(tm, D_MODEL, GATE1_BLOCK),
                  _full_spec((2, D_MODEL))],
        out_specs=[row, row, row, row, _full_spec((2, D_MODEL))],
        out_shape=[jax.ShapeDtypeStruct((t, D_MODEL), BF16)] * 4 + [jax.ShapeDtypeStruct((2, D_MODEL), F32)],
        compiler_params=_cparams(("arbitrary",)),
    )(dm, ru, pu, proj, proj, bias)


def _half_scale(acc):
    return (FFN_RES_WEIGHT * acc,)


def _residual_half(acc, res):
    return (res + FFN_RES_WEIGHT * acc,)


def _normed(h, g):
    return h * lax.rsqrt(jnp.mean(h * h, axis=-1, keepdims=True) + NORM_EPS) * g


def _residual_half_norm(acc, res, g):
    h = res + FFN_RES_WEIGHT * acc
    return h, _normed(h, g)


def _residual_norm(acc, res, g):
    h = res + acc
    return h, _normed(h, g)


FF_TILE = D_FF // 2
DW_TILE = 256
FF_CHUNKS = ((0, 512), (512, 1024), (1024, FF_TILE))


def _ffn_in(name, nrm, w_in, tm=512):
    t = nrm.shape[0]
    nj = D_FF // FF_TILE

    def body(n_ref, wg_ref, wu_ref, a_ref, mid_ref):
        nv = n_ref[...]
        for c0, c1 in FF_CHUNKS:
            gate = _dot(nv, wg_ref[:, c0:c1])
            up = _dot(nv, wu_ref[:, c0:c1])
            a_ref[0, :, c0:c1] = gate
            a_ref[1, :, c0:c1] = up
            mid_ref[:, c0:c1] = (gate * _sigmoid(gate) * up).astype(BF16)

    return pl.pallas_call(
        body, name=name, grid=(nj, t // tm),
        in_specs=[pl.BlockSpec((tm, D_MODEL), lambda j, i: (i, 0)),
                  pl.BlockSpec((D_MODEL, FF_TILE), lambda j, i: (0, j)),
                  pl.BlockSpec((D_MODEL, FF_TILE), lambda j, i: (0, j + nj))],
        out_specs=[pl.BlockSpec((2, tm, FF_TILE), lambda j, i: (0, i, j)), pl.BlockSpec((tm, FF_TILE), lambda j, i: (i, j))],
        out_shape=[jax.ShapeDtypeStruct((2, t, D_FF), F32), jax.ShapeDtypeStruct((t, D_FF), BF16)],
        compiler_params=_cparams(("parallel", "parallel")),
    )(nrm, w_in, w_in)


def _ffn_dact(name, dout_b, w_out, a, tm=512):
    t = dout_b.shape[0]

    def body(d_ref, w_ref, a_ref, da_ref):
        dv = d_ref[...]
        for c0, c1 in FF_CHUNKS:
            dm = FFN_RES_WEIGHT * _dot(dv, w_ref[c0:c1, :], "nt")
            gate, up = a_ref[0, :, c0:c1], a_ref[1, :, c0:c1]
            s = _sigmoid(gate)
            da_ref[0, :, c0:c1] = (dm * up * (s * (1.0 + gate * (1.0 - s)))).astype(BF16)
            da_ref[1, :, c0:c1] = (dm * (gate * s)).astype(BF16)

    blk = pl.BlockSpec((2, tm, FF_TILE), lambda j, i: (0, i, j))
    return pl.pallas_call(
        body, name=name, grid=(D_FF // FF_TILE, t // tm),
        in_specs=[pl.BlockSpec((tm, D_MODEL), lambda j, i: (i, 0)), pl.BlockSpec((FF_TILE, D_MODEL), lambda j, i: (j, 0)), blk],
        out_specs=blk,
        out_shape=jax.ShapeDtypeStruct((2, t, D_FF), BF16),
        compiler_params=_cparams(("parallel", "parallel")),
    )(dout_b, w_out, a)


def _ffn_fwd(tag, h, nrm, get_w_in, get_w_out, next_g=None):
    t = h.shape[0]
    w_in = get_w_in(nrm)
    a, mid = _ffn_in(f"{tag}_in", nrm, w_in, tm=min(512, t))
    w_out = get_w_out(mid)
    if next_g is None:
        out = _matmul(f"{tag}_out", mid, w_out, "nn", t, D_MODEL, D_FF, 512, D_MODEL, D_FF, [F32],
                      extras=(h,), epilogue=_residual_half)
        nxt = None
    else:
        out, nxt = _matmul(f"{tag}_out", mid, w_out, "nn", t, D_MODEL, D_FF, 512, D_MODEL, D_FF, [F32, BF16],
                           extras=(h,), consts=(next_g,), epilogue=_residual_half_norm)
    return out, nxt, (nrm, a, mid, w_in, w_out)


def _ffn_bwd(tag, h, g, saved, dout, dout_b, on_grads, flush):
    t = h.shape[0]
    nrm, a, mid, w_in, w_out = saved
    d_w_out = _matmul(f"{tag}_dwout", mid, dout_b, "tn", D_FF, D_MODEL, t, DW_TILE, D_MODEL, t, [BF16], epilogue=_half_scale,
                      resident="b")
    da = _ffn_dact(f"{tag}_dact", dout_b, w_out, a, tm=min(512, t))
    nj = D_FF // DW_TILE
    d_w_in = _matmul(f"{tag}_dwin", nrm, da, "tn", D_MODEL, 2 * D_FF, t, D_MODEL, DW_TILE, t, [BF16], resident="a",
                     b_spec=pl.BlockSpec((None, t, DW_TILE), lambda i, j, kk: (j // nj, 0, j % nj)))
    tie = on_grads({f"{tag}_w_in": d_w_in, f"{tag}_w_out": d_w_out})
    tm = min(256, t)
    dh, dh_b, dg = _proj_norm_bwd(f"{tag}_dn", [da], [pl.BlockSpec((2, tm, D_FF), lambda i: (0, i, 0))],
                                  ((0, 0, 0, D_FF), (0, 1, D_FF, 2 * D_FF)), w_in, h, g if tie is None else g + tie, dout, tm)
    return dh, dh_b, dg, flush(dh)


def _mix_dwin(name, u, pieces, tn=256):
    t = u.shape[0]
    nj = D_MODEL // tn
    npc = len(pieces)

    def body(*refs):
        u_ref, p_refs, o_ref = refs[0], refs[1:1 + npc], refs[1 + npc]
        s = pl.program_id(0)
        for which in range(npc):
            @pl.when(s // nj == which)
            def _(which=which):
                o_ref[...] = _dot(u_ref[...], p_refs[which][...], "tn").astype(BF16)

    def piece_spec(which):
        return pl.BlockSpec((t, tn), lambda s: (0, jnp.clip(s - which * nj, 0, nj - 1)))

    return pl.pallas_call(
        body, name=name, grid=(npc * nj,),
        in_specs=[pl.BlockSpec((t, D_MODEL), lambda s: (0, 0), pipeline_mode=pl.Buffered(1))] + [piece_spec(k) for k in range(npc)],
        out_specs=pl.BlockSpec((D_MODEL, tn), lambda s: (0, s)),
        out_shape=jax.ShapeDtypeStruct((D_MODEL, npc * D_MODEL), BF16),
        compiler_params=_cparams(("parallel",)),
    )(u, *pieces)


def _local_step(x, target, vec, get_w, on_grads, flush):
    t = x.shape[0]
    cos, sin = _rope_tables(t)
    rtab = _retention_tables()
    ptab = _pool_tables()
    w = {}

    def getter(group, name):
        def get(after):
            if name not in w:
                w.update(get_w(group, after))
            return w[name]
        return get

    nrm1 = _rmsnorm_fwd("ffn1_norm", x, vec["norm_ffn1"])
    h1, u, s1 = _ffn_fwd("ffn1", x, nrm1, getter(0, "ffn1_w_in"), getter(1, "ffn1_w_out"), vec["norm_mix"])
    w.update(get_w(2, u))
    proj = _matmul("mix_in", u, w["w_in"], "nn", t, IN_WIDTH, D_MODEL, 1024, 1024, D_MODEL, [F32], n_outer=True)
    qr, kr, vb = _rotary_fwd("rotary", proj, cos, sin)
    o, ret, states = _retention_fwd("retention", qr, kr, vb, proj, rtab)
    pm, mixed, po = _pool_fwd("pool", proj, w["pool_w"], vec["pool_scale"], ptab)
    merged, ru, pu = _merge_fwd("merge", ret, po, w["w_ret_up"], w["w_pool_up"], proj, vec["gate_bias"])
    h2, nrm2 = _matmul("mix_out", merged, w["w_out"], "nn", t, D_MODEL, D_MODEL, 512, D_MODEL, D_MODEL, [F32, BF16],
                       extras=(h1,), consts=(vec["norm_ffn2"],), epilogue=_residual_norm)
    h3, _, s2 = _ffn_fwd("ffn2", h2, nrm2, getter(3, "ffn2_w_in"), getter(3, "ffn2_w_out"))
    dh3, dh3_b, dg_final, loss = _loss_and_grad("loss", h3, vec["norm_final"], target)

    def tied(v, tie):
        return v if tie is None else v + tie

    dh2, dh2_b, dg_ffn2, tie = _ffn_bwd("ffn2", h2, vec["norm_ffn2"], s2, dh3, dh3_b, on_grads, flush)
    dm = _matmul("mix_dmerged", dh2_b, w["w_out"], "nt", t, D_MODEL, D_MODEL, 1024, D_MODEL, D_MODEL, [F32])
    d_w_out = _matmul("mix_dwout", merged, dh2_b, "tn", D_MODEL, D_MODEL, t, D_MODEL, D_MODEL, 1024, [BF16])
    dru, dpu, dg0, dg1, d_bias = _merge_bwd("merge_bwd", dm, ru, pu, proj, tied(vec["gate_bias"], tie))
    dret = _matmul("mix_dret", dru, w["w_ret_up"], "nt", t, D_MODEL, D_MODEL, 1024, D_MODEL, D_MODEL, [F32])
    d_w_ru = _matmul("mix_dwru", ret, dru, "tn", D_MODEL, D_MODEL, t, D_MODEL, D_MODEL, 1024, [BF16])
    dpo = _matmul("mix_dpool", dpu, w["w_pool_up"], "nt", t, D_MODEL, D_MODEL, 1024, D_MODEL, D_MODEL, [F32])
    d_w_pu = _matmul("mix_dwpu", po, dpu, "tn", D_MODEL, D_MODEL, t, D_MODEL, D_MODEL, 1024, [BF16])
    dp, d_pool_w, d_scale = _pool_bwd("pool_bwd", dpo, pm, mixed, w["pool_w"], vec["pool_scale"], ptab)
    dqr, dkr, dv, dgr = _retention_bwd("retention_bwd", dret, o, qr, kr, vb, proj, states, rtab)
    dq, dk = _rotary_bwd("rotary_bwd", dqr, dkr, cos, sin)
    dproj = [dq, dk, dv, dgr, dp, dg0, dg1]
    d_w_in = _mix_dwin("mix_dwin", u, dproj)
    tie = on_grads(dict(w_in=d_w_in, pool_w=d_pool_w.astype(BF16), w_ret_up=d_w_ru, w_pool_up=d_w_pu, w_out=d_w_out))
    tm = min(256, t)
    dh1, dh1_b, dg_mix = _proj_norm_bwd("mix_du", dproj, [_row_spec(tm, D_MODEL)] * len(dproj),
                                        [(k, None, k * D_MODEL, (k + 1) * D_MODEL) for k in range(len(dproj))],
                                        w["w_in"], h1, tied(vec["norm_mix"], tie), dh2, tm)
    tie = flush(dh1)
    dx, _, dg_ffn1, _ = _ffn_bwd("ffn1", x, tied(vec["norm_ffn1"], tie), s1, dh1, dh1_b, on_grads, flush)

    small = dict(norm_ffn1=dg_ffn1, norm_mix=dg_mix, gate_bias=d_bias, pool_scale=d_scale, norm_ffn2=dg_ffn2,
                 norm_final=dg_final)
    return loss[0, 0], dx, small


BIG = ("ffn1_w_in", "ffn1_w_out", "w_in", "pool_w", "w_ret_up", "w_pool_up", "w_out", "ffn2_w_in", "ffn2_w_out")
KIND = dict(ffn1_w_in="col", ffn1_w_out="row", w_in="col", pool_w="pool", w_ret_up="row", w_pool_up="row", w_out="row",
            ffn2_w_in="col", ffn2_w_out="row")
ANY = pl.BlockSpec(memory_space=pl.ANY)


def _place():
    x, y, c = lax.axis_index("x"), lax.axis_index("y"), lax.axis_index("c")
    chips = [(1 - x, y), (x, 1 - y), (1 - x, 1 - y)]
    return x, y, c, chips


def _full_view_shape(kind, local_shape):
    if kind == "col":
        return (2, local_shape[0] // 2, N_CHIPS * local_shape[1])
    if kind == "row":
        return (N_CHIPS, 2, local_shape[0] // 2, local_shape[1])
    return (GROUPS, N_CHIPS, 2, local_shape[1] // 2, local_shape[2])


def _local_view(kind, arr):
    if kind == "pool":
        return arr.reshape(GROUPS, 2, arr.shape[1] // 2, arr.shape[2])
    return arr.reshape(2, arr.shape[0] // 2, arr.shape[1])


def _blk(kind, ref, s, c):
    if kind == "col":
        cs = ref.shape[2] // N_CHIPS
        return ref.at[c, :, pl.ds(pl.multiple_of(s * cs, 128), cs)]
    if kind == "row":
        return ref.at[s, c]
    return ref.at[:, s, c]


def _half(kind, ref, c):
    return ref.at[:, c] if kind == "pool" else ref.at[c]


def _shard(kind, ref, s):
    if kind == "col":
        cs = ref.shape[2] // N_CHIPS
        return ref.at[:, :, pl.ds(pl.multiple_of(s * cs, 128), cs)]
    if kind == "row":
        return ref.at[s]
    return ref.at[:, s]


HBM = pl.BlockSpec(memory_space=pltpu.HBM)
SEM = pl.BlockSpec(memory_space=pltpu.SEMAPHORE)
EFFECT = pltpu.SideEffectType.DATAFLOW_SIDE_EFFECTING
WEIGHT_GROUPS = (("ffn1_w_in",), ("ffn1_w_out",), ("w_in", "pool_w", "w_ret_up", "w_pool_up", "w_out"), ("ffn2_w_in", "ffn2_w_out"))
GRAD_GROUPS = (("ffn2_w_in", "ffn2_w_out"), ("w_in", "pool_w", "w_ret_up", "w_pool_up", "w_out"), ("ffn1_w_in", "ffn1_w_out"))


def _hbm(a):
    return pltpu.with_memory_space_constraint(a, pltpu.HBM)


def _natural(kind, o):
    if kind == "col":
        return o.reshape(o.shape[0] * o.shape[1], o.shape[2])
    if kind == "row":
        return o.reshape(-1, o.shape[3])
    return o.reshape(GROUPS, -1, o.shape[4])


def _ici_copy(kind, loc, full, j, chips, s, c, send_sem, recv_sem):
    px, py = chips[j]
    return (pltpu.make_async_remote_copy(src_ref=_half(kind, loc, c), dst_ref=_blk(kind, full, s, c), send_sem=send_sem,
                                         recv_sem=recv_sem, device_id=(px, py, c), device_id_type=MESH),
            pltpu.make_async_remote_copy(src_ref=_half(kind, loc, c), dst_ref=_blk(kind, full, 2 * px + py, c), send_sem=send_sem,
                                         recv_sem=recv_sem, device_id=(px, py, c), device_id_type=MESH))


def _gather_start(tag, group_ids, shards):
    grps = [WEIGHT_GROUPS[g] for g in group_ids]
    names = [nm for grp in grps for nm in grp]
    kinds = [KIND[nm] for nm in names]
    n, ng = len(names), len(grps)
    locs = [_hbm(_local_view(KIND[nm], shards[nm])) for nm in names]
    lands = [_hbm(lax.empty(_full_view_shape(KIND[nm], shards[nm].shape), BF16)) for nm in names]
    first = np.cumsum([0] + [len(grp) for grp in grps])

    def body(*refs):
        loc, full = refs[:n], refs[n:2 * n]
        send_sems, recv_sems = refs[2 * n:2 * n + ng], refs[2 * n + ng:2 * n + 2 * ng]
        token = refs[-1]
        x, y, c, chips = _place()
        s = 2 * x + y
        for g in range(ng):
            for a in range(first[g], first[g + 1]):
                for j in range(3):
                    k = 3 * (a - first[g]) + j
                    _ici_copy(kinds[a], loc[a], full[a], j, chips, s, c, send_sems[g].at[k], recv_sems[g].at[k])[0].start()
        token[...] = jnp.zeros(token.shape, F32)

    sem_shapes = [pltpu.SemaphoreType.DMA((3 * len(grp),)) for grp in grps]
    outs = pl.pallas_call(
        body, name=f"gather_start_{tag}",
        in_specs=[HBM] * (2 * n),
        out_specs=[SEM] * (2 * ng) + [HBM] * (2 * n) + [pl.BlockSpec(memory_space=pltpu.VMEM)],
        out_shape=sem_shapes + sem_shapes + [pltpu.HBM(a.shape, a.dtype) for a in locs + lands] + [jax.ShapeDtypeStruct((8, 128), F32)],
        input_output_aliases={i: 2 * ng + i for i in range(2 * n)},
        compiler_params=pltpu.CompilerParams(has_side_effects=EFFECT),
    )(*locs, *lands)
    send_sems, recv_sems = outs[:ng], outs[ng:2 * ng]
    locs_t, lands_t = outs[2 * ng:2 * ng + n], outs[2 * ng + n:2 * ng + 2 * n]
    groups = {}
    for k, g in enumerate(group_ids):
        sl = slice(first[k], first[k + 1])
        groups[g] = (send_sems[k], recv_sems[k], list(locs_t[sl]), list(lands_t[sl]))
    return groups, outs[-1]


def _gather_finish(g, group, after):
    names = WEIGHT_GROUPS[g]
    kinds = [KIND[nm] for nm in names]
    m = len(names)
    send_sem, recv_sem, locs, lands = group

    def wait_body(*refs):
        loc, full = refs[:m], refs[m:2 * m]
        send_sems, recv_sems = refs[2 * m], refs[2 * m + 1]
        x, y, c, chips = _place()
        s = 2 * x + y
        for a in range(m):
            for j in range(3):
                k = 3 * a + j
                sent, landed = _ici_copy(kinds[a], loc[a], full[a], j, chips, s, c, send_sems.at[k], recv_sems.at[k])
                sent.wait_send()
                landed.wait_recv()

    outs = pl.pallas_call(
        wait_body, name=f"gather_wait_{g}",
        in_specs=[HBM] * (2 * m) + [SEM, SEM] + [ANY] * len(after), out_specs=[HBM] * (2 * m),
        out_shape=[pltpu.HBM(a.shape, a.dtype) for a in locs + lands],
        input_output_aliases={i: i for i in range(2 * m)},
        compiler_params=pltpu.CompilerParams(has_side_effects=EFFECT),
    )(*locs, *lands, send_sem, recv_sem, *after)
    locs, lands = outs[:m], outs[m:]

    def forward_body(*refs):
        loc, full = refs[:m], refs[2 * m:3 * m]
        send_sems, recv_sems = refs[3 * m:]
        x, y, c, chips = _place()
        s = 2 * x + y
        sib = (x, y, 1 - c)

        def remote(a, k, src, dst):
            return pltpu.make_async_remote_copy(src_ref=src, dst_ref=dst, send_sem=send_sems.at[4 * a + k],
                                                recv_sem=recv_sems.at[4 * a + k], device_id=sib, device_id_type=MESH)

        sends = []
        for a in range(m):
            for j, (px, py) in enumerate(chips):
                theirs = _blk(kinds[a], full[a], 2 * px + py, c)
                sends.append(remote(a, j, theirs, theirs))
            sends.append(remote(a, 3, loc[a], _shard(kinds[a], full[a], s)))
        for cp in sends:
            cp.start()
        for a in range(m):
            for j, (px, py) in enumerate(chips):
                from_sib = _blk(kinds[a], full[a], 2 * px + py, 1 - c)
                remote(a, j, from_sib, from_sib).wait_recv()
            own = _shard(kinds[a], full[a], s)
            remote(a, 3, own, own).wait_recv()
        for cp in sends:
            cp.wait_send()

    outs = pl.pallas_call(
        forward_body, name=f"gather_forward_{g}",
        in_specs=[ANY] * (2 * m), out_specs=[ANY] * m,
        out_shape=[jax.ShapeDtypeStruct(a.shape, a.dtype) for a in lands],
        input_output_aliases={m + i: i for i in range(m)},
        scratch_shapes=[pltpu.SemaphoreType.DMA((4 * m,)), pltpu.SemaphoreType.DMA((4 * m,))],
    )(*locs, *lands)
    return {nm: _natural(k, o) for nm, k, o in zip(names, kinds, outs)}


def _grad_view(kind, g):
    if kind == "col":
        return g.reshape(2, g.shape[0] // 2, g.shape[1])
    if kind == "row":
        return g.reshape(N_CHIPS, 2, g.shape[0] // (2 * N_CHIPS), g.shape[1])
    return g.reshape(GROUPS, N_CHIPS, 2, g.shape[1] // (2 * N_CHIPS), g.shape[2])


def _pair_copies(kinds, g, got, send_sems, recv_sems):
    x, y, c, _ = _place()

    def other_half(kind, ref):
        if kind == "col":
            return ref.at[1 - c]
        if kind == "row":
            return ref.at[:, 1 - c]
        return ref.at[:, :, 1 - c]

    return [pltpu.make_async_remote_copy(src_ref=other_half(kinds[a], g[a]), dst_ref=got[a], send_sem=send_sems.at[a],
                                         recv_sem=recv_sems.at[a], device_id=(x, y, 1 - c), device_id_type=MESH)
            for a in range(len(kinds))]


def _pair_exchange_start(tag, names, views):
    kinds = [KIND[nm] for nm in names]
    n = len(names)

    def got_shape(kind, v):
        if kind == "col":
            return v.shape[1:]
        if kind == "row":
            return (v.shape[0],) + v.shape[2:]
        return v.shape[:2] + v.shape[3:]

    srcs = [_hbm(views[nm]) for nm in names]
    lands = [_hbm(lax.empty(got_shape(k, views[nm]), BF16)) for nm, k in zip(names, kinds)]

    def body(*refs):
        g, got = refs[:n], refs[n:2 * n]
        for cp in _pair_copies(kinds, g, got, refs[2 * n], refs[2 * n + 1]):
            cp.start()
        refs[-1][...] = jnp.zeros(refs[-1].shape, F32)

    sem_shape = pltpu.SemaphoreType.DMA((n,))
    outs = pl.pallas_call(
        body, name=f"grad_pair_exchange_start_{tag}",
        in_specs=[HBM] * (2 * n),
        out_specs=[SEM, SEM] + [HBM] * (2 * n) + [pl.BlockSpec(memory_space=pltpu.VMEM)],
        out_shape=[sem_shape, sem_shape] + [pltpu.HBM(a.shape, a.dtype) for a in srcs + lands] + [jax.ShapeDtypeStruct((8, 128), F32)],
        input_output_aliases={i: 2 + i for i in range(2 * n)},
        compiler_params=pltpu.CompilerParams(has_side_effects=EFFECT),
    )(*srcs, *lands)
    return (outs[0], outs[1], list(outs[2:2 + n]), list(outs[2 + n:2 + 2 * n])), outs[-1]


def _pair_exchange_wait(tag, names, state, after):
    kinds = [KIND[nm] for nm in names]
    n = len(names)
    send_sem, recv_sem, srcs, lands = state

    def body(*refs):
        g, got = refs[:n], refs[n:2 * n]
        for cp in _pair_copies(kinds, g, got, refs[2 * n], refs[2 * n + 1]):
            cp.wait_send()
            cp.wait_recv()

    outs = pl.pallas_call(
        body, name=f"grad_pair_exchange_wait_{tag}",
        in_specs=[HBM] * (2 * n) + [SEM, SEM, ANY], out_specs=[HBM] * (2 * n),
        out_shape=[pltpu.HBM(a.shape, a.dtype) for a in srcs + lands],
        input_output_aliases={i: i for i in range(2 * n)},
        compiler_params=pltpu.CompilerParams(has_side_effects=EFFECT),
    )(*srcs, *lands, send_sem, recv_sem, after)
    return dict(zip(names, outs[:n])), dict(zip(names, outs[n:]))


def _pair_sum(name, kind, view, got, c_arr):
    if kind == "col":
        _, rows, cols = view.shape
        tr = 128
        grid = (rows // tr,)
        v_spec = pl.BlockSpec((None, tr, cols), lambda i, c: (c[0], i, 0))
        g_spec = pl.BlockSpec((tr, cols), lambda i, c: (i, 0))
    elif kind == "row":
        _, _, rows, cols = view.shape
        grid = (N_CHIPS,)
        v_spec = pl.BlockSpec((None, None, rows, cols), lambda i, c: (i, c[0], 0, 0))
        g_spec = pl.BlockSpec((None, rows, cols), lambda i, c: (i, 0, 0))
    else:
        _, _, _, rows, cols = view.shape
        grid = (GROUPS,)
        v_spec = pl.BlockSpec((None, N_CHIPS, None, rows, cols), lambda i, c: (i, 0, c[0], 0, 0))
        g_spec = pl.BlockSpec((None, N_CHIPS, rows, cols), lambda i, c: (i, 0, 0, 0))

    def body(c_ref, v_ref, g_ref, o_ref):
        o_ref[...] = (v_ref[...].astype(F32) + g_ref[...].astype(F32)).astype(BF16)

    return pl.pallas_call(
        body, name=name,
        grid_spec=pltpu.PrefetchScalarGridSpec(num_scalar_prefetch=1, grid=grid, in_specs=[v_spec, g_spec], out_specs=g_spec),
        out_shape=jax.ShapeDtypeStruct(got.shape, BF16),
        compiler_params=_cparams(("parallel",)),
    )(c_arr, view, got)


def _piece(kind, ref, s):
    if kind == "col":
        cs = ref.shape[1] // N_CHIPS
        return ref.at[:, pl.ds(pl.multiple_of(s * cs, 128), cs)]
    if kind == "row":
        return ref.at[s]
    return ref.at[:, s]


def _piece_shape(kind, shape):
    if kind == "col":
        return (shape[0], shape[1] // N_CHIPS)
    if kind == "row":
        return shape[1:]
    return (shape[0],) + shape[2:]


def _shard_copies(kinds, p, got, send_sems, recv_sems):
    x, y, c, chips = _place()
    return [pltpu.make_async_remote_copy(src_ref=_piece(kinds[a], p[a], 2 * px + py), dst_ref=got[a].at[j],
                                         send_sem=send_sems.at[3 * a + j], recv_sem=recv_sems.at[3 * a + j],
                                         device_id=(px, py, c), device_id_type=MESH)
            for a in range(len(kinds)) for j, (px, py) in enumerate(chips)]


def _shard_exchange_start(g, names, psums):
    kinds = [KIND[nm] for nm in names]
    n = len(names)
    srcs = [_hbm(psums[nm]) for nm in names]
    lands = [_hbm(lax.empty((3,) + _piece_shape(k, psums[nm].shape), BF16)) for nm, k in zip(names, kinds)]

    def body(*refs):
        p, got = refs[:n], refs[n:2 * n]
        send_sems, recv_sems = refs[2 * n], refs[2 * n + 1]
        token = refs[-1]
        for cp in _shard_copies(kinds, p, got, send_sems, recv_sems):
            cp.start()
        token[...] = jnp.zeros(token.shape, F32)

    sem_shape = pltpu.SemaphoreType.DMA((3 * n,))
    outs = pl.pallas_call(
        body, name=f"grad_shard_exchange_start_{g}",
        in_specs=[HBM] * (2 * n),
        out_specs=[SEM, SEM] + [HBM] * (2 * n) + [pl.BlockSpec(memory_space=pltpu.VMEM)],
        out_shape=[sem_shape, sem_shape] + [pltpu.HBM(a.shape, a.dtype) for a in srcs + lands] + [jax.ShapeDtypeStruct((8, 128), F32)],
        input_output_aliases={i: 2 + i for i in range(2 * n)},
        compiler_params=pltpu.CompilerParams(has_side_effects=EFFECT),
    )(*srcs, *lands)
    return (outs[0], outs[1], list(outs[2:2 + n]), list(outs[2 + n:2 + 2 * n])), outs[-1]


def _shard_exchange_wait(g, names, state, after):
    kinds = [KIND[nm] for nm in names]
    n = len(names)
    send_sem, recv_sem, srcs, lands = state

    def body(*refs):
        p, got = refs[:n], refs[n:2 * n]
        for cp in _shard_copies(kinds, p, got, refs[2 * n], refs[2 * n + 1]):
            cp.wait_send()
            cp.wait_recv()

    outs = pl.pallas_call(
        body, name=f"grad_shard_exchange_wait_{g}",
        in_specs=[HBM] * (2 * n) + [SEM, SEM, ANY], out_specs=[HBM] * (2 * n),
        out_shape=[pltpu.HBM(a.shape, a.dtype) for a in srcs + lands],
        input_output_aliases={i: i for i in range(2 * n)},
        compiler_params=pltpu.CompilerParams(has_side_effects=EFFECT),
    )(*srcs, *lands, send_sem, recv_sem, after)
    return dict(zip(names, outs[:n])), dict(zip(names, outs[n:]))


def _shard_sum(name, kind, psum, got, sc_arr):
    if kind == "col":
        rows, cols = psum.shape
        cs = cols // N_CHIPS
        tr = 128
        grid = (rows // tr,)
        p_spec = pl.BlockSpec((tr, cs), lambda i, sc: (i, sc[0]))
        g_spec = pl.BlockSpec((3, tr, cs), lambda i, sc: (0, i, 0))
        o_spec = pl.BlockSpec((None, tr, cs), lambda i, sc: (sc[1], i, 0))
        out_shape = (2, rows, cs)
    elif kind == "row":
        _, rows, cols = psum.shape
        grid = (1,)
        p_spec = pl.BlockSpec((None, rows, cols), lambda i, sc: (sc[0], 0, 0))
        g_spec = pl.BlockSpec((3, rows, cols), lambda i, sc: (0, 0, 0))
        o_spec = pl.BlockSpec((None, rows, cols), lambda i, sc: (sc[1], 0, 0))
        out_shape = (2, rows, cols)
    else:
        _, _, rows, cols = psum.shape
        grid = (1,)
        p_spec = pl.BlockSpec((GROUPS, None, rows, cols), lambda i, sc: (0, sc[0], 0, 0))
        g_spec = pl.BlockSpec((3, GROUPS, rows, cols), lambda i, sc: (0, 0, 0, 0))
        o_spec = pl.BlockSpec((GROUPS, None, rows, cols), lambda i, sc: (0, sc[1], 0, 0))
        out_shape = (GROUPS, 2, rows, cols)

    def body(sc_ref, p_ref, g_ref, o_ref):
        o_ref[...] = ((p_ref[...].astype(F32) + g_ref[0].astype(F32)) + g_ref[1].astype(F32)) + g_ref[2].astype(F32)

    return pl.pallas_call(
        body, name=name,
        grid_spec=pltpu.PrefetchScalarGridSpec(num_scalar_prefetch=1, grid=grid, in_specs=[p_spec, g_spec], out_specs=o_spec),
        out_shape=jax.ShapeDtypeStruct(out_shape, F32),
        compiler_params=_cparams(("parallel",)),
    )(sc_arr, psum, got)


def _half_exchange(tag, names, bufs):
    kinds = [KIND[nm] for nm in names]
    n = len(names)

    def body(*refs):
        out = refs[n:2 * n]
        send_sems, recv_sems = refs[2 * n:]
        x, y, c, _ = _place()
        sib = (x, y, 1 - c)
        cps = []
        for a in range(n):
            mine = _half(kinds[a], out[a], c)
            cp = pltpu.make_async_remote_copy(src_ref=mine, dst_ref=mine, send_sem=send_sems.at[a], recv_sem=recv_sems.at[a],
                                              device_id=sib, device_id_type=MESH)
            cp.start()
            cps.append(cp)
        for a, cp in enumerate(cps):
            cp.wait_send()
            theirs = _half(kinds[a], out[a], 1 - c)
            pltpu.make_async_remote_copy(src_ref=theirs, dst_ref=theirs, send_sem=send_sems.at[a], recv_sem=recv_sems.at[a],
                                         device_id=sib, device_id_type=MESH).wait_recv()

    outs = pl.pallas_call(
        body, name=f"grad_half_exchange_{tag}",
        in_specs=[ANY] * n, out_specs=[ANY] * n,
        out_shape=[jax.ShapeDtypeStruct(bufs[nm].shape, F32) for nm in names],
        input_output_aliases={a: a for a in range(n)},
        scratch_shapes=[pltpu.SemaphoreType.DMA((n,)), pltpu.SemaphoreType.DMA((n,))],
    )(*[bufs[nm] for nm in names])
    return dict(zip(names, outs))


N_DEV = 8
SMALL_ROWS = 8


def _all_reduce_small(name, v):
    def body(v_ref, o_ref, token, buf, send_sems, recv_sems):
        token[...] = jnp.zeros(token.shape, F32)
        x, y, c, _ = _place()
        me = 4 * x + 2 * y + c
        buf[me] = v_ref[...]
        cps = []
        for r in range(1, N_DEV):
            to = (x ^ (r >> 2), y ^ ((r >> 1) & 1), c ^ (r & 1))
            cp = pltpu.make_async_remote_copy(src_ref=v_ref, dst_ref=buf.at[me], send_sem=send_sems.at[r - 1],
                                              recv_sem=recv_sems.at[r - 1], device_id=to, device_id_type=MESH)
            cp.start()
            cps.append(cp)
        for r in range(1, N_DEV):
            pltpu.make_async_remote_copy(src_ref=v_ref, dst_ref=buf.at[me ^ r], send_sem=send_sems.at[r - 1],
                                         recv_sem=recv_sems.at[r - 1], device_id=(x, y, c), device_id_type=MESH).wait_recv()
        for cp in cps:
            cp.wait_send()
        acc = buf[0]
        for d in range(1, N_DEV):
            acc = acc + buf[d]
        o_ref[...] = acc

    vm = pl.BlockSpec(memory_space=pltpu.VMEM)
    return pl.pallas_call(
        body, name=name, in_specs=[vm], out_specs=[vm, vm],
        out_shape=[jax.ShapeDtypeStruct((SMALL_ROWS, D_MODEL), F32), jax.ShapeDtypeStruct((8, 128), F32)],
        scratch_shapes=[pltpu.VMEM((N_DEV, SMALL_ROWS, D_MODEL), F32), pltpu.SemaphoreType.DMA((N_DEV - 1,)),
                        pltpu.SemaphoreType.DMA((N_DEV - 1,))],
    )(v)


def _adamw(name, w, g, m, v):
    rows, cols = w.shape
    tr = next((c for c in (256, 176, 128, 64, 32, 8) if rows % c == 0), rows)
    spec = pl.BlockSpec((tr, cols), lambda i: (i, 0))

    def body(w_ref, g_ref, m_ref, v_ref, d_ref, mo_ref, vo_ref):
        gv = g_ref[...]
        m_new = ADAM_B1 * m_ref[...] + (1.0 - ADAM_B1) * gv
        v_new = ADAM_B2 * v_ref[...] + (1.0 - ADAM_B2) * jnp.square(gv)
        m_hat = m_new / (1.0 - ADAM_B1 ** ADAM_STEP)
        v_hat = v_new / (1.0 - ADAM_B2 ** ADAM_STEP)
        d_ref[...] = -ADAM_LR * (m_hat / (jnp.sqrt(v_hat) + ADAM_EPS) + ADAM_WD * w_ref[...])
        mo_ref[...] = m_new
        vo_ref[...] = v_new

    return pl.pallas_call(
        body, name=name, grid=(rows // tr,),
        in_specs=[spec] * 4, out_specs=[spec] * 3,
        out_shape=[jax.ShapeDtypeStruct((rows, cols), F32)] * 3,
        compiler_params=_cparams(("parallel",)),
    )(w, g, m, v)


WEIGHTS = ("norm_ffn1", "ffn1_w_in", "ffn1_w_out", "norm_mix", "w_in", "gate_bias", "pool_w", "pool_scale", "w_ret_up",
           "w_pool_up", "w_out", "norm_ffn2", "ffn2_w_in", "ffn2_w_out", "norm_final")
SMALL_ROW = dict(norm_ffn1=0, norm_mix=1, gate_bias=2, pool_scale=4, norm_ffn2=5, norm_final=6)


def _as2d(a):
    return a.reshape(-1, a.shape[-1])


def kernel(x, norm_ffn1, ffn1_w_in, ffn1_w_out, norm_mix, w_in, gate_bias, pool_w, pool_scale, w_ret_up, w_pool_up, w_out, norm_ffn2, ffn2_w_in, ffn2_w_out, norm_final, loss_target, m_norm_ffn1, m_ffn1_w_in, m_ffn1_w_out, m_norm_mix, m_w_in, m_gate_bias, m_pool_w, m_pool_scale, m_w_ret_up, m_w_pool_up, m_w_out, m_norm_ffn2, m_ffn2_w_in, m_ffn2_w_out, m_norm_final, v_norm_ffn1, v_ffn1_w_in, v_ffn1_w_out, v_norm_mix, v_w_in, v_gate_bias, v_pool_w, v_pool_scale, v_w_ret_up, v_w_pool_up, v_w_out, v_norm_ffn2, v_ffn2_w_in, v_ffn2_w_out, v_norm_final):
    wt = dict(norm_ffn1=norm_ffn1, ffn1_w_in=ffn1_w_in, ffn1_w_out=ffn1_w_out, norm_mix=norm_mix, w_in=w_in, gate_bias=gate_bias,
              pool_w=pool_w, pool_scale=pool_scale, w_ret_up=w_ret_up, w_pool_up=w_pool_up, w_out=w_out, norm_ffn2=norm_ffn2,
              ffn2_w_in=ffn2_w_in, ffn2_w_out=ffn2_w_out, norm_final=norm_final)
    mom = dict(norm_ffn1=m_norm_ffn1, ffn1_w_in=m_ffn1_w_in, ffn1_w_out=m_ffn1_w_out, norm_mix=m_norm_mix, w_in=m_w_in,
               gate_bias=m_gate_bias, pool_w=m_pool_w, pool_scale=m_pool_scale, w_ret_up=m_w_ret_up, w_pool_up=m_w_pool_up,
               w_out=m_w_out, norm_ffn2=m_norm_ffn2, ffn2_w_in=m_ffn2_w_in, ffn2_w_out=m_ffn2_w_out, norm_final=m_norm_final)
    var = dict(norm_ffn1=v_norm_ffn1, ffn1_w_in=v_ffn1_w_in, ffn1_w_out=v_ffn1_w_out, norm_mix=v_norm_mix, w_in=v_w_in,
               gate_bias=v_gate_bias, pool_w=v_pool_w, pool_scale=v_pool_scale, w_ret_up=v_w_ret_up, w_pool_up=v_w_pool_up,
               w_out=v_w_out, norm_ffn2=v_norm_ffn2, ffn2_w_in=v_ffn2_w_in, ffn2_w_out=v_ffn2_w_out, norm_final=v_norm_final)

    ax, ay, ac = lax.axis_index("x"), lax.axis_index("y"), lax.axis_index("c")
    chip = 2 * ax + ay
    c_arr = jnp.reshape(ac, (1,)).astype(jnp.int32)
    sc_arr = jnp.stack([chip, ac]).astype(jnp.int32)
    bias_cols = gate_bias.shape[-1]

    placed = lax.dynamic_update_slice(jnp.zeros((SMALL_ROWS, D_MODEL), F32), gate_bias[0], (0, chip * bias_cols))
    bias_sum, token = _all_reduce_small("gather_gate_bias", jnp.where(ac == 0, placed, 0.0))
    bias_full = bias_sum[:2]
    first = WEIGHT_GROUPS[0]
    gather_groups, token = _gather_start("first", [0], {nm: wt[nm][0].astype(BF16) + token[0, 0].astype(BF16) for nm in first})
    rest, rest_token = _gather_start("rest", [1, 2, 3],
                                     {nm: wt[nm][0].astype(BF16) + token[0, 0].astype(BF16) for nm in BIG if nm not in first})
    gather_groups.update(rest)
    vec = dict(norm_ffn1=norm_ffn1, norm_mix=norm_mix, norm_ffn2=norm_ffn2, pool_scale=pool_scale,
               norm_final=norm_final.reshape(1, D_MODEL), gate_bias=bias_full)

    def get_w(g, after):
        return _gather_finish(g, gather_groups[g], (after, rest_token) if g == 0 else (after,))

    pairs, pending = [], []

    def on_grads(gr):
        g = len(pairs)
        names = GRAD_GROUPS[g]
        assert set(names) == set(gr), (names, list(gr))
        state, token = _pair_exchange_start(g, names, {nm: _grad_view(KIND[nm], gr[nm]) for nm in names})
        pairs.append(state)
        return token[0:1, 0:1]

    def flush(after):
        g = len(pending)
        names = GRAD_GROUPS[g]
        views, from_sib = _pair_exchange_wait(g, names, pairs[g], after)
        psums = {nm: _pair_sum(f"pair_sum_{nm}", KIND[nm], views[nm], from_sib[nm], c_arr) for nm in names}
        state, token = _shard_exchange_start(g, names, psums)
        pending.append(state)
        tokens.append(token)
        return token[0:1, 0:1]

    tokens = []
    loss_local, dx, small = _local_step(x[0], loss_target[0], vec, get_w, on_grads, flush)

    packed = jnp.concatenate([small["norm_ffn1"], small["norm_mix"], small["gate_bias"], small["pool_scale"],
                              small["norm_ffn2"], small["norm_final"], jnp.broadcast_to(loss_local, (1, D_MODEL))], axis=0)
    small_sum, _ = _all_reduce_small("reduce_small_grads", packed)
    loss = small_sum[SMALL_ROWS - 1, 0]
    grads, delta, new_m, new_v = {}, {}, {}, {}

    def adamw(nm):
        shape = wt[nm].shape
        d, m2, v2 = _adamw(f"adamw_{nm}", _as2d(wt[nm]), _as2d(grads[nm]), _as2d(mom[nm]), _as2d(var[nm]))
        delta[nm], new_m[nm], new_v[nm] = d.reshape(shape), m2.reshape(shape), v2.reshape(shape)
        return d

    for nm in ("norm_ffn1", "norm_mix", "pool_scale", "norm_ffn2"):
        grads[nm] = small_sum[SMALL_ROW[nm]][None, :]
    grads["norm_final"] = small_sum[SMALL_ROW["norm_final"]]
    grads["gate_bias"] = lax.dynamic_slice(small_sum, (SMALL_ROW["gate_bias"], chip * bias_cols), (2, bias_cols))[None]
    after = tokens[-1]
    for g, names in enumerate(GRAD_GROUPS):
        psums, from_chips = _shard_exchange_wait(g, names, pending[g], after)
        bufs = {nm: _shard_sum(f"shard_sum_{nm}", KIND[nm], psums[nm], from_chips[nm], sc_arr) for nm in names}
        reduced = _half_exchange(g, names, bufs)
        for nm in names:
            grads[nm] = reduced[nm].reshape(wt[nm].shape)
            after = adamw(nm)
    for nm in WEIGHTS:
        if nm not in delta:
            adamw(nm)

    return (loss, dx[None], *[grads[nm] for nm in WEIGHTS], *[delta[nm] for nm in WEIGHTS],
            *[new_m[nm] for nm in WEIGHTS], *[new_v[nm] for nm in WEIGHTS])
```

```python
import functools

import numpy as np
import jax
import jax.numpy as jnp
from jax import lax
from jax.experimental import pallas as pl
from jax.experimental.pallas import tpu as pltpu

F32 = jnp.float32
BF16 = jnp.bfloat16
MESH = pl.DeviceIdType.MESH

D_MODEL = 1024
D_FF = 2816
HEADS = 4
HEAD_DIM = 256
GROUPS = 4
GROUP_DIM = 256
POOL_WINDOWS = (2, 4, 8, 16)
IN_WIDTH = 7 * D_MODEL
ROPE_BASE = 10000.0
NORM_EPS = 1e-6
FFN_RES_WEIGHT = 0.5
ADAM_LR, ADAM_B1, ADAM_B2, ADAM_EPS, ADAM_WD, ADAM_STEP = 0.001, 0.9, 0.999, 1e-08, 0.01, 10

N_CHIPS = 4
RET_BLOCK = 256
V7X_VMEM_LIMIT = 48 * 1024 * 1024


def _cparams(sem):
    return pltpu.CompilerParams(dimension_semantics=sem, vmem_limit_bytes=V7X_VMEM_LIMIT)


def _sigmoid(x):
    return jax.nn.sigmoid(x)


_DIMS = {"nn": (((1,), (0,)), ((), ())), "nt": (((1,), (1,)), ((), ())), "tn": (((0,), (0,)), ((), ()))}


def _matmul(name, a, b, mode, m, n, k, tm, tn, tk, out_dtypes, a_spec=None, b_spec=None, extras=(), consts=(), epilogue=None,
            resident=None, n_outer=False):
    tm, tn, tk = min(tm, m), min(tn, n), min(tk, k)
    gi, gj, gk = m // tm, n // tn, k // tk
    assert gi * tm == m and gj * tn == n and gk * tk == k, (name, m, n, k, tm, tn, tk)
    assert not (n_outer and (a_spec is not None or b_spec is not None)), name
    once = dict(pipeline_mode=pl.Buffered(1))

    def spec(shape, index, **kw):
        return pl.BlockSpec(shape, (lambda j, i, kk: index(i, j, kk)) if n_outer else index, **kw)

    if a_spec is None:
        kw = once if resident == "a" else {}
        a_spec = (spec((tk, tm), lambda i, j, kk: (kk, i), **kw) if mode == "tn"
                  else spec((tm, tk), lambda i, j, kk: (i, kk), **kw))
    if b_spec is None:
        kw = once if resident == "b" else {}
        b_spec = (spec((tn, tk), lambda i, j, kk: (j, kk), **kw) if mode == "nt"
                  else spec((tk, tn), lambda i, j, kk: (kk, j), **kw))
    n_ex, n_out = len(extras) + len(consts), len(out_dtypes)
    dims = _DIMS[mode]

    def body(a_ref, b_ref, *rest):
        ex_refs, out_refs = rest[:n_ex], rest[n_ex:n_ex + n_out]

        def finish(acc):
            outs = (acc,) if epilogue is None else epilogue(acc, *[e[...] for e in ex_refs])
            for o_ref, o in zip(out_refs, outs):
                o_ref[...] = o.astype(o_ref.dtype)

        prod = lax.dot_general(a_ref[...], b_ref[...], dims, preferred_element_type=F32)
        if gk == 1:
            finish(prod)
        else:
            acc_ref = rest[n_ex + n_out]
            kk = pl.program_id(2)

            @pl.when(kk == 0)
            def _():
                acc_ref[...] = prod

            @pl.when(kk > 0)
            def _():
                acc_ref[...] += prod

            @pl.when(kk == gk - 1)
            def _():
                finish(acc_ref[...])

    o_spec = spec((tm, tn), lambda i, j, kk: (i, j))
    outs = pl.pallas_call(
        body, name=name, grid=(gj, gi, gk) if n_outer else (gi, gj, gk),
        in_specs=[a_spec, b_spec] + [o_spec] * len(extras) + [spec((1, tn), lambda i, j, kk: (0, j))] * len(consts),
        out_specs=[o_spec] * n_out,
        out_shape=[jax.ShapeDtypeStruct((m, n), dt) for dt in out_dtypes],
        scratch_shapes=[pltpu.VMEM((tm, tn), F32)] if gk > 1 else [],
        compiler_params=_cparams(("parallel", "parallel", "arbitrary")),
    )(a, b, *extras, *consts)
    return outs[0] if n_out == 1 else outs


def _row_spec(tm, width, col_block=0):
    return pl.BlockSpec((tm, width), lambda i: (i, col_block))


def _full_spec(shape):
    return pl.BlockSpec(shape, lambda *_: (0,) * len(shape))


def _rmsnorm_fwd(name, h, g, tm=512):
    t = h.shape[0]

    def body(h_ref, g_ref, o_ref):
        x = h_ref[...]
        r = lax.rsqrt(jnp.mean(x * x, axis=-1, keepdims=True) + NORM_EPS)
        o_ref[...] = (x * r * g_ref[...]).astype(BF16)

    return pl.pallas_call(
        body, name=name, grid=(t // tm,),
        in_specs=[_row_spec(tm, D_MODEL), _full_spec((1, D_MODEL))],
        out_specs=_row_spec(tm, D_MODEL),
        out_shape=jax.ShapeDtypeStruct((t, D_MODEL), BF16),
        compiler_params=_cparams(("parallel",)),
    )(h, g)


def _proj_norm_bwd(name, a_list, a_specs, parts, w, h, g, dres, tm):
    t = h.shape[0]
    na = len(a_list)

    def body(*refs):
        a_refs = refs[:na]
        w_ref, h_ref, g_ref, dres_ref, dh_ref, dhb_ref, dg_ref = refs[na:]
        i = pl.program_id(0)
        dn_v = None
        for which, lead, k0, k1 in parts:
            a_ref = a_refs[which]
            term = _dot(a_ref[...] if lead is None else a_ref[lead], w_ref[:, k0:k1], "nt")
            dn_v = term if dn_v is None else dn_v + term
        x = h_ref[...]
        r = lax.rsqrt(jnp.mean(x * x, axis=-1, keepdims=True) + NORM_EPS)
        xh = x * r
        dxh = dn_v * g_ref[...]
        dh = dres_ref[...] + r * (dxh - xh * jnp.mean(dxh * xh, axis=-1, keepdims=True))
        dh_ref[...] = dh
        dhb_ref[...] = dh.astype(BF16)
        part = jnp.sum(dn_v * xh, axis=0, keepdims=True)

        @pl.when(i == 0)
        def _():
            dg_ref[...] = part

        @pl.when(i > 0)
        def _():
            dg_ref[...] += part

    row = _row_spec(tm, D_MODEL)
    return pl.pallas_call(
        body, name=name, grid=(t // tm,),
        in_specs=list(a_specs) + [pl.BlockSpec(w.shape, lambda i: (0, 0), pipeline_mode=pl.Buffered(1)), row,
                                  _full_spec((1, D_MODEL)), row],
        out_specs=[row, row, _full_spec((1, D_MODEL))],
        out_shape=[jax.ShapeDtypeStruct((t, D_MODEL), F32), jax.ShapeDtypeStruct((t, D_MODEL), BF16),
                   jax.ShapeDtypeStruct((1, D_MODEL), F32)],
        compiler_params=_cparams(("arbitrary",)),
    )(*a_list, w, h, g, dres)


def _loss_and_grad(name, h, g, target, tm=512):
    t = h.shape[0]

    def body(h_ref, g_ref, t_ref, dh_ref, dhb_ref, dg_ref, loss_ref):
        i = pl.program_id(0)
        x = h_ref[...]
        gv = g_ref[...]
        r = lax.rsqrt(jnp.mean(x * x, axis=-1, keepdims=True) + NORM_EPS)
        xh = x * r
        err = xh * gv - t_ref[...]
        row = jnp.mean(err * err, axis=-1, keepdims=True)
        part_loss = 0.5 * jnp.sum(row, axis=0, keepdims=True)
        dy = err * (1.0 / D_MODEL)
        dxh = dy * gv
        dh = r * (dxh - xh * jnp.mean(dxh * xh, axis=-1, keepdims=True))
        dh_ref[...] = dh
        dhb_ref[...] = dh.astype(BF16)
        part = jnp.sum(dy * xh, axis=0, keepdims=True)

        @pl.when(i == 0)
        def _():
            dg_ref[...] = part
            loss_ref[...] = jnp.zeros(loss_ref.shape, F32) + part_loss

        @pl.when(i > 0)
        def _():
            dg_ref[...] += part
            loss_ref[...] += part_loss

    return pl.pallas_call(
        body, name=name, grid=(t // tm,),
        in_specs=[_row_spec(tm, D_MODEL), _full_spec((1, D_MODEL)), _row_spec(tm, D_MODEL)],
        out_specs=[_row_spec(tm, D_MODEL), _row_spec(tm, D_MODEL), _full_spec((1, D_MODEL)), _full_spec((8, 128))],
        out_shape=[jax.ShapeDtypeStruct((t, D_MODEL), F32), jax.ShapeDtypeStruct((t, D_MODEL), BF16),
                   jax.ShapeDtypeStruct((1, D_MODEL), F32), jax.ShapeDtypeStruct((8, 128), F32)],
        compiler_params=_cparams(("arbitrary",)),
    )(h, g, target)


def _rope_tables(t):
    half = HEAD_DIM // 2
    inv_freq = ROPE_BASE ** (-jnp.arange(half, dtype=F32) / half)
    ang = jnp.arange(t, dtype=F32)[:, None] * inv_freq[None, :]
    return jnp.cos(ang), jnp.sin(ang)


ROPE_HALF = HEAD_DIM // 2
K_SCALE = HEAD_DIM ** -0.5


def _rotate(ref, hh, c, s, scale=None):
    lo, mid, hi = hh * HEAD_DIM, hh * HEAD_DIM + ROPE_HALF, (hh + 1) * HEAD_DIM
    x1, x2 = ref[:, lo:mid], ref[:, mid:hi]
    y = jnp.concatenate([x1 * c - x2 * s, x1 * s + x2 * c], axis=1)
    return y if scale is None else y * scale


def _unrotate_into(ref, hh, dy, c, s, scale=None):
    lo, mid, hi = hh * HEAD_DIM, hh * HEAD_DIM + ROPE_HALF, (hh + 1) * HEAD_DIM
    y1, y2 = dy[:, :ROPE_HALF], dy[:, ROPE_HALF:]
    d1, d2 = y1 * c + y2 * s, y2 * c - y1 * s
    if scale is not None:
        d1, d2 = d1 * scale, d2 * scale
    ref[:, lo:mid] = d1.astype(ref.dtype)
    ref[:, mid:hi] = d2.astype(ref.dtype)


def _retention_tables():
    b, chunk = RET_BLOCK, 64
    gamma = 1.0 - 2.0 ** (-5.0 - np.arange(HEADS, dtype=np.float64))
    log_g = np.log(gamma)[:, None, None]
    i = np.arange(b)[:, None]
    j = np.arange(b)[None, :]
    same = (i // chunk) == (j // chunk)
    earlier = (j // chunk) < (i // chunk)
    expo = np.where(same, np.abs(i - j), np.where(earlier, i - j, 0)).astype(np.float64)
    mask = np.where(same | earlier, 1.0, 0.0)
    dmat = np.exp(log_g * expo[None]) * mask[None]
    qd = np.exp(log_g[:, :, 0] * (np.arange(b)[None, :] + 1.0))
    kd = np.exp(log_g[:, :, 0] * (b - 1.0 - np.arange(b)[None, :]))
    cd = np.exp(log_g[:, :, 0] * b) * np.ones((1, HEAD_DIM))
    as32 = lambda v: jnp.asarray(v.astype(np.float32))
    return (as32(dmat), as32(np.swapaxes(dmat, 1, 2)), as32(qd[:, :, None]), as32(kd[:, :, None]), as32(cd[:, None, :]))


def _dot(a, b, mode="nn"):
    return lax.dot_general(a, b, _DIMS[mode], preferred_element_type=F32)


GRET_BLOCK = 3


def _head_specs(nb, rev=False):
    pos = (lambda n: nb - 1 - n) if rev else (lambda n: n)
    tok = pl.BlockSpec((RET_BLOCK, D_MODEL), lambda n: (pos(n), 0))
    blk = [pl.BlockSpec((RET_BLOCK, D_MODEL), lambda n, b=b: (pos(n), b)) for b in range(GRET_BLOCK + 1)]
    rope = pl.BlockSpec((RET_BLOCK, ROPE_HALF), lambda n: (pos(n), 0))
    tab = _full_spec((HEADS, RET_BLOCK, RET_BLOCK))
    col = _full_spec((HEADS, RET_BLOCK, 1))
    rowv = _full_spec((HEADS, 1, HEAD_DIM))
    st = pl.BlockSpec((HEADS, None, HEAD_DIM, HEAD_DIM), lambda n: (0, pos(n), 0, 0))
    return tok, blk, rope, tab, col, rowv, st


def _retention_fwd(name, proj, cos, sin, tables):
    t = proj.shape[0]
    nb = t // RET_BLOCK
    dmat, _, qd, kd, cd = tables
    tok, blk, rope, tab, col, rowv, st = _head_specs(nb)

    def body(q_ref, k_ref, v_ref, g_ref, c_ref, s_ref, d_ref, qd_ref, kd_ref, cd_ref, o_ref, ret_ref, st_ref, state):
        n = pl.program_id(0)

        @pl.when(n == 0)
        def _():
            state[...] = jnp.zeros(state.shape, F32)

        cs, sn = c_ref[...], s_ref[...]
        for hh in range(HEADS):
            sl = slice(hh * HEAD_DIM, (hh + 1) * HEAD_DIM)
            q, k, v = _rotate(q_ref, hh, cs, sn), _rotate(k_ref, hh, cs, sn, K_SCALE), v_ref[:, sl].astype(BF16)
            s = _dot(q.astype(BF16), k.astype(BF16), "nt") * d_ref[hh]
            stb = state[hh].astype(BF16)
            st_ref[hh] = stb
            o = _dot(s.astype(BF16), v) + _dot((q * qd_ref[hh]).astype(BF16), stb)
            o_ref[:, sl] = o
            rn = o * lax.rsqrt(jnp.mean(o * o, axis=-1, keepdims=True) + NORM_EPS)
            g = g_ref[:, sl]
            ret_ref[:, sl] = (rn * (g * _sigmoid(g))).astype(BF16)
            state[hh] = state[hh] * cd_ref[hh] + _dot((k * kd_ref[hh]).astype(BF16), v, "tn")

    return pl.pallas_call(
        body, name=name, grid=(nb,),
        in_specs=blk + [rope, rope, tab, col, col, rowv],
        out_specs=[tok, tok, st],
        out_shape=[jax.ShapeDtypeStruct((t, D_MODEL), F32), jax.ShapeDtypeStruct((t, D_MODEL), BF16),
                   jax.ShapeDtypeStruct((HEADS, nb, HEAD_DIM, HEAD_DIM), BF16)],
        scratch_shapes=[pltpu.VMEM((HEADS, HEAD_DIM, HEAD_DIM), F32)],
        compiler_params=_cparams(("arbitrary",)),
    )(proj, proj, proj, proj, cos, sin, dmat, qd, kd, cd)


def _retention_bwd(name, dru, w_ru, o, proj, cos, sin, states, tables):
    t = proj.shape[0]
    nb = t // RET_BLOCK
    dmat, dmat_t, qd, kd, cd = tables
    tok, blk, rope, tab, col, rowv, st = _head_specs(nb, rev=True)

    def body(dru_ref, wru_ref, o_ref, q_ref, k_ref, v_ref, g_ref, c_ref, s_ref, st_ref, d_ref, dt_ref, qd_ref, kd_ref, cd_ref,
             dq_ref, dk_ref, dv_ref, dg_ref, gstate):
        n = pl.program_id(0)

        @pl.when(n == 0)
        def _():
            gstate[...] = jnp.zeros(gstate.shape, F32)

        cs, sn = c_ref[...], s_ref[...]
        dret = _dot(dru_ref[...], wru_ref[...], "nt")
        for hh in range(HEADS):
            sl = slice(hh * HEAD_DIM, (hh + 1) * HEAD_DIM)
            o_v, g, dr = o_ref[:, sl], g_ref[:, sl], dret[:, sl]
            sg = _sigmoid(g)
            r = lax.rsqrt(jnp.mean(o_v * o_v, axis=-1, keepdims=True) + NORM_EPS)
            rn = o_v * r
            d_rn = dr * (g * sg)
            dg_ref[:, sl] = (dr * rn * (sg * (1.0 + g * (1.0 - sg)))).astype(BF16)
            d_o = r * (d_rn - rn * jnp.mean(d_rn * rn, axis=-1, keepdims=True))
            dob = d_o.astype(BF16)

            q, k, v = _rotate(q_ref, hh, cs, sn), _rotate(k_ref, hh, cs, sn, K_SCALE), v_ref[:, sl].astype(BF16)
            qb, kb = q.astype(BF16), k.astype(BF16)
            qdv, kdv = qd_ref[hh], kd_ref[hh]
            s_t = (_dot(kb, qb, "nt") * dt_ref[hh]).astype(BF16)
            p_t = (_dot(v, dob, "nt") * dt_ref[hh]).astype(BF16)
            p = (_dot(dob, v, "nt") * d_ref[hh]).astype(BF16)
            stb = st_ref[hh]
            gb = gstate[hh].astype(BF16)
            _unrotate_into(dq_ref, hh, _dot(p, kb) + _dot(dob, stb, "nt") * qdv, cs, sn)
            _unrotate_into(dk_ref, hh, _dot(p_t, qb) + _dot(v, gb, "nt") * kdv, cs, sn, K_SCALE)
            dv_ref[:, sl] = (_dot(s_t, dob) + _dot((k * kdv).astype(BF16), gb)).astype(BF16)
            gstate[hh] = gstate[hh] * cd_ref[hh] + _dot((q * qdv).astype(BF16), dob, "tn")

    return pl.pallas_call(
        body, name=name, grid=(nb,),
        in_specs=[tok, pl.BlockSpec((D_MODEL, D_MODEL), lambda n: (0, 0), pipeline_mode=pl.Buffered(1)), tok] + blk
                 + [rope, rope, st, tab, tab, col, col, rowv],
        out_specs=[tok, tok, tok, tok],
        out_shape=[jax.ShapeDtypeStruct((t, D_MODEL), BF16)] * 4,
        scratch_shapes=[pltpu.VMEM((HEADS, HEAD_DIM, HEAD_DIM), F32)],
        compiler_params=_cparams(("arbitrary",)),
    )(dru, w_ru, o, proj, proj, proj, proj, cos, sin, states, dmat, dmat_t, qd, kd, cd)


POOL_TILE = 256


def _pool_tables():
    b = POOL_TILE
    tt = np.arange(b)[:, None]
    jj = np.arange(b)[None, :]
    cur, prev = [], []
    for w in POOL_WINDOWS:
        cur.append(((tt - jj >= 0) & (tt - jj <= w - 1)).astype(np.float32))
        prev.append((tt - (jj - b) <= w - 1).astype(np.float32))
    cur, prev = np.stack(cur), np.stack(prev)
    as16 = lambda v: jnp.asarray(v, dtype=BF16)
    return as16(cur), as16(prev), as16(np.swapaxes(cur, 1, 2)), as16(np.swapaxes(prev, 1, 2))


def _split2(x):
    hi = x.astype(BF16)
    return hi, (x - hi.astype(F32)).astype(BF16)


POOL_BLOCK = 4


def _pool_count(n, window):
    tpos = n * POOL_TILE + lax.broadcasted_iota(jnp.int32, (POOL_TILE, 1), 0)
    return jnp.minimum(tpos + 1, window).astype(F32)


def _pool_fwd(name, proj, pool_w, scale, tables):
    t = proj.shape[0]
    nb = t // POOL_TILE
    mc, mp, _, _ = tables
    tab = _full_spec((GROUPS, POOL_TILE, POOL_TILE))
    row = _row_spec(POOL_TILE, D_MODEL)

    def body(pc_ref, pp_ref, mc_ref, mp_ref, w_ref, sc_ref, pm_ref, mix_ref, po_ref):
        n = pl.program_id(0)
        for g, window in enumerate(POOL_WINDOWS):
            sl = slice(g * GROUP_DIM, (g + 1) * GROUP_DIM)
            p = pc_ref[:, sl]
            c_hi, c_lo = _split2(p)
            p_hi, p_lo = _split2(pp_ref[:, sl])
            mcv, mpv = mc_ref[g], mp_ref[g]
            win = _dot(mcv, c_hi) + _dot(mcv, c_lo)
            before = _dot(mpv, p_hi) + _dot(mpv, p_lo)
            win = win + jnp.where(n > 0, before, 0.0)
            pm = (win / _pool_count(n, window) - p).astype(BF16)
            pm_ref[:, sl] = pm
            mixed = _dot(pm, w_ref[g])
            mix_ref[:, sl] = mixed
            po_ref[:, sl] = (mixed * sc_ref[:, sl]).astype(BF16)

    return pl.pallas_call(
        body, name=name, grid=(nb,),
        in_specs=[_row_spec(POOL_TILE, D_MODEL, POOL_BLOCK),
                  pl.BlockSpec((POOL_TILE, D_MODEL), lambda n: (jnp.maximum(n - 1, 0), POOL_BLOCK)),
                  tab, tab, _full_spec((GROUPS, GROUP_DIM, GROUP_DIM)), _full_spec((1, D_MODEL))],
        out_specs=[row] * 3,
        out_shape=[jax.ShapeDtypeStruct((t, D_MODEL), BF16), jax.ShapeDtypeStruct((t, D_MODEL), F32),
                   jax.ShapeDtypeStruct((t, D_MODEL), BF16)],
        compiler_params=_cparams(("parallel",)),
    )(proj, proj, mc, mp, pool_w, scale)


def _pool_bwd(name, dpu, w_pu, pm, mixed, pool_w, scale, tables):
    t = dpu.shape[0]
    nb = t // POOL_TILE
    _, _, mct, mpt = tables
    cur = pl.BlockSpec((POOL_TILE, D_MODEL), lambda n: (nb - 1 - n, 0))
    tab = _full_spec((GROUPS, POOL_TILE, POOL_TILE))
    wspec = _full_spec((GROUPS, GROUP_DIM, GROUP_DIM))
    sspec = _full_spec((1, D_MODEL))

    def body(dpu_ref, wpu_ref, pm_ref, mix_ref, mct_ref, mpt_ref, w_ref, sc_ref, dp_ref, dw_ref, ds_ref, later):
        n = pl.program_id(0)

        @pl.when(n == 0)
        def _():
            dw_ref[...] = jnp.zeros(dw_ref.shape, F32)
            ds_ref[...] = jnp.zeros(ds_ref.shape, F32)
            later[...] = jnp.zeros(later.shape, F32)

        dpo = _dot(dpu_ref[...], wpu_ref[...], "nt")
        for g, window in enumerate(POOL_WINDOWS):
            sl = slice(g * GROUP_DIM, (g + 1) * GROUP_DIM)
            dc, sc = dpo[:, sl], sc_ref[:, sl]
            dmix = (dc * sc).astype(BF16)
            dpm = _dot(dmix, w_ref[g], "nt")
            e = dpm / _pool_count(nb - 1 - n, window)
            e_hi, e_lo = _split2(e)
            f_hi, f_lo = _split2(later[g])
            mctv, mptv = mct_ref[g], mpt_ref[g]
            back = _dot(mctv, e_hi) + _dot(mctv, e_lo)
            after = _dot(mptv, f_hi) + _dot(mptv, f_lo)
            dp_ref[:, sl] = (back + after - dpm).astype(BF16)
            later[g] = e
            dw_ref[g] += _dot(pm_ref[:, sl], dmix, "tn")
            ds_ref[:, sl] += jnp.sum(dc * mix_ref[:, sl], axis=0, keepdims=True)

    return pl.pallas_call(
        body, name=name, grid=(nb,),
        in_specs=[cur, pl.BlockSpec((D_MODEL, D_MODEL), lambda n: (0, 0), pipeline_mode=pl.Buffered(1)), cur, cur, tab, tab,
                  wspec, sspec],
        out_specs=[cur, wspec, sspec],
        out_shape=[jax.ShapeDtypeStruct((t, D_MODEL), BF16), jax.ShapeDtypeStruct((GROUPS, GROUP_DIM, GROUP_DIM), F32),
                   jax.ShapeDtypeStruct((1, D_MODEL), F32)],
        scratch_shapes=[pltpu.VMEM((GROUPS, POOL_TILE, GROUP_DIM), F32)],
        compiler_params=_cparams(("arbitrary",)),
    )(dpu, w_pu, pm, mixed, mct, mpt, pool_w, scale)


GATE0_BLOCK, GATE1_BLOCK = 5, 6


def _merge_fwd(name, ret, po, w_ru, w_pu, proj, bias, tm=512):
    t = ret.shape[0]

    def body(r_ref, p_ref, wr_ref, wp_ref, g0_ref, g1_ref, b_ref, m_ref, ru_ref, pu_ref):
        ru = _dot(r_ref[...], wr_ref[...])
        pu = _dot(p_ref[...], wp_ref[...])
        ru_ref[...] = ru
        pu_ref[...] = pu
        m_ref[...] = (_sigmoid(g0_ref[...] + b_ref[0:1, :]) * ru + _sigmoid(g1_ref[...] + b_ref[1:2, :]) * pu).astype(BF16)

    row = _row_spec(tm, D_MODEL)
    wspec = _full_spec((D_MODEL, D_MODEL))
    return pl.pallas_call(
        body, name=name, grid=(t // tm,),
        in_specs=[row, row, wspec, wspec, _row_spec(tm, D_MODEL, GATE0_BLOCK), _row_spec(tm, D_MODEL, GATE1_BLOCK),
                  _full_spec((2, D_MODEL))],
        out_specs=[row, row, row],
        out_shape=[jax.ShapeDtypeStruct((t, D_MODEL), BF16), jax.ShapeDtypeStruct((t, D_MODEL), F32),
                   jax.ShapeDtypeStruct((t, D_MODEL), F32)],
        compiler_params=_cparams(("parallel",)),
    )(ret, po, w_ru, w_pu, proj, proj, bias)


def _merge_bwd(name, dh_b, w_out, ru, pu, proj, bias, tm=512):
    t = dh_b.shape[0]

    def body(dh_ref, wo_ref, ru_ref, pu_ref, g0_ref, g1_ref, b_ref, dru_ref, dpu_ref, dg0_ref, dg1_ref, db_ref):
        i = pl.program_id(0)
        d = _dot(dh_ref[...], wo_ref[...], "nt")
        s0 = _sigmoid(g0_ref[...] + b_ref[0:1, :])
        s1 = _sigmoid(g1_ref[...] + b_ref[1:2, :])
        dru_ref[...] = (d * s0).astype(BF16)
        dpu_ref[...] = (d * s1).astype(BF16)
        dg0 = d * ru_ref[...] * (s0 * (1.0 - s0))
        dg1 = d * pu_ref[...] * (s1 * (1.0 - s1))
        dg0_ref[...] = dg0.astype(BF16)
        dg1_ref[...] = dg1.astype(BF16)
        part0 = jnp.sum(dg0, axis=0, keepdims=True)
        part1 = jnp.sum(dg1, axis=0, keepdims=True)

        @pl.when(i == 0)
        def _():
            db_ref[0:1, :] = part0
            db_ref[1:2, :] = part1

        @pl.when(i > 0)
        def _():
            db_ref[0:1, :] += part0
            db_ref[1:2, :] += part1

    row = _row_spec(tm, D_MODEL)
    return pl.pallas_call(
        body, name=name, grid=(t // tm,),
        in_specs=[row, pl.BlockSpec((D_MODEL, D_MODEL), lambda i: (0, 0), pipeline_mode=pl.Buffered(1)), row, row,
                  _row_spec(tm, D_MODEL, GATE0_BLOCK), _row_spec(tm, D_MODEL, GATE1_BLOCK), _full_spec((2, D_MODEL))],
        out_specs=[row, row, row, row, _full_spec((2, D_MODEL))],
        out_shape=[jax.ShapeDtypeStruct((t, D_MODEL), BF16)] * 4 + [jax.ShapeDtypeStruct((2, D_MODEL), F32)],
        compiler_params=_cparams(("arbitrary",)),
    )(dh_b, w_out, ru, pu, proj, proj, bias)


def _half_scale(acc):
    return (FFN_RES_WEIGHT * acc,)


def _residual_half(acc, res):
    return (res + FFN_RES_WEIGHT * acc,)


def _normed(h, g):
    return h * lax.rsqrt(jnp.mean(h * h, axis=-1, keepdims=True) + NORM_EPS) * g


def _residual_half_norm(acc, res, g):
    h = res + FFN_RES_WEIGHT * acc
    return h, _normed(h, g)


def _residual_norm(acc, res, g):
    h = res + acc
    return h, _normed(h, g)


FF_TILE = D_FF // 2
DW_TILE = 256
SAVED_FF_DTYPE = BF16
FF_CHUNKS = ((0, 512), (512, 1024), (1024, FF_TILE))


def _ffn_in(name, nrm, w_in, tm=512):
    t = nrm.shape[0]
    nj = D_FF // FF_TILE

    def body(n_ref, wg_ref, wu_ref, a_ref, mid_ref):
        nv = n_ref[...]
        for c0, c1 in FF_CHUNKS:
            gate = _dot(nv, wg_ref[:, c0:c1])
            up = _dot(nv, wu_ref[:, c0:c1])
            a_ref[0, :, c0:c1] = gate.astype(a_ref.dtype)
            a_ref[1, :, c0:c1] = up.astype(a_ref.dtype)
            mid_ref[:, c0:c1] = (gate * _sigmoid(gate) * up).astype(BF16)

    return pl.pallas_call(
        body, name=name, grid=(nj, t // tm),
        in_specs=[pl.BlockSpec((tm, D_MODEL), lambda j, i: (i, 0)),
                  pl.BlockSpec((D_MODEL, FF_TILE), lambda j, i: (0, j)),
                  pl.BlockSpec((D_MODEL, FF_TILE), lambda j, i: (0, j + nj))],
        out_specs=[pl.BlockSpec((2, tm, FF_TILE), lambda j, i: (0, i, j)), pl.BlockSpec((tm, FF_TILE), lambda j, i: (i, j))],
        out_shape=[jax.ShapeDtypeStruct((2, t, D_FF), SAVED_FF_DTYPE), jax.ShapeDtypeStruct((t, D_FF), BF16)],
        compiler_params=_cparams(("parallel", "parallel")),
    )(nrm, w_in, w_in)


def _ffn_dact(name, dout_b, w_out, a, tm=512):
    t = dout_b.shape[0]

    def body(d_ref, w_ref, a_ref, da_ref):
        dv = d_ref[...]
        for c0, c1 in FF_CHUNKS:
            dm = FFN_RES_WEIGHT * _dot(dv, w_ref[c0:c1, :], "nt")
            gate, up = a_ref[0, :, c0:c1].astype(F32), a_ref[1, :, c0:c1].astype(F32)
            s = _sigmoid(gate)
            da_ref[0, :, c0:c1] = (dm * up * (s * (1.0 + gate * (1.0 - s)))).astype(BF16)
            da_ref[1, :, c0:c1] = (dm * (gate * s)).astype(BF16)

    blk = pl.BlockSpec((2, tm, FF_TILE), lambda j, i: (0, i, j))
    return pl.pallas_call(
        body, name=name, grid=(D_FF // FF_TILE, t // tm),
        in_specs=[pl.BlockSpec((tm, D_MODEL), lambda j, i: (i, 0)), pl.BlockSpec((FF_TILE, D_MODEL), lambda j, i: (j, 0)), blk],
        out_specs=blk,
        out_shape=jax.ShapeDtypeStruct((2, t, D_FF), BF16),
        compiler_params=_cparams(("parallel", "parallel")),
    )(dout_b, w_out, a)


def _ffn_fwd(tag, h, nrm, get_w_in, get_w_out, next_g=None):
    t = h.shape[0]
    w_in = get_w_in(nrm)
    a, mid = _ffn_in(f"{tag}_in", nrm, w_in, tm=min(512, t))
    w_out = get_w_out(mid)
    if next_g is None:
        out = _matmul(f"{tag}_out", mid, w_out, "nn", t, D_MODEL, D_FF, 512, D_MODEL, D_FF, [F32],
                      extras=(h,), epilogue=_residual_half)
        nxt = None
    else:
        out, nxt = _matmul(f"{tag}_out", mid, w_out, "nn", t, D_MODEL, D_FF, 512, D_MODEL, D_FF, [F32, BF16],
                           extras=(h,), consts=(next_g,), epilogue=_residual_half_norm)
    return out, nxt, (nrm, a, mid, w_in, w_out)


def _ffn_bwd(tag, h, g, saved, dout, dout_b, on_grads, flush):
    t = h.shape[0]
    nrm, a, mid, w_in, w_out = saved
    d_w_out = _matmul(f"{tag}_dwout", mid, dout_b, "tn", D_FF, D_MODEL, t, DW_TILE, D_MODEL, t, [BF16], epilogue=_half_scale,
                      resident="b")
    da = _ffn_dact(f"{tag}_dact", dout_b, w_out, a, tm=min(512, t))
    nj = D_FF // DW_TILE
    d_w_in = _matmul(f"{tag}_dwin", nrm, da, "tn", D_MODEL, 2 * D_FF, t, D_MODEL, DW_TILE, t, [BF16], resident="a",
                     b_spec=pl.BlockSpec((None, t, DW_TILE), lambda i, j, kk: (j // nj, 0, j % nj)))
    tie = on_grads({f"{tag}_w_in": d_w_in, f"{tag}_w_out": d_w_out})
    tm = min(256, t)
    dh, dh_b, dg = _proj_norm_bwd(f"{tag}_dn", [da], [pl.BlockSpec((2, tm, D_FF), lambda i: (0, i, 0))],
                                  ((0, 0, 0, D_FF), (0, 1, D_FF, 2 * D_FF)), w_in, h, g if tie is None else g + tie, dout, tm)
    return dh, dh_b, dg, flush(dh)


def _mix_dwin(name, u, pieces, tn=256):
    t = u.shape[0]
    nj = D_MODEL // tn
    npc = len(pieces)

    def body(*refs):
        u_ref, p_refs, o_ref = refs[0], refs[1:1 + npc], refs[1 + npc]
        s = pl.program_id(0)
        for which in range(npc):
            @pl.when(s // nj == which)
            def _(which=which):
                o_ref[...] = _dot(u_ref[...], p_refs[which][...], "tn").astype(BF16)

    def piece_spec(which):
        return pl.BlockSpec((t, tn), lambda s: (0, jnp.clip(s - which * nj, 0, nj - 1)))

    return pl.pallas_call(
        body, name=name, grid=(npc * nj,),
        in_specs=[pl.BlockSpec((t, D_MODEL), lambda s: (0, 0), pipeline_mode=pl.Buffered(1))] + [piece_spec(k) for k in range(npc)],
        out_specs=pl.BlockSpec((D_MODEL, tn), lambda s: (0, s)),
        out_shape=jax.ShapeDtypeStruct((D_MODEL, npc * D_MODEL), BF16),
        compiler_params=_cparams(("parallel",)),
    )(u, *pieces)


def _local_step(x, target, vec, get_w, on_grads, flush):
    t = x.shape[0]
    cos, sin = _rope_tables(t)
    rtab = _retention_tables()
    ptab = _pool_tables()
    w = {}

    def getter(group, name):
        def get(after):
            if name not in w:
                w.update(get_w(group, after))
            return w[name]
        return get

    nrm1 = _rmsnorm_fwd("ffn1_norm", x, vec["norm_ffn1"])
    h1, u, s1 = _ffn_fwd("ffn1", x, nrm1, getter(0, "ffn1_w_in"), getter(1, "ffn1_w_out"), vec["norm_mix"])
    w.update(get_w(2, u))
    proj = _matmul("mix_in", u, w["w_in"], "nn", t, IN_WIDTH, D_MODEL, 1024, 1024, D_MODEL, [F32], n_outer=True)
    o, ret, states = _retention_fwd("retention", proj, cos, sin, rtab)
    pm, mixed, po = _pool_fwd("pool", proj, w["pool_w"], vec["pool_scale"], ptab)
    merged, ru, pu = _merge_fwd("merge", ret, po, w["w_ret_up"], w["w_pool_up"], proj, vec["gate_bias"])
    h2, nrm2 = _matmul("mix_out", merged, w["w_out"], "nn", t, D_MODEL, D_MODEL, 512, D_MODEL, D_MODEL, [F32, BF16],
                       extras=(h1,), consts=(vec["norm_ffn2"],), epilogue=_residual_norm)
    h3, _, s2 = _ffn_fwd("ffn2", h2, nrm2, getter(3, "ffn2_w_in"), getter(3, "ffn2_w_out"))
    dh3, dh3_b, dg_final, loss = _loss_and_grad("loss", h3, vec["norm_final"], target)

    def tied(v, tie):
        return v if tie is None else v + tie

    dh2, dh2_b, dg_ffn2, tie = _ffn_bwd("ffn2", h2, vec["norm_ffn2"], s2, dh3, dh3_b, on_grads, flush)
    d_w_out = _matmul("mix_dwout", merged, dh2_b, "tn", D_MODEL, D_MODEL, t, D_MODEL, D_MODEL, 1024, [BF16])
    dru, dpu, dg0, dg1, d_bias = _merge_bwd("merge_bwd", dh2_b, w["w_out"], ru, pu, proj, tied(vec["gate_bias"], tie))
    d_w_ru = _matmul("mix_dwru", ret, dru, "tn", D_MODEL, D_MODEL, t, D_MODEL, D_MODEL, 1024, [BF16])
    d_w_pu = _matmul("mix_dwpu", po, dpu, "tn", D_MODEL, D_MODEL, t, D_MODEL, D_MODEL, 1024, [BF16])
    dp, d_pool_w, d_scale = _pool_bwd("pool_bwd", dpu, w["w_pool_up"], pm, mixed, w["pool_w"], vec["pool_scale"], ptab)
    dq, dk, dv, dgr = _retention_bwd("retention_bwd", dru, w["w_ret_up"], o, proj, cos, sin, states, rtab)
    dproj = [dq, dk, dv, dgr, dp, dg0, dg1]
    d_w_in = _mix_dwin("mix_dwin", u, dproj)
    tie = on_grads(dict(w_in=d_w_in, pool_w=d_pool_w.astype(BF16), w_ret_up=d_w_ru, w_pool_up=d_w_pu, w_out=d_w_out))
    tm = min(256, t)
    dh1, dh1_b, dg_mix = _proj_norm_bwd("mix_du", dproj, [_row_spec(tm, D_MODEL)] * len(dproj),
                                        [(k, None, k * D_MODEL, (k + 1) * D_MODEL) for k in range(len(dproj))],
                                        w["w_in"], h1, tied(vec["norm_mix"], tie), dh2, tm)
    tie = flush(dh1)
    dx, _, dg_ffn1, _ = _ffn_bwd("ffn1", x, tied(vec["norm_ffn1"], tie), s1, dh1, dh1_b, on_grads, flush)

    small = dict(norm_ffn1=dg_ffn1, norm_mix=dg_mix, gate_bias=d_bias, pool_scale=d_scale, norm_ffn2=dg_ffn2,
                 norm_final=dg_final)
    return loss[0, 0], dx, small


BIG = ("ffn1_w_in", "ffn1_w_out", "w_in", "pool_w", "w_ret_up", "w_pool_up", "w_out", "ffn2_w_in", "ffn2_w_out")
KIND = dict(ffn1_w_in="col", ffn1_w_out="row", w_in="col", pool_w="pool", w_ret_up="row", w_pool_up="row", w_out="row",
            ffn2_w_in="col", ffn2_w_out="row")
ANY = pl.BlockSpec(memory_space=pl.ANY)


def _place():
    x, y, c = lax.axis_index("x"), lax.axis_index("y"), lax.axis_index("c")
    chips = [(1 - x, y), (x, 1 - y), (1 - x, 1 - y)]
    return x, y, c, chips


def _full_view_shape(kind, local_shape):
    if kind == "col":
        return (2, local_shape[0] // 2, N_CHIPS * local_shape[1])
    if kind == "row":
        return (N_CHIPS, 2, local_shape[0] // 2, local_shape[1])
    return (GROUPS, N_CHIPS, 2, local_shape[1] // 2, local_shape[2])


def _local_view(kind, arr):
    if kind == "pool":
        return arr.reshape(GROUPS, 2, arr.shape[1] // 2, arr.shape[2])
    return arr.reshape(2, arr.shape[0] // 2, arr.shape[1])


def _blk(kind, ref, s, c):
    if kind == "col":
        cs = ref.shape[2] // N_CHIPS
        return ref.at[c, :, pl.ds(pl.multiple_of(s * cs, 128), cs)]
    if kind == "row":
        return ref.at[s, c]
    return ref.at[:, s, c]


def _half(kind, ref, c):
    return ref.at[:, c] if kind == "pool" else ref.at[c]


def _shard(kind, ref, s):
    if kind == "col":
        cs = ref.shape[2] // N_CHIPS
        return ref.at[:, :, pl.ds(pl.multiple_of(s * cs, 128), cs)]
    if kind == "row":
        return ref.at[s]
    return ref.at[:, s]


HBM = pl.BlockSpec(memory_space=pltpu.HBM)
SEM = pl.BlockSpec(memory_space=pltpu.SEMAPHORE)
EFFECT = pltpu.SideEffectType.DATAFLOW_SIDE_EFFECTING
WEIGHT_GROUPS = (("ffn1_w_in",), ("ffn1_w_out",), ("w_in", "pool_w", "w_ret_up", "w_pool_up", "w_out"), ("ffn2_w_in", "ffn2_w_out"))
GRAD_GROUPS = (("ffn2_w_in", "ffn2_w_out"), ("w_in", "pool_w", "w_ret_up", "w_pool_up", "w_out"), ("ffn1_w_in", "ffn1_w_out"))


def _hbm(a):
    return pltpu.with_memory_space_constraint(a, pltpu.HBM)


def _natural(kind, o):
    if kind == "col":
        return o.reshape(o.shape[0] * o.shape[1], o.shape[2])
    if kind == "row":
        return o.reshape(-1, o.shape[3])
    return o.reshape(GROUPS, -1, o.shape[4])


def _ici_copy(kind, loc, full, j, chips, s, c, send_sem, recv_sem):
    px, py = chips[j]
    return (pltpu.make_async_remote_copy(src_ref=_half(kind, loc, c), dst_ref=_blk(kind, full, s, c), send_sem=send_sem,
                                         recv_sem=recv_sem, device_id=(px, py, c), device_id_type=MESH),
            pltpu.make_async_remote_copy(src_ref=_half(kind, loc, c), dst_ref=_blk(kind, full, 2 * px + py, c), send_sem=send_sem,
                                         recv_sem=recv_sem, device_id=(px, py, c), device_id_type=MESH))


def _gather_start(tag, group_ids, shards):
    grps = [WEIGHT_GROUPS[g] for g in group_ids]
    names = [nm for grp in grps for nm in grp]
    kinds = [KIND[nm] for nm in names]
    n, ng = len(names), len(grps)
    locs = [_hbm(_local_view(KIND[nm], shards[nm])) for nm in names]
    lands = [_hbm(lax.empty(_full_view_shape(KIND[nm], shards[nm].shape), BF16)) for nm in names]
    first = np.cumsum([0] + [len(grp) for grp in grps])

    def body(*refs):
        loc, full = refs[:n], refs[n:2 * n]
        send_sems, recv_sems = refs[2 * n:2 * n + ng], refs[2 * n + ng:2 * n + 2 * ng]
        token = refs[-1]
        x, y, c, chips = _place()
        s = 2 * x + y
        for g in range(ng):
            for a in range(first[g], first[g + 1]):
                for j in range(3):
                    k = 3 * (a - first[g]) + j
                    _ici_copy(kinds[a], loc[a], full[a], j, chips, s, c, send_sems[g].at[k], recv_sems[g].at[k])[0].start()
        token[...] = jnp.zeros(token.shape, F32)

    sem_shapes = [pltpu.SemaphoreType.DMA((3 * len(grp),)) for grp in grps]
    outs = pl.pallas_call(
        body, name=f"gather_start_{tag}",
        in_specs=[HBM] * (2 * n),
        out_specs=[SEM] * (2 * ng) + [HBM] * (2 * n) + [pl.BlockSpec(memory_space=pltpu.VMEM)],
        out_shape=sem_shapes + sem_shapes + [pltpu.HBM(a.shape, a.dtype) for a in locs + lands] + [jax.ShapeDtypeStruct((8, 128), F32)],
        input_output_aliases={i: 2 * ng + i for i in range(2 * n)},
        compiler_params=pltpu.CompilerParams(has_side_effects=EFFECT),
    )(*locs, *lands)
    send_sems, recv_sems = outs[:ng], outs[ng:2 * ng]
    locs_t, lands_t = outs[2 * ng:2 * ng + n], outs[2 * ng + n:2 * ng + 2 * n]
    groups = {}
    for k, g in enumerate(group_ids):
        sl = slice(first[k], first[k + 1])
        groups[g] = (send_sems[k], recv_sems[k], list(locs_t[sl]), list(lands_t[sl]))
    return groups, outs[-1]


def _gather_finish(g, group, after):
    names = WEIGHT_GROUPS[g]
    kinds = [KIND[nm] for nm in names]
    m = len(names)
    send_sem, recv_sem, locs, lands = group

    def wait_body(*refs):
        loc, full = refs[:m], refs[m:2 * m]
        send_sems, recv_sems = refs[2 * m], refs[2 * m + 1]
        x, y, c, chips = _place()
        s = 2 * x + y
        for a in range(m):
            for j in range(3):
                k = 3 * a + j
                sent, landed = _ici_copy(kinds[a], loc[a], full[a], j, chips, s, c, send_sems.at[k], recv_sems.at[k])
                sent.wait_send()
                landed.wait_recv()

    outs = pl.pallas_call(
        wait_body, name=f"gather_wait_{g}",
        in_specs=[HBM] * (2 * m) + [SEM, SEM] + [ANY] * len(after), out_specs=[HBM] * (2 * m),
        out_shape=[pltpu.HBM(a.shape, a.dtype) for a in locs + lands],
        input_output_aliases={i: i for i in range(2 * m)},
        compiler_params=pltpu.CompilerParams(has_side_effects=EFFECT),
    )(*locs, *lands, send_sem, recv_sem, *after)
    locs, lands = outs[:m], outs[m:]

    def forward_body(*refs):
        loc, full = refs[:m], refs[2 * m:3 * m]
        send_sems, recv_sems = refs[3 * m:]
        x, y, c, chips = _place()
        s = 2 * x + y
        sib = (x, y, 1 - c)

        def remote(a, k, src, dst):
            return pltpu.make_async_remote_copy(src_ref=src, dst_ref=dst, send_sem=send_sems.at[4 * a + k],
                                                recv_sem=recv_sems.at[4 * a + k], device_id=sib, device_id_type=MESH)

        sends = []
        for a in range(m):
            for j, (px, py) in enumerate(chips):
                theirs = _blk(kinds[a], full[a], 2 * px + py, c)
                sends.append(remote(a, j, theirs, theirs))
            sends.append(remote(a, 3, loc[a], _shard(kinds[a], full[a], s)))
        for cp in sends:
            cp.start()
        for a in range(m):
            for j, (px, py) in enumerate(chips):
                from_sib = _blk(kinds[a], full[a], 2 * px + py, 1 - c)
                remote(a, j, from_sib, from_sib).wait_recv()
            own = _shard(kinds[a], full[a], s)
            remote(a, 3, own, own).wait_recv()
        for cp in sends:
            cp.wait_send()

    outs = pl.pallas_call(
        forward_body, name=f"gather_forward_{g}",
        in_specs=[ANY] * (2 * m), out_specs=[ANY] * m,
        out_shape=[jax.ShapeDtypeStruct(a.shape, a.dtype) for a in lands],
        input_output_aliases={m + i: i for i in range(m)},
        scratch_shapes=[pltpu.SemaphoreType.DMA((4 * m,)), pltpu.SemaphoreType.DMA((4 * m,))],
    )(*locs, *lands)
    return {nm: _natural(k, o) for nm, k, o in zip(names, kinds, outs)}


def _grad_view(kind, g):
    if kind == "col":
        return g.reshape(2, g.shape[0] // 2, g.shape[1])
    if kind == "row":
        return g.reshape(N_CHIPS, 2, g.shape[0] // (2 * N_CHIPS), g.shape[1])
    return g.reshape(GROUPS, N_CHIPS, 2, g.shape[1] // (2 * N_CHIPS), g.shape[2])


def _pair_copies(kinds, g, got, send_sems, recv_sems):
    x, y, c, _ = _place()

    def other_half(kind, ref):
        if kind == "col":
            return ref.at[1 - c]
        if kind == "row":
            return ref.at[:, 1 - c]
        return ref.at[:, :, 1 - c]

    return [pltpu.make_async_remote_copy(src_ref=other_half(kinds[a], g[a]), dst_ref=got[a], send_sem=send_sems.at[a],
                                         recv_sem=recv_sems.at[a], device_id=(x, y, 1 - c), device_id_type=MESH)
            for a in range(len(kinds))]


def _pair_exchange_start(tag, names, views):
    kinds = [KIND[nm] for nm in names]
    n = len(names)

    def got_shape(kind, v):
        if kind == "col":
            return v.shape[1:]
        if kind == "row":
            return (v.shape[0],) + v.shape[2:]
        return v.shape[:2] + v.shape[3:]

    srcs = [_hbm(views[nm]) for nm in names]
    lands = [_hbm(lax.empty(got_shape(k, views[nm]), BF16)) for nm, k in zip(names, kinds)]

    def body(*refs):
        g, got = refs[:n], refs[n:2 * n]
        for cp in _pair_copies(kinds, g, got, refs[2 * n], refs[2 * n + 1]):
            cp.start()
        refs[-1][...] = jnp.zeros(refs[-1].shape, F32)

    sem_shape = pltpu.SemaphoreType.DMA((n,))
    outs = pl.pallas_call(
        body, name=f"grad_pair_exchange_start_{tag}",
        in_specs=[HBM] * (2 * n),
        out_specs=[SEM, SEM] + [HBM] * (2 * n) + [pl.BlockSpec(memory_space=pltpu.VMEM)],
        out_shape=[sem_shape, sem_shape] + [pltpu.HBM(a.shape, a.dtype) for a in srcs + lands] + [jax.ShapeDtypeStruct((8, 128), F32)],
        input_output_aliases={i: 2 + i for i in range(2 * n)},
        compiler_params=pltpu.CompilerParams(has_side_effects=EFFECT),
    )(*srcs, *lands)
    return (outs[0], outs[1], list(outs[2:2 + n]), list(outs[2 + n:2 + 2 * n])), outs[-1]


def _pair_exchange_wait(tag, names, state, after):
    kinds = [KIND[nm] for nm in names]
    n = len(names)
    send_sem, recv_sem, srcs, lands = state

    def body(*refs):
        g, got = refs[:n], refs[n:2 * n]
        for cp in _pair_copies(kinds, g, got, refs[2 * n], refs[2 * n + 1]):
            cp.wait_send()
            cp.wait_recv()

    outs = pl.pallas_call(
        body, name=f"grad_pair_exchange_wait_{tag}",
        in_specs=[HBM] * (2 * n) + [SEM, SEM, ANY], out_specs=[HBM] * (2 * n),
        out_shape=[pltpu.HBM(a.shape, a.dtype) for a in srcs + lands],
        input_output_aliases={i: i for i in range(2 * n)},
        compiler_params=pltpu.CompilerParams(has_side_effects=EFFECT),
    )(*srcs, *lands, send_sem, recv_sem, after)
    return dict(zip(names, outs[:n])), dict(zip(names, outs[n:]))


def _pair_sum(name, kind, view, got, c_arr):
    if kind == "col":
        _, rows, cols = view.shape
        tr = 128
        grid = (rows // tr,)
        v_spec = pl.BlockSpec((None, tr, cols), lambda i, c: (c[0], i, 0))
        g_spec = pl.BlockSpec((tr, cols), lambda i, c: (i, 0))
    elif kind == "row":
        _, _, rows, cols = view.shape
        grid = (N_CHIPS,)
        v_spec = pl.BlockSpec((None, None, rows, cols), lambda i, c: (i, c[0], 0, 0))
        g_spec = pl.BlockSpec((None, rows, cols), lambda i, c: (i, 0, 0))
    else:
        _, _, _, rows, cols = view.shape
        grid = (GROUPS,)
        v_spec = pl.BlockSpec((None, N_CHIPS, None, rows, cols), lambda i, c: (i, 0, c[0], 0, 0))
        g_spec = pl.BlockSpec((None, N_CHIPS, rows, cols), lambda i, c: (i, 0, 0, 0))

    def body(c_ref, v_ref, g_ref, o_ref):
        o_ref[...] = (v_ref[...].astype(F32) + g_ref[...].astype(F32)).astype(BF16)

    return pl.pallas_call(
        body, name=name,
        grid_spec=pltpu.PrefetchScalarGridSpec(num_scalar_prefetch=1, grid=grid, in_specs=[v_spec, g_spec], out_specs=g_spec),
        out_shape=jax.ShapeDtypeStruct(got.shape, BF16),
        compiler_params=_cparams(("parallel",)),
    )(c_arr, view, got)


def _piece(kind, ref, s):
    if kind == "col":
        cs = ref.shape[1] // N_CHIPS
        return ref.at[:, pl.ds(pl.multiple_of(s * cs, 128), cs)]
    if kind == "row":
        return ref.at[s]
    return ref.at[:, s]


def _piece_shape(kind, shape):
    if kind == "col":
        return (shape[0], shape[1] // N_CHIPS)
    if kind == "row":
        return shape[1:]
    return (shape[0],) + shape[2:]


def _shard_copies(kinds, p, got, send_sems, recv_sems):
    x, y, c, chips = _place()
    return [pltpu.make_async_remote_copy(src_ref=_piece(kinds[a], p[a], 2 * px + py), dst_ref=got[a].at[j],
                                         send_sem=send_sems.at[3 * a + j], recv_sem=recv_sems.at[3 * a + j],
                                         device_id=(px, py, c), device_id_type=MESH)
            for a in range(len(kinds)) for j, (px, py) in enumerate(chips)]


def _shard_exchange_start(g, names, psums):
    kinds = [KIND[nm] for nm in names]
    n = len(names)
    srcs = [_hbm(psums[nm]) for nm in names]
    lands = [_hbm(lax.empty((3,) + _piece_shape(k, psums[nm].shape), BF16)) for nm, k in zip(names, kinds)]

    def body(*refs):
        p, got = refs[:n], refs[n:2 * n]
        send_sems, recv_sems = refs[2 * n], refs[2 * n + 1]
        token = refs[-1]
        for cp in _shard_copies(kinds, p, got, send_sems, recv_sems):
            cp.start()
        token[...] = jnp.zeros(token.shape, F32)

    sem_shape = pltpu.SemaphoreType.DMA((3 * n,))
    outs = pl.pallas_call(
        body, name=f"grad_shard_exchange_start_{g}",
        in_specs=[HBM] * (2 * n),
        out_specs=[SEM, SEM] + [HBM] * (2 * n) + [pl.BlockSpec(memory_space=pltpu.VMEM)],
        out_shape=[sem_shape, sem_shape] + [pltpu.HBM(a.shape, a.dtype) for a in srcs + lands] + [jax.ShapeDtypeStruct((8, 128), F32)],
        input_output_aliases={i: 2 + i for i in range(2 * n)},
        compiler_params=pltpu.CompilerParams(has_side_effects=EFFECT),
    )(*srcs, *lands)
    return (outs[0], outs[1], list(outs[2:2 + n]), list(outs[2 + n:2 + 2 * n])), outs[-1]


def _shard_exchange_wait(g, names, state, after):
    kinds = [KIND[nm] for nm in names]
    n = len(names)
    send_sem, recv_sem, srcs, lands = state

    def body(*refs):
        p, got = refs[:n], refs[n:2 * n]
        for cp in _shard_copies(kinds, p, got, refs[2 * n], refs[2 * n + 1]):
            cp.wait_send()
            cp.wait_recv()

    outs = pl.pallas_call(
        body, name=f"grad_shard_exchange_wait_{g}",
        in_specs=[HBM] * (2 * n) + [SEM, SEM, ANY], out_specs=[HBM] * (2 * n),
        out_shape=[pltpu.HBM(a.shape, a.dtype) for a in srcs + lands],
        input_output_aliases={i: i for i in range(2 * n)},
        compiler_params=pltpu.CompilerParams(has_side_effects=EFFECT),
    )(*srcs, *lands, send_sem, recv_sem, after)
    return dict(zip(names, outs[:n])), dict(zip(names, outs[n:]))


def _shard_sum(name, kind, psum, got, sc_arr):
    if kind == "col":
        rows, cols = psum.shape
        cs = cols // N_CHIPS
        tr = 128
        grid = (rows // tr,)
        p_spec = pl.BlockSpec((tr, cs), lambda i, sc: (i, sc[0]))
        g_spec = pl.BlockSpec((3, tr, cs), lambda i, sc: (0, i, 0))
        o_spec = pl.BlockSpec((None, tr, cs), lambda i, sc: (sc[1], i, 0))
        out_shape = (2, rows, cs)
    elif kind == "row":
        _, rows, cols = psum.shape
        grid = (1,)
        p_spec = pl.BlockSpec((None, rows, cols), lambda i, sc: (sc[0], 0, 0))
        g_spec = pl.BlockSpec((3, rows, cols), lambda i, sc: (0, 0, 0))
        o_spec = pl.BlockSpec((None, rows, cols), lambda i, sc: (sc[1], 0, 0))
        out_shape = (2, rows, cols)
    else:
        _, _, rows, cols = psum.shape
        grid = (1,)
        p_spec = pl.BlockSpec((GROUPS, None, rows, cols), lambda i, sc: (0, sc[0], 0, 0))
        g_spec = pl.BlockSpec((3, GROUPS, rows, cols), lambda i, sc: (0, 0, 0, 0))
        o_spec = pl.BlockSpec((GROUPS, None, rows, cols), lambda i, sc: (0, sc[1], 0, 0))
        out_shape = (GROUPS, 2, rows, cols)

    def body(sc_ref, p_ref, g_ref, o_ref):
        o_ref[...] = ((p_ref[...].astype(F32) + g_ref[0].astype(F32)) + g_ref[1].astype(F32)) + g_ref[2].astype(F32)

    return pl.pallas_call(
        body, name=name,
        grid_spec=pltpu.PrefetchScalarGridSpec(num_scalar_prefetch=1, grid=grid, in_specs=[p_spec, g_spec], out_specs=o_spec),
        out_shape=jax.ShapeDtypeStruct(out_shape, F32),
        compiler_params=_cparams(("parallel",)),
    )(sc_arr, psum, got)


def _half_exchange(tag, names, bufs):
    kinds = [KIND[nm] for nm in names]
    n = len(names)

    def body(*refs):
        out = refs[n:2 * n]
        send_sems, recv_sems = refs[2 * n:]
        x, y, c, _ = _place()
        sib = (x, y, 1 - c)
        cps = []
        for a in range(n):
            mine = _half(kinds[a], out[a], c)
            cp = pltpu.make_async_remote_copy(src_ref=mine, dst_ref=mine, send_sem=send_sems.at[a], recv_sem=recv_sems.at[a],
                                              device_id=sib, device_id_type=MESH)
            cp.start()
            cps.append(cp)
        for a, cp in enumerate(cps):
            cp.wait_send()
            theirs = _half(kinds[a], out[a], 1 - c)
            pltpu.make_async_remote_copy(src_ref=theirs, dst_ref=theirs, send_sem=send_sems.at[a], recv_sem=recv_sems.at[a],
                                         device_id=sib, device_id_type=MESH).wait_recv()

    outs = pl.pallas_call(
        body, name=f"grad_half_exchange_{tag}",
        in_specs=[ANY] * n, out_specs=[ANY] * n,
        out_shape=[jax.ShapeDtypeStruct(bufs[nm].shape, F32) for nm in names],
        input_output_aliases={a: a for a in range(n)},
        scratch_shapes=[pltpu.SemaphoreType.DMA((n,)), pltpu.SemaphoreType.DMA((n,))],
    )(*[bufs[nm] for nm in names])
    return dict(zip(names, outs))


N_DEV = 8
SMALL_ROWS = 8


def _all_reduce_small(name, v):
    def body(v_ref, o_ref, token, buf, send_sems, recv_sems):
        token[...] = jnp.zeros(token.shape, F32)
        x, y, c, _ = _place()
        me = 4 * x + 2 * y + c
        buf[me] = v_ref[...]
        cps = []
        for r in range(1, N_DEV):
            to = (x ^ (r >> 2), y ^ ((r >> 1) & 1), c ^ (r & 1))
            cp = pltpu.make_async_remote_copy(src_ref=v_ref, dst_ref=buf.at[me], send_sem=send_sems.at[r - 1],
                                              recv_sem=recv_sems.at[r - 1], device_id=to, device_id_type=MESH)
            cp.start()
            cps.append(cp)
        for r in range(1, N_DEV):
            pltpu.make_async_remote_copy(src_ref=v_ref, dst_ref=buf.at[me ^ r], send_sem=send_sems.at[r - 1],
                                         recv_sem=recv_sems.at[r - 1], device_id=(x, y, c), device_id_type=MESH).wait_recv()
        for cp in cps:
            cp.wait_send()
        acc = buf[0]
        for d in range(1, N_DEV):
            acc = acc + buf[d]
        o_ref[...] = acc

    vm = pl.BlockSpec(memory_space=pltpu.VMEM)
    return pl.pallas_call(
        body, name=name, in_specs=[vm], out_specs=[vm, vm],
        out_shape=[jax.ShapeDtypeStruct((SMALL_ROWS, D_MODEL), F32), jax.ShapeDtypeStruct((8, 128), F32)],
        scratch_shapes=[pltpu.VMEM((N_DEV, SMALL_ROWS, D_MODEL), F32), pltpu.SemaphoreType.DMA((N_DEV - 1,)),
                        pltpu.SemaphoreType.DMA((N_DEV - 1,))],
    )(v)


def _adamw(name, w, g, m, v):
    rows, cols = w.shape
    tr = next((c for c in (256, 176, 128, 64, 32, 8) if rows % c == 0), rows)
    spec = pl.BlockSpec((tr, cols), lambda i: (i, 0))

    def body(w_ref, g_ref, m_ref, v_ref, d_ref, mo_ref, vo_ref):
        gv = g_ref[...]
        m_new = ADAM_B1 * m_ref[...] + (1.0 - ADAM_B1) * gv
        v_new = ADAM_B2 * v_ref[...] + (1.0 - ADAM_B2) * jnp.square(gv)
        m_hat = m_new / (1.0 - ADAM_B1 ** ADAM_STEP)
        v_hat = v_new / (1.0 - ADAM_B2 ** ADAM_STEP)
        d_ref[...] = -ADAM_LR * (m_hat / (jnp.sqrt(v_hat) + ADAM_EPS) + ADAM_WD * w_ref[...])
        mo_ref[...] = m_new
        vo_ref[...] = v_new

    return pl.pallas_call(
        body, name=name, grid=(rows // tr,),
        in_specs=[spec] * 4, out_specs=[spec] * 3,
        out_shape=[jax.ShapeDtypeStruct((rows, cols), F32)] * 3,
        compiler_params=_cparams(("parallel",)),
    )(w, g, m, v)


WEIGHTS = ("norm_ffn1", "ffn1_w_in", "ffn1_w_out", "norm_mix", "w_in", "gate_bias", "pool_w", "pool_scale", "w_ret_up",
           "w_pool_up", "w_out", "norm_ffn2", "ffn2_w_in", "ffn2_w_out", "norm_final")
SMALL_ROW = dict(norm_ffn1=0, norm_mix=1, gate_bias=2, pool_scale=4, norm_ffn2=5, norm_final=6)


def _as2d(a):
    return a.reshape(-1, a.shape[-1])


def kernel(x, norm_ffn1, ffn1_w_in, ffn1_w_out, norm_mix, w_in, gate_bias, pool_w, pool_scale, w_ret_up, w_pool_up, w_out, norm_ffn2, ffn2_w_in, ffn2_w_out, norm_final, loss_target, m_norm_ffn1, m_ffn1_w_in, m_ffn1_w_out, m_norm_mix, m_w_in, m_gate_bias, m_pool_w, m_pool_scale, m_w_ret_up, m_w_pool_up, m_w_out, m_norm_ffn2, m_ffn2_w_in, m_ffn2_w_out, m_norm_final, v_norm_ffn1, v_ffn1_w_in, v_ffn1_w_out, v_norm_mix, v_w_in, v_gate_bias, v_pool_w, v_pool_scale, v_w_ret_up, v_w_pool_up, v_w_out, v_norm_ffn2, v_ffn2_w_in, v_ffn2_w_out, v_norm_final):
    wt = dict(norm_ffn1=norm_ffn1, ffn1_w_in=ffn1_w_in, ffn1_w_out=ffn1_w_out, norm_mix=norm_mix, w_in=w_in, gate_bias=gate_bias,
              pool_w=pool_w, pool_scale=pool_scale, w_ret_up=w_ret_up, w_pool_up=w_pool_up, w_out=w_out, norm_ffn2=norm_ffn2,
              ffn2_w_in=ffn2_w_in, ffn2_w_out=ffn2_w_out, norm_final=norm_final)
    mom = dict(norm_ffn1=m_norm_ffn1, ffn1_w_in=m_ffn1_w_in, ffn1_w_out=m_ffn1_w_out, norm_mix=m_norm_mix, w_in=m_w_in,
               gate_bias=m_gate_bias, pool_w=m_pool_w, pool_scale=m_pool_scale, w_ret_up=m_w_ret_up, w_pool_up=m_w_pool_up,
               w_out=m_w_out, norm_ffn2=m_norm_ffn2, ffn2_w_in=m_ffn2_w_in, ffn2_w_out=m_ffn2_w_out, norm_final=m_norm_final)
    var = dict(norm_ffn1=v_norm_ffn1, ffn1_w_in=v_ffn1_w_in, ffn1_w_out=v_ffn1_w_out, norm_mix=v_norm_mix, w_in=v_w_in,
               gate_bias=v_gate_bias, pool_w=v_pool_w, pool_scale=v_pool_scale, w_ret_up=v_w_ret_up, w_pool_up=v_w_pool_up,
               w_out=v_w_out, norm_ffn2=v_norm_ffn2, ffn2_w_in=v_ffn2_w_in, ffn2_w_out=v_ffn2_w_out, norm_final=v_norm_final)

    ax, ay, ac = lax.axis_index("x"), lax.axis_index("y"), lax.axis_index("c")
    chip = 2 * ax + ay
    c_arr = jnp.reshape(ac, (1,)).astype(jnp.int32)
    sc_arr = jnp.stack([chip, ac]).astype(jnp.int32)
    bias_cols = gate_bias.shape[-1]

    placed = lax.dynamic_update_slice(jnp.zeros((SMALL_ROWS, D_MODEL), F32), gate_bias[0], (0, chip * bias_cols))
    bias_sum, token = _all_reduce_small("gather_gate_bias", jnp.where(ac == 0, placed, 0.0))
    bias_full = bias_sum[:2]
    first = WEIGHT_GROUPS[0]
    gather_groups, token = _gather_start("first", [0], {nm: wt[nm][0].astype(BF16) + token[0, 0].astype(BF16) for nm in first})
    rest, rest_token = _gather_start("rest", [1, 2, 3],
                                     {nm: wt[nm][0].astype(BF16) + token[0, 0].astype(BF16) for nm in BIG if nm not in first})
    gather_groups.update(rest)
    vec = dict(norm_ffn1=norm_ffn1, norm_mix=norm_mix, norm_ffn2=norm_ffn2, pool_scale=pool_scale,
               norm_final=norm_final.reshape(1, D_MODEL), gate_bias=bias_full)

    def get_w(g, after):
        return _gather_finish(g, gather_groups[g], (after, rest_token) if g == 0 else (after,))

    pairs, pending = [], []

    def on_grads(gr):
        g = len(pairs)
        names = GRAD_GROUPS[g]
        assert set(names) == set(gr), (names, list(gr))
        state, token = _pair_exchange_start(g, names, {nm: _grad_view(KIND[nm], gr[nm]) for nm in names})
        pairs.append(state)
        return token[0:1, 0:1]

    def flush(after):
        g = len(pending)
        names = GRAD_GROUPS[g]
        views, from_sib = _pair_exchange_wait(g, names, pairs[g], after)
        psums = {nm: _pair_sum(f"pair_sum_{nm}", KIND[nm], views[nm], from_sib[nm], c_arr) for nm in names}
        state, token = _shard_exchange_start(g, names, psums)
        pending.append(state)
        tokens.append(token)
        return token[0:1, 0:1]

    tokens = []
    loss_local, dx, small = _local_step(x[0], loss_target[0], vec, get_w, on_grads, flush)

    packed = jnp.concatenate([small["norm_ffn1"], small["norm_mix"], small["gate_bias"], small["pool_scale"],
                              small["norm_ffn2"], small["norm_final"], jnp.broadcast_to(loss_local, (1, D_MODEL))], axis=0)
    small_sum, _ = _all_reduce_small("reduce_small_grads", packed)
    loss = small_sum[SMALL_ROWS - 1, 0]
    grads, delta, new_m, new_v = {}, {}, {}, {}

    def adamw(nm):
        shape = wt[nm].shape
        d, m2, v2 = _adamw(f"adamw_{nm}", _as2d(wt[nm]), _as2d(grads[nm]), _as2d(mom[nm]), _as2d(var[nm]))
        delta[nm], new_m[nm], new_v[nm] = d.reshape(shape), m2.reshape(shape), v2.reshape(shape)
        return d

    for nm in ("norm_ffn1", "norm_mix", "pool_scale", "norm_ffn2"):
        grads[nm] = small_sum[SMALL_ROW[nm]][None, :]
    grads["norm_final"] = small_sum[SMALL_ROW["norm_final"]]
    grads["gate_bias"] = lax.dynamic_slice(small_sum, (SMALL_ROW["gate_bias"], chip * bias_cols), (2, bias_cols))[None]
    after = tokens[-1]
    for g, names in enumerate(GRAD_GROUPS):
        psums, from_chips = _shard_exchange_wait(g, names, pending[g], after)
        bufs = {nm: _shard_sum(f"shard_sum_{nm}", KIND[nm], psums[nm], from_chips[nm], sc_arr) for nm in names}
        reduced = _half_exchange(g, names, bufs)
        for nm in names:
            grads[nm] = reduced[nm].reshape(wt[nm].shape)
            after = adamw(nm)
    for nm in WEIGHTS:
        if nm not in delta:
            adamw(nm)

    return (loss, dx[None], *[grads[nm] for nm in WEIGHTS], *[delta[nm] for nm in WEIGHTS],
            *[new_m[nm] for nm in WEIGHTS], *[new_v[nm] for nm in WEIGHTS])
```

```python
import functools

import numpy as np
import jax
import jax.numpy as jnp
from jax import lax
from jax.experimental import pallas as pl
from jax.experimental.pallas import tpu as pltpu

F32 = jnp.float32
BF16 = jnp.bfloat16
MESH = pl.DeviceIdType.MESH

D_MODEL = 1024
D_FF = 2816
HEADS = 4
HEAD_DIM = 256
GROUPS = 4
GROUP_DIM = 256
POOL_WINDOWS = (2, 4, 8, 16)
IN_WIDTH = 7 * D_MODEL
ROPE_BASE = 10000.0
NORM_EPS = 1e-6
FFN_RES_WEIGHT = 0.5
ADAM_LR, ADAM_B1, ADAM_B2, ADAM_EPS, ADAM_WD, ADAM_STEP = 0.001, 0.9, 0.999, 1e-08, 0.01, 10

N_CHIPS = 4
RET_BLOCK = 256
V7X_VMEM_LIMIT = 48 * 1024 * 1024


def _cparams(sem):
    return pltpu.CompilerParams(dimension_semantics=sem, vmem_limit_bytes=V7X_VMEM_LIMIT)


def _sigmoid(x):
    return jax.nn.sigmoid(x)


_DIMS = {"nn": (((1,), (0,)), ((), ())), "nt": (((1,), (1,)), ((), ())), "tn": (((0,), (0,)), ((), ()))}


def _matmul(name, a, b, mode, m, n, k, tm, tn, tk, out_dtypes, a_spec=None, b_spec=None, extras=(), consts=(), epilogue=None,
            resident=None, n_outer=False):
    tm, tn, tk = min(tm, m), min(tn, n), min(tk, k)
    gi, gj, gk = m // tm, n // tn, k // tk
    assert gi * tm == m and gj * tn == n and gk * tk == k, (name, m, n, k, tm, tn, tk)
    assert not (n_outer and (a_spec is not None or b_spec is not None)), name
    once = dict(pipeline_mode=pl.Buffered(1))

    def spec(shape, index, **kw):
        return pl.BlockSpec(shape, (lambda j, i, kk: index(i, j, kk)) if n_outer else index, **kw)

    if a_spec is None:
        kw = once if resident == "a" else {}
        a_spec = (spec((tk, tm), lambda i, j, kk: (kk, i), **kw) if mode == "tn"
                  else spec((tm, tk), lambda i, j, kk: (i, kk), **kw))
    if b_spec is None:
        kw = once if resident == "b" else {}
        b_spec = (spec((tn, tk), lambda i, j, kk: (j, kk), **kw) if mode == "nt"
                  else spec((tk, tn), lambda i, j, kk: (kk, j), **kw))
    n_ex, n_out = len(extras) + len(consts), len(out_dtypes)
    dims = _DIMS[mode]

    def body(a_ref, b_ref, *rest):
        ex_refs, out_refs = rest[:n_ex], rest[n_ex:n_ex + n_out]

        def finish(acc):
            outs = (acc,) if epilogue is None else epilogue(acc, *[e[...] for e in ex_refs])
            for o_ref, o in zip(out_refs, outs):
                o_ref[...] = o.astype(o_ref.dtype)

        prod = lax.dot_general(a_ref[...], b_ref[...], dims, preferred_element_type=F32)
        if gk == 1:
            finish(prod)
        else:
            acc_ref = rest[n_ex + n_out]
            kk = pl.program_id(2)

            @pl.when(kk == 0)
            def _():
                acc_ref[...] = prod

            @pl.when(kk > 0)
            def _():
                acc_ref[...] += prod

            @pl.when(kk == gk - 1)
            def _():
                finish(acc_ref[...])

    o_spec = spec((tm, tn), lambda i, j, kk: (i, j))
    outs = pl.pallas_call(
        body, name=name, grid=(gj, gi, gk) if n_outer else (gi, gj, gk),
        in_specs=[a_spec, b_spec] + [o_spec] * len(extras) + [spec((1, tn), lambda i, j, kk: (0, j))] * len(consts),
        out_specs=[o_spec] * n_out,
        out_shape=[jax.ShapeDtypeStruct((m, n), dt) for dt in out_dtypes],
        scratch_shapes=[pltpu.VMEM((tm, tn), F32)] if gk > 1 else [],
        compiler_params=_cparams(("parallel", "parallel", "arbitrary")),
    )(a, b, *extras, *consts)
    return outs[0] if n_out == 1 else outs


def _row_spec(tm, width, col_block=0):
    return pl.BlockSpec((tm, width), lambda i: (i, col_block))


def _full_spec(shape):
    return pl.BlockSpec(shape, lambda *_: (0,) * len(shape))


def _rmsnorm_fwd(name, h, g, tm=512):
    t = h.shape[0]

    def body(h_ref, g_ref, o_ref):
        x = h_ref[...]
        r = lax.rsqrt(jnp.mean(x * x, axis=-1, keepdims=True) + NORM_EPS)
        o_ref[...] = (x * r * g_ref[...]).astype(BF16)

    return pl.pallas_call(
        body, name=name, grid=(t // tm,),
        in_specs=[_row_spec(tm, D_MODEL), _full_spec((1, D_MODEL))],
        out_specs=_row_spec(tm, D_MODEL),
        out_shape=jax.ShapeDtypeStruct((t, D_MODEL), BF16),
        compiler_params=_cparams(("parallel",)),
    )(h, g)


def _proj_norm_bwd(name, a_list, a_specs, parts, w, h, g, dres, tm):
    t = h.shape[0]
    na = len(a_list)

    def body(*refs):
        a_refs = refs[:na]
        w_ref, h_ref, g_ref, dres_ref, dh_ref, dhb_ref, dg_ref = refs[na:]
        i = pl.program_id(0)
        dn_v = None
        for which, lead, k0, k1 in parts:
            a_ref = a_refs[which]
            term = _dot(a_ref[...] if lead is None else a_ref[lead], w_ref[:, k0:k1], "nt")
            dn_v = term if dn_v is None else dn_v + term
        x = h_ref[...]
        r = lax.rsqrt(jnp.mean(x * x, axis=-1, keepdims=True) + NORM_EPS)
        xh = x * r
        dxh = dn_v * g_ref[...]
        dh = dres_ref[...] + r * (dxh - xh * jnp.mean(dxh * xh, axis=-1, keepdims=True))
        dh_ref[...] = dh
        dhb_ref[...] = dh.astype(BF16)
        part = jnp.sum(dn_v * xh, axis=0, keepdims=True)

        @pl.when(i == 0)
        def _():
            dg_ref[...] = part

        @pl.when(i > 0)
        def _():
            dg_ref[...] += part

    row = _row_spec(tm, D_MODEL)
    return pl.pallas_call(
        body, name=name, grid=(t // tm,),
        in_specs=list(a_specs) + [pl.BlockSpec(w.shape, lambda i: (0, 0), pipeline_mode=pl.Buffered(1)), row,
                                  _full_spec((1, D_MODEL)), row],
        out_specs=[row, row, _full_spec((1, D_MODEL))],
        out_shape=[jax.ShapeDtypeStruct((t, D_MODEL), F32), jax.ShapeDtypeStruct((t, D_MODEL), BF16),
                   jax.ShapeDtypeStruct((1, D_MODEL), F32)],
        compiler_params=_cparams(("arbitrary",)),
    )(*a_list, w, h, g, dres)


def _loss_and_grad(name, h, g, target, tm=512):
    t = h.shape[0]

    def body(h_ref, g_ref, t_ref, dh_ref, dhb_ref, dg_ref, loss_ref):
        i = pl.program_id(0)
        x = h_ref[...]
        gv = g_ref[...]
        r = lax.rsqrt(jnp.mean(x * x, axis=-1, keepdims=True) + NORM_EPS)
        xh = x * r
        err = xh * gv - t_ref[...]
        row = jnp.mean(err * err, axis=-1, keepdims=True)
        part_loss = 0.5 * jnp.sum(row, axis=0, keepdims=True)
        dy = err * (1.0 / D_MODEL)
        dxh = dy * gv
        dh = r * (dxh - xh * jnp.mean(dxh * xh, axis=-1, keepdims=True))
        dh_ref[...] = dh
        dhb_ref[...] = dh.astype(BF16)
        part = jnp.sum(dy * xh, axis=0, keepdims=True)

        @pl.when(i == 0)
        def _():
            dg_ref[...] = part
            loss_ref[...] = jnp.zeros(loss_ref.shape, F32) + part_loss

        @pl.when(i > 0)
        def _():
            dg_ref[...] += part
            loss_ref[...] += part_loss

    return pl.pallas_call(
        body, name=name, grid=(t // tm,),
        in_specs=[_row_spec(tm, D_MODEL), _full_spec((1, D_MODEL)), _row_spec(tm, D_MODEL)],
        out_specs=[_row_spec(tm, D_MODEL), _row_spec(tm, D_MODEL), _full_spec((1, D_MODEL)), _full_spec((8, 128))],
        out_shape=[jax.ShapeDtypeStruct((t, D_MODEL), F32), jax.ShapeDtypeStruct((t, D_MODEL), BF16),
                   jax.ShapeDtypeStruct((1, D_MODEL), F32), jax.ShapeDtypeStruct((8, 128), F32)],
        compiler_params=_cparams(("arbitrary",)),
    )(h, g, target)


def _rope_tables(t):
    half = HEAD_DIM // 2
    inv_freq = ROPE_BASE ** (-jnp.arange(half, dtype=F32) / half)
    ang = jnp.arange(t, dtype=F32)[:, None] * inv_freq[None, :]
    return jnp.cos(ang), jnp.sin(ang)


ROPE_HALF = HEAD_DIM // 2
K_SCALE = HEAD_DIM ** -0.5


def _rotate(ref, hh, c, s, scale=None):
    lo, mid, hi = hh * HEAD_DIM, hh * HEAD_DIM + ROPE_HALF, (hh + 1) * HEAD_DIM
    x1, x2 = ref[:, lo:mid], ref[:, mid:hi]
    y = jnp.concatenate([x1 * c - x2 * s, x1 * s + x2 * c], axis=1)
    return y if scale is None else y * scale


def _unrotate_into(ref, hh, dy, c, s, scale=None):
    lo, mid, hi = hh * HEAD_DIM, hh * HEAD_DIM + ROPE_HALF, (hh + 1) * HEAD_DIM
    y1, y2 = dy[:, :ROPE_HALF], dy[:, ROPE_HALF:]
    d1, d2 = y1 * c + y2 * s, y2 * c - y1 * s
    if scale is not None:
        d1, d2 = d1 * scale, d2 * scale
    ref[:, lo:mid] = d1.astype(ref.dtype)
    ref[:, mid:hi] = d2.astype(ref.dtype)


def _retention_tables():
    b, chunk = RET_BLOCK, 64
    gamma = 1.0 - 2.0 ** (-5.0 - np.arange(HEADS, dtype=np.float64))
    log_g = np.log(gamma)[:, None, None]
    i = np.arange(b)[:, None]
    j = np.arange(b)[None, :]
    same = (i // chunk) == (j // chunk)
    earlier = (j // chunk) < (i // chunk)
    expo = np.where(same, np.abs(i - j), np.where(earlier, i - j, 0)).astype(np.float64)
    mask = np.where(same | earlier, 1.0, 0.0)
    dmat = np.exp(log_g * expo[None]) * mask[None]
    qd = np.exp(log_g[:, :, 0] * (np.arange(b)[None, :] + 1.0))
    kd = np.exp(log_g[:, :, 0] * (b - 1.0 - np.arange(b)[None, :]))
    cd = np.exp(log_g[:, :, 0] * b) * np.ones((1, HEAD_DIM))
    as32 = lambda v: jnp.asarray(v.astype(np.float32))
    return (as32(dmat), as32(np.swapaxes(dmat, 1, 2)), as32(qd[:, :, None]), as32(kd[:, :, None]), as32(cd[:, None, :]))


def _dot(a, b, mode="nn"):
    return lax.dot_general(a, b, _DIMS[mode], preferred_element_type=F32)


GRET_BLOCK = 3


def _head_specs(nb, rev=False):
    pos = (lambda n: nb - 1 - n) if rev else (lambda n: n)
    tok = pl.BlockSpec((RET_BLOCK, D_MODEL), lambda n: (pos(n), 0))
    blk = [pl.BlockSpec((RET_BLOCK, D_MODEL), lambda n, b=b: (pos(n), b)) for b in range(GRET_BLOCK + 1)]
    rope = pl.BlockSpec((RET_BLOCK, ROPE_HALF), lambda n: (pos(n), 0))
    tab = _full_spec((HEADS, RET_BLOCK, RET_BLOCK))
    col = _full_spec((HEADS, RET_BLOCK, 1))
    rowv = _full_spec((HEADS, 1, HEAD_DIM))
    st = pl.BlockSpec((HEADS, None, HEAD_DIM, HEAD_DIM), lambda n: (0, pos(n), 0, 0))
    return tok, blk, rope, tab, col, rowv, st


def _retention_fwd(name, proj, cos, sin, tables):
    t = proj.shape[0]
    nb = t // RET_BLOCK
    dmat, _, qd, kd, cd = tables
    tok, blk, rope, tab, col, rowv, st = _head_specs(nb)

    def body(q_ref, k_ref, v_ref, g_ref, c_ref, s_ref, d_ref, qd_ref, kd_ref, cd_ref, o_ref, ret_ref, st_ref, state):
        n = pl.program_id(0)

        @pl.when(n == 0)
        def _():
            state[...] = jnp.zeros(state.shape, F32)

        cs, sn = c_ref[...], s_ref[...]
        for hh in range(HEADS):
            sl = slice(hh * HEAD_DIM, (hh + 1) * HEAD_DIM)
            q, k, v = _rotate(q_ref, hh, cs, sn), _rotate(k_ref, hh, cs, sn, K_SCALE), v_ref[:, sl].astype(BF16)
            s = _dot(q.astype(BF16), k.astype(BF16), "nt") * d_ref[hh]
            stb = state[hh].astype(BF16)
            st_ref[hh] = stb
            o = _dot(s.astype(BF16), v) + _dot((q * qd_ref[hh]).astype(BF16), stb)
            o_ref[:, sl] = o
            rn = o * lax.rsqrt(jnp.mean(o * o, axis=-1, keepdims=True) + NORM_EPS)
            g = g_ref[:, sl]
            ret_ref[:, sl] = (rn * (g * _sigmoid(g))).astype(BF16)
            state[hh] = state[hh] * cd_ref[hh] + _dot((k * kd_ref[hh]).astype(BF16), v, "tn")

    return pl.pallas_call(
        body, name=name, grid=(nb,),
        in_specs=blk + [rope, rope, tab, col, col, rowv],
        out_specs=[tok, tok, st],
        out_shape=[jax.ShapeDtypeStruct((t, D_MODEL), F32), jax.ShapeDtypeStruct((t, D_MODEL), BF16),
                   jax.ShapeDtypeStruct((HEADS, nb, HEAD_DIM, HEAD_DIM), BF16)],
        scratch_shapes=[pltpu.VMEM((HEADS, HEAD_DIM, HEAD_DIM), F32)],
        compiler_params=_cparams(("arbitrary",)),
    )(proj, proj, proj, proj, cos, sin, dmat, qd, kd, cd)


def _retention_bwd(name, dru, w_ru, o, proj, cos, sin, states, tables):
    t = proj.shape[0]
    nb = t // RET_BLOCK
    dmat, dmat_t, qd, kd, cd = tables
    tok, blk, rope, tab, col, rowv, st = _head_specs(nb, rev=True)

    def body(dru_ref, wru_ref, o_ref, q_ref, k_ref, v_ref, g_ref, c_ref, s_ref, st_ref, d_ref, dt_ref, qd_ref, kd_ref, cd_ref,
             dq_ref, dk_ref, dv_ref, dg_ref, gstate):
        n = pl.program_id(0)

        @pl.when(n == 0)
        def _():
            gstate[...] = jnp.zeros(gstate.shape, F32)

        cs, sn = c_ref[...], s_ref[...]
        dret = _dot(dru_ref[...], wru_ref[...], "nt")
        for hh in range(HEADS):
            sl = slice(hh * HEAD_DIM, (hh + 1) * HEAD_DIM)
            o_v, g, dr = o_ref[:, sl], g_ref[:, sl], dret[:, sl]
            sg = _sigmoid(g)
            r = lax.rsqrt(jnp.mean(o_v * o_v, axis=-1, keepdims=True) + NORM_EPS)
            rn = o_v * r
            d_rn = dr * (g * sg)
            dg_ref[:, sl] = (dr * rn * (sg * (1.0 + g * (1.0 - sg)))).astype(BF16)
            d_o = r * (d_rn - rn * jnp.mean(d_rn * rn, axis=-1, keepdims=True))
            dob = d_o.astype(BF16)

            q, k, v = _rotate(q_ref, hh, cs, sn), _rotate(k_ref, hh, cs, sn, K_SCALE), v_ref[:, sl].astype(BF16)
            qb, kb = q.astype(BF16), k.astype(BF16)
            qdv, kdv = qd_ref[hh], kd_ref[hh]
            s_t = (_dot(kb, qb, "nt") * dt_ref[hh]).astype(BF16)
            p_t = (_dot(v, dob, "nt") * dt_ref[hh]).astype(BF16)
            p = (_dot(dob, v, "nt") * d_ref[hh]).astype(BF16)
            stb = st_ref[hh]
            gb = gstate[hh].astype(BF16)
            _unrotate_into(dq_ref, hh, _dot(p, kb) + _dot(dob, stb, "nt") * qdv, cs, sn)
            _unrotate_into(dk_ref, hh, _dot(p_t, qb) + _dot(v, gb, "nt") * kdv, cs, sn, K_SCALE)
            dv_ref[:, sl] = (_dot(s_t, dob) + _dot((k * kdv).astype(BF16), gb)).astype(BF16)
            gstate[hh] = gstate[hh] * cd_ref[hh] + _dot((q * qdv).astype(BF16), dob, "tn")

    return pl.pallas_call(
        body, name=name, grid=(nb,),
        in_specs=[tok, pl.BlockSpec((D_MODEL, D_MODEL), lambda n: (0, 0), pipeline_mode=pl.Buffered(1)), tok] + blk
                 + [rope, rope, st, tab, tab, col, col, rowv],
        out_specs=[tok, tok, tok, tok],
        out_shape=[jax.ShapeDtypeStruct((t, D_MODEL), BF16)] * 4,
        scratch_shapes=[pltpu.VMEM((HEADS, HEAD_DIM, HEAD_DIM), F32)],
        compiler_params=_cparams(("arbitrary",)),
    )(dru, w_ru, o, proj, proj, proj, proj, cos, sin, states, dmat, dmat_t, qd, kd, cd)


POOL_TILE = 256


def _pool_tables():
    b = POOL_TILE
    tt = np.arange(b)[:, None]
    jj = np.arange(b)[None, :]
    cur, prev = [], []
    for w in POOL_WINDOWS:
        cur.append(((tt - jj >= 0) & (tt - jj <= w - 1)).astype(np.float32))
        prev.append((tt - (jj - b) <= w - 1).astype(np.float32))
    cur, prev = np.stack(cur), np.stack(prev)
    as16 = lambda v: jnp.asarray(v, dtype=BF16)
    return as16(cur), as16(prev), as16(np.swapaxes(cur, 1, 2)), as16(np.swapaxes(prev, 1, 2))


def _split2(x):
    hi = x.astype(BF16)
    return hi, (x - hi.astype(F32)).astype(BF16)


POOL_BLOCK = 4


def _pool_count(n, window):
    tpos = n * POOL_TILE + lax.broadcasted_iota(jnp.int32, (POOL_TILE, 1), 0)
    return jnp.minimum(tpos + 1, window).astype(F32)


def _pool_fwd(name, proj, pool_w, scale, tables):
    t = proj.shape[0]
    nb = t // POOL_TILE
    mc, mp, _, _ = tables
    tab = _full_spec((GROUPS, POOL_TILE, POOL_TILE))
    row = _row_spec(POOL_TILE, D_MODEL)

    def body(pc_ref, pp_ref, mc_ref, mp_ref, w_ref, sc_ref, pm_ref, mix_ref, po_ref):
        n = pl.program_id(0)
        for g, window in enumerate(POOL_WINDOWS):
            sl = slice(g * GROUP_DIM, (g + 1) * GROUP_DIM)
            p = pc_ref[:, sl]
            c_hi, c_lo = _split2(p)
            p_hi, p_lo = _split2(pp_ref[:, sl])
            mcv, mpv = mc_ref[g], mp_ref[g]
            win = _dot(mcv, c_hi) + _dot(mcv, c_lo)
            before = _dot(mpv, p_hi) + _dot(mpv, p_lo)
            win = win + jnp.where(n > 0, before, 0.0)
            pm = (win / _pool_count(n, window) - p).astype(BF16)
            pm_ref[:, sl] = pm
            mixed = _dot(pm, w_ref[g])
            mix_ref[:, sl] = mixed
            po_ref[:, sl] = (mixed * sc_ref[:, sl]).astype(BF16)

    return pl.pallas_call(
        body, name=name, grid=(nb,),
        in_specs=[_row_spec(POOL_TILE, D_MODEL, POOL_BLOCK),
                  pl.BlockSpec((POOL_TILE, D_MODEL), lambda n: (jnp.maximum(n - 1, 0), POOL_BLOCK)),
                  tab, tab, _full_spec((GROUPS, GROUP_DIM, GROUP_DIM)), _full_spec((1, D_MODEL))],
        out_specs=[row] * 3,
        out_shape=[jax.ShapeDtypeStruct((t, D_MODEL), BF16), jax.ShapeDtypeStruct((t, D_MODEL), F32),
                   jax.ShapeDtypeStruct((t, D_MODEL), BF16)],
        compiler_params=_cparams(("parallel",)),
    )(proj, proj, mc, mp, pool_w, scale)


def _pool_bwd(name, dpu, w_pu, pm, mixed, pool_w, scale, tables):
    t = dpu.shape[0]
    nb = t // POOL_TILE
    _, _, mct, mpt = tables
    cur = pl.BlockSpec((POOL_TILE, D_MODEL), lambda n: (nb - 1 - n, 0))
    tab = _full_spec((GROUPS, POOL_TILE, POOL_TILE))
    wspec = _full_spec((GROUPS, GROUP_DIM, GROUP_DIM))
    sspec = _full_spec((1, D_MODEL))

    def body(dpu_ref, wpu_ref, pm_ref, mix_ref, mct_ref, mpt_ref, w_ref, sc_ref, dp_ref, dw_ref, ds_ref, later):
        n = pl.program_id(0)

        @pl.when(n == 0)
        def _():
            dw_ref[...] = jnp.zeros(dw_ref.shape, F32)
            ds_ref[...] = jnp.zeros(ds_ref.shape, F32)
            later[...] = jnp.zeros(later.shape, F32)

        dpo = _dot(dpu_ref[...], wpu_ref[...], "nt")
        for g, window in enumerate(POOL_WINDOWS):
            sl = slice(g * GROUP_DIM, (g + 1) * GROUP_DIM)
            dc, sc = dpo[:, sl], sc_ref[:, sl]
            dmix = (dc * sc).astype(BF16)
            dpm = _dot(dmix, w_ref[g], "nt")
            e = dpm / _pool_count(nb - 1 - n, window)
            e_hi, e_lo = _split2(e)
            f_hi, f_lo = _split2(later[g])
            mctv, mptv = mct_ref[g], mpt_ref[g]
            back = _dot(mctv, e_hi) + _dot(mctv, e_lo)
            after = _dot(mptv, f_hi) + _dot(mptv, f_lo)
            dp_ref[:, sl] = (back + after - dpm).astype(BF16)
            later[g] = e
            dw_ref[g] += _dot(pm_ref[:, sl], dmix, "tn")
            ds_ref[:, sl] += jnp.sum(dc * mix_ref[:, sl], axis=0, keepdims=True)

    return pl.pallas_call(
        body, name=name, grid=(nb,),
        in_specs=[cur, pl.BlockSpec((D_MODEL, D_MODEL), lambda n: (0, 0), pipeline_mode=pl.Buffered(1)), cur, cur, tab, tab,
                  wspec, sspec],
        out_specs=[cur, wspec, sspec],
        out_shape=[jax.ShapeDtypeStruct((t, D_MODEL), BF16), jax.ShapeDtypeStruct((GROUPS, GROUP_DIM, GROUP_DIM), F32),
                   jax.ShapeDtypeStruct((1, D_MODEL), F32)],
        scratch_shapes=[pltpu.VMEM((GROUPS, POOL_TILE, GROUP_DIM), F32)],
        compiler_params=_cparams(("arbitrary",)),
    )(dpu, w_pu, pm, mixed, mct, mpt, pool_w, scale)


GATE0_BLOCK, GATE1_BLOCK = 5, 6


def _merge_fwd(name, ret, po, w_ru, w_pu, proj, bias, tm=512):
    t = ret.shape[0]

    def body(r_ref, p_ref, wr_ref, wp_ref, g0_ref, g1_ref, b_ref, m_ref, ru_ref, pu_ref):
        ru = _dot(r_ref[...], wr_ref[...])
        pu = _dot(p_ref[...], wp_ref[...])
        ru_ref[...] = ru
        pu_ref[...] = pu
        m_ref[...] = (_sigmoid(g0_ref[...] + b_ref[0:1, :]) * ru + _sigmoid(g1_ref[...] + b_ref[1:2, :]) * pu).astype(BF16)

    row = _row_spec(tm, D_MODEL)
    wspec = _full_spec((D_MODEL, D_MODEL))
    return pl.pallas_call(
        body, name=name, grid=(t // tm,),
        in_specs=[row, row, wspec, wspec, _row_spec(tm, D_MODEL, GATE0_BLOCK), _row_spec(tm, D_MODEL, GATE1_BLOCK),
                  _full_spec((2, D_MODEL))],
        out_specs=[row, row, row],
        out_shape=[jax.ShapeDtypeStruct((t, D_MODEL), BF16), jax.ShapeDtypeStruct((t, D_MODEL), F32),
                   jax.ShapeDtypeStruct((t, D_MODEL), F32)],
        compiler_params=_cparams(("parallel",)),
    )(ret, po, w_ru, w_pu, proj, proj, bias)


def _merge_bwd(name, dh_b, w_out, ru, pu, proj, bias, tm=512):
    t = dh_b.shape[0]

    def body(dh_ref, wo_ref, ru_ref, pu_ref, g0_ref, g1_ref, b_ref, dru_ref, dpu_ref, dg0_ref, dg1_ref, db_ref):
        i = pl.program_id(0)
        d = _dot(dh_ref[...], wo_ref[...], "nt")
        s0 = _sigmoid(g0_ref[...] + b_ref[0:1, :])
        s1 = _sigmoid(g1_ref[...] + b_ref[1:2, :])
        dru_ref[...] = (d * s0).astype(BF16)
        dpu_ref[...] = (d * s1).astype(BF16)
        dg0 = d * ru_ref[...] * (s0 * (1.0 - s0))
        dg1 = d * pu_ref[...] * (s1 * (1.0 - s1))
        dg0_ref[...] = dg0.astype(BF16)
        dg1_ref[...] = dg1.astype(BF16)
        part0 = jnp.sum(dg0, axis=0, keepdims=True)
        part1 = jnp.sum(dg1, axis=0, keepdims=True)

        @pl.when(i == 0)
        def _():
            db_ref[0:1, :] = part0
            db_ref[1:2, :] = part1

        @pl.when(i > 0)
        def _():
            db_ref[0:1, :] += part0
            db_ref[1:2, :] += part1

    row = _row_spec(tm, D_MODEL)
    return pl.pallas_call(
        body, name=name, grid=(t // tm,),
        in_specs=[row, pl.BlockSpec((D_MODEL, D_MODEL), lambda i: (0, 0), pipeline_mode=pl.Buffered(1)), row, row,
                  _row_spec(tm, D_MODEL, GATE0_BLOCK), _row_spec(tm, D_MODEL, GATE1_BLOCK), _full_spec((2, D_MODEL))],
        out_specs=[row, row, row, row, _full_spec((2, D_MODEL))],
        out_shape=[jax.ShapeDtypeStruct((t, D_MODEL), BF16)] * 4 + [jax.ShapeDtypeStruct((2, D_MODEL), F32)],
        compiler_params=_cparams(("arbitrary",)),
    )(dh_b, w_out, ru, pu, proj, proj, bias)


def _half_scale(acc):
    return (FFN_RES_WEIGHT * acc,)


def _residual_half(acc, res):
    return (res + FFN_RES_WEIGHT * acc,)


def _normed(h, g):
    return h * lax.rsqrt(jnp.mean(h * h, axis=-1, keepdims=True) + NORM_EPS) * g


def _residual_half_norm(acc, res, g):
    h = res + FFN_RES_WEIGHT * acc
    return h, _normed(h, g)


def _residual_norm(acc, res, g):
    h = res + acc
    return h, _normed(h, g)


FF_TILE = D_FF // 2
DW_TILE = 256
SAVED_FF_DTYPE = BF16
FF_CHUNKS = ((0, 512), (512, 1024), (1024, FF_TILE))


def _ffn_in(name, nrm, w_in, tm=512):
    t = nrm.shape[0]
    nj = D_FF // FF_TILE

    def body(n_ref, wg_ref, wu_ref, a_ref, mid_ref):
        nv = n_ref[...]
        for c0, c1 in FF_CHUNKS:
            gate = _dot(nv, wg_ref[:, c0:c1])
            up = _dot(nv, wu_ref[:, c0:c1])
            a_ref[0, :, c0:c1] = gate.astype(a_ref.dtype)
            a_ref[1, :, c0:c1] = up.astype(a_ref.dtype)
            mid_ref[:, c0:c1] = (gate * _sigmoid(gate) * up).astype(BF16)

    return pl.pallas_call(
        body, name=name, grid=(nj, t // tm),
        in_specs=[pl.BlockSpec((tm, D_MODEL), lambda j, i: (i, 0)),
                  pl.BlockSpec((D_MODEL, FF_TILE), lambda j, i: (0, j)),
                  pl.BlockSpec((D_MODEL, FF_TILE), lambda j, i: (0, j + nj))],
        out_specs=[pl.BlockSpec((2, tm, FF_TILE), lambda j, i: (0, i, j)), pl.BlockSpec((tm, FF_TILE), lambda j, i: (i, j))],
        out_shape=[jax.ShapeDtypeStruct((2, t, D_FF), SAVED_FF_DTYPE), jax.ShapeDtypeStruct((t, D_FF), BF16)],
        compiler_params=_cparams(("parallel", "parallel")),
    )(nrm, w_in, w_in)


def _ffn_dact(name, dout_b, w_out, a, tm=512):
    t = dout_b.shape[0]

    def body(d_ref, w_ref, a_ref, da_ref):
        dv = d_ref[...]
        for c0, c1 in FF_CHUNKS:
            dm = FFN_RES_WEIGHT * _dot(dv, w_ref[c0:c1, :], "nt")
            gate, up = a_ref[0, :, c0:c1].astype(F32), a_ref[1, :, c0:c1].astype(F32)
            s = _sigmoid(gate)
            da_ref[0, :, c0:c1] = (dm * up * (s * (1.0 + gate * (1.0 - s)))).astype(BF16)
            da_ref[1, :, c0:c1] = (dm * (gate * s)).astype(BF16)

    blk = pl.BlockSpec((2, tm, FF_TILE), lambda j, i: (0, i, j))
    return pl.pallas_call(
        body, name=name, grid=(D_FF // FF_TILE, t // tm),
        in_specs=[pl.BlockSpec((tm, D_MODEL), lambda j, i: (i, 0)), pl.BlockSpec((FF_TILE, D_MODEL), lambda j, i: (j, 0)), blk],
        out_specs=blk,
        out_shape=jax.ShapeDtypeStruct((2, t, D_FF), BF16),
        compiler_params=_cparams(("parallel", "parallel")),
    )(dout_b, w_out, a)


def _ffn_fwd(tag, h, nrm, get_w_in, get_w_out, next_g=None):
    t = h.shape[0]
    w_in = get_w_in(nrm)
    a, mid = _ffn_in(f"{tag}_in", nrm, w_in, tm=min(512, t))
    w_out = get_w_out(mid)
    if next_g is None:
        out = _matmul(f"{tag}_out", mid, w_out, "nn", t, D_MODEL, D_FF, 512, D_MODEL, D_FF, [F32],
                      extras=(h,), epilogue=_residual_half)
        nxt = None
    else:
        out, nxt = _matmul(f"{tag}_out", mid, w_out, "nn", t, D_MODEL, D_FF, 512, D_MODEL, D_FF, [F32, BF16],
                           extras=(h,), consts=(next_g,), epilogue=_residual_half_norm)
    return out, nxt, (nrm, a, mid, w_in, w_out)


def _ffn_bwd(tag, h, g, saved, dout, dout_b, on_grads, flush):
    t = h.shape[0]
    nrm, a, mid, w_in, w_out = saved
    d_w_out = _matmul(f"{tag}_dwout", mid, dout_b, "tn", D_FF, D_MODEL, t, DW_TILE, D_MODEL, t, [BF16], epilogue=_half_scale,
                      resident="b")
    da = _ffn_dact(f"{tag}_dact", dout_b, w_out, a, tm=min(512, t))
    nj = D_FF // DW_TILE
    d_w_in = _dw_resident(f"{tag}_dwin", nrm, [da], [pl.BlockSpec((None, t, DW_TILE), lambda s: (s // nj, 0, s % nj))],
                          2 * nj, None, DW_TILE)
    tie = on_grads({f"{tag}_w_in": d_w_in, f"{tag}_w_out": d_w_out})
    tm = min(256, t)
    dh, dh_b, dg = _proj_norm_bwd(f"{tag}_dn", [da], [pl.BlockSpec((2, tm, D_FF), lambda i: (0, i, 0))],
                                  ((0, 0, 0, D_FF), (0, 1, D_FF, 2 * D_FF)), w_in, h, g if tie is None else g + tie, dout, tm)
    return dh, dh_b, dg, flush(dh)


def _dw_resident(name, u, pieces, piece_specs, n_tiles, which_piece, tn):
    t = u.shape[0]
    npc = len(pieces)

    def body(*refs):
        u_ref, p_refs, o_ref, ut_ref = refs[0], refs[1:1 + npc], refs[1 + npc], refs[2 + npc]
        s = pl.program_id(0)

        @pl.when(s == 0)
        def _():
            ut_ref[...] = u_ref[...].T

        if npc == 1:
            o_ref[...] = _dot(ut_ref[...], p_refs[0][...]).astype(BF16)
        for which in range(npc if npc > 1 else 0):
            @pl.when(which_piece(s) == which)
            def _(which=which):
                o_ref[...] = _dot(ut_ref[...], p_refs[which][...]).astype(BF16)

    return pl.pallas_call(
        body, name=name, grid=(n_tiles,),
        in_specs=[pl.BlockSpec((t, D_MODEL), lambda s: (0, 0), pipeline_mode=pl.Buffered(1))] + list(piece_specs),
        out_specs=pl.BlockSpec((D_MODEL, tn), lambda s: (0, s)),
        out_shape=jax.ShapeDtypeStruct((D_MODEL, n_tiles * tn), BF16),
        scratch_shapes=[pltpu.VMEM((D_MODEL, t), BF16)],
        compiler_params=_cparams(("arbitrary",)),
    )(u, *pieces)


def _mix_dwin(name, u, pieces, tn=256):
    t = u.shape[0]
    nj = D_MODEL // tn
    specs = [pl.BlockSpec((t, tn), lambda s, k=k: (0, jnp.clip(s - k * nj, 0, nj - 1))) for k in range(len(pieces))]
    return _dw_resident(name, u, pieces, specs, len(pieces) * nj, lambda s: s // nj, tn)


def _local_step(x, target, vec, get_w, on_grads, flush):
    t = x.shape[0]
    cos, sin = _rope_tables(t)
    rtab = _retention_tables()
    ptab = _pool_tables()
    w = {}

    def getter(group, name):
        def get(after):
            if name not in w:
                w.update(get_w(group, after))
            return w[name]
        return get

    nrm1 = _rmsnorm_fwd("ffn1_norm", x, vec["norm_ffn1"])
    h1, u, s1 = _ffn_fwd("ffn1", x, nrm1, getter(0, "ffn1_w_in"), getter(1, "ffn1_w_out"), vec["norm_mix"])
    w.update(get_w(2, u))
    proj = _matmul("mix_in", u, w["w_in"], "nn", t, IN_WIDTH, D_MODEL, 1024, 1024, D_MODEL, [F32], n_outer=True)
    o, ret, states = _retention_fwd("retention", proj, cos, sin, rtab)
    pm, mixed, po = _pool_fwd("pool", proj, w["pool_w"], vec["pool_scale"], ptab)
    merged, ru, pu = _merge_fwd("merge", ret, po, w["w_ret_up"], w["w_pool_up"], proj, vec["gate_bias"])
    h2, nrm2 = _matmul("mix_out", merged, w["w_out"], "nn", t, D_MODEL, D_MODEL, 512, D_MODEL, D_MODEL, [F32, BF16],
                       extras=(h1,), consts=(vec["norm_ffn2"],), epilogue=_residual_norm)
    h3, _, s2 = _ffn_fwd("ffn2", h2, nrm2, getter(3, "ffn2_w_in"), getter(3, "ffn2_w_out"))
    dh3, dh3_b, dg_final, loss = _loss_and_grad("loss", h3, vec["norm_final"], target)

    def tied(v, tie):
        return v if tie is None else v + tie

    dh2, dh2_b, dg_ffn2, tie = _ffn_bwd("ffn2", h2, vec["norm_ffn2"], s2, dh3, dh3_b, on_grads, flush)
    def square_dw(name, act, grad):
        return _dw_resident(name, act, [grad], [pl.BlockSpec((t, DW_TILE), lambda s: (0, s))], D_MODEL // DW_TILE, None, DW_TILE)

    d_w_out = square_dw("mix_dwout", merged, dh2_b)
    dru, dpu, dg0, dg1, d_bias = _merge_bwd("merge_bwd", dh2_b, w["w_out"], ru, pu, proj, tied(vec["gate_bias"], tie))
    d_w_ru = square_dw("mix_dwru", ret, dru)
    d_w_pu = square_dw("mix_dwpu", po, dpu)
    dp, d_pool_w, d_scale = _pool_bwd("pool_bwd", dpu, w["w_pool_up"], pm, mixed, w["pool_w"], vec["pool_scale"], ptab)
    dq, dk, dv, dgr = _retention_bwd("retention_bwd", dru, w["w_ret_up"], o, proj, cos, sin, states, rtab)
    dproj = [dq, dk, dv, dgr, dp, dg0, dg1]
    d_w_in = _mix_dwin("mix_dwin", u, dproj)
    tie = on_grads(dict(w_in=d_w_in, pool_w=d_pool_w.astype(BF16), w_ret_up=d_w_ru, w_pool_up=d_w_pu, w_out=d_w_out))
    tm = min(256, t)
    dh1, dh1_b, dg_mix = _proj_norm_bwd("mix_du", dproj, [_row_spec(tm, D_MODEL)] * len(dproj),
                                        [(k, None, k * D_MODEL, (k + 1) * D_MODEL) for k in range(len(dproj))],
                                        w["w_in"], h1, tied(vec["norm_mix"], tie), dh2, tm)
    tie = flush(dh1)
    dx, _, dg_ffn1, _ = _ffn_bwd("ffn1", x, tied(vec["norm_ffn1"], tie), s1, dh1, dh1_b, on_grads, flush)

    small = dict(norm_ffn1=dg_ffn1, norm_mix=dg_mix, gate_bias=d_bias, pool_scale=d_scale, norm_ffn2=dg_ffn2,
                 norm_final=dg_final)
    return loss[0, 0], dx, small


BIG = ("ffn1_w_in", "ffn1_w_out", "w_in", "pool_w", "w_ret_up", "w_pool_up", "w_out", "ffn2_w_in", "ffn2_w_out")
KIND = dict(ffn1_w_in="col", ffn1_w_out="row", w_in="col", pool_w="pool", w_ret_up="row", w_pool_up="row", w_out="row",
            ffn2_w_in="col", ffn2_w_out="row")
ANY = pl.BlockSpec(memory_space=pl.ANY)


def _place():
    x, y, c = lax.axis_index("x"), lax.axis_index("y"), lax.axis_index("c")
    chips = [(1 - x, y), (x, 1 - y), (1 - x, 1 - y)]
    return x, y, c, chips


def _full_view_shape(kind, local_shape):
    if kind == "col":
        return (2, local_shape[0] // 2, N_CHIPS * local_shape[1])
    if kind == "row":
        return (N_CHIPS, 2, local_shape[0] // 2, local_shape[1])
    return (GROUPS, N_CHIPS, 2, local_shape[1] // 2, local_shape[2])


def _local_view(kind, arr):
    if kind == "pool":
        return arr.reshape(GROUPS, 2, arr.shape[1] // 2, arr.shape[2])
    return arr.reshape(2, arr.shape[0] // 2, arr.shape[1])


def _blk(kind, ref, s, c):
    if kind == "col":
        cs = ref.shape[2] // N_CHIPS
        return ref.at[c, :, pl.ds(pl.multiple_of(s * cs, 128), cs)]
    if kind == "row":
        return ref.at[s, c]
    return ref.at[:, s, c]


def _half(kind, ref, c):
    return ref.at[:, c] if kind == "pool" else ref.at[c]


def _shard(kind, ref, s):
    if kind == "col":
        cs = ref.shape[2] // N_CHIPS
        return ref.at[:, :, pl.ds(pl.multiple_of(s * cs, 128), cs)]
    if kind == "row":
        return ref.at[s]
    return ref.at[:, s]


HBM = pl.BlockSpec(memory_space=pltpu.HBM)
SEM = pl.BlockSpec(memory_space=pltpu.SEMAPHORE)
EFFECT = pltpu.SideEffectType.DATAFLOW_SIDE_EFFECTING
WEIGHT_GROUPS = (("ffn1_w_in",), ("ffn1_w_out",), ("w_in", "pool_w", "w_ret_up", "w_pool_up", "w_out"), ("ffn2_w_in", "ffn2_w_out"))
GRAD_GROUPS = (("ffn2_w_in", "ffn2_w_out"), ("w_in", "pool_w", "w_ret_up", "w_pool_up", "w_out"), ("ffn1_w_in", "ffn1_w_out"))


def _hbm(a):
    return pltpu.with_memory_space_constraint(a, pltpu.HBM)


def _natural(kind, o):
    if kind == "col":
        return o.reshape(o.shape[0] * o.shape[1], o.shape[2])
    if kind == "row":
        return o.reshape(-1, o.shape[3])
    return o.reshape(GROUPS, -1, o.shape[4])


def _ici_copy(kind, loc, full, j, chips, s, c, send_sem, recv_sem):
    px, py = chips[j]
    return (pltpu.make_async_remote_copy(src_ref=_half(kind, loc, c), dst_ref=_blk(kind, full, s, c), send_sem=send_sem,
                                         recv_sem=recv_sem, device_id=(px, py, c), device_id_type=MESH),
            pltpu.make_async_remote_copy(src_ref=_half(kind, loc, c), dst_ref=_blk(kind, full, 2 * px + py, c), send_sem=send_sem,
                                         recv_sem=recv_sem, device_id=(px, py, c), device_id_type=MESH))


def _gather_start(tag, group_ids, shards):
    grps = [WEIGHT_GROUPS[g] for g in group_ids]
    names = [nm for grp in grps for nm in grp]
    kinds = [KIND[nm] for nm in names]
    n, ng = len(names), len(grps)
    locs = [_hbm(_local_view(KIND[nm], shards[nm])) for nm in names]
    lands = [_hbm(lax.empty(_full_view_shape(KIND[nm], shards[nm].shape), BF16)) for nm in names]
    first = np.cumsum([0] + [len(grp) for grp in grps])

    def body(*refs):
        loc, full = refs[:n], refs[n:2 * n]
        send_sems, recv_sems = refs[2 * n:2 * n + ng], refs[2 * n + ng:2 * n + 2 * ng]
        token = refs[-1]
        x, y, c, chips = _place()
        s = 2 * x + y
        for g in range(ng):
            for a in range(first[g], first[g + 1]):
                for j in range(3):
                    k = 3 * (a - first[g]) + j
                    _ici_copy(kinds[a], loc[a], full[a], j, chips, s, c, send_sems[g].at[k], recv_sems[g].at[k])[0].start()
        token[...] = jnp.zeros(token.shape, F32)

    sem_shapes = [pltpu.SemaphoreType.DMA((3 * len(grp),)) for grp in grps]
    outs = pl.pallas_call(
        body, name=f"gather_start_{tag}",
        in_specs=[HBM] * (2 * n),
        out_specs=[SEM] * (2 * ng) + [HBM] * (2 * n) + [pl.BlockSpec(memory_space=pltpu.VMEM)],
        out_shape=sem_shapes + sem_shapes + [pltpu.HBM(a.shape, a.dtype) for a in locs + lands] + [jax.ShapeDtypeStruct((8, 128), F32)],
        input_output_aliases={i: 2 * ng + i for i in range(2 * n)},
        compiler_params=pltpu.CompilerParams(has_side_effects=EFFECT),
    )(*locs, *lands)
    send_sems, recv_sems = outs[:ng], outs[ng:2 * ng]
    locs_t, lands_t = outs[2 * ng:2 * ng + n], outs[2 * ng + n:2 * ng + 2 * n]
    groups = {}
    for k, g in enumerate(group_ids):
        sl = slice(first[k], first[k + 1])
        groups[g] = (send_sems[k], recv_sems[k], list(locs_t[sl]), list(lands_t[sl]))
    return groups, outs[-1]


def _gather_finish(g, group, after):
    names = WEIGHT_GROUPS[g]
    kinds = [KIND[nm] for nm in names]
    m = len(names)
    send_sem, recv_sem, locs, lands = group

    def wait_body(*refs):
        loc, full = refs[:m], refs[m:2 * m]
        send_sems, recv_sems = refs[2 * m], refs[2 * m + 1]
        x, y, c, chips = _place()
        s = 2 * x + y
        for a in range(m):
            for j in range(3):
                k = 3 * a + j
                sent, landed = _ici_copy(kinds[a], loc[a], full[a], j, chips, s, c, send_sems.at[k], recv_sems.at[k])
                sent.wait_send()
                landed.wait_recv()

    outs = pl.pallas_call(
        wait_body, name=f"gather_wait_{g}",
        in_specs=[HBM] * (2 * m) + [SEM, SEM] + [ANY] * len(after), out_specs=[HBM] * (2 * m),
        out_shape=[pltpu.HBM(a.shape, a.dtype) for a in locs + lands],
        input_output_aliases={i: i for i in range(2 * m)},
        compiler_params=pltpu.CompilerParams(has_side_effects=EFFECT),
    )(*locs, *lands, send_sem, recv_sem, *after)
    locs, lands = outs[:m], outs[m:]

    def forward_body(*refs):
        loc, full = refs[:m], refs[2 * m:3 * m]
        send_sems, recv_sems = refs[3 * m:]
        x, y, c, chips = _place()
        s = 2 * x + y
        sib = (x, y, 1 - c)

        def remote(a, k, src, dst):
            return pltpu.make_async_remote_copy(src_ref=src, dst_ref=dst, send_sem=send_sems.at[4 * a + k],
                                                recv_sem=recv_sems.at[4 * a + k], device_id=sib, device_id_type=MESH)

        sends = []
        for a in range(m):
            for j, (px, py) in enumerate(chips):
                theirs = _blk(kinds[a], full[a], 2 * px + py, c)
                sends.append(remote(a, j, theirs, theirs))
            sends.append(remote(a, 3, loc[a], _shard(kinds[a], full[a], s)))
        for cp in sends:
            cp.start()
        for a in range(m):
            for j, (px, py) in enumerate(chips):
                from_sib = _blk(kinds[a], full[a], 2 * px + py, 1 - c)
                remote(a, j, from_sib, from_sib).wait_recv()
            own = _shard(kinds[a], full[a], s)
            remote(a, 3, own, own).wait_recv()
        for cp in sends:
            cp.wait_send()

    outs = pl.pallas_call(
        forward_body, name=f"gather_forward_{g}",
        in_specs=[ANY] * (2 * m), out_specs=[ANY] * m,
        out_shape=[jax.ShapeDtypeStruct(a.shape, a.dtype) for a in lands],
        input_output_aliases={m + i: i for i in range(m)},
        scratch_shapes=[pltpu.SemaphoreType.DMA((4 * m,)), pltpu.SemaphoreType.DMA((4 * m,))],
    )(*locs, *lands)
    return {nm: _natural(k, o) for nm, k, o in zip(names, kinds, outs)}


def _grad_view(kind, g):
    if kind == "col":
        return g.reshape(2, g.shape[0] // 2, g.shape[1])
    if kind == "row":
        return g.reshape(N_CHIPS, 2, g.shape[0] // (2 * N_CHIPS), g.shape[1])
    return g.reshape(GROUPS, N_CHIPS, 2, g.shape[1] // (2 * N_CHIPS), g.shape[2])


def _pair_copies(kinds, g, got, send_sems, recv_sems):
    x, y, c, _ = _place()

    def other_half(kind, ref):
        if kind == "col":
            return ref.at[1 - c]
        if kind == "row":
            return ref.at[:, 1 - c]
        return ref.at[:, :, 1 - c]

    return [pltpu.make_async_remote_copy(src_ref=other_half(kinds[a], g[a]), dst_ref=got[a], send_sem=send_sems.at[a],
                                         recv_sem=recv_sems.at[a], device_id=(x, y, 1 - c), device_id_type=MESH)
            for a in range(len(kinds))]


def _pair_exchange_start(tag, names, views):
    kinds = [KIND[nm] for nm in names]
    n = len(names)

    def got_shape(kind, v):
        if kind == "col":
            return v.shape[1:]
        if kind == "row":
            return (v.shape[0],) + v.shape[2:]
        return v.shape[:2] + v.shape[3:]

    srcs = [_hbm(views[nm]) for nm in names]
    lands = [_hbm(lax.empty(got_shape(k, views[nm]), BF16)) for nm, k in zip(names, kinds)]

    def body(*refs):
        g, got = refs[:n], refs[n:2 * n]
        for cp in _pair_copies(kinds, g, got, refs[2 * n], refs[2 * n + 1]):
            cp.start()
        refs[-1][...] = jnp.zeros(refs[-1].shape, F32)

    sem_shape = pltpu.SemaphoreType.DMA((n,))
    outs = pl.pallas_call(
        body, name=f"grad_pair_exchange_start_{tag}",
        in_specs=[HBM] * (2 * n),
        out_specs=[SEM, SEM] + [HBM] * (2 * n) + [pl.BlockSpec(memory_space=pltpu.VMEM)],
        out_shape=[sem_shape, sem_shape] + [pltpu.HBM(a.shape, a.dtype) for a in srcs + lands] + [jax.ShapeDtypeStruct((8, 128), F32)],
        input_output_aliases={i: 2 + i for i in range(2 * n)},
        compiler_params=pltpu.CompilerParams(has_side_effects=EFFECT),
    )(*srcs, *lands)
    return (outs[0], outs[1], list(outs[2:2 + n]), list(outs[2 + n:2 + 2 * n])), outs[-1]


def _pair_exchange_wait(tag, names, state, after):
    kinds = [KIND[nm] for nm in names]
    n = len(names)
    send_sem, recv_sem, srcs, lands = state

    def body(*refs):
        g, got = refs[:n], refs[n:2 * n]
        for cp in _pair_copies(kinds, g, got, refs[2 * n], refs[2 * n + 1]):
            cp.wait_send()
            cp.wait_recv()

    outs = pl.pallas_call(
        body, name=f"grad_pair_exchange_wait_{tag}",
        in_specs=[HBM] * (2 * n) + [SEM, SEM, ANY], out_specs=[HBM] * (2 * n),
        out_shape=[pltpu.HBM(a.shape, a.dtype) for a in srcs + lands],
        input_output_aliases={i: i for i in range(2 * n)},
        compiler_params=pltpu.CompilerParams(has_side_effects=EFFECT),
    )(*srcs, *lands, send_sem, recv_sem, after)
    return dict(zip(names, outs[:n])), dict(zip(names, outs[n:]))


def _pair_sum(name, kind, view, got, c_arr):
    if kind == "col":
        _, rows, cols = view.shape
        tr = 128
        grid = (rows // tr,)
        v_spec = pl.BlockSpec((None, tr, cols), lambda i, c: (c[0], i, 0))
        g_spec = pl.BlockSpec((tr, cols), lambda i, c: (i, 0))
    elif kind == "row":
        _, _, rows, cols = view.shape
        grid = (N_CHIPS,)
        v_spec = pl.BlockSpec((None, None, rows, cols), lambda i, c: (i, c[0], 0, 0))
        g_spec = pl.BlockSpec((None, rows, cols), lambda i, c: (i, 0, 0))
    else:
        _, _, _, rows, cols = view.shape
        grid = (GROUPS,)
        v_spec = pl.BlockSpec((None, N_CHIPS, None, rows, cols), lambda i, c: (i, 0, c[0], 0, 0))
        g_spec = pl.BlockSpec((None, N_CHIPS, rows, cols), lambda i, c: (i, 0, 0, 0))

    def body(c_ref, v_ref, g_ref, o_ref):
        o_ref[...] = (v_ref[...].astype(F32) + g_ref[...].astype(F32)).astype(BF16)

    return pl.pallas_call(
        body, name=name,
        grid_spec=pltpu.PrefetchScalarGridSpec(num_scalar_prefetch=1, grid=grid, in_specs=[v_spec, g_spec], out_specs=g_spec),
        out_shape=jax.ShapeDtypeStruct(got.shape, BF16),
        compiler_params=_cparams(("parallel",)),
    )(c_arr, view, got)


def _piece(kind, ref, s):
    if kind == "col":
        cs = ref.shape[1] // N_CHIPS
        return ref.at[:, pl.ds(pl.multiple_of(s * cs, 128), cs)]
    if kind == "row":
        return ref.at[s]
    return ref.at[:, s]


def _piece_shape(kind, shape):
    if kind == "col":
        return (shape[0], shape[1] // N_CHIPS)
    if kind == "row":
        return shape[1:]
    return (shape[0],) + shape[2:]


def _shard_copies(kinds, p, got, send_sems, recv_sems):
    x, y, c, chips = _place()
    return [pltpu.make_async_remote_copy(src_ref=_piece(kinds[a], p[a], 2 * px + py), dst_ref=got[a].at[j],
                                         send_sem=send_sems.at[3 * a + j], recv_sem=recv_sems.at[3 * a + j],
                                         device_id=(px, py, c), device_id_type=MESH)
            for a in range(len(kinds)) for j, (px, py) in enumerate(chips)]


def _shard_exchange_start(g, names, psums):
    kinds = [KIND[nm] for nm in names]
    n = len(names)
    srcs = [_hbm(psums[nm]) for nm in names]
    lands = [_hbm(lax.empty((3,) + _piece_shape(k, psums[nm].shape), BF16)) for nm, k in zip(names, kinds)]

    def body(*refs):
        p, got = refs[:n], refs[n:2 * n]
        send_sems, recv_sems = refs[2 * n], refs[2 * n + 1]
        token = refs[-1]
        for cp in _shard_copies(kinds, p, got, send_sems, recv_sems):
            cp.start()
        token[...] = jnp.zeros(token.shape, F32)

    sem_shape = pltpu.SemaphoreType.DMA((3 * n,))
    outs = pl.pallas_call(
        body, name=f"grad_shard_exchange_start_{g}",
        in_specs=[HBM] * (2 * n),
        out_specs=[SEM, SEM] + [HBM] * (2 * n) + [pl.BlockSpec(memory_space=pltpu.VMEM)],
        out_shape=[sem_shape, sem_shape] + [pltpu.HBM(a.shape, a.dtype) for a in srcs + lands] + [jax.ShapeDtypeStruct((8, 128), F32)],
        input_output_aliases={i: 2 + i for i in range(2 * n)},
        compiler_params=pltpu.CompilerParams(has_side_effects=EFFECT),
    )(*srcs, *lands)
    return (outs[0], outs[1], list(outs[2:2 + n]), list(outs[2 + n:2 + 2 * n])), outs[-1]


def _shard_exchange_wait(g, names, state, after):
    kinds = [KIND[nm] for nm in names]
    n = len(names)
    send_sem, recv_sem, srcs, lands = state

    def body(*refs):
        p, got = refs[:n], refs[n:2 * n]
        for cp in _shard_copies(kinds, p, got, refs[2 * n], refs[2 * n + 1]):
            cp.wait_send()
            cp.wait_recv()

    outs = pl.pallas_call(
        body, name=f"grad_shard_exchange_wait_{g}",
        in_specs=[HBM] * (2 * n) + [SEM, SEM, ANY], out_specs=[HBM] * (2 * n),
        out_shape=[pltpu.HBM(a.shape, a.dtype) for a in srcs + lands],
        input_output_aliases={i: i for i in range(2 * n)},
        compiler_params=pltpu.CompilerParams(has_side_effects=EFFECT),
    )(*srcs, *lands, send_sem, recv_sem, after)
    return dict(zip(names, outs[:n])), dict(zip(names, outs[n:]))


def _shard_sum(name, kind, psum, got, sc_arr):
    if kind == "col":
        rows, cols = psum.shape
        cs = cols // N_CHIPS
        tr = 128
        grid = (rows // tr,)
        p_spec = pl.BlockSpec((tr, cs), lambda i, sc: (i, sc[0]))
        g_spec = pl.BlockSpec((3, tr, cs), lambda i, sc: (0, i, 0))
        o_spec = pl.BlockSpec((None, tr, cs), lambda i, sc: (sc[1], i, 0))
        out_shape = (2, rows, cs)
    elif kind == "row":
        _, rows, cols = psum.shape
        grid = (1,)
        p_spec = pl.BlockSpec((None, rows, cols), lambda i, sc: (sc[0], 0, 0))
        g_spec = pl.BlockSpec((3, rows, cols), lambda i, sc: (0, 0, 0))
        o_spec = pl.BlockSpec((None, rows, cols), lambda i, sc: (sc[1], 0, 0))
        out_shape = (2, rows, cols)
    else:
        _, _, rows, cols = psum.shape
        grid = (1,)
        p_spec = pl.BlockSpec((GROUPS, None, rows, cols), lambda i, sc: (0, sc[0], 0, 0))
        g_spec = pl.BlockSpec((3, GROUPS, rows, cols), lambda i, sc: (0, 0, 0, 0))
        o_spec = pl.BlockSpec((GROUPS, None, rows, cols), lambda i, sc: (0, sc[1], 0, 0))
        out_shape = (GROUPS, 2, rows, cols)

    def body(sc_ref, p_ref, g_ref, o_ref):
        o_ref[...] = ((p_ref[...].astype(F32) + g_ref[0].astype(F32)) + g_ref[1].astype(F32)) + g_ref[2].astype(F32)

    return pl.pallas_call(
        body, name=name,
        grid_spec=pltpu.PrefetchScalarGridSpec(num_scalar_prefetch=1, grid=grid, in_specs=[p_spec, g_spec], out_specs=o_spec),
        out_shape=jax.ShapeDtypeStruct(out_shape, F32),
        compiler_params=_cparams(("parallel",)),
    )(sc_arr, psum, got)


def _half_exchange(tag, names, bufs):
    kinds = [KIND[nm] for nm in names]
    n = len(names)

    def body(*refs):
        out = refs[n:2 * n]
        send_sems, recv_sems = refs[2 * n:]
        x, y, c, _ = _place()
        sib = (x, y, 1 - c)
        cps = []
        for a in range(n):
            mine = _half(kinds[a], out[a], c)
            cp = pltpu.make_async_remote_copy(src_ref=mine, dst_ref=mine, send_sem=send_sems.at[a], recv_sem=recv_sems.at[a],
                                              device_id=sib, device_id_type=MESH)
            cp.start()
            cps.append(cp)
        for a, cp in enumerate(cps):
            cp.wait_send()
            theirs = _half(kinds[a], out[a], 1 - c)
            pltpu.make_async_remote_copy(src_ref=theirs, dst_ref=theirs, send_sem=send_sems.at[a], recv_sem=recv_sems.at[a],
                                         device_id=sib, device_id_type=MESH).wait_recv()

    outs = pl.pallas_call(
        body, name=f"grad_half_exchange_{tag}",
        in_specs=[ANY] * n, out_specs=[ANY] * n,
        out_shape=[jax.ShapeDtypeStruct(bufs[nm].shape, F32) for nm in names],
        input_output_aliases={a: a for a in range(n)},
        scratch_shapes=[pltpu.SemaphoreType.DMA((n,)), pltpu.SemaphoreType.DMA((n,))],
    )(*[bufs[nm] for nm in names])
    return dict(zip(names, outs))


N_DEV = 8
SMALL_ROWS = 8


def _all_reduce_small(name, v):
    def body(v_ref, o_ref, token, buf, send_sems, recv_sems):
        token[...] = jnp.zeros(token.shape, F32)
        x, y, c, _ = _place()
        me = 4 * x + 2 * y + c
        buf[me] = v_ref[...]
        cps = []
        for r in range(1, N_DEV):
            to = (x ^ (r >> 2), y ^ ((r >> 1) & 1), c ^ (r & 1))
            cp = pltpu.make_async_remote_copy(src_ref=v_ref, dst_ref=buf.at[me], send_sem=send_sems.at[r - 1],
                                              recv_sem=recv_sems.at[r - 1], device_id=to, device_id_type=MESH)
            cp.start()
            cps.append(cp)
        for r in range(1, N_DEV):
            pltpu.make_async_remote_copy(src_ref=v_ref, dst_ref=buf.at[me ^ r], send_sem=send_sems.at[r - 1],
                                         recv_sem=recv_sems.at[r - 1], device_id=(x, y, c), device_id_type=MESH).wait_recv()
        for cp in cps:
            cp.wait_send()
        acc = buf[0]
        for d in range(1, N_DEV):
            acc = acc + buf[d]
        o_ref[...] = acc

    vm = pl.BlockSpec(memory_space=pltpu.VMEM)
    return pl.pallas_call(
        body, name=name, in_specs=[vm], out_specs=[vm, vm],
        out_shape=[jax.ShapeDtypeStruct((SMALL_ROWS, D_MODEL), F32), jax.ShapeDtypeStruct((8, 128), F32)],
        scratch_shapes=[pltpu.VMEM((N_DEV, SMALL_ROWS, D_MODEL), F32), pltpu.SemaphoreType.DMA((N_DEV - 1,)),
                        pltpu.SemaphoreType.DMA((N_DEV - 1,))],
    )(v)


def _adamw(name, w, g, m, v):
    rows, cols = w.shape
    tr = next((c for c in (256, 176, 128, 64, 32, 8) if rows % c == 0), rows)
    spec = pl.BlockSpec((tr, cols), lambda i: (i, 0))

    def body(w_ref, g_ref, m_ref, v_ref, d_ref, mo_ref, vo_ref):
        gv = g_ref[...]
        m_new = ADAM_B1 * m_ref[...] + (1.0 - ADAM_B1) * gv
        v_new = ADAM_B2 * v_ref[...] + (1.0 - ADAM_B2) * jnp.square(gv)
        m_hat = m_new / (1.0 - ADAM_B1 ** ADAM_STEP)
        v_hat = v_new / (1.0 - ADAM_B2 ** ADAM_STEP)
        d_ref[...] = -ADAM_LR * (m_hat / (jnp.sqrt(v_hat) + ADAM_EPS) + ADAM_WD * w_ref[...])
        mo_ref[...] = m_new
        vo_ref[...] = v_new

    return pl.pallas_call(
        body, name=name, grid=(rows // tr,),
        in_specs=[spec] * 4, out_specs=[spec] * 3,
        out_shape=[jax.ShapeDtypeStruct((rows, cols), F32)] * 3,
        compiler_params=_cparams(("parallel",)),
    )(w, g, m, v)


WEIGHTS = ("norm_ffn1", "ffn1_w_in", "ffn1_w_out", "norm_mix", "w_in", "gate_bias", "pool_w", "pool_scale", "w_ret_up",
           "w_pool_up", "w_out", "norm_ffn2", "ffn2_w_in", "ffn2_w_out", "norm_final")
SMALL_ROW = dict(norm_ffn1=0, norm_mix=1, gate_bias=2, pool_scale=4, norm_ffn2=5, norm_final=6)


def _as2d(a):
    return a.reshape(-1, a.shape[-1])


def kernel(x, norm_ffn1, ffn1_w_in, ffn1_w_out, norm_mix, w_in, gate_bias, pool_w, pool_scale, w_ret_up, w_pool_up, w_out, norm_ffn2, ffn2_w_in, ffn2_w_out, norm_final, loss_target, m_norm_ffn1, m_ffn1_w_in, m_ffn1_w_out, m_norm_mix, m_w_in, m_gate_bias, m_pool_w, m_pool_scale, m_w_ret_up, m_w_pool_up, m_w_out, m_norm_ffn2, m_ffn2_w_in, m_ffn2_w_out, m_norm_final, v_norm_ffn1, v_ffn1_w_in, v_ffn1_w_out, v_norm_mix, v_w_in, v_gate_bias, v_pool_w, v_pool_scale, v_w_ret_up, v_w_pool_up, v_w_out, v_norm_ffn2, v_ffn2_w_in, v_ffn2_w_out, v_norm_final):
    wt = dict(norm_ffn1=norm_ffn1, ffn1_w_in=ffn1_w_in, ffn1_w_out=ffn1_w_out, norm_mix=norm_mix, w_in=w_in, gate_bias=gate_bias,
              pool_w=pool_w, pool_scale=pool_scale, w_ret_up=w_ret_up, w_pool_up=w_pool_up, w_out=w_out, norm_ffn2=norm_ffn2,
              ffn2_w_in=ffn2_w_in, ffn2_w_out=ffn2_w_out, norm_final=norm_final)
    mom = dict(norm_ffn1=m_norm_ffn1, ffn1_w_in=m_ffn1_w_in, ffn1_w_out=m_ffn1_w_out, norm_mix=m_norm_mix, w_in=m_w_in,
               gate_bias=m_gate_bias, pool_w=m_pool_w, pool_scale=m_pool_scale, w_ret_up=m_w_ret_up, w_pool_up=m_w_pool_up,
               w_out=m_w_out, norm_ffn2=m_norm_ffn2, ffn2_w_in=m_ffn2_w_in, ffn2_w_out=m_ffn2_w_out, norm_final=m_norm_final)
    var = dict(norm_ffn1=v_norm_ffn1, ffn1_w_in=v_ffn1_w_in, ffn1_w_out=v_ffn1_w_out, norm_mix=v_norm_mix, w_in=v_w_in,
               gate_bias=v_gate_bias, pool_w=v_pool_w, pool_scale=v_pool_scale, w_ret_up=v_w_ret_up, w_pool_up=v_w_pool_up,
               w_out=v_w_out, norm_ffn2=v_norm_ffn2, ffn2_w_in=v_ffn2_w_in, ffn2_w_out=v_ffn2_w_out, norm_final=v_norm_final)

    ax, ay, ac = lax.axis_index("x"), lax.axis_index("y"), lax.axis_index("c")
    chip = 2 * ax + ay
    c_arr = jnp.reshape(ac, (1,)).astype(jnp.int32)
    sc_arr = jnp.stack([chip, ac]).astype(jnp.int32)
    bias_cols = gate_bias.shape[-1]

    placed = lax.dynamic_update_slice(jnp.zeros((SMALL_ROWS, D_MODEL), F32), gate_bias[0], (0, chip * bias_cols))
    bias_sum, token = _all_reduce_small("gather_gate_bias", jnp.where(ac == 0, placed, 0.0))
    bias_full = bias_sum[:2]
    first = WEIGHT_GROUPS[0]
    gather_groups, token = _gather_start("first", [0], {nm: wt[nm][0].astype(BF16) + token[0, 0].astype(BF16) for nm in first})
    rest, rest_token = _gather_start("rest", [1, 2, 3],
                                     {nm: wt[nm][0].astype(BF16) + token[0, 0].astype(BF16) for nm in BIG if nm not in first})
    gather_groups.update(rest)
    vec = dict(norm_ffn1=norm_ffn1, norm_mix=norm_mix, norm_ffn2=norm_ffn2, pool_scale=pool_scale,
               norm_final=norm_final.reshape(1, D_MODEL), gate_bias=bias_full)

    def get_w(g, after):
        return _gather_finish(g, gather_groups[g], (after, rest_token) if g == 0 else (after,))

    pairs, pending = [], []

    def on_grads(gr):
        g = len(pairs)
        names = GRAD_GROUPS[g]
        assert set(names) == set(gr), (names, list(gr))
        state, token = _pair_exchange_start(g, names, {nm: _grad_view(KIND[nm], gr[nm]) for nm in names})
        pairs.append(state)
        return token[0:1, 0:1]

    def flush(after):
        g = len(pending)
        names = GRAD_GROUPS[g]
        views, from_sib = _pair_exchange_wait(g, names, pairs[g], after)
        psums = {nm: _pair_sum(f"pair_sum_{nm}", KIND[nm], views[nm], from_sib[nm], c_arr) for nm in names}
        state, token = _shard_exchange_start(g, names, psums)
        pending.append(state)
        tokens.append(token)
        return token[0:1, 0:1]

    tokens = []
    loss_local, dx, small = _local_step(x[0], loss_target[0], vec, get_w, on_grads, flush)

    packed = jnp.concatenate([small["norm_ffn1"], small["norm_mix"], small["gate_bias"], small["pool_scale"],
                              small["norm_ffn2"], small["norm_final"], jnp.broadcast_to(loss_local, (1, D_MODEL))], axis=0)
    small_sum, _ = _all_reduce_small("reduce_small_grads", packed)
    loss = small_sum[SMALL_ROWS - 1, 0]
    grads, delta, new_m, new_v = {}, {}, {}, {}

    def adamw(nm):
        shape = wt[nm].shape
        d, m2, v2 = _adamw(f"adamw_{nm}", _as2d(wt[nm]), _as2d(grads[nm]), _as2d(mom[nm]), _as2d(var[nm]))
        delta[nm], new_m[nm], new_v[nm] = d.reshape(shape), m2.reshape(shape), v2.reshape(shape)
        return d

    for nm in ("norm_ffn1", "norm_mix", "pool_scale", "norm_ffn2"):
        grads[nm] = small_sum[SMALL_ROW[nm]][None, :]
    grads["norm_final"] = small_sum[SMALL_ROW["norm_final"]]
    grads["gate_bias"] = lax.dynamic_slice(small_sum, (SMALL_ROW["gate_bias"], chip * bias_cols), (2, bias_cols))[None]
    after = tokens[-1]
    for g, names in enumerate(GRAD_GROUPS):
        psums, from_chips = _shard_exchange_wait(g, names, pending[g], after)
        bufs = {nm: _shard_sum(f"shard_sum_{nm}", KIND[nm], psums[nm], from_chips[nm], sc_arr) for nm in names}
        reduced = _half_exchange(g, names, bufs)
        for nm in names:
            grads[nm] = reduced[nm].reshape(wt[nm].shape)
            after = adamw(nm)
    for nm in WEIGHTS:
        if nm not in delta:
            adamw(nm)

    return (loss, dx[None], *[grads[nm] for nm in WEIGHTS], *[delta[nm] for nm in WEIGHTS],
            *[new_m[nm] for nm in WEIGHTS], *[new_v[nm] for nm in WEIGHTS])
```

```python
import functools

import numpy as np
import jax
import jax.numpy as jnp
from jax import lax
from jax.experimental import pallas as pl
from jax.experimental.pallas import tpu as pltpu

F32 = jnp.float32
BF16 = jnp.bfloat16
MESH = pl.DeviceIdType.MESH

D_MODEL = 1024
D_FF = 2816
HEADS = 4
HEAD_DIM = 256
GROUPS = 4
GROUP_DIM = 256
POOL_WINDOWS = (2, 4, 8, 16)
IN_WIDTH = 7 * D_MODEL
ROPE_BASE = 10000.0
NORM_EPS = 1e-6
FFN_RES_WEIGHT = 0.5
ADAM_LR, ADAM_B1, ADAM_B2, ADAM_EPS, ADAM_WD, ADAM_STEP = 0.001, 0.9, 0.999, 1e-08, 0.01, 10

N_CHIPS = 4
RET_BLOCK = 256
V7X_VMEM_LIMIT = 48 * 1024 * 1024


def _cparams(sem):
    return pltpu.CompilerParams(dimension_semantics=sem, vmem_limit_bytes=V7X_VMEM_LIMIT)


def _sigmoid(x):
    return jax.nn.sigmoid(x)


_DIMS = {"nn": (((1,), (0,)), ((), ())), "nt": (((1,), (1,)), ((), ())), "tn": (((0,), (0,)), ((), ()))}


def _matmul(name, a, b, mode, m, n, k, tm, tn, tk, out_dtypes, a_spec=None, b_spec=None, extras=(), consts=(), epilogue=None,
            resident=None, n_outer=False):
    tm, tn, tk = min(tm, m), min(tn, n), min(tk, k)
    gi, gj, gk = m // tm, n // tn, k // tk
    assert gi * tm == m and gj * tn == n and gk * tk == k, (name, m, n, k, tm, tn, tk)
    assert not (n_outer and (a_spec is not None or b_spec is not None)), name
    once = dict(pipeline_mode=pl.Buffered(1))

    def spec(shape, index, **kw):
        return pl.BlockSpec(shape, (lambda j, i, kk: index(i, j, kk)) if n_outer else index, **kw)

    if a_spec is None:
        kw = once if resident == "a" else {}
        a_spec = (spec((tk, tm), lambda i, j, kk: (kk, i), **kw) if mode == "tn"
                  else spec((tm, tk), lambda i, j, kk: (i, kk), **kw))
    if b_spec is None:
        kw = once if resident == "b" else {}
        b_spec = (spec((tn, tk), lambda i, j, kk: (j, kk), **kw) if mode == "nt"
                  else spec((tk, tn), lambda i, j, kk: (kk, j), **kw))
    n_ex, n_out = len(extras) + len(consts), len(out_dtypes)
    dims = _DIMS[mode]

    def body(a_ref, b_ref, *rest):
        ex_refs, out_refs = rest[:n_ex], rest[n_ex:n_ex + n_out]

        def finish(acc):
            outs = (acc,) if epilogue is None else epilogue(acc, *[e[...] for e in ex_refs])
            for o_ref, o in zip(out_refs, outs):
                o_ref[...] = o.astype(o_ref.dtype)

        prod = lax.dot_general(a_ref[...], b_ref[...], dims, preferred_element_type=F32)
        if gk == 1:
            finish(prod)
        else:
            acc_ref = rest[n_ex + n_out]
            kk = pl.program_id(2)

            @pl.when(kk == 0)
            def _():
                acc_ref[...] = prod

            @pl.when(kk > 0)
            def _():
                acc_ref[...] += prod

            @pl.when(kk == gk - 1)
            def _():
                finish(acc_ref[...])

    o_spec = spec((tm, tn), lambda i, j, kk: (i, j))
    outs = pl.pallas_call(
        body, name=name, grid=(gj, gi, gk) if n_outer else (gi, gj, gk),
        in_specs=[a_spec, b_spec] + [o_spec] * len(extras) + [spec((1, tn), lambda i, j, kk: (0, j))] * len(consts),
        out_specs=[o_spec] * n_out,
        out_shape=[jax.ShapeDtypeStruct((m, n), dt) for dt in out_dtypes],
        scratch_shapes=[pltpu.VMEM((tm, tn), F32)] if gk > 1 else [],
        compiler_params=_cparams(("parallel", "parallel", "arbitrary")),
    )(a, b, *extras, *consts)
    return outs[0] if n_out == 1 else outs


def _row_spec(tm, width, col_block=0):
    return pl.BlockSpec((tm, width), lambda i: (i, col_block))


def _full_spec(shape):
    return pl.BlockSpec(shape, lambda *_: (0,) * len(shape))


def _rmsnorm_fwd(name, h, g, tm=512):
    t = h.shape[0]

    def body(h_ref, g_ref, o_ref):
        x = h_ref[...]
        r = lax.rsqrt(jnp.mean(x * x, axis=-1, keepdims=True) + NORM_EPS)
        o_ref[...] = (x * r * g_ref[...]).astype(BF16)

    return pl.pallas_call(
        body, name=name, grid=(t // tm,),
        in_specs=[_row_spec(tm, D_MODEL), _full_spec((1, D_MODEL))],
        out_specs=_row_spec(tm, D_MODEL),
        out_shape=jax.ShapeDtypeStruct((t, D_MODEL), BF16),
        compiler_params=_cparams(("parallel",)),
    )(h, g)


def _proj_norm_bwd(name, a_list, a_specs, parts, w, h, g, dres, tm):
    t = h.shape[0]
    na = len(a_list)

    def body(*refs):
        a_refs = refs[:na]
        w_ref, h_ref, g_ref, dres_ref, dh_ref, dhb_ref, dg_ref = refs[na:]
        i = pl.program_id(0)
        dn_v = None
        for which, lead, k0, k1 in parts:
            a_ref = a_refs[which]
            term = _dot(a_ref[...] if lead is None else a_ref[lead], w_ref[:, k0:k1], "nt")
            dn_v = term if dn_v is None else dn_v + term
        x = h_ref[...]
        r = lax.rsqrt(jnp.mean(x * x, axis=-1, keepdims=True) + NORM_EPS)
        xh = x * r
        dxh = dn_v * g_ref[...]
        dh = dres_ref[...] + r * (dxh - xh * jnp.mean(dxh * xh, axis=-1, keepdims=True))
        dh_ref[...] = dh
        dhb_ref[...] = dh.astype(BF16)
        part = jnp.sum(dn_v * xh, axis=0, keepdims=True)

        @pl.when(i == 0)
        def _():
            dg_ref[...] = part

        @pl.when(i > 0)
        def _():
            dg_ref[...] += part

    row = _row_spec(tm, D_MODEL)
    return pl.pallas_call(
        body, name=name, grid=(t // tm,),
        in_specs=list(a_specs) + [pl.BlockSpec(w.shape, lambda i: (0, 0), pipeline_mode=pl.Buffered(1)), row,
                                  _full_spec((1, D_MODEL)), row],
        out_specs=[row, row, _full_spec((1, D_MODEL))],
        out_shape=[jax.ShapeDtypeStruct((t, D_MODEL), F32), jax.ShapeDtypeStruct((t, D_MODEL), BF16),
                   jax.ShapeDtypeStruct((1, D_MODEL), F32)],
        compiler_params=_cparams(("arbitrary",)),
    )(*a_list, w, h, g, dres)


def _loss_and_grad(name, h, g, target, tm=512):
    t = h.shape[0]

    def body(h_ref, g_ref, t_ref, dh_ref, dhb_ref, dg_ref, loss_ref):
        i = pl.program_id(0)
        x = h_ref[...]
        gv = g_ref[...]
        r = lax.rsqrt(jnp.mean(x * x, axis=-1, keepdims=True) + NORM_EPS)
        xh = x * r
        err = xh * gv - t_ref[...]
        row = jnp.mean(err * err, axis=-1, keepdims=True)
        part_loss = 0.5 * jnp.sum(row, axis=0, keepdims=True)
        dy = err * (1.0 / D_MODEL)
        dxh = dy * gv
        dh = r * (dxh - xh * jnp.mean(dxh * xh, axis=-1, keepdims=True))
        dh_ref[...] = dh
        dhb_ref[...] = dh.astype(BF16)
        part = jnp.sum(dy * xh, axis=0, keepdims=True)

        @pl.when(i == 0)
        def _():
            dg_ref[...] = part
            loss_ref[...] = jnp.zeros(loss_ref.shape, F32) + part_loss

        @pl.when(i > 0)
        def _():
            dg_ref[...] += part
            loss_ref[...] += part_loss

    return pl.pallas_call(
        body, name=name, grid=(t // tm,),
        in_specs=[_row_spec(tm, D_MODEL), _full_spec((1, D_MODEL)), _row_spec(tm, D_MODEL)],
        out_specs=[_row_spec(tm, D_MODEL), _row_spec(tm, D_MODEL), _full_spec((1, D_MODEL)), _full_spec((8, 128))],
        out_shape=[jax.ShapeDtypeStruct((t, D_MODEL), F32), jax.ShapeDtypeStruct((t, D_MODEL), BF16),
                   jax.ShapeDtypeStruct((1, D_MODEL), F32), jax.ShapeDtypeStruct((8, 128), F32)],
        compiler_params=_cparams(("arbitrary",)),
    )(h, g, target)


def _rope_tables(t):
    half = HEAD_DIM // 2
    inv_freq = np.float32(ROPE_BASE) ** (-np.arange(half, dtype=np.float32) / np.float32(half))
    ang = (np.arange(t, dtype=np.float32)[:, None] * inv_freq[None, :].astype(np.float32)).astype(np.float32)
    return jnp.asarray(np.cos(ang.astype(np.float64)).astype(np.float32)), jnp.asarray(np.sin(ang.astype(np.float64)).astype(np.float32))


ROPE_HALF = HEAD_DIM // 2
K_SCALE = HEAD_DIM ** -0.5


def _rotate(ref, hh, c, s, scale=None):
    lo, mid, hi = hh * HEAD_DIM, hh * HEAD_DIM + ROPE_HALF, (hh + 1) * HEAD_DIM
    x1, x2 = ref[:, lo:mid].astype(F32), ref[:, mid:hi].astype(F32)
    y = jnp.concatenate([x1 * c - x2 * s, x1 * s + x2 * c], axis=1)
    return y if scale is None else y * scale


def _unrotate_into(ref, hh, dy, c, s, scale=None):
    lo, mid, hi = hh * HEAD_DIM, hh * HEAD_DIM + ROPE_HALF, (hh + 1) * HEAD_DIM
    y1, y2 = dy[:, :ROPE_HALF], dy[:, ROPE_HALF:]
    d1, d2 = y1 * c + y2 * s, y2 * c - y1 * s
    if scale is not None:
        d1, d2 = d1 * scale, d2 * scale
    ref[:, lo:mid] = d1.astype(ref.dtype)
    ref[:, mid:hi] = d2.astype(ref.dtype)


def _retention_tables():
    b, chunk = RET_BLOCK, 64
    gamma = 1.0 - 2.0 ** (-5.0 - np.arange(HEADS, dtype=np.float64))
    log_g = np.log(gamma)[:, None, None]
    i = np.arange(b)[:, None]
    j = np.arange(b)[None, :]
    same = (i // chunk) == (j // chunk)
    earlier = (j // chunk) < (i // chunk)
    expo = np.where(same, np.abs(i - j), np.where(earlier, i - j, 0)).astype(np.float64)
    mask = np.where(same | earlier, 1.0, 0.0)
    dmat = np.exp(log_g * expo[None]) * mask[None]
    qd = np.exp(log_g[:, :, 0] * (np.arange(b)[None, :] + 1.0))
    kd = np.exp(log_g[:, :, 0] * (b - 1.0 - np.arange(b)[None, :]))
    cd = np.exp(log_g[:, :, 0] * b) * np.ones((1, HEAD_DIM))
    as32 = lambda v: jnp.asarray(v.astype(np.float32))
    return (as32(dmat), as32(np.swapaxes(dmat, 1, 2)), as32(qd[:, :, None]), as32(kd[:, :, None]), as32(cd[:, None, :]))


def _dot(a, b, mode="nn"):
    return lax.dot_general(a, b, _DIMS[mode], preferred_element_type=F32)


GRET_BLOCK = 3


def _head_specs(nb, rev=False):
    pos = (lambda n: nb - 1 - n) if rev else (lambda n: n)
    tok = pl.BlockSpec((RET_BLOCK, D_MODEL), lambda n: (pos(n), 0))
    blk = [pl.BlockSpec((RET_BLOCK, D_MODEL), lambda n, b=b: (pos(n), b)) for b in range(GRET_BLOCK + 1)]
    rope = pl.BlockSpec((RET_BLOCK, ROPE_HALF), lambda n: (pos(n), 0))
    tab = _full_spec((HEADS, RET_BLOCK, RET_BLOCK))
    col = _full_spec((HEADS, RET_BLOCK, 1))
    rowv = _full_spec((HEADS, 1, HEAD_DIM))
    st = pl.BlockSpec((HEADS, None, HEAD_DIM, HEAD_DIM), lambda n: (0, pos(n), 0, 0))
    return tok, blk, rope, tab, col, rowv, st


def _retention_fwd(name, proj, cos, sin, tables):
    t = proj.shape[0]
    nb = t // RET_BLOCK
    dmat, _, qd, kd, cd = tables
    tok, blk, rope, tab, col, rowv, st = _head_specs(nb)

    def body(q_ref, k_ref, v_ref, g_ref, c_ref, s_ref, d_ref, qd_ref, kd_ref, cd_ref, o_ref, ret_ref, st_ref, state):
        n = pl.program_id(0)

        @pl.when(n == 0)
        def _():
            state[...] = jnp.zeros(state.shape, F32)

        cs, sn = c_ref[...], s_ref[...]
        for hh in range(HEADS):
            sl = slice(hh * HEAD_DIM, (hh + 1) * HEAD_DIM)
            q, k, v = _rotate(q_ref, hh, cs, sn), _rotate(k_ref, hh, cs, sn, K_SCALE), v_ref[:, sl].astype(BF16)
            s = _dot(q.astype(BF16), k.astype(BF16), "nt") * d_ref[hh]
            stb = state[hh].astype(BF16)
            st_ref[hh] = stb
            o = _dot(s.astype(BF16), v) + _dot((q * qd_ref[hh]).astype(BF16), stb)
            o_ref[:, sl] = o
            rn = o * lax.rsqrt(jnp.mean(o * o, axis=-1, keepdims=True) + NORM_EPS)
            g = g_ref[:, sl].astype(F32)
            ret_ref[:, sl] = (rn * (g * _sigmoid(g))).astype(BF16)
            state[hh] = state[hh] * cd_ref[hh] + _dot((k * kd_ref[hh]).astype(BF16), v, "tn")

    return pl.pallas_call(
        body, name=name, grid=(nb,),
        in_specs=blk + [rope, rope, tab, col, col, rowv],
        out_specs=[tok, tok, st],
        out_shape=[jax.ShapeDtypeStruct((t, D_MODEL), F32), jax.ShapeDtypeStruct((t, D_MODEL), BF16),
                   jax.ShapeDtypeStruct((HEADS, nb, HEAD_DIM, HEAD_DIM), BF16)],
        scratch_shapes=[pltpu.VMEM((HEADS, HEAD_DIM, HEAD_DIM), F32)],
        compiler_params=_cparams(("arbitrary",)),
    )(proj, proj, proj, proj, cos, sin, dmat, qd, kd, cd)


def _retention_bwd(name, dru, w_ru, o, proj, cos, sin, states, tables):
    t = proj.shape[0]
    nb = t // RET_BLOCK
    dmat, dmat_t, qd, kd, cd = tables
    tok, blk, rope, tab, col, rowv, st = _head_specs(nb, rev=True)

    def body(dru_ref, wru_ref, o_ref, q_ref, k_ref, v_ref, g_ref, c_ref, s_ref, st_ref, d_ref, dt_ref, qd_ref, kd_ref, cd_ref,
             dq_ref, dk_ref, dv_ref, dg_ref, gstate):
        n = pl.program_id(0)

        @pl.when(n == 0)
        def _():
            gstate[...] = jnp.zeros(gstate.shape, F32)

        cs, sn = c_ref[...], s_ref[...]
        dret = _dot(dru_ref[...], wru_ref[...], "nt")
        for hh in range(HEADS):
            sl = slice(hh * HEAD_DIM, (hh + 1) * HEAD_DIM)
            o_v, g, dr = o_ref[:, sl], g_ref[:, sl].astype(F32), dret[:, sl]
            sg = _sigmoid(g)
            r = lax.rsqrt(jnp.mean(o_v * o_v, axis=-1, keepdims=True) + NORM_EPS)
            rn = o_v * r
            d_rn = dr * (g * sg)
            dg_ref[:, sl] = (dr * rn * (sg * (1.0 + g * (1.0 - sg)))).astype(BF16)
            d_o = r * (d_rn - rn * jnp.mean(d_rn * rn, axis=-1, keepdims=True))
            dob = d_o.astype(BF16)

            q, k, v = _rotate(q_ref, hh, cs, sn), _rotate(k_ref, hh, cs, sn, K_SCALE), v_ref[:, sl].astype(BF16)
            qb, kb = q.astype(BF16), k.astype(BF16)
            qdv, kdv = qd_ref[hh], kd_ref[hh]
            s_t = (_dot(kb, qb, "nt") * dt_ref[hh]).astype(BF16)
            p_t = (_dot(v, dob, "nt") * dt_ref[hh]).astype(BF16)
            p = (_dot(dob, v, "nt") * d_ref[hh]).astype(BF16)
            stb = st_ref[hh]
            gb = gstate[hh].astype(BF16)
            _unrotate_into(dq_ref, hh, _dot(p, kb) + _dot(dob, stb, "nt") * qdv, cs, sn)
            _unrotate_into(dk_ref, hh, _dot(p_t, qb) + _dot(v, gb, "nt") * kdv, cs, sn, K_SCALE)
            dv_ref[:, sl] = (_dot(s_t, dob) + _dot((k * kdv).astype(BF16), gb)).astype(BF16)
            gstate[hh] = gstate[hh] * cd_ref[hh] + _dot((q * qdv).astype(BF16), dob, "tn")

    return pl.pallas_call(
        body, name=name, grid=(nb,),
        in_specs=[tok, pl.BlockSpec((D_MODEL, D_MODEL), lambda n: (0, 0), pipeline_mode=pl.Buffered(1)), tok] + blk
                 + [rope, rope, st, tab, tab, col, col, rowv],
        out_specs=[tok, tok, tok, tok],
        out_shape=[jax.ShapeDtypeStruct((t, D_MODEL), BF16)] * 4,
        scratch_shapes=[pltpu.VMEM((HEADS, HEAD_DIM, HEAD_DIM), F32)],
        compiler_params=_cparams(("arbitrary",)),
    )(dru, w_ru, o, proj, proj, proj, proj, cos, sin, states, dmat, dmat_t, qd, kd, cd)


POOL_TILE = 256


def _pool_tables():
    b = POOL_TILE
    tt = np.arange(b)[:, None]
    jj = np.arange(b)[None, :]
    cur, prev = [], []
    for w in POOL_WINDOWS:
        cur.append(((tt - jj >= 0) & (tt - jj <= w - 1)).astype(np.float32))
        prev.append((tt - (jj - b) <= w - 1).astype(np.float32))
    cur, prev = np.stack(cur), np.stack(prev)
    as16 = lambda v: jnp.asarray(v, dtype=BF16)
    return as16(cur), as16(prev), as16(np.swapaxes(cur, 1, 2)), as16(np.swapaxes(prev, 1, 2))


def _split2(x):
    hi = x.astype(BF16)
    return hi, (x - hi.astype(F32)).astype(BF16)


POOL_BLOCK = 4


def _pool_count(n, window):
    tpos = n * POOL_TILE + lax.broadcasted_iota(jnp.int32, (POOL_TILE, 1), 0)
    return jnp.minimum(tpos + 1, window).astype(F32)


def _pool_fwd(name, proj, pool_w, scale, tables):
    t = proj.shape[0]
    nb = t // POOL_TILE
    mc, mp, _, _ = tables
    tab = _full_spec((GROUPS, POOL_TILE, POOL_TILE))
    row = _row_spec(POOL_TILE, D_MODEL)

    def body(pc_ref, pp_ref, mc_ref, mp_ref, w_ref, sc_ref, pm_ref, mix_ref, po_ref):
        n = pl.program_id(0)
        for g, window in enumerate(POOL_WINDOWS):
            sl = slice(g * GROUP_DIM, (g + 1) * GROUP_DIM)
            p = pc_ref[:, sl]
            win = _dot(mc_ref[g], p) + jnp.where(n > 0, _dot(mp_ref[g], pp_ref[:, sl]), 0.0)
            pm = (win / _pool_count(n, window) - p.astype(F32)).astype(BF16)
            pm_ref[:, sl] = pm
            mixed = _dot(pm, w_ref[g])
            mix_ref[:, sl] = mixed
            po_ref[:, sl] = (mixed * sc_ref[:, sl]).astype(BF16)

    return pl.pallas_call(
        body, name=name, grid=(nb,),
        in_specs=[_row_spec(POOL_TILE, D_MODEL, POOL_BLOCK),
                  pl.BlockSpec((POOL_TILE, D_MODEL), lambda n: (jnp.maximum(n - 1, 0), POOL_BLOCK)),
                  tab, tab, _full_spec((GROUPS, GROUP_DIM, GROUP_DIM)), _full_spec((1, D_MODEL))],
        out_specs=[row] * 3,
        out_shape=[jax.ShapeDtypeStruct((t, D_MODEL), BF16), jax.ShapeDtypeStruct((t, D_MODEL), F32),
                   jax.ShapeDtypeStruct((t, D_MODEL), BF16)],
        compiler_params=_cparams(("parallel",)),
    )(proj, proj, mc, mp, pool_w, scale)


def _pool_bwd(name, dpu, w_pu, pm, mixed, pool_w, scale, tables):
    t = dpu.shape[0]
    nb = t // POOL_TILE
    _, _, mct, mpt = tables
    cur = pl.BlockSpec((POOL_TILE, D_MODEL), lambda n: (nb - 1 - n, 0))
    tab = _full_spec((GROUPS, POOL_TILE, POOL_TILE))
    wspec = _full_spec((GROUPS, GROUP_DIM, GROUP_DIM))
    sspec = _full_spec((1, D_MODEL))

    def body(dpu_ref, wpu_ref, pm_ref, mix_ref, mct_ref, mpt_ref, w_ref, sc_ref, dp_ref, dw_ref, ds_ref, later):
        n = pl.program_id(0)

        @pl.when(n == 0)
        def _():
            dw_ref[...] = jnp.zeros(dw_ref.shape, F32)
            ds_ref[...] = jnp.zeros(ds_ref.shape, F32)
            later[...] = jnp.zeros(later.shape, F32)

        dpo = _dot(dpu_ref[...], wpu_ref[...], "nt")
        for g, window in enumerate(POOL_WINDOWS):
            sl = slice(g * GROUP_DIM, (g + 1) * GROUP_DIM)
            dc, sc = dpo[:, sl], sc_ref[:, sl]
            dmix = (dc * sc).astype(BF16)
            dpm = _dot(dmix, w_ref[g], "nt")
            e = dpm / _pool_count(nb - 1 - n, window)
            e_hi, e_lo = _split2(e)
            f_hi, f_lo = _split2(later[g])
            mctv, mptv = mct_ref[g], mpt_ref[g]
            back = _dot(mctv, e_hi) + _dot(mctv, e_lo)
            after = _dot(mptv, f_hi) + _dot(mptv, f_lo)
            dp_ref[:, sl] = (back + after - dpm).astype(BF16)
            later[g] = e
            dw_ref[g] += _dot(pm_ref[:, sl], dmix, "tn")
            ds_ref[:, sl] += jnp.sum(dc * mix_ref[:, sl], axis=0, keepdims=True)

    return pl.pallas_call(
        body, name=name, grid=(nb,),
        in_specs=[cur, pl.BlockSpec((D_MODEL, D_MODEL), lambda n: (0, 0), pipeline_mode=pl.Buffered(1)), cur, cur, tab, tab,
                  wspec, sspec],
        out_specs=[cur, wspec, sspec],
        out_shape=[jax.ShapeDtypeStruct((t, D_MODEL), BF16), jax.ShapeDtypeStruct((GROUPS, GROUP_DIM, GROUP_DIM), F32),
                   jax.ShapeDtypeStruct((1, D_MODEL), F32)],
        scratch_shapes=[pltpu.VMEM((GROUPS, POOL_TILE, GROUP_DIM), F32)],
        compiler_params=_cparams(("arbitrary",)),
    )(dpu, w_pu, pm, mixed, mct, mpt, pool_w, scale)


GATE0_BLOCK, GATE1_BLOCK = 5, 6


def _merge_fwd(name, ret, po, w_ru, w_pu, w_out, proj, bias, h, next_g, tm=512):
    t = ret.shape[0]

    def body(r_ref, p_ref, wr_ref, wp_ref, wo_ref, g0_ref, g1_ref, b_ref, h_ref, ng_ref, m_ref, ru_ref, pu_ref, ho_ref, n_ref):
        ru = _dot(r_ref[...], wr_ref[...])
        pu = _dot(p_ref[...], wp_ref[...])
        ru_ref[...] = ru
        pu_ref[...] = pu
        merged = (_sigmoid(g0_ref[...].astype(F32) + b_ref[0:1, :]) * ru
                  + _sigmoid(g1_ref[...].astype(F32) + b_ref[1:2, :]) * pu).astype(BF16)
        m_ref[...] = merged
        h_new = h_ref[...] + _dot(merged, wo_ref[...])
        ho_ref[...] = h_new
        n_ref[...] = _normed(h_new, ng_ref[...]).astype(BF16)

    row = _row_spec(tm, D_MODEL)
    wspec = pl.BlockSpec((D_MODEL, D_MODEL), lambda i: (0, 0), pipeline_mode=pl.Buffered(1))
    return pl.pallas_call(
        body, name=name, grid=(t // tm,),
        in_specs=[row, row, wspec, wspec, wspec, _row_spec(tm, D_MODEL, GATE0_BLOCK), _row_spec(tm, D_MODEL, GATE1_BLOCK),
                  _full_spec((2, D_MODEL)), row, _full_spec((1, D_MODEL))],
        out_specs=[row] * 5,
        out_shape=[jax.ShapeDtypeStruct((t, D_MODEL), BF16), jax.ShapeDtypeStruct((t, D_MODEL), F32),
                   jax.ShapeDtypeStruct((t, D_MODEL), F32), jax.ShapeDtypeStruct((t, D_MODEL), F32),
                   jax.ShapeDtypeStruct((t, D_MODEL), BF16)],
        compiler_params=_cparams(("parallel",)),
    )(ret, po, w_ru, w_pu, w_out, proj, proj, bias, h, next_g)


def _merge_bwd(name, dh_b, w_out, ru, pu, proj, bias, tm=512):
    t = dh_b.shape[0]

    def body(dh_ref, wo_ref, ru_ref, pu_ref, g0_ref, g1_ref, b_ref, dru_ref, dpu_ref, dg0_ref, dg1_ref, db_ref):
        i = pl.program_id(0)
        d = _dot(dh_ref[...], wo_ref[...], "nt")
        s0 = _sigmoid(g0_ref[...].astype(F32) + b_ref[0:1, :])
        s1 = _sigmoid(g1_ref[...].astype(F32) + b_ref[1:2, :])
        dru_ref[...] = (d * s0).astype(BF16)
        dpu_ref[...] = (d * s1).astype(BF16)
        dg0 = d * ru_ref[...] * (s0 * (1.0 - s0))
        dg1 = d * pu_ref[...] * (s1 * (1.0 - s1))
        dg0_ref[...] = dg0.astype(BF16)
        dg1_ref[...] = dg1.astype(BF16)
        part0 = jnp.sum(dg0, axis=0, keepdims=True)
        part1 = jnp.sum(dg1, axis=0, keepdims=True)

        @pl.when(i == 0)
        def _():
            db_ref[0:1, :] = part0
            db_ref[1:2, :] = part1

        @pl.when(i > 0)
        def _():
            db_ref[0:1, :] += part0
            db_ref[1:2, :] += part1

    row = _row_spec(tm, D_MODEL)
    return pl.pallas_call(
        body, name=name, grid=(t // tm,),
        in_specs=[row, pl.BlockSpec((D_MODEL, D_MODEL), lambda i: (0, 0), pipeline_mode=pl.Buffered(1)), row, row,
                  _row_spec(tm, D_MODEL, GATE0_BLOCK), _row_spec(tm, D_MODEL, GATE1_BLOCK), _full_spec((2, D_MODEL))],
        out_specs=[row, row, row, row, _full_spec((2, D_MODEL))],
        out_shape=[jax.ShapeDtypeStruct((t, D_MODEL), BF16)] * 4 + [jax.ShapeDtypeStruct((2, D_MODEL), F32)],
        compiler_params=_cparams(("arbitrary",)),
    )(dh_b, w_out, ru, pu, proj, proj, bias)


def _half_scale(acc):
    return (FFN_RES_WEIGHT * acc,)


def _residual_half(acc, res):
    return (res + FFN_RES_WEIGHT * acc,)


def _normed(h, g):
    return h * lax.rsqrt(jnp.mean(h * h, axis=-1, keepdims=True) + NORM_EPS) * g


def _residual_half_norm(acc, res, g):
    h = res + FFN_RES_WEIGHT * acc
    return h, _normed(h, g)


FF_TILE = D_FF // 2
DW_TILE = 256
SAVED_FF_DTYPE = BF16
FF_CHUNKS = ((0, 512), (512, 1024), (1024, FF_TILE))


def _ffn_in(name, nrm, w_in, tm=512):
    t = nrm.shape[0]
    nj = D_FF // FF_TILE

    def body(n_ref, wg_ref, wu_ref, a_ref, mid_ref):
        nv = n_ref[...]
        for c0, c1 in FF_CHUNKS:
            gate = _dot(nv, wg_ref[:, c0:c1])
            up = _dot(nv, wu_ref[:, c0:c1])
            a_ref[0, :, c0:c1] = gate.astype(a_ref.dtype)
            a_ref[1, :, c0:c1] = up.astype(a_ref.dtype)
            mid_ref[:, c0:c1] = (gate * _sigmoid(gate) * up).astype(BF16)

    return pl.pallas_call(
        body, name=name, grid=(nj, t // tm),
        in_specs=[pl.BlockSpec((tm, D_MODEL), lambda j, i: (i, 0)),
                  pl.BlockSpec((D_MODEL, FF_TILE), lambda j, i: (0, j)),
                  pl.BlockSpec((D_MODEL, FF_TILE), lambda j, i: (0, j + nj))],
        out_specs=[pl.BlockSpec((2, tm, FF_TILE), lambda j, i: (0, i, j)), pl.BlockSpec((tm, FF_TILE), lambda j, i: (i, j))],
        out_shape=[jax.ShapeDtypeStruct((2, t, D_FF), SAVED_FF_DTYPE), jax.ShapeDtypeStruct((t, D_FF), BF16)],
        compiler_params=_cparams(("parallel", "parallel")),
    )(nrm, w_in, w_in)


def _ffn_dact(name, dout_b, w_out, a, tm=512):
    t = dout_b.shape[0]

    def body(d_ref, w_ref, a_ref, da_ref):
        dv = d_ref[...]
        for c0, c1 in FF_CHUNKS:
            dm = FFN_RES_WEIGHT * _dot(dv, w_ref[c0:c1, :], "nt")
            gate, up = a_ref[0, :, c0:c1].astype(F32), a_ref[1, :, c0:c1].astype(F32)
            s = _sigmoid(gate)
            da_ref[0, :, c0:c1] = (dm * up * (s * (1.0 + gate * (1.0 - s)))).astype(BF16)
            da_ref[1, :, c0:c1] = (dm * (gate * s)).astype(BF16)

    blk = pl.BlockSpec((2, tm, FF_TILE), lambda j, i: (0, i, j))
    return pl.pallas_call(
        body, name=name, grid=(D_FF // FF_TILE, t // tm),
        in_specs=[pl.BlockSpec((tm, D_MODEL), lambda j, i: (i, 0)), pl.BlockSpec((FF_TILE, D_MODEL), lambda j, i: (j, 0)), blk],
        out_specs=blk,
        out_shape=jax.ShapeDtypeStruct((2, t, D_FF), BF16),
        compiler_params=_cparams(("parallel", "parallel")),
    )(dout_b, w_out, a)


def _ffn_fwd(tag, h, nrm, get_w_in, get_w_out, next_g=None):
    t = h.shape[0]
    w_in = get_w_in(nrm)
    a, mid = _ffn_in(f"{tag}_in", nrm, w_in, tm=min(512, t))
    w_out = get_w_out(mid)
    if next_g is None:
        out = _matmul(f"{tag}_out", mid, w_out, "nn", t, D_MODEL, D_FF, 512, D_MODEL, D_FF, [F32],
                      extras=(h,), epilogue=_residual_half)
        nxt = None
    else:
        out, nxt = _matmul(f"{tag}_out", mid, w_out, "nn", t, D_MODEL, D_FF, 512, D_MODEL, D_FF, [F32, BF16],
                           extras=(h,), consts=(next_g,), epilogue=_residual_half_norm)
    return out, nxt, (nrm, a, mid, w_in, w_out)


def _ffn_bwd(tag, h, g, saved, dout, dout_b, on_grads, flush):
    t = h.shape[0]
    nrm, a, mid, w_in, w_out = saved
    d_w_out = _matmul(f"{tag}_dwout", mid, dout_b, "tn", D_FF, D_MODEL, t, DW_TILE, D_MODEL, t, [BF16], epilogue=_half_scale,
                      resident="b")
    da = _ffn_dact(f"{tag}_dact", dout_b, w_out, a, tm=min(512, t))
    nj = D_FF // DW_TILE
    d_w_in = _dw_resident(f"{tag}_dwin", nrm, [da], [pl.BlockSpec((None, t, DW_TILE), lambda s: (s // nj, 0, s % nj))],
                          2 * nj, None, DW_TILE)
    tie = on_grads({f"{tag}_w_in": d_w_in, f"{tag}_w_out": d_w_out})
    tm = min(256, t)
    dh, dh_b, dg = _proj_norm_bwd(f"{tag}_dn", [da], [pl.BlockSpec((2, tm, D_FF), lambda i: (0, i, 0))],
                                  ((0, 0, 0, D_FF), (0, 1, D_FF, 2 * D_FF)), w_in, h, g if tie is None else g + tie, dout, tm)
    return dh, dh_b, dg, flush(dh)


def _dw_resident(name, u, pieces, piece_specs, n_tiles, which_piece, tn):
    t = u.shape[0]
    npc = len(pieces)

    def body(*refs):
        u_ref, p_refs, o_ref, ut_ref = refs[0], refs[1:1 + npc], refs[1 + npc], refs[2 + npc]
        s = pl.program_id(0)

        @pl.when(s == 0)
        def _():
            ut_ref[...] = u_ref[...].T

        if npc == 1:
            o_ref[...] = _dot(ut_ref[...], p_refs[0][...]).astype(BF16)
        for which in range(npc if npc > 1 else 0):
            @pl.when(which_piece(s) == which)
            def _(which=which):
                o_ref[...] = _dot(ut_ref[...], p_refs[which][...]).astype(BF16)

    return pl.pallas_call(
        body, name=name, grid=(n_tiles,),
        in_specs=[pl.BlockSpec((t, D_MODEL), lambda s: (0, 0), pipeline_mode=pl.Buffered(1))] + list(piece_specs),
        out_specs=pl.BlockSpec((D_MODEL, tn), lambda s: (0, s)),
        out_shape=jax.ShapeDtypeStruct((D_MODEL, n_tiles * tn), BF16),
        scratch_shapes=[pltpu.VMEM((D_MODEL, t), BF16)],
        compiler_params=_cparams(("arbitrary",)),
    )(u, *pieces)


def _mix_dwin(name, u, pieces, tn=256):
    t = u.shape[0]
    nj = D_MODEL // tn
    specs = [pl.BlockSpec((t, tn), lambda s, k=k: (0, jnp.clip(s - k * nj, 0, nj - 1))) for k in range(len(pieces))]
    return _dw_resident(name, u, pieces, specs, len(pieces) * nj, lambda s: s // nj, tn)


def _local_step(x, target, vec, get_w, on_grads, flush):
    t = x.shape[0]
    cos, sin = _rope_tables(t)
    rtab = _retention_tables()
    ptab = _pool_tables()
    w = {}

    def getter(group, name):
        def get(after):
            if name not in w:
                w.update(get_w(group, after))
            return w[name]
        return get

    nrm1 = _rmsnorm_fwd("ffn1_norm", x, vec["norm_ffn1"])
    h1, u, s1 = _ffn_fwd("ffn1", x, nrm1, getter(0, "ffn1_w_in"), getter(1, "ffn1_w_out"), vec["norm_mix"])
    w.update(get_w(2, u))
    proj = _matmul("mix_in", u, w["w_in"], "nn", t, IN_WIDTH, D_MODEL, 1024, 1024, D_MODEL, [BF16], n_outer=True)
    o, ret, states = _retention_fwd("retention", proj, cos, sin, rtab)
    pm, mixed, po = _pool_fwd("pool", proj, w["pool_w"], vec["pool_scale"], ptab)
    merged, ru, pu, h2, nrm2 = _merge_fwd("merge", ret, po, w["w_ret_up"], w["w_pool_up"], w["w_out"], proj, vec["gate_bias"],
                                          h1, vec["norm_ffn2"], tm=min(512, t))
    h3, _, s2 = _ffn_fwd("ffn2", h2, nrm2, getter(3, "ffn2_w_in"), getter(3, "ffn2_w_out"))
    dh3, dh3_b, dg_final, loss = _loss_and_grad("loss", h3, vec["norm_final"], target)

    def tied(v, tie):
        return v if tie is None else v + tie

    dh2, dh2_b, dg_ffn2, tie = _ffn_bwd("ffn2", h2, vec["norm_ffn2"], s2, dh3, dh3_b, on_grads, flush)
    def square_dw(name, act, grad):
        return _matmul(name, act, grad, "tn", D_MODEL, D_MODEL, t, D_MODEL, D_MODEL, 1024, [BF16])

    d_w_out = square_dw("mix_dwout", merged, dh2_b)
    dru, dpu, dg0, dg1, d_bias = _merge_bwd("merge_bwd", dh2_b, w["w_out"], ru, pu, proj, tied(vec["gate_bias"], tie))
    d_w_ru = square_dw("mix_dwru", ret, dru)
    d_w_pu = square_dw("mix_dwpu", po, dpu)
    dp, d_pool_w, d_scale = _pool_bwd("pool_bwd", dpu, w["w_pool_up"], pm, mixed, w["pool_w"], vec["pool_scale"], ptab)
    dq, dk, dv, dgr = _retention_bwd("retention_bwd", dru, w["w_ret_up"], o, proj, cos, sin, states, rtab)
    dproj = [dq, dk, dv, dgr, dp, dg0, dg1]
    d_w_in = _mix_dwin("mix_dwin", u, dproj)
    tie = on_grads(dict(w_in=d_w_in, pool_w=d_pool_w.astype(BF16), w_ret_up=d_w_ru, w_pool_up=d_w_pu, w_out=d_w_out))
    tm = min(256, t)
    dh1, dh1_b, dg_mix = _proj_norm_bwd("mix_du", dproj, [_row_spec(tm, D_MODEL)] * len(dproj),
                                        [(k, None, k * D_MODEL, (k + 1) * D_MODEL) for k in range(len(dproj))],
                                        w["w_in"], h1, tied(vec["norm_mix"], tie), dh2, tm)
    tie = flush(dh1)
    dx, _, dg_ffn1, _ = _ffn_bwd("ffn1", x, tied(vec["norm_ffn1"], tie), s1, dh1, dh1_b, on_grads, flush)

    small = dict(norm_ffn1=dg_ffn1, norm_mix=dg_mix, gate_bias=d_bias, pool_scale=d_scale, norm_ffn2=dg_ffn2,
                 norm_final=dg_final)
    return loss[0, 0], dx, small


BIG = ("ffn1_w_in", "ffn1_w_out", "w_in", "pool_w", "w_ret_up", "w_pool_up", "w_out", "ffn2_w_in", "ffn2_w_out")
KIND = dict(ffn1_w_in="col", ffn1_w_out="row", w_in="col", pool_w="pool", w_ret_up="row", w_pool_up="row", w_out="row",
            ffn2_w_in="col", ffn2_w_out="row")
ANY = pl.BlockSpec(memory_space=pl.ANY)


def _place():
    x, y, c = lax.axis_index("x"), lax.axis_index("y"), lax.axis_index("c")
    chips = [(1 - x, y), (x, 1 - y), (1 - x, 1 - y)]
    return x, y, c, chips


def _full_view_shape(kind, local_shape):
    if kind == "col":
        return (2, local_shape[0] // 2, N_CHIPS * local_shape[1])
    if kind == "row":
        return (N_CHIPS, 2, local_shape[0] // 2, local_shape[1])
    return (GROUPS, N_CHIPS, 2, local_shape[1] // 2, local_shape[2])


def _local_view(kind, arr):
    if kind == "pool":
        return arr.reshape(GROUPS, 2, arr.shape[1] // 2, arr.shape[2])
    return arr.reshape(2, arr.shape[0] // 2, arr.shape[1])


def _blk(kind, ref, s, c):
    if kind == "col":
        cs = ref.shape[2] // N_CHIPS
        return ref.at[c, :, pl.ds(pl.multiple_of(s * cs, 128), cs)]
    if kind == "row":
        return ref.at[s, c]
    return ref.at[:, s, c]


def _half(kind, ref, c):
    return ref.at[:, c] if kind == "pool" else ref.at[c]


def _shard(kind, ref, s):
    if kind == "col":
        cs = ref.shape[2] // N_CHIPS
        return ref.at[:, :, pl.ds(pl.multiple_of(s * cs, 128), cs)]
    if kind == "row":
        return ref.at[s]
    return ref.at[:, s]


HBM = pl.BlockSpec(memory_space=pltpu.HBM)
SEM = pl.BlockSpec(memory_space=pltpu.SEMAPHORE)
EFFECT = pltpu.SideEffectType.DATAFLOW_SIDE_EFFECTING
WEIGHT_GROUPS = (("ffn1_w_in",), ("ffn1_w_out",), ("w_in", "pool_w", "w_ret_up", "w_pool_up", "w_out"), ("ffn2_w_in", "ffn2_w_out"))
GRAD_GROUPS = (("ffn2_w_in", "ffn2_w_out"), ("w_in", "pool_w", "w_ret_up", "w_pool_up", "w_out"), ("ffn1_w_in", "ffn1_w_out"))


def _hbm(a):
    return pltpu.with_memory_space_constraint(a, pltpu.HBM)


def _natural(kind, o):
    if kind == "col":
        return o.reshape(o.shape[0] * o.shape[1], o.shape[2])
    if kind == "row":
        return o.reshape(-1, o.shape[3])
    return o.reshape(GROUPS, -1, o.shape[4])


def _ici_copy(kind, loc, full, j, chips, s, c, send_sem, recv_sem):
    px, py = chips[j]
    return (pltpu.make_async_remote_copy(src_ref=_half(kind, loc, c), dst_ref=_blk(kind, full, s, c), send_sem=send_sem,
                                         recv_sem=recv_sem, device_id=(px, py, c), device_id_type=MESH),
            pltpu.make_async_remote_copy(src_ref=_half(kind, loc, c), dst_ref=_blk(kind, full, 2 * px + py, c), send_sem=send_sem,
                                         recv_sem=recv_sem, device_id=(px, py, c), device_id_type=MESH))


def _gather_start(tag, group_ids, shards):
    grps = [WEIGHT_GROUPS[g] for g in group_ids]
    names = [nm for grp in grps for nm in grp]
    kinds = [KIND[nm] for nm in names]
    n, ng = len(names), len(grps)
    locs = [_hbm(_local_view(KIND[nm], shards[nm])) for nm in names]
    lands = [_hbm(lax.empty(_full_view_shape(KIND[nm], shards[nm].shape), BF16)) for nm in names]
    first = np.cumsum([0] + [len(grp) for grp in grps])

    def body(*refs):
        loc, full = refs[:n], refs[n:2 * n]
        send_sems, recv_sems = refs[2 * n:2 * n + ng], refs[2 * n + ng:2 * n + 2 * ng]
        token = refs[-1]
        x, y, c, chips = _place()
        s = 2 * x + y
        for g in range(ng):
            for a in range(first[g], first[g + 1]):
                for j in range(3):
                    k = 3 * (a - first[g]) + j
                    _ici_copy(kinds[a], loc[a], full[a], j, chips, s, c, send_sems[g].at[k], recv_sems[g].at[k])[0].start()
        token[...] = jnp.zeros(token.shape, F32)

    sem_shapes = [pltpu.SemaphoreType.DMA((3 * len(grp),)) for grp in grps]
    outs = pl.pallas_call(
        body, name=f"gather_start_{tag}",
        in_specs=[HBM] * (2 * n),
        out_specs=[SEM] * (2 * ng) + [HBM] * (2 * n) + [pl.BlockSpec(memory_space=pltpu.VMEM)],
        out_shape=sem_shapes + sem_shapes + [pltpu.HBM(a.shape, a.dtype) for a in locs + lands] + [jax.ShapeDtypeStruct((8, 128), F32)],
        input_output_aliases={i: 2 * ng + i for i in range(2 * n)},
        compiler_params=pltpu.CompilerParams(has_side_effects=EFFECT),
    )(*locs, *lands)
    send_sems, recv_sems = outs[:ng], outs[ng:2 * ng]
    locs_t, lands_t = outs[2 * ng:2 * ng + n], outs[2 * ng + n:2 * ng + 2 * n]
    groups = {}
    for k, g in enumerate(group_ids):
        sl = slice(first[k], first[k + 1])
        groups[g] = (send_sems[k], recv_sems[k], list(locs_t[sl]), list(lands_t[sl]))
    return groups, outs[-1]


def _gather_finish(g, group, after):
    names = WEIGHT_GROUPS[g]
    kinds = [KIND[nm] for nm in names]
    m = len(names)
    send_sem, recv_sem, locs, lands = group

    def wait_body(*refs):
        loc, full = refs[:m], refs[m:2 * m]
        send_sems, recv_sems = refs[2 * m], refs[2 * m + 1]
        x, y, c, chips = _place()
        s = 2 * x + y
        for a in range(m):
            for j in range(3):
                k = 3 * a + j
                sent, landed = _ici_copy(kinds[a], loc[a], full[a], j, chips, s, c, send_sems.at[k], recv_sems.at[k])
                sent.wait_send()
                landed.wait_recv()

    outs = pl.pallas_call(
        wait_body, name=f"gather_wait_{g}",
        in_specs=[HBM] * (2 * m) + [SEM, SEM] + [ANY] * len(after), out_specs=[HBM] * (2 * m),
        out_shape=[pltpu.HBM(a.shape, a.dtype) for a in locs + lands],
        input_output_aliases={i: i for i in range(2 * m)},
        compiler_params=pltpu.CompilerParams(has_side_effects=EFFECT),
    )(*locs, *lands, send_sem, recv_sem, *after)
    locs, lands = outs[:m], outs[m:]

    def forward_body(*refs):
        loc, full = refs[:m], refs[2 * m:3 * m]
        send_sems, recv_sems = refs[3 * m:]
        x, y, c, chips = _place()
        s = 2 * x + y
        sib = (x, y, 1 - c)

        def remote(a, k, src, dst):
            return pltpu.make_async_remote_copy(src_ref=src, dst_ref=dst, send_sem=send_sems.at[4 * a + k],
                                                recv_sem=recv_sems.at[4 * a + k], device_id=sib, device_id_type=MESH)

        sends = []
        for a in range(m):
            for j, (px, py) in enumerate(chips):
                theirs = _blk(kinds[a], full[a], 2 * px + py, c)
                sends.append(remote(a, j, theirs, theirs))
            sends.append(remote(a, 3, loc[a], _shard(kinds[a], full[a], s)))
        for cp in sends:
            cp.start()
        for a in range(m):
            for j, (px, py) in enumerate(chips):
                from_sib = _blk(kinds[a], full[a], 2 * px + py, 1 - c)
                remote(a, j, from_sib, from_sib).wait_recv()
            own = _shard(kinds[a], full[a], s)
            remote(a, 3, own, own).wait_recv()
        for cp in sends:
            cp.wait_send()

    outs = pl.pallas_call(
        forward_body, name=f"gather_forward_{g}",
        in_specs=[ANY] * (2 * m), out_specs=[ANY] * m,
        out_shape=[jax.ShapeDtypeStruct(a.shape, a.dtype) for a in lands],
        input_output_aliases={m + i: i for i in range(m)},
        scratch_shapes=[pltpu.SemaphoreType.DMA((4 * m,)), pltpu.SemaphoreType.DMA((4 * m,))],
    )(*locs, *lands)
    return {nm: _natural(k, o) for nm, k, o in zip(names, kinds, outs)}


def _grad_view(kind, g):
    if kind == "col":
        return g.reshape(2, g.shape[0] // 2, g.shape[1])
    if kind == "row":
        return g.reshape(N_CHIPS, 2, g.shape[0] // (2 * N_CHIPS), g.shape[1])
    return g.reshape(GROUPS, N_CHIPS, 2, g.shape[1] // (2 * N_CHIPS), g.shape[2])


def _pair_copies(kinds, g, got, send_sems, recv_sems):
    x, y, c, _ = _place()

    def other_half(kind, ref):
        if kind == "col":
            return ref.at[1 - c]
        if kind == "row":
            return ref.at[:, 1 - c]
        return ref.at[:, :, 1 - c]

    return [pltpu.make_async_remote_copy(src_ref=other_half(kinds[a], g[a]), dst_ref=got[a], send_sem=send_sems.at[a],
                                         recv_sem=recv_sems.at[a], device_id=(x, y, 1 - c), device_id_type=MESH)
            for a in range(len(kinds))]


def _pair_exchange_start(tag, names, views):
    kinds = [KIND[nm] for nm in names]
    n = len(names)

    def got_shape(kind, v):
        if kind == "col":
            return v.shape[1:]
        if kind == "row":
            return (v.shape[0],) + v.shape[2:]
        return v.shape[:2] + v.shape[3:]

    srcs = [_hbm(views[nm]) for nm in names]
    lands = [_hbm(lax.empty(got_shape(k, views[nm]), BF16)) for nm, k in zip(names, kinds)]

    def body(*refs):
        g, got = refs[:n], refs[n:2 * n]
        for cp in _pair_copies(kinds, g, got, refs[2 * n], refs[2 * n + 1]):
            cp.start()
        refs[-1][...] = jnp.zeros(refs[-1].shape, F32)

    sem_shape = pltpu.SemaphoreType.DMA((n,))
    outs = pl.pallas_call(
        body, name=f"grad_pair_exchange_start_{tag}",
        in_specs=[HBM] * (2 * n),
        out_specs=[SEM, SEM] + [HBM] * (2 * n) + [pl.BlockSpec(memory_space=pltpu.VMEM)],
        out_shape=[sem_shape, sem_shape] + [pltpu.HBM(a.shape, a.dtype) for a in srcs + lands] + [jax.ShapeDtypeStruct((8, 128), F32)],
        input_output_aliases={i: 2 + i for i in range(2 * n)},
        compiler_params=pltpu.CompilerParams(has_side_effects=EFFECT),
    )(*srcs, *lands)
    return (outs[0], outs[1], list(outs[2:2 + n]), list(outs[2 + n:2 + 2 * n])), outs[-1]


def _pair_exchange_wait(tag, names, state, after):
    kinds = [KIND[nm] for nm in names]
    n = len(names)
    send_sem, recv_sem, srcs, lands = state

    def body(*refs):
        g, got = refs[:n], refs[n:2 * n]
        for cp in _pair_copies(kinds, g, got, refs[2 * n], refs[2 * n + 1]):
            cp.wait_send()
            cp.wait_recv()

    outs = pl.pallas_call(
        body, name=f"grad_pair_exchange_wait_{tag}",
        in_specs=[HBM] * (2 * n) + [SEM, SEM, ANY], out_specs=[HBM] * (2 * n),
        out_shape=[pltpu.HBM(a.shape, a.dtype) for a in srcs + lands],
        input_output_aliases={i: i for i in range(2 * n)},
        compiler_params=pltpu.CompilerParams(has_side_effects=EFFECT),
    )(*srcs, *lands, send_sem, recv_sem, after)
    return dict(zip(names, outs[:n])), dict(zip(names, outs[n:]))


def _pair_sum(name, kind, view, got, c_arr):
    if kind == "col":
        _, rows, cols = view.shape
        tr = 128
        grid = (rows // tr,)
        v_spec = pl.BlockSpec((None, tr, cols), lambda i, c: (c[0], i, 0))
        g_spec = pl.BlockSpec((tr, cols), lambda i, c: (i, 0))
    elif kind == "row":
        _, _, rows, cols = view.shape
        grid = (N_CHIPS,)
        v_spec = pl.BlockSpec((None, None, rows, cols), lambda i, c: (i, c[0], 0, 0))
        g_spec = pl.BlockSpec((None, rows, cols), lambda i, c: (i, 0, 0))
    else:
        _, _, _, rows, cols = view.shape
        grid = (GROUPS,)
        v_spec = pl.BlockSpec((None, N_CHIPS, None, rows, cols), lambda i, c: (i, 0, c[0], 0, 0))
        g_spec = pl.BlockSpec((None, N_CHIPS, rows, cols), lambda i, c: (i, 0, 0, 0))

    def body(c_ref, v_ref, g_ref, o_ref):
        o_ref[...] = (v_ref[...].astype(F32) + g_ref[...].astype(F32)).astype(BF16)

    return pl.pallas_call(
        body, name=name,
        grid_spec=pltpu.PrefetchScalarGridSpec(num_scalar_prefetch=1, grid=grid, in_specs=[v_spec, g_spec], out_specs=g_spec),
        out_shape=jax.ShapeDtypeStruct(got.shape, BF16),
        compiler_params=_cparams(("parallel",)),
    )(c_arr, view, got)


def _piece(kind, ref, s):
    if kind == "col":
        cs = ref.shape[1] // N_CHIPS
        return ref.at[:, pl.ds(pl.multiple_of(s * cs, 128), cs)]
    if kind == "row":
        return ref.at[s]
    return ref.at[:, s]


def _piece_shape(kind, shape):
    if kind == "col":
        return (shape[0], shape[1] // N_CHIPS)
    if kind == "row":
        return shape[1:]
    return (shape[0],) + shape[2:]


def _shard_copies(kinds, p, got, send_sems, recv_sems):
    x, y, c, chips = _place()
    return [pltpu.make_async_remote_copy(src_ref=_piece(kinds[a], p[a], 2 * px + py), dst_ref=got[a].at[j],
                                         send_sem=send_sems.at[3 * a + j], recv_sem=recv_sems.at[3 * a + j],
                                         device_id=(px, py, c), device_id_type=MESH)
            for a in range(len(kinds)) for j, (px, py) in enumerate(chips)]


def _shard_exchange_start(g, names, psums):
    kinds = [KIND[nm] for nm in names]
    n = len(names)
    srcs = [_hbm(psums[nm]) for nm in names]
    lands = [_hbm(lax.empty((3,) + _piece_shape(k, psums[nm].shape), BF16)) for nm, k in zip(names, kinds)]

    def body(*refs):
        p, got = refs[:n], refs[n:2 * n]
        send_sems, recv_sems = refs[2 * n], refs[2 * n + 1]
        token = refs[-1]
        for cp in _shard_copies(kinds, p, got, send_sems, recv_sems):
            cp.start()
        token[...] = jnp.zeros(token.shape, F32)

    sem_shape = pltpu.SemaphoreType.DMA((3 * n,))
    outs = pl.pallas_call(
        body, name=f"grad_shard_exchange_start_{g}",
        in_specs=[HBM] * (2 * n),
        out_specs=[SEM, SEM] + [HBM] * (2 * n) + [pl.BlockSpec(memory_space=pltpu.VMEM)],
        out_shape=[sem_shape, sem_shape] + [pltpu.HBM(a.shape, a.dtype) for a in srcs + lands] + [jax.ShapeDtypeStruct((8, 128), F32)],
        input_output_aliases={i: 2 + i for i in range(2 * n)},
        compiler_params=pltpu.CompilerParams(has_side_effects=EFFECT),
    )(*srcs, *lands)
    return (outs[0], outs[1], list(outs[2:2 + n]), list(outs[2 + n:2 + 2 * n])), outs[-1]


def _shard_exchange_wait(g, names, state, after):
    kinds = [KIND[nm] for nm in names]
    n = len(names)
    send_sem, recv_sem, srcs, lands = state

    def body(*refs):
        p, got = refs[:n], refs[n:2 * n]
        for cp in _shard_copies(kinds, p, got, refs[2 * n], refs[2 * n + 1]):
            cp.wait_send()
            cp.wait_recv()

    outs = pl.pallas_call(
        body, name=f"grad_shard_exchange_wait_{g}",
        in_specs=[HBM] * (2 * n) + [SEM, SEM, ANY], out_specs=[HBM] * (2 * n),
        out_shape=[pltpu.HBM(a.shape, a.dtype) for a in srcs + lands],
        input_output_aliases={i: i for i in range(2 * n)},
        compiler_params=pltpu.CompilerParams(has_side_effects=EFFECT),
    )(*srcs, *lands, send_sem, recv_sem, after)
    return dict(zip(names, outs[:n])), dict(zip(names, outs[n:]))


def _shard_sum(name, kind, psum, got, sc_arr):
    if kind == "col":
        rows, cols = psum.shape
        cs = cols // N_CHIPS
        tr = 128
        grid = (rows // tr,)
        p_spec = pl.BlockSpec((tr, cs), lambda i, sc: (i, sc[0]))
        g_spec = pl.BlockSpec((3, tr, cs), lambda i, sc: (0, i, 0))
        o_spec = pl.BlockSpec((None, tr, cs), lambda i, sc: (sc[1], i, 0))
        out_shape = (2, rows, cs)
    elif kind == "row":
        _, rows, cols = psum.shape
        grid = (1,)
        p_spec = pl.BlockSpec((None, rows, cols), lambda i, sc: (sc[0], 0, 0))
        g_spec = pl.BlockSpec((3, rows, cols), lambda i, sc: (0, 0, 0))
        o_spec = pl.BlockSpec((None, rows, cols), lambda i, sc: (sc[1], 0, 0))
        out_shape = (2, rows, cols)
    else:
        _, _, rows, cols = psum.shape
        grid = (1,)
        p_spec = pl.BlockSpec((GROUPS, None, rows, cols), lambda i, sc: (0, sc[0], 0, 0))
        g_spec = pl.BlockSpec((3, GROUPS, rows, cols), lambda i, sc: (0, 0, 0, 0))
        o_spec = pl.BlockSpec((GROUPS, None, rows, cols), lambda i, sc: (0, sc[1], 0, 0))
        out_shape = (GROUPS, 2, rows, cols)

    def body(sc_ref, p_ref, g_ref, o_ref):
        o_ref[...] = ((p_ref[...].astype(F32) + g_ref[0].astype(F32)) + g_ref[1].astype(F32)) + g_ref[2].astype(F32)

    return pl.pallas_call(
        body, name=name,
        grid_spec=pltpu.PrefetchScalarGridSpec(num_scalar_prefetch=1, grid=grid, in_specs=[p_spec, g_spec], out_specs=o_spec),
        out_shape=jax.ShapeDtypeStruct(out_shape, F32),
        compiler_params=_cparams(("parallel",)),
    )(sc_arr, psum, got)


def _half_exchange(tag, names, bufs):
    kinds = [KIND[nm] for nm in names]
    n = len(names)

    def body(*refs):
        out = refs[n:2 * n]
        send_sems, recv_sems = refs[2 * n:]
        x, y, c, _ = _place()
        sib = (x, y, 1 - c)
        cps = []
        for a in range(n):
            mine = _half(kinds[a], out[a], c)
            cp = pltpu.make_async_remote_copy(src_ref=mine, dst_ref=mine, send_sem=send_sems.at[a], recv_sem=recv_sems.at[a],
                                              device_id=sib, device_id_type=MESH)
            cp.start()
            cps.append(cp)
        for a, cp in enumerate(cps):
            cp.wait_send()
            theirs = _half(kinds[a], out[a], 1 - c)
            pltpu.make_async_remote_copy(src_ref=theirs, dst_ref=theirs, send_sem=send_sems.at[a], recv_sem=recv_sems.at[a],
                                         device_id=sib, device_id_type=MESH).wait_recv()

    outs = pl.pallas_call(
        body, name=f"grad_half_exchange_{tag}",
        in_specs=[ANY] * n, out_specs=[ANY] * n,
        out_shape=[jax.ShapeDtypeStruct(bufs[nm].shape, F32) for nm in names],
        input_output_aliases={a: a for a in range(n)},
        scratch_shapes=[pltpu.SemaphoreType.DMA((n,)), pltpu.SemaphoreType.DMA((n,))],
    )(*[bufs[nm] for nm in names])
    return dict(zip(names, outs))


N_DEV = 8
SMALL_ROWS = 8


def _all_reduce_small(name, v):
    def body(v_ref, o_ref, token, buf, send_sems, recv_sems):
        token[...] = jnp.zeros(token.shape, F32)
        x, y, c, _ = _place()
        me = 4 * x + 2 * y + c
        buf[me] = v_ref[...]
        cps = []
        for r in range(1, N_DEV):
            to = (x ^ (r >> 2), y ^ ((r >> 1) & 1), c ^ (r & 1))
            cp = pltpu.make_async_remote_copy(src_ref=v_ref, dst_ref=buf.at[me], send_sem=send_sems.at[r - 1],
                                              recv_sem=recv_sems.at[r - 1], device_id=to, device_id_type=MESH)
            cp.start()
            cps.append(cp)
        for r in range(1, N_DEV):
            pltpu.make_async_remote_copy(src_ref=v_ref, dst_ref=buf.at[me ^ r], send_sem=send_sems.at[r - 1],
                                         recv_sem=recv_sems.at[r - 1], device_id=(x, y, c), device_id_type=MESH).wait_recv()
        for cp in cps:
            cp.wait_send()
        acc = buf[0]
        for d in range(1, N_DEV):
            acc = acc + buf[d]
        o_ref[...] = acc

    vm = pl.BlockSpec(memory_space=pltpu.VMEM)
    return pl.pallas_call(
        body, name=name, in_specs=[vm], out_specs=[vm, vm],
        out_shape=[jax.ShapeDtypeStruct((SMALL_ROWS, D_MODEL), F32), jax.ShapeDtypeStruct((8, 128), F32)],
        scratch_shapes=[pltpu.VMEM((N_DEV, SMALL_ROWS, D_MODEL), F32), pltpu.SemaphoreType.DMA((N_DEV - 1,)),
                        pltpu.SemaphoreType.DMA((N_DEV - 1,))],
    )(v)


def _adamw(name, w, g, m, v):
    rows, cols = w.shape
    tr = next((c for c in (256, 176, 128, 64, 32, 8) if rows % c == 0), rows)
    spec = pl.BlockSpec((tr, cols), lambda i: (i, 0))

    def body(w_ref, g_ref, m_ref, v_ref, d_ref, mo_ref, vo_ref):
        gv = g_ref[...]
        m_new = ADAM_B1 * m_ref[...] + (1.0 - ADAM_B1) * gv
        v_new = ADAM_B2 * v_ref[...] + (1.0 - ADAM_B2) * jnp.square(gv)
        m_hat = m_new / (1.0 - ADAM_B1 ** ADAM_STEP)
        v_hat = v_new / (1.0 - ADAM_B2 ** ADAM_STEP)
        d_ref[...] = -ADAM_LR * (m_hat / (jnp.sqrt(v_hat) + ADAM_EPS) + ADAM_WD * w_ref[...])
        mo_ref[...] = m_new
        vo_ref[...] = v_new

    return pl.pallas_call(
        body, name=name, grid=(rows // tr,),
        in_specs=[spec] * 4, out_specs=[spec] * 3,
        out_shape=[jax.ShapeDtypeStruct((rows, cols), F32)] * 3,
        compiler_params=_cparams(("parallel",)),
    )(w, g, m, v)


WEIGHTS = ("norm_ffn1", "ffn1_w_in", "ffn1_w_out", "norm_mix", "w_in", "gate_bias", "pool_w", "pool_scale", "w_ret_up",
           "w_pool_up", "w_out", "norm_ffn2", "ffn2_w_in", "ffn2_w_out", "norm_final")
SMALL_ROW = dict(norm_ffn1=0, norm_mix=1, gate_bias=2, pool_scale=4, norm_ffn2=5, norm_final=6)


def _as2d(a):
    return a.reshape(-1, a.shape[-1])


def kernel(x, norm_ffn1, ffn1_w_in, ffn1_w_out, norm_mix, w_in, gate_bias, pool_w, pool_scale, w_ret_up, w_pool_up, w_out, norm_ffn2, ffn2_w_in, ffn2_w_out, norm_final, loss_target, m_norm_ffn1, m_ffn1_w_in, m_ffn1_w_out, m_norm_mix, m_w_in, m_gate_bias, m_pool_w, m_pool_scale, m_w_ret_up, m_w_pool_up, m_w_out, m_norm_ffn2, m_ffn2_w_in, m_ffn2_w_out, m_norm_final, v_norm_ffn1, v_ffn1_w_in, v_ffn1_w_out, v_norm_mix, v_w_in, v_gate_bias, v_pool_w, v_pool_scale, v_w_ret_up, v_w_pool_up, v_w_out, v_norm_ffn2, v_ffn2_w_in, v_ffn2_w_out, v_norm_final):
    wt = dict(norm_ffn1=norm_ffn1, ffn1_w_in=ffn1_w_in, ffn1_w_out=ffn1_w_out, norm_mix=norm_mix, w_in=w_in, gate_bias=gate_bias,
              pool_w=pool_w, pool_scale=pool_scale, w_ret_up=w_ret_up, w_pool_up=w_pool_up, w_out=w_out, norm_ffn2=norm_ffn2,
              ffn2_w_in=ffn2_w_in, ffn2_w_out=ffn2_w_out, norm_final=norm_final)
    mom = dict(norm_ffn1=m_norm_ffn1, ffn1_w_in=m_ffn1_w_in, ffn1_w_out=m_ffn1_w_out, norm_mix=m_norm_mix, w_in=m_w_in,
               gate_bias=m_gate_bias, pool_w=m_pool_w, pool_scale=m_pool_scale, w_ret_up=m_w_ret_up, w_pool_up=m_w_pool_up,
               w_out=m_w_out, norm_ffn2=m_norm_ffn2, ffn2_w_in=m_ffn2_w_in, ffn2_w_out=m_ffn2_w_out, norm_final=m_norm_final)
    var = dict(norm_ffn1=v_norm_ffn1, ffn1_w_in=v_ffn1_w_in, ffn1_w_out=v_ffn1_w_out, norm_mix=v_norm_mix, w_in=v_w_in,
               gate_bias=v_gate_bias, pool_w=v_pool_w, pool_scale=v_pool_scale, w_ret_up=v_w_ret_up, w_pool_up=v_w_pool_up,
               w_out=v_w_out, norm_ffn2=v_norm_ffn2, ffn2_w_in=v_ffn2_w_in, ffn2_w_out=v_ffn2_w_out, norm_final=v_norm_final)

    ax, ay, ac = lax.axis_index("x"), lax.axis_index("y"), lax.axis_index("c")
    chip = 2 * ax + ay
    c_arr = jnp.reshape(ac, (1,)).astype(jnp.int32)
    sc_arr = jnp.stack([chip, ac]).astype(jnp.int32)
    bias_cols = gate_bias.shape[-1]

    placed = lax.dynamic_update_slice(jnp.zeros((SMALL_ROWS, D_MODEL), F32), gate_bias[0], (0, chip * bias_cols))
    bias_sum, token = _all_reduce_small("gather_gate_bias", jnp.where(ac == 0, placed, 0.0))
    bias_full = bias_sum[:2]
    first = WEIGHT_GROUPS[0]
    gather_groups, token = _gather_start("first", [0], {nm: wt[nm][0].astype(BF16) + token[0, 0].astype(BF16) for nm in first})
    rest, rest_token = _gather_start("rest", [1, 2, 3],
                                     {nm: wt[nm][0].astype(BF16) + token[0, 0].astype(BF16) for nm in BIG if nm not in first})
    gather_groups.update(rest)
    vec = dict(norm_ffn1=norm_ffn1, norm_mix=norm_mix, norm_ffn2=norm_ffn2, pool_scale=pool_scale,
               norm_final=norm_final.reshape(1, D_MODEL), gate_bias=bias_full)

    def get_w(g, after):
        return _gather_finish(g, gather_groups[g], (after, rest_token) if g == 0 else (after,))

    pairs, pending = [], []

    def on_grads(gr):
        g = len(pairs)
        names = GRAD_GROUPS[g]
        assert set(names) == set(gr), (names, list(gr))
        state, token = _pair_exchange_start(g, names, {nm: _grad_view(KIND[nm], gr[nm]) for nm in names})
        pairs.append(state)
        return token[0:1, 0:1]

    def flush(after):
        g = len(pending)
        names = GRAD_GROUPS[g]
        views, from_sib = _pair_exchange_wait(g, names, pairs[g], after)
        psums = {nm: _pair_sum(f"pair_sum_{nm}", KIND[nm], views[nm], from_sib[nm], c_arr) for nm in names}
        state, token = _shard_exchange_start(g, names, psums)
        pending.append(state)
        tokens.append(token)
        return token[0:1, 0:1]

    tokens = []
    loss_local, dx, small = _local_step(x[0], loss_target[0], vec, get_w, on_grads, flush)

    packed = jnp.concatenate([small["norm_ffn1"], small["norm_mix"], small["gate_bias"], small["pool_scale"],
                              small["norm_ffn2"], small["norm_final"], jnp.broadcast_to(loss_local, (1, D_MODEL))], axis=0)
    small_sum, _ = _all_reduce_small("reduce_small_grads", packed)
    loss = small_sum[SMALL_ROWS - 1, 0]
    grads, delta, new_m, new_v = {}, {}, {}, {}

    def adamw(nm):
        shape = wt[nm].shape
        d, m2, v2 = _adamw(f"adamw_{nm}", _as2d(wt[nm]), _as2d(grads[nm]), _as2d(mom[nm]), _as2d(var[nm]))
        delta[nm], new_m[nm], new_v[nm] = d.reshape(shape), m2.reshape(shape), v2.reshape(shape)
        return d

    for nm in ("norm_ffn1", "norm_mix", "pool_scale", "norm_ffn2"):
        grads[nm] = small_sum[SMALL_ROW[nm]][None, :]
    grads["norm_final"] = small_sum[SMALL_ROW["norm_final"]]
    grads["gate_bias"] = lax.dynamic_slice(small_sum, (SMALL_ROW["gate_bias"], chip * bias_cols), (2, bias_cols))[None]
    after = tokens[-1]
    for g, names in enumerate(GRAD_GROUPS):
        psums, from_chips = _shard_exchange_wait(g, names, pending[g], after)
        bufs = {nm: _shard_sum(f"shard_sum_{nm}", KIND[nm], psums[nm], from_chips[nm], sc_arr) for nm in names}
        reduced = _half_exchange(g, names, bufs)
        for nm in names:
            grads[nm] = reduced[nm].reshape(wt[nm].shape)
            after = adamw(nm)
    for nm in WEIGHTS:
        if nm not in delta:
            adamw(nm)

    return (loss, dx[None], *[grads[nm] for nm in WEIGHTS], *[delta[nm] for nm in WEIGHTS],
            *[new_m[nm] for nm in WEIGHTS], *[new_v[nm] for nm in WEIGHTS])
```

```python
import functools

import numpy as np
import jax
import jax.numpy as jnp
from jax import lax
from jax.experimental import pallas as pl
from jax.experimental.pallas import tpu as pltpu

F32 = jnp.float32
BF16 = jnp.bfloat16
MESH = pl.DeviceIdType.MESH

D_MODEL = 1024
D_FF = 2816
HEADS = 4
HEAD_DIM = 256
GROUPS = 4
GROUP_DIM = 256
POOL_WINDOWS = (2, 4, 8, 16)
IN_WIDTH = 7 * D_MODEL
ROPE_BASE = 10000.0
NORM_EPS = 1e-6
FFN_RES_WEIGHT = 0.5
ADAM_LR, ADAM_B1, ADAM_B2, ADAM_EPS, ADAM_WD, ADAM_STEP = 0.001, 0.9, 0.999, 1e-08, 0.01, 10

N_CHIPS = 4
RET_BLOCK = 256
V7X_VMEM_LIMIT = 48 * 1024 * 1024


def _cparams(sem):
    return pltpu.CompilerParams(dimension_semantics=sem, vmem_limit_bytes=V7X_VMEM_LIMIT)


def _sigmoid(x):
    return jax.nn.sigmoid(x)


_DIMS = {"nn": (((1,), (0,)), ((), ())), "nt": (((1,), (1,)), ((), ())), "tn": (((0,), (0,)), ((), ()))}


def _matmul(name, a, b, mode, m, n, k, tm, tn, tk, out_dtypes, a_spec=None, b_spec=None, extras=(), consts=(), epilogue=None,
            resident=None, n_outer=False):
    tm, tn, tk = min(tm, m), min(tn, n), min(tk, k)
    gi, gj, gk = m // tm, n // tn, k // tk
    assert gi * tm == m and gj * tn == n and gk * tk == k, (name, m, n, k, tm, tn, tk)
    assert not (n_outer and (a_spec is not None or b_spec is not None)), name
    once = dict(pipeline_mode=pl.Buffered(1))

    def spec(shape, index, **kw):
        return pl.BlockSpec(shape, (lambda j, i, kk: index(i, j, kk)) if n_outer else index, **kw)

    if a_spec is None:
        kw = once if resident == "a" else {}
        a_spec = (spec((tk, tm), lambda i, j, kk: (kk, i), **kw) if mode == "tn"
                  else spec((tm, tk), lambda i, j, kk: (i, kk), **kw))
    if b_spec is None:
        kw = once if resident == "b" else {}
        b_spec = (spec((tn, tk), lambda i, j, kk: (j, kk), **kw) if mode == "nt"
                  else spec((tk, tn), lambda i, j, kk: (kk, j), **kw))
    n_ex, n_out = len(extras) + len(consts), len(out_dtypes)
    dims = _DIMS[mode]

    def body(a_ref, b_ref, *rest):
        ex_refs, out_refs = rest[:n_ex], rest[n_ex:n_ex + n_out]

        def finish(acc):
            outs = (acc,) if epilogue is None else epilogue(acc, *[e[...] for e in ex_refs])
            for o_ref, o in zip(out_refs, outs):
                o_ref[...] = o.astype(o_ref.dtype)

        prod = lax.dot_general(a_ref[...], b_ref[...], dims, preferred_element_type=F32)
        if gk == 1:
            finish(prod)
        else:
            acc_ref = rest[n_ex + n_out]
            kk = pl.program_id(2)

            @pl.when(kk == 0)
            def _():
                acc_ref[...] = prod

            @pl.when(kk > 0)
            def _():
                acc_ref[...] += prod

            @pl.when(kk == gk - 1)
            def _():
                finish(acc_ref[...])

    o_spec = spec((tm, tn), lambda i, j, kk: (i, j))
    outs = pl.pallas_call(
        body, name=name, grid=(gj, gi, gk) if n_outer else (gi, gj, gk),
        in_specs=[a_spec, b_spec] + [o_spec] * len(extras) + [spec((1, tn), lambda i, j, kk: (0, j))] * len(consts),
        out_specs=[o_spec] * n_out,
        out_shape=[jax.ShapeDtypeStruct((m, n), dt) for dt in out_dtypes],
        scratch_shapes=[pltpu.VMEM((tm, tn), F32)] if gk > 1 else [],
        compiler_params=_cparams(("parallel", "parallel", "arbitrary")),
    )(a, b, *extras, *consts)
    return outs[0] if n_out == 1 else outs


def _row_spec(tm, width, col_block=0):
    return pl.BlockSpec((tm, width), lambda i: (i, col_block))


def _full_spec(shape):
    return pl.BlockSpec(shape, lambda *_: (0,) * len(shape))


def _rmsnorm_fwd(name, h, g, tm=512):
    t = h.shape[0]

    def body(h_ref, g_ref, o_ref):
        x = h_ref[...]
        r = lax.rsqrt(jnp.mean(x * x, axis=-1, keepdims=True) + NORM_EPS)
        o_ref[...] = (x * r * g_ref[...]).astype(BF16)

    return pl.pallas_call(
        body, name=name, grid=(t // tm,),
        in_specs=[_row_spec(tm, D_MODEL), _full_spec((1, D_MODEL))],
        out_specs=_row_spec(tm, D_MODEL),
        out_shape=jax.ShapeDtypeStruct((t, D_MODEL), BF16),
        compiler_params=_cparams(("parallel",)),
    )(h, g)


def _proj_norm_bwd(name, a_list, a_specs, parts, w, h, g, dres, tm):
    t = h.shape[0]
    na = len(a_list)

    def body(*refs):
        a_refs = refs[:na]
        w_ref, h_ref, g_ref, dres_ref, dh_ref, dhb_ref, dg_ref = refs[na:]
        i = pl.program_id(0)
        dn_v = None
        for which, lead, k0, k1 in parts:
            a_ref = a_refs[which]
            term = _dot(a_ref[...] if lead is None else a_ref[lead], w_ref[:, k0:k1], "nt")
            dn_v = term if dn_v is None else dn_v + term
        x = h_ref[...]
        r = lax.rsqrt(jnp.mean(x * x, axis=-1, keepdims=True) + NORM_EPS)
        xh = x * r
        dxh = dn_v * g_ref[...]
        dh = dres_ref[...] + r * (dxh - xh * jnp.mean(dxh * xh, axis=-1, keepdims=True))
        dh_ref[...] = dh
        dhb_ref[...] = dh.astype(BF16)
        part = jnp.sum(dn_v * xh, axis=0, keepdims=True)

        @pl.when(i == 0)
        def _():
            dg_ref[...] = part

        @pl.when(i > 0)
        def _():
            dg_ref[...] += part

    row = _row_spec(tm, D_MODEL)
    return pl.pallas_call(
        body, name=name, grid=(t // tm,),
        in_specs=list(a_specs) + [pl.BlockSpec(w.shape, lambda i: (0, 0), pipeline_mode=pl.Buffered(1)), row,
                                  _full_spec((1, D_MODEL)), row],
        out_specs=[row, row, _full_spec((1, D_MODEL))],
        out_shape=[jax.ShapeDtypeStruct((t, D_MODEL), F32), jax.ShapeDtypeStruct((t, D_MODEL), BF16),
                   jax.ShapeDtypeStruct((1, D_MODEL), F32)],
        compiler_params=_cparams(("arbitrary",)),
    )(*a_list, w, h, g, dres)


def _out_loss_and_grad(name, mid, w_out, h, g, target, tm=512):
    t = h.shape[0]

    def body(m_ref, w_ref, h_ref, g_ref, t_ref, dh_ref, dhb_ref, dg_ref, loss_ref):
        i = pl.program_id(0)
        x = h_ref[...] + FFN_RES_WEIGHT * _dot(m_ref[...], w_ref[...])
        gv = g_ref[...]
        r = lax.rsqrt(jnp.mean(x * x, axis=-1, keepdims=True) + NORM_EPS)
        xh = x * r
        err = xh * gv - t_ref[...]
        row = jnp.mean(err * err, axis=-1, keepdims=True)
        part_loss = 0.5 * jnp.sum(row, axis=0, keepdims=True)
        dy = err * (1.0 / D_MODEL)
        dxh = dy * gv
        dh = r * (dxh - xh * jnp.mean(dxh * xh, axis=-1, keepdims=True))
        dh_ref[...] = dh
        dhb_ref[...] = dh.astype(BF16)
        part = jnp.sum(dy * xh, axis=0, keepdims=True)

        @pl.when(i == 0)
        def _():
            dg_ref[...] = part
            loss_ref[...] = jnp.zeros(loss_ref.shape, F32) + part_loss

        @pl.when(i > 0)
        def _():
            dg_ref[...] += part
            loss_ref[...] += part_loss

    return pl.pallas_call(
        body, name=name, grid=(t // tm,),
        in_specs=[_row_spec(tm, D_FF), pl.BlockSpec((D_FF, D_MODEL), lambda i: (0, 0), pipeline_mode=pl.Buffered(1)),
                  _row_spec(tm, D_MODEL), _full_spec((1, D_MODEL)), _row_spec(tm, D_MODEL)],
        out_specs=[_row_spec(tm, D_MODEL), _row_spec(tm, D_MODEL), _full_spec((1, D_MODEL)), _full_spec((8, 128))],
        out_shape=[jax.ShapeDtypeStruct((t, D_MODEL), F32), jax.ShapeDtypeStruct((t, D_MODEL), BF16),
                   jax.ShapeDtypeStruct((1, D_MODEL), F32), jax.ShapeDtypeStruct((8, 128), F32)],
        compiler_params=_cparams(("arbitrary",)),
    )(mid, w_out, h, g, target)


def _rope_tables(t):
    half = HEAD_DIM // 2
    inv_freq = np.float32(ROPE_BASE) ** (-np.arange(half, dtype=np.float32) / np.float32(half))
    ang = (np.arange(t, dtype=np.float32)[:, None] * inv_freq[None, :].astype(np.float32)).astype(np.float32)
    return jnp.asarray(np.cos(ang.astype(np.float64)).astype(np.float32)), jnp.asarray(np.sin(ang.astype(np.float64)).astype(np.float32))


ROPE_HALF = HEAD_DIM // 2
K_SCALE = HEAD_DIM ** -0.5


def _rotate(ref, hh, c, s, scale=None):
    lo, mid, hi = hh * HEAD_DIM, hh * HEAD_DIM + ROPE_HALF, (hh + 1) * HEAD_DIM
    x1, x2 = ref[:, lo:mid].astype(F32), ref[:, mid:hi].astype(F32)
    y = jnp.concatenate([x1 * c - x2 * s, x1 * s + x2 * c], axis=1)
    return y if scale is None else y * scale


def _unrotate_into(ref, hh, dy, c, s, scale=None):
    lo, mid, hi = hh * HEAD_DIM, hh * HEAD_DIM + ROPE_HALF, (hh + 1) * HEAD_DIM
    y1, y2 = dy[:, :ROPE_HALF], dy[:, ROPE_HALF:]
    d1, d2 = y1 * c + y2 * s, y2 * c - y1 * s
    if scale is not None:
        d1, d2 = d1 * scale, d2 * scale
    ref[:, lo:mid] = d1.astype(ref.dtype)
    ref[:, mid:hi] = d2.astype(ref.dtype)


def _retention_tables():
    b, chunk = RET_BLOCK, 64
    gamma = 1.0 - 2.0 ** (-5.0 - np.arange(HEADS, dtype=np.float64))
    log_g = np.log(gamma)[:, None, None]
    i = np.arange(b)[:, None]
    j = np.arange(b)[None, :]
    same = (i // chunk) == (j // chunk)
    earlier = (j // chunk) < (i // chunk)
    expo = np.where(same, np.abs(i - j), np.where(earlier, i - j, 0)).astype(np.float64)
    mask = np.where(same | earlier, 1.0, 0.0)
    dmat = np.exp(log_g * expo[None]) * mask[None]
    qd = np.exp(log_g[:, :, 0] * (np.arange(b)[None, :] + 1.0))
    kd = np.exp(log_g[:, :, 0] * (b - 1.0 - np.arange(b)[None, :]))
    cd = np.exp(log_g[:, :, 0] * b) * np.ones((1, HEAD_DIM))
    as32 = lambda v: jnp.asarray(v.astype(np.float32))
    return (as32(dmat), as32(np.swapaxes(dmat, 1, 2)), as32(qd[:, :, None]), as32(kd[:, :, None]), as32(cd[:, None, :]))


def _dot(a, b, mode="nn"):
    return lax.dot_general(a, b, _DIMS[mode], preferred_element_type=F32)


GRET_BLOCK = 3


def _head_specs(nb, rev=False):
    pos = (lambda n: nb - 1 - n) if rev else (lambda n: n)
    tok = pl.BlockSpec((RET_BLOCK, D_MODEL), lambda n: (pos(n), 0))
    blk = [pl.BlockSpec((RET_BLOCK, D_MODEL), lambda n, b=b: (pos(n), b)) for b in range(GRET_BLOCK + 1)]
    rope = pl.BlockSpec((RET_BLOCK, ROPE_HALF), lambda n: (pos(n), 0))
    tab = _full_spec((HEADS, RET_BLOCK, RET_BLOCK))
    col = _full_spec((HEADS, RET_BLOCK, 1))
    rowv = _full_spec((HEADS, 1, HEAD_DIM))
    st = pl.BlockSpec((HEADS, None, HEAD_DIM, HEAD_DIM), lambda n: (0, pos(n), 0, 0))
    return tok, blk, rope, tab, col, rowv, st


def _retention_fwd(name, proj, cos, sin, tables):
    t = proj.shape[0]
    nb = t // RET_BLOCK
    dmat, _, qd, kd, cd = tables
    tok, blk, rope, tab, col, rowv, st = _head_specs(nb)

    def body(q_ref, k_ref, v_ref, g_ref, c_ref, s_ref, d_ref, qd_ref, kd_ref, cd_ref, o_ref, ret_ref, st_ref, state):
        n = pl.program_id(0)

        @pl.when(n == 0)
        def _():
            state[...] = jnp.zeros(state.shape, F32)

        cs, sn = c_ref[...], s_ref[...]
        for hh in range(HEADS):
            sl = slice(hh * HEAD_DIM, (hh + 1) * HEAD_DIM)
            q, k, v = _rotate(q_ref, hh, cs, sn), _rotate(k_ref, hh, cs, sn, K_SCALE), v_ref[:, sl].astype(BF16)
            s = _dot(q.astype(BF16), k.astype(BF16), "nt") * d_ref[hh]
            stb = state[hh].astype(BF16)
            st_ref[hh] = stb
            o = _dot(s.astype(BF16), v) + _dot((q * qd_ref[hh]).astype(BF16), stb)
            o_ref[:, sl] = o
            rn = o * lax.rsqrt(jnp.mean(o * o, axis=-1, keepdims=True) + NORM_EPS)
            g = g_ref[:, sl].astype(F32)
            ret_ref[:, sl] = (rn * (g * _sigmoid(g))).astype(BF16)
            state[hh] = state[hh] * cd_ref[hh] + _dot((k * kd_ref[hh]).astype(BF16), v, "tn")

    return pl.pallas_call(
        body, name=name, grid=(nb,),
        in_specs=blk + [rope, rope, tab, col, col, rowv],
        out_specs=[tok, tok, st],
        out_shape=[jax.ShapeDtypeStruct((t, D_MODEL), F32), jax.ShapeDtypeStruct((t, D_MODEL), BF16),
                   jax.ShapeDtypeStruct((HEADS, nb, HEAD_DIM, HEAD_DIM), BF16)],
        scratch_shapes=[pltpu.VMEM((HEADS, HEAD_DIM, HEAD_DIM), F32)],
        compiler_params=_cparams(("arbitrary",)),
    )(proj, proj, proj, proj, cos, sin, dmat, qd, kd, cd)


def _retention_bwd(name, dru, w_ru, o, proj, cos, sin, states, tables):
    t = proj.shape[0]
    nb = t // RET_BLOCK
    dmat, dmat_t, qd, kd, cd = tables
    tok, blk, rope, tab, col, rowv, st = _head_specs(nb, rev=True)

    def body(dru_ref, wru_ref, o_ref, q_ref, k_ref, v_ref, g_ref, c_ref, s_ref, st_ref, d_ref, dt_ref, qd_ref, kd_ref, cd_ref,
             dq_ref, dk_ref, dv_ref, dg_ref, gstate):
        n = pl.program_id(0)

        @pl.when(n == 0)
        def _():
            gstate[...] = jnp.zeros(gstate.shape, F32)

        cs, sn = c_ref[...], s_ref[...]
        dret = _dot(dru_ref[...], wru_ref[...], "nt")
        for hh in range(HEADS):
            sl = slice(hh * HEAD_DIM, (hh + 1) * HEAD_DIM)
            o_v, g, dr = o_ref[:, sl], g_ref[:, sl].astype(F32), dret[:, sl]
            sg = _sigmoid(g)
            r = lax.rsqrt(jnp.mean(o_v * o_v, axis=-1, keepdims=True) + NORM_EPS)
            rn = o_v * r
            d_rn = dr * (g * sg)
            dg_ref[:, sl] = (dr * rn * (sg * (1.0 + g * (1.0 - sg)))).astype(BF16)
            d_o = r * (d_rn - rn * jnp.mean(d_rn * rn, axis=-1, keepdims=True))
            dob = d_o.astype(BF16)

            q, k, v = _rotate(q_ref, hh, cs, sn), _rotate(k_ref, hh, cs, sn, K_SCALE), v_ref[:, sl].astype(BF16)
            qb, kb = q.astype(BF16), k.astype(BF16)
            qdv, kdv = qd_ref[hh], kd_ref[hh]
            s_t = (_dot(kb, qb, "nt") * dt_ref[hh]).astype(BF16)
            p_t = (_dot(v, dob, "nt") * dt_ref[hh]).astype(BF16)
            p = (_dot(dob, v, "nt") * d_ref[hh]).astype(BF16)
            stb = st_ref[hh]
            gb = gstate[hh].astype(BF16)
            _unrotate_into(dq_ref, hh, _dot(p, kb) + _dot(dob, stb, "nt") * qdv, cs, sn)
            _unrotate_into(dk_ref, hh, _dot(p_t, qb) + _dot(v, gb, "nt") * kdv, cs, sn, K_SCALE)
            dv_ref[:, sl] = (_dot(s_t, dob) + _dot((k * kdv).astype(BF16), gb)).astype(BF16)
            gstate[hh] = gstate[hh] * cd_ref[hh] + _dot((q * qdv).astype(BF16), dob, "tn")

    return pl.pallas_call(
        body, name=name, grid=(nb,),
        in_specs=[tok, pl.BlockSpec((D_MODEL, D_MODEL), lambda n: (0, 0), pipeline_mode=pl.Buffered(1)), tok] + blk
                 + [rope, rope, st, tab, tab, col, col, rowv],
        out_specs=[tok, tok, tok, tok],
        out_shape=[jax.ShapeDtypeStruct((t, D_MODEL), BF16)] * 4,
        scratch_shapes=[pltpu.VMEM((HEADS, HEAD_DIM, HEAD_DIM), F32)],
        compiler_params=_cparams(("arbitrary",)),
    )(dru, w_ru, o, proj, proj, proj, proj, cos, sin, states, dmat, dmat_t, qd, kd, cd)


POOL_TILE = 256


def _pool_tables():
    b = POOL_TILE
    tt = np.arange(b)[:, None]
    jj = np.arange(b)[None, :]
    cur, prev = [], []
    for w in POOL_WINDOWS:
        cur.append(((tt - jj >= 0) & (tt - jj <= w - 1)).astype(np.float32))
        prev.append((tt - (jj - b) <= w - 1).astype(np.float32))
    cur, prev = np.stack(cur), np.stack(prev)
    as16 = lambda v: jnp.asarray(v, dtype=BF16)
    return as16(cur), as16(prev), as16(np.swapaxes(cur, 1, 2)), as16(np.swapaxes(prev, 1, 2))


def _split2(x):
    hi = x.astype(BF16)
    return hi, (x - hi.astype(F32)).astype(BF16)


POOL_BLOCK = 4


def _pool_count(n, window):
    tpos = n * POOL_TILE + lax.broadcasted_iota(jnp.int32, (POOL_TILE, 1), 0)
    return jnp.minimum(tpos + 1, window).astype(F32)


def _pool_fwd(name, proj, pool_w, scale, tables):
    t = proj.shape[0]
    nb = t // POOL_TILE
    mc, mp, _, _ = tables
    tab = _full_spec((GROUPS, POOL_TILE, POOL_TILE))
    row = _row_spec(POOL_TILE, D_MODEL)

    def body(pc_ref, pp_ref, mc_ref, mp_ref, w_ref, sc_ref, pm_ref, mix_ref, po_ref):
        n = pl.program_id(0)
        for g, window in enumerate(POOL_WINDOWS):
            sl = slice(g * GROUP_DIM, (g + 1) * GROUP_DIM)
            p = pc_ref[:, sl]
            win = _dot(mc_ref[g], p) + jnp.where(n > 0, _dot(mp_ref[g], pp_ref[:, sl]), 0.0)
            pm = (win / _pool_count(n, window) - p.astype(F32)).astype(BF16)
            pm_ref[:, sl] = pm
            mixed = _dot(pm, w_ref[g])
            mix_ref[:, sl] = mixed
            po_ref[:, sl] = (mixed * sc_ref[:, sl]).astype(BF16)

    return pl.pallas_call(
        body, name=name, grid=(nb,),
        in_specs=[_row_spec(POOL_TILE, D_MODEL, POOL_BLOCK),
                  pl.BlockSpec((POOL_TILE, D_MODEL), lambda n: (jnp.maximum(n - 1, 0), POOL_BLOCK)),
                  tab, tab, _full_spec((GROUPS, GROUP_DIM, GROUP_DIM)), _full_spec((1, D_MODEL))],
        out_specs=[row] * 3,
        out_shape=[jax.ShapeDtypeStruct((t, D_MODEL), BF16), jax.ShapeDtypeStruct((t, D_MODEL), F32),
                   jax.ShapeDtypeStruct((t, D_MODEL), BF16)],
        compiler_params=_cparams(("parallel",)),
    )(proj, proj, mc, mp, pool_w, scale)


def _pool_bwd(name, dpu, w_pu, pm, mixed, pool_w, scale, tables):
    t = dpu.shape[0]
    nb = t // POOL_TILE
    _, _, mct, mpt = tables
    cur = pl.BlockSpec((POOL_TILE, D_MODEL), lambda n: (nb - 1 - n, 0))
    tab = _full_spec((GROUPS, POOL_TILE, POOL_TILE))
    wspec = _full_spec((GROUPS, GROUP_DIM, GROUP_DIM))
    sspec = _full_spec((1, D_MODEL))

    def body(dpu_ref, wpu_ref, pm_ref, mix_ref, mct_ref, mpt_ref, w_ref, sc_ref, dp_ref, dw_ref, ds_ref, later):
        n = pl.program_id(0)

        @pl.when(n == 0)
        def _():
            dw_ref[...] = jnp.zeros(dw_ref.shape, F32)
            ds_ref[...] = jnp.zeros(ds_ref.shape, F32)
            later[...] = jnp.zeros(later.shape, F32)

        dpo = _dot(dpu_ref[...], wpu_ref[...], "nt")
        for g, window in enumerate(POOL_WINDOWS):
            sl = slice(g * GROUP_DIM, (g + 1) * GROUP_DIM)
            dc, sc = dpo[:, sl], sc_ref[:, sl]
            dmix = (dc * sc).astype(BF16)
            dpm = _dot(dmix, w_ref[g], "nt")
            e = dpm / _pool_count(nb - 1 - n, window)
            e_hi, e_lo = _split2(e)
            f_hi, f_lo = _split2(later[g])
            mctv, mptv = mct_ref[g], mpt_ref[g]
            back = _dot(mctv, e_hi) + _dot(mctv, e_lo)
            after = _dot(mptv, f_hi) + _dot(mptv, f_lo)
            dp_ref[:, sl] = (back + after - dpm).astype(BF16)
            later[g] = e
            dw_ref[g] += _dot(pm_ref[:, sl], dmix, "tn")
            ds_ref[:, sl] += jnp.sum(dc * mix_ref[:, sl], axis=0, keepdims=True)

    return pl.pallas_call(
        body, name=name, grid=(nb,),
        in_specs=[cur, pl.BlockSpec((D_MODEL, D_MODEL), lambda n: (0, 0), pipeline_mode=pl.Buffered(1)), cur, cur, tab, tab,
                  wspec, sspec],
        out_specs=[cur, wspec, sspec],
        out_shape=[jax.ShapeDtypeStruct((t, D_MODEL), BF16), jax.ShapeDtypeStruct((GROUPS, GROUP_DIM, GROUP_DIM), F32),
                   jax.ShapeDtypeStruct((1, D_MODEL), F32)],
        scratch_shapes=[pltpu.VMEM((GROUPS, POOL_TILE, GROUP_DIM), F32)],
        compiler_params=_cparams(("arbitrary",)),
    )(dpu, w_pu, pm, mixed, mct, mpt, pool_w, scale)


GATE0_BLOCK, GATE1_BLOCK = 5, 6


def _merge_fwd(name, ret, po, w_ru, w_pu, w_out, proj, bias, h, next_g, tm=512):
    t = ret.shape[0]

    def body(r_ref, p_ref, wr_ref, wp_ref, wo_ref, g0_ref, g1_ref, b_ref, h_ref, ng_ref, m_ref, ru_ref, pu_ref, ho_ref, n_ref):
        ru = _dot(r_ref[...], wr_ref[...])
        pu = _dot(p_ref[...], wp_ref[...])
        ru_ref[...] = ru
        pu_ref[...] = pu
        merged = (_sigmoid(g0_ref[...].astype(F32) + b_ref[0:1, :]) * ru
                  + _sigmoid(g1_ref[...].astype(F32) + b_ref[1:2, :]) * pu).astype(BF16)
        m_ref[...] = merged
        h_new = h_ref[...] + _dot(merged, wo_ref[...])
        ho_ref[...] = h_new
        n_ref[...] = _normed(h_new, ng_ref[...]).astype(BF16)

    row = _row_spec(tm, D_MODEL)
    wspec = pl.BlockSpec((D_MODEL, D_MODEL), lambda i: (0, 0), pipeline_mode=pl.Buffered(1))
    return pl.pallas_call(
        body, name=name, grid=(t // tm,),
        in_specs=[row, row, wspec, wspec, wspec, _row_spec(tm, D_MODEL, GATE0_BLOCK), _row_spec(tm, D_MODEL, GATE1_BLOCK),
                  _full_spec((2, D_MODEL)), row, _full_spec((1, D_MODEL))],
        out_specs=[row] * 5,
        out_shape=[jax.ShapeDtypeStruct((t, D_MODEL), BF16), jax.ShapeDtypeStruct((t, D_MODEL), F32),
                   jax.ShapeDtypeStruct((t, D_MODEL), F32), jax.ShapeDtypeStruct((t, D_MODEL), F32),
                   jax.ShapeDtypeStruct((t, D_MODEL), BF16)],
        compiler_params=_cparams(("parallel",)),
    )(ret, po, w_ru, w_pu, w_out, proj, proj, bias, h, next_g)


def _merge_bwd(name, dh_b, w_out, ru, pu, proj, bias, tm=512):
    t = dh_b.shape[0]

    def body(dh_ref, wo_ref, ru_ref, pu_ref, g0_ref, g1_ref, b_ref, dru_ref, dpu_ref, dg0_ref, dg1_ref, db_ref):
        i = pl.program_id(0)
        d = _dot(dh_ref[...], wo_ref[...], "nt")
        s0 = _sigmoid(g0_ref[...].astype(F32) + b_ref[0:1, :])
        s1 = _sigmoid(g1_ref[...].astype(F32) + b_ref[1:2, :])
        dru_ref[...] = (d * s0).astype(BF16)
        dpu_ref[...] = (d * s1).astype(BF16)
        dg0 = d * ru_ref[...] * (s0 * (1.0 - s0))
        dg1 = d * pu_ref[...] * (s1 * (1.0 - s1))
        dg0_ref[...] = dg0.astype(BF16)
        dg1_ref[...] = dg1.astype(BF16)
        part0 = jnp.sum(dg0, axis=0, keepdims=True)
        part1 = jnp.sum(dg1, axis=0, keepdims=True)

        @pl.when(i == 0)
        def _():
            db_ref[0:1, :] = part0
            db_ref[1:2, :] = part1

        @pl.when(i > 0)
        def _():
            db_ref[0:1, :] += part0
            db_ref[1:2, :] += part1

    row = _row_spec(tm, D_MODEL)
    return pl.pallas_call(
        body, name=name, grid=(t // tm,),
        in_specs=[row, pl.BlockSpec((D_MODEL, D_MODEL), lambda i: (0, 0), pipeline_mode=pl.Buffered(1)), row, row,
                  _row_spec(tm, D_MODEL, GATE0_BLOCK), _row_spec(tm, D_MODEL, GATE1_BLOCK), _full_spec((2, D_MODEL))],
        out_specs=[row, row, row, row, _full_spec((2, D_MODEL))],
        out_shape=[jax.ShapeDtypeStruct((t, D_MODEL), BF16)] * 4 + [jax.ShapeDtypeStruct((2, D_MODEL), F32)],
        compiler_params=_cparams(("arbitrary",)),
    )(dh_b, w_out, ru, pu, proj, proj, bias)


def _half_scale(acc):
    return (FFN_RES_WEIGHT * acc,)


def _normed(h, g):
    return h * lax.rsqrt(jnp.mean(h * h, axis=-1, keepdims=True) + NORM_EPS) * g


def _residual_half_norm(acc, res, g):
    h = res + FFN_RES_WEIGHT * acc
    return h, _normed(h, g)


FF_TILE = D_FF // 2
DW_TILE = 256
SAVED_FF_DTYPE = BF16
FF_CHUNKS = ((0, 512), (512, 1024), (1024, FF_TILE))


def _ffn_in(name, nrm, w_in, tm=512):
    t = nrm.shape[0]
    nj = D_FF // FF_TILE

    def body(n_ref, wg_ref, wu_ref, a_ref, mid_ref):
        nv = n_ref[...]
        for c0, c1 in FF_CHUNKS:
            gate = _dot(nv, wg_ref[:, c0:c1])
            up = _dot(nv, wu_ref[:, c0:c1])
            s = _sigmoid(gate)
            silu = gate * s
            a_ref[0, :, c0:c1] = (FFN_RES_WEIGHT * up * (s * (1.0 + gate * (1.0 - s)))).astype(a_ref.dtype)
            a_ref[1, :, c0:c1] = (FFN_RES_WEIGHT * silu).astype(a_ref.dtype)
            mid_ref[:, c0:c1] = (silu * up).astype(BF16)

    return pl.pallas_call(
        body, name=name, grid=(nj, t // tm),
        in_specs=[pl.BlockSpec((tm, D_MODEL), lambda j, i: (i, 0)),
                  pl.BlockSpec((D_MODEL, FF_TILE), lambda j, i: (0, j)),
                  pl.BlockSpec((D_MODEL, FF_TILE), lambda j, i: (0, j + nj))],
        out_specs=[pl.BlockSpec((2, tm, FF_TILE), lambda j, i: (0, i, j)), pl.BlockSpec((tm, FF_TILE), lambda j, i: (i, j))],
        out_shape=[jax.ShapeDtypeStruct((2, t, D_FF), SAVED_FF_DTYPE), jax.ShapeDtypeStruct((t, D_FF), BF16)],
        compiler_params=_cparams(("parallel", "parallel")),
    )(nrm, w_in, w_in)


def _ffn_dact(name, dout_b, w_out, a, tm=512):
    t = dout_b.shape[0]

    def body(d_ref, w_ref, a_ref, da_ref):
        dv = d_ref[...]
        for c0, c1 in FF_CHUNKS:
            dm = _dot(dv, w_ref[c0:c1, :], "nt")
            da_ref[0, :, c0:c1] = (dm * a_ref[0, :, c0:c1].astype(F32)).astype(BF16)
            da_ref[1, :, c0:c1] = (dm * a_ref[1, :, c0:c1].astype(F32)).astype(BF16)

    blk = pl.BlockSpec((2, tm, FF_TILE), lambda j, i: (0, i, j))
    return pl.pallas_call(
        body, name=name, grid=(D_FF // FF_TILE, t // tm),
        in_specs=[pl.BlockSpec((tm, D_MODEL), lambda j, i: (i, 0)), pl.BlockSpec((FF_TILE, D_MODEL), lambda j, i: (j, 0)), blk],
        out_specs=blk,
        out_shape=jax.ShapeDtypeStruct((2, t, D_FF), BF16),
        compiler_params=_cparams(("parallel", "parallel")),
    )(dout_b, w_out, a)


def _ffn_fwd(tag, h, nrm, get_w_in, get_w_out, finish):
    t = h.shape[0]
    w_in = get_w_in(nrm)
    a, mid = _ffn_in(f"{tag}_in", nrm, w_in, tm=min(512, t))
    w_out = get_w_out(mid)
    return finish(mid, w_out), (nrm, a, mid, w_in, w_out)


def _ffn_bwd(tag, h, g, saved, dout, dout_b, on_grads, flush):
    t = h.shape[0]
    nrm, a, mid, w_in, w_out = saved
    d_w_out = _matmul(f"{tag}_dwout", mid, dout_b, "tn", D_FF, D_MODEL, t, DW_TILE, D_MODEL, t, [BF16], epilogue=_half_scale,
                      resident="b")
    da = _ffn_dact(f"{tag}_dact", dout_b, w_out, a, tm=min(512, t))
    nj = D_FF // DW_TILE
    d_w_in = _dw_resident(f"{tag}_dwin", nrm, [da], [pl.BlockSpec((None, t, DW_TILE), lambda s: (s // nj, 0, s % nj))],
                          2 * nj, None, DW_TILE)
    tie = on_grads({f"{tag}_w_in": d_w_in, f"{tag}_w_out": d_w_out})
    tm = min(256, t)
    dh, dh_b, dg = _proj_norm_bwd(f"{tag}_dn", [da], [pl.BlockSpec((2, tm, D_FF), lambda i: (0, i, 0))],
                                  ((0, 0, 0, D_FF), (0, 1, D_FF, 2 * D_FF)), w_in, h, g if tie is None else g + tie, dout, tm)
    return dh, dh_b, dg, flush(dh)


def _dw_resident(name, u, pieces, piece_specs, n_tiles, which_piece, tn):
    t = u.shape[0]
    npc = len(pieces)

    def body(*refs):
        u_ref, p_refs, o_ref, ut_ref = refs[0], refs[1:1 + npc], refs[1 + npc], refs[2 + npc]
        s = pl.program_id(0)

        @pl.when(s == 0)
        def _():
            ut_ref[...] = u_ref[...].T

        if npc == 1:
            o_ref[...] = _dot(ut_ref[...], p_refs[0][...]).astype(BF16)
        for which in range(npc if npc > 1 else 0):
            @pl.when(which_piece(s) == which)
            def _(which=which):
                o_ref[...] = _dot(ut_ref[...], p_refs[which][...]).astype(BF16)

    return pl.pallas_call(
        body, name=name, grid=(n_tiles,),
        in_specs=[pl.BlockSpec((t, D_MODEL), lambda s: (0, 0), pipeline_mode=pl.Buffered(1))] + list(piece_specs),
        out_specs=pl.BlockSpec((D_MODEL, tn), lambda s: (0, s)),
        out_shape=jax.ShapeDtypeStruct((D_MODEL, n_tiles * tn), BF16),
        scratch_shapes=[pltpu.VMEM((D_MODEL, t), BF16)],
        compiler_params=_cparams(("arbitrary",)),
    )(u, *pieces)


def _mix_dwin(name, u, pieces, tn=256):
    t = u.shape[0]
    nj = D_MODEL // tn
    specs = [pl.BlockSpec((t, tn), lambda s, k=k: (0, jnp.clip(s - k * nj, 0, nj - 1))) for k in range(len(pieces))]
    return _dw_resident(name, u, pieces, specs, len(pieces) * nj, lambda s: s // nj, tn)


def _local_step(x, target, vec, get_w, on_grads, flush):
    t = x.shape[0]
    cos, sin = _rope_tables(t)
    rtab = _retention_tables()
    ptab = _pool_tables()
    w = {}

    def getter(group, name):
        def get(after):
            if name not in w:
                w.update(get_w(group, after))
            return w[name]
        return get

    nrm1 = _rmsnorm_fwd("ffn1_norm", x, vec["norm_ffn1"])
    def out_and_norm(mid, w_out):
        return _matmul("ffn1_out", mid, w_out, "nn", t, D_MODEL, D_FF, 512, D_MODEL, D_FF, [F32, BF16],
                       extras=(x,), consts=(vec["norm_mix"],), epilogue=_residual_half_norm)

    (h1, u), s1 = _ffn_fwd("ffn1", x, nrm1, getter(0, "ffn1_w_in"), getter(1, "ffn1_w_out"), out_and_norm)
    w.update(get_w(2, u))
    proj = _matmul("mix_in", u, w["w_in"], "nn", t, IN_WIDTH, D_MODEL, 1024, 1024, D_MODEL, [BF16], n_outer=True)
    o, ret, states = _retention_fwd("retention", proj, cos, sin, rtab)
    pm, mixed, po = _pool_fwd("pool", proj, w["pool_w"], vec["pool_scale"], ptab)
    merged, ru, pu, h2, nrm2 = _merge_fwd("merge", ret, po, w["w_ret_up"], w["w_pool_up"], w["w_out"], proj, vec["gate_bias"],
                                          h1, vec["norm_ffn2"], tm=min(512, t))
    def out_and_loss(mid, w_out):
        return _out_loss_and_grad("ffn2_out_loss", mid, w_out, h2, vec["norm_final"], target, tm=min(512, t))

    (dh3, dh3_b, dg_final, loss), s2 = _ffn_fwd("ffn2", h2, nrm2, getter(3, "ffn2_w_in"), getter(3, "ffn2_w_out"), out_and_loss)

    def tied(v, tie):
        return v if tie is None else v + tie

    dh2, dh2_b, dg_ffn2, tie = _ffn_bwd("ffn2", h2, vec["norm_ffn2"], s2, dh3, dh3_b, on_grads, flush)
    def square_dw(name, act, grad):
        return _matmul(name, act, grad, "tn", D_MODEL, D_MODEL, t, D_MODEL, D_MODEL, 1024, [BF16])

    d_w_out = square_dw("mix_dwout", merged, dh2_b)
    dru, dpu, dg0, dg1, d_bias = _merge_bwd("merge_bwd", dh2_b, w["w_out"], ru, pu, proj, tied(vec["gate_bias"], tie))
    d_w_ru = square_dw("mix_dwru", ret, dru)
    d_w_pu = square_dw("mix_dwpu", po, dpu)
    dp, d_pool_w, d_scale = _pool_bwd("pool_bwd", dpu, w["w_pool_up"], pm, mixed, w["pool_w"], vec["pool_scale"], ptab)
    dq, dk, dv, dgr = _retention_bwd("retention_bwd", dru, w["w_ret_up"], o, proj, cos, sin, states, rtab)
    dproj = [dq, dk, dv, dgr, dp, dg0, dg1]
    d_w_in = _mix_dwin("mix_dwin", u, dproj)
    tie = on_grads(dict(w_in=d_w_in, pool_w=d_pool_w.astype(BF16), w_ret_up=d_w_ru, w_pool_up=d_w_pu, w_out=d_w_out))
    tm = min(256, t)
    dh1, dh1_b, dg_mix = _proj_norm_bwd("mix_du", dproj, [_row_spec(tm, D_MODEL)] * len(dproj),
                                        [(k, None, k * D_MODEL, (k + 1) * D_MODEL) for k in range(len(dproj))],
                                        w["w_in"], h1, tied(vec["norm_mix"], tie), dh2, tm)
    tie = flush(dh1)
    dx, _, dg_ffn1, _ = _ffn_bwd("ffn1", x, tied(vec["norm_ffn1"], tie), s1, dh1, dh1_b, on_grads, flush)

    small = dict(norm_ffn1=dg_ffn1, norm_mix=dg_mix, gate_bias=d_bias, pool_scale=d_scale, norm_ffn2=dg_ffn2,
                 norm_final=dg_final)
    return loss[0, 0], dx, small


BIG = ("ffn1_w_in", "ffn1_w_out", "w_in", "pool_w", "w_ret_up", "w_pool_up", "w_out", "ffn2_w_in", "ffn2_w_out")
KIND = dict(ffn1_w_in="col", ffn1_w_out="row", w_in="col", pool_w="pool", w_ret_up="row", w_pool_up="row", w_out="row",
            ffn2_w_in="col", ffn2_w_out="row")
ANY = pl.BlockSpec(memory_space=pl.ANY)


def _place():
    x, y, c = lax.axis_index("x"), lax.axis_index("y"), lax.axis_index("c")
    chips = [(1 - x, y), (x, 1 - y), (1 - x, 1 - y)]
    return x, y, c, chips


def _full_view_shape(kind, local_shape):
    if kind == "col":
        return (2, local_shape[0] // 2, N_CHIPS * local_shape[1])
    if kind == "row":
        return (N_CHIPS, 2, local_shape[0] // 2, local_shape[1])
    return (GROUPS, N_CHIPS, 2, local_shape[1] // 2, local_shape[2])


def _local_view(kind, arr):
    if kind == "pool":
        return arr.reshape(GROUPS, 2, arr.shape[1] // 2, arr.shape[2])
    return arr.reshape(2, arr.shape[0] // 2, arr.shape[1])


def _blk(kind, ref, s, c):
    if kind == "col":
        cs = ref.shape[2] // N_CHIPS
        return ref.at[c, :, pl.ds(pl.multiple_of(s * cs, 128), cs)]
    if kind == "row":
        return ref.at[s, c]
    return ref.at[:, s, c]


def _half(kind, ref, c):
    return ref.at[:, c] if kind == "pool" else ref.at[c]


def _shard(kind, ref, s):
    if kind == "col":
        cs = ref.shape[2] // N_CHIPS
        return ref.at[:, :, pl.ds(pl.multiple_of(s * cs, 128), cs)]
    if kind == "row":
        return ref.at[s]
    return ref.at[:, s]


HBM = pl.BlockSpec(memory_space=pltpu.HBM)
SEM = pl.BlockSpec(memory_space=pltpu.SEMAPHORE)
EFFECT = pltpu.SideEffectType.DATAFLOW_SIDE_EFFECTING
WEIGHT_GROUPS = (("ffn1_w_in",), ("ffn1_w_out",), ("w_in", "pool_w", "w_ret_up", "w_pool_up", "w_out"), ("ffn2_w_in", "ffn2_w_out"))
GRAD_GROUPS = (("ffn2_w_in", "ffn2_w_out"), ("w_in", "pool_w", "w_ret_up", "w_pool_up", "w_out"), ("ffn1_w_in", "ffn1_w_out"))


def _hbm(a):
    return pltpu.with_memory_space_constraint(a, pltpu.HBM)


def _natural(kind, o):
    if kind == "col":
        return o.reshape(o.shape[0] * o.shape[1], o.shape[2])
    if kind == "row":
        return o.reshape(-1, o.shape[3])
    return o.reshape(GROUPS, -1, o.shape[4])


def _ici_copy(kind, loc, full, j, chips, s, c, send_sem, recv_sem):
    px, py = chips[j]
    return (pltpu.make_async_remote_copy(src_ref=_half(kind, loc, c), dst_ref=_blk(kind, full, s, c), send_sem=send_sem,
                                         recv_sem=recv_sem, device_id=(px, py, c), device_id_type=MESH),
            pltpu.make_async_remote_copy(src_ref=_half(kind, loc, c), dst_ref=_blk(kind, full, 2 * px + py, c), send_sem=send_sem,
                                         recv_sem=recv_sem, device_id=(px, py, c), device_id_type=MESH))


def _gather_start(tag, group_ids, shards):
    grps = [WEIGHT_GROUPS[g] for g in group_ids]
    names = [nm for grp in grps for nm in grp]
    kinds = [KIND[nm] for nm in names]
    n, ng = len(names), len(grps)
    locs = [_hbm(_local_view(KIND[nm], shards[nm])) for nm in names]
    lands = [_hbm(lax.empty(_full_view_shape(KIND[nm], shards[nm].shape), BF16)) for nm in names]
    first = np.cumsum([0] + [len(grp) for grp in grps])

    def body(*refs):
        loc, full = refs[:n], refs[n:2 * n]
        send_sems, recv_sems = refs[2 * n:2 * n + ng], refs[2 * n + ng:2 * n + 2 * ng]
        token = refs[-1]
        x, y, c, chips = _place()
        s = 2 * x + y
        for g in range(ng):
            for a in range(first[g], first[g + 1]):
                for j in range(3):
                    k = 3 * (a - first[g]) + j
                    _ici_copy(kinds[a], loc[a], full[a], j, chips, s, c, send_sems[g].at[k], recv_sems[g].at[k])[0].start()
        token[...] = jnp.zeros(token.shape, F32)

    sem_shapes = [pltpu.SemaphoreType.DMA((3 * len(grp),)) for grp in grps]
    outs = pl.pallas_call(
        body, name=f"gather_start_{tag}",
        in_specs=[HBM] * (2 * n),
        out_specs=[SEM] * (2 * ng) + [HBM] * (2 * n) + [pl.BlockSpec(memory_space=pltpu.VMEM)],
        out_shape=sem_shapes + sem_shapes + [pltpu.HBM(a.shape, a.dtype) for a in locs + lands] + [jax.ShapeDtypeStruct((8, 128), F32)],
        input_output_aliases={i: 2 * ng + i for i in range(2 * n)},
        compiler_params=pltpu.CompilerParams(has_side_effects=EFFECT),
    )(*locs, *lands)
    send_sems, recv_sems = outs[:ng], outs[ng:2 * ng]
    locs_t, lands_t = outs[2 * ng:2 * ng + n], outs[2 * ng + n:2 * ng + 2 * n]
    groups = {}
    for k, g in enumerate(group_ids):
        sl = slice(first[k], first[k + 1])
        groups[g] = (send_sems[k], recv_sems[k], list(locs_t[sl]), list(lands_t[sl]))
    return groups, outs[-1]


def _gather_finish(g, group, after):
    names = WEIGHT_GROUPS[g]
    kinds = [KIND[nm] for nm in names]
    m = len(names)
    send_sem, recv_sem, locs, lands = group

    def wait_body(*refs):
        loc, full = refs[:m], refs[m:2 * m]
        send_sems, recv_sems = refs[2 * m], refs[2 * m + 1]
        x, y, c, chips = _place()
        s = 2 * x + y
        for a in range(m):
            for j in range(3):
                k = 3 * a + j
                sent, landed = _ici_copy(kinds[a], loc[a], full[a], j, chips, s, c, send_sems.at[k], recv_sems.at[k])
                sent.wait_send()
                landed.wait_recv()

    outs = pl.pallas_call(
        wait_body, name=f"gather_wait_{g}",
        in_specs=[HBM] * (2 * m) + [SEM, SEM] + [ANY] * len(after), out_specs=[HBM] * (2 * m),
        out_shape=[pltpu.HBM(a.shape, a.dtype) for a in locs + lands],
        input_output_aliases={i: i for i in range(2 * m)},
        compiler_params=pltpu.CompilerParams(has_side_effects=EFFECT),
    )(*locs, *lands, send_sem, recv_sem, *after)
    locs, lands = outs[:m], outs[m:]

    def forward_body(*refs):
        loc, full = refs[:m], refs[2 * m:3 * m]
        send_sems, recv_sems = refs[3 * m:]
        x, y, c, chips = _place()
        s = 2 * x + y
        sib = (x, y, 1 - c)

        def remote(a, k, src, dst):
            return pltpu.make_async_remote_copy(src_ref=src, dst_ref=dst, send_sem=send_sems.at[4 * a + k],
                                                recv_sem=recv_sems.at[4 * a + k], device_id=sib, device_id_type=MESH)

        sends = []
        for a in range(m):
            for j, (px, py) in enumerate(chips):
                theirs = _blk(kinds[a], full[a], 2 * px + py, c)
                sends.append(remote(a, j, theirs, theirs))
            sends.append(remote(a, 3, loc[a], _shard(kinds[a], full[a], s)))
        for cp in sends:
            cp.start()
        for a in range(m):
            for j, (px, py) in enumerate(chips):
                from_sib = _blk(kinds[a], full[a], 2 * px + py, 1 - c)
                remote(a, j, from_sib, from_sib).wait_recv()
            own = _shard(kinds[a], full[a], s)
            remote(a, 3, own, own).wait_recv()
        for cp in sends:
            cp.wait_send()

    outs = pl.pallas_call(
        forward_body, name=f"gather_forward_{g}",
        in_specs=[ANY] * (2 * m), out_specs=[ANY] * m,
        out_shape=[jax.ShapeDtypeStruct(a.shape, a.dtype) for a in lands],
        input_output_aliases={m + i: i for i in range(m)},
        scratch_shapes=[pltpu.SemaphoreType.DMA((4 * m,)), pltpu.SemaphoreType.DMA((4 * m,))],
    )(*locs, *lands)
    return {nm: _natural(k, o) for nm, k, o in zip(names, kinds, outs)}


def _grad_view(kind, g):
    if kind == "col":
        return g.reshape(2, g.shape[0] // 2, g.shape[1])
    if kind == "row":
        return g.reshape(N_CHIPS, 2, g.shape[0] // (2 * N_CHIPS), g.shape[1])
    return g.reshape(GROUPS, N_CHIPS, 2, g.shape[1] // (2 * N_CHIPS), g.shape[2])


def _pair_copies(kinds, g, got, send_sems, recv_sems):
    x, y, c, _ = _place()

    def other_half(kind, ref):
        if kind == "col":
            return ref.at[1 - c]
        if kind == "row":
            return ref.at[:, 1 - c]
        return ref.at[:, :, 1 - c]

    return [pltpu.make_async_remote_copy(src_ref=other_half(kinds[a], g[a]), dst_ref=got[a], send_sem=send_sems.at[a],
                                         recv_sem=recv_sems.at[a], device_id=(x, y, 1 - c), device_id_type=MESH)
            for a in range(len(kinds))]


def _pair_exchange_start(tag, names, views):
    kinds = [KIND[nm] for nm in names]
    n = len(names)

    def got_shape(kind, v):
        if kind == "col":
            return v.shape[1:]
        if kind == "row":
            return (v.shape[0],) + v.shape[2:]
        return v.shape[:2] + v.shape[3:]

    srcs = [_hbm(views[nm]) for nm in names]
    lands = [_hbm(lax.empty(got_shape(k, views[nm]), BF16)) for nm, k in zip(names, kinds)]

    def body(*refs):
        g, got = refs[:n], refs[n:2 * n]
        for cp in _pair_copies(kinds, g, got, refs[2 * n], refs[2 * n + 1]):
            cp.start()
        refs[-1][...] = jnp.zeros(refs[-1].shape, F32)

    sem_shape = pltpu.SemaphoreType.DMA((n,))
    outs = pl.pallas_call(
        body, name=f"grad_pair_exchange_start_{tag}",
        in_specs=[HBM] * (2 * n),
        out_specs=[SEM, SEM] + [HBM] * (2 * n) + [pl.BlockSpec(memory_space=pltpu.VMEM)],
        out_shape=[sem_shape, sem_shape] + [pltpu.HBM(a.shape, a.dtype) for a in srcs + lands] + [jax.ShapeDtypeStruct((8, 128), F32)],
        input_output_aliases={i: 2 + i for i in range(2 * n)},
        compiler_params=pltpu.CompilerParams(has_side_effects=EFFECT),
    )(*srcs, *lands)
    return (outs[0], outs[1], list(outs[2:2 + n]), list(outs[2 + n:2 + 2 * n])), outs[-1]


def _pair_exchange_wait(tag, names, state, after):
    kinds = [KIND[nm] for nm in names]
    n = len(names)
    send_sem, recv_sem, srcs, lands = state

    def body(*refs):
        g, got = refs[:n], refs[n:2 * n]
        for cp in _pair_copies(kinds, g, got, refs[2 * n], refs[2 * n + 1]):
            cp.wait_send()
            cp.wait_recv()

    outs = pl.pallas_call(
        body, name=f"grad_pair_exchange_wait_{tag}",
        in_specs=[HBM] * (2 * n) + [SEM, SEM, ANY], out_specs=[HBM] * (2 * n),
        out_shape=[pltpu.HBM(a.shape, a.dtype) for a in srcs + lands],
        input_output_aliases={i: i for i in range(2 * n)},
        compiler_params=pltpu.CompilerParams(has_side_effects=EFFECT),
    )(*srcs, *lands, send_sem, recv_sem, after)
    return dict(zip(names, outs[:n])), dict(zip(names, outs[n:]))


def _pair_sum(name, kind, view, got, c_arr):
    if kind == "col":
        _, rows, cols = view.shape
        tr = 128
        grid = (rows // tr,)
        v_spec = pl.BlockSpec((None, tr, cols), lambda i, c: (c[0], i, 0))
        g_spec = pl.BlockSpec((tr, cols), lambda i, c: (i, 0))
    elif kind == "row":
        _, _, rows, cols = view.shape
        grid = (N_CHIPS,)
        v_spec = pl.BlockSpec((None, None, rows, cols), lambda i, c: (i, c[0], 0, 0))
        g_spec = pl.BlockSpec((None, rows, cols), lambda i, c: (i, 0, 0))
    else:
        _, _, _, rows, cols = view.shape
        grid = (GROUPS,)
        v_spec = pl.BlockSpec((None, N_CHIPS, None, rows, cols), lambda i, c: (i, 0, c[0], 0, 0))
        g_spec = pl.BlockSpec((None, N_CHIPS, rows, cols), lambda i, c: (i, 0, 0, 0))

    def body(c_ref, v_ref, g_ref, o_ref):
        o_ref[...] = (v_ref[...].astype(F32) + g_ref[...].astype(F32)).astype(BF16)

    return pl.pallas_call(
        body, name=name,
        grid_spec=pltpu.PrefetchScalarGridSpec(num_scalar_prefetch=1, grid=grid, in_specs=[v_spec, g_spec], out_specs=g_spec),
        out_shape=jax.ShapeDtypeStruct(got.shape, BF16),
        compiler_params=_cparams(("parallel",)),
    )(c_arr, view, got)


def _piece(kind, ref, s):
    if kind == "col":
        cs = ref.shape[1] // N_CHIPS
        return ref.at[:, pl.ds(pl.multiple_of(s * cs, 128), cs)]
    if kind == "row":
        return ref.at[s]
    return ref.at[:, s]


def _piece_shape(kind, shape):
    if kind == "col":
        return (shape[0], shape[1] // N_CHIPS)
    if kind == "row":
        return shape[1:]
    return (shape[0],) + shape[2:]


def _shard_copies(kinds, p, got, send_sems, recv_sems):
    x, y, c, chips = _place()
    return [pltpu.make_async_remote_copy(src_ref=_piece(kinds[a], p[a], 2 * px + py), dst_ref=got[a].at[j],
                                         send_sem=send_sems.at[3 * a + j], recv_sem=recv_sems.at[3 * a + j],
                                         device_id=(px, py, c), device_id_type=MESH)
            for a in range(len(kinds)) for j, (px, py) in enumerate(chips)]


def _shard_exchange_start(g, names, psums):
    kinds = [KIND[nm] for nm in names]
    n = len(names)
    srcs = [_hbm(psums[nm]) for nm in names]
    lands = [_hbm(lax.empty((3,) + _piece_shape(k, psums[nm].shape), BF16)) for nm, k in zip(names, kinds)]

    def body(*refs):
        p, got = refs[:n], refs[n:2 * n]
        send_sems, recv_sems = refs[2 * n], refs[2 * n + 1]
        token = refs[-1]
        for cp in _shard_copies(kinds, p, got, send_sems, recv_sems):
            cp.start()
        token[...] = jnp.zeros(token.shape, F32)

    sem_shape = pltpu.SemaphoreType.DMA((3 * n,))
    outs = pl.pallas_call(
        body, name=f"grad_shard_exchange_start_{g}",
        in_specs=[HBM] * (2 * n),
        out_specs=[SEM, SEM] + [HBM] * (2 * n) + [pl.BlockSpec(memory_space=pltpu.VMEM)],
        out_shape=[sem_shape, sem_shape] + [pltpu.HBM(a.shape, a.dtype) for a in srcs + lands] + [jax.ShapeDtypeStruct((8, 128), F32)],
        input_output_aliases={i: 2 + i for i in range(2 * n)},
        compiler_params=pltpu.CompilerParams(has_side_effects=EFFECT),
    )(*srcs, *lands)
    return (outs[0], outs[1], list(outs[2:2 + n]), list(outs[2 + n:2 + 2 * n])), outs[-1]


def _shard_exchange_wait(g, names, state, after):
    kinds = [KIND[nm] for nm in names]
    n = len(names)
    send_sem, recv_sem, srcs, lands = state

    def body(*refs):
        p, got = refs[:n], refs[n:2 * n]
        for cp in _shard_copies(kinds, p, got, refs[2 * n], refs[2 * n + 1]):
            cp.wait_send()
            cp.wait_recv()

    outs = pl.pallas_call(
        body, name=f"grad_shard_exchange_wait_{g}",
        in_specs=[HBM] * (2 * n) + [SEM, SEM] + [ANY] * len(after), out_specs=[HBM] * (2 * n),
        out_shape=[pltpu.HBM(a.shape, a.dtype) for a in srcs + lands],
        input_output_aliases={i: i for i in range(2 * n)},
        compiler_params=pltpu.CompilerParams(has_side_effects=EFFECT),
    )(*srcs, *lands, send_sem, recv_sem, *after)
    return dict(zip(names, outs[:n])), dict(zip(names, outs[n:]))


def _shard_sum(name, kind, psum, got, sc_arr):
    if kind == "col":
        rows, cols = psum.shape
        cs = cols // N_CHIPS
        tr = 128
        grid = (rows // tr,)
        p_spec = pl.BlockSpec((tr, cs), lambda i, sc: (i, sc[0]))
        g_spec = pl.BlockSpec((3, tr, cs), lambda i, sc: (0, i, 0))
        o_spec = pl.BlockSpec((None, tr, cs), lambda i, sc: (sc[1], i, 0))
        out_shape = (2, rows, cs)
    elif kind == "row":
        _, rows, cols = psum.shape
        grid = (1,)
        p_spec = pl.BlockSpec((None, rows, cols), lambda i, sc: (sc[0], 0, 0))
        g_spec = pl.BlockSpec((3, rows, cols), lambda i, sc: (0, 0, 0))
        o_spec = pl.BlockSpec((None, rows, cols), lambda i, sc: (sc[1], 0, 0))
        out_shape = (2, rows, cols)
    else:
        _, _, rows, cols = psum.shape
        grid = (1,)
        p_spec = pl.BlockSpec((GROUPS, None, rows, cols), lambda i, sc: (0, sc[0], 0, 0))
        g_spec = pl.BlockSpec((3, GROUPS, rows, cols), lambda i, sc: (0, 0, 0, 0))
        o_spec = pl.BlockSpec((GROUPS, None, rows, cols), lambda i, sc: (0, sc[1], 0, 0))
        out_shape = (GROUPS, 2, rows, cols)

    def body(sc_ref, p_ref, g_ref, o_ref):
        o_ref[...] = ((p_ref[...].astype(F32) + g_ref[0].astype(F32)) + g_ref[1].astype(F32)) + g_ref[2].astype(F32)

    return pl.pallas_call(
        body, name=name,
        grid_spec=pltpu.PrefetchScalarGridSpec(num_scalar_prefetch=1, grid=grid, in_specs=[p_spec, g_spec], out_specs=o_spec),
        out_shape=jax.ShapeDtypeStruct(out_shape, F32),
        compiler_params=_cparams(("parallel",)),
    )(sc_arr, psum, got)


def _half_exchange(tag, names, bufs):
    kinds = [KIND[nm] for nm in names]
    n = len(names)

    def body(*refs):
        out = refs[n:2 * n]
        send_sems, recv_sems = refs[2 * n:]
        x, y, c, _ = _place()
        sib = (x, y, 1 - c)
        cps = []
        for a in range(n):
            mine = _half(kinds[a], out[a], c)
            cp = pltpu.make_async_remote_copy(src_ref=mine, dst_ref=mine, send_sem=send_sems.at[a], recv_sem=recv_sems.at[a],
                                              device_id=sib, device_id_type=MESH)
            cp.start()
            cps.append(cp)
        for a, cp in enumerate(cps):
            cp.wait_send()
            theirs = _half(kinds[a], out[a], 1 - c)
            pltpu.make_async_remote_copy(src_ref=theirs, dst_ref=theirs, send_sem=send_sems.at[a], recv_sem=recv_sems.at[a],
                                         device_id=sib, device_id_type=MESH).wait_recv()

    outs = pl.pallas_call(
        body, name=f"grad_half_exchange_{tag}",
        in_specs=[ANY] * n, out_specs=[ANY] * n,
        out_shape=[jax.ShapeDtypeStruct(bufs[nm].shape, F32) for nm in names],
        input_output_aliases={a: a for a in range(n)},
        scratch_shapes=[pltpu.SemaphoreType.DMA((n,)), pltpu.SemaphoreType.DMA((n,))],
    )(*[bufs[nm] for nm in names])
    return dict(zip(names, outs))


N_DEV = 8
SMALL_ROWS = 8


def _all_reduce_small(name, v):
    def body(v_ref, o_ref, token, buf, send_sems, recv_sems):
        token[...] = jnp.zeros(token.shape, F32)
        x, y, c, _ = _place()
        me = 4 * x + 2 * y + c
        buf[me] = v_ref[...]
        cps = []
        for r in range(1, N_DEV):
            to = (x ^ (r >> 2), y ^ ((r >> 1) & 1), c ^ (r & 1))
            cp = pltpu.make_async_remote_copy(src_ref=v_ref, dst_ref=buf.at[me], send_sem=send_sems.at[r - 1],
                                              recv_sem=recv_sems.at[r - 1], device_id=to, device_id_type=MESH)
            cp.start()
            cps.append(cp)
        for r in range(1, N_DEV):
            pltpu.make_async_remote_copy(src_ref=v_ref, dst_ref=buf.at[me ^ r], send_sem=send_sems.at[r - 1],
                                         recv_sem=recv_sems.at[r - 1], device_id=(x, y, c), device_id_type=MESH).wait_recv()
        for cp in cps:
            cp.wait_send()
        acc = buf[0]
        for d in range(1, N_DEV):
            acc = acc + buf[d]
        o_ref[...] = acc

    vm = pl.BlockSpec(memory_space=pltpu.VMEM)
    return pl.pallas_call(
        body, name=name, in_specs=[vm], out_specs=[vm, vm],
        out_shape=[jax.ShapeDtypeStruct((SMALL_ROWS, D_MODEL), F32), jax.ShapeDtypeStruct((8, 128), F32)],
        scratch_shapes=[pltpu.VMEM((N_DEV, SMALL_ROWS, D_MODEL), F32), pltpu.SemaphoreType.DMA((N_DEV - 1,)),
                        pltpu.SemaphoreType.DMA((N_DEV - 1,))],
    )(v)


def _adamw(name, w, g, m, v):
    rows, cols = w.shape
    tr = next((c for c in (256, 176, 128, 64, 32, 8) if rows % c == 0), rows)
    spec = pl.BlockSpec((tr, cols), lambda i: (i, 0))

    def body(w_ref, g_ref, m_ref, v_ref, d_ref, mo_ref, vo_ref):
        gv = g_ref[...]
        m_new = ADAM_B1 * m_ref[...] + (1.0 - ADAM_B1) * gv
        v_new = ADAM_B2 * v_ref[...] + (1.0 - ADAM_B2) * jnp.square(gv)
        m_hat = m_new / (1.0 - ADAM_B1 ** ADAM_STEP)
        v_hat = v_new / (1.0 - ADAM_B2 ** ADAM_STEP)
        d_ref[...] = -ADAM_LR * (m_hat / (jnp.sqrt(v_hat) + ADAM_EPS) + ADAM_WD * w_ref[...])
        mo_ref[...] = m_new
        vo_ref[...] = v_new

    return pl.pallas_call(
        body, name=name, grid=(rows // tr,),
        in_specs=[spec] * 4, out_specs=[spec] * 3,
        out_shape=[jax.ShapeDtypeStruct((rows, cols), F32)] * 3,
        compiler_params=_cparams(("parallel",)),
    )(w, g, m, v)


WEIGHTS = ("norm_ffn1", "ffn1_w_in", "ffn1_w_out", "norm_mix", "w_in", "gate_bias", "pool_w", "pool_scale", "w_ret_up",
           "w_pool_up", "w_out", "norm_ffn2", "ffn2_w_in", "ffn2_w_out", "norm_final")
SMALL_ROW = dict(norm_ffn1=0, norm_mix=1, gate_bias=2, pool_scale=4, norm_ffn2=5, norm_final=6)


def _as2d(a):
    return a.reshape(-1, a.shape[-1])


def kernel(x, norm_ffn1, ffn1_w_in, ffn1_w_out, norm_mix, w_in, gate_bias, pool_w, pool_scale, w_ret_up, w_pool_up, w_out, norm_ffn2, ffn2_w_in, ffn2_w_out, norm_final, loss_target, m_norm_ffn1, m_ffn1_w_in, m_ffn1_w_out, m_norm_mix, m_w_in, m_gate_bias, m_pool_w, m_pool_scale, m_w_ret_up, m_w_pool_up, m_w_out, m_norm_ffn2, m_ffn2_w_in, m_ffn2_w_out, m_norm_final, v_norm_ffn1, v_ffn1_w_in, v_ffn1_w_out, v_norm_mix, v_w_in, v_gate_bias, v_pool_w, v_pool_scale, v_w_ret_up, v_w_pool_up, v_w_out, v_norm_ffn2, v_ffn2_w_in, v_ffn2_w_out, v_norm_final):
    wt = dict(norm_ffn1=norm_ffn1, ffn1_w_in=ffn1_w_in, ffn1_w_out=ffn1_w_out, norm_mix=norm_mix, w_in=w_in, gate_bias=gate_bias,
              pool_w=pool_w, pool_scale=pool_scale, w_ret_up=w_ret_up, w_pool_up=w_pool_up, w_out=w_out, norm_ffn2=norm_ffn2,
              ffn2_w_in=ffn2_w_in, ffn2_w_out=ffn2_w_out, norm_final=norm_final)
    mom = dict(norm_ffn1=m_norm_ffn1, ffn1_w_in=m_ffn1_w_in, ffn1_w_out=m_ffn1_w_out, norm_mix=m_norm_mix, w_in=m_w_in,
               gate_bias=m_gate_bias, pool_w=m_pool_w, pool_scale=m_pool_scale, w_ret_up=m_w_ret_up, w_pool_up=m_w_pool_up,
               w_out=m_w_out, norm_ffn2=m_norm_ffn2, ffn2_w_in=m_ffn2_w_in, ffn2_w_out=m_ffn2_w_out, norm_final=m_norm_final)
    var = dict(norm_ffn1=v_norm_ffn1, ffn1_w_in=v_ffn1_w_in, ffn1_w_out=v_ffn1_w_out, norm_mix=v_norm_mix, w_in=v_w_in,
               gate_bias=v_gate_bias, pool_w=v_pool_w, pool_scale=v_pool_scale, w_ret_up=v_w_ret_up, w_pool_up=v_w_pool_up,
               w_out=v_w_out, norm_ffn2=v_norm_ffn2, ffn2_w_in=v_ffn2_w_in, ffn2_w_out=v_ffn2_w_out, norm_final=v_norm_final)

    ax, ay, ac = lax.axis_index("x"), lax.axis_index("y"), lax.axis_index("c")
    chip = 2 * ax + ay
    c_arr = jnp.reshape(ac, (1,)).astype(jnp.int32)
    sc_arr = jnp.stack([chip, ac]).astype(jnp.int32)
    bias_cols = gate_bias.shape[-1]

    placed = lax.dynamic_update_slice(jnp.zeros((SMALL_ROWS, D_MODEL), F32), gate_bias[0], (0, chip * bias_cols))
    bias_sum, token = _all_reduce_small("gather_gate_bias", jnp.where(ac == 0, placed, 0.0))
    bias_full = bias_sum[:2]
    first = WEIGHT_GROUPS[0]
    gather_groups, token = _gather_start("first", [0], {nm: wt[nm][0].astype(BF16) + token[0, 0].astype(BF16) for nm in first})
    rest, rest_token = _gather_start("rest", [1, 2, 3],
                                     {nm: wt[nm][0].astype(BF16) + token[0, 0].astype(BF16) for nm in BIG if nm not in first})
    gather_groups.update(rest)
    vec = dict(norm_ffn1=norm_ffn1, norm_mix=norm_mix, norm_ffn2=norm_ffn2, pool_scale=pool_scale,
               norm_final=norm_final.reshape(1, D_MODEL), gate_bias=bias_full)

    def get_w(g, after):
        return _gather_finish(g, gather_groups[g], (after, rest_token) if g == 0 else (after,))

    pairs, pending = [], []

    def on_grads(gr):
        g = len(pairs)
        names = GRAD_GROUPS[g]
        assert set(names) == set(gr), (names, list(gr))
        state, token = _pair_exchange_start(g, names, {nm: _grad_view(KIND[nm], gr[nm]) for nm in names})
        pairs.append(state)
        return token[0:1, 0:1]

    def flush(after):
        g = len(pending)
        names = GRAD_GROUPS[g]
        views, from_sib = _pair_exchange_wait(g, names, pairs[g], after)
        psums = {nm: _pair_sum(f"pair_sum_{nm}", KIND[nm], views[nm], from_sib[nm], c_arr) for nm in names}
        state, token = _shard_exchange_start(g, names, psums)
        pending.append(state)
        tokens.append(token)
        return token[0:1, 0:1]

    tokens = []
    loss_local, dx, small = _local_step(x[0], loss_target[0], vec, get_w, on_grads, flush)

    packed = jnp.concatenate([small["norm_ffn1"], small["norm_mix"], small["gate_bias"], small["pool_scale"],
                              small["norm_ffn2"], small["norm_final"], jnp.broadcast_to(loss_local, (1, D_MODEL))], axis=0)
    small_sum, _ = _all_reduce_small("reduce_small_grads", packed)
    loss = small_sum[SMALL_ROWS - 1, 0]
    grads, delta, new_m, new_v = {}, {}, {}, {}

    def adamw(nm):
        shape = wt[nm].shape
        d, m2, v2 = _adamw(f"adamw_{nm}", _as2d(wt[nm]), _as2d(grads[nm]), _as2d(mom[nm]), _as2d(var[nm]))
        delta[nm], new_m[nm], new_v[nm] = d.reshape(shape), m2.reshape(shape), v2.reshape(shape)
        return d

    for nm in ("norm_ffn1", "norm_mix", "pool_scale", "norm_ffn2"):
        grads[nm] = small_sum[SMALL_ROW[nm]][None, :]
    grads["norm_final"] = small_sum[SMALL_ROW["norm_final"]]
    grads["gate_bias"] = lax.dynamic_slice(small_sum, (SMALL_ROW["gate_bias"], chip * bias_cols), (2, bias_cols))[None]
    after = (tokens[-1],)
    for g, names in enumerate(GRAD_GROUPS):
        psums, from_chips = _shard_exchange_wait(g, names, pending[g], after)
        bufs = {nm: _shard_sum(f"shard_sum_{nm}", KIND[nm], psums[nm], from_chips[nm], sc_arr) for nm in names}
        reduced = _half_exchange(g, names, bufs)
        for nm in names:
            grads[nm] = reduced[nm].reshape(wt[nm].shape)
        after = tuple(adamw(nm) for nm in names)
    for nm in WEIGHTS:
        if nm not in delta:
            adamw(nm)

    return (loss, dx[None], *[grads[nm] for nm in WEIGHTS], *[delta[nm] for nm in WEIGHTS],
            *[new_m[nm] for nm in WEIGHTS], *[new_v[nm] for nm in WEIGHTS])
```

```python
import functools

import numpy as np
import jax
import jax.numpy as jnp
from jax import lax
from jax.experimental import pallas as pl
from jax.experimental.pallas import tpu as pltpu

F32 = jnp.float32
BF16 = jnp.bfloat16
MESH = pl.DeviceIdType.MESH

D_MODEL = 1024
D_FF = 2816
HEADS = 4
HEAD_DIM = 256
GROUPS = 4
GROUP_DIM = 256
POOL_WINDOWS = (2, 4, 8, 16)
IN_WIDTH = 7 * D_MODEL
ROPE_BASE = 10000.0
NORM_EPS = 1e-6
FFN_RES_WEIGHT = 0.5
ADAM_LR, ADAM_B1, ADAM_B2, ADAM_EPS, ADAM_WD, ADAM_STEP = 0.001, 0.9, 0.999, 1e-08, 0.01, 10

N_CHIPS = 4
RET_BLOCK = 256
V7X_VMEM_LIMIT = 56 * 1024 * 1024


def _cparams(sem):
    return pltpu.CompilerParams(dimension_semantics=sem, vmem_limit_bytes=V7X_VMEM_LIMIT)


def _sigmoid(x):
    return jax.nn.sigmoid(x)


_DIMS = {"nn": (((1,), (0,)), ((), ())), "nt": (((1,), (1,)), ((), ())), "tn": (((0,), (0,)), ((), ()))}


def _matmul(name, a, b, mode, m, n, k, tm, tn, tk, out_dtypes, a_spec=None, b_spec=None, extras=(), consts=(), epilogue=None,
            resident=None, n_outer=False):
    tm, tn, tk = min(tm, m), min(tn, n), min(tk, k)
    gi, gj, gk = m // tm, n // tn, k // tk
    assert gi * tm == m and gj * tn == n and gk * tk == k, (name, m, n, k, tm, tn, tk)
    assert not (n_outer and (a_spec is not None or b_spec is not None)), name
    once = dict(pipeline_mode=pl.Buffered(1))

    def spec(shape, index, **kw):
        return pl.BlockSpec(shape, (lambda j, i, kk: index(i, j, kk)) if n_outer else index, **kw)

    if a_spec is None:
        kw = once if resident == "a" else {}
        a_spec = (spec((tk, tm), lambda i, j, kk: (kk, i), **kw) if mode == "tn"
                  else spec((tm, tk), lambda i, j, kk: (i, kk), **kw))
    if b_spec is None:
        kw = once if resident == "b" else {}
        b_spec = (spec((tn, tk), lambda i, j, kk: (j, kk), **kw) if mode == "nt"
                  else spec((tk, tn), lambda i, j, kk: (kk, j), **kw))
    n_ex, n_out = len(extras) + len(consts), len(out_dtypes)
    dims = _DIMS[mode]

    def body(a_ref, b_ref, *rest):
        ex_refs, out_refs = rest[:n_ex], rest[n_ex:n_ex + n_out]

        def finish(acc):
            outs = (acc,) if epilogue is None else epilogue(acc, *[e[...] for e in ex_refs])
            for o_ref, o in zip(out_refs, outs):
                o_ref[...] = o.astype(o_ref.dtype)

        prod = lax.dot_general(a_ref[...], b_ref[...], dims, preferred_element_type=F32)
        if gk == 1:
            finish(prod)
        else:
            acc_ref = rest[n_ex + n_out]
            kk = pl.program_id(2)

            @pl.when(kk == 0)
            def _():
                acc_ref[...] = prod

            @pl.when(kk > 0)
            def _():
                acc_ref[...] += prod

            @pl.when(kk == gk - 1)
            def _():
                finish(acc_ref[...])

    o_spec = spec((tm, tn), lambda i, j, kk: (i, j))
    outs = pl.pallas_call(
        body, name=name, grid=(gj, gi, gk) if n_outer else (gi, gj, gk),
        in_specs=[a_spec, b_spec] + [o_spec] * len(extras) + [spec((1, tn), lambda i, j, kk: (0, j))] * len(consts),
        out_specs=[o_spec] * n_out,
        out_shape=[jax.ShapeDtypeStruct((m, n), dt) for dt in out_dtypes],
        scratch_shapes=[pltpu.VMEM((tm, tn), F32)] if gk > 1 else [],
        compiler_params=_cparams(("parallel", "parallel", "arbitrary")),
    )(a, b, *extras, *consts)
    return outs[0] if n_out == 1 else outs


def _row_spec(tm, width, col_block=0):
    return pl.BlockSpec((tm, width), lambda i: (i, col_block))


def _full_spec(shape):
    return pl.BlockSpec(shape, lambda *_: (0,) * len(shape))


def _rmsnorm_fwd(name, h, g, tm=512):
    t = h.shape[0]

    def body(h_ref, g_ref, o_ref):
        x = h_ref[...]
        r = lax.rsqrt(jnp.mean(x * x, axis=-1, keepdims=True) + NORM_EPS)
        o_ref[...] = (x * r * g_ref[...]).astype(BF16)

    return pl.pallas_call(
        body, name=name, grid=(t // tm,),
        in_specs=[_row_spec(tm, D_MODEL), _full_spec((1, D_MODEL))],
        out_specs=_row_spec(tm, D_MODEL),
        out_shape=jax.ShapeDtypeStruct((t, D_MODEL), BF16),
        compiler_params=_cparams(("parallel",)),
    )(h, g)


def _proj_norm_bwd(name, a_list, a_specs, parts, w, h, g, dres, tm):
    t = h.shape[0]
    na = len(a_list)

    def body(*refs):
        a_refs = refs[:na]
        w_ref, h_ref, g_ref, dres_ref, dh_ref, dhb_ref, dg_ref = refs[na:]
        i = pl.program_id(0)
        dn_v = None
        for which, lead, k0, k1 in parts:
            a_ref = a_refs[which]
            term = _dot(a_ref[...] if lead is None else a_ref[lead], w_ref[:, k0:k1], "nt")
            dn_v = term if dn_v is None else dn_v + term
        x = h_ref[...]
        r = lax.rsqrt(jnp.mean(x * x, axis=-1, keepdims=True) + NORM_EPS)
        xh = x * r
        dxh = dn_v * g_ref[...]
        dh = dres_ref[...] + r * (dxh - xh * jnp.mean(dxh * xh, axis=-1, keepdims=True))
        dh_ref[...] = dh
        dhb_ref[...] = dh.astype(BF16)
        part = jnp.sum(dn_v * xh, axis=0, keepdims=True)

        @pl.when(i == 0)
        def _():
            dg_ref[...] = part

        @pl.when(i > 0)
        def _():
            dg_ref[...] += part

    row = _row_spec(tm, D_MODEL)
    return pl.pallas_call(
        body, name=name, grid=(t // tm,),
        in_specs=list(a_specs) + [pl.BlockSpec(w.shape, lambda i: (0, 0), pipeline_mode=pl.Buffered(1)), row,
                                  _full_spec((1, D_MODEL)), row],
        out_specs=[row, row, _full_spec((1, D_MODEL))],
        out_shape=[jax.ShapeDtypeStruct((t, D_MODEL), F32), jax.ShapeDtypeStruct((t, D_MODEL), BF16),
                   jax.ShapeDtypeStruct((1, D_MODEL), F32)],
        compiler_params=_cparams(("arbitrary",)),
    )(*a_list, w, h, g, dres)


def _out_loss_and_grad(name, mid, w_out, h, g, target, tm=512):
    t = h.shape[0]

    def body(m_ref, w_ref, h_ref, g_ref, t_ref, dh_ref, dhb_ref, dg_ref, loss_ref):
        i = pl.program_id(0)
        x = h_ref[...] + FFN_RES_WEIGHT * _dot(m_ref[...], w_ref[...])
        gv = g_ref[...]
        r = lax.rsqrt(jnp.mean(x * x, axis=-1, keepdims=True) + NORM_EPS)
        xh = x * r
        err = xh * gv - t_ref[...]
        row = jnp.mean(err * err, axis=-1, keepdims=True)
        part_loss = 0.5 * jnp.sum(row, axis=0, keepdims=True)
        dy = err * (1.0 / D_MODEL)
        dxh = dy * gv
        dh = r * (dxh - xh * jnp.mean(dxh * xh, axis=-1, keepdims=True))
        dh_ref[...] = dh
        dhb_ref[...] = dh.astype(BF16)
        part = jnp.sum(dy * xh, axis=0, keepdims=True)

        @pl.when(i == 0)
        def _():
            dg_ref[...] = part
            loss_ref[...] = jnp.zeros(loss_ref.shape, F32) + part_loss

        @pl.when(i > 0)
        def _():
            dg_ref[...] += part
            loss_ref[...] += part_loss

    return pl.pallas_call(
        body, name=name, grid=(t // tm,),
        in_specs=[_row_spec(tm, D_FF), pl.BlockSpec((D_FF, D_MODEL), lambda i: (0, 0), pipeline_mode=pl.Buffered(1)),
                  _row_spec(tm, D_MODEL), _full_spec((1, D_MODEL)), _row_spec(tm, D_MODEL)],
        out_specs=[_row_spec(tm, D_MODEL), _row_spec(tm, D_MODEL), _full_spec((1, D_MODEL)), _full_spec((8, 128))],
        out_shape=[jax.ShapeDtypeStruct((t, D_MODEL), F32), jax.ShapeDtypeStruct((t, D_MODEL), BF16),
                   jax.ShapeDtypeStruct((1, D_MODEL), F32), jax.ShapeDtypeStruct((8, 128), F32)],
        compiler_params=_cparams(("arbitrary",)),
    )(mid, w_out, h, g, target)


def _rope_tables(t):
    half = HEAD_DIM // 2
    inv_freq = np.float32(ROPE_BASE) ** (-np.arange(half, dtype=np.float32) / np.float32(half))
    ang = (np.arange(t, dtype=np.float32)[:, None] * inv_freq[None, :].astype(np.float32)).astype(np.float32)
    return jnp.asarray(np.cos(ang.astype(np.float64)).astype(np.float32)), jnp.asarray(np.sin(ang.astype(np.float64)).astype(np.float32))


ROPE_HALF = HEAD_DIM // 2
K_SCALE = HEAD_DIM ** -0.5


def _rotate(ref, hh, c, s, scale=None):
    lo, mid, hi = hh * HEAD_DIM, hh * HEAD_DIM + ROPE_HALF, (hh + 1) * HEAD_DIM
    x1, x2 = ref[:, lo:mid].astype(F32), ref[:, mid:hi].astype(F32)
    y = jnp.concatenate([x1 * c - x2 * s, x1 * s + x2 * c], axis=1)
    return y if scale is None else y * scale


def _unrotate_into(ref, hh, dy, c, s, scale=None):
    lo, mid, hi = hh * HEAD_DIM, hh * HEAD_DIM + ROPE_HALF, (hh + 1) * HEAD_DIM
    y1, y2 = dy[:, :ROPE_HALF], dy[:, ROPE_HALF:]
    d1, d2 = y1 * c + y2 * s, y2 * c - y1 * s
    if scale is not None:
        d1, d2 = d1 * scale, d2 * scale
    ref[:, lo:mid] = d1.astype(ref.dtype)
    ref[:, mid:hi] = d2.astype(ref.dtype)


def _retention_tables():
    b, chunk = RET_BLOCK, 64
    gamma = 1.0 - 2.0 ** (-5.0 - np.arange(HEADS, dtype=np.float64))
    log_g = np.log(gamma)[:, None, None]
    i = np.arange(b)[:, None]
    j = np.arange(b)[None, :]
    same = (i // chunk) == (j // chunk)
    earlier = (j // chunk) < (i // chunk)
    expo = np.where(same, np.abs(i - j), np.where(earlier, i - j, 0)).astype(np.float64)
    mask = np.where(same | earlier, 1.0, 0.0)
    dmat = np.exp(log_g * expo[None]) * mask[None]
    qd = np.exp(log_g[:, :, 0] * (np.arange(b)[None, :] + 1.0))
    kd = np.exp(log_g[:, :, 0] * (b - 1.0 - np.arange(b)[None, :]))
    cd = np.exp(log_g[:, :, 0] * b) * np.ones((1, HEAD_DIM))
    as32 = lambda v: jnp.asarray(v.astype(np.float32))
    return (as32(dmat), as32(np.swapaxes(dmat, 1, 2)), as32(qd[:, :, None]), as32(kd[:, :, None]), as32(cd[:, None, :]))


def _dot(a, b, mode="nn"):
    return lax.dot_general(a, b, _DIMS[mode], preferred_element_type=F32)


GRET_BLOCK = 3


def _head_specs(nb, rev=False):
    pos = (lambda n: nb - 1 - n) if rev else (lambda n: n)
    tok = pl.BlockSpec((RET_BLOCK, D_MODEL), lambda n: (pos(n), 0))
    blk = [pl.BlockSpec((RET_BLOCK, D_MODEL), lambda n, b=b: (pos(n), b)) for b in range(GRET_BLOCK + 1)]
    rope = pl.BlockSpec((RET_BLOCK, ROPE_HALF), lambda n: (pos(n), 0))
    tab = _full_spec((HEADS, RET_BLOCK, RET_BLOCK))
    col = _full_spec((HEADS, RET_BLOCK, 1))
    rowv = _full_spec((HEADS, 1, HEAD_DIM))
    st = pl.BlockSpec((HEADS, None, HEAD_DIM, HEAD_DIM), lambda n: (0, pos(n), 0, 0))
    return tok, blk, rope, tab, col, rowv, st


def _retention_fwd(name, proj, cos, sin, tables):
    t = proj.shape[0]
    nb = t // RET_BLOCK
    dmat, _, qd, kd, cd = tables
    tok, blk, rope, tab, col, rowv, st = _head_specs(nb)

    def body(q_ref, k_ref, v_ref, g_ref, c_ref, s_ref, d_ref, qd_ref, kd_ref, cd_ref, o_ref, ret_ref, st_ref, state):
        n = pl.program_id(0)

        @pl.when(n == 0)
        def _():
            state[...] = jnp.zeros(state.shape, F32)

        cs, sn = c_ref[...], s_ref[...]
        for hh in range(HEADS):
            sl = slice(hh * HEAD_DIM, (hh + 1) * HEAD_DIM)
            q, k, v = _rotate(q_ref, hh, cs, sn), _rotate(k_ref, hh, cs, sn, K_SCALE), v_ref[:, sl].astype(BF16)
            s = _dot(q.astype(BF16), k.astype(BF16), "nt") * d_ref[hh]
            stb = state[hh].astype(BF16)
            st_ref[hh] = stb
            o = _dot(s.astype(BF16), v) + _dot((q * qd_ref[hh]).astype(BF16), stb)
            o_ref[:, sl] = o
            rn = o * lax.rsqrt(jnp.mean(o * o, axis=-1, keepdims=True) + NORM_EPS)
            g = g_ref[:, sl].astype(F32)
            ret_ref[:, sl] = (rn * (g * _sigmoid(g))).astype(BF16)
            state[hh] = state[hh] * cd_ref[hh] + _dot((k * kd_ref[hh]).astype(BF16), v, "tn")

    return pl.pallas_call(
        body, name=name, grid=(nb,),
        in_specs=blk + [rope, rope, tab, col, col, rowv],
        out_specs=[tok, tok, st],
        out_shape=[jax.ShapeDtypeStruct((t, D_MODEL), F32), jax.ShapeDtypeStruct((t, D_MODEL), BF16),
                   jax.ShapeDtypeStruct((HEADS, nb, HEAD_DIM, HEAD_DIM), BF16)],
        scratch_shapes=[pltpu.VMEM((HEADS, HEAD_DIM, HEAD_DIM), F32)],
        compiler_params=_cparams(("arbitrary",)),
    )(proj, proj, proj, proj, cos, sin, dmat, qd, kd, cd)


def _retention_bwd(name, dru, w_ru, o, proj, cos, sin, states, tables):
    t = proj.shape[0]
    nb = t // RET_BLOCK
    dmat, dmat_t, qd, kd, cd = tables
    tok, blk, rope, tab, col, rowv, st = _head_specs(nb, rev=True)

    def body(dru_ref, wru_ref, o_ref, q_ref, k_ref, v_ref, g_ref, c_ref, s_ref, st_ref, d_ref, dt_ref, qd_ref, kd_ref, cd_ref,
             dq_ref, dk_ref, dv_ref, dg_ref, gstate):
        n = pl.program_id(0)

        @pl.when(n == 0)
        def _():
            gstate[...] = jnp.zeros(gstate.shape, F32)

        cs, sn = c_ref[...], s_ref[...]
        dret = _dot(dru_ref[...], wru_ref[...], "nt")
        for hh in range(HEADS):
            sl = slice(hh * HEAD_DIM, (hh + 1) * HEAD_DIM)
            o_v, g, dr = o_ref[:, sl], g_ref[:, sl].astype(F32), dret[:, sl]
            sg = _sigmoid(g)
            r = lax.rsqrt(jnp.mean(o_v * o_v, axis=-1, keepdims=True) + NORM_EPS)
            rn = o_v * r
            d_rn = dr * (g * sg)
            dg_ref[:, sl] = (dr * rn * (sg * (1.0 + g * (1.0 - sg)))).astype(BF16)
            d_o = r * (d_rn - rn * jnp.mean(d_rn * rn, axis=-1, keepdims=True))
            dob = d_o.astype(BF16)

            q, k, v = _rotate(q_ref, hh, cs, sn), _rotate(k_ref, hh, cs, sn, K_SCALE), v_ref[:, sl].astype(BF16)
            qb, kb = q.astype(BF16), k.astype(BF16)
            qdv, kdv = qd_ref[hh], kd_ref[hh]
            s_t = (_dot(kb, qb, "nt") * dt_ref[hh]).astype(BF16)
            p_t = (_dot(v, dob, "nt") * dt_ref[hh]).astype(BF16)
            p = (_dot(dob, v, "nt") * d_ref[hh]).astype(BF16)
            stb = st_ref[hh]
            gb = gstate[hh].astype(BF16)
            _unrotate_into(dq_ref, hh, _dot(p, kb) + _dot(dob, stb, "nt") * qdv, cs, sn)
            _unrotate_into(dk_ref, hh, _dot(p_t, qb) + _dot(v, gb, "nt") * kdv, cs, sn, K_SCALE)
            dv_ref[:, sl] = (_dot(s_t, dob) + _dot((k * kdv).astype(BF16), gb)).astype(BF16)
            gstate[hh] = gstate[hh] * cd_ref[hh] + _dot((q * qdv).astype(BF16), dob, "tn")

    return pl.pallas_call(
        body, name=name, grid=(nb,),
        in_specs=[tok, pl.BlockSpec((D_MODEL, D_MODEL), lambda n: (0, 0), pipeline_mode=pl.Buffered(1)), tok] + blk
                 + [rope, rope, st, tab, tab, col, col, rowv],
        out_specs=[tok, tok, tok, tok],
        out_shape=[jax.ShapeDtypeStruct((t, D_MODEL), BF16)] * 4,
        scratch_shapes=[pltpu.VMEM((HEADS, HEAD_DIM, HEAD_DIM), F32)],
        compiler_params=_cparams(("arbitrary",)),
    )(dru, w_ru, o, proj, proj, proj, proj, cos, sin, states, dmat, dmat_t, qd, kd, cd)


POOL_TILE = 256


def _pool_tables():
    b = POOL_TILE
    tt = np.arange(b)[:, None]
    jj = np.arange(b)[None, :]
    cur, prev = [], []
    for w in POOL_WINDOWS:
        cur.append(((tt - jj >= 0) & (tt - jj <= w - 1)).astype(np.float32))
        prev.append((tt - (jj - b) <= w - 1).astype(np.float32))
    cur, prev = np.stack(cur), np.stack(prev)
    as16 = lambda v: jnp.asarray(v, dtype=BF16)
    return as16(cur), as16(prev), as16(np.swapaxes(cur, 1, 2)), as16(np.swapaxes(prev, 1, 2))


def _split2(x):
    hi = x.astype(BF16)
    return hi, (x - hi.astype(F32)).astype(BF16)


POOL_BLOCK = 4


def _pool_count(n, window):
    tpos = n * POOL_TILE + lax.broadcasted_iota(jnp.int32, (POOL_TILE, 1), 0)
    return jnp.minimum(tpos + 1, window).astype(F32)


def _pool_fwd(name, proj, pool_w, scale, tables):
    t = proj.shape[0]
    nb = t // POOL_TILE
    mc, mp, _, _ = tables
    tab = _full_spec((GROUPS, POOL_TILE, POOL_TILE))
    row = _row_spec(POOL_TILE, D_MODEL)

    def body(pc_ref, pp_ref, mc_ref, mp_ref, w_ref, sc_ref, pm_ref, mix_ref, po_ref):
        n = pl.program_id(0)
        for g, window in enumerate(POOL_WINDOWS):
            sl = slice(g * GROUP_DIM, (g + 1) * GROUP_DIM)
            p = pc_ref[:, sl]
            win = _dot(mc_ref[g], p) + jnp.where(n > 0, _dot(mp_ref[g], pp_ref[:, sl]), 0.0)
            pm = (win / _pool_count(n, window) - p.astype(F32)).astype(BF16)
            pm_ref[:, sl] = pm
            mixed = _dot(pm, w_ref[g])
            mix_ref[:, sl] = mixed
            po_ref[:, sl] = (mixed * sc_ref[:, sl]).astype(BF16)

    return pl.pallas_call(
        body, name=name, grid=(nb,),
        in_specs=[_row_spec(POOL_TILE, D_MODEL, POOL_BLOCK),
                  pl.BlockSpec((POOL_TILE, D_MODEL), lambda n: (jnp.maximum(n - 1, 0), POOL_BLOCK)),
                  tab, tab, _full_spec((GROUPS, GROUP_DIM, GROUP_DIM)), _full_spec((1, D_MODEL))],
        out_specs=[row] * 3,
        out_shape=[jax.ShapeDtypeStruct((t, D_MODEL), BF16), jax.ShapeDtypeStruct((t, D_MODEL), F32),
                   jax.ShapeDtypeStruct((t, D_MODEL), BF16)],
        compiler_params=_cparams(("parallel",)),
    )(proj, proj, mc, mp, pool_w, scale)


def _pool_bwd(name, dpu, w_pu, pm, mixed, pool_w, scale, tables):
    t = dpu.shape[0]
    nb = t // POOL_TILE
    _, _, mct, mpt = tables
    cur = pl.BlockSpec((POOL_TILE, D_MODEL), lambda n: (nb - 1 - n, 0))
    tab = _full_spec((GROUPS, POOL_TILE, POOL_TILE))
    wspec = _full_spec((GROUPS, GROUP_DIM, GROUP_DIM))
    sspec = _full_spec((1, D_MODEL))

    def body(dpu_ref, wpu_ref, pm_ref, mix_ref, mct_ref, mpt_ref, w_ref, sc_ref, dp_ref, dw_ref, ds_ref, later):
        n = pl.program_id(0)

        @pl.when(n == 0)
        def _():
            dw_ref[...] = jnp.zeros(dw_ref.shape, F32)
            ds_ref[...] = jnp.zeros(ds_ref.shape, F32)
            later[...] = jnp.zeros(later.shape, F32)

        dpo = _dot(dpu_ref[...], wpu_ref[...], "nt")
        for g, window in enumerate(POOL_WINDOWS):
            sl = slice(g * GROUP_DIM, (g + 1) * GROUP_DIM)
            dc, sc = dpo[:, sl], sc_ref[:, sl]
            dmix = (dc * sc).astype(BF16)
            dpm = _dot(dmix, w_ref[g], "nt")
            e = dpm / _pool_count(nb - 1 - n, window)
            e_hi, e_lo = _split2(e)
            f_hi, f_lo = _split2(later[g])
            mctv, mptv = mct_ref[g], mpt_ref[g]
            back = _dot(mctv, e_hi) + _dot(mctv, e_lo)
            after = _dot(mptv, f_hi) + _dot(mptv, f_lo)
            dp_ref[:, sl] = (back + after - dpm).astype(BF16)
            later[g] = e
            dw_ref[g] += _dot(pm_ref[:, sl], dmix, "tn")
            ds_ref[:, sl] += jnp.sum(dc * mix_ref[:, sl], axis=0, keepdims=True)

    return pl.pallas_call(
        body, name=name, grid=(nb,),
        in_specs=[cur, pl.BlockSpec((D_MODEL, D_MODEL), lambda n: (0, 0), pipeline_mode=pl.Buffered(1)), cur, cur, tab, tab,
                  wspec, sspec],
        out_specs=[cur, wspec, sspec],
        out_shape=[jax.ShapeDtypeStruct((t, D_MODEL), BF16), jax.ShapeDtypeStruct((GROUPS, GROUP_DIM, GROUP_DIM), F32),
                   jax.ShapeDtypeStruct((1, D_MODEL), F32)],
        scratch_shapes=[pltpu.VMEM((GROUPS, POOL_TILE, GROUP_DIM), F32)],
        compiler_params=_cparams(("arbitrary",)),
    )(dpu, w_pu, pm, mixed, mct, mpt, pool_w, scale)


GATE0_BLOCK, GATE1_BLOCK = 5, 6


def _merge_fwd(name, ret, po, w_ru, w_pu, w_out, proj, bias, h, next_g, tm=512):
    t = ret.shape[0]

    def body(r_ref, p_ref, wr_ref, wp_ref, wo_ref, g0_ref, g1_ref, b_ref, h_ref, ng_ref, m_ref, ru_ref, pu_ref, ho_ref, n_ref):
        ru = _dot(r_ref[...], wr_ref[...])
        pu = _dot(p_ref[...], wp_ref[...])
        ru_ref[...] = ru
        pu_ref[...] = pu
        merged = (_sigmoid(g0_ref[...].astype(F32) + b_ref[0:1, :]) * ru
                  + _sigmoid(g1_ref[...].astype(F32) + b_ref[1:2, :]) * pu).astype(BF16)
        m_ref[...] = merged
        h_new = h_ref[...] + _dot(merged, wo_ref[...])
        ho_ref[...] = h_new
        n_ref[...] = _normed(h_new, ng_ref[...]).astype(BF16)

    row = _row_spec(tm, D_MODEL)
    wspec = pl.BlockSpec((D_MODEL, D_MODEL), lambda i: (0, 0), pipeline_mode=pl.Buffered(1))
    return pl.pallas_call(
        body, name=name, grid=(t // tm,),
        in_specs=[row, row, wspec, wspec, wspec, _row_spec(tm, D_MODEL, GATE0_BLOCK), _row_spec(tm, D_MODEL, GATE1_BLOCK),
                  _full_spec((2, D_MODEL)), row, _full_spec((1, D_MODEL))],
        out_specs=[row] * 5,
        out_shape=[jax.ShapeDtypeStruct((t, D_MODEL), BF16), jax.ShapeDtypeStruct((t, D_MODEL), F32),
                   jax.ShapeDtypeStruct((t, D_MODEL), F32), jax.ShapeDtypeStruct((t, D_MODEL), F32),
                   jax.ShapeDtypeStruct((t, D_MODEL), BF16)],
        compiler_params=_cparams(("parallel",)),
    )(ret, po, w_ru, w_pu, w_out, proj, proj, bias, h, next_g)


def _merge_bwd(name, dh_b, w_out, ru, pu, proj, bias, tm=512):
    t = dh_b.shape[0]

    def body(dh_ref, wo_ref, ru_ref, pu_ref, g0_ref, g1_ref, b_ref, dru_ref, dpu_ref, dg0_ref, dg1_ref, db_ref):
        i = pl.program_id(0)
        d = _dot(dh_ref[...], wo_ref[...], "nt")
        s0 = _sigmoid(g0_ref[...].astype(F32) + b_ref[0:1, :])
        s1 = _sigmoid(g1_ref[...].astype(F32) + b_ref[1:2, :])
        dru_ref[...] = (d * s0).astype(BF16)
        dpu_ref[...] = (d * s1).astype(BF16)
        dg0 = d * ru_ref[...] * (s0 * (1.0 - s0))
        dg1 = d * pu_ref[...] * (s1 * (1.0 - s1))
        dg0_ref[...] = dg0.astype(BF16)
        dg1_ref[...] = dg1.astype(BF16)
        part0 = jnp.sum(dg0, axis=0, keepdims=True)
        part1 = jnp.sum(dg1, axis=0, keepdims=True)

        @pl.when(i == 0)
        def _():
            db_ref[0:1, :] = part0
            db_ref[1:2, :] = part1

        @pl.when(i > 0)
        def _():
            db_ref[0:1, :] += part0
            db_ref[1:2, :] += part1

    row = _row_spec(tm, D_MODEL)
    return pl.pallas_call(
        body, name=name, grid=(t // tm,),
        in_specs=[row, pl.BlockSpec((D_MODEL, D_MODEL), lambda i: (0, 0), pipeline_mode=pl.Buffered(1)), row, row,
                  _row_spec(tm, D_MODEL, GATE0_BLOCK), _row_spec(tm, D_MODEL, GATE1_BLOCK), _full_spec((2, D_MODEL))],
        out_specs=[row, row, row, row, _full_spec((2, D_MODEL))],
        out_shape=[jax.ShapeDtypeStruct((t, D_MODEL), BF16)] * 4 + [jax.ShapeDtypeStruct((2, D_MODEL), F32)],
        compiler_params=_cparams(("arbitrary",)),
    )(dh_b, w_out, ru, pu, proj, proj, bias)


def _half_scale(acc):
    return (FFN_RES_WEIGHT * acc,)


def _normed(h, g):
    return h * lax.rsqrt(jnp.mean(h * h, axis=-1, keepdims=True) + NORM_EPS) * g


def _residual_half_norm(acc, res, g):
    h = res + FFN_RES_WEIGHT * acc
    return h, _normed(h, g)


FF_TILE = D_FF // 2
DW_TILE = 256
SAVED_FF_DTYPE = BF16
FF_CHUNKS = ((0, 512), (512, 1024), (1024, 1280), (1280, FF_TILE))


def _ffn_in(name, nrm, w_in, tm=512):
    t = nrm.shape[0]
    nj = D_FF // FF_TILE

    def body(n_ref, wg_ref, wu_ref, a_ref, mid_ref):
        nv = n_ref[...]
        for c0, c1 in FF_CHUNKS:
            gate = _dot(nv, wg_ref[:, c0:c1])
            up = _dot(nv, wu_ref[:, c0:c1])
            s = _sigmoid(gate)
            silu = gate * s
            a_ref[0, :, c0:c1] = (FFN_RES_WEIGHT * up * (s * (1.0 + gate * (1.0 - s)))).astype(a_ref.dtype)
            a_ref[1, :, c0:c1] = (FFN_RES_WEIGHT * silu).astype(a_ref.dtype)
            mid_ref[:, c0:c1] = (silu * up).astype(BF16)

    return pl.pallas_call(
        body, name=name, grid=(nj, t // tm),
        in_specs=[pl.BlockSpec((tm, D_MODEL), lambda j, i: (i, 0)),
                  pl.BlockSpec((D_MODEL, FF_TILE), lambda j, i: (0, j)),
                  pl.BlockSpec((D_MODEL, FF_TILE), lambda j, i: (0, j + nj))],
        out_specs=[pl.BlockSpec((2, tm, FF_TILE), lambda j, i: (0, i, j)), pl.BlockSpec((tm, FF_TILE), lambda j, i: (i, j))],
        out_shape=[jax.ShapeDtypeStruct((2, t, D_FF), SAVED_FF_DTYPE), jax.ShapeDtypeStruct((t, D_FF), BF16)],
        compiler_params=_cparams(("parallel", "parallel")),
    )(nrm, w_in, w_in)


def _ffn_dact(name, dout_b, w_out, a, tm=512):
    t = dout_b.shape[0]

    def body(d_ref, w_ref, a_ref, da_ref):
        dv = d_ref[...]
        for c0, c1 in FF_CHUNKS:
            dm = _dot(dv, w_ref[c0:c1, :], "nt")
            da_ref[0, :, c0:c1] = (dm * a_ref[0, :, c0:c1].astype(F32)).astype(BF16)
            da_ref[1, :, c0:c1] = (dm * a_ref[1, :, c0:c1].astype(F32)).astype(BF16)

    blk = pl.BlockSpec((2, tm, FF_TILE), lambda j, i: (0, i, j))
    return pl.pallas_call(
        body, name=name, grid=(D_FF // FF_TILE, t // tm),
        in_specs=[pl.BlockSpec((tm, D_MODEL), lambda j, i: (i, 0)), pl.BlockSpec((FF_TILE, D_MODEL), lambda j, i: (j, 0)), blk],
        out_specs=blk,
        out_shape=jax.ShapeDtypeStruct((2, t, D_FF), BF16),
        compiler_params=_cparams(("parallel", "parallel")),
    )(dout_b, w_out, a)


def _ffn_fwd(tag, h, nrm, get_w_in, get_w_out, finish):
    t = h.shape[0]
    w_in = get_w_in(nrm)
    a, mid = _ffn_in(f"{tag}_in", nrm, w_in, tm=min(512, t))
    w_out = get_w_out(mid)
    return finish(mid, w_out), (nrm, a, mid, w_in, w_out)


def _ffn_bwd(tag, h, g, saved, dout, dout_b, on_grads, flush):
    t = h.shape[0]
    nrm, a, mid, w_in, w_out = saved
    d_w_out = _matmul(f"{tag}_dwout", mid, dout_b, "tn", D_FF, D_MODEL, t, DW_TILE, D_MODEL, t, [BF16], epilogue=_half_scale,
                      resident="b")
    da = _ffn_dact(f"{tag}_dact", dout_b, w_out, a, tm=min(512, t))
    nj = D_FF // DW_TILE
    d_w_in = _dw_resident(f"{tag}_dwin", nrm, [da], [pl.BlockSpec((None, t, DW_TILE), lambda s: (s // nj, 0, s % nj))],
                          2 * nj, None, DW_TILE)
    tie = on_grads({f"{tag}_w_in": d_w_in, f"{tag}_w_out": d_w_out})
    tm = min(512, t)
    dh, dh_b, dg = _proj_norm_bwd(f"{tag}_dn", [da], [pl.BlockSpec((2, tm, D_FF), lambda i: (0, i, 0))],
                                  ((0, 0, 0, D_FF), (0, 1, D_FF, 2 * D_FF)), w_in, h, g if tie is None else g + tie, dout, tm)
    return dh, dh_b, dg, flush(dh)


def _dw_resident(name, u, pieces, piece_specs, n_tiles, which_piece, tn):
    t = u.shape[0]
    npc = len(pieces)

    def body(*refs):
        u_ref, p_refs, o_ref, ut_ref = refs[0], refs[1:1 + npc], refs[1 + npc], refs[2 + npc]
        s = pl.program_id(0)

        @pl.when(s == 0)
        def _():
            ut_ref[...] = u_ref[...].T

        if npc == 1:
            o_ref[...] = _dot(ut_ref[...], p_refs[0][...]).astype(BF16)
        for which in range(npc if npc > 1 else 0):
            @pl.when(which_piece(s) == which)
            def _(which=which):
                o_ref[...] = _dot(ut_ref[...], p_refs[which][...]).astype(BF16)

    return pl.pallas_call(
        body, name=name, grid=(n_tiles,),
        in_specs=[pl.BlockSpec((t, D_MODEL), lambda s: (0, 0), pipeline_mode=pl.Buffered(1))] + list(piece_specs),
        out_specs=pl.BlockSpec((D_MODEL, tn), lambda s: (0, s)),
        out_shape=jax.ShapeDtypeStruct((D_MODEL, n_tiles * tn), BF16),
        scratch_shapes=[pltpu.VMEM((D_MODEL, t), BF16)],
        compiler_params=_cparams(("arbitrary",)),
    )(u, *pieces)


def _mix_dwin(name, u, pieces, tn=256):
    t = u.shape[0]
    nj = D_MODEL // tn
    specs = [pl.BlockSpec((t, tn), lambda s, k=k: (0, jnp.clip(s - k * nj, 0, nj - 1))) for k in range(len(pieces))]
    return _dw_resident(name, u, pieces, specs, len(pieces) * nj, lambda s: s // nj, tn)


def _local_step(x, target, vec, get_w, on_grads, flush):
    t = x.shape[0]
    cos, sin = _rope_tables(t)
    rtab = _retention_tables()
    ptab = _pool_tables()
    w = {}

    def getter(group, name):
        def get(after):
            if name not in w:
                w.update(get_w(group, after))
            return w[name]
        return get

    nrm1 = _rmsnorm_fwd("ffn1_norm", x, vec["norm_ffn1"])
    def out_and_norm(mid, w_out):
        return _matmul("ffn1_out", mid, w_out, "nn", t, D_MODEL, D_FF, 512, D_MODEL, D_FF, [F32, BF16],
                       extras=(x,), consts=(vec["norm_mix"],), epilogue=_residual_half_norm)

    (h1, u), s1 = _ffn_fwd("ffn1", x, nrm1, getter(0, "ffn1_w_in"), getter(1, "ffn1_w_out"), out_and_norm)
    w.update(get_w(2, u))
    proj = _matmul("mix_in", u, w["w_in"], "nn", t, IN_WIDTH, D_MODEL, 1024, 1024, D_MODEL, [BF16], n_outer=True)
    o, ret, states = _retention_fwd("retention", proj, cos, sin, rtab)
    pm, mixed, po = _pool_fwd("pool", proj, w["pool_w"], vec["pool_scale"], ptab)
    merged, ru, pu, h2, nrm2 = _merge_fwd("merge", ret, po, w["w_ret_up"], w["w_pool_up"], w["w_out"], proj, vec["gate_bias"],
                                          h1, vec["norm_ffn2"], tm=min(512, t))
    def out_and_loss(mid, w_out):
        return _out_loss_and_grad("ffn2_out_loss", mid, w_out, h2, vec["norm_final"], target, tm=min(512, t))

    (dh3, dh3_b, dg_final, loss), s2 = _ffn_fwd("ffn2", h2, nrm2, getter(3, "ffn2_w_in"), getter(3, "ffn2_w_out"), out_and_loss)

    def tied(v, tie):
        return v if tie is None else v + tie

    dh2, dh2_b, dg_ffn2, tie = _ffn_bwd("ffn2", h2, vec["norm_ffn2"], s2, dh3, dh3_b, on_grads, flush)
    def square_dw(name, act, grad):
        return _matmul(name, act, grad, "tn", D_MODEL, D_MODEL, t, D_MODEL, D_MODEL, 1024, [BF16])

    d_w_out = square_dw("mix_dwout", merged, dh2_b)
    dru, dpu, dg0, dg1, d_bias = _merge_bwd("merge_bwd", dh2_b, w["w_out"], ru, pu, proj, tied(vec["gate_bias"], tie))
    d_w_ru = square_dw("mix_dwru", ret, dru)
    d_w_pu = square_dw("mix_dwpu", po, dpu)
    dp, d_pool_w, d_scale = _pool_bwd("pool_bwd", dpu, w["w_pool_up"], pm, mixed, w["pool_w"], vec["pool_scale"], ptab)
    dq, dk, dv, dgr = _retention_bwd("retention_bwd", dru, w["w_ret_up"], o, proj, cos, sin, states, rtab)
    dproj = [dq, dk, dv, dgr, dp, dg0, dg1]
    d_w_in = _mix_dwin("mix_dwin", u, dproj)
    tie = on_grads(dict(w_in=d_w_in, pool_w=d_pool_w.astype(BF16), w_ret_up=d_w_ru, w_pool_up=d_w_pu, w_out=d_w_out))
    tm = min(512, t)
    dh1, dh1_b, dg_mix = _proj_norm_bwd("mix_du", dproj, [_row_spec(tm, D_MODEL)] * len(dproj),
                                        [(k, None, k * D_MODEL, (k + 1) * D_MODEL) for k in range(len(dproj))],
                                        w["w_in"], h1, tied(vec["norm_mix"], tie), dh2, tm)
    tie = flush(dh1)
    dx, _, dg_ffn1, _ = _ffn_bwd("ffn1", x, tied(vec["norm_ffn1"], tie), s1, dh1, dh1_b, on_grads, flush)

    small = dict(norm_ffn1=dg_ffn1, norm_mix=dg_mix, gate_bias=d_bias, pool_scale=d_scale, norm_ffn2=dg_ffn2,
                 norm_final=dg_final)
    return loss[0, 0], dx, small


BIG = ("ffn1_w_in", "ffn1_w_out", "w_in", "pool_w", "w_ret_up", "w_pool_up", "w_out", "ffn2_w_in", "ffn2_w_out")
KIND = dict(ffn1_w_in="col", ffn1_w_out="row", w_in="col", pool_w="pool", w_ret_up="row", w_pool_up="row", w_out="row",
            ffn2_w_in="col", ffn2_w_out="row")
ANY = pl.BlockSpec(memory_space=pl.ANY)


def _place():
    x, y, c = lax.axis_index("x"), lax.axis_index("y"), lax.axis_index("c")
    chips = [(1 - x, y), (x, 1 - y), (1 - x, 1 - y)]
    return x, y, c, chips


def _full_view_shape(kind, local_shape):
    if kind == "col":
        return (2, local_shape[0] // 2, N_CHIPS * local_shape[1])
    if kind == "row":
        return (N_CHIPS, 2, local_shape[0] // 2, local_shape[1])
    return (GROUPS, N_CHIPS, 2, local_shape[1] // 2, local_shape[2])


def _local_view(kind, arr):
    if kind == "pool":
        return arr.reshape(GROUPS, 2, arr.shape[1] // 2, arr.shape[2])
    return arr.reshape(2, arr.shape[0] // 2, arr.shape[1])


def _blk(kind, ref, s, c):
    if kind == "col":
        cs = ref.shape[2] // N_CHIPS
        return ref.at[c, :, pl.ds(pl.multiple_of(s * cs, 128), cs)]
    if kind == "row":
        return ref.at[s, c]
    return ref.at[:, s, c]


def _half(kind, ref, c):
    return ref.at[:, c] if kind == "pool" else ref.at[c]


def _shard(kind, ref, s):
    if kind == "col":
        cs = ref.shape[2] // N_CHIPS
        return ref.at[:, :, pl.ds(pl.multiple_of(s * cs, 128), cs)]
    if kind == "row":
        return ref.at[s]
    return ref.at[:, s]


HBM = pl.BlockSpec(memory_space=pltpu.HBM)
SEM = pl.BlockSpec(memory_space=pltpu.SEMAPHORE)
EFFECT = pltpu.SideEffectType.DATAFLOW_SIDE_EFFECTING
WEIGHT_GROUPS = (("ffn1_w_in",), ("ffn1_w_out",), ("w_in", "pool_w", "w_ret_up", "w_pool_up", "w_out"), ("ffn2_w_in", "ffn2_w_out"))
GRAD_GROUPS = (("ffn2_w_in", "ffn2_w_out"), ("w_in", "pool_w", "w_ret_up", "w_pool_up", "w_out"), ("ffn1_w_in", "ffn1_w_out"))


def _hbm(a):
    return pltpu.with_memory_space_constraint(a, pltpu.HBM)


def _natural(kind, o):
    if kind == "col":
        return o.reshape(o.shape[0] * o.shape[1], o.shape[2])
    if kind == "row":
        return o.reshape(-1, o.shape[3])
    return o.reshape(GROUPS, -1, o.shape[4])


def _ici_copy(kind, loc, full, j, chips, s, c, send_sem, recv_sem):
    px, py = chips[j]
    return (pltpu.make_async_remote_copy(src_ref=_half(kind, loc, c), dst_ref=_blk(kind, full, s, c), send_sem=send_sem,
                                         recv_sem=recv_sem, device_id=(px, py, c), device_id_type=MESH),
            pltpu.make_async_remote_copy(src_ref=_half(kind, loc, c), dst_ref=_blk(kind, full, 2 * px + py, c), send_sem=send_sem,
                                         recv_sem=recv_sem, device_id=(px, py, c), device_id_type=MESH))


def _gather_start(tag, group_ids, shards):
    grps = [WEIGHT_GROUPS[g] for g in group_ids]
    names = [nm for grp in grps for nm in grp]
    kinds = [KIND[nm] for nm in names]
    n, ng = len(names), len(grps)
    locs = [_hbm(_local_view(KIND[nm], shards[nm])) for nm in names]
    lands = [_hbm(lax.empty(_full_view_shape(KIND[nm], shards[nm].shape), BF16)) for nm in names]
    first = np.cumsum([0] + [len(grp) for grp in grps])

    def body(*refs):
        loc, full = refs[:n], refs[n:2 * n]
        send_sems, recv_sems = refs[2 * n:2 * n + ng], refs[2 * n + ng:2 * n + 2 * ng]
        token = refs[-1]
        x, y, c, chips = _place()
        s = 2 * x + y
        for g in range(ng):
            for a in range(first[g], first[g + 1]):
                for j in range(3):
                    k = 3 * (a - first[g]) + j
                    _ici_copy(kinds[a], loc[a], full[a], j, chips, s, c, send_sems[g].at[k], recv_sems[g].at[k])[0].start()
        token[...] = jnp.zeros(token.shape, F32)

    sem_shapes = [pltpu.SemaphoreType.DMA((3 * len(grp),)) for grp in grps]
    outs = pl.pallas_call(
        body, name=f"gather_start_{tag}",
        in_specs=[HBM] * (2 * n),
        out_specs=[SEM] * (2 * ng) + [HBM] * (2 * n) + [pl.BlockSpec(memory_space=pltpu.VMEM)],
        out_shape=sem_shapes + sem_shapes + [pltpu.HBM(a.shape, a.dtype) for a in locs + lands] + [jax.ShapeDtypeStruct((8, 128), F32)],
        input_output_aliases={i: 2 * ng + i for i in range(2 * n)},
        compiler_params=pltpu.CompilerParams(has_side_effects=EFFECT),
    )(*locs, *lands)
    send_sems, recv_sems = outs[:ng], outs[ng:2 * ng]
    locs_t, lands_t = outs[2 * ng:2 * ng + n], outs[2 * ng + n:2 * ng + 2 * n]
    groups = {}
    for k, g in enumerate(group_ids):
        sl = slice(first[k], first[k + 1])
        groups[g] = (send_sems[k], recv_sems[k], list(locs_t[sl]), list(lands_t[sl]))
    return groups, outs[-1]


def _gather_finish(g, group, after):
    names = WEIGHT_GROUPS[g]
    kinds = [KIND[nm] for nm in names]
    m = len(names)
    send_sem, recv_sem, locs, lands = group

    def wait_body(*refs):
        loc, full = refs[:m], refs[m:2 * m]
        send_sems, recv_sems = refs[2 * m], refs[2 * m + 1]
        x, y, c, chips = _place()
        s = 2 * x + y
        for a in range(m):
            for j in range(3):
                k = 3 * a + j
                sent, landed = _ici_copy(kinds[a], loc[a], full[a], j, chips, s, c, send_sems.at[k], recv_sems.at[k])
                sent.wait_send()
                landed.wait_recv()

    outs = pl.pallas_call(
        wait_body, name=f"gather_wait_{g}",
        in_specs=[HBM] * (2 * m) + [SEM, SEM] + [ANY] * len(after), out_specs=[HBM] * (2 * m),
        out_shape=[pltpu.HBM(a.shape, a.dtype) for a in locs + lands],
        input_output_aliases={i: i for i in range(2 * m)},
        compiler_params=pltpu.CompilerParams(has_side_effects=EFFECT),
    )(*locs, *lands, send_sem, recv_sem, *after)
    locs, lands = outs[:m], outs[m:]

    def forward_body(*refs):
        loc, full = refs[:m], refs[2 * m:3 * m]
        send_sems, recv_sems = refs[3 * m:]
        x, y, c, chips = _place()
        s = 2 * x + y
        sib = (x, y, 1 - c)

        def remote(a, k, src, dst):
            return pltpu.make_async_remote_copy(src_ref=src, dst_ref=dst, send_sem=send_sems.at[4 * a + k],
                                                recv_sem=recv_sems.at[4 * a + k], device_id=sib, device_id_type=MESH)

        sends = []
        for a in range(m):
            for j, (px, py) in enumerate(chips):
                theirs = _blk(kinds[a], full[a], 2 * px + py, c)
                sends.append(remote(a, j, theirs, theirs))
            sends.append(remote(a, 3, loc[a], _shard(kinds[a], full[a], s)))
        for cp in sends:
            cp.start()
        for a in range(m):
            for j, (px, py) in enumerate(chips):
                from_sib = _blk(kinds[a], full[a], 2 * px + py, 1 - c)
                remote(a, j, from_sib, from_sib).wait_recv()
            own = _shard(kinds[a], full[a], s)
            remote(a, 3, own, own).wait_recv()
        for cp in sends:
            cp.wait_send()

    outs = pl.pallas_call(
        forward_body, name=f"gather_forward_{g}",
        in_specs=[ANY] * (2 * m), out_specs=[ANY] * m,
        out_shape=[jax.ShapeDtypeStruct(a.shape, a.dtype) for a in lands],
        input_output_aliases={m + i: i for i in range(m)},
        scratch_shapes=[pltpu.SemaphoreType.DMA((4 * m,)), pltpu.SemaphoreType.DMA((4 * m,))],
    )(*locs, *lands)
    return {nm: _natural(k, o) for nm, k, o in zip(names, kinds, outs)}


def _grad_view(kind, g):
    if kind == "col":
        return g.reshape(2, g.shape[0] // 2, g.shape[1])
    if kind == "row":
        return g.reshape(N_CHIPS, 2, g.shape[0] // (2 * N_CHIPS), g.shape[1])
    return g.reshape(GROUPS, N_CHIPS, 2, g.shape[1] // (2 * N_CHIPS), g.shape[2])


def _pair_copies(kinds, g, got, send_sems, recv_sems):
    x, y, c, _ = _place()

    def other_half(kind, ref):
        if kind == "col":
            return ref.at[1 - c]
        if kind == "row":
            return ref.at[:, 1 - c]
        return ref.at[:, :, 1 - c]

    return [pltpu.make_async_remote_copy(src_ref=other_half(kinds[a], g[a]), dst_ref=got[a], send_sem=send_sems.at[a],
                                         recv_sem=recv_sems.at[a], device_id=(x, y, 1 - c), device_id_type=MESH)
            for a in range(len(kinds))]


def _pair_exchange_start(tag, names, views):
    kinds = [KIND[nm] for nm in names]
    n = len(names)

    def got_shape(kind, v):
        if kind == "col":
            return v.shape[1:]
        if kind == "row":
            return (v.shape[0],) + v.shape[2:]
        return v.shape[:2] + v.shape[3:]

    srcs = [_hbm(views[nm]) for nm in names]
    lands = [_hbm(lax.empty(got_shape(k, views[nm]), BF16)) for nm, k in zip(names, kinds)]

    def body(*refs):
        g, got = refs[:n], refs[n:2 * n]
        for cp in _pair_copies(kinds, g, got, refs[2 * n], refs[2 * n + 1]):
            cp.start()
        refs[-1][...] = jnp.zeros(refs[-1].shape, F32)

    sem_shape = pltpu.SemaphoreType.DMA((n,))
    outs = pl.pallas_call(
        body, name=f"grad_pair_exchange_start_{tag}",
        in_specs=[HBM] * (2 * n),
        out_specs=[SEM, SEM] + [HBM] * (2 * n) + [pl.BlockSpec(memory_space=pltpu.VMEM)],
        out_shape=[sem_shape, sem_shape] + [pltpu.HBM(a.shape, a.dtype) for a in srcs + lands] + [jax.ShapeDtypeStruct((8, 128), F32)],
        input_output_aliases={i: 2 + i for i in range(2 * n)},
        compiler_params=pltpu.CompilerParams(has_side_effects=EFFECT),
    )(*srcs, *lands)
    return (outs[0], outs[1], list(outs[2:2 + n]), list(outs[2 + n:2 + 2 * n])), outs[-1]


def _pair_exchange_wait(tag, names, state, after):
    kinds = [KIND[nm] for nm in names]
    n = len(names)
    send_sem, recv_sem, srcs, lands = state

    def body(*refs):
        g, got = refs[:n], refs[n:2 * n]
        for cp in _pair_copies(kinds, g, got, refs[2 * n], refs[2 * n + 1]):
            cp.wait_send()
            cp.wait_recv()

    outs = pl.pallas_call(
        body, name=f"grad_pair_exchange_wait_{tag}",
        in_specs=[HBM] * (2 * n) + [SEM, SEM, ANY], out_specs=[HBM] * (2 * n),
        out_shape=[pltpu.HBM(a.shape, a.dtype) for a in srcs + lands],
        input_output_aliases={i: i for i in range(2 * n)},
        compiler_params=pltpu.CompilerParams(has_side_effects=EFFECT),
    )(*srcs, *lands, send_sem, recv_sem, after)
    return dict(zip(names, outs[:n])), dict(zip(names, outs[n:]))


def _pair_sum(name, kind, view, got, c_arr):
    if kind == "col":
        _, rows, cols = view.shape
        tr = 128
        grid = (rows // tr,)
        v_spec = pl.BlockSpec((None, tr, cols), lambda i, c: (c[0], i, 0))
        g_spec = pl.BlockSpec((tr, cols), lambda i, c: (i, 0))
    elif kind == "row":
        _, _, rows, cols = view.shape
        grid = (N_CHIPS,)
        v_spec = pl.BlockSpec((None, None, rows, cols), lambda i, c: (i, c[0], 0, 0))
        g_spec = pl.BlockSpec((None, rows, cols), lambda i, c: (i, 0, 0))
    else:
        _, _, _, rows, cols = view.shape
        grid = (GROUPS,)
        v_spec = pl.BlockSpec((None, N_CHIPS, None, rows, cols), lambda i, c: (i, 0, c[0], 0, 0))
        g_spec = pl.BlockSpec((None, N_CHIPS, rows, cols), lambda i, c: (i, 0, 0, 0))

    def body(c_ref, v_ref, g_ref, o_ref):
        o_ref[...] = (v_ref[...].astype(F32) + g_ref[...].astype(F32)).astype(BF16)

    return pl.pallas_call(
        body, name=name,
        grid_spec=pltpu.PrefetchScalarGridSpec(num_scalar_prefetch=1, grid=grid, in_specs=[v_spec, g_spec], out_specs=g_spec),
        out_shape=jax.ShapeDtypeStruct(got.shape, BF16),
        compiler_params=_cparams(("parallel",)),
    )(c_arr, view, got)


def _piece(kind, ref, s):
    if kind == "col":
        cs = ref.shape[1] // N_CHIPS
        return ref.at[:, pl.ds(pl.multiple_of(s * cs, 128), cs)]
    if kind == "row":
        return ref.at[s]
    return ref.at[:, s]


def _piece_shape(kind, shape):
    if kind == "col":
        return (shape[0], shape[1] // N_CHIPS)
    if kind == "row":
        return shape[1:]
    return (shape[0],) + shape[2:]


def _shard_copies(kinds, p, got, send_sems, recv_sems):
    x, y, c, chips = _place()
    return [pltpu.make_async_remote_copy(src_ref=_piece(kinds[a], p[a], 2 * px + py), dst_ref=got[a].at[j],
                                         send_sem=send_sems.at[3 * a + j], recv_sem=recv_sems.at[3 * a + j],
                                         device_id=(px, py, c), device_id_type=MESH)
            for a in range(len(kinds)) for j, (px, py) in enumerate(chips)]


def _shard_exchange_start(g, names, psums):
    kinds = [KIND[nm] for nm in names]
    n = len(names)
    srcs = [_hbm(psums[nm]) for nm in names]
    lands = [_hbm(lax.empty((3,) + _piece_shape(k, psums[nm].shape), BF16)) for nm, k in zip(names, kinds)]

    def body(*refs):
        p, got = refs[:n], refs[n:2 * n]
        send_sems, recv_sems = refs[2 * n], refs[2 * n + 1]
        token = refs[-1]
        for cp in _shard_copies(kinds, p, got, send_sems, recv_sems):
            cp.start()
        token[...] = jnp.zeros(token.shape, F32)

    sem_shape = pltpu.SemaphoreType.DMA((3 * n,))
    outs = pl.pallas_call(
        body, name=f"grad_shard_exchange_start_{g}",
        in_specs=[HBM] * (2 * n),
        out_specs=[SEM, SEM] + [HBM] * (2 * n) + [pl.BlockSpec(memory_space=pltpu.VMEM)],
        out_shape=[sem_shape, sem_shape] + [pltpu.HBM(a.shape, a.dtype) for a in srcs + lands] + [jax.ShapeDtypeStruct((8, 128), F32)],
        input_output_aliases={i: 2 + i for i in range(2 * n)},
        compiler_params=pltpu.CompilerParams(has_side_effects=EFFECT),
    )(*srcs, *lands)
    return (outs[0], outs[1], list(outs[2:2 + n]), list(outs[2 + n:2 + 2 * n])), outs[-1]


def _shard_exchange_wait(g, names, state, after):
    kinds = [KIND[nm] for nm in names]
    n = len(names)
    send_sem, recv_sem, srcs, lands = state

    def body(*refs):
        p, got = refs[:n], refs[n:2 * n]
        for cp in _shard_copies(kinds, p, got, refs[2 * n], refs[2 * n + 1]):
            cp.wait_send()
            cp.wait_recv()

    outs = pl.pallas_call(
        body, name=f"grad_shard_exchange_wait_{g}",
        in_specs=[HBM] * (2 * n) + [SEM, SEM] + [ANY] * len(after), out_specs=[HBM] * (2 * n),
        out_shape=[pltpu.HBM(a.shape, a.dtype) for a in srcs + lands],
        input_output_aliases={i: i for i in range(2 * n)},
        compiler_params=pltpu.CompilerParams(has_side_effects=EFFECT),
    )(*srcs, *lands, send_sem, recv_sem, *after)
    return dict(zip(names, outs[:n])), dict(zip(names, outs[n:]))


def _shard_sum(name, kind, psum, got, sc_arr):
    if kind == "col":
        rows, cols = psum.shape
        cs = cols // N_CHIPS
        tr = 128
        grid = (rows // tr,)
        p_spec = pl.BlockSpec((tr, cs), lambda i, sc: (i, sc[0]))
        g_spec = pl.BlockSpec((3, tr, cs), lambda i, sc: (0, i, 0))
        o_spec = pl.BlockSpec((None, tr, cs), lambda i, sc: (sc[1], i, 0))
        out_shape = (2, rows, cs)
    elif kind == "row":
        _, rows, cols = psum.shape
        grid = (1,)
        p_spec = pl.BlockSpec((None, rows, cols), lambda i, sc: (sc[0], 0, 0))
        g_spec = pl.BlockSpec((3, rows, cols), lambda i, sc: (0, 0, 0))
        o_spec = pl.BlockSpec((None, rows, cols), lambda i, sc: (sc[1], 0, 0))
        out_shape = (2, rows, cols)
    else:
        _, _, rows, cols = psum.shape
        grid = (1,)
        p_spec = pl.BlockSpec((GROUPS, None, rows, cols), lambda i, sc: (0, sc[0], 0, 0))
        g_spec = pl.BlockSpec((3, GROUPS, rows, cols), lambda i, sc: (0, 0, 0, 0))
        o_spec = pl.BlockSpec((GROUPS, None, rows, cols), lambda i, sc: (0, sc[1], 0, 0))
        out_shape = (GROUPS, 2, rows, cols)

    def body(sc_ref, p_ref, g_ref, o_ref):
        o_ref[...] = ((p_ref[...].astype(F32) + g_ref[0].astype(F32)) + g_ref[1].astype(F32)) + g_ref[2].astype(F32)

    return pl.pallas_call(
        body, name=name,
        grid_spec=pltpu.PrefetchScalarGridSpec(num_scalar_prefetch=1, grid=grid, in_specs=[p_spec, g_spec], out_specs=o_spec),
        out_shape=jax.ShapeDtypeStruct(out_shape, F32),
        compiler_params=_cparams(("parallel",)),
    )(sc_arr, psum, got)


def _half_exchange(tag, names, bufs):
    kinds = [KIND[nm] for nm in names]
    n = len(names)

    def body(*refs):
        out = refs[n:2 * n]
        send_sems, recv_sems = refs[2 * n:]
        x, y, c, _ = _place()
        sib = (x, y, 1 - c)
        cps = []
        for a in range(n):
            mine = _half(kinds[a], out[a], c)
            cp = pltpu.make_async_remote_copy(src_ref=mine, dst_ref=mine, send_sem=send_sems.at[a], recv_sem=recv_sems.at[a],
                                              device_id=sib, device_id_type=MESH)
            cp.start()
            cps.append(cp)
        for a, cp in enumerate(cps):
            cp.wait_send()
            theirs = _half(kinds[a], out[a], 1 - c)
            pltpu.make_async_remote_copy(src_ref=theirs, dst_ref=theirs, send_sem=send_sems.at[a], recv_sem=recv_sems.at[a],
                                         device_id=sib, device_id_type=MESH).wait_recv()

    outs = pl.pallas_call(
        body, name=f"grad_half_exchange_{tag}",
        in_specs=[ANY] * n, out_specs=[ANY] * n,
        out_shape=[jax.ShapeDtypeStruct(bufs[nm].shape, F32) for nm in names],
        input_output_aliases={a: a for a in range(n)},
        scratch_shapes=[pltpu.SemaphoreType.DMA((n,)), pltpu.SemaphoreType.DMA((n,))],
    )(*[bufs[nm] for nm in names])
    return dict(zip(names, outs))


N_DEV = 8
SMALL_ROWS = 8


def _all_reduce_small(name, v):
    def body(v_ref, o_ref, token, buf, send_sems, recv_sems):
        token[...] = jnp.zeros(token.shape, F32)
        x, y, c, _ = _place()
        me = 4 * x + 2 * y + c
        buf[me] = v_ref[...]
        cps = []
        for r in range(1, N_DEV):
            to = (x ^ (r >> 2), y ^ ((r >> 1) & 1), c ^ (r & 1))
            cp = pltpu.make_async_remote_copy(src_ref=v_ref, dst_ref=buf.at[me], send_sem=send_sems.at[r - 1],
                                              recv_sem=recv_sems.at[r - 1], device_id=to, device_id_type=MESH)
            cp.start()
            cps.append(cp)
        for r in range(1, N_DEV):
            pltpu.make_async_remote_copy(src_ref=v_ref, dst_ref=buf.at[me ^ r], send_sem=send_sems.at[r - 1],
                                         recv_sem=recv_sems.at[r - 1], device_id=(x, y, c), device_id_type=MESH).wait_recv()
        for cp in cps:
            cp.wait_send()
        acc = buf[0]
        for d in range(1, N_DEV):
            acc = acc + buf[d]
        o_ref[...] = acc

    vm = pl.BlockSpec(memory_space=pltpu.VMEM)
    return pl.pallas_call(
        body, name=name, in_specs=[vm], out_specs=[vm, vm],
        out_shape=[jax.ShapeDtypeStruct((SMALL_ROWS, D_MODEL), F32), jax.ShapeDtypeStruct((8, 128), F32)],
        scratch_shapes=[pltpu.VMEM((N_DEV, SMALL_ROWS, D_MODEL), F32), pltpu.SemaphoreType.DMA((N_DEV - 1,)),
                        pltpu.SemaphoreType.DMA((N_DEV - 1,))],
    )(v)


def _adamw(name, w, g, m, v):
    rows, cols = w.shape
    tr = next((c for c in (256, 176, 128, 64, 32, 8) if rows % c == 0), rows)
    spec = pl.BlockSpec((tr, cols), lambda i: (i, 0))

    def body(w_ref, g_ref, m_ref, v_ref, d_ref, mo_ref, vo_ref):
        gv = g_ref[...]
        m_new = ADAM_B1 * m_ref[...] + (1.0 - ADAM_B1) * gv
        v_new = ADAM_B2 * v_ref[...] + (1.0 - ADAM_B2) * jnp.square(gv)
        m_hat = m_new / (1.0 - ADAM_B1 ** ADAM_STEP)
        v_hat = v_new / (1.0 - ADAM_B2 ** ADAM_STEP)
        d_ref[...] = -ADAM_LR * (m_hat / (jnp.sqrt(v_hat) + ADAM_EPS) + ADAM_WD * w_ref[...])
        mo_ref[...] = m_new
        vo_ref[...] = v_new

    return pl.pallas_call(
        body, name=name, grid=(rows // tr,),
        in_specs=[spec] * 4, out_specs=[spec] * 3,
        out_shape=[jax.ShapeDtypeStruct((rows, cols), F32)] * 3,
        compiler_params=_cparams(("parallel",)),
    )(w, g, m, v)


WEIGHTS = ("norm_ffn1", "ffn1_w_in", "ffn1_w_out", "norm_mix", "w_in", "gate_bias", "pool_w", "pool_scale", "w_ret_up",
           "w_pool_up", "w_out", "norm_ffn2", "ffn2_w_in", "ffn2_w_out", "norm_final")
SMALL_ROW = dict(norm_ffn1=0, norm_mix=1, gate_bias=2, pool_scale=4, norm_ffn2=5, norm_final=6)


def _as2d(a):
    return a.reshape(-1, a.shape[-1])


def kernel(x, norm_ffn1, ffn1_w_in, ffn1_w_out, norm_mix, w_in, gate_bias, pool_w, pool_scale, w_ret_up, w_pool_up, w_out, norm_ffn2, ffn2_w_in, ffn2_w_out, norm_final, loss_target, m_norm_ffn1, m_ffn1_w_in, m_ffn1_w_out, m_norm_mix, m_w_in, m_gate_bias, m_pool_w, m_pool_scale, m_w_ret_up, m_w_pool_up, m_w_out, m_norm_ffn2, m_ffn2_w_in, m_ffn2_w_out, m_norm_final, v_norm_ffn1, v_ffn1_w_in, v_ffn1_w_out, v_norm_mix, v_w_in, v_gate_bias, v_pool_w, v_pool_scale, v_w_ret_up, v_w_pool_up, v_w_out, v_norm_ffn2, v_ffn2_w_in, v_ffn2_w_out, v_norm_final):
    wt = dict(norm_ffn1=norm_ffn1, ffn1_w_in=ffn1_w_in, ffn1_w_out=ffn1_w_out, norm_mix=norm_mix, w_in=w_in, gate_bias=gate_bias,
              pool_w=pool_w, pool_scale=pool_scale, w_ret_up=w_ret_up, w_pool_up=w_pool_up, w_out=w_out, norm_ffn2=norm_ffn2,
              ffn2_w_in=ffn2_w_in, ffn2_w_out=ffn2_w_out, norm_final=norm_final)
    mom = dict(norm_ffn1=m_norm_ffn1, ffn1_w_in=m_ffn1_w_in, ffn1_w_out=m_ffn1_w_out, norm_mix=m_norm_mix, w_in=m_w_in,
               gate_bias=m_gate_bias, pool_w=m_pool_w, pool_scale=m_pool_scale, w_ret_up=m_w_ret_up, w_pool_up=m_w_pool_up,
               w_out=m_w_out, norm_ffn2=m_norm_ffn2, ffn2_w_in=m_ffn2_w_in, ffn2_w_out=m_ffn2_w_out, norm_final=m_norm_final)
    var = dict(norm_ffn1=v_norm_ffn1, ffn1_w_in=v_ffn1_w_in, ffn1_w_out=v_ffn1_w_out, norm_mix=v_norm_mix, w_in=v_w_in,
               gate_bias=v_gate_bias, pool_w=v_pool_w, pool_scale=v_pool_scale, w_ret_up=v_w_ret_up, w_pool_up=v_w_pool_up,
               w_out=v_w_out, norm_ffn2=v_norm_ffn2, ffn2_w_in=v_ffn2_w_in, ffn2_w_out=v_ffn2_w_out, norm_final=v_norm_final)

    ax, ay, ac = lax.axis_index("x"), lax.axis_index("y"), lax.axis_index("c")
    chip = 2 * ax + ay
    c_arr = jnp.reshape(ac, (1,)).astype(jnp.int32)
    sc_arr = jnp.stack([chip, ac]).astype(jnp.int32)
    bias_cols = gate_bias.shape[-1]

    placed = lax.dynamic_update_slice(jnp.zeros((SMALL_ROWS, D_MODEL), F32), gate_bias[0], (0, chip * bias_cols))
    bias_sum, token = _all_reduce_small("gather_gate_bias", jnp.where(ac == 0, placed, 0.0))
    bias_full = bias_sum[:2]
    first = WEIGHT_GROUPS[0]
    gather_groups, token = _gather_start("first", [0], {nm: wt[nm][0].astype(BF16) + token[0, 0].astype(BF16) for nm in first})
    rest, rest_token = _gather_start("rest", [1, 2, 3],
                                     {nm: wt[nm][0].astype(BF16) + token[0, 0].astype(BF16) for nm in BIG if nm not in first})
    gather_groups.update(rest)
    vec = dict(norm_ffn1=norm_ffn1, norm_mix=norm_mix, norm_ffn2=norm_ffn2, pool_scale=pool_scale,
               norm_final=norm_final.reshape(1, D_MODEL), gate_bias=bias_full)

    def get_w(g, after):
        return _gather_finish(g, gather_groups[g], (after, rest_token) if g == 0 else (after,))

    pairs, pending = [], []

    def on_grads(gr):
        g = len(pairs)
        names = GRAD_GROUPS[g]
        assert set(names) == set(gr), (names, list(gr))
        state, token = _pair_exchange_start(g, names, {nm: _grad_view(KIND[nm], gr[nm]) for nm in names})
        pairs.append(state)
        return token[0:1, 0:1]

    def flush(after):
        g = len(pending)
        names = GRAD_GROUPS[g]
        views, from_sib = _pair_exchange_wait(g, names, pairs[g], after)
        psums = {nm: _pair_sum(f"pair_sum_{nm}", KIND[nm], views[nm], from_sib[nm], c_arr) for nm in names}
        state, token = _shard_exchange_start(g, names, psums)
        pending.append(state)
        tokens.append(token)
        return token[0:1, 0:1]

    tokens = []
    loss_local, dx, small = _local_step(x[0], loss_target[0], vec, get_w, on_grads, flush)

    packed = jnp.concatenate([small["norm_ffn1"], small["norm_mix"], small["gate_bias"], small["pool_scale"],
                              small["norm_ffn2"], small["norm_final"], jnp.broadcast_to(loss_local, (1, D_MODEL))], axis=0)
    small_sum, _ = _all_reduce_small("reduce_small_grads", packed)
    loss = small_sum[SMALL_ROWS - 1, 0]
    grads, delta, new_m, new_v = {}, {}, {}, {}

    def adamw(nm):
        shape = wt[nm].shape
        d, m2, v2 = _adamw(f"adamw_{nm}", _as2d(wt[nm]), _as2d(grads[nm]), _as2d(mom[nm]), _as2d(var[nm]))
        delta[nm], new_m[nm], new_v[nm] = d.reshape(shape), m2.reshape(shape), v2.reshape(shape)
        return d

    for nm in ("norm_ffn1", "norm_mix", "pool_scale", "norm_ffn2"):
        grads[nm] = small_sum[SMALL_ROW[nm]][None, :]
    grads["norm_final"] = small_sum[SMALL_ROW["norm_final"]]
    grads["gate_bias"] = lax.dynamic_slice(small_sum, (SMALL_ROW["gate_bias"], chip * bias_cols), (2, bias_cols))[None]
    after = (tokens[-1],)
    for g, names in enumerate(GRAD_GROUPS):
        psums, from_chips = _shard_exchange_wait(g, names, pending[g], after)
        bufs = {nm: _shard_sum(f"shard_sum_{nm}", KIND[nm], psums[nm], from_chips[nm], sc_arr) for nm in names}
        reduced = _half_exchange(g, names, bufs)
        for nm in names:
            grads[nm] = reduced[nm].reshape(wt[nm].shape)
        after = tuple(adamw(nm) for nm in names)
    for nm in WEIGHTS:
        if nm not in delta:
            adamw(nm)

    return (loss, dx[None], *[grads[nm] for nm in WEIGHTS], *[delta[nm] for nm in WEIGHTS],
            *[new_m[nm] for nm in WEIGHTS], *[new_v[nm] for nm in WEIGHTS])
```

```python
import functools

import numpy as np
import jax
import jax.numpy as jnp
from jax import lax
from jax.experimental import pallas as pl
from jax.experimental.pallas import tpu as pltpu

F32 = jnp.float32
BF16 = jnp.bfloat16
MESH = pl.DeviceIdType.MESH

D_MODEL = 1024
D_FF = 2816
HEADS = 4
HEAD_DIM = 256
GROUPS = 4
GROUP_DIM = 256
POOL_WINDOWS = (2, 4, 8, 16)
IN_WIDTH = 7 * D_MODEL
ROPE_BASE = 10000.0
NORM_EPS = 1e-6
FFN_RES_WEIGHT = 0.5
ADAM_LR, ADAM_B1, ADAM_B2, ADAM_EPS, ADAM_WD, ADAM_STEP = 0.001, 0.9, 0.999, 1e-08, 0.01, 10

N_CHIPS = 4
RET_BLOCK = 256
V7X_VMEM_LIMIT = 40 * 1024 * 1024


def _cparams(sem):
    return pltpu.CompilerParams(dimension_semantics=sem, vmem_limit_bytes=V7X_VMEM_LIMIT)


def _sigmoid(x):
    return jax.nn.sigmoid(x)


_DIMS = {"nn": (((1,), (0,)), ((), ())), "nt": (((1,), (1,)), ((), ())), "tn": (((0,), (0,)), ((), ()))}


def _matmul(name, a, b, mode, m, n, k, tm, tn, tk, out_dtypes, a_spec=None, b_spec=None, extras=(), consts=(), epilogue=None,
            resident=None, n_outer=False):
    tm, tn, tk = min(tm, m), min(tn, n), min(tk, k)
    gi, gj, gk = m // tm, n // tn, k // tk
    assert gi * tm == m and gj * tn == n and gk * tk == k, (name, m, n, k, tm, tn, tk)
    assert not (n_outer and (a_spec is not None or b_spec is not None)), name
    once = dict(pipeline_mode=pl.Buffered(1))

    def spec(shape, index, **kw):
        return pl.BlockSpec(shape, (lambda j, i, kk: index(i, j, kk)) if n_outer else index, **kw)

    if a_spec is None:
        kw = once if resident == "a" else {}
        a_spec = (spec((tk, tm), lambda i, j, kk: (kk, i), **kw) if mode == "tn"
                  else spec((tm, tk), lambda i, j, kk: (i, kk), **kw))
    if b_spec is None:
        kw = once if resident == "b" else {}
        b_spec = (spec((tn, tk), lambda i, j, kk: (j, kk), **kw) if mode == "nt"
                  else spec((tk, tn), lambda i, j, kk: (kk, j), **kw))
    n_ex, n_out = len(extras) + len(consts), len(out_dtypes)
    dims = _DIMS[mode]

    def body(a_ref, b_ref, *rest):
        ex_refs, out_refs = rest[:n_ex], rest[n_ex:n_ex + n_out]

        def finish(acc):
            outs = (acc,) if epilogue is None else epilogue(acc, *[e[...] for e in ex_refs])
            for o_ref, o in zip(out_refs, outs):
                o_ref[...] = o.astype(o_ref.dtype)

        prod = lax.dot_general(a_ref[...], b_ref[...], dims, preferred_element_type=F32)
        if gk == 1:
            finish(prod)
        else:
            acc_ref = rest[n_ex + n_out]
            kk = pl.program_id(2)

            @pl.when(kk == 0)
            def _():
                acc_ref[...] = prod

            @pl.when(kk > 0)
            def _():
                acc_ref[...] += prod

            @pl.when(kk == gk - 1)
            def _():
                finish(acc_ref[...])

    o_spec = spec((tm, tn), lambda i, j, kk: (i, j))
    outs = pl.pallas_call(
        body, name=name, grid=(gj, gi, gk) if n_outer else (gi, gj, gk),
        in_specs=[a_spec, b_spec] + [o_spec] * len(extras) + [spec((1, tn), lambda i, j, kk: (0, j))] * len(consts),
        out_specs=[o_spec] * n_out,
        out_shape=[jax.ShapeDtypeStruct((m, n), dt) for dt in out_dtypes],
        scratch_shapes=[pltpu.VMEM((tm, tn), F32)] if gk > 1 else [],
        compiler_params=_cparams(("parallel", "parallel", "arbitrary")),
    )(a, b, *extras, *consts)
    return outs[0] if n_out == 1 else outs


def _row_spec(tm, width, col_block=0):
    return pl.BlockSpec((tm, width), lambda i: (i, col_block))


def _full_spec(shape):
    return pl.BlockSpec(shape, lambda *_: (0,) * len(shape))


def _rmsnorm_fwd(name, h, g, tm=512):
    t = h.shape[0]

    def body(h_ref, g_ref, o_ref):
        x = h_ref[...]
        r = lax.rsqrt(jnp.mean(x * x, axis=-1, keepdims=True) + NORM_EPS)
        o_ref[...] = (x * r * g_ref[...]).astype(BF16)

    return pl.pallas_call(
        body, name=name, grid=(t // tm,),
        in_specs=[_row_spec(tm, D_MODEL), _full_spec((1, D_MODEL))],
        out_specs=_row_spec(tm, D_MODEL),
        out_shape=jax.ShapeDtypeStruct((t, D_MODEL), BF16),
        compiler_params=_cparams(("parallel",)),
    )(h, g)


def _proj_norm_bwd(name, a_list, a_specs, parts, w, h, g, dres, tm):
    t = h.shape[0]
    na = len(a_list)

    def body(*refs):
        a_refs = refs[:na]
        w_ref, h_ref, g_ref, dres_ref, dh_ref, dhb_ref, dg_ref = refs[na:]
        i = pl.program_id(0)
        dn_v = None
        for which, lead, k0, k1 in parts:
            a_ref = a_refs[which]
            term = _dot(a_ref[...] if lead is None else a_ref[lead], w_ref[:, k0:k1], "nt")
            dn_v = term if dn_v is None else dn_v + term
        x = h_ref[...]
        r = lax.rsqrt(jnp.mean(x * x, axis=-1, keepdims=True) + NORM_EPS)
        xh = x * r
        dxh = dn_v * g_ref[...]
        dh = dres_ref[...] + r * (dxh - xh * jnp.mean(dxh * xh, axis=-1, keepdims=True))
        dh_ref[...] = dh
        dhb_ref[...] = dh.astype(BF16)
        part = jnp.sum(dn_v * xh, axis=0, keepdims=True)

        @pl.when(i == 0)
        def _():
            dg_ref[...] = part

        @pl.when(i > 0)
        def _():
            dg_ref[...] += part

    row = _row_spec(tm, D_MODEL)
    return pl.pallas_call(
        body, name=name, grid=(t // tm,),
        in_specs=list(a_specs) + [pl.BlockSpec(w.shape, lambda i: (0, 0), pipeline_mode=pl.Buffered(1)), row,
                                  _full_spec((1, D_MODEL)), row],
        out_specs=[row, row, _full_spec((1, D_MODEL))],
        out_shape=[jax.ShapeDtypeStruct((t, D_MODEL), F32), jax.ShapeDtypeStruct((t, D_MODEL), BF16),
                   jax.ShapeDtypeStruct((1, D_MODEL), F32)],
        compiler_params=_cparams(("arbitrary",)),
    )(*a_list, w, h, g, dres)


def _out_loss_and_grad(name, mid, w_out, h, g, target, tm=512):
    t = h.shape[0]

    def body(m_ref, w_ref, h_ref, g_ref, t_ref, dh_ref, dhb_ref, dg_ref, loss_ref):
        i = pl.program_id(0)
        x = h_ref[...] + FFN_RES_WEIGHT * _dot(m_ref[...], w_ref[...])
        gv = g_ref[...]
        r = lax.rsqrt(jnp.mean(x * x, axis=-1, keepdims=True) + NORM_EPS)
        xh = x * r
        err = xh * gv - t_ref[...]
        row = jnp.mean(err * err, axis=-1, keepdims=True)
        part_loss = 0.5 * jnp.sum(row, axis=0, keepdims=True)
        dy = err * (1.0 / D_MODEL)
        dxh = dy * gv
        dh = r * (dxh - xh * jnp.mean(dxh * xh, axis=-1, keepdims=True))
        dh_ref[...] = dh
        dhb_ref[...] = dh.astype(BF16)
        part = jnp.sum(dy * xh, axis=0, keepdims=True)

        @pl.when(i == 0)
        def _():
            dg_ref[...] = part
            loss_ref[...] = jnp.zeros(loss_ref.shape, F32) + part_loss

        @pl.when(i > 0)
        def _():
            dg_ref[...] += part
            loss_ref[...] += part_loss

    return pl.pallas_call(
        body, name=name, grid=(t // tm,),
        in_specs=[_row_spec(tm, D_FF), pl.BlockSpec((D_FF, D_MODEL), lambda i: (0, 0), pipeline_mode=pl.Buffered(1)),
                  _row_spec(tm, D_MODEL), _full_spec((1, D_MODEL)), _row_spec(tm, D_MODEL)],
        out_specs=[_row_spec(tm, D_MODEL), _row_spec(tm, D_MODEL), _full_spec((1, D_MODEL)), _full_spec((8, 128))],
        out_shape=[jax.ShapeDtypeStruct((t, D_MODEL), F32), jax.ShapeDtypeStruct((t, D_MODEL), BF16),
                   jax.ShapeDtypeStruct((1, D_MODEL), F32), jax.ShapeDtypeStruct((8, 128), F32)],
        compiler_params=_cparams(("arbitrary",)),
    )(mid, w_out, h, g, target)


def _rope_tables(t):
    half = HEAD_DIM // 2
    inv_freq = np.float32(ROPE_BASE) ** (-np.arange(half, dtype=np.float32) / np.float32(half))
    ang = (np.arange(t, dtype=np.float32)[:, None] * inv_freq[None, :].astype(np.float32)).astype(np.float32)
    return jnp.asarray(np.cos(ang.astype(np.float64)).astype(np.float32)), jnp.asarray(np.sin(ang.astype(np.float64)).astype(np.float32))


ROPE_HALF = HEAD_DIM // 2
K_SCALE = HEAD_DIM ** -0.5


def _rotate(ref, hh, c, s, scale=None):
    lo, mid, hi = hh * HEAD_DIM, hh * HEAD_DIM + ROPE_HALF, (hh + 1) * HEAD_DIM
    x1, x2 = ref[:, lo:mid].astype(F32), ref[:, mid:hi].astype(F32)
    y = jnp.concatenate([x1 * c - x2 * s, x1 * s + x2 * c], axis=1)
    return y if scale is None else y * scale


def _unrotate_into(ref, hh, dy, c, s, scale=None):
    lo, mid, hi = hh * HEAD_DIM, hh * HEAD_DIM + ROPE_HALF, (hh + 1) * HEAD_DIM
    y1, y2 = dy[:, :ROPE_HALF], dy[:, ROPE_HALF:]
    d1, d2 = y1 * c + y2 * s, y2 * c - y1 * s
    if scale is not None:
        d1, d2 = d1 * scale, d2 * scale
    ref[:, lo:mid] = d1.astype(ref.dtype)
    ref[:, mid:hi] = d2.astype(ref.dtype)


def _retention_tables():
    b, chunk = RET_BLOCK, 64
    gamma = 1.0 - 2.0 ** (-5.0 - np.arange(HEADS, dtype=np.float64))
    log_g = np.log(gamma)[:, None, None]
    i = np.arange(b)[:, None]
    j = np.arange(b)[None, :]
    same = (i // chunk) == (j // chunk)
    earlier = (j // chunk) < (i // chunk)
    expo = np.where(same, np.abs(i - j), np.where(earlier, i - j, 0)).astype(np.float64)
    mask = np.where(same | earlier, 1.0, 0.0)
    dmat = np.exp(log_g * expo[None]) * mask[None]
    qd = np.exp(log_g[:, :, 0] * (np.arange(b)[None, :] + 1.0))
    kd = np.exp(log_g[:, :, 0] * (b - 1.0 - np.arange(b)[None, :]))
    cd = np.exp(log_g[:, :, 0] * b) * np.ones((1, HEAD_DIM))
    as32 = lambda v: jnp.asarray(v.astype(np.float32))
    return (as32(dmat), as32(np.swapaxes(dmat, 1, 2)), as32(qd[:, :, None]), as32(kd[:, :, None]), as32(cd[:, None, :]))


def _dot(a, b, mode="nn"):
    return lax.dot_general(a, b, _DIMS[mode], preferred_element_type=F32)


GRET_BLOCK = 3


def _head_specs(nb, rev=False):
    pos = (lambda n: nb - 1 - n) if rev else (lambda n: n)
    tok = pl.BlockSpec((RET_BLOCK, D_MODEL), lambda n: (pos(n), 0))
    blk = [pl.BlockSpec((RET_BLOCK, D_MODEL), lambda n, b=b: (pos(n), b)) for b in range(GRET_BLOCK + 1)]
    rope = pl.BlockSpec((RET_BLOCK, ROPE_HALF), lambda n: (pos(n), 0))
    tab = _full_spec((HEADS, RET_BLOCK, RET_BLOCK))
    col = _full_spec((HEADS, RET_BLOCK, 1))
    rowv = _full_spec((HEADS, 1, HEAD_DIM))
    st = pl.BlockSpec((HEADS, None, HEAD_DIM, HEAD_DIM), lambda n: (0, pos(n), 0, 0))
    return tok, blk, rope, tab, col, rowv, st


def _retention_fwd(name, proj, cos, sin, tables):
    t = proj.shape[0]
    nb = t // RET_BLOCK
    dmat, _, qd, kd, cd = tables
    tok, blk, rope, tab, col, rowv, st = _head_specs(nb)

    def body(q_ref, k_ref, v_ref, g_ref, c_ref, s_ref, d_ref, qd_ref, kd_ref, cd_ref, o_ref, ret_ref, st_ref, state):
        n = pl.program_id(0)

        @pl.when(n == 0)
        def _():
            state[...] = jnp.zeros(state.shape, F32)

        cs, sn = c_ref[...], s_ref[...]
        for hh in range(HEADS):
            sl = slice(hh * HEAD_DIM, (hh + 1) * HEAD_DIM)
            q, k, v = _rotate(q_ref, hh, cs, sn), _rotate(k_ref, hh, cs, sn, K_SCALE), v_ref[:, sl].astype(BF16)
            s = _dot(q.astype(BF16), k.astype(BF16), "nt") * d_ref[hh]
            stb = state[hh].astype(BF16)
            st_ref[hh] = stb
            o = _dot(s.astype(BF16), v) + _dot((q * qd_ref[hh]).astype(BF16), stb)
            o_ref[:, sl] = o
            rn = o * lax.rsqrt(jnp.mean(o * o, axis=-1, keepdims=True) + NORM_EPS)
            g = g_ref[:, sl].astype(F32)
            ret_ref[:, sl] = (rn * (g * _sigmoid(g))).astype(BF16)
            state[hh] = state[hh] * cd_ref[hh] + _dot((k * kd_ref[hh]).astype(BF16), v, "tn")

    return pl.pallas_call(
        body, name=name, grid=(nb,),
        in_specs=blk + [rope, rope, tab, col, col, rowv],
        out_specs=[tok, tok, st],
        out_shape=[jax.ShapeDtypeStruct((t, D_MODEL), F32), jax.ShapeDtypeStruct((t, D_MODEL), BF16),
                   jax.ShapeDtypeStruct((HEADS, nb, HEAD_DIM, HEAD_DIM), BF16)],
        scratch_shapes=[pltpu.VMEM((HEADS, HEAD_DIM, HEAD_DIM), F32)],
        compiler_params=_cparams(("arbitrary",)),
    )(proj, proj, proj, proj, cos, sin, dmat, qd, kd, cd)


def _retention_bwd(name, dru, w_ru, o, proj, cos, sin, states, tables):
    t = proj.shape[0]
    nb = t // RET_BLOCK
    dmat, dmat_t, qd, kd, cd = tables
    tok, blk, rope, tab, col, rowv, st = _head_specs(nb, rev=True)

    def body(dru_ref, wru_ref, o_ref, q_ref, k_ref, v_ref, g_ref, c_ref, s_ref, st_ref, d_ref, dt_ref, qd_ref, kd_ref, cd_ref,
             dq_ref, dk_ref, dv_ref, dg_ref, gstate):
        n = pl.program_id(0)

        @pl.when(n == 0)
        def _():
            gstate[...] = jnp.zeros(gstate.shape, F32)

        cs, sn = c_ref[...], s_ref[...]
        dret = _dot(dru_ref[...], wru_ref[...], "nt")
        for hh in range(HEADS):
            sl = slice(hh * HEAD_DIM, (hh + 1) * HEAD_DIM)
            o_v, g, dr = o_ref[:, sl], g_ref[:, sl].astype(F32), dret[:, sl]
            sg = _sigmoid(g)
            r = lax.rsqrt(jnp.mean(o_v * o_v, axis=-1, keepdims=True) + NORM_EPS)
            rn = o_v * r
            d_rn = dr * (g * sg)
            dg_ref[:, sl] = (dr * rn * (sg * (1.0 + g * (1.0 - sg)))).astype(BF16)
            d_o = r * (d_rn - rn * jnp.mean(d_rn * rn, axis=-1, keepdims=True))
            dob = d_o.astype(BF16)

            q, k, v = _rotate(q_ref, hh, cs, sn), _rotate(k_ref, hh, cs, sn, K_SCALE), v_ref[:, sl].astype(BF16)
            qb, kb = q.astype(BF16), k.astype(BF16)
            qdv, kdv = qd_ref[hh], kd_ref[hh]
            s_t = (_dot(kb, qb, "nt") * dt_ref[hh]).astype(BF16)
            p_t = (_dot(v, dob, "nt") * dt_ref[hh]).astype(BF16)
            p = (_dot(dob, v, "nt") * d_ref[hh]).astype(BF16)
            stb = st_ref[hh]
            gb = gstate[hh].astype(BF16)
            _unrotate_into(dq_ref, hh, _dot(p, kb) + _dot(dob, stb, "nt") * qdv, cs, sn)
            _unrotate_into(dk_ref, hh, _dot(p_t, qb) + _dot(v, gb, "nt") * kdv, cs, sn, K_SCALE)
            dv_ref[:, sl] = (_dot(s_t, dob) + _dot((k * kdv).astype(BF16), gb)).astype(BF16)
            gstate[hh] = gstate[hh] * cd_ref[hh] + _dot((q * qdv).astype(BF16), dob, "tn")

    return pl.pallas_call(
        body, name=name, grid=(nb,),
        in_specs=[tok, pl.BlockSpec((D_MODEL, D_MODEL), lambda n: (0, 0), pipeline_mode=pl.Buffered(1)), tok] + blk
                 + [rope, rope, st, tab, tab, col, col, rowv],
        out_specs=[tok, tok, tok, tok],
        out_shape=[jax.ShapeDtypeStruct((t, D_MODEL), BF16)] * 4,
        scratch_shapes=[pltpu.VMEM((HEADS, HEAD_DIM, HEAD_DIM), F32)],
        compiler_params=_cparams(("arbitrary",)),
    )(dru, w_ru, o, proj, proj, proj, proj, cos, sin, states, dmat, dmat_t, qd, kd, cd)


POOL_TILE = 256


def _pool_tables():
    b = POOL_TILE
    tt = np.arange(b)[:, None]
    jj = np.arange(b)[None, :]
    cur, prev = [], []
    for w in POOL_WINDOWS:
        cur.append(((tt - jj >= 0) & (tt - jj <= w - 1)).astype(np.float32))
        prev.append((tt - (jj - b) <= w - 1).astype(np.float32))
    cur, prev = np.stack(cur), np.stack(prev)
    as16 = lambda v: jnp.asarray(v, dtype=BF16)
    return as16(cur), as16(prev), as16(np.swapaxes(cur, 1, 2)), as16(np.swapaxes(prev, 1, 2))


def _split2(x):
    hi = x.astype(BF16)
    return hi, (x - hi.astype(F32)).astype(BF16)


POOL_BLOCK = 4


def _pool_count(n, window):
    tpos = n * POOL_TILE + lax.broadcasted_iota(jnp.int32, (POOL_TILE, 1), 0)
    return jnp.minimum(tpos + 1, window).astype(F32)


def _pool_fwd(name, proj, pool_w, scale, tables):
    t = proj.shape[0]
    nb = t // POOL_TILE
    mc, mp, _, _ = tables
    tab = _full_spec((GROUPS, POOL_TILE, POOL_TILE))
    row = _row_spec(POOL_TILE, D_MODEL)

    def body(pc_ref, pp_ref, mc_ref, mp_ref, w_ref, sc_ref, pm_ref, mix_ref, po_ref):
        n = pl.program_id(0)
        for g, window in enumerate(POOL_WINDOWS):
            sl = slice(g * GROUP_DIM, (g + 1) * GROUP_DIM)
            p = pc_ref[:, sl]
            win = _dot(mc_ref[g], p) + jnp.where(n > 0, _dot(mp_ref[g], pp_ref[:, sl]), 0.0)
            pm = (win / _pool_count(n, window) - p.astype(F32)).astype(BF16)
            pm_ref[:, sl] = pm
            mixed = _dot(pm, w_ref[g])
            mix_ref[:, sl] = mixed
            po_ref[:, sl] = (mixed * sc_ref[:, sl]).astype(BF16)

    return pl.pallas_call(
        body, name=name, grid=(nb,),
        in_specs=[_row_spec(POOL_TILE, D_MODEL, POOL_BLOCK),
                  pl.BlockSpec((POOL_TILE, D_MODEL), lambda n: (jnp.maximum(n - 1, 0), POOL_BLOCK)),
                  tab, tab, _full_spec((GROUPS, GROUP_DIM, GROUP_DIM)), _full_spec((1, D_MODEL))],
        out_specs=[row] * 3,
        out_shape=[jax.ShapeDtypeStruct((t, D_MODEL), BF16), jax.ShapeDtypeStruct((t, D_MODEL), F32),
                   jax.ShapeDtypeStruct((t, D_MODEL), BF16)],
        compiler_params=_cparams(("parallel",)),
    )(proj, proj, mc, mp, pool_w, scale)


def _pool_bwd(name, dpu, w_pu, pm, mixed, pool_w, scale, tables):
    t = dpu.shape[0]
    nb = t // POOL_TILE
    _, _, mct, mpt = tables
    cur = pl.BlockSpec((POOL_TILE, D_MODEL), lambda n: (nb - 1 - n, 0))
    tab = _full_spec((GROUPS, POOL_TILE, POOL_TILE))
    wspec = _full_spec((GROUPS, GROUP_DIM, GROUP_DIM))
    sspec = _full_spec((1, D_MODEL))

    def body(dpu_ref, wpu_ref, pm_ref, mix_ref, mct_ref, mpt_ref, w_ref, sc_ref, dp_ref, dw_ref, ds_ref, later):
        n = pl.program_id(0)

        @pl.when(n == 0)
        def _():
            dw_ref[...] = jnp.zeros(dw_ref.shape, F32)
            ds_ref[...] = jnp.zeros(ds_ref.shape, F32)
            later[...] = jnp.zeros(later.shape, F32)

        dpo = _dot(dpu_ref[...], wpu_ref[...], "nt")
        for g, window in enumerate(POOL_WINDOWS):
            sl = slice(g * GROUP_DIM, (g + 1) * GROUP_DIM)
            dc, sc = dpo[:, sl], sc_ref[:, sl]
            dmix = (dc * sc).astype(BF16)
            dpm = _dot(dmix, w_ref[g], "nt")
            e = dpm / _pool_count(nb - 1 - n, window)
            e_hi, e_lo = _split2(e)
            f_hi, f_lo = _split2(later[g])
            mctv, mptv = mct_ref[g], mpt_ref[g]
            back = _dot(mctv, e_hi) + _dot(mctv, e_lo)
            after = _dot(mptv, f_hi) + _dot(mptv, f_lo)
            dp_ref[:, sl] = (back + after - dpm).astype(BF16)
            later[g] = e
            dw_ref[g] += _dot(pm_ref[:, sl], dmix, "tn")
            ds_ref[:, sl] += jnp.sum(dc * mix_ref[:, sl], axis=0, keepdims=True)

    return pl.pallas_call(
        body, name=name, grid=(nb,),
        in_specs=[cur, pl.BlockSpec((D_MODEL, D_MODEL), lambda n: (0, 0), pipeline_mode=pl.Buffered(1)), cur, cur, tab, tab,
                  wspec, sspec],
        out_specs=[cur, wspec, sspec],
        out_shape=[jax.ShapeDtypeStruct((t, D_MODEL), BF16), jax.ShapeDtypeStruct((GROUPS, GROUP_DIM, GROUP_DIM), F32),
                   jax.ShapeDtypeStruct((1, D_MODEL), F32)],
        scratch_shapes=[pltpu.VMEM((GROUPS, POOL_TILE, GROUP_DIM), F32)],
        compiler_params=_cparams(("arbitrary",)),
    )(dpu, w_pu, pm, mixed, mct, mpt, pool_w, scale)


GATE0_BLOCK, GATE1_BLOCK = 5, 6


def _merge_fwd(name, ret, po, w_ru, w_pu, w_out, proj, bias, h, next_g, tm=512):
    t = ret.shape[0]

    def body(r_ref, p_ref, wr_ref, wp_ref, wo_ref, g0_ref, g1_ref, b_ref, h_ref, ng_ref, m_ref, ru_ref, pu_ref, ho_ref, n_ref):
        ru = _dot(r_ref[...], wr_ref[...])
        pu = _dot(p_ref[...], wp_ref[...])
        ru_ref[...] = ru
        pu_ref[...] = pu
        merged = (_sigmoid(g0_ref[...].astype(F32) + b_ref[0:1, :]) * ru
                  + _sigmoid(g1_ref[...].astype(F32) + b_ref[1:2, :]) * pu).astype(BF16)
        m_ref[...] = merged
        h_new = h_ref[...] + _dot(merged, wo_ref[...])
        ho_ref[...] = h_new
        n_ref[...] = _normed(h_new, ng_ref[...]).astype(BF16)

    row = _row_spec(tm, D_MODEL)
    wspec = pl.BlockSpec((D_MODEL, D_MODEL), lambda i: (0, 0), pipeline_mode=pl.Buffered(1))
    return pl.pallas_call(
        body, name=name, grid=(t // tm,),
        in_specs=[row, row, wspec, wspec, wspec, _row_spec(tm, D_MODEL, GATE0_BLOCK), _row_spec(tm, D_MODEL, GATE1_BLOCK),
                  _full_spec((2, D_MODEL)), row, _full_spec((1, D_MODEL))],
        out_specs=[row] * 5,
        out_shape=[jax.ShapeDtypeStruct((t, D_MODEL), BF16), jax.ShapeDtypeStruct((t, D_MODEL), F32),
                   jax.ShapeDtypeStruct((t, D_MODEL), F32), jax.ShapeDtypeStruct((t, D_MODEL), F32),
                   jax.ShapeDtypeStruct((t, D_MODEL), BF16)],
        compiler_params=_cparams(("parallel",)),
    )(ret, po, w_ru, w_pu, w_out, proj, proj, bias, h, next_g)


def _merge_bwd(name, dh_b, w_out, ru, pu, proj, bias, tm=512):
    t = dh_b.shape[0]

    def body(dh_ref, wo_ref, ru_ref, pu_ref, g0_ref, g1_ref, b_ref, dru_ref, dpu_ref, dg0_ref, dg1_ref, db_ref):
        i = pl.program_id(0)
        d = _dot(dh_ref[...], wo_ref[...], "nt")
        s0 = _sigmoid(g0_ref[...].astype(F32) + b_ref[0:1, :])
        s1 = _sigmoid(g1_ref[...].astype(F32) + b_ref[1:2, :])
        dru_ref[...] = (d * s0).astype(BF16)
        dpu_ref[...] = (d * s1).astype(BF16)
        dg0 = d * ru_ref[...] * (s0 * (1.0 - s0))
        dg1 = d * pu_ref[...] * (s1 * (1.0 - s1))
        dg0_ref[...] = dg0.astype(BF16)
        dg1_ref[...] = dg1.astype(BF16)
        part0 = jnp.sum(dg0, axis=0, keepdims=True)
        part1 = jnp.sum(dg1, axis=0, keepdims=True)

        @pl.when(i == 0)
        def _():
            db_ref[0:1, :] = part0
            db_ref[1:2, :] = part1

        @pl.when(i > 0)
        def _():
            db_ref[0:1, :] += part0
            db_ref[1:2, :] += part1

    row = _row_spec(tm, D_MODEL)
    return pl.pallas_call(
        body, name=name, grid=(t // tm,),
        in_specs=[row, pl.BlockSpec((D_MODEL, D_MODEL), lambda i: (0, 0), pipeline_mode=pl.Buffered(1)), row, row,
                  _row_spec(tm, D_MODEL, GATE0_BLOCK), _row_spec(tm, D_MODEL, GATE1_BLOCK), _full_spec((2, D_MODEL))],
        out_specs=[row, row, row, row, _full_spec((2, D_MODEL))],
        out_shape=[jax.ShapeDtypeStruct((t, D_MODEL), BF16)] * 4 + [jax.ShapeDtypeStruct((2, D_MODEL), F32)],
        compiler_params=_cparams(("arbitrary",)),
    )(dh_b, w_out, ru, pu, proj, proj, bias)


def _half_scale(acc):
    return (FFN_RES_WEIGHT * acc,)


def _normed(h, g):
    return h * lax.rsqrt(jnp.mean(h * h, axis=-1, keepdims=True) + NORM_EPS) * g


def _residual_half_norm(acc, res, g):
    h = res + FFN_RES_WEIGHT * acc
    return h, _normed(h, g)


FF_TILE = D_FF // 2
DW_TILE = 256
SAVED_FF_DTYPE = BF16
FF_CHUNKS = ((0, 512), (512, 1024), (1024, FF_TILE))


def _ffn_in(name, nrm, w_in, tm=512):
    t = nrm.shape[0]
    nj = D_FF // FF_TILE

    def body(n_ref, wg_ref, wu_ref, a_ref, mid_ref):
        nv = n_ref[...]
        for c0, c1 in FF_CHUNKS:
            gate = _dot(nv, wg_ref[:, c0:c1])
            up = _dot(nv, wu_ref[:, c0:c1])
            s = _sigmoid(gate)
            silu = gate * s
            a_ref[0, :, c0:c1] = (FFN_RES_WEIGHT * up * (s * (1.0 + gate * (1.0 - s)))).astype(a_ref.dtype)
            a_ref[1, :, c0:c1] = (FFN_RES_WEIGHT * silu).astype(a_ref.dtype)
            mid_ref[:, c0:c1] = (silu * up).astype(BF16)

    return pl.pallas_call(
        body, name=name, grid=(nj, t // tm),
        in_specs=[pl.BlockSpec((tm, D_MODEL), lambda j, i: (i, 0)),
                  pl.BlockSpec((D_MODEL, FF_TILE), lambda j, i: (0, j)),
                  pl.BlockSpec((D_MODEL, FF_TILE), lambda j, i: (0, j + nj))],
        out_specs=[pl.BlockSpec((2, tm, FF_TILE), lambda j, i: (0, i, j)), pl.BlockSpec((tm, FF_TILE), lambda j, i: (i, j))],
        out_shape=[jax.ShapeDtypeStruct((2, t, D_FF), SAVED_FF_DTYPE), jax.ShapeDtypeStruct((t, D_FF), BF16)],
        compiler_params=_cparams(("parallel", "parallel")),
    )(nrm, w_in, w_in)


def _ffn_dact(name, dout_b, w_out, a, tm=512):
    t = dout_b.shape[0]

    def body(d_ref, w_ref, a_ref, da_ref):
        dv = d_ref[...]
        for c0, c1 in FF_CHUNKS:
            dm = _dot(dv, w_ref[c0:c1, :], "nt")
            da_ref[0, :, c0:c1] = (dm * a_ref[0, :, c0:c1].astype(F32)).astype(BF16)
            da_ref[1, :, c0:c1] = (dm * a_ref[1, :, c0:c1].astype(F32)).astype(BF16)

    blk = pl.BlockSpec((2, tm, FF_TILE), lambda j, i: (0, i, j))
    return pl.pallas_call(
        body, name=name, grid=(D_FF // FF_TILE, t // tm),
        in_specs=[pl.BlockSpec((tm, D_MODEL), lambda j, i: (i, 0)), pl.BlockSpec((FF_TILE, D_MODEL), lambda j, i: (j, 0)), blk],
        out_specs=blk,
        out_shape=jax.ShapeDtypeStruct((2, t, D_FF), BF16),
        compiler_params=_cparams(("parallel", "parallel")),
    )(dout_b, w_out, a)


def _ffn_fwd(tag, h, nrm, get_w_in, get_w_out, finish):
    t = h.shape[0]
    w_in = get_w_in(nrm)
    a, mid = _ffn_in(f"{tag}_in", nrm, w_in, tm=min(512, t))
    w_out = get_w_out(mid)
    return finish(mid, w_out), (nrm, a, mid, w_in, w_out)


def _ffn_bwd(tag, h, g, saved, dout, dout_b, on_grads, flush):
    t = h.shape[0]
    nrm, a, mid, w_in, w_out = saved
    d_w_out = _matmul(f"{tag}_dwout", mid, dout_b, "tn", D_FF, D_MODEL, t, DW_TILE, D_MODEL, t, [BF16], epilogue=_half_scale,
                      resident="b")
    da = _ffn_dact(f"{tag}_dact", dout_b, w_out, a, tm=min(512, t))
    nj = D_FF // DW_TILE
    d_w_in = _dw_resident(f"{tag}_dwin", nrm, [da], [pl.BlockSpec((None, t, DW_TILE), lambda s: (s // nj, 0, s % nj))],
                          2 * nj, None, DW_TILE)
    tie = on_grads({f"{tag}_w_in": d_w_in, f"{tag}_w_out": d_w_out})
    tm = min(256, t)
    dh, dh_b, dg = _proj_norm_bwd(f"{tag}_dn", [da], [pl.BlockSpec((2, tm, D_FF), lambda i: (0, i, 0))],
                                  ((0, 0, 0, D_FF), (0, 1, D_FF, 2 * D_FF)), w_in, h, g if tie is None else g + tie, dout, tm)
    return dh, dh_b, dg, flush(dh)


def _dw_resident(name, u, pieces, piece_specs, n_tiles, which_piece, tn):
    t = u.shape[0]
    npc = len(pieces)

    def body(*refs):
        u_ref, p_refs, o_ref, ut_ref = refs[0], refs[1:1 + npc], refs[1 + npc], refs[2 + npc]
        s = pl.program_id(0)

        @pl.when(s == 0)
        def _():
            ut_ref[...] = u_ref[...].T

        if npc == 1:
            o_ref[...] = _dot(ut_ref[...], p_refs[0][...]).astype(BF16)
        for which in range(npc if npc > 1 else 0):
            @pl.when(which_piece(s) == which)
            def _(which=which):
                o_ref[...] = _dot(ut_ref[...], p_refs[which][...]).astype(BF16)

    return pl.pallas_call(
        body, name=name, grid=(n_tiles,),
        in_specs=[pl.BlockSpec((t, D_MODEL), lambda s: (0, 0), pipeline_mode=pl.Buffered(1))] + list(piece_specs),
        out_specs=pl.BlockSpec((D_MODEL, tn), lambda s: (0, s)),
        out_shape=jax.ShapeDtypeStruct((D_MODEL, n_tiles * tn), BF16),
        scratch_shapes=[pltpu.VMEM((D_MODEL, t), BF16)],
        compiler_params=_cparams(("arbitrary",)),
    )(u, *pieces)


def _mix_dwin(name, u, pieces, tn=256):
    t = u.shape[0]
    nj = D_MODEL // tn
    specs = [pl.BlockSpec((t, tn), lambda s, k=k: (0, jnp.clip(s - k * nj, 0, nj - 1))) for k in range(len(pieces))]
    return _dw_resident(name, u, pieces, specs, len(pieces) * nj, lambda s: s // nj, tn)


def _local_step(x, target, vec, get_w, on_grads, flush):
    t = x.shape[0]
    cos, sin = _rope_tables(t)
    rtab = _retention_tables()
    ptab = _pool_tables()
    w = {}

    def getter(group, name):
        def get(after):
            if name not in w:
                w.update(get_w(group, after))
            return w[name]
        return get

    nrm1 = _rmsnorm_fwd("ffn1_norm", x, vec["norm_ffn1"])
    def out_and_norm(mid, w_out):
        return _matmul("ffn1_out", mid, w_out, "nn", t, D_MODEL, D_FF, 512, D_MODEL, D_FF, [F32, BF16],
                       extras=(x,), consts=(vec["norm_mix"],), epilogue=_residual_half_norm)

    (h1, u), s1 = _ffn_fwd("ffn1", x, nrm1, getter(0, "ffn1_w_in"), getter(1, "ffn1_w_out"), out_and_norm)
    w.update(get_w(2, u))
    proj = _matmul("mix_in", u, w["w_in"], "nn", t, IN_WIDTH, D_MODEL, 1024, 1024, D_MODEL, [BF16], n_outer=True)
    o, ret, states = _retention_fwd("retention", proj, cos, sin, rtab)
    pm, mixed, po = _pool_fwd("pool", proj, w["pool_w"], vec["pool_scale"], ptab)
    merged, ru, pu, h2, nrm2 = _merge_fwd("merge", ret, po, w["w_ret_up"], w["w_pool_up"], w["w_out"], proj, vec["gate_bias"],
                                          h1, vec["norm_ffn2"], tm=min(512, t))
    def out_and_loss(mid, w_out):
        return _out_loss_and_grad("ffn2_out_loss", mid, w_out, h2, vec["norm_final"], target, tm=min(512, t))

    (dh3, dh3_b, dg_final, loss), s2 = _ffn_fwd("ffn2", h2, nrm2, getter(3, "ffn2_w_in"), getter(3, "ffn2_w_out"), out_and_loss)

    def tied(v, tie):
        return v if tie is None else v + tie

    dh2, dh2_b, dg_ffn2, tie = _ffn_bwd("ffn2", h2, vec["norm_ffn2"], s2, dh3, dh3_b, on_grads, flush)
    def square_dw(name, act, grad):
        return _matmul(name, act, grad, "tn", D_MODEL, D_MODEL, t, D_MODEL, D_MODEL, 1024, [BF16])

    d_w_out = square_dw("mix_dwout", merged, dh2_b)
    dru, dpu, dg0, dg1, d_bias = _merge_bwd("merge_bwd", dh2_b, w["w_out"], ru, pu, proj, tied(vec["gate_bias"], tie))
    d_w_ru = square_dw("mix_dwru", ret, dru)
    d_w_pu = square_dw("mix_dwpu", po, dpu)
    dp, d_pool_w, d_scale = _pool_bwd("pool_bwd", dpu, w["w_pool_up"], pm, mixed, w["pool_w"], vec["pool_scale"], ptab)
    dq, dk, dv, dgr = _retention_bwd("retention_bwd", dru, w["w_ret_up"], o, proj, cos, sin, states, rtab)
    dproj = [dq, dk, dv, dgr, dp, dg0, dg1]
    d_w_in = _mix_dwin("mix_dwin", u, dproj)
    tie = on_grads(dict(w_in=d_w_in, pool_w=d_pool_w.astype(BF16), w_ret_up=d_w_ru, w_pool_up=d_w_pu, w_out=d_w_out))
    tm = min(256, t)
    dh1, dh1_b, dg_mix = _proj_norm_bwd("mix_du", dproj, [_row_spec(tm, D_MODEL)] * len(dproj),
                                        [(k, None, k * D_MODEL, (k + 1) * D_MODEL) for k in range(len(dproj))],
                                        w["w_in"], h1, tied(vec["norm_mix"], tie), dh2, tm)
    tie = flush(dh1)
    dx, _, dg_ffn1, _ = _ffn_bwd("ffn1", x, tied(vec["norm_ffn1"], tie), s1, dh1, dh1_b, on_grads, flush)

    small = dict(norm_ffn1=dg_ffn1, norm_mix=dg_mix, gate_bias=d_bias, pool_scale=d_scale, norm_ffn2=dg_ffn2,
                 norm_final=dg_final)
    return loss[0, 0], dx, small


BIG = ("ffn1_w_in", "ffn1_w_out", "w_in", "pool_w", "w_ret_up", "w_pool_up", "w_out", "ffn2_w_in", "ffn2_w_out")
KIND = dict(ffn1_w_in="col", ffn1_w_out="row", w_in="col", pool_w="pool", w_ret_up="row", w_pool_up="row", w_out="row",
            ffn2_w_in="col", ffn2_w_out="row")
ANY = pl.BlockSpec(memory_space=pl.ANY)


def _place():
    x, y, c = lax.axis_index("x"), lax.axis_index("y"), lax.axis_index("c")
    chips = [(1 - x, y), (x, 1 - y), (1 - x, 1 - y)]
    return x, y, c, chips


def _full_view_shape(kind, local_shape):
    if kind == "col":
        return (2, local_shape[0] // 2, N_CHIPS * local_shape[1])
    if kind == "row":
        return (N_CHIPS, 2, local_shape[0] // 2, local_shape[1])
    return (GROUPS, N_CHIPS, 2, local_shape[1] // 2, local_shape[2])


def _local_view(kind, arr):
    if kind == "pool":
        return arr.reshape(GROUPS, 2, arr.shape[1] // 2, arr.shape[2])
    return arr.reshape(2, arr.shape[0] // 2, arr.shape[1])


def _blk(kind, ref, s, c):
    if kind == "col":
        cs = ref.shape[2] // N_CHIPS
        return ref.at[c, :, pl.ds(pl.multiple_of(s * cs, 128), cs)]
    if kind == "row":
        return ref.at[s, c]
    return ref.at[:, s, c]


def _half(kind, ref, c):
    return ref.at[:, c] if kind == "pool" else ref.at[c]


def _shard(kind, ref, s):
    if kind == "col":
        cs = ref.shape[2] // N_CHIPS
        return ref.at[:, :, pl.ds(pl.multiple_of(s * cs, 128), cs)]
    if kind == "row":
        return ref.at[s]
    return ref.at[:, s]


HBM = pl.BlockSpec(memory_space=pltpu.HBM)
SEM = pl.BlockSpec(memory_space=pltpu.SEMAPHORE)
EFFECT = pltpu.SideEffectType.DATAFLOW_SIDE_EFFECTING
WEIGHT_GROUPS = (("ffn1_w_in",), ("ffn1_w_out",), ("w_in", "pool_w", "w_ret_up", "w_pool_up", "w_out"), ("ffn2_w_in", "ffn2_w_out"))
GRAD_GROUPS = (("ffn2_w_in", "ffn2_w_out"), ("w_in", "pool_w", "w_ret_up", "w_pool_up", "w_out"), ("ffn1_w_in", "ffn1_w_out"))


def _hbm(a):
    return pltpu.with_memory_space_constraint(a, pltpu.HBM)


def _natural(kind, o):
    if kind == "col":
        return o.reshape(o.shape[0] * o.shape[1], o.shape[2])
    if kind == "row":
        return o.reshape(-1, o.shape[3])
    return o.reshape(GROUPS, -1, o.shape[4])


def _ici_copy(kind, loc, full, j, chips, s, c, send_sem, recv_sem):
    px, py = chips[j]
    return (pltpu.make_async_remote_copy(src_ref=_half(kind, loc, c), dst_ref=_blk(kind, full, s, c), send_sem=send_sem,
                                         recv_sem=recv_sem, device_id=(px, py, c), device_id_type=MESH),
            pltpu.make_async_remote_copy(src_ref=_half(kind, loc, c), dst_ref=_blk(kind, full, 2 * px + py, c), send_sem=send_sem,
                                         recv_sem=recv_sem, device_id=(px, py, c), device_id_type=MESH))


def _gather_start(tag, group_ids, shards):
    grps = [WEIGHT_GROUPS[g] for g in group_ids]
    names = [nm for grp in grps for nm in grp]
    kinds = [KIND[nm] for nm in names]
    n, ng = len(names), len(grps)
    locs = [_hbm(_local_view(KIND[nm], shards[nm])) for nm in names]
    lands = [_hbm(lax.empty(_full_view_shape(KIND[nm], shards[nm].shape), BF16)) for nm in names]
    first = np.cumsum([0] + [len(grp) for grp in grps])

    def body(*refs):
        loc, full = refs[:n], refs[n:2 * n]
        send_sems, recv_sems = refs[2 * n:2 * n + ng], refs[2 * n + ng:2 * n + 2 * ng]
        token = refs[-1]
        x, y, c, chips = _place()
        s = 2 * x + y
        for g in range(ng):
            for a in range(first[g], first[g + 1]):
                for j in range(3):
                    k = 3 * (a - first[g]) + j
                    _ici_copy(kinds[a], loc[a], full[a], j, chips, s, c, send_sems[g].at[k], recv_sems[g].at[k])[0].start()
        token[...] = jnp.zeros(token.shape, F32)

    sem_shapes = [pltpu.SemaphoreType.DMA((3 * len(grp),)) for grp in grps]
    outs = pl.pallas_call(
        body, name=f"gather_start_{tag}",
        in_specs=[HBM] * (2 * n),
        out_specs=[SEM] * (2 * ng) + [HBM] * (2 * n) + [pl.BlockSpec(memory_space=pltpu.VMEM)],
        out_shape=sem_shapes + sem_shapes + [pltpu.HBM(a.shape, a.dtype) for a in locs + lands] + [jax.ShapeDtypeStruct((8, 128), F32)],
        input_output_aliases={i: 2 * ng + i for i in range(2 * n)},
        compiler_params=pltpu.CompilerParams(has_side_effects=EFFECT),
    )(*locs, *lands)
    send_sems, recv_sems = outs[:ng], outs[ng:2 * ng]
    locs_t, lands_t = outs[2 * ng:2 * ng + n], outs[2 * ng + n:2 * ng + 2 * n]
    groups = {}
    for k, g in enumerate(group_ids):
        sl = slice(first[k], first[k + 1])
        groups[g] = (send_sems[k], recv_sems[k], list(locs_t[sl]), list(lands_t[sl]))
    return groups, outs[-1]


def _gather_finish(g, group, after):
    names = WEIGHT_GROUPS[g]
    kinds = [KIND[nm] for nm in names]
    m = len(names)
    send_sem, recv_sem, locs, lands = group

    def wait_body(*refs):
        loc, full = refs[:m], refs[m:2 * m]
        send_sems, recv_sems = refs[2 * m], refs[2 * m + 1]
        x, y, c, chips = _place()
        s = 2 * x + y
        for a in range(m):
            for j in range(3):
                k = 3 * a + j
                sent, landed = _ici_copy(kinds[a], loc[a], full[a], j, chips, s, c, send_sems.at[k], recv_sems.at[k])
                sent.wait_send()
                landed.wait_recv()

    outs = pl.pallas_call(
        wait_body, name=f"gather_wait_{g}",
        in_specs=[HBM] * (2 * m) + [SEM, SEM] + [ANY] * len(after), out_specs=[HBM] * (2 * m),
        out_shape=[pltpu.HBM(a.shape, a.dtype) for a in locs + lands],
        input_output_aliases={i: i for i in range(2 * m)},
        compiler_params=pltpu.CompilerParams(has_side_effects=EFFECT),
    )(*locs, *lands, send_sem, recv_sem, *after)
    locs, lands = outs[:m], outs[m:]

    def forward_body(*refs):
        loc, full = refs[:m], refs[2 * m:3 * m]
        send_sems, recv_sems = refs[3 * m:]
        x, y, c, chips = _place()
        s = 2 * x + y
        sib = (x, y, 1 - c)

        def remote(a, k, src, dst):
            return pltpu.make_async_remote_copy(src_ref=src, dst_ref=dst, send_sem=send_sems.at[4 * a + k],
                                                recv_sem=recv_sems.at[4 * a + k], device_id=sib, device_id_type=MESH)

        sends = []
        for a in range(m):
            for j, (px, py) in enumerate(chips):
                theirs = _blk(kinds[a], full[a], 2 * px + py, c)
                sends.append(remote(a, j, theirs, theirs))
            sends.append(remote(a, 3, loc[a], _shard(kinds[a], full[a], s)))
        for cp in sends:
            cp.start()
        for a in range(m):
            for j, (px, py) in enumerate(chips):
                from_sib = _blk(kinds[a], full[a], 2 * px + py, 1 - c)
                remote(a, j, from_sib, from_sib).wait_recv()
            own = _shard(kinds[a], full[a], s)
            remote(a, 3, own, own).wait_recv()
        for cp in sends:
            cp.wait_send()

    outs = pl.pallas_call(
        forward_body, name=f"gather_forward_{g}",
        in_specs=[ANY] * (2 * m), out_specs=[ANY] * m,
        out_shape=[jax.ShapeDtypeStruct(a.shape, a.dtype) for a in lands],
        input_output_aliases={m + i: i for i in range(m)},
        scratch_shapes=[pltpu.SemaphoreType.DMA((4 * m,)), pltpu.SemaphoreType.DMA((4 * m,))],
    )(*locs, *lands)
    return {nm: _natural(k, o) for nm, k, o in zip(names, kinds, outs)}


def _grad_view(kind, g):
    if kind == "col":
        return g.reshape(2, g.shape[0] // 2, g.shape[1])
    if kind == "row":
        return g.reshape(N_CHIPS, 2, g.shape[0] // (2 * N_CHIPS), g.shape[1])
    return g.reshape(GROUPS, N_CHIPS, 2, g.shape[1] // (2 * N_CHIPS), g.shape[2])


def _pair_copies(kinds, g, got, send_sems, recv_sems):
    x, y, c, _ = _place()

    def other_half(kind, ref):
        if kind == "col":
            return ref.at[1 - c]
        if kind == "row":
            return ref.at[:, 1 - c]
        return ref.at[:, :, 1 - c]

    return [pltpu.make_async_remote_copy(src_ref=other_half(kinds[a], g[a]), dst_ref=got[a], send_sem=send_sems.at[a],
                                         recv_sem=recv_sems.at[a], device_id=(x, y, 1 - c), device_id_type=MESH)
            for a in range(len(kinds))]


def _pair_exchange_start(tag, names, views):
    kinds = [KIND[nm] for nm in names]
    n = len(names)

    def got_shape(kind, v):
        if kind == "col":
            return v.shape[1:]
        if kind == "row":
            return (v.shape[0],) + v.shape[2:]
        return v.shape[:2] + v.shape[3:]

    srcs = [_hbm(views[nm]) for nm in names]
    lands = [_hbm(lax.empty(got_shape(k, views[nm]), BF16)) for nm, k in zip(names, kinds)]

    def body(*refs):
        g, got = refs[:n], refs[n:2 * n]
        for cp in _pair_copies(kinds, g, got, refs[2 * n], refs[2 * n + 1]):
            cp.start()
        refs[-1][...] = jnp.zeros(refs[-1].shape, F32)

    sem_shape = pltpu.SemaphoreType.DMA((n,))
    outs = pl.pallas_call(
        body, name=f"grad_pair_exchange_start_{tag}",
        in_specs=[HBM] * (2 * n),
        out_specs=[SEM, SEM] + [HBM] * (2 * n) + [pl.BlockSpec(memory_space=pltpu.VMEM)],
        out_shape=[sem_shape, sem_shape] + [pltpu.HBM(a.shape, a.dtype) for a in srcs + lands] + [jax.ShapeDtypeStruct((8, 128), F32)],
        input_output_aliases={i: 2 + i for i in range(2 * n)},
        compiler_params=pltpu.CompilerParams(has_side_effects=EFFECT),
    )(*srcs, *lands)
    return (outs[0], outs[1], list(outs[2:2 + n]), list(outs[2 + n:2 + 2 * n])), outs[-1]


def _pair_exchange_wait(tag, names, state, after):
    kinds = [KIND[nm] for nm in names]
    n = len(names)
    send_sem, recv_sem, srcs, lands = state

    def body(*refs):
        g, got = refs[:n], refs[n:2 * n]
        for cp in _pair_copies(kinds, g, got, refs[2 * n], refs[2 * n + 1]):
            cp.wait_send()
            cp.wait_recv()

    outs = pl.pallas_call(
        body, name=f"grad_pair_exchange_wait_{tag}",
        in_specs=[HBM] * (2 * n) + [SEM, SEM, ANY], out_specs=[HBM] * (2 * n),
        out_shape=[pltpu.HBM(a.shape, a.dtype) for a in srcs + lands],
        input_output_aliases={i: i for i in range(2 * n)},
        compiler_params=pltpu.CompilerParams(has_side_effects=EFFECT),
    )(*srcs, *lands, send_sem, recv_sem, after)
    return dict(zip(names, outs[:n])), dict(zip(names, outs[n:]))


def _pair_sum(name, kind, view, got, c_arr):
    if kind == "col":
        _, rows, cols = view.shape
        tr = 128
        grid = (rows // tr,)
        v_spec = pl.BlockSpec((None, tr, cols), lambda i, c: (c[0], i, 0))
        g_spec = pl.BlockSpec((tr, cols), lambda i, c: (i, 0))
    elif kind == "row":
        _, _, rows, cols = view.shape
        grid = (N_CHIPS,)
        v_spec = pl.BlockSpec((None, None, rows, cols), lambda i, c: (i, c[0], 0, 0))
        g_spec = pl.BlockSpec((None, rows, cols), lambda i, c: (i, 0, 0))
    else:
        _, _, _, rows, cols = view.shape
        grid = (GROUPS,)
        v_spec = pl.BlockSpec((None, N_CHIPS, None, rows, cols), lambda i, c: (i, 0, c[0], 0, 0))
        g_spec = pl.BlockSpec((None, N_CHIPS, rows, cols), lambda i, c: (i, 0, 0, 0))

    def body(c_ref, v_ref, g_ref, o_ref):
        o_ref[...] = (v_ref[...].astype(F32) + g_ref[...].astype(F32)).astype(BF16)

    return pl.pallas_call(
        body, name=name,
        grid_spec=pltpu.PrefetchScalarGridSpec(num_scalar_prefetch=1, grid=grid, in_specs=[v_spec, g_spec], out_specs=g_spec),
        out_shape=jax.ShapeDtypeStruct(got.shape, BF16),
        compiler_params=_cparams(("parallel",)),
    )(c_arr, view, got)


def _piece(kind, ref, s):
    if kind == "col":
        cs = ref.shape[1] // N_CHIPS
        return ref.at[:, pl.ds(pl.multiple_of(s * cs, 128), cs)]
    if kind == "row":
        return ref.at[s]
    return ref.at[:, s]


def _piece_shape(kind, shape):
    if kind == "col":
        return (shape[0], shape[1] // N_CHIPS)
    if kind == "row":
        return shape[1:]
    return (shape[0],) + shape[2:]


def _shard_copies(kinds, p, got, send_sems, recv_sems):
    x, y, c, chips = _place()
    return [pltpu.make_async_remote_copy(src_ref=_piece(kinds[a], p[a], 2 * px + py), dst_ref=got[a].at[j],
                                         send_sem=send_sems.at[3 * a + j], recv_sem=recv_sems.at[3 * a + j],
                                         device_id=(px, py, c), device_id_type=MESH)
            for a in range(len(kinds)) for j, (px, py) in enumerate(chips)]


def _shard_exchange_start(g, names, psums):
    kinds = [KIND[nm] for nm in names]
    n = len(names)
    srcs = [_hbm(psums[nm]) for nm in names]
    lands = [_hbm(lax.empty((3,) + _piece_shape(k, psums[nm].shape), BF16)) for nm, k in zip(names, kinds)]

    def body(*refs):
        p, got = refs[:n], refs[n:2 * n]
        send_sems, recv_sems = refs[2 * n], refs[2 * n + 1]
        token = refs[-1]
        for cp in _shard_copies(kinds, p, got, send_sems, recv_sems):
            cp.start()
        token[...] = jnp.zeros(token.shape, F32)

    sem_shape = pltpu.SemaphoreType.DMA((3 * n,))
    outs = pl.pallas_call(
        body, name=f"grad_shard_exchange_start_{g}",
        in_specs=[HBM] * (2 * n),
        out_specs=[SEM, SEM] + [HBM] * (2 * n) + [pl.BlockSpec(memory_space=pltpu.VMEM)],
        out_shape=[sem_shape, sem_shape] + [pltpu.HBM(a.shape, a.dtype) for a in srcs + lands] + [jax.ShapeDtypeStruct((8, 128), F32)],
        input_output_aliases={i: 2 + i for i in range(2 * n)},
        compiler_params=pltpu.CompilerParams(has_side_effects=EFFECT),
    )(*srcs, *lands)
    return (outs[0], outs[1], list(outs[2:2 + n]), list(outs[2 + n:2 + 2 * n])), outs[-1]


def _shard_exchange_wait(g, names, state, after):
    kinds = [KIND[nm] for nm in names]
    n = len(names)
    send_sem, recv_sem, srcs, lands = state

    def body(*refs):
        p, got = refs[:n], refs[n:2 * n]
        for cp in _shard_copies(kinds, p, got, refs[2 * n], refs[2 * n + 1]):
            cp.wait_send()
            cp.wait_recv()

    outs = pl.pallas_call(
        body, name=f"grad_shard_exchange_wait_{g}",
        in_specs=[HBM] * (2 * n) + [SEM, SEM] + [ANY] * len(after), out_specs=[HBM] * (2 * n),
        out_shape=[pltpu.HBM(a.shape, a.dtype) for a in srcs + lands],
        input_output_aliases={i: i for i in range(2 * n)},
        compiler_params=pltpu.CompilerParams(has_side_effects=EFFECT),
    )(*srcs, *lands, send_sem, recv_sem, *after)
    return dict(zip(names, outs[:n])), dict(zip(names, outs[n:]))


def _shard_sum(name, kind, psum, got, sc_arr):
    if kind == "col":
        rows, cols = psum.shape
        cs = cols // N_CHIPS
        tr = 128
        grid = (rows // tr,)
        p_spec = pl.BlockSpec((tr, cs), lambda i, sc: (i, sc[0]))
        g_spec = pl.BlockSpec((3, tr, cs), lambda i, sc: (0, i, 0))
        o_spec = pl.BlockSpec((None, tr, cs), lambda i, sc: (sc[1], i, 0))
        out_shape = (2, rows, cs)
    elif kind == "row":
        _, rows, cols = psum.shape
        grid = (1,)
        p_spec = pl.BlockSpec((None, rows, cols), lambda i, sc: (sc[0], 0, 0))
        g_spec = pl.BlockSpec((3, rows, cols), lambda i, sc: (0, 0, 0))
        o_spec = pl.BlockSpec((None, rows, cols), lambda i, sc: (sc[1], 0, 0))
        out_shape = (2, rows, cols)
    else:
        _, _, rows, cols = psum.shape
        grid = (1,)
        p_spec = pl.BlockSpec((GROUPS, None, rows, cols), lambda i, sc: (0, sc[0], 0, 0))
        g_spec = pl.BlockSpec((3, GROUPS, rows, cols), lambda i, sc: (0, 0, 0, 0))
        o_spec = pl.BlockSpec((GROUPS, None, rows, cols), lambda i, sc: (0, sc[1], 0, 0))
        out_shape = (GROUPS, 2, rows, cols)

    def body(sc_ref, p_ref, g_ref, o_ref):
        o_ref[...] = ((p_ref[...].astype(F32) + g_ref[0].astype(F32)) + g_ref[1].astype(F32)) + g_ref[2].astype(F32)

    return pl.pallas_call(
        body, name=name,
        grid_spec=pltpu.PrefetchScalarGridSpec(num_scalar_prefetch=1, grid=grid, in_specs=[p_spec, g_spec], out_specs=o_spec),
        out_shape=jax.ShapeDtypeStruct(out_shape, F32),
        compiler_params=_cparams(("parallel",)),
    )(sc_arr, psum, got)


def _half_exchange(tag, names, bufs):
    kinds = [KIND[nm] for nm in names]
    n = len(names)

    def body(*refs):
        out = refs[n:2 * n]
        send_sems, recv_sems = refs[2 * n:]
        x, y, c, _ = _place()
        sib = (x, y, 1 - c)
        cps = []
        for a in range(n):
            mine = _half(kinds[a], out[a], c)
            cp = pltpu.make_async_remote_copy(src_ref=mine, dst_ref=mine, send_sem=send_sems.at[a], recv_sem=recv_sems.at[a],
                                              device_id=sib, device_id_type=MESH)
            cp.start()
            cps.append(cp)
        for a, cp in enumerate(cps):
            cp.wait_send()
            theirs = _half(kinds[a], out[a], 1 - c)
            pltpu.make_async_remote_copy(src_ref=theirs, dst_ref=theirs, send_sem=send_sems.at[a], recv_sem=recv_sems.at[a],
                                         device_id=sib, device_id_type=MESH).wait_recv()

    outs = pl.pallas_call(
        body, name=f"grad_half_exchange_{tag}",
        in_specs=[ANY] * n, out_specs=[ANY] * n,
        out_shape=[jax.ShapeDtypeStruct(bufs[nm].shape, F32) for nm in names],
        input_output_aliases={a: a for a in range(n)},
        scratch_shapes=[pltpu.SemaphoreType.DMA((n,)), pltpu.SemaphoreType.DMA((n,))],
    )(*[bufs[nm] for nm in names])
    return dict(zip(names, outs))


N_DEV = 8
SMALL_ROWS = 8


def _all_reduce_small(name, v):
    def body(v_ref, o_ref, token, buf, send_sems, recv_sems):
        token[...] = jnp.zeros(token.shape, F32)
        x, y, c, _ = _place()
        me = 4 * x + 2 * y + c
        buf[me] = v_ref[...]
        cps = []
        for r in range(1, N_DEV):
            to = (x ^ (r >> 2), y ^ ((r >> 1) & 1), c ^ (r & 1))
            cp = pltpu.make_async_remote_copy(src_ref=v_ref, dst_ref=buf.at[me], send_sem=send_sems.at[r - 1],
                                              recv_sem=recv_sems.at[r - 1], device_id=to, device_id_type=MESH)
            cp.start()
            cps.append(cp)
        for r in range(1, N_DEV):
            pltpu.make_async_remote_copy(src_ref=v_ref, dst_ref=buf.at[me ^ r], send_sem=send_sems.at[r - 1],
                                         recv_sem=recv_sems.at[r - 1], device_id=(x, y, c), device_id_type=MESH).wait_recv()
        for cp in cps:
            cp.wait_send()
        acc = buf[0]
        for d in range(1, N_DEV):
            acc = acc + buf[d]
        o_ref[...] = acc

    vm = pl.BlockSpec(memory_space=pltpu.VMEM)
    return pl.pallas_call(
        body, name=name, in_specs=[vm], out_specs=[vm, vm],
        out_shape=[jax.ShapeDtypeStruct((SMALL_ROWS, D_MODEL), F32), jax.ShapeDtypeStruct((8, 128), F32)],
        scratch_shapes=[pltpu.VMEM((N_DEV, SMALL_ROWS, D_MODEL), F32), pltpu.SemaphoreType.DMA((N_DEV - 1,)),
                        pltpu.SemaphoreType.DMA((N_DEV - 1,))],
    )(v)


def _adamw(name, w, g, m, v):
    rows, cols = w.shape
    tr = next((c for c in (256, 176, 128, 64, 32, 8) if rows % c == 0), rows)
    spec = pl.BlockSpec((tr, cols), lambda i: (i, 0))

    def body(w_ref, g_ref, m_ref, v_ref, d_ref, mo_ref, vo_ref):
        gv = g_ref[...]
        m_new = ADAM_B1 * m_ref[...] + (1.0 - ADAM_B1) * gv
        v_new = ADAM_B2 * v_ref[...] + (1.0 - ADAM_B2) * jnp.square(gv)
        m_hat = m_new / (1.0 - ADAM_B1 ** ADAM_STEP)
        v_hat = v_new / (1.0 - ADAM_B2 ** ADAM_STEP)
        d_ref[...] = -ADAM_LR * (m_hat / (jnp.sqrt(v_hat) + ADAM_EPS) + ADAM_WD * w_ref[...])
        mo_ref[...] = m_new
        vo_ref[...] = v_new

    return pl.pallas_call(
        body, name=name, grid=(rows // tr,),
        in_specs=[spec] * 4, out_specs=[spec] * 3,
        out_shape=[jax.ShapeDtypeStruct((rows, cols), F32)] * 3,
        compiler_params=_cparams(("parallel",)),
    )(w, g, m, v)


WEIGHTS = ("norm_ffn1", "ffn1_w_in", "ffn1_w_out", "norm_mix", "w_in", "gate_bias", "pool_w", "pool_scale", "w_ret_up",
           "w_pool_up", "w_out", "norm_ffn2", "ffn2_w_in", "ffn2_w_out", "norm_final")
SMALL_ROW = dict(norm_ffn1=0, norm_mix=1, gate_bias=2, pool_scale=4, norm_ffn2=5, norm_final=6)


def _as2d(a):
    return a.reshape(-1, a.shape[-1])


def kernel(x, norm_ffn1, ffn1_w_in, ffn1_w_out, norm_mix, w_in, gate_bias, pool_w, pool_scale, w_ret_up, w_pool_up, w_out, norm_ffn2, ffn2_w_in, ffn2_w_out, norm_final, loss_target, m_norm_ffn1, m_ffn1_w_in, m_ffn1_w_out, m_norm_mix, m_w_in, m_gate_bias, m_pool_w, m_pool_scale, m_w_ret_up, m_w_pool_up, m_w_out, m_norm_ffn2, m_ffn2_w_in, m_ffn2_w_out, m_norm_final, v_norm_ffn1, v_ffn1_w_in, v_ffn1_w_out, v_norm_mix, v_w_in, v_gate_bias, v_pool_w, v_pool_scale, v_w_ret_up, v_w_pool_up, v_w_out, v_norm_ffn2, v_ffn2_w_in, v_ffn2_w_out, v_norm_final):
    wt = dict(norm_ffn1=norm_ffn1, ffn1_w_in=ffn1_w_in, ffn1_w_out=ffn1_w_out, norm_mix=norm_mix, w_in=w_in, gate_bias=gate_bias,
              pool_w=pool_w, pool_scale=pool_scale, w_ret_up=w_ret_up, w_pool_up=w_pool_up, w_out=w_out, norm_ffn2=norm_ffn2,
              ffn2_w_in=ffn2_w_in, ffn2_w_out=ffn2_w_out, norm_final=norm_final)
    mom = dict(norm_ffn1=m_norm_ffn1, ffn1_w_in=m_ffn1_w_in, ffn1_w_out=m_ffn1_w_out, norm_mix=m_norm_mix, w_in=m_w_in,
               gate_bias=m_gate_bias, pool_w=m_pool_w, pool_scale=m_pool_scale, w_ret_up=m_w_ret_up, w_pool_up=m_w_pool_up,
               w_out=m_w_out, norm_ffn2=m_norm_ffn2, ffn2_w_in=m_ffn2_w_in, ffn2_w_out=m_ffn2_w_out, norm_final=m_norm_final)
    var = dict(norm_ffn1=v_norm_ffn1, ffn1_w_in=v_ffn1_w_in, ffn1_w_out=v_ffn1_w_out, norm_mix=v_norm_mix, w_in=v_w_in,
               gate_bias=v_gate_bias, pool_w=v_pool_w, pool_scale=v_pool_scale, w_ret_up=v_w_ret_up, w_pool_up=v_w_pool_up,
               w_out=v_w_out, norm_ffn2=v_norm_ffn2, ffn2_w_in=v_ffn2_w_in, ffn2_w_out=v_ffn2_w_out, norm_final=v_norm_final)

    ax, ay, ac = lax.axis_index("x"), lax.axis_index("y"), lax.axis_index("c")
    chip = 2 * ax + ay
    c_arr = jnp.reshape(ac, (1,)).astype(jnp.int32)
    sc_arr = jnp.stack([chip, ac]).astype(jnp.int32)
    bias_cols = gate_bias.shape[-1]

    placed = lax.dynamic_update_slice(jnp.zeros((SMALL_ROWS, D_MODEL), F32), gate_bias[0], (0, chip * bias_cols))
    bias_sum, token = _all_reduce_small("gather_gate_bias", jnp.where(ac == 0, placed, 0.0))
    bias_full = bias_sum[:2]
    first = WEIGHT_GROUPS[0]
    gather_groups, token = _gather_start("first", [0], {nm: wt[nm][0].astype(BF16) + token[0, 0].astype(BF16) for nm in first})
    rest, rest_token = _gather_start("rest", [1, 2, 3],
                                     {nm: wt[nm][0].astype(BF16) + token[0, 0].astype(BF16) for nm in BIG if nm not in first})
    gather_groups.update(rest)
    vec = dict(norm_ffn1=norm_ffn1, norm_mix=norm_mix, norm_ffn2=norm_ffn2, pool_scale=pool_scale,
               norm_final=norm_final.reshape(1, D_MODEL), gate_bias=bias_full)

    def get_w(g, after):
        return _gather_finish(g, gather_groups[g], (after, rest_token) if g == 0 else (after,))

    pairs, pending = [], []

    def on_grads(gr):
        g = len(pairs)
        names = GRAD_GROUPS[g]
        assert set(names) == set(gr), (names, list(gr))
        state, token = _pair_exchange_start(g, names, {nm: _grad_view(KIND[nm], gr[nm]) for nm in names})
        pairs.append(state)
        return token[0:1, 0:1]

    def flush(after):
        g = len(pending)
        names = GRAD_GROUPS[g]
        views, from_sib = _pair_exchange_wait(g, names, pairs[g], after)
        psums = {nm: _pair_sum(f"pair_sum_{nm}", KIND[nm], views[nm], from_sib[nm], c_arr) for nm in names}
        state, token = _shard_exchange_start(g, names, psums)
        pending.append(state)
        tokens.append(token)
        return token[0:1, 0:1]

    tokens = []
    loss_local, dx, small = _local_step(x[0], loss_target[0], vec, get_w, on_grads, flush)

    packed = jnp.concatenate([small["norm_ffn1"], small["norm_mix"], small["gate_bias"], small["pool_scale"],
                              small["norm_ffn2"], small["norm_final"], jnp.broadcast_to(loss_local, (1, D_MODEL))], axis=0)
    small_sum, _ = _all_reduce_small("reduce_small_grads", packed)
    loss = small_sum[SMALL_ROWS - 1, 0]
    grads, delta, new_m, new_v = {}, {}, {}, {}

    def adamw(nm):
        shape = wt[nm].shape
        d, m2, v2 = _adamw(f"adamw_{nm}", _as2d(wt[nm]), _as2d(grads[nm]), _as2d(mom[nm]), _as2d(var[nm]))
        delta[nm], new_m[nm], new_v[nm] = d.reshape(shape), m2.reshape(shape), v2.reshape(shape)
        return d

    for nm in ("norm_ffn1", "norm_mix", "pool_scale", "norm_ffn2"):
        grads[nm] = small_sum[SMALL_ROW[nm]][None, :]
    grads["norm_final"] = small_sum[SMALL_ROW["norm_final"]]
    grads["gate_bias"] = lax.dynamic_slice(small_sum, (SMALL_ROW["gate_bias"], chip * bias_cols), (2, bias_cols))[None]
    after = (tokens[-1],)
    for g, names in enumerate(GRAD_GROUPS):
        psums, from_chips = _shard_exchange_wait(g, names, pending[g], after)
        bufs = {nm: _shard_sum(f"shard_sum_{nm}", KIND[nm], psums[nm], from_chips[nm], sc_arr) for nm in names}
        reduced = _half_exchange(g, names, bufs)
        for nm in names:
            grads[nm] = reduced[nm].reshape(wt[nm].shape)
        after = tuple(adamw(nm) for nm in names)
    for nm in WEIGHTS:
        if nm not in delta:
            adamw(nm)

    return (loss, dx[None], *[grads[nm] for nm in WEIGHTS], *[delta[nm] for nm in WEIGHTS],
            *[new_m[nm] for nm in WEIGHTS], *[new_v[nm] for nm in WEIGHTS])
```

```python
import functools

import numpy as np
import jax
import jax.numpy as jnp
from jax import lax
from jax.experimental import pallas as pl
from jax.experimental.pallas import tpu as pltpu

F32 = jnp.float32
BF16 = jnp.bfloat16
MESH = pl.DeviceIdType.MESH

D_MODEL = 1024
D_FF = 2816
HEADS = 4
HEAD_DIM = 256
GROUPS = 4
GROUP_DIM = 256
POOL_WINDOWS = (2, 4, 8, 16)
IN_WIDTH = 7 * D_MODEL
ROPE_BASE = 10000.0
NORM_EPS = 1e-6
FFN_RES_WEIGHT = 0.5
ADAM_LR, ADAM_B1, ADAM_B2, ADAM_EPS, ADAM_WD, ADAM_STEP = 0.001, 0.9, 0.999, 1e-08, 0.01, 10

N_CHIPS = 4
RET_BLOCK = 256
V7X_VMEM_LIMIT = 48 * 1024 * 1024


def _cparams(sem):
    return pltpu.CompilerParams(dimension_semantics=sem, vmem_limit_bytes=V7X_VMEM_LIMIT)


def _sigmoid(x):
    return jax.nn.sigmoid(x)


_DIMS = {"nn": (((1,), (0,)), ((), ())), "nt": (((1,), (1,)), ((), ())), "tn": (((0,), (0,)), ((), ()))}


def _matmul(name, a, b, mode, m, n, k, tm, tn, tk, out_dtypes, a_spec=None, b_spec=None, extras=(), consts=(), epilogue=None,
            resident=None, n_outer=False):
    tm, tn, tk = min(tm, m), min(tn, n), min(tk, k)
    gi, gj, gk = m // tm, n // tn, k // tk
    assert gi * tm == m and gj * tn == n and gk * tk == k, (name, m, n, k, tm, tn, tk)
    assert not (n_outer and (a_spec is not None or b_spec is not None)), name
    once = dict(pipeline_mode=pl.Buffered(1))

    def spec(shape, index, **kw):
        return pl.BlockSpec(shape, (lambda j, i, kk: index(i, j, kk)) if n_outer else index, **kw)

    if a_spec is None:
        kw = once if resident == "a" else {}
        a_spec = (spec((tk, tm), lambda i, j, kk: (kk, i), **kw) if mode == "tn"
                  else spec((tm, tk), lambda i, j, kk: (i, kk), **kw))
    if b_spec is None:
        kw = once if resident == "b" else {}
        b_spec = (spec((tn, tk), lambda i, j, kk: (j, kk), **kw) if mode == "nt"
                  else spec((tk, tn), lambda i, j, kk: (kk, j), **kw))
    n_ex, n_out = len(extras) + len(consts), len(out_dtypes)
    dims = _DIMS[mode]

    def body(a_ref, b_ref, *rest):
        ex_refs, out_refs = rest[:n_ex], rest[n_ex:n_ex + n_out]

        def finish(acc):
            outs = (acc,) if epilogue is None else epilogue(acc, *[e[...] for e in ex_refs])
            for o_ref, o in zip(out_refs, outs):
                o_ref[...] = o.astype(o_ref.dtype)

        prod = lax.dot_general(a_ref[...], b_ref[...], dims, preferred_element_type=F32)
        if gk == 1:
            finish(prod)
        else:
            acc_ref = rest[n_ex + n_out]
            kk = pl.program_id(2)

            @pl.when(kk == 0)
            def _():
                acc_ref[...] = prod

            @pl.when(kk > 0)
            def _():
                acc_ref[...] += prod

            @pl.when(kk == gk - 1)
            def _():
                finish(acc_ref[...])

    o_spec = spec((tm, tn), lambda i, j, kk: (i, j))
    outs = pl.pallas_call(
        body, name=name, grid=(gj, gi, gk) if n_outer else (gi, gj, gk),
        in_specs=[a_spec, b_spec] + [o_spec] * len(extras) + [spec((1, tn), lambda i, j, kk: (0, j))] * len(consts),
        out_specs=[o_spec] * n_out,
        out_shape=[jax.ShapeDtypeStruct((m, n), dt) for dt in out_dtypes],
        scratch_shapes=[pltpu.VMEM((tm, tn), F32)] if gk > 1 else [],
        compiler_params=_cparams(("parallel", "parallel", "arbitrary")),
    )(a, b, *extras, *consts)
    return outs[0] if n_out == 1 else outs


def _row_spec(tm, width, col_block=0):
    return pl.BlockSpec((tm, width), lambda i: (i, col_block))


def _full_spec(shape):
    return pl.BlockSpec(shape, lambda *_: (0,) * len(shape))


def _rmsnorm_fwd(name, h, g, tm=512):
    t = h.shape[0]

    def body(h_ref, g_ref, o_ref):
        x = h_ref[...]
        r = lax.rsqrt(jnp.mean(x * x, axis=-1, keepdims=True) + NORM_EPS)
        o_ref[...] = (x * r * g_ref[...]).astype(BF16)

    return pl.pallas_call(
        body, name=name, grid=(t // tm,),
        in_specs=[_row_spec(tm, D_MODEL), _full_spec((1, D_MODEL))],
        out_specs=_row_spec(tm, D_MODEL),
        out_shape=jax.ShapeDtypeStruct((t, D_MODEL), BF16),
        compiler_params=_cparams(("parallel",)),
    )(h, g)


def _proj_norm_bwd(name, a_list, a_specs, parts, w, h, g, dres, tm):
    t = h.shape[0]
    na = len(a_list)

    def body(*refs):
        a_refs = refs[:na]
        w_ref, h_ref, g_ref, dres_ref, dh_ref, dhb_ref, dg_ref = refs[na:]
        i = pl.program_id(0)
        dn_v = None
        for which, lead, k0, k1 in parts:
            a_ref = a_refs[which]
            term = _dot(a_ref[...] if lead is None else a_ref[lead], w_ref[:, k0:k1], "nt")
            dn_v = term if dn_v is None else dn_v + term
        x = h_ref[...]
        r = lax.rsqrt(jnp.mean(x * x, axis=-1, keepdims=True) + NORM_EPS)
        xh = x * r
        dxh = dn_v * g_ref[...]
        dh = dres_ref[...] + r * (dxh - xh * jnp.mean(dxh * xh, axis=-1, keepdims=True))
        dh_ref[...] = dh
        dhb_ref[...] = dh.astype(BF16)
        part = jnp.sum(dn_v * xh, axis=0, keepdims=True)

        @pl.when(i == 0)
        def _():
            dg_ref[...] = part

        @pl.when(i > 0)
        def _():
            dg_ref[...] += part

    row = _row_spec(tm, D_MODEL)
    return pl.pallas_call(
        body, name=name, grid=(t // tm,),
        in_specs=list(a_specs) + [pl.BlockSpec(w.shape, lambda i: (0, 0), pipeline_mode=pl.Buffered(1)), row,
                                  _full_spec((1, D_MODEL)), row],
        out_specs=[row, row, _full_spec((1, D_MODEL))],
        out_shape=[jax.ShapeDtypeStruct((t, D_MODEL), F32), jax.ShapeDtypeStruct((t, D_MODEL), BF16),
                   jax.ShapeDtypeStruct((1, D_MODEL), F32)],
        compiler_params=_cparams(("arbitrary",)),
    )(*a_list, w, h, g, dres)


def _out_loss_and_grad(name, mid, w_out, h, g, target, tm=512):
    t = h.shape[0]

    def body(m_ref, w_ref, h_ref, g_ref, t_ref, dh_ref, dhb_ref, dg_ref, loss_ref):
        i = pl.program_id(0)
        x = h_ref[...] + FFN_RES_WEIGHT * _dot(m_ref[...], w_ref[...])
        gv = g_ref[...]
        r = lax.rsqrt(jnp.mean(x * x, axis=-1, keepdims=True) + NORM_EPS)
        xh = x * r
        err = xh * gv - t_ref[...]
        row = jnp.mean(err * err, axis=-1, keepdims=True)
        part_loss = 0.5 * jnp.sum(row, axis=0, keepdims=True)
        dy = err * (1.0 / D_MODEL)
        dxh = dy * gv
        dh = r * (dxh - xh * jnp.mean(dxh * xh, axis=-1, keepdims=True))
        dh_ref[...] = dh
        dhb_ref[...] = dh.astype(BF16)
        part = jnp.sum(dy * xh, axis=0, keepdims=True)

        @pl.when(i == 0)
        def _():
            dg_ref[...] = part
            loss_ref[...] = jnp.zeros(loss_ref.shape, F32) + part_loss

        @pl.when(i > 0)
        def _():
            dg_ref[...] += part
            loss_ref[...] += part_loss

    return pl.pallas_call(
        body, name=name, grid=(t // tm,),
        in_specs=[_row_spec(tm, D_FF), pl.BlockSpec((D_FF, D_MODEL), lambda i: (0, 0), pipeline_mode=pl.Buffered(1)),
                  _row_spec(tm, D_MODEL), _full_spec((1, D_MODEL)), _row_spec(tm, D_MODEL)],
        out_specs=[_row_spec(tm, D_MODEL), _row_spec(tm, D_MODEL), _full_spec((1, D_MODEL)), _full_spec((8, 128))],
        out_shape=[jax.ShapeDtypeStruct((t, D_MODEL), F32), jax.ShapeDtypeStruct((t, D_MODEL), BF16),
                   jax.ShapeDtypeStruct((1, D_MODEL), F32), jax.ShapeDtypeStruct((8, 128), F32)],
        compiler_params=_cparams(("arbitrary",)),
    )(mid, w_out, h, g, target)


def _rope_tables(t):
    half = HEAD_DIM // 2
    inv_freq = np.float32(ROPE_BASE) ** (-np.arange(half, dtype=np.float32) / np.float32(half))
    ang = (np.arange(t, dtype=np.float32)[:, None] * inv_freq[None, :].astype(np.float32)).astype(np.float32)
    return jnp.asarray(np.cos(ang.astype(np.float64)).astype(np.float32)), jnp.asarray(np.sin(ang.astype(np.float64)).astype(np.float32))


ROPE_HALF = HEAD_DIM // 2
K_SCALE = HEAD_DIM ** -0.5


def _rotate(ref, hh, c, s, scale=None):
    lo, mid, hi = hh * HEAD_DIM, hh * HEAD_DIM + ROPE_HALF, (hh + 1) * HEAD_DIM
    x1, x2 = ref[:, lo:mid].astype(F32), ref[:, mid:hi].astype(F32)
    y = jnp.concatenate([x1 * c - x2 * s, x1 * s + x2 * c], axis=1)
    return y if scale is None else y * scale


def _unrotate_into(ref, hh, dy, c, s, scale=None):
    lo, mid, hi = hh * HEAD_DIM, hh * HEAD_DIM + ROPE_HALF, (hh + 1) * HEAD_DIM
    y1, y2 = dy[:, :ROPE_HALF], dy[:, ROPE_HALF:]
    d1, d2 = y1 * c + y2 * s, y2 * c - y1 * s
    if scale is not None:
        d1, d2 = d1 * scale, d2 * scale
    ref[:, lo:mid] = d1.astype(ref.dtype)
    ref[:, mid:hi] = d2.astype(ref.dtype)


def _retention_tables():
    b, chunk = RET_BLOCK, 64
    gamma = 1.0 - 2.0 ** (-5.0 - np.arange(HEADS, dtype=np.float64))
    log_g = np.log(gamma)[:, None, None]
    i = np.arange(b)[:, None]
    j = np.arange(b)[None, :]
    same = (i // chunk) == (j // chunk)
    earlier = (j // chunk) < (i // chunk)
    expo = np.where(same, np.abs(i - j), np.where(earlier, i - j, 0)).astype(np.float64)
    mask = np.where(same | earlier, 1.0, 0.0)
    dmat = np.exp(log_g * expo[None]) * mask[None]
    qd = np.exp(log_g[:, :, 0] * (np.arange(b)[None, :] + 1.0))
    kd = np.exp(log_g[:, :, 0] * (b - 1.0 - np.arange(b)[None, :]))
    cd = np.exp(log_g[:, :, 0] * b) * np.ones((1, HEAD_DIM))
    as32 = lambda v: jnp.asarray(v.astype(np.float32))
    return (as32(dmat), as32(np.swapaxes(dmat, 1, 2)), as32(qd[:, :, None]), as32(kd[:, :, None]), as32(cd[:, None, :]))


def _dot(a, b, mode="nn"):
    return lax.dot_general(a, b, _DIMS[mode], preferred_element_type=F32)


GRET_BLOCK = 3


def _head_specs(nb, rev=False):
    pos = (lambda n: nb - 1 - n) if rev else (lambda n: n)
    tok = pl.BlockSpec((RET_BLOCK, D_MODEL), lambda n: (pos(n), 0))
    blk = [pl.BlockSpec((RET_BLOCK, D_MODEL), lambda n, b=b: (pos(n), b)) for b in range(GRET_BLOCK + 1)]
    rope = pl.BlockSpec((RET_BLOCK, ROPE_HALF), lambda n: (pos(n), 0))
    tab = _full_spec((HEADS, RET_BLOCK, RET_BLOCK))
    col = _full_spec((HEADS, RET_BLOCK, 1))
    rowv = _full_spec((HEADS, 1, HEAD_DIM))
    st = pl.BlockSpec((HEADS, None, HEAD_DIM, HEAD_DIM), lambda n: (0, pos(n), 0, 0))
    return tok, blk, rope, tab, col, rowv, st


def _retention_fwd(name, proj, cos, sin, tables):
    t = proj.shape[0]
    nb = t // RET_BLOCK
    dmat, _, qd, kd, cd = tables
    tok, blk, rope, tab, col, rowv, st = _head_specs(nb)

    def body(q_ref, k_ref, v_ref, g_ref, c_ref, s_ref, d_ref, qd_ref, kd_ref, cd_ref, o_ref, ret_ref, st_ref, state):
        n = pl.program_id(0)

        @pl.when(n == 0)
        def _():
            state[...] = jnp.zeros(state.shape, F32)

        cs, sn = c_ref[...], s_ref[...]
        for hh in range(HEADS):
            sl = slice(hh * HEAD_DIM, (hh + 1) * HEAD_DIM)
            q, k, v = _rotate(q_ref, hh, cs, sn), _rotate(k_ref, hh, cs, sn, K_SCALE), v_ref[:, sl].astype(BF16)
            s = _dot(q.astype(BF16), k.astype(BF16), "nt") * d_ref[hh]
            stb = state[hh].astype(BF16)
            st_ref[hh] = stb
            o = _dot(s.astype(BF16), v) + _dot((q * qd_ref[hh]).astype(BF16), stb)
            o_ref[:, sl] = o
            rn = o * lax.rsqrt(jnp.mean(o * o, axis=-1, keepdims=True) + NORM_EPS)
            g = g_ref[:, sl].astype(F32)
            ret_ref[:, sl] = (rn * (g * _sigmoid(g))).astype(BF16)
            state[hh] = state[hh] * cd_ref[hh] + _dot((k * kd_ref[hh]).astype(BF16), v, "tn")

    return pl.pallas_call(
        body, name=name, grid=(nb,),
        in_specs=blk + [rope, rope, tab, col, col, rowv],
        out_specs=[tok, tok, st],
        out_shape=[jax.ShapeDtypeStruct((t, D_MODEL), F32), jax.ShapeDtypeStruct((t, D_MODEL), BF16),
                   jax.ShapeDtypeStruct((HEADS, nb, HEAD_DIM, HEAD_DIM), BF16)],
        scratch_shapes=[pltpu.VMEM((HEADS, HEAD_DIM, HEAD_DIM), F32)],
        compiler_params=_cparams(("arbitrary",)),
    )(proj, proj, proj, proj, cos, sin, dmat, qd, kd, cd)


def _retention_bwd(name, dru, w_ru, o, proj, cos, sin, states, tables):
    t = proj.shape[0]
    nb = t // RET_BLOCK
    dmat, dmat_t, qd, kd, cd = tables
    tok, blk, rope, tab, col, rowv, st = _head_specs(nb, rev=True)

    def body(dru_ref, wru_ref, o_ref, q_ref, k_ref, v_ref, g_ref, c_ref, s_ref, st_ref, d_ref, dt_ref, qd_ref, kd_ref, cd_ref,
             dq_ref, dk_ref, dv_ref, dg_ref, gstate):
        n = pl.program_id(0)

        @pl.when(n == 0)
        def _():
            gstate[...] = jnp.zeros(gstate.shape, F32)

        cs, sn = c_ref[...], s_ref[...]
        dret = _dot(dru_ref[...], wru_ref[...], "nt")
        for hh in range(HEADS):
            sl = slice(hh * HEAD_DIM, (hh + 1) * HEAD_DIM)
            o_v, g, dr = o_ref[:, sl], g_ref[:, sl].astype(F32), dret[:, sl]
            sg = _sigmoid(g)
            r = lax.rsqrt(jnp.mean(o_v * o_v, axis=-1, keepdims=True) + NORM_EPS)
            rn = o_v * r
            d_rn = dr * (g * sg)
            dg_ref[:, sl] = (dr * rn * (sg * (1.0 + g * (1.0 - sg)))).astype(BF16)
            d_o = r * (d_rn - rn * jnp.mean(d_rn * rn, axis=-1, keepdims=True))
            dob = d_o.astype(BF16)

            q, k, v = _rotate(q_ref, hh, cs, sn), _rotate(k_ref, hh, cs, sn, K_SCALE), v_ref[:, sl].astype(BF16)
            qb, kb = q.astype(BF16), k.astype(BF16)
            qdv, kdv = qd_ref[hh], kd_ref[hh]
            s_t = (_dot(kb, qb, "nt") * dt_ref[hh]).astype(BF16)
            p_t = (_dot(v, dob, "nt") * dt_ref[hh]).astype(BF16)
            p = (_dot(dob, v, "nt") * d_ref[hh]).astype(BF16)
            stb = st_ref[hh]
            gb = gstate[hh].astype(BF16)
            _unrotate_into(dq_ref, hh, _dot(p, kb) + _dot(dob, stb, "nt") * qdv, cs, sn)
            _unrotate_into(dk_ref, hh, _dot(p_t, qb) + _dot(v, gb, "nt") * kdv, cs, sn, K_SCALE)
            dv_ref[:, sl] = (_dot(s_t, dob) + _dot((k * kdv).astype(BF16), gb)).astype(BF16)
            gstate[hh] = gstate[hh] * cd_ref[hh] + _dot((q * qdv).astype(BF16), dob, "tn")

    return pl.pallas_call(
        body, name=name, grid=(nb,),
        in_specs=[tok, pl.BlockSpec((D_MODEL, D_MODEL), lambda n: (0, 0), pipeline_mode=pl.Buffered(1)), tok] + blk
                 + [rope, rope, st, tab, tab, col, col, rowv],
        out_specs=[tok, tok, tok, tok],
        out_shape=[jax.ShapeDtypeStruct((t, D_MODEL), BF16)] * 4,
        scratch_shapes=[pltpu.VMEM((HEADS, HEAD_DIM, HEAD_DIM), F32)],
        compiler_params=_cparams(("arbitrary",)),
    )(dru, w_ru, o, proj, proj, proj, proj, cos, sin, states, dmat, dmat_t, qd, kd, cd)


POOL_TILE = 256


def _pool_tables():
    b = POOL_TILE
    tt = np.arange(b)[:, None]
    jj = np.arange(b)[None, :]
    cur, prev = [], []
    for w in POOL_WINDOWS:
        cur.append(((tt - jj >= 0) & (tt - jj <= w - 1)).astype(np.float32))
        prev.append((tt - (jj - b) <= w - 1).astype(np.float32))
    cur, prev = np.stack(cur), np.stack(prev)
    as16 = lambda v: jnp.asarray(v, dtype=BF16)
    return as16(cur), as16(prev), as16(np.swapaxes(cur, 1, 2)), as16(np.swapaxes(prev, 1, 2))


def _split2(x):
    hi = x.astype(BF16)
    return hi, (x - hi.astype(F32)).astype(BF16)


POOL_BLOCK = 4


def _pool_count(n, window):
    tpos = n * POOL_TILE + lax.broadcasted_iota(jnp.int32, (POOL_TILE, 1), 0)
    return jnp.minimum(tpos + 1, window).astype(F32)


def _pool_fwd(name, proj, pool_w, scale, tables):
    t = proj.shape[0]
    nb = t // POOL_TILE
    mc, mp, _, _ = tables
    tab = _full_spec((GROUPS, POOL_TILE, POOL_TILE))
    row = _row_spec(POOL_TILE, D_MODEL)

    def body(pc_ref, pp_ref, mc_ref, mp_ref, w_ref, sc_ref, pm_ref, mix_ref, po_ref):
        n = pl.program_id(0)
        for g, window in enumerate(POOL_WINDOWS):
            sl = slice(g * GROUP_DIM, (g + 1) * GROUP_DIM)
            p = pc_ref[:, sl]
            win = _dot(mc_ref[g], p) + jnp.where(n > 0, _dot(mp_ref[g], pp_ref[:, sl]), 0.0)
            pm = (win / _pool_count(n, window) - p.astype(F32)).astype(BF16)
            pm_ref[:, sl] = pm
            mixed = _dot(pm, w_ref[g])
            mix_ref[:, sl] = mixed
            po_ref[:, sl] = (mixed * sc_ref[:, sl]).astype(BF16)

    return pl.pallas_call(
        body, name=name, grid=(nb,),
        in_specs=[_row_spec(POOL_TILE, D_MODEL, POOL_BLOCK),
                  pl.BlockSpec((POOL_TILE, D_MODEL), lambda n: (jnp.maximum(n - 1, 0), POOL_BLOCK)),
                  tab, tab, _full_spec((GROUPS, GROUP_DIM, GROUP_DIM)), _full_spec((1, D_MODEL))],
        out_specs=[row] * 3,
        out_shape=[jax.ShapeDtypeStruct((t, D_MODEL), BF16), jax.ShapeDtypeStruct((t, D_MODEL), F32),
                   jax.ShapeDtypeStruct((t, D_MODEL), BF16)],
        compiler_params=_cparams(("parallel",)),
    )(proj, proj, mc, mp, pool_w, scale)


def _pool_bwd(name, dpu, w_pu, pm, mixed, pool_w, scale, tables):
    t = dpu.shape[0]
    nb = t // POOL_TILE
    _, _, mct, mpt = tables
    cur = pl.BlockSpec((POOL_TILE, D_MODEL), lambda n: (nb - 1 - n, 0))
    tab = _full_spec((GROUPS, POOL_TILE, POOL_TILE))
    wspec = _full_spec((GROUPS, GROUP_DIM, GROUP_DIM))
    sspec = _full_spec((1, D_MODEL))

    def body(dpu_ref, wpu_ref, pm_ref, mix_ref, mct_ref, mpt_ref, w_ref, sc_ref, dp_ref, dw_ref, ds_ref, later):
        n = pl.program_id(0)

        @pl.when(n == 0)
        def _():
            dw_ref[...] = jnp.zeros(dw_ref.shape, F32)
            ds_ref[...] = jnp.zeros(ds_ref.shape, F32)
            later[...] = jnp.zeros(later.shape, F32)

        dpo = _dot(dpu_ref[...], wpu_ref[...], "nt")
        for g, window in enumerate(POOL_WINDOWS):
            sl = slice(g * GROUP_DIM, (g + 1) * GROUP_DIM)
            dc, sc = dpo[:, sl], sc_ref[:, sl]
            dmix = (dc * sc).astype(BF16)
            dpm = _dot(dmix, w_ref[g], "nt")
            e = dpm / _pool_count(nb - 1 - n, window)
            e_hi, e_lo = _split2(e)
            f_hi, f_lo = _split2(later[g])
            mctv, mptv = mct_ref[g], mpt_ref[g]
            back = _dot(mctv, e_hi) + _dot(mctv, e_lo)
            after = _dot(mptv, f_hi) + _dot(mptv, f_lo)
            dp_ref[:, sl] = (back + after - dpm).astype(BF16)
            later[g] = e
            dw_ref[g] += _dot(pm_ref[:, sl], dmix, "tn")
            ds_ref[:, sl] += jnp.sum(dc * mix_ref[:, sl], axis=0, keepdims=True)

    return pl.pallas_call(
        body, name=name, grid=(nb,),
        in_specs=[cur, pl.BlockSpec((D_MODEL, D_MODEL), lambda n: (0, 0), pipeline_mode=pl.Buffered(1)), cur, cur, tab, tab,
                  wspec, sspec],
        out_specs=[cur, wspec, sspec],
        out_shape=[jax.ShapeDtypeStruct((t, D_MODEL), BF16), jax.ShapeDtypeStruct((GROUPS, GROUP_DIM, GROUP_DIM), F32),
                   jax.ShapeDtypeStruct((1, D_MODEL), F32)],
        scratch_shapes=[pltpu.VMEM((GROUPS, POOL_TILE, GROUP_DIM), F32)],
        compiler_params=_cparams(("arbitrary",)),
    )(dpu, w_pu, pm, mixed, mct, mpt, pool_w, scale)


GATE0_BLOCK, GATE1_BLOCK = 5, 6


def _merge_fwd(name, ret, po, w_ru, w_pu, w_out, proj, bias, h, next_g, tm=512):
    t = ret.shape[0]

    def body(r_ref, p_ref, wr_ref, wp_ref, wo_ref, g0_ref, g1_ref, b_ref, h_ref, ng_ref, m_ref, ru_ref, pu_ref, ho_ref, n_ref):
        ru = _dot(r_ref[...], wr_ref[...])
        pu = _dot(p_ref[...], wp_ref[...])
        ru_ref[...] = ru
        pu_ref[...] = pu
        merged = (_sigmoid(g0_ref[...].astype(F32) + b_ref[0:1, :]) * ru
                  + _sigmoid(g1_ref[...].astype(F32) + b_ref[1:2, :]) * pu).astype(BF16)
        m_ref[...] = merged
        h_new = h_ref[...] + _dot(merged, wo_ref[...])
        ho_ref[...] = h_new
        n_ref[...] = _normed(h_new, ng_ref[...]).astype(BF16)

    row = _row_spec(tm, D_MODEL)
    wspec = pl.BlockSpec((D_MODEL, D_MODEL), lambda i: (0, 0), pipeline_mode=pl.Buffered(1))
    return pl.pallas_call(
        body, name=name, grid=(t // tm,),
        in_specs=[row, row, wspec, wspec, wspec, _row_spec(tm, D_MODEL, GATE0_BLOCK), _row_spec(tm, D_MODEL, GATE1_BLOCK),
                  _full_spec((2, D_MODEL)), row, _full_spec((1, D_MODEL))],
        out_specs=[row] * 5,
        out_shape=[jax.ShapeDtypeStruct((t, D_MODEL), BF16), jax.ShapeDtypeStruct((t, D_MODEL), F32),
                   jax.ShapeDtypeStruct((t, D_MODEL), F32), jax.ShapeDtypeStruct((t, D_MODEL), F32),
                   jax.ShapeDtypeStruct((t, D_MODEL), BF16)],
        compiler_params=_cparams(("parallel",)),
    )(ret, po, w_ru, w_pu, w_out, proj, proj, bias, h, next_g)


def _merge_bwd(name, dh_b, w_out, ru, pu, proj, bias, tm=512):
    t = dh_b.shape[0]

    def body(dh_ref, wo_ref, ru_ref, pu_ref, g0_ref, g1_ref, b_ref, dru_ref, dpu_ref, dg0_ref, dg1_ref, db_ref):
        i = pl.program_id(0)
        d = _dot(dh_ref[...], wo_ref[...], "nt")
        s0 = _sigmoid(g0_ref[...].astype(F32) + b_ref[0:1, :])
        s1 = _sigmoid(g1_ref[...].astype(F32) + b_ref[1:2, :])
        dru_ref[...] = (d * s0).astype(BF16)
        dpu_ref[...] = (d * s1).astype(BF16)
        dg0 = d * ru_ref[...] * (s0 * (1.0 - s0))
        dg1 = d * pu_ref[...] * (s1 * (1.0 - s1))
        dg0_ref[...] = dg0.astype(BF16)
        dg1_ref[...] = dg1.astype(BF16)
        part0 = jnp.sum(dg0, axis=0, keepdims=True)
        part1 = jnp.sum(dg1, axis=0, keepdims=True)

        @pl.when(i == 0)
        def _():
            db_ref[0:1, :] = part0
            db_ref[1:2, :] = part1

        @pl.when(i > 0)
        def _():
            db_ref[0:1, :] += part0
            db_ref[1:2, :] += part1

    row = _row_spec(tm, D_MODEL)
    return pl.pallas_call(
        body, name=name, grid=(t // tm,),
        in_specs=[row, pl.BlockSpec((D_MODEL, D_MODEL), lambda i: (0, 0), pipeline_mode=pl.Buffered(1)), row, row,
                  _row_spec(tm, D_MODEL, GATE0_BLOCK), _row_spec(tm, D_MODEL, GATE1_BLOCK), _full_spec((2, D_MODEL))],
        out_specs=[row, row, row, row, _full_spec((2, D_MODEL))],
        out_shape=[jax.ShapeDtypeStruct((t, D_MODEL), BF16)] * 4 + [jax.ShapeDtypeStruct((2, D_MODEL), F32)],
        compiler_params=_cparams(("arbitrary",)),
    )(dh_b, w_out, ru, pu, proj, proj, bias)


def _half_scale(acc):
    return (FFN_RES_WEIGHT * acc,)


def _normed(h, g):
    return h * lax.rsqrt(jnp.mean(h * h, axis=-1, keepdims=True) + NORM_EPS) * g


def _residual_half_norm(acc, res, g):
    h = res + FFN_RES_WEIGHT * acc
    return h, _normed(h, g)


FF_TILE = D_FF // 2
DW_TILE = 256
SAVED_FF_DTYPE = BF16
FF_CHUNKS = ((0, 512), (512, 1024), (1024, FF_TILE))


def _ffn_in(name, nrm, w_in, tm=512):
    t = nrm.shape[0]
    nj = D_FF // FF_TILE

    def body(n_ref, wg_ref, wu_ref, a_ref, mid_ref):
        nv = n_ref[...]
        for c0, c1 in FF_CHUNKS:
            gate = _dot(nv, wg_ref[:, c0:c1])
            up = _dot(nv, wu_ref[:, c0:c1])
            s = _sigmoid(gate)
            silu = gate * s
            a_ref[0, :, c0:c1] = (FFN_RES_WEIGHT * up * (s * (1.0 + gate * (1.0 - s)))).astype(a_ref.dtype)
            a_ref[1, :, c0:c1] = (FFN_RES_WEIGHT * silu).astype(a_ref.dtype)
            mid_ref[:, c0:c1] = (silu * up).astype(BF16)

    return pl.pallas_call(
        body, name=name, grid=(nj, t // tm),
        in_specs=[pl.BlockSpec((tm, D_MODEL), lambda j, i: (i, 0)),
                  pl.BlockSpec((D_MODEL, FF_TILE), lambda j, i: (0, j)),
                  pl.BlockSpec((D_MODEL, FF_TILE), lambda j, i: (0, j + nj))],
        out_specs=[pl.BlockSpec((2, tm, FF_TILE), lambda j, i: (0, i, j)), pl.BlockSpec((tm, FF_TILE), lambda j, i: (i, j))],
        out_shape=[jax.ShapeDtypeStruct((2, t, D_FF), SAVED_FF_DTYPE), jax.ShapeDtypeStruct((t, D_FF), BF16)],
        compiler_params=_cparams(("parallel", "parallel")),
    )(nrm, w_in, w_in)


def _ffn_dact(name, dout_b, w_out, a, tm=512):
    t = dout_b.shape[0]

    def body(d_ref, w_ref, a_ref, da_ref):
        dv = d_ref[...]
        for c0, c1 in FF_CHUNKS:
            dm = _dot(dv, w_ref[c0:c1, :], "nt")
            da_ref[0, :, c0:c1] = (dm * a_ref[0, :, c0:c1].astype(F32)).astype(BF16)
            da_ref[1, :, c0:c1] = (dm * a_ref[1, :, c0:c1].astype(F32)).astype(BF16)

    blk = pl.BlockSpec((2, tm, FF_TILE), lambda j, i: (0, i, j))
    return pl.pallas_call(
        body, name=name, grid=(D_FF // FF_TILE, t // tm),
        in_specs=[pl.BlockSpec((tm, D_MODEL), lambda j, i: (i, 0)), pl.BlockSpec((FF_TILE, D_MODEL), lambda j, i: (j, 0)), blk],
        out_specs=blk,
        out_shape=jax.ShapeDtypeStruct((2, t, D_FF), BF16),
        compiler_params=_cparams(("parallel", "parallel")),
    )(dout_b, w_out, a)


def _ffn_fwd(tag, h, nrm, get_w_in, get_w_out, finish):
    t = h.shape[0]
    w_in = get_w_in(nrm)
    a, mid = _ffn_in(f"{tag}_in", nrm, w_in, tm=min(512, t))
    w_out = get_w_out(mid)
    return finish(mid, w_out), (nrm, a, mid, w_in, w_out)


def _ffn_bwd(tag, h, g, saved, dout, dout_b, on_grads, flush):
    t = h.shape[0]
    nrm, a, mid, w_in, w_out = saved
    d_w_out = _matmul(f"{tag}_dwout", mid, dout_b, "tn", D_FF, D_MODEL, t, DW_TILE, D_MODEL, t, [BF16], epilogue=_half_scale,
                      resident="b")
    da = _ffn_dact(f"{tag}_dact", dout_b, w_out, a, tm=min(512, t))
    nj = D_FF // DW_TILE
    d_w_in = _dw_resident(f"{tag}_dwin", nrm, [da], [pl.BlockSpec((None, t, DW_TILE), lambda s: (s // nj, 0, s % nj))],
                          2 * nj, None, DW_TILE)
    tie = on_grads({f"{tag}_w_in": d_w_in, f"{tag}_w_out": d_w_out})
    tm = min(256, t)
    dh, dh_b, dg = _proj_norm_bwd(f"{tag}_dn", [da], [pl.BlockSpec((2, tm, D_FF), lambda i: (0, i, 0))],
                                  ((0, 0, 0, D_FF), (0, 1, D_FF, 2 * D_FF)), w_in, h, g if tie is None else g + tie, dout, tm)
    return dh, dh_b, dg, flush(dh)


def _dw_resident(name, u, pieces, piece_specs, n_tiles, which_piece, tn):
    t = u.shape[0]
    npc = len(pieces)

    def body(*refs):
        u_ref, p_refs, o_ref, ut_ref = refs[0], refs[1:1 + npc], refs[1 + npc], refs[2 + npc]
        s = pl.program_id(0)

        @pl.when(s == 0)
        def _():
            ut_ref[...] = u_ref[...].T

        if npc == 1:
            o_ref[...] = _dot(ut_ref[...], p_refs[0][...]).astype(BF16)
        for which in range(npc if npc > 1 else 0):
            @pl.when(which_piece(s) == which)
            def _(which=which):
                o_ref[...] = _dot(ut_ref[...], p_refs[which][...]).astype(BF16)

    return pl.pallas_call(
        body, name=name, grid=(n_tiles,),
        in_specs=[pl.BlockSpec((t, D_MODEL), lambda s: (0, 0), pipeline_mode=pl.Buffered(1))] + list(piece_specs),
        out_specs=pl.BlockSpec((D_MODEL, tn), lambda s: (0, s)),
        out_shape=jax.ShapeDtypeStruct((D_MODEL, n_tiles * tn), BF16),
        scratch_shapes=[pltpu.VMEM((D_MODEL, t), BF16)],
        compiler_params=_cparams(("arbitrary",)),
    )(u, *pieces)


def _mix_dwin(name, u, pieces, tn=256):
    t = u.shape[0]
    nj = D_MODEL // tn
    specs = [pl.BlockSpec((t, tn), lambda s, k=k: (0, jnp.clip(s - k * nj, 0, nj - 1))) for k in range(len(pieces))]
    return _dw_resident(name, u, pieces, specs, len(pieces) * nj, lambda s: s // nj, tn)


def _local_step(x, target, vec, get_w, on_grads, flush):
    t = x.shape[0]
    cos, sin = _rope_tables(t)
    rtab = _retention_tables()
    ptab = _pool_tables()
    w = {}

    def getter(group, name):
        def get(after):
            if name not in w:
                w.update(get_w(group, after))
            return w[name]
        return get

    nrm1 = _rmsnorm_fwd("ffn1_norm", x, vec["norm_ffn1"])
    def out_and_norm(mid, w_out):
        return _matmul("ffn1_out", mid, w_out, "nn", t, D_MODEL, D_FF, 512, D_MODEL, D_FF, [F32, BF16],
                       extras=(x,), consts=(vec["norm_mix"],), epilogue=_residual_half_norm)

    (h1, u), s1 = _ffn_fwd("ffn1", x, nrm1, getter(0, "ffn1_w_in"), getter(1, "ffn1_w_out"), out_and_norm)
    w.update(get_w(2, u))
    proj = _matmul("mix_in", u, w["w_in"], "nn", t, IN_WIDTH, D_MODEL, 1024, 1024, D_MODEL, [BF16], n_outer=True)
    o, ret, states = _retention_fwd("retention", proj, cos, sin, rtab)
    pm, mixed, po = _pool_fwd("pool", proj, w["pool_w"], vec["pool_scale"], ptab)
    merged, ru, pu, h2, nrm2 = _merge_fwd("merge", ret, po, w["w_ret_up"], w["w_pool_up"], w["w_out"], proj, w["gate_bias"],
                                          h1, vec["norm_ffn2"], tm=min(512, t))
    def out_and_loss(mid, w_out):
        return _out_loss_and_grad("ffn2_out_loss", mid, w_out, h2, vec["norm_final"], target, tm=min(512, t))

    (dh3, dh3_b, dg_final, loss), s2 = _ffn_fwd("ffn2", h2, nrm2, getter(3, "ffn2_w_in"), getter(3, "ffn2_w_out"), out_and_loss)

    def tied(v, tie):
        return v if tie is None else v + tie

    dh2, dh2_b, dg_ffn2, tie = _ffn_bwd("ffn2", h2, vec["norm_ffn2"], s2, dh3, dh3_b, on_grads, flush)
    def square_dw(name, act, grad):
        return _matmul(name, act, grad, "tn", D_MODEL, D_MODEL, t, D_MODEL, D_MODEL, 1024, [BF16])

    d_w_out = square_dw("mix_dwout", merged, dh2_b)
    dru, dpu, dg0, dg1, d_bias = _merge_bwd("merge_bwd", dh2_b, w["w_out"], ru, pu, proj, tied(w["gate_bias"], tie))
    d_w_ru = square_dw("mix_dwru", ret, dru)
    d_w_pu = square_dw("mix_dwpu", po, dpu)
    dp, d_pool_w, d_scale = _pool_bwd("pool_bwd", dpu, w["w_pool_up"], pm, mixed, w["pool_w"], vec["pool_scale"], ptab)
    dq, dk, dv, dgr = _retention_bwd("retention_bwd", dru, w["w_ret_up"], o, proj, cos, sin, states, rtab)
    dproj = [dq, dk, dv, dgr, dp, dg0, dg1]
    d_w_in = _mix_dwin("mix_dwin", u, dproj)
    tie = on_grads(dict(w_in=d_w_in, pool_w=d_pool_w.astype(BF16), w_ret_up=d_w_ru, w_pool_up=d_w_pu, w_out=d_w_out))
    tm = min(256, t)
    dh1, dh1_b, dg_mix = _proj_norm_bwd("mix_du", dproj, [_row_spec(tm, D_MODEL)] * len(dproj),
                                        [(k, None, k * D_MODEL, (k + 1) * D_MODEL) for k in range(len(dproj))],
                                        w["w_in"], h1, tied(vec["norm_mix"], tie), dh2, tm)
    tie = flush(dh1)
    dx, _, dg_ffn1, _ = _ffn_bwd("ffn1", x, tied(vec["norm_ffn1"], tie), s1, dh1, dh1_b, on_grads, flush)

    small = dict(norm_ffn1=dg_ffn1, norm_mix=dg_mix, gate_bias=d_bias, pool_scale=d_scale, norm_ffn2=dg_ffn2,
                 norm_final=dg_final)
    return loss[0, 0], dx, small


BIG = ("ffn1_w_in", "ffn1_w_out", "w_in", "pool_w", "w_ret_up", "w_pool_up", "w_out", "ffn2_w_in", "ffn2_w_out")
KIND = dict(ffn1_w_in="col", ffn1_w_out="row", w_in="col", pool_w="pool", w_ret_up="row", w_pool_up="row", w_out="row",
            ffn2_w_in="col", ffn2_w_out="row", gate_bias="col")
ANY = pl.BlockSpec(memory_space=pl.ANY)


def _place():
    x, y, c = lax.axis_index("x"), lax.axis_index("y"), lax.axis_index("c")
    chips = [(1 - x, y), (x, 1 - y), (1 - x, 1 - y)]
    return x, y, c, chips


def _full_view_shape(kind, local_shape):
    if kind == "col":
        return (2, local_shape[0] // 2, N_CHIPS * local_shape[1])
    if kind == "row":
        return (N_CHIPS, 2, local_shape[0] // 2, local_shape[1])
    return (GROUPS, N_CHIPS, 2, local_shape[1] // 2, local_shape[2])


def _local_view(kind, arr):
    if kind == "pool":
        return arr.reshape(GROUPS, 2, arr.shape[1] // 2, arr.shape[2])
    return arr.reshape(2, arr.shape[0] // 2, arr.shape[1])


def _blk(kind, ref, s, c):
    if kind == "col":
        cs = ref.shape[2] // N_CHIPS
        return ref.at[c, :, pl.ds(pl.multiple_of(s * cs, 128), cs)]
    if kind == "row":
        return ref.at[s, c]
    return ref.at[:, s, c]


def _half(kind, ref, c):
    return ref.at[:, c] if kind == "pool" else ref.at[c]


def _shard(kind, ref, s):
    if kind == "col":
        cs = ref.shape[2] // N_CHIPS
        return ref.at[:, :, pl.ds(pl.multiple_of(s * cs, 128), cs)]
    if kind == "row":
        return ref.at[s]
    return ref.at[:, s]


HBM = pl.BlockSpec(memory_space=pltpu.HBM)
SEM = pl.BlockSpec(memory_space=pltpu.SEMAPHORE)
EFFECT = pltpu.SideEffectType.DATAFLOW_SIDE_EFFECTING
WEIGHT_GROUPS = (("gate_bias", "ffn1_w_in"), ("ffn1_w_out",), ("w_in", "pool_w", "w_ret_up", "w_pool_up", "w_out"), ("ffn2_w_in", "ffn2_w_out"))
GRAD_GROUPS = (("ffn2_w_in", "ffn2_w_out"), ("w_in", "pool_w", "w_ret_up", "w_pool_up", "w_out"), ("ffn1_w_in", "ffn1_w_out"))


def _hbm(a):
    return pltpu.with_memory_space_constraint(a, pltpu.HBM)


def _natural(kind, o):
    if kind == "col":
        return o.reshape(o.shape[0] * o.shape[1], o.shape[2])
    if kind == "row":
        return o.reshape(-1, o.shape[3])
    return o.reshape(GROUPS, -1, o.shape[4])


def _ici_copy(kind, loc, full, j, chips, s, c, send_sem, recv_sem):
    px, py = chips[j]
    return (pltpu.make_async_remote_copy(src_ref=_half(kind, loc, c), dst_ref=_blk(kind, full, s, c), send_sem=send_sem,
                                         recv_sem=recv_sem, device_id=(px, py, c), device_id_type=MESH),
            pltpu.make_async_remote_copy(src_ref=_half(kind, loc, c), dst_ref=_blk(kind, full, 2 * px + py, c), send_sem=send_sem,
                                         recv_sem=recv_sem, device_id=(px, py, c), device_id_type=MESH))


def _gather_start(tag, group_ids, shards):
    grps = [WEIGHT_GROUPS[g] for g in group_ids]
    names = [nm for grp in grps for nm in grp]
    kinds = [KIND[nm] for nm in names]
    n, ng = len(names), len(grps)
    locs = [_hbm(_local_view(KIND[nm], shards[nm])) for nm in names]
    lands = [_hbm(lax.empty(_full_view_shape(KIND[nm], shards[nm].shape), shards[nm].dtype)) for nm in names]
    first = np.cumsum([0] + [len(grp) for grp in grps])

    def body(*refs):
        loc, full = refs[:n], refs[n:2 * n]
        send_sems, recv_sems = refs[2 * n:2 * n + ng], refs[2 * n + ng:2 * n + 2 * ng]
        token = refs[-1]
        x, y, c, chips = _place()
        s = 2 * x + y
        for g in range(ng):
            for a in range(first[g], first[g + 1]):
                for j in range(3):
                    k = 3 * (a - first[g]) + j
                    _ici_copy(kinds[a], loc[a], full[a], j, chips, s, c, send_sems[g].at[k], recv_sems[g].at[k])[0].start()
        token[...] = jnp.zeros(token.shape, F32)

    sem_shapes = [pltpu.SemaphoreType.DMA((3 * len(grp),)) for grp in grps]
    outs = pl.pallas_call(
        body, name=f"gather_start_{tag}",
        in_specs=[HBM] * (2 * n),
        out_specs=[SEM] * (2 * ng) + [HBM] * (2 * n) + [pl.BlockSpec(memory_space=pltpu.VMEM)],
        out_shape=sem_shapes + sem_shapes + [pltpu.HBM(a.shape, a.dtype) for a in locs + lands] + [jax.ShapeDtypeStruct((8, 128), F32)],
        input_output_aliases={i: 2 * ng + i for i in range(2 * n)},
        compiler_params=pltpu.CompilerParams(has_side_effects=EFFECT),
    )(*locs, *lands)
    send_sems, recv_sems = outs[:ng], outs[ng:2 * ng]
    locs_t, lands_t = outs[2 * ng:2 * ng + n], outs[2 * ng + n:2 * ng + 2 * n]
    groups = {}
    for k, g in enumerate(group_ids):
        sl = slice(first[k], first[k + 1])
        groups[g] = (send_sems[k], recv_sems[k], list(locs_t[sl]), list(lands_t[sl]))
    return groups, outs[-1]


def _gather_finish(g, group, after):
    names = WEIGHT_GROUPS[g]
    kinds = [KIND[nm] for nm in names]
    m = len(names)
    send_sem, recv_sem, locs, lands = group

    def wait_body(*refs):
        loc, full = refs[:m], refs[m:2 * m]
        send_sems, recv_sems = refs[2 * m], refs[2 * m + 1]
        x, y, c, chips = _place()
        s = 2 * x + y
        for a in range(m):
            for j in range(3):
                k = 3 * a + j
                sent, landed = _ici_copy(kinds[a], loc[a], full[a], j, chips, s, c, send_sems.at[k], recv_sems.at[k])
                sent.wait_send()
                landed.wait_recv()

    outs = pl.pallas_call(
        wait_body, name=f"gather_wait_{g}",
        in_specs=[HBM] * (2 * m) + [SEM, SEM] + [ANY] * len(after), out_specs=[HBM] * (2 * m),
        out_shape=[pltpu.HBM(a.shape, a.dtype) for a in locs + lands],
        input_output_aliases={i: i for i in range(2 * m)},
        compiler_params=pltpu.CompilerParams(has_side_effects=EFFECT),
    )(*locs, *lands, send_sem, recv_sem, *after)
    locs, lands = outs[:m], outs[m:]

    def forward_body(*refs):
        loc, full = refs[:m], refs[2 * m:3 * m]
        send_sems, recv_sems = refs[3 * m:]
        x, y, c, chips = _place()
        s = 2 * x + y
        sib = (x, y, 1 - c)

        def remote(a, k, src, dst):
            return pltpu.make_async_remote_copy(src_ref=src, dst_ref=dst, send_sem=send_sems.at[4 * a + k],
                                                recv_sem=recv_sems.at[4 * a + k], device_id=sib, device_id_type=MESH)

        sends = []
        for a in range(m):
            for j, (px, py) in enumerate(chips):
                theirs = _blk(kinds[a], full[a], 2 * px + py, c)
                sends.append(remote(a, j, theirs, theirs))
            sends.append(remote(a, 3, loc[a], _shard(kinds[a], full[a], s)))
        for cp in sends:
            cp.start()
        for a in range(m):
            for j, (px, py) in enumerate(chips):
                from_sib = _blk(kinds[a], full[a], 2 * px + py, 1 - c)
                remote(a, j, from_sib, from_sib).wait_recv()
            own = _shard(kinds[a], full[a], s)
            remote(a, 3, own, own).wait_recv()
        for cp in sends:
            cp.wait_send()

    outs = pl.pallas_call(
        forward_body, name=f"gather_forward_{g}",
        in_specs=[ANY] * (2 * m), out_specs=[ANY] * m,
        out_shape=[jax.ShapeDtypeStruct(a.shape, a.dtype) for a in lands],
        input_output_aliases={m + i: i for i in range(m)},
        scratch_shapes=[pltpu.SemaphoreType.DMA((4 * m,)), pltpu.SemaphoreType.DMA((4 * m,))],
    )(*locs, *lands)
    return {nm: _natural(k, o) for nm, k, o in zip(names, kinds, outs)}


def _grad_view(kind, g):
    if kind == "col":
        return g.reshape(2, g.shape[0] // 2, g.shape[1])
    if kind == "row":
        return g.reshape(N_CHIPS, 2, g.shape[0] // (2 * N_CHIPS), g.shape[1])
    return g.reshape(GROUPS, N_CHIPS, 2, g.shape[1] // (2 * N_CHIPS), g.shape[2])


def _pair_copies(kinds, g, got, send_sems, recv_sems):
    x, y, c, _ = _place()

    def other_half(kind, ref):
        if kind == "col":
            return ref.at[1 - c]
        if kind == "row":
            return ref.at[:, 1 - c]
        return ref.at[:, :, 1 - c]

    return [pltpu.make_async_remote_copy(src_ref=other_half(kinds[a], g[a]), dst_ref=got[a], send_sem=send_sems.at[a],
                                         recv_sem=recv_sems.at[a], device_id=(x, y, 1 - c), device_id_type=MESH)
            for a in range(len(kinds))]


def _pair_exchange_start(tag, names, views):
    kinds = [KIND[nm] for nm in names]
    n = len(names)

    def got_shape(kind, v):
        if kind == "col":
            return v.shape[1:]
        if kind == "row":
            return (v.shape[0],) + v.shape[2:]
        return v.shape[:2] + v.shape[3:]

    srcs = [_hbm(views[nm]) for nm in names]
    lands = [_hbm(lax.empty(got_shape(k, views[nm]), BF16)) for nm, k in zip(names, kinds)]

    def body(*refs):
        g, got = refs[:n], refs[n:2 * n]
        for cp in _pair_copies(kinds, g, got, refs[2 * n], refs[2 * n + 1]):
            cp.start()
        refs[-1][...] = jnp.zeros(refs[-1].shape, F32)

    sem_shape = pltpu.SemaphoreType.DMA((n,))
    outs = pl.pallas_call(
        body, name=f"grad_pair_exchange_start_{tag}",
        in_specs=[HBM] * (2 * n),
        out_specs=[SEM, SEM] + [HBM] * (2 * n) + [pl.BlockSpec(memory_space=pltpu.VMEM)],
        out_shape=[sem_shape, sem_shape] + [pltpu.HBM(a.shape, a.dtype) for a in srcs + lands] + [jax.ShapeDtypeStruct((8, 128), F32)],
        input_output_aliases={i: 2 + i for i in range(2 * n)},
        compiler_params=pltpu.CompilerParams(has_side_effects=EFFECT),
    )(*srcs, *lands)
    return (outs[0], outs[1], list(outs[2:2 + n]), list(outs[2 + n:2 + 2 * n])), outs[-1]


def _pair_exchange_wait(tag, names, state, after):
    kinds = [KIND[nm] for nm in names]
    n = len(names)
    send_sem, recv_sem, srcs, lands = state

    def body(*refs):
        g, got = refs[:n], refs[n:2 * n]
        for cp in _pair_copies(kinds, g, got, refs[2 * n], refs[2 * n + 1]):
            cp.wait_send()
            cp.wait_recv()

    outs = pl.pallas_call(
        body, name=f"grad_pair_exchange_wait_{tag}",
        in_specs=[HBM] * (2 * n) + [SEM, SEM, ANY], out_specs=[HBM] * (2 * n),
        out_shape=[pltpu.HBM(a.shape, a.dtype) for a in srcs + lands],
        input_output_aliases={i: i for i in range(2 * n)},
        compiler_params=pltpu.CompilerParams(has_side_effects=EFFECT),
    )(*srcs, *lands, send_sem, recv_sem, after)
    return dict(zip(names, outs[:n])), dict(zip(names, outs[n:]))


def _pair_sum(name, kind, view, got, c_arr):
    if kind == "col":
        _, rows, cols = view.shape
        tr = 128
        grid = (rows // tr,)
        v_spec = pl.BlockSpec((None, tr, cols), lambda i, c: (c[0], i, 0))
        g_spec = pl.BlockSpec((tr, cols), lambda i, c: (i, 0))
    elif kind == "row":
        _, _, rows, cols = view.shape
        grid = (N_CHIPS,)
        v_spec = pl.BlockSpec((None, None, rows, cols), lambda i, c: (i, c[0], 0, 0))
        g_spec = pl.BlockSpec((None, rows, cols), lambda i, c: (i, 0, 0))
    else:
        _, _, _, rows, cols = view.shape
        grid = (GROUPS,)
        v_spec = pl.BlockSpec((None, N_CHIPS, None, rows, cols), lambda i, c: (i, 0, c[0], 0, 0))
        g_spec = pl.BlockSpec((None, N_CHIPS, rows, cols), lambda i, c: (i, 0, 0, 0))

    def body(c_ref, v_ref, g_ref, o_ref):
        o_ref[...] = (v_ref[...].astype(F32) + g_ref[...].astype(F32)).astype(BF16)

    return pl.pallas_call(
        body, name=name,
        grid_spec=pltpu.PrefetchScalarGridSpec(num_scalar_prefetch=1, grid=grid, in_specs=[v_spec, g_spec], out_specs=g_spec),
        out_shape=jax.ShapeDtypeStruct(got.shape, BF16),
        compiler_params=_cparams(("parallel",)),
    )(c_arr, view, got)


def _piece(kind, ref, s):
    if kind == "col":
        cs = ref.shape[1] // N_CHIPS
        return ref.at[:, pl.ds(pl.multiple_of(s * cs, 128), cs)]
    if kind == "row":
        return ref.at[s]
    return ref.at[:, s]


def _piece_shape(kind, shape):
    if kind == "col":
        return (shape[0], shape[1] // N_CHIPS)
    if kind == "row":
        return shape[1:]
    return (shape[0],) + shape[2:]


def _shard_copies(kinds, p, got, send_sems, recv_sems):
    x, y, c, chips = _place()
    return [pltpu.make_async_remote_copy(src_ref=_piece(kinds[a], p[a], 2 * px + py), dst_ref=got[a].at[j],
                                         send_sem=send_sems.at[3 * a + j], recv_sem=recv_sems.at[3 * a + j],
                                         device_id=(px, py, c), device_id_type=MESH)
            for a in range(len(kinds)) for j, (px, py) in enumerate(chips)]


def _shard_exchange_start(g, names, psums):
    kinds = [KIND[nm] for nm in names]
    n = len(names)
    srcs = [_hbm(psums[nm]) for nm in names]
    lands = [_hbm(lax.empty((3,) + _piece_shape(k, psums[nm].shape), BF16)) for nm, k in zip(names, kinds)]

    def body(*refs):
        p, got = refs[:n], refs[n:2 * n]
        send_sems, recv_sems = refs[2 * n], refs[2 * n + 1]
        token = refs[-1]
        for cp in _shard_copies(kinds, p, got, send_sems, recv_sems):
            cp.start()
        token[...] = jnp.zeros(token.shape, F32)

    sem_shape = pltpu.SemaphoreType.DMA((3 * n,))
    outs = pl.pallas_call(
        body, name=f"grad_shard_exchange_start_{g}",
        in_specs=[HBM] * (2 * n),
        out_specs=[SEM, SEM] + [HBM] * (2 * n) + [pl.BlockSpec(memory_space=pltpu.VMEM)],
        out_shape=[sem_shape, sem_shape] + [pltpu.HBM(a.shape, a.dtype) for a in srcs + lands] + [jax.ShapeDtypeStruct((8, 128), F32)],
        input_output_aliases={i: 2 + i for i in range(2 * n)},
        compiler_params=pltpu.CompilerParams(has_side_effects=EFFECT),
    )(*srcs, *lands)
    return (outs[0], outs[1], list(outs[2:2 + n]), list(outs[2 + n:2 + 2 * n])), outs[-1]


def _shard_exchange_wait(g, names, state, after):
    kinds = [KIND[nm] for nm in names]
    n = len(names)
    send_sem, recv_sem, srcs, lands = state

    def body(*refs):
        p, got = refs[:n], refs[n:2 * n]
        for cp in _shard_copies(kinds, p, got, refs[2 * n], refs[2 * n + 1]):
            cp.wait_send()
            cp.wait_recv()

    outs = pl.pallas_call(
        body, name=f"grad_shard_exchange_wait_{g}",
        in_specs=[HBM] * (2 * n) + [SEM, SEM] + [ANY] * len(after), out_specs=[HBM] * (2 * n),
        out_shape=[pltpu.HBM(a.shape, a.dtype) for a in srcs + lands],
        input_output_aliases={i: i for i in range(2 * n)},
        compiler_params=pltpu.CompilerParams(has_side_effects=EFFECT),
    )(*srcs, *lands, send_sem, recv_sem, *after)
    return dict(zip(names, outs[:n])), dict(zip(names, outs[n:]))


def _shard_sum(name, kind, psum, got, sc_arr):
    if kind == "col":
        rows, cols = psum.shape
        cs = cols // N_CHIPS
        tr = 128
        grid = (rows // tr,)
        p_spec = pl.BlockSpec((tr, cs), lambda i, sc: (i, sc[0]))
        g_spec = pl.BlockSpec((3, tr, cs), lambda i, sc: (0, i, 0))
        o_spec = pl.BlockSpec((None, tr, cs), lambda i, sc: (sc[1], i, 0))
        out_shape = (2, rows, cs)
    elif kind == "row":
        _, rows, cols = psum.shape
        grid = (1,)
        p_spec = pl.BlockSpec((None, rows, cols), lambda i, sc: (sc[0], 0, 0))
        g_spec = pl.BlockSpec((3, rows, cols), lambda i, sc: (0, 0, 0))
        o_spec = pl.BlockSpec((None, rows, cols), lambda i, sc: (sc[1], 0, 0))
        out_shape = (2, rows, cols)
    else:
        _, _, rows, cols = psum.shape
        grid = (1,)
        p_spec = pl.BlockSpec((GROUPS, None, rows, cols), lambda i, sc: (0, sc[0], 0, 0))
        g_spec = pl.BlockSpec((3, GROUPS, rows, cols), lambda i, sc: (0, 0, 0, 0))
        o_spec = pl.BlockSpec((GROUPS, None, rows, cols), lambda i, sc: (0, sc[1], 0, 0))
        out_shape = (GROUPS, 2, rows, cols)

    def body(sc_ref, p_ref, g_ref, o_ref):
        o_ref[...] = ((p_ref[...].astype(F32) + g_ref[0].astype(F32)) + g_ref[1].astype(F32)) + g_ref[2].astype(F32)

    return pl.pallas_call(
        body, name=name,
        grid_spec=pltpu.PrefetchScalarGridSpec(num_scalar_prefetch=1, grid=grid, in_specs=[p_spec, g_spec], out_specs=o_spec),
        out_shape=jax.ShapeDtypeStruct(out_shape, F32),
        compiler_params=_cparams(("parallel",)),
    )(sc_arr, psum, got)


def _half_exchange(tag, names, bufs):
    kinds = [KIND[nm] for nm in names]
    n = len(names)

    def body(*refs):
        out = refs[n:2 * n]
        send_sems, recv_sems = refs[2 * n:]
        x, y, c, _ = _place()
        sib = (x, y, 1 - c)
        cps = []
        for a in range(n):
            mine = _half(kinds[a], out[a], c)
            cp = pltpu.make_async_remote_copy(src_ref=mine, dst_ref=mine, send_sem=send_sems.at[a], recv_sem=recv_sems.at[a],
                                              device_id=sib, device_id_type=MESH)
            cp.start()
            cps.append(cp)
        for a, cp in enumerate(cps):
            cp.wait_send()
            theirs = _half(kinds[a], out[a], 1 - c)
            pltpu.make_async_remote_copy(src_ref=theirs, dst_ref=theirs, send_sem=send_sems.at[a], recv_sem=recv_sems.at[a],
                                         device_id=sib, device_id_type=MESH).wait_recv()

    outs = pl.pallas_call(
        body, name=f"grad_half_exchange_{tag}",
        in_specs=[ANY] * n, out_specs=[ANY] * n,
        out_shape=[jax.ShapeDtypeStruct(bufs[nm].shape, F32) for nm in names],
        input_output_aliases={a: a for a in range(n)},
        scratch_shapes=[pltpu.SemaphoreType.DMA((n,)), pltpu.SemaphoreType.DMA((n,))],
    )(*[bufs[nm] for nm in names])
    return dict(zip(names, outs))


N_DEV = 8
SMALL_ROWS = 8


def _all_reduce_small(name, v):
    def body(v_ref, o_ref, token, buf, send_sems, recv_sems):
        token[...] = jnp.zeros(token.shape, F32)
        x, y, c, _ = _place()
        me = 4 * x + 2 * y + c
        buf[me] = v_ref[...]
        cps = []
        for r in range(1, N_DEV):
            to = (x ^ (r >> 2), y ^ ((r >> 1) & 1), c ^ (r & 1))
            cp = pltpu.make_async_remote_copy(src_ref=v_ref, dst_ref=buf.at[me], send_sem=send_sems.at[r - 1],
                                              recv_sem=recv_sems.at[r - 1], device_id=to, device_id_type=MESH)
            cp.start()
            cps.append(cp)
        for r in range(1, N_DEV):
            pltpu.make_async_remote_copy(src_ref=v_ref, dst_ref=buf.at[me ^ r], send_sem=send_sems.at[r - 1],
                                         recv_sem=recv_sems.at[r - 1], device_id=(x, y, c), device_id_type=MESH).wait_recv()
        for cp in cps:
            cp.wait_send()
        acc = buf[0]
        for d in range(1, N_DEV):
            acc = acc + buf[d]
        o_ref[...] = acc

    vm = pl.BlockSpec(memory_space=pltpu.VMEM)
    return pl.pallas_call(
        body, name=name, in_specs=[vm], out_specs=[vm, vm],
        out_shape=[jax.ShapeDtypeStruct((SMALL_ROWS, D_MODEL), F32), jax.ShapeDtypeStruct((8, 128), F32)],
        scratch_shapes=[pltpu.VMEM((N_DEV, SMALL_ROWS, D_MODEL), F32), pltpu.SemaphoreType.DMA((N_DEV - 1,)),
                        pltpu.SemaphoreType.DMA((N_DEV - 1,))],
    )(v)


def _adamw(name, w, g, m, v):
    rows, cols = w.shape
    tr = next((c for c in (256, 176, 128, 64, 32, 8) if rows % c == 0), rows)
    spec = pl.BlockSpec((tr, cols), lambda i: (i, 0))

    def body(w_ref, g_ref, m_ref, v_ref, d_ref, mo_ref, vo_ref):
        gv = g_ref[...]
        m_new = ADAM_B1 * m_ref[...] + (1.0 - ADAM_B1) * gv
        v_new = ADAM_B2 * v_ref[...] + (1.0 - ADAM_B2) * jnp.square(gv)
        m_hat = m_new / (1.0 - ADAM_B1 ** ADAM_STEP)
        v_hat = v_new / (1.0 - ADAM_B2 ** ADAM_STEP)
        d_ref[...] = -ADAM_LR * (m_hat / (jnp.sqrt(v_hat) + ADAM_EPS) + ADAM_WD * w_ref[...])
        mo_ref[...] = m_new
        vo_ref[...] = v_new

    return pl.pallas_call(
        body, name=name, grid=(rows // tr,),
        in_specs=[spec] * 4, out_specs=[spec] * 3,
        out_shape=[jax.ShapeDtypeStruct((rows, cols), F32)] * 3,
        compiler_params=_cparams(("parallel",)),
    )(w, g, m, v)


WEIGHTS = ("norm_ffn1", "ffn1_w_in", "ffn1_w_out", "norm_mix", "w_in", "gate_bias", "pool_w", "pool_scale", "w_ret_up",
           "w_pool_up", "w_out", "norm_ffn2", "ffn2_w_in", "ffn2_w_out", "norm_final")
SMALL_ROW = dict(norm_ffn1=0, norm_mix=1, gate_bias=2, pool_scale=4, norm_ffn2=5, norm_final=6)


def _as2d(a):
    return a.reshape(-1, a.shape[-1])


def kernel(x, norm_ffn1, ffn1_w_in, ffn1_w_out, norm_mix, w_in, gate_bias, pool_w, pool_scale, w_ret_up, w_pool_up, w_out, norm_ffn2, ffn2_w_in, ffn2_w_out, norm_final, loss_target, m_norm_ffn1, m_ffn1_w_in, m_ffn1_w_out, m_norm_mix, m_w_in, m_gate_bias, m_pool_w, m_pool_scale, m_w_ret_up, m_w_pool_up, m_w_out, m_norm_ffn2, m_ffn2_w_in, m_ffn2_w_out, m_norm_final, v_norm_ffn1, v_ffn1_w_in, v_ffn1_w_out, v_norm_mix, v_w_in, v_gate_bias, v_pool_w, v_pool_scale, v_w_ret_up, v_w_pool_up, v_w_out, v_norm_ffn2, v_ffn2_w_in, v_ffn2_w_out, v_norm_final):
    wt = dict(norm_ffn1=norm_ffn1, ffn1_w_in=ffn1_w_in, ffn1_w_out=ffn1_w_out, norm_mix=norm_mix, w_in=w_in, gate_bias=gate_bias,
              pool_w=pool_w, pool_scale=pool_scale, w_ret_up=w_ret_up, w_pool_up=w_pool_up, w_out=w_out, norm_ffn2=norm_ffn2,
              ffn2_w_in=ffn2_w_in, ffn2_w_out=ffn2_w_out, norm_final=norm_final)
    mom = dict(norm_ffn1=m_norm_ffn1, ffn1_w_in=m_ffn1_w_in, ffn1_w_out=m_ffn1_w_out, norm_mix=m_norm_mix, w_in=m_w_in,
               gate_bias=m_gate_bias, pool_w=m_pool_w, pool_scale=m_pool_scale, w_ret_up=m_w_ret_up, w_pool_up=m_w_pool_up,
               w_out=m_w_out, norm_ffn2=m_norm_ffn2, ffn2_w_in=m_ffn2_w_in, ffn2_w_out=m_ffn2_w_out, norm_final=m_norm_final)
    var = dict(norm_ffn1=v_norm_ffn1, ffn1_w_in=v_ffn1_w_in, ffn1_w_out=v_ffn1_w_out, norm_mix=v_norm_mix, w_in=v_w_in,
               gate_bias=v_gate_bias, pool_w=v_pool_w, pool_scale=v_pool_scale, w_ret_up=v_w_ret_up, w_pool_up=v_w_pool_up,
               w_out=v_w_out, norm_ffn2=v_norm_ffn2, ffn2_w_in=v_ffn2_w_in, ffn2_w_out=v_ffn2_w_out, norm_final=v_norm_final)

    ax, ay, ac = lax.axis_index("x"), lax.axis_index("y"), lax.axis_index("c")
    chip = 2 * ax + ay
    c_arr = jnp.reshape(ac, (1,)).astype(jnp.int32)
    sc_arr = jnp.stack([chip, ac]).astype(jnp.int32)
    bias_cols = gate_bias.shape[-1]

    first = {"gate_bias": gate_bias[0], "ffn1_w_in": ffn1_w_in[0].astype(BF16)}
    gather_groups, token = _gather_start("first", [0], first)
    rest, rest_token = _gather_start("rest", [1, 2, 3],
                                     {nm: wt[nm][0].astype(BF16) + token[0, 0].astype(BF16) for nm in BIG if nm not in first})
    gather_groups.update(rest)
    vec = dict(norm_ffn1=norm_ffn1, norm_mix=norm_mix, norm_ffn2=norm_ffn2, pool_scale=pool_scale,
               norm_final=norm_final.reshape(1, D_MODEL))

    def get_w(g, after):
        return _gather_finish(g, gather_groups[g], (after, rest_token) if g == 0 else (after,))

    pairs, pending = [], []

    def on_grads(gr):
        g = len(pairs)
        names = GRAD_GROUPS[g]
        assert set(names) == set(gr), (names, list(gr))
        state, token = _pair_exchange_start(g, names, {nm: _grad_view(KIND[nm], gr[nm]) for nm in names})
        pairs.append(state)
        return token[0:1, 0:1]

    def flush(after):
        g = len(pending)
        names = GRAD_GROUPS[g]
        views, from_sib = _pair_exchange_wait(g, names, pairs[g], after)
        psums = {nm: _pair_sum(f"pair_sum_{nm}", KIND[nm], views[nm], from_sib[nm], c_arr) for nm in names}
        state, token = _shard_exchange_start(g, names, psums)
        pending.append(state)
        tokens.append(token)
        return token[0:1, 0:1]

    tokens = []
    loss_local, dx, small = _local_step(x[0], loss_target[0], vec, get_w, on_grads, flush)

    packed = jnp.concatenate([small["norm_ffn1"], small["norm_mix"], small["gate_bias"], small["pool_scale"],
                              small["norm_ffn2"], small["norm_final"], jnp.broadcast_to(loss_local, (1, D_MODEL))], axis=0)
    small_sum, _ = _all_reduce_small("reduce_small_grads", packed)
    loss = small_sum[SMALL_ROWS - 1, 0]
    grads, delta, new_m, new_v = {}, {}, {}, {}

    def adamw(nm):
        shape = wt[nm].shape
        d, m2, v2 = _adamw(f"adamw_{nm}", _as2d(wt[nm]), _as2d(grads[nm]), _as2d(mom[nm]), _as2d(var[nm]))
        delta[nm], new_m[nm], new_v[nm] = d.reshape(shape), m2.reshape(shape), v2.reshape(shape)
        return d

    for nm in ("norm_ffn1", "norm_mix", "pool_scale", "norm_ffn2"):
        grads[nm] = small_sum[SMALL_ROW[nm]][None, :]
    grads["norm_final"] = small_sum[SMALL_ROW["norm_final"]]
    grads["gate_bias"] = lax.dynamic_slice(small_sum, (SMALL_ROW["gate_bias"], chip * bias_cols), (2, bias_cols))[None]
    after = (tokens[-1],)
    for g, names in enumerate(GRAD_GROUPS):
        psums, from_chips = _shard_exchange_wait(g, names, pending[g], after)
        bufs = {nm: _shard_sum(f"shard_sum_{nm}", KIND[nm], psums[nm], from_chips[nm], sc_arr) for nm in names}
        reduced = _half_exchange(g, names, bufs)
        for nm in names:
            grads[nm] = reduced[nm].reshape(wt[nm].shape)
        after = tuple(adamw(nm) for nm in names)
    for nm in WEIGHTS:
        if nm not in delta:
            adamw(nm)

    return (loss, dx[None], *[grads[nm] for nm in WEIGHTS], *[delta[nm] for nm in WEIGHTS],
            *[new_m[nm] for nm in WEIGHTS], *[new_v[nm] for nm in WEIGHTS])
```

```python
import functools

import numpy as np
import jax
import jax.numpy as jnp
from jax import lax
from jax.experimental import pallas as pl
from jax.experimental.pallas import tpu as pltpu

F32 = jnp.float32
BF16 = jnp.bfloat16
MESH = pl.DeviceIdType.MESH

D_MODEL = 1024
D_FF = 2816
HEADS = 4
HEAD_DIM = 256
GROUPS = 4
GROUP_DIM = 256
POOL_WINDOWS = (2, 4, 8, 16)
IN_WIDTH = 7 * D_MODEL
ROPE_BASE = 10000.0
NORM_EPS = 1e-6
FFN_RES_WEIGHT = 0.5
ADAM_LR, ADAM_B1, ADAM_B2, ADAM_EPS, ADAM_WD, ADAM_STEP = 0.001, 0.9, 0.999, 1e-08, 0.01, 10

N_CHIPS = 4
RET_BLOCK = 256
V7X_VMEM_LIMIT = 48 * 1024 * 1024


def _cparams(sem):
    return pltpu.CompilerParams(dimension_semantics=sem, vmem_limit_bytes=V7X_VMEM_LIMIT)


def _sigmoid(x):
    return jax.nn.sigmoid(x)


_DIMS = {"nn": (((1,), (0,)), ((), ())), "nt": (((1,), (1,)), ((), ())), "tn": (((0,), (0,)), ((), ()))}


def _matmul(name, a, b, mode, m, n, k, tm, tn, tk, out_dtypes, a_spec=None, b_spec=None, extras=(), consts=(), epilogue=None,
            resident=None, n_outer=False):
    tm, tn, tk = min(tm, m), min(tn, n), min(tk, k)
    gi, gj, gk = m // tm, n // tn, k // tk
    assert gi * tm == m and gj * tn == n and gk * tk == k, (name, m, n, k, tm, tn, tk)
    assert not (n_outer and (a_spec is not None or b_spec is not None)), name
    once = dict(pipeline_mode=pl.Buffered(1))

    def spec(shape, index, **kw):
        return pl.BlockSpec(shape, (lambda j, i, kk: index(i, j, kk)) if n_outer else index, **kw)

    if a_spec is None:
        kw = once if resident == "a" else {}
        a_spec = (spec((tk, tm), lambda i, j, kk: (kk, i), **kw) if mode == "tn"
                  else spec((tm, tk), lambda i, j, kk: (i, kk), **kw))
    if b_spec is None:
        kw = once if resident == "b" else {}
        b_spec = (spec((tn, tk), lambda i, j, kk: (j, kk), **kw) if mode == "nt"
                  else spec((tk, tn), lambda i, j, kk: (kk, j), **kw))
    n_ex, n_out = len(extras) + len(consts), len(out_dtypes)
    dims = _DIMS[mode]

    def body(a_ref, b_ref, *rest):
        ex_refs, out_refs = rest[:n_ex], rest[n_ex:n_ex + n_out]

        def finish(acc):
            outs = (acc,) if epilogue is None else epilogue(acc, *[e[...] for e in ex_refs])
            for o_ref, o in zip(out_refs, outs):
                o_ref[...] = o.astype(o_ref.dtype)

        prod = lax.dot_general(a_ref[...], b_ref[...], dims, preferred_element_type=F32)
        if gk == 1:
            finish(prod)
        else:
            acc_ref = rest[n_ex + n_out]
            kk = pl.program_id(2)

            @pl.when(kk == 0)
            def _():
                acc_ref[...] = prod

            @pl.when(kk > 0)
            def _():
                acc_ref[...] += prod

            @pl.when(kk == gk - 1)
            def _():
                finish(acc_ref[...])

    o_spec = spec((tm, tn), lambda i, j, kk: (i, j))
    outs = pl.pallas_call(
        body, name=name, grid=(gj, gi, gk) if n_outer else (gi, gj, gk),
        in_specs=[a_spec, b_spec] + [o_spec] * len(extras) + [spec((1, tn), lambda i, j, kk: (0, j))] * len(consts),
        out_specs=[o_spec] * n_out,
        out_shape=[jax.ShapeDtypeStruct((m, n), dt) for dt in out_dtypes],
        scratch_shapes=[pltpu.VMEM((tm, tn), F32)] if gk > 1 else [],
        compiler_params=_cparams(("parallel", "parallel", "arbitrary")),
    )(a, b, *extras, *consts)
    return outs[0] if n_out == 1 else outs


def _row_spec(tm, width, col_block=0):
    return pl.BlockSpec((tm, width), lambda i: (i, col_block))


def _full_spec(shape):
    return pl.BlockSpec(shape, lambda *_: (0,) * len(shape))


def _rmsnorm_fwd(name, h, g, tm=512):
    t = h.shape[0]

    def body(h_ref, g_ref, o_ref):
        x = h_ref[...]
        r = lax.rsqrt(jnp.mean(x * x, axis=-1, keepdims=True) + NORM_EPS)
        o_ref[...] = (x * r * g_ref[...]).astype(BF16)

    return pl.pallas_call(
        body, name=name, grid=(t // tm,),
        in_specs=[_row_spec(tm, D_MODEL), _full_spec((1, D_MODEL))],
        out_specs=_row_spec(tm, D_MODEL),
        out_shape=jax.ShapeDtypeStruct((t, D_MODEL), BF16),
        compiler_params=_cparams(("parallel",)),
    )(h, g)


def _proj_norm_bwd(name, a_list, a_specs, parts, w, h, g, dres, tm):
    t = h.shape[0]
    na = len(a_list)

    def body(*refs):
        a_refs = refs[:na]
        w_ref, h_ref, g_ref, dres_ref, dh_ref, dhb_ref, dg_ref = refs[na:]
        i = pl.program_id(0)
        dn_v = None
        for which, lead, k0, k1 in parts:
            a_ref = a_refs[which]
            term = _dot(a_ref[...] if lead is None else a_ref[lead], w_ref[:, k0:k1], "nt")
            dn_v = term if dn_v is None else dn_v + term
        x = h_ref[...]
        r = lax.rsqrt(jnp.mean(x * x, axis=-1, keepdims=True) + NORM_EPS)
        xh = x * r
        dxh = dn_v * g_ref[...]
        dh = dres_ref[...] + r * (dxh - xh * jnp.mean(dxh * xh, axis=-1, keepdims=True))
        dh_ref[...] = dh
        dhb_ref[...] = dh.astype(BF16)
        part = jnp.sum(dn_v * xh, axis=0, keepdims=True)

        @pl.when(i == 0)
        def _():
            dg_ref[...] = part

        @pl.when(i > 0)
        def _():
            dg_ref[...] += part

    row = _row_spec(tm, D_MODEL)
    return pl.pallas_call(
        body, name=name, grid=(t // tm,),
        in_specs=list(a_specs) + [pl.BlockSpec(w.shape, lambda i: (0, 0), pipeline_mode=pl.Buffered(1)), row,
                                  _full_spec((1, D_MODEL)), row],
        out_specs=[row, row, _full_spec((1, D_MODEL))],
        out_shape=[jax.ShapeDtypeStruct((t, D_MODEL), F32), jax.ShapeDtypeStruct((t, D_MODEL), BF16),
                   jax.ShapeDtypeStruct((1, D_MODEL), F32)],
        compiler_params=_cparams(("arbitrary",)),
    )(*a_list, w, h, g, dres)


def _out_loss_and_grad(name, mid, w_out, h, g, target, tm=512):
    t = h.shape[0]

    def body(m_ref, w_ref, h_ref, g_ref, t_ref, dh_ref, dhb_ref, dg_ref, loss_ref):
        i = pl.program_id(0)
        x = h_ref[...] + FFN_RES_WEIGHT * _dot(m_ref[...], w_ref[...])
        gv = g_ref[...]
        r = lax.rsqrt(jnp.mean(x * x, axis=-1, keepdims=True) + NORM_EPS)
        xh = x * r
        err = xh * gv - t_ref[...]
        row = jnp.mean(err * err, axis=-1, keepdims=True)
        part_loss = 0.5 * jnp.sum(row, axis=0, keepdims=True)
        dy = err * (1.0 / D_MODEL)
        dxh = dy * gv
        dh = r * (dxh - xh * jnp.mean(dxh * xh, axis=-1, keepdims=True))
        dh_ref[...] = dh
        dhb_ref[...] = dh.astype(BF16)
        part = jnp.sum(dy * xh, axis=0, keepdims=True)

        @pl.when(i == 0)
        def _():
            dg_ref[...] = part
            loss_ref[...] = jnp.zeros(loss_ref.shape, F32) + part_loss

        @pl.when(i > 0)
        def _():
            dg_ref[...] += part
            loss_ref[...] += part_loss

    return pl.pallas_call(
        body, name=name, grid=(t // tm,),
        in_specs=[_row_spec(tm, D_FF), pl.BlockSpec((D_FF, D_MODEL), lambda i: (0, 0), pipeline_mode=pl.Buffered(1)),
                  _row_spec(tm, D_MODEL), _full_spec((1, D_MODEL)), _row_spec(tm, D_MODEL)],
        out_specs=[_row_spec(tm, D_MODEL), _row_spec(tm, D_MODEL), _full_spec((1, D_MODEL)), _full_spec((8, 128))],
        out_shape=[jax.ShapeDtypeStruct((t, D_MODEL), F32), jax.ShapeDtypeStruct((t, D_MODEL), BF16),
                   jax.ShapeDtypeStruct((1, D_MODEL), F32), jax.ShapeDtypeStruct((8, 128), F32)],
        compiler_params=_cparams(("arbitrary",)),
    )(mid, w_out, h, g, target)


def _rope_tables(t):
    half = HEAD_DIM // 2
    inv_freq = np.float32(ROPE_BASE) ** (-np.arange(half, dtype=np.float32) / np.float32(half))
    ang = (np.arange(t, dtype=np.float32)[:, None] * inv_freq[None, :].astype(np.float32)).astype(np.float32)
    return jnp.asarray(np.cos(ang.astype(np.float64)).astype(np.float32)), jnp.asarray(np.sin(ang.astype(np.float64)).astype(np.float32))


ROPE_HALF = HEAD_DIM // 2
K_SCALE = HEAD_DIM ** -0.5


def _rotate(ref, hh, c, s, scale=None):
    lo, mid, hi = hh * HEAD_DIM, hh * HEAD_DIM + ROPE_HALF, (hh + 1) * HEAD_DIM
    x1, x2 = ref[:, lo:mid].astype(F32), ref[:, mid:hi].astype(F32)
    y = jnp.concatenate([x1 * c - x2 * s, x1 * s + x2 * c], axis=1)
    return y if scale is None else y * scale


def _unrotate_into(ref, hh, dy, c, s, scale=None):
    lo, mid, hi = hh * HEAD_DIM, hh * HEAD_DIM + ROPE_HALF, (hh + 1) * HEAD_DIM
    y1, y2 = dy[:, :ROPE_HALF], dy[:, ROPE_HALF:]
    d1, d2 = y1 * c + y2 * s, y2 * c - y1 * s
    if scale is not None:
        d1, d2 = d1 * scale, d2 * scale
    ref[:, lo:mid] = d1.astype(ref.dtype)
    ref[:, mid:hi] = d2.astype(ref.dtype)


def _retention_tables():
    b, chunk = RET_BLOCK, 64
    gamma = 1.0 - 2.0 ** (-5.0 - np.arange(HEADS, dtype=np.float64))
    log_g = np.log(gamma)[:, None, None]
    i = np.arange(b)[:, None]
    j = np.arange(b)[None, :]
    same = (i // chunk) == (j // chunk)
    earlier = (j // chunk) < (i // chunk)
    expo = np.where(same, np.abs(i - j), np.where(earlier, i - j, 0)).astype(np.float64)
    mask = np.where(same | earlier, 1.0, 0.0)
    dmat = np.exp(log_g * expo[None]) * mask[None]
    qd = np.exp(log_g[:, :, 0] * (np.arange(b)[None, :] + 1.0))
    kd = np.exp(log_g[:, :, 0] * (b - 1.0 - np.arange(b)[None, :]))
    cd = np.exp(log_g[:, :, 0] * b) * np.ones((1, HEAD_DIM))
    as32 = lambda v: jnp.asarray(v.astype(np.float32))
    return (as32(dmat), as32(np.swapaxes(dmat, 1, 2)), as32(qd[:, :, None]), as32(kd[:, :, None]), as32(cd[:, None, :]))


def _dot(a, b, mode="nn"):
    return lax.dot_general(a, b, _DIMS[mode], preferred_element_type=F32)


GRET_BLOCK = 3


def _head_specs(nb, rev=False):
    pos = (lambda n: nb - 1 - n) if rev else (lambda n: n)
    tok = pl.BlockSpec((RET_BLOCK, D_MODEL), lambda n: (pos(n), 0))
    blk = [pl.BlockSpec((RET_BLOCK, D_MODEL), lambda n, b=b: (pos(n), b)) for b in range(GRET_BLOCK + 1)]
    rope = pl.BlockSpec((RET_BLOCK, ROPE_HALF), lambda n: (pos(n), 0))
    tab = _full_spec((HEADS, RET_BLOCK, RET_BLOCK))
    col = _full_spec((HEADS, RET_BLOCK, 1))
    rowv = _full_spec((HEADS, 1, HEAD_DIM))
    st = pl.BlockSpec((HEADS, None, HEAD_DIM, HEAD_DIM), lambda n: (0, pos(n), 0, 0))
    return tok, blk, rope, tab, col, rowv, st


def _retention_fwd(name, proj, cos, sin, tables):
    t = proj.shape[0]
    nb = t // RET_BLOCK
    dmat, _, qd, kd, cd = tables
    tok, blk, rope, tab, col, rowv, st = _head_specs(nb)

    def body(q_ref, k_ref, v_ref, g_ref, c_ref, s_ref, d_ref, qd_ref, kd_ref, cd_ref, o_ref, ret_ref, st_ref, state):
        n = pl.program_id(0)

        @pl.when(n == 0)
        def _():
            state[...] = jnp.zeros(state.shape, F32)

        cs, sn = c_ref[...], s_ref[...]
        for hh in range(HEADS):
            sl = slice(hh * HEAD_DIM, (hh + 1) * HEAD_DIM)
            q, k, v = _rotate(q_ref, hh, cs, sn), _rotate(k_ref, hh, cs, sn, K_SCALE), v_ref[:, sl].astype(BF16)
            s = _dot(q.astype(BF16), k.astype(BF16), "nt") * d_ref[hh]
            stb = state[hh].astype(BF16)
            st_ref[hh] = stb
            o = _dot(s.astype(BF16), v) + _dot((q * qd_ref[hh]).astype(BF16), stb)
            o_ref[:, sl] = o
            rn = o * lax.rsqrt(jnp.mean(o * o, axis=-1, keepdims=True) + NORM_EPS)
            g = g_ref[:, sl].astype(F32)
            ret_ref[:, sl] = (rn * (g * _sigmoid(g))).astype(BF16)
            state[hh] = state[hh] * cd_ref[hh] + _dot((k * kd_ref[hh]).astype(BF16), v, "tn")

    return pl.pallas_call(
        body, name=name, grid=(nb,),
        in_specs=blk + [rope, rope, tab, col, col, rowv],
        out_specs=[tok, tok, st],
        out_shape=[jax.ShapeDtypeStruct((t, D_MODEL), F32), jax.ShapeDtypeStruct((t, D_MODEL), BF16),
                   jax.ShapeDtypeStruct((HEADS, nb, HEAD_DIM, HEAD_DIM), BF16)],
        scratch_shapes=[pltpu.VMEM((HEADS, HEAD_DIM, HEAD_DIM), F32)],
        compiler_params=_cparams(("arbitrary",)),
    )(proj, proj, proj, proj, cos, sin, dmat, qd, kd, cd)


def _retention_bwd(name, dru, w_ru, o, proj, cos, sin, states, tables):
    t = proj.shape[0]
    nb = t // RET_BLOCK
    dmat, dmat_t, qd, kd, cd = tables
    tok, blk, rope, tab, col, rowv, st = _head_specs(nb, rev=True)

    def body(dru_ref, wru_ref, o_ref, q_ref, k_ref, v_ref, g_ref, c_ref, s_ref, st_ref, d_ref, dt_ref, qd_ref, kd_ref, cd_ref,
             dq_ref, dk_ref, dv_ref, dg_ref, gstate):
        n = pl.program_id(0)

        @pl.when(n == 0)
        def _():
            gstate[...] = jnp.zeros(gstate.shape, F32)

        cs, sn = c_ref[...], s_ref[...]
        dret = _dot(dru_ref[...], wru_ref[...], "nt")
        for hh in range(HEADS):
            sl = slice(hh * HEAD_DIM, (hh + 1) * HEAD_DIM)
            o_v, g, dr = o_ref[:, sl], g_ref[:, sl].astype(F32), dret[:, sl]
            sg = _sigmoid(g)
            r = lax.rsqrt(jnp.mean(o_v * o_v, axis=-1, keepdims=True) + NORM_EPS)
            rn = o_v * r
            d_rn = dr * (g * sg)
            dg_ref[:, sl] = (dr * rn * (sg * (1.0 + g * (1.0 - sg)))).astype(BF16)
            d_o = r * (d_rn - rn * jnp.mean(d_rn * rn, axis=-1, keepdims=True))
            dob = d_o.astype(BF16)

            q, k, v = _rotate(q_ref, hh, cs, sn), _rotate(k_ref, hh, cs, sn, K_SCALE), v_ref[:, sl].astype(BF16)
            qb, kb = q.astype(BF16), k.astype(BF16)
            qdv, kdv = qd_ref[hh], kd_ref[hh]
            s_t = (_dot(kb, qb, "nt") * dt_ref[hh]).astype(BF16)
            p_t = (_dot(v, dob, "nt") * dt_ref[hh]).astype(BF16)
            p = (_dot(dob, v, "nt") * d_ref[hh]).astype(BF16)
            stb = st_ref[hh]
            gb = gstate[hh].astype(BF16)
            _unrotate_into(dq_ref, hh, _dot(p, kb) + _dot(dob, stb, "nt") * qdv, cs, sn)
            _unrotate_into(dk_ref, hh, _dot(p_t, qb) + _dot(v, gb, "nt") * kdv, cs, sn, K_SCALE)
            dv_ref[:, sl] = (_dot(s_t, dob) + _dot((k * kdv).astype(BF16), gb)).astype(BF16)
            gstate[hh] = gstate[hh] * cd_ref[hh] + _dot((q * qdv).astype(BF16), dob, "tn")

    return pl.pallas_call(
        body, name=name, grid=(nb,),
        in_specs=[tok, pl.BlockSpec((D_MODEL, D_MODEL), lambda n: (0, 0), pipeline_mode=pl.Buffered(1)), tok] + blk
                 + [rope, rope, st, tab, tab, col, col, rowv],
        out_specs=[tok, tok, tok, tok],
        out_shape=[jax.ShapeDtypeStruct((t, D_MODEL), BF16)] * 4,
        scratch_shapes=[pltpu.VMEM((HEADS, HEAD_DIM, HEAD_DIM), F32)],
        compiler_params=_cparams(("arbitrary",)),
    )(dru, w_ru, o, proj, proj, proj, proj, cos, sin, states, dmat, dmat_t, qd, kd, cd)


POOL_TILE = 256


def _pool_tables():
    b = POOL_TILE
    tt = np.arange(b)[:, None]
    jj = np.arange(b)[None, :]
    cur, prev = [], []
    for w in POOL_WINDOWS:
        cur.append(((tt - jj >= 0) & (tt - jj <= w - 1)).astype(np.float32))
        prev.append((tt - (jj - b) <= w - 1).astype(np.float32))
    cur, prev = np.stack(cur), np.stack(prev)
    as16 = lambda v: jnp.asarray(v, dtype=BF16)
    return as16(cur), as16(prev), as16(np.swapaxes(cur, 1, 2)), as16(np.swapaxes(prev, 1, 2))


def _split2(x):
    hi = x.astype(BF16)
    return hi, (x - hi.astype(F32)).astype(BF16)


POOL_BLOCK = 4


def _pool_count(n, window):
    tpos = n * POOL_TILE + lax.broadcasted_iota(jnp.int32, (POOL_TILE, 1), 0)
    return jnp.minimum(tpos + 1, window).astype(F32)


def _pool_fwd(name, proj, pool_w, scale, tables):
    t = proj.shape[0]
    nb = t // POOL_TILE
    mc, mp, _, _ = tables
    tab = _full_spec((GROUPS, POOL_TILE, POOL_TILE))
    row = _row_spec(POOL_TILE, D_MODEL)

    def body(pc_ref, pp_ref, mc_ref, mp_ref, w_ref, sc_ref, pm_ref, mix_ref, po_ref):
        n = pl.program_id(0)
        for g, window in enumerate(POOL_WINDOWS):
            sl = slice(g * GROUP_DIM, (g + 1) * GROUP_DIM)
            p = pc_ref[:, sl]
            win = _dot(mc_ref[g], p) + jnp.where(n > 0, _dot(mp_ref[g], pp_ref[:, sl]), 0.0)
            pm = (win / _pool_count(n, window) - p.astype(F32)).astype(BF16)
            pm_ref[:, sl] = pm
            mixed = _dot(pm, w_ref[g])
            mix_ref[:, sl] = mixed
            po_ref[:, sl] = (mixed * sc_ref[:, sl]).astype(BF16)

    return pl.pallas_call(
        body, name=name, grid=(nb,),
        in_specs=[_row_spec(POOL_TILE, D_MODEL, POOL_BLOCK),
                  pl.BlockSpec((POOL_TILE, D_MODEL), lambda n: (jnp.maximum(n - 1, 0), POOL_BLOCK)),
                  tab, tab, _full_spec((GROUPS, GROUP_DIM, GROUP_DIM)), _full_spec((1, D_MODEL))],
        out_specs=[row] * 3,
        out_shape=[jax.ShapeDtypeStruct((t, D_MODEL), BF16), jax.ShapeDtypeStruct((t, D_MODEL), F32),
                   jax.ShapeDtypeStruct((t, D_MODEL), BF16)],
        compiler_params=_cparams(("parallel",)),
    )(proj, proj, mc, mp, pool_w, scale)


def _pool_bwd(name, dpu, w_pu, pm, mixed, pool_w, scale, tables):
    t = dpu.shape[0]
    nb = t // POOL_TILE
    _, _, mct, mpt = tables
    cur = pl.BlockSpec((POOL_TILE, D_MODEL), lambda n: (nb - 1 - n, 0))
    tab = _full_spec((GROUPS, POOL_TILE, POOL_TILE))
    wspec = _full_spec((GROUPS, GROUP_DIM, GROUP_DIM))
    sspec = _full_spec((1, D_MODEL))

    def body(dpu_ref, wpu_ref, pm_ref, mix_ref, mct_ref, mpt_ref, w_ref, sc_ref, dp_ref, dw_ref, ds_ref, later):
        n = pl.program_id(0)

        @pl.when(n == 0)
        def _():
            dw_ref[...] = jnp.zeros(dw_ref.shape, F32)
            ds_ref[...] = jnp.zeros(ds_ref.shape, F32)
            later[...] = jnp.zeros(later.shape, F32)

        dpo = _dot(dpu_ref[...], wpu_ref[...], "nt")
        for g, window in enumerate(POOL_WINDOWS):
            sl = slice(g * GROUP_DIM, (g + 1) * GROUP_DIM)
            dc, sc = dpo[:, sl], sc_ref[:, sl]
            dmix = (dc * sc).astype(BF16)
            dpm = _dot(dmix, w_ref[g], "nt")
            e = dpm / _pool_count(nb - 1 - n, window)
            e_hi, e_lo = _split2(e)
            f_hi, f_lo = _split2(later[g])
            mctv, mptv = mct_ref[g], mpt_ref[g]
            back = _dot(mctv, e_hi) + _dot(mctv, e_lo)
            after = _dot(mptv, f_hi) + _dot(mptv, f_lo)
            dp_ref[:, sl] = (back + after - dpm).astype(BF16)
            later[g] = e
            dw_ref[g] += _dot(pm_ref[:, sl], dmix, "tn")
            ds_ref[:, sl] += jnp.sum(dc * mix_ref[:, sl], axis=0, keepdims=True)

    return pl.pallas_call(
        body, name=name, grid=(nb,),
        in_specs=[cur, pl.BlockSpec((D_MODEL, D_MODEL), lambda n: (0, 0), pipeline_mode=pl.Buffered(1)), cur, cur, tab, tab,
                  wspec, sspec],
        out_specs=[cur, wspec, sspec],
        out_shape=[jax.ShapeDtypeStruct((t, D_MODEL), BF16), jax.ShapeDtypeStruct((GROUPS, GROUP_DIM, GROUP_DIM), F32),
                   jax.ShapeDtypeStruct((1, D_MODEL), F32)],
        scratch_shapes=[pltpu.VMEM((GROUPS, POOL_TILE, GROUP_DIM), F32)],
        compiler_params=_cparams(("arbitrary",)),
    )(dpu, w_pu, pm, mixed, mct, mpt, pool_w, scale)


GATE0_BLOCK, GATE1_BLOCK = 5, 6


def _merge_fwd(name, ret, po, w_ru, w_pu, w_out, proj, bias, h, next_g, tm=512):
    t = ret.shape[0]

    def body(r_ref, p_ref, wr_ref, wp_ref, wo_ref, g0_ref, g1_ref, b_ref, h_ref, ng_ref, m_ref, ru_ref, pu_ref, ho_ref, n_ref):
        ru = _dot(r_ref[...], wr_ref[...])
        pu = _dot(p_ref[...], wp_ref[...])
        ru_ref[...] = ru
        pu_ref[...] = pu
        merged = (_sigmoid(g0_ref[...].astype(F32) + b_ref[0:1, :]) * ru
                  + _sigmoid(g1_ref[...].astype(F32) + b_ref[1:2, :]) * pu).astype(BF16)
        m_ref[...] = merged
        h_new = h_ref[...] + _dot(merged, wo_ref[...])
        ho_ref[...] = h_new
        n_ref[...] = _normed(h_new, ng_ref[...]).astype(BF16)

    row = _row_spec(tm, D_MODEL)
    wspec = pl.BlockSpec((D_MODEL, D_MODEL), lambda i: (0, 0), pipeline_mode=pl.Buffered(1))
    return pl.pallas_call(
        body, name=name, grid=(t // tm,),
        in_specs=[row, row, wspec, wspec, wspec, _row_spec(tm, D_MODEL, GATE0_BLOCK), _row_spec(tm, D_MODEL, GATE1_BLOCK),
                  _full_spec((2, D_MODEL)), row, _full_spec((1, D_MODEL))],
        out_specs=[row] * 5,
        out_shape=[jax.ShapeDtypeStruct((t, D_MODEL), BF16), jax.ShapeDtypeStruct((t, D_MODEL), F32),
                   jax.ShapeDtypeStruct((t, D_MODEL), F32), jax.ShapeDtypeStruct((t, D_MODEL), F32),
                   jax.ShapeDtypeStruct((t, D_MODEL), BF16)],
        compiler_params=_cparams(("parallel",)),
    )(ret, po, w_ru, w_pu, w_out, proj, proj, bias, h, next_g)


def _merge_bwd(name, dh_b, w_out, ru, pu, proj, bias, tm=512):
    t = dh_b.shape[0]

    def body(dh_ref, wo_ref, ru_ref, pu_ref, g0_ref, g1_ref, b_ref, dru_ref, dpu_ref, dg0_ref, dg1_ref, db_ref):
        i = pl.program_id(0)
        d = _dot(dh_ref[...], wo_ref[...], "nt")
        s0 = _sigmoid(g0_ref[...].astype(F32) + b_ref[0:1, :])
        s1 = _sigmoid(g1_ref[...].astype(F32) + b_ref[1:2, :])
        dru_ref[...] = (d * s0).astype(BF16)
        dpu_ref[...] = (d * s1).astype(BF16)
        dg0 = d * ru_ref[...] * (s0 * (1.0 - s0))
        dg1 = d * pu_ref[...] * (s1 * (1.0 - s1))
        dg0_ref[...] = dg0.astype(BF16)
        dg1_ref[...] = dg1.astype(BF16)
        part0 = jnp.sum(dg0, axis=0, keepdims=True)
        part1 = jnp.sum(dg1, axis=0, keepdims=True)

        @pl.when(i == 0)
        def _():
            db_ref[0:1, :] = part0
            db_ref[1:2, :] = part1

        @pl.when(i > 0)
        def _():
            db_ref[0:1, :] += part0
            db_ref[1:2, :] += part1

    row = _row_spec(tm, D_MODEL)
    return pl.pallas_call(
        body, name=name, grid=(t // tm,),
        in_specs=[row, pl.BlockSpec((D_MODEL, D_MODEL), lambda i: (0, 0), pipeline_mode=pl.Buffered(1)), row, row,
                  _row_spec(tm, D_MODEL, GATE0_BLOCK), _row_spec(tm, D_MODEL, GATE1_BLOCK), _full_spec((2, D_MODEL))],
        out_specs=[row, row, row, row, _full_spec((2, D_MODEL))],
        out_shape=[jax.ShapeDtypeStruct((t, D_MODEL), BF16)] * 4 + [jax.ShapeDtypeStruct((2, D_MODEL), F32)],
        compiler_params=_cparams(("arbitrary",)),
    )(dh_b, w_out, ru, pu, proj, proj, bias)


def _half_scale(acc):
    return (FFN_RES_WEIGHT * acc,)


def _normed(h, g):
    return h * lax.rsqrt(jnp.mean(h * h, axis=-1, keepdims=True) + NORM_EPS) * g


def _residual_half_norm(acc, res, g):
    h = res + FFN_RES_WEIGHT * acc
    return h, _normed(h, g)


FF_TILE = D_FF // 2
DW_TILE = 256
SAVED_FF_DTYPE = BF16
FF_CHUNKS = ((0, 512), (512, 1024), (1024, FF_TILE))


def _ffn_in(name, nrm, w_in, tm=512):
    t = nrm.shape[0]
    nj = D_FF // FF_TILE

    def body(n_ref, wg_ref, wu_ref, a_ref, mid_ref):
        nv = n_ref[...]
        for c0, c1 in FF_CHUNKS:
            gate = _dot(nv, wg_ref[:, c0:c1])
            up = _dot(nv, wu_ref[:, c0:c1])
            s = _sigmoid(gate)
            silu = gate * s
            a_ref[0, :, c0:c1] = (FFN_RES_WEIGHT * up * (s * (1.0 + gate * (1.0 - s)))).astype(a_ref.dtype)
            a_ref[1, :, c0:c1] = (FFN_RES_WEIGHT * silu).astype(a_ref.dtype)
            mid_ref[:, c0:c1] = (silu * up).astype(BF16)

    return pl.pallas_call(
        body, name=name, grid=(nj, t // tm),
        in_specs=[pl.BlockSpec((tm, D_MODEL), lambda j, i: (i, 0)),
                  pl.BlockSpec((D_MODEL, FF_TILE), lambda j, i: (0, j)),
                  pl.BlockSpec((D_MODEL, FF_TILE), lambda j, i: (0, j + nj))],
        out_specs=[pl.BlockSpec((2, tm, FF_TILE), lambda j, i: (0, i, j)), pl.BlockSpec((tm, FF_TILE), lambda j, i: (i, j))],
        out_shape=[jax.ShapeDtypeStruct((2, t, D_FF), SAVED_FF_DTYPE), jax.ShapeDtypeStruct((t, D_FF), BF16)],
        compiler_params=_cparams(("parallel", "parallel")),
    )(nrm, w_in, w_in)


def _ffn_dact(name, dout_b, w_out, a, tm=512):
    t = dout_b.shape[0]

    def body(d_ref, w_ref, a_ref, da_ref):
        dv = d_ref[...]
        for c0, c1 in FF_CHUNKS:
            dm = _dot(dv, w_ref[c0:c1, :], "nt")
            da_ref[0, :, c0:c1] = (dm * a_ref[0, :, c0:c1].astype(F32)).astype(BF16)
            da_ref[1, :, c0:c1] = (dm * a_ref[1, :, c0:c1].astype(F32)).astype(BF16)

    blk = pl.BlockSpec((2, tm, FF_TILE), lambda j, i: (0, i, j))
    return pl.pallas_call(
        body, name=name, grid=(D_FF // FF_TILE, t // tm),
        in_specs=[pl.BlockSpec((tm, D_MODEL), lambda j, i: (i, 0)), pl.BlockSpec((FF_TILE, D_MODEL), lambda j, i: (j, 0)), blk],
        out_specs=blk,
        out_shape=jax.ShapeDtypeStruct((2, t, D_FF), BF16),
        compiler_params=_cparams(("parallel", "parallel")),
    )(dout_b, w_out, a)


def _ffn_fwd(tag, h, nrm, get_w_in, get_w_out, finish):
    t = h.shape[0]
    w_in = get_w_in(nrm)
    a, mid = _ffn_in(f"{tag}_in", nrm, w_in, tm=min(512, t))
    w_out = get_w_out(mid)
    return finish(mid, w_out), (nrm, a, mid, w_in, w_out)


def _ffn_bwd(tag, h, g, saved, dout, dout_b, on_grads, flush):
    t = h.shape[0]
    nrm, a, mid, w_in, w_out = saved
    d_w_out = _matmul(f"{tag}_dwout", mid, dout_b, "tn", D_FF, D_MODEL, t, DW_TILE, D_MODEL, t, [BF16], epilogue=_half_scale,
                      resident="b")
    da = _ffn_dact(f"{tag}_dact", dout_b, w_out, a, tm=min(512, t))
    nj = D_FF // DW_TILE
    d_w_in = _dw_resident(f"{tag}_dwin", nrm, [da], [pl.BlockSpec((None, t, DW_TILE), lambda s: (s // nj, 0, s % nj))],
                          2 * nj, None, DW_TILE)
    tie = on_grads({f"{tag}_w_in": d_w_in, f"{tag}_w_out": d_w_out})
    tm = min(256, t)
    dh, dh_b, dg = _proj_norm_bwd(f"{tag}_dn", [da], [pl.BlockSpec((2, tm, D_FF), lambda i: (0, i, 0))],
                                  ((0, 0, 0, D_FF), (0, 1, D_FF, 2 * D_FF)), w_in, h, g if tie is None else g + tie, dout, tm)
    return dh, dh_b, dg, flush(dh)


def _dw_resident(name, u, pieces, piece_specs, n_tiles, which_piece, tn):
    t = u.shape[0]
    npc = len(pieces)

    def body(*refs):
        u_ref, p_refs, o_ref, ut_ref = refs[0], refs[1:1 + npc], refs[1 + npc], refs[2 + npc]
        s = pl.program_id(0)

        @pl.when(s == 0)
        def _():
            ut_ref[...] = u_ref[...].T

        if npc == 1:
            o_ref[...] = _dot(ut_ref[...], p_refs[0][...]).astype(BF16)
        for which in range(npc if npc > 1 else 0):
            @pl.when(which_piece(s) == which)
            def _(which=which):
                o_ref[...] = _dot(ut_ref[...], p_refs[which][...]).astype(BF16)

    return pl.pallas_call(
        body, name=name, grid=(n_tiles,),
        in_specs=[pl.BlockSpec((t, D_MODEL), lambda s: (0, 0), pipeline_mode=pl.Buffered(1))] + list(piece_specs),
        out_specs=pl.BlockSpec((D_MODEL, tn), lambda s: (0, s)),
        out_shape=jax.ShapeDtypeStruct((D_MODEL, n_tiles * tn), BF16),
        scratch_shapes=[pltpu.VMEM((D_MODEL, t), BF16)],
        compiler_params=_cparams(("arbitrary",)),
    )(u, *pieces)


def _mix_dwin(name, u, pieces, tn=256):
    t = u.shape[0]
    nj = D_MODEL // tn
    specs = [pl.BlockSpec((t, tn), lambda s, k=k: (0, jnp.clip(s - k * nj, 0, nj - 1))) for k in range(len(pieces))]
    return _dw_resident(name, u, pieces, specs, len(pieces) * nj, lambda s: s // nj, tn)


def _local_step(x, target, vec, get_w, relay_w, on_grads, flush):
    t = x.shape[0]
    cos, sin = _rope_tables(t)
    rtab = _retention_tables()
    ptab = _pool_tables()
    w = {}

    def getter(group, name):
        def get(after):
            if name not in w:
                w.update(get_w(group, after))
            return w[name]
        return get

    nrm1 = _rmsnorm_fwd("ffn1_norm", x, vec["norm_ffn1"])
    def out_and_norm(mid, w_out):
        return _matmul("ffn1_out", mid, w_out, "nn", t, D_MODEL, D_FF, 512, D_MODEL, D_FF, [F32, BF16],
                       extras=(x,), consts=(vec["norm_mix"],), epilogue=_residual_half_norm)

    (h1, u), s1 = _ffn_fwd("ffn1", x, nrm1, getter(0, "ffn1_w_in"), getter(1, "ffn1_w_out"), out_and_norm)
    w.update(get_w(2, u))
    proj = _matmul("mix_in", u, w["w_in"], "nn", t, IN_WIDTH, D_MODEL, 1024, 1024, D_MODEL, [BF16], n_outer=True)
    o, ret, states = _retention_fwd("retention", proj, cos, sin, rtab)
    pm, mixed, po = _pool_fwd("pool", proj, w["pool_w"], vec["pool_scale"], ptab)
    relay_w(3, po)
    merged, ru, pu, h2, nrm2 = _merge_fwd("merge", ret, po, w["w_ret_up"], w["w_pool_up"], w["w_out"], proj, w["gate_bias"],
                                          h1, vec["norm_ffn2"], tm=min(512, t))
    def out_and_loss(mid, w_out):
        return _out_loss_and_grad("ffn2_out_loss", mid, w_out, h2, vec["norm_final"], target, tm=min(512, t))

    (dh3, dh3_b, dg_final, loss), s2 = _ffn_fwd("ffn2", h2, nrm2, getter(3, "ffn2_w_in"), getter(3, "ffn2_w_out"), out_and_loss)

    def tied(v, tie):
        return v if tie is None else v + tie

    dh2, dh2_b, dg_ffn2, tie = _ffn_bwd("ffn2", h2, vec["norm_ffn2"], s2, dh3, dh3_b, on_grads, flush)
    def square_dw(name, act, grad):
        return _matmul(name, act, grad, "tn", D_MODEL, D_MODEL, t, D_MODEL, D_MODEL, 1024, [BF16])

    d_w_out = square_dw("mix_dwout", merged, dh2_b)
    dru, dpu, dg0, dg1, d_bias = _merge_bwd("merge_bwd", dh2_b, w["w_out"], ru, pu, proj, tied(w["gate_bias"], tie))
    d_w_ru = square_dw("mix_dwru", ret, dru)
    d_w_pu = square_dw("mix_dwpu", po, dpu)
    dp, d_pool_w, d_scale = _pool_bwd("pool_bwd", dpu, w["w_pool_up"], pm, mixed, w["pool_w"], vec["pool_scale"], ptab)
    dq, dk, dv, dgr = _retention_bwd("retention_bwd", dru, w["w_ret_up"], o, proj, cos, sin, states, rtab)
    dproj = [dq, dk, dv, dgr, dp, dg0, dg1]
    d_w_in = _mix_dwin("mix_dwin", u, dproj)
    tie = on_grads(dict(w_in=d_w_in, pool_w=d_pool_w.astype(BF16), w_ret_up=d_w_ru, w_pool_up=d_w_pu, w_out=d_w_out))
    tm = min(256, t)
    dh1, dh1_b, dg_mix = _proj_norm_bwd("mix_du", dproj, [_row_spec(tm, D_MODEL)] * len(dproj),
                                        [(k, None, k * D_MODEL, (k + 1) * D_MODEL) for k in range(len(dproj))],
                                        w["w_in"], h1, tied(vec["norm_mix"], tie), dh2, tm)
    tie = flush(dh1)
    dx, _, dg_ffn1, _ = _ffn_bwd("ffn1", x, tied(vec["norm_ffn1"], tie), s1, dh1, dh1_b, on_grads, flush)

    small = dict(norm_ffn1=dg_ffn1, norm_mix=dg_mix, gate_bias=d_bias, pool_scale=d_scale, norm_ffn2=dg_ffn2,
                 norm_final=dg_final)
    return loss[0, 0], dx, small


BIG = ("ffn1_w_in", "ffn1_w_out", "w_in", "pool_w", "w_ret_up", "w_pool_up", "w_out", "ffn2_w_in", "ffn2_w_out")
KIND = dict(ffn1_w_in="col", ffn1_w_out="row", w_in="col", pool_w="pool", w_ret_up="row", w_pool_up="row", w_out="row",
            ffn2_w_in="col", ffn2_w_out="row", gate_bias="col")
ANY = pl.BlockSpec(memory_space=pl.ANY)


def _place():
    x, y, c = lax.axis_index("x"), lax.axis_index("y"), lax.axis_index("c")
    chips = [(1 - x, y), (x, 1 - y), (1 - x, 1 - y)]
    return x, y, c, chips


def _full_view_shape(kind, local_shape):
    if kind == "col":
        return (2, local_shape[0] // 2, N_CHIPS * local_shape[1])
    if kind == "row":
        return (N_CHIPS, 2, local_shape[0] // 2, local_shape[1])
    return (GROUPS, N_CHIPS, 2, local_shape[1] // 2, local_shape[2])


def _local_view(kind, arr):
    if kind == "pool":
        return arr.reshape(GROUPS, 2, arr.shape[1] // 2, arr.shape[2])
    return arr.reshape(2, arr.shape[0] // 2, arr.shape[1])


def _blk(kind, ref, s, c):
    if kind == "col":
        cs = ref.shape[2] // N_CHIPS
        return ref.at[c, :, pl.ds(pl.multiple_of(s * cs, 128), cs)]
    if kind == "row":
        return ref.at[s, c]
    return ref.at[:, s, c]


def _half(kind, ref, c):
    return ref.at[:, c] if kind == "pool" else ref.at[c]


def _shard(kind, ref, s):
    if kind == "col":
        cs = ref.shape[2] // N_CHIPS
        return ref.at[:, :, pl.ds(pl.multiple_of(s * cs, 128), cs)]
    if kind == "row":
        return ref.at[s]
    return ref.at[:, s]


HBM = pl.BlockSpec(memory_space=pltpu.HBM)
SEM = pl.BlockSpec(memory_space=pltpu.SEMAPHORE)
EFFECT = pltpu.SideEffectType.DATAFLOW_SIDE_EFFECTING
WEIGHT_GROUPS = (("gate_bias", "ffn1_w_in"), ("ffn1_w_out",), ("w_in", "pool_w", "w_ret_up", "w_pool_up", "w_out"), ("ffn2_w_in", "ffn2_w_out"))
GRAD_GROUPS = (("ffn2_w_in", "ffn2_w_out"), ("w_in", "pool_w", "w_ret_up", "w_pool_up", "w_out"), ("ffn1_w_in", "ffn1_w_out"))


def _hbm(a):
    return pltpu.with_memory_space_constraint(a, pltpu.HBM)


def _natural(kind, o):
    if kind == "col":
        return o.reshape(o.shape[0] * o.shape[1], o.shape[2])
    if kind == "row":
        return o.reshape(-1, o.shape[3])
    return o.reshape(GROUPS, -1, o.shape[4])


def _ici_copy(kind, loc, full, j, chips, s, c, send_sem, recv_sem):
    px, py = chips[j]
    return (pltpu.make_async_remote_copy(src_ref=_half(kind, loc, c), dst_ref=_blk(kind, full, s, c), send_sem=send_sem,
                                         recv_sem=recv_sem, device_id=(px, py, c), device_id_type=MESH),
            pltpu.make_async_remote_copy(src_ref=_half(kind, loc, c), dst_ref=_blk(kind, full, 2 * px + py, c), send_sem=send_sem,
                                         recv_sem=recv_sem, device_id=(px, py, c), device_id_type=MESH))


def _gather_start(tag, group_ids, shards):
    grps = [WEIGHT_GROUPS[g] for g in group_ids]
    names = [nm for grp in grps for nm in grp]
    kinds = [KIND[nm] for nm in names]
    n, ng = len(names), len(grps)
    locs = [_hbm(_local_view(KIND[nm], shards[nm])) for nm in names]
    lands = [_hbm(lax.empty(_full_view_shape(KIND[nm], shards[nm].shape), shards[nm].dtype)) for nm in names]
    first = np.cumsum([0] + [len(grp) for grp in grps])

    def body(*refs):
        loc, full = refs[:n], refs[n:2 * n]
        send_sems, recv_sems = refs[2 * n:2 * n + ng], refs[2 * n + ng:2 * n + 2 * ng]
        token = refs[-1]
        x, y, c, chips = _place()
        s = 2 * x + y
        for g in range(ng):
            for a in range(first[g], first[g + 1]):
                for j in range(3):
                    k = 3 * (a - first[g]) + j
                    _ici_copy(kinds[a], loc[a], full[a], j, chips, s, c, send_sems[g].at[k], recv_sems[g].at[k])[0].start()
        token[...] = jnp.zeros(token.shape, F32)

    sem_shapes = [pltpu.SemaphoreType.DMA((3 * len(grp),)) for grp in grps]
    outs = pl.pallas_call(
        body, name=f"gather_start_{tag}",
        in_specs=[HBM] * (2 * n),
        out_specs=[SEM] * (2 * ng) + [HBM] * (2 * n) + [pl.BlockSpec(memory_space=pltpu.VMEM)],
        out_shape=sem_shapes + sem_shapes + [pltpu.HBM(a.shape, a.dtype) for a in locs + lands] + [jax.ShapeDtypeStruct((8, 128), F32)],
        input_output_aliases={i: 2 * ng + i for i in range(2 * n)},
        compiler_params=pltpu.CompilerParams(has_side_effects=EFFECT),
    )(*locs, *lands)
    send_sems, recv_sems = outs[:ng], outs[ng:2 * ng]
    locs_t, lands_t = outs[2 * ng:2 * ng + n], outs[2 * ng + n:2 * ng + 2 * n]
    groups = {}
    for k, g in enumerate(group_ids):
        sl = slice(first[k], first[k + 1])
        groups[g] = (send_sems[k], recv_sems[k], list(locs_t[sl]), list(lands_t[sl]))
    return groups, outs[-1]


def _forward_copies(kinds, loc, full, send_sems, recv_sems):
    x, y, c, chips = _place()
    s = 2 * x + y

    def remote(a, k, src, dst):
        return pltpu.make_async_remote_copy(src_ref=src, dst_ref=dst, send_sem=send_sems.at[4 * a + k],
                                            recv_sem=recv_sems.at[4 * a + k], device_id=(x, y, 1 - c), device_id_type=MESH)

    sends, arrivals = [], []
    for a, kind in enumerate(kinds):
        for j, (px, py) in enumerate(chips):
            theirs, from_sib = _blk(kind, full[a], 2 * px + py, c), _blk(kind, full[a], 2 * px + py, 1 - c)
            sends.append(remote(a, j, theirs, theirs))
            arrivals.append(remote(a, j, from_sib, from_sib))
        own = _shard(kind, full[a], s)
        sends.append(remote(a, 3, loc[a], own))
        arrivals.append(remote(a, 3, own, own))
    return sends, arrivals


def _gather_relay(g, group, after):
    names = WEIGHT_GROUPS[g]
    kinds = [KIND[nm] for nm in names]
    m = len(names)
    ici_send, ici_recv, locs, lands = group

    def body(*refs):
        loc, full = refs[:m], refs[m:2 * m]
        ici_s, ici_r = refs[2 * m], refs[2 * m + 1]
        d2d_s, d2d_r = refs[2 * m + 2 + len(after)], refs[2 * m + 3 + len(after)]
        x, y, c, chips = _place()
        for a in range(m):
            for j in range(3):
                sent, landed = _ici_copy(kinds[a], loc[a], full[a], j, chips, 2 * x + y, c, ici_s.at[3 * a + j], ici_r.at[3 * a + j])
                sent.wait_send()
                landed.wait_recv()
        for cp in _forward_copies(kinds, loc, full, d2d_s, d2d_r)[0]:
            cp.start()

    sem_shape = pltpu.SemaphoreType.DMA((4 * m,))
    outs = pl.pallas_call(
        body, name=f"gather_relay_{g}",
        in_specs=[HBM] * (2 * m) + [SEM, SEM] + [ANY] * len(after), out_specs=[SEM, SEM] + [HBM] * (2 * m),
        out_shape=[sem_shape, sem_shape] + [pltpu.HBM(a.shape, a.dtype) for a in locs + lands],
        input_output_aliases={i: 2 + i for i in range(2 * m)},
        compiler_params=pltpu.CompilerParams(has_side_effects=EFFECT),
    )(*locs, *lands, ici_send, ici_recv, *after)
    return outs[0], outs[1], list(outs[2:2 + m]), list(outs[2 + m:2 + 2 * m])


def _gather_land(g, state, after):
    names = WEIGHT_GROUPS[g]
    kinds = [KIND[nm] for nm in names]
    m = len(names)
    d2d_send, d2d_recv, locs, lands = state

    def body(*refs):
        sends, arrivals = _forward_copies(kinds, refs[:m], refs[m:2 * m], refs[2 * m], refs[2 * m + 1])
        for cp in sends:
            cp.wait_send()
        for cp in arrivals:
            cp.wait_recv()

    outs = pl.pallas_call(
        body, name=f"gather_land_{g}",
        in_specs=[HBM] * (2 * m) + [SEM, SEM] + [ANY] * len(after), out_specs=[HBM] * (2 * m),
        out_shape=[pltpu.HBM(a.shape, a.dtype) for a in locs + lands],
        input_output_aliases={i: i for i in range(2 * m)},
        compiler_params=pltpu.CompilerParams(has_side_effects=EFFECT),
    )(*locs, *lands, d2d_send, d2d_recv, *after)
    return {nm: _natural(k, o) for nm, k, o in zip(names, kinds, outs[m:])}


def _gather_finish(g, group, after):
    names = WEIGHT_GROUPS[g]
    kinds = [KIND[nm] for nm in names]
    m = len(names)
    send_sem, recv_sem, locs, lands = group

    def wait_body(*refs):
        loc, full = refs[:m], refs[m:2 * m]
        send_sems, recv_sems = refs[2 * m], refs[2 * m + 1]
        x, y, c, chips = _place()
        s = 2 * x + y
        for a in range(m):
            for j in range(3):
                k = 3 * a + j
                sent, landed = _ici_copy(kinds[a], loc[a], full[a], j, chips, s, c, send_sems.at[k], recv_sems.at[k])
                sent.wait_send()
                landed.wait_recv()

    outs = pl.pallas_call(
        wait_body, name=f"gather_wait_{g}",
        in_specs=[HBM] * (2 * m) + [SEM, SEM] + [ANY] * len(after), out_specs=[HBM] * (2 * m),
        out_shape=[pltpu.HBM(a.shape, a.dtype) for a in locs + lands],
        input_output_aliases={i: i for i in range(2 * m)},
        compiler_params=pltpu.CompilerParams(has_side_effects=EFFECT),
    )(*locs, *lands, send_sem, recv_sem, *after)
    locs, lands = outs[:m], outs[m:]

    def forward_body(*refs):
        sends, arrivals = _forward_copies(kinds, refs[:m], refs[2 * m:3 * m], *refs[3 * m:])
        for cp in sends:
            cp.start()
        for cp in arrivals:
            cp.wait_recv()
        for cp in sends:
            cp.wait_send()

    outs = pl.pallas_call(
        forward_body, name=f"gather_forward_{g}",
        in_specs=[ANY] * (2 * m), out_specs=[ANY] * m,
        out_shape=[jax.ShapeDtypeStruct(a.shape, a.dtype) for a in lands],
        input_output_aliases={m + i: i for i in range(m)},
        scratch_shapes=[pltpu.SemaphoreType.DMA((4 * m,)), pltpu.SemaphoreType.DMA((4 * m,))],
    )(*locs, *lands)
    return {nm: _natural(k, o) for nm, k, o in zip(names, kinds, outs)}


def _grad_view(kind, g):
    if kind == "col":
        return g.reshape(2, g.shape[0] // 2, g.shape[1])
    if kind == "row":
        return g.reshape(N_CHIPS, 2, g.shape[0] // (2 * N_CHIPS), g.shape[1])
    return g.reshape(GROUPS, N_CHIPS, 2, g.shape[1] // (2 * N_CHIPS), g.shape[2])


def _pair_copies(kinds, g, got, send_sems, recv_sems):
    x, y, c, _ = _place()

    def other_half(kind, ref):
        if kind == "col":
            return ref.at[1 - c]
        if kind == "row":
            return ref.at[:, 1 - c]
        return ref.at[:, :, 1 - c]

    return [pltpu.make_async_remote_copy(src_ref=other_half(kinds[a], g[a]), dst_ref=got[a], send_sem=send_sems.at[a],
                                         recv_sem=recv_sems.at[a], device_id=(x, y, 1 - c), device_id_type=MESH)
            for a in range(len(kinds))]


def _pair_exchange_start(tag, names, views):
    kinds = [KIND[nm] for nm in names]
    n = len(names)

    def got_shape(kind, v):
        if kind == "col":
            return v.shape[1:]
        if kind == "row":
            return (v.shape[0],) + v.shape[2:]
        return v.shape[:2] + v.shape[3:]

    srcs = [_hbm(views[nm]) for nm in names]
    lands = [_hbm(lax.empty(got_shape(k, views[nm]), BF16)) for nm, k in zip(names, kinds)]

    def body(*refs):
        g, got = refs[:n], refs[n:2 * n]
        for cp in _pair_copies(kinds, g, got, refs[2 * n], refs[2 * n + 1]):
            cp.start()
        refs[-1][...] = jnp.zeros(refs[-1].shape, F32)

    sem_shape = pltpu.SemaphoreType.DMA((n,))
    outs = pl.pallas_call(
        body, name=f"grad_pair_exchange_start_{tag}",
        in_specs=[HBM] * (2 * n),
        out_specs=[SEM, SEM] + [HBM] * (2 * n) + [pl.BlockSpec(memory_space=pltpu.VMEM)],
        out_shape=[sem_shape, sem_shape] + [pltpu.HBM(a.shape, a.dtype) for a in srcs + lands] + [jax.ShapeDtypeStruct((8, 128), F32)],
        input_output_aliases={i: 2 + i for i in range(2 * n)},
        compiler_params=pltpu.CompilerParams(has_side_effects=EFFECT),
    )(*srcs, *lands)
    return (outs[0], outs[1], list(outs[2:2 + n]), list(outs[2 + n:2 + 2 * n])), outs[-1]


def _pair_exchange_wait(tag, names, state, after):
    kinds = [KIND[nm] for nm in names]
    n = len(names)
    send_sem, recv_sem, srcs, lands = state

    def body(*refs):
        g, got = refs[:n], refs[n:2 * n]
        for cp in _pair_copies(kinds, g, got, refs[2 * n], refs[2 * n + 1]):
            cp.wait_send()
            cp.wait_recv()

    outs = pl.pallas_call(
        body, name=f"grad_pair_exchange_wait_{tag}",
        in_specs=[HBM] * (2 * n) + [SEM, SEM, ANY], out_specs=[HBM] * (2 * n),
        out_shape=[pltpu.HBM(a.shape, a.dtype) for a in srcs + lands],
        input_output_aliases={i: i for i in range(2 * n)},
        compiler_params=pltpu.CompilerParams(has_side_effects=EFFECT),
    )(*srcs, *lands, send_sem, recv_sem, after)
    return dict(zip(names, outs[:n])), dict(zip(names, outs[n:]))


def _pair_sum(name, kind, view, got, c_arr):
    if kind == "col":
        _, rows, cols = view.shape
        tr = 128
        grid = (rows // tr,)
        v_spec = pl.BlockSpec((None, tr, cols), lambda i, c: (c[0], i, 0))
        g_spec = pl.BlockSpec((tr, cols), lambda i, c: (i, 0))
    elif kind == "row":
        _, _, rows, cols = view.shape
        grid = (N_CHIPS,)
        v_spec = pl.BlockSpec((None, None, rows, cols), lambda i, c: (i, c[0], 0, 0))
        g_spec = pl.BlockSpec((None, rows, cols), lambda i, c: (i, 0, 0))
    else:
        _, _, _, rows, cols = view.shape
        grid = (GROUPS,)
        v_spec = pl.BlockSpec((None, N_CHIPS, None, rows, cols), lambda i, c: (i, 0, c[0], 0, 0))
        g_spec = pl.BlockSpec((None, N_CHIPS, rows, cols), lambda i, c: (i, 0, 0, 0))

    def body(c_ref, v_ref, g_ref, o_ref):
        o_ref[...] = (v_ref[...].astype(F32) + g_ref[...].astype(F32)).astype(BF16)

    return pl.pallas_call(
        body, name=name,
        grid_spec=pltpu.PrefetchScalarGridSpec(num_scalar_prefetch=1, grid=grid, in_specs=[v_spec, g_spec], out_specs=g_spec),
        out_shape=jax.ShapeDtypeStruct(got.shape, BF16),
        compiler_params=_cparams(("parallel",)),
    )(c_arr, view, got)


def _piece(kind, ref, s):
    if kind == "col":
        cs = ref.shape[1] // N_CHIPS
        return ref.at[:, pl.ds(pl.multiple_of(s * cs, 128), cs)]
    if kind == "row":
        return ref.at[s]
    return ref.at[:, s]


def _piece_shape(kind, shape):
    if kind == "col":
        return (shape[0], shape[1] // N_CHIPS)
    if kind == "row":
        return shape[1:]
    return (shape[0],) + shape[2:]


def _shard_copies(kinds, p, got, send_sems, recv_sems):
    x, y, c, chips = _place()
    return [pltpu.make_async_remote_copy(src_ref=_piece(kinds[a], p[a], 2 * px + py), dst_ref=got[a].at[j],
                                         send_sem=send_sems.at[3 * a + j], recv_sem=recv_sems.at[3 * a + j],
                                         device_id=(px, py, c), device_id_type=MESH)
            for a in range(len(kinds)) for j, (px, py) in enumerate(chips)]


def _shard_exchange_start(g, names, psums):
    kinds = [KIND[nm] for nm in names]
    n = len(names)
    srcs = [_hbm(psums[nm]) for nm in names]
    lands = [_hbm(lax.empty((3,) + _piece_shape(k, psums[nm].shape), BF16)) for nm, k in zip(names, kinds)]

    def body(*refs):
        p, got = refs[:n], refs[n:2 * n]
        send_sems, recv_sems = refs[2 * n], refs[2 * n + 1]
        token = refs[-1]
        for cp in _shard_copies(kinds, p, got, send_sems, recv_sems):
            cp.start()
        token[...] = jnp.zeros(token.shape, F32)

    sem_shape = pltpu.SemaphoreType.DMA((3 * n,))
    outs = pl.pallas_call(
        body, name=f"grad_shard_exchange_start_{g}",
        in_specs=[HBM] * (2 * n),
        out_specs=[SEM, SEM] + [HBM] * (2 * n) + [pl.BlockSpec(memory_space=pltpu.VMEM)],
        out_shape=[sem_shape, sem_shape] + [pltpu.HBM(a.shape, a.dtype) for a in srcs + lands] + [jax.ShapeDtypeStruct((8, 128), F32)],
        input_output_aliases={i: 2 + i for i in range(2 * n)},
        compiler_params=pltpu.CompilerParams(has_side_effects=EFFECT),
    )(*srcs, *lands)
    return (outs[0], outs[1], list(outs[2:2 + n]), list(outs[2 + n:2 + 2 * n])), outs[-1]


def _shard_exchange_wait(g, names, state, after):
    kinds = [KIND[nm] for nm in names]
    n = len(names)
    send_sem, recv_sem, srcs, lands = state

    def body(*refs):
        p, got = refs[:n], refs[n:2 * n]
        for cp in _shard_copies(kinds, p, got, refs[2 * n], refs[2 * n + 1]):
            cp.wait_send()
            cp.wait_recv()

    outs = pl.pallas_call(
        body, name=f"grad_shard_exchange_wait_{g}",
        in_specs=[HBM] * (2 * n) + [SEM, SEM] + [ANY] * len(after), out_specs=[HBM] * (2 * n),
        out_shape=[pltpu.HBM(a.shape, a.dtype) for a in srcs + lands],
        input_output_aliases={i: i for i in range(2 * n)},
        compiler_params=pltpu.CompilerParams(has_side_effects=EFFECT),
    )(*srcs, *lands, send_sem, recv_sem, *after)
    return dict(zip(names, outs[:n])), dict(zip(names, outs[n:]))


def _shard_sum(name, kind, psum, got, sc_arr):
    if kind == "col":
        rows, cols = psum.shape
        cs = cols // N_CHIPS
        tr = 128
        grid = (rows // tr,)
        p_spec = pl.BlockSpec((tr, cs), lambda i, sc: (i, sc[0]))
        g_spec = pl.BlockSpec((3, tr, cs), lambda i, sc: (0, i, 0))
        o_spec = pl.BlockSpec((None, tr, cs), lambda i, sc: (sc[1], i, 0))
        out_shape = (2, rows, cs)
    elif kind == "row":
        _, rows, cols = psum.shape
        grid = (1,)
        p_spec = pl.BlockSpec((None, rows, cols), lambda i, sc: (sc[0], 0, 0))
        g_spec = pl.BlockSpec((3, rows, cols), lambda i, sc: (0, 0, 0))
        o_spec = pl.BlockSpec((None, rows, cols), lambda i, sc: (sc[1], 0, 0))
        out_shape = (2, rows, cols)
    else:
        _, _, rows, cols = psum.shape
        grid = (1,)
        p_spec = pl.BlockSpec((GROUPS, None, rows, cols), lambda i, sc: (0, sc[0], 0, 0))
        g_spec = pl.BlockSpec((3, GROUPS, rows, cols), lambda i, sc: (0, 0, 0, 0))
        o_spec = pl.BlockSpec((GROUPS, None, rows, cols), lambda i, sc: (0, sc[1], 0, 0))
        out_shape = (GROUPS, 2, rows, cols)

    def body(sc_ref, p_ref, g_ref, o_ref):
        o_ref[...] = ((p_ref[...].astype(F32) + g_ref[0].astype(F32)) + g_ref[1].astype(F32)) + g_ref[2].astype(F32)

    return pl.pallas_call(
        body, name=name,
        grid_spec=pltpu.PrefetchScalarGridSpec(num_scalar_prefetch=1, grid=grid, in_specs=[p_spec, g_spec], out_specs=o_spec),
        out_shape=jax.ShapeDtypeStruct(out_shape, F32),
        compiler_params=_cparams(("parallel",)),
    )(sc_arr, psum, got)


def _half_exchange(tag, names, bufs):
    kinds = [KIND[nm] for nm in names]
    n = len(names)

    def body(*refs):
        out = refs[n:2 * n]
        send_sems, recv_sems = refs[2 * n:]
        x, y, c, _ = _place()
        sib = (x, y, 1 - c)
        cps = []
        for a in range(n):
            mine = _half(kinds[a], out[a], c)
            cp = pltpu.make_async_remote_copy(src_ref=mine, dst_ref=mine, send_sem=send_sems.at[a], recv_sem=recv_sems.at[a],
                                              device_id=sib, device_id_type=MESH)
            cp.start()
            cps.append(cp)
        for a, cp in enumerate(cps):
            cp.wait_send()
            theirs = _half(kinds[a], out[a], 1 - c)
            pltpu.make_async_remote_copy(src_ref=theirs, dst_ref=theirs, send_sem=send_sems.at[a], recv_sem=recv_sems.at[a],
                                         device_id=sib, device_id_type=MESH).wait_recv()

    outs = pl.pallas_call(
        body, name=f"grad_half_exchange_{tag}",
        in_specs=[ANY] * n, out_specs=[ANY] * n,
        out_shape=[jax.ShapeDtypeStruct(bufs[nm].shape, F32) for nm in names],
        input_output_aliases={a: a for a in range(n)},
        scratch_shapes=[pltpu.SemaphoreType.DMA((n,)), pltpu.SemaphoreType.DMA((n,))],
    )(*[bufs[nm] for nm in names])
    return dict(zip(names, outs))


N_DEV = 8
SMALL_ROWS = 8


def _all_reduce_small(name, v):
    def body(v_ref, o_ref, token, buf, send_sems, recv_sems):
        token[...] = jnp.zeros(token.shape, F32)
        x, y, c, _ = _place()
        me = 4 * x + 2 * y + c
        buf[me] = v_ref[...]
        cps = []
        for r in range(1, N_DEV):
            to = (x ^ (r >> 2), y ^ ((r >> 1) & 1), c ^ (r & 1))
            cp = pltpu.make_async_remote_copy(src_ref=v_ref, dst_ref=buf.at[me], send_sem=send_sems.at[r - 1],
                                              recv_sem=recv_sems.at[r - 1], device_id=to, device_id_type=MESH)
            cp.start()
            cps.append(cp)
        for r in range(1, N_DEV):
            pltpu.make_async_remote_copy(src_ref=v_ref, dst_ref=buf.at[me ^ r], send_sem=send_sems.at[r - 1],
                                         recv_sem=recv_sems.at[r - 1], device_id=(x, y, c), device_id_type=MESH).wait_recv()
        for cp in cps:
            cp.wait_send()
        acc = buf[0]
        for d in range(1, N_DEV):
            acc = acc + buf[d]
        o_ref[...] = acc

    vm = pl.BlockSpec(memory_space=pltpu.VMEM)
    return pl.pallas_call(
        body, name=name, in_specs=[vm], out_specs=[vm, vm],
        out_shape=[jax.ShapeDtypeStruct((SMALL_ROWS, D_MODEL), F32), jax.ShapeDtypeStruct((8, 128), F32)],
        scratch_shapes=[pltpu.VMEM((N_DEV, SMALL_ROWS, D_MODEL), F32), pltpu.SemaphoreType.DMA((N_DEV - 1,)),
                        pltpu.SemaphoreType.DMA((N_DEV - 1,))],
    )(v)


def _adamw(name, w, g, m, v):
    rows, cols = w.shape
    tr = next((c for c in (256, 176, 128, 64, 32, 8) if rows % c == 0), rows)
    spec = pl.BlockSpec((tr, cols), lambda i: (i, 0))

    def body(w_ref, g_ref, m_ref, v_ref, d_ref, mo_ref, vo_ref):
        gv = g_ref[...]
        m_new = ADAM_B1 * m_ref[...] + (1.0 - ADAM_B1) * gv
        v_new = ADAM_B2 * v_ref[...] + (1.0 - ADAM_B2) * jnp.square(gv)
        m_hat = m_new / (1.0 - ADAM_B1 ** ADAM_STEP)
        v_hat = v_new / (1.0 - ADAM_B2 ** ADAM_STEP)
        d_ref[...] = -ADAM_LR * (m_hat / (jnp.sqrt(v_hat) + ADAM_EPS) + ADAM_WD * w_ref[...])
        mo_ref[...] = m_new
        vo_ref[...] = v_new

    return pl.pallas_call(
        body, name=name, grid=(rows // tr,),
        in_specs=[spec] * 4, out_specs=[spec] * 3,
        out_shape=[jax.ShapeDtypeStruct((rows, cols), F32)] * 3,
        compiler_params=_cparams(("parallel",)),
    )(w, g, m, v)


WEIGHTS = ("norm_ffn1", "ffn1_w_in", "ffn1_w_out", "norm_mix", "w_in", "gate_bias", "pool_w", "pool_scale", "w_ret_up",
           "w_pool_up", "w_out", "norm_ffn2", "ffn2_w_in", "ffn2_w_out", "norm_final")
SMALL_ROW = dict(norm_ffn1=0, norm_mix=1, gate_bias=2, pool_scale=4, norm_ffn2=5, norm_final=6)


def _as2d(a):
    return a.reshape(-1, a.shape[-1])


def kernel(x, norm_ffn1, ffn1_w_in, ffn1_w_out, norm_mix, w_in, gate_bias, pool_w, pool_scale, w_ret_up, w_pool_up, w_out, norm_ffn2, ffn2_w_in, ffn2_w_out, norm_final, loss_target, m_norm_ffn1, m_ffn1_w_in, m_ffn1_w_out, m_norm_mix, m_w_in, m_gate_bias, m_pool_w, m_pool_scale, m_w_ret_up, m_w_pool_up, m_w_out, m_norm_ffn2, m_ffn2_w_in, m_ffn2_w_out, m_norm_final, v_norm_ffn1, v_ffn1_w_in, v_ffn1_w_out, v_norm_mix, v_w_in, v_gate_bias, v_pool_w, v_pool_scale, v_w_ret_up, v_w_pool_up, v_w_out, v_norm_ffn2, v_ffn2_w_in, v_ffn2_w_out, v_norm_final):
    wt = dict(norm_ffn1=norm_ffn1, ffn1_w_in=ffn1_w_in, ffn1_w_out=ffn1_w_out, norm_mix=norm_mix, w_in=w_in, gate_bias=gate_bias,
              pool_w=pool_w, pool_scale=pool_scale, w_ret_up=w_ret_up, w_pool_up=w_pool_up, w_out=w_out, norm_ffn2=norm_ffn2,
              ffn2_w_in=ffn2_w_in, ffn2_w_out=ffn2_w_out, norm_final=norm_final)
    mom = dict(norm_ffn1=m_norm_ffn1, ffn1_w_in=m_ffn1_w_in, ffn1_w_out=m_ffn1_w_out, norm_mix=m_norm_mix, w_in=m_w_in,
               gate_bias=m_gate_bias, pool_w=m_pool_w, pool_scale=m_pool_scale, w_ret_up=m_w_ret_up, w_pool_up=m_w_pool_up,
               w_out=m_w_out, norm_ffn2=m_norm_ffn2, ffn2_w_in=m_ffn2_w_in, ffn2_w_out=m_ffn2_w_out, norm_final=m_norm_final)
    var = dict(norm_ffn1=v_norm_ffn1, ffn1_w_in=v_ffn1_w_in, ffn1_w_out=v_ffn1_w_out, norm_mix=v_norm_mix, w_in=v_w_in,
               gate_bias=v_gate_bias, pool_w=v_pool_w, pool_scale=v_pool_scale, w_ret_up=v_w_ret_up, w_pool_up=v_w_pool_up,
               w_out=v_w_out, norm_ffn2=v_norm_ffn2, ffn2_w_in=v_ffn2_w_in, ffn2_w_out=v_ffn2_w_out, norm_final=v_norm_final)

    ax, ay, ac = lax.axis_index("x"), lax.axis_index("y"), lax.axis_index("c")
    chip = 2 * ax + ay
    c_arr = jnp.reshape(ac, (1,)).astype(jnp.int32)
    sc_arr = jnp.stack([chip, ac]).astype(jnp.int32)
    bias_cols = gate_bias.shape[-1]

    first = {"gate_bias": gate_bias[0], "ffn1_w_in": ffn1_w_in[0].astype(BF16)}
    gather_groups, token = _gather_start("first", [0], first)
    rest, rest_token = _gather_start("rest", [1, 2, 3],
                                     {nm: wt[nm][0].astype(BF16) + token[0, 0].astype(BF16) for nm in BIG if nm not in first})
    gather_groups.update(rest)
    vec = dict(norm_ffn1=norm_ffn1, norm_mix=norm_mix, norm_ffn2=norm_ffn2, pool_scale=pool_scale,
               norm_final=norm_final.reshape(1, D_MODEL))

    relayed = {}

    def relay_w(g, after):
        relayed[g] = _gather_relay(g, gather_groups[g], (after,))

    def get_w(g, after):
        if g in relayed:
            return _gather_land(g, relayed[g], (after,))
        return _gather_finish(g, gather_groups[g], (after, rest_token) if g == 0 else (after,))

    pairs, pending = [], []

    def on_grads(gr):
        g = len(pairs)
        names = GRAD_GROUPS[g]
        assert set(names) == set(gr), (names, list(gr))
        state, token = _pair_exchange_start(g, names, {nm: _grad_view(KIND[nm], gr[nm]) for nm in names})
        pairs.append(state)
        return token[0:1, 0:1]

    def flush(after):
        g = len(pending)
        names = GRAD_GROUPS[g]
        views, from_sib = _pair_exchange_wait(g, names, pairs[g], after)
        psums = {nm: _pair_sum(f"pair_sum_{nm}", KIND[nm], views[nm], from_sib[nm], c_arr) for nm in names}
        state, token = _shard_exchange_start(g, names, psums)
        pending.append(state)
        tokens.append(token)
        return token[0:1, 0:1]

    tokens = []
    loss_local, dx, small = _local_step(x[0], loss_target[0], vec, get_w, relay_w, on_grads, flush)

    packed = jnp.concatenate([small["norm_ffn1"], small["norm_mix"], small["gate_bias"], small["pool_scale"],
                              small["norm_ffn2"], small["norm_final"], jnp.broadcast_to(loss_local, (1, D_MODEL))], axis=0)
    small_sum, _ = _all_reduce_small("reduce_small_grads", packed)
    loss = small_sum[SMALL_ROWS - 1, 0]
    grads, delta, new_m, new_v = {}, {}, {}, {}

    def adamw(nm):
        shape = wt[nm].shape
        d, m2, v2 = _adamw(f"adamw_{nm}", _as2d(wt[nm]), _as2d(grads[nm]), _as2d(mom[nm]), _as2d(var[nm]))
        delta[nm], new_m[nm], new_v[nm] = d.reshape(shape), m2.reshape(shape), v2.reshape(shape)
        return d

    for nm in ("norm_ffn1", "norm_mix", "pool_scale", "norm_ffn2"):
        grads[nm] = small_sum[SMALL_ROW[nm]][None, :]
    grads["norm_final"] = small_sum[SMALL_ROW["norm_final"]]
    grads["gate_bias"] = lax.dynamic_slice(small_sum, (SMALL_ROW["gate_bias"], chip * bias_cols), (2, bias_cols))[None]
    after = (tokens[-1],)
    for g, names in enumerate(GRAD_GROUPS):
        psums, from_chips = _shard_exchange_wait(g, names, pending[g], after)
        bufs = {nm: _shard_sum(f"shard_sum_{nm}", KIND[nm], psums[nm], from_chips[nm], sc_arr) for nm in names}
        reduced = _half_exchange(g, names, bufs)
        for nm in names:
            grads[nm] = reduced[nm].reshape(wt[nm].shape)
        after = tuple(adamw(nm) for nm in names)
    for nm in WEIGHTS:
        if nm not in delta:
            adamw(nm)

    return (loss, dx[None], *[grads[nm] for nm in WEIGHTS], *[delta[nm] for nm in WEIGHTS],
            *[new_m[nm] for nm in WEIGHTS], *[new_v[nm] for nm in WEIGHTS])
```

```python
import functools

import numpy as np
import jax
import jax.numpy as jnp
from jax import lax
from jax.experimental import pallas as pl
from jax.experimental.pallas import tpu as pltpu

F32 = jnp.float32
BF16 = jnp.bfloat16
MESH = pl.DeviceIdType.MESH

D_MODEL = 1024
D_FF = 2816
HEADS = 4
HEAD_DIM = 256
GROUPS = 4
GROUP_DIM = 256
POOL_WINDOWS = (2, 4, 8, 16)
IN_WIDTH = 7 * D_MODEL
ROPE_BASE = 10000.0
NORM_EPS = 1e-6
FFN_RES_WEIGHT = 0.5
ADAM_LR, ADAM_B1, ADAM_B2, ADAM_EPS, ADAM_WD, ADAM_STEP = 0.001, 0.9, 0.999, 1e-08, 0.01, 10

N_CHIPS = 4
RET_BLOCK = 256
V7X_VMEM_LIMIT = 48 * 1024 * 1024


def _cparams(sem):
    return pltpu.CompilerParams(dimension_semantics=sem, vmem_limit_bytes=V7X_VMEM_LIMIT)


def _sigmoid(x):
    return jax.nn.sigmoid(x)


_DIMS = {"nn": (((1,), (0,)), ((), ())), "nt": (((1,), (1,)), ((), ())), "tn": (((0,), (0,)), ((), ()))}


def _matmul(name, a, b, mode, m, n, k, tm, tn, tk, out_dtypes, a_spec=None, b_spec=None, extras=(), consts=(), epilogue=None,
            resident=None, n_outer=False):
    tm, tn, tk = min(tm, m), min(tn, n), min(tk, k)
    gi, gj, gk = m // tm, n // tn, k // tk
    assert gi * tm == m and gj * tn == n and gk * tk == k, (name, m, n, k, tm, tn, tk)
    assert not (n_outer and (a_spec is not None or b_spec is not None)), name
    once = dict(pipeline_mode=pl.Buffered(1))

    def spec(shape, index, **kw):
        return pl.BlockSpec(shape, (lambda j, i, kk: index(i, j, kk)) if n_outer else index, **kw)

    if a_spec is None:
        kw = once if resident == "a" else {}
        a_spec = (spec((tk, tm), lambda i, j, kk: (kk, i), **kw) if mode == "tn"
                  else spec((tm, tk), lambda i, j, kk: (i, kk), **kw))
    if b_spec is None:
        kw = once if resident == "b" else {}
        b_spec = (spec((tn, tk), lambda i, j, kk: (j, kk), **kw) if mode == "nt"
                  else spec((tk, tn), lambda i, j, kk: (kk, j), **kw))
    n_ex, n_out = len(extras) + len(consts), len(out_dtypes)
    dims = _DIMS[mode]

    def body(a_ref, b_ref, *rest):
        ex_refs, out_refs = rest[:n_ex], rest[n_ex:n_ex + n_out]

        def finish(acc):
            outs = (acc,) if epilogue is None else epilogue(acc, *[e[...] for e in ex_refs])
            for o_ref, o in zip(out_refs, outs):
                o_ref[...] = o.astype(o_ref.dtype)

        prod = lax.dot_general(a_ref[...], b_ref[...], dims, preferred_element_type=F32)
        if gk == 1:
            finish(prod)
        else:
            acc_ref = rest[n_ex + n_out]
            kk = pl.program_id(2)

            @pl.when(kk == 0)
            def _():
                acc_ref[...] = prod

            @pl.when(kk > 0)
            def _():
                acc_ref[...] += prod

            @pl.when(kk == gk - 1)
            def _():
                finish(acc_ref[...])

    o_spec = spec((tm, tn), lambda i, j, kk: (i, j))
    outs = pl.pallas_call(
        body, name=name, grid=(gj, gi, gk) if n_outer else (gi, gj, gk),
        in_specs=[a_spec, b_spec] + [o_spec] * len(extras) + [spec((1, tn), lambda i, j, kk: (0, j))] * len(consts),
        out_specs=[o_spec] * n_out,
        out_shape=[jax.ShapeDtypeStruct((m, n), dt) for dt in out_dtypes],
        scratch_shapes=[pltpu.VMEM((tm, tn), F32)] if gk > 1 else [],
        compiler_params=_cparams(("parallel", "parallel", "arbitrary")),
    )(a, b, *extras, *consts)
    return outs[0] if n_out == 1 else outs


def _row_spec(tm, width, col_block=0):
    return pl.BlockSpec((tm, width), lambda i: (i, col_block))


def _full_spec(shape):
    return pl.BlockSpec(shape, lambda *_: (0,) * len(shape))


def _rmsnorm_fwd(name, h, g, tm=512):
    t = h.shape[0]

    def body(h_ref, g_ref, o_ref):
        x = h_ref[...]
        r = lax.rsqrt(jnp.mean(x * x, axis=-1, keepdims=True) + NORM_EPS)
        o_ref[...] = (x * r * g_ref[...]).astype(BF16)

    return pl.pallas_call(
        body, name=name, grid=(t // tm,),
        in_specs=[_row_spec(tm, D_MODEL), _full_spec((1, D_MODEL))],
        out_specs=_row_spec(tm, D_MODEL),
        out_shape=jax.ShapeDtypeStruct((t, D_MODEL), BF16),
        compiler_params=_cparams(("parallel",)),
    )(h, g)


def _proj_norm_bwd(name, a_list, a_specs, parts, w, h, g, dres, tm):
    t = h.shape[0]
    na = len(a_list)

    def body(*refs):
        a_refs = refs[:na]
        w_ref, h_ref, g_ref, dres_ref, dh_ref, dhb_ref, dg_ref = refs[na:]
        i = pl.program_id(0)
        dn_v = None
        for which, lead, k0, k1 in parts:
            a_ref = a_refs[which]
            term = _dot(a_ref[...] if lead is None else a_ref[lead], w_ref[:, k0:k1], "nt")
            dn_v = term if dn_v is None else dn_v + term
        x = h_ref[...]
        r = lax.rsqrt(jnp.mean(x * x, axis=-1, keepdims=True) + NORM_EPS)
        xh = x * r
        dxh = dn_v * g_ref[...]
        dh = dres_ref[...] + r * (dxh - xh * jnp.mean(dxh * xh, axis=-1, keepdims=True))
        dh_ref[...] = dh
        dhb_ref[...] = dh.astype(BF16)
        part = jnp.sum(dn_v * xh, axis=0, keepdims=True)

        @pl.when(i == 0)
        def _():
            dg_ref[...] = part

        @pl.when(i > 0)
        def _():
            dg_ref[...] += part

    row = _row_spec(tm, D_MODEL)
    return pl.pallas_call(
        body, name=name, grid=(t // tm,),
        in_specs=list(a_specs) + [pl.BlockSpec(w.shape, lambda i: (0, 0), pipeline_mode=pl.Buffered(1)), row,
                                  _full_spec((1, D_MODEL)), row],
        out_specs=[row, row, _full_spec((1, D_MODEL))],
        out_shape=[jax.ShapeDtypeStruct((t, D_MODEL), F32), jax.ShapeDtypeStruct((t, D_MODEL), BF16),
                   jax.ShapeDtypeStruct((1, D_MODEL), F32)],
        compiler_params=_cparams(("arbitrary",)),
    )(*a_list, w, h, g, dres)


def _out_loss_and_grad(name, mid, w_out, h, g, target, tm=512):
    t = h.shape[0]

    def body(m_ref, w_ref, h_ref, g_ref, t_ref, dh_ref, dhb_ref, dg_ref, loss_ref):
        i = pl.program_id(0)
        x = h_ref[...] + FFN_RES_WEIGHT * _dot(m_ref[...], w_ref[...])
        gv = g_ref[...]
        r = lax.rsqrt(jnp.mean(x * x, axis=-1, keepdims=True) + NORM_EPS)
        xh = x * r
        err = xh * gv - t_ref[...]
        row = jnp.mean(err * err, axis=-1, keepdims=True)
        part_loss = 0.5 * jnp.sum(row, axis=0, keepdims=True)
        dy = err * (1.0 / D_MODEL)
        dxh = dy * gv
        dh = r * (dxh - xh * jnp.mean(dxh * xh, axis=-1, keepdims=True))
        dh_ref[...] = dh
        dhb_ref[...] = dh.astype(BF16)
        part = jnp.sum(dy * xh, axis=0, keepdims=True)

        @pl.when(i == 0)
        def _():
            dg_ref[...] = part
            loss_ref[...] = jnp.zeros(loss_ref.shape, F32) + part_loss

        @pl.when(i > 0)
        def _():
            dg_ref[...] += part
            loss_ref[...] += part_loss

    return pl.pallas_call(
        body, name=name, grid=(t // tm,),
        in_specs=[_row_spec(tm, D_FF), pl.BlockSpec((D_FF, D_MODEL), lambda i: (0, 0), pipeline_mode=pl.Buffered(1)),
                  _row_spec(tm, D_MODEL), _full_spec((1, D_MODEL)), _row_spec(tm, D_MODEL)],
        out_specs=[_row_spec(tm, D_MODEL), _row_spec(tm, D_MODEL), _full_spec((1, D_MODEL)), _full_spec((8, 128))],
        out_shape=[jax.ShapeDtypeStruct((t, D_MODEL), F32), jax.ShapeDtypeStruct((t, D_MODEL), BF16),
                   jax.ShapeDtypeStruct((1, D_MODEL), F32), jax.ShapeDtypeStruct((8, 128), F32)],
        compiler_params=_cparams(("arbitrary",)),
    )(mid, w_out, h, g, target)


def _rope_tables(t):
    half = HEAD_DIM // 2
    inv_freq = np.float32(ROPE_BASE) ** (-np.arange(half, dtype=np.float32) / np.float32(half))
    ang = (np.arange(t, dtype=np.float32)[:, None] * inv_freq[None, :].astype(np.float32)).astype(np.float32)
    return jnp.asarray(np.cos(ang.astype(np.float64)).astype(np.float32)), jnp.asarray(np.sin(ang.astype(np.float64)).astype(np.float32))


ROPE_HALF = HEAD_DIM // 2
K_SCALE = HEAD_DIM ** -0.5


def _rotate(ref, hh, c, s, scale=None):
    lo, mid, hi = hh * HEAD_DIM, hh * HEAD_DIM + ROPE_HALF, (hh + 1) * HEAD_DIM
    x1, x2 = ref[:, lo:mid].astype(F32), ref[:, mid:hi].astype(F32)
    y = jnp.concatenate([x1 * c - x2 * s, x1 * s + x2 * c], axis=1)
    return y if scale is None else y * scale


def _unrotate_into(ref, hh, dy, c, s, scale=None):
    lo, mid, hi = hh * HEAD_DIM, hh * HEAD_DIM + ROPE_HALF, (hh + 1) * HEAD_DIM
    y1, y2 = dy[:, :ROPE_HALF], dy[:, ROPE_HALF:]
    d1, d2 = y1 * c + y2 * s, y2 * c - y1 * s
    if scale is not None:
        d1, d2 = d1 * scale, d2 * scale
    ref[:, lo:mid] = d1.astype(ref.dtype)
    ref[:, mid:hi] = d2.astype(ref.dtype)


def _retention_tables():
    b, chunk = RET_BLOCK, 64
    gamma = 1.0 - 2.0 ** (-5.0 - np.arange(HEADS, dtype=np.float64))
    log_g = np.log(gamma)[:, None, None]
    i = np.arange(b)[:, None]
    j = np.arange(b)[None, :]
    same = (i // chunk) == (j // chunk)
    earlier = (j // chunk) < (i // chunk)
    expo = np.where(same, np.abs(i - j), np.where(earlier, i - j, 0)).astype(np.float64)
    mask = np.where(same | earlier, 1.0, 0.0)
    dmat = np.exp(log_g * expo[None]) * mask[None]
    qd = np.exp(log_g[:, :, 0] * (np.arange(b)[None, :] + 1.0))
    kd = np.exp(log_g[:, :, 0] * (b - 1.0 - np.arange(b)[None, :]))
    cd = np.exp(log_g[:, :, 0] * b) * np.ones((1, HEAD_DIM))
    as32 = lambda v: jnp.asarray(v.astype(np.float32))
    return (as32(dmat), as32(np.swapaxes(dmat, 1, 2)), as32(qd[:, :, None]), as32(kd[:, :, None]), as32(cd[:, None, :]))


def _dot(a, b, mode="nn"):
    return lax.dot_general(a, b, _DIMS[mode], preferred_element_type=F32)


GRET_BLOCK = 3


def _head_specs(nb, rev=False):
    pos = (lambda n: nb - 1 - n) if rev else (lambda n: n)
    tok = pl.BlockSpec((RET_BLOCK, D_MODEL), lambda n: (pos(n), 0))
    blk = [pl.BlockSpec((RET_BLOCK, D_MODEL), lambda n, b=b: (pos(n), b)) for b in range(GRET_BLOCK + 1)]
    rope = pl.BlockSpec((RET_BLOCK, ROPE_HALF), lambda n: (pos(n), 0))
    tab = _full_spec((HEADS, RET_BLOCK, RET_BLOCK))
    col = _full_spec((HEADS, RET_BLOCK, 1))
    rowv = _full_spec((HEADS, 1, HEAD_DIM))
    st = pl.BlockSpec((HEADS, None, HEAD_DIM, HEAD_DIM), lambda n: (0, pos(n), 0, 0))
    return tok, blk, rope, tab, col, rowv, st


def _retention_fwd(name, proj, cos, sin, tables):
    t = proj.shape[0]
    nb = t // RET_BLOCK
    dmat, _, qd, kd, cd = tables
    tok, blk, rope, tab, col, rowv, st = _head_specs(nb)

    def body(q_ref, k_ref, v_ref, g_ref, c_ref, s_ref, d_ref, qd_ref, kd_ref, cd_ref, o_ref, ret_ref, st_ref, state):
        n = pl.program_id(0)

        @pl.when(n == 0)
        def _():
            state[...] = jnp.zeros(state.shape, F32)

        cs, sn = c_ref[...], s_ref[...]
        for hh in range(HEADS):
            sl = slice(hh * HEAD_DIM, (hh + 1) * HEAD_DIM)
            q, k, v = _rotate(q_ref, hh, cs, sn), _rotate(k_ref, hh, cs, sn, K_SCALE), v_ref[:, sl].astype(BF16)
            s = _dot(q.astype(BF16), k.astype(BF16), "nt") * d_ref[hh]
            stb = state[hh].astype(BF16)
            st_ref[hh] = stb
            o = _dot(s.astype(BF16), v) + _dot((q * qd_ref[hh]).astype(BF16), stb)
            o_ref[:, sl] = o
            rn = o * lax.rsqrt(jnp.mean(o * o, axis=-1, keepdims=True) + NORM_EPS)
            g = g_ref[:, sl].astype(F32)
            ret_ref[:, sl] = (rn * (g * _sigmoid(g))).astype(BF16)
            state[hh] = state[hh] * cd_ref[hh] + _dot((k * kd_ref[hh]).astype(BF16), v, "tn")

    return pl.pallas_call(
        body, name=name, grid=(nb,),
        in_specs=blk + [rope, rope, tab, col, col, rowv],
        out_specs=[tok, tok, st],
        out_shape=[jax.ShapeDtypeStruct((t, D_MODEL), F32), jax.ShapeDtypeStruct((t, D_MODEL), BF16),
                   jax.ShapeDtypeStruct((HEADS, nb, HEAD_DIM, HEAD_DIM), BF16)],
        scratch_shapes=[pltpu.VMEM((HEADS, HEAD_DIM, HEAD_DIM), F32)],
        compiler_params=_cparams(("arbitrary",)),
    )(proj, proj, proj, proj, cos, sin, dmat, qd, kd, cd)


def _retention_bwd(name, dru, w_ru, o, proj, cos, sin, states, tables):
    t = proj.shape[0]
    nb = t // RET_BLOCK
    dmat, dmat_t, qd, kd, cd = tables
    tok, blk, rope, tab, col, rowv, st = _head_specs(nb, rev=True)

    def body(dru_ref, wru_ref, o_ref, q_ref, k_ref, v_ref, g_ref, c_ref, s_ref, st_ref, d_ref, dt_ref, qd_ref, kd_ref, cd_ref,
             dq_ref, dk_ref, dv_ref, dg_ref, gstate):
        n = pl.program_id(0)

        @pl.when(n == 0)
        def _():
            gstate[...] = jnp.zeros(gstate.shape, F32)

        cs, sn = c_ref[...], s_ref[...]
        dret = _dot(dru_ref[...], wru_ref[...], "nt")
        for hh in range(HEADS):
            sl = slice(hh * HEAD_DIM, (hh + 1) * HEAD_DIM)
            o_v, g, dr = o_ref[:, sl], g_ref[:, sl].astype(F32), dret[:, sl]
            sg = _sigmoid(g)
            r = lax.rsqrt(jnp.mean(o_v * o_v, axis=-1, keepdims=True) + NORM_EPS)
            rn = o_v * r
            d_rn = dr * (g * sg)
            dg_ref[:, sl] = (dr * rn * (sg * (1.0 + g * (1.0 - sg)))).astype(BF16)
            d_o = r * (d_rn - rn * jnp.mean(d_rn * rn, axis=-1, keepdims=True))
            dob = d_o.astype(BF16)

            q, k, v = _rotate(q_ref, hh, cs, sn), _rotate(k_ref, hh, cs, sn, K_SCALE), v_ref[:, sl].astype(BF16)
            qb, kb = q.astype(BF16), k.astype(BF16)
            qdv, kdv = qd_ref[hh], kd_ref[hh]
            s_t = (_dot(kb, qb, "nt") * dt_ref[hh]).astype(BF16)
            p_t = (_dot(v, dob, "nt") * dt_ref[hh]).astype(BF16)
            p = (_dot(dob, v, "nt") * d_ref[hh]).astype(BF16)
            stb = st_ref[hh]
            gb = gstate[hh].astype(BF16)
            _unrotate_into(dq_ref, hh, _dot(p, kb) + _dot(dob, stb, "nt") * qdv, cs, sn)
            _unrotate_into(dk_ref, hh, _dot(p_t, qb) + _dot(v, gb, "nt") * kdv, cs, sn, K_SCALE)
            dv_ref[:, sl] = (_dot(s_t, dob) + _dot((k * kdv).astype(BF16), gb)).astype(BF16)
            gstate[hh] = gstate[hh] * cd_ref[hh] + _dot((q * qdv).astype(BF16), dob, "tn")

    return pl.pallas_call(
        body, name=name, grid=(nb,),
        in_specs=[tok, pl.BlockSpec((D_MODEL, D_MODEL), lambda n: (0, 0), pipeline_mode=pl.Buffered(1)), tok] + blk
                 + [rope, rope, st, tab, tab, col, col, rowv],
        out_specs=[tok, tok, tok, tok],
        out_shape=[jax.ShapeDtypeStruct((t, D_MODEL), BF16)] * 4,
        scratch_shapes=[pltpu.VMEM((HEADS, HEAD_DIM, HEAD_DIM), F32)],
        compiler_params=_cparams(("arbitrary",)),
    )(dru, w_ru, o, proj, proj, proj, proj, cos, sin, states, dmat, dmat_t, qd, kd, cd)


POOL_TILE = 256


def _pool_tables():
    b = POOL_TILE
    tt = np.arange(b)[:, None]
    jj = np.arange(b)[None, :]
    cur, prev = [], []
    for w in POOL_WINDOWS:
        cur.append(((tt - jj >= 0) & (tt - jj <= w - 1)).astype(np.float32))
        prev.append((tt - (jj - b) <= w - 1).astype(np.float32))
    cur, prev = np.stack(cur), np.stack(prev)
    as16 = lambda v: jnp.asarray(v, dtype=BF16)
    return as16(cur), as16(prev), as16(np.swapaxes(cur, 1, 2)), as16(np.swapaxes(prev, 1, 2))


def _split2(x):
    hi = x.astype(BF16)
    return hi, (x - hi.astype(F32)).astype(BF16)


POOL_BLOCK = 4


def _pool_count(n, window):
    tpos = n * POOL_TILE + lax.broadcasted_iota(jnp.int32, (POOL_TILE, 1), 0)
    return jnp.minimum(tpos + 1, window).astype(F32)


def _pool_fwd(name, proj, pool_w, scale, tables):
    t = proj.shape[0]
    nb = t // POOL_TILE
    mc, mp, _, _ = tables
    tab = _full_spec((GROUPS, POOL_TILE, POOL_TILE))
    row = _row_spec(POOL_TILE, D_MODEL)

    def body(pc_ref, pp_ref, mc_ref, mp_ref, w_ref, sc_ref, pm_ref, mix_ref, po_ref):
        n = pl.program_id(0)
        for g, window in enumerate(POOL_WINDOWS):
            sl = slice(g * GROUP_DIM, (g + 1) * GROUP_DIM)
            p = pc_ref[:, sl]
            win = _dot(mc_ref[g], p) + jnp.where(n > 0, _dot(mp_ref[g], pp_ref[:, sl]), 0.0)
            pm = (win / _pool_count(n, window) - p.astype(F32)).astype(BF16)
            pm_ref[:, sl] = pm
            mixed = _dot(pm, w_ref[g])
            mix_ref[:, sl] = mixed
            po_ref[:, sl] = (mixed * sc_ref[:, sl]).astype(BF16)

    return pl.pallas_call(
        body, name=name, grid=(nb,),
        in_specs=[_row_spec(POOL_TILE, D_MODEL, POOL_BLOCK),
                  pl.BlockSpec((POOL_TILE, D_MODEL), lambda n: (jnp.maximum(n - 1, 0), POOL_BLOCK)),
                  tab, tab, _full_spec((GROUPS, GROUP_DIM, GROUP_DIM)), _full_spec((1, D_MODEL))],
        out_specs=[row] * 3,
        out_shape=[jax.ShapeDtypeStruct((t, D_MODEL), BF16), jax.ShapeDtypeStruct((t, D_MODEL), F32),
                   jax.ShapeDtypeStruct((t, D_MODEL), BF16)],
        compiler_params=_cparams(("parallel",)),
    )(proj, proj, mc, mp, pool_w, scale)


def _pool_bwd(name, dpu, w_pu, pm, mixed, pool_w, scale, tables):
    t = dpu.shape[0]
    nb = t // POOL_TILE
    _, _, mct, mpt = tables
    cur = pl.BlockSpec((POOL_TILE, D_MODEL), lambda n: (nb - 1 - n, 0))
    tab = _full_spec((GROUPS, POOL_TILE, POOL_TILE))
    wspec = _full_spec((GROUPS, GROUP_DIM, GROUP_DIM))
    sspec = _full_spec((1, D_MODEL))

    def body(dpu_ref, wpu_ref, pm_ref, mix_ref, mct_ref, mpt_ref, w_ref, sc_ref, dp_ref, dw_ref, ds_ref, later):
        n = pl.program_id(0)

        @pl.when(n == 0)
        def _():
            dw_ref[...] = jnp.zeros(dw_ref.shape, F32)
            ds_ref[...] = jnp.zeros(ds_ref.shape, F32)
            later[...] = jnp.zeros(later.shape, F32)

        dpo = _dot(dpu_ref[...], wpu_ref[...], "nt")
        for g, window in enumerate(POOL_WINDOWS):
            sl = slice(g * GROUP_DIM, (g + 1) * GROUP_DIM)
            dc, sc = dpo[:, sl], sc_ref[:, sl]
            dmix = (dc * sc).astype(BF16)
            dpm = _dot(dmix, w_ref[g], "nt")
            e = dpm / _pool_count(nb - 1 - n, window)
            e_hi, e_lo = _split2(e)
            f_hi, f_lo = _split2(later[g])
            mctv, mptv = mct_ref[g], mpt_ref[g]
            back = _dot(mctv, e_hi) + _dot(mctv, e_lo)
            after = _dot(mptv, f_hi) + _dot(mptv, f_lo)
            dp_ref[:, sl] = (back + after - dpm).astype(BF16)
            later[g] = e
            dw_ref[g] += _dot(pm_ref[:, sl], dmix, "tn")
            ds_ref[:, sl] += jnp.sum(dc * mix_ref[:, sl], axis=0, keepdims=True)

    return pl.pallas_call(
        body, name=name, grid=(nb,),
        in_specs=[cur, pl.BlockSpec((D_MODEL, D_MODEL), lambda n: (0, 0), pipeline_mode=pl.Buffered(1)), cur, cur, tab, tab,
                  wspec, sspec],
        out_specs=[cur, wspec, sspec],
        out_shape=[jax.ShapeDtypeStruct((t, D_MODEL), BF16), jax.ShapeDtypeStruct((GROUPS, GROUP_DIM, GROUP_DIM), F32),
                   jax.ShapeDtypeStruct((1, D_MODEL), F32)],
        scratch_shapes=[pltpu.VMEM((GROUPS, POOL_TILE, GROUP_DIM), F32)],
        compiler_params=_cparams(("arbitrary",)),
    )(dpu, w_pu, pm, mixed, mct, mpt, pool_w, scale)


GATE0_BLOCK, GATE1_BLOCK = 5, 6


def _merge_fwd(name, ret, po, w_ru, w_pu, w_out, proj, bias, h, next_g, tm=512):
    t = ret.shape[0]

    def body(r_ref, p_ref, wr_ref, wp_ref, wo_ref, g0_ref, g1_ref, b_ref, h_ref, ng_ref, m_ref, ru_ref, pu_ref, ho_ref, n_ref):
        ru = _dot(r_ref[...], wr_ref[...])
        pu = _dot(p_ref[...], wp_ref[...])
        ru_ref[...] = ru
        pu_ref[...] = pu
        merged = (_sigmoid(g0_ref[...].astype(F32) + b_ref[0:1, :]) * ru
                  + _sigmoid(g1_ref[...].astype(F32) + b_ref[1:2, :]) * pu).astype(BF16)
        m_ref[...] = merged
        h_new = h_ref[...] + _dot(merged, wo_ref[...])
        ho_ref[...] = h_new
        n_ref[...] = _normed(h_new, ng_ref[...]).astype(BF16)

    row = _row_spec(tm, D_MODEL)
    wspec = pl.BlockSpec((D_MODEL, D_MODEL), lambda i: (0, 0), pipeline_mode=pl.Buffered(1))
    return pl.pallas_call(
        body, name=name, grid=(t // tm,),
        in_specs=[row, row, wspec, wspec, wspec, _row_spec(tm, D_MODEL, GATE0_BLOCK), _row_spec(tm, D_MODEL, GATE1_BLOCK),
                  _full_spec((2, D_MODEL)), row, _full_spec((1, D_MODEL))],
        out_specs=[row] * 5,
        out_shape=[jax.ShapeDtypeStruct((t, D_MODEL), BF16), jax.ShapeDtypeStruct((t, D_MODEL), F32),
                   jax.ShapeDtypeStruct((t, D_MODEL), F32), jax.ShapeDtypeStruct((t, D_MODEL), F32),
                   jax.ShapeDtypeStruct((t, D_MODEL), BF16)],
        compiler_params=_cparams(("parallel",)),
    )(ret, po, w_ru, w_pu, w_out, proj, proj, bias, h, next_g)


def _merge_bwd(name, dh_b, w_out, ru, pu, proj, bias, tm=512):
    t = dh_b.shape[0]

    def body(dh_ref, wo_ref, ru_ref, pu_ref, g0_ref, g1_ref, b_ref, dru_ref, dpu_ref, dg0_ref, dg1_ref, db_ref):
        i = pl.program_id(0)
        d = _dot(dh_ref[...], wo_ref[...], "nt")
        s0 = _sigmoid(g0_ref[...].astype(F32) + b_ref[0:1, :])
        s1 = _sigmoid(g1_ref[...].astype(F32) + b_ref[1:2, :])
        dru_ref[...] = (d * s0).astype(BF16)
        dpu_ref[...] = (d * s1).astype(BF16)
        dg0 = d * ru_ref[...] * (s0 * (1.0 - s0))
        dg1 = d * pu_ref[...] * (s1 * (1.0 - s1))
        dg0_ref[...] = dg0.astype(BF16)
        dg1_ref[...] = dg1.astype(BF16)
        part0 = jnp.sum(dg0, axis=0, keepdims=True)
        part1 = jnp.sum(dg1, axis=0, keepdims=True)

        @pl.when(i == 0)
        def _():
            db_ref[0:1, :] = part0
            db_ref[1:2, :] = part1

        @pl.when(i > 0)
        def _():
            db_ref[0:1, :] += part0
            db_ref[1:2, :] += part1

    row = _row_spec(tm, D_MODEL)
    return pl.pallas_call(
        body, name=name, grid=(t // tm,),
        in_specs=[row, pl.BlockSpec((D_MODEL, D_MODEL), lambda i: (0, 0), pipeline_mode=pl.Buffered(1)), row, row,
                  _row_spec(tm, D_MODEL, GATE0_BLOCK), _row_spec(tm, D_MODEL, GATE1_BLOCK), _full_spec((2, D_MODEL))],
        out_specs=[row, row, row, row, _full_spec((2, D_MODEL))],
        out_shape=[jax.ShapeDtypeStruct((t, D_MODEL), BF16)] * 4 + [jax.ShapeDtypeStruct((2, D_MODEL), F32)],
        compiler_params=_cparams(("arbitrary",)),
    )(dh_b, w_out, ru, pu, proj, proj, bias)


def _half_scale(acc):
    return (FFN_RES_WEIGHT * acc,)


def _normed(h, g):
    return h * lax.rsqrt(jnp.mean(h * h, axis=-1, keepdims=True) + NORM_EPS) * g


def _residual_half_norm(acc, res, g):
    h = res + FFN_RES_WEIGHT * acc
    return h, _normed(h, g)


FF_TILE = D_FF // 2
DW_TILE = 256
SAVED_FF_DTYPE = BF16
FF_CHUNKS = ((0, 512), (512, 1024), (1024, FF_TILE))


def _ffn_in(name, nrm, w_in, tm=512):
    t = nrm.shape[0]
    nj = D_FF // FF_TILE

    def body(n_ref, wg_ref, wu_ref, a_ref, mid_ref):
        nv = n_ref[...]
        for c0, c1 in FF_CHUNKS:
            gate = _dot(nv, wg_ref[:, c0:c1])
            up = _dot(nv, wu_ref[:, c0:c1])
            s = _sigmoid(gate)
            silu = gate * s
            a_ref[0, :, c0:c1] = (FFN_RES_WEIGHT * up * (s * (1.0 + gate * (1.0 - s)))).astype(a_ref.dtype)
            a_ref[1, :, c0:c1] = (FFN_RES_WEIGHT * silu).astype(a_ref.dtype)
            mid_ref[:, c0:c1] = (silu * up).astype(BF16)

    return pl.pallas_call(
        body, name=name, grid=(nj, t // tm),
        in_specs=[pl.BlockSpec((tm, D_MODEL), lambda j, i: (i, 0)),
                  pl.BlockSpec((D_MODEL, FF_TILE), lambda j, i: (0, j)),
                  pl.BlockSpec((D_MODEL, FF_TILE), lambda j, i: (0, j + nj))],
        out_specs=[pl.BlockSpec((2, tm, FF_TILE), lambda j, i: (0, i, j)), pl.BlockSpec((tm, FF_TILE), lambda j, i: (i, j))],
        out_shape=[jax.ShapeDtypeStruct((2, t, D_FF), SAVED_FF_DTYPE), jax.ShapeDtypeStruct((t, D_FF), BF16)],
        compiler_params=_cparams(("parallel", "parallel")),
    )(nrm, w_in, w_in)


def _ffn_dact(name, dout_b, w_out, a, tm=512):
    t = dout_b.shape[0]

    def body(d_ref, w_ref, a_ref, da_ref):
        dv = d_ref[...]
        for c0, c1 in FF_CHUNKS:
            dm = _dot(dv, w_ref[c0:c1, :], "nt")
            da_ref[0, :, c0:c1] = (dm * a_ref[0, :, c0:c1].astype(F32)).astype(BF16)
            da_ref[1, :, c0:c1] = (dm * a_ref[1, :, c0:c1].astype(F32)).astype(BF16)

    blk = pl.BlockSpec((2, tm, FF_TILE), lambda j, i: (0, i, j))
    return pl.pallas_call(
        body, name=name, grid=(D_FF // FF_TILE, t // tm),
        in_specs=[pl.BlockSpec((tm, D_MODEL), lambda j, i: (i, 0)), pl.BlockSpec((FF_TILE, D_MODEL), lambda j, i: (j, 0)), blk],
        out_specs=blk,
        out_shape=jax.ShapeDtypeStruct((2, t, D_FF), BF16),
        compiler_params=_cparams(("parallel", "parallel")),
    )(dout_b, w_out, a)


def _ffn_fwd(tag, h, nrm, get_w_in, get_w_out, finish):
    t = h.shape[0]
    w_in = get_w_in(nrm)
    a, mid = _ffn_in(f"{tag}_in", nrm, w_in, tm=min(512, t))
    w_out = get_w_out(mid)
    return finish(mid, w_out), (nrm, a, mid, w_in, w_out)


def _ffn_bwd(tag, h, g, saved, dout, dout_b, on_grads, flush):
    t = h.shape[0]
    nrm, a, mid, w_in, w_out = saved
    d_w_out = _matmul(f"{tag}_dwout", mid, dout_b, "tn", D_FF, D_MODEL, t, DW_TILE, D_MODEL, t, [BF16], epilogue=_half_scale,
                      resident="b")
    da = _ffn_dact(f"{tag}_dact", dout_b, w_out, a, tm=min(512, t))
    nj = D_FF // DW_TILE
    d_w_in = _dw_resident(f"{tag}_dwin", nrm, [da], [pl.BlockSpec((None, t, DW_TILE), lambda s: (s // nj, 0, s % nj))],
                          2 * nj, None, DW_TILE)
    tie = on_grads({f"{tag}_w_in": d_w_in, f"{tag}_w_out": d_w_out})
    tm = min(256, t)
    dh, dh_b, dg = _proj_norm_bwd(f"{tag}_dn", [da], [pl.BlockSpec((2, tm, D_FF), lambda i: (0, i, 0))],
                                  ((0, 0, 0, D_FF), (0, 1, D_FF, 2 * D_FF)), w_in, h, g if tie is None else g + tie, dout, tm)
    return dh, dh_b, dg, flush(dh)


def _dw_resident(name, u, pieces, piece_specs, n_tiles, which_piece, tn):
    t = u.shape[0]
    npc = len(pieces)

    def body(*refs):
        u_ref, p_refs, o_ref, ut_ref = refs[0], refs[1:1 + npc], refs[1 + npc], refs[2 + npc]
        s = pl.program_id(0)

        @pl.when(s == 0)
        def _():
            ut_ref[...] = u_ref[...].T

        if npc == 1:
            o_ref[...] = _dot(ut_ref[...], p_refs[0][...]).astype(BF16)
        for which in range(npc if npc > 1 else 0):
            @pl.when(which_piece(s) == which)
            def _(which=which):
                o_ref[...] = _dot(ut_ref[...], p_refs[which][...]).astype(BF16)

    return pl.pallas_call(
        body, name=name, grid=(n_tiles,),
        in_specs=[pl.BlockSpec((t, D_MODEL), lambda s: (0, 0), pipeline_mode=pl.Buffered(1))] + list(piece_specs),
        out_specs=pl.BlockSpec((D_MODEL, tn), lambda s: (0, s)),
        out_shape=jax.ShapeDtypeStruct((D_MODEL, n_tiles * tn), BF16),
        scratch_shapes=[pltpu.VMEM((D_MODEL, t), BF16)],
        compiler_params=_cparams(("arbitrary",)),
    )(u, *pieces)


def _mix_dwin(name, u, pieces, tn=256):
    t = u.shape[0]
    nj = D_MODEL // tn
    specs = [pl.BlockSpec((t, tn), lambda s, k=k: (0, jnp.clip(s - k * nj, 0, nj - 1))) for k in range(len(pieces))]
    return _dw_resident(name, u, pieces, specs, len(pieces) * nj, lambda s: s // nj, tn)


def _local_step(x, target, vec, get_w, relay_w, on_grads, flush):
    t = x.shape[0]
    cos, sin = _rope_tables(t)
    rtab = _retention_tables()
    ptab = _pool_tables()
    w = {}

    def getter(group, name):
        def get(after):
            if name not in w:
                w.update(get_w(group, after))
            return w[name]
        return get

    nrm1 = _rmsnorm_fwd("ffn1_norm", x, vec["norm_ffn1"])
    def out_and_norm(mid, w_out):
        return _matmul("ffn1_out", mid, w_out, "nn", t, D_MODEL, D_FF, 512, D_MODEL, D_FF, [F32, BF16],
                       extras=(x,), consts=(vec["norm_mix"],), epilogue=_residual_half_norm)

    (h1, u), s1 = _ffn_fwd("ffn1", x, nrm1, getter(0, "ffn1_w_in"), getter(1, "ffn1_w_out"), out_and_norm)
    w.update(get_w(2, u))
    proj = _matmul("mix_in", u, w["w_in"], "nn", t, IN_WIDTH, D_MODEL, 1024, 1024, D_MODEL, [BF16], n_outer=True)
    o, ret, states = _retention_fwd("retention", proj, cos, sin, rtab)
    pm, mixed, po = _pool_fwd("pool", proj, w["pool_w"], vec["pool_scale"], ptab)
    relay_w(3, po)
    merged, ru, pu, h2, nrm2 = _merge_fwd("merge", ret, po, w["w_ret_up"], w["w_pool_up"], w["w_out"], proj, w["gate_bias"],
                                          h1, vec["norm_ffn2"], tm=min(512, t))
    def out_and_loss(mid, w_out):
        return _out_loss_and_grad("ffn2_out_loss", mid, w_out, h2, vec["norm_final"], target, tm=min(512, t))

    (dh3, dh3_b, dg_final, loss), s2 = _ffn_fwd("ffn2", h2, nrm2, getter(3, "ffn2_w_in"), getter(3, "ffn2_w_out"), out_and_loss)

    def tied(v, tie):
        return v if tie is None else v + tie

    dh2, dh2_b, dg_ffn2, tie = _ffn_bwd("ffn2", h2, vec["norm_ffn2"], s2, dh3, dh3_b, on_grads, flush)
    def square_dw(name, act, grad):
        return _matmul(name, act, grad, "tn", D_MODEL, D_MODEL, t, D_MODEL, D_MODEL, 1024, [BF16])

    d_w_out = square_dw("mix_dwout", merged, dh2_b)
    dru, dpu, dg0, dg1, d_bias = _merge_bwd("merge_bwd", dh2_b, w["w_out"], ru, pu, proj, tied(w["gate_bias"], tie))
    d_w_ru = square_dw("mix_dwru", ret, dru)
    d_w_pu = square_dw("mix_dwpu", po, dpu)
    dp, d_pool_w, d_scale = _pool_bwd("pool_bwd", dpu, w["w_pool_up"], pm, mixed, w["pool_w"], vec["pool_scale"], ptab)
    dq, dk, dv, dgr = _retention_bwd("retention_bwd", dru, w["w_ret_up"], o, proj, cos, sin, states, rtab)
    dproj = [dq, dk, dv, dgr, dp, dg0, dg1]
    d_w_in = _mix_dwin("mix_dwin", u, dproj)
    tie = on_grads(dict(w_in=d_w_in, pool_w=d_pool_w.astype(BF16), w_ret_up=d_w_ru, w_pool_up=d_w_pu, w_out=d_w_out))
    tm = min(256, t)
    dh1, dh1_b, dg_mix = _proj_norm_bwd("mix_du", dproj, [_row_spec(tm, D_MODEL)] * len(dproj),
                                        [(k, None, k * D_MODEL, (k + 1) * D_MODEL) for k in range(len(dproj))],
                                        w["w_in"], h1, tied(vec["norm_mix"], tie), dh2, tm)
    tie = flush(dh1)
    dx, _, dg_ffn1, _ = _ffn_bwd("ffn1", x, tied(vec["norm_ffn1"], tie), s1, dh1, dh1_b, on_grads, flush)

    small = dict(norm_ffn1=dg_ffn1, norm_mix=dg_mix, gate_bias=d_bias, pool_scale=d_scale, norm_ffn2=dg_ffn2,
                 norm_final=dg_final)
    return loss[0, 0], dx, small


BIG = ("ffn1_w_in", "ffn1_w_out", "w_in", "pool_w", "w_ret_up", "w_pool_up", "w_out", "ffn2_w_in", "ffn2_w_out")
KIND = dict(ffn1_w_in="col", ffn1_w_out="row", w_in="col", pool_w="pool", w_ret_up="row", w_pool_up="row", w_out="row",
            ffn2_w_in="col", ffn2_w_out="row", gate_bias="col")
ANY = pl.BlockSpec(memory_space=pl.ANY)


def _place():
    x, y, c = lax.axis_index("x"), lax.axis_index("y"), lax.axis_index("c")
    chips = [(1 - x, y), (x, 1 - y), (1 - x, 1 - y)]
    return x, y, c, chips


def _full_view_shape(kind, local_shape):
    if kind == "col":
        return (2, local_shape[0] // 2, N_CHIPS * local_shape[1])
    if kind == "row":
        return (N_CHIPS, 2, local_shape[0] // 2, local_shape[1])
    return (GROUPS, N_CHIPS, 2, local_shape[1] // 2, local_shape[2])


def _local_view(kind, arr):
    if kind == "pool":
        return arr.reshape(GROUPS, 2, arr.shape[1] // 2, arr.shape[2])
    return arr.reshape(2, arr.shape[0] // 2, arr.shape[1])


def _blk(kind, ref, s, c):
    if kind == "col":
        cs = ref.shape[2] // N_CHIPS
        return ref.at[c, :, pl.ds(pl.multiple_of(s * cs, 128), cs)]
    if kind == "row":
        return ref.at[s, c]
    return ref.at[:, s, c]


def _half(kind, ref, c):
    return ref.at[:, c] if kind == "pool" else ref.at[c]


def _shard(kind, ref, s):
    if kind == "col":
        cs = ref.shape[2] // N_CHIPS
        return ref.at[:, :, pl.ds(pl.multiple_of(s * cs, 128), cs)]
    if kind == "row":
        return ref.at[s]
    return ref.at[:, s]


HBM = pl.BlockSpec(memory_space=pltpu.HBM)
SEM = pl.BlockSpec(memory_space=pltpu.SEMAPHORE)
EFFECT = pltpu.SideEffectType.DATAFLOW_SIDE_EFFECTING
WEIGHT_GROUPS = (("gate_bias", "ffn1_w_in"), ("ffn1_w_out",), ("w_in", "pool_w", "w_ret_up", "w_pool_up", "w_out"), ("ffn2_w_in", "ffn2_w_out"))
GRAD_GROUPS = (("ffn2_w_in", "ffn2_w_out"), ("w_in", "pool_w", "w_ret_up", "w_pool_up", "w_out"), ("ffn1_w_in", "ffn1_w_out"))


def _hbm(a):
    return pltpu.with_memory_space_constraint(a, pltpu.HBM)


def _natural(kind, o):
    if kind == "col":
        return o.reshape(o.shape[0] * o.shape[1], o.shape[2])
    if kind == "row":
        return o.reshape(-1, o.shape[3])
    return o.reshape(GROUPS, -1, o.shape[4])


def _ici_copy(kind, loc, full, j, chips, s, c, send_sem, recv_sem):
    px, py = chips[j]
    return (pltpu.make_async_remote_copy(src_ref=_half(kind, loc, c), dst_ref=_blk(kind, full, s, c), send_sem=send_sem,
                                         recv_sem=recv_sem, device_id=(px, py, c), device_id_type=MESH),
            pltpu.make_async_remote_copy(src_ref=_half(kind, loc, c), dst_ref=_blk(kind, full, 2 * px + py, c), send_sem=send_sem,
                                         recv_sem=recv_sem, device_id=(px, py, c), device_id_type=MESH))


def _gather_start(tag, group_ids, shards):
    grps = [WEIGHT_GROUPS[g] for g in group_ids]
    names = [nm for grp in grps for nm in grp]
    kinds = [KIND[nm] for nm in names]
    n, ng = len(names), len(grps)
    locs = [_hbm(_local_view(KIND[nm], shards[nm])) for nm in names]
    lands = [_hbm(lax.empty(_full_view_shape(KIND[nm], shards[nm].shape), shards[nm].dtype)) for nm in names]
    first = np.cumsum([0] + [len(grp) for grp in grps])

    def body(*refs):
        loc, full = refs[:n], refs[n:2 * n]
        send_sems, recv_sems = refs[2 * n:2 * n + ng], refs[2 * n + ng:2 * n + 2 * ng]
        token = refs[-1]
        x, y, c, chips = _place()
        s = 2 * x + y
        for g in range(ng):
            for a in range(first[g], first[g + 1]):
                for j in range(3):
                    k = 3 * (a - first[g]) + j
                    _ici_copy(kinds[a], loc[a], full[a], j, chips, s, c, send_sems[g].at[k], recv_sems[g].at[k])[0].start()
        token[...] = jnp.zeros(token.shape, F32)

    sem_shapes = [pltpu.SemaphoreType.DMA((3 * len(grp),)) for grp in grps]
    outs = pl.pallas_call(
        body, name=f"gather_start_{tag}",
        in_specs=[HBM] * (2 * n),
        out_specs=[SEM] * (2 * ng) + [HBM] * (2 * n) + [pl.BlockSpec(memory_space=pltpu.VMEM)],
        out_shape=sem_shapes + sem_shapes + [pltpu.HBM(a.shape, a.dtype) for a in locs + lands] + [jax.ShapeDtypeStruct((8, 128), F32)],
        input_output_aliases={i: 2 * ng + i for i in range(2 * n)},
        compiler_params=pltpu.CompilerParams(has_side_effects=EFFECT),
    )(*locs, *lands)
    send_sems, recv_sems = outs[:ng], outs[ng:2 * ng]
    locs_t, lands_t = outs[2 * ng:2 * ng + n], outs[2 * ng + n:2 * ng + 2 * n]
    groups = {}
    for k, g in enumerate(group_ids):
        sl = slice(first[k], first[k + 1])
        groups[g] = (send_sems[k], recv_sems[k], list(locs_t[sl]), list(lands_t[sl]))
    return groups, outs[-1]


def _forward_copies(kinds, loc, full, send_sems, recv_sems):
    x, y, c, chips = _place()
    s = 2 * x + y

    def remote(a, k, src, dst):
        return pltpu.make_async_remote_copy(src_ref=src, dst_ref=dst, send_sem=send_sems.at[4 * a + k],
                                            recv_sem=recv_sems.at[4 * a + k], device_id=(x, y, 1 - c), device_id_type=MESH)

    sends, arrivals = [], []
    for a, kind in enumerate(kinds):
        for j, (px, py) in enumerate(chips):
            theirs, from_sib = _blk(kind, full[a], 2 * px + py, c), _blk(kind, full[a], 2 * px + py, 1 - c)
            sends.append(remote(a, j, theirs, theirs))
            arrivals.append(remote(a, j, from_sib, from_sib))
        own = _shard(kind, full[a], s)
        sends.append(remote(a, 3, loc[a], own))
        arrivals.append(remote(a, 3, own, own))
    return sends, arrivals


def _gather_relay(g, group, after):
    names = WEIGHT_GROUPS[g]
    kinds = [KIND[nm] for nm in names]
    m = len(names)
    ici_send, ici_recv, locs, lands = group

    def body(*refs):
        loc, full = refs[:m], refs[m:2 * m]
        ici_s, ici_r = refs[2 * m], refs[2 * m + 1]
        d2d_s, d2d_r = refs[2 * m + 2 + len(after)], refs[2 * m + 3 + len(after)]
        x, y, c, chips = _place()
        for a in range(m):
            for j in range(3):
                sent, landed = _ici_copy(kinds[a], loc[a], full[a], j, chips, 2 * x + y, c, ici_s.at[3 * a + j], ici_r.at[3 * a + j])
                sent.wait_send()
                landed.wait_recv()
        for cp in _forward_copies(kinds, loc, full, d2d_s, d2d_r)[0]:
            cp.start()

    sem_shape = pltpu.SemaphoreType.DMA((4 * m,))
    outs = pl.pallas_call(
        body, name=f"gather_relay_{g}",
        in_specs=[HBM] * (2 * m) + [SEM, SEM] + [ANY] * len(after), out_specs=[SEM, SEM] + [HBM] * (2 * m),
        out_shape=[sem_shape, sem_shape] + [pltpu.HBM(a.shape, a.dtype) for a in locs + lands],
        input_output_aliases={i: 2 + i for i in range(2 * m)},
        compiler_params=pltpu.CompilerParams(has_side_effects=EFFECT),
    )(*locs, *lands, ici_send, ici_recv, *after)
    return outs[0], outs[1], list(outs[2:2 + m]), list(outs[2 + m:2 + 2 * m])


def _gather_land(g, state, after):
    names = WEIGHT_GROUPS[g]
    kinds = [KIND[nm] for nm in names]
    m = len(names)
    d2d_send, d2d_recv, locs, lands = state

    def body(*refs):
        sends, arrivals = _forward_copies(kinds, refs[:m], refs[m:2 * m], refs[2 * m], refs[2 * m + 1])
        for cp in sends:
            cp.wait_send()
        for cp in arrivals:
            cp.wait_recv()

    outs = pl.pallas_call(
        body, name=f"gather_land_{g}",
        in_specs=[HBM] * (2 * m) + [SEM, SEM] + [ANY] * len(after), out_specs=[HBM] * (2 * m),
        out_shape=[pltpu.HBM(a.shape, a.dtype) for a in locs + lands],
        input_output_aliases={i: i for i in range(2 * m)},
        compiler_params=pltpu.CompilerParams(has_side_effects=EFFECT),
    )(*locs, *lands, d2d_send, d2d_recv, *after)
    return {nm: _natural(k, o) for nm, k, o in zip(names, kinds, outs[m:])}


def _gather_finish(g, group, after):
    names = WEIGHT_GROUPS[g]
    kinds = [KIND[nm] for nm in names]
    m = len(names)
    send_sem, recv_sem, locs, lands = group

    def wait_body(*refs):
        loc, full = refs[:m], refs[m:2 * m]
        send_sems, recv_sems = refs[2 * m], refs[2 * m + 1]
        x, y, c, chips = _place()
        s = 2 * x + y
        for a in range(m):
            for j in range(3):
                k = 3 * a + j
                sent, landed = _ici_copy(kinds[a], loc[a], full[a], j, chips, s, c, send_sems.at[k], recv_sems.at[k])
                sent.wait_send()
                landed.wait_recv()

    outs = pl.pallas_call(
        wait_body, name=f"gather_wait_{g}",
        in_specs=[HBM] * (2 * m) + [SEM, SEM] + [ANY] * len(after), out_specs=[HBM] * (2 * m),
        out_shape=[pltpu.HBM(a.shape, a.dtype) for a in locs + lands],
        input_output_aliases={i: i for i in range(2 * m)},
        compiler_params=pltpu.CompilerParams(has_side_effects=EFFECT),
    )(*locs, *lands, send_sem, recv_sem, *after)
    locs, lands = outs[:m], outs[m:]

    def forward_body(*refs):
        sends, arrivals = _forward_copies(kinds, refs[:m], refs[2 * m:3 * m], *refs[3 * m:])
        for cp in sends:
            cp.start()
        for cp in arrivals:
            cp.wait_recv()
        for cp in sends:
            cp.wait_send()

    outs = pl.pallas_call(
        forward_body, name=f"gather_forward_{g}",
        in_specs=[ANY] * (2 * m), out_specs=[ANY] * m,
        out_shape=[jax.ShapeDtypeStruct(a.shape, a.dtype) for a in lands],
        input_output_aliases={m + i: i for i in range(m)},
        scratch_shapes=[pltpu.SemaphoreType.DMA((4 * m,)), pltpu.SemaphoreType.DMA((4 * m,))],
    )(*locs, *lands)
    return {nm: _natural(k, o) for nm, k, o in zip(names, kinds, outs)}


def _grad_view(kind, g):
    if kind == "col":
        return g.reshape(2, g.shape[0] // 2, g.shape[1])
    if kind == "row":
        return g.reshape(N_CHIPS, 2, g.shape[0] // (2 * N_CHIPS), g.shape[1])
    return g.reshape(GROUPS, N_CHIPS, 2, g.shape[1] // (2 * N_CHIPS), g.shape[2])


def _pair_copies(kinds, g, got, send_sems, recv_sems):
    x, y, c, _ = _place()

    def other_half(kind, ref):
        if kind == "col":
            return ref.at[1 - c]
        if kind == "row":
            return ref.at[:, 1 - c]
        return ref.at[:, :, 1 - c]

    return [pltpu.make_async_remote_copy(src_ref=other_half(kinds[a], g[a]), dst_ref=got[a], send_sem=send_sems.at[a],
                                         recv_sem=recv_sems.at[a], device_id=(x, y, 1 - c), device_id_type=MESH)
            for a in range(len(kinds))]


def _pair_exchange_start(tag, names, views):
    kinds = [KIND[nm] for nm in names]
    n = len(names)

    def got_shape(kind, v):
        if kind == "col":
            return v.shape[1:]
        if kind == "row":
            return (v.shape[0],) + v.shape[2:]
        return v.shape[:2] + v.shape[3:]

    srcs = [_hbm(views[nm]) for nm in names]
    lands = [_hbm(lax.empty(got_shape(k, views[nm]), BF16)) for nm, k in zip(names, kinds)]

    def body(*refs):
        g, got = refs[:n], refs[n:2 * n]
        for cp in _pair_copies(kinds, g, got, refs[2 * n], refs[2 * n + 1]):
            cp.start()
        refs[-1][...] = jnp.zeros(refs[-1].shape, F32)

    sem_shape = pltpu.SemaphoreType.DMA((n,))
    outs = pl.pallas_call(
        body, name=f"grad_pair_exchange_start_{tag}",
        in_specs=[HBM] * (2 * n),
        out_specs=[SEM, SEM] + [HBM] * (2 * n) + [pl.BlockSpec(memory_space=pltpu.VMEM)],
        out_shape=[sem_shape, sem_shape] + [pltpu.HBM(a.shape, a.dtype) for a in srcs + lands] + [jax.ShapeDtypeStruct((8, 128), F32)],
        input_output_aliases={i: 2 + i for i in range(2 * n)},
        compiler_params=pltpu.CompilerParams(has_side_effects=EFFECT),
    )(*srcs, *lands)
    return (outs[0], outs[1], list(outs[2:2 + n]), list(outs[2 + n:2 + 2 * n])), outs[-1]


def _pair_exchange_wait(tag, names, state, after):
    kinds = [KIND[nm] for nm in names]
    n = len(names)
    send_sem, recv_sem, srcs, lands = state

    def body(*refs):
        g, got = refs[:n], refs[n:2 * n]
        for cp in _pair_copies(kinds, g, got, refs[2 * n], refs[2 * n + 1]):
            cp.wait_send()
            cp.wait_recv()

    outs = pl.pallas_call(
        body, name=f"grad_pair_exchange_wait_{tag}",
        in_specs=[HBM] * (2 * n) + [SEM, SEM, ANY], out_specs=[HBM] * (2 * n),
        out_shape=[pltpu.HBM(a.shape, a.dtype) for a in srcs + lands],
        input_output_aliases={i: i for i in range(2 * n)},
        compiler_params=pltpu.CompilerParams(has_side_effects=EFFECT),
    )(*srcs, *lands, send_sem, recv_sem, after)
    return dict(zip(names, outs[:n])), dict(zip(names, outs[n:]))


def _pair_sum(name, kind, view, got, c_arr):
    if kind == "col":
        _, rows, cols = view.shape
        tr = 128
        grid = (rows // tr,)
        v_spec = pl.BlockSpec((None, tr, cols), lambda i, c: (c[0], i, 0))
        g_spec = pl.BlockSpec((tr, cols), lambda i, c: (i, 0))
    elif kind == "row":
        _, _, rows, cols = view.shape
        grid = (N_CHIPS,)
        v_spec = pl.BlockSpec((None, None, rows, cols), lambda i, c: (i, c[0], 0, 0))
        g_spec = pl.BlockSpec((None, rows, cols), lambda i, c: (i, 0, 0))
    else:
        _, _, _, rows, cols = view.shape
        grid = (GROUPS,)
        v_spec = pl.BlockSpec((None, N_CHIPS, None, rows, cols), lambda i, c: (i, 0, c[0], 0, 0))
        g_spec = pl.BlockSpec((None, N_CHIPS, rows, cols), lambda i, c: (i, 0, 0, 0))

    def body(c_ref, v_ref, g_ref, o_ref):
        o_ref[...] = (v_ref[...].astype(F32) + g_ref[...].astype(F32)).astype(BF16)

    return pl.pallas_call(
        body, name=name,
        grid_spec=pltpu.PrefetchScalarGridSpec(num_scalar_prefetch=1, grid=grid, in_specs=[v_spec, g_spec], out_specs=g_spec),
        out_shape=jax.ShapeDtypeStruct(got.shape, BF16),
        compiler_params=_cparams(("parallel",)),
    )(c_arr, view, got)


def _piece(kind, ref, s):
    if kind == "col":
        cs = ref.shape[1] // N_CHIPS
        return ref.at[:, pl.ds(pl.multiple_of(s * cs, 128), cs)]
    if kind == "row":
        return ref.at[s]
    return ref.at[:, s]


def _piece_shape(kind, shape):
    if kind == "col":
        return (shape[0], shape[1] // N_CHIPS)
    if kind == "row":
        return shape[1:]
    return (shape[0],) + shape[2:]


def _shard_copies(kinds, p, got, send_sems, recv_sems):
    x, y, c, chips = _place()
    return [pltpu.make_async_remote_copy(src_ref=_piece(kinds[a], p[a], 2 * px + py), dst_ref=got[a].at[j],
                                         send_sem=send_sems.at[3 * a + j], recv_sem=recv_sems.at[3 * a + j],
                                         device_id=(px, py, c), device_id_type=MESH)
            for a in range(len(kinds)) for j, (px, py) in enumerate(chips)]


def _shard_exchange_start(g, names, psums):
    kinds = [KIND[nm] for nm in names]
    n = len(names)
    srcs = [_hbm(psums[nm]) for nm in names]
    lands = [_hbm(lax.empty((3,) + _piece_shape(k, psums[nm].shape), BF16)) for nm, k in zip(names, kinds)]

    def body(*refs):
        p, got = refs[:n], refs[n:2 * n]
        send_sems, recv_sems = refs[2 * n], refs[2 * n + 1]
        token = refs[-1]
        for cp in _shard_copies(kinds, p, got, send_sems, recv_sems):
            cp.start()
        token[...] = jnp.zeros(token.shape, F32)

    sem_shape = pltpu.SemaphoreType.DMA((3 * n,))
    outs = pl.pallas_call(
        body, name=f"grad_shard_exchange_start_{g}",
        in_specs=[HBM] * (2 * n),
        out_specs=[SEM, SEM] + [HBM] * (2 * n) + [pl.BlockSpec(memory_space=pltpu.VMEM)],
        out_shape=[sem_shape, sem_shape] + [pltpu.HBM(a.shape, a.dtype) for a in srcs + lands] + [jax.ShapeDtypeStruct((8, 128), F32)],
        input_output_aliases={i: 2 + i for i in range(2 * n)},
        compiler_params=pltpu.CompilerParams(has_side_effects=EFFECT),
    )(*srcs, *lands)
    return (outs[0], outs[1], list(outs[2:2 + n]), list(outs[2 + n:2 + 2 * n])), outs[-1]


def _shard_exchange_wait(g, names, state, after):
    kinds = [KIND[nm] for nm in names]
    n = len(names)
    send_sem, recv_sem, srcs, lands = state

    def body(*refs):
        p, got = refs[:n], refs[n:2 * n]
        for cp in _shard_copies(kinds, p, got, refs[2 * n], refs[2 * n + 1]):
            cp.wait_send()
            cp.wait_recv()

    outs = pl.pallas_call(
        body, name=f"grad_shard_exchange_wait_{g}",
        in_specs=[HBM] * (2 * n) + [SEM, SEM] + [ANY] * len(after), out_specs=[HBM] * (2 * n),
        out_shape=[pltpu.HBM(a.shape, a.dtype) for a in srcs + lands],
        input_output_aliases={i: i for i in range(2 * n)},
        compiler_params=pltpu.CompilerParams(has_side_effects=EFFECT),
    )(*srcs, *lands, send_sem, recv_sem, *after)
    return dict(zip(names, outs[:n])), dict(zip(names, outs[n:]))


def _shard_sum(name, kind, psum, got, sc_arr):
    if kind == "col":
        rows, cols = psum.shape
        cs = cols // N_CHIPS
        tr = 128
        grid = (rows // tr,)
        p_spec = pl.BlockSpec((tr, cs), lambda i, sc: (i, sc[0]))
        g_spec = pl.BlockSpec((3, tr, cs), lambda i, sc: (0, i, 0))
        o_spec = pl.BlockSpec((None, tr, cs), lambda i, sc: (sc[1], i, 0))
        out_shape = (2, rows, cs)
    elif kind == "row":
        _, rows, cols = psum.shape
        grid = (1,)
        p_spec = pl.BlockSpec((None, rows, cols), lambda i, sc: (sc[0], 0, 0))
        g_spec = pl.BlockSpec((3, rows, cols), lambda i, sc: (0, 0, 0))
        o_spec = pl.BlockSpec((None, rows, cols), lambda i, sc: (sc[1], 0, 0))
        out_shape = (2, rows, cols)
    else:
        _, _, rows, cols = psum.shape
        grid = (1,)
        p_spec = pl.BlockSpec((GROUPS, None, rows, cols), lambda i, sc: (0, sc[0], 0, 0))
        g_spec = pl.BlockSpec((3, GROUPS, rows, cols), lambda i, sc: (0, 0, 0, 0))
        o_spec = pl.BlockSpec((GROUPS, None, rows, cols), lambda i, sc: (0, sc[1], 0, 0))
        out_shape = (GROUPS, 2, rows, cols)

    def body(sc_ref, p_ref, g_ref, o_ref):
        o_ref[...] = ((p_ref[...].astype(F32) + g_ref[0].astype(F32)) + g_ref[1].astype(F32)) + g_ref[2].astype(F32)

    return pl.pallas_call(
        body, name=name,
        grid_spec=pltpu.PrefetchScalarGridSpec(num_scalar_prefetch=1, grid=grid, in_specs=[p_spec, g_spec], out_specs=o_spec),
        out_shape=jax.ShapeDtypeStruct(out_shape, F32),
        compiler_params=_cparams(("parallel",)),
    )(sc_arr, psum, got)


def _half_copies(kinds, bufs, send_sems, recv_sems):
    x, y, c, _ = _place()

    def remote(a, half):
        part = _half(kinds[a], bufs[a], half)
        return pltpu.make_async_remote_copy(src_ref=part, dst_ref=part, send_sem=send_sems.at[a], recv_sem=recv_sems.at[a],
                                            device_id=(x, y, 1 - c), device_id_type=MESH)

    return [remote(a, c) for a in range(len(kinds))], [remote(a, 1 - c) for a in range(len(kinds))]


def _half_exchange_start(tag, names, bufs):
    kinds = [KIND[nm] for nm in names]
    n = len(names)
    arrs = [_hbm(bufs[nm]) for nm in names]

    def body(*refs):
        for cp in _half_copies(kinds, refs[:n], refs[n], refs[n + 1])[0]:
            cp.start()
        refs[-1][...] = jnp.zeros(refs[-1].shape, F32)

    sem_shape = pltpu.SemaphoreType.DMA((n,))
    outs = pl.pallas_call(
        body, name=f"grad_half_exchange_start_{tag}",
        in_specs=[HBM] * n,
        out_specs=[SEM, SEM] + [HBM] * n + [pl.BlockSpec(memory_space=pltpu.VMEM)],
        out_shape=[sem_shape, sem_shape] + [pltpu.HBM(a.shape, a.dtype) for a in arrs] + [jax.ShapeDtypeStruct((8, 128), F32)],
        input_output_aliases={i: 2 + i for i in range(n)},
        compiler_params=pltpu.CompilerParams(has_side_effects=EFFECT),
    )(*arrs)
    return (outs[0], outs[1], list(outs[2:2 + n])), outs[-1]


def _half_exchange_wait(tag, names, state, after):
    kinds = [KIND[nm] for nm in names]
    n = len(names)
    send_sem, recv_sem, arrs = state

    def body(*refs):
        sends, arrivals = _half_copies(kinds, refs[:n], refs[n], refs[n + 1])
        for cp in sends:
            cp.wait_send()
        for cp in arrivals:
            cp.wait_recv()

    outs = pl.pallas_call(
        body, name=f"grad_half_exchange_wait_{tag}",
        in_specs=[HBM] * n + [SEM, SEM] + [ANY] * len(after), out_specs=[HBM] * n,
        out_shape=[pltpu.HBM(a.shape, a.dtype) for a in arrs],
        input_output_aliases={i: i for i in range(n)},
        compiler_params=pltpu.CompilerParams(has_side_effects=EFFECT),
    )(*arrs, send_sem, recv_sem, *after)
    return dict(zip(names, outs))


N_DEV = 8
SMALL_ROWS = 8


def _all_reduce_small(name, v):
    def body(v_ref, o_ref, token, buf, send_sems, recv_sems):
        token[...] = jnp.zeros(token.shape, F32)
        x, y, c, _ = _place()
        me = 4 * x + 2 * y + c
        buf[me] = v_ref[...]
        cps = []
        for r in range(1, N_DEV):
            to = (x ^ (r >> 2), y ^ ((r >> 1) & 1), c ^ (r & 1))
            cp = pltpu.make_async_remote_copy(src_ref=v_ref, dst_ref=buf.at[me], send_sem=send_sems.at[r - 1],
                                              recv_sem=recv_sems.at[r - 1], device_id=to, device_id_type=MESH)
            cp.start()
            cps.append(cp)
        for r in range(1, N_DEV):
            pltpu.make_async_remote_copy(src_ref=v_ref, dst_ref=buf.at[me ^ r], send_sem=send_sems.at[r - 1],
                                         recv_sem=recv_sems.at[r - 1], device_id=(x, y, c), device_id_type=MESH).wait_recv()
        for cp in cps:
            cp.wait_send()
        acc = buf[0]
        for d in range(1, N_DEV):
            acc = acc + buf[d]
        o_ref[...] = acc

    vm = pl.BlockSpec(memory_space=pltpu.VMEM)
    return pl.pallas_call(
        body, name=name, in_specs=[vm], out_specs=[vm, vm],
        out_shape=[jax.ShapeDtypeStruct((SMALL_ROWS, D_MODEL), F32), jax.ShapeDtypeStruct((8, 128), F32)],
        scratch_shapes=[pltpu.VMEM((N_DEV, SMALL_ROWS, D_MODEL), F32), pltpu.SemaphoreType.DMA((N_DEV - 1,)),
                        pltpu.SemaphoreType.DMA((N_DEV - 1,))],
    )(v)


def _adamw(name, w, g, m, v):
    rows, cols = w.shape
    tr = next((c for c in (256, 176, 128, 64, 32, 8) if rows % c == 0), rows)
    spec = pl.BlockSpec((tr, cols), lambda i: (i, 0))

    def body(w_ref, g_ref, m_ref, v_ref, d_ref, mo_ref, vo_ref):
        gv = g_ref[...]
        m_new = ADAM_B1 * m_ref[...] + (1.0 - ADAM_B1) * gv
        v_new = ADAM_B2 * v_ref[...] + (1.0 - ADAM_B2) * jnp.square(gv)
        m_hat = m_new / (1.0 - ADAM_B1 ** ADAM_STEP)
        v_hat = v_new / (1.0 - ADAM_B2 ** ADAM_STEP)
        d_ref[...] = -ADAM_LR * (m_hat / (jnp.sqrt(v_hat) + ADAM_EPS) + ADAM_WD * w_ref[...])
        mo_ref[...] = m_new
        vo_ref[...] = v_new

    return pl.pallas_call(
        body, name=name, grid=(rows // tr,),
        in_specs=[spec] * 4, out_specs=[spec] * 3,
        out_shape=[jax.ShapeDtypeStruct((rows, cols), F32)] * 3,
        compiler_params=_cparams(("parallel",)),
    )(w, g, m, v)


WEIGHTS = ("norm_ffn1", "ffn1_w_in", "ffn1_w_out", "norm_mix", "w_in", "gate_bias", "pool_w", "pool_scale", "w_ret_up",
           "w_pool_up", "w_out", "norm_ffn2", "ffn2_w_in", "ffn2_w_out", "norm_final")
SMALL_ROW = dict(norm_ffn1=0, norm_mix=1, gate_bias=2, pool_scale=4, norm_ffn2=5, norm_final=6)


def _as2d(a):
    return a.reshape(-1, a.shape[-1])


def kernel(x, norm_ffn1, ffn1_w_in, ffn1_w_out, norm_mix, w_in, gate_bias, pool_w, pool_scale, w_ret_up, w_pool_up, w_out, norm_ffn2, ffn2_w_in, ffn2_w_out, norm_final, loss_target, m_norm_ffn1, m_ffn1_w_in, m_ffn1_w_out, m_norm_mix, m_w_in, m_gate_bias, m_pool_w, m_pool_scale, m_w_ret_up, m_w_pool_up, m_w_out, m_norm_ffn2, m_ffn2_w_in, m_ffn2_w_out, m_norm_final, v_norm_ffn1, v_ffn1_w_in, v_ffn1_w_out, v_norm_mix, v_w_in, v_gate_bias, v_pool_w, v_pool_scale, v_w_ret_up, v_w_pool_up, v_w_out, v_norm_ffn2, v_ffn2_w_in, v_ffn2_w_out, v_norm_final):
    wt = dict(norm_ffn1=norm_ffn1, ffn1_w_in=ffn1_w_in, ffn1_w_out=ffn1_w_out, norm_mix=norm_mix, w_in=w_in, gate_bias=gate_bias,
              pool_w=pool_w, pool_scale=pool_scale, w_ret_up=w_ret_up, w_pool_up=w_pool_up, w_out=w_out, norm_ffn2=norm_ffn2,
              ffn2_w_in=ffn2_w_in, ffn2_w_out=ffn2_w_out, norm_final=norm_final)
    mom = dict(norm_ffn1=m_norm_ffn1, ffn1_w_in=m_ffn1_w_in, ffn1_w_out=m_ffn1_w_out, norm_mix=m_norm_mix, w_in=m_w_in,
               gate_bias=m_gate_bias, pool_w=m_pool_w, pool_scale=m_pool_scale, w_ret_up=m_w_ret_up, w_pool_up=m_w_pool_up,
               w_out=m_w_out, norm_ffn2=m_norm_ffn2, ffn2_w_in=m_ffn2_w_in, ffn2_w_out=m_ffn2_w_out, norm_final=m_norm_final)
    var = dict(norm_ffn1=v_norm_ffn1, ffn1_w_in=v_ffn1_w_in, ffn1_w_out=v_ffn1_w_out, norm_mix=v_norm_mix, w_in=v_w_in,
               gate_bias=v_gate_bias, pool_w=v_pool_w, pool_scale=v_pool_scale, w_ret_up=v_w_ret_up, w_pool_up=v_w_pool_up,
               w_out=v_w_out, norm_ffn2=v_norm_ffn2, ffn2_w_in=v_ffn2_w_in, ffn2_w_out=v_ffn2_w_out, norm_final=v_norm_final)

    ax, ay, ac = lax.axis_index("x"), lax.axis_index("y"), lax.axis_index("c")
    chip = 2 * ax + ay
    c_arr = jnp.reshape(ac, (1,)).astype(jnp.int32)
    sc_arr = jnp.stack([chip, ac]).astype(jnp.int32)
    bias_cols = gate_bias.shape[-1]

    first = {"gate_bias": gate_bias[0], "ffn1_w_in": ffn1_w_in[0].astype(BF16)}
    gather_groups, token = _gather_start("first", [0], first)
    rest, rest_token = _gather_start("rest", [1, 2, 3],
                                     {nm: wt[nm][0].astype(BF16) + token[0, 0].astype(BF16) for nm in BIG if nm not in first})
    gather_groups.update(rest)
    vec = dict(norm_ffn1=norm_ffn1, norm_mix=norm_mix, norm_ffn2=norm_ffn2, pool_scale=pool_scale,
               norm_final=norm_final.reshape(1, D_MODEL))

    relayed = {}

    def relay_w(g, after):
        relayed[g] = _gather_relay(g, gather_groups[g], (after,))

    def get_w(g, after):
        if g in relayed:
            return _gather_land(g, relayed[g], (after,))
        return _gather_finish(g, gather_groups[g], (after, rest_token) if g == 0 else (after,))

    pairs, pending = [], []

    def on_grads(gr):
        g = len(pairs)
        names = GRAD_GROUPS[g]
        assert set(names) == set(gr), (names, list(gr))
        state, token = _pair_exchange_start(g, names, {nm: _grad_view(KIND[nm], gr[nm]) for nm in names})
        pairs.append(state)
        return token[0:1, 0:1]

    def flush(after):
        g = len(pending)
        names = GRAD_GROUPS[g]
        views, from_sib = _pair_exchange_wait(g, names, pairs[g], after)
        psums = {nm: _pair_sum(f"pair_sum_{nm}", KIND[nm], views[nm], from_sib[nm], c_arr) for nm in names}
        state, token = _shard_exchange_start(g, names, psums)
        pending.append(state)
        tokens.append(token)
        return token[0:1, 0:1]

    tokens = []
    loss_local, dx, small = _local_step(x[0], loss_target[0], vec, get_w, relay_w, on_grads, flush)

    grads, delta, new_m, new_v = {}, {}, {}, {}

    def adamw(nm):
        shape = wt[nm].shape
        d, m2, v2 = _adamw(f"adamw_{nm}", _as2d(wt[nm]), _as2d(grads[nm]), _as2d(mom[nm]), _as2d(var[nm]))
        delta[nm], new_m[nm], new_v[nm] = d.reshape(shape), m2.reshape(shape), v2.reshape(shape)
        return d

    def reduce_start(g, after):
        names = GRAD_GROUPS[g]
        psums, from_chips = _shard_exchange_wait(g, names, pending[g], after)
        bufs = {nm: _shard_sum(f"shard_sum_{nm}", KIND[nm], psums[nm], from_chips[nm], sc_arr) for nm in names}
        return _half_exchange_start(g, names, bufs)

    def reduce_finish(g, state, after):
        names = GRAD_GROUPS[g]
        reduced = _half_exchange_wait(g, names, state, after)
        for nm in names:
            grads[nm] = reduced[nm].reshape(wt[nm].shape)
        return tuple(adamw(nm) for nm in names)

    swap0, token = reduce_start(0, (tokens[-1],))
    swap1, token = reduce_start(1, (token,))
    done = reduce_finish(0, swap0, (token,))
    done = reduce_finish(1, swap1, done)
    swap2, token = reduce_start(2, done)
    packed = jnp.concatenate([small["norm_ffn1"], small["norm_mix"], small["gate_bias"], small["pool_scale"],
                              small["norm_ffn2"], small["norm_final"], jnp.broadcast_to(loss_local, (1, D_MODEL))], axis=0)
    small_sum, _ = _all_reduce_small("reduce_small_grads", packed + token[0, 0])
    loss = small_sum[SMALL_ROWS - 1, 0]
    for nm in ("norm_ffn1", "norm_mix", "pool_scale", "norm_ffn2"):
        grads[nm] = small_sum[SMALL_ROW[nm]][None, :]
    grads["norm_final"] = small_sum[SMALL_ROW["norm_final"]]
    grads["gate_bias"] = lax.dynamic_slice(small_sum, (SMALL_ROW["gate_bias"], chip * bias_cols), (2, bias_cols))[None]
    reduce_finish(2, swap2, (small_sum,))
    for nm in WEIGHTS:
        if nm not in delta:
            adamw(nm)

    return (loss, dx[None], *[grads[nm] for nm in WEIGHTS], *[delta[nm] for nm in WEIGHTS],
            *[new_m[nm] for nm in WEIGHTS], *[new_v[nm] for nm in WEIGHTS])
```

```python
import functools

import numpy as np
import jax
import jax.numpy as jnp
from jax import lax
from jax.experimental import pallas as pl
from jax.experimental.pallas import tpu as pltpu

F32 = jnp.float32
BF16 = jnp.bfloat16
MESH = pl.DeviceIdType.MESH

D_MODEL = 1024
D_FF = 2816
HEADS = 4
HEAD_DIM = 256
GROUPS = 4
GROUP_DIM = 256
POOL_WINDOWS = (2, 4, 8, 16)
IN_WIDTH = 7 * D_MODEL
ROPE_BASE = 10000.0
NORM_EPS = 1e-6
FFN_RES_WEIGHT = 0.5
ADAM_LR, ADAM_B1, ADAM_B2, ADAM_EPS, ADAM_WD, ADAM_STEP = 0.001, 0.9, 0.999, 1e-08, 0.01, 10

N_CHIPS = 4
RET_BLOCK = 256
V7X_VMEM_LIMIT = 48 * 1024 * 1024


def _cparams(sem):
    return pltpu.CompilerParams(dimension_semantics=sem, vmem_limit_bytes=V7X_VMEM_LIMIT)


def _sigmoid(x):
    return jax.nn.sigmoid(x)


_DIMS = {"nn": (((1,), (0,)), ((), ())), "nt": (((1,), (1,)), ((), ())), "tn": (((0,), (0,)), ((), ()))}


def _matmul(name, a, b, mode, m, n, k, tm, tn, tk, out_dtypes, a_spec=None, b_spec=None, extras=(), consts=(), epilogue=None,
            resident=None, n_outer=False):
    tm, tn, tk = min(tm, m), min(tn, n), min(tk, k)
    gi, gj, gk = m // tm, n // tn, k // tk
    assert gi * tm == m and gj * tn == n and gk * tk == k, (name, m, n, k, tm, tn, tk)
    assert not (n_outer and (a_spec is not None or b_spec is not None)), name
    once = dict(pipeline_mode=pl.Buffered(1))

    def spec(shape, index, **kw):
        return pl.BlockSpec(shape, (lambda j, i, kk: index(i, j, kk)) if n_outer else index, **kw)

    if a_spec is None:
        kw = once if resident == "a" else {}
        a_spec = (spec((tk, tm), lambda i, j, kk: (kk, i), **kw) if mode == "tn"
                  else spec((tm, tk), lambda i, j, kk: (i, kk), **kw))
    if b_spec is None:
        kw = once if resident == "b" else {}
        b_spec = (spec((tn, tk), lambda i, j, kk: (j, kk), **kw) if mode == "nt"
                  else spec((tk, tn), lambda i, j, kk: (kk, j), **kw))
    n_ex, n_out = len(extras) + len(consts), len(out_dtypes)
    dims = _DIMS[mode]

    def body(a_ref, b_ref, *rest):
        ex_refs, out_refs = rest[:n_ex], rest[n_ex:n_ex + n_out]

        def finish(acc):
            outs = (acc,) if epilogue is None else epilogue(acc, *[e[...] for e in ex_refs])
            for o_ref, o in zip(out_refs, outs):
                o_ref[...] = o.astype(o_ref.dtype)

        prod = lax.dot_general(a_ref[...], b_ref[...], dims, preferred_element_type=F32)
        if gk == 1:
            finish(prod)
        else:
            acc_ref = rest[n_ex + n_out]
            kk = pl.program_id(2)

            @pl.when(kk == 0)
            def _():
                acc_ref[...] = prod

            @pl.when(kk > 0)
            def _():
                acc_ref[...] += prod

            @pl.when(kk == gk - 1)
            def _():
                finish(acc_ref[...])

    o_spec = spec((tm, tn), lambda i, j, kk: (i, j))
    outs = pl.pallas_call(
        body, name=name, grid=(gj, gi, gk) if n_outer else (gi, gj, gk),
        in_specs=[a_spec, b_spec] + [o_spec] * len(extras) + [spec((1, tn), lambda i, j, kk: (0, j))] * len(consts),
        out_specs=[o_spec] * n_out,
        out_shape=[jax.ShapeDtypeStruct((m, n), dt) for dt in out_dtypes],
        scratch_shapes=[pltpu.VMEM((tm, tn), F32)] if gk > 1 else [],
        compiler_params=_cparams(("parallel", "parallel", "arbitrary")),
    )(a, b, *extras, *consts)
    return outs[0] if n_out == 1 else outs


def _row_spec(tm, width, col_block=0):
    return pl.BlockSpec((tm, width), lambda i: (i, col_block))


def _full_spec(shape):
    return pl.BlockSpec(shape, lambda *_: (0,) * len(shape))


def _rmsnorm_fwd(name, h, g, tm=512):
    t = h.shape[0]

    def body(h_ref, g_ref, o_ref):
        x = h_ref[...]
        r = lax.rsqrt(jnp.mean(x * x, axis=-1, keepdims=True) + NORM_EPS)
        o_ref[...] = (x * r * g_ref[...]).astype(BF16)

    return pl.pallas_call(
        body, name=name, grid=(t // tm,),
        in_specs=[_row_spec(tm, D_MODEL), _full_spec((1, D_MODEL))],
        out_specs=_row_spec(tm, D_MODEL),
        out_shape=jax.ShapeDtypeStruct((t, D_MODEL), BF16),
        compiler_params=_cparams(("parallel",)),
    )(h, g)


def _proj_norm_bwd(name, a_list, a_specs, parts, w, h, g, dres, tm):
    t = h.shape[0]
    na = len(a_list)

    def body(*refs):
        a_refs = refs[:na]
        w_ref, h_ref, g_ref, dres_ref, dh_ref, dhb_ref, dg_ref = refs[na:]
        i = pl.program_id(0)
        dn_v = None
        for which, lead, k0, k1 in parts:
            a_ref = a_refs[which]
            term = _dot(a_ref[...] if lead is None else a_ref[lead], w_ref[:, k0:k1], "nt")
            dn_v = term if dn_v is None else dn_v + term
        x = h_ref[...]
        r = lax.rsqrt(jnp.mean(x * x, axis=-1, keepdims=True) + NORM_EPS)
        xh = x * r
        dxh = dn_v * g_ref[...]
        dh = dres_ref[...] + r * (dxh - xh * jnp.mean(dxh * xh, axis=-1, keepdims=True))
        dh_ref[...] = dh
        dhb_ref[...] = dh.astype(BF16)
        part = jnp.sum(dn_v * xh, axis=0, keepdims=True)

        @pl.when(i == 0)
        def _():
            dg_ref[...] = part

        @pl.when(i > 0)
        def _():
            dg_ref[...] += part

    row = _row_spec(tm, D_MODEL)
    return pl.pallas_call(
        body, name=name, grid=(t // tm,),
        in_specs=list(a_specs) + [pl.BlockSpec(w.shape, lambda i: (0, 0), pipeline_mode=pl.Buffered(1)), row,
                                  _full_spec((1, D_MODEL)), row],
        out_specs=[row, row, _full_spec((1, D_MODEL))],
        out_shape=[jax.ShapeDtypeStruct((t, D_MODEL), F32), jax.ShapeDtypeStruct((t, D_MODEL), BF16),
                   jax.ShapeDtypeStruct((1, D_MODEL), F32)],
        compiler_params=_cparams(("arbitrary",)),
    )(*a_list, w, h, g, dres)


def _out_loss_and_grad(name, mid, w_out, h, g, target, tm=512):
    t = h.shape[0]

    def body(m_ref, w_ref, h_ref, g_ref, t_ref, dh_ref, dhb_ref, dg_ref, loss_ref):
        i = pl.program_id(0)
        x = h_ref[...] + FFN_RES_WEIGHT * _dot(m_ref[...], w_ref[...])
        gv = g_ref[...]
        r = lax.rsqrt(jnp.mean(x * x, axis=-1, keepdims=True) + NORM_EPS)
        xh = x * r
        err = xh * gv - t_ref[...]
        row = jnp.mean(err * err, axis=-1, keepdims=True)
        part_loss = 0.5 * jnp.sum(row, axis=0, keepdims=True)
        dy = err * (1.0 / D_MODEL)
        dxh = dy * gv
        dh = r * (dxh - xh * jnp.mean(dxh * xh, axis=-1, keepdims=True))
        dh_ref[...] = dh
        dhb_ref[...] = dh.astype(BF16)
        part = jnp.sum(dy * xh, axis=0, keepdims=True)

        @pl.when(i == 0)
        def _():
            dg_ref[...] = part
            loss_ref[...] = jnp.zeros(loss_ref.shape, F32) + part_loss

        @pl.when(i > 0)
        def _():
            dg_ref[...] += part
            loss_ref[...] += part_loss

    return pl.pallas_call(
        body, name=name, grid=(t // tm,),
        in_specs=[_row_spec(tm, D_FF), pl.BlockSpec((D_FF, D_MODEL), lambda i: (0, 0), pipeline_mode=pl.Buffered(1)),
                  _row_spec(tm, D_MODEL), _full_spec((1, D_MODEL)), _row_spec(tm, D_MODEL)],
        out_specs=[_row_spec(tm, D_MODEL), _row_spec(tm, D_MODEL), _full_spec((1, D_MODEL)), _full_spec((8, 128))],
        out_shape=[jax.ShapeDtypeStruct((t, D_MODEL), F32), jax.ShapeDtypeStruct((t, D_MODEL), BF16),
                   jax.ShapeDtypeStruct((1, D_MODEL), F32), jax.ShapeDtypeStruct((8, 128), F32)],
        compiler_params=_cparams(("arbitrary",)),
    )(mid, w_out, h, g, target)


def _rope_tables(t):
    half = HEAD_DIM // 2
    inv_freq = np.float32(ROPE_BASE) ** (-np.arange(half, dtype=np.float32) / np.float32(half))
    ang = (np.arange(t, dtype=np.float32)[:, None] * inv_freq[None, :].astype(np.float32)).astype(np.float32)
    return jnp.asarray(np.cos(ang.astype(np.float64)).astype(np.float32)), jnp.asarray(np.sin(ang.astype(np.float64)).astype(np.float32))


ROPE_HALF = HEAD_DIM // 2
K_SCALE = HEAD_DIM ** -0.5


def _rotate(ref, hh, c, s, scale=None):
    lo, mid, hi = hh * HEAD_DIM, hh * HEAD_DIM + ROPE_HALF, (hh + 1) * HEAD_DIM
    x1, x2 = ref[:, lo:mid].astype(F32), ref[:, mid:hi].astype(F32)
    y = jnp.concatenate([x1 * c - x2 * s, x1 * s + x2 * c], axis=1)
    return y if scale is None else y * scale


def _unrotate_into(ref, hh, dy, c, s, scale=None):
    lo, mid, hi = hh * HEAD_DIM, hh * HEAD_DIM + ROPE_HALF, (hh + 1) * HEAD_DIM
    y1, y2 = dy[:, :ROPE_HALF], dy[:, ROPE_HALF:]
    d1, d2 = y1 * c + y2 * s, y2 * c - y1 * s
    if scale is not None:
        d1, d2 = d1 * scale, d2 * scale
    ref[:, lo:mid] = d1.astype(ref.dtype)
    ref[:, mid:hi] = d2.astype(ref.dtype)


def _retention_tables():
    b, chunk = RET_BLOCK, 64
    gamma = 1.0 - 2.0 ** (-5.0 - np.arange(HEADS, dtype=np.float64))
    log_g = np.log(gamma)[:, None, None]
    i = np.arange(b)[:, None]
    j = np.arange(b)[None, :]
    same = (i // chunk) == (j // chunk)
    earlier = (j // chunk) < (i // chunk)
    expo = np.where(same, np.abs(i - j), np.where(earlier, i - j, 0)).astype(np.float64)
    mask = np.where(same | earlier, 1.0, 0.0)
    dmat = np.exp(log_g * expo[None]) * mask[None]
    qd = np.exp(log_g[:, :, 0] * (np.arange(b)[None, :] + 1.0))
    kd = np.exp(log_g[:, :, 0] * (b - 1.0 - np.arange(b)[None, :]))
    cd = np.exp(log_g[:, :, 0] * b) * np.ones((1, HEAD_DIM))
    as32 = lambda v: jnp.asarray(v.astype(np.float32))
    return (as32(dmat), as32(np.swapaxes(dmat, 1, 2)), as32(qd[:, :, None]), as32(kd[:, :, None]), as32(cd[:, None, :]))


def _dot(a, b, mode="nn"):
    return lax.dot_general(a, b, _DIMS[mode], preferred_element_type=F32)


GRET_BLOCK = 3


def _head_specs(nb, rev=False):
    pos = (lambda n: nb - 1 - n) if rev else (lambda n: n)
    tok = pl.BlockSpec((RET_BLOCK, D_MODEL), lambda n: (pos(n), 0))
    blk = [pl.BlockSpec((RET_BLOCK, D_MODEL), lambda n, b=b: (pos(n), b)) for b in range(GRET_BLOCK + 1)]
    rope = pl.BlockSpec((RET_BLOCK, ROPE_HALF), lambda n: (pos(n), 0))
    tab = _full_spec((HEADS, RET_BLOCK, RET_BLOCK))
    col = _full_spec((HEADS, RET_BLOCK, 1))
    rowv = _full_spec((HEADS, 1, HEAD_DIM))
    st = pl.BlockSpec((HEADS, None, HEAD_DIM, HEAD_DIM), lambda n: (0, pos(n), 0, 0))
    return tok, blk, rope, tab, col, rowv, st


def _retention_fwd(name, proj, cos, sin, tables):
    t = proj.shape[0]
    nb = t // RET_BLOCK
    dmat, _, qd, kd, cd = tables
    tok, blk, rope, tab, col, rowv, st = _head_specs(nb)

    def body(q_ref, k_ref, v_ref, g_ref, c_ref, s_ref, d_ref, qd_ref, kd_ref, cd_ref, o_ref, ret_ref, st_ref, state):
        n = pl.program_id(0)

        @pl.when(n == 0)
        def _():
            state[...] = jnp.zeros(state.shape, F32)

        cs, sn = c_ref[...], s_ref[...]
        for hh in range(HEADS):
            sl = slice(hh * HEAD_DIM, (hh + 1) * HEAD_DIM)
            q, k, v = _rotate(q_ref, hh, cs, sn), _rotate(k_ref, hh, cs, sn, K_SCALE), v_ref[:, sl].astype(BF16)
            s = _dot(q.astype(BF16), k.astype(BF16), "nt") * d_ref[hh]
            stb = state[hh].astype(BF16)
            st_ref[hh] = stb
            o = _dot(s.astype(BF16), v) + _dot((q * qd_ref[hh]).astype(BF16), stb)
            o_ref[:, sl] = o
            rn = o * lax.rsqrt(jnp.mean(o * o, axis=-1, keepdims=True) + NORM_EPS)
            g = g_ref[:, sl].astype(F32)
            ret_ref[:, sl] = (rn * (g * _sigmoid(g))).astype(BF16)
            state[hh] = state[hh] * cd_ref[hh] + _dot((k * kd_ref[hh]).astype(BF16), v, "tn")

    return pl.pallas_call(
        body, name=name, grid=(nb,),
        in_specs=blk + [rope, rope, tab, col, col, rowv],
        out_specs=[tok, tok, st],
        out_shape=[jax.ShapeDtypeStruct((t, D_MODEL), F32), jax.ShapeDtypeStruct((t, D_MODEL), BF16),
                   jax.ShapeDtypeStruct((HEADS, nb, HEAD_DIM, HEAD_DIM), BF16)],
        scratch_shapes=[pltpu.VMEM((HEADS, HEAD_DIM, HEAD_DIM), F32)],
        compiler_params=_cparams(("arbitrary",)),
    )(proj, proj, proj, proj, cos, sin, dmat, qd, kd, cd)


def _retention_bwd(name, dru, w_ru, o, proj, cos, sin, states, tables):
    t = proj.shape[0]
    nb = t // RET_BLOCK
    dmat, dmat_t, qd, kd, cd = tables
    tok, blk, rope, tab, col, rowv, st = _head_specs(nb, rev=True)

    def body(dru_ref, wru_ref, o_ref, q_ref, k_ref, v_ref, g_ref, c_ref, s_ref, st_ref, d_ref, dt_ref, qd_ref, kd_ref, cd_ref,
             dq_ref, dk_ref, dv_ref, dg_ref, gstate):
        n = pl.program_id(0)

        @pl.when(n == 0)
        def _():
            gstate[...] = jnp.zeros(gstate.shape, F32)

        cs, sn = c_ref[...], s_ref[...]
        dret = _dot(dru_ref[...], wru_ref[...], "nt")
        for hh in range(HEADS):
            sl = slice(hh * HEAD_DIM, (hh + 1) * HEAD_DIM)
            o_v, g, dr = o_ref[:, sl], g_ref[:, sl].astype(F32), dret[:, sl]
            sg = _sigmoid(g)
            r = lax.rsqrt(jnp.mean(o_v * o_v, axis=-1, keepdims=True) + NORM_EPS)
            rn = o_v * r
            d_rn = dr * (g * sg)
            dg_ref[:, sl] = (dr * rn * (sg * (1.0 + g * (1.0 - sg)))).astype(BF16)
            d_o = r * (d_rn - rn * jnp.mean(d_rn * rn, axis=-1, keepdims=True))
            dob = d_o.astype(BF16)

            q, k, v = _rotate(q_ref, hh, cs, sn), _rotate(k_ref, hh, cs, sn, K_SCALE), v_ref[:, sl].astype(BF16)
            qb, kb = q.astype(BF16), k.astype(BF16)
            qdv, kdv = qd_ref[hh], kd_ref[hh]
            s_t = (_dot(kb, qb, "nt") * dt_ref[hh]).astype(BF16)
            p_t = (_dot(v, dob, "nt") * dt_ref[hh]).astype(BF16)
            p = (_dot(dob, v, "nt") * d_ref[hh]).astype(BF16)
            stb = st_ref[hh]
            gb = gstate[hh].astype(BF16)
            _unrotate_into(dq_ref, hh, _dot(p, kb) + _dot(dob, stb, "nt") * qdv, cs, sn)
            _unrotate_into(dk_ref, hh, _dot(p_t, qb) + _dot(v, gb, "nt") * kdv, cs, sn, K_SCALE)
            dv_ref[:, sl] = (_dot(s_t, dob) + _dot((k * kdv).astype(BF16), gb)).astype(BF16)
            gstate[hh] = gstate[hh] * cd_ref[hh] + _dot((q * qdv).astype(BF16), dob, "tn")

    return pl.pallas_call(
        body, name=name, grid=(nb,),
        in_specs=[tok, pl.BlockSpec((D_MODEL, D_MODEL), lambda n: (0, 0), pipeline_mode=pl.Buffered(1)), tok] + blk
                 + [rope, rope, st, tab, tab, col, col, rowv],
        out_specs=[tok, tok, tok, tok],
        out_shape=[jax.ShapeDtypeStruct((t, D_MODEL), BF16)] * 4,
        scratch_shapes=[pltpu.VMEM((HEADS, HEAD_DIM, HEAD_DIM), F32)],
        compiler_params=_cparams(("arbitrary",)),
    )(dru, w_ru, o, proj, proj, proj, proj, cos, sin, states, dmat, dmat_t, qd, kd, cd)


POOL_TILE = 256


def _pool_tables():
    b = POOL_TILE
    tt = np.arange(b)[:, None]
    jj = np.arange(b)[None, :]
    cur, prev = [], []
    for w in POOL_WINDOWS:
        cur.append(((tt - jj >= 0) & (tt - jj <= w - 1)).astype(np.float32))
        prev.append((tt - (jj - b) <= w - 1).astype(np.float32))
    cur, prev = np.stack(cur), np.stack(prev)
    as16 = lambda v: jnp.asarray(v, dtype=BF16)
    return as16(cur), as16(prev), as16(np.swapaxes(cur, 1, 2)), as16(np.swapaxes(prev, 1, 2))


def _split2(x):
    hi = x.astype(BF16)
    return hi, (x - hi.astype(F32)).astype(BF16)


POOL_BLOCK = 4


def _pool_count(n, window):
    tpos = n * POOL_TILE + lax.broadcasted_iota(jnp.int32, (POOL_TILE, 1), 0)
    return jnp.minimum(tpos + 1, window).astype(F32)


def _pool_fwd(name, proj, pool_w, scale, tables):
    t = proj.shape[0]
    nb = t // POOL_TILE
    mc, mp, _, _ = tables
    tab = _full_spec((GROUPS, POOL_TILE, POOL_TILE))
    row = _row_spec(POOL_TILE, D_MODEL)

    def body(pc_ref, pp_ref, mc_ref, mp_ref, w_ref, sc_ref, pm_ref, mix_ref, po_ref):
        n = pl.program_id(0)
        for g, window in enumerate(POOL_WINDOWS):
            sl = slice(g * GROUP_DIM, (g + 1) * GROUP_DIM)
            p = pc_ref[:, sl]
            win = _dot(mc_ref[g], p) + jnp.where(n > 0, _dot(mp_ref[g], pp_ref[:, sl]), 0.0)
            pm = (win / _pool_count(n, window) - p.astype(F32)).astype(BF16)
            pm_ref[:, sl] = pm
            mixed = _dot(pm, w_ref[g])
            mix_ref[:, sl] = mixed
            po_ref[:, sl] = (mixed * sc_ref[:, sl]).astype(BF16)

    return pl.pallas_call(
        body, name=name, grid=(nb,),
        in_specs=[_row_spec(POOL_TILE, D_MODEL, POOL_BLOCK),
                  pl.BlockSpec((POOL_TILE, D_MODEL), lambda n: (jnp.maximum(n - 1, 0), POOL_BLOCK)),
                  tab, tab, _full_spec((GROUPS, GROUP_DIM, GROUP_DIM)), _full_spec((1, D_MODEL))],
        out_specs=[row] * 3,
        out_shape=[jax.ShapeDtypeStruct((t, D_MODEL), BF16), jax.ShapeDtypeStruct((t, D_MODEL), F32),
                   jax.ShapeDtypeStruct((t, D_MODEL), BF16)],
        compiler_params=_cparams(("parallel",)),
    )(proj, proj, mc, mp, pool_w, scale)


def _pool_bwd(name, dpu, w_pu, pm, mixed, pool_w, scale, tables):
    t = dpu.shape[0]
    nb = t // POOL_TILE
    _, _, mct, mpt = tables
    cur = pl.BlockSpec((POOL_TILE, D_MODEL), lambda n: (nb - 1 - n, 0))
    tab = _full_spec((GROUPS, POOL_TILE, POOL_TILE))
    wspec = _full_spec((GROUPS, GROUP_DIM, GROUP_DIM))
    sspec = _full_spec((1, D_MODEL))

    def body(dpu_ref, wpu_ref, pm_ref, mix_ref, mct_ref, mpt_ref, w_ref, sc_ref, dp_ref, dw_ref, ds_ref, later):
        n = pl.program_id(0)

        @pl.when(n == 0)
        def _():
            dw_ref[...] = jnp.zeros(dw_ref.shape, F32)
            ds_ref[...] = jnp.zeros(ds_ref.shape, F32)
            later[...] = jnp.zeros(later.shape, F32)

        dpo = _dot(dpu_ref[...], wpu_ref[...], "nt")
        for g, window in enumerate(POOL_WINDOWS):
            sl = slice(g * GROUP_DIM, (g + 1) * GROUP_DIM)
            dc, sc = dpo[:, sl], sc_ref[:, sl]
            dmix = (dc * sc).astype(BF16)
            dpm = _dot(dmix, w_ref[g], "nt")
            e = dpm / _pool_count(nb - 1 - n, window)
            e_hi, e_lo = _split2(e)
            f_hi, f_lo = _split2(later[g])
            mctv, mptv = mct_ref[g], mpt_ref[g]
            back = _dot(mctv, e_hi) + _dot(mctv, e_lo)
            after = _dot(mptv, f_hi) + _dot(mptv, f_lo)
            dp_ref[:, sl] = (back + after - dpm).astype(BF16)
            later[g] = e
            dw_ref[g] += _dot(pm_ref[:, sl], dmix, "tn")
            ds_ref[:, sl] += jnp.sum(dc * mix_ref[:, sl], axis=0, keepdims=True)

    return pl.pallas_call(
        body, name=name, grid=(nb,),
        in_specs=[cur, pl.BlockSpec((D_MODEL, D_MODEL), lambda n: (0, 0), pipeline_mode=pl.Buffered(1)), cur, cur, tab, tab,
                  wspec, sspec],
        out_specs=[cur, wspec, sspec],
        out_shape=[jax.ShapeDtypeStruct((t, D_MODEL), BF16), jax.ShapeDtypeStruct((GROUPS, GROUP_DIM, GROUP_DIM), F32),
                   jax.ShapeDtypeStruct((1, D_MODEL), F32)],
        scratch_shapes=[pltpu.VMEM((GROUPS, POOL_TILE, GROUP_DIM), F32)],
        compiler_params=_cparams(("arbitrary",)),
    )(dpu, w_pu, pm, mixed, mct, mpt, pool_w, scale)


GATE0_BLOCK, GATE1_BLOCK = 5, 6


def _merge_fwd(name, ret, po, w_ru, w_pu, w_out, proj, bias, h, next_g, tm=512):
    t = ret.shape[0]

    def body(r_ref, p_ref, wr_ref, wp_ref, wo_ref, g0_ref, g1_ref, b_ref, h_ref, ng_ref, m_ref, ru_ref, pu_ref, ho_ref, n_ref):
        ru = _dot(r_ref[...], wr_ref[...])
        pu = _dot(p_ref[...], wp_ref[...])
        ru_ref[...] = ru
        pu_ref[...] = pu
        merged = (_sigmoid(g0_ref[...].astype(F32) + b_ref[0:1, :]) * ru
                  + _sigmoid(g1_ref[...].astype(F32) + b_ref[1:2, :]) * pu).astype(BF16)
        m_ref[...] = merged
        h_new = h_ref[...] + _dot(merged, wo_ref[...])
        ho_ref[...] = h_new
        n_ref[...] = _normed(h_new, ng_ref[...]).astype(BF16)

    row = _row_spec(tm, D_MODEL)
    wspec = pl.BlockSpec((D_MODEL, D_MODEL), lambda i: (0, 0), pipeline_mode=pl.Buffered(1))
    return pl.pallas_call(
        body, name=name, grid=(t // tm,),
        in_specs=[row, row, wspec, wspec, wspec, _row_spec(tm, D_MODEL, GATE0_BLOCK), _row_spec(tm, D_MODEL, GATE1_BLOCK),
                  _full_spec((2, D_MODEL)), row, _full_spec((1, D_MODEL))],
        out_specs=[row] * 5,
        out_shape=[jax.ShapeDtypeStruct((t, D_MODEL), BF16), jax.ShapeDtypeStruct((t, D_MODEL), F32),
                   jax.ShapeDtypeStruct((t, D_MODEL), F32), jax.ShapeDtypeStruct((t, D_MODEL), F32),
                   jax.ShapeDtypeStruct((t, D_MODEL), BF16)],
        compiler_params=_cparams(("parallel",)),
    )(ret, po, w_ru, w_pu, w_out, proj, proj, bias, h, next_g)


def _merge_bwd(name, dh_b, w_out, ru, pu, proj, bias, tm=512):
    t = dh_b.shape[0]

    def body(dh_ref, wo_ref, ru_ref, pu_ref, g0_ref, g1_ref, b_ref, dru_ref, dpu_ref, dg0_ref, dg1_ref, db_ref):
        i = pl.program_id(0)
        d = _dot(dh_ref[...], wo_ref[...], "nt")
        s0 = _sigmoid(g0_ref[...].astype(F32) + b_ref[0:1, :])
        s1 = _sigmoid(g1_ref[...].astype(F32) + b_ref[1:2, :])
        dru_ref[...] = (d * s0).astype(BF16)
        dpu_ref[...] = (d * s1).astype(BF16)
        dg0 = d * ru_ref[...] * (s0 * (1.0 - s0))
        dg1 = d * pu_ref[...] * (s1 * (1.0 - s1))
        dg0_ref[...] = dg0.astype(BF16)
        dg1_ref[...] = dg1.astype(BF16)
        part0 = jnp.sum(dg0, axis=0, keepdims=True)
        part1 = jnp.sum(dg1, axis=0, keepdims=True)

        @pl.when(i == 0)
        def _():
            db_ref[0:1, :] = part0
            db_ref[1:2, :] = part1

        @pl.when(i > 0)
        def _():
            db_ref[0:1, :] += part0
            db_ref[1:2, :] += part1

    row = _row_spec(tm, D_MODEL)
    return pl.pallas_call(
        body, name=name, grid=(t // tm,),
        in_specs=[row, pl.BlockSpec((D_MODEL, D_MODEL), lambda i: (0, 0), pipeline_mode=pl.Buffered(1)), row, row,
                  _row_spec(tm, D_MODEL, GATE0_BLOCK), _row_spec(tm, D_MODEL, GATE1_BLOCK), _full_spec((2, D_MODEL))],
        out_specs=[row, row, row, row, _full_spec((2, D_MODEL))],
        out_shape=[jax.ShapeDtypeStruct((t, D_MODEL), BF16)] * 4 + [jax.ShapeDtypeStruct((2, D_MODEL), F32)],
        compiler_params=_cparams(("arbitrary",)),
    )(dh_b, w_out, ru, pu, proj, proj, bias)


def _half_scale(acc):
    return (FFN_RES_WEIGHT * acc,)


def _normed(h, g):
    return h * lax.rsqrt(jnp.mean(h * h, axis=-1, keepdims=True) + NORM_EPS) * g


def _residual_half_norm(acc, res, g):
    h = res + FFN_RES_WEIGHT * acc
    return h, _normed(h, g)


FF_TILE = D_FF // 2
DW_TILE = 256
SAVED_FF_DTYPE = BF16
FF_CHUNKS = ((0, 512), (512, 1024), (1024, FF_TILE))


def _ffn_in(name, nrm, w_in, tm=512):
    t = nrm.shape[0]
    nj = D_FF // FF_TILE

    def body(n_ref, wg_ref, wu_ref, a_ref, mid_ref):
        nv = n_ref[...]
        for c0, c1 in FF_CHUNKS:
            gate = _dot(nv, wg_ref[:, c0:c1])
            up = _dot(nv, wu_ref[:, c0:c1])
            s = _sigmoid(gate)
            silu = gate * s
            a_ref[0, :, c0:c1] = (FFN_RES_WEIGHT * up * (s * (1.0 + gate * (1.0 - s)))).astype(a_ref.dtype)
            a_ref[1, :, c0:c1] = (FFN_RES_WEIGHT * silu).astype(a_ref.dtype)
            mid_ref[:, c0:c1] = (silu * up).astype(BF16)

    return pl.pallas_call(
        body, name=name, grid=(nj, t // tm),
        in_specs=[pl.BlockSpec((tm, D_MODEL), lambda j, i: (i, 0)),
                  pl.BlockSpec((D_MODEL, FF_TILE), lambda j, i: (0, j)),
                  pl.BlockSpec((D_MODEL, FF_TILE), lambda j, i: (0, j + nj))],
        out_specs=[pl.BlockSpec((2, tm, FF_TILE), lambda j, i: (0, i, j)), pl.BlockSpec((tm, FF_TILE), lambda j, i: (i, j))],
        out_shape=[jax.ShapeDtypeStruct((2, t, D_FF), SAVED_FF_DTYPE), jax.ShapeDtypeStruct((t, D_FF), BF16)],
        compiler_params=_cparams(("parallel", "parallel")),
    )(nrm, w_in, w_in)


def _ffn_dact(name, dout_b, w_out, a, tm=512):
    t = dout_b.shape[0]

    def body(d_ref, w_ref, a_ref, da_ref):
        dv = d_ref[...]
        for c0, c1 in FF_CHUNKS:
            dm = _dot(dv, w_ref[c0:c1, :], "nt")
            da_ref[0, :, c0:c1] = (dm * a_ref[0, :, c0:c1].astype(F32)).astype(BF16)
            da_ref[1, :, c0:c1] = (dm * a_ref[1, :, c0:c1].astype(F32)).astype(BF16)

    blk = pl.BlockSpec((2, tm, FF_TILE), lambda j, i: (0, i, j))
    return pl.pallas_call(
        body, name=name, grid=(D_FF // FF_TILE, t // tm),
        in_specs=[pl.BlockSpec((tm, D_MODEL), lambda j, i: (i, 0)), pl.BlockSpec((FF_TILE, D_MODEL), lambda j, i: (j, 0)), blk],
        out_specs=blk,
        out_shape=jax.ShapeDtypeStruct((2, t, D_FF), BF16),
        compiler_params=_cparams(("parallel", "parallel")),
    )(dout_b, w_out, a)


def _ffn_fwd(tag, h, nrm, get_w_in, get_w_out, finish):
    t = h.shape[0]
    w_in = get_w_in(nrm)
    a, mid = _ffn_in(f"{tag}_in", nrm, w_in, tm=min(512, t))
    w_out = get_w_out(mid)
    return finish(mid, w_out), (nrm, a, mid, w_in, w_out)


def _ffn_bwd(tag, h, g, saved, dout, dout_b, on_grads, flush):
    t = h.shape[0]
    nrm, a, mid, w_in, w_out = saved
    d_w_out = _matmul(f"{tag}_dwout", mid, dout_b, "tn", D_FF, D_MODEL, t, DW_TILE, D_MODEL, t, [BF16], epilogue=_half_scale,
                      resident="b")
    da = _ffn_dact(f"{tag}_dact", dout_b, w_out, a, tm=min(512, t))
    nj = D_FF // DW_TILE
    d_w_in = _dw_resident(f"{tag}_dwin", nrm, [da], [pl.BlockSpec((None, t, DW_TILE), lambda s: (s // nj, 0, s % nj))],
                          2 * nj, None, DW_TILE)
    tie = on_grads({f"{tag}_w_in": d_w_in, f"{tag}_w_out": d_w_out})
    tm = min(256, t)
    dh, dh_b, dg = _proj_norm_bwd(f"{tag}_dn", [da], [pl.BlockSpec((2, tm, D_FF), lambda i: (0, i, 0))],
                                  ((0, 0, 0, D_FF), (0, 1, D_FF, 2 * D_FF)), w_in, h, g if tie is None else g + tie, dout, tm)
    return dh, dh_b, dg, flush(dh)


def _dw_resident(name, u, pieces, piece_specs, n_tiles, which_piece, tn):
    t = u.shape[0]
    npc = len(pieces)

    def body(*refs):
        u_ref, p_refs, o_ref, ut_ref = refs[0], refs[1:1 + npc], refs[1 + npc], refs[2 + npc]
        s = pl.program_id(0)

        @pl.when(s == 0)
        def _():
            ut_ref[...] = u_ref[...].T

        if npc == 1:
            o_ref[...] = _dot(ut_ref[...], p_refs[0][...]).astype(BF16)
        for which in range(npc if npc > 1 else 0):
            @pl.when(which_piece(s) == which)
            def _(which=which):
                o_ref[...] = _dot(ut_ref[...], p_refs[which][...]).astype(BF16)

    return pl.pallas_call(
        body, name=name, grid=(n_tiles,),
        in_specs=[pl.BlockSpec((t, D_MODEL), lambda s: (0, 0), pipeline_mode=pl.Buffered(1))] + list(piece_specs),
        out_specs=pl.BlockSpec((D_MODEL, tn), lambda s: (0, s)),
        out_shape=jax.ShapeDtypeStruct((D_MODEL, n_tiles * tn), BF16),
        scratch_shapes=[pltpu.VMEM((D_MODEL, t), BF16)],
        compiler_params=_cparams(("arbitrary",)),
    )(u, *pieces)


def _mix_dwin(name, u, pieces, tn=256):
    t = u.shape[0]
    nj = D_MODEL // tn
    specs = [pl.BlockSpec((t, tn), lambda s, k=k: (0, jnp.clip(s - k * nj, 0, nj - 1))) for k in range(len(pieces))]
    return _dw_resident(name, u, pieces, specs, len(pieces) * nj, lambda s: s // nj, tn)


def _local_step(x, target, vec, get_w, relay_w, on_grads, flush):
    t = x.shape[0]
    cos, sin = _rope_tables(t)
    rtab = _retention_tables()
    ptab = _pool_tables()
    w = {}

    def getter(group, name):
        def get(after):
            if name not in w:
                w.update(get_w(group, after))
            return w[name]
        return get

    nrm1 = _rmsnorm_fwd("ffn1_norm", x, vec["norm_ffn1"])
    def out_and_norm(mid, w_out):
        return _matmul("ffn1_out", mid, w_out, "nn", t, D_MODEL, D_FF, 512, D_MODEL, D_FF, [F32, BF16],
                       extras=(x,), consts=(vec["norm_mix"],), epilogue=_residual_half_norm)

    (h1, u), s1 = _ffn_fwd("ffn1", x, nrm1, getter(0, "ffn1_w_in"), getter(1, "ffn1_w_out"), out_and_norm)
    w.update(get_w(2, u))
    proj = _matmul("mix_in", u, w["w_in"], "nn", t, IN_WIDTH, D_MODEL, 1024, 1024, D_MODEL, [BF16], n_outer=True)
    o, ret, states = _retention_fwd("retention", proj, cos, sin, rtab)
    pm, mixed, po = _pool_fwd("pool", proj, w["pool_w"], vec["pool_scale"], ptab)
    relay_w(3, po)
    merged, ru, pu, h2, nrm2 = _merge_fwd("merge", ret, po, w["w_ret_up"], w["w_pool_up"], w["w_out"], proj, w["gate_bias"],
                                          h1, vec["norm_ffn2"], tm=min(512, t))
    def out_and_loss(mid, w_out):
        return _out_loss_and_grad("ffn2_out_loss", mid, w_out, h2, vec["norm_final"], target, tm=min(512, t))

    (dh3, dh3_b, dg_final, loss), s2 = _ffn_fwd("ffn2", h2, nrm2, getter(3, "ffn2_w_in"), getter(3, "ffn2_w_out"), out_and_loss)

    def tied(v, tie):
        return v if tie is None else v + tie

    dh2, dh2_b, dg_ffn2, tie = _ffn_bwd("ffn2", h2, vec["norm_ffn2"], s2, dh3, dh3_b, on_grads, flush)
    def square_dw(name, act, grad):
        return _matmul(name, act, grad, "tn", D_MODEL, D_MODEL, t, D_MODEL, D_MODEL, 1024, [BF16])

    d_w_out = square_dw("mix_dwout", merged, dh2_b)
    dru, dpu, dg0, dg1, d_bias = _merge_bwd("merge_bwd", dh2_b, w["w_out"], ru, pu, proj, tied(w["gate_bias"], tie))
    d_w_ru = square_dw("mix_dwru", ret, dru)
    d_w_pu = square_dw("mix_dwpu", po, dpu)
    dp, d_pool_w, d_scale = _pool_bwd("pool_bwd", dpu, w["w_pool_up"], pm, mixed, w["pool_w"], vec["pool_scale"], ptab)
    dq, dk, dv, dgr = _retention_bwd("retention_bwd", dru, w["w_ret_up"], o, proj, cos, sin, states, rtab)
    dproj = [dq, dk, dv, dgr, dp, dg0, dg1]
    d_w_in = _mix_dwin("mix_dwin", u, dproj)
    tie = on_grads(dict(w_in=d_w_in, pool_w=d_pool_w.astype(BF16), w_ret_up=d_w_ru, w_pool_up=d_w_pu, w_out=d_w_out))
    tm = min(256, t)
    dh1, dh1_b, dg_mix = _proj_norm_bwd("mix_du", dproj, [_row_spec(tm, D_MODEL)] * len(dproj),
                                        [(k, None, k * D_MODEL, (k + 1) * D_MODEL) for k in range(len(dproj))],
                                        w["w_in"], h1, tied(vec["norm_mix"], tie), dh2, tm)
    tie = flush(dh1)
    dx, _, dg_ffn1, _ = _ffn_bwd("ffn1", x, tied(vec["norm_ffn1"], tie), s1, dh1, dh1_b, on_grads, flush)

    small = dict(norm_ffn1=dg_ffn1, norm_mix=dg_mix, gate_bias=d_bias, pool_scale=d_scale, norm_ffn2=dg_ffn2,
                 norm_final=dg_final)
    return loss[0, 0], dx, small


BIG = ("ffn1_w_in", "ffn1_w_out", "w_in", "pool_w", "w_ret_up", "w_pool_up", "w_out", "ffn2_w_in", "ffn2_w_out")
KIND = dict(ffn1_w_in="col", ffn1_w_out="row", w_in="col", pool_w="pool", w_ret_up="row", w_pool_up="row", w_out="row",
            ffn2_w_in="col", ffn2_w_out="row", gate_bias="col")
ANY = pl.BlockSpec(memory_space=pl.ANY)


def _place():
    x, y, c = lax.axis_index("x"), lax.axis_index("y"), lax.axis_index("c")
    chips = [(1 - x, y), (x, 1 - y), (1 - x, 1 - y)]
    return x, y, c, chips


def _full_view_shape(kind, local_shape):
    if kind == "col":
        return (2, local_shape[0] // 2, N_CHIPS * local_shape[1])
    if kind == "row":
        return (N_CHIPS, 2, local_shape[0] // 2, local_shape[1])
    return (GROUPS, N_CHIPS, 2, local_shape[1] // 2, local_shape[2])


def _local_view(kind, arr):
    if kind == "pool":
        return arr.reshape(GROUPS, 2, arr.shape[1] // 2, arr.shape[2])
    return arr.reshape(2, arr.shape[0] // 2, arr.shape[1])


def _blk(kind, ref, s, c):
    if kind == "col":
        cs = ref.shape[2] // N_CHIPS
        return ref.at[c, :, pl.ds(pl.multiple_of(s * cs, 128), cs)]
    if kind == "row":
        return ref.at[s, c]
    return ref.at[:, s, c]


def _half(kind, ref, c):
    return ref.at[:, c] if kind == "pool" else ref.at[c]


def _shard(kind, ref, s):
    if kind == "col":
        cs = ref.shape[2] // N_CHIPS
        return ref.at[:, :, pl.ds(pl.multiple_of(s * cs, 128), cs)]
    if kind == "row":
        return ref.at[s]
    return ref.at[:, s]


HBM = pl.BlockSpec(memory_space=pltpu.HBM)
SEM = pl.BlockSpec(memory_space=pltpu.SEMAPHORE)
EFFECT = pltpu.SideEffectType.DATAFLOW_SIDE_EFFECTING
WEIGHT_GROUPS = (("gate_bias", "ffn1_w_in"), ("ffn1_w_out",), ("w_in", "pool_w", "w_ret_up", "w_pool_up", "w_out"), ("ffn2_w_in", "ffn2_w_out"))
GRAD_GROUPS = (("ffn2_w_in", "ffn2_w_out"), ("w_in", "pool_w", "w_ret_up", "w_pool_up", "w_out"), ("ffn1_w_in", "ffn1_w_out"))


def _hbm(a):
    return pltpu.with_memory_space_constraint(a, pltpu.HBM)


def _natural(kind, o):
    if kind == "col":
        return o.reshape(o.shape[0] * o.shape[1], o.shape[2])
    if kind == "row":
        return o.reshape(-1, o.shape[3])
    return o.reshape(GROUPS, -1, o.shape[4])


def _ici_copy(kind, loc, full, j, chips, s, c, send_sem, recv_sem):
    px, py = chips[j]
    return (pltpu.make_async_remote_copy(src_ref=_half(kind, loc, c), dst_ref=_blk(kind, full, s, c), send_sem=send_sem,
                                         recv_sem=recv_sem, device_id=(px, py, c), device_id_type=MESH),
            pltpu.make_async_remote_copy(src_ref=_half(kind, loc, c), dst_ref=_blk(kind, full, 2 * px + py, c), send_sem=send_sem,
                                         recv_sem=recv_sem, device_id=(px, py, c), device_id_type=MESH))


def _gather_start(tag, group_ids, shards):
    grps = [WEIGHT_GROUPS[g] for g in group_ids]
    names = [nm for grp in grps for nm in grp]
    kinds = [KIND[nm] for nm in names]
    n, ng = len(names), len(grps)
    locs = [_hbm(_local_view(KIND[nm], shards[nm])) for nm in names]
    lands = [_hbm(lax.empty(_full_view_shape(KIND[nm], shards[nm].shape), shards[nm].dtype)) for nm in names]
    first = np.cumsum([0] + [len(grp) for grp in grps])

    def body(*refs):
        loc, full = refs[:n], refs[n:2 * n]
        send_sems, recv_sems = refs[2 * n:2 * n + ng], refs[2 * n + ng:2 * n + 2 * ng]
        token = refs[-1]
        x, y, c, chips = _place()
        s = 2 * x + y
        for g in range(ng):
            for a in range(first[g], first[g + 1]):
                for j in range(3):
                    k = 3 * (a - first[g]) + j
                    _ici_copy(kinds[a], loc[a], full[a], j, chips, s, c, send_sems[g].at[k], recv_sems[g].at[k])[0].start()
        token[...] = jnp.zeros(token.shape, F32)

    sem_shapes = [pltpu.SemaphoreType.DMA((3 * len(grp),)) for grp in grps]
    outs = pl.pallas_call(
        body, name=f"gather_start_{tag}",
        in_specs=[HBM] * (2 * n),
        out_specs=[SEM] * (2 * ng) + [HBM] * (2 * n) + [pl.BlockSpec(memory_space=pltpu.VMEM)],
        out_shape=sem_shapes + sem_shapes + [pltpu.HBM(a.shape, a.dtype) for a in locs + lands] + [jax.ShapeDtypeStruct((8, 128), F32)],
        input_output_aliases={i: 2 * ng + i for i in range(2 * n)},
        compiler_params=pltpu.CompilerParams(has_side_effects=EFFECT),
    )(*locs, *lands)
    send_sems, recv_sems = outs[:ng], outs[ng:2 * ng]
    locs_t, lands_t = outs[2 * ng:2 * ng + n], outs[2 * ng + n:2 * ng + 2 * n]
    groups = {}
    for k, g in enumerate(group_ids):
        sl = slice(first[k], first[k + 1])
        groups[g] = (send_sems[k], recv_sems[k], list(locs_t[sl]), list(lands_t[sl]))
    return groups, outs[-1]


def _forward_copies(kinds, loc, full, send_sems, recv_sems):
    x, y, c, chips = _place()
    s = 2 * x + y

    def remote(a, k, src, dst):
        return pltpu.make_async_remote_copy(src_ref=src, dst_ref=dst, send_sem=send_sems.at[4 * a + k],
                                            recv_sem=recv_sems.at[4 * a + k], device_id=(x, y, 1 - c), device_id_type=MESH)

    sends, arrivals = [], []
    for a, kind in enumerate(kinds):
        for j, (px, py) in enumerate(chips):
            theirs, from_sib = _blk(kind, full[a], 2 * px + py, c), _blk(kind, full[a], 2 * px + py, 1 - c)
            sends.append(remote(a, j, theirs, theirs))
            arrivals.append(remote(a, j, from_sib, from_sib))
        own = _shard(kind, full[a], s)
        sends.append(remote(a, 3, loc[a], own))
        arrivals.append(remote(a, 3, own, own))
    return sends, arrivals


def _gather_relay(g, group, after):
    names = WEIGHT_GROUPS[g]
    kinds = [KIND[nm] for nm in names]
    m = len(names)
    ici_send, ici_recv, locs, lands = group

    def body(*refs):
        loc, full = refs[:m], refs[m:2 * m]
        ici_s, ici_r = refs[2 * m], refs[2 * m + 1]
        d2d_s, d2d_r = refs[2 * m + 2 + len(after)], refs[2 * m + 3 + len(after)]
        x, y, c, chips = _place()
        for a in range(m):
            for j in range(3):
                sent, landed = _ici_copy(kinds[a], loc[a], full[a], j, chips, 2 * x + y, c, ici_s.at[3 * a + j], ici_r.at[3 * a + j])
                sent.wait_send()
                landed.wait_recv()
        for cp in _forward_copies(kinds, loc, full, d2d_s, d2d_r)[0]:
            cp.start()

    sem_shape = pltpu.SemaphoreType.DMA((4 * m,))
    outs = pl.pallas_call(
        body, name=f"gather_relay_{g}",
        in_specs=[HBM] * (2 * m) + [SEM, SEM] + [ANY] * len(after), out_specs=[SEM, SEM] + [HBM] * (2 * m),
        out_shape=[sem_shape, sem_shape] + [pltpu.HBM(a.shape, a.dtype) for a in locs + lands],
        input_output_aliases={i: 2 + i for i in range(2 * m)},
        compiler_params=pltpu.CompilerParams(has_side_effects=EFFECT),
    )(*locs, *lands, ici_send, ici_recv, *after)
    return outs[0], outs[1], list(outs[2:2 + m]), list(outs[2 + m:2 + 2 * m])


def _gather_land(g, state, after):
    names = WEIGHT_GROUPS[g]
    kinds = [KIND[nm] for nm in names]
    m = len(names)
    d2d_send, d2d_recv, locs, lands = state

    def body(*refs):
        sends, arrivals = _forward_copies(kinds, refs[:m], refs[m:2 * m], refs[2 * m], refs[2 * m + 1])
        for cp in sends:
            cp.wait_send()
        for cp in arrivals:
            cp.wait_recv()

    outs = pl.pallas_call(
        body, name=f"gather_land_{g}",
        in_specs=[HBM] * (2 * m) + [SEM, SEM] + [ANY] * len(after), out_specs=[HBM] * (2 * m),
        out_shape=[pltpu.HBM(a.shape, a.dtype) for a in locs + lands],
        input_output_aliases={i: i for i in range(2 * m)},
        compiler_params=pltpu.CompilerParams(has_side_effects=EFFECT),
    )(*locs, *lands, d2d_send, d2d_recv, *after)
    return {nm: _natural(k, o) for nm, k, o in zip(names, kinds, outs[m:])}


def _gather_finish(g, group, after):
    names = WEIGHT_GROUPS[g]
    kinds = [KIND[nm] for nm in names]
    m = len(names)
    send_sem, recv_sem, locs, lands = group

    def wait_body(*refs):
        loc, full = refs[:m], refs[m:2 * m]
        send_sems, recv_sems = refs[2 * m], refs[2 * m + 1]
        x, y, c, chips = _place()
        s = 2 * x + y
        for a in range(m):
            for j in range(3):
                k = 3 * a + j
                sent, landed = _ici_copy(kinds[a], loc[a], full[a], j, chips, s, c, send_sems.at[k], recv_sems.at[k])
                sent.wait_send()
                landed.wait_recv()

    outs = pl.pallas_call(
        wait_body, name=f"gather_wait_{g}",
        in_specs=[HBM] * (2 * m) + [SEM, SEM] + [ANY] * len(after), out_specs=[HBM] * (2 * m),
        out_shape=[pltpu.HBM(a.shape, a.dtype) for a in locs + lands],
        input_output_aliases={i: i for i in range(2 * m)},
        compiler_params=pltpu.CompilerParams(has_side_effects=EFFECT),
    )(*locs, *lands, send_sem, recv_sem, *after)
    locs, lands = outs[:m], outs[m:]

    def forward_body(*refs):
        sends, arrivals = _forward_copies(kinds, refs[:m], refs[2 * m:3 * m], *refs[3 * m:])
        for cp in sends:
            cp.start()
        for cp in arrivals:
            cp.wait_recv()
        for cp in sends:
            cp.wait_send()

    outs = pl.pallas_call(
        forward_body, name=f"gather_forward_{g}",
        in_specs=[ANY] * (2 * m), out_specs=[ANY] * m,
        out_shape=[jax.ShapeDtypeStruct(a.shape, a.dtype) for a in lands],
        input_output_aliases={m + i: i for i in range(m)},
        scratch_shapes=[pltpu.SemaphoreType.DMA((4 * m,)), pltpu.SemaphoreType.DMA((4 * m,))],
    )(*locs, *lands)
    return {nm: _natural(k, o) for nm, k, o in zip(names, kinds, outs)}


def _grad_view(kind, g):
    if kind == "col":
        return g.reshape(2, g.shape[0] // 2, g.shape[1])
    if kind == "row":
        return g.reshape(N_CHIPS, 2, g.shape[0] // (2 * N_CHIPS), g.shape[1])
    return g.reshape(GROUPS, N_CHIPS, 2, g.shape[1] // (2 * N_CHIPS), g.shape[2])


def _pair_copies(kinds, g, got, send_sems, recv_sems):
    x, y, c, _ = _place()

    def other_half(kind, ref):
        if kind == "col":
            return ref.at[1 - c]
        if kind == "row":
            return ref.at[:, 1 - c]
        return ref.at[:, :, 1 - c]

    return [pltpu.make_async_remote_copy(src_ref=other_half(kinds[a], g[a]), dst_ref=got[a], send_sem=send_sems.at[a],
                                         recv_sem=recv_sems.at[a], device_id=(x, y, 1 - c), device_id_type=MESH)
            for a in range(len(kinds))]


def _pair_exchange_start(tag, names, views):
    kinds = [KIND[nm] for nm in names]
    n = len(names)

    def got_shape(kind, v):
        if kind == "col":
            return v.shape[1:]
        if kind == "row":
            return (v.shape[0],) + v.shape[2:]
        return v.shape[:2] + v.shape[3:]

    srcs = [_hbm(views[nm]) for nm in names]
    lands = [_hbm(lax.empty(got_shape(k, views[nm]), BF16)) for nm, k in zip(names, kinds)]

    def body(*refs):
        g, got = refs[:n], refs[n:2 * n]
        for cp in _pair_copies(kinds, g, got, refs[2 * n], refs[2 * n + 1]):
            cp.start()
        refs[-1][...] = jnp.zeros(refs[-1].shape, F32)

    sem_shape = pltpu.SemaphoreType.DMA((n,))
    outs = pl.pallas_call(
        body, name=f"grad_pair_exchange_start_{tag}",
        in_specs=[HBM] * (2 * n),
        out_specs=[SEM, SEM] + [HBM] * (2 * n) + [pl.BlockSpec(memory_space=pltpu.VMEM)],
        out_shape=[sem_shape, sem_shape] + [pltpu.HBM(a.shape, a.dtype) for a in srcs + lands] + [jax.ShapeDtypeStruct((8, 128), F32)],
        input_output_aliases={i: 2 + i for i in range(2 * n)},
        compiler_params=pltpu.CompilerParams(has_side_effects=EFFECT),
    )(*srcs, *lands)
    return (outs[0], outs[1], list(outs[2:2 + n]), list(outs[2 + n:2 + 2 * n])), outs[-1]


def _pair_exchange_wait(tag, names, state, after):
    kinds = [KIND[nm] for nm in names]
    n = len(names)
    send_sem, recv_sem, srcs, lands = state

    def body(*refs):
        g, got = refs[:n], refs[n:2 * n]
        for cp in _pair_copies(kinds, g, got, refs[2 * n], refs[2 * n + 1]):
            cp.wait_send()
            cp.wait_recv()

    outs = pl.pallas_call(
        body, name=f"grad_pair_exchange_wait_{tag}",
        in_specs=[HBM] * (2 * n) + [SEM, SEM, ANY], out_specs=[HBM] * (2 * n),
        out_shape=[pltpu.HBM(a.shape, a.dtype) for a in srcs + lands],
        input_output_aliases={i: i for i in range(2 * n)},
        compiler_params=pltpu.CompilerParams(has_side_effects=EFFECT),
    )(*srcs, *lands, send_sem, recv_sem, after)
    return dict(zip(names, outs[:n])), dict(zip(names, outs[n:]))


def _pair_sum(name, kind, view, got, c_arr):
    if kind == "col":
        _, rows, cols = view.shape
        tr = 128
        grid = (rows // tr,)
        v_spec = pl.BlockSpec((None, tr, cols), lambda i, c: (c[0], i, 0))
        g_spec = pl.BlockSpec((tr, cols), lambda i, c: (i, 0))
    elif kind == "row":
        _, _, rows, cols = view.shape
        grid = (N_CHIPS,)
        v_spec = pl.BlockSpec((None, None, rows, cols), lambda i, c: (i, c[0], 0, 0))
        g_spec = pl.BlockSpec((None, rows, cols), lambda i, c: (i, 0, 0))
    else:
        _, _, _, rows, cols = view.shape
        grid = (GROUPS,)
        v_spec = pl.BlockSpec((None, N_CHIPS, None, rows, cols), lambda i, c: (i, 0, c[0], 0, 0))
        g_spec = pl.BlockSpec((None, N_CHIPS, rows, cols), lambda i, c: (i, 0, 0, 0))

    def body(c_ref, v_ref, g_ref, o_ref):
        o_ref[...] = (v_ref[...].astype(F32) + g_ref[...].astype(F32)).astype(BF16)

    return pl.pallas_call(
        body, name=name,
        grid_spec=pltpu.PrefetchScalarGridSpec(num_scalar_prefetch=1, grid=grid, in_specs=[v_spec, g_spec], out_specs=g_spec),
        out_shape=jax.ShapeDtypeStruct(got.shape, BF16),
        compiler_params=_cparams(("parallel",)),
    )(c_arr, view, got)


def _piece(kind, ref, s):
    if kind == "col":
        cs = ref.shape[1] // N_CHIPS
        return ref.at[:, pl.ds(pl.multiple_of(s * cs, 128), cs)]
    if kind == "row":
        return ref.at[s]
    return ref.at[:, s]


def _piece_shape(kind, shape):
    if kind == "col":
        return (shape[0], shape[1] // N_CHIPS)
    if kind == "row":
        return shape[1:]
    return (shape[0],) + shape[2:]


def _shard_copies(kinds, p, got, send_sems, recv_sems):
    x, y, c, chips = _place()
    return [pltpu.make_async_remote_copy(src_ref=_piece(kinds[a], p[a], 2 * px + py), dst_ref=got[a].at[j],
                                         send_sem=send_sems.at[3 * a + j], recv_sem=recv_sems.at[3 * a + j],
                                         device_id=(px, py, c), device_id_type=MESH)
            for a in range(len(kinds)) for j, (px, py) in enumerate(chips)]


def _shard_exchange_start(g, names, psums):
    kinds = [KIND[nm] for nm in names]
    n = len(names)
    srcs = [_hbm(psums[nm]) for nm in names]
    lands = [_hbm(lax.empty((3,) + _piece_shape(k, psums[nm].shape), BF16)) for nm, k in zip(names, kinds)]

    def body(*refs):
        p, got = refs[:n], refs[n:2 * n]
        send_sems, recv_sems = refs[2 * n], refs[2 * n + 1]
        token = refs[-1]
        for cp in _shard_copies(kinds, p, got, send_sems, recv_sems):
            cp.start()
        token[...] = jnp.zeros(token.shape, F32)

    sem_shape = pltpu.SemaphoreType.DMA((3 * n,))
    outs = pl.pallas_call(
        body, name=f"grad_shard_exchange_start_{g}",
        in_specs=[HBM] * (2 * n),
        out_specs=[SEM, SEM] + [HBM] * (2 * n) + [pl.BlockSpec(memory_space=pltpu.VMEM)],
        out_shape=[sem_shape, sem_shape] + [pltpu.HBM(a.shape, a.dtype) for a in srcs + lands] + [jax.ShapeDtypeStruct((8, 128), F32)],
        input_output_aliases={i: 2 + i for i in range(2 * n)},
        compiler_params=pltpu.CompilerParams(has_side_effects=EFFECT),
    )(*srcs, *lands)
    return (outs[0], outs[1], list(outs[2:2 + n]), list(outs[2 + n:2 + 2 * n])), outs[-1]


def _shard_exchange_wait(g, names, state, after):
    kinds = [KIND[nm] for nm in names]
    n = len(names)
    send_sem, recv_sem, srcs, lands = state

    def body(*refs):
        p, got = refs[:n], refs[n:2 * n]
        for cp in _shard_copies(kinds, p, got, refs[2 * n], refs[2 * n + 1]):
            cp.wait_send()
            cp.wait_recv()

    outs = pl.pallas_call(
        body, name=f"grad_shard_exchange_wait_{g}",
        in_specs=[HBM] * (2 * n) + [SEM, SEM] + [ANY] * len(after), out_specs=[HBM] * (2 * n),
        out_shape=[pltpu.HBM(a.shape, a.dtype) for a in srcs + lands],
        input_output_aliases={i: i for i in range(2 * n)},
        compiler_params=pltpu.CompilerParams(has_side_effects=EFFECT),
    )(*srcs, *lands, send_sem, recv_sem, *after)
    return dict(zip(names, outs[:n])), dict(zip(names, outs[n:]))


def _shard_sum(name, kind, psum, got, sc_arr):
    if kind == "col":
        rows, cols = psum.shape
        cs = cols // N_CHIPS
        tr = 128
        grid = (rows // tr,)
        p_spec = pl.BlockSpec((tr, cs), lambda i, sc: (i, sc[0]))
        g_spec = pl.BlockSpec((3, tr, cs), lambda i, sc: (0, i, 0))
        o_spec = pl.BlockSpec((None, tr, cs), lambda i, sc: (sc[1], i, 0))
        out_shape = (2, rows, cs)
    elif kind == "row":
        _, rows, cols = psum.shape
        grid = (1,)
        p_spec = pl.BlockSpec((None, rows, cols), lambda i, sc: (sc[0], 0, 0))
        g_spec = pl.BlockSpec((3, rows, cols), lambda i, sc: (0, 0, 0))
        o_spec = pl.BlockSpec((None, rows, cols), lambda i, sc: (sc[1], 0, 0))
        out_shape = (2, rows, cols)
    else:
        _, _, rows, cols = psum.shape
        grid = (1,)
        p_spec = pl.BlockSpec((GROUPS, None, rows, cols), lambda i, sc: (0, sc[0], 0, 0))
        g_spec = pl.BlockSpec((3, GROUPS, rows, cols), lambda i, sc: (0, 0, 0, 0))
        o_spec = pl.BlockSpec((GROUPS, None, rows, cols), lambda i, sc: (0, sc[1], 0, 0))
        out_shape = (GROUPS, 2, rows, cols)

    def body(sc_ref, p_ref, g_ref, o_ref):
        o_ref[...] = ((p_ref[...].astype(F32) + g_ref[0].astype(F32)) + g_ref[1].astype(F32)) + g_ref[2].astype(F32)

    return pl.pallas_call(
        body, name=name,
        grid_spec=pltpu.PrefetchScalarGridSpec(num_scalar_prefetch=1, grid=grid, in_specs=[p_spec, g_spec], out_specs=o_spec),
        out_shape=jax.ShapeDtypeStruct(out_shape, F32),
        compiler_params=_cparams(("parallel",)),
    )(sc_arr, psum, got)


def _half_copies(kinds, bufs, send_sems, recv_sems):
    x, y, c, _ = _place()

    def remote(a, half):
        part = _half(kinds[a], bufs[a], half)
        return pltpu.make_async_remote_copy(src_ref=part, dst_ref=part, send_sem=send_sems.at[a], recv_sem=recv_sems.at[a],
                                            device_id=(x, y, 1 - c), device_id_type=MESH)

    return [remote(a, c) for a in range(len(kinds))], [remote(a, 1 - c) for a in range(len(kinds))]


def _half_exchange_start(tag, names, bufs):
    kinds = [KIND[nm] for nm in names]
    n = len(names)
    arrs = [_hbm(bufs[nm]) for nm in names]

    def body(*refs):
        for cp in _half_copies(kinds, refs[:n], refs[n], refs[n + 1])[0]:
            cp.start()
        refs[-1][...] = jnp.zeros(refs[-1].shape, F32)

    sem_shape = pltpu.SemaphoreType.DMA((n,))
    outs = pl.pallas_call(
        body, name=f"grad_half_exchange_start_{tag}",
        in_specs=[HBM] * n,
        out_specs=[SEM, SEM] + [HBM] * n + [pl.BlockSpec(memory_space=pltpu.VMEM)],
        out_shape=[sem_shape, sem_shape] + [pltpu.HBM(a.shape, a.dtype) for a in arrs] + [jax.ShapeDtypeStruct((8, 128), F32)],
        input_output_aliases={i: 2 + i for i in range(n)},
        compiler_params=pltpu.CompilerParams(has_side_effects=EFFECT),
    )(*arrs)
    return (outs[0], outs[1], list(outs[2:2 + n])), outs[-1]


def _half_exchange_wait(tag, names, state, after):
    kinds = [KIND[nm] for nm in names]
    n = len(names)
    send_sem, recv_sem, arrs = state

    def body(*refs):
        sends, arrivals = _half_copies(kinds, refs[:n], refs[n], refs[n + 1])
        for cp in sends:
            cp.wait_send()
        for cp in arrivals:
            cp.wait_recv()

    outs = pl.pallas_call(
        body, name=f"grad_half_exchange_wait_{tag}",
        in_specs=[HBM] * n + [SEM, SEM] + [ANY] * len(after), out_specs=[HBM] * n,
        out_shape=[pltpu.HBM(a.shape, a.dtype) for a in arrs],
        input_output_aliases={i: i for i in range(n)},
        compiler_params=pltpu.CompilerParams(has_side_effects=EFFECT),
    )(*arrs, send_sem, recv_sem, *after)
    return dict(zip(names, outs))


N_DEV = 8
SMALL_ROWS = 8


def _all_reduce_small(name, v):
    def body(v_ref, o_ref, token, buf, send_sems, recv_sems):
        token[...] = jnp.zeros(token.shape, F32)
        x, y, c, _ = _place()
        me = 4 * x + 2 * y + c
        buf[me] = v_ref[...]
        cps = []
        for r in range(1, N_DEV):
            to = (x ^ (r >> 2), y ^ ((r >> 1) & 1), c ^ (r & 1))
            cp = pltpu.make_async_remote_copy(src_ref=v_ref, dst_ref=buf.at[me], send_sem=send_sems.at[r - 1],
                                              recv_sem=recv_sems.at[r - 1], device_id=to, device_id_type=MESH)
            cp.start()
            cps.append(cp)
        for r in range(1, N_DEV):
            pltpu.make_async_remote_copy(src_ref=v_ref, dst_ref=buf.at[me ^ r], send_sem=send_sems.at[r - 1],
                                         recv_sem=recv_sems.at[r - 1], device_id=(x, y, c), device_id_type=MESH).wait_recv()
        for cp in cps:
            cp.wait_send()
        acc = buf[0]
        for d in range(1, N_DEV):
            acc = acc + buf[d]
        o_ref[...] = acc

    vm = pl.BlockSpec(memory_space=pltpu.VMEM)
    return pl.pallas_call(
        body, name=name, in_specs=[vm], out_specs=[vm, vm],
        out_shape=[jax.ShapeDtypeStruct((SMALL_ROWS, D_MODEL), F32), jax.ShapeDtypeStruct((8, 128), F32)],
        scratch_shapes=[pltpu.VMEM((N_DEV, SMALL_ROWS, D_MODEL), F32), pltpu.SemaphoreType.DMA((N_DEV - 1,)),
                        pltpu.SemaphoreType.DMA((N_DEV - 1,))],
    )(v)


def _adamw(name, w, g, m, v, with_grad=False):
    rows, cols = w.shape
    tr = next((c for c in (64, 32, 8) if rows % c == 0), rows)
    spec = pl.BlockSpec((tr, cols), lambda i: (i, 0))

    def body(w_ref, g_ref, m_ref, v_ref, d_ref, mo_ref, vo_ref, *go_ref):
        gv = g_ref[...]
        if with_grad:
            go_ref[0][...] = gv
        m_new = ADAM_B1 * m_ref[...] + (1.0 - ADAM_B1) * gv
        v_new = ADAM_B2 * v_ref[...] + (1.0 - ADAM_B2) * jnp.square(gv)
        m_hat = m_new / (1.0 - ADAM_B1 ** ADAM_STEP)
        v_hat = v_new / (1.0 - ADAM_B2 ** ADAM_STEP)
        d_ref[...] = -ADAM_LR * (m_hat / (jnp.sqrt(v_hat) + ADAM_EPS) + ADAM_WD * w_ref[...])
        mo_ref[...] = m_new
        vo_ref[...] = v_new

    return pl.pallas_call(
        body, name=name, grid=(rows // tr,),
        in_specs=[spec] * 4, out_specs=[spec] * (4 if with_grad else 3),
        out_shape=[jax.ShapeDtypeStruct((rows, cols), F32)] * (4 if with_grad else 3),
        compiler_params=_cparams(("parallel",)),
    )(w, g, m, v)


WEIGHTS = ("norm_ffn1", "ffn1_w_in", "ffn1_w_out", "norm_mix", "w_in", "gate_bias", "pool_w", "pool_scale", "w_ret_up",
           "w_pool_up", "w_out", "norm_ffn2", "ffn2_w_in", "ffn2_w_out", "norm_final")
SMALL_ROW = dict(norm_ffn1=0, norm_mix=1, gate_bias=2, pool_scale=4, norm_ffn2=5, norm_final=6)


def _as2d(a):
    return a.reshape(-1, a.shape[-1])


def kernel(x, norm_ffn1, ffn1_w_in, ffn1_w_out, norm_mix, w_in, gate_bias, pool_w, pool_scale, w_ret_up, w_pool_up, w_out, norm_ffn2, ffn2_w_in, ffn2_w_out, norm_final, loss_target, m_norm_ffn1, m_ffn1_w_in, m_ffn1_w_out, m_norm_mix, m_w_in, m_gate_bias, m_pool_w, m_pool_scale, m_w_ret_up, m_w_pool_up, m_w_out, m_norm_ffn2, m_ffn2_w_in, m_ffn2_w_out, m_norm_final, v_norm_ffn1, v_ffn1_w_in, v_ffn1_w_out, v_norm_mix, v_w_in, v_gate_bias, v_pool_w, v_pool_scale, v_w_ret_up, v_w_pool_up, v_w_out, v_norm_ffn2, v_ffn2_w_in, v_ffn2_w_out, v_norm_final):
    wt = dict(norm_ffn1=norm_ffn1, ffn1_w_in=ffn1_w_in, ffn1_w_out=ffn1_w_out, norm_mix=norm_mix, w_in=w_in, gate_bias=gate_bias,
              pool_w=pool_w, pool_scale=pool_scale, w_ret_up=w_ret_up, w_pool_up=w_pool_up, w_out=w_out, norm_ffn2=norm_ffn2,
              ffn2_w_in=ffn2_w_in, ffn2_w_out=ffn2_w_out, norm_final=norm_final)
    mom = dict(norm_ffn1=m_norm_ffn1, ffn1_w_in=m_ffn1_w_in, ffn1_w_out=m_ffn1_w_out, norm_mix=m_norm_mix, w_in=m_w_in,
               gate_bias=m_gate_bias, pool_w=m_pool_w, pool_scale=m_pool_scale, w_ret_up=m_w_ret_up, w_pool_up=m_w_pool_up,
               w_out=m_w_out, norm_ffn2=m_norm_ffn2, ffn2_w_in=m_ffn2_w_in, ffn2_w_out=m_ffn2_w_out, norm_final=m_norm_final)
    var = dict(norm_ffn1=v_norm_ffn1, ffn1_w_in=v_ffn1_w_in, ffn1_w_out=v_ffn1_w_out, norm_mix=v_norm_mix, w_in=v_w_in,
               gate_bias=v_gate_bias, pool_w=v_pool_w, pool_scale=v_pool_scale, w_ret_up=v_w_ret_up, w_pool_up=v_w_pool_up,
               w_out=v_w_out, norm_ffn2=v_norm_ffn2, ffn2_w_in=v_ffn2_w_in, ffn2_w_out=v_ffn2_w_out, norm_final=v_norm_final)

    ax, ay, ac = lax.axis_index("x"), lax.axis_index("y"), lax.axis_index("c")
    chip = 2 * ax + ay
    c_arr = jnp.reshape(ac, (1,)).astype(jnp.int32)
    sc_arr = jnp.stack([chip, ac]).astype(jnp.int32)
    bias_cols = gate_bias.shape[-1]

    first = {"gate_bias": gate_bias[0], "ffn1_w_in": ffn1_w_in[0].astype(BF16)}
    gather_groups, token = _gather_start("first", [0], first)
    rest, rest_token = _gather_start("rest", [1, 2, 3],
                                     {nm: wt[nm][0].astype(BF16) + token[0, 0].astype(BF16) for nm in BIG if nm not in first})
    gather_groups.update(rest)
    vec = dict(norm_ffn1=norm_ffn1, norm_mix=norm_mix, norm_ffn2=norm_ffn2, pool_scale=pool_scale,
               norm_final=norm_final.reshape(1, D_MODEL))

    relayed = {}

    def relay_w(g, after):
        relayed[g] = _gather_relay(g, gather_groups[g], (after,))

    def get_w(g, after):
        if g in relayed:
            return _gather_land(g, relayed[g], (after,))
        return _gather_finish(g, gather_groups[g], (after, rest_token) if g == 0 else (after,))

    pairs, pending = [], []

    def on_grads(gr):
        g = len(pairs)
        names = GRAD_GROUPS[g]
        assert set(names) == set(gr), (names, list(gr))
        state, token = _pair_exchange_start(g, names, {nm: _grad_view(KIND[nm], gr[nm]) for nm in names})
        pairs.append(state)
        return token[0:1, 0:1]

    def flush(after):
        g = len(pending)
        names = GRAD_GROUPS[g]
        views, from_sib = _pair_exchange_wait(g, names, pairs[g], after)
        psums = {nm: _pair_sum(f"pair_sum_{nm}", KIND[nm], views[nm], from_sib[nm], c_arr) for nm in names}
        state, token = _shard_exchange_start(g, names, psums)
        pending.append(state)
        tokens.append(token)
        return token[0:1, 0:1]

    tokens = []
    loss_local, dx, small = _local_step(x[0], loss_target[0], vec, get_w, relay_w, on_grads, flush)

    grads, delta, new_m, new_v = {}, {}, {}, {}

    def adamw(nm):
        shape = wt[nm].shape
        outs = _adamw(f"adamw_{nm}", _as2d(wt[nm]), _as2d(grads[nm]), _as2d(mom[nm]), _as2d(var[nm]), with_grad=nm in BIG)
        delta[nm], new_m[nm], new_v[nm] = (o.reshape(shape) for o in outs[:3])
        if nm in BIG:
            grads[nm] = outs[3].reshape(shape)
        return outs[0]

    def reduce_start(g, after):
        names = GRAD_GROUPS[g]
        psums, from_chips = _shard_exchange_wait(g, names, pending[g], after)
        bufs = {nm: _shard_sum(f"shard_sum_{nm}", KIND[nm], psums[nm], from_chips[nm], sc_arr) for nm in names}
        return _half_exchange_start(g, names, bufs)

    def reduce_finish(g, state, after):
        names = GRAD_GROUPS[g]
        reduced = _half_exchange_wait(g, names, state, after)
        for nm in names:
            grads[nm] = reduced[nm].reshape(wt[nm].shape)
        return tuple(adamw(nm) for nm in names)

    swap0, token = reduce_start(0, (tokens[-1],))
    swap1, token = reduce_start(1, (token,))
    done = reduce_finish(0, swap0, (token,))
    done = reduce_finish(1, swap1, done)
    swap2, token = reduce_start(2, done)
    packed = jnp.concatenate([small["norm_ffn1"], small["norm_mix"], small["gate_bias"], small["pool_scale"],
                              small["norm_ffn2"], small["norm_final"], jnp.broadcast_to(loss_local, (1, D_MODEL))], axis=0)
    small_sum, _ = _all_reduce_small("reduce_small_grads", packed + token[0, 0])
    loss = small_sum[SMALL_ROWS - 1, 0]
    for nm in ("norm_ffn1", "norm_mix", "pool_scale", "norm_ffn2"):
        grads[nm] = small_sum[SMALL_ROW[nm]][None, :]
    grads["norm_final"] = small_sum[SMALL_ROW["norm_final"]]
    grads["gate_bias"] = lax.dynamic_slice(small_sum, (SMALL_ROW["gate_bias"], chip * bias_cols), (2, bias_cols))[None]
    reduce_finish(2, swap2, (small_sum,))
    for nm in WEIGHTS:
        if nm not in delta:
            adamw(nm)

    return (loss, dx[None], *[grads[nm] for nm in WEIGHTS], *[delta[nm] for nm in WEIGHTS],
            *[new_m[nm] for nm in WEIGHTS], *[new_v[nm] for nm in WEIGHTS])
```

```python
import functools

import numpy as np
import jax
import jax.numpy as jnp
from jax import lax
from jax.experimental import pallas as pl
from jax.experimental.pallas import tpu as pltpu

F32 = jnp.float32
BF16 = jnp.bfloat16
MESH = pl.DeviceIdType.MESH

D_MODEL = 1024
D_FF = 2816
HEADS = 4
HEAD_DIM = 256
GROUPS = 4
GROUP_DIM = 256
POOL_WINDOWS = (2, 4, 8, 16)
IN_WIDTH = 7 * D_MODEL
ROPE_BASE = 10000.0
NORM_EPS = 1e-6
FFN_RES_WEIGHT = 0.5
ADAM_LR, ADAM_B1, ADAM_B2, ADAM_EPS, ADAM_WD, ADAM_STEP = 0.001, 0.9, 0.999, 1e-08, 0.01, 10

N_CHIPS = 4
RET_BLOCK = 256
V7X_VMEM_LIMIT = 48 * 1024 * 1024


def _cparams(sem):
    return pltpu.CompilerParams(dimension_semantics=sem, vmem_limit_bytes=V7X_VMEM_LIMIT)


def _sigmoid(x):
    return jax.nn.sigmoid(x)


_DIMS = {"nn": (((1,), (0,)), ((), ())), "nt": (((1,), (1,)), ((), ())), "tn": (((0,), (0,)), ((), ()))}


def _matmul(name, a, b, mode, m, n, k, tm, tn, tk, out_dtypes, a_spec=None, b_spec=None, extras=(), consts=(), epilogue=None,
            resident=None, n_outer=False):
    tm, tn, tk = min(tm, m), min(tn, n), min(tk, k)
    gi, gj, gk = m // tm, n // tn, k // tk
    assert gi * tm == m and gj * tn == n and gk * tk == k, (name, m, n, k, tm, tn, tk)
    assert not (n_outer and (a_spec is not None or b_spec is not None)), name
    once = dict(pipeline_mode=pl.Buffered(1))

    def spec(shape, index, **kw):
        return pl.BlockSpec(shape, (lambda j, i, kk: index(i, j, kk)) if n_outer else index, **kw)

    if a_spec is None:
        kw = once if resident == "a" else {}
        a_spec = (spec((tk, tm), lambda i, j, kk: (kk, i), **kw) if mode == "tn"
                  else spec((tm, tk), lambda i, j, kk: (i, kk), **kw))
    if b_spec is None:
        kw = once if resident == "b" else {}
        b_spec = (spec((tn, tk), lambda i, j, kk: (j, kk), **kw) if mode == "nt"
                  else spec((tk, tn), lambda i, j, kk: (kk, j), **kw))
    n_ex, n_out = len(extras) + len(consts), len(out_dtypes)
    dims = _DIMS[mode]

    def body(a_ref, b_ref, *rest):
        ex_refs, out_refs = rest[:n_ex], rest[n_ex:n_ex + n_out]

        def finish(acc):
            outs = (acc,) if epilogue is None else epilogue(acc, *[e[...] for e in ex_refs])
            for o_ref, o in zip(out_refs, outs):
                o_ref[...] = o.astype(o_ref.dtype)

        prod = lax.dot_general(a_ref[...], b_ref[...], dims, preferred_element_type=F32)
        if gk == 1:
            finish(prod)
        else:
            acc_ref = rest[n_ex + n_out]
            kk = pl.program_id(2)

            @pl.when(kk == 0)
            def _():
                acc_ref[...] = prod

            @pl.when(kk > 0)
            def _():
                acc_ref[...] += prod

            @pl.when(kk == gk - 1)
            def _():
                finish(acc_ref[...])

    o_spec = spec((tm, tn), lambda i, j, kk: (i, j))
    outs = pl.pallas_call(
        body, name=name, grid=(gj, gi, gk) if n_outer else (gi, gj, gk),
        in_specs=[a_spec, b_spec] + [o_spec] * len(extras) + [spec((1, tn), lambda i, j, kk: (0, j))] * len(consts),
        out_specs=[o_spec] * n_out,
        out_shape=[jax.ShapeDtypeStruct((m, n), dt) for dt in out_dtypes],
        scratch_shapes=[pltpu.VMEM((tm, tn), F32)] if gk > 1 else [],
        compiler_params=_cparams(("parallel", "parallel", "arbitrary")),
    )(a, b, *extras, *consts)
    return outs[0] if n_out == 1 else outs


def _row_spec(tm, width, col_block=0):
    return pl.BlockSpec((tm, width), lambda i: (i, col_block))


def _full_spec(shape):
    return pl.BlockSpec(shape, lambda *_: (0,) * len(shape))


def _rmsnorm_fwd(name, h, g, tm=512):
    t = h.shape[0]

    def body(h_ref, g_ref, o_ref):
        x = h_ref[...]
        r = lax.rsqrt(jnp.mean(x * x, axis=-1, keepdims=True) + NORM_EPS)
        o_ref[...] = (x * r * g_ref[...]).astype(BF16)

    return pl.pallas_call(
        body, name=name, grid=(t // tm,),
        in_specs=[_row_spec(tm, D_MODEL), _full_spec((1, D_MODEL))],
        out_specs=_row_spec(tm, D_MODEL),
        out_shape=jax.ShapeDtypeStruct((t, D_MODEL), BF16),
        compiler_params=_cparams(("parallel",)),
    )(h, g)


def _proj_norm_bwd(name, a_list, a_specs, parts, w, h, g, dres, tm):
    t = h.shape[0]
    na = len(a_list)

    def body(*refs):
        a_refs = refs[:na]
        w_ref, h_ref, g_ref, dres_ref, dh_ref, dhb_ref, dg_ref = refs[na:]
        i = pl.program_id(0)
        dn_v = None
        for which, lead, k0, k1 in parts:
            a_ref = a_refs[which]
            term = _dot(a_ref[...] if lead is None else a_ref[lead], w_ref[:, k0:k1], "nt")
            dn_v = term if dn_v is None else dn_v + term
        x = h_ref[...]
        r = lax.rsqrt(jnp.mean(x * x, axis=-1, keepdims=True) + NORM_EPS)
        xh = x * r
        dxh = dn_v * g_ref[...]
        dh = dres_ref[...] + r * (dxh - xh * jnp.mean(dxh * xh, axis=-1, keepdims=True))
        dh_ref[...] = dh
        dhb_ref[...] = dh.astype(BF16)
        part = jnp.sum(dn_v * xh, axis=0, keepdims=True)

        @pl.when(i == 0)
        def _():
            dg_ref[...] = part

        @pl.when(i > 0)
        def _():
            dg_ref[...] += part

    row = _row_spec(tm, D_MODEL)
    return pl.pallas_call(
        body, name=name, grid=(t // tm,),
        in_specs=list(a_specs) + [pl.BlockSpec(w.shape, lambda i: (0, 0), pipeline_mode=pl.Buffered(1)), row,
                                  _full_spec((1, D_MODEL)), row],
        out_specs=[row, row, _full_spec((1, D_MODEL))],
        out_shape=[jax.ShapeDtypeStruct((t, D_MODEL), F32), jax.ShapeDtypeStruct((t, D_MODEL), BF16),
                   jax.ShapeDtypeStruct((1, D_MODEL), F32)],
        compiler_params=_cparams(("arbitrary",)),
    )(*a_list, w, h, g, dres)


def _out_loss_and_grad(name, mid, w_out, h, g, target, tm=512):
    t = h.shape[0]

    def body(m_ref, w_ref, h_ref, g_ref, t_ref, dh_ref, dhb_ref, dg_ref, loss_ref):
        i = pl.program_id(0)
        x = h_ref[...] + FFN_RES_WEIGHT * _dot(m_ref[...], w_ref[...])
        gv = g_ref[...]
        r = lax.rsqrt(jnp.mean(x * x, axis=-1, keepdims=True) + NORM_EPS)
        xh = x * r
        err = xh * gv - t_ref[...]
        row = jnp.mean(err * err, axis=-1, keepdims=True)
        part_loss = 0.5 * jnp.sum(row, axis=0, keepdims=True)
        dy = err * (1.0 / D_MODEL)
        dxh = dy * gv
        dh = r * (dxh - xh * jnp.mean(dxh * xh, axis=-1, keepdims=True))
        dh_ref[...] = dh
        dhb_ref[...] = dh.astype(BF16)
        part = jnp.sum(dy * xh, axis=0, keepdims=True)

        @pl.when(i == 0)
        def _():
            dg_ref[...] = part
            loss_ref[...] = jnp.zeros(loss_ref.shape, F32) + part_loss

        @pl.when(i > 0)
        def _():
            dg_ref[...] += part
            loss_ref[...] += part_loss

    return pl.pallas_call(
        body, name=name, grid=(t // tm,),
        in_specs=[_row_spec(tm, D_FF), pl.BlockSpec((D_FF, D_MODEL), lambda i: (0, 0), pipeline_mode=pl.Buffered(1)),
                  _row_spec(tm, D_MODEL), _full_spec((1, D_MODEL)), _row_spec(tm, D_MODEL)],
        out_specs=[_row_spec(tm, D_MODEL), _row_spec(tm, D_MODEL), _full_spec((1, D_MODEL)), _full_spec((8, 128))],
        out_shape=[jax.ShapeDtypeStruct((t, D_MODEL), F32), jax.ShapeDtypeStruct((t, D_MODEL), BF16),
                   jax.ShapeDtypeStruct((1, D_MODEL), F32), jax.ShapeDtypeStruct((8, 128), F32)],
        compiler_params=_cparams(("arbitrary",)),
    )(mid, w_out, h, g, target)


def _rope_tables(t):
    half = HEAD_DIM // 2
    inv_freq = np.float32(ROPE_BASE) ** (-np.arange(half, dtype=np.float32) / np.float32(half))
    ang = (np.arange(t, dtype=np.float32)[:, None] * inv_freq[None, :].astype(np.float32)).astype(np.float32)
    return jnp.asarray(np.cos(ang.astype(np.float64)).astype(np.float32)), jnp.asarray(np.sin(ang.astype(np.float64)).astype(np.float32))


ROPE_HALF = HEAD_DIM // 2
K_SCALE = HEAD_DIM ** -0.5


def _rotate(ref, hh, c, s, scale=None):
    lo, mid, hi = hh * HEAD_DIM, hh * HEAD_DIM + ROPE_HALF, (hh + 1) * HEAD_DIM
    x1, x2 = ref[:, lo:mid].astype(F32), ref[:, mid:hi].astype(F32)
    y = jnp.concatenate([x1 * c - x2 * s, x1 * s + x2 * c], axis=1)
    return y if scale is None else y * scale


def _unrotate_into(ref, hh, dy, c, s, scale=None):
    lo, mid, hi = hh * HEAD_DIM, hh * HEAD_DIM + ROPE_HALF, (hh + 1) * HEAD_DIM
    y1, y2 = dy[:, :ROPE_HALF], dy[:, ROPE_HALF:]
    d1, d2 = y1 * c + y2 * s, y2 * c - y1 * s
    if scale is not None:
        d1, d2 = d1 * scale, d2 * scale
    ref[:, lo:mid] = d1.astype(ref.dtype)
    ref[:, mid:hi] = d2.astype(ref.dtype)


def _retention_tables():
    b, chunk = RET_BLOCK, 64
    gamma = 1.0 - 2.0 ** (-5.0 - np.arange(HEADS, dtype=np.float64))
    log_g = np.log(gamma)[:, None, None]
    i = np.arange(b)[:, None]
    j = np.arange(b)[None, :]
    same = (i // chunk) == (j // chunk)
    earlier = (j // chunk) < (i // chunk)
    expo = np.where(same, np.abs(i - j), np.where(earlier, i - j, 0)).astype(np.float64)
    mask = np.where(same | earlier, 1.0, 0.0)
    dmat = np.exp(log_g * expo[None]) * mask[None]
    qd = np.exp(log_g[:, :, 0] * (np.arange(b)[None, :] + 1.0))
    kd = np.exp(log_g[:, :, 0] * (b - 1.0 - np.arange(b)[None, :]))
    cd = np.exp(log_g[:, :, 0] * b) * np.ones((1, HEAD_DIM))
    as32 = lambda v: jnp.asarray(v.astype(np.float32))
    return (as32(dmat), as32(np.swapaxes(dmat, 1, 2)), as32(qd[:, :, None]), as32(kd[:, :, None]), as32(cd[:, None, :]))


def _dot(a, b, mode="nn"):
    return lax.dot_general(a, b, _DIMS[mode], preferred_element_type=F32)


GRET_BLOCK = 3


def _head_specs(nb, rev=False):
    pos = (lambda n: nb - 1 - n) if rev else (lambda n: n)
    tok = pl.BlockSpec((RET_BLOCK, D_MODEL), lambda n: (pos(n), 0))
    blk = [pl.BlockSpec((RET_BLOCK, D_MODEL), lambda n, b=b: (pos(n), b)) for b in range(GRET_BLOCK + 1)]
    rope = pl.BlockSpec((RET_BLOCK, ROPE_HALF), lambda n: (pos(n), 0))
    tab = _full_spec((HEADS, RET_BLOCK, RET_BLOCK))
    col = _full_spec((HEADS, RET_BLOCK, 1))
    rowv = _full_spec((HEADS, 1, HEAD_DIM))
    st = pl.BlockSpec((HEADS, None, HEAD_DIM, HEAD_DIM), lambda n: (0, pos(n), 0, 0))
    return tok, blk, rope, tab, col, rowv, st


def _retention_fwd(name, proj, cos, sin, tables):
    t = proj.shape[0]
    nb = t // RET_BLOCK
    dmat, _, qd, kd, cd = tables
    tok, blk, rope, tab, col, rowv, st = _head_specs(nb)

    def body(q_ref, k_ref, v_ref, g_ref, c_ref, s_ref, d_ref, qd_ref, kd_ref, cd_ref, o_ref, ret_ref, st_ref, state):
        n = pl.program_id(0)

        @pl.when(n == 0)
        def _():
            state[...] = jnp.zeros(state.shape, F32)

        cs, sn = c_ref[...], s_ref[...]
        for hh in range(HEADS):
            sl = slice(hh * HEAD_DIM, (hh + 1) * HEAD_DIM)
            q, k, v = _rotate(q_ref, hh, cs, sn), _rotate(k_ref, hh, cs, sn, K_SCALE), v_ref[:, sl].astype(BF16)
            s = _dot(q.astype(BF16), k.astype(BF16), "nt") * d_ref[hh]
            stb = state[hh].astype(BF16)
            st_ref[hh] = stb
            o = _dot(s.astype(BF16), v) + _dot((q * qd_ref[hh]).astype(BF16), stb)
            o_ref[:, sl] = o
            rn = o * lax.rsqrt(jnp.mean(o * o, axis=-1, keepdims=True) + NORM_EPS)
            g = g_ref[:, sl].astype(F32)
            ret_ref[:, sl] = (rn * (g * _sigmoid(g))).astype(BF16)
            state[hh] = state[hh] * cd_ref[hh] + _dot((k * kd_ref[hh]).astype(BF16), v, "tn")

    return pl.pallas_call(
        body, name=name, grid=(nb,),
        in_specs=blk + [rope, rope, tab, col, col, rowv],
        out_specs=[tok, tok, st],
        out_shape=[jax.ShapeDtypeStruct((t, D_MODEL), F32), jax.ShapeDtypeStruct((t, D_MODEL), BF16),
                   jax.ShapeDtypeStruct((HEADS, nb, HEAD_DIM, HEAD_DIM), BF16)],
        scratch_shapes=[pltpu.VMEM((HEADS, HEAD_DIM, HEAD_DIM), F32)],
        compiler_params=_cparams(("arbitrary",)),
    )(proj, proj, proj, proj, cos, sin, dmat, qd, kd, cd)


def _retention_bwd(name, dru, w_ru, o, proj, cos, sin, states, tables):
    t = proj.shape[0]
    nb = t // RET_BLOCK
    dmat, dmat_t, qd, kd, cd = tables
    tok, blk, rope, tab, col, rowv, st = _head_specs(nb, rev=True)

    def body(dru_ref, wru_ref, o_ref, q_ref, k_ref, v_ref, g_ref, c_ref, s_ref, st_ref, d_ref, dt_ref, qd_ref, kd_ref, cd_ref,
             dq_ref, dk_ref, dv_ref, dg_ref, gstate):
        n = pl.program_id(0)

        @pl.when(n == 0)
        def _():
            gstate[...] = jnp.zeros(gstate.shape, F32)

        cs, sn = c_ref[...], s_ref[...]
        dret = _dot(dru_ref[...], wru_ref[...], "nt")
        for hh in range(HEADS):
            sl = slice(hh * HEAD_DIM, (hh + 1) * HEAD_DIM)
            o_v, g, dr = o_ref[:, sl], g_ref[:, sl].astype(F32), dret[:, sl]
            sg = _sigmoid(g)
            r = lax.rsqrt(jnp.mean(o_v * o_v, axis=-1, keepdims=True) + NORM_EPS)
            rn = o_v * r
            d_rn = dr * (g * sg)
            dg_ref[:, sl] = (dr * rn * (sg * (1.0 + g * (1.0 - sg)))).astype(BF16)
            d_o = r * (d_rn - rn * jnp.mean(d_rn * rn, axis=-1, keepdims=True))
            dob = d_o.astype(BF16)

            q, k, v = _rotate(q_ref, hh, cs, sn), _rotate(k_ref, hh, cs, sn, K_SCALE), v_ref[:, sl].astype(BF16)
            qb, kb = q.astype(BF16), k.astype(BF16)
            qdv, kdv = qd_ref[hh], kd_ref[hh]
            s_t = (_dot(kb, qb, "nt") * dt_ref[hh]).astype(BF16)
            p_t = (_dot(v, dob, "nt") * dt_ref[hh]).astype(BF16)
            p = (_dot(dob, v, "nt") * d_ref[hh]).astype(BF16)
            stb = st_ref[hh]
            gb = gstate[hh].astype(BF16)
            _unrotate_into(dq_ref, hh, _dot(p, kb) + _dot(dob, stb, "nt") * qdv, cs, sn)
            _unrotate_into(dk_ref, hh, _dot(p_t, qb) + _dot(v, gb, "nt") * kdv, cs, sn, K_SCALE)
            dv_ref[:, sl] = (_dot(s_t, dob) + _dot((k * kdv).astype(BF16), gb)).astype(BF16)
            gstate[hh] = gstate[hh] * cd_ref[hh] + _dot((q * qdv).astype(BF16), dob, "tn")

    return pl.pallas_call(
        body, name=name, grid=(nb,),
        in_specs=[tok, pl.BlockSpec((D_MODEL, D_MODEL), lambda n: (0, 0), pipeline_mode=pl.Buffered(1)), tok] + blk
                 + [rope, rope, st, tab, tab, col, col, rowv],
        out_specs=[tok, tok, tok, tok],
        out_shape=[jax.ShapeDtypeStruct((t, D_MODEL), BF16)] * 4,
        scratch_shapes=[pltpu.VMEM((HEADS, HEAD_DIM, HEAD_DIM), F32)],
        compiler_params=_cparams(("arbitrary",)),
    )(dru, w_ru, o, proj, proj, proj, proj, cos, sin, states, dmat, dmat_t, qd, kd, cd)


POOL_TILE = 256


def _pool_tables():
    b = POOL_TILE
    tt = np.arange(b)[:, None]
    jj = np.arange(b)[None, :]
    cur, prev = [], []
    for w in POOL_WINDOWS:
        cur.append(((tt - jj >= 0) & (tt - jj <= w - 1)).astype(np.float32))
        prev.append((tt - (jj - b) <= w - 1).astype(np.float32))
    cur, prev = np.stack(cur), np.stack(prev)
    as16 = lambda v: jnp.asarray(v, dtype=BF16)
    return as16(cur), as16(prev), as16(np.swapaxes(cur, 1, 2)), as16(np.swapaxes(prev, 1, 2))


def _split2(x):
    hi = x.astype(BF16)
    return hi, (x - hi.astype(F32)).astype(BF16)


POOL_BLOCK = 4


def _pool_count(n, window):
    tpos = n * POOL_TILE + lax.broadcasted_iota(jnp.int32, (POOL_TILE, 1), 0)
    return jnp.minimum(tpos + 1, window).astype(F32)


def _pool_fwd(name, proj, pool_w, scale, tables):
    t = proj.shape[0]
    nb = t // POOL_TILE
    mc, mp, _, _ = tables
    tab = _full_spec((GROUPS, POOL_TILE, POOL_TILE))
    row = _row_spec(POOL_TILE, D_MODEL)

    def body(pc_ref, pp_ref, mc_ref, mp_ref, w_ref, sc_ref, pm_ref, mix_ref, po_ref):
        n = pl.program_id(0)
        for g, window in enumerate(POOL_WINDOWS):
            sl = slice(g * GROUP_DIM, (g + 1) * GROUP_DIM)
            p = pc_ref[:, sl]
            win = _dot(mc_ref[g], p) + jnp.where(n > 0, _dot(mp_ref[g], pp_ref[:, sl]), 0.0)
            pm = (win / _pool_count(n, window) - p.astype(F32)).astype(BF16)
            pm_ref[:, sl] = pm
            mixed = _dot(pm, w_ref[g])
            mix_ref[:, sl] = mixed
            po_ref[:, sl] = (mixed * sc_ref[:, sl]).astype(BF16)

    return pl.pallas_call(
        body, name=name, grid=(nb,),
        in_specs=[_row_spec(POOL_TILE, D_MODEL, POOL_BLOCK),
                  pl.BlockSpec((POOL_TILE, D_MODEL), lambda n: (jnp.maximum(n - 1, 0), POOL_BLOCK)),
                  tab, tab, _full_spec((GROUPS, GROUP_DIM, GROUP_DIM)), _full_spec((1, D_MODEL))],
        out_specs=[row] * 3,
        out_shape=[jax.ShapeDtypeStruct((t, D_MODEL), BF16), jax.ShapeDtypeStruct((t, D_MODEL), F32),
                   jax.ShapeDtypeStruct((t, D_MODEL), BF16)],
        compiler_params=_cparams(("parallel",)),
    )(proj, proj, mc, mp, pool_w, scale)


def _pool_bwd(name, dpu, w_pu, pm, mixed, pool_w, scale, tables):
    t = dpu.shape[0]
    nb = t // POOL_TILE
    _, _, mct, mpt = tables
    cur = pl.BlockSpec((POOL_TILE, D_MODEL), lambda n: (nb - 1 - n, 0))
    tab = _full_spec((GROUPS, POOL_TILE, POOL_TILE))
    wspec = _full_spec((GROUPS, GROUP_DIM, GROUP_DIM))
    sspec = _full_spec((1, D_MODEL))

    def body(dpu_ref, wpu_ref, pm_ref, mix_ref, mct_ref, mpt_ref, w_ref, sc_ref, dp_ref, dw_ref, ds_ref, later):
        n = pl.program_id(0)

        @pl.when(n == 0)
        def _():
            dw_ref[...] = jnp.zeros(dw_ref.shape, F32)
            ds_ref[...] = jnp.zeros(ds_ref.shape, F32)
            later[...] = jnp.zeros(later.shape, F32)

        dpo = _dot(dpu_ref[...], wpu_ref[...], "nt")
        for g, window in enumerate(POOL_WINDOWS):
            sl = slice(g * GROUP_DIM, (g + 1) * GROUP_DIM)
            dc, sc = dpo[:, sl], sc_ref[:, sl]
            dmix = (dc * sc).astype(BF16)
            dpm = _dot(dmix, w_ref[g], "nt")
            e = dpm / _pool_count(nb - 1 - n, window)
            e_hi, e_lo = _split2(e)
            f_hi, f_lo = _split2(later[g])
            mctv, mptv = mct_ref[g], mpt_ref[g]
            back = _dot(mctv, e_hi) + _dot(mctv, e_lo)
            after = _dot(mptv, f_hi) + _dot(mptv, f_lo)
            dp_ref[:, sl] = (back + after - dpm).astype(BF16)
            later[g] = e
            dw_ref[g] += _dot(pm_ref[:, sl], dmix, "tn")
            ds_ref[:, sl] += jnp.sum(dc * mix_ref[:, sl], axis=0, keepdims=True)

    return pl.pallas_call(
        body, name=name, grid=(nb,),
        in_specs=[cur, pl.BlockSpec((D_MODEL, D_MODEL), lambda n: (0, 0), pipeline_mode=pl.Buffered(1)), cur, cur, tab, tab,
                  wspec, sspec],
        out_specs=[cur, wspec, sspec],
        out_shape=[jax.ShapeDtypeStruct((t, D_MODEL), BF16), jax.ShapeDtypeStruct((GROUPS, GROUP_DIM, GROUP_DIM), F32),
                   jax.ShapeDtypeStruct((1, D_MODEL), F32)],
        scratch_shapes=[pltpu.VMEM((GROUPS, POOL_TILE, GROUP_DIM), F32)],
        compiler_params=_cparams(("arbitrary",)),
    )(dpu, w_pu, pm, mixed, mct, mpt, pool_w, scale)


GATE0_BLOCK, GATE1_BLOCK = 5, 6


def _merge_fwd(name, ret, po, w_ru, w_pu, w_out, proj, bias, h, next_g, tm=512):
    t = ret.shape[0]

    def body(r_ref, p_ref, wr_ref, wp_ref, wo_ref, g0_ref, g1_ref, b_ref, h_ref, ng_ref, m_ref, ru_ref, pu_ref, ho_ref, n_ref):
        ru = _dot(r_ref[...], wr_ref[...])
        pu = _dot(p_ref[...], wp_ref[...])
        ru_ref[...] = ru
        pu_ref[...] = pu
        merged = (_sigmoid(g0_ref[...].astype(F32) + b_ref[0:1, :]) * ru
                  + _sigmoid(g1_ref[...].astype(F32) + b_ref[1:2, :]) * pu).astype(BF16)
        m_ref[...] = merged
        h_new = h_ref[...] + _dot(merged, wo_ref[...])
        ho_ref[...] = h_new
        n_ref[...] = _normed(h_new, ng_ref[...]).astype(BF16)

    row = _row_spec(tm, D_MODEL)
    wspec = pl.BlockSpec((D_MODEL, D_MODEL), lambda i: (0, 0), pipeline_mode=pl.Buffered(1))
    return pl.pallas_call(
        body, name=name, grid=(t // tm,),
        in_specs=[row, row, wspec, wspec, wspec, _row_spec(tm, D_MODEL, GATE0_BLOCK), _row_spec(tm, D_MODEL, GATE1_BLOCK),
                  _full_spec((2, D_MODEL)), row, _full_spec((1, D_MODEL))],
        out_specs=[row] * 5,
        out_shape=[jax.ShapeDtypeStruct((t, D_MODEL), BF16), jax.ShapeDtypeStruct((t, D_MODEL), F32),
                   jax.ShapeDtypeStruct((t, D_MODEL), F32), jax.ShapeDtypeStruct((t, D_MODEL), F32),
                   jax.ShapeDtypeStruct((t, D_MODEL), BF16)],
        compiler_params=_cparams(("parallel",)),
    )(ret, po, w_ru, w_pu, w_out, proj, proj, bias, h, next_g)


def _merge_bwd(name, dh_b, w_out, ru, pu, proj, bias, tm=512):
    t = dh_b.shape[0]

    def body(dh_ref, wo_ref, ru_ref, pu_ref, g0_ref, g1_ref, b_ref, dru_ref, dpu_ref, dg0_ref, dg1_ref, db_ref):
        i = pl.program_id(0)
        d = _dot(dh_ref[...], wo_ref[...], "nt")
        s0 = _sigmoid(g0_ref[...].astype(F32) + b_ref[0:1, :])
        s1 = _sigmoid(g1_ref[...].astype(F32) + b_ref[1:2, :])
        dru_ref[...] = (d * s0).astype(BF16)
        dpu_ref[...] = (d * s1).astype(BF16)
        dg0 = d * ru_ref[...] * (s0 * (1.0 - s0))
        dg1 = d * pu_ref[...] * (s1 * (1.0 - s1))
        dg0_ref[...] = dg0.astype(BF16)
        dg1_ref[...] = dg1.astype(BF16)
        part0 = jnp.sum(dg0, axis=0, keepdims=True)
        part1 = jnp.sum(dg1, axis=0, keepdims=True)

        @pl.when(i == 0)
        def _():
            db_ref[0:1, :] = part0
            db_ref[1:2, :] = part1

        @pl.when(i > 0)
        def _():
            db_ref[0:1, :] += part0
            db_ref[1:2, :] += part1

    row = _row_spec(tm, D_MODEL)
    return pl.pallas_call(
        body, name=name, grid=(t // tm,),
        in_specs=[row, pl.BlockSpec((D_MODEL, D_MODEL), lambda i: (0, 0), pipeline_mode=pl.Buffered(1)), row, row,
                  _row_spec(tm, D_MODEL, GATE0_BLOCK), _row_spec(tm, D_MODEL, GATE1_BLOCK), _full_spec((2, D_MODEL))],
        out_specs=[row, row, row, row, _full_spec((2, D_MODEL))],
        out_shape=[jax.ShapeDtypeStruct((t, D_MODEL), BF16)] * 4 + [jax.ShapeDtypeStruct((2, D_MODEL), F32)],
        compiler_params=_cparams(("arbitrary",)),
    )(dh_b, w_out, ru, pu, proj, proj, bias)


def _half_scale(acc):
    return (FFN_RES_WEIGHT * acc,)


def _normed(h, g):
    return h * lax.rsqrt(jnp.mean(h * h, axis=-1, keepdims=True) + NORM_EPS) * g


def _residual_half_norm(acc, res, g):
    h = res + FFN_RES_WEIGHT * acc
    return h, _normed(h, g)


FF_TILE = D_FF // 2
DW_TILE = 256
SAVED_FF_DTYPE = BF16
FF_CHUNKS = ((0, 512), (512, 1024), (1024, FF_TILE))


def _ffn_in(name, nrm, w_in, tm=512):
    t = nrm.shape[0]
    nj = D_FF // FF_TILE

    def body(n_ref, wg_ref, wu_ref, a_ref, mid_ref):
        nv = n_ref[...]
        for c0, c1 in FF_CHUNKS:
            gate = _dot(nv, wg_ref[:, c0:c1])
            up = _dot(nv, wu_ref[:, c0:c1])
            s = _sigmoid(gate)
            silu = gate * s
            a_ref[0, :, c0:c1] = (FFN_RES_WEIGHT * up * (s * (1.0 + gate * (1.0 - s)))).astype(a_ref.dtype)
            a_ref[1, :, c0:c1] = (FFN_RES_WEIGHT * silu).astype(a_ref.dtype)
            mid_ref[:, c0:c1] = (silu * up).astype(BF16)

    return pl.pallas_call(
        body, name=name, grid=(nj, t // tm),
        in_specs=[pl.BlockSpec((tm, D_MODEL), lambda j, i: (i, 0)),
                  pl.BlockSpec((D_MODEL, FF_TILE), lambda j, i: (0, j)),
                  pl.BlockSpec((D_MODEL, FF_TILE), lambda j, i: (0, j + nj))],
        out_specs=[pl.BlockSpec((2, tm, FF_TILE), lambda j, i: (0, i, j)), pl.BlockSpec((tm, FF_TILE), lambda j, i: (i, j))],
        out_shape=[jax.ShapeDtypeStruct((2, t, D_FF), SAVED_FF_DTYPE), jax.ShapeDtypeStruct((t, D_FF), BF16)],
        compiler_params=_cparams(("parallel", "parallel")),
    )(nrm, w_in, w_in)


def _ffn_dact(name, dout_b, w_out, a, tm=512):
    t = dout_b.shape[0]

    def body(d_ref, w_ref, a_ref, da_ref):
        dv = d_ref[...]
        for c0, c1 in FF_CHUNKS:
            dm = _dot(dv, w_ref[c0:c1, :], "nt")
            da_ref[0, :, c0:c1] = (dm * a_ref[0, :, c0:c1].astype(F32)).astype(BF16)
            da_ref[1, :, c0:c1] = (dm * a_ref[1, :, c0:c1].astype(F32)).astype(BF16)

    blk = pl.BlockSpec((2, tm, FF_TILE), lambda j, i: (0, i, j))
    return pl.pallas_call(
        body, name=name, grid=(D_FF // FF_TILE, t // tm),
        in_specs=[pl.BlockSpec((tm, D_MODEL), lambda j, i: (i, 0)), pl.BlockSpec((FF_TILE, D_MODEL), lambda j, i: (j, 0)), blk],
        out_specs=blk,
        out_shape=jax.ShapeDtypeStruct((2, t, D_FF), BF16),
        compiler_params=_cparams(("parallel", "parallel")),
    )(dout_b, w_out, a)


def _ffn_fwd(tag, h, nrm, get_w_in, get_w_out, finish):
    t = h.shape[0]
    w_in = get_w_in(nrm)
    a, mid = _ffn_in(f"{tag}_in", nrm, w_in, tm=min(512, t))
    w_out = get_w_out(mid)
    return finish(mid, w_out), (nrm, a, mid, w_in, w_out)


def _ffn_bwd(tag, h, g, saved, dout, dout_b, on_grads, flush):
    t = h.shape[0]
    nrm, a, mid, w_in, w_out = saved
    d_w_out = _matmul(f"{tag}_dwout", mid, dout_b, "tn", D_FF, D_MODEL, t, DW_TILE, D_MODEL, t, [BF16], epilogue=_half_scale,
                      resident="b")
    da = _ffn_dact(f"{tag}_dact", dout_b, w_out, a, tm=min(512, t))
    nj = D_FF // DW_TILE
    d_w_in = _dw_resident(f"{tag}_dwin", nrm, [da], [pl.BlockSpec((None, t, DW_TILE), lambda s: (s // nj, 0, s % nj))],
                          2 * nj, None, DW_TILE)
    tie = on_grads({f"{tag}_w_in": d_w_in, f"{tag}_w_out": d_w_out})
    tm = min(256, t)
    dh, dh_b, dg = _proj_norm_bwd(f"{tag}_dn", [da], [pl.BlockSpec((2, tm, D_FF), lambda i: (0, i, 0))],
                                  ((0, 0, 0, D_FF), (0, 1, D_FF, 2 * D_FF)), w_in, h, g if tie is None else g + tie, dout, tm)
    return dh, dh_b, dg, flush(dh)


def _dw_resident(name, u, pieces, piece_specs, n_tiles, which_piece, tn):
    t = u.shape[0]
    npc = len(pieces)

    def body(*refs):
        u_ref, p_refs, o_ref, ut_ref = refs[0], refs[1:1 + npc], refs[1 + npc], refs[2 + npc]
        s = pl.program_id(0)

        @pl.when(s == 0)
        def _():
            ut_ref[...] = u_ref[...].T

        if npc == 1:
            o_ref[...] = _dot(ut_ref[...], p_refs[0][...]).astype(BF16)
        for which in range(npc if npc > 1 else 0):
            @pl.when(which_piece(s) == which)
            def _(which=which):
                o_ref[...] = _dot(ut_ref[...], p_refs[which][...]).astype(BF16)

    return pl.pallas_call(
        body, name=name, grid=(n_tiles,),
        in_specs=[pl.BlockSpec((t, D_MODEL), lambda s: (0, 0), pipeline_mode=pl.Buffered(1))] + list(piece_specs),
        out_specs=pl.BlockSpec((D_MODEL, tn), lambda s: (0, s)),
        out_shape=jax.ShapeDtypeStruct((D_MODEL, n_tiles * tn), BF16),
        scratch_shapes=[pltpu.VMEM((D_MODEL, t), BF16)],
        compiler_params=_cparams(("arbitrary",)),
    )(u, *pieces)


def _mix_dwin(name, u, pieces, tn=256):
    t = u.shape[0]
    nj = D_MODEL // tn
    specs = [pl.BlockSpec((t, tn), lambda s, k=k: (0, jnp.clip(s - k * nj, 0, nj - 1))) for k in range(len(pieces))]
    return _dw_resident(name, u, pieces, specs, len(pieces) * nj, lambda s: s // nj, tn)


def _local_step(x, target, vec, get_w, relay_w, on_grads, flush):
    t = x.shape[0]
    cos, sin = _rope_tables(t)
    rtab = _retention_tables()
    ptab = _pool_tables()
    w = {}

    def getter(group, name):
        def get(after):
            if name not in w:
                w.update(get_w(group, after))
            return w[name]
        return get

    nrm1 = _rmsnorm_fwd("ffn1_norm", x, vec["norm_ffn1"])
    def out_and_norm(mid, w_out):
        return _matmul("ffn1_out", mid, w_out, "nn", t, D_MODEL, D_FF, 512, D_MODEL, D_FF, [F32, BF16],
                       extras=(x,), consts=(vec["norm_mix"],), epilogue=_residual_half_norm)

    (h1, u), s1 = _ffn_fwd("ffn1", x, nrm1, getter(0, "ffn1_w_in"), getter(1, "ffn1_w_out"), out_and_norm)
    w.update(get_w(2, u))
    proj = _matmul("mix_in", u, w["w_in"], "nn", t, IN_WIDTH, D_MODEL, 1024, 1024, D_MODEL, [BF16], n_outer=True)
    o, ret, states = _retention_fwd("retention", proj, cos, sin, rtab)
    pm, mixed, po = _pool_fwd("pool", proj, w["pool_w"], vec["pool_scale"], ptab)
    relay_w(3, po)
    merged, ru, pu, h2, nrm2 = _merge_fwd("merge", ret, po, w["w_ret_up"], w["w_pool_up"], w["w_out"], proj, w["gate_bias"],
                                          h1, vec["norm_ffn2"], tm=min(512, t))
    def out_and_loss(mid, w_out):
        return _out_loss_and_grad("ffn2_out_loss", mid, w_out, h2, vec["norm_final"], target, tm=min(512, t))

    (dh3, dh3_b, dg_final, loss), s2 = _ffn_fwd("ffn2", h2, nrm2, getter(3, "ffn2_w_in"), getter(3, "ffn2_w_out"), out_and_loss)

    def tied(v, tie):
        return v if tie is None else v + tie

    dh2, dh2_b, dg_ffn2, tie = _ffn_bwd("ffn2", h2, vec["norm_ffn2"], s2, dh3, dh3_b, on_grads, flush)
    def square_dw(name, act, grad):
        return _matmul(name, act, grad, "tn", D_MODEL, D_MODEL, t, D_MODEL, D_MODEL, 1024, [BF16])

    d_w_out = square_dw("mix_dwout", merged, dh2_b)
    dru, dpu, dg0, dg1, d_bias = _merge_bwd("merge_bwd", dh2_b, w["w_out"], ru, pu, proj, tied(w["gate_bias"], tie))
    d_w_ru = square_dw("mix_dwru", ret, dru)
    d_w_pu = square_dw("mix_dwpu", po, dpu)
    dp, d_pool_w, d_scale = _pool_bwd("pool_bwd", dpu, w["w_pool_up"], pm, mixed, w["pool_w"], vec["pool_scale"], ptab)
    dq, dk, dv, dgr = _retention_bwd("retention_bwd", dru, w["w_ret_up"], o, proj, cos, sin, states, rtab)
    dproj = [dq, dk, dv, dgr, dp, dg0, dg1]
    d_w_in = _mix_dwin("mix_dwin", u, dproj)
    tie = on_grads(dict(w_in=d_w_in, pool_w=d_pool_w.astype(BF16), w_ret_up=d_w_ru, w_pool_up=d_w_pu, w_out=d_w_out))
    tm = min(256, t)
    dh1, dh1_b, dg_mix = _proj_norm_bwd("mix_du", dproj, [_row_spec(tm, D_MODEL)] * len(dproj),
                                        [(k, None, k * D_MODEL, (k + 1) * D_MODEL) for k in range(len(dproj))],
                                        w["w_in"], h1, tied(vec["norm_mix"], tie), dh2, tm)
    tie = flush(dh1)
    dx, _, dg_ffn1, _ = _ffn_bwd("ffn1", x, tied(vec["norm_ffn1"], tie), s1, dh1, dh1_b, on_grads, flush)

    small = dict(norm_ffn1=dg_ffn1, norm_mix=dg_mix, gate_bias=d_bias, pool_scale=d_scale, norm_ffn2=dg_ffn2,
                 norm_final=dg_final)
    return loss[0, 0], dx, small


BIG = ("ffn1_w_in", "ffn1_w_out", "w_in", "pool_w", "w_ret_up", "w_pool_up", "w_out", "ffn2_w_in", "ffn2_w_out")
KIND = dict(ffn1_w_in="col", ffn1_w_out="row", w_in="col", pool_w="pool", w_ret_up="row", w_pool_up="row", w_out="row",
            ffn2_w_in="col", ffn2_w_out="row", gate_bias="col")
ANY = pl.BlockSpec(memory_space=pl.ANY)


def _place():
    x, y, c = lax.axis_index("x"), lax.axis_index("y"), lax.axis_index("c")
    chips = [(1 - x, y), (x, 1 - y), (1 - x, 1 - y)]
    return x, y, c, chips


def _full_view_shape(kind, local_shape):
    if kind == "col":
        return (2, local_shape[0] // 2, N_CHIPS * local_shape[1])
    if kind == "row":
        return (N_CHIPS, 2, local_shape[0] // 2, local_shape[1])
    return (GROUPS, N_CHIPS, 2, local_shape[1] // 2, local_shape[2])


def _local_view(kind, arr):
    if kind == "pool":
        return arr.reshape(GROUPS, 2, arr.shape[1] // 2, arr.shape[2])
    return arr.reshape(2, arr.shape[0] // 2, arr.shape[1])


def _blk(kind, ref, s, c):
    if kind == "col":
        cs = ref.shape[2] // N_CHIPS
        return ref.at[c, :, pl.ds(pl.multiple_of(s * cs, 128), cs)]
    if kind == "row":
        return ref.at[s, c]
    return ref.at[:, s, c]


def _half(kind, ref, c):
    return ref.at[:, c] if kind == "pool" else ref.at[c]


def _shard(kind, ref, s):
    if kind == "col":
        cs = ref.shape[2] // N_CHIPS
        return ref.at[:, :, pl.ds(pl.multiple_of(s * cs, 128), cs)]
    if kind == "row":
        return ref.at[s]
    return ref.at[:, s]


HBM = pl.BlockSpec(memory_space=pltpu.HBM)
SEM = pl.BlockSpec(memory_space=pltpu.SEMAPHORE)
EFFECT = pltpu.SideEffectType.DATAFLOW_SIDE_EFFECTING
WEIGHT_GROUPS = (("gate_bias", "ffn1_w_in"), ("ffn1_w_out",), ("w_in", "pool_w", "w_ret_up", "w_pool_up", "w_out"), ("ffn2_w_in", "ffn2_w_out"))
GRAD_GROUPS = (("ffn2_w_in", "ffn2_w_out"), ("w_in", "pool_w", "w_ret_up", "w_pool_up", "w_out"), ("ffn1_w_in", "ffn1_w_out"))


def _hbm(a):
    return pltpu.with_memory_space_constraint(a, pltpu.HBM)


def _natural(kind, o):
    if kind == "col":
        return o.reshape(o.shape[0] * o.shape[1], o.shape[2])
    if kind == "row":
        return o.reshape(-1, o.shape[3])
    return o.reshape(GROUPS, -1, o.shape[4])


def _ici_copy(kind, loc, full, j, chips, s, c, send_sem, recv_sem):
    px, py = chips[j]
    return (pltpu.make_async_remote_copy(src_ref=_half(kind, loc, c), dst_ref=_blk(kind, full, s, c), send_sem=send_sem,
                                         recv_sem=recv_sem, device_id=(px, py, c), device_id_type=MESH),
            pltpu.make_async_remote_copy(src_ref=_half(kind, loc, c), dst_ref=_blk(kind, full, 2 * px + py, c), send_sem=send_sem,
                                         recv_sem=recv_sem, device_id=(px, py, c), device_id_type=MESH))


def _gather_start(tag, group_ids, shards):
    grps = [WEIGHT_GROUPS[g] for g in group_ids]
    names = [nm for grp in grps for nm in grp]
    kinds = [KIND[nm] for nm in names]
    n, ng = len(names), len(grps)
    locs = [_hbm(_local_view(KIND[nm], shards[nm])) for nm in names]
    lands = [_hbm(lax.empty(_full_view_shape(KIND[nm], shards[nm].shape), shards[nm].dtype)) for nm in names]
    first = np.cumsum([0] + [len(grp) for grp in grps])

    def body(*refs):
        loc, full = refs[:n], refs[n:2 * n]
        send_sems, recv_sems = refs[2 * n:2 * n + ng], refs[2 * n + ng:2 * n + 2 * ng]
        token = refs[-1]
        x, y, c, chips = _place()
        s = 2 * x + y
        for g in range(ng):
            for a in range(first[g], first[g + 1]):
                for j in range(3):
                    k = 3 * (a - first[g]) + j
                    _ici_copy(kinds[a], loc[a], full[a], j, chips, s, c, send_sems[g].at[k], recv_sems[g].at[k])[0].start()
        token[...] = jnp.zeros(token.shape, F32)

    sem_shapes = [pltpu.SemaphoreType.DMA((3 * len(grp),)) for grp in grps]
    outs = pl.pallas_call(
        body, name=f"gather_start_{tag}",
        in_specs=[HBM] * (2 * n),
        out_specs=[SEM] * (2 * ng) + [HBM] * (2 * n) + [pl.BlockSpec(memory_space=pltpu.VMEM)],
        out_shape=sem_shapes + sem_shapes + [pltpu.HBM(a.shape, a.dtype) for a in locs + lands] + [jax.ShapeDtypeStruct((8, 128), F32)],
        input_output_aliases={i: 2 * ng + i for i in range(2 * n)},
        compiler_params=pltpu.CompilerParams(has_side_effects=EFFECT),
    )(*locs, *lands)
    send_sems, recv_sems = outs[:ng], outs[ng:2 * ng]
    locs_t, lands_t = outs[2 * ng:2 * ng + n], outs[2 * ng + n:2 * ng + 2 * n]
    groups = {}
    for k, g in enumerate(group_ids):
        sl = slice(first[k], first[k + 1])
        groups[g] = (send_sems[k], recv_sems[k], list(locs_t[sl]), list(lands_t[sl]))
    return groups, outs[-1]


def _forward_copies(kinds, loc, full, send_sems, recv_sems):
    x, y, c, chips = _place()
    s = 2 * x + y

    def remote(a, k, src, dst):
        return pltpu.make_async_remote_copy(src_ref=src, dst_ref=dst, send_sem=send_sems.at[4 * a + k],
                                            recv_sem=recv_sems.at[4 * a + k], device_id=(x, y, 1 - c), device_id_type=MESH)

    sends, arrivals = [], []
    for a, kind in enumerate(kinds):
        for j, (px, py) in enumerate(chips):
            theirs, from_sib = _blk(kind, full[a], 2 * px + py, c), _blk(kind, full[a], 2 * px + py, 1 - c)
            sends.append(remote(a, j, theirs, theirs))
            arrivals.append(remote(a, j, from_sib, from_sib))
        own = _shard(kind, full[a], s)
        sends.append(remote(a, 3, loc[a], own))
        arrivals.append(remote(a, 3, own, own))
    return sends, arrivals


def _gather_relay(g, group, after):
    names = WEIGHT_GROUPS[g]
    kinds = [KIND[nm] for nm in names]
    m = len(names)
    ici_send, ici_recv, locs, lands = group

    def body(*refs):
        loc, full = refs[:m], refs[m:2 * m]
        ici_s, ici_r = refs[2 * m], refs[2 * m + 1]
        d2d_s, d2d_r = refs[2 * m + 2 + len(after)], refs[2 * m + 3 + len(after)]
        x, y, c, chips = _place()
        for a in range(m):
            for j in range(3):
                sent, landed = _ici_copy(kinds[a], loc[a], full[a], j, chips, 2 * x + y, c, ici_s.at[3 * a + j], ici_r.at[3 * a + j])
                sent.wait_send()
                landed.wait_recv()
        for cp in _forward_copies(kinds, loc, full, d2d_s, d2d_r)[0]:
            cp.start()

    sem_shape = pltpu.SemaphoreType.DMA((4 * m,))
    outs = pl.pallas_call(
        body, name=f"gather_relay_{g}",
        in_specs=[HBM] * (2 * m) + [SEM, SEM] + [ANY] * len(after), out_specs=[SEM, SEM] + [HBM] * (2 * m),
        out_shape=[sem_shape, sem_shape] + [pltpu.HBM(a.shape, a.dtype) for a in locs + lands],
        input_output_aliases={i: 2 + i for i in range(2 * m)},
        compiler_params=pltpu.CompilerParams(has_side_effects=EFFECT),
    )(*locs, *lands, ici_send, ici_recv, *after)
    return outs[0], outs[1], list(outs[2:2 + m]), list(outs[2 + m:2 + 2 * m])


def _gather_land(g, state, after):
    names = WEIGHT_GROUPS[g]
    kinds = [KIND[nm] for nm in names]
    m = len(names)
    d2d_send, d2d_recv, locs, lands = state

    def body(*refs):
        sends, arrivals = _forward_copies(kinds, refs[:m], refs[m:2 * m], refs[2 * m], refs[2 * m + 1])
        for cp in sends:
            cp.wait_send()
        for cp in arrivals:
            cp.wait_recv()

    outs = pl.pallas_call(
        body, name=f"gather_land_{g}",
        in_specs=[HBM] * (2 * m) + [SEM, SEM] + [ANY] * len(after), out_specs=[HBM] * (2 * m),
        out_shape=[pltpu.HBM(a.shape, a.dtype) for a in locs + lands],
        input_output_aliases={i: i for i in range(2 * m)},
        compiler_params=pltpu.CompilerParams(has_side_effects=EFFECT),
    )(*locs, *lands, d2d_send, d2d_recv, *after)
    return {nm: _natural(k, o) for nm, k, o in zip(names, kinds, outs[m:])}


def _gather_finish(g, group, after):
    names = WEIGHT_GROUPS[g]
    kinds = [KIND[nm] for nm in names]
    m = len(names)
    send_sem, recv_sem, locs, lands = group

    def wait_body(*refs):
        loc, full = refs[:m], refs[m:2 * m]
        send_sems, recv_sems = refs[2 * m], refs[2 * m + 1]
        x, y, c, chips = _place()
        s = 2 * x + y
        for a in range(m):
            for j in range(3):
                k = 3 * a + j
                sent, landed = _ici_copy(kinds[a], loc[a], full[a], j, chips, s, c, send_sems.at[k], recv_sems.at[k])
                sent.wait_send()
                landed.wait_recv()

    outs = pl.pallas_call(
        wait_body, name=f"gather_wait_{g}",
        in_specs=[HBM] * (2 * m) + [SEM, SEM] + [ANY] * len(after), out_specs=[HBM] * (2 * m),
        out_shape=[pltpu.HBM(a.shape, a.dtype) for a in locs + lands],
        input_output_aliases={i: i for i in range(2 * m)},
        compiler_params=pltpu.CompilerParams(has_side_effects=EFFECT),
    )(*locs, *lands, send_sem, recv_sem, *after)
    locs, lands = outs[:m], outs[m:]

    def forward_body(*refs):
        sends, arrivals = _forward_copies(kinds, refs[:m], refs[2 * m:3 * m], *refs[3 * m:])
        for cp in sends:
            cp.start()
        for cp in arrivals:
            cp.wait_recv()
        for cp in sends:
            cp.wait_send()

    outs = pl.pallas_call(
        forward_body, name=f"gather_forward_{g}",
        in_specs=[ANY] * (2 * m), out_specs=[ANY] * m,
        out_shape=[jax.ShapeDtypeStruct(a.shape, a.dtype) for a in lands],
        input_output_aliases={m + i: i for i in range(m)},
        scratch_shapes=[pltpu.SemaphoreType.DMA((4 * m,)), pltpu.SemaphoreType.DMA((4 * m,))],
    )(*locs, *lands)
    return {nm: _natural(k, o) for nm, k, o in zip(names, kinds, outs)}


def _grad_view(kind, g):
    if kind == "col":
        return g.reshape(2, g.shape[0] // 2, g.shape[1])
    if kind == "row":
        return g.reshape(N_CHIPS, 2, g.shape[0] // (2 * N_CHIPS), g.shape[1])
    return g.reshape(GROUPS, N_CHIPS, 2, g.shape[1] // (2 * N_CHIPS), g.shape[2])


def _pair_copies(kinds, g, got, send_sems, recv_sems):
    x, y, c, _ = _place()

    def other_half(kind, ref):
        if kind == "col":
            return ref.at[1 - c]
        if kind == "row":
            return ref.at[:, 1 - c]
        return ref.at[:, :, 1 - c]

    return [pltpu.make_async_remote_copy(src_ref=other_half(kinds[a], g[a]), dst_ref=got[a], send_sem=send_sems.at[a],
                                         recv_sem=recv_sems.at[a], device_id=(x, y, 1 - c), device_id_type=MESH)
            for a in range(len(kinds))]


def _pair_exchange_start(tag, names, views):
    kinds = [KIND[nm] for nm in names]
    n = len(names)

    def got_shape(kind, v):
        if kind == "col":
            return v.shape[1:]
        if kind == "row":
            return (v.shape[0],) + v.shape[2:]
        return v.shape[:2] + v.shape[3:]

    srcs = [_hbm(views[nm]) for nm in names]
    lands = [_hbm(lax.empty(got_shape(k, views[nm]), BF16)) for nm, k in zip(names, kinds)]

    def body(*refs):
        g, got = refs[:n], refs[n:2 * n]
        for cp in _pair_copies(kinds, g, got, refs[2 * n], refs[2 * n + 1]):
            cp.start()
        refs[-1][...] = jnp.zeros(refs[-1].shape, F32)

    sem_shape = pltpu.SemaphoreType.DMA((n,))
    outs = pl.pallas_call(
        body, name=f"grad_pair_exchange_start_{tag}",
        in_specs=[HBM] * (2 * n),
        out_specs=[SEM, SEM] + [HBM] * (2 * n) + [pl.BlockSpec(memory_space=pltpu.VMEM)],
        out_shape=[sem_shape, sem_shape] + [pltpu.HBM(a.shape, a.dtype) for a in srcs + lands] + [jax.ShapeDtypeStruct((8, 128), F32)],
        input_output_aliases={i: 2 + i for i in range(2 * n)},
        compiler_params=pltpu.CompilerParams(has_side_effects=EFFECT),
    )(*srcs, *lands)
    return (outs[0], outs[1], list(outs[2:2 + n]), list(outs[2 + n:2 + 2 * n])), outs[-1]


def _pair_exchange_wait(tag, names, state, after):
    kinds = [KIND[nm] for nm in names]
    n = len(names)
    send_sem, recv_sem, srcs, lands = state

    def body(*refs):
        g, got = refs[:n], refs[n:2 * n]
        for cp in _pair_copies(kinds, g, got, refs[2 * n], refs[2 * n + 1]):
            cp.wait_send()
            cp.wait_recv()

    outs = pl.pallas_call(
        body, name=f"grad_pair_exchange_wait_{tag}",
        in_specs=[HBM] * (2 * n) + [SEM, SEM, ANY], out_specs=[HBM] * (2 * n),
        out_shape=[pltpu.HBM(a.shape, a.dtype) for a in srcs + lands],
        input_output_aliases={i: i for i in range(2 * n)},
        compiler_params=pltpu.CompilerParams(has_side_effects=EFFECT),
    )(*srcs, *lands, send_sem, recv_sem, after)
    return dict(zip(names, outs[:n])), dict(zip(names, outs[n:]))


def _pair_sum(name, kind, view, got, c_arr):
    if kind == "col":
        _, rows, cols = view.shape
        tr = 128
        grid = (rows // tr,)
        v_spec = pl.BlockSpec((None, tr, cols), lambda i, c: (c[0], i, 0))
        g_spec = pl.BlockSpec((tr, cols), lambda i, c: (i, 0))
    elif kind == "row":
        _, _, rows, cols = view.shape
        grid = (N_CHIPS,)
        v_spec = pl.BlockSpec((None, None, rows, cols), lambda i, c: (i, c[0], 0, 0))
        g_spec = pl.BlockSpec((None, rows, cols), lambda i, c: (i, 0, 0))
    else:
        _, _, _, rows, cols = view.shape
        grid = (GROUPS,)
        v_spec = pl.BlockSpec((None, N_CHIPS, None, rows, cols), lambda i, c: (i, 0, c[0], 0, 0))
        g_spec = pl.BlockSpec((None, N_CHIPS, rows, cols), lambda i, c: (i, 0, 0, 0))

    def body(c_ref, v_ref, g_ref, o_ref):
        o_ref[...] = (v_ref[...].astype(F32) + g_ref[...].astype(F32)).astype(BF16)

    return pl.pallas_call(
        body, name=name,
        grid_spec=pltpu.PrefetchScalarGridSpec(num_scalar_prefetch=1, grid=grid, in_specs=[v_spec, g_spec], out_specs=g_spec),
        out_shape=jax.ShapeDtypeStruct(got.shape, BF16),
        compiler_params=_cparams(("parallel",)),
    )(c_arr, view, got)


def _piece(kind, ref, s):
    if kind == "col":
        cs = ref.shape[1] // N_CHIPS
        return ref.at[:, pl.ds(pl.multiple_of(s * cs, 128), cs)]
    if kind == "row":
        return ref.at[s]
    return ref.at[:, s]


def _piece_shape(kind, shape):
    if kind == "col":
        return (shape[0], shape[1] // N_CHIPS)
    if kind == "row":
        return shape[1:]
    return (shape[0],) + shape[2:]


def _shard_copies(kinds, p, got, send_sems, recv_sems):
    x, y, c, chips = _place()
    return [pltpu.make_async_remote_copy(src_ref=_piece(kinds[a], p[a], 2 * px + py), dst_ref=got[a].at[j],
                                         send_sem=send_sems.at[3 * a + j], recv_sem=recv_sems.at[3 * a + j],
                                         device_id=(px, py, c), device_id_type=MESH)
            for a in range(len(kinds)) for j, (px, py) in enumerate(chips)]


def _shard_exchange_start(g, names, psums):
    kinds = [KIND[nm] for nm in names]
    n = len(names)
    srcs = [_hbm(psums[nm]) for nm in names]
    lands = [_hbm(lax.empty((3,) + _piece_shape(k, psums[nm].shape), BF16)) for nm, k in zip(names, kinds)]

    def body(*refs):
        p, got = refs[:n], refs[n:2 * n]
        send_sems, recv_sems = refs[2 * n], refs[2 * n + 1]
        token = refs[-1]
        for cp in _shard_copies(kinds, p, got, send_sems, recv_sems):
            cp.start()
        token[...] = jnp.zeros(token.shape, F32)

    sem_shape = pltpu.SemaphoreType.DMA((3 * n,))
    outs = pl.pallas_call(
        body, name=f"grad_shard_exchange_start_{g}",
        in_specs=[HBM] * (2 * n),
        out_specs=[SEM, SEM] + [HBM] * (2 * n) + [pl.BlockSpec(memory_space=pltpu.VMEM)],
        out_shape=[sem_shape, sem_shape] + [pltpu.HBM(a.shape, a.dtype) for a in srcs + lands] + [jax.ShapeDtypeStruct((8, 128), F32)],
        input_output_aliases={i: 2 + i for i in range(2 * n)},
        compiler_params=pltpu.CompilerParams(has_side_effects=EFFECT),
    )(*srcs, *lands)
    return (outs[0], outs[1], list(outs[2:2 + n]), list(outs[2 + n:2 + 2 * n])), outs[-1]


def _shard_exchange_wait(g, names, state, after):
    kinds = [KIND[nm] for nm in names]
    n = len(names)
    send_sem, recv_sem, srcs, lands = state

    def body(*refs):
        p, got = refs[:n], refs[n:2 * n]
        for cp in _shard_copies(kinds, p, got, refs[2 * n], refs[2 * n + 1]):
            cp.wait_send()
            cp.wait_recv()

    outs = pl.pallas_call(
        body, name=f"grad_shard_exchange_wait_{g}",
        in_specs=[HBM] * (2 * n) + [SEM, SEM] + [ANY] * len(after), out_specs=[HBM] * (2 * n),
        out_shape=[pltpu.HBM(a.shape, a.dtype) for a in srcs + lands],
        input_output_aliases={i: i for i in range(2 * n)},
        compiler_params=pltpu.CompilerParams(has_side_effects=EFFECT),
    )(*srcs, *lands, send_sem, recv_sem, *after)
    return dict(zip(names, outs[:n])), dict(zip(names, outs[n:]))


def _shard_sum(name, kind, psum, got, sc_arr):
    if kind == "col":
        rows, cols = psum.shape
        cs = cols // N_CHIPS
        tr = 128
        grid = (rows // tr,)
        p_spec = pl.BlockSpec((tr, cs), lambda i, sc: (i, sc[0]))
        g_spec = pl.BlockSpec((3, tr, cs), lambda i, sc: (0, i, 0))
        o_spec = pl.BlockSpec((None, tr, cs), lambda i, sc: (sc[1], i, 0))
        out_shape = (2, rows, cs)
    elif kind == "row":
        _, rows, cols = psum.shape
        grid = (1,)
        p_spec = pl.BlockSpec((None, rows, cols), lambda i, sc: (sc[0], 0, 0))
        g_spec = pl.BlockSpec((3, rows, cols), lambda i, sc: (0, 0, 0))
        o_spec = pl.BlockSpec((None, rows, cols), lambda i, sc: (sc[1], 0, 0))
        out_shape = (2, rows, cols)
    else:
        _, _, rows, cols = psum.shape
        grid = (1,)
        p_spec = pl.BlockSpec((GROUPS, None, rows, cols), lambda i, sc: (0, sc[0], 0, 0))
        g_spec = pl.BlockSpec((3, GROUPS, rows, cols), lambda i, sc: (0, 0, 0, 0))
        o_spec = pl.BlockSpec((GROUPS, None, rows, cols), lambda i, sc: (0, sc[1], 0, 0))
        out_shape = (GROUPS, 2, rows, cols)

    def body(sc_ref, p_ref, g_ref, o_ref):
        o_ref[...] = ((p_ref[...].astype(F32) + g_ref[0].astype(F32)) + g_ref[1].astype(F32)) + g_ref[2].astype(F32)

    return pl.pallas_call(
        body, name=name,
        grid_spec=pltpu.PrefetchScalarGridSpec(num_scalar_prefetch=1, grid=grid, in_specs=[p_spec, g_spec], out_specs=o_spec),
        out_shape=jax.ShapeDtypeStruct(out_shape, F32),
        compiler_params=_cparams(("parallel",)),
    )(sc_arr, psum, got)


def _half_copies(kinds, bufs, send_sems, recv_sems):
    x, y, c, _ = _place()

    def remote(a, half):
        part = _half(kinds[a], bufs[a], half)
        return pltpu.make_async_remote_copy(src_ref=part, dst_ref=part, send_sem=send_sems.at[a], recv_sem=recv_sems.at[a],
                                            device_id=(x, y, 1 - c), device_id_type=MESH)

    return [remote(a, c) for a in range(len(kinds))], [remote(a, 1 - c) for a in range(len(kinds))]


def _half_exchange_start(tag, names, bufs):
    kinds = [KIND[nm] for nm in names]
    n = len(names)
    arrs = [_hbm(bufs[nm]) for nm in names]

    def body(*refs):
        for cp in _half_copies(kinds, refs[:n], refs[n], refs[n + 1])[0]:
            cp.start()
        refs[-1][...] = jnp.zeros(refs[-1].shape, F32)

    sem_shape = pltpu.SemaphoreType.DMA((n,))
    outs = pl.pallas_call(
        body, name=f"grad_half_exchange_start_{tag}",
        in_specs=[HBM] * n,
        out_specs=[SEM, SEM] + [HBM] * n + [pl.BlockSpec(memory_space=pltpu.VMEM)],
        out_shape=[sem_shape, sem_shape] + [pltpu.HBM(a.shape, a.dtype) for a in arrs] + [jax.ShapeDtypeStruct((8, 128), F32)],
        input_output_aliases={i: 2 + i for i in range(n)},
        compiler_params=pltpu.CompilerParams(has_side_effects=EFFECT),
    )(*arrs)
    return (outs[0], outs[1], list(outs[2:2 + n])), outs[-1]


def _half_exchange_wait(tag, names, state, after):
    kinds = [KIND[nm] for nm in names]
    n = len(names)
    send_sem, recv_sem, arrs = state

    def body(*refs):
        sends, arrivals = _half_copies(kinds, refs[:n], refs[n], refs[n + 1])
        for cp in sends:
            cp.wait_send()
        for cp in arrivals:
            cp.wait_recv()

    outs = pl.pallas_call(
        body, name=f"grad_half_exchange_wait_{tag}",
        in_specs=[HBM] * n + [SEM, SEM] + [ANY] * len(after), out_specs=[HBM] * n,
        out_shape=[pltpu.HBM(a.shape, a.dtype) for a in arrs],
        input_output_aliases={i: i for i in range(n)},
        compiler_params=pltpu.CompilerParams(has_side_effects=EFFECT),
    )(*arrs, send_sem, recv_sem, *after)
    return dict(zip(names, outs))


N_DEV = 8
SMALL_ROWS = 8


def _all_reduce_small(name, v):
    def body(v_ref, o_ref, token, buf, send_sems, recv_sems):
        token[...] = jnp.zeros(token.shape, F32)
        x, y, c, _ = _place()
        me = 4 * x + 2 * y + c
        buf[me] = v_ref[...]
        cps = []
        for r in range(1, N_DEV):
            to = (x ^ (r >> 2), y ^ ((r >> 1) & 1), c ^ (r & 1))
            cp = pltpu.make_async_remote_copy(src_ref=v_ref, dst_ref=buf.at[me], send_sem=send_sems.at[r - 1],
                                              recv_sem=recv_sems.at[r - 1], device_id=to, device_id_type=MESH)
            cp.start()
            cps.append(cp)
        for r in range(1, N_DEV):
            pltpu.make_async_remote_copy(src_ref=v_ref, dst_ref=buf.at[me ^ r], send_sem=send_sems.at[r - 1],
                                         recv_sem=recv_sems.at[r - 1], device_id=(x, y, c), device_id_type=MESH).wait_recv()
        for cp in cps:
            cp.wait_send()
        acc = buf[0]
        for d in range(1, N_DEV):
            acc = acc + buf[d]
        o_ref[...] = acc

    vm = pl.BlockSpec(memory_space=pltpu.VMEM)
    return pl.pallas_call(
        body, name=name, in_specs=[vm], out_specs=[vm, vm],
        out_shape=[jax.ShapeDtypeStruct((SMALL_ROWS, D_MODEL), F32), jax.ShapeDtypeStruct((8, 128), F32)],
        scratch_shapes=[pltpu.VMEM((N_DEV, SMALL_ROWS, D_MODEL), F32), pltpu.SemaphoreType.DMA((N_DEV - 1,)),
                        pltpu.SemaphoreType.DMA((N_DEV - 1,))],
    )(v)


def _adamw(name, w, g, m, v, with_grad=False):
    rows, cols = w.shape
    tr = next((c for c in (256, 176, 128, 64, 32, 8) if rows % c == 0), rows)
    spec = pl.BlockSpec((tr, cols), lambda i: (i, 0))

    def body(w_ref, g_ref, m_ref, v_ref, d_ref, mo_ref, vo_ref, *go_ref):
        gv = g_ref[...]
        if with_grad:
            go_ref[0][...] = gv
        m_new = ADAM_B1 * m_ref[...] + (1.0 - ADAM_B1) * gv
        v_new = ADAM_B2 * v_ref[...] + (1.0 - ADAM_B2) * jnp.square(gv)
        m_hat = m_new / (1.0 - ADAM_B1 ** ADAM_STEP)
        v_hat = v_new / (1.0 - ADAM_B2 ** ADAM_STEP)
        d_ref[...] = -ADAM_LR * (m_hat / (jnp.sqrt(v_hat) + ADAM_EPS) + ADAM_WD * w_ref[...])
        mo_ref[...] = m_new
        vo_ref[...] = v_new

    return pl.pallas_call(
        body, name=name, grid=(rows // tr,),
        in_specs=[spec] * 4, out_specs=[spec] * (4 if with_grad else 3),
        out_shape=[jax.ShapeDtypeStruct((rows, cols), F32)] * (4 if with_grad else 3),
        compiler_params=_cparams(("parallel",)),
    )(w, g, m, v)


WEIGHTS = ("norm_ffn1", "ffn1_w_in", "ffn1_w_out", "norm_mix", "w_in", "gate_bias", "pool_w", "pool_scale", "w_ret_up",
           "w_pool_up", "w_out", "norm_ffn2", "ffn2_w_in", "ffn2_w_out", "norm_final")
SMALL_ROW = dict(norm_ffn1=0, norm_mix=1, gate_bias=2, pool_scale=4, norm_ffn2=5, norm_final=6)


def _as2d(a):
    return a.reshape(-1, a.shape[-1])


def kernel(x, norm_ffn1, ffn1_w_in, ffn1_w_out, norm_mix, w_in, gate_bias, pool_w, pool_scale, w_ret_up, w_pool_up, w_out, norm_ffn2, ffn2_w_in, ffn2_w_out, norm_final, loss_target, m_norm_ffn1, m_ffn1_w_in, m_ffn1_w_out, m_norm_mix, m_w_in, m_gate_bias, m_pool_w, m_pool_scale, m_w_ret_up, m_w_pool_up, m_w_out, m_norm_ffn2, m_ffn2_w_in, m_ffn2_w_out, m_norm_final, v_norm_ffn1, v_ffn1_w_in, v_ffn1_w_out, v_norm_mix, v_w_in, v_gate_bias, v_pool_w, v_pool_scale, v_w_ret_up, v_w_pool_up, v_w_out, v_norm_ffn2, v_ffn2_w_in, v_ffn2_w_out, v_norm_final):
    wt = dict(norm_ffn1=norm_ffn1, ffn1_w_in=ffn1_w_in, ffn1_w_out=ffn1_w_out, norm_mix=norm_mix, w_in=w_in, gate_bias=gate_bias,
              pool_w=pool_w, pool_scale=pool_scale, w_ret_up=w_ret_up, w_pool_up=w_pool_up, w_out=w_out, norm_ffn2=norm_ffn2,
              ffn2_w_in=ffn2_w_in, ffn2_w_out=ffn2_w_out, norm_final=norm_final)
    mom = dict(norm_ffn1=m_norm_ffn1, ffn1_w_in=m_ffn1_w_in, ffn1_w_out=m_ffn1_w_out, norm_mix=m_norm_mix, w_in=m_w_in,
               gate_bias=m_gate_bias, pool_w=m_pool_w, pool_scale=m_pool_scale, w_ret_up=m_w_ret_up, w_pool_up=m_w_pool_up,
               w_out=m_w_out, norm_ffn2=m_norm_ffn2, ffn2_w_in=m_ffn2_w_in, ffn2_w_out=m_ffn2_w_out, norm_final=m_norm_final)
    var = dict(norm_ffn1=v_norm_ffn1, ffn1_w_in=v_ffn1_w_in, ffn1_w_out=v_ffn1_w_out, norm_mix=v_norm_mix, w_in=v_w_in,
               gate_bias=v_gate_bias, pool_w=v_pool_w, pool_scale=v_pool_scale, w_ret_up=v_w_ret_up, w_pool_up=v_w_pool_up,
               w_out=v_w_out, norm_ffn2=v_norm_ffn2, ffn2_w_in=v_ffn2_w_in, ffn2_w_out=v_ffn2_w_out, norm_final=v_norm_final)

    ax, ay, ac = lax.axis_index("x"), lax.axis_index("y"), lax.axis_index("c")
    chip = 2 * ax + ay
    c_arr = jnp.reshape(ac, (1,)).astype(jnp.int32)
    sc_arr = jnp.stack([chip, ac]).astype(jnp.int32)
    bias_cols = gate_bias.shape[-1]

    first = {"gate_bias": gate_bias[0], "ffn1_w_in": ffn1_w_in[0].astype(BF16)}
    gather_groups, token = _gather_start("first", [0], first)
    rest, rest_token = _gather_start("rest", [1, 2, 3],
                                     {nm: wt[nm][0].astype(BF16) + token[0, 0].astype(BF16) for nm in BIG if nm not in first})
    gather_groups.update(rest)
    vec = dict(norm_ffn1=norm_ffn1, norm_mix=norm_mix, norm_ffn2=norm_ffn2, pool_scale=pool_scale,
               norm_final=norm_final.reshape(1, D_MODEL))

    relayed = {}

    def relay_w(g, after):
        relayed[g] = _gather_relay(g, gather_groups[g], (after,))

    def get_w(g, after):
        if g in relayed:
            return _gather_land(g, relayed[g], (after,))
        return _gather_finish(g, gather_groups[g], (after, rest_token) if g == 0 else (after,))

    pairs, pending = [], []

    def on_grads(gr):
        g = len(pairs)
        names = GRAD_GROUPS[g]
        assert set(names) == set(gr), (names, list(gr))
        state, token = _pair_exchange_start(g, names, {nm: _grad_view(KIND[nm], gr[nm]) for nm in names})
        pairs.append(state)
        return token[0:1, 0:1]

    def flush(after):
        g = len(pending)
        names = GRAD_GROUPS[g]
        views, from_sib = _pair_exchange_wait(g, names, pairs[g], after)
        psums = {nm: _pair_sum(f"pair_sum_{nm}", KIND[nm], views[nm], from_sib[nm], c_arr) for nm in names}
        state, token = _shard_exchange_start(g, names, psums)
        pending.append(state)
        tokens.append(token)
        return token[0:1, 0:1]

    tokens = []
    loss_local, dx, small = _local_step(x[0], loss_target[0], vec, get_w, relay_w, on_grads, flush)

    grads, delta, new_m, new_v = {}, {}, {}, {}

    def adamw(nm):
        shape = wt[nm].shape
        outs = _adamw(f"adamw_{nm}", _as2d(wt[nm]), _as2d(grads[nm]), _as2d(mom[nm]), _as2d(var[nm]), with_grad=nm in BIG)
        delta[nm], new_m[nm], new_v[nm] = (o.reshape(shape) for o in outs[:3])
        if nm in BIG:
            grads[nm] = outs[3].reshape(shape)
        return outs[0]

    def reduce_start(g, after):
        names = GRAD_GROUPS[g]
        psums, from_chips = _shard_exchange_wait(g, names, pending[g], after)
        bufs = {nm: _shard_sum(f"shard_sum_{nm}", KIND[nm], psums[nm], from_chips[nm], sc_arr) for nm in names}
        return _half_exchange_start(g, names, bufs)

    def reduce_finish(g, state, after):
        names = GRAD_GROUPS[g]
        reduced = _half_exchange_wait(g, names, state, after)
        for nm in names:
            grads[nm] = reduced[nm].reshape(wt[nm].shape)
        return tuple(adamw(nm) for nm in names)

    swap0, token = reduce_start(0, (tokens[-1],))
    swap1, token = reduce_start(1, (token,))
    done = reduce_finish(0, swap0, (token,))
    done = reduce_finish(1, swap1, done)
    swap2, token = reduce_start(2, done)
    packed = jnp.concatenate([small["norm_ffn1"], small["norm_mix"], small["gate_bias"], small["pool_scale"],
                              small["norm_ffn2"], small["norm_final"], jnp.broadcast_to(loss_local, (1, D_MODEL))], axis=0)
    small_sum, _ = _all_reduce_small("reduce_small_grads", packed + token[0, 0])
    loss = small_sum[SMALL_ROWS - 1, 0]
    for nm in ("norm_ffn1", "norm_mix", "pool_scale", "norm_ffn2"):
        grads[nm] = small_sum[SMALL_ROW[nm]][None, :]
    grads["norm_final"] = small_sum[SMALL_ROW["norm_final"]]
    grads["gate_bias"] = lax.dynamic_slice(small_sum, (SMALL_ROW["gate_bias"], chip * bias_cols), (2, bias_cols))[None]
    reduce_finish(2, swap2, (small_sum,))
    for nm in WEIGHTS:
        if nm not in delta:
            adamw(nm)

    return (loss, dx[None], *[grads[nm] for nm in WEIGHTS], *[delta[nm] for nm in WEIGHTS],
            *[new_m[nm] for nm in WEIGHTS], *[new_v[nm] for nm in WEIGHTS])
```

```python
import functools

import numpy as np
import jax
import jax.numpy as jnp
from jax import lax
from jax.experimental import pallas as pl
from jax.experimental.pallas import tpu as pltpu

F32 = jnp.float32
BF16 = jnp.bfloat16
MESH = pl.DeviceIdType.MESH

D_MODEL = 1024
D_FF = 2816
HEADS = 4
HEAD_DIM = 256
GROUPS = 4
GROUP_DIM = 256
POOL_WINDOWS = (2, 4, 8, 16)
IN_WIDTH = 7 * D_MODEL
ROPE_BASE = 10000.0
NORM_EPS = 1e-6
FFN_RES_WEIGHT = 0.5
ADAM_LR, ADAM_B1, ADAM_B2, ADAM_EPS, ADAM_WD, ADAM_STEP = 0.001, 0.9, 0.999, 1e-08, 0.01, 10

N_CHIPS = 4
RET_BLOCK = 256
V7X_VMEM_LIMIT = 48 * 1024 * 1024


def _cparams(sem):
    return pltpu.CompilerParams(dimension_semantics=sem, vmem_limit_bytes=V7X_VMEM_LIMIT)


def _sigmoid(x):
    return jax.nn.sigmoid(x)


_DIMS = {"nn": (((1,), (0,)), ((), ())), "nt": (((1,), (1,)), ((), ())), "tn": (((0,), (0,)), ((), ()))}


def _matmul(name, a, b, mode, m, n, k, tm, tn, tk, out_dtypes, a_spec=None, b_spec=None, extras=(), consts=(), epilogue=None,
            resident=None, n_outer=False):
    tm, tn, tk = min(tm, m), min(tn, n), min(tk, k)
    gi, gj, gk = m // tm, n // tn, k // tk
    assert gi * tm == m and gj * tn == n and gk * tk == k, (name, m, n, k, tm, tn, tk)
    assert not (n_outer and (a_spec is not None or b_spec is not None)), name
    once = dict(pipeline_mode=pl.Buffered(1))

    def spec(shape, index, **kw):
        return pl.BlockSpec(shape, (lambda j, i, kk: index(i, j, kk)) if n_outer else index, **kw)

    if a_spec is None:
        kw = once if resident == "a" else {}
        a_spec = (spec((tk, tm), lambda i, j, kk: (kk, i), **kw) if mode == "tn"
                  else spec((tm, tk), lambda i, j, kk: (i, kk), **kw))
    if b_spec is None:
        kw = once if resident == "b" else {}
        b_spec = (spec((tn, tk), lambda i, j, kk: (j, kk), **kw) if mode == "nt"
                  else spec((tk, tn), lambda i, j, kk: (kk, j), **kw))
    n_ex, n_out = len(extras) + len(consts), len(out_dtypes)
    dims = _DIMS[mode]

    def body(a_ref, b_ref, *rest):
        ex_refs, out_refs = rest[:n_ex], rest[n_ex:n_ex + n_out]

        def finish(acc):
            outs = (acc,) if epilogue is None else epilogue(acc, *[e[...] for e in ex_refs])
            for o_ref, o in zip(out_refs, outs):
                o_ref[...] = o.astype(o_ref.dtype)

        prod = lax.dot_general(a_ref[...], b_ref[...], dims, preferred_element_type=F32)
        if gk == 1:
            finish(prod)
        else:
            acc_ref = rest[n_ex + n_out]
            kk = pl.program_id(2)

            @pl.when(kk == 0)
            def _():
                acc_ref[...] = prod

            @pl.when(kk > 0)
            def _():
                acc_ref[...] += prod

            @pl.when(kk == gk - 1)
            def _():
                finish(acc_ref[...])

    o_spec = spec((tm, tn), lambda i, j, kk: (i, j))
    outs = pl.pallas_call(
        body, name=name, grid=(gj, gi, gk) if n_outer else (gi, gj, gk),
        in_specs=[a_spec, b_spec] + [o_spec] * len(extras) + [spec((1, tn), lambda i, j, kk: (0, j))] * len(consts),
        out_specs=[o_spec] * n_out,
        out_shape=[jax.ShapeDtypeStruct((m, n), dt) for dt in out_dtypes],
        scratch_shapes=[pltpu.VMEM((tm, tn), F32)] if gk > 1 else [],
        compiler_params=_cparams(("parallel", "parallel", "arbitrary")),
    )(a, b, *extras, *consts)
    return outs[0] if n_out == 1 else outs


def _row_spec(tm, width, col_block=0):
    return pl.BlockSpec((tm, width), lambda i: (i, col_block))


def _full_spec(shape):
    return pl.BlockSpec(shape, lambda *_: (0,) * len(shape))


def _rmsnorm_fwd(name, h, g, tm=512):
    t = h.shape[0]

    def body(h_ref, g_ref, o_ref):
        x = h_ref[...]
        r = lax.rsqrt(jnp.mean(x * x, axis=-1, keepdims=True) + NORM_EPS)
        o_ref[...] = (x * r * g_ref[...]).astype(BF16)

    return pl.pallas_call(
        body, name=name, grid=(t // tm,),
        in_specs=[_row_spec(tm, D_MODEL), _full_spec((1, D_MODEL))],
        out_specs=_row_spec(tm, D_MODEL),
        out_shape=jax.ShapeDtypeStruct((t, D_MODEL), BF16),
        compiler_params=_cparams(("parallel",)),
    )(h, g)


def _proj_norm_bwd(name, a_list, a_specs, parts, w, h, g, dres, tm):
    t = h.shape[0]
    na = len(a_list)

    def body(*refs):
        a_refs = refs[:na]
        w_ref, h_ref, g_ref, dres_ref, dh_ref, dhb_ref, dg_ref = refs[na:]
        i = pl.program_id(0)
        dn_v = None
        for which, lead, k0, k1 in parts:
            a_ref = a_refs[which]
            term = _dot(a_ref[...] if lead is None else a_ref[lead], w_ref[:, k0:k1], "nt")
            dn_v = term if dn_v is None else dn_v + term
        x = h_ref[...]
        r = lax.rsqrt(jnp.mean(x * x, axis=-1, keepdims=True) + NORM_EPS)
        xh = x * r
        dxh = dn_v * g_ref[...]
        dh = dres_ref[...] + r * (dxh - xh * jnp.mean(dxh * xh, axis=-1, keepdims=True))
        dh_ref[...] = dh
        dhb_ref[...] = dh.astype(BF16)
        part = jnp.sum(dn_v * xh, axis=0, keepdims=True)

        @pl.when(i == 0)
        def _():
            dg_ref[...] = part

        @pl.when(i > 0)
        def _():
            dg_ref[...] += part

    row = _row_spec(tm, D_MODEL)
    return pl.pallas_call(
        body, name=name, grid=(t // tm,),
        in_specs=list(a_specs) + [pl.BlockSpec(w.shape, lambda i: (0, 0), pipeline_mode=pl.Buffered(1)), row,
                                  _full_spec((1, D_MODEL)), row],
        out_specs=[row, row, _full_spec((1, D_MODEL))],
        out_shape=[jax.ShapeDtypeStruct((t, D_MODEL), F32), jax.ShapeDtypeStruct((t, D_MODEL), BF16),
                   jax.ShapeDtypeStruct((1, D_MODEL), F32)],
        compiler_params=_cparams(("arbitrary",)),
    )(*a_list, w, h, g, dres)


def _out_loss_and_grad(name, mid, w_out, h, g, target, tm=512):
    t = h.shape[0]

    def body(m_ref, w_ref, h_ref, g_ref, t_ref, dh_ref, dhb_ref, dg_ref, loss_ref):
        i = pl.program_id(0)
        x = h_ref[...] + FFN_RES_WEIGHT * _dot(m_ref[...], w_ref[...])
        gv = g_ref[...]
        r = lax.rsqrt(jnp.mean(x * x, axis=-1, keepdims=True) + NORM_EPS)
        xh = x * r
        err = xh * gv - t_ref[...]
        row = jnp.mean(err * err, axis=-1, keepdims=True)
        part_loss = 0.5 * jnp.sum(row, axis=0, keepdims=True)
        dy = err * (1.0 / D_MODEL)
        dxh = dy * gv
        dh = r * (dxh - xh * jnp.mean(dxh * xh, axis=-1, keepdims=True))
        dh_ref[...] = dh
        dhb_ref[...] = dh.astype(BF16)
        part = jnp.sum(dy * xh, axis=0, keepdims=True)

        @pl.when(i == 0)
        def _():
            dg_ref[...] = part
            loss_ref[...] = jnp.zeros(loss_ref.shape, F32) + part_loss

        @pl.when(i > 0)
        def _():
            dg_ref[...] += part
            loss_ref[...] += part_loss

    return pl.pallas_call(
        body, name=name, grid=(t // tm,),
        in_specs=[_row_spec(tm, D_FF), pl.BlockSpec((D_FF, D_MODEL), lambda i: (0, 0), pipeline_mode=pl.Buffered(1)),
                  _row_spec(tm, D_MODEL), _full_spec((1, D_MODEL)), _row_spec(tm, D_MODEL)],
        out_specs=[_row_spec(tm, D_MODEL), _row_spec(tm, D_MODEL), _full_spec((1, D_MODEL)), _full_spec((8, 128))],
        out_shape=[jax.ShapeDtypeStruct((t, D_MODEL), F32), jax.ShapeDtypeStruct((t, D_MODEL), BF16),
                   jax.ShapeDtypeStruct((1, D_MODEL), F32), jax.ShapeDtypeStruct((8, 128), F32)],
        compiler_params=_cparams(("arbitrary",)),
    )(mid, w_out, h, g, target)


def _rope_tables(t):
    half = HEAD_DIM // 2
    inv_freq = np.float32(ROPE_BASE) ** (-np.arange(half, dtype=np.float32) / np.float32(half))
    ang = (np.arange(t, dtype=np.float32)[:, None] * inv_freq[None, :].astype(np.float32)).astype(np.float32)
    return jnp.asarray(np.cos(ang.astype(np.float64)).astype(np.float32)), jnp.asarray(np.sin(ang.astype(np.float64)).astype(np.float32))


ROPE_HALF = HEAD_DIM // 2
K_SCALE = HEAD_DIM ** -0.5


def _rotate(ref, rows, hh, c, s, scale=None):
    lo, mid, hi = hh * HEAD_DIM, hh * HEAD_DIM + ROPE_HALF, (hh + 1) * HEAD_DIM
    x1, x2 = ref[rows, lo:mid].astype(F32), ref[rows, mid:hi].astype(F32)
    y = jnp.concatenate([x1 * c - x2 * s, x1 * s + x2 * c], axis=1)
    return y if scale is None else y * scale


def _unrotate_into(ref, rows, hh, dy, c, s, scale=None):
    lo, mid, hi = hh * HEAD_DIM, hh * HEAD_DIM + ROPE_HALF, (hh + 1) * HEAD_DIM
    y1, y2 = dy[:, :ROPE_HALF], dy[:, ROPE_HALF:]
    d1, d2 = y1 * c + y2 * s, y2 * c - y1 * s
    if scale is not None:
        d1, d2 = d1 * scale, d2 * scale
    ref[rows, lo:mid] = d1.astype(ref.dtype)
    ref[rows, mid:hi] = d2.astype(ref.dtype)


def _retention_tables():
    b, chunk = RET_BLOCK, 64
    gamma = 1.0 - 2.0 ** (-5.0 - np.arange(HEADS, dtype=np.float64))
    log_g = np.log(gamma)[:, None, None]
    i = np.arange(b)[:, None]
    j = np.arange(b)[None, :]
    same = (i // chunk) == (j // chunk)
    earlier = (j // chunk) < (i // chunk)
    expo = np.where(same, np.abs(i - j), np.where(earlier, i - j, 0)).astype(np.float64)
    mask = np.where(same | earlier, 1.0, 0.0)
    dmat = np.exp(log_g * expo[None]) * mask[None]
    qd = np.exp(log_g[:, :, 0] * (np.arange(b)[None, :] + 1.0))
    kd = np.exp(log_g[:, :, 0] * (b - 1.0 - np.arange(b)[None, :]))
    cd = np.exp(log_g[:, :, 0] * b) * np.ones((1, HEAD_DIM))
    as32 = lambda v: jnp.asarray(v.astype(np.float32))
    return (as32(dmat), as32(np.swapaxes(dmat, 1, 2)), as32(qd[:, :, None]), as32(kd[:, :, None]), as32(cd[:, None, :]))


def _dot(a, b, mode="nn"):
    return lax.dot_general(a, b, _DIMS[mode], preferred_element_type=F32)


GRET_BLOCK = 3


RET_SUBS = 2
RET_STEP = RET_SUBS * RET_BLOCK


def _head_specs(steps, rev=False):
    pos = (lambda n: steps - 1 - n) if rev else (lambda n: n)
    tok = pl.BlockSpec((RET_STEP, D_MODEL), lambda n: (pos(n), 0))
    blk = [pl.BlockSpec((RET_STEP, D_MODEL), lambda n, b=b: (pos(n), b)) for b in range(GRET_BLOCK + 1)]
    rope = pl.BlockSpec((RET_STEP, ROPE_HALF), lambda n: (pos(n), 0))
    tab = _full_spec((HEADS, RET_BLOCK, RET_BLOCK))
    col = _full_spec((HEADS, RET_BLOCK, 1))
    rowv = _full_spec((HEADS, 1, HEAD_DIM))
    st = pl.BlockSpec((HEADS, RET_SUBS, HEAD_DIM, HEAD_DIM), lambda n: (0, pos(n), 0, 0))
    return tok, blk, rope, tab, col, rowv, st


def _retention_fwd(name, proj, cos, sin, tables):
    t = proj.shape[0]
    nb, steps = t // RET_BLOCK, t // RET_STEP
    dmat, _, qd, kd, cd = tables
    tok, blk, rope, tab, col, rowv, st = _head_specs(steps)

    def body(q_ref, k_ref, v_ref, g_ref, c_ref, s_ref, d_ref, qd_ref, kd_ref, cd_ref, o_ref, ret_ref, st_ref, state):
        n = pl.program_id(0)

        @pl.when(n == 0)
        def _():
            state[...] = jnp.zeros(state.shape, F32)

        for sub in range(RET_SUBS):
            rows = slice(sub * RET_BLOCK, (sub + 1) * RET_BLOCK)
            cs, sn = c_ref[rows, :], s_ref[rows, :]
            for hh in range(HEADS):
                sl = slice(hh * HEAD_DIM, (hh + 1) * HEAD_DIM)
                q, k = _rotate(q_ref, rows, hh, cs, sn), _rotate(k_ref, rows, hh, cs, sn, K_SCALE)
                v = v_ref[rows, sl].astype(BF16)
                s = _dot(q.astype(BF16), k.astype(BF16), "nt") * d_ref[hh]
                stb = state[hh].astype(BF16)
                st_ref[hh, sub] = stb
                o = _dot(s.astype(BF16), v) + _dot((q * qd_ref[hh]).astype(BF16), stb)
                o_ref[rows, sl] = o
                rn = o * lax.rsqrt(jnp.mean(o * o, axis=-1, keepdims=True) + NORM_EPS)
                g = g_ref[rows, sl].astype(F32)
                ret_ref[rows, sl] = (rn * (g * _sigmoid(g))).astype(BF16)
                state[hh] = state[hh] * cd_ref[hh] + _dot((k * kd_ref[hh]).astype(BF16), v, "tn")

    return pl.pallas_call(
        body, name=name, grid=(steps,),
        in_specs=blk + [rope, rope, tab, col, col, rowv],
        out_specs=[tok, tok, st],
        out_shape=[jax.ShapeDtypeStruct((t, D_MODEL), F32), jax.ShapeDtypeStruct((t, D_MODEL), BF16),
                   jax.ShapeDtypeStruct((HEADS, nb, HEAD_DIM, HEAD_DIM), BF16)],
        scratch_shapes=[pltpu.VMEM((HEADS, HEAD_DIM, HEAD_DIM), F32)],
        compiler_params=_cparams(("arbitrary",)),
    )(proj, proj, proj, proj, cos, sin, dmat, qd, kd, cd)


def _retention_bwd(name, dru, w_ru, o, proj, cos, sin, states, tables):
    t = proj.shape[0]
    steps = t // RET_STEP
    dmat, dmat_t, qd, kd, cd = tables
    tok, blk, rope, tab, col, rowv, st = _head_specs(steps, rev=True)

    def body(dru_ref, wru_ref, o_ref, q_ref, k_ref, v_ref, g_ref, c_ref, s_ref, st_ref, d_ref, dt_ref, qd_ref, kd_ref, cd_ref,
             dq_ref, dk_ref, dv_ref, dg_ref, gstate):
        n = pl.program_id(0)

        @pl.when(n == 0)
        def _():
            gstate[...] = jnp.zeros(gstate.shape, F32)

        for sub in reversed(range(RET_SUBS)):
            rows = slice(sub * RET_BLOCK, (sub + 1) * RET_BLOCK)
            cs, sn = c_ref[rows, :], s_ref[rows, :]
            dret = _dot(dru_ref[rows, :], wru_ref[...], "nt")
            for hh in range(HEADS):
                sl = slice(hh * HEAD_DIM, (hh + 1) * HEAD_DIM)
                o_v, g, dr = o_ref[rows, sl], g_ref[rows, sl].astype(F32), dret[:, sl]
                sg = _sigmoid(g)
                r = lax.rsqrt(jnp.mean(o_v * o_v, axis=-1, keepdims=True) + NORM_EPS)
                rn = o_v * r
                d_rn = dr * (g * sg)
                dg_ref[rows, sl] = (dr * rn * (sg * (1.0 + g * (1.0 - sg)))).astype(BF16)
                d_o = r * (d_rn - rn * jnp.mean(d_rn * rn, axis=-1, keepdims=True))
                dob = d_o.astype(BF16)

                q, k = _rotate(q_ref, rows, hh, cs, sn), _rotate(k_ref, rows, hh, cs, sn, K_SCALE)
                v = v_ref[rows, sl].astype(BF16)
                qb, kb = q.astype(BF16), k.astype(BF16)
                qdv, kdv = qd_ref[hh], kd_ref[hh]
                s_t = (_dot(kb, qb, "nt") * dt_ref[hh]).astype(BF16)
                p_t = (_dot(v, dob, "nt") * dt_ref[hh]).astype(BF16)
                p = (_dot(dob, v, "nt") * d_ref[hh]).astype(BF16)
                stb = st_ref[hh, sub]
                gb = gstate[hh].astype(BF16)
                _unrotate_into(dq_ref, rows, hh, _dot(p, kb) + _dot(dob, stb, "nt") * qdv, cs, sn)
                _unrotate_into(dk_ref, rows, hh, _dot(p_t, qb) + _dot(v, gb, "nt") * kdv, cs, sn, K_SCALE)
                dv_ref[rows, sl] = (_dot(s_t, dob) + _dot((k * kdv).astype(BF16), gb)).astype(BF16)
                gstate[hh] = gstate[hh] * cd_ref[hh] + _dot((q * qdv).astype(BF16), dob, "tn")

    return pl.pallas_call(
        body, name=name, grid=(steps,),
        in_specs=[tok, pl.BlockSpec((D_MODEL, D_MODEL), lambda n: (0, 0), pipeline_mode=pl.Buffered(1)), tok] + blk
                 + [rope, rope, st, tab, tab, col, col, rowv],
        out_specs=[tok, tok, tok, tok],
        out_shape=[jax.ShapeDtypeStruct((t, D_MODEL), BF16)] * 4,
        scratch_shapes=[pltpu.VMEM((HEADS, HEAD_DIM, HEAD_DIM), F32)],
        compiler_params=_cparams(("arbitrary",)),
    )(dru, w_ru, o, proj, proj, proj, proj, cos, sin, states, dmat, dmat_t, qd, kd, cd)


POOL_TILE = 256
POOL_SUBS = 2
POOL_STEP = POOL_SUBS * POOL_TILE


def _pool_tables():
    b = POOL_TILE
    tt = np.arange(b)[:, None]
    jj = np.arange(b)[None, :]
    cur, prev = [], []
    for w in POOL_WINDOWS:
        cur.append(((tt - jj >= 0) & (tt - jj <= w - 1)).astype(np.float32))
        prev.append((tt - (jj - b) <= w - 1).astype(np.float32))
    cur, prev = np.stack(cur), np.stack(prev)
    as16 = lambda v: jnp.asarray(v, dtype=BF16)
    return as16(cur), as16(prev), as16(np.swapaxes(cur, 1, 2)), as16(np.swapaxes(prev, 1, 2))


def _split2(x):
    hi = x.astype(BF16)
    return hi, (x - hi.astype(F32)).astype(BF16)


POOL_BLOCK = 4


def _pool_count(n, window):
    tpos = n * POOL_TILE + lax.broadcasted_iota(jnp.int32, (POOL_TILE, 1), 0)
    return jnp.minimum(tpos + 1, window).astype(F32)


def _pool_fwd(name, proj, pool_w, scale, tables):
    t = proj.shape[0]
    steps = t // POOL_STEP
    mc, mp, _, _ = tables
    tab = _full_spec((GROUPS, POOL_TILE, POOL_TILE))
    row = _row_spec(POOL_STEP, D_MODEL)

    def body(pc_ref, pp_ref, mc_ref, mp_ref, w_ref, sc_ref, pm_ref, mix_ref, po_ref):
        n = pl.program_id(0)
        for sub in range(POOL_SUBS):
            rows = slice(sub * POOL_TILE, (sub + 1) * POOL_TILE)
            tile = n * POOL_SUBS + sub
            for g, window in enumerate(POOL_WINDOWS):
                sl = slice(g * GROUP_DIM, (g + 1) * GROUP_DIM)
                p = pc_ref[rows, sl]
                if sub == 0:
                    before = jnp.where(n > 0, _dot(mp_ref[g], pp_ref[:, sl]), 0.0)
                else:
                    before = _dot(mp_ref[g], pc_ref[(sub - 1) * POOL_TILE:sub * POOL_TILE, sl])
                pm = ((_dot(mc_ref[g], p) + before) / _pool_count(tile, window) - p.astype(F32)).astype(BF16)
                pm_ref[rows, sl] = pm
                mixed = _dot(pm, w_ref[g])
                mix_ref[rows, sl] = mixed
                po_ref[rows, sl] = (mixed * sc_ref[:, sl]).astype(BF16)

    return pl.pallas_call(
        body, name=name, grid=(steps,),
        in_specs=[_row_spec(POOL_STEP, D_MODEL, POOL_BLOCK),
                  pl.BlockSpec((POOL_TILE, D_MODEL), lambda n: (jnp.maximum(n * POOL_SUBS - 1, 0), POOL_BLOCK)),
                  tab, tab, _full_spec((GROUPS, GROUP_DIM, GROUP_DIM)), _full_spec((1, D_MODEL))],
        out_specs=[row] * 3,
        out_shape=[jax.ShapeDtypeStruct((t, D_MODEL), BF16), jax.ShapeDtypeStruct((t, D_MODEL), F32),
                   jax.ShapeDtypeStruct((t, D_MODEL), BF16)],
        compiler_params=_cparams(("parallel",)),
    )(proj, proj, mc, mp, pool_w, scale)


def _pool_bwd(name, dpu, w_pu, pm, mixed, pool_w, scale, tables):
    t = dpu.shape[0]
    steps = t // POOL_STEP
    _, _, mct, mpt = tables
    cur = pl.BlockSpec((POOL_STEP, D_MODEL), lambda n: (steps - 1 - n, 0))
    tab = _full_spec((GROUPS, POOL_TILE, POOL_TILE))
    wspec = _full_spec((GROUPS, GROUP_DIM, GROUP_DIM))
    sspec = _full_spec((1, D_MODEL))

    def body(dpu_ref, wpu_ref, pm_ref, mix_ref, mct_ref, mpt_ref, w_ref, sc_ref, dp_ref, dw_ref, ds_ref, later):
        n = pl.program_id(0)

        @pl.when(n == 0)
        def _():
            dw_ref[...] = jnp.zeros(dw_ref.shape, F32)
            ds_ref[...] = jnp.zeros(ds_ref.shape, F32)
            later[...] = jnp.zeros(later.shape, F32)

        for sub in reversed(range(POOL_SUBS)):
            rows = slice(sub * POOL_TILE, (sub + 1) * POOL_TILE)
            tile = (steps - 1 - n) * POOL_SUBS + sub
            dpo = _dot(dpu_ref[rows, :], wpu_ref[...], "nt")
            for g, window in enumerate(POOL_WINDOWS):
                sl = slice(g * GROUP_DIM, (g + 1) * GROUP_DIM)
                dc, sc = dpo[:, sl], sc_ref[:, sl]
                dmix = (dc * sc).astype(BF16)
                dpm = _dot(dmix, w_ref[g], "nt")
                e = dpm / _pool_count(tile, window)
                e_hi, e_lo = _split2(e)
                f_hi, f_lo = _split2(later[g])
                mctv, mptv = mct_ref[g], mpt_ref[g]
                back = _dot(mctv, e_hi) + _dot(mctv, e_lo)
                after = _dot(mptv, f_hi) + _dot(mptv, f_lo)
                dp_ref[rows, sl] = (back + after - dpm).astype(BF16)
                later[g] = e
                dw_ref[g] += _dot(pm_ref[rows, sl], dmix, "tn")
                ds_ref[:, sl] += jnp.sum(dc * mix_ref[rows, sl], axis=0, keepdims=True)

    return pl.pallas_call(
        body, name=name, grid=(steps,),
        in_specs=[cur, pl.BlockSpec((D_MODEL, D_MODEL), lambda n: (0, 0), pipeline_mode=pl.Buffered(1)), cur, cur, tab, tab,
                  wspec, sspec],
        out_specs=[cur, wspec, sspec],
        out_shape=[jax.ShapeDtypeStruct((t, D_MODEL), BF16), jax.ShapeDtypeStruct((GROUPS, GROUP_DIM, GROUP_DIM), F32),
                   jax.ShapeDtypeStruct((1, D_MODEL), F32)],
        scratch_shapes=[pltpu.VMEM((GROUPS, POOL_TILE, GROUP_DIM), F32)],
        compiler_params=_cparams(("arbitrary",)),
    )(dpu, w_pu, pm, mixed, mct, mpt, pool_w, scale)


GATE0_BLOCK, GATE1_BLOCK = 5, 6


def _merge_fwd(name, ret, po, w_ru, w_pu, w_out, proj, bias, h, next_g, tm=512):
    t = ret.shape[0]

    def body(r_ref, p_ref, wr_ref, wp_ref, wo_ref, g0_ref, g1_ref, b_ref, h_ref, ng_ref, m_ref, ru_ref, pu_ref, ho_ref, n_ref):
        ru = _dot(r_ref[...], wr_ref[...])
        pu = _dot(p_ref[...], wp_ref[...])
        ru_ref[...] = ru
        pu_ref[...] = pu
        merged = (_sigmoid(g0_ref[...].astype(F32) + b_ref[0:1, :]) * ru
                  + _sigmoid(g1_ref[...].astype(F32) + b_ref[1:2, :]) * pu).astype(BF16)
        m_ref[...] = merged
        h_new = h_ref[...] + _dot(merged, wo_ref[...])
        ho_ref[...] = h_new
        n_ref[...] = _normed(h_new, ng_ref[...]).astype(BF16)

    row = _row_spec(tm, D_MODEL)
    wspec = pl.BlockSpec((D_MODEL, D_MODEL), lambda i: (0, 0), pipeline_mode=pl.Buffered(1))
    return pl.pallas_call(
        body, name=name, grid=(t // tm,),
        in_specs=[row, row, wspec, wspec, wspec, _row_spec(tm, D_MODEL, GATE0_BLOCK), _row_spec(tm, D_MODEL, GATE1_BLOCK),
                  _full_spec((2, D_MODEL)), row, _full_spec((1, D_MODEL))],
        out_specs=[row] * 5,
        out_shape=[jax.ShapeDtypeStruct((t, D_MODEL), BF16), jax.ShapeDtypeStruct((t, D_MODEL), F32),
                   jax.ShapeDtypeStruct((t, D_MODEL), F32), jax.ShapeDtypeStruct((t, D_MODEL), F32),
                   jax.ShapeDtypeStruct((t, D_MODEL), BF16)],
        compiler_params=_cparams(("parallel",)),
    )(ret, po, w_ru, w_pu, w_out, proj, proj, bias, h, next_g)


def _merge_bwd(name, dh_b, w_out, ru, pu, proj, bias, tm=512):
    t = dh_b.shape[0]

    def body(dh_ref, wo_ref, ru_ref, pu_ref, g0_ref, g1_ref, b_ref, dru_ref, dpu_ref, dg0_ref, dg1_ref, db_ref):
        i = pl.program_id(0)
        d = _dot(dh_ref[...], wo_ref[...], "nt")
        s0 = _sigmoid(g0_ref[...].astype(F32) + b_ref[0:1, :])
        s1 = _sigmoid(g1_ref[...].astype(F32) + b_ref[1:2, :])
        dru_ref[...] = (d * s0).astype(BF16)
        dpu_ref[...] = (d * s1).astype(BF16)
        dg0 = d * ru_ref[...] * (s0 * (1.0 - s0))
        dg1 = d * pu_ref[...] * (s1 * (1.0 - s1))
        dg0_ref[...] = dg0.astype(BF16)
        dg1_ref[...] = dg1.astype(BF16)
        part0 = jnp.sum(dg0, axis=0, keepdims=True)
        part1 = jnp.sum(dg1, axis=0, keepdims=True)

        @pl.when(i == 0)
        def _():
            db_ref[0:1, :] = part0
            db_ref[1:2, :] = part1

        @pl.when(i > 0)
        def _():
            db_ref[0:1, :] += part0
            db_ref[1:2, :] += part1

    row = _row_spec(tm, D_MODEL)
    return pl.pallas_call(
        body, name=name, grid=(t // tm,),
        in_specs=[row, pl.BlockSpec((D_MODEL, D_MODEL), lambda i: (0, 0), pipeline_mode=pl.Buffered(1)), row, row,
                  _row_spec(tm, D_MODEL, GATE0_BLOCK), _row_spec(tm, D_MODEL, GATE1_BLOCK), _full_spec((2, D_MODEL))],
        out_specs=[row, row, row, row, _full_spec((2, D_MODEL))],
        out_shape=[jax.ShapeDtypeStruct((t, D_MODEL), BF16)] * 4 + [jax.ShapeDtypeStruct((2, D_MODEL), F32)],
        compiler_params=_cparams(("arbitrary",)),
    )(dh_b, w_out, ru, pu, proj, proj, bias)


def _half_scale(acc):
    return (FFN_RES_WEIGHT * acc,)


def _normed(h, g):
    return h * lax.rsqrt(jnp.mean(h * h, axis=-1, keepdims=True) + NORM_EPS) * g


def _residual_half_norm(acc, res, g):
    h = res + FFN_RES_WEIGHT * acc
    return h, _normed(h, g)


FF_TILE = D_FF // 2
DW_TILE = 256
SAVED_FF_DTYPE = BF16
FF_CHUNKS = ((0, 512), (512, 1024), (1024, FF_TILE))


def _ffn_in(name, nrm, w_in, tm=512):
    t = nrm.shape[0]
    nj = D_FF // FF_TILE

    def body(n_ref, wg_ref, wu_ref, a_ref, mid_ref):
        nv = n_ref[...]
        for c0, c1 in FF_CHUNKS:
            gate = _dot(nv, wg_ref[:, c0:c1])
            up = _dot(nv, wu_ref[:, c0:c1])
            s = _sigmoid(gate)
            silu = gate * s
            a_ref[0, :, c0:c1] = (FFN_RES_WEIGHT * up * (s * (1.0 + gate * (1.0 - s)))).astype(a_ref.dtype)
            a_ref[1, :, c0:c1] = (FFN_RES_WEIGHT * silu).astype(a_ref.dtype)
            mid_ref[:, c0:c1] = (silu * up).astype(BF16)

    return pl.pallas_call(
        body, name=name, grid=(nj, t // tm),
        in_specs=[pl.BlockSpec((tm, D_MODEL), lambda j, i: (i, 0)),
                  pl.BlockSpec((D_MODEL, FF_TILE), lambda j, i: (0, j)),
                  pl.BlockSpec((D_MODEL, FF_TILE), lambda j, i: (0, j + nj))],
        out_specs=[pl.BlockSpec((2, tm, FF_TILE), lambda j, i: (0, i, j)), pl.BlockSpec((tm, FF_TILE), lambda j, i: (i, j))],
        out_shape=[jax.ShapeDtypeStruct((2, t, D_FF), SAVED_FF_DTYPE), jax.ShapeDtypeStruct((t, D_FF), BF16)],
        compiler_params=_cparams(("parallel", "parallel")),
    )(nrm, w_in, w_in)


def _ffn_dact(name, dout_b, w_out, a, tm=512):
    t = dout_b.shape[0]

    def body(d_ref, w_ref, a_ref, da_ref):
        dv = d_ref[...]
        for c0, c1 in FF_CHUNKS:
            dm = _dot(dv, w_ref[c0:c1, :], "nt")
            da_ref[0, :, c0:c1] = (dm * a_ref[0, :, c0:c1].astype(F32)).astype(BF16)
            da_ref[1, :, c0:c1] = (dm * a_ref[1, :, c0:c1].astype(F32)).astype(BF16)

    blk = pl.BlockSpec((2, tm, FF_TILE), lambda j, i: (0, i, j))
    return pl.pallas_call(
        body, name=name, grid=(D_FF // FF_TILE, t // tm),
        in_specs=[pl.BlockSpec((tm, D_MODEL), lambda j, i: (i, 0)), pl.BlockSpec((FF_TILE, D_MODEL), lambda j, i: (j, 0)), blk],
        out_specs=blk,
        out_shape=jax.ShapeDtypeStruct((2, t, D_FF), BF16),
        compiler_params=_cparams(("parallel", "parallel")),
    )(dout_b, w_out, a)


def _ffn_fwd(tag, h, nrm, get_w_in, get_w_out, finish):
    t = h.shape[0]
    w_in = get_w_in(nrm)
    a, mid = _ffn_in(f"{tag}_in", nrm, w_in, tm=min(512, t))
    w_out = get_w_out(mid)
    return finish(mid, w_out), (nrm, a, mid, w_in, w_out)


def _ffn_bwd(tag, h, g, saved, dout, dout_b, on_grads, flush):
    t = h.shape[0]
    nrm, a, mid, w_in, w_out = saved
    d_w_out = _matmul(f"{tag}_dwout", mid, dout_b, "tn", D_FF, D_MODEL, t, DW_TILE, D_MODEL, t, [BF16], epilogue=_half_scale,
                      resident="b")
    da = _ffn_dact(f"{tag}_dact", dout_b, w_out, a, tm=min(512, t))
    nj = D_FF // DW_TILE
    d_w_in = _dw_resident(f"{tag}_dwin", nrm, [da], [pl.BlockSpec((None, t, DW_TILE), lambda s: (s // nj, 0, s % nj))],
                          2 * nj, None, DW_TILE)
    tie = on_grads({f"{tag}_w_in": d_w_in, f"{tag}_w_out": d_w_out})
    tm = min(256, t)
    dh, dh_b, dg = _proj_norm_bwd(f"{tag}_dn", [da], [pl.BlockSpec((2, tm, D_FF), lambda i: (0, i, 0))],
                                  ((0, 0, 0, D_FF), (0, 1, D_FF, 2 * D_FF)), w_in, h, g if tie is None else g + tie, dout, tm)
    return dh, dh_b, dg, flush(dh)


def _dw_resident(name, u, pieces, piece_specs, n_tiles, which_piece, tn):
    t = u.shape[0]
    npc = len(pieces)

    def body(*refs):
        u_ref, p_refs, o_ref, ut_ref = refs[0], refs[1:1 + npc], refs[1 + npc], refs[2 + npc]
        s = pl.program_id(0)

        @pl.when(s == 0)
        def _():
            ut_ref[...] = u_ref[...].T

        if npc == 1:
            o_ref[...] = _dot(ut_ref[...], p_refs[0][...]).astype(BF16)
        for which in range(npc if npc > 1 else 0):
            @pl.when(which_piece(s) == which)
            def _(which=which):
                o_ref[...] = _dot(ut_ref[...], p_refs[which][...]).astype(BF16)

    return pl.pallas_call(
        body, name=name, grid=(n_tiles,),
        in_specs=[pl.BlockSpec((t, D_MODEL), lambda s: (0, 0), pipeline_mode=pl.Buffered(1))] + list(piece_specs),
        out_specs=pl.BlockSpec((D_MODEL, tn), lambda s: (0, s)),
        out_shape=jax.ShapeDtypeStruct((D_MODEL, n_tiles * tn), BF16),
        scratch_shapes=[pltpu.VMEM((D_MODEL, t), BF16)],
        compiler_params=_cparams(("arbitrary",)),
    )(u, *pieces)


def _mix_dwin(name, u, pieces, tn=256):
    t = u.shape[0]
    nj = D_MODEL // tn
    specs = [pl.BlockSpec((t, tn), lambda s, k=k: (0, jnp.clip(s - k * nj, 0, nj - 1))) for k in range(len(pieces))]
    return _dw_resident(name, u, pieces, specs, len(pieces) * nj, lambda s: s // nj, tn)


def _local_step(x, target, vec, get_w, relay_w, on_grads, flush):
    t = x.shape[0]
    cos, sin = _rope_tables(t)
    rtab = _retention_tables()
    ptab = _pool_tables()
    w = {}

    def getter(group, name):
        def get(after):
            if name not in w:
                w.update(get_w(group, after))
            return w[name]
        return get

    nrm1 = _rmsnorm_fwd("ffn1_norm", x, vec["norm_ffn1"])
    def out_and_norm(mid, w_out):
        return _matmul("ffn1_out", mid, w_out, "nn", t, D_MODEL, D_FF, 512, D_MODEL, D_FF, [F32, BF16],
                       extras=(x,), consts=(vec["norm_mix"],), epilogue=_residual_half_norm)

    (h1, u), s1 = _ffn_fwd("ffn1", x, nrm1, getter(0, "ffn1_w_in"), getter(1, "ffn1_w_out"), out_and_norm)
    w.update(get_w(2, u))
    proj = _matmul("mix_in", u, w["w_in"], "nn", t, IN_WIDTH, D_MODEL, 1024, 1024, D_MODEL, [BF16], n_outer=True)
    o, ret, states = _retention_fwd("retention", proj, cos, sin, rtab)
    pm, mixed, po = _pool_fwd("pool", proj, w["pool_w"], vec["pool_scale"], ptab)
    relay_w(3, po)
    merged, ru, pu, h2, nrm2 = _merge_fwd("merge", ret, po, w["w_ret_up"], w["w_pool_up"], w["w_out"], proj, w["gate_bias"],
                                          h1, vec["norm_ffn2"], tm=min(512, t))
    def out_and_loss(mid, w_out):
        return _out_loss_and_grad("ffn2_out_loss", mid, w_out, h2, vec["norm_final"], target, tm=min(512, t))

    (dh3, dh3_b, dg_final, loss), s2 = _ffn_fwd("ffn2", h2, nrm2, getter(3, "ffn2_w_in"), getter(3, "ffn2_w_out"), out_and_loss)

    def tied(v, tie):
        return v if tie is None else v + tie

    dh2, dh2_b, dg_ffn2, tie = _ffn_bwd("ffn2", h2, vec["norm_ffn2"], s2, dh3, dh3_b, on_grads, flush)
    def square_dw(name, act, grad):
        return _matmul(name, act, grad, "tn", D_MODEL, D_MODEL, t, D_MODEL, D_MODEL, 1024, [BF16])

    d_w_out = square_dw("mix_dwout", merged, dh2_b)
    dru, dpu, dg0, dg1, d_bias = _merge_bwd("merge_bwd", dh2_b, w["w_out"], ru, pu, proj, tied(w["gate_bias"], tie))
    d_w_ru = square_dw("mix_dwru", ret, dru)
    d_w_pu = square_dw("mix_dwpu", po, dpu)
    dp, d_pool_w, d_scale = _pool_bwd("pool_bwd", dpu, w["w_pool_up"], pm, mixed, w["pool_w"], vec["pool_scale"], ptab)
    dq, dk, dv, dgr = _retention_bwd("retention_bwd", dru, w["w_ret_up"], o, proj, cos, sin, states, rtab)
    dproj = [dq, dk, dv, dgr, dp, dg0, dg1]
    d_w_in = _mix_dwin("mix_dwin", u, dproj)
    tie = on_grads(dict(w_in=d_w_in, pool_w=d_pool_w.astype(BF16), w_ret_up=d_w_ru, w_pool_up=d_w_pu, w_out=d_w_out))
    tm = min(256, t)
    dh1, dh1_b, dg_mix = _proj_norm_bwd("mix_du", dproj, [_row_spec(tm, D_MODEL)] * len(dproj),
                                        [(k, None, k * D_MODEL, (k + 1) * D_MODEL) for k in range(len(dproj))],
                                        w["w_in"], h1, tied(vec["norm_mix"], tie), dh2, tm)
    tie = flush(dh1)
    dx, _, dg_ffn1, _ = _ffn_bwd("ffn1", x, tied(vec["norm_ffn1"], tie), s1, dh1, dh1_b, on_grads, flush)

    small = dict(norm_ffn1=dg_ffn1, norm_mix=dg_mix, gate_bias=d_bias, pool_scale=d_scale, norm_ffn2=dg_ffn2,
                 norm_final=dg_final)
    return loss[0, 0], dx, small


BIG = ("ffn1_w_in", "ffn1_w_out", "w_in", "pool_w", "w_ret_up", "w_pool_up", "w_out", "ffn2_w_in", "ffn2_w_out")
KIND = dict(ffn1_w_in="col", ffn1_w_out="row", w_in="col", pool_w="pool", w_ret_up="row", w_pool_up="row", w_out="row",
            ffn2_w_in="col", ffn2_w_out="row", gate_bias="col")
ANY = pl.BlockSpec(memory_space=pl.ANY)


def _place():
    x, y, c = lax.axis_index("x"), lax.axis_index("y"), lax.axis_index("c")
    chips = [(1 - x, y), (x, 1 - y), (1 - x, 1 - y)]
    return x, y, c, chips


def _full_view_shape(kind, local_shape):
    if kind == "col":
        return (2, local_shape[0] // 2, N_CHIPS * local_shape[1])
    if kind == "row":
        return (N_CHIPS, 2, local_shape[0] // 2, local_shape[1])
    return (GROUPS, N_CHIPS, 2, local_shape[1] // 2, local_shape[2])


def _local_view(kind, arr):
    if kind == "pool":
        return arr.reshape(GROUPS, 2, arr.shape[1] // 2, arr.shape[2])
    return arr.reshape(2, arr.shape[0] // 2, arr.shape[1])


def _blk(kind, ref, s, c):
    if kind == "col":
        cs = ref.shape[2] // N_CHIPS
        return ref.at[c, :, pl.ds(pl.multiple_of(s * cs, 128), cs)]
    if kind == "row":
        return ref.at[s, c]
    return ref.at[:, s, c]


def _half(kind, ref, c):
    return ref.at[:, c] if kind == "pool" else ref.at[c]


def _shard(kind, ref, s):
    if kind == "col":
        cs = ref.shape[2] // N_CHIPS
        return ref.at[:, :, pl.ds(pl.multiple_of(s * cs, 128), cs)]
    if kind == "row":
        return ref.at[s]
    return ref.at[:, s]


HBM = pl.BlockSpec(memory_space=pltpu.HBM)
SEM = pl.BlockSpec(memory_space=pltpu.SEMAPHORE)
EFFECT = pltpu.SideEffectType.DATAFLOW_SIDE_EFFECTING
WEIGHT_GROUPS = (("gate_bias", "ffn1_w_in"), ("ffn1_w_out",), ("w_in", "pool_w", "w_ret_up", "w_pool_up", "w_out"), ("ffn2_w_in", "ffn2_w_out"))
GRAD_GROUPS = (("ffn2_w_in", "ffn2_w_out"), ("w_in", "pool_w", "w_ret_up", "w_pool_up", "w_out"), ("ffn1_w_in", "ffn1_w_out"))


def _hbm(a):
    return pltpu.with_memory_space_constraint(a, pltpu.HBM)


def _natural(kind, o):
    if kind == "col":
        return o.reshape(o.shape[0] * o.shape[1], o.shape[2])
    if kind == "row":
        return o.reshape(-1, o.shape[3])
    return o.reshape(GROUPS, -1, o.shape[4])


def _ici_copy(kind, loc, full, j, chips, s, c, send_sem, recv_sem):
    px, py = chips[j]
    return (pltpu.make_async_remote_copy(src_ref=_half(kind, loc, c), dst_ref=_blk(kind, full, s, c), send_sem=send_sem,
                                         recv_sem=recv_sem, device_id=(px, py, c), device_id_type=MESH),
            pltpu.make_async_remote_copy(src_ref=_half(kind, loc, c), dst_ref=_blk(kind, full, 2 * px + py, c), send_sem=send_sem,
                                         recv_sem=recv_sem, device_id=(px, py, c), device_id_type=MESH))


def _gather_start(tag, group_ids, shards):
    grps = [WEIGHT_GROUPS[g] for g in group_ids]
    names = [nm for grp in grps for nm in grp]
    kinds = [KIND[nm] for nm in names]
    n, ng = len(names), len(grps)
    locs = [_hbm(_local_view(KIND[nm], shards[nm])) for nm in names]
    lands = [_hbm(lax.empty(_full_view_shape(KIND[nm], shards[nm].shape), shards[nm].dtype)) for nm in names]
    first = np.cumsum([0] + [len(grp) for grp in grps])

    def body(*refs):
        loc, full = refs[:n], refs[n:2 * n]
        send_sems, recv_sems = refs[2 * n:2 * n + ng], refs[2 * n + ng:2 * n + 2 * ng]
        token = refs[-1]
        x, y, c, chips = _place()
        s = 2 * x + y
        for g in range(ng):
            for a in range(first[g], first[g + 1]):
                for j in range(3):
                    k = 3 * (a - first[g]) + j
                    _ici_copy(kinds[a], loc[a], full[a], j, chips, s, c, send_sems[g].at[k], recv_sems[g].at[k])[0].start()
        token[...] = jnp.zeros(token.shape, F32)

    sem_shapes = [pltpu.SemaphoreType.DMA((3 * len(grp),)) for grp in grps]
    outs = pl.pallas_call(
        body, name=f"gather_start_{tag}",
        in_specs=[HBM] * (2 * n),
        out_specs=[SEM] * (2 * ng) + [HBM] * (2 * n) + [pl.BlockSpec(memory_space=pltpu.VMEM)],
        out_shape=sem_shapes + sem_shapes + [pltpu.HBM(a.shape, a.dtype) for a in locs + lands] + [jax.ShapeDtypeStruct((8, 128), F32)],
        input_output_aliases={i: 2 * ng + i for i in range(2 * n)},
        compiler_params=pltpu.CompilerParams(has_side_effects=EFFECT),
    )(*locs, *lands)
    send_sems, recv_sems = outs[:ng], outs[ng:2 * ng]
    locs_t, lands_t = outs[2 * ng:2 * ng + n], outs[2 * ng + n:2 * ng + 2 * n]
    groups = {}
    for k, g in enumerate(group_ids):
        sl = slice(first[k], first[k + 1])
        groups[g] = (send_sems[k], recv_sems[k], list(locs_t[sl]), list(lands_t[sl]))
    return groups, outs[-1]


def _forward_copies(kinds, loc, full, send_sems, recv_sems):
    x, y, c, chips = _place()
    s = 2 * x + y

    def remote(a, k, src, dst):
        return pltpu.make_async_remote_copy(src_ref=src, dst_ref=dst, send_sem=send_sems.at[4 * a + k],
                                            recv_sem=recv_sems.at[4 * a + k], device_id=(x, y, 1 - c), device_id_type=MESH)

    sends, arrivals = [], []
    for a, kind in enumerate(kinds):
        for j, (px, py) in enumerate(chips):
            theirs, from_sib = _blk(kind, full[a], 2 * px + py, c), _blk(kind, full[a], 2 * px + py, 1 - c)
            sends.append(remote(a, j, theirs, theirs))
            arrivals.append(remote(a, j, from_sib, from_sib))
        own = _shard(kind, full[a], s)
        sends.append(remote(a, 3, loc[a], own))
        arrivals.append(remote(a, 3, own, own))
    return sends, arrivals


def _gather_relay(g, group, after):
    names = WEIGHT_GROUPS[g]
    kinds = [KIND[nm] for nm in names]
    m = len(names)
    ici_send, ici_recv, locs, lands = group

    def body(*refs):
        loc, full = refs[:m], refs[m:2 * m]
        ici_s, ici_r = refs[2 * m], refs[2 * m + 1]
        d2d_s, d2d_r = refs[2 * m + 2 + len(after)], refs[2 * m + 3 + len(after)]
        x, y, c, chips = _place()
        for a in range(m):
            for j in range(3):
                sent, landed = _ici_copy(kinds[a], loc[a], full[a], j, chips, 2 * x + y, c, ici_s.at[3 * a + j], ici_r.at[3 * a + j])
                sent.wait_send()
                landed.wait_recv()
        for cp in _forward_copies(kinds, loc, full, d2d_s, d2d_r)[0]:
            cp.start()

    sem_shape = pltpu.SemaphoreType.DMA((4 * m,))
    outs = pl.pallas_call(
        body, name=f"gather_relay_{g}",
        in_specs=[HBM] * (2 * m) + [SEM, SEM] + [ANY] * len(after), out_specs=[SEM, SEM] + [HBM] * (2 * m),
        out_shape=[sem_shape, sem_shape] + [pltpu.HBM(a.shape, a.dtype) for a in locs + lands],
        input_output_aliases={i: 2 + i for i in range(2 * m)},
        compiler_params=pltpu.CompilerParams(has_side_effects=EFFECT),
    )(*locs, *lands, ici_send, ici_recv, *after)
    return outs[0], outs[1], list(outs[2:2 + m]), list(outs[2 + m:2 + 2 * m])


def _gather_land(g, state, after):
    names = WEIGHT_GROUPS[g]
    kinds = [KIND[nm] for nm in names]
    m = len(names)
    d2d_send, d2d_recv, locs, lands = state

    def body(*refs):
        sends, arrivals = _forward_copies(kinds, refs[:m], refs[m:2 * m], refs[2 * m], refs[2 * m + 1])
        for cp in sends:
            cp.wait_send()
        for cp in arrivals:
            cp.wait_recv()

    outs = pl.pallas_call(
        body, name=f"gather_land_{g}",
        in_specs=[HBM] * (2 * m) + [SEM, SEM] + [ANY] * len(after), out_specs=[HBM] * (2 * m),
        out_shape=[pltpu.HBM(a.shape, a.dtype) for a in locs + lands],
        input_output_aliases={i: i for i in range(2 * m)},
        compiler_params=pltpu.CompilerParams(has_side_effects=EFFECT),
    )(*locs, *lands, d2d_send, d2d_recv, *after)
    return {nm: _natural(k, o) for nm, k, o in zip(names, kinds, outs[m:])}


def _gather_finish(g, group, after):
    names = WEIGHT_GROUPS[g]
    kinds = [KIND[nm] for nm in names]
    m = len(names)
    send_sem, recv_sem, locs, lands = group

    def wait_body(*refs):
        loc, full = refs[:m], refs[m:2 * m]
        send_sems, recv_sems = refs[2 * m], refs[2 * m + 1]
        x, y, c, chips = _place()
        s = 2 * x + y
        for a in range(m):
            for j in range(3):
                k = 3 * a + j
                sent, landed = _ici_copy(kinds[a], loc[a], full[a], j, chips, s, c, send_sems.at[k], recv_sems.at[k])
                sent.wait_send()
                landed.wait_recv()

    outs = pl.pallas_call(
        wait_body, name=f"gather_wait_{g}",
        in_specs=[HBM] * (2 * m) + [SEM, SEM] + [ANY] * len(after), out_specs=[HBM] * (2 * m),
        out_shape=[pltpu.HBM(a.shape, a.dtype) for a in locs + lands],
        input_output_aliases={i: i for i in range(2 * m)},
        compiler_params=pltpu.CompilerParams(has_side_effects=EFFECT),
    )(*locs, *lands, send_sem, recv_sem, *after)
    locs, lands = outs[:m], outs[m:]

    def forward_body(*refs):
        sends, arrivals = _forward_copies(kinds, refs[:m], refs[2 * m:3 * m], *refs[3 * m:])
        for cp in sends:
            cp.start()
        for cp in arrivals:
            cp.wait_recv()
        for cp in sends:
            cp.wait_send()

    outs = pl.pallas_call(
        forward_body, name=f"gather_forward_{g}",
        in_specs=[ANY] * (2 * m), out_specs=[ANY] * m,
        out_shape=[jax.ShapeDtypeStruct(a.shape, a.dtype) for a in lands],
        input_output_aliases={m + i: i for i in range(m)},
        scratch_shapes=[pltpu.SemaphoreType.DMA((4 * m,)), pltpu.SemaphoreType.DMA((4 * m,))],
    )(*locs, *lands)
    return {nm: _natural(k, o) for nm, k, o in zip(names, kinds, outs)}


def _grad_view(kind, g):
    if kind == "col":
        return g.reshape(2, g.shape[0] // 2, g.shape[1])
    if kind == "row":
        return g.reshape(N_CHIPS, 2, g.shape[0] // (2 * N_CHIPS), g.shape[1])
    return g.reshape(GROUPS, N_CHIPS, 2, g.shape[1] // (2 * N_CHIPS), g.shape[2])


def _pair_copies(kinds, g, got, send_sems, recv_sems):
    x, y, c, _ = _place()

    def other_half(kind, ref):
        if kind == "col":
            return ref.at[1 - c]
        if kind == "row":
            return ref.at[:, 1 - c]
        return ref.at[:, :, 1 - c]

    return [pltpu.make_async_remote_copy(src_ref=other_half(kinds[a], g[a]), dst_ref=got[a], send_sem=send_sems.at[a],
                                         recv_sem=recv_sems.at[a], device_id=(x, y, 1 - c), device_id_type=MESH)
            for a in range(len(kinds))]


def _pair_exchange_start(tag, names, views):
    kinds = [KIND[nm] for nm in names]
    n = len(names)

    def got_shape(kind, v):
        if kind == "col":
            return v.shape[1:]
        if kind == "row":
            return (v.shape[0],) + v.shape[2:]
        return v.shape[:2] + v.shape[3:]

    srcs = [_hbm(views[nm]) for nm in names]
    lands = [_hbm(lax.empty(got_shape(k, views[nm]), BF16)) for nm, k in zip(names, kinds)]

    def body(*refs):
        g, got = refs[:n], refs[n:2 * n]
        for cp in _pair_copies(kinds, g, got, refs[2 * n], refs[2 * n + 1]):
            cp.start()
        refs[-1][...] = jnp.zeros(refs[-1].shape, F32)

    sem_shape = pltpu.SemaphoreType.DMA((n,))
    outs = pl.pallas_call(
        body, name=f"grad_pair_exchange_start_{tag}",
        in_specs=[HBM] * (2 * n),
        out_specs=[SEM, SEM] + [HBM] * (2 * n) + [pl.BlockSpec(memory_space=pltpu.VMEM)],
        out_shape=[sem_shape, sem_shape] + [pltpu.HBM(a.shape, a.dtype) for a in srcs + lands] + [jax.ShapeDtypeStruct((8, 128), F32)],
        input_output_aliases={i: 2 + i for i in range(2 * n)},
        compiler_params=pltpu.CompilerParams(has_side_effects=EFFECT),
    )(*srcs, *lands)
    return (outs[0], outs[1], list(outs[2:2 + n]), list(outs[2 + n:2 + 2 * n])), outs[-1]


def _pair_exchange_wait(tag, names, state, after):
    kinds = [KIND[nm] for nm in names]
    n = len(names)
    send_sem, recv_sem, srcs, lands = state

    def body(*refs):
        g, got = refs[:n], refs[n:2 * n]
        for cp in _pair_copies(kinds, g, got, refs[2 * n], refs[2 * n + 1]):
            cp.wait_send()
            cp.wait_recv()

    outs = pl.pallas_call(
        body, name=f"grad_pair_exchange_wait_{tag}",
        in_specs=[HBM] * (2 * n) + [SEM, SEM, ANY], out_specs=[HBM] * (2 * n),
        out_shape=[pltpu.HBM(a.shape, a.dtype) for a in srcs + lands],
        input_output_aliases={i: i for i in range(2 * n)},
        compiler_params=pltpu.CompilerParams(has_side_effects=EFFECT),
    )(*srcs, *lands, send_sem, recv_sem, after)
    return dict(zip(names, outs[:n])), dict(zip(names, outs[n:]))


def _pair_sum(name, kind, view, got, c_arr):
    if kind == "col":
        _, rows, cols = view.shape
        tr = 128
        grid = (rows // tr,)
        v_spec = pl.BlockSpec((None, tr, cols), lambda i, c: (c[0], i, 0))
        g_spec = pl.BlockSpec((tr, cols), lambda i, c: (i, 0))
    elif kind == "row":
        _, _, rows, cols = view.shape
        grid = (N_CHIPS,)
        v_spec = pl.BlockSpec((None, None, rows, cols), lambda i, c: (i, c[0], 0, 0))
        g_spec = pl.BlockSpec((None, rows, cols), lambda i, c: (i, 0, 0))
    else:
        _, _, _, rows, cols = view.shape
        grid = (GROUPS,)
        v_spec = pl.BlockSpec((None, N_CHIPS, None, rows, cols), lambda i, c: (i, 0, c[0], 0, 0))
        g_spec = pl.BlockSpec((None, N_CHIPS, rows, cols), lambda i, c: (i, 0, 0, 0))

    def body(c_ref, v_ref, g_ref, o_ref):
        o_ref[...] = (v_ref[...].astype(F32) + g_ref[...].astype(F32)).astype(BF16)

    return pl.pallas_call(
        body, name=name,
        grid_spec=pltpu.PrefetchScalarGridSpec(num_scalar_prefetch=1, grid=grid, in_specs=[v_spec, g_spec], out_specs=g_spec),
        out_shape=jax.ShapeDtypeStruct(got.shape, BF16),
        compiler_params=_cparams(("parallel",)),
    )(c_arr, view, got)


def _piece(kind, ref, s):
    if kind == "col":
        cs = ref.shape[1] // N_CHIPS
        return ref.at[:, pl.ds(pl.multiple_of(s * cs, 128), cs)]
    if kind == "row":
        return ref.at[s]
    return ref.at[:, s]


def _piece_shape(kind, shape):
    if kind == "col":
        return (shape[0], shape[1] // N_CHIPS)
    if kind == "row":
        return shape[1:]
    return (shape[0],) + shape[2:]


def _shard_copies(kinds, p, got, send_sems, recv_sems):
    x, y, c, chips = _place()
    return [pltpu.make_async_remote_copy(src_ref=_piece(kinds[a], p[a], 2 * px + py), dst_ref=got[a].at[j],
                                         send_sem=send_sems.at[3 * a + j], recv_sem=recv_sems.at[3 * a + j],
                                         device_id=(px, py, c), device_id_type=MESH)
            for a in range(len(kinds)) for j, (px, py) in enumerate(chips)]


def _shard_exchange_start(g, names, psums):
    kinds = [KIND[nm] for nm in names]
    n = len(names)
    srcs = [_hbm(psums[nm]) for nm in names]
    lands = [_hbm(lax.empty((3,) + _piece_shape(k, psums[nm].shape), BF16)) for nm, k in zip(names, kinds)]

    def body(*refs):
        p, got = refs[:n], refs[n:2 * n]
        send_sems, recv_sems = refs[2 * n], refs[2 * n + 1]
        token = refs[-1]
        for cp in _shard_copies(kinds, p, got, send_sems, recv_sems):
            cp.start()
        token[...] = jnp.zeros(token.shape, F32)

    sem_shape = pltpu.SemaphoreType.DMA((3 * n,))
    outs = pl.pallas_call(
        body, name=f"grad_shard_exchange_start_{g}",
        in_specs=[HBM] * (2 * n),
        out_specs=[SEM, SEM] + [HBM] * (2 * n) + [pl.BlockSpec(memory_space=pltpu.VMEM)],
        out_shape=[sem_shape, sem_shape] + [pltpu.HBM(a.shape, a.dtype) for a in srcs + lands] + [jax.ShapeDtypeStruct((8, 128), F32)],
        input_output_aliases={i: 2 + i for i in range(2 * n)},
        compiler_params=pltpu.CompilerParams(has_side_effects=EFFECT),
    )(*srcs, *lands)
    return (outs[0], outs[1], list(outs[2:2 + n]), list(outs[2 + n:2 + 2 * n])), outs[-1]


def _shard_exchange_wait(g, names, state, after):
    kinds = [KIND[nm] for nm in names]
    n = len(names)
    send_sem, recv_sem, srcs, lands = state

    def body(*refs):
        p, got = refs[:n], refs[n:2 * n]
        for cp in _shard_copies(kinds, p, got, refs[2 * n], refs[2 * n + 1]):
            cp.wait_send()
            cp.wait_recv()

    outs = pl.pallas_call(
        body, name=f"grad_shard_exchange_wait_{g}",
        in_specs=[HBM] * (2 * n) + [SEM, SEM] + [ANY] * len(after), out_specs=[HBM] * (2 * n),
        out_shape=[pltpu.HBM(a.shape, a.dtype) for a in srcs + lands],
        input_output_aliases={i: i for i in range(2 * n)},
        compiler_params=pltpu.CompilerParams(has_side_effects=EFFECT),
    )(*srcs, *lands, send_sem, recv_sem, *after)
    return dict(zip(names, outs[:n])), dict(zip(names, outs[n:]))


def _shard_sum(name, kind, psum, got, sc_arr):
    if kind == "col":
        rows, cols = psum.shape
        cs = cols // N_CHIPS
        tr = 128
        grid = (rows // tr,)
        p_spec = pl.BlockSpec((tr, cs), lambda i, sc: (i, sc[0]))
        g_spec = pl.BlockSpec((3, tr, cs), lambda i, sc: (0, i, 0))
        o_spec = pl.BlockSpec((None, tr, cs), lambda i, sc: (sc[1], i, 0))
        out_shape = (2, rows, cs)
    elif kind == "row":
        _, rows, cols = psum.shape
        grid = (1,)
        p_spec = pl.BlockSpec((None, rows, cols), lambda i, sc: (sc[0], 0, 0))
        g_spec = pl.BlockSpec((3, rows, cols), lambda i, sc: (0, 0, 0))
        o_spec = pl.BlockSpec((None, rows, cols), lambda i, sc: (sc[1], 0, 0))
        out_shape = (2, rows, cols)
    else:
        _, _, rows, cols = psum.shape
        grid = (1,)
        p_spec = pl.BlockSpec((GROUPS, None, rows, cols), lambda i, sc: (0, sc[0], 0, 0))
        g_spec = pl.BlockSpec((3, GROUPS, rows, cols), lambda i, sc: (0, 0, 0, 0))
        o_spec = pl.BlockSpec((GROUPS, None, rows, cols), lambda i, sc: (0, sc[1], 0, 0))
        out_shape = (GROUPS, 2, rows, cols)

    def body(sc_ref, p_ref, g_ref, o_ref):
        o_ref[...] = ((p_ref[...].astype(F32) + g_ref[0].astype(F32)) + g_ref[1].astype(F32)) + g_ref[2].astype(F32)

    return pl.pallas_call(
        body, name=name,
        grid_spec=pltpu.PrefetchScalarGridSpec(num_scalar_prefetch=1, grid=grid, in_specs=[p_spec, g_spec], out_specs=o_spec),
        out_shape=jax.ShapeDtypeStruct(out_shape, F32),
        compiler_params=_cparams(("parallel",)),
    )(sc_arr, psum, got)


def _half_copies(kinds, bufs, send_sems, recv_sems):
    x, y, c, _ = _place()

    def remote(a, half):
        part = _half(kinds[a], bufs[a], half)
        return pltpu.make_async_remote_copy(src_ref=part, dst_ref=part, send_sem=send_sems.at[a], recv_sem=recv_sems.at[a],
                                            device_id=(x, y, 1 - c), device_id_type=MESH)

    return [remote(a, c) for a in range(len(kinds))], [remote(a, 1 - c) for a in range(len(kinds))]


def _half_exchange_start(tag, names, bufs):
    kinds = [KIND[nm] for nm in names]
    n = len(names)
    arrs = [_hbm(bufs[nm]) for nm in names]

    def body(*refs):
        for cp in _half_copies(kinds, refs[:n], refs[n], refs[n + 1])[0]:
            cp.start()
        refs[-1][...] = jnp.zeros(refs[-1].shape, F32)

    sem_shape = pltpu.SemaphoreType.DMA((n,))
    outs = pl.pallas_call(
        body, name=f"grad_half_exchange_start_{tag}",
        in_specs=[HBM] * n,
        out_specs=[SEM, SEM] + [HBM] * n + [pl.BlockSpec(memory_space=pltpu.VMEM)],
        out_shape=[sem_shape, sem_shape] + [pltpu.HBM(a.shape, a.dtype) for a in arrs] + [jax.ShapeDtypeStruct((8, 128), F32)],
        input_output_aliases={i: 2 + i for i in range(n)},
        compiler_params=pltpu.CompilerParams(has_side_effects=EFFECT),
    )(*arrs)
    return (outs[0], outs[1], list(outs[2:2 + n])), outs[-1]


def _half_exchange_wait(tag, names, state, after):
    kinds = [KIND[nm] for nm in names]
    n = len(names)
    send_sem, recv_sem, arrs = state

    def body(*refs):
        sends, arrivals = _half_copies(kinds, refs[:n], refs[n], refs[n + 1])
        for cp in sends:
            cp.wait_send()
        for cp in arrivals:
            cp.wait_recv()

    outs = pl.pallas_call(
        body, name=f"grad_half_exchange_wait_{tag}",
        in_specs=[HBM] * n + [SEM, SEM] + [ANY] * len(after), out_specs=[HBM] * n,
        out_shape=[pltpu.HBM(a.shape, a.dtype) for a in arrs],
        input_output_aliases={i: i for i in range(n)},
        compiler_params=pltpu.CompilerParams(has_side_effects=EFFECT),
    )(*arrs, send_sem, recv_sem, *after)
    return dict(zip(names, outs))


N_DEV = 8
SMALL_ROWS = 8


def _all_reduce_small(name, v):
    def body(v_ref, o_ref, token, buf, send_sems, recv_sems):
        token[...] = jnp.zeros(token.shape, F32)
        x, y, c, _ = _place()
        me = 4 * x + 2 * y + c
        buf[me] = v_ref[...]
        cps = []
        for r in range(1, N_DEV):
            to = (x ^ (r >> 2), y ^ ((r >> 1) & 1), c ^ (r & 1))
            cp = pltpu.make_async_remote_copy(src_ref=v_ref, dst_ref=buf.at[me], send_sem=send_sems.at[r - 1],
                                              recv_sem=recv_sems.at[r - 1], device_id=to, device_id_type=MESH)
            cp.start()
            cps.append(cp)
        for r in range(1, N_DEV):
            pltpu.make_async_remote_copy(src_ref=v_ref, dst_ref=buf.at[me ^ r], send_sem=send_sems.at[r - 1],
                                         recv_sem=recv_sems.at[r - 1], device_id=(x, y, c), device_id_type=MESH).wait_recv()
        for cp in cps:
            cp.wait_send()
        acc = buf[0]
        for d in range(1, N_DEV):
            acc = acc + buf[d]
        o_ref[...] = acc

    vm = pl.BlockSpec(memory_space=pltpu.VMEM)
    return pl.pallas_call(
        body, name=name, in_specs=[vm], out_specs=[vm, vm],
        out_shape=[jax.ShapeDtypeStruct((SMALL_ROWS, D_MODEL), F32), jax.ShapeDtypeStruct((8, 128), F32)],
        scratch_shapes=[pltpu.VMEM((N_DEV, SMALL_ROWS, D_MODEL), F32), pltpu.SemaphoreType.DMA((N_DEV - 1,)),
                        pltpu.SemaphoreType.DMA((N_DEV - 1,))],
    )(v)


def _adamw(name, w, g, m, v, with_grad=False):
    rows, cols = w.shape
    tr = next((c for c in (256, 176, 128, 64, 32, 8) if rows % c == 0), rows)
    spec = pl.BlockSpec((tr, cols), lambda i: (i, 0))

    def body(w_ref, g_ref, m_ref, v_ref, d_ref, mo_ref, vo_ref, *go_ref):
        gv = g_ref[...]
        if with_grad:
            go_ref[0][...] = gv
        m_new = ADAM_B1 * m_ref[...] + (1.0 - ADAM_B1) * gv
        v_new = ADAM_B2 * v_ref[...] + (1.0 - ADAM_B2) * jnp.square(gv)
        m_hat = m_new / (1.0 - ADAM_B1 ** ADAM_STEP)
        v_hat = v_new / (1.0 - ADAM_B2 ** ADAM_STEP)
        d_ref[...] = -ADAM_LR * (m_hat / (jnp.sqrt(v_hat) + ADAM_EPS) + ADAM_WD * w_ref[...])
        mo_ref[...] = m_new
        vo_ref[...] = v_new

    return pl.pallas_call(
        body, name=name, grid=(rows // tr,),
        in_specs=[spec] * 4, out_specs=[spec] * (4 if with_grad else 3),
        out_shape=[jax.ShapeDtypeStruct((rows, cols), F32)] * (4 if with_grad else 3),
        compiler_params=_cparams(("parallel",)),
    )(w, g, m, v)


WEIGHTS = ("norm_ffn1", "ffn1_w_in", "ffn1_w_out", "norm_mix", "w_in", "gate_bias", "pool_w", "pool_scale", "w_ret_up",
           "w_pool_up", "w_out", "norm_ffn2", "ffn2_w_in", "ffn2_w_out", "norm_final")
SMALL_ROW = dict(norm_ffn1=0, norm_mix=1, gate_bias=2, pool_scale=4, norm_ffn2=5, norm_final=6)


def _as2d(a):
    return a.reshape(-1, a.shape[-1])


def kernel(x, norm_ffn1, ffn1_w_in, ffn1_w_out, norm_mix, w_in, gate_bias, pool_w, pool_scale, w_ret_up, w_pool_up, w_out, norm_ffn2, ffn2_w_in, ffn2_w_out, norm_final, loss_target, m_norm_ffn1, m_ffn1_w_in, m_ffn1_w_out, m_norm_mix, m_w_in, m_gate_bias, m_pool_w, m_pool_scale, m_w_ret_up, m_w_pool_up, m_w_out, m_norm_ffn2, m_ffn2_w_in, m_ffn2_w_out, m_norm_final, v_norm_ffn1, v_ffn1_w_in, v_ffn1_w_out, v_norm_mix, v_w_in, v_gate_bias, v_pool_w, v_pool_scale, v_w_ret_up, v_w_pool_up, v_w_out, v_norm_ffn2, v_ffn2_w_in, v_ffn2_w_out, v_norm_final):
    wt = dict(norm_ffn1=norm_ffn1, ffn1_w_in=ffn1_w_in, ffn1_w_out=ffn1_w_out, norm_mix=norm_mix, w_in=w_in, gate_bias=gate_bias,
              pool_w=pool_w, pool_scale=pool_scale, w_ret_up=w_ret_up, w_pool_up=w_pool_up, w_out=w_out, norm_ffn2=norm_ffn2,
              ffn2_w_in=ffn2_w_in, ffn2_w_out=ffn2_w_out, norm_final=norm_final)
    mom = dict(norm_ffn1=m_norm_ffn1, ffn1_w_in=m_ffn1_w_in, ffn1_w_out=m_ffn1_w_out, norm_mix=m_norm_mix, w_in=m_w_in,
               gate_bias=m_gate_bias, pool_w=m_pool_w, pool_scale=m_pool_scale, w_ret_up=m_w_ret_up, w_pool_up=m_w_pool_up,
               w_out=m_w_out, norm_ffn2=m_norm_ffn2, ffn2_w_in=m_ffn2_w_in, ffn2_w_out=m_ffn2_w_out, norm_final=m_norm_final)
    var = dict(norm_ffn1=v_norm_ffn1, ffn1_w_in=v_ffn1_w_in, ffn1_w_out=v_ffn1_w_out, norm_mix=v_norm_mix, w_in=v_w_in,
               gate_bias=v_gate_bias, pool_w=v_pool_w, pool_scale=v_pool_scale, w_ret_up=v_w_ret_up, w_pool_up=v_w_pool_up,
               w_out=v_w_out, norm_ffn2=v_norm_ffn2, ffn2_w_in=v_ffn2_w_in, ffn2_w_out=v_ffn2_w_out, norm_final=v_norm_final)

    ax, ay, ac = lax.axis_index("x"), lax.axis_index("y"), lax.axis_index("c")
    chip = 2 * ax + ay
    c_arr = jnp.reshape(ac, (1,)).astype(jnp.int32)
    sc_arr = jnp.stack([chip, ac]).astype(jnp.int32)
    bias_cols = gate_bias.shape[-1]

    first = {"gate_bias": gate_bias[0], "ffn1_w_in": ffn1_w_in[0].astype(BF16)}
    gather_groups, token = _gather_start("first", [0], first)
    rest, rest_token = _gather_start("rest", [1, 2, 3],
                                     {nm: wt[nm][0].astype(BF16) + token[0, 0].astype(BF16) for nm in BIG if nm not in first})
    gather_groups.update(rest)
    vec = dict(norm_ffn1=norm_ffn1, norm_mix=norm_mix, norm_ffn2=norm_ffn2, pool_scale=pool_scale,
               norm_final=norm_final.reshape(1, D_MODEL))

    relayed = {}

    def relay_w(g, after):
        relayed[g] = _gather_relay(g, gather_groups[g], (after,))

    def get_w(g, after):
        if g in relayed:
            return _gather_land(g, relayed[g], (after,))
        return _gather_finish(g, gather_groups[g], (after, rest_token) if g == 0 else (after,))

    pairs, pending = [], []

    def on_grads(gr):
        g = len(pairs)
        names = GRAD_GROUPS[g]
        assert set(names) == set(gr), (names, list(gr))
        state, token = _pair_exchange_start(g, names, {nm: _grad_view(KIND[nm], gr[nm]) for nm in names})
        pairs.append(state)
        return token[0:1, 0:1]

    def flush(after):
        g = len(pending)
        names = GRAD_GROUPS[g]
        views, from_sib = _pair_exchange_wait(g, names, pairs[g], after)
        psums = {nm: _pair_sum(f"pair_sum_{nm}", KIND[nm], views[nm], from_sib[nm], c_arr) for nm in names}
        state, token = _shard_exchange_start(g, names, psums)
        pending.append(state)
        tokens.append(token)
        return token[0:1, 0:1]

    tokens = []
    loss_local, dx, small = _local_step(x[0], loss_target[0], vec, get_w, relay_w, on_grads, flush)

    grads, delta, new_m, new_v = {}, {}, {}, {}

    def adamw(nm):
        shape = wt[nm].shape
        outs = _adamw(f"adamw_{nm}", _as2d(wt[nm]), _as2d(grads[nm]), _as2d(mom[nm]), _as2d(var[nm]), with_grad=nm in BIG)
        delta[nm], new_m[nm], new_v[nm] = (o.reshape(shape) for o in outs[:3])
        if nm in BIG:
            grads[nm] = outs[3].reshape(shape)
        return outs[0]

    def reduce_start(g, after):
        names = GRAD_GROUPS[g]
        psums, from_chips = _shard_exchange_wait(g, names, pending[g], after)
        bufs = {nm: _shard_sum(f"shard_sum_{nm}", KIND[nm], psums[nm], from_chips[nm], sc_arr) for nm in names}
        return _half_exchange_start(g, names, bufs)

    def reduce_finish(g, state, after):
        names = GRAD_GROUPS[g]
        reduced = _half_exchange_wait(g, names, state, after)
        for nm in names:
            grads[nm] = reduced[nm].reshape(wt[nm].shape)
        return tuple(adamw(nm) for nm in names)

    swap0, token = reduce_start(0, (tokens[-1],))
    swap1, token = reduce_start(1, (token,))
    done = reduce_finish(0, swap0, (token,))
    done = reduce_finish(1, swap1, done)
    swap2, token = reduce_start(2, done)
    packed = jnp.concatenate([small["norm_ffn1"], small["norm_mix"], small["gate_bias"], small["pool_scale"],
                              small["norm_ffn2"], small["norm_final"], jnp.broadcast_to(loss_local, (1, D_MODEL))], axis=0)
    small_sum, _ = _all_reduce_small("reduce_small_grads", packed + token[0, 0])
    loss = small_sum[SMALL_ROWS - 1, 0]
    for nm in ("norm_ffn1", "norm_mix", "pool_scale", "norm_ffn2"):
        grads[nm] = small_sum[SMALL_ROW[nm]][None, :]
    grads["norm_final"] = small_sum[SMALL_ROW["norm_final"]]
    grads["gate_bias"] = lax.dynamic_slice(small_sum, (SMALL_ROW["gate_bias"], chip * bias_cols), (2, bias_cols))[None]
    reduce_finish(2, swap2, (small_sum,))
    for nm in WEIGHTS:
        if nm not in delta:
            adamw(nm)

    return (loss, dx[None], *[grads[nm] for nm in WEIGHTS], *[delta[nm] for nm in WEIGHTS],
            *[new_m[nm] for nm in WEIGHTS], *[new_v[nm] for nm in WEIGHTS])
```

```python
import numpy as np
import jax
import jax.numpy as jnp
from jax import lax
from jax.experimental import pallas as pl
from jax.experimental.pallas import tpu as pltpu

F32 = jnp.float32
BF16 = jnp.bfloat16
MESH = pl.DeviceIdType.MESH

D_MODEL = 1024
D_FF = 2816
HEADS = 4
HEAD_DIM = 256
GROUPS = 4
GROUP_DIM = 256
POOL_WINDOWS = (2, 4, 8, 16)
IN_WIDTH = 7 * D_MODEL
ROPE_BASE = 10000.0
NORM_EPS = 1e-6
FFN_RES_WEIGHT = 0.5
ADAM_LR, ADAM_B1, ADAM_B2, ADAM_EPS, ADAM_WD, ADAM_STEP = 0.001, 0.9, 0.999, 1e-08, 0.01, 10

N_CHIPS = 4
RET_BLOCK = 256
V7X_VMEM_LIMIT = 48 * 1024 * 1024


def _cparams(sem):
    return pltpu.CompilerParams(dimension_semantics=sem, vmem_limit_bytes=V7X_VMEM_LIMIT)


def _sigmoid(x):
    return jax.nn.sigmoid(x)


_DIMS = {"nn": (((1,), (0,)), ((), ())), "nt": (((1,), (1,)), ((), ())), "tn": (((0,), (0,)), ((), ()))}


def _matmul(name, a, b, mode, m, n, k, tm, tn, tk, out_dtypes, extras=(), consts=(), epilogue=None, resident=None,
            n_outer=False):
    tm, tn, tk = min(tm, m), min(tn, n), min(tk, k)
    gi, gj, gk = m // tm, n // tn, k // tk
    assert gi * tm == m and gj * tn == n and gk * tk == k, (name, m, n, k, tm, tn, tk)
    once = dict(pipeline_mode=pl.Buffered(1))

    def spec(shape, index, **kw):
        return pl.BlockSpec(shape, (lambda j, i, kk: index(i, j, kk)) if n_outer else index, **kw)

    kw = once if resident == "a" else {}
    a_spec = (spec((tk, tm), lambda i, j, kk: (kk, i), **kw) if mode == "tn" else spec((tm, tk), lambda i, j, kk: (i, kk), **kw))
    kw = once if resident == "b" else {}
    b_spec = (spec((tn, tk), lambda i, j, kk: (j, kk), **kw) if mode == "nt" else spec((tk, tn), lambda i, j, kk: (kk, j), **kw))
    n_ex, n_out = len(extras) + len(consts), len(out_dtypes)
    dims = _DIMS[mode]

    def body(a_ref, b_ref, *rest):
        ex_refs, out_refs = rest[:n_ex], rest[n_ex:n_ex + n_out]

        def finish(acc):
            outs = (acc,) if epilogue is None else epilogue(acc, *[e[...] for e in ex_refs])
            for o_ref, o in zip(out_refs, outs):
                o_ref[...] = o.astype(o_ref.dtype)

        prod = lax.dot_general(a_ref[...], b_ref[...], dims, preferred_element_type=F32)
        if gk == 1:
            finish(prod)
        else:
            acc_ref = rest[n_ex + n_out]
            kk = pl.program_id(2)

            @pl.when(kk == 0)
            def _():
                acc_ref[...] = prod

            @pl.when(kk > 0)
            def _():
                acc_ref[...] += prod

            @pl.when(kk == gk - 1)
            def _():
                finish(acc_ref[...])

    o_spec = spec((tm, tn), lambda i, j, kk: (i, j))
    outs = pl.pallas_call(
        body, name=name, grid=(gj, gi, gk) if n_outer else (gi, gj, gk),
        in_specs=[a_spec, b_spec] + [o_spec] * len(extras) + [spec((1, tn), lambda i, j, kk: (0, j))] * len(consts),
        out_specs=[o_spec] * n_out,
        out_shape=[jax.ShapeDtypeStruct((m, n), dt) for dt in out_dtypes],
        scratch_shapes=[pltpu.VMEM((tm, tn), F32)] if gk > 1 else [],
        compiler_params=_cparams(("parallel", "parallel", "arbitrary")),
    )(a, b, *extras, *consts)
    return outs[0] if n_out == 1 else outs


def _row_spec(tm, width, col_block=0):
    return pl.BlockSpec((tm, width), lambda i: (i, col_block))


def _full_spec(shape):
    return pl.BlockSpec(shape, lambda *_: (0,) * len(shape))


def _rmsnorm_fwd(name, h, g, tm=512):
    t = h.shape[0]

    def body(h_ref, g_ref, o_ref):
        x = h_ref[...]
        r = lax.rsqrt(jnp.mean(x * x, axis=-1, keepdims=True) + NORM_EPS)
        o_ref[...] = (x * r * g_ref[...]).astype(BF16)

    return pl.pallas_call(
        body, name=name, grid=(t // tm,),
        in_specs=[_row_spec(tm, D_MODEL), _full_spec((1, D_MODEL))],
        out_specs=_row_spec(tm, D_MODEL),
        out_shape=jax.ShapeDtypeStruct((t, D_MODEL), BF16),
        compiler_params=_cparams(("parallel",)),
    )(h, g)


def _proj_norm_bwd(name, a_list, a_specs, parts, w, h, g, dres, tm):
    t = h.shape[0]
    na = len(a_list)

    def body(*refs):
        a_refs = refs[:na]
        w_ref, h_ref, g_ref, dres_ref, dh_ref, dhb_ref, dg_ref = refs[na:]
        i = pl.program_id(0)
        dn_v = None
        for which, lead, k0, k1 in parts:
            a_ref = a_refs[which]
            term = _dot(a_ref[...] if lead is None else a_ref[lead], w_ref[:, k0:k1], "nt")
            dn_v = term if dn_v is None else dn_v + term
        x = h_ref[...]
        r = lax.rsqrt(jnp.mean(x * x, axis=-1, keepdims=True) + NORM_EPS)
        xh = x * r
        dxh = dn_v * g_ref[...]
        dh = dres_ref[...] + r * (dxh - xh * jnp.mean(dxh * xh, axis=-1, keepdims=True))
        dh_ref[...] = dh
        dhb_ref[...] = dh.astype(BF16)
        part = jnp.sum(dn_v * xh, axis=0, keepdims=True)

        @pl.when(i == 0)
        def _():
            dg_ref[...] = part

        @pl.when(i > 0)
        def _():
            dg_ref[...] += part

    row = _row_spec(tm, D_MODEL)
    return pl.pallas_call(
        body, name=name, grid=(t // tm,),
        in_specs=list(a_specs) + [pl.BlockSpec(w.shape, lambda i: (0, 0), pipeline_mode=pl.Buffered(1)), row,
                                  _full_spec((1, D_MODEL)), row],
        out_specs=[row, row, _full_spec((1, D_MODEL))],
        out_shape=[jax.ShapeDtypeStruct((t, D_MODEL), F32), jax.ShapeDtypeStruct((t, D_MODEL), BF16),
                   jax.ShapeDtypeStruct((1, D_MODEL), F32)],
        compiler_params=_cparams(("arbitrary",)),
    )(*a_list, w, h, g, dres)


def _out_loss_and_grad(name, mid, w_out, h, g, target, tm=512):
    t = h.shape[0]

    def body(m_ref, w_ref, h_ref, g_ref, t_ref, dh_ref, dhb_ref, dg_ref, loss_ref):
        i = pl.program_id(0)
        x = h_ref[...] + FFN_RES_WEIGHT * _dot(m_ref[...], w_ref[...])
        gv = g_ref[...]
        r = lax.rsqrt(jnp.mean(x * x, axis=-1, keepdims=True) + NORM_EPS)
        xh = x * r
        err = xh * gv - t_ref[...]
        row = jnp.mean(err * err, axis=-1, keepdims=True)
        part_loss = 0.5 * jnp.sum(row, axis=0, keepdims=True)
        dy = err * (1.0 / D_MODEL)
        dxh = dy * gv
        dh = r * (dxh - xh * jnp.mean(dxh * xh, axis=-1, keepdims=True))
        dh_ref[...] = dh
        dhb_ref[...] = dh.astype(BF16)
        part = jnp.sum(dy * xh, axis=0, keepdims=True)

        @pl.when(i == 0)
        def _():
            dg_ref[...] = part
            loss_ref[...] = jnp.zeros(loss_ref.shape, F32) + part_loss

        @pl.when(i > 0)
        def _():
            dg_ref[...] += part
            loss_ref[...] += part_loss

    return pl.pallas_call(
        body, name=name, grid=(t // tm,),
        in_specs=[_row_spec(tm, D_FF), pl.BlockSpec((D_FF, D_MODEL), lambda i: (0, 0), pipeline_mode=pl.Buffered(1)),
                  _row_spec(tm, D_MODEL), _full_spec((1, D_MODEL)), _row_spec(tm, D_MODEL)],
        out_specs=[_row_spec(tm, D_MODEL), _row_spec(tm, D_MODEL), _full_spec((1, D_MODEL)), _full_spec((8, 128))],
        out_shape=[jax.ShapeDtypeStruct((t, D_MODEL), F32), jax.ShapeDtypeStruct((t, D_MODEL), BF16),
                   jax.ShapeDtypeStruct((1, D_MODEL), F32), jax.ShapeDtypeStruct((8, 128), F32)],
        compiler_params=_cparams(("arbitrary",)),
    )(mid, w_out, h, g, target)


def _rope_tables(t):
    half = HEAD_DIM // 2
    inv_freq = np.float32(ROPE_BASE) ** (-np.arange(half, dtype=np.float32) / np.float32(half))
    ang = (np.arange(t, dtype=np.float32)[:, None] * inv_freq[None, :].astype(np.float32)).astype(np.float32)
    return jnp.asarray(np.cos(ang.astype(np.float64)).astype(np.float32)), jnp.asarray(np.sin(ang.astype(np.float64)).astype(np.float32))


ROPE_HALF = HEAD_DIM // 2
K_SCALE = HEAD_DIM ** -0.5


def _rotate(ref, rows, hh, c, s, scale=None):
    lo, mid, hi = hh * HEAD_DIM, hh * HEAD_DIM + ROPE_HALF, (hh + 1) * HEAD_DIM
    x1, x2 = ref[rows, lo:mid].astype(F32), ref[rows, mid:hi].astype(F32)
    y = jnp.concatenate([x1 * c - x2 * s, x1 * s + x2 * c], axis=1)
    return y if scale is None else y * scale


def _unrotate_into(ref, rows, hh, dy, c, s, scale=None):
    lo, mid, hi = hh * HEAD_DIM, hh * HEAD_DIM + ROPE_HALF, (hh + 1) * HEAD_DIM
    y1, y2 = dy[:, :ROPE_HALF], dy[:, ROPE_HALF:]
    d1, d2 = y1 * c + y2 * s, y2 * c - y1 * s
    if scale is not None:
        d1, d2 = d1 * scale, d2 * scale
    ref[rows, lo:mid] = d1.astype(ref.dtype)
    ref[rows, mid:hi] = d2.astype(ref.dtype)


def _retention_tables():
    b, chunk = RET_BLOCK, 64
    gamma = 1.0 - 2.0 ** (-5.0 - np.arange(HEADS, dtype=np.float64))
    log_g = np.log(gamma)[:, None, None]
    i = np.arange(b)[:, None]
    j = np.arange(b)[None, :]
    same = (i // chunk) == (j // chunk)
    earlier = (j // chunk) < (i // chunk)
    expo = np.where(same, np.abs(i - j), np.where(earlier, i - j, 0)).astype(np.float64)
    mask = np.where(same | earlier, 1.0, 0.0)
    dmat = np.exp(log_g * expo[None]) * mask[None]
    qd = np.exp(log_g[:, :, 0] * (np.arange(b)[None, :] + 1.0))
    kd = np.exp(log_g[:, :, 0] * (b - 1.0 - np.arange(b)[None, :]))
    cd = np.exp(log_g[:, :, 0] * b) * np.ones((1, HEAD_DIM))
    as32 = lambda v: jnp.asarray(v.astype(np.float32))
    return (as32(dmat), as32(np.swapaxes(dmat, 1, 2)), as32(qd[:, :, None]), as32(kd[:, :, None]), as32(cd[:, None, :]))


def _dot(a, b, mode="nn"):
    return lax.dot_general(a, b, _DIMS[mode], preferred_element_type=F32)


GRET_BLOCK = 3


RET_SUBS = 2
RET_STEP = RET_SUBS * RET_BLOCK


def _head_specs(steps, rev=False):
    pos = (lambda n: steps - 1 - n) if rev else (lambda n: n)
    tok = pl.BlockSpec((RET_STEP, D_MODEL), lambda n: (pos(n), 0))
    blk = [pl.BlockSpec((RET_STEP, D_MODEL), lambda n, b=b: (pos(n), b)) for b in range(GRET_BLOCK + 1)]
    rope = pl.BlockSpec((RET_STEP, ROPE_HALF), lambda n: (pos(n), 0))
    tab = _full_spec((HEADS, RET_BLOCK, RET_BLOCK))
    col = _full_spec((HEADS, RET_BLOCK, 1))
    rowv = _full_spec((HEADS, 1, HEAD_DIM))
    st = pl.BlockSpec((HEADS, RET_SUBS, HEAD_DIM, HEAD_DIM), lambda n: (0, pos(n), 0, 0))
    return tok, blk, rope, tab, col, rowv, st


def _retention_fwd(name, proj, cos, sin, tables):
    t = proj.shape[0]
    nb, steps = t // RET_BLOCK, t // RET_STEP
    dmat, _, qd, kd, cd = tables
    tok, blk, rope, tab, col, rowv, st = _head_specs(steps)

    def body(q_ref, k_ref, v_ref, g_ref, c_ref, s_ref, d_ref, qd_ref, kd_ref, cd_ref, o_ref, ret_ref, st_ref, state):
        n = pl.program_id(0)

        @pl.when(n == 0)
        def _():
            state[...] = jnp.zeros(state.shape, F32)

        for sub in range(RET_SUBS):
            rows = slice(sub * RET_BLOCK, (sub + 1) * RET_BLOCK)
            cs, sn = c_ref[rows, :], s_ref[rows, :]
            for hh in range(HEADS):
                sl = slice(hh * HEAD_DIM, (hh + 1) * HEAD_DIM)
                q, k = _rotate(q_ref, rows, hh, cs, sn), _rotate(k_ref, rows, hh, cs, sn, K_SCALE)
                v = v_ref[rows, sl].astype(BF16)
                s = _dot(q.astype(BF16), k.astype(BF16), "nt") * d_ref[hh]
                stb = state[hh].astype(BF16)
                st_ref[hh, sub] = stb
                o = _dot(s.astype(BF16), v) + _dot((q * qd_ref[hh]).astype(BF16), stb)
                o_ref[rows, sl] = o
                rn = o * lax.rsqrt(jnp.mean(o * o, axis=-1, keepdims=True) + NORM_EPS)
                g = g_ref[rows, sl].astype(F32)
                ret_ref[rows, sl] = (rn * (g * _sigmoid(g))).astype(BF16)
                state[hh] = state[hh] * cd_ref[hh] + _dot((k * kd_ref[hh]).astype(BF16), v, "tn")

    return pl.pallas_call(
        body, name=name, grid=(steps,),
        in_specs=blk + [rope, rope, tab, col, col, rowv],
        out_specs=[tok, tok, st],
        out_shape=[jax.ShapeDtypeStruct((t, D_MODEL), F32), jax.ShapeDtypeStruct((t, D_MODEL), BF16),
                   jax.ShapeDtypeStruct((HEADS, nb, HEAD_DIM, HEAD_DIM), BF16)],
        scratch_shapes=[pltpu.VMEM((HEADS, HEAD_DIM, HEAD_DIM), F32)],
        compiler_params=_cparams(("arbitrary",)),
    )(proj, proj, proj, proj, cos, sin, dmat, qd, kd, cd)


def _retention_bwd(name, dru, w_ru, o, proj, cos, sin, states, tables):
    t = proj.shape[0]
    steps = t // RET_STEP
    dmat, dmat_t, qd, kd, cd = tables
    tok, blk, rope, tab, col, rowv, st = _head_specs(steps, rev=True)

    def body(dru_ref, wru_ref, o_ref, q_ref, k_ref, v_ref, g_ref, c_ref, s_ref, st_ref, d_ref, dt_ref, qd_ref, kd_ref, cd_ref,
             dqkvg_ref, gstate):
        n = pl.program_id(0)
        dq_ref, dk_ref, dv_ref, dg_ref = (dqkvg_ref.at[i] for i in range(4))

        @pl.when(n == 0)
        def _():
            gstate[...] = jnp.zeros(gstate.shape, F32)

        for sub in reversed(range(RET_SUBS)):
            rows = slice(sub * RET_BLOCK, (sub + 1) * RET_BLOCK)
            cs, sn = c_ref[rows, :], s_ref[rows, :]
            dret = _dot(dru_ref[rows, :], wru_ref[...], "nt")
            for hh in range(HEADS):
                sl = slice(hh * HEAD_DIM, (hh + 1) * HEAD_DIM)
                o_v, g, dr = o_ref[rows, sl], g_ref[rows, sl].astype(F32), dret[:, sl]
                sg = _sigmoid(g)
                r = lax.rsqrt(jnp.mean(o_v * o_v, axis=-1, keepdims=True) + NORM_EPS)
                rn = o_v * r
                d_rn = dr * (g * sg)
                dg_ref[rows, sl] = (dr * rn * (sg * (1.0 + g * (1.0 - sg)))).astype(BF16)
                d_o = r * (d_rn - rn * jnp.mean(d_rn * rn, axis=-1, keepdims=True))
                dob = d_o.astype(BF16)

                q, k = _rotate(q_ref, rows, hh, cs, sn), _rotate(k_ref, rows, hh, cs, sn, K_SCALE)
                v = v_ref[rows, sl].astype(BF16)
                qb, kb = q.astype(BF16), k.astype(BF16)
                qdv, kdv = qd_ref[hh], kd_ref[hh]
                s_t = (_dot(kb, qb, "nt") * dt_ref[hh]).astype(BF16)
                p_t = (_dot(v, dob, "nt") * dt_ref[hh]).astype(BF16)
                p = (_dot(dob, v, "nt") * d_ref[hh]).astype(BF16)
                stb = st_ref[hh, sub]
                gb = gstate[hh].astype(BF16)
                _unrotate_into(dq_ref, rows, hh, _dot(p, kb) + _dot(dob, stb, "nt") * qdv, cs, sn)
                _unrotate_into(dk_ref, rows, hh, _dot(p_t, qb) + _dot(v, gb, "nt") * kdv, cs, sn, K_SCALE)
                dv_ref[rows, sl] = (_dot(s_t, dob) + _dot((k * kdv).astype(BF16), gb)).astype(BF16)
                gstate[hh] = gstate[hh] * cd_ref[hh] + _dot((q * qdv).astype(BF16), dob, "tn")

    return pl.pallas_call(
        body, name=name, grid=(steps,),
        in_specs=[tok, pl.BlockSpec((D_MODEL, D_MODEL), lambda n: (0, 0), pipeline_mode=pl.Buffered(1)), tok] + blk
                 + [rope, rope, st, tab, tab, col, col, rowv],
        out_specs=pl.BlockSpec((4, RET_STEP, D_MODEL), lambda n: (0, steps - 1 - n, 0)),
        out_shape=jax.ShapeDtypeStruct((4, t, D_MODEL), BF16),
        scratch_shapes=[pltpu.VMEM((HEADS, HEAD_DIM, HEAD_DIM), F32)],
        compiler_params=_cparams(("arbitrary",)),
    )(dru, w_ru, o, proj, proj, proj, proj, cos, sin, states, dmat, dmat_t, qd, kd, cd)


POOL_TILE = 256
POOL_SUBS = 2
POOL_STEP = POOL_SUBS * POOL_TILE


def _pool_tables():
    b = POOL_TILE
    tt = np.arange(b)[:, None]
    jj = np.arange(b)[None, :]
    cur, prev = [], []
    for w in POOL_WINDOWS:
        cur.append(((tt - jj >= 0) & (tt - jj <= w - 1)).astype(np.float32))
        prev.append((tt - (jj - b) <= w - 1).astype(np.float32))
    cur, prev = np.stack(cur), np.stack(prev)
    as16 = lambda v: jnp.asarray(v, dtype=BF16)
    return as16(cur), as16(prev), as16(np.swapaxes(cur, 1, 2)), as16(np.swapaxes(prev, 1, 2))


def _split2(x):
    hi = x.astype(BF16)
    return hi, (x - hi.astype(F32)).astype(BF16)


POOL_BLOCK = 4


def _pool_count(n, window):
    tpos = n * POOL_TILE + lax.broadcasted_iota(jnp.int32, (POOL_TILE, 1), 0)
    return jnp.minimum(tpos + 1, window).astype(F32)


def _pool_fwd(name, proj, pool_w, scale, tables):
    t = proj.shape[0]
    steps = t // POOL_STEP
    mc, mp, _, _ = tables
    tab = _full_spec((GROUPS, POOL_TILE, POOL_TILE))
    row = _row_spec(POOL_STEP, D_MODEL)

    def body(pc_ref, pp_ref, mc_ref, mp_ref, w_ref, sc_ref, pm_ref, mix_ref, po_ref):
        n = pl.program_id(0)
        for sub in range(POOL_SUBS):
            rows = slice(sub * POOL_TILE, (sub + 1) * POOL_TILE)
            tile = n * POOL_SUBS + sub
            for g, window in enumerate(POOL_WINDOWS):
                sl = slice(g * GROUP_DIM, (g + 1) * GROUP_DIM)
                p = pc_ref[rows, sl]
                if sub == 0:
                    before = jnp.where(n > 0, _dot(mp_ref[g], pp_ref[:, sl]), 0.0)
                else:
                    before = _dot(mp_ref[g], pc_ref[(sub - 1) * POOL_TILE:sub * POOL_TILE, sl])
                pm = ((_dot(mc_ref[g], p) + before) / _pool_count(tile, window) - p.astype(F32)).astype(BF16)
                pm_ref[rows, sl] = pm
                mixed = _dot(pm, w_ref[g])
                mix_ref[rows, sl] = mixed
                po_ref[rows, sl] = (mixed * sc_ref[:, sl]).astype(BF16)

    return pl.pallas_call(
        body, name=name, grid=(steps,),
        in_specs=[_row_spec(POOL_STEP, D_MODEL, POOL_BLOCK),
                  pl.BlockSpec((POOL_TILE, D_MODEL), lambda n: (jnp.maximum(n * POOL_SUBS - 1, 0), POOL_BLOCK)),
                  tab, tab, _full_spec((GROUPS, GROUP_DIM, GROUP_DIM)), _full_spec((1, D_MODEL))],
        out_specs=[row] * 3,
        out_shape=[jax.ShapeDtypeStruct((t, D_MODEL), BF16), jax.ShapeDtypeStruct((t, D_MODEL), F32),
                   jax.ShapeDtypeStruct((t, D_MODEL), BF16)],
        compiler_params=_cparams(("parallel",)),
    )(proj, proj, mc, mp, pool_w, scale)


def _pool_bwd(name, dpu, w_pu, pm, mixed, pool_w, scale, tables):
    t = dpu.shape[0]
    steps = t // POOL_STEP
    _, _, mct, mpt = tables
    cur = pl.BlockSpec((POOL_STEP, D_MODEL), lambda n: (steps - 1 - n, 0))
    tab = _full_spec((GROUPS, POOL_TILE, POOL_TILE))
    wspec = _full_spec((GROUPS, GROUP_DIM, GROUP_DIM))
    sspec = _full_spec((1, D_MODEL))

    def body(dpu_ref, wpu_ref, pm_ref, mix_ref, mct_ref, mpt_ref, w_ref, sc_ref, dp_ref, dw_ref, ds_ref, later):
        n = pl.program_id(0)

        @pl.when(n == 0)
        def _():
            dw_ref[...] = jnp.zeros(dw_ref.shape, F32)
            ds_ref[...] = jnp.zeros(ds_ref.shape, F32)
            later[...] = jnp.zeros(later.shape, F32)

        for sub in reversed(range(POOL_SUBS)):
            rows = slice(sub * POOL_TILE, (sub + 1) * POOL_TILE)
            tile = (steps - 1 - n) * POOL_SUBS + sub
            dpo = _dot(dpu_ref[rows, :], wpu_ref[...], "nt")
            for g, window in enumerate(POOL_WINDOWS):
                sl = slice(g * GROUP_DIM, (g + 1) * GROUP_DIM)
                dc, sc = dpo[:, sl], sc_ref[:, sl]
                dmix = (dc * sc).astype(BF16)
                dpm = _dot(dmix, w_ref[g], "nt")
                e = dpm / _pool_count(tile, window)
                e_hi, e_lo = _split2(e)
                f_hi, f_lo = _split2(later[g])
                mctv, mptv = mct_ref[g], mpt_ref[g]
                back = _dot(mctv, e_hi) + _dot(mctv, e_lo)
                after = _dot(mptv, f_hi) + _dot(mptv, f_lo)
                dp_ref[rows, sl] = (back + after - dpm).astype(BF16)
                later[g] = e
                dw_ref[g] += _dot(pm_ref[rows, sl], dmix, "tn")
                ds_ref[:, sl] += jnp.sum(dc * mix_ref[rows, sl], axis=0, keepdims=True)

    return pl.pallas_call(
        body, name=name, grid=(steps,),
        in_specs=[cur, pl.BlockSpec((D_MODEL, D_MODEL), lambda n: (0, 0), pipeline_mode=pl.Buffered(1)), cur, cur, tab, tab,
                  wspec, sspec],
        out_specs=[cur, wspec, sspec],
        out_shape=[jax.ShapeDtypeStruct((t, D_MODEL), BF16), jax.ShapeDtypeStruct((GROUPS, GROUP_DIM, GROUP_DIM), F32),
                   jax.ShapeDtypeStruct((1, D_MODEL), F32)],
        scratch_shapes=[pltpu.VMEM((GROUPS, POOL_TILE, GROUP_DIM), F32)],
        compiler_params=_cparams(("arbitrary",)),
    )(dpu, w_pu, pm, mixed, mct, mpt, pool_w, scale)


GATE0_BLOCK, GATE1_BLOCK = 5, 6


def _merge_fwd(name, ret, po, w_ru, w_pu, w_out, proj, bias, h, next_g, tm=512):
    t = ret.shape[0]

    def body(r_ref, p_ref, wr_ref, wp_ref, wo_ref, g0_ref, g1_ref, b_ref, h_ref, ng_ref, m_ref, ru_ref, pu_ref, ho_ref, n_ref):
        ru = _dot(r_ref[...], wr_ref[...])
        pu = _dot(p_ref[...], wp_ref[...])
        ru_ref[...] = ru
        pu_ref[...] = pu
        merged = (_sigmoid(g0_ref[...].astype(F32) + b_ref[0:1, :]) * ru
                  + _sigmoid(g1_ref[...].astype(F32) + b_ref[1:2, :]) * pu).astype(BF16)
        m_ref[...] = merged
        h_new = h_ref[...] + _dot(merged, wo_ref[...])
        ho_ref[...] = h_new
        n_ref[...] = _normed(h_new, ng_ref[...]).astype(BF16)

    row = _row_spec(tm, D_MODEL)
    wspec = pl.BlockSpec((D_MODEL, D_MODEL), lambda i: (0, 0), pipeline_mode=pl.Buffered(1))
    return pl.pallas_call(
        body, name=name, grid=(t // tm,),
        in_specs=[row, row, wspec, wspec, wspec, _row_spec(tm, D_MODEL, GATE0_BLOCK), _row_spec(tm, D_MODEL, GATE1_BLOCK),
                  _full_spec((2, D_MODEL)), row, _full_spec((1, D_MODEL))],
        out_specs=[row] * 5,
        out_shape=[jax.ShapeDtypeStruct((t, D_MODEL), BF16), jax.ShapeDtypeStruct((t, D_MODEL), F32),
                   jax.ShapeDtypeStruct((t, D_MODEL), F32), jax.ShapeDtypeStruct((t, D_MODEL), F32),
                   jax.ShapeDtypeStruct((t, D_MODEL), BF16)],
        compiler_params=_cparams(("parallel",)),
    )(ret, po, w_ru, w_pu, w_out, proj, proj, bias, h, next_g)


def _merge_bwd(name, dh_b, w_out, ru, pu, proj, bias, tm=512):
    t = dh_b.shape[0]

    def body(dh_ref, wo_ref, ru_ref, pu_ref, g0_ref, g1_ref, b_ref, dru_ref, dpu_ref, dg_ref, db_ref):
        i = pl.program_id(0)
        dg0_ref, dg1_ref = dg_ref.at[0], dg_ref.at[1]
        d = _dot(dh_ref[...], wo_ref[...], "nt")
        s0 = _sigmoid(g0_ref[...].astype(F32) + b_ref[0:1, :])
        s1 = _sigmoid(g1_ref[...].astype(F32) + b_ref[1:2, :])
        dru_ref[...] = (d * s0).astype(BF16)
        dpu_ref[...] = (d * s1).astype(BF16)
        dg0 = d * ru_ref[...] * (s0 * (1.0 - s0))
        dg1 = d * pu_ref[...] * (s1 * (1.0 - s1))
        dg0_ref[...] = dg0.astype(BF16)
        dg1_ref[...] = dg1.astype(BF16)
        part0 = jnp.sum(dg0, axis=0, keepdims=True)
        part1 = jnp.sum(dg1, axis=0, keepdims=True)

        @pl.when(i == 0)
        def _():
            db_ref[0:1, :] = part0
            db_ref[1:2, :] = part1

        @pl.when(i > 0)
        def _():
            db_ref[0:1, :] += part0
            db_ref[1:2, :] += part1

    row = _row_spec(tm, D_MODEL)
    return pl.pallas_call(
        body, name=name, grid=(t // tm,),
        in_specs=[row, pl.BlockSpec((D_MODEL, D_MODEL), lambda i: (0, 0), pipeline_mode=pl.Buffered(1)), row, row,
                  _row_spec(tm, D_MODEL, GATE0_BLOCK), _row_spec(tm, D_MODEL, GATE1_BLOCK), _full_spec((2, D_MODEL))],
        out_specs=[row, row, pl.BlockSpec((2, tm, D_MODEL), lambda i: (0, i, 0)), _full_spec((2, D_MODEL))],
        out_shape=[jax.ShapeDtypeStruct((t, D_MODEL), BF16)] * 2 + [jax.ShapeDtypeStruct((2, t, D_MODEL), BF16),
                                                                    jax.ShapeDtypeStruct((2, D_MODEL), F32)],
        compiler_params=_cparams(("arbitrary",)),
    )(dh_b, w_out, ru, pu, proj, proj, bias)


def _half_scale(acc):
    return (FFN_RES_WEIGHT * acc,)


def _normed(h, g):
    return h * lax.rsqrt(jnp.mean(h * h, axis=-1, keepdims=True) + NORM_EPS) * g


def _residual_half_norm(acc, res, g):
    h = res + FFN_RES_WEIGHT * acc
    return h, _normed(h, g)


FF_TILE = D_FF // 2
DW_TILE = 256
SAVED_FF_DTYPE = BF16
FF_CHUNKS = ((0, 512), (512, 1024), (1024, FF_TILE))


def _ffn_in(name, nrm, w_in, tm=512):
    t = nrm.shape[0]
    nj = D_FF // FF_TILE

    def body(n_ref, wg_ref, wu_ref, a_ref, mid_ref):
        nv = n_ref[...]
        for c0, c1 in FF_CHUNKS:
            gate = _dot(nv, wg_ref[:, c0:c1])
            up = _dot(nv, wu_ref[:, c0:c1])
            s = _sigmoid(gate)
            silu = gate * s
            a_ref[0, :, c0:c1] = (FFN_RES_WEIGHT * up * (s * (1.0 + gate * (1.0 - s)))).astype(a_ref.dtype)
            a_ref[1, :, c0:c1] = (FFN_RES_WEIGHT * silu).astype(a_ref.dtype)
            mid_ref[:, c0:c1] = (silu * up).astype(BF16)

    return pl.pallas_call(
        body, name=name, grid=(nj, t // tm),
        in_specs=[pl.BlockSpec((tm, D_MODEL), lambda j, i: (i, 0)),
                  pl.BlockSpec((D_MODEL, FF_TILE), lambda j, i: (0, j)),
                  pl.BlockSpec((D_MODEL, FF_TILE), lambda j, i: (0, j + nj))],
        out_specs=[pl.BlockSpec((2, tm, FF_TILE), lambda j, i: (0, i, j)), pl.BlockSpec((tm, FF_TILE), lambda j, i: (i, j))],
        out_shape=[jax.ShapeDtypeStruct((2, t, D_FF), SAVED_FF_DTYPE), jax.ShapeDtypeStruct((t, D_FF), BF16)],
        compiler_params=_cparams(("parallel", "parallel")),
    )(nrm, w_in, w_in)


def _ffn_dact(name, dout_b, w_out, a, tm=512):
    t = dout_b.shape[0]

    def body(d_ref, w_ref, a_ref, da_ref):
        dv = d_ref[...]
        for c0, c1 in FF_CHUNKS:
            dm = _dot(dv, w_ref[c0:c1, :], "nt")
            da_ref[0, :, c0:c1] = (dm * a_ref[0, :, c0:c1].astype(F32)).astype(BF16)
            da_ref[1, :, c0:c1] = (dm * a_ref[1, :, c0:c1].astype(F32)).astype(BF16)

    blk = pl.BlockSpec((2, tm, FF_TILE), lambda j, i: (0, i, j))
    return pl.pallas_call(
        body, name=name, grid=(D_FF // FF_TILE, t // tm),
        in_specs=[pl.BlockSpec((tm, D_MODEL), lambda j, i: (i, 0)), pl.BlockSpec((FF_TILE, D_MODEL), lambda j, i: (j, 0)), blk],
        out_specs=blk,
        out_shape=jax.ShapeDtypeStruct((2, t, D_FF), BF16),
        compiler_params=_cparams(("parallel", "parallel")),
    )(dout_b, w_out, a)


def _ffn_fwd(tag, h, nrm, get_w_in, get_w_out, finish):
    t = h.shape[0]
    w_in = get_w_in(nrm)
    a, mid = _ffn_in(f"{tag}_in", nrm, w_in, tm=min(512, t))
    w_out = get_w_out(mid)
    return finish(mid, w_out), (nrm, a, mid, w_in, w_out)


def _ffn_bwd(tag, h, g, saved, dout, dout_b, on_grads, flush):
    t = h.shape[0]
    nrm, a, mid, w_in, w_out = saved
    d_w_out = _matmul(f"{tag}_dwout", mid, dout_b, "tn", D_FF, D_MODEL, t, DW_TILE, D_MODEL, t, [BF16], epilogue=_half_scale,
                      resident="b")
    da = _ffn_dact(f"{tag}_dact", dout_b, w_out, a, tm=min(512, t))
    nj = D_FF // DW_TILE
    d_w_in = _dw_resident(f"{tag}_dwin", nrm, [da], [pl.BlockSpec((None, t, DW_TILE), lambda s: (s // nj, 0, s % nj))],
                          2 * nj, None, DW_TILE)
    tie = on_grads({f"{tag}_w_in": d_w_in, f"{tag}_w_out": d_w_out})
    tm = min(256, t)
    dh, dh_b, dg = _proj_norm_bwd(f"{tag}_dn", [da], [pl.BlockSpec((2, tm, D_FF), lambda i: (0, i, 0))],
                                  ((0, 0, 0, D_FF), (0, 1, D_FF, 2 * D_FF)), w_in, h, g if tie is None else g + tie, dout, tm)
    return dh, dh_b, dg, flush(dh)


def _dw_resident(name, u, pieces, piece_specs, n_tiles, which_piece, tn):
    t = u.shape[0]
    npc = len(pieces)

    def body(*refs):
        u_ref, p_refs, o_ref, ut_ref = refs[0], refs[1:1 + npc], refs[1 + npc], refs[2 + npc]
        s = pl.program_id(0)

        @pl.when(s == 0)
        def _():
            ut_ref[...] = u_ref[...].T

        if npc == 1:
            o_ref[...] = _dot(ut_ref[...], p_refs[0][...]).astype(BF16)
        for which in range(npc if npc > 1 else 0):
            @pl.when(which_piece(s) == which)
            def _(which=which):
                o_ref[...] = _dot(ut_ref[...], p_refs[which][...]).astype(BF16)

    return pl.pallas_call(
        body, name=name, grid=(n_tiles,),
        in_specs=[pl.BlockSpec((t, D_MODEL), lambda s: (0, 0), pipeline_mode=pl.Buffered(1))] + list(piece_specs),
        out_specs=pl.BlockSpec((D_MODEL, tn), lambda s: (0, s)),
        out_shape=jax.ShapeDtypeStruct((D_MODEL, n_tiles * tn), BF16),
        scratch_shapes=[pltpu.VMEM((D_MODEL, t), BF16)],
        compiler_params=_cparams(("arbitrary",)),
    )(u, *pieces)


def _mix_dwin(name, u, dqkvg, dp, dgates, tn=512):
    t = u.shape[0]
    nj = D_MODEL // tn
    n4, n5 = 4 * nj, 5 * nj
    specs = [pl.BlockSpec((None, t, tn), lambda s: (jnp.minimum(s // nj, 3), 0, jnp.where(s < n4, s % nj, nj - 1))),
             pl.BlockSpec((t, tn), lambda s: (0, jnp.clip(s - n4, 0, nj - 1))),
             pl.BlockSpec((None, t, tn), lambda s: (jnp.clip(s // nj - 5, 0, 1), 0, jnp.where(s < n5, 0, s % nj)))]
    which = lambda s: jnp.where(s < n4, 0, jnp.where(s < n5, 1, 2))
    return _dw_resident(name, u, [dqkvg, dp, dgates], specs, 7 * nj, which, tn)


def _local_step(x, target, vec, get_w, relay_w, on_grads, flush):
    t = x.shape[0]
    cos, sin = _rope_tables(t)
    rtab = _retention_tables()
    ptab = _pool_tables()
    w = {}

    def getter(group, name):
        def get(after):
            if name not in w:
                w.update(get_w(group, after))
            return w[name]
        return get

    nrm1 = _rmsnorm_fwd("ffn1_norm", x, vec["norm_ffn1"])
    def out_and_norm(mid, w_out):
        return _matmul("ffn1_out", mid, w_out, "nn", t, D_MODEL, D_FF, 512, D_MODEL, D_FF, [F32, BF16],
                       extras=(x,), consts=(vec["norm_mix"],), epilogue=_residual_half_norm)

    (h1, u), s1 = _ffn_fwd("ffn1", x, nrm1, getter(0, "ffn1_w_in"), getter(1, "ffn1_w_out"), out_and_norm)
    w.update(get_w(2, u))
    proj = _matmul("mix_in", u, w["w_in"], "nn", t, IN_WIDTH, D_MODEL, 1024, 1024, D_MODEL, [BF16], n_outer=True)
    o, ret, states = _retention_fwd("retention", proj, cos, sin, rtab)
    pm, mixed, po = _pool_fwd("pool", proj, w["pool_w"], vec["pool_scale"], ptab)
    relay_w(3, po)
    merged, ru, pu, h2, nrm2 = _merge_fwd("merge", ret, po, w["w_ret_up"], w["w_pool_up"], w["w_out"], proj, w["gate_bias"],
                                          h1, vec["norm_ffn2"], tm=min(512, t))
    def out_and_loss(mid, w_out):
        return _out_loss_and_grad("ffn2_out_loss", mid, w_out, h2, vec["norm_final"], target, tm=min(512, t))

    (dh3, dh3_b, dg_final, loss), s2 = _ffn_fwd("ffn2", h2, nrm2, getter(3, "ffn2_w_in"), getter(3, "ffn2_w_out"), out_and_loss)

    def tied(v, tie):
        return v if tie is None else v + tie

    dh2, dh2_b, dg_ffn2, tie = _ffn_bwd("ffn2", h2, vec["norm_ffn2"], s2, dh3, dh3_b, on_grads, flush)
    def square_dw(name, act, grad):
        return _matmul(name, act, grad, "tn", D_MODEL, D_MODEL, t, D_MODEL, D_MODEL, 1024, [BF16])

    d_w_out = square_dw("mix_dwout", merged, dh2_b)
    dru, dpu, dgates, d_bias = _merge_bwd("merge_bwd", dh2_b, w["w_out"], ru, pu, proj, tied(w["gate_bias"], tie))
    d_w_ru = square_dw("mix_dwru", ret, dru)
    d_w_pu = square_dw("mix_dwpu", po, dpu)
    dp, d_pool_w, d_scale = _pool_bwd("pool_bwd", dpu, w["w_pool_up"], pm, mixed, w["pool_w"], vec["pool_scale"], ptab)
    dqkvg = _retention_bwd("retention_bwd", dru, w["w_ret_up"], o, proj, cos, sin, states, rtab)
    d_w_in = _mix_dwin("mix_dwin", u, dqkvg, dp, dgates)
    tie = on_grads(dict(w_in=d_w_in, pool_w=d_pool_w.astype(BF16), w_ret_up=d_w_ru, w_pool_up=d_w_pu, w_out=d_w_out))
    tm = min(256, t)
    parts = ([(0, k, k * D_MODEL, (k + 1) * D_MODEL) for k in range(4)] + [(1, None, 4 * D_MODEL, 5 * D_MODEL)]
             + [(2, k, (5 + k) * D_MODEL, (6 + k) * D_MODEL) for k in range(2)])
    dh1, dh1_b, dg_mix = _proj_norm_bwd("mix_du", [dqkvg, dp, dgates],
                                        [pl.BlockSpec((4, tm, D_MODEL), lambda i: (0, i, 0)), _row_spec(tm, D_MODEL),
                                         pl.BlockSpec((2, tm, D_MODEL), lambda i: (0, i, 0))],
                                        parts, w["w_in"], h1, tied(vec["norm_mix"], tie), dh2, tm)
    tie = flush(dh1)
    dx, _, dg_ffn1, _ = _ffn_bwd("ffn1", x, tied(vec["norm_ffn1"], tie), s1, dh1, dh1_b, on_grads, flush)

    small = dict(norm_ffn1=dg_ffn1, norm_mix=dg_mix, gate_bias=d_bias, pool_scale=d_scale, norm_ffn2=dg_ffn2,
                 norm_final=dg_final)
    return loss[0, 0], dx, small


BIG = ("ffn1_w_in", "ffn1_w_out", "w_in", "pool_w", "w_ret_up", "w_pool_up", "w_out", "ffn2_w_in", "ffn2_w_out")
KIND = dict(ffn1_w_in="col", ffn1_w_out="row", w_in="col", pool_w="pool", w_ret_up="row", w_pool_up="row", w_out="row",
            ffn2_w_in="col", ffn2_w_out="row", gate_bias="col")
ANY = pl.BlockSpec(memory_space=pl.ANY)


def _place():
    x, y, c = lax.axis_index("x"), lax.axis_index("y"), lax.axis_index("c")
    chips = [(1 - x, y), (x, 1 - y), (1 - x, 1 - y)]
    return x, y, c, chips


def _full_view_shape(kind, local_shape):
    if kind == "col":
        return (2, local_shape[0] // 2, N_CHIPS * local_shape[1])
    if kind == "row":
        return (N_CHIPS, 2, local_shape[0] // 2, local_shape[1])
    return (GROUPS, N_CHIPS, 2, local_shape[1] // 2, local_shape[2])


def _local_view(kind, arr):
    if kind == "pool":
        return arr.reshape(GROUPS, 2, arr.shape[1] // 2, arr.shape[2])
    return arr.reshape(2, arr.shape[0] // 2, arr.shape[1])


def _blk(kind, ref, s, c):
    if kind == "col":
        cs = ref.shape[2] // N_CHIPS
        return ref.at[c, :, pl.ds(pl.multiple_of(s * cs, 128), cs)]
    if kind == "row":
        return ref.at[s, c]
    return ref.at[:, s, c]


def _half(kind, ref, c):
    return ref.at[:, c] if kind == "pool" else ref.at[c]


def _shard(kind, ref, s):
    if kind == "col":
        cs = ref.shape[2] // N_CHIPS
        return ref.at[:, :, pl.ds(pl.multiple_of(s * cs, 128), cs)]
    if kind == "row":
        return ref.at[s]
    return ref.at[:, s]


HBM = pl.BlockSpec(memory_space=pltpu.HBM)
SEM = pl.BlockSpec(memory_space=pltpu.SEMAPHORE)
EFFECT = pltpu.SideEffectType.DATAFLOW_SIDE_EFFECTING
WEIGHT_GROUPS = (("gate_bias", "ffn1_w_in"), ("ffn1_w_out",), ("w_in", "pool_w", "w_ret_up", "w_pool_up", "w_out"), ("ffn2_w_in", "ffn2_w_out"))
GRAD_GROUPS = (("ffn2_w_in", "ffn2_w_out"), ("w_in", "pool_w", "w_ret_up", "w_pool_up", "w_out"), ("ffn1_w_in", "ffn1_w_out"))


def _hbm(a):
    return pltpu.with_memory_space_constraint(a, pltpu.HBM)


def _natural(kind, o):
    if kind == "col":
        return o.reshape(o.shape[0] * o.shape[1], o.shape[2])
    if kind == "row":
        return o.reshape(-1, o.shape[3])
    return o.reshape(GROUPS, -1, o.shape[4])


def _ici_copy(kind, loc, full, j, chips, s, c, send_sem, recv_sem):
    px, py = chips[j]
    return (pltpu.make_async_remote_copy(src_ref=_half(kind, loc, c), dst_ref=_blk(kind, full, s, c), send_sem=send_sem,
                                         recv_sem=recv_sem, device_id=(px, py, c), device_id_type=MESH),
            pltpu.make_async_remote_copy(src_ref=_half(kind, loc, c), dst_ref=_blk(kind, full, 2 * px + py, c), send_sem=send_sem,
                                         recv_sem=recv_sem, device_id=(px, py, c), device_id_type=MESH))


def _gather_start(tag, group_ids, shards):
    grps = [WEIGHT_GROUPS[g] for g in group_ids]
    names = [nm for grp in grps for nm in grp]
    kinds = [KIND[nm] for nm in names]
    n, ng = len(names), len(grps)
    locs = [_hbm(_local_view(KIND[nm], shards[nm])) for nm in names]
    lands = [_hbm(lax.empty(_full_view_shape(KIND[nm], shards[nm].shape), shards[nm].dtype)) for nm in names]
    first = np.cumsum([0] + [len(grp) for grp in grps])

    def body(*refs):
        loc, full = refs[:n], refs[n:2 * n]
        send_sems, recv_sems = refs[2 * n:2 * n + ng], refs[2 * n + ng:2 * n + 2 * ng]
        token = refs[-1]
        x, y, c, chips = _place()
        s = 2 * x + y
        for g in range(ng):
            for a in range(first[g], first[g + 1]):
                for j in range(3):
                    k = 3 * (a - first[g]) + j
                    _ici_copy(kinds[a], loc[a], full[a], j, chips, s, c, send_sems[g].at[k], recv_sems[g].at[k])[0].start()
        token[...] = jnp.zeros(token.shape, F32)

    sem_shapes = [pltpu.SemaphoreType.DMA((3 * len(grp),)) for grp in grps]
    outs = pl.pallas_call(
        body, name=f"gather_start_{tag}",
        in_specs=[HBM] * (2 * n),
        out_specs=[SEM] * (2 * ng) + [HBM] * (2 * n) + [pl.BlockSpec(memory_space=pltpu.VMEM)],
        out_shape=sem_shapes + sem_shapes + [pltpu.HBM(a.shape, a.dtype) for a in locs + lands] + [jax.ShapeDtypeStruct((8, 128), F32)],
        input_output_aliases={i: 2 * ng + i for i in range(2 * n)},
        compiler_params=pltpu.CompilerParams(has_side_effects=EFFECT),
    )(*locs, *lands)
    send_sems, recv_sems = outs[:ng], outs[ng:2 * ng]
    locs_t, lands_t = outs[2 * ng:2 * ng + n], outs[2 * ng + n:2 * ng + 2 * n]
    groups = {}
    for k, g in enumerate(group_ids):
        sl = slice(first[k], first[k + 1])
        groups[g] = (send_sems[k], recv_sems[k], list(locs_t[sl]), list(lands_t[sl]))
    return groups, outs[-1]


def _forward_copies(kinds, loc, full, send_sems, recv_sems):
    x, y, c, chips = _place()
    s = 2 * x + y

    def remote(a, k, src, dst):
        return pltpu.make_async_remote_copy(src_ref=src, dst_ref=dst, send_sem=send_sems.at[4 * a + k],
                                            recv_sem=recv_sems.at[4 * a + k], device_id=(x, y, 1 - c), device_id_type=MESH)

    sends, arrivals = [], []
    for a, kind in enumerate(kinds):
        for j, (px, py) in enumerate(chips):
            theirs, from_sib = _blk(kind, full[a], 2 * px + py, c), _blk(kind, full[a], 2 * px + py, 1 - c)
            sends.append(remote(a, j, theirs, theirs))
            arrivals.append(remote(a, j, from_sib, from_sib))
        own = _shard(kind, full[a], s)
        sends.append(remote(a, 3, loc[a], own))
        arrivals.append(remote(a, 3, own, own))
    return sends, arrivals


def _gather_relay(g, group, after):
    names = WEIGHT_GROUPS[g]
    kinds = [KIND[nm] for nm in names]
    m = len(names)
    ici_send, ici_recv, locs, lands = group

    def body(*refs):
        loc, full = refs[:m], refs[m:2 * m]
        ici_s, ici_r = refs[2 * m], refs[2 * m + 1]
        d2d_s, d2d_r = refs[2 * m + 2 + len(after)], refs[2 * m + 3 + len(after)]
        x, y, c, chips = _place()
        for a in range(m):
            for j in range(3):
                sent, landed = _ici_copy(kinds[a], loc[a], full[a], j, chips, 2 * x + y, c, ici_s.at[3 * a + j], ici_r.at[3 * a + j])
                sent.wait_send()
                landed.wait_recv()
        for cp in _forward_copies(kinds, loc, full, d2d_s, d2d_r)[0]:
            cp.start()

    sem_shape = pltpu.SemaphoreType.DMA((4 * m,))
    outs = pl.pallas_call(
        body, name=f"gather_relay_{g}",
        in_specs=[HBM] * (2 * m) + [SEM, SEM] + [ANY] * len(after), out_specs=[SEM, SEM] + [HBM] * (2 * m),
        out_shape=[sem_shape, sem_shape] + [pltpu.HBM(a.shape, a.dtype) for a in locs + lands],
        input_output_aliases={i: 2 + i for i in range(2 * m)},
        compiler_params=pltpu.CompilerParams(has_side_effects=EFFECT),
    )(*locs, *lands, ici_send, ici_recv, *after)
    return outs[0], outs[1], list(outs[2:2 + m]), list(outs[2 + m:2 + 2 * m])


def _gather_land(g, state, after):
    names = WEIGHT_GROUPS[g]
    kinds = [KIND[nm] for nm in names]
    m = len(names)
    d2d_send, d2d_recv, locs, lands = state

    def body(*refs):
        sends, arrivals = _forward_copies(kinds, refs[:m], refs[m:2 * m], refs[2 * m], refs[2 * m + 1])
        for cp in sends:
            cp.wait_send()
        for cp in arrivals:
            cp.wait_recv()

    outs = pl.pallas_call(
        body, name=f"gather_land_{g}",
        in_specs=[HBM] * (2 * m) + [SEM, SEM] + [ANY] * len(after), out_specs=[HBM] * (2 * m),
        out_shape=[pltpu.HBM(a.shape, a.dtype) for a in locs + lands],
        input_output_aliases={i: i for i in range(2 * m)},
        compiler_params=pltpu.CompilerParams(has_side_effects=EFFECT),
    )(*locs, *lands, d2d_send, d2d_recv, *after)
    return {nm: _natural(k, o) for nm, k, o in zip(names, kinds, outs[m:])}


def _gather_finish(g, group, after):
    names = WEIGHT_GROUPS[g]
    kinds = [KIND[nm] for nm in names]
    m = len(names)
    send_sem, recv_sem, locs, lands = group

    def wait_body(*refs):
        loc, full = refs[:m], refs[m:2 * m]
        send_sems, recv_sems = refs[2 * m], refs[2 * m + 1]
        x, y, c, chips = _place()
        s = 2 * x + y
        for a in range(m):
            for j in range(3):
                k = 3 * a + j
                sent, landed = _ici_copy(kinds[a], loc[a], full[a], j, chips, s, c, send_sems.at[k], recv_sems.at[k])
                sent.wait_send()
                landed.wait_recv()

    outs = pl.pallas_call(
        wait_body, name=f"gather_wait_{g}",
        in_specs=[HBM] * (2 * m) + [SEM, SEM] + [ANY] * len(after), out_specs=[HBM] * (2 * m),
        out_shape=[pltpu.HBM(a.shape, a.dtype) for a in locs + lands],
        input_output_aliases={i: i for i in range(2 * m)},
        compiler_params=pltpu.CompilerParams(has_side_effects=EFFECT),
    )(*locs, *lands, send_sem, recv_sem, *after)
    locs, lands = outs[:m], outs[m:]

    def forward_body(*refs):
        sends, arrivals = _forward_copies(kinds, refs[:m], refs[2 * m:3 * m], *refs[3 * m:])
        for cp in sends:
            cp.start()
        for cp in arrivals:
            cp.wait_recv()
        for cp in sends:
            cp.wait_send()

    outs = pl.pallas_call(
        forward_body, name=f"gather_forward_{g}",
        in_specs=[ANY] * (2 * m), out_specs=[ANY] * m,
        out_shape=[jax.ShapeDtypeStruct(a.shape, a.dtype) for a in lands],
        input_output_aliases={m + i: i for i in range(m)},
        scratch_shapes=[pltpu.SemaphoreType.DMA((4 * m,)), pltpu.SemaphoreType.DMA((4 * m,))],
    )(*locs, *lands)
    return {nm: _natural(k, o) for nm, k, o in zip(names, kinds, outs)}


def _grad_view(kind, g):
    if kind == "col":
        return g.reshape(2, g.shape[0] // 2, g.shape[1])
    if kind == "row":
        return g.reshape(N_CHIPS, 2, g.shape[0] // (2 * N_CHIPS), g.shape[1])
    return g.reshape(GROUPS, N_CHIPS, 2, g.shape[1] // (2 * N_CHIPS), g.shape[2])


def _pair_copies(kinds, g, got, send_sems, recv_sems):
    x, y, c, _ = _place()

    def other_half(kind, ref):
        if kind == "col":
            return ref.at[1 - c]
        if kind == "row":
            return ref.at[:, 1 - c]
        return ref.at[:, :, 1 - c]

    return [pltpu.make_async_remote_copy(src_ref=other_half(kinds[a], g[a]), dst_ref=got[a], send_sem=send_sems.at[a],
                                         recv_sem=recv_sems.at[a], device_id=(x, y, 1 - c), device_id_type=MESH)
            for a in range(len(kinds))]


def _pair_exchange_start(tag, names, views):
    kinds = [KIND[nm] for nm in names]
    n = len(names)

    def got_shape(kind, v):
        if kind == "col":
            return v.shape[1:]
        if kind == "row":
            return (v.shape[0],) + v.shape[2:]
        return v.shape[:2] + v.shape[3:]

    srcs = [_hbm(views[nm]) for nm in names]
    lands = [_hbm(lax.empty(got_shape(k, views[nm]), BF16)) for nm, k in zip(names, kinds)]

    def body(*refs):
        g, got = refs[:n], refs[n:2 * n]
        for cp in _pair_copies(kinds, g, got, refs[2 * n], refs[2 * n + 1]):
            cp.start()
        refs[-1][...] = jnp.zeros(refs[-1].shape, F32)

    sem_shape = pltpu.SemaphoreType.DMA((n,))
    outs = pl.pallas_call(
        body, name=f"grad_pair_exchange_start_{tag}",
        in_specs=[HBM] * (2 * n),
        out_specs=[SEM, SEM] + [HBM] * (2 * n) + [pl.BlockSpec(memory_space=pltpu.VMEM)],
        out_shape=[sem_shape, sem_shape] + [pltpu.HBM(a.shape, a.dtype) for a in srcs + lands] + [jax.ShapeDtypeStruct((8, 128), F32)],
        input_output_aliases={i: 2 + i for i in range(2 * n)},
        compiler_params=pltpu.CompilerParams(has_side_effects=EFFECT),
    )(*srcs, *lands)
    return (outs[0], outs[1], list(outs[2:2 + n]), list(outs[2 + n:2 + 2 * n])), outs[-1]


def _pair_exchange_wait(tag, names, state, after):
    kinds = [KIND[nm] for nm in names]
    n = len(names)
    send_sem, recv_sem, srcs, lands = state

    def body(*refs):
        g, got = refs[:n], refs[n:2 * n]
        for cp in _pair_copies(kinds, g, got, refs[2 * n], refs[2 * n + 1]):
            cp.wait_send()
            cp.wait_recv()

    outs = pl.pallas_call(
        body, name=f"grad_pair_exchange_wait_{tag}",
        in_specs=[HBM] * (2 * n) + [SEM, SEM, ANY], out_specs=[HBM] * (2 * n),
        out_shape=[pltpu.HBM(a.shape, a.dtype) for a in srcs + lands],
        input_output_aliases={i: i for i in range(2 * n)},
        compiler_params=pltpu.CompilerParams(has_side_effects=EFFECT),
    )(*srcs, *lands, send_sem, recv_sem, after)
    return dict(zip(names, outs[:n])), dict(zip(names, outs[n:]))


def _pair_sum(name, kind, view, got, c_arr):
    if kind == "col":
        _, rows, cols = view.shape
        tr = 128
        grid = (rows // tr,)
        v_spec = pl.BlockSpec((None, tr, cols), lambda i, c: (c[0], i, 0))
        g_spec = pl.BlockSpec((tr, cols), lambda i, c: (i, 0))
    elif kind == "row":
        _, _, rows, cols = view.shape
        grid = (N_CHIPS,)
        v_spec = pl.BlockSpec((None, None, rows, cols), lambda i, c: (i, c[0], 0, 0))
        g_spec = pl.BlockSpec((None, rows, cols), lambda i, c: (i, 0, 0))
    else:
        _, _, _, rows, cols = view.shape
        grid = (GROUPS,)
        v_spec = pl.BlockSpec((None, N_CHIPS, None, rows, cols), lambda i, c: (i, 0, c[0], 0, 0))
        g_spec = pl.BlockSpec((None, N_CHIPS, rows, cols), lambda i, c: (i, 0, 0, 0))

    def body(c_ref, v_ref, g_ref, o_ref):
        o_ref[...] = (v_ref[...].astype(F32) + g_ref[...].astype(F32)).astype(BF16)

    return pl.pallas_call(
        body, name=name,
        grid_spec=pltpu.PrefetchScalarGridSpec(num_scalar_prefetch=1, grid=grid, in_specs=[v_spec, g_spec], out_specs=g_spec),
        out_shape=jax.ShapeDtypeStruct(got.shape, BF16),
        compiler_params=_cparams(("parallel",)),
    )(c_arr, view, got)


def _piece(kind, ref, s):
    if kind == "col":
        cs = ref.shape[1] // N_CHIPS
        return ref.at[:, pl.ds(pl.multiple_of(s * cs, 128), cs)]
    if kind == "row":
        return ref.at[s]
    return ref.at[:, s]


def _piece_shape(kind, shape):
    if kind == "col":
        return (shape[0], shape[1] // N_CHIPS)
    if kind == "row":
        return shape[1:]
    return (shape[0],) + shape[2:]


def _shard_copies(kinds, p, got, send_sems, recv_sems):
    x, y, c, chips = _place()
    return [pltpu.make_async_remote_copy(src_ref=_piece(kinds[a], p[a], 2 * px + py), dst_ref=got[a].at[j],
                                         send_sem=send_sems.at[3 * a + j], recv_sem=recv_sems.at[3 * a + j],
                                         device_id=(px, py, c), device_id_type=MESH)
            for a in range(len(kinds)) for j, (px, py) in enumerate(chips)]


def _shard_exchange_start(g, names, psums):
    kinds = [KIND[nm] for nm in names]
    n = len(names)
    srcs = [_hbm(psums[nm]) for nm in names]
    lands = [_hbm(lax.empty((3,) + _piece_shape(k, psums[nm].shape), BF16)) for nm, k in zip(names, kinds)]

    def body(*refs):
        p, got = refs[:n], refs[n:2 * n]
        send_sems, recv_sems = refs[2 * n], refs[2 * n + 1]
        token = refs[-1]
        for cp in _shard_copies(kinds, p, got, send_sems, recv_sems):
            cp.start()
        token[...] = jnp.zeros(token.shape, F32)

    sem_shape = pltpu.SemaphoreType.DMA((3 * n,))
    outs = pl.pallas_call(
        body, name=f"grad_shard_exchange_start_{g}",
        in_specs=[HBM] * (2 * n),
        out_specs=[SEM, SEM] + [HBM] * (2 * n) + [pl.BlockSpec(memory_space=pltpu.VMEM)],
        out_shape=[sem_shape, sem_shape] + [pltpu.HBM(a.shape, a.dtype) for a in srcs + lands] + [jax.ShapeDtypeStruct((8, 128), F32)],
        input_output_aliases={i: 2 + i for i in range(2 * n)},
        compiler_params=pltpu.CompilerParams(has_side_effects=EFFECT),
    )(*srcs, *lands)
    return (outs[0], outs[1], list(outs[2:2 + n]), list(outs[2 + n:2 + 2 * n])), outs[-1]


def _shard_exchange_wait(g, names, state, after):
    kinds = [KIND[nm] for nm in names]
    n = len(names)
    send_sem, recv_sem, srcs, lands = state

    def body(*refs):
        p, got = refs[:n], refs[n:2 * n]
        for cp in _shard_copies(kinds, p, got, refs[2 * n], refs[2 * n + 1]):
            cp.wait_send()
            cp.wait_recv()

    outs = pl.pallas_call(
        body, name=f"grad_shard_exchange_wait_{g}",
        in_specs=[HBM] * (2 * n) + [SEM, SEM] + [ANY] * len(after), out_specs=[HBM] * (2 * n),
        out_shape=[pltpu.HBM(a.shape, a.dtype) for a in srcs + lands],
        input_output_aliases={i: i for i in range(2 * n)},
        compiler_params=pltpu.CompilerParams(has_side_effects=EFFECT),
    )(*srcs, *lands, send_sem, recv_sem, *after)
    return dict(zip(names, outs[:n])), dict(zip(names, outs[n:]))


def _shard_sum(name, kind, psum, got, sc_arr):
    if kind == "col":
        rows, cols = psum.shape
        cs = cols // N_CHIPS
        tr = 128
        grid = (rows // tr,)
        p_spec = pl.BlockSpec((tr, cs), lambda i, sc: (i, sc[0]))
        g_spec = pl.BlockSpec((3, tr, cs), lambda i, sc: (0, i, 0))
        o_spec = pl.BlockSpec((None, tr, cs), lambda i, sc: (sc[1], i, 0))
        out_shape = (2, rows, cs)
    elif kind == "row":
        _, rows, cols = psum.shape
        grid = (1,)
        p_spec = pl.BlockSpec((None, rows, cols), lambda i, sc: (sc[0], 0, 0))
        g_spec = pl.BlockSpec((3, rows, cols), lambda i, sc: (0, 0, 0))
        o_spec = pl.BlockSpec((None, rows, cols), lambda i, sc: (sc[1], 0, 0))
        out_shape = (2, rows, cols)
    else:
        _, _, rows, cols = psum.shape
        grid = (1,)
        p_spec = pl.BlockSpec((GROUPS, None, rows, cols), lambda i, sc: (0, sc[0], 0, 0))
        g_spec = pl.BlockSpec((3, GROUPS, rows, cols), lambda i, sc: (0, 0, 0, 0))
        o_spec = pl.BlockSpec((GROUPS, None, rows, cols), lambda i, sc: (0, sc[1], 0, 0))
        out_shape = (GROUPS, 2, rows, cols)

    def body(sc_ref, p_ref, g_ref, o_ref):
        o_ref[...] = ((p_ref[...].astype(F32) + g_ref[0].astype(F32)) + g_ref[1].astype(F32)) + g_ref[2].astype(F32)

    return pl.pallas_call(
        body, name=name,
        grid_spec=pltpu.PrefetchScalarGridSpec(num_scalar_prefetch=1, grid=grid, in_specs=[p_spec, g_spec], out_specs=o_spec),
        out_shape=jax.ShapeDtypeStruct(out_shape, F32),
        compiler_params=_cparams(("parallel",)),
    )(sc_arr, psum, got)


def _half_copies(kinds, bufs, send_sems, recv_sems):
    x, y, c, _ = _place()

    def remote(a, half):
        part = _half(kinds[a], bufs[a], half)
        return pltpu.make_async_remote_copy(src_ref=part, dst_ref=part, send_sem=send_sems.at[a], recv_sem=recv_sems.at[a],
                                            device_id=(x, y, 1 - c), device_id_type=MESH)

    return [remote(a, c) for a in range(len(kinds))], [remote(a, 1 - c) for a in range(len(kinds))]


def _half_exchange_start(tag, names, bufs):
    kinds = [KIND[nm] for nm in names]
    n = len(names)
    arrs = [_hbm(bufs[nm]) for nm in names]

    def body(*refs):
        for cp in _half_copies(kinds, refs[:n], refs[n], refs[n + 1])[0]:
            cp.start()
        refs[-1][...] = jnp.zeros(refs[-1].shape, F32)

    sem_shape = pltpu.SemaphoreType.DMA((n,))
    outs = pl.pallas_call(
        body, name=f"grad_half_exchange_start_{tag}",
        in_specs=[HBM] * n,
        out_specs=[SEM, SEM] + [HBM] * n + [pl.BlockSpec(memory_space=pltpu.VMEM)],
        out_shape=[sem_shape, sem_shape] + [pltpu.HBM(a.shape, a.dtype) for a in arrs] + [jax.ShapeDtypeStruct((8, 128), F32)],
        input_output_aliases={i: 2 + i for i in range(n)},
        compiler_params=pltpu.CompilerParams(has_side_effects=EFFECT),
    )(*arrs)
    return (outs[0], outs[1], list(outs[2:2 + n])), outs[-1]


def _half_exchange_wait(tag, names, state, after):
    kinds = [KIND[nm] for nm in names]
    n = len(names)
    send_sem, recv_sem, arrs = state

    def body(*refs):
        sends, arrivals = _half_copies(kinds, refs[:n], refs[n], refs[n + 1])
        for cp in sends:
            cp.wait_send()
        for cp in arrivals:
            cp.wait_recv()

    outs = pl.pallas_call(
        body, name=f"grad_half_exchange_wait_{tag}",
        in_specs=[HBM] * n + [SEM, SEM] + [ANY] * len(after), out_specs=[HBM] * n,
        out_shape=[pltpu.HBM(a.shape, a.dtype) for a in arrs],
        input_output_aliases={i: i for i in range(n)},
        compiler_params=pltpu.CompilerParams(has_side_effects=EFFECT),
    )(*arrs, send_sem, recv_sem, *after)
    return dict(zip(names, outs))


N_DEV = 8
SMALL_ROWS = 8


def _all_reduce_small(name, v):
    def body(v_ref, o_ref, token, buf, send_sems, recv_sems):
        token[...] = jnp.zeros(token.shape, F32)
        x, y, c, _ = _place()
        me = 4 * x + 2 * y + c
        buf[me] = v_ref[...]
        cps = []
        for r in range(1, N_DEV):
            to = (x ^ (r >> 2), y ^ ((r >> 1) & 1), c ^ (r & 1))
            cp = pltpu.make_async_remote_copy(src_ref=v_ref, dst_ref=buf.at[me], send_sem=send_sems.at[r - 1],
                                              recv_sem=recv_sems.at[r - 1], device_id=to, device_id_type=MESH)
            cp.start()
            cps.append(cp)
        for r in range(1, N_DEV):
            pltpu.make_async_remote_copy(src_ref=v_ref, dst_ref=buf.at[me ^ r], send_sem=send_sems.at[r - 1],
                                         recv_sem=recv_sems.at[r - 1], device_id=(x, y, c), device_id_type=MESH).wait_recv()
        for cp in cps:
            cp.wait_send()
        acc = buf[0]
        for d in range(1, N_DEV):
            acc = acc + buf[d]
        o_ref[...] = acc

    vm = pl.BlockSpec(memory_space=pltpu.VMEM)
    return pl.pallas_call(
        body, name=name, in_specs=[vm], out_specs=[vm, vm],
        out_shape=[jax.ShapeDtypeStruct((SMALL_ROWS, D_MODEL), F32), jax.ShapeDtypeStruct((8, 128), F32)],
        scratch_shapes=[pltpu.VMEM((N_DEV, SMALL_ROWS, D_MODEL), F32), pltpu.SemaphoreType.DMA((N_DEV - 1,)),
                        pltpu.SemaphoreType.DMA((N_DEV - 1,))],
    )(v)


def _adamw(name, w, g, m, v, with_grad=False):
    rows, cols = w.shape
    tr = next((c for c in (256, 176, 128, 64, 32, 8) if rows % c == 0), rows)
    spec = pl.BlockSpec((tr, cols), lambda i: (i, 0))

    def body(w_ref, g_ref, m_ref, v_ref, d_ref, mo_ref, vo_ref, *go_ref):
        gv = g_ref[...]
        if with_grad:
            go_ref[0][...] = gv
        m_new = ADAM_B1 * m_ref[...] + (1.0 - ADAM_B1) * gv
        v_new = ADAM_B2 * v_ref[...] + (1.0 - ADAM_B2) * jnp.square(gv)
        m_hat = m_new / (1.0 - ADAM_B1 ** ADAM_STEP)
        v_hat = v_new / (1.0 - ADAM_B2 ** ADAM_STEP)
        d_ref[...] = -ADAM_LR * (m_hat / (jnp.sqrt(v_hat) + ADAM_EPS) + ADAM_WD * w_ref[...])
        mo_ref[...] = m_new
        vo_ref[...] = v_new

    return pl.pallas_call(
        body, name=name, grid=(rows // tr,),
        in_specs=[spec] * 4, out_specs=[spec] * (4 if with_grad else 3),
        out_shape=[jax.ShapeDtypeStruct((rows, cols), F32)] * (4 if with_grad else 3),
        compiler_params=_cparams(("parallel",)),
    )(w, g, m, v)


WEIGHTS = ("norm_ffn1", "ffn1_w_in", "ffn1_w_out", "norm_mix", "w_in", "gate_bias", "pool_w", "pool_scale", "w_ret_up",
           "w_pool_up", "w_out", "norm_ffn2", "ffn2_w_in", "ffn2_w_out", "norm_final")
SMALL_ROW = dict(norm_ffn1=0, norm_mix=1, gate_bias=2, pool_scale=4, norm_ffn2=5, norm_final=6)


def _as2d(a):
    return a.reshape(-1, a.shape[-1])


def kernel(x, norm_ffn1, ffn1_w_in, ffn1_w_out, norm_mix, w_in, gate_bias, pool_w, pool_scale, w_ret_up, w_pool_up, w_out, norm_ffn2, ffn2_w_in, ffn2_w_out, norm_final, loss_target, m_norm_ffn1, m_ffn1_w_in, m_ffn1_w_out, m_norm_mix, m_w_in, m_gate_bias, m_pool_w, m_pool_scale, m_w_ret_up, m_w_pool_up, m_w_out, m_norm_ffn2, m_ffn2_w_in, m_ffn2_w_out, m_norm_final, v_norm_ffn1, v_ffn1_w_in, v_ffn1_w_out, v_norm_mix, v_w_in, v_gate_bias, v_pool_w, v_pool_scale, v_w_ret_up, v_w_pool_up, v_w_out, v_norm_ffn2, v_ffn2_w_in, v_ffn2_w_out, v_norm_final):
    wt = dict(norm_ffn1=norm_ffn1, ffn1_w_in=ffn1_w_in, ffn1_w_out=ffn1_w_out, norm_mix=norm_mix, w_in=w_in, gate_bias=gate_bias,
              pool_w=pool_w, pool_scale=pool_scale, w_ret_up=w_ret_up, w_pool_up=w_pool_up, w_out=w_out, norm_ffn2=norm_ffn2,
              ffn2_w_in=ffn2_w_in, ffn2_w_out=ffn2_w_out, norm_final=norm_final)
    mom = dict(norm_ffn1=m_norm_ffn1, ffn1_w_in=m_ffn1_w_in, ffn1_w_out=m_ffn1_w_out, norm_mix=m_norm_mix, w_in=m_w_in,
               gate_bias=m_gate_bias, pool_w=m_pool_w, pool_scale=m_pool_scale, w_ret_up=m_w_ret_up, w_pool_up=m_w_pool_up,
               w_out=m_w_out, norm_ffn2=m_norm_ffn2, ffn2_w_in=m_ffn2_w_in, ffn2_w_out=m_ffn2_w_out, norm_final=m_norm_final)
    var = dict(norm_ffn1=v_norm_ffn1, ffn1_w_in=v_ffn1_w_in, ffn1_w_out=v_ffn1_w_out, norm_mix=v_norm_mix, w_in=v_w_in,
               gate_bias=v_gate_bias, pool_w=v_pool_w, pool_scale=v_pool_scale, w_ret_up=v_w_ret_up, w_pool_up=v_w_pool_up,
               w_out=v_w_out, norm_ffn2=v_norm_ffn2, ffn2_w_in=v_ffn2_w_in, ffn2_w_out=v_ffn2_w_out, norm_final=v_norm_final)

    ax, ay, ac = lax.axis_index("x"), lax.axis_index("y"), lax.axis_index("c")
    chip = 2 * ax + ay
    c_arr = jnp.reshape(ac, (1,)).astype(jnp.int32)
    sc_arr = jnp.stack([chip, ac]).astype(jnp.int32)
    bias_cols = gate_bias.shape[-1]

    first = {"gate_bias": gate_bias[0], "ffn1_w_in": ffn1_w_in[0].astype(BF16)}
    gather_groups, token = _gather_start("first", [0], first)
    rest, rest_token = _gather_start("rest", [1, 2, 3],
                                     {nm: wt[nm][0].astype(BF16) + token[0, 0].astype(BF16) for nm in BIG if nm not in first})
    gather_groups.update(rest)
    vec = dict(norm_ffn1=norm_ffn1, norm_mix=norm_mix, norm_ffn2=norm_ffn2, pool_scale=pool_scale,
               norm_final=norm_final.reshape(1, D_MODEL))

    relayed = {}

    def relay_w(g, after):
        relayed[g] = _gather_relay(g, gather_groups[g], (after,))

    def get_w(g, after):
        if g in relayed:
            return _gather_land(g, relayed[g], (after,))
        return _gather_finish(g, gather_groups[g], (after, rest_token) if g == 0 else (after,))

    pairs, pending = [], []

    def on_grads(gr):
        g = len(pairs)
        names = GRAD_GROUPS[g]
        assert set(names) == set(gr), (names, list(gr))
        state, token = _pair_exchange_start(g, names, {nm: _grad_view(KIND[nm], gr[nm]) for nm in names})
        pairs.append(state)
        return token[0:1, 0:1]

    def flush(after):
        g = len(pending)
        names = GRAD_GROUPS[g]
        views, from_sib = _pair_exchange_wait(g, names, pairs[g], after)
        psums = {nm: _pair_sum(f"pair_sum_{nm}", KIND[nm], views[nm], from_sib[nm], c_arr) for nm in names}
        state, token = _shard_exchange_start(g, names, psums)
        pending.append(state)
        tokens.append(token)
        return token[0:1, 0:1]

    tokens = []
    loss_local, dx, small = _local_step(x[0], loss_target[0], vec, get_w, relay_w, on_grads, flush)

    grads, delta, new_m, new_v = {}, {}, {}, {}

    def adamw(nm):
        shape = wt[nm].shape
        outs = _adamw(f"adamw_{nm}", _as2d(wt[nm]), _as2d(grads[nm]), _as2d(mom[nm]), _as2d(var[nm]), with_grad=nm in BIG)
        delta[nm], new_m[nm], new_v[nm] = (o.reshape(shape) for o in outs[:3])
        if nm in BIG:
            grads[nm] = outs[3].reshape(shape)
        return outs[0]

    def reduce_start(g, after):
        names = GRAD_GROUPS[g]
        psums, from_chips = _shard_exchange_wait(g, names, pending[g], after)
        bufs = {nm: _shard_sum(f"shard_sum_{nm}", KIND[nm], psums[nm], from_chips[nm], sc_arr) for nm in names}
        return _half_exchange_start(g, names, bufs)

    def reduce_finish(g, state, after):
        names = GRAD_GROUPS[g]
        reduced = _half_exchange_wait(g, names, state, after)
        for nm in names:
            grads[nm] = reduced[nm].reshape(wt[nm].shape)
        return tuple(adamw(nm) for nm in names)

    swap0, token = reduce_start(0, (tokens[-1],))
    swap1, token = reduce_start(1, (token,))
    done = reduce_finish(0, swap0, (token,))
    done = reduce_finish(1, swap1, done)
    swap2, token = reduce_start(2, done)
    packed = jnp.concatenate([small["norm_ffn1"], small["norm_mix"], small["gate_bias"], small["pool_scale"],
                              small["norm_ffn2"], small["norm_final"], jnp.broadcast_to(loss_local, (1, D_MODEL))], axis=0)
    small_sum, _ = _all_reduce_small("reduce_small_grads", packed + token[0, 0])
    loss = small_sum[SMALL_ROWS - 1, 0]
    for nm in ("norm_ffn1", "norm_mix", "pool_scale", "norm_ffn2"):
        grads[nm] = small_sum[SMALL_ROW[nm]][None, :]
    grads["norm_final"] = small_sum[SMALL_ROW["norm_final"]]
    grads["gate_bias"] = lax.dynamic_slice(small_sum, (SMALL_ROW["gate_bias"], chip * bias_cols), (2, bias_cols))[None]
    reduce_finish(2, swap2, (small_sum,))
    for nm in WEIGHTS:
        if nm not in delta:
            adamw(nm)

    return (loss, dx[None], *[grads[nm] for nm in WEIGHTS], *[delta[nm] for nm in WEIGHTS],
            *[new_m[nm] for nm in WEIGHTS], *[new_v[nm] for nm in WEIGHTS])
```

```python
import numpy as np
import jax
import jax.numpy as jnp
from jax import lax
from jax.experimental import pallas as pl
from jax.experimental.pallas import tpu as pltpu

F32 = jnp.float32
BF16 = jnp.bfloat16
MESH = pl.DeviceIdType.MESH

D_MODEL = 1024
D_FF = 2816
HEADS = 4
HEAD_DIM = 256
GROUPS = 4
GROUP_DIM = 256
POOL_WINDOWS = (2, 4, 8, 16)
IN_WIDTH = 7 * D_MODEL
ROPE_BASE = 10000.0
NORM_EPS = 1e-6
FFN_RES_WEIGHT = 0.5
ADAM_LR, ADAM_B1, ADAM_B2, ADAM_EPS, ADAM_WD, ADAM_STEP = 0.001, 0.9, 0.999, 1e-08, 0.01, 10

N_CHIPS = 4
RET_BLOCK = 256
V7X_VMEM_LIMIT = 48 * 1024 * 1024


def _cparams(sem):
    return pltpu.CompilerParams(dimension_semantics=sem, vmem_limit_bytes=V7X_VMEM_LIMIT)


def _sigmoid(x):
    return jax.nn.sigmoid(x)


_DIMS = {"nn": (((1,), (0,)), ((), ())), "nt": (((1,), (1,)), ((), ())), "tn": (((0,), (0,)), ((), ()))}


def _matmul(name, a, b, mode, m, n, k, tm, tn, tk, out_dtypes, extras=(), consts=(), epilogue=None, resident=None,
            n_outer=False):
    tm, tn, tk = min(tm, m), min(tn, n), min(tk, k)
    gi, gj, gk = m // tm, n // tn, k // tk
    assert gi * tm == m and gj * tn == n and gk * tk == k, (name, m, n, k, tm, tn, tk)
    once = dict(pipeline_mode=pl.Buffered(1))

    def spec(shape, index, **kw):
        return pl.BlockSpec(shape, (lambda j, i, kk: index(i, j, kk)) if n_outer else index, **kw)

    kw = once if resident == "a" else {}
    a_spec = (spec((tk, tm), lambda i, j, kk: (kk, i), **kw) if mode == "tn" else spec((tm, tk), lambda i, j, kk: (i, kk), **kw))
    kw = once if resident == "b" else {}
    b_spec = (spec((tn, tk), lambda i, j, kk: (j, kk), **kw) if mode == "nt" else spec((tk, tn), lambda i, j, kk: (kk, j), **kw))
    n_ex, n_out = len(extras) + len(consts), len(out_dtypes)
    dims = _DIMS[mode]

    def body(a_ref, b_ref, *rest):
        ex_refs, out_refs = rest[:n_ex], rest[n_ex:n_ex + n_out]

        def finish(acc):
            outs = (acc,) if epilogue is None else epilogue(acc, *[e[...] for e in ex_refs])
            for o_ref, o in zip(out_refs, outs):
                o_ref[...] = o.astype(o_ref.dtype)

        prod = lax.dot_general(a_ref[...], b_ref[...], dims, preferred_element_type=F32)
        if gk == 1:
            finish(prod)
        else:
            acc_ref = rest[n_ex + n_out]
            kk = pl.program_id(2)

            @pl.when(kk == 0)
            def _():
                acc_ref[...] = prod

            @pl.when(kk > 0)
            def _():
                acc_ref[...] += prod

            @pl.when(kk == gk - 1)
            def _():
                finish(acc_ref[...])

    o_spec = spec((tm, tn), lambda i, j, kk: (i, j))
    outs = pl.pallas_call(
        body, name=name, grid=(gj, gi, gk) if n_outer else (gi, gj, gk),
        in_specs=[a_spec, b_spec] + [o_spec] * len(extras) + [spec((1, tn), lambda i, j, kk: (0, j))] * len(consts),
        out_specs=[o_spec] * n_out,
        out_shape=[jax.ShapeDtypeStruct((m, n), dt) for dt in out_dtypes],
        scratch_shapes=[pltpu.VMEM((tm, tn), F32)] if gk > 1 else [],
        compiler_params=_cparams(("parallel", "parallel", "arbitrary")),
    )(a, b, *extras, *consts)
    return outs[0] if n_out == 1 else outs


def _row_spec(tm, width, col_block=0):
    return pl.BlockSpec((tm, width), lambda i: (i, col_block))


def _full_spec(shape):
    return pl.BlockSpec(shape, lambda *_: (0,) * len(shape))


def _rmsnorm_fwd(name, h, g, tm=512):
    t = h.shape[0]

    def body(h_ref, g_ref, o_ref):
        x = h_ref[...]
        r = lax.rsqrt(jnp.mean(x * x, axis=-1, keepdims=True) + NORM_EPS)
        o_ref[...] = (x * r * g_ref[...]).astype(BF16)

    return pl.pallas_call(
        body, name=name, grid=(t // tm,),
        in_specs=[_row_spec(tm, D_MODEL), _full_spec((1, D_MODEL))],
        out_specs=_row_spec(tm, D_MODEL),
        out_shape=jax.ShapeDtypeStruct((t, D_MODEL), BF16),
        compiler_params=_cparams(("parallel",)),
    )(h, g)


def _proj_norm_bwd(name, a_list, a_specs, parts, w, h, g, dres, tm):
    t = h.shape[0]
    na = len(a_list)

    def body(*refs):
        a_refs = refs[:na]
        w_ref, h_ref, g_ref, dres_ref, dh_ref, dhb_ref, dg_ref = refs[na:]
        i = pl.program_id(0)
        dn_v = None
        for which, lead, k0, k1 in parts:
            a_ref = a_refs[which]
            term = _dot(a_ref[...] if lead is None else a_ref[lead], w_ref[:, k0:k1], "nt")
            dn_v = term if dn_v is None else dn_v + term
        x = h_ref[...]
        r = lax.rsqrt(jnp.mean(x * x, axis=-1, keepdims=True) + NORM_EPS)
        xh = x * r
        dxh = dn_v * g_ref[...]
        dh = dres_ref[...] + r * (dxh - xh * jnp.mean(dxh * xh, axis=-1, keepdims=True))
        dh_ref[...] = dh
        dhb_ref[...] = dh.astype(BF16)
        part = jnp.sum(dn_v * xh, axis=0, keepdims=True)

        @pl.when(i == 0)
        def _():
            dg_ref[...] = part

        @pl.when(i > 0)
        def _():
            dg_ref[...] += part

    row = _row_spec(tm, D_MODEL)
    return pl.pallas_call(
        body, name=name, grid=(t // tm,),
        in_specs=list(a_specs) + [pl.BlockSpec(w.shape, lambda i: (0, 0), pipeline_mode=pl.Buffered(1)), row,
                                  _full_spec((1, D_MODEL)), row],
        out_specs=[row, row, _full_spec((1, D_MODEL))],
        out_shape=[jax.ShapeDtypeStruct((t, D_MODEL), F32), jax.ShapeDtypeStruct((t, D_MODEL), BF16),
                   jax.ShapeDtypeStruct((1, D_MODEL), F32)],
        compiler_params=_cparams(("arbitrary",)),
    )(*a_list, w, h, g, dres)


def _out_loss_and_grad(name, mid, w_out, h, g, target, tm=512):
    t = h.shape[0]

    def body(m_ref, w_ref, h_ref, g_ref, t_ref, dh_ref, dhb_ref, dg_ref, loss_ref):
        i = pl.program_id(0)
        x = h_ref[...] + FFN_RES_WEIGHT * _dot(m_ref[...], w_ref[...])
        gv = g_ref[...]
        r = lax.rsqrt(jnp.mean(x * x, axis=-1, keepdims=True) + NORM_EPS)
        xh = x * r
        err = xh * gv - t_ref[...]
        row = jnp.mean(err * err, axis=-1, keepdims=True)
        part_loss = 0.5 * jnp.sum(row, axis=0, keepdims=True)
        dy = err * (1.0 / D_MODEL)
        dxh = dy * gv
        dh = r * (dxh - xh * jnp.mean(dxh * xh, axis=-1, keepdims=True))
        dh_ref[...] = dh
        dhb_ref[...] = dh.astype(BF16)
        part = jnp.sum(dy * xh, axis=0, keepdims=True)

        @pl.when(i == 0)
        def _():
            dg_ref[...] = part
            loss_ref[...] = jnp.zeros(loss_ref.shape, F32) + part_loss

        @pl.when(i > 0)
        def _():
            dg_ref[...] += part
            loss_ref[...] += part_loss

    return pl.pallas_call(
        body, name=name, grid=(t // tm,),
        in_specs=[_row_spec(tm, D_FF), pl.BlockSpec((D_FF, D_MODEL), lambda i: (0, 0), pipeline_mode=pl.Buffered(1)),
                  _row_spec(tm, D_MODEL), _full_spec((1, D_MODEL)), _row_spec(tm, D_MODEL)],
        out_specs=[_row_spec(tm, D_MODEL), _row_spec(tm, D_MODEL), _full_spec((1, D_MODEL)), _full_spec((8, 128))],
        out_shape=[jax.ShapeDtypeStruct((t, D_MODEL), F32), jax.ShapeDtypeStruct((t, D_MODEL), BF16),
                   jax.ShapeDtypeStruct((1, D_MODEL), F32), jax.ShapeDtypeStruct((8, 128), F32)],
        compiler_params=_cparams(("arbitrary",)),
    )(mid, w_out, h, g, target)


def _rope_tables(t):
    half = HEAD_DIM // 2
    inv_freq = np.float32(ROPE_BASE) ** (-np.arange(half, dtype=np.float32) / np.float32(half))
    ang = (np.arange(t, dtype=np.float32)[:, None] * inv_freq[None, :].astype(np.float32)).astype(np.float32)
    return jnp.asarray(np.cos(ang.astype(np.float64)).astype(np.float32)), jnp.asarray(np.sin(ang.astype(np.float64)).astype(np.float32))


ROPE_HALF = HEAD_DIM // 2
K_SCALE = HEAD_DIM ** -0.5


def _rotate(ref, rows, hh, c, s, scale=None):
    lo, mid, hi = hh * HEAD_DIM, hh * HEAD_DIM + ROPE_HALF, (hh + 1) * HEAD_DIM
    x1, x2 = ref[rows, lo:mid].astype(F32), ref[rows, mid:hi].astype(F32)
    y = jnp.concatenate([x1 * c - x2 * s, x1 * s + x2 * c], axis=1)
    return y if scale is None else y * scale


def _unrotate_into(ref, rows, hh, dy, c, s, scale=None):
    lo, mid, hi = hh * HEAD_DIM, hh * HEAD_DIM + ROPE_HALF, (hh + 1) * HEAD_DIM
    y1, y2 = dy[:, :ROPE_HALF], dy[:, ROPE_HALF:]
    d1, d2 = y1 * c + y2 * s, y2 * c - y1 * s
    if scale is not None:
        d1, d2 = d1 * scale, d2 * scale
    ref[rows, lo:mid] = d1.astype(ref.dtype)
    ref[rows, mid:hi] = d2.astype(ref.dtype)


def _retention_tables():
    b, chunk = RET_BLOCK, 64
    gamma = 1.0 - 2.0 ** (-5.0 - np.arange(HEADS, dtype=np.float64))
    log_g = np.log(gamma)[:, None, None]
    i = np.arange(b)[:, None]
    j = np.arange(b)[None, :]
    same = (i // chunk) == (j // chunk)
    earlier = (j // chunk) < (i // chunk)
    expo = np.where(same, np.abs(i - j), np.where(earlier, i - j, 0)).astype(np.float64)
    mask = np.where(same | earlier, 1.0, 0.0)
    dmat = np.exp(log_g * expo[None]) * mask[None]
    qd = np.exp(log_g[:, :, 0] * (np.arange(b)[None, :] + 1.0))
    kd = np.exp(log_g[:, :, 0] * (b - 1.0 - np.arange(b)[None, :]))
    cd = np.exp(log_g[:, :, 0] * b) * np.ones((1, HEAD_DIM))
    as32 = lambda v: jnp.asarray(v.astype(np.float32))
    return (as32(dmat), as32(np.swapaxes(dmat, 1, 2)), as32(qd[:, :, None]), as32(kd[:, :, None]), as32(cd[:, None, :]))


def _dot(a, b, mode="nn"):
    return lax.dot_general(a, b, _DIMS[mode], preferred_element_type=F32)


GRET_BLOCK = 3


RET_SUBS = 2
RET_STEP = RET_SUBS * RET_BLOCK


def _head_specs(steps, rev=False):
    pos = (lambda n: steps - 1 - n) if rev else (lambda n: n)
    tok = pl.BlockSpec((RET_STEP, D_MODEL), lambda n: (pos(n), 0))
    blk = [pl.BlockSpec((RET_STEP, D_MODEL), lambda n, b=b: (pos(n), b)) for b in range(GRET_BLOCK + 1)]
    rope = pl.BlockSpec((RET_STEP, ROPE_HALF), lambda n: (pos(n), 0))
    tab = _full_spec((HEADS, RET_BLOCK, RET_BLOCK))
    col = _full_spec((HEADS, RET_BLOCK, 1))
    rowv = _full_spec((HEADS, 1, HEAD_DIM))
    st = pl.BlockSpec((HEADS, RET_SUBS, HEAD_DIM, HEAD_DIM), lambda n: (0, pos(n), 0, 0))
    return tok, blk, rope, tab, col, rowv, st


def _retention_fwd(name, proj, cos, sin, tables):
    t = proj.shape[0]
    nb, steps = t // RET_BLOCK, t // RET_STEP
    dmat, _, qd, kd, cd = tables
    tok, blk, rope, tab, col, rowv, st = _head_specs(steps)

    def body(q_ref, k_ref, v_ref, g_ref, c_ref, s_ref, d_ref, qd_ref, kd_ref, cd_ref, o_ref, ret_ref, st_ref, state):
        n = pl.program_id(0)

        @pl.when(n == 0)
        def _():
            state[...] = jnp.zeros(state.shape, F32)

        for sub in range(RET_SUBS):
            rows = slice(sub * RET_BLOCK, (sub + 1) * RET_BLOCK)
            cs, sn = c_ref[rows, :], s_ref[rows, :]
            for hh in range(HEADS):
                sl = slice(hh * HEAD_DIM, (hh + 1) * HEAD_DIM)
                q, k = _rotate(q_ref, rows, hh, cs, sn), _rotate(k_ref, rows, hh, cs, sn, K_SCALE)
                v = v_ref[rows, sl].astype(BF16)
                s = _dot(q.astype(BF16), k.astype(BF16), "nt") * d_ref[hh]
                stb = state[hh].astype(BF16)
                st_ref[hh, sub] = stb
                o = _dot(s.astype(BF16), v) + _dot((q * qd_ref[hh]).astype(BF16), stb)
                o_ref[rows, sl] = o
                rn = o * lax.rsqrt(jnp.mean(o * o, axis=-1, keepdims=True) + NORM_EPS)
                g = g_ref[rows, sl].astype(F32)
                ret_ref[rows, sl] = (rn * (g * _sigmoid(g))).astype(BF16)
                state[hh] = state[hh] * cd_ref[hh] + _dot((k * kd_ref[hh]).astype(BF16), v, "tn")

    return pl.pallas_call(
        body, name=name, grid=(steps,),
        in_specs=blk + [rope, rope, tab, col, col, rowv],
        out_specs=[tok, tok, st],
        out_shape=[jax.ShapeDtypeStruct((t, D_MODEL), F32), jax.ShapeDtypeStruct((t, D_MODEL), BF16),
                   jax.ShapeDtypeStruct((HEADS, nb, HEAD_DIM, HEAD_DIM), BF16)],
        scratch_shapes=[pltpu.VMEM((HEADS, HEAD_DIM, HEAD_DIM), F32)],
        compiler_params=_cparams(("arbitrary",)),
    )(proj, proj, proj, proj, cos, sin, dmat, qd, kd, cd)


def _retention_bwd(name, dru, w_ru, o, proj, cos, sin, states, tables):
    t = proj.shape[0]
    steps = t // RET_STEP
    dmat, dmat_t, qd, kd, cd = tables
    tok, blk, rope, tab, col, rowv, st = _head_specs(steps, rev=True)

    def body(dru_ref, wru_ref, o_ref, q_ref, k_ref, v_ref, g_ref, c_ref, s_ref, st_ref, d_ref, dt_ref, qd_ref, kd_ref, cd_ref,
             dq_ref, dk_ref, dv_ref, dg_ref, gstate):
        n = pl.program_id(0)

        @pl.when(n == 0)
        def _():
            gstate[...] = jnp.zeros(gstate.shape, F32)

        for sub in reversed(range(RET_SUBS)):
            rows = slice(sub * RET_BLOCK, (sub + 1) * RET_BLOCK)
            cs, sn = c_ref[rows, :], s_ref[rows, :]
            dret = _dot(dru_ref[rows, :], wru_ref[...], "nt")
            for hh in range(HEADS):
                sl = slice(hh * HEAD_DIM, (hh + 1) * HEAD_DIM)
                o_v, g, dr = o_ref[rows, sl], g_ref[rows, sl].astype(F32), dret[:, sl]
                sg = _sigmoid(g)
                r = lax.rsqrt(jnp.mean(o_v * o_v, axis=-1, keepdims=True) + NORM_EPS)
                rn = o_v * r
                d_rn = dr * (g * sg)
                dg_ref[rows, sl] = (dr * rn * (sg * (1.0 + g * (1.0 - sg)))).astype(BF16)
                d_o = r * (d_rn - rn * jnp.mean(d_rn * rn, axis=-1, keepdims=True))
                dob = d_o.astype(BF16)

                q, k = _rotate(q_ref, rows, hh, cs, sn), _rotate(k_ref, rows, hh, cs, sn, K_SCALE)
                v = v_ref[rows, sl].astype(BF16)
                qb, kb = q.astype(BF16), k.astype(BF16)
                qdv, kdv = qd_ref[hh], kd_ref[hh]
                s_t = (_dot(kb, qb, "nt") * dt_ref[hh]).astype(BF16)
                p_t = (_dot(v, dob, "nt") * dt_ref[hh]).astype(BF16)
                p = (_dot(dob, v, "nt") * d_ref[hh]).astype(BF16)
                stb = st_ref[hh, sub]
                gb = gstate[hh].astype(BF16)
                _unrotate_into(dq_ref, rows, hh, _dot(p, kb) + _dot(dob, stb, "nt") * qdv, cs, sn)
                _unrotate_into(dk_ref, rows, hh, _dot(p_t, qb) + _dot(v, gb, "nt") * kdv, cs, sn, K_SCALE)
                dv_ref[rows, sl] = (_dot(s_t, dob) + _dot((k * kdv).astype(BF16), gb)).astype(BF16)
                gstate[hh] = gstate[hh] * cd_ref[hh] + _dot((q * qdv).astype(BF16), dob, "tn")

    return pl.pallas_call(
        body, name=name, grid=(steps,),
        in_specs=[tok, pl.BlockSpec((D_MODEL, D_MODEL), lambda n: (0, 0), pipeline_mode=pl.Buffered(1)), tok] + blk
                 + [rope, rope, st, tab, tab, col, col, rowv],
        out_specs=[tok, tok, tok, tok],
        out_shape=[jax.ShapeDtypeStruct((t, D_MODEL), BF16)] * 4,
        scratch_shapes=[pltpu.VMEM((HEADS, HEAD_DIM, HEAD_DIM), F32)],
        compiler_params=_cparams(("arbitrary",)),
    )(dru, w_ru, o, proj, proj, proj, proj, cos, sin, states, dmat, dmat_t, qd, kd, cd)


POOL_TILE = 256
POOL_SUBS = 2
POOL_STEP = POOL_SUBS * POOL_TILE


def _pool_tables():
    b = POOL_TILE
    tt = np.arange(b)[:, None]
    jj = np.arange(b)[None, :]
    cur, prev = [], []
    for w in POOL_WINDOWS:
        cur.append(((tt - jj >= 0) & (tt - jj <= w - 1)).astype(np.float32))
        prev.append((tt - (jj - b) <= w - 1).astype(np.float32))
    cur, prev = np.stack(cur), np.stack(prev)
    as16 = lambda v: jnp.asarray(v, dtype=BF16)
    return as16(cur), as16(prev), as16(np.swapaxes(cur, 1, 2)), as16(np.swapaxes(prev, 1, 2))


def _split2(x):
    hi = x.astype(BF16)
    return hi, (x - hi.astype(F32)).astype(BF16)


POOL_BLOCK = 4


def _pool_count(n, window):
    tpos = n * POOL_TILE + lax.broadcasted_iota(jnp.int32, (POOL_TILE, 1), 0)
    return jnp.minimum(tpos + 1, window).astype(F32)


def _pool_fwd(name, proj, pool_w, scale, tables):
    t = proj.shape[0]
    steps = t // POOL_STEP
    mc, mp, _, _ = tables
    tab = _full_spec((GROUPS, POOL_TILE, POOL_TILE))
    row = _row_spec(POOL_STEP, D_MODEL)

    def body(pc_ref, pp_ref, mc_ref, mp_ref, w_ref, sc_ref, pm_ref, mix_ref, po_ref):
        n = pl.program_id(0)
        for sub in range(POOL_SUBS):
            rows = slice(sub * POOL_TILE, (sub + 1) * POOL_TILE)
            tile = n * POOL_SUBS + sub
            for g, window in enumerate(POOL_WINDOWS):
                sl = slice(g * GROUP_DIM, (g + 1) * GROUP_DIM)
                p = pc_ref[rows, sl]
                if sub == 0:
                    before = jnp.where(n > 0, _dot(mp_ref[g], pp_ref[:, sl]), 0.0)
                else:
                    before = _dot(mp_ref[g], pc_ref[(sub - 1) * POOL_TILE:sub * POOL_TILE, sl])
                pm = ((_dot(mc_ref[g], p) + before) / _pool_count(tile, window) - p.astype(F32)).astype(BF16)
                pm_ref[rows, sl] = pm
                mixed = _dot(pm, w_ref[g])
                mix_ref[rows, sl] = mixed
                po_ref[rows, sl] = (mixed * sc_ref[:, sl]).astype(BF16)

    return pl.pallas_call(
        body, name=name, grid=(steps,),
        in_specs=[_row_spec(POOL_STEP, D_MODEL, POOL_BLOCK),
                  pl.BlockSpec((POOL_TILE, D_MODEL), lambda n: (jnp.maximum(n * POOL_SUBS - 1, 0), POOL_BLOCK)),
                  tab, tab, _full_spec((GROUPS, GROUP_DIM, GROUP_DIM)), _full_spec((1, D_MODEL))],
        out_specs=[row] * 3,
        out_shape=[jax.ShapeDtypeStruct((t, D_MODEL), BF16), jax.ShapeDtypeStruct((t, D_MODEL), F32),
                   jax.ShapeDtypeStruct((t, D_MODEL), BF16)],
        compiler_params=_cparams(("parallel",)),
    )(proj, proj, mc, mp, pool_w, scale)


def _pool_bwd(name, dpu, w_pu, pm, mixed, pool_w, scale, tables):
    t = dpu.shape[0]
    steps = t // POOL_STEP
    _, _, mct, mpt = tables
    cur = pl.BlockSpec((POOL_STEP, D_MODEL), lambda n: (steps - 1 - n, 0))
    tab = _full_spec((GROUPS, POOL_TILE, POOL_TILE))
    wspec = _full_spec((GROUPS, GROUP_DIM, GROUP_DIM))
    sspec = _full_spec((1, D_MODEL))

    def body(dpu_ref, wpu_ref, pm_ref, mix_ref, mct_ref, mpt_ref, w_ref, sc_ref, dp_ref, dw_ref, ds_ref, later):
        n = pl.program_id(0)

        @pl.when(n == 0)
        def _():
            dw_ref[...] = jnp.zeros(dw_ref.shape, F32)
            ds_ref[...] = jnp.zeros(ds_ref.shape, F32)
            later[...] = jnp.zeros(later.shape, F32)

        for sub in reversed(range(POOL_SUBS)):
            rows = slice(sub * POOL_TILE, (sub + 1) * POOL_TILE)
            tile = (steps - 1 - n) * POOL_SUBS + sub
            dpo = _dot(dpu_ref[rows, :], wpu_ref[...], "nt")
            for g, window in enumerate(POOL_WINDOWS):
                sl = slice(g * GROUP_DIM, (g + 1) * GROUP_DIM)
                dc, sc = dpo[:, sl], sc_ref[:, sl]
                dmix = (dc * sc).astype(BF16)
                dpm = _dot(dmix, w_ref[g], "nt")
                e = dpm / _pool_count(tile, window)
                e_hi, e_lo = _split2(e)
                f_hi, f_lo = _split2(later[g])
                mctv, mptv = mct_ref[g], mpt_ref[g]
                back = _dot(mctv, e_hi) + _dot(mctv, e_lo)
                after = _dot(mptv, f_hi) + _dot(mptv, f_lo)
                dp_ref[rows, sl] = (back + after - dpm).astype(BF16)
                later[g] = e
                dw_ref[g] += _dot(pm_ref[rows, sl], dmix, "tn")
                ds_ref[:, sl] += jnp.sum(dc * mix_ref[rows, sl], axis=0, keepdims=True)

    return pl.pallas_call(
        body, name=name, grid=(steps,),
        in_specs=[cur, pl.BlockSpec((D_MODEL, D_MODEL), lambda n: (0, 0), pipeline_mode=pl.Buffered(1)), cur, cur, tab, tab,
                  wspec, sspec],
        out_specs=[cur, wspec, sspec],
        out_shape=[jax.ShapeDtypeStruct((t, D_MODEL), BF16), jax.ShapeDtypeStruct((GROUPS, GROUP_DIM, GROUP_DIM), F32),
                   jax.ShapeDtypeStruct((1, D_MODEL), F32)],
        scratch_shapes=[pltpu.VMEM((GROUPS, POOL_TILE, GROUP_DIM), F32)],
        compiler_params=_cparams(("arbitrary",)),
    )(dpu, w_pu, pm, mixed, mct, mpt, pool_w, scale)


GATE0_BLOCK, GATE1_BLOCK = 5, 6


def _merge_fwd(name, ret, po, w_ru, w_pu, w_out, proj, bias, h, next_g, tm=512):
    t = ret.shape[0]

    def body(r_ref, p_ref, wr_ref, wp_ref, wo_ref, g0_ref, g1_ref, b_ref, h_ref, ng_ref, m_ref, ru_ref, pu_ref, ho_ref, n_ref):
        ru = _dot(r_ref[...], wr_ref[...])
        pu = _dot(p_ref[...], wp_ref[...])
        ru_ref[...] = ru
        pu_ref[...] = pu
        merged = (_sigmoid(g0_ref[...].astype(F32) + b_ref[0:1, :]) * ru
                  + _sigmoid(g1_ref[...].astype(F32) + b_ref[1:2, :]) * pu).astype(BF16)
        m_ref[...] = merged
        h_new = h_ref[...] + _dot(merged, wo_ref[...])
        ho_ref[...] = h_new
        n_ref[...] = _normed(h_new, ng_ref[...]).astype(BF16)

    row = _row_spec(tm, D_MODEL)
    wspec = pl.BlockSpec((D_MODEL, D_MODEL), lambda i: (0, 0), pipeline_mode=pl.Buffered(1))
    return pl.pallas_call(
        body, name=name, grid=(t // tm,),
        in_specs=[row, row, wspec, wspec, wspec, _row_spec(tm, D_MODEL, GATE0_BLOCK), _row_spec(tm, D_MODEL, GATE1_BLOCK),
                  _full_spec((2, D_MODEL)), row, _full_spec((1, D_MODEL))],
        out_specs=[row] * 5,
        out_shape=[jax.ShapeDtypeStruct((t, D_MODEL), BF16), jax.ShapeDtypeStruct((t, D_MODEL), F32),
                   jax.ShapeDtypeStruct((t, D_MODEL), F32), jax.ShapeDtypeStruct((t, D_MODEL), F32),
                   jax.ShapeDtypeStruct((t, D_MODEL), BF16)],
        compiler_params=_cparams(("parallel",)),
    )(ret, po, w_ru, w_pu, w_out, proj, proj, bias, h, next_g)


def _merge_bwd(name, dh_b, w_out, ru, pu, proj, bias, tm=512):
    t = dh_b.shape[0]

    def body(dh_ref, wo_ref, ru_ref, pu_ref, g0_ref, g1_ref, b_ref, dru_ref, dpu_ref, dg0_ref, dg1_ref, db_ref):
        i = pl.program_id(0)
        d = _dot(dh_ref[...], wo_ref[...], "nt")
        s0 = _sigmoid(g0_ref[...].astype(F32) + b_ref[0:1, :])
        s1 = _sigmoid(g1_ref[...].astype(F32) + b_ref[1:2, :])
        dru_ref[...] = (d * s0).astype(BF16)
        dpu_ref[...] = (d * s1).astype(BF16)
        dg0 = d * ru_ref[...] * (s0 * (1.0 - s0))
        dg1 = d * pu_ref[...] * (s1 * (1.0 - s1))
        dg0_ref[...] = dg0.astype(BF16)
        dg1_ref[...] = dg1.astype(BF16)
        part0 = jnp.sum(dg0, axis=0, keepdims=True)
        part1 = jnp.sum(dg1, axis=0, keepdims=True)

        @pl.when(i == 0)
        def _():
            db_ref[0:1, :] = part0
            db_ref[1:2, :] = part1

        @pl.when(i > 0)
        def _():
            db_ref[0:1, :] += part0
            db_ref[1:2, :] += part1

    row = _row_spec(tm, D_MODEL)
    return pl.pallas_call(
        body, name=name, grid=(t // tm,),
        in_specs=[row, pl.BlockSpec((D_MODEL, D_MODEL), lambda i: (0, 0), pipeline_mode=pl.Buffered(1)), row, row,
                  _row_spec(tm, D_MODEL, GATE0_BLOCK), _row_spec(tm, D_MODEL, GATE1_BLOCK), _full_spec((2, D_MODEL))],
        out_specs=[row, row, row, row, _full_spec((2, D_MODEL))],
        out_shape=[jax.ShapeDtypeStruct((t, D_MODEL), BF16)] * 4 + [jax.ShapeDtypeStruct((2, D_MODEL), F32)],
        compiler_params=_cparams(("arbitrary",)),
    )(dh_b, w_out, ru, pu, proj, proj, bias)


def _half_scale(acc):
    return (FFN_RES_WEIGHT * acc,)


def _normed(h, g):
    return h * lax.rsqrt(jnp.mean(h * h, axis=-1, keepdims=True) + NORM_EPS) * g


def _residual_half_norm(acc, res, g):
    h = res + FFN_RES_WEIGHT * acc
    return h, _normed(h, g)


FF_TILE = D_FF // 2
DW_TILE = 256
SAVED_FF_DTYPE = BF16
FF_CHUNKS = ((0, 512), (512, 1024), (1024, FF_TILE))


def _ffn_in(name, nrm, w_in, tm=512):
    t = nrm.shape[0]
    nj = D_FF // FF_TILE

    def body(n_ref, wg_ref, wu_ref, a_ref, mid_ref):
        nv = n_ref[...]
        for c0, c1 in FF_CHUNKS:
            gate = _dot(nv, wg_ref[:, c0:c1])
            up = _dot(nv, wu_ref[:, c0:c1])
            s = _sigmoid(gate)
            silu = gate * s
            a_ref[0, :, c0:c1] = (FFN_RES_WEIGHT * up * (s * (1.0 + gate * (1.0 - s)))).astype(a_ref.dtype)
            a_ref[1, :, c0:c1] = (FFN_RES_WEIGHT * silu).astype(a_ref.dtype)
            mid_ref[:, c0:c1] = (silu * up).astype(BF16)

    return pl.pallas_call(
        body, name=name, grid=(nj, t // tm),
        in_specs=[pl.BlockSpec((tm, D_MODEL), lambda j, i: (i, 0)),
                  pl.BlockSpec((D_MODEL, FF_TILE), lambda j, i: (0, j)),
                  pl.BlockSpec((D_MODEL, FF_TILE), lambda j, i: (0, j + nj))],
        out_specs=[pl.BlockSpec((2, tm, FF_TILE), lambda j, i: (0, i, j)), pl.BlockSpec((tm, FF_TILE), lambda j, i: (i, j))],
        out_shape=[jax.ShapeDtypeStruct((2, t, D_FF), SAVED_FF_DTYPE), jax.ShapeDtypeStruct((t, D_FF), BF16)],
        compiler_params=_cparams(("parallel", "parallel")),
    )(nrm, w_in, w_in)


def _ffn_dact(name, dout_b, w_out, a, tm=512):
    t = dout_b.shape[0]

    def body(d_ref, w_ref, a_ref, da_ref):
        dv = d_ref[...]
        for c0, c1 in FF_CHUNKS:
            dm = _dot(dv, w_ref[c0:c1, :], "nt")
            da_ref[0, :, c0:c1] = (dm * a_ref[0, :, c0:c1].astype(F32)).astype(BF16)
            da_ref[1, :, c0:c1] = (dm * a_ref[1, :, c0:c1].astype(F32)).astype(BF16)

    blk = pl.BlockSpec((2, tm, FF_TILE), lambda j, i: (0, i, j))
    return pl.pallas_call(
        body, name=name, grid=(D_FF // FF_TILE, t // tm),
        in_specs=[pl.BlockSpec((tm, D_MODEL), lambda j, i: (i, 0)), pl.BlockSpec((FF_TILE, D_MODEL), lambda j, i: (j, 0)), blk],
        out_specs=blk,
        out_shape=jax.ShapeDtypeStruct((2, t, D_FF), BF16),
        compiler_params=_cparams(("parallel", "parallel")),
    )(dout_b, w_out, a)


def _ffn_fwd(tag, h, nrm, get_w_in, get_w_out, finish):
    t = h.shape[0]
    w_in = get_w_in(nrm)
    a, mid = _ffn_in(f"{tag}_in", nrm, w_in, tm=min(512, t))
    w_out = get_w_out(mid)
    return finish(mid, w_out), (nrm, a, mid, w_in, w_out)


def _ffn_bwd(tag, h, g, saved, dout, dout_b, on_grads, flush):
    t = h.shape[0]
    nrm, a, mid, w_in, w_out = saved
    d_w_out = _matmul(f"{tag}_dwout", mid, dout_b, "tn", D_FF, D_MODEL, t, DW_TILE, D_MODEL, t, [BF16], epilogue=_half_scale,
                      resident="b")
    da = _ffn_dact(f"{tag}_dact", dout_b, w_out, a, tm=min(512, t))
    nj = D_FF // DW_TILE
    d_w_in = _dw_resident(f"{tag}_dwin", nrm, [da], [pl.BlockSpec((None, t, DW_TILE), lambda s: (s // nj, 0, s % nj))],
                          2 * nj, None, DW_TILE)
    tie = on_grads({f"{tag}_w_in": d_w_in, f"{tag}_w_out": d_w_out})
    tm = min(256, t)
    dh, dh_b, dg = _proj_norm_bwd(f"{tag}_dn", [da], [pl.BlockSpec((2, tm, D_FF), lambda i: (0, i, 0))],
                                  ((0, 0, 0, D_FF), (0, 1, D_FF, 2 * D_FF)), w_in, h, g if tie is None else g + tie, dout, tm)
    return dh, dh_b, dg, flush(dh)


def _dw_resident(name, u, pieces, piece_specs, n_tiles, which_piece, tn):
    t = u.shape[0]
    npc = len(pieces)

    def body(*refs):
        u_ref, p_refs, o_ref, ut_ref = refs[0], refs[1:1 + npc], refs[1 + npc], refs[2 + npc]
        s = pl.program_id(0)

        @pl.when(s == 0)
        def _():
            ut_ref[...] = u_ref[...].T

        if npc == 1:
            o_ref[...] = _dot(ut_ref[...], p_refs[0][...]).astype(BF16)
        for which in range(npc if npc > 1 else 0):
            @pl.when(which_piece(s) == which)
            def _(which=which):
                o_ref[...] = _dot(ut_ref[...], p_refs[which][...]).astype(BF16)

    return pl.pallas_call(
        body, name=name, grid=(n_tiles,),
        in_specs=[pl.BlockSpec((t, D_MODEL), lambda s: (0, 0), pipeline_mode=pl.Buffered(1))] + list(piece_specs),
        out_specs=pl.BlockSpec((D_MODEL, tn), lambda s: (0, s)),
        out_shape=jax.ShapeDtypeStruct((D_MODEL, n_tiles * tn), BF16),
        scratch_shapes=[pltpu.VMEM((D_MODEL, t), BF16)],
        compiler_params=_cparams(("arbitrary",)),
    )(u, *pieces)


def _mix_dwin(name, u, pieces, tn=256):
    t = u.shape[0]
    nj = D_MODEL // tn
    specs = [pl.BlockSpec((t, tn), lambda s, k=k: (0, jnp.clip(s - k * nj, 0, nj - 1))) for k in range(len(pieces))]
    return _dw_resident(name, u, pieces, specs, len(pieces) * nj, lambda s: s // nj, tn)


def _local_step(x, target, vec, get_w, relay_w, on_grads, flush):
    t = x.shape[0]
    cos, sin = _rope_tables(t)
    rtab = _retention_tables()
    ptab = _pool_tables()
    w = {}

    def getter(group, name):
        def get(after):
            if name not in w:
                w.update(get_w(group, after))
            return w[name]
        return get

    nrm1 = _rmsnorm_fwd("ffn1_norm", x, vec["norm_ffn1"])
    def out_and_norm(mid, w_out):
        return _matmul("ffn1_out", mid, w_out, "nn", t, D_MODEL, D_FF, 512, D_MODEL, D_FF, [F32, BF16],
                       extras=(x,), consts=(vec["norm_mix"],), epilogue=_residual_half_norm)

    (h1, u), s1 = _ffn_fwd("ffn1", x, nrm1, getter(0, "ffn1_w_in"), getter(1, "ffn1_w_out"), out_and_norm)
    w.update(get_w(2, u))
    relay_w(3, w["w_in"])
    proj = _matmul("mix_in", u, w["w_in"], "nn", t, IN_WIDTH, D_MODEL, 1024, 1024, D_MODEL, [BF16], n_outer=True)
    w.update(get_w(3, proj))
    o, ret, states = _retention_fwd("retention", proj, cos, sin, rtab)
    pm, mixed, po = _pool_fwd("pool", proj, w["pool_w"], vec["pool_scale"], ptab)
    relay_w(4, po)
    merged, ru, pu, h2, nrm2 = _merge_fwd("merge", ret, po, w["w_ret_up"], w["w_pool_up"], w["w_out"], proj, w["gate_bias"],
                                          h1, vec["norm_ffn2"], tm=min(512, t))
    def out_and_loss(mid, w_out):
        return _out_loss_and_grad("ffn2_out_loss", mid, w_out, h2, vec["norm_final"], target, tm=min(512, t))

    (dh3, dh3_b, dg_final, loss), s2 = _ffn_fwd("ffn2", h2, nrm2, getter(4, "ffn2_w_in"), getter(4, "ffn2_w_out"), out_and_loss)

    def tied(v, tie):
        return v if tie is None else v + tie

    dh2, dh2_b, dg_ffn2, tie = _ffn_bwd("ffn2", h2, vec["norm_ffn2"], s2, dh3, dh3_b, on_grads, flush)
    def square_dw(name, act, grad):
        return _matmul(name, act, grad, "tn", D_MODEL, D_MODEL, t, D_MODEL, D_MODEL, 1024, [BF16])

    d_w_out = square_dw("mix_dwout", merged, dh2_b)
    dru, dpu, dg0, dg1, d_bias = _merge_bwd("merge_bwd", dh2_b, w["w_out"], ru, pu, proj, tied(w["gate_bias"], tie))
    d_w_ru = square_dw("mix_dwru", ret, dru)
    d_w_pu = square_dw("mix_dwpu", po, dpu)
    dp, d_pool_w, d_scale = _pool_bwd("pool_bwd", dpu, w["w_pool_up"], pm, mixed, w["pool_w"], vec["pool_scale"], ptab)
    dq, dk, dv, dgr = _retention_bwd("retention_bwd", dru, w["w_ret_up"], o, proj, cos, sin, states, rtab)
    dproj = [dq, dk, dv, dgr, dp, dg0, dg1]
    d_w_in = _mix_dwin("mix_dwin", u, dproj)
    tie = on_grads(dict(w_in=d_w_in, pool_w=d_pool_w.astype(BF16), w_ret_up=d_w_ru, w_pool_up=d_w_pu, w_out=d_w_out))
    tm = min(256, t)
    dh1, dh1_b, dg_mix = _proj_norm_bwd("mix_du", dproj, [_row_spec(tm, D_MODEL)] * len(dproj),
                                        [(k, None, k * D_MODEL, (k + 1) * D_MODEL) for k in range(len(dproj))],
                                        w["w_in"], h1, tied(vec["norm_mix"], tie), dh2, tm)
    tie = flush(dh1)
    dx, _, dg_ffn1, _ = _ffn_bwd("ffn1", x, tied(vec["norm_ffn1"], tie), s1, dh1, dh1_b, on_grads, flush)

    small = dict(norm_ffn1=dg_ffn1, norm_mix=dg_mix, gate_bias=d_bias, pool_scale=d_scale, norm_ffn2=dg_ffn2,
                 norm_final=dg_final)
    return loss[0, 0], dx, small


BIG = ("ffn1_w_in", "ffn1_w_out", "w_in", "pool_w", "w_ret_up", "w_pool_up", "w_out", "ffn2_w_in", "ffn2_w_out")
KIND = dict(ffn1_w_in="col", ffn1_w_out="row", w_in="col", pool_w="pool", w_ret_up="row", w_pool_up="row", w_out="row",
            ffn2_w_in="col", ffn2_w_out="row", gate_bias="col")
ANY = pl.BlockSpec(memory_space=pl.ANY)


def _place():
    x, y, c = lax.axis_index("x"), lax.axis_index("y"), lax.axis_index("c")
    chips = [(1 - x, y), (x, 1 - y), (1 - x, 1 - y)]
    return x, y, c, chips


def _full_view_shape(kind, local_shape):
    if kind == "col":
        return (2, local_shape[0] // 2, N_CHIPS * local_shape[1])
    if kind == "row":
        return (N_CHIPS, 2, local_shape[0] // 2, local_shape[1])
    return (GROUPS, N_CHIPS, 2, local_shape[1] // 2, local_shape[2])


def _local_view(kind, arr):
    if kind == "pool":
        return arr.reshape(GROUPS, 2, arr.shape[1] // 2, arr.shape[2])
    return arr.reshape(2, arr.shape[0] // 2, arr.shape[1])


def _blk(kind, ref, s, c):
    if kind == "col":
        cs = ref.shape[2] // N_CHIPS
        return ref.at[c, :, pl.ds(pl.multiple_of(s * cs, 128), cs)]
    if kind == "row":
        return ref.at[s, c]
    return ref.at[:, s, c]


def _half(kind, ref, c):
    return ref.at[:, c] if kind == "pool" else ref.at[c]


def _shard(kind, ref, s):
    if kind == "col":
        cs = ref.shape[2] // N_CHIPS
        return ref.at[:, :, pl.ds(pl.multiple_of(s * cs, 128), cs)]
    if kind == "row":
        return ref.at[s]
    return ref.at[:, s]


HBM = pl.BlockSpec(memory_space=pltpu.HBM)
SEM = pl.BlockSpec(memory_space=pltpu.SEMAPHORE)
EFFECT = pltpu.SideEffectType.DATAFLOW_SIDE_EFFECTING
WEIGHT_GROUPS = (("gate_bias", "ffn1_w_in"), ("ffn1_w_out",), ("w_in",), ("pool_w", "w_ret_up", "w_pool_up", "w_out"),
                 ("ffn2_w_in", "ffn2_w_out"))
GRAD_GROUPS = (("ffn2_w_in", "ffn2_w_out"), ("w_in", "pool_w", "w_ret_up", "w_pool_up", "w_out"), ("ffn1_w_in", "ffn1_w_out"))


def _hbm(a):
    return pltpu.with_memory_space_constraint(a, pltpu.HBM)


def _natural(kind, o):
    if kind == "col":
        return o.reshape(o.shape[0] * o.shape[1], o.shape[2])
    if kind == "row":
        return o.reshape(-1, o.shape[3])
    return o.reshape(GROUPS, -1, o.shape[4])


def _ici_copy(kind, loc, full, j, chips, s, c, send_sem, recv_sem):
    px, py = chips[j]
    return (pltpu.make_async_remote_copy(src_ref=_half(kind, loc, c), dst_ref=_blk(kind, full, s, c), send_sem=send_sem,
                                         recv_sem=recv_sem, device_id=(px, py, c), device_id_type=MESH),
            pltpu.make_async_remote_copy(src_ref=_half(kind, loc, c), dst_ref=_blk(kind, full, 2 * px + py, c), send_sem=send_sem,
                                         recv_sem=recv_sem, device_id=(px, py, c), device_id_type=MESH))


def _gather_start(tag, group_ids, shards):
    grps = [WEIGHT_GROUPS[g] for g in group_ids]
    names = [nm for grp in grps for nm in grp]
    kinds = [KIND[nm] for nm in names]
    n, ng = len(names), len(grps)
    locs = [_hbm(_local_view(KIND[nm], shards[nm])) for nm in names]
    lands = [_hbm(lax.empty(_full_view_shape(KIND[nm], shards[nm].shape), shards[nm].dtype)) for nm in names]
    first = np.cumsum([0] + [len(grp) for grp in grps])

    def body(*refs):
        loc, full = refs[:n], refs[n:2 * n]
        send_sems, recv_sems = refs[2 * n:2 * n + ng], refs[2 * n + ng:2 * n + 2 * ng]
        token = refs[-1]
        x, y, c, chips = _place()
        s = 2 * x + y
        for g in range(ng):
            for a in range(first[g], first[g + 1]):
                for j in range(3):
                    k = 3 * (a - first[g]) + j
                    _ici_copy(kinds[a], loc[a], full[a], j, chips, s, c, send_sems[g].at[k], recv_sems[g].at[k])[0].start()
        token[...] = jnp.zeros(token.shape, F32)

    sem_shapes = [pltpu.SemaphoreType.DMA((3 * len(grp),)) for grp in grps]
    outs = pl.pallas_call(
        body, name=f"gather_start_{tag}",
        in_specs=[HBM] * (2 * n),
        out_specs=[SEM] * (2 * ng) + [HBM] * (2 * n) + [pl.BlockSpec(memory_space=pltpu.VMEM)],
        out_shape=sem_shapes + sem_shapes + [pltpu.HBM(a.shape, a.dtype) for a in locs + lands] + [jax.ShapeDtypeStruct((8, 128), F32)],
        input_output_aliases={i: 2 * ng + i for i in range(2 * n)},
        compiler_params=pltpu.CompilerParams(has_side_effects=EFFECT),
    )(*locs, *lands)
    send_sems, recv_sems = outs[:ng], outs[ng:2 * ng]
    locs_t, lands_t = outs[2 * ng:2 * ng + n], outs[2 * ng + n:2 * ng + 2 * n]
    groups = {}
    for k, g in enumerate(group_ids):
        sl = slice(first[k], first[k + 1])
        groups[g] = (send_sems[k], recv_sems[k], list(locs_t[sl]), list(lands_t[sl]))
    return groups, outs[-1]


def _forward_copies(kinds, loc, full, send_sems, recv_sems):
    x, y, c, chips = _place()
    s = 2 * x + y

    def remote(a, k, src, dst):
        return pltpu.make_async_remote_copy(src_ref=src, dst_ref=dst, send_sem=send_sems.at[4 * a + k],
                                            recv_sem=recv_sems.at[4 * a + k], device_id=(x, y, 1 - c), device_id_type=MESH)

    sends, arrivals = [], []
    for a, kind in enumerate(kinds):
        for j, (px, py) in enumerate(chips):
            theirs, from_sib = _blk(kind, full[a], 2 * px + py, c), _blk(kind, full[a], 2 * px + py, 1 - c)
            sends.append(remote(a, j, theirs, theirs))
            arrivals.append(remote(a, j, from_sib, from_sib))
        own = _shard(kind, full[a], s)
        sends.append(remote(a, 3, loc[a], own))
        arrivals.append(remote(a, 3, own, own))
    return sends, arrivals


def _gather_relay(g, group, after):
    names = WEIGHT_GROUPS[g]
    kinds = [KIND[nm] for nm in names]
    m = len(names)
    ici_send, ici_recv, locs, lands = group

    def body(*refs):
        loc, full = refs[:m], refs[m:2 * m]
        ici_s, ici_r = refs[2 * m], refs[2 * m + 1]
        d2d_s, d2d_r = refs[2 * m + 2 + len(after)], refs[2 * m + 3 + len(after)]
        x, y, c, chips = _place()
        for a in range(m):
            for j in range(3):
                sent, landed = _ici_copy(kinds[a], loc[a], full[a], j, chips, 2 * x + y, c, ici_s.at[3 * a + j], ici_r.at[3 * a + j])
                sent.wait_send()
                landed.wait_recv()
        for cp in _forward_copies(kinds, loc, full, d2d_s, d2d_r)[0]:
            cp.start()

    sem_shape = pltpu.SemaphoreType.DMA((4 * m,))
    outs = pl.pallas_call(
        body, name=f"gather_relay_{g}",
        in_specs=[HBM] * (2 * m) + [SEM, SEM] + [ANY] * len(after), out_specs=[SEM, SEM] + [HBM] * (2 * m),
        out_shape=[sem_shape, sem_shape] + [pltpu.HBM(a.shape, a.dtype) for a in locs + lands],
        input_output_aliases={i: 2 + i for i in range(2 * m)},
        compiler_params=pltpu.CompilerParams(has_side_effects=EFFECT),
    )(*locs, *lands, ici_send, ici_recv, *after)
    return outs[0], outs[1], list(outs[2:2 + m]), list(outs[2 + m:2 + 2 * m])


def _gather_land(g, state, after):
    names = WEIGHT_GROUPS[g]
    kinds = [KIND[nm] for nm in names]
    m = len(names)
    d2d_send, d2d_recv, locs, lands = state

    def body(*refs):
        sends, arrivals = _forward_copies(kinds, refs[:m], refs[m:2 * m], refs[2 * m], refs[2 * m + 1])
        for cp in sends:
            cp.wait_send()
        for cp in arrivals:
            cp.wait_recv()

    outs = pl.pallas_call(
        body, name=f"gather_land_{g}",
        in_specs=[HBM] * (2 * m) + [SEM, SEM] + [ANY] * len(after), out_specs=[HBM] * (2 * m),
        out_shape=[pltpu.HBM(a.shape, a.dtype) for a in locs + lands],
        input_output_aliases={i: i for i in range(2 * m)},
        compiler_params=pltpu.CompilerParams(has_side_effects=EFFECT),
    )(*locs, *lands, d2d_send, d2d_recv, *after)
    return {nm: _natural(k, o) for nm, k, o in zip(names, kinds, outs[m:])}


def _gather_finish(g, group, after):
    names = WEIGHT_GROUPS[g]
    kinds = [KIND[nm] for nm in names]
    m = len(names)
    send_sem, recv_sem, locs, lands = group

    def wait_body(*refs):
        loc, full = refs[:m], refs[m:2 * m]
        send_sems, recv_sems = refs[2 * m], refs[2 * m + 1]
        x, y, c, chips = _place()
        s = 2 * x + y
        for a in range(m):
            for j in range(3):
                k = 3 * a + j
                sent, landed = _ici_copy(kinds[a], loc[a], full[a], j, chips, s, c, send_sems.at[k], recv_sems.at[k])
                sent.wait_send()
                landed.wait_recv()

    outs = pl.pallas_call(
        wait_body, name=f"gather_wait_{g}",
        in_specs=[HBM] * (2 * m) + [SEM, SEM] + [ANY] * len(after), out_specs=[HBM] * (2 * m),
        out_shape=[pltpu.HBM(a.shape, a.dtype) for a in locs + lands],
        input_output_aliases={i: i for i in range(2 * m)},
        compiler_params=pltpu.CompilerParams(has_side_effects=EFFECT),
    )(*locs, *lands, send_sem, recv_sem, *after)
    locs, lands = outs[:m], outs[m:]

    def forward_body(*refs):
        sends, arrivals = _forward_copies(kinds, refs[:m], refs[2 * m:3 * m], *refs[3 * m:])
        for cp in sends:
            cp.start()
        for cp in arrivals:
            cp.wait_recv()
        for cp in sends:
            cp.wait_send()

    outs = pl.pallas_call(
        forward_body, name=f"gather_forward_{g}",
        in_specs=[ANY] * (2 * m), out_specs=[ANY] * m,
        out_shape=[jax.ShapeDtypeStruct(a.shape, a.dtype) for a in lands],
        input_output_aliases={m + i: i for i in range(m)},
        scratch_shapes=[pltpu.SemaphoreType.DMA((4 * m,)), pltpu.SemaphoreType.DMA((4 * m,))],
    )(*locs, *lands)
    return {nm: _natural(k, o) for nm, k, o in zip(names, kinds, outs)}


def _grad_view(kind, g):
    if kind == "col":
        return g.reshape(2, g.shape[0] // 2, g.shape[1])
    if kind == "row":
        return g.reshape(N_CHIPS, 2, g.shape[0] // (2 * N_CHIPS), g.shape[1])
    return g.reshape(GROUPS, N_CHIPS, 2, g.shape[1] // (2 * N_CHIPS), g.shape[2])


def _pair_copies(kinds, g, got, send_sems, recv_sems):
    x, y, c, _ = _place()

    def other_half(kind, ref):
        if kind == "col":
            return ref.at[1 - c]
        if kind == "row":
            return ref.at[:, 1 - c]
        return ref.at[:, :, 1 - c]

    return [pltpu.make_async_remote_copy(src_ref=other_half(kinds[a], g[a]), dst_ref=got[a], send_sem=send_sems.at[a],
                                         recv_sem=recv_sems.at[a], device_id=(x, y, 1 - c), device_id_type=MESH)
            for a in range(len(kinds))]


def _pair_exchange_start(tag, names, views):
    kinds = [KIND[nm] for nm in names]
    n = len(names)

    def got_shape(kind, v):
        if kind == "col":
            return v.shape[1:]
        if kind == "row":
            return (v.shape[0],) + v.shape[2:]
        return v.shape[:2] + v.shape[3:]

    srcs = [_hbm(views[nm]) for nm in names]
    lands = [_hbm(lax.empty(got_shape(k, views[nm]), BF16)) for nm, k in zip(names, kinds)]

    def body(*refs):
        g, got = refs[:n], refs[n:2 * n]
        for cp in _pair_copies(kinds, g, got, refs[2 * n], refs[2 * n + 1]):
            cp.start()
        refs[-1][...] = jnp.zeros(refs[-1].shape, F32)

    sem_shape = pltpu.SemaphoreType.DMA((n,))
    outs = pl.pallas_call(
        body, name=f"grad_pair_exchange_start_{tag}",
        in_specs=[HBM] * (2 * n),
        out_specs=[SEM, SEM] + [HBM] * (2 * n) + [pl.BlockSpec(memory_space=pltpu.VMEM)],
        out_shape=[sem_shape, sem_shape] + [pltpu.HBM(a.shape, a.dtype) for a in srcs + lands] + [jax.ShapeDtypeStruct((8, 128), F32)],
        input_output_aliases={i: 2 + i for i in range(2 * n)},
        compiler_params=pltpu.CompilerParams(has_side_effects=EFFECT),
    )(*srcs, *lands)
    return (outs[0], outs[1], list(outs[2:2 + n]), list(outs[2 + n:2 + 2 * n])), outs[-1]


def _pair_exchange_wait(tag, names, state, after):
    kinds = [KIND[nm] for nm in names]
    n = len(names)
    send_sem, recv_sem, srcs, lands = state

    def body(*refs):
        g, got = refs[:n], refs[n:2 * n]
        for cp in _pair_copies(kinds, g, got, refs[2 * n], refs[2 * n + 1]):
            cp.wait_send()
            cp.wait_recv()

    outs = pl.pallas_call(
        body, name=f"grad_pair_exchange_wait_{tag}",
        in_specs=[HBM] * (2 * n) + [SEM, SEM, ANY], out_specs=[HBM] * (2 * n),
        out_shape=[pltpu.HBM(a.shape, a.dtype) for a in srcs + lands],
        input_output_aliases={i: i for i in range(2 * n)},
        compiler_params=pltpu.CompilerParams(has_side_effects=EFFECT),
    )(*srcs, *lands, send_sem, recv_sem, after)
    return dict(zip(names, outs[:n])), dict(zip(names, outs[n:]))


def _pair_sum(name, kind, view, got, c_arr):
    if kind == "col":
        _, rows, cols = view.shape
        tr = 128
        grid = (rows // tr,)
        v_spec = pl.BlockSpec((None, tr, cols), lambda i, c: (c[0], i, 0))
        g_spec = pl.BlockSpec((tr, cols), lambda i, c: (i, 0))
    elif kind == "row":
        _, _, rows, cols = view.shape
        grid = (N_CHIPS,)
        v_spec = pl.BlockSpec((None, None, rows, cols), lambda i, c: (i, c[0], 0, 0))
        g_spec = pl.BlockSpec((None, rows, cols), lambda i, c: (i, 0, 0))
    else:
        _, _, _, rows, cols = view.shape
        grid = (GROUPS,)
        v_spec = pl.BlockSpec((None, N_CHIPS, None, rows, cols), lambda i, c: (i, 0, c[0], 0, 0))
        g_spec = pl.BlockSpec((None, N_CHIPS, rows, cols), lambda i, c: (i, 0, 0, 0))

    def body(c_ref, v_ref, g_ref, o_ref):
        o_ref[...] = (v_ref[...].astype(F32) + g_ref[...].astype(F32)).astype(BF16)

    return pl.pallas_call(
        body, name=name,
        grid_spec=pltpu.PrefetchScalarGridSpec(num_scalar_prefetch=1, grid=grid, in_specs=[v_spec, g_spec], out_specs=g_spec),
        out_shape=jax.ShapeDtypeStruct(got.shape, BF16),
        compiler_params=_cparams(("parallel",)),
    )(c_arr, view, got)


def _piece(kind, ref, s):
    if kind == "col":
        cs = ref.shape[1] // N_CHIPS
        return ref.at[:, pl.ds(pl.multiple_of(s * cs, 128), cs)]
    if kind == "row":
        return ref.at[s]
    return ref.at[:, s]


def _piece_shape(kind, shape):
    if kind == "col":
        return (shape[0], shape[1] // N_CHIPS)
    if kind == "row":
        return shape[1:]
    return (shape[0],) + shape[2:]


def _shard_copies(kinds, p, got, send_sems, recv_sems):
    x, y, c, chips = _place()
    return [pltpu.make_async_remote_copy(src_ref=_piece(kinds[a], p[a], 2 * px + py), dst_ref=got[a].at[j],
                                         send_sem=send_sems.at[3 * a + j], recv_sem=recv_sems.at[3 * a + j],
                                         device_id=(px, py, c), device_id_type=MESH)
            for a in range(len(kinds)) for j, (px, py) in enumerate(chips)]


def _shard_exchange_start(g, names, psums):
    kinds = [KIND[nm] for nm in names]
    n = len(names)
    srcs = [_hbm(psums[nm]) for nm in names]
    lands = [_hbm(lax.empty((3,) + _piece_shape(k, psums[nm].shape), BF16)) for nm, k in zip(names, kinds)]

    def body(*refs):
        p, got = refs[:n], refs[n:2 * n]
        send_sems, recv_sems = refs[2 * n], refs[2 * n + 1]
        token = refs[-1]
        for cp in _shard_copies(kinds, p, got, send_sems, recv_sems):
            cp.start()
        token[...] = jnp.zeros(token.shape, F32)

    sem_shape = pltpu.SemaphoreType.DMA((3 * n,))
    outs = pl.pallas_call(
        body, name=f"grad_shard_exchange_start_{g}",
        in_specs=[HBM] * (2 * n),
        out_specs=[SEM, SEM] + [HBM] * (2 * n) + [pl.BlockSpec(memory_space=pltpu.VMEM)],
        out_shape=[sem_shape, sem_shape] + [pltpu.HBM(a.shape, a.dtype) for a in srcs + lands] + [jax.ShapeDtypeStruct((8, 128), F32)],
        input_output_aliases={i: 2 + i for i in range(2 * n)},
        compiler_params=pltpu.CompilerParams(has_side_effects=EFFECT),
    )(*srcs, *lands)
    return (outs[0], outs[1], list(outs[2:2 + n]), list(outs[2 + n:2 + 2 * n])), outs[-1]


def _shard_exchange_wait(g, names, state, after):
    kinds = [KIND[nm] for nm in names]
    n = len(names)
    send_sem, recv_sem, srcs, lands = state

    def body(*refs):
        p, got = refs[:n], refs[n:2 * n]
        for cp in _shard_copies(kinds, p, got, refs[2 * n], refs[2 * n + 1]):
            cp.wait_send()
            cp.wait_recv()

    outs = pl.pallas_call(
        body, name=f"grad_shard_exchange_wait_{g}",
        in_specs=[HBM] * (2 * n) + [SEM, SEM] + [ANY] * len(after), out_specs=[HBM] * (2 * n),
        out_shape=[pltpu.HBM(a.shape, a.dtype) for a in srcs + lands],
        input_output_aliases={i: i for i in range(2 * n)},
        compiler_params=pltpu.CompilerParams(has_side_effects=EFFECT),
    )(*srcs, *lands, send_sem, recv_sem, *after)
    return dict(zip(names, outs[:n])), dict(zip(names, outs[n:]))


def _shard_sum(name, kind, psum, got, sc_arr):
    if kind == "col":
        rows, cols = psum.shape
        cs = cols // N_CHIPS
        tr = 128
        grid = (rows // tr,)
        p_spec = pl.BlockSpec((tr, cs), lambda i, sc: (i, sc[0]))
        g_spec = pl.BlockSpec((3, tr, cs), lambda i, sc: (0, i, 0))
        o_spec = pl.BlockSpec((None, tr, cs), lambda i, sc: (sc[1], i, 0))
        out_shape = (2, rows, cs)
    elif kind == "row":
        _, rows, cols = psum.shape
        grid = (1,)
        p_spec = pl.BlockSpec((None, rows, cols), lambda i, sc: (sc[0], 0, 0))
        g_spec = pl.BlockSpec((3, rows, cols), lambda i, sc: (0, 0, 0))
        o_spec = pl.BlockSpec((None, rows, cols), lambda i, sc: (sc[1], 0, 0))
        out_shape = (2, rows, cols)
    else:
        _, _, rows, cols = psum.shape
        grid = (1,)
        p_spec = pl.BlockSpec((GROUPS, None, rows, cols), lambda i, sc: (0, sc[0], 0, 0))
        g_spec = pl.BlockSpec((3, GROUPS, rows, cols), lambda i, sc: (0, 0, 0, 0))
        o_spec = pl.BlockSpec((GROUPS, None, rows, cols), lambda i, sc: (0, sc[1], 0, 0))
        out_shape = (GROUPS, 2, rows, cols)

    def body(sc_ref, p_ref, g_ref, o_ref):
        o_ref[...] = ((p_ref[...].astype(F32) + g_ref[0].astype(F32)) + g_ref[1].astype(F32)) + g_ref[2].astype(F32)

    return pl.pallas_call(
        body, name=name,
        grid_spec=pltpu.PrefetchScalarGridSpec(num_scalar_prefetch=1, grid=grid, in_specs=[p_spec, g_spec], out_specs=o_spec),
        out_shape=jax.ShapeDtypeStruct(out_shape, F32),
        compiler_params=_cparams(("parallel",)),
    )(sc_arr, psum, got)


def _half_copies(kinds, bufs, send_sems, recv_sems):
    x, y, c, _ = _place()

    def remote(a, half):
        part = _half(kinds[a], bufs[a], half)
        return pltpu.make_async_remote_copy(src_ref=part, dst_ref=part, send_sem=send_sems.at[a], recv_sem=recv_sems.at[a],
                                            device_id=(x, y, 1 - c), device_id_type=MESH)

    return [remote(a, c) for a in range(len(kinds))], [remote(a, 1 - c) for a in range(len(kinds))]


def _half_exchange_start(tag, names, bufs):
    kinds = [KIND[nm] for nm in names]
    n = len(names)
    arrs = [_hbm(bufs[nm]) for nm in names]

    def body(*refs):
        for cp in _half_copies(kinds, refs[:n], refs[n], refs[n + 1])[0]:
            cp.start()
        refs[-1][...] = jnp.zeros(refs[-1].shape, F32)

    sem_shape = pltpu.SemaphoreType.DMA((n,))
    outs = pl.pallas_call(
        body, name=f"grad_half_exchange_start_{tag}",
        in_specs=[HBM] * n,
        out_specs=[SEM, SEM] + [HBM] * n + [pl.BlockSpec(memory_space=pltpu.VMEM)],
        out_shape=[sem_shape, sem_shape] + [pltpu.HBM(a.shape, a.dtype) for a in arrs] + [jax.ShapeDtypeStruct((8, 128), F32)],
        input_output_aliases={i: 2 + i for i in range(n)},
        compiler_params=pltpu.CompilerParams(has_side_effects=EFFECT),
    )(*arrs)
    return (outs[0], outs[1], list(outs[2:2 + n])), outs[-1]


def _half_exchange_wait(tag, names, state, after):
    kinds = [KIND[nm] for nm in names]
    n = len(names)
    send_sem, recv_sem, arrs = state

    def body(*refs):
        sends, arrivals = _half_copies(kinds, refs[:n], refs[n], refs[n + 1])
        for cp in sends:
            cp.wait_send()
        for cp in arrivals:
            cp.wait_recv()

    outs = pl.pallas_call(
        body, name=f"grad_half_exchange_wait_{tag}",
        in_specs=[HBM] * n + [SEM, SEM] + [ANY] * len(after), out_specs=[HBM] * n,
        out_shape=[pltpu.HBM(a.shape, a.dtype) for a in arrs],
        input_output_aliases={i: i for i in range(n)},
        compiler_params=pltpu.CompilerParams(has_side_effects=EFFECT),
    )(*arrs, send_sem, recv_sem, *after)
    return dict(zip(names, outs))


N_DEV = 8
SMALL_ROWS = 8


def _all_reduce_small(name, v):
    def body(v_ref, o_ref, token, buf, send_sems, recv_sems):
        token[...] = jnp.zeros(token.shape, F32)
        x, y, c, _ = _place()
        me = 4 * x + 2 * y + c
        buf[me] = v_ref[...]
        cps = []
        for r in range(1, N_DEV):
            to = (x ^ (r >> 2), y ^ ((r >> 1) & 1), c ^ (r & 1))
            cp = pltpu.make_async_remote_copy(src_ref=v_ref, dst_ref=buf.at[me], send_sem=send_sems.at[r - 1],
                                              recv_sem=recv_sems.at[r - 1], device_id=to, device_id_type=MESH)
            cp.start()
            cps.append(cp)
        for r in range(1, N_DEV):
            pltpu.make_async_remote_copy(src_ref=v_ref, dst_ref=buf.at[me ^ r], send_sem=send_sems.at[r - 1],
                                         recv_sem=recv_sems.at[r - 1], device_id=(x, y, c), device_id_type=MESH).wait_recv()
        for cp in cps:
            cp.wait_send()
        acc = buf[0]
        for d in range(1, N_DEV):
            acc = acc + buf[d]
        o_ref[...] = acc

    vm = pl.BlockSpec(memory_space=pltpu.VMEM)
    return pl.pallas_call(
        body, name=name, in_specs=[vm], out_specs=[vm, vm],
        out_shape=[jax.ShapeDtypeStruct((SMALL_ROWS, D_MODEL), F32), jax.ShapeDtypeStruct((8, 128), F32)],
        scratch_shapes=[pltpu.VMEM((N_DEV, SMALL_ROWS, D_MODEL), F32), pltpu.SemaphoreType.DMA((N_DEV - 1,)),
                        pltpu.SemaphoreType.DMA((N_DEV - 1,))],
    )(v)


def _adamw(name, w, g, m, v, with_grad=False):
    rows, cols = w.shape
    tr = next((c for c in (256, 176, 128, 64, 32, 8) if rows % c == 0), rows)
    spec = pl.BlockSpec((tr, cols), lambda i: (i, 0))

    def body(w_ref, g_ref, m_ref, v_ref, d_ref, mo_ref, vo_ref, *go_ref):
        gv = g_ref[...]
        if with_grad:
            go_ref[0][...] = gv
        m_new = ADAM_B1 * m_ref[...] + (1.0 - ADAM_B1) * gv
        v_new = ADAM_B2 * v_ref[...] + (1.0 - ADAM_B2) * jnp.square(gv)
        m_hat = m_new / (1.0 - ADAM_B1 ** ADAM_STEP)
        v_hat = v_new / (1.0 - ADAM_B2 ** ADAM_STEP)
        d_ref[...] = -ADAM_LR * (m_hat / (jnp.sqrt(v_hat) + ADAM_EPS) + ADAM_WD * w_ref[...])
        mo_ref[...] = m_new
        vo_ref[...] = v_new

    return pl.pallas_call(
        body, name=name, grid=(rows // tr,),
        in_specs=[spec] * 4, out_specs=[spec] * (4 if with_grad else 3),
        out_shape=[jax.ShapeDtypeStruct((rows, cols), F32)] * (4 if with_grad else 3),
        compiler_params=_cparams(("parallel",)),
    )(w, g, m, v)


WEIGHTS = ("norm_ffn1", "ffn1_w_in", "ffn1_w_out", "norm_mix", "w_in", "gate_bias", "pool_w", "pool_scale", "w_ret_up",
           "w_pool_up", "w_out", "norm_ffn2", "ffn2_w_in", "ffn2_w_out", "norm_final")
SMALL_ROW = dict(norm_ffn1=0, norm_mix=1, gate_bias=2, pool_scale=4, norm_ffn2=5, norm_final=6)


def _as2d(a):
    return a.reshape(-1, a.shape[-1])


def kernel(x, norm_ffn1, ffn1_w_in, ffn1_w_out, norm_mix, w_in, gate_bias, pool_w, pool_scale, w_ret_up, w_pool_up, w_out, norm_ffn2, ffn2_w_in, ffn2_w_out, norm_final, loss_target, m_norm_ffn1, m_ffn1_w_in, m_ffn1_w_out, m_norm_mix, m_w_in, m_gate_bias, m_pool_w, m_pool_scale, m_w_ret_up, m_w_pool_up, m_w_out, m_norm_ffn2, m_ffn2_w_in, m_ffn2_w_out, m_norm_final, v_norm_ffn1, v_ffn1_w_in, v_ffn1_w_out, v_norm_mix, v_w_in, v_gate_bias, v_pool_w, v_pool_scale, v_w_ret_up, v_w_pool_up, v_w_out, v_norm_ffn2, v_ffn2_w_in, v_ffn2_w_out, v_norm_final):
    wt = dict(norm_ffn1=norm_ffn1, ffn1_w_in=ffn1_w_in, ffn1_w_out=ffn1_w_out, norm_mix=norm_mix, w_in=w_in, gate_bias=gate_bias,
              pool_w=pool_w, pool_scale=pool_scale, w_ret_up=w_ret_up, w_pool_up=w_pool_up, w_out=w_out, norm_ffn2=norm_ffn2,
              ffn2_w_in=ffn2_w_in, ffn2_w_out=ffn2_w_out, norm_final=norm_final)
    mom = dict(norm_ffn1=m_norm_ffn1, ffn1_w_in=m_ffn1_w_in, ffn1_w_out=m_ffn1_w_out, norm_mix=m_norm_mix, w_in=m_w_in,
               gate_bias=m_gate_bias, pool_w=m_pool_w, pool_scale=m_pool_scale, w_ret_up=m_w_ret_up, w_pool_up=m_w_pool_up,
               w_out=m_w_out, norm_ffn2=m_norm_ffn2, ffn2_w_in=m_ffn2_w_in, ffn2_w_out=m_ffn2_w_out, norm_final=m_norm_final)
    var = dict(norm_ffn1=v_norm_ffn1, ffn1_w_in=v_ffn1_w_in, ffn1_w_out=v_ffn1_w_out, norm_mix=v_norm_mix, w_in=v_w_in,
               gate_bias=v_gate_bias, pool_w=v_pool_w, pool_scale=v_pool_scale, w_ret_up=v_w_ret_up, w_pool_up=v_w_pool_up,
               w_out=v_w_out, norm_ffn2=v_norm_ffn2, ffn2_w_in=v_ffn2_w_in, ffn2_w_out=v_ffn2_w_out, norm_final=v_norm_final)

    ax, ay, ac = lax.axis_index("x"), lax.axis_index("y"), lax.axis_index("c")
    chip = 2 * ax + ay
    c_arr = jnp.reshape(ac, (1,)).astype(jnp.int32)
    sc_arr = jnp.stack([chip, ac]).astype(jnp.int32)
    bias_cols = gate_bias.shape[-1]

    first = {"gate_bias": gate_bias[0], "ffn1_w_in": ffn1_w_in[0].astype(BF16)}
    gather_groups, token = _gather_start("first", [0], first)
    rest, rest_token = _gather_start("rest", [1, 2, 3, 4],
                                     {nm: wt[nm][0].astype(BF16) + token[0, 0].astype(BF16) for nm in BIG if nm not in first})
    gather_groups.update(rest)
    vec = dict(norm_ffn1=norm_ffn1, norm_mix=norm_mix, norm_ffn2=norm_ffn2, pool_scale=pool_scale,
               norm_final=norm_final.reshape(1, D_MODEL))

    relayed = {}

    def relay_w(g, after):
        relayed[g] = _gather_relay(g, gather_groups[g], (after,))

    def get_w(g, after):
        if g in relayed:
            return _gather_land(g, relayed[g], (after,))
        return _gather_finish(g, gather_groups[g], (after, rest_token) if g == 0 else (after,))

    pairs, pending = [], []

    def on_grads(gr):
        g = len(pairs)
        names = GRAD_GROUPS[g]
        assert set(names) == set(gr), (names, list(gr))
        state, token = _pair_exchange_start(g, names, {nm: _grad_view(KIND[nm], gr[nm]) for nm in names})
        pairs.append(state)
        return token[0:1, 0:1]

    def flush(after):
        g = len(pending)
        names = GRAD_GROUPS[g]
        views, from_sib = _pair_exchange_wait(g, names, pairs[g], after)
        psums = {nm: _pair_sum(f"pair_sum_{nm}", KIND[nm], views[nm], from_sib[nm], c_arr) for nm in names}
        state, token = _shard_exchange_start(g, names, psums)
        pending.append(state)
        tokens.append(token)
        return token[0:1, 0:1]

    tokens = []
    loss_local, dx, small = _local_step(x[0], loss_target[0], vec, get_w, relay_w, on_grads, flush)

    grads, delta, new_m, new_v = {}, {}, {}, {}

    def adamw(nm):
        shape = wt[nm].shape
        outs = _adamw(f"adamw_{nm}", _as2d(wt[nm]), _as2d(grads[nm]), _as2d(mom[nm]), _as2d(var[nm]), with_grad=nm in BIG)
        delta[nm], new_m[nm], new_v[nm] = (o.reshape(shape) for o in outs[:3])
        if nm in BIG:
            grads[nm] = outs[3].reshape(shape)
        return outs[0]

    def reduce_start(g, after):
        names = GRAD_GROUPS[g]
        psums, from_chips = _shard_exchange_wait(g, names, pending[g], after)
        bufs = {nm: _shard_sum(f"shard_sum_{nm}", KIND[nm], psums[nm], from_chips[nm], sc_arr) for nm in names}
        return _half_exchange_start(g, names, bufs)

    def reduce_finish(g, state, after):
        names = GRAD_GROUPS[g]
        reduced = _half_exchange_wait(g, names, state, after)
        for nm in names:
            grads[nm] = reduced[nm].reshape(wt[nm].shape)
        return tuple(adamw(nm) for nm in names)

    swap0, token = reduce_start(0, (tokens[-1],))
    swap1, token = reduce_start(1, (token,))
    done = reduce_finish(0, swap0, (token,))
    done = reduce_finish(1, swap1, done)
    swap2, token = reduce_start(2, done)
    packed = jnp.concatenate([small["norm_ffn1"], small["norm_mix"], small["gate_bias"], small["pool_scale"],
                              small["norm_ffn2"], small["norm_final"], jnp.broadcast_to(loss_local, (1, D_MODEL))], axis=0)
    small_sum, _ = _all_reduce_small("reduce_small_grads", packed + token[0, 0])
    loss = small_sum[SMALL_ROWS - 1, 0]
    for nm in ("norm_ffn1", "norm_mix", "pool_scale", "norm_ffn2"):
        grads[nm] = small_sum[SMALL_ROW[nm]][None, :]
    grads["norm_final"] = small_sum[SMALL_ROW["norm_final"]]
    grads["gate_bias"] = lax.dynamic_slice(small_sum, (SMALL_ROW["gate_bias"], chip * bias_cols), (2, bias_cols))[None]
    reduce_finish(2, swap2, (small_sum,))
    for nm in WEIGHTS:
        if nm not in delta:
            adamw(nm)

    return (loss, dx[None], *[grads[nm] for nm in WEIGHTS], *[delta[nm] for nm in WEIGHTS],
            *[new_m[nm] for nm in WEIGHTS], *[new_v[nm] for nm in WEIGHTS])
```

```python
import numpy as np
import jax
import jax.numpy as jnp
from jax import lax
from jax.experimental import pallas as pl
from jax.experimental.pallas import tpu as pltpu

F32 = jnp.float32
BF16 = jnp.bfloat16
MESH = pl.DeviceIdType.MESH

D_MODEL = 1024
D_FF = 2816
HEADS = 4
HEAD_DIM = 256
GROUPS = 4
GROUP_DIM = 256
POOL_WINDOWS = (2, 4, 8, 16)
IN_WIDTH = 7 * D_MODEL
ROPE_BASE = 10000.0
NORM_EPS = 1e-6
FFN_RES_WEIGHT = 0.5
ADAM_LR, ADAM_B1, ADAM_B2, ADAM_EPS, ADAM_WD, ADAM_STEP = 0.001, 0.9, 0.999, 1e-08, 0.01, 10

N_CHIPS = 4
RET_BLOCK = 256
V7X_VMEM_LIMIT = 48 * 1024 * 1024


def _cparams(sem):
    return pltpu.CompilerParams(dimension_semantics=sem, vmem_limit_bytes=V7X_VMEM_LIMIT)


def _sigmoid(x):
    return jax.nn.sigmoid(x)


_DIMS = {"nn": (((1,), (0,)), ((), ())), "nt": (((1,), (1,)), ((), ())), "tn": (((0,), (0,)), ((), ()))}


def _matmul(name, a, b, mode, m, n, k, tm, tn, tk, out_dtypes, extras=(), consts=(), epilogue=None, resident=None,
            n_outer=False):
    tm, tn, tk = min(tm, m), min(tn, n), min(tk, k)
    gi, gj, gk = m // tm, n // tn, k // tk
    assert gi * tm == m and gj * tn == n and gk * tk == k, (name, m, n, k, tm, tn, tk)
    once = dict(pipeline_mode=pl.Buffered(1))

    def spec(shape, index, **kw):
        return pl.BlockSpec(shape, (lambda j, i, kk: index(i, j, kk)) if n_outer else index, **kw)

    kw = once if resident == "a" else {}
    a_spec = (spec((tk, tm), lambda i, j, kk: (kk, i), **kw) if mode == "tn" else spec((tm, tk), lambda i, j, kk: (i, kk), **kw))
    kw = once if resident == "b" else {}
    b_spec = (spec((tn, tk), lambda i, j, kk: (j, kk), **kw) if mode == "nt" else spec((tk, tn), lambda i, j, kk: (kk, j), **kw))
    n_ex, n_out = len(extras) + len(consts), len(out_dtypes)
    dims = _DIMS[mode]

    def body(a_ref, b_ref, *rest):
        ex_refs, out_refs = rest[:n_ex], rest[n_ex:n_ex + n_out]

        def finish(acc):
            outs = (acc,) if epilogue is None else epilogue(acc, *[e[...] for e in ex_refs])
            for o_ref, o in zip(out_refs, outs):
                o_ref[...] = o.astype(o_ref.dtype)

        prod = lax.dot_general(a_ref[...], b_ref[...], dims, preferred_element_type=F32)
        if gk == 1:
            finish(prod)
        else:
            acc_ref = rest[n_ex + n_out]
            kk = pl.program_id(2)

            @pl.when(kk == 0)
            def _():
                acc_ref[...] = prod

            @pl.when(kk > 0)
            def _():
                acc_ref[...] += prod

            @pl.when(kk == gk - 1)
            def _():
                finish(acc_ref[...])

    o_spec = spec((tm, tn), lambda i, j, kk: (i, j))
    outs = pl.pallas_call(
        body, name=name, grid=(gj, gi, gk) if n_outer else (gi, gj, gk),
        in_specs=[a_spec, b_spec] + [o_spec] * len(extras) + [spec((1, tn), lambda i, j, kk: (0, j))] * len(consts),
        out_specs=[o_spec] * n_out,
        out_shape=[jax.ShapeDtypeStruct((m, n), dt) for dt in out_dtypes],
        scratch_shapes=[pltpu.VMEM((tm, tn), F32)] if gk > 1 else [],
        compiler_params=_cparams(("parallel", "parallel", "arbitrary")),
    )(a, b, *extras, *consts)
    return outs[0] if n_out == 1 else outs


def _row_spec(tm, width, col_block=0):
    return pl.BlockSpec((tm, width), lambda i: (i, col_block))


def _full_spec(shape):
    return pl.BlockSpec(shape, lambda *_: (0,) * len(shape))


def _rmsnorm_fwd(name, h, g, tm=512):
    t = h.shape[0]

    def body(h_ref, g_ref, o_ref):
        x = h_ref[...]
        r = lax.rsqrt(jnp.mean(x * x, axis=-1, keepdims=True) + NORM_EPS)
        o_ref[...] = (x * r * g_ref[...]).astype(BF16)

    return pl.pallas_call(
        body, name=name, grid=(t // tm,),
        in_specs=[_row_spec(tm, D_MODEL), _full_spec((1, D_MODEL))],
        out_specs=_row_spec(tm, D_MODEL),
        out_shape=jax.ShapeDtypeStruct((t, D_MODEL), BF16),
        compiler_params=_cparams(("parallel",)),
    )(h, g)


def _proj_norm_bwd(name, a_list, a_specs, parts, w, h, g, dres, tm):
    t = h.shape[0]
    na = len(a_list)

    def body(*refs):
        a_refs = refs[:na]
        w_ref, h_ref, g_ref, dres_ref, dh_ref, dhb_ref, dg_ref = refs[na:]
        i = pl.program_id(0)
        dn_v = None
        for which, lead, k0, k1 in parts:
            a_ref = a_refs[which]
            term = _dot(a_ref[...] if lead is None else a_ref[lead], w_ref[:, k0:k1], "nt")
            dn_v = term if dn_v is None else dn_v + term
        x = h_ref[...]
        r = lax.rsqrt(jnp.mean(x * x, axis=-1, keepdims=True) + NORM_EPS)
        xh = x * r
        dxh = dn_v * g_ref[...]
        dh = dres_ref[...] + r * (dxh - xh * jnp.mean(dxh * xh, axis=-1, keepdims=True))
        dh_ref[...] = dh
        dhb_ref[...] = dh.astype(BF16)
        part = jnp.sum(dn_v * xh, axis=0, keepdims=True)

        @pl.when(i == 0)
        def _():
            dg_ref[...] = part

        @pl.when(i > 0)
        def _():
            dg_ref[...] += part

    row = _row_spec(tm, D_MODEL)
    return pl.pallas_call(
        body, name=name, grid=(t // tm,),
        in_specs=list(a_specs) + [pl.BlockSpec(w.shape, lambda i: (0, 0), pipeline_mode=pl.Buffered(1)), row,
                                  _full_spec((1, D_MODEL)), row],
        out_specs=[row, row, _full_spec((1, D_MODEL))],
        out_shape=[jax.ShapeDtypeStruct((t, D_MODEL), F32), jax.ShapeDtypeStruct((t, D_MODEL), BF16),
                   jax.ShapeDtypeStruct((1, D_MODEL), F32)],
        compiler_params=_cparams(("arbitrary",)),
    )(*a_list, w, h, g, dres)


def _out_loss_and_grad(name, mid, w_out, h, g, target, tm=512):
    t = h.shape[0]

    def body(m_ref, w_ref, h_ref, g_ref, t_ref, dh_ref, dhb_ref, dg_ref, loss_ref):
        i = pl.program_id(0)
        x = h_ref[...] + FFN_RES_WEIGHT * _dot(m_ref[...], w_ref[...])
        gv = g_ref[...]
        r = lax.rsqrt(jnp.mean(x * x, axis=-1, keepdims=True) + NORM_EPS)
        xh = x * r
        err = xh * gv - t_ref[...]
        row = jnp.mean(err * err, axis=-1, keepdims=True)
        part_loss = 0.5 * jnp.sum(row, axis=0, keepdims=True)
        dy = err * (1.0 / D_MODEL)
        dxh = dy * gv
        dh = r * (dxh - xh * jnp.mean(dxh * xh, axis=-1, keepdims=True))
        dh_ref[...] = dh
        dhb_ref[...] = dh.astype(BF16)
        part = jnp.sum(dy * xh, axis=0, keepdims=True)

        @pl.when(i == 0)
        def _():
            dg_ref[...] = part
            loss_ref[...] = jnp.zeros(loss_ref.shape, F32) + part_loss

        @pl.when(i > 0)
        def _():
            dg_ref[...] += part
            loss_ref[...] += part_loss

    return pl.pallas_call(
        body, name=name, grid=(t // tm,),
        in_specs=[_row_spec(tm, D_FF), pl.BlockSpec((D_FF, D_MODEL), lambda i: (0, 0), pipeline_mode=pl.Buffered(1)),
                  _row_spec(tm, D_MODEL), _full_spec((1, D_MODEL)), _row_spec(tm, D_MODEL)],
        out_specs=[_row_spec(tm, D_MODEL), _row_spec(tm, D_MODEL), _full_spec((1, D_MODEL)), _full_spec((8, 128))],
        out_shape=[jax.ShapeDtypeStruct((t, D_MODEL), F32), jax.ShapeDtypeStruct((t, D_MODEL), BF16),
                   jax.ShapeDtypeStruct((1, D_MODEL), F32), jax.ShapeDtypeStruct((8, 128), F32)],
        compiler_params=_cparams(("arbitrary",)),
    )(mid, w_out, h, g, target)


def _rope_tables(t):
    half = HEAD_DIM // 2
    inv_freq = np.float32(ROPE_BASE) ** (-np.arange(half, dtype=np.float32) / np.float32(half))
    ang = (np.arange(t, dtype=np.float32)[:, None] * inv_freq[None, :].astype(np.float32)).astype(np.float32)
    return jnp.asarray(np.cos(ang.astype(np.float64)).astype(np.float32)), jnp.asarray(np.sin(ang.astype(np.float64)).astype(np.float32))


ROPE_HALF = HEAD_DIM // 2
K_SCALE = HEAD_DIM ** -0.5


def _rotate(ref, rows, hh, c, s, scale=None):
    lo, mid, hi = hh * HEAD_DIM, hh * HEAD_DIM + ROPE_HALF, (hh + 1) * HEAD_DIM
    x1, x2 = ref[rows, lo:mid].astype(F32), ref[rows, mid:hi].astype(F32)
    y = jnp.concatenate([x1 * c - x2 * s, x1 * s + x2 * c], axis=1)
    return y if scale is None else y * scale


def _unrotate_into(ref, rows, hh, dy, c, s, scale=None):
    lo, mid, hi = hh * HEAD_DIM, hh * HEAD_DIM + ROPE_HALF, (hh + 1) * HEAD_DIM
    y1, y2 = dy[:, :ROPE_HALF], dy[:, ROPE_HALF:]
    d1, d2 = y1 * c + y2 * s, y2 * c - y1 * s
    if scale is not None:
        d1, d2 = d1 * scale, d2 * scale
    ref[rows, lo:mid] = d1.astype(ref.dtype)
    ref[rows, mid:hi] = d2.astype(ref.dtype)


def _retention_tables():
    b, chunk = RET_BLOCK, 64
    gamma = 1.0 - 2.0 ** (-5.0 - np.arange(HEADS, dtype=np.float64))
    log_g = np.log(gamma)[:, None, None]
    i = np.arange(b)[:, None]
    j = np.arange(b)[None, :]
    same = (i // chunk) == (j // chunk)
    earlier = (j // chunk) < (i // chunk)
    expo = np.where(same, np.abs(i - j), np.where(earlier, i - j, 0)).astype(np.float64)
    mask = np.where(same | earlier, 1.0, 0.0)
    dmat = np.exp(log_g * expo[None]) * mask[None]
    qd = np.exp(log_g[:, :, 0] * (np.arange(b)[None, :] + 1.0))
    kd = np.exp(log_g[:, :, 0] * (b - 1.0 - np.arange(b)[None, :]))
    cd = np.exp(log_g[:, :, 0] * b) * np.ones((1, HEAD_DIM))
    as32 = lambda v: jnp.asarray(v.astype(np.float32))
    return (as32(dmat), as32(np.swapaxes(dmat, 1, 2)), as32(qd[:, :, None]), as32(kd[:, :, None]), as32(cd[:, None, :]))


def _dot(a, b, mode="nn"):
    return lax.dot_general(a, b, _DIMS[mode], preferred_element_type=F32)


GRET_BLOCK = 3


RET_SUBS = 2
RET_STEP = RET_SUBS * RET_BLOCK


def _head_specs(steps, rev=False):
    pos = (lambda n: steps - 1 - n) if rev else (lambda n: n)
    tok = pl.BlockSpec((RET_STEP, D_MODEL), lambda n: (pos(n), 0))
    blk = [pl.BlockSpec((RET_STEP, D_MODEL), lambda n, b=b: (pos(n), b)) for b in range(GRET_BLOCK + 1)]
    rope = pl.BlockSpec((RET_STEP, ROPE_HALF), lambda n: (pos(n), 0))
    tab = _full_spec((HEADS, RET_BLOCK, RET_BLOCK))
    col = _full_spec((HEADS, RET_BLOCK, 1))
    rowv = _full_spec((HEADS, 1, HEAD_DIM))
    st = pl.BlockSpec((HEADS, RET_SUBS, HEAD_DIM, HEAD_DIM), lambda n: (0, pos(n), 0, 0))
    return tok, blk, rope, tab, col, rowv, st


def _retention_fwd(name, proj, cos, sin, tables):
    t = proj.shape[0]
    nb, steps = t // RET_BLOCK, t // RET_STEP
    dmat, _, qd, kd, cd = tables
    tok, blk, rope, tab, col, rowv, st = _head_specs(steps)

    def body(q_ref, k_ref, v_ref, g_ref, c_ref, s_ref, d_ref, qd_ref, kd_ref, cd_ref, o_ref, ret_ref, st_ref, state):
        n = pl.program_id(0)

        @pl.when(n == 0)
        def _():
            state[...] = jnp.zeros(state.shape, F32)

        for sub in range(RET_SUBS):
            rows = slice(sub * RET_BLOCK, (sub + 1) * RET_BLOCK)
            cs, sn = c_ref[rows, :], s_ref[rows, :]
            for hh in range(HEADS):
                sl = slice(hh * HEAD_DIM, (hh + 1) * HEAD_DIM)
                q, k = _rotate(q_ref, rows, hh, cs, sn), _rotate(k_ref, rows, hh, cs, sn, K_SCALE)
                v = v_ref[rows, sl].astype(BF16)
                s = _dot(q.astype(BF16), k.astype(BF16), "nt") * d_ref[hh]
                stb = state[hh].astype(BF16)
                st_ref[hh, sub] = stb
                o = _dot(s.astype(BF16), v) + _dot((q * qd_ref[hh]).astype(BF16), stb)
                o_ref[rows, sl] = o
                rn = o * lax.rsqrt(jnp.mean(o * o, axis=-1, keepdims=True) + NORM_EPS)
                g = g_ref[rows, sl].astype(F32)
                ret_ref[rows, sl] = (rn * (g * _sigmoid(g))).astype(BF16)
                state[hh] = state[hh] * cd_ref[hh] + _dot((k * kd_ref[hh]).astype(BF16), v, "tn")

    return pl.pallas_call(
        body, name=name, grid=(steps,),
        in_specs=blk + [rope, rope, tab, col, col, rowv],
        out_specs=[tok, tok, st],
        out_shape=[jax.ShapeDtypeStruct((t, D_MODEL), F32), jax.ShapeDtypeStruct((t, D_MODEL), BF16),
                   jax.ShapeDtypeStruct((HEADS, nb, HEAD_DIM, HEAD_DIM), BF16)],
        scratch_shapes=[pltpu.VMEM((HEADS, HEAD_DIM, HEAD_DIM), F32)],
        compiler_params=_cparams(("arbitrary",)),
    )(proj, proj, proj, proj, cos, sin, dmat, qd, kd, cd)


def _retention_bwd(name, dru, w_ru, o, proj, cos, sin, states, tables):
    t = proj.shape[0]
    steps = t // RET_STEP
    dmat, dmat_t, qd, kd, cd = tables
    tok, blk, rope, tab, col, rowv, st = _head_specs(steps, rev=True)

    def body(dru_ref, wru_ref, o_ref, q_ref, k_ref, v_ref, g_ref, c_ref, s_ref, st_ref, d_ref, dt_ref, qd_ref, kd_ref, cd_ref,
             dq_ref, dk_ref, dv_ref, dg_ref, gstate):
        n = pl.program_id(0)

        @pl.when(n == 0)
        def _():
            gstate[...] = jnp.zeros(gstate.shape, F32)

        for sub in reversed(range(RET_SUBS)):
            rows = slice(sub * RET_BLOCK, (sub + 1) * RET_BLOCK)
            cs, sn = c_ref[rows, :], s_ref[rows, :]
            dret = _dot(dru_ref[rows, :], wru_ref[...], "nt")
            for hh in range(HEADS):
                sl = slice(hh * HEAD_DIM, (hh + 1) * HEAD_DIM)
                o_v, g, dr = o_ref[rows, sl], g_ref[rows, sl].astype(F32), dret[:, sl]
                sg = _sigmoid(g)
                r = lax.rsqrt(jnp.mean(o_v * o_v, axis=-1, keepdims=True) + NORM_EPS)
                rn = o_v * r
                d_rn = dr * (g * sg)
                dg_ref[rows, sl] = (dr * rn * (sg * (1.0 + g * (1.0 - sg)))).astype(BF16)
                d_o = r * (d_rn - rn * jnp.mean(d_rn * rn, axis=-1, keepdims=True))
                dob = d_o.astype(BF16)

                q, k = _rotate(q_ref, rows, hh, cs, sn), _rotate(k_ref, rows, hh, cs, sn, K_SCALE)
                v = v_ref[rows, sl].astype(BF16)
                qb, kb = q.astype(BF16), k.astype(BF16)
                qdv, kdv = qd_ref[hh], kd_ref[hh]
                s_t = (_dot(kb, qb, "nt") * dt_ref[hh]).astype(BF16)
                p_t = (_dot(v, dob, "nt") * dt_ref[hh]).astype(BF16)
                p = (_dot(dob, v, "nt") * d_ref[hh]).astype(BF16)
                stb = st_ref[hh, sub]
                gb = gstate[hh].astype(BF16)
                _unrotate_into(dq_ref, rows, hh, _dot(p, kb) + _dot(dob, stb, "nt") * qdv, cs, sn)
                _unrotate_into(dk_ref, rows, hh, _dot(p_t, qb) + _dot(v, gb, "nt") * kdv, cs, sn, K_SCALE)
                dv_ref[rows, sl] = (_dot(s_t, dob) + _dot((k * kdv).astype(BF16), gb)).astype(BF16)
                gstate[hh] = gstate[hh] * cd_ref[hh] + _dot((q * qdv).astype(BF16), dob, "tn")

    return pl.pallas_call(
        body, name=name, grid=(steps,),
        in_specs=[tok, pl.BlockSpec((D_MODEL, D_MODEL), lambda n: (0, 0), pipeline_mode=pl.Buffered(1)), tok] + blk
                 + [rope, rope, st, tab, tab, col, col, rowv],
        out_specs=[tok, tok, tok, tok],
        out_shape=[jax.ShapeDtypeStruct((t, D_MODEL), BF16)] * 4,
        scratch_shapes=[pltpu.VMEM((HEADS, HEAD_DIM, HEAD_DIM), F32)],
        compiler_params=_cparams(("arbitrary",)),
    )(dru, w_ru, o, proj, proj, proj, proj, cos, sin, states, dmat, dmat_t, qd, kd, cd)


POOL_TILE = 256
POOL_SUBS = 4
POOL_STEP = POOL_SUBS * POOL_TILE


def _pool_tables():
    b = POOL_TILE
    tt = np.arange(b)[:, None]
    jj = np.arange(b)[None, :]
    cur, prev = [], []
    for w in POOL_WINDOWS:
        cur.append(((tt - jj >= 0) & (tt - jj <= w - 1)).astype(np.float32))
        prev.append((tt - (jj - b) <= w - 1).astype(np.float32))
    cur, prev = np.stack(cur), np.stack(prev)
    as16 = lambda v: jnp.asarray(v, dtype=BF16)
    return as16(cur), as16(prev), as16(np.swapaxes(cur, 1, 2)), as16(np.swapaxes(prev, 1, 2))


def _split2(x):
    hi = x.astype(BF16)
    return hi, (x - hi.astype(F32)).astype(BF16)


POOL_BLOCK = 4


def _pool_count(n, window):
    tpos = n * POOL_TILE + lax.broadcasted_iota(jnp.int32, (POOL_TILE, 1), 0)
    return jnp.minimum(tpos + 1, window).astype(F32)


def _pool_fwd(name, proj, pool_w, scale, tables):
    t = proj.shape[0]
    steps = t // POOL_STEP
    mc, mp, _, _ = tables
    tab = _full_spec((GROUPS, POOL_TILE, POOL_TILE))
    row = _row_spec(POOL_STEP, D_MODEL)

    def body(pc_ref, pp_ref, mc_ref, mp_ref, w_ref, sc_ref, pm_ref, mix_ref, po_ref):
        n = pl.program_id(0)
        for sub in range(POOL_SUBS):
            rows = slice(sub * POOL_TILE, (sub + 1) * POOL_TILE)
            tile = n * POOL_SUBS + sub
            for g, window in enumerate(POOL_WINDOWS):
                sl = slice(g * GROUP_DIM, (g + 1) * GROUP_DIM)
                p = pc_ref[rows, sl]
                if sub == 0:
                    before = jnp.where(n > 0, _dot(mp_ref[g], pp_ref[:, sl]), 0.0)
                else:
                    before = _dot(mp_ref[g], pc_ref[(sub - 1) * POOL_TILE:sub * POOL_TILE, sl])
                pm = ((_dot(mc_ref[g], p) + before) / _pool_count(tile, window) - p.astype(F32)).astype(BF16)
                pm_ref[rows, sl] = pm
                mixed = _dot(pm, w_ref[g])
                mix_ref[rows, sl] = mixed
                po_ref[rows, sl] = (mixed * sc_ref[:, sl]).astype(BF16)

    return pl.pallas_call(
        body, name=name, grid=(steps,),
        in_specs=[_row_spec(POOL_STEP, D_MODEL, POOL_BLOCK),
                  pl.BlockSpec((POOL_TILE, D_MODEL), lambda n: (jnp.maximum(n * POOL_SUBS - 1, 0), POOL_BLOCK)),
                  tab, tab, _full_spec((GROUPS, GROUP_DIM, GROUP_DIM)), _full_spec((1, D_MODEL))],
        out_specs=[row] * 3,
        out_shape=[jax.ShapeDtypeStruct((t, D_MODEL), BF16), jax.ShapeDtypeStruct((t, D_MODEL), F32),
                   jax.ShapeDtypeStruct((t, D_MODEL), BF16)],
        compiler_params=_cparams(("parallel",)),
    )(proj, proj, mc, mp, pool_w, scale)


def _pool_bwd(name, dpu, w_pu, pm, mixed, pool_w, scale, tables):
    t = dpu.shape[0]
    steps = t // POOL_STEP
    _, _, mct, mpt = tables
    cur = pl.BlockSpec((POOL_STEP, D_MODEL), lambda n: (steps - 1 - n, 0))
    tab = _full_spec((GROUPS, POOL_TILE, POOL_TILE))
    wspec = _full_spec((GROUPS, GROUP_DIM, GROUP_DIM))
    sspec = _full_spec((1, D_MODEL))

    def body(dpu_ref, wpu_ref, pm_ref, mix_ref, mct_ref, mpt_ref, w_ref, sc_ref, dp_ref, dw_ref, ds_ref, later):
        n = pl.program_id(0)

        @pl.when(n == 0)
        def _():
            dw_ref[...] = jnp.zeros(dw_ref.shape, F32)
            ds_ref[...] = jnp.zeros(ds_ref.shape, F32)
            later[...] = jnp.zeros(later.shape, F32)

        for sub in reversed(range(POOL_SUBS)):
            rows = slice(sub * POOL_TILE, (sub + 1) * POOL_TILE)
            tile = (steps - 1 - n) * POOL_SUBS + sub
            dpo = _dot(dpu_ref[rows, :], wpu_ref[...], "nt")
            for g, window in enumerate(POOL_WINDOWS):
                sl = slice(g * GROUP_DIM, (g + 1) * GROUP_DIM)
                dc, sc = dpo[:, sl], sc_ref[:, sl]
                dmix = (dc * sc).astype(BF16)
                dpm = _dot(dmix, w_ref[g], "nt")
                e = dpm / _pool_count(tile, window)
                e_hi, e_lo = _split2(e)
                f_hi, f_lo = _split2(later[g])
                mctv, mptv = mct_ref[g], mpt_ref[g]
                back = _dot(mctv, e_hi) + _dot(mctv, e_lo)
                after = _dot(mptv, f_hi) + _dot(mptv, f_lo)
                dp_ref[rows, sl] = (back + after - dpm).astype(BF16)
                later[g] = e
                dw_ref[g] += _dot(pm_ref[rows, sl], dmix, "tn")
                ds_ref[:, sl] += jnp.sum(dc * mix_ref[rows, sl], axis=0, keepdims=True)

    return pl.pallas_call(
        body, name=name, grid=(steps,),
        in_specs=[cur, pl.BlockSpec((D_MODEL, D_MODEL), lambda n: (0, 0), pipeline_mode=pl.Buffered(1)), cur, cur, tab, tab,
                  wspec, sspec],
        out_specs=[cur, wspec, sspec],
        out_shape=[jax.ShapeDtypeStruct((t, D_MODEL), BF16), jax.ShapeDtypeStruct((GROUPS, GROUP_DIM, GROUP_DIM), F32),
                   jax.ShapeDtypeStruct((1, D_MODEL), F32)],
        scratch_shapes=[pltpu.VMEM((GROUPS, POOL_TILE, GROUP_DIM), F32)],
        compiler_params=_cparams(("arbitrary",)),
    )(dpu, w_pu, pm, mixed, mct, mpt, pool_w, scale)


GATE0_BLOCK, GATE1_BLOCK = 5, 6


def _merge_fwd(name, ret, po, w_ru, w_pu, w_out, proj, bias, h, next_g, tm=512):
    t = ret.shape[0]

    def body(r_ref, p_ref, wr_ref, wp_ref, wo_ref, g0_ref, g1_ref, b_ref, h_ref, ng_ref, m_ref, ru_ref, pu_ref, ho_ref, n_ref):
        ru = _dot(r_ref[...], wr_ref[...])
        pu = _dot(p_ref[...], wp_ref[...])
        ru_ref[...] = ru
        pu_ref[...] = pu
        merged = (_sigmoid(g0_ref[...].astype(F32) + b_ref[0:1, :]) * ru
                  + _sigmoid(g1_ref[...].astype(F32) + b_ref[1:2, :]) * pu).astype(BF16)
        m_ref[...] = merged
        h_new = h_ref[...] + _dot(merged, wo_ref[...])
        ho_ref[...] = h_new
        n_ref[...] = _normed(h_new, ng_ref[...]).astype(BF16)

    row = _row_spec(tm, D_MODEL)
    wspec = pl.BlockSpec((D_MODEL, D_MODEL), lambda i: (0, 0), pipeline_mode=pl.Buffered(1))
    return pl.pallas_call(
        body, name=name, grid=(t // tm,),
        in_specs=[row, row, wspec, wspec, wspec, _row_spec(tm, D_MODEL, GATE0_BLOCK), _row_spec(tm, D_MODEL, GATE1_BLOCK),
                  _full_spec((2, D_MODEL)), row, _full_spec((1, D_MODEL))],
        out_specs=[row] * 5,
        out_shape=[jax.ShapeDtypeStruct((t, D_MODEL), BF16), jax.ShapeDtypeStruct((t, D_MODEL), F32),
                   jax.ShapeDtypeStruct((t, D_MODEL), F32), jax.ShapeDtypeStruct((t, D_MODEL), F32),
                   jax.ShapeDtypeStruct((t, D_MODEL), BF16)],
        compiler_params=_cparams(("parallel",)),
    )(ret, po, w_ru, w_pu, w_out, proj, proj, bias, h, next_g)


def _merge_bwd(name, dh_b, w_out, ru, pu, proj, bias, tm=512):
    t = dh_b.shape[0]

    def body(dh_ref, wo_ref, ru_ref, pu_ref, g0_ref, g1_ref, b_ref, dru_ref, dpu_ref, dg0_ref, dg1_ref, db_ref):
        i = pl.program_id(0)
        d = _dot(dh_ref[...], wo_ref[...], "nt")
        s0 = _sigmoid(g0_ref[...].astype(F32) + b_ref[0:1, :])
        s1 = _sigmoid(g1_ref[...].astype(F32) + b_ref[1:2, :])
        dru_ref[...] = (d * s0).astype(BF16)
        dpu_ref[...] = (d * s1).astype(BF16)
        dg0 = d * ru_ref[...] * (s0 * (1.0 - s0))
        dg1 = d * pu_ref[...] * (s1 * (1.0 - s1))
        dg0_ref[...] = dg0.astype(BF16)
        dg1_ref[...] = dg1.astype(BF16)
        part0 = jnp.sum(dg0, axis=0, keepdims=True)
        part1 = jnp.sum(dg1, axis=0, keepdims=True)

        @pl.when(i == 0)
        def _():
            db_ref[0:1, :] = part0
            db_ref[1:2, :] = part1

        @pl.when(i > 0)
        def _():
            db_ref[0:1, :] += part0
            db_ref[1:2, :] += part1

    row = _row_spec(tm, D_MODEL)
    return pl.pallas_call(
        body, name=name, grid=(t // tm,),
        in_specs=[row, pl.BlockSpec((D_MODEL, D_MODEL), lambda i: (0, 0), pipeline_mode=pl.Buffered(1)), row, row,
                  _row_spec(tm, D_MODEL, GATE0_BLOCK), _row_spec(tm, D_MODEL, GATE1_BLOCK), _full_spec((2, D_MODEL))],
        out_specs=[row, row, row, row, _full_spec((2, D_MODEL))],
        out_shape=[jax.ShapeDtypeStruct((t, D_MODEL), BF16)] * 4 + [jax.ShapeDtypeStruct((2, D_MODEL), F32)],
        compiler_params=_cparams(("arbitrary",)),
    )(dh_b, w_out, ru, pu, proj, proj, bias)


def _half_scale(acc):
    return (FFN_RES_WEIGHT * acc,)


def _normed(h, g):
    return h * lax.rsqrt(jnp.mean(h * h, axis=-1, keepdims=True) + NORM_EPS) * g


def _residual_half_norm(acc, res, g):
    h = res + FFN_RES_WEIGHT * acc
    return h, _normed(h, g)


FF_TILE = D_FF // 2
DW_TILE = 256
SAVED_FF_DTYPE = BF16
FF_CHUNKS = ((0, 512), (512, 1024), (1024, FF_TILE))


def _ffn_in(name, nrm, w_in, tm=512):
    t = nrm.shape[0]
    nj = D_FF // FF_TILE

    def body(n_ref, wg_ref, wu_ref, a_ref, mid_ref):
        nv = n_ref[...]
        for c0, c1 in FF_CHUNKS:
            gate = _dot(nv, wg_ref[:, c0:c1])
            up = _dot(nv, wu_ref[:, c0:c1])
            s = _sigmoid(gate)
            silu = gate * s
            a_ref[0, :, c0:c1] = (FFN_RES_WEIGHT * up * (s * (1.0 + gate * (1.0 - s)))).astype(a_ref.dtype)
            a_ref[1, :, c0:c1] = (FFN_RES_WEIGHT * silu).astype(a_ref.dtype)
            mid_ref[:, c0:c1] = (silu * up).astype(BF16)

    return pl.pallas_call(
        body, name=name, grid=(nj, t // tm),
        in_specs=[pl.BlockSpec((tm, D_MODEL), lambda j, i: (i, 0)),
                  pl.BlockSpec((D_MODEL, FF_TILE), lambda j, i: (0, j)),
                  pl.BlockSpec((D_MODEL, FF_TILE), lambda j, i: (0, j + nj))],
        out_specs=[pl.BlockSpec((2, tm, FF_TILE), lambda j, i: (0, i, j)), pl.BlockSpec((tm, FF_TILE), lambda j, i: (i, j))],
        out_shape=[jax.ShapeDtypeStruct((2, t, D_FF), SAVED_FF_DTYPE), jax.ShapeDtypeStruct((t, D_FF), BF16)],
        compiler_params=_cparams(("parallel", "parallel")),
    )(nrm, w_in, w_in)


def _ffn_dact(name, dout_b, w_out, a, tm=512):
    t = dout_b.shape[0]

    def body(d_ref, w_ref, a_ref, da_ref):
        dv = d_ref[...]
        for c0, c1 in FF_CHUNKS:
            dm = _dot(dv, w_ref[c0:c1, :], "nt")
            da_ref[0, :, c0:c1] = (dm * a_ref[0, :, c0:c1].astype(F32)).astype(BF16)
            da_ref[1, :, c0:c1] = (dm * a_ref[1, :, c0:c1].astype(F32)).astype(BF16)

    blk = pl.BlockSpec((2, tm, FF_TILE), lambda j, i: (0, i, j))
    return pl.pallas_call(
        body, name=name, grid=(D_FF // FF_TILE, t // tm),
        in_specs=[pl.BlockSpec((tm, D_MODEL), lambda j, i: (i, 0)), pl.BlockSpec((FF_TILE, D_MODEL), lambda j, i: (j, 0)), blk],
        out_specs=blk,
        out_shape=jax.ShapeDtypeStruct((2, t, D_FF), BF16),
        compiler_params=_cparams(("parallel", "parallel")),
    )(dout_b, w_out, a)


def _ffn_fwd(tag, h, nrm, get_w_in, get_w_out, finish):
    t = h.shape[0]
    w_in = get_w_in(nrm)
    a, mid = _ffn_in(f"{tag}_in", nrm, w_in, tm=min(512, t))
    w_out = get_w_out(mid)
    return finish(mid, w_out), (nrm, a, mid, w_in, w_out)


def _ffn_bwd(tag, h, g, saved, dout, dout_b, on_grads, flush):
    t = h.shape[0]
    nrm, a, mid, w_in, w_out = saved
    d_w_out = _matmul(f"{tag}_dwout", mid, dout_b, "tn", D_FF, D_MODEL, t, DW_TILE, D_MODEL, t, [BF16], epilogue=_half_scale,
                      resident="b")
    da = _ffn_dact(f"{tag}_dact", dout_b, w_out, a, tm=min(512, t))
    nj = D_FF // DW_TILE
    d_w_in = _dw_resident(f"{tag}_dwin", nrm, [da], [pl.BlockSpec((None, t, DW_TILE), lambda s: (s // nj, 0, s % nj))],
                          2 * nj, None, DW_TILE)
    tie = on_grads({f"{tag}_w_in": d_w_in, f"{tag}_w_out": d_w_out})
    tm = min(256, t)
    dh, dh_b, dg = _proj_norm_bwd(f"{tag}_dn", [da], [pl.BlockSpec((2, tm, D_FF), lambda i: (0, i, 0))],
                                  ((0, 0, 0, D_FF), (0, 1, D_FF, 2 * D_FF)), w_in, h, g if tie is None else g + tie, dout, tm)
    return dh, dh_b, dg, flush(dh)


def _dw_resident(name, u, pieces, piece_specs, n_tiles, which_piece, tn):
    t = u.shape[0]
    npc = len(pieces)

    def body(*refs):
        u_ref, p_refs, o_ref, ut_ref = refs[0], refs[1:1 + npc], refs[1 + npc], refs[2 + npc]
        s = pl.program_id(0)

        @pl.when(s == 0)
        def _():
            ut_ref[...] = u_ref[...].T

        if npc == 1:
            o_ref[...] = _dot(ut_ref[...], p_refs[0][...]).astype(BF16)
        for which in range(npc if npc > 1 else 0):
            @pl.when(which_piece(s) == which)
            def _(which=which):
                o_ref[...] = _dot(ut_ref[...], p_refs[which][...]).astype(BF16)

    return pl.pallas_call(
        body, name=name, grid=(n_tiles,),
        in_specs=[pl.BlockSpec((t, D_MODEL), lambda s: (0, 0), pipeline_mode=pl.Buffered(1))] + list(piece_specs),
        out_specs=pl.BlockSpec((D_MODEL, tn), lambda s: (0, s)),
        out_shape=jax.ShapeDtypeStruct((D_MODEL, n_tiles * tn), BF16),
        scratch_shapes=[pltpu.VMEM((D_MODEL, t), BF16)],
        compiler_params=_cparams(("arbitrary",)),
    )(u, *pieces)


def _mix_dwin(name, u, pieces, tn=256):
    t = u.shape[0]
    nj = D_MODEL // tn
    specs = [pl.BlockSpec((t, tn), lambda s, k=k: (0, jnp.clip(s - k * nj, 0, nj - 1))) for k in range(len(pieces))]
    return _dw_resident(name, u, pieces, specs, len(pieces) * nj, lambda s: s // nj, tn)


def _local_step(x, target, vec, get_w, relay_w, on_grads, flush):
    t = x.shape[0]
    cos, sin = _rope_tables(t)
    rtab = _retention_tables()
    ptab = _pool_tables()
    w = {}

    def getter(group, name):
        def get(after):
            if name not in w:
                w.update(get_w(group, after))
            return w[name]
        return get

    nrm1 = _rmsnorm_fwd("ffn1_norm", x, vec["norm_ffn1"])
    def out_and_norm(mid, w_out):
        return _matmul("ffn1_out", mid, w_out, "nn", t, D_MODEL, D_FF, 512, D_MODEL, D_FF, [F32, BF16],
                       extras=(x,), consts=(vec["norm_mix"],), epilogue=_residual_half_norm)

    (h1, u), s1 = _ffn_fwd("ffn1", x, nrm1, getter(0, "ffn1_w_in"), getter(1, "ffn1_w_out"), out_and_norm)
    w.update(get_w(2, u))
    relay_w(3, w["w_in"])
    proj = _matmul("mix_in", u, w["w_in"], "nn", t, IN_WIDTH, D_MODEL, 2048, 1024, D_MODEL, [BF16], n_outer=True)
    w.update(get_w(3, proj))
    o, ret, states = _retention_fwd("retention", proj, cos, sin, rtab)
    pm, mixed, po = _pool_fwd("pool", proj, w["pool_w"], vec["pool_scale"], ptab)
    relay_w(4, po)
    merged, ru, pu, h2, nrm2 = _merge_fwd("merge", ret, po, w["w_ret_up"], w["w_pool_up"], w["w_out"], proj, w["gate_bias"],
                                          h1, vec["norm_ffn2"], tm=min(512, t))
    def out_and_loss(mid, w_out):
        return _out_loss_and_grad("ffn2_out_loss", mid, w_out, h2, vec["norm_final"], target, tm=min(512, t))

    (dh3, dh3_b, dg_final, loss), s2 = _ffn_fwd("ffn2", h2, nrm2, getter(4, "ffn2_w_in"), getter(4, "ffn2_w_out"), out_and_loss)

    def tied(v, tie):
        return v if tie is None else v + tie

    dh2, dh2_b, dg_ffn2, tie = _ffn_bwd("ffn2", h2, vec["norm_ffn2"], s2, dh3, dh3_b, on_grads, flush)
    def square_dw(name, act, grad):
        return _matmul(name, act, grad, "tn", D_MODEL, D_MODEL, t, D_MODEL, D_MODEL, 1024, [BF16])

    d_w_out = square_dw("mix_dwout", merged, dh2_b)
    dru, dpu, dg0, dg1, d_bias = _merge_bwd("merge_bwd", dh2_b, w["w_out"], ru, pu, proj, tied(w["gate_bias"], tie))
    d_w_ru = square_dw("mix_dwru", ret, dru)
    d_w_pu = square_dw("mix_dwpu", po, dpu)
    dp, d_pool_w, d_scale = _pool_bwd("pool_bwd", dpu, w["w_pool_up"], pm, mixed, w["pool_w"], vec["pool_scale"], ptab)
    dq, dk, dv, dgr = _retention_bwd("retention_bwd", dru, w["w_ret_up"], o, proj, cos, sin, states, rtab)
    dproj = [dq, dk, dv, dgr, dp, dg0, dg1]
    d_w_in = _mix_dwin("mix_dwin", u, dproj)
    tie = on_grads(dict(w_in=d_w_in, pool_w=d_pool_w.astype(BF16), w_ret_up=d_w_ru, w_pool_up=d_w_pu, w_out=d_w_out))
    tm = min(256, t)
    dh1, dh1_b, dg_mix = _proj_norm_bwd("mix_du", dproj, [_row_spec(tm, D_MODEL)] * len(dproj),
                                        [(k, None, k * D_MODEL, (k + 1) * D_MODEL) for k in range(len(dproj))],
                                        w["w_in"], h1, tied(vec["norm_mix"], tie), dh2, tm)
    tie = flush(dh1)
    dx, _, dg_ffn1, _ = _ffn_bwd("ffn1", x, tied(vec["norm_ffn1"], tie), s1, dh1, dh1_b, on_grads, flush)

    small = dict(norm_ffn1=dg_ffn1, norm_mix=dg_mix, gate_bias=d_bias, pool_scale=d_scale, norm_ffn2=dg_ffn2,
                 norm_final=dg_final)
    return loss[0, 0], dx, small


BIG = ("ffn1_w_in", "ffn1_w_out", "w_in", "pool_w", "w_ret_up", "w_pool_up", "w_out", "ffn2_w_in", "ffn2_w_out")
KIND = dict(ffn1_w_in="col", ffn1_w_out="row", w_in="col", pool_w="pool", w_ret_up="row", w_pool_up="row", w_out="row",
            ffn2_w_in="col", ffn2_w_out="row", gate_bias="col")
ANY = pl.BlockSpec(memory_space=pl.ANY)


def _place():
    x, y, c = lax.axis_index("x"), lax.axis_index("y"), lax.axis_index("c")
    chips = [(1 - x, y), (x, 1 - y), (1 - x, 1 - y)]
    return x, y, c, chips


def _full_view_shape(kind, local_shape):
    if kind == "col":
        return (2, local_shape[0] // 2, N_CHIPS * local_shape[1])
    if kind == "row":
        return (N_CHIPS, 2, local_shape[0] // 2, local_shape[1])
    return (GROUPS, N_CHIPS, 2, local_shape[1] // 2, local_shape[2])


def _local_view(kind, arr):
    if kind == "pool":
        return arr.reshape(GROUPS, 2, arr.shape[1] // 2, arr.shape[2])
    return arr.reshape(2, arr.shape[0] // 2, arr.shape[1])


def _blk(kind, ref, s, c):
    if kind == "col":
        cs = ref.shape[2] // N_CHIPS
        return ref.at[c, :, pl.ds(pl.multiple_of(s * cs, 128), cs)]
    if kind == "row":
        return ref.at[s, c]
    return ref.at[:, s, c]


def _half(kind, ref, c):
    return ref.at[:, c] if kind == "pool" else ref.at[c]


def _shard(kind, ref, s):
    if kind == "col":
        cs = ref.shape[2] // N_CHIPS
        return ref.at[:, :, pl.ds(pl.multiple_of(s * cs, 128), cs)]
    if kind == "row":
        return ref.at[s]
    return ref.at[:, s]


HBM = pl.BlockSpec(memory_space=pltpu.HBM)
SEM = pl.BlockSpec(memory_space=pltpu.SEMAPHORE)
EFFECT = pltpu.SideEffectType.DATAFLOW_SIDE_EFFECTING
WEIGHT_GROUPS = (("gate_bias", "ffn1_w_in"), ("ffn1_w_out",), ("w_in",), ("pool_w", "w_ret_up", "w_pool_up", "w_out"),
                 ("ffn2_w_in", "ffn2_w_out"))
GRAD_GROUPS = (("ffn2_w_in", "ffn2_w_out"), ("w_in", "pool_w", "w_ret_up", "w_pool_up", "w_out"), ("ffn1_w_in", "ffn1_w_out"))


def _hbm(a):
    return pltpu.with_memory_space_constraint(a, pltpu.HBM)


def _natural(kind, o):
    if kind == "col":
        return o.reshape(o.shape[0] * o.shape[1], o.shape[2])
    if kind == "row":
        return o.reshape(-1, o.shape[3])
    return o.reshape(GROUPS, -1, o.shape[4])


def _ici_copy(kind, loc, full, j, chips, s, c, send_sem, recv_sem):
    px, py = chips[j]
    return (pltpu.make_async_remote_copy(src_ref=_half(kind, loc, c), dst_ref=_blk(kind, full, s, c), send_sem=send_sem,
                                         recv_sem=recv_sem, device_id=(px, py, c), device_id_type=MESH),
            pltpu.make_async_remote_copy(src_ref=_half(kind, loc, c), dst_ref=_blk(kind, full, 2 * px + py, c), send_sem=send_sem,
                                         recv_sem=recv_sem, device_id=(px, py, c), device_id_type=MESH))


def _gather_start(tag, group_ids, shards):
    grps = [WEIGHT_GROUPS[g] for g in group_ids]
    names = [nm for grp in grps for nm in grp]
    kinds = [KIND[nm] for nm in names]
    n, ng = len(names), len(grps)
    locs = [_hbm(_local_view(KIND[nm], shards[nm])) for nm in names]
    lands = [_hbm(lax.empty(_full_view_shape(KIND[nm], shards[nm].shape), shards[nm].dtype)) for nm in names]
    first = np.cumsum([0] + [len(grp) for grp in grps])

    def body(*refs):
        loc, full = refs[:n], refs[n:2 * n]
        send_sems, recv_sems = refs[2 * n:2 * n + ng], refs[2 * n + ng:2 * n + 2 * ng]
        token = refs[-1]
        x, y, c, chips = _place()
        s = 2 * x + y
        for g in range(ng):
            for a in range(first[g], first[g + 1]):
                for j in range(3):
                    k = 3 * (a - first[g]) + j
                    _ici_copy(kinds[a], loc[a], full[a], j, chips, s, c, send_sems[g].at[k], recv_sems[g].at[k])[0].start()
        token[...] = jnp.zeros(token.shape, F32)

    sem_shapes = [pltpu.SemaphoreType.DMA((3 * len(grp),)) for grp in grps]
    outs = pl.pallas_call(
        body, name=f"gather_start_{tag}",
        in_specs=[HBM] * (2 * n),
        out_specs=[SEM] * (2 * ng) + [HBM] * (2 * n) + [pl.BlockSpec(memory_space=pltpu.VMEM)],
        out_shape=sem_shapes + sem_shapes + [pltpu.HBM(a.shape, a.dtype) for a in locs + lands] + [jax.ShapeDtypeStruct((8, 128), F32)],
        input_output_aliases={i: 2 * ng + i for i in range(2 * n)},
        compiler_params=pltpu.CompilerParams(has_side_effects=EFFECT),
    )(*locs, *lands)
    send_sems, recv_sems = outs[:ng], outs[ng:2 * ng]
    locs_t, lands_t = outs[2 * ng:2 * ng + n], outs[2 * ng + n:2 * ng + 2 * n]
    groups = {}
    for k, g in enumerate(group_ids):
        sl = slice(first[k], first[k + 1])
        groups[g] = (send_sems[k], recv_sems[k], list(locs_t[sl]), list(lands_t[sl]))
    return groups, outs[-1]


def _forward_copies(kinds, loc, full, send_sems, recv_sems):
    x, y, c, chips = _place()
    s = 2 * x + y

    def remote(a, k, src, dst):
        return pltpu.make_async_remote_copy(src_ref=src, dst_ref=dst, send_sem=send_sems.at[4 * a + k],
                                            recv_sem=recv_sems.at[4 * a + k], device_id=(x, y, 1 - c), device_id_type=MESH)

    sends, arrivals = [], []
    for a, kind in enumerate(kinds):
        for j, (px, py) in enumerate(chips):
            theirs, from_sib = _blk(kind, full[a], 2 * px + py, c), _blk(kind, full[a], 2 * px + py, 1 - c)
            sends.append(remote(a, j, theirs, theirs))
            arrivals.append(remote(a, j, from_sib, from_sib))
        own = _shard(kind, full[a], s)
        sends.append(remote(a, 3, loc[a], own))
        arrivals.append(remote(a, 3, own, own))
    return sends, arrivals


def _gather_relay(g, group, after):
    names = WEIGHT_GROUPS[g]
    kinds = [KIND[nm] for nm in names]
    m = len(names)
    ici_send, ici_recv, locs, lands = group

    def body(*refs):
        loc, full = refs[:m], refs[m:2 * m]
        ici_s, ici_r = refs[2 * m], refs[2 * m + 1]
        d2d_s, d2d_r = refs[2 * m + 2 + len(after)], refs[2 * m + 3 + len(after)]
        x, y, c, chips = _place()
        for a in range(m):
            for j in range(3):
                sent, landed = _ici_copy(kinds[a], loc[a], full[a], j, chips, 2 * x + y, c, ici_s.at[3 * a + j], ici_r.at[3 * a + j])
                sent.wait_send()
                landed.wait_recv()
        for cp in _forward_copies(kinds, loc, full, d2d_s, d2d_r)[0]:
            cp.start()

    sem_shape = pltpu.SemaphoreType.DMA((4 * m,))
    outs = pl.pallas_call(
        body, name=f"gather_relay_{g}",
        in_specs=[HBM] * (2 * m) + [SEM, SEM] + [ANY] * len(after), out_specs=[SEM, SEM] + [HBM] * (2 * m),
        out_shape=[sem_shape, sem_shape] + [pltpu.HBM(a.shape, a.dtype) for a in locs + lands],
        input_output_aliases={i: 2 + i for i in range(2 * m)},
        compiler_params=pltpu.CompilerParams(has_side_effects=EFFECT),
    )(*locs, *lands, ici_send, ici_recv, *after)
    return outs[0], outs[1], list(outs[2:2 + m]), list(outs[2 + m:2 + 2 * m])


def _gather_land(g, state, after):
    names = WEIGHT_GROUPS[g]
    kinds = [KIND[nm] for nm in names]
    m = len(names)
    d2d_send, d2d_recv, locs, lands = state

    def body(*refs):
        sends, arrivals = _forward_copies(kinds, refs[:m], refs[m:2 * m], refs[2 * m], refs[2 * m + 1])
        for cp in sends:
            cp.wait_send()
        for cp in arrivals:
            cp.wait_recv()

    outs = pl.pallas_call(
        body, name=f"gather_land_{g}",
        in_specs=[HBM] * (2 * m) + [SEM, SEM] + [ANY] * len(after), out_specs=[HBM] * (2 * m),
        out_shape=[pltpu.HBM(a.shape, a.dtype) for a in locs + lands],
        input_output_aliases={i: i for i in range(2 * m)},
        compiler_params=pltpu.CompilerParams(has_side_effects=EFFECT),
    )(*locs, *lands, d2d_send, d2d_recv, *after)
    return {nm: _natural(k, o) for nm, k, o in zip(names, kinds, outs[m:])}


def _gather_finish(g, group, after):
    names = WEIGHT_GROUPS[g]
    kinds = [KIND[nm] for nm in names]
    m = len(names)
    send_sem, recv_sem, locs, lands = group

    def wait_body(*refs):
        loc, full = refs[:m], refs[m:2 * m]
        send_sems, recv_sems = refs[2 * m], refs[2 * m + 1]
        x, y, c, chips = _place()
        s = 2 * x + y
        for a in range(m):
            for j in range(3):
                k = 3 * a + j
                sent, landed = _ici_copy(kinds[a], loc[a], full[a], j, chips, s, c, send_sems.at[k], recv_sems.at[k])
                sent.wait_send()
                landed.wait_recv()

    outs = pl.pallas_call(
        wait_body, name=f"gather_wait_{g}",
        in_specs=[HBM] * (2 * m) + [SEM, SEM] + [ANY] * len(after), out_specs=[HBM] * (2 * m),
        out_shape=[pltpu.HBM(a.shape, a.dtype) for a in locs + lands],
        input_output_aliases={i: i for i in range(2 * m)},
        compiler_params=pltpu.CompilerParams(has_side_effects=EFFECT),
    )(*locs, *lands, send_sem, recv_sem, *after)
    locs, lands = outs[:m], outs[m:]

    def forward_body(*refs):
        sends, arrivals = _forward_copies(kinds, refs[:m], refs[2 * m:3 * m], *refs[3 * m:])
        for cp in sends:
            cp.start()
        for cp in arrivals:
            cp.wait_recv()
        for cp in sends:
            cp.wait_send()

    outs = pl.pallas_call(
        forward_body, name=f"gather_forward_{g}",
        in_specs=[ANY] * (2 * m), out_specs=[ANY] * m,
        out_shape=[jax.ShapeDtypeStruct(a.shape, a.dtype) for a in lands],
        input_output_aliases={m + i: i for i in range(m)},
        scratch_shapes=[pltpu.SemaphoreType.DMA((4 * m,)), pltpu.SemaphoreType.DMA((4 * m,))],
    )(*locs, *lands)
    return {nm: _natural(k, o) for nm, k, o in zip(names, kinds, outs)}


def _grad_view(kind, g):
    if kind == "col":
        return g.reshape(2, g.shape[0] // 2, g.shape[1])
    if kind == "row":
        return g.reshape(N_CHIPS, 2, g.shape[0] // (2 * N_CHIPS), g.shape[1])
    return g.reshape(GROUPS, N_CHIPS, 2, g.shape[1] // (2 * N_CHIPS), g.shape[2])


def _pair_copies(kinds, g, got, send_sems, recv_sems):
    x, y, c, _ = _place()

    def other_half(kind, ref):
        if kind == "col":
            return ref.at[1 - c]
        if kind == "row":
            return ref.at[:, 1 - c]
        return ref.at[:, :, 1 - c]

    return [pltpu.make_async_remote_copy(src_ref=other_half(kinds[a], g[a]), dst_ref=got[a], send_sem=send_sems.at[a],
                                         recv_sem=recv_sems.at[a], device_id=(x, y, 1 - c), device_id_type=MESH)
            for a in range(len(kinds))]


def _pair_exchange_start(tag, names, views):
    kinds = [KIND[nm] for nm in names]
    n = len(names)

    def got_shape(kind, v):
        if kind == "col":
            return v.shape[1:]
        if kind == "row":
            return (v.shape[0],) + v.shape[2:]
        return v.shape[:2] + v.shape[3:]

    srcs = [_hbm(views[nm]) for nm in names]
    lands = [_hbm(lax.empty(got_shape(k, views[nm]), BF16)) for nm, k in zip(names, kinds)]

    def body(*refs):
        g, got = refs[:n], refs[n:2 * n]
        for cp in _pair_copies(kinds, g, got, refs[2 * n], refs[2 * n + 1]):
            cp.start()
        refs[-1][...] = jnp.zeros(refs[-1].shape, F32)

    sem_shape = pltpu.SemaphoreType.DMA((n,))
    outs = pl.pallas_call(
        body, name=f"grad_pair_exchange_start_{tag}",
        in_specs=[HBM] * (2 * n),
        out_specs=[SEM, SEM] + [HBM] * (2 * n) + [pl.BlockSpec(memory_space=pltpu.VMEM)],
        out_shape=[sem_shape, sem_shape] + [pltpu.HBM(a.shape, a.dtype) for a in srcs + lands] + [jax.ShapeDtypeStruct((8, 128), F32)],
        input_output_aliases={i: 2 + i for i in range(2 * n)},
        compiler_params=pltpu.CompilerParams(has_side_effects=EFFECT),
    )(*srcs, *lands)
    return (outs[0], outs[1], list(outs[2:2 + n]), list(outs[2 + n:2 + 2 * n])), outs[-1]


def _pair_exchange_wait(tag, names, state, after):
    kinds = [KIND[nm] for nm in names]
    n = len(names)
    send_sem, recv_sem, srcs, lands = state

    def body(*refs):
        g, got = refs[:n], refs[n:2 * n]
        for cp in _pair_copies(kinds, g, got, refs[2 * n], refs[2 * n + 1]):
            cp.wait_send()
            cp.wait_recv()

    outs = pl.pallas_call(
        body, name=f"grad_pair_exchange_wait_{tag}",
        in_specs=[HBM] * (2 * n) + [SEM, SEM, ANY], out_specs=[HBM] * (2 * n),
        out_shape=[pltpu.HBM(a.shape, a.dtype) for a in srcs + lands],
        input_output_aliases={i: i for i in range(2 * n)},
        compiler_params=pltpu.CompilerParams(has_side_effects=EFFECT),
    )(*srcs, *lands, send_sem, recv_sem, after)
    return dict(zip(names, outs[:n])), dict(zip(names, outs[n:]))


def _pair_sum(name, kind, view, got, c_arr):
    if kind == "col":
        _, rows, cols = view.shape
        tr = 128
        grid = (rows // tr,)
        v_spec = pl.BlockSpec((None, tr, cols), lambda i, c: (c[0], i, 0))
        g_spec = pl.BlockSpec((tr, cols), lambda i, c: (i, 0))
    elif kind == "row":
        _, _, rows, cols = view.shape
        grid = (N_CHIPS,)
        v_spec = pl.BlockSpec((None, None, rows, cols), lambda i, c: (i, c[0], 0, 0))
        g_spec = pl.BlockSpec((None, rows, cols), lambda i, c: (i, 0, 0))
    else:
        _, _, _, rows, cols = view.shape
        grid = (GROUPS,)
        v_spec = pl.BlockSpec((None, N_CHIPS, None, rows, cols), lambda i, c: (i, 0, c[0], 0, 0))
        g_spec = pl.BlockSpec((None, N_CHIPS, rows, cols), lambda i, c: (i, 0, 0, 0))

    def body(c_ref, v_ref, g_ref, o_ref):
        o_ref[...] = (v_ref[...].astype(F32) + g_ref[...].astype(F32)).astype(BF16)

    return pl.pallas_call(
        body, name=name,
        grid_spec=pltpu.PrefetchScalarGridSpec(num_scalar_prefetch=1, grid=grid, in_specs=[v_spec, g_spec], out_specs=g_spec),
        out_shape=jax.ShapeDtypeStruct(got.shape, BF16),
        compiler_params=_cparams(("parallel",)),
    )(c_arr, view, got)


def _piece(kind, ref, s):
    if kind == "col":
        cs = ref.shape[1] // N_CHIPS
        return ref.at[:, pl.ds(pl.multiple_of(s * cs, 128), cs)]
    if kind == "row":
        return ref.at[s]
    return ref.at[:, s]


def _piece_shape(kind, shape):
    if kind == "col":
        return (shape[0], shape[1] // N_CHIPS)
    if kind == "row":
        return shape[1:]
    return (shape[0],) + shape[2:]


def _shard_copies(kinds, p, got, send_sems, recv_sems):
    x, y, c, chips = _place()
    return [pltpu.make_async_remote_copy(src_ref=_piece(kinds[a], p[a], 2 * px + py), dst_ref=got[a].at[j],
                                         send_sem=send_sems.at[3 * a + j], recv_sem=recv_sems.at[3 * a + j],
                                         device_id=(px, py, c), device_id_type=MESH)
            for a in range(len(kinds)) for j, (px, py) in enumerate(chips)]


def _shard_exchange_start(g, names, psums):
    kinds = [KIND[nm] for nm in names]
    n = len(names)
    srcs = [_hbm(psums[nm]) for nm in names]
    lands = [_hbm(lax.empty((3,) + _piece_shape(k, psums[nm].shape), BF16)) for nm, k in zip(names, kinds)]

    def body(*refs):
        p, got = refs[:n], refs[n:2 * n]
        send_sems, recv_sems = refs[2 * n], refs[2 * n + 1]
        token = refs[-1]
        for cp in _shard_copies(kinds, p, got, send_sems, recv_sems):
            cp.start()
        token[...] = jnp.zeros(token.shape, F32)

    sem_shape = pltpu.SemaphoreType.DMA((3 * n,))
    outs = pl.pallas_call(
        body, name=f"grad_shard_exchange_start_{g}",
        in_specs=[HBM] * (2 * n),
        out_specs=[SEM, SEM] + [HBM] * (2 * n) + [pl.BlockSpec(memory_space=pltpu.VMEM)],
        out_shape=[sem_shape, sem_shape] + [pltpu.HBM(a.shape, a.dtype) for a in srcs + lands] + [jax.ShapeDtypeStruct((8, 128), F32)],
        input_output_aliases={i: 2 + i for i in range(2 * n)},
        compiler_params=pltpu.CompilerParams(has_side_effects=EFFECT),
    )(*srcs, *lands)
    return (outs[0], outs[1], list(outs[2:2 + n]), list(outs[2 + n:2 + 2 * n])), outs[-1]


def _shard_exchange_wait(g, names, state, after):
    kinds = [KIND[nm] for nm in names]
    n = len(names)
    send_sem, recv_sem, srcs, lands = state

    def body(*refs):
        p, got = refs[:n], refs[n:2 * n]
        for cp in _shard_copies(kinds, p, got, refs[2 * n], refs[2 * n + 1]):
            cp.wait_send()
            cp.wait_recv()

    outs = pl.pallas_call(
        body, name=f"grad_shard_exchange_wait_{g}",
        in_specs=[HBM] * (2 * n) + [SEM, SEM] + [ANY] * len(after), out_specs=[HBM] * (2 * n),
        out_shape=[pltpu.HBM(a.shape, a.dtype) for a in srcs + lands],
        input_output_aliases={i: i for i in range(2 * n)},
        compiler_params=pltpu.CompilerParams(has_side_effects=EFFECT),
    )(*srcs, *lands, send_sem, recv_sem, *after)
    return dict(zip(names, outs[:n])), dict(zip(names, outs[n:]))


def _shard_sum(name, kind, psum, got, sc_arr):
    if kind == "col":
        rows, cols = psum.shape
        cs = cols // N_CHIPS
        tr = 128
        grid = (rows // tr,)
        p_spec = pl.BlockSpec((tr, cs), lambda i, sc: (i, sc[0]))
        g_spec = pl.BlockSpec((3, tr, cs), lambda i, sc: (0, i, 0))
        o_spec = pl.BlockSpec((None, tr, cs), lambda i, sc: (sc[1], i, 0))
        out_shape = (2, rows, cs)
    elif kind == "row":
        _, rows, cols = psum.shape
        grid = (1,)
        p_spec = pl.BlockSpec((None, rows, cols), lambda i, sc: (sc[0], 0, 0))
        g_spec = pl.BlockSpec((3, rows, cols), lambda i, sc: (0, 0, 0))
        o_spec = pl.BlockSpec((None, rows, cols), lambda i, sc: (sc[1], 0, 0))
        out_shape = (2, rows, cols)
    else:
        _, _, rows, cols = psum.shape
        grid = (1,)
        p_spec = pl.BlockSpec((GROUPS, None, rows, cols), lambda i, sc: (0, sc[0], 0, 0))
        g_spec = pl.BlockSpec((3, GROUPS, rows, cols), lambda i, sc: (0, 0, 0, 0))
        o_spec = pl.BlockSpec((GROUPS, None, rows, cols), lambda i, sc: (0, sc[1], 0, 0))
        out_shape = (GROUPS, 2, rows, cols)

    def body(sc_ref, p_ref, g_ref, o_ref):
        o_ref[...] = ((p_ref[...].astype(F32) + g_ref[0].astype(F32)) + g_ref[1].astype(F32)) + g_ref[2].astype(F32)

    return pl.pallas_call(
        body, name=name,
        grid_spec=pltpu.PrefetchScalarGridSpec(num_scalar_prefetch=1, grid=grid, in_specs=[p_spec, g_spec], out_specs=o_spec),
        out_shape=jax.ShapeDtypeStruct(out_shape, F32),
        compiler_params=_cparams(("parallel",)),
    )(sc_arr, psum, got)


def _half_copies(kinds, bufs, send_sems, recv_sems):
    x, y, c, _ = _place()

    def remote(a, half):
        part = _half(kinds[a], bufs[a], half)
        return pltpu.make_async_remote_copy(src_ref=part, dst_ref=part, send_sem=send_sems.at[a], recv_sem=recv_sems.at[a],
                                            device_id=(x, y, 1 - c), device_id_type=MESH)

    return [remote(a, c) for a in range(len(kinds))], [remote(a, 1 - c) for a in range(len(kinds))]


def _half_exchange_start(tag, names, bufs):
    kinds = [KIND[nm] for nm in names]
    n = len(names)
    arrs = [_hbm(bufs[nm]) for nm in names]

    def body(*refs):
        for cp in _half_copies(kinds, refs[:n], refs[n], refs[n + 1])[0]:
            cp.start()
        refs[-1][...] = jnp.zeros(refs[-1].shape, F32)

    sem_shape = pltpu.SemaphoreType.DMA((n,))
    outs = pl.pallas_call(
        body, name=f"grad_half_exchange_start_{tag}",
        in_specs=[HBM] * n,
        out_specs=[SEM, SEM] + [HBM] * n + [pl.BlockSpec(memory_space=pltpu.VMEM)],
        out_shape=[sem_shape, sem_shape] + [pltpu.HBM(a.shape, a.dtype) for a in arrs] + [jax.ShapeDtypeStruct((8, 128), F32)],
        input_output_aliases={i: 2 + i for i in range(n)},
        compiler_params=pltpu.CompilerParams(has_side_effects=EFFECT),
    )(*arrs)
    return (outs[0], outs[1], list(outs[2:2 + n])), outs[-1]


def _half_exchange_wait(tag, names, state, after):
    kinds = [KIND[nm] for nm in names]
    n = len(names)
    send_sem, recv_sem, arrs = state

    def body(*refs):
        sends, arrivals = _half_copies(kinds, refs[:n], refs[n], refs[n + 1])
        for cp in sends:
            cp.wait_send()
        for cp in arrivals:
            cp.wait_recv()

    outs = pl.pallas_call(
        body, name=f"grad_half_exchange_wait_{tag}",
        in_specs=[HBM] * n + [SEM, SEM] + [ANY] * len(after), out_specs=[HBM] * n,
        out_shape=[pltpu.HBM(a.shape, a.dtype) for a in arrs],
        input_output_aliases={i: i for i in range(n)},
        compiler_params=pltpu.CompilerParams(has_side_effects=EFFECT),
    )(*arrs, send_sem, recv_sem, *after)
    return dict(zip(names, outs))


N_DEV = 8
SMALL_ROWS = 8


def _all_reduce_small(name, v):
    def body(v_ref, o_ref, token, buf, send_sems, recv_sems):
        token[...] = jnp.zeros(token.shape, F32)
        x, y, c, _ = _place()
        me = 4 * x + 2 * y + c
        buf[me] = v_ref[...]
        cps = []
        for r in range(1, N_DEV):
            to = (x ^ (r >> 2), y ^ ((r >> 1) & 1), c ^ (r & 1))
            cp = pltpu.make_async_remote_copy(src_ref=v_ref, dst_ref=buf.at[me], send_sem=send_sems.at[r - 1],
                                              recv_sem=recv_sems.at[r - 1], device_id=to, device_id_type=MESH)
            cp.start()
            cps.append(cp)
        for r in range(1, N_DEV):
            pltpu.make_async_remote_copy(src_ref=v_ref, dst_ref=buf.at[me ^ r], send_sem=send_sems.at[r - 1],
                                         recv_sem=recv_sems.at[r - 1], device_id=(x, y, c), device_id_type=MESH).wait_recv()
        for cp in cps:
            cp.wait_send()
        acc = buf[0]
        for d in range(1, N_DEV):
            acc = acc + buf[d]
        o_ref[...] = acc

    vm = pl.BlockSpec(memory_space=pltpu.VMEM)
    return pl.pallas_call(
        body, name=name, in_specs=[vm], out_specs=[vm, vm],
        out_shape=[jax.ShapeDtypeStruct((SMALL_ROWS, D_MODEL), F32), jax.ShapeDtypeStruct((8, 128), F32)],
        scratch_shapes=[pltpu.VMEM((N_DEV, SMALL_ROWS, D_MODEL), F32), pltpu.SemaphoreType.DMA((N_DEV - 1,)),
                        pltpu.SemaphoreType.DMA((N_DEV - 1,))],
    )(v)


def _adamw(name, w, g, m, v, with_grad=False):
    rows, cols = w.shape
    tr = next((c for c in (256, 176, 128, 64, 32, 8) if rows % c == 0), rows)
    spec = pl.BlockSpec((tr, cols), lambda i: (i, 0))

    def body(w_ref, g_ref, m_ref, v_ref, d_ref, mo_ref, vo_ref, *go_ref):
        gv = g_ref[...]
        if with_grad:
            go_ref[0][...] = gv
        m_new = ADAM_B1 * m_ref[...] + (1.0 - ADAM_B1) * gv
        v_new = ADAM_B2 * v_ref[...] + (1.0 - ADAM_B2) * jnp.square(gv)
        m_hat = m_new / (1.0 - ADAM_B1 ** ADAM_STEP)
        v_hat = v_new / (1.0 - ADAM_B2 ** ADAM_STEP)
        d_ref[...] = -ADAM_LR * (m_hat / (jnp.sqrt(v_hat) + ADAM_EPS) + ADAM_WD * w_ref[...])
        mo_ref[...] = m_new
        vo_ref[...] = v_new

    return pl.pallas_call(
        body, name=name, grid=(rows // tr,),
        in_specs=[spec] * 4, out_specs=[spec] * (4 if with_grad else 3),
        out_shape=[jax.ShapeDtypeStruct((rows, cols), F32)] * (4 if with_grad else 3),
        compiler_params=_cparams(("parallel",)),
    )(w, g, m, v)


WEIGHTS = ("norm_ffn1", "ffn1_w_in", "ffn1_w_out", "norm_mix", "w_in", "gate_bias", "pool_w", "pool_scale", "w_ret_up",
           "w_pool_up", "w_out", "norm_ffn2", "ffn2_w_in", "ffn2_w_out", "norm_final")
SMALL_ROW = dict(norm_ffn1=0, norm_mix=1, gate_bias=2, pool_scale=4, norm_ffn2=5, norm_final=6)


def _as2d(a):
    return a.reshape(-1, a.shape[-1])


def kernel(x, norm_ffn1, ffn1_w_in, ffn1_w_out, norm_mix, w_in, gate_bias, pool_w, pool_scale, w_ret_up, w_pool_up, w_out, norm_ffn2, ffn2_w_in, ffn2_w_out, norm_final, loss_target, m_norm_ffn1, m_ffn1_w_in, m_ffn1_w_out, m_norm_mix, m_w_in, m_gate_bias, m_pool_w, m_pool_scale, m_w_ret_up, m_w_pool_up, m_w_out, m_norm_ffn2, m_ffn2_w_in, m_ffn2_w_out, m_norm_final, v_norm_ffn1, v_ffn1_w_in, v_ffn1_w_out, v_norm_mix, v_w_in, v_gate_bias, v_pool_w, v_pool_scale, v_w_ret_up, v_w_pool_up, v_w_out, v_norm_ffn2, v_ffn2_w_in, v_ffn2_w_out, v_norm_final):
    wt = dict(norm_ffn1=norm_ffn1, ffn1_w_in=ffn1_w_in, ffn1_w_out=ffn1_w_out, norm_mix=norm_mix, w_in=w_in, gate_bias=gate_bias,
              pool_w=pool_w, pool_scale=pool_scale, w_ret_up=w_ret_up, w_pool_up=w_pool_up, w_out=w_out, norm_ffn2=norm_ffn2,
              ffn2_w_in=ffn2_w_in, ffn2_w_out=ffn2_w_out, norm_final=norm_final)
    mom = dict(norm_ffn1=m_norm_ffn1, ffn1_w_in=m_ffn1_w_in, ffn1_w_out=m_ffn1_w_out, norm_mix=m_norm_mix, w_in=m_w_in,
               gate_bias=m_gate_bias, pool_w=m_pool_w, pool_scale=m_pool_scale, w_ret_up=m_w_ret_up, w_pool_up=m_w_pool_up,
               w_out=m_w_out, norm_ffn2=m_norm_ffn2, ffn2_w_in=m_ffn2_w_in, ffn2_w_out=m_ffn2_w_out, norm_final=m_norm_final)
    var = dict(norm_ffn1=v_norm_ffn1, ffn1_w_in=v_ffn1_w_in, ffn1_w_out=v_ffn1_w_out, norm_mix=v_norm_mix, w_in=v_w_in,
               gate_bias=v_gate_bias, pool_w=v_pool_w, pool_scale=v_pool_scale, w_ret_up=v_w_ret_up, w_pool_up=v_w_pool_up,
               w_out=v_w_out, norm_ffn2=v_norm_ffn2, ffn2_w_in=v_ffn2_w_in, ffn2_w_out=v_ffn2_w_out, norm_final=v_norm_final)

    ax, ay, ac = lax.axis_index("x"), lax.axis_index("y"), lax.axis_index("c")
    chip = 2 * ax + ay
    c_arr = jnp.reshape(ac, (1,)).astype(jnp.int32)
    sc_arr = jnp.stack([chip, ac]).astype(jnp.int32)
    bias_cols = gate_bias.shape[-1]

    first = {"gate_bias": gate_bias[0], "ffn1_w_in": ffn1_w_in[0].astype(BF16)}
    gather_groups, token = _gather_start("first", [0], first)
    rest, rest_token = _gather_start("rest", [1, 2, 3, 4],
                                     {nm: wt[nm][0].astype(BF16) + token[0, 0].astype(BF16) for nm in BIG if nm not in first})
    gather_groups.update(rest)
    vec = dict(norm_ffn1=norm_ffn1, norm_mix=norm_mix, norm_ffn2=norm_ffn2, pool_scale=pool_scale,
               norm_final=norm_final.reshape(1, D_MODEL))

    relayed = {}

    def relay_w(g, after):
        relayed[g] = _gather_relay(g, gather_groups[g], (after,))

    def get_w(g, after):
        if g in relayed:
            return _gather_land(g, relayed[g], (after,))
        return _gather_finish(g, gather_groups[g], (after, rest_token) if g == 0 else (after,))

    pairs, pending = [], []

    def on_grads(gr):
        g = len(pairs)
        names = GRAD_GROUPS[g]
        assert set(names) == set(gr), (names, list(gr))
        state, token = _pair_exchange_start(g, names, {nm: _grad_view(KIND[nm], gr[nm]) for nm in names})
        pairs.append(state)
        return token[0:1, 0:1]

    def flush(after):
        g = len(pending)
        names = GRAD_GROUPS[g]
        views, from_sib = _pair_exchange_wait(g, names, pairs[g], after)
        psums = {nm: _pair_sum(f"pair_sum_{nm}", KIND[nm], views[nm], from_sib[nm], c_arr) for nm in names}
        state, token = _shard_exchange_start(g, names, psums)
        pending.append(state)
        tokens.append(token)
        return token[0:1, 0:1]

    tokens = []
    loss_local, dx, small = _local_step(x[0], loss_target[0], vec, get_w, relay_w, on_grads, flush)

    grads, delta, new_m, new_v = {}, {}, {}, {}

    def adamw(nm):
        shape = wt[nm].shape
        outs = _adamw(f"adamw_{nm}", _as2d(wt[nm]), _as2d(grads[nm]), _as2d(mom[nm]), _as2d(var[nm]), with_grad=nm in BIG)
        delta[nm], new_m[nm], new_v[nm] = (o.reshape(shape) for o in outs[:3])
        if nm in BIG:
            grads[nm] = outs[3].reshape(shape)
        return outs[0]

    def reduce_start(g, after):
        names = GRAD_GROUPS[g]
        psums, from_chips = _shard_exchange_wait(g, names, pending[g], after)
        bufs = {nm: _shard_sum(f"shard_sum_{nm}", KIND[nm], psums[nm], from_chips[nm], sc_arr) for nm in names}
        return _half_exchange_start(g, names, bufs)

    def reduce_finish(g, state, after):
        names = GRAD_GROUPS[g]
        reduced = _half_exchange_wait(g, names, state, after)
        for nm in names:
            grads[nm] = reduced[nm].reshape(wt[nm].shape)
        return tuple(adamw(nm) for nm in names)

    swap0, token = reduce_start(0, (tokens[-1],))
    swap1, token = reduce_start(1, (token,))
    done = reduce_finish(0, swap0, (token,))
    done = reduce_finish(1, swap1, done)
    swap2, token = reduce_start(2, done)
    packed = jnp.concatenate([small["norm_ffn1"], small["norm_mix"], small["gate_bias"], small["pool_scale"],
                              small["norm_ffn2"], small["norm_final"], jnp.broadcast_to(loss_local, (1, D_MODEL))], axis=0)
    small_sum, _ = _all_reduce_small("reduce_small_grads", packed + token[0, 0])
    loss = small_sum[SMALL_ROWS - 1, 0]
    for nm in ("norm_ffn1", "norm_mix", "pool_scale", "norm_ffn2"):
        grads[nm] = small_sum[SMALL_ROW[nm]][None, :]
    grads["norm_final"] = small_sum[SMALL_ROW["norm_final"]]
    grads["gate_bias"] = lax.dynamic_slice(small_sum, (SMALL_ROW["gate_bias"], chip * bias_cols), (2, bias_cols))[None]
    reduce_finish(2, swap2, (small_sum,))
    for nm in WEIGHTS:
        if nm not in delta:
            adamw(nm)

    return (loss, dx[None], *[grads[nm] for nm in WEIGHTS], *[delta[nm] for nm in WEIGHTS],
            *[new_m[nm] for nm in WEIGHTS], *[new_v[nm] for nm in WEIGHTS])
```

```python
import numpy as np
import jax
import jax.numpy as jnp
from jax import lax
from jax.experimental import pallas as pl
from jax.experimental.pallas import tpu as pltpu

F32 = jnp.float32
BF16 = jnp.bfloat16
MESH = pl.DeviceIdType.MESH

D_MODEL = 1024
D_FF = 2816
HEADS = 4
HEAD_DIM = 256
GROUPS = 4
GROUP_DIM = 256
POOL_WINDOWS = (2, 4, 8, 16)
IN_WIDTH = 7 * D_MODEL
ROPE_BASE = 10000.0
NORM_EPS = 1e-6
FFN_RES_WEIGHT = 0.5
ADAM_LR, ADAM_B1, ADAM_B2, ADAM_EPS, ADAM_WD, ADAM_STEP = 0.001, 0.9, 0.999, 1e-08, 0.01, 10

N_CHIPS = 4
RET_BLOCK = 256
V7X_VMEM_LIMIT = 48 * 1024 * 1024


def _cparams(sem):
    return pltpu.CompilerParams(dimension_semantics=sem, vmem_limit_bytes=V7X_VMEM_LIMIT)


def _sigmoid(x):
    return jax.nn.sigmoid(x)


_DIMS = {"nn": (((1,), (0,)), ((), ())), "nt": (((1,), (1,)), ((), ())), "tn": (((0,), (0,)), ((), ()))}


def _matmul(name, a, b, mode, m, n, k, tm, tn, tk, out_dtypes, extras=(), consts=(), epilogue=None, resident=None,
            n_outer=False):
    tm, tn, tk = min(tm, m), min(tn, n), min(tk, k)
    gi, gj, gk = m // tm, n // tn, k // tk
    assert gi * tm == m and gj * tn == n and gk * tk == k, (name, m, n, k, tm, tn, tk)
    once = dict(pipeline_mode=pl.Buffered(1))

    def spec(shape, index, **kw):
        return pl.BlockSpec(shape, (lambda j, i, kk: index(i, j, kk)) if n_outer else index, **kw)

    kw = once if resident == "a" else {}
    a_spec = (spec((tk, tm), lambda i, j, kk: (kk, i), **kw) if mode == "tn" else spec((tm, tk), lambda i, j, kk: (i, kk), **kw))
    kw = once if resident == "b" else {}
    b_spec = (spec((tn, tk), lambda i, j, kk: (j, kk), **kw) if mode == "nt" else spec((tk, tn), lambda i, j, kk: (kk, j), **kw))
    n_ex, n_out = len(extras) + len(consts), len(out_dtypes)
    dims = _DIMS[mode]

    def body(a_ref, b_ref, *rest):
        ex_refs, out_refs = rest[:n_ex], rest[n_ex:n_ex + n_out]

        def finish(acc):
            outs = (acc,) if epilogue is None else epilogue(acc, *[e[...] for e in ex_refs])
            for o_ref, o in zip(out_refs, outs):
                o_ref[...] = o.astype(o_ref.dtype)

        prod = lax.dot_general(a_ref[...], b_ref[...], dims, preferred_element_type=F32)
        if gk == 1:
            finish(prod)
        else:
            acc_ref = rest[n_ex + n_out]
            kk = pl.program_id(2)

            @pl.when(kk == 0)
            def _():
                acc_ref[...] = prod

            @pl.when(kk > 0)
            def _():
                acc_ref[...] += prod

            @pl.when(kk == gk - 1)
            def _():
                finish(acc_ref[...])

    o_spec = spec((tm, tn), lambda i, j, kk: (i, j))
    outs = pl.pallas_call(
        body, name=name, grid=(gj, gi, gk) if n_outer else (gi, gj, gk),
        in_specs=[a_spec, b_spec] + [o_spec] * len(extras) + [spec((1, tn), lambda i, j, kk: (0, j))] * len(consts),
        out_specs=[o_spec] * n_out,
        out_shape=[jax.ShapeDtypeStruct((m, n), dt) for dt in out_dtypes],
        scratch_shapes=[pltpu.VMEM((tm, tn), F32)] if gk > 1 else [],
        compiler_params=_cparams(("parallel", "parallel", "arbitrary")),
    )(a, b, *extras, *consts)
    return outs[0] if n_out == 1 else outs


def _row_spec(tm, width, col_block=0):
    return pl.BlockSpec((tm, width), lambda i: (i, col_block))


def _full_spec(shape):
    return pl.BlockSpec(shape, lambda *_: (0,) * len(shape))


def _rmsnorm_fwd(name, h, g, tm=512):
    t = h.shape[0]

    def body(h_ref, g_ref, o_ref):
        x = h_ref[...]
        r = lax.rsqrt(jnp.mean(x * x, axis=-1, keepdims=True) + NORM_EPS)
        o_ref[...] = (x * r * g_ref[...]).astype(BF16)

    return pl.pallas_call(
        body, name=name, grid=(t // tm,),
        in_specs=[_row_spec(tm, D_MODEL), _full_spec((1, D_MODEL))],
        out_specs=_row_spec(tm, D_MODEL),
        out_shape=jax.ShapeDtypeStruct((t, D_MODEL), BF16),
        compiler_params=_cparams(("parallel",)),
    )(h, g)


def _proj_norm_bwd(name, a_list, a_specs, parts, w, h, g, dres, tm):
    t = h.shape[0]
    na = len(a_list)

    def body(*refs):
        a_refs = refs[:na]
        w_ref, h_ref, g_ref, dres_ref, dh_ref, dhb_ref, dg_ref = refs[na:]
        i = pl.program_id(0)
        dn_v = None
        for which, lead, k0, k1 in parts:
            a_ref = a_refs[which]
            term = _dot(a_ref[...] if lead is None else a_ref[lead], w_ref[:, k0:k1], "nt")
            dn_v = term if dn_v is None else dn_v + term
        x = h_ref[...]
        r = lax.rsqrt(jnp.mean(x * x, axis=-1, keepdims=True) + NORM_EPS)
        xh = x * r
        dxh = dn_v * g_ref[...]
        dh = dres_ref[...] + r * (dxh - xh * jnp.mean(dxh * xh, axis=-1, keepdims=True))
        dh_ref[...] = dh
        dhb_ref[...] = dh.astype(BF16)
        part = jnp.sum(dn_v * xh, axis=0, keepdims=True)

        @pl.when(i == 0)
        def _():
            dg_ref[...] = part

        @pl.when(i > 0)
        def _():
            dg_ref[...] += part

    row = _row_spec(tm, D_MODEL)
    return pl.pallas_call(
        body, name=name, grid=(t // tm,),
        in_specs=list(a_specs) + [pl.BlockSpec(w.shape, lambda i: (0, 0), pipeline_mode=pl.Buffered(1)), row,
                                  _full_spec((1, D_MODEL)), row],
        out_specs=[row, row, _full_spec((1, D_MODEL))],
        out_shape=[jax.ShapeDtypeStruct((t, D_MODEL), F32), jax.ShapeDtypeStruct((t, D_MODEL), BF16),
                   jax.ShapeDtypeStruct((1, D_MODEL), F32)],
        compiler_params=_cparams(("arbitrary",)),
    )(*a_list, w, h, g, dres)


def _out_loss_and_grad(name, mid, w_out, h, g, target, tm=512):
    t = h.shape[0]

    def body(m_ref, w_ref, h_ref, g_ref, t_ref, dh_ref, dhb_ref, dg_ref, loss_ref):
        i = pl.program_id(0)
        x = h_ref[...] + FFN_RES_WEIGHT * _dot(m_ref[...], w_ref[...])
        gv = g_ref[...]
        r = lax.rsqrt(jnp.mean(x * x, axis=-1, keepdims=True) + NORM_EPS)
        xh = x * r
        err = xh * gv - t_ref[...]
        row = jnp.mean(err * err, axis=-1, keepdims=True)
        part_loss = 0.5 * jnp.sum(row, axis=0, keepdims=True)
        dy = err * (1.0 / D_MODEL)
        dxh = dy * gv
        dh = r * (dxh - xh * jnp.mean(dxh * xh, axis=-1, keepdims=True))
        dh_ref[...] = dh
        dhb_ref[...] = dh.astype(BF16)
        part = jnp.sum(dy * xh, axis=0, keepdims=True)

        @pl.when(i == 0)
        def _():
            dg_ref[...] = part
            loss_ref[...] = jnp.zeros(loss_ref.shape, F32) + part_loss

        @pl.when(i > 0)
        def _():
            dg_ref[...] += part
            loss_ref[...] += part_loss

    return pl.pallas_call(
        body, name=name, grid=(t // tm,),
        in_specs=[_row_spec(tm, D_FF), pl.BlockSpec((D_FF, D_MODEL), lambda i: (0, 0), pipeline_mode=pl.Buffered(1)),
                  _row_spec(tm, D_MODEL), _full_spec((1, D_MODEL)), _row_spec(tm, D_MODEL)],
        out_specs=[_row_spec(tm, D_MODEL), _row_spec(tm, D_MODEL), _full_spec((1, D_MODEL)), _full_spec((8, 128))],
        out_shape=[jax.ShapeDtypeStruct((t, D_MODEL), F32), jax.ShapeDtypeStruct((t, D_MODEL), BF16),
                   jax.ShapeDtypeStruct((1, D_MODEL), F32), jax.ShapeDtypeStruct((8, 128), F32)],
        compiler_params=_cparams(("arbitrary",)),
    )(mid, w_out, h, g, target)


def _rope_tables(t):
    half = HEAD_DIM // 2
    inv_freq = np.float32(ROPE_BASE) ** (-np.arange(half, dtype=np.float32) / np.float32(half))
    ang = (np.arange(t, dtype=np.float32)[:, None] * inv_freq[None, :].astype(np.float32)).astype(np.float32)
    return jnp.asarray(np.cos(ang.astype(np.float64)).astype(np.float32)), jnp.asarray(np.sin(ang.astype(np.float64)).astype(np.float32))


ROPE_HALF = HEAD_DIM // 2
K_SCALE = HEAD_DIM ** -0.5


def _rotate(ref, rows, hh, c, s, scale=None):
    lo, mid, hi = hh * HEAD_DIM, hh * HEAD_DIM + ROPE_HALF, (hh + 1) * HEAD_DIM
    x1, x2 = ref[rows, lo:mid].astype(F32), ref[rows, mid:hi].astype(F32)
    y = jnp.concatenate([x1 * c - x2 * s, x1 * s + x2 * c], axis=1)
    return y if scale is None else y * scale


def _unrotate_into(ref, rows, hh, dy, c, s, scale=None):
    lo, mid, hi = hh * HEAD_DIM, hh * HEAD_DIM + ROPE_HALF, (hh + 1) * HEAD_DIM
    y1, y2 = dy[:, :ROPE_HALF], dy[:, ROPE_HALF:]
    d1, d2 = y1 * c + y2 * s, y2 * c - y1 * s
    if scale is not None:
        d1, d2 = d1 * scale, d2 * scale
    ref[rows, lo:mid] = d1.astype(ref.dtype)
    ref[rows, mid:hi] = d2.astype(ref.dtype)


def _retention_tables():
    b, chunk = RET_BLOCK, 64
    gamma = 1.0 - 2.0 ** (-5.0 - np.arange(HEADS, dtype=np.float64))
    log_g = np.log(gamma)[:, None, None]
    i = np.arange(b)[:, None]
    j = np.arange(b)[None, :]
    same = (i // chunk) == (j // chunk)
    earlier = (j // chunk) < (i // chunk)
    expo = np.where(same, np.abs(i - j), np.where(earlier, i - j, 0)).astype(np.float64)
    mask = np.where(same | earlier, 1.0, 0.0)
    dmat = np.exp(log_g * expo[None]) * mask[None]
    qd = np.exp(log_g[:, :, 0] * (np.arange(b)[None, :] + 1.0))
    kd = np.exp(log_g[:, :, 0] * (b - 1.0 - np.arange(b)[None, :]))
    cd = np.exp(log_g[:, :, 0] * b) * np.ones((1, HEAD_DIM))
    as32 = lambda v: jnp.asarray(v.astype(np.float32))
    return (as32(dmat), as32(np.swapaxes(dmat, 1, 2)), as32(qd[:, :, None]), as32(kd[:, :, None]), as32(cd[:, None, :]))


def _dot(a, b, mode="nn"):
    return lax.dot_general(a, b, _DIMS[mode], preferred_element_type=F32)


GRET_BLOCK = 3


RET_SUBS = 2
RET_STEP = RET_SUBS * RET_BLOCK


def _head_specs(steps, rev=False):
    pos = (lambda n: steps - 1 - n) if rev else (lambda n: n)
    tok = pl.BlockSpec((RET_STEP, D_MODEL), lambda n: (pos(n), 0))
    blk = [pl.BlockSpec((RET_STEP, D_MODEL), lambda n, b=b: (pos(n), b)) for b in range(GRET_BLOCK + 1)]
    rope = pl.BlockSpec((RET_STEP, ROPE_HALF), lambda n: (pos(n), 0))
    tab = _full_spec((HEADS, RET_BLOCK, RET_BLOCK))
    col = _full_spec((HEADS, RET_BLOCK, 1))
    rowv = _full_spec((HEADS, 1, HEAD_DIM))
    st = pl.BlockSpec((HEADS, RET_SUBS, HEAD_DIM, HEAD_DIM), lambda n: (0, pos(n), 0, 0))
    return tok, blk, rope, tab, col, rowv, st


def _retention_fwd(name, proj, cos, sin, tables):
    t = proj.shape[0]
    nb, steps = t // RET_BLOCK, t // RET_STEP
    dmat, _, qd, kd, cd = tables
    tok, blk, rope, tab, col, rowv, st = _head_specs(steps)

    def body(q_ref, k_ref, v_ref, g_ref, c_ref, s_ref, d_ref, qd_ref, kd_ref, cd_ref, o_ref, ret_ref, st_ref, state):
        n = pl.program_id(0)

        @pl.when(n == 0)
        def _():
            state[...] = jnp.zeros(state.shape, F32)

        for sub in range(RET_SUBS):
            rows = slice(sub * RET_BLOCK, (sub + 1) * RET_BLOCK)
            cs, sn = c_ref[rows, :], s_ref[rows, :]
            for hh in range(HEADS):
                sl = slice(hh * HEAD_DIM, (hh + 1) * HEAD_DIM)
                q, k = _rotate(q_ref, rows, hh, cs, sn), _rotate(k_ref, rows, hh, cs, sn, K_SCALE)
                v = v_ref[rows, sl].astype(BF16)
                s = _dot(q.astype(BF16), k.astype(BF16), "nt") * d_ref[hh]
                stb = state[hh].astype(BF16)
                st_ref[hh, sub] = stb
                o = _dot(s.astype(BF16), v) + _dot((q * qd_ref[hh]).astype(BF16), stb)
                o_ref[rows, sl] = o
                rn = o * lax.rsqrt(jnp.mean(o * o, axis=-1, keepdims=True) + NORM_EPS)
                g = g_ref[rows, sl].astype(F32)
                ret_ref[rows, sl] = (rn * (g * _sigmoid(g))).astype(BF16)
                state[hh] = state[hh] * cd_ref[hh] + _dot((k * kd_ref[hh]).astype(BF16), v, "tn")

    return pl.pallas_call(
        body, name=name, grid=(steps,),
        in_specs=blk + [rope, rope, tab, col, col, rowv],
        out_specs=[tok, tok, st],
        out_shape=[jax.ShapeDtypeStruct((t, D_MODEL), F32), jax.ShapeDtypeStruct((t, D_MODEL), BF16),
                   jax.ShapeDtypeStruct((HEADS, nb, HEAD_DIM, HEAD_DIM), BF16)],
        scratch_shapes=[pltpu.VMEM((HEADS, HEAD_DIM, HEAD_DIM), F32)],
        compiler_params=_cparams(("arbitrary",)),
    )(proj, proj, proj, proj, cos, sin, dmat, qd, kd, cd)


def _retention_bwd(name, dru, w_ru, o, proj, cos, sin, states, tables):
    t = proj.shape[0]
    steps = t // RET_STEP
    dmat, dmat_t, qd, kd, cd = tables
    tok, blk, rope, tab, col, rowv, st = _head_specs(steps, rev=True)

    def body(dru_ref, wru_ref, o_ref, q_ref, k_ref, v_ref, g_ref, c_ref, s_ref, st_ref, d_ref, dt_ref, qd_ref, kd_ref, cd_ref,
             dq_ref, dk_ref, dv_ref, dg_ref, gstate):
        n = pl.program_id(0)

        @pl.when(n == 0)
        def _():
            gstate[...] = jnp.zeros(gstate.shape, F32)

        for sub in reversed(range(RET_SUBS)):
            rows = slice(sub * RET_BLOCK, (sub + 1) * RET_BLOCK)
            cs, sn = c_ref[rows, :], s_ref[rows, :]
            dret = _dot(dru_ref[rows, :], wru_ref[...], "nt")
            for hh in range(HEADS):
                sl = slice(hh * HEAD_DIM, (hh + 1) * HEAD_DIM)
                o_v, g, dr = o_ref[rows, sl], g_ref[rows, sl].astype(F32), dret[:, sl]
                sg = _sigmoid(g)
                r = lax.rsqrt(jnp.mean(o_v * o_v, axis=-1, keepdims=True) + NORM_EPS)
                rn = o_v * r
                d_rn = dr * (g * sg)
                dg_ref[rows, sl] = (dr * rn * (sg * (1.0 + g * (1.0 - sg)))).astype(BF16)
                d_o = r * (d_rn - rn * jnp.mean(d_rn * rn, axis=-1, keepdims=True))
                dob = d_o.astype(BF16)

                q, k = _rotate(q_ref, rows, hh, cs, sn), _rotate(k_ref, rows, hh, cs, sn, K_SCALE)
                v = v_ref[rows, sl].astype(BF16)
                qb, kb = q.astype(BF16), k.astype(BF16)
                qdv, kdv = qd_ref[hh], kd_ref[hh]
                s_t = (_dot(kb, qb, "nt") * dt_ref[hh]).astype(BF16)
                p_t = (_dot(v, dob, "nt") * dt_ref[hh]).astype(BF16)
                p = (_dot(dob, v, "nt") * d_ref[hh]).astype(BF16)
                stb = st_ref[hh, sub]
                gb = gstate[hh].astype(BF16)
                _unrotate_into(dq_ref, rows, hh, _dot(p, kb) + _dot(dob, stb, "nt") * qdv, cs, sn)
                _unrotate_into(dk_ref, rows, hh, _dot(p_t, qb) + _dot(v, gb, "nt") * kdv, cs, sn, K_SCALE)
                dv_ref[rows, sl] = (_dot(s_t, dob) + _dot((k * kdv).astype(BF16), gb)).astype(BF16)
                gstate[hh] = gstate[hh] * cd_ref[hh] + _dot((q * qdv).astype(BF16), dob, "tn")

    return pl.pallas_call(
        body, name=name, grid=(steps,),
        in_specs=[tok, pl.BlockSpec((D_MODEL, D_MODEL), lambda n: (0, 0), pipeline_mode=pl.Buffered(1)), tok] + blk
                 + [rope, rope, st, tab, tab, col, col, rowv],
        out_specs=[tok, tok, tok, tok],
        out_shape=[jax.ShapeDtypeStruct((t, D_MODEL), BF16)] * 4,
        scratch_shapes=[pltpu.VMEM((HEADS, HEAD_DIM, HEAD_DIM), F32)],
        compiler_params=_cparams(("arbitrary",)),
    )(dru, w_ru, o, proj, proj, proj, proj, cos, sin, states, dmat, dmat_t, qd, kd, cd)


POOL_TILE = 256
POOL_SUBS = 4
POOL_STEP = POOL_SUBS * POOL_TILE


def _pool_tables():
    b = POOL_TILE
    tt = np.arange(b)[:, None]
    jj = np.arange(b)[None, :]
    cur, prev = [], []
    for w in POOL_WINDOWS:
        cur.append(((tt - jj >= 0) & (tt - jj <= w - 1)).astype(np.float32))
        prev.append((tt - (jj - b) <= w - 1).astype(np.float32))
    cur, prev = np.stack(cur), np.stack(prev)
    as16 = lambda v: jnp.asarray(v, dtype=BF16)
    return as16(cur), as16(prev), as16(np.swapaxes(cur, 1, 2)), as16(np.swapaxes(prev, 1, 2))


def _split2(x):
    hi = x.astype(BF16)
    return hi, (x - hi.astype(F32)).astype(BF16)


POOL_BLOCK = 4


def _pool_count(n, window):
    tpos = n * POOL_TILE + lax.broadcasted_iota(jnp.int32, (POOL_TILE, 1), 0)
    return jnp.minimum(tpos + 1, window).astype(F32)


def _pool_fwd(name, proj, pool_w, scale, tables):
    t = proj.shape[0]
    steps = t // POOL_STEP
    mc, mp, _, _ = tables
    tab = _full_spec((GROUPS, POOL_TILE, POOL_TILE))
    row = _row_spec(POOL_STEP, D_MODEL)

    def body(pc_ref, pp_ref, mc_ref, mp_ref, w_ref, sc_ref, pm_ref, mix_ref, po_ref):
        n = pl.program_id(0)
        for sub in range(POOL_SUBS):
            rows = slice(sub * POOL_TILE, (sub + 1) * POOL_TILE)
            tile = n * POOL_SUBS + sub
            for g, window in enumerate(POOL_WINDOWS):
                sl = slice(g * GROUP_DIM, (g + 1) * GROUP_DIM)
                p = pc_ref[rows, sl]
                if sub == 0:
                    before = jnp.where(n > 0, _dot(mp_ref[g], pp_ref[:, sl]), 0.0)
                else:
                    before = _dot(mp_ref[g], pc_ref[(sub - 1) * POOL_TILE:sub * POOL_TILE, sl])
                pm = ((_dot(mc_ref[g], p) + before) / _pool_count(tile, window) - p.astype(F32)).astype(BF16)
                pm_ref[rows, sl] = pm
                mixed = _dot(pm, w_ref[g])
                mix_ref[rows, sl] = mixed
                po_ref[rows, sl] = (mixed * sc_ref[:, sl]).astype(BF16)

    return pl.pallas_call(
        body, name=name, grid=(steps,),
        in_specs=[_row_spec(POOL_STEP, D_MODEL, POOL_BLOCK),
                  pl.BlockSpec((POOL_TILE, D_MODEL), lambda n: (jnp.maximum(n * POOL_SUBS - 1, 0), POOL_BLOCK)),
                  tab, tab, _full_spec((GROUPS, GROUP_DIM, GROUP_DIM)), _full_spec((1, D_MODEL))],
        out_specs=[row] * 3,
        out_shape=[jax.ShapeDtypeStruct((t, D_MODEL), BF16), jax.ShapeDtypeStruct((t, D_MODEL), F32),
                   jax.ShapeDtypeStruct((t, D_MODEL), BF16)],
        compiler_params=_cparams(("parallel",)),
    )(proj, proj, mc, mp, pool_w, scale)


def _pool_bwd(name, dpu, w_pu, pm, mixed, pool_w, scale, tables):
    t = dpu.shape[0]
    steps = t // POOL_STEP
    _, _, mct, mpt = tables
    cur = pl.BlockSpec((POOL_STEP, D_MODEL), lambda n: (steps - 1 - n, 0))
    tab = _full_spec((GROUPS, POOL_TILE, POOL_TILE))
    wspec = _full_spec((GROUPS, GROUP_DIM, GROUP_DIM))
    sspec = _full_spec((1, D_MODEL))

    def body(dpu_ref, wpu_ref, pm_ref, mix_ref, mct_ref, mpt_ref, w_ref, sc_ref, dp_ref, dw_ref, ds_ref, later):
        n = pl.program_id(0)

        @pl.when(n == 0)
        def _():
            dw_ref[...] = jnp.zeros(dw_ref.shape, F32)
            ds_ref[...] = jnp.zeros(ds_ref.shape, F32)
            later[...] = jnp.zeros(later.shape, F32)

        for sub in reversed(range(POOL_SUBS)):
            rows = slice(sub * POOL_TILE, (sub + 1) * POOL_TILE)
            tile = (steps - 1 - n) * POOL_SUBS + sub
            dpo = _dot(dpu_ref[rows, :], wpu_ref[...], "nt")
            for g, window in enumerate(POOL_WINDOWS):
                sl = slice(g * GROUP_DIM, (g + 1) * GROUP_DIM)
                dc, sc = dpo[:, sl], sc_ref[:, sl]
                dmix = (dc * sc).astype(BF16)
                dpm = _dot(dmix, w_ref[g], "nt")
                e = dpm / _pool_count(tile, window)
                e_hi, e_lo = _split2(e)
                f_hi, f_lo = _split2(later[g])
                mctv, mptv = mct_ref[g], mpt_ref[g]
                back = _dot(mctv, e_hi) + _dot(mctv, e_lo)
                after = _dot(mptv, f_hi) + _dot(mptv, f_lo)
                dp_ref[rows, sl] = (back + after - dpm).astype(BF16)
                later[g] = e
                dw_ref[g] += _dot(pm_ref[rows, sl], dmix, "tn")
                ds_ref[:, sl] += jnp.sum(dc * mix_ref[rows, sl], axis=0, keepdims=True)

    return pl.pallas_call(
        body, name=name, grid=(steps,),
        in_specs=[cur, pl.BlockSpec((D_MODEL, D_MODEL), lambda n: (0, 0), pipeline_mode=pl.Buffered(1)), cur, cur, tab, tab,
                  wspec, sspec],
        out_specs=[cur, wspec, sspec],
        out_shape=[jax.ShapeDtypeStruct((t, D_MODEL), BF16), jax.ShapeDtypeStruct((GROUPS, GROUP_DIM, GROUP_DIM), F32),
                   jax.ShapeDtypeStruct((1, D_MODEL), F32)],
        scratch_shapes=[pltpu.VMEM((GROUPS, POOL_TILE, GROUP_DIM), F32)],
        compiler_params=_cparams(("arbitrary",)),
    )(dpu, w_pu, pm, mixed, mct, mpt, pool_w, scale)


GATE0_BLOCK, GATE1_BLOCK = 5, 6


def _merge_fwd(name, ret, po, w_ru, w_pu, w_out, proj, bias, h, next_g, tm=512):
    t = ret.shape[0]

    def body(r_ref, p_ref, wr_ref, wp_ref, wo_ref, g0_ref, g1_ref, b_ref, h_ref, ng_ref, m_ref, ru_ref, pu_ref, ho_ref, n_ref):
        ru = _dot(r_ref[...], wr_ref[...])
        pu = _dot(p_ref[...], wp_ref[...])
        ru_ref[...] = ru
        pu_ref[...] = pu
        merged = (_sigmoid(g0_ref[...].astype(F32) + b_ref[0:1, :]) * ru
                  + _sigmoid(g1_ref[...].astype(F32) + b_ref[1:2, :]) * pu).astype(BF16)
        m_ref[...] = merged
        h_new = h_ref[...] + _dot(merged, wo_ref[...])
        ho_ref[...] = h_new
        n_ref[...] = _normed(h_new, ng_ref[...]).astype(BF16)

    row = _row_spec(tm, D_MODEL)
    wspec = pl.BlockSpec((D_MODEL, D_MODEL), lambda i: (0, 0), pipeline_mode=pl.Buffered(1))
    return pl.pallas_call(
        body, name=name, grid=(t // tm,),
        in_specs=[row, row, wspec, wspec, wspec, _row_spec(tm, D_MODEL, GATE0_BLOCK), _row_spec(tm, D_MODEL, GATE1_BLOCK),
                  _full_spec((2, D_MODEL)), row, _full_spec((1, D_MODEL))],
        out_specs=[row] * 5,
        out_shape=[jax.ShapeDtypeStruct((t, D_MODEL), BF16), jax.ShapeDtypeStruct((t, D_MODEL), F32),
                   jax.ShapeDtypeStruct((t, D_MODEL), F32), jax.ShapeDtypeStruct((t, D_MODEL), F32),
                   jax.ShapeDtypeStruct((t, D_MODEL), BF16)],
        compiler_params=_cparams(("parallel",)),
    )(ret, po, w_ru, w_pu, w_out, proj, proj, bias, h, next_g)


def _merge_bwd(name, dh_b, w_out, ru, pu, proj, bias, tm=512):
    t = dh_b.shape[0]

    def body(dh_ref, wo_ref, ru_ref, pu_ref, g0_ref, g1_ref, b_ref, dru_ref, dpu_ref, dg0_ref, dg1_ref, db_ref):
        i = pl.program_id(0)
        d = _dot(dh_ref[...], wo_ref[...], "nt")
        s0 = _sigmoid(g0_ref[...].astype(F32) + b_ref[0:1, :])
        s1 = _sigmoid(g1_ref[...].astype(F32) + b_ref[1:2, :])
        dru_ref[...] = (d * s0).astype(BF16)
        dpu_ref[...] = (d * s1).astype(BF16)
        dg0 = d * ru_ref[...] * (s0 * (1.0 - s0))
        dg1 = d * pu_ref[...] * (s1 * (1.0 - s1))
        dg0_ref[...] = dg0.astype(BF16)
        dg1_ref[...] = dg1.astype(BF16)
        part0 = jnp.sum(dg0, axis=0, keepdims=True)
        part1 = jnp.sum(dg1, axis=0, keepdims=True)

        @pl.when(i == 0)
        def _():
            db_ref[0:1, :] = part0
            db_ref[1:2, :] = part1

        @pl.when(i > 0)
        def _():
            db_ref[0:1, :] += part0
            db_ref[1:2, :] += part1

    row = _row_spec(tm, D_MODEL)
    return pl.pallas_call(
        body, name=name, grid=(t // tm,),
        in_specs=[row, pl.BlockSpec((D_MODEL, D_MODEL), lambda i: (0, 0), pipeline_mode=pl.Buffered(1)), row, row,
                  _row_spec(tm, D_MODEL, GATE0_BLOCK), _row_spec(tm, D_MODEL, GATE1_BLOCK), _full_spec((2, D_MODEL))],
        out_specs=[row, row, row, row, _full_spec((2, D_MODEL))],
        out_shape=[jax.ShapeDtypeStruct((t, D_MODEL), BF16)] * 4 + [jax.ShapeDtypeStruct((2, D_MODEL), F32)],
        compiler_params=_cparams(("arbitrary",)),
    )(dh_b, w_out, ru, pu, proj, proj, bias)


def _half_scale(acc):
    return (FFN_RES_WEIGHT * acc,)


def _normed(h, g):
    return h * lax.rsqrt(jnp.mean(h * h, axis=-1, keepdims=True) + NORM_EPS) * g


def _residual_half_norm(acc, res, g):
    h = res + FFN_RES_WEIGHT * acc
    return h, _normed(h, g)


FF_TILE = D_FF // 2
DW_TILE = 256
SAVED_FF_DTYPE = BF16
FF_CHUNKS = ((0, 512), (512, 1024), (1024, FF_TILE))


def _ffn_in(name, nrm, w_in, tm=512):
    t = nrm.shape[0]
    nj = D_FF // FF_TILE

    def body(n_ref, wg_ref, wu_ref, a_ref, mid_ref):
        nv = n_ref[...]
        for c0, c1 in FF_CHUNKS:
            gate = _dot(nv, wg_ref[:, c0:c1])
            up = _dot(nv, wu_ref[:, c0:c1])
            s = _sigmoid(gate)
            silu = gate * s
            a_ref[0, :, c0:c1] = (FFN_RES_WEIGHT * up * (s * (1.0 + gate * (1.0 - s)))).astype(a_ref.dtype)
            a_ref[1, :, c0:c1] = (FFN_RES_WEIGHT * silu).astype(a_ref.dtype)
            mid_ref[:, c0:c1] = (silu * up).astype(BF16)

    return pl.pallas_call(
        body, name=name, grid=(nj, t // tm),
        in_specs=[pl.BlockSpec((tm, D_MODEL), lambda j, i: (i, 0)),
                  pl.BlockSpec((D_MODEL, FF_TILE), lambda j, i: (0, j)),
                  pl.BlockSpec((D_MODEL, FF_TILE), lambda j, i: (0, j + nj))],
        out_specs=[pl.BlockSpec((2, tm, FF_TILE), lambda j, i: (0, i, j)), pl.BlockSpec((tm, FF_TILE), lambda j, i: (i, j))],
        out_shape=[jax.ShapeDtypeStruct((2, t, D_FF), SAVED_FF_DTYPE), jax.ShapeDtypeStruct((t, D_FF), BF16)],
        compiler_params=_cparams(("parallel", "parallel")),
    )(nrm, w_in, w_in)


def _ffn_dact(name, dout_b, w_out, a, tm=512):
    t = dout_b.shape[0]

    def body(d_ref, w_ref, a_ref, da_ref):
        dv = d_ref[...]
        for c0, c1 in FF_CHUNKS:
            dm = _dot(dv, w_ref[c0:c1, :], "nt")
            da_ref[0, :, c0:c1] = (dm * a_ref[0, :, c0:c1].astype(F32)).astype(BF16)
            da_ref[1, :, c0:c1] = (dm * a_ref[1, :, c0:c1].astype(F32)).astype(BF16)

    blk = pl.BlockSpec((2, tm, FF_TILE), lambda j, i: (0, i, j))
    return pl.pallas_call(
        body, name=name, grid=(D_FF // FF_TILE, t // tm),
        in_specs=[pl.BlockSpec((tm, D_MODEL), lambda j, i: (i, 0)), pl.BlockSpec((FF_TILE, D_MODEL), lambda j, i: (j, 0)), blk],
        out_specs=blk,
        out_shape=jax.ShapeDtypeStruct((2, t, D_FF), BF16),
        compiler_params=_cparams(("parallel", "parallel")),
    )(dout_b, w_out, a)


def _ffn_fwd(tag, h, nrm, get_w_in, get_w_out, finish):
    t = h.shape[0]
    w_in = get_w_in(nrm)
    a, mid = _ffn_in(f"{tag}_in", nrm, w_in, tm=min(1024, t))
    w_out = get_w_out(mid)
    return finish(mid, w_out), (nrm, a, mid, w_in, w_out)


def _ffn_bwd(tag, h, g, saved, dout, dout_b, on_grads, flush):
    t = h.shape[0]
    nrm, a, mid, w_in, w_out = saved
    d_w_out = _matmul(f"{tag}_dwout", mid, dout_b, "tn", D_FF, D_MODEL, t, DW_TILE, D_MODEL, t, [BF16], epilogue=_half_scale,
                      resident="b")
    da = _ffn_dact(f"{tag}_dact", dout_b, w_out, a, tm=min(1024, t))
    nj = D_FF // DW_TILE
    d_w_in = _dw_resident(f"{tag}_dwin", nrm, [da], [pl.BlockSpec((None, t, DW_TILE), lambda s: (s // nj, 0, s % nj))],
                          2 * nj, None, DW_TILE)
    tie = on_grads({f"{tag}_w_in": d_w_in, f"{tag}_w_out": d_w_out})
    tm = min(256, t)
    dh, dh_b, dg = _proj_norm_bwd(f"{tag}_dn", [da], [pl.BlockSpec((2, tm, D_FF), lambda i: (0, i, 0))],
                                  ((0, 0, 0, D_FF), (0, 1, D_FF, 2 * D_FF)), w_in, h, g if tie is None else g + tie, dout, tm)
    return dh, dh_b, dg, flush(dh)


def _dw_resident(name, u, pieces, piece_specs, n_tiles, which_piece, tn):
    t = u.shape[0]
    npc = len(pieces)

    def body(*refs):
        u_ref, p_refs, o_ref, ut_ref = refs[0], refs[1:1 + npc], refs[1 + npc], refs[2 + npc]
        s = pl.program_id(0)

        @pl.when(s == 0)
        def _():
            ut_ref[...] = u_ref[...].T

        if npc == 1:
            o_ref[...] = _dot(ut_ref[...], p_refs[0][...]).astype(BF16)
        for which in range(npc if npc > 1 else 0):
            @pl.when(which_piece(s) == which)
            def _(which=which):
                o_ref[...] = _dot(ut_ref[...], p_refs[which][...]).astype(BF16)

    return pl.pallas_call(
        body, name=name, grid=(n_tiles,),
        in_specs=[pl.BlockSpec((t, D_MODEL), lambda s: (0, 0), pipeline_mode=pl.Buffered(1))] + list(piece_specs),
        out_specs=pl.BlockSpec((D_MODEL, tn), lambda s: (0, s)),
        out_shape=jax.ShapeDtypeStruct((D_MODEL, n_tiles * tn), BF16),
        scratch_shapes=[pltpu.VMEM((D_MODEL, t), BF16)],
        compiler_params=_cparams(("arbitrary",)),
    )(u, *pieces)


def _mix_dwin(name, u, pieces, tn=256):
    t = u.shape[0]
    nj = D_MODEL // tn
    specs = [pl.BlockSpec((t, tn), lambda s, k=k: (0, jnp.clip(s - k * nj, 0, nj - 1))) for k in range(len(pieces))]
    return _dw_resident(name, u, pieces, specs, len(pieces) * nj, lambda s: s // nj, tn)


def _local_step(x, target, vec, get_w, relay_w, on_grads, flush):
    t = x.shape[0]
    cos, sin = _rope_tables(t)
    rtab = _retention_tables()
    ptab = _pool_tables()
    w = {}

    def getter(group, name):
        def get(after):
            if name not in w:
                w.update(get_w(group, after))
            return w[name]
        return get

    nrm1 = _rmsnorm_fwd("ffn1_norm", x, vec["norm_ffn1"])
    def out_and_norm(mid, w_out):
        return _matmul("ffn1_out", mid, w_out, "nn", t, D_MODEL, D_FF, 512, D_MODEL, D_FF, [F32, BF16],
                       extras=(x,), consts=(vec["norm_mix"],), epilogue=_residual_half_norm)

    (h1, u), s1 = _ffn_fwd("ffn1", x, nrm1, getter(0, "ffn1_w_in"), getter(1, "ffn1_w_out"), out_and_norm)
    w.update(get_w(2, u))
    relay_w(3, w["w_in"])
    proj = _matmul("mix_in", u, w["w_in"], "nn", t, IN_WIDTH, D_MODEL, 2048, 1024, D_MODEL, [BF16], n_outer=True)
    w.update(get_w(3, proj))
    o, ret, states = _retention_fwd("retention", proj, cos, sin, rtab)
    pm, mixed, po = _pool_fwd("pool", proj, w["pool_w"], vec["pool_scale"], ptab)
    relay_w(4, po)
    merged, ru, pu, h2, nrm2 = _merge_fwd("merge", ret, po, w["w_ret_up"], w["w_pool_up"], w["w_out"], proj, w["gate_bias"],
                                          h1, vec["norm_ffn2"], tm=min(512, t))
    def out_and_loss(mid, w_out):
        return _out_loss_and_grad("ffn2_out_loss", mid, w_out, h2, vec["norm_final"], target, tm=min(512, t))

    (dh3, dh3_b, dg_final, loss), s2 = _ffn_fwd("ffn2", h2, nrm2, getter(4, "ffn2_w_in"), getter(4, "ffn2_w_out"), out_and_loss)

    def tied(v, tie):
        return v if tie is None else v + tie

    dh2, dh2_b, dg_ffn2, tie = _ffn_bwd("ffn2", h2, vec["norm_ffn2"], s2, dh3, dh3_b, on_grads, flush)
    def square_dw(name, act, grad):
        return _matmul(name, act, grad, "tn", D_MODEL, D_MODEL, t, D_MODEL, D_MODEL, 1024, [BF16])

    d_w_out = square_dw("mix_dwout", merged, dh2_b)
    dru, dpu, dg0, dg1, d_bias = _merge_bwd("merge_bwd", dh2_b, w["w_out"], ru, pu, proj, tied(w["gate_bias"], tie))
    d_w_ru = square_dw("mix_dwru", ret, dru)
    d_w_pu = square_dw("mix_dwpu", po, dpu)
    dp, d_pool_w, d_scale = _pool_bwd("pool_bwd", dpu, w["w_pool_up"], pm, mixed, w["pool_w"], vec["pool_scale"], ptab)
    dq, dk, dv, dgr = _retention_bwd("retention_bwd", dru, w["w_ret_up"], o, proj, cos, sin, states, rtab)
    dproj = [dq, dk, dv, dgr, dp, dg0, dg1]
    d_w_in = _mix_dwin("mix_dwin", u, dproj)
    tie = on_grads(dict(w_in=d_w_in, pool_w=d_pool_w.astype(BF16), w_ret_up=d_w_ru, w_pool_up=d_w_pu, w_out=d_w_out))
    tm = min(256, t)
    dh1, dh1_b, dg_mix = _proj_norm_bwd("mix_du", dproj, [_row_spec(tm, D_MODEL)] * len(dproj),
                                        [(k, None, k * D_MODEL, (k + 1) * D_MODEL) for k in range(len(dproj))],
                                        w["w_in"], h1, tied(vec["norm_mix"], tie), dh2, tm)
    tie = flush(dh1)
    dx, _, dg_ffn1, _ = _ffn_bwd("ffn1", x, tied(vec["norm_ffn1"], tie), s1, dh1, dh1_b, on_grads, flush)

    small = dict(norm_ffn1=dg_ffn1, norm_mix=dg_mix, gate_bias=d_bias, pool_scale=d_scale, norm_ffn2=dg_ffn2,
                 norm_final=dg_final)
    return loss[0, 0], dx, small


BIG = ("ffn1_w_in", "ffn1_w_out", "w_in", "pool_w", "w_ret_up", "w_pool_up", "w_out", "ffn2_w_in", "ffn2_w_out")
KIND = dict(ffn1_w_in="col", ffn1_w_out="row", w_in="col", pool_w="pool", w_ret_up="row", w_pool_up="row", w_out="row",
            ffn2_w_in="col", ffn2_w_out="row", gate_bias="col")
ANY = pl.BlockSpec(memory_space=pl.ANY)


def _place():
    x, y, c = lax.axis_index("x"), lax.axis_index("y"), lax.axis_index("c")
    chips = [(1 - x, y), (x, 1 - y), (1 - x, 1 - y)]
    return x, y, c, chips


def _full_view_shape(kind, local_shape):
    if kind == "col":
        return (2, local_shape[0] // 2, N_CHIPS * local_shape[1])
    if kind == "row":
        return (N_CHIPS, 2, local_shape[0] // 2, local_shape[1])
    return (GROUPS, N_CHIPS, 2, local_shape[1] // 2, local_shape[2])


def _local_view(kind, arr):
    if kind == "pool":
        return arr.reshape(GROUPS, 2, arr.shape[1] // 2, arr.shape[2])
    return arr.reshape(2, arr.shape[0] // 2, arr.shape[1])


def _blk(kind, ref, s, c):
    if kind == "col":
        cs = ref.shape[2] // N_CHIPS
        return ref.at[c, :, pl.ds(pl.multiple_of(s * cs, 128), cs)]
    if kind == "row":
        return ref.at[s, c]
    return ref.at[:, s, c]


def _half(kind, ref, c):
    return ref.at[:, c] if kind == "pool" else ref.at[c]


def _shard(kind, ref, s):
    if kind == "col":
        cs = ref.shape[2] // N_CHIPS
        return ref.at[:, :, pl.ds(pl.multiple_of(s * cs, 128), cs)]
    if kind == "row":
        return ref.at[s]
    return ref.at[:, s]


HBM = pl.BlockSpec(memory_space=pltpu.HBM)
SEM = pl.BlockSpec(memory_space=pltpu.SEMAPHORE)
EFFECT = pltpu.SideEffectType.DATAFLOW_SIDE_EFFECTING
WEIGHT_GROUPS = (("gate_bias", "ffn1_w_in"), ("ffn1_w_out",), ("w_in",), ("pool_w", "w_ret_up", "w_pool_up", "w_out"),
                 ("ffn2_w_in", "ffn2_w_out"))
GRAD_GROUPS = (("ffn2_w_in", "ffn2_w_out"), ("w_in", "pool_w", "w_ret_up", "w_pool_up", "w_out"), ("ffn1_w_in", "ffn1_w_out"))


def _hbm(a):
    return pltpu.with_memory_space_constraint(a, pltpu.HBM)


def _natural(kind, o):
    if kind == "col":
        return o.reshape(o.shape[0] * o.shape[1], o.shape[2])
    if kind == "row":
        return o.reshape(-1, o.shape[3])
    return o.reshape(GROUPS, -1, o.shape[4])


def _ici_copy(kind, loc, full, j, chips, s, c, send_sem, recv_sem):
    px, py = chips[j]
    return (pltpu.make_async_remote_copy(src_ref=_half(kind, loc, c), dst_ref=_blk(kind, full, s, c), send_sem=send_sem,
                                         recv_sem=recv_sem, device_id=(px, py, c), device_id_type=MESH),
            pltpu.make_async_remote_copy(src_ref=_half(kind, loc, c), dst_ref=_blk(kind, full, 2 * px + py, c), send_sem=send_sem,
                                         recv_sem=recv_sem, device_id=(px, py, c), device_id_type=MESH))


def _gather_start(tag, group_ids, shards):
    grps = [WEIGHT_GROUPS[g] for g in group_ids]
    names = [nm for grp in grps for nm in grp]
    kinds = [KIND[nm] for nm in names]
    n, ng = len(names), len(grps)
    locs = [_hbm(_local_view(KIND[nm], shards[nm])) for nm in names]
    lands = [_hbm(lax.empty(_full_view_shape(KIND[nm], shards[nm].shape), shards[nm].dtype)) for nm in names]
    first = np.cumsum([0] + [len(grp) for grp in grps])

    def body(*refs):
        loc, full = refs[:n], refs[n:2 * n]
        send_sems, recv_sems = refs[2 * n:2 * n + ng], refs[2 * n + ng:2 * n + 2 * ng]
        token = refs[-1]
        x, y, c, chips = _place()
        s = 2 * x + y
        for g in range(ng):
            for a in range(first[g], first[g + 1]):
                for j in range(3):
                    k = 3 * (a - first[g]) + j
                    _ici_copy(kinds[a], loc[a], full[a], j, chips, s, c, send_sems[g].at[k], recv_sems[g].at[k])[0].start()
        token[...] = jnp.zeros(token.shape, F32)

    sem_shapes = [pltpu.SemaphoreType.DMA((3 * len(grp),)) for grp in grps]
    outs = pl.pallas_call(
        body, name=f"gather_start_{tag}",
        in_specs=[HBM] * (2 * n),
        out_specs=[SEM] * (2 * ng) + [HBM] * (2 * n) + [pl.BlockSpec(memory_space=pltpu.VMEM)],
        out_shape=sem_shapes + sem_shapes + [pltpu.HBM(a.shape, a.dtype) for a in locs + lands] + [jax.ShapeDtypeStruct((8, 128), F32)],
        input_output_aliases={i: 2 * ng + i for i in range(2 * n)},
        compiler_params=pltpu.CompilerParams(has_side_effects=EFFECT),
    )(*locs, *lands)
    send_sems, recv_sems = outs[:ng], outs[ng:2 * ng]
    locs_t, lands_t = outs[2 * ng:2 * ng + n], outs[2 * ng + n:2 * ng + 2 * n]
    groups = {}
    for k, g in enumerate(group_ids):
        sl = slice(first[k], first[k + 1])
        groups[g] = (send_sems[k], recv_sems[k], list(locs_t[sl]), list(lands_t[sl]))
    return groups, outs[-1]


def _forward_copies(kinds, loc, full, send_sems, recv_sems):
    x, y, c, chips = _place()
    s = 2 * x + y

    def remote(a, k, src, dst):
        return pltpu.make_async_remote_copy(src_ref=src, dst_ref=dst, send_sem=send_sems.at[4 * a + k],
                                            recv_sem=recv_sems.at[4 * a + k], device_id=(x, y, 1 - c), device_id_type=MESH)

    sends, arrivals = [], []
    for a, kind in enumerate(kinds):
        for j, (px, py) in enumerate(chips):
            theirs, from_sib = _blk(kind, full[a], 2 * px + py, c), _blk(kind, full[a], 2 * px + py, 1 - c)
            sends.append(remote(a, j, theirs, theirs))
            arrivals.append(remote(a, j, from_sib, from_sib))
        own = _shard(kind, full[a], s)
        sends.append(remote(a, 3, loc[a], own))
        arrivals.append(remote(a, 3, own, own))
    return sends, arrivals


def _gather_relay(g, group, after):
    names = WEIGHT_GROUPS[g]
    kinds = [KIND[nm] for nm in names]
    m = len(names)
    ici_send, ici_recv, locs, lands = group

    def body(*refs):
        loc, full = refs[:m], refs[m:2 * m]
        ici_s, ici_r = refs[2 * m], refs[2 * m + 1]
        d2d_s, d2d_r = refs[2 * m + 2 + len(after)], refs[2 * m + 3 + len(after)]
        x, y, c, chips = _place()
        for a in range(m):
            for j in range(3):
                sent, landed = _ici_copy(kinds[a], loc[a], full[a], j, chips, 2 * x + y, c, ici_s.at[3 * a + j], ici_r.at[3 * a + j])
                sent.wait_send()
                landed.wait_recv()
        for cp in _forward_copies(kinds, loc, full, d2d_s, d2d_r)[0]:
            cp.start()

    sem_shape = pltpu.SemaphoreType.DMA((4 * m,))
    outs = pl.pallas_call(
        body, name=f"gather_relay_{g}",
        in_specs=[HBM] * (2 * m) + [SEM, SEM] + [ANY] * len(after), out_specs=[SEM, SEM] + [HBM] * (2 * m),
        out_shape=[sem_shape, sem_shape] + [pltpu.HBM(a.shape, a.dtype) for a in locs + lands],
        input_output_aliases={i: 2 + i for i in range(2 * m)},
        compiler_params=pltpu.CompilerParams(has_side_effects=EFFECT),
    )(*locs, *lands, ici_send, ici_recv, *after)
    return outs[0], outs[1], list(outs[2:2 + m]), list(outs[2 + m:2 + 2 * m])


def _gather_land(g, state, after):
    names = WEIGHT_GROUPS[g]
    kinds = [KIND[nm] for nm in names]
    m = len(names)
    d2d_send, d2d_recv, locs, lands = state

    def body(*refs):
        sends, arrivals = _forward_copies(kinds, refs[:m], refs[m:2 * m], refs[2 * m], refs[2 * m + 1])
        for cp in sends:
            cp.wait_send()
        for cp in arrivals:
            cp.wait_recv()

    outs = pl.pallas_call(
        body, name=f"gather_land_{g}",
        in_specs=[HBM] * (2 * m) + [SEM, SEM] + [ANY] * len(after), out_specs=[HBM] * (2 * m),
        out_shape=[pltpu.HBM(a.shape, a.dtype) for a in locs + lands],
        input_output_aliases={i: i for i in range(2 * m)},
        compiler_params=pltpu.CompilerParams(has_side_effects=EFFECT),
    )(*locs, *lands, d2d_send, d2d_recv, *after)
    return {nm: _natural(k, o) for nm, k, o in zip(names, kinds, outs[m:])}


def _gather_finish(g, group, after):
    names = WEIGHT_GROUPS[g]
    kinds = [KIND[nm] for nm in names]
    m = len(names)
    send_sem, recv_sem, locs, lands = group

    def wait_body(*refs):
        loc, full = refs[:m], refs[m:2 * m]
        send_sems, recv_sems = refs[2 * m], refs[2 * m + 1]
        x, y, c, chips = _place()
        s = 2 * x + y
        for a in range(m):
            for j in range(3):
                k = 3 * a + j
                sent, landed = _ici_copy(kinds[a], loc[a], full[a], j, chips, s, c, send_sems.at[k], recv_sems.at[k])
                sent.wait_send()
                landed.wait_recv()

    outs = pl.pallas_call(
        wait_body, name=f"gather_wait_{g}",
        in_specs=[HBM] * (2 * m) + [SEM, SEM] + [ANY] * len(after), out_specs=[HBM] * (2 * m),
        out_shape=[pltpu.HBM(a.shape, a.dtype) for a in locs + lands],
        input_output_aliases={i: i for i in range(2 * m)},
        compiler_params=pltpu.CompilerParams(has_side_effects=EFFECT),
    )(*locs, *lands, send_sem, recv_sem, *after)
    locs, lands = outs[:m], outs[m:]

    def forward_body(*refs):
        sends, arrivals = _forward_copies(kinds, refs[:m], refs[2 * m:3 * m], *refs[3 * m:])
        for cp in sends:
            cp.start()
        for cp in arrivals:
            cp.wait_recv()
        for cp in sends:
            cp.wait_send()

    outs = pl.pallas_call(
        forward_body, name=f"gather_forward_{g}",
        in_specs=[ANY] * (2 * m), out_specs=[ANY] * m,
        out_shape=[jax.ShapeDtypeStruct(a.shape, a.dtype) for a in lands],
        input_output_aliases={m + i: i for i in range(m)},
        scratch_shapes=[pltpu.SemaphoreType.DMA((4 * m,)), pltpu.SemaphoreType.DMA((4 * m,))],
    )(*locs, *lands)
    return {nm: _natural(k, o) for nm, k, o in zip(names, kinds, outs)}


def _grad_view(kind, g):
    if kind == "col":
        return g.reshape(2, g.shape[0] // 2, g.shape[1])
    if kind == "row":
        return g.reshape(N_CHIPS, 2, g.shape[0] // (2 * N_CHIPS), g.shape[1])
    return g.reshape(GROUPS, N_CHIPS, 2, g.shape[1] // (2 * N_CHIPS), g.shape[2])


def _pair_copies(kinds, g, got, send_sems, recv_sems):
    x, y, c, _ = _place()

    def other_half(kind, ref):
        if kind == "col":
            return ref.at[1 - c]
        if kind == "row":
            return ref.at[:, 1 - c]
        return ref.at[:, :, 1 - c]

    return [pltpu.make_async_remote_copy(src_ref=other_half(kinds[a], g[a]), dst_ref=got[a], send_sem=send_sems.at[a],
                                         recv_sem=recv_sems.at[a], device_id=(x, y, 1 - c), device_id_type=MESH)
            for a in range(len(kinds))]


def _pair_exchange_start(tag, names, views):
    kinds = [KIND[nm] for nm in names]
    n = len(names)

    def got_shape(kind, v):
        if kind == "col":
            return v.shape[1:]
        if kind == "row":
            return (v.shape[0],) + v.shape[2:]
        return v.shape[:2] + v.shape[3:]

    srcs = [_hbm(views[nm]) for nm in names]
    lands = [_hbm(lax.empty(got_shape(k, views[nm]), BF16)) for nm, k in zip(names, kinds)]

    def body(*refs):
        g, got = refs[:n], refs[n:2 * n]
        for cp in _pair_copies(kinds, g, got, refs[2 * n], refs[2 * n + 1]):
            cp.start()
        refs[-1][...] = jnp.zeros(refs[-1].shape, F32)

    sem_shape = pltpu.SemaphoreType.DMA((n,))
    outs = pl.pallas_call(
        body, name=f"grad_pair_exchange_start_{tag}",
        in_specs=[HBM] * (2 * n),
        out_specs=[SEM, SEM] + [HBM] * (2 * n) + [pl.BlockSpec(memory_space=pltpu.VMEM)],
        out_shape=[sem_shape, sem_shape] + [pltpu.HBM(a.shape, a.dtype) for a in srcs + lands] + [jax.ShapeDtypeStruct((8, 128), F32)],
        input_output_aliases={i: 2 + i for i in range(2 * n)},
        compiler_params=pltpu.CompilerParams(has_side_effects=EFFECT),
    )(*srcs, *lands)
    return (outs[0], outs[1], list(outs[2:2 + n]), list(outs[2 + n:2 + 2 * n])), outs[-1]


def _pair_exchange_wait(tag, names, state, after):
    kinds = [KIND[nm] for nm in names]
    n = len(names)
    send_sem, recv_sem, srcs, lands = state

    def body(*refs):
        g, got = refs[:n], refs[n:2 * n]
        for cp in _pair_copies(kinds, g, got, refs[2 * n], refs[2 * n + 1]):
            cp.wait_send()
            cp.wait_recv()

    outs = pl.pallas_call(
        body, name=f"grad_pair_exchange_wait_{tag}",
        in_specs=[HBM] * (2 * n) + [SEM, SEM, ANY], out_specs=[HBM] * (2 * n),
        out_shape=[pltpu.HBM(a.shape, a.dtype) for a in srcs + lands],
        input_output_aliases={i: i for i in range(2 * n)},
        compiler_params=pltpu.CompilerParams(has_side_effects=EFFECT),
    )(*srcs, *lands, send_sem, recv_sem, after)
    return dict(zip(names, outs[:n])), dict(zip(names, outs[n:]))


def _pair_sum(name, kind, view, got, c_arr):
    if kind == "col":
        _, rows, cols = view.shape
        tr = 128
        grid = (rows // tr,)
        v_spec = pl.BlockSpec((None, tr, cols), lambda i, c: (c[0], i, 0))
        g_spec = pl.BlockSpec((tr, cols), lambda i, c: (i, 0))
    elif kind == "row":
        _, _, rows, cols = view.shape
        grid = (N_CHIPS,)
        v_spec = pl.BlockSpec((None, None, rows, cols), lambda i, c: (i, c[0], 0, 0))
        g_spec = pl.BlockSpec((None, rows, cols), lambda i, c: (i, 0, 0))
    else:
        _, _, _, rows, cols = view.shape
        grid = (GROUPS,)
        v_spec = pl.BlockSpec((None, N_CHIPS, None, rows, cols), lambda i, c: (i, 0, c[0], 0, 0))
        g_spec = pl.BlockSpec((None, N_CHIPS, rows, cols), lambda i, c: (i, 0, 0, 0))

    def body(c_ref, v_ref, g_ref, o_ref):
        o_ref[...] = (v_ref[...].astype(F32) + g_ref[...].astype(F32)).astype(BF16)

    return pl.pallas_call(
        body, name=name,
        grid_spec=pltpu.PrefetchScalarGridSpec(num_scalar_prefetch=1, grid=grid, in_specs=[v_spec, g_spec], out_specs=g_spec),
        out_shape=jax.ShapeDtypeStruct(got.shape, BF16),
        compiler_params=_cparams(("parallel",)),
    )(c_arr, view, got)


def _piece(kind, ref, s):
    if kind == "col":
        cs = ref.shape[1] // N_CHIPS
        return ref.at[:, pl.ds(pl.multiple_of(s * cs, 128), cs)]
    if kind == "row":
        return ref.at[s]
    return ref.at[:, s]


def _piece_shape(kind, shape):
    if kind == "col":
        return (shape[0], shape[1] // N_CHIPS)
    if kind == "row":
        return shape[1:]
    return (shape[0],) + shape[2:]


def _shard_copies(kinds, p, got, send_sems, recv_sems):
    x, y, c, chips = _place()
    return [pltpu.make_async_remote_copy(src_ref=_piece(kinds[a], p[a], 2 * px + py), dst_ref=got[a].at[j],
                                         send_sem=send_sems.at[3 * a + j], recv_sem=recv_sems.at[3 * a + j],
                                         device_id=(px, py, c), device_id_type=MESH)
            for a in range(len(kinds)) for j, (px, py) in enumerate(chips)]


def _shard_exchange_start(g, names, psums):
    kinds = [KIND[nm] for nm in names]
    n = len(names)
    srcs = [_hbm(psums[nm]) for nm in names]
    lands = [_hbm(lax.empty((3,) + _piece_shape(k, psums[nm].shape), BF16)) for nm, k in zip(names, kinds)]

    def body(*refs):
        p, got = refs[:n], refs[n:2 * n]
        send_sems, recv_sems = refs[2 * n], refs[2 * n + 1]
        token = refs[-1]
        for cp in _shard_copies(kinds, p, got, send_sems, recv_sems):
            cp.start()
        token[...] = jnp.zeros(token.shape, F32)

    sem_shape = pltpu.SemaphoreType.DMA((3 * n,))
    outs = pl.pallas_call(
        body, name=f"grad_shard_exchange_start_{g}",
        in_specs=[HBM] * (2 * n),
        out_specs=[SEM, SEM] + [HBM] * (2 * n) + [pl.BlockSpec(memory_space=pltpu.VMEM)],
        out_shape=[sem_shape, sem_shape] + [pltpu.HBM(a.shape, a.dtype) for a in srcs + lands] + [jax.ShapeDtypeStruct((8, 128), F32)],
        input_output_aliases={i: 2 + i for i in range(2 * n)},
        compiler_params=pltpu.CompilerParams(has_side_effects=EFFECT),
    )(*srcs, *lands)
    return (outs[0], outs[1], list(outs[2:2 + n]), list(outs[2 + n:2 + 2 * n])), outs[-1]


def _shard_exchange_wait(g, names, state, after):
    kinds = [KIND[nm] for nm in names]
    n = len(names)
    send_sem, recv_sem, srcs, lands = state

    def body(*refs):
        p, got = refs[:n], refs[n:2 * n]
        for cp in _shard_copies(kinds, p, got, refs[2 * n], refs[2 * n + 1]):
            cp.wait_send()
            cp.wait_recv()

    outs = pl.pallas_call(
        body, name=f"grad_shard_exchange_wait_{g}",
        in_specs=[HBM] * (2 * n) + [SEM, SEM] + [ANY] * len(after), out_specs=[HBM] * (2 * n),
        out_shape=[pltpu.HBM(a.shape, a.dtype) for a in srcs + lands],
        input_output_aliases={i: i for i in range(2 * n)},
        compiler_params=pltpu.CompilerParams(has_side_effects=EFFECT),
    )(*srcs, *lands, send_sem, recv_sem, *after)
    return dict(zip(names, outs[:n])), dict(zip(names, outs[n:]))


def _shard_sum(name, kind, psum, got, sc_arr):
    if kind == "col":
        rows, cols = psum.shape
        cs = cols // N_CHIPS
        tr = 128
        grid = (rows // tr,)
        p_spec = pl.BlockSpec((tr, cs), lambda i, sc: (i, sc[0]))
        g_spec = pl.BlockSpec((3, tr, cs), lambda i, sc: (0, i, 0))
        o_spec = pl.BlockSpec((None, tr, cs), lambda i, sc: (sc[1], i, 0))
        out_shape = (2, rows, cs)
    elif kind == "row":
        _, rows, cols = psum.shape
        grid = (1,)
        p_spec = pl.BlockSpec((None, rows, cols), lambda i, sc: (sc[0], 0, 0))
        g_spec = pl.BlockSpec((3, rows, cols), lambda i, sc: (0, 0, 0))
        o_spec = pl.BlockSpec((None, rows, cols), lambda i, sc: (sc[1], 0, 0))
        out_shape = (2, rows, cols)
    else:
        _, _, rows, cols = psum.shape
        grid = (1,)
        p_spec = pl.BlockSpec((GROUPS, None, rows, cols), lambda i, sc: (0, sc[0], 0, 0))
        g_spec = pl.BlockSpec((3, GROUPS, rows, cols), lambda i, sc: (0, 0, 0, 0))
        o_spec = pl.BlockSpec((GROUPS, None, rows, cols), lambda i, sc: (0, sc[1], 0, 0))
        out_shape = (GROUPS, 2, rows, cols)

    def body(sc_ref, p_ref, g_ref, o_ref):
        o_ref[...] = ((p_ref[...].astype(F32) + g_ref[0].astype(F32)) + g_ref[1].astype(F32)) + g_ref[2].astype(F32)

    return pl.pallas_call(
        body, name=name,
        grid_spec=pltpu.PrefetchScalarGridSpec(num_scalar_prefetch=1, grid=grid, in_specs=[p_spec, g_spec], out_specs=o_spec),
        out_shape=jax.ShapeDtypeStruct(out_shape, F32),
        compiler_params=_cparams(("parallel",)),
    )(sc_arr, psum, got)


def _half_copies(kinds, bufs, send_sems, recv_sems):
    x, y, c, _ = _place()

    def remote(a, half):
        part = _half(kinds[a], bufs[a], half)
        return pltpu.make_async_remote_copy(src_ref=part, dst_ref=part, send_sem=send_sems.at[a], recv_sem=recv_sems.at[a],
                                            device_id=(x, y, 1 - c), device_id_type=MESH)

    return [remote(a, c) for a in range(len(kinds))], [remote(a, 1 - c) for a in range(len(kinds))]


def _half_exchange_start(tag, names, bufs):
    kinds = [KIND[nm] for nm in names]
    n = len(names)
    arrs = [_hbm(bufs[nm]) for nm in names]

    def body(*refs):
        for cp in _half_copies(kinds, refs[:n], refs[n], refs[n + 1])[0]:
            cp.start()
        refs[-1][...] = jnp.zeros(refs[-1].shape, F32)

    sem_shape = pltpu.SemaphoreType.DMA((n,))
    outs = pl.pallas_call(
        body, name=f"grad_half_exchange_start_{tag}",
        in_specs=[HBM] * n,
        out_specs=[SEM, SEM] + [HBM] * n + [pl.BlockSpec(memory_space=pltpu.VMEM)],
        out_shape=[sem_shape, sem_shape] + [pltpu.HBM(a.shape, a.dtype) for a in arrs] + [jax.ShapeDtypeStruct((8, 128), F32)],
        input_output_aliases={i: 2 + i for i in range(n)},
        compiler_params=pltpu.CompilerParams(has_side_effects=EFFECT),
    )(*arrs)
    return (outs[0], outs[1], list(outs[2:2 + n])), outs[-1]


def _half_exchange_wait(tag, names, state, after):
    kinds = [KIND[nm] for nm in names]
    n = len(names)
    send_sem, recv_sem, arrs = state

    def body(*refs):
        sends, arrivals = _half_copies(kinds, refs[:n], refs[n], refs[n + 1])
        for cp in sends:
            cp.wait_send()
        for cp in arrivals:
            cp.wait_recv()

    outs = pl.pallas_call(
        body, name=f"grad_half_exchange_wait_{tag}",
        in_specs=[HBM] * n + [SEM, SEM] + [ANY] * len(after), out_specs=[HBM] * n,
        out_shape=[pltpu.HBM(a.shape, a.dtype) for a in arrs],
        input_output_aliases={i: i for i in range(n)},
        compiler_params=pltpu.CompilerParams(has_side_effects=EFFECT),
    )(*arrs, send_sem, recv_sem, *after)
    return dict(zip(names, outs))


N_DEV = 8
SMALL_ROWS = 8


def _all_reduce_small(name, v):
    def body(v_ref, o_ref, token, buf, send_sems, recv_sems):
        token[...] = jnp.zeros(token.shape, F32)
        x, y, c, _ = _place()
        me = 4 * x + 2 * y + c
        buf[me] = v_ref[...]
        cps = []
        for r in range(1, N_DEV):
            to = (x ^ (r >> 2), y ^ ((r >> 1) & 1), c ^ (r & 1))
            cp = pltpu.make_async_remote_copy(src_ref=v_ref, dst_ref=buf.at[me], send_sem=send_sems.at[r - 1],
                                              recv_sem=recv_sems.at[r - 1], device_id=to, device_id_type=MESH)
            cp.start()
            cps.append(cp)
        for r in range(1, N_DEV):
            pltpu.make_async_remote_copy(src_ref=v_ref, dst_ref=buf.at[me ^ r], send_sem=send_sems.at[r - 1],
                                         recv_sem=recv_sems.at[r - 1], device_id=(x, y, c), device_id_type=MESH).wait_recv()
        for cp in cps:
            cp.wait_send()
        acc = buf[0]
        for d in range(1, N_DEV):
            acc = acc + buf[d]
        o_ref[...] = acc

    vm = pl.BlockSpec(memory_space=pltpu.VMEM)
    return pl.pallas_call(
        body, name=name, in_specs=[vm], out_specs=[vm, vm],
        out_shape=[jax.ShapeDtypeStruct((SMALL_ROWS, D_MODEL), F32), jax.ShapeDtypeStruct((8, 128), F32)],
        scratch_shapes=[pltpu.VMEM((N_DEV, SMALL_ROWS, D_MODEL), F32), pltpu.SemaphoreType.DMA((N_DEV - 1,)),
                        pltpu.SemaphoreType.DMA((N_DEV - 1,))],
    )(v)


def _adamw(name, w, g, m, v, with_grad=False):
    rows, cols = w.shape
    tr = next((c for c in (256, 176, 128, 64, 32, 8) if rows % c == 0), rows)
    spec = pl.BlockSpec((tr, cols), lambda i: (i, 0))

    def body(w_ref, g_ref, m_ref, v_ref, d_ref, mo_ref, vo_ref, *go_ref):
        gv = g_ref[...]
        if with_grad:
            go_ref[0][...] = gv
        m_new = ADAM_B1 * m_ref[...] + (1.0 - ADAM_B1) * gv
        v_new = ADAM_B2 * v_ref[...] + (1.0 - ADAM_B2) * jnp.square(gv)
        m_hat = m_new / (1.0 - ADAM_B1 ** ADAM_STEP)
        v_hat = v_new / (1.0 - ADAM_B2 ** ADAM_STEP)
        d_ref[...] = -ADAM_LR * (m_hat / (jnp.sqrt(v_hat) + ADAM_EPS) + ADAM_WD * w_ref[...])
        mo_ref[...] = m_new
        vo_ref[...] = v_new

    return pl.pallas_call(
        body, name=name, grid=(rows // tr,),
        in_specs=[spec] * 4, out_specs=[spec] * (4 if with_grad else 3),
        out_shape=[jax.ShapeDtypeStruct((rows, cols), F32)] * (4 if with_grad else 3),
        compiler_params=_cparams(("parallel",)),
    )(w, g, m, v)


WEIGHTS = ("norm_ffn1", "ffn1_w_in", "ffn1_w_out", "norm_mix", "w_in", "gate_bias", "pool_w", "pool_scale", "w_ret_up",
           "w_pool_up", "w_out", "norm_ffn2", "ffn2_w_in", "ffn2_w_out", "norm_final")
SMALL_ROW = dict(norm_ffn1=0, norm_mix=1, gate_bias=2, pool_scale=4, norm_ffn2=5, norm_final=6)


def _as2d(a):
    return a.reshape(-1, a.shape[-1])


def kernel(x, norm_ffn1, ffn1_w_in, ffn1_w_out, norm_mix, w_in, gate_bias, pool_w, pool_scale, w_ret_up, w_pool_up, w_out, norm_ffn2, ffn2_w_in, ffn2_w_out, norm_final, loss_target, m_norm_ffn1, m_ffn1_w_in, m_ffn1_w_out, m_norm_mix, m_w_in, m_gate_bias, m_pool_w, m_pool_scale, m_w_ret_up, m_w_pool_up, m_w_out, m_norm_ffn2, m_ffn2_w_in, m_ffn2_w_out, m_norm_final, v_norm_ffn1, v_ffn1_w_in, v_ffn1_w_out, v_norm_mix, v_w_in, v_gate_bias, v_pool_w, v_pool_scale, v_w_ret_up, v_w_pool_up, v_w_out, v_norm_ffn2, v_ffn2_w_in, v_ffn2_w_out, v_norm_final):
    wt = dict(norm_ffn1=norm_ffn1, ffn1_w_in=ffn1_w_in, ffn1_w_out=ffn1_w_out, norm_mix=norm_mix, w_in=w_in, gate_bias=gate_bias,
              pool_w=pool_w, pool_scale=pool_scale, w_ret_up=w_ret_up, w_pool_up=w_pool_up, w_out=w_out, norm_ffn2=norm_ffn2,
              ffn2_w_in=ffn2_w_in, ffn2_w_out=ffn2_w_out, norm_final=norm_final)
    mom = dict(norm_ffn1=m_norm_ffn1, ffn1_w_in=m_ffn1_w_in, ffn1_w_out=m_ffn1_w_out, norm_mix=m_norm_mix, w_in=m_w_in,
               gate_bias=m_gate_bias, pool_w=m_pool_w, pool_scale=m_pool_scale, w_ret_up=m_w_ret_up, w_pool_up=m_w_pool_up,
               w_out=m_w_out, norm_ffn2=m_norm_ffn2, ffn2_w_in=m_ffn2_w_in, ffn2_w_out=m_ffn2_w_out, norm_final=m_norm_final)
    var = dict(norm_ffn1=v_norm_ffn1, ffn1_w_in=v_ffn1_w_in, ffn1_w_out=v_ffn1_w_out, norm_mix=v_norm_mix, w_in=v_w_in,
               gate_bias=v_gate_bias, pool_w=v_pool_w, pool_scale=v_pool_scale, w_ret_up=v_w_ret_up, w_pool_up=v_w_pool_up,
               w_out=v_w_out, norm_ffn2=v_norm_ffn2, ffn2_w_in=v_ffn2_w_in, ffn2_w_out=v_ffn2_w_out, norm_final=v_norm_final)

    ax, ay, ac = lax.axis_index("x"), lax.axis_index("y"), lax.axis_index("c")
    chip = 2 * ax + ay
    c_arr = jnp.reshape(ac, (1,)).astype(jnp.int32)
    sc_arr = jnp.stack([chip, ac]).astype(jnp.int32)
    bias_cols = gate_bias.shape[-1]

    first = {"gate_bias": gate_bias[0], "ffn1_w_in": ffn1_w_in[0].astype(BF16)}
    gather_groups, token = _gather_start("first", [0], first)
    rest, rest_token = _gather_start("rest", [1, 2, 3, 4],
                                     {nm: wt[nm][0].astype(BF16) + token[0, 0].astype(BF16) for nm in BIG if nm not in first})
    gather_groups.update(rest)
    vec = dict(norm_ffn1=norm_ffn1, norm_mix=norm_mix, norm_ffn2=norm_ffn2, pool_scale=pool_scale,
               norm_final=norm_final.reshape(1, D_MODEL))

    relayed = {}

    def relay_w(g, after):
        relayed[g] = _gather_relay(g, gather_groups[g], (after,))

    def get_w(g, after):
        if g in relayed:
            return _gather_land(g, relayed[g], (after,))
        return _gather_finish(g, gather_groups[g], (after, rest_token) if g == 0 else (after,))

    pairs, pending = [], []

    def on_grads(gr):
        g = len(pairs)
        names = GRAD_GROUPS[g]
        assert set(names) == set(gr), (names, list(gr))
        state, token = _pair_exchange_start(g, names, {nm: _grad_view(KIND[nm], gr[nm]) for nm in names})
        pairs.append(state)
        return token[0:1, 0:1]

    def flush(after):
        g = len(pending)
        names = GRAD_GROUPS[g]
        views, from_sib = _pair_exchange_wait(g, names, pairs[g], after)
        psums = {nm: _pair_sum(f"pair_sum_{nm}", KIND[nm], views[nm], from_sib[nm], c_arr) for nm in names}
        state, token = _shard_exchange_start(g, names, psums)
        pending.append(state)
        tokens.append(token)
        return token[0:1, 0:1]

    tokens = []
    loss_local, dx, small = _local_step(x[0], loss_target[0], vec, get_w, relay_w, on_grads, flush)

    grads, delta, new_m, new_v = {}, {}, {}, {}

    def adamw(nm):
        shape = wt[nm].shape
        outs = _adamw(f"adamw_{nm}", _as2d(wt[nm]), _as2d(grads[nm]), _as2d(mom[nm]), _as2d(var[nm]), with_grad=nm in BIG)
        delta[nm], new_m[nm], new_v[nm] = (o.reshape(shape) for o in outs[:3])
        if nm in BIG:
            grads[nm] = outs[3].reshape(shape)
        return outs[0]

    def reduce_start(g, after):
        names = GRAD_GROUPS[g]
        psums, from_chips = _shard_exchange_wait(g, names, pending[g], after)
        bufs = {nm: _shard_sum(f"shard_sum_{nm}", KIND[nm], psums[nm], from_chips[nm], sc_arr) for nm in names}
        return _half_exchange_start(g, names, bufs)

    def reduce_finish(g, state, after):
        names = GRAD_GROUPS[g]
        reduced = _half_exchange_wait(g, names, state, after)
        for nm in names:
            grads[nm] = reduced[nm].reshape(wt[nm].shape)
        return tuple(adamw(nm) for nm in names)

    swap0, token = reduce_start(0, (tokens[-1],))
    swap1, token = reduce_start(1, (token,))
    done = reduce_finish(0, swap0, (token,))
    done = reduce_finish(1, swap1, done)
    swap2, token = reduce_start(2, done)
    packed = jnp.concatenate([small["norm_ffn1"], small["norm_mix"], small["gate_bias"], small["pool_scale"],
                              small["norm_ffn2"], small["norm_final"], jnp.broadcast_to(loss_local, (1, D_MODEL))], axis=0)
    small_sum, _ = _all_reduce_small("reduce_small_grads", packed + token[0, 0])
    loss = small_sum[SMALL_ROWS - 1, 0]
    for nm in ("norm_ffn1", "norm_mix", "pool_scale", "norm_ffn2"):
        grads[nm] = small_sum[SMALL_ROW[nm]][None, :]
    grads["norm_final"] = small_sum[SMALL_ROW["norm_final"]]
    grads["gate_bias"] = lax.dynamic_slice(small_sum, (SMALL_ROW["gate_bias"], chip * bias_cols), (2, bias_cols))[None]
    reduce_finish(2, swap2, (small_sum,))
    for nm in WEIGHTS:
        if nm not in delta:
            adamw(nm)

    return (loss, dx[None], *[grads[nm] for nm in WEIGHTS], *[delta[nm] for nm in WEIGHTS],
            *[new_m[nm] for nm in WEIGHTS], *[new_v[nm] for nm in WEIGHTS])
```

```python
import numpy as np
import jax
import jax.numpy as jnp
from jax import lax
from jax.experimental import pallas as pl
from jax.experimental.pallas import tpu as pltpu

F32 = jnp.float32
BF16 = jnp.bfloat16
MESH = pl.DeviceIdType.MESH

D_MODEL = 1024
D_FF = 2816
HEADS = 4
HEAD_DIM = 256
GROUPS = 4
GROUP_DIM = 256
POOL_WINDOWS = (2, 4, 8, 16)
IN_WIDTH = 7 * D_MODEL
ROPE_BASE = 10000.0
NORM_EPS = 1e-6
FFN_RES_WEIGHT = 0.5
ADAM_LR, ADAM_B1, ADAM_B2, ADAM_EPS, ADAM_WD, ADAM_STEP = 0.001, 0.9, 0.999, 1e-08, 0.01, 10

N_CHIPS = 4
RET_BLOCK = 256
V7X_VMEM_LIMIT = 48 * 1024 * 1024


def _cparams(sem):
    return pltpu.CompilerParams(dimension_semantics=sem, vmem_limit_bytes=V7X_VMEM_LIMIT)


def _sigmoid(x):
    return jax.nn.sigmoid(x)


_DIMS = {"nn": (((1,), (0,)), ((), ())), "nt": (((1,), (1,)), ((), ())), "tn": (((0,), (0,)), ((), ()))}


def _matmul(name, a, b, mode, m, n, k, tm, tn, tk, out_dtypes, extras=(), consts=(), epilogue=None, resident=None,
            n_outer=False):
    tm, tn, tk = min(tm, m), min(tn, n), min(tk, k)
    gi, gj, gk = m // tm, n // tn, k // tk
    assert gi * tm == m and gj * tn == n and gk * tk == k, (name, m, n, k, tm, tn, tk)
    once = dict(pipeline_mode=pl.Buffered(1))

    def spec(shape, index, **kw):
        return pl.BlockSpec(shape, (lambda j, i, kk: index(i, j, kk)) if n_outer else index, **kw)

    kw = once if resident == "a" else {}
    a_spec = (spec((tk, tm), lambda i, j, kk: (kk, i), **kw) if mode == "tn" else spec((tm, tk), lambda i, j, kk: (i, kk), **kw))
    kw = once if resident == "b" else {}
    b_spec = (spec((tn, tk), lambda i, j, kk: (j, kk), **kw) if mode == "nt" else spec((tk, tn), lambda i, j, kk: (kk, j), **kw))
    n_ex, n_out = len(extras) + len(consts), len(out_dtypes)
    dims = _DIMS[mode]

    def body(a_ref, b_ref, *rest):
        ex_refs, out_refs = rest[:n_ex], rest[n_ex:n_ex + n_out]

        def finish(acc):
            outs = (acc,) if epilogue is None else epilogue(acc, *[e[...] for e in ex_refs])
            for o_ref, o in zip(out_refs, outs):
                o_ref[...] = o.astype(o_ref.dtype)

        prod = lax.dot_general(a_ref[...], b_ref[...], dims, preferred_element_type=F32)
        if gk == 1:
            finish(prod)
        else:
            acc_ref = rest[n_ex + n_out]
            kk = pl.program_id(2)

            @pl.when(kk == 0)
            def _():
                acc_ref[...] = prod

            @pl.when(kk > 0)
            def _():
                acc_ref[...] += prod

            @pl.when(kk == gk - 1)
            def _():
                finish(acc_ref[...])

    o_spec = spec((tm, tn), lambda i, j, kk: (i, j))
    outs = pl.pallas_call(
        body, name=name, grid=(gj, gi, gk) if n_outer else (gi, gj, gk),
        in_specs=[a_spec, b_spec] + [o_spec] * len(extras) + [spec((1, tn), lambda i, j, kk: (0, j))] * len(consts),
        out_specs=[o_spec] * n_out,
        out_shape=[jax.ShapeDtypeStruct((m, n), dt) for dt in out_dtypes],
        scratch_shapes=[pltpu.VMEM((tm, tn), F32)] if gk > 1 else [],
        compiler_params=_cparams(("parallel", "parallel", "arbitrary")),
    )(a, b, *extras, *consts)
    return outs[0] if n_out == 1 else outs


def _row_spec(tm, width, col_block=0):
    return pl.BlockSpec((tm, width), lambda i: (i, col_block))


def _full_spec(shape):
    return pl.BlockSpec(shape, lambda *_: (0,) * len(shape))


def _rmsnorm_fwd(name, h, g, tm=512):
    t = h.shape[0]

    def body(h_ref, g_ref, o_ref):
        x = h_ref[...]
        r = lax.rsqrt(jnp.mean(x * x, axis=-1, keepdims=True) + NORM_EPS)
        o_ref[...] = (x * r * g_ref[...]).astype(BF16)

    return pl.pallas_call(
        body, name=name, grid=(t // tm,),
        in_specs=[_row_spec(tm, D_MODEL), _full_spec((1, D_MODEL))],
        out_specs=_row_spec(tm, D_MODEL),
        out_shape=jax.ShapeDtypeStruct((t, D_MODEL), BF16),
        compiler_params=_cparams(("parallel",)),
    )(h, g)


def _proj_norm_bwd(name, a_list, a_specs, parts, w, h, g, dres, tm):
    t = h.shape[0]
    na = len(a_list)

    def body(*refs):
        a_refs = refs[:na]
        w_ref, h_ref, g_ref, dres_ref, dh_ref, dhb_ref, dg_ref = refs[na:]
        i = pl.program_id(0)
        dn_v = None
        for which, lead, k0, k1 in parts:
            a_ref = a_refs[which]
            term = _dot(a_ref[...] if lead is None else a_ref[lead], w_ref[:, k0:k1], "nt")
            dn_v = term if dn_v is None else dn_v + term
        x = h_ref[...]
        r = lax.rsqrt(jnp.mean(x * x, axis=-1, keepdims=True) + NORM_EPS)
        xh = x * r
        dxh = dn_v * g_ref[...]
        dh = dres_ref[...] + r * (dxh - xh * jnp.mean(dxh * xh, axis=-1, keepdims=True))
        dh_ref[...] = dh
        dhb_ref[...] = dh.astype(BF16)
        part = jnp.sum(dn_v * xh, axis=0, keepdims=True)

        @pl.when(i == 0)
        def _():
            dg_ref[...] = part

        @pl.when(i > 0)
        def _():
            dg_ref[...] += part

    row = _row_spec(tm, D_MODEL)
    return pl.pallas_call(
        body, name=name, grid=(t // tm,),
        in_specs=list(a_specs) + [pl.BlockSpec(w.shape, lambda i: (0, 0), pipeline_mode=pl.Buffered(1)), row,
                                  _full_spec((1, D_MODEL)), row],
        out_specs=[row, row, _full_spec((1, D_MODEL))],
        out_shape=[jax.ShapeDtypeStruct((t, D_MODEL), F32), jax.ShapeDtypeStruct((t, D_MODEL), BF16),
                   jax.ShapeDtypeStruct((1, D_MODEL), F32)],
        compiler_params=_cparams(("arbitrary",)),
    )(*a_list, w, h, g, dres)


def _out_loss_and_grad(name, mid, w_out, h, g, target, tm=512):
    t = h.shape[0]

    def body(m_ref, w_ref, h_ref, g_ref, t_ref, dh_ref, dhb_ref, dg_ref, loss_ref):
        i = pl.program_id(0)
        x = h_ref[...] + FFN_RES_WEIGHT * _dot(m_ref[...], w_ref[...])
        gv = g_ref[...]
        r = lax.rsqrt(jnp.mean(x * x, axis=-1, keepdims=True) + NORM_EPS)
        xh = x * r
        err = xh * gv - t_ref[...]
        row = jnp.mean(err * err, axis=-1, keepdims=True)
        part_loss = 0.5 * jnp.sum(row, axis=0, keepdims=True)
        dy = err * (1.0 / D_MODEL)
        dxh = dy * gv
        dh = r * (dxh - xh * jnp.mean(dxh * xh, axis=-1, keepdims=True))
        dh_ref[...] = dh
        dhb_ref[...] = dh.astype(BF16)
        part = jnp.sum(dy * xh, axis=0, keepdims=True)

        @pl.when(i == 0)
        def _():
            dg_ref[...] = part
            loss_ref[...] = jnp.zeros(loss_ref.shape, F32) + part_loss

        @pl.when(i > 0)
        def _():
            dg_ref[...] += part
            loss_ref[...] += part_loss

    return pl.pallas_call(
        body, name=name, grid=(t // tm,),
        in_specs=[_row_spec(tm, D_FF), pl.BlockSpec((D_FF, D_MODEL), lambda i: (0, 0), pipeline_mode=pl.Buffered(1)),
                  _row_spec(tm, D_MODEL), _full_spec((1, D_MODEL)), _row_spec(tm, D_MODEL)],
        out_specs=[_row_spec(tm, D_MODEL), _row_spec(tm, D_MODEL), _full_spec((1, D_MODEL)), _full_spec((8, 128))],
        out_shape=[jax.ShapeDtypeStruct((t, D_MODEL), F32), jax.ShapeDtypeStruct((t, D_MODEL), BF16),
                   jax.ShapeDtypeStruct((1, D_MODEL), F32), jax.ShapeDtypeStruct((8, 128), F32)],
        compiler_params=_cparams(("arbitrary",)),
    )(mid, w_out, h, g, target)


def _rope_tables(t):
    half = HEAD_DIM // 2
    inv_freq = np.float32(ROPE_BASE) ** (-np.arange(half, dtype=np.float32) / np.float32(half))
    ang = (np.arange(t, dtype=np.float32)[:, None] * inv_freq[None, :].astype(np.float32)).astype(np.float32)
    return jnp.asarray(np.cos(ang.astype(np.float64)).astype(np.float32)), jnp.asarray(np.sin(ang.astype(np.float64)).astype(np.float32))


ROPE_HALF = HEAD_DIM // 2
K_SCALE = HEAD_DIM ** -0.5


def _rotate(ref, rows, hh, c, s, scale=None):
    lo, mid, hi = hh * HEAD_DIM, hh * HEAD_DIM + ROPE_HALF, (hh + 1) * HEAD_DIM
    x1, x2 = ref[rows, lo:mid].astype(F32), ref[rows, mid:hi].astype(F32)
    y = jnp.concatenate([x1 * c - x2 * s, x1 * s + x2 * c], axis=1)
    return y if scale is None else y * scale


def _unrotate_into(ref, rows, hh, dy, c, s, scale=None):
    lo, mid, hi = hh * HEAD_DIM, hh * HEAD_DIM + ROPE_HALF, (hh + 1) * HEAD_DIM
    y1, y2 = dy[:, :ROPE_HALF], dy[:, ROPE_HALF:]
    d1, d2 = y1 * c + y2 * s, y2 * c - y1 * s
    if scale is not None:
        d1, d2 = d1 * scale, d2 * scale
    ref[rows, lo:mid] = d1.astype(ref.dtype)
    ref[rows, mid:hi] = d2.astype(ref.dtype)


def _retention_tables():
    b, chunk = RET_BLOCK, 64
    gamma = 1.0 - 2.0 ** (-5.0 - np.arange(HEADS, dtype=np.float64))
    log_g = np.log(gamma)[:, None, None]
    i = np.arange(b)[:, None]
    j = np.arange(b)[None, :]
    same = (i // chunk) == (j // chunk)
    earlier = (j // chunk) < (i // chunk)
    expo = np.where(same, np.abs(i - j), np.where(earlier, i - j, 0)).astype(np.float64)
    mask = np.where(same | earlier, 1.0, 0.0)
    dmat = np.exp(log_g * expo[None]) * mask[None]
    qd = np.exp(log_g[:, :, 0] * (np.arange(b)[None, :] + 1.0))
    kd = np.exp(log_g[:, :, 0] * (b - 1.0 - np.arange(b)[None, :]))
    cd = np.exp(log_g[:, :, 0] * b) * np.ones((1, HEAD_DIM))
    as32 = lambda v: jnp.asarray(v.astype(np.float32))
    return (as32(dmat), as32(np.swapaxes(dmat, 1, 2)), as32(qd[:, :, None]), as32(kd[:, :, None]), as32(cd[:, None, :]))


def _dot(a, b, mode="nn"):
    return lax.dot_general(a, b, _DIMS[mode], preferred_element_type=F32)


GRET_BLOCK = 3


RET_SUBS = 2
RET_STEP = RET_SUBS * RET_BLOCK


def _head_specs(steps, rev=False):
    pos = (lambda n: steps - 1 - n) if rev else (lambda n: n)
    tok = pl.BlockSpec((RET_STEP, D_MODEL), lambda n: (pos(n), 0))
    blk = [pl.BlockSpec((RET_STEP, D_MODEL), lambda n, b=b: (pos(n), b)) for b in range(GRET_BLOCK + 1)]
    rope = pl.BlockSpec((RET_STEP, ROPE_HALF), lambda n: (pos(n), 0))
    tab = _full_spec((HEADS, RET_BLOCK, RET_BLOCK))
    col = _full_spec((HEADS, RET_BLOCK, 1))
    rowv = _full_spec((HEADS, 1, HEAD_DIM))
    st = pl.BlockSpec((HEADS, RET_SUBS, HEAD_DIM, HEAD_DIM), lambda n: (0, pos(n), 0, 0))
    return tok, blk, rope, tab, col, rowv, st


def _retention_fwd(name, proj, cos, sin, tables):
    t = proj.shape[0]
    nb, steps = t // RET_BLOCK, t // RET_STEP
    dmat, _, qd, kd, cd = tables
    tok, blk, rope, tab, col, rowv, st = _head_specs(steps)

    def body(q_ref, k_ref, v_ref, g_ref, c_ref, s_ref, d_ref, qd_ref, kd_ref, cd_ref, o_ref, ret_ref, st_ref, state):
        n = pl.program_id(0)

        @pl.when(n == 0)
        def _():
            state[...] = jnp.zeros(state.shape, F32)

        for sub in range(RET_SUBS):
            rows = slice(sub * RET_BLOCK, (sub + 1) * RET_BLOCK)
            cs, sn = c_ref[rows, :], s_ref[rows, :]
            for hh in range(HEADS):
                sl = slice(hh * HEAD_DIM, (hh + 1) * HEAD_DIM)
                q, k = _rotate(q_ref, rows, hh, cs, sn), _rotate(k_ref, rows, hh, cs, sn, K_SCALE)
                v = v_ref[rows, sl].astype(BF16)
                s = _dot(q.astype(BF16), k.astype(BF16), "nt") * d_ref[hh]
                stb = state[hh].astype(BF16)
                st_ref[hh, sub] = stb
                o = _dot(s.astype(BF16), v) + _dot((q * qd_ref[hh]).astype(BF16), stb)
                o_ref[rows, sl] = o
                rn = o * lax.rsqrt(jnp.mean(o * o, axis=-1, keepdims=True) + NORM_EPS)
                g = g_ref[rows, sl].astype(F32)
                ret_ref[rows, sl] = (rn * (g * _sigmoid(g))).astype(BF16)
                state[hh] = state[hh] * cd_ref[hh] + _dot((k * kd_ref[hh]).astype(BF16), v, "tn")

    return pl.pallas_call(
        body, name=name, grid=(steps,),
        in_specs=blk + [rope, rope, tab, col, col, rowv],
        out_specs=[tok, tok, st],
        out_shape=[jax.ShapeDtypeStruct((t, D_MODEL), F32), jax.ShapeDtypeStruct((t, D_MODEL), BF16),
                   jax.ShapeDtypeStruct((HEADS, nb, HEAD_DIM, HEAD_DIM), BF16)],
        scratch_shapes=[pltpu.VMEM((HEADS, HEAD_DIM, HEAD_DIM), F32)],
        compiler_params=_cparams(("arbitrary",)),
    )(proj, proj, proj, proj, cos, sin, dmat, qd, kd, cd)


def _retention_bwd(name, dru, w_ru, o, proj, cos, sin, states, tables):
    t = proj.shape[0]
    steps = t // RET_STEP
    dmat, dmat_t, qd, kd, cd = tables
    tok, blk, rope, tab, col, rowv, st = _head_specs(steps, rev=True)

    def body(dru_ref, wru_ref, o_ref, q_ref, k_ref, v_ref, g_ref, c_ref, s_ref, st_ref, d_ref, dt_ref, qd_ref, kd_ref, cd_ref,
             dq_ref, dk_ref, dv_ref, dg_ref, gstate):
        n = pl.program_id(0)

        @pl.when(n == 0)
        def _():
            gstate[...] = jnp.zeros(gstate.shape, F32)

        for sub in reversed(range(RET_SUBS)):
            rows = slice(sub * RET_BLOCK, (sub + 1) * RET_BLOCK)
            cs, sn = c_ref[rows, :], s_ref[rows, :]
            dret = _dot(dru_ref[rows, :], wru_ref[...], "nt")
            for hh in range(HEADS):
                sl = slice(hh * HEAD_DIM, (hh + 1) * HEAD_DIM)
                o_v, g, dr = o_ref[rows, sl], g_ref[rows, sl].astype(F32), dret[:, sl]
                sg = _sigmoid(g)
                r = lax.rsqrt(jnp.mean(o_v * o_v, axis=-1, keepdims=True) + NORM_EPS)
                rn = o_v * r
                d_rn = dr * (g * sg)
                dg_ref[rows, sl] = (dr * rn * (sg * (1.0 + g * (1.0 - sg)))).astype(BF16)
                d_o = r * (d_rn - rn * jnp.mean(d_rn * rn, axis=-1, keepdims=True))
                dob = d_o.astype(BF16)

                q, k = _rotate(q_ref, rows, hh, cs, sn), _rotate(k_ref, rows, hh, cs, sn, K_SCALE)
                v = v_ref[rows, sl].astype(BF16)
                qb, kb = q.astype(BF16), k.astype(BF16)
                qdv, kdv = qd_ref[hh], kd_ref[hh]
                s_t = (_dot(kb, qb, "nt") * dt_ref[hh]).astype(BF16)
                p_t = (_dot(v, dob, "nt") * dt_ref[hh]).astype(BF16)
                p = (_dot(dob, v, "nt") * d_ref[hh]).astype(BF16)
                stb = st_ref[hh, sub]
                gb = gstate[hh].astype(BF16)
                _unrotate_into(dq_ref, rows, hh, _dot(p, kb) + _dot(dob, stb, "nt") * qdv, cs, sn)
                _unrotate_into(dk_ref, rows, hh, _dot(p_t, qb) + _dot(v, gb, "nt") * kdv, cs, sn, K_SCALE)
                dv_ref[rows, sl] = (_dot(s_t, dob) + _dot((k * kdv).astype(BF16), gb)).astype(BF16)
                gstate[hh] = gstate[hh] * cd_ref[hh] + _dot((q * qdv).astype(BF16), dob, "tn")

    return pl.pallas_call(
        body, name=name, grid=(steps,),
        in_specs=[tok, pl.BlockSpec((D_MODEL, D_MODEL), lambda n: (0, 0), pipeline_mode=pl.Buffered(1)), tok] + blk
                 + [rope, rope, st, tab, tab, col, col, rowv],
        out_specs=[tok, tok, tok, tok],
        out_shape=[jax.ShapeDtypeStruct((t, D_MODEL), BF16)] * 4,
        scratch_shapes=[pltpu.VMEM((HEADS, HEAD_DIM, HEAD_DIM), F32)],
        compiler_params=_cparams(("arbitrary",)),
    )(dru, w_ru, o, proj, proj, proj, proj, cos, sin, states, dmat, dmat_t, qd, kd, cd)


POOL_TILE = 256
POOL_SUBS = 4
POOL_STEP = POOL_SUBS * POOL_TILE


def _pool_tables():
    b = POOL_TILE
    tt = np.arange(b)[:, None]
    jj = np.arange(b)[None, :]
    cur, prev = [], []
    for w in POOL_WINDOWS:
        cur.append(((tt - jj >= 0) & (tt - jj <= w - 1)).astype(np.float32))
        prev.append((tt - (jj - b) <= w - 1).astype(np.float32))
    cur, prev = np.stack(cur), np.stack(prev)
    as16 = lambda v: jnp.asarray(v, dtype=BF16)
    return as16(cur), as16(prev), as16(np.swapaxes(cur, 1, 2)), as16(np.swapaxes(prev, 1, 2))


def _split2(x):
    hi = x.astype(BF16)
    return hi, (x - hi.astype(F32)).astype(BF16)


POOL_BLOCK = 4


def _pool_count(n, window):
    tpos = n * POOL_TILE + lax.broadcasted_iota(jnp.int32, (POOL_TILE, 1), 0)
    return jnp.minimum(tpos + 1, window).astype(F32)


def _pool_fwd(name, proj, pool_w, scale, tables):
    t = proj.shape[0]
    steps = t // POOL_STEP
    mc, mp, _, _ = tables
    tab = _full_spec((GROUPS, POOL_TILE, POOL_TILE))
    row = _row_spec(POOL_STEP, D_MODEL)

    def body(pc_ref, pp_ref, mc_ref, mp_ref, w_ref, sc_ref, pm_ref, mix_ref, po_ref):
        n = pl.program_id(0)
        for sub in range(POOL_SUBS):
            rows = slice(sub * POOL_TILE, (sub + 1) * POOL_TILE)
            tile = n * POOL_SUBS + sub
            for g, window in enumerate(POOL_WINDOWS):
                sl = slice(g * GROUP_DIM, (g + 1) * GROUP_DIM)
                p = pc_ref[rows, sl]
                if sub == 0:
                    before = jnp.where(n > 0, _dot(mp_ref[g], pp_ref[:, sl]), 0.0)
                else:
                    before = _dot(mp_ref[g], pc_ref[(sub - 1) * POOL_TILE:sub * POOL_TILE, sl])
                pm = ((_dot(mc_ref[g], p) + before) / _pool_count(tile, window) - p.astype(F32)).astype(BF16)
                pm_ref[rows, sl] = pm
                mixed = _dot(pm, w_ref[g])
                mix_ref[rows, sl] = mixed
                po_ref[rows, sl] = (mixed * sc_ref[:, sl]).astype(BF16)

    return pl.pallas_call(
        body, name=name, grid=(steps,),
        in_specs=[_row_spec(POOL_STEP, D_MODEL, POOL_BLOCK),
                  pl.BlockSpec((POOL_TILE, D_MODEL), lambda n: (jnp.maximum(n * POOL_SUBS - 1, 0), POOL_BLOCK)),
                  tab, tab, _full_spec((GROUPS, GROUP_DIM, GROUP_DIM)), _full_spec((1, D_MODEL))],
        out_specs=[row] * 3,
        out_shape=[jax.ShapeDtypeStruct((t, D_MODEL), BF16), jax.ShapeDtypeStruct((t, D_MODEL), F32),
                   jax.ShapeDtypeStruct((t, D_MODEL), BF16)],
        compiler_params=_cparams(("parallel",)),
    )(proj, proj, mc, mp, pool_w, scale)


def _pool_bwd(name, dpu, w_pu, pm, mixed, pool_w, scale, tables):
    t = dpu.shape[0]
    steps = t // POOL_STEP
    _, _, mct, mpt = tables
    cur = pl.BlockSpec((POOL_STEP, D_MODEL), lambda n: (steps - 1 - n, 0))
    tab = _full_spec((GROUPS, POOL_TILE, POOL_TILE))
    wspec = _full_spec((GROUPS, GROUP_DIM, GROUP_DIM))
    sspec = _full_spec((1, D_MODEL))

    def body(dpu_ref, wpu_ref, pm_ref, mix_ref, mct_ref, mpt_ref, w_ref, sc_ref, dp_ref, dw_ref, ds_ref, later):
        n = pl.program_id(0)

        @pl.when(n == 0)
        def _():
            dw_ref[...] = jnp.zeros(dw_ref.shape, F32)
            ds_ref[...] = jnp.zeros(ds_ref.shape, F32)
            later[...] = jnp.zeros(later.shape, F32)

        for sub in reversed(range(POOL_SUBS)):
            rows = slice(sub * POOL_TILE, (sub + 1) * POOL_TILE)
            tile = (steps - 1 - n) * POOL_SUBS + sub
            dpo = _dot(dpu_ref[rows, :], wpu_ref[...], "nt")
            for g, window in enumerate(POOL_WINDOWS):
                sl = slice(g * GROUP_DIM, (g + 1) * GROUP_DIM)
                dc, sc = dpo[:, sl], sc_ref[:, sl]
                dmix = (dc * sc).astype(BF16)
                dpm = _dot(dmix, w_ref[g], "nt")
                e = dpm / _pool_count(tile, window)
                e_hi, e_lo = _split2(e)
                f_hi, f_lo = _split2(later[g])
                mctv, mptv = mct_ref[g], mpt_ref[g]
                back = _dot(mctv, e_hi) + _dot(mctv, e_lo)
                after = _dot(mptv, f_hi) + _dot(mptv, f_lo)
                dp_ref[rows, sl] = (back + after - dpm).astype(BF16)
                later[g] = e
                dw_ref[g] += _dot(pm_ref[rows, sl], dmix, "tn")
                ds_ref[:, sl] += jnp.sum(dc * mix_ref[rows, sl], axis=0, keepdims=True)

    return pl.pallas_call(
        body, name=name, grid=(steps,),
        in_specs=[cur, pl.BlockSpec((D_MODEL, D_MODEL), lambda n: (0, 0), pipeline_mode=pl.Buffered(1)), cur, cur, tab, tab,
                  wspec, sspec],
        out_specs=[cur, wspec, sspec],
        out_shape=[jax.ShapeDtypeStruct((t, D_MODEL), BF16), jax.ShapeDtypeStruct((GROUPS, GROUP_DIM, GROUP_DIM), F32),
                   jax.ShapeDtypeStruct((1, D_MODEL), F32)],
        scratch_shapes=[pltpu.VMEM((GROUPS, POOL_TILE, GROUP_DIM), F32)],
        compiler_params=_cparams(("arbitrary",)),
    )(dpu, w_pu, pm, mixed, mct, mpt, pool_w, scale)


GATE0_BLOCK, GATE1_BLOCK = 5, 6


def _merge_fwd(name, ret, po, w_ru, w_pu, w_out, proj, bias, h, next_g, tm=512):
    t = ret.shape[0]

    def body(r_ref, p_ref, wr_ref, wp_ref, wo_ref, g0_ref, g1_ref, b_ref, h_ref, ng_ref, m_ref, ru_ref, pu_ref, ho_ref, n_ref):
        ru = _dot(r_ref[...], wr_ref[...])
        pu = _dot(p_ref[...], wp_ref[...])
        ru_ref[...] = ru
        pu_ref[...] = pu
        merged = (_sigmoid(g0_ref[...].astype(F32) + b_ref[0:1, :]) * ru
                  + _sigmoid(g1_ref[...].astype(F32) + b_ref[1:2, :]) * pu).astype(BF16)
        m_ref[...] = merged
        h_new = h_ref[...] + _dot(merged, wo_ref[...])
        ho_ref[...] = h_new
        n_ref[...] = _normed(h_new, ng_ref[...]).astype(BF16)

    row = _row_spec(tm, D_MODEL)
    wspec = pl.BlockSpec((D_MODEL, D_MODEL), lambda i: (0, 0), pipeline_mode=pl.Buffered(1))
    return pl.pallas_call(
        body, name=name, grid=(t // tm,),
        in_specs=[row, row, wspec, wspec, wspec, _row_spec(tm, D_MODEL, GATE0_BLOCK), _row_spec(tm, D_MODEL, GATE1_BLOCK),
                  _full_spec((2, D_MODEL)), row, _full_spec((1, D_MODEL))],
        out_specs=[row] * 5,
        out_shape=[jax.ShapeDtypeStruct((t, D_MODEL), BF16), jax.ShapeDtypeStruct((t, D_MODEL), F32),
                   jax.ShapeDtypeStruct((t, D_MODEL), F32), jax.ShapeDtypeStruct((t, D_MODEL), F32),
                   jax.ShapeDtypeStruct((t, D_MODEL), BF16)],
        compiler_params=_cparams(("parallel",)),
    )(ret, po, w_ru, w_pu, w_out, proj, proj, bias, h, next_g)


def _merge_bwd(name, dh_b, w_out, ru, pu, proj, bias, tm=512):
    t = dh_b.shape[0]

    def body(dh_ref, wo_ref, ru_ref, pu_ref, g0_ref, g1_ref, b_ref, dru_ref, dpu_ref, dg0_ref, dg1_ref, db_ref):
        i = pl.program_id(0)
        d = _dot(dh_ref[...], wo_ref[...], "nt")
        s0 = _sigmoid(g0_ref[...].astype(F32) + b_ref[0:1, :])
        s1 = _sigmoid(g1_ref[...].astype(F32) + b_ref[1:2, :])
        dru_ref[...] = (d * s0).astype(BF16)
        dpu_ref[...] = (d * s1).astype(BF16)
        dg0 = d * ru_ref[...] * (s0 * (1.0 - s0))
        dg1 = d * pu_ref[...] * (s1 * (1.0 - s1))
        dg0_ref[...] = dg0.astype(BF16)
        dg1_ref[...] = dg1.astype(BF16)
        part0 = jnp.sum(dg0, axis=0, keepdims=True)
        part1 = jnp.sum(dg1, axis=0, keepdims=True)

        @pl.when(i == 0)
        def _():
            db_ref[0:1, :] = part0
            db_ref[1:2, :] = part1

        @pl.when(i > 0)
        def _():
            db_ref[0:1, :] += part0
            db_ref[1:2, :] += part1

    row = _row_spec(tm, D_MODEL)
    return pl.pallas_call(
        body, name=name, grid=(t // tm,),
        in_specs=[row, pl.BlockSpec((D_MODEL, D_MODEL), lambda i: (0, 0), pipeline_mode=pl.Buffered(1)), row, row,
                  _row_spec(tm, D_MODEL, GATE0_BLOCK), _row_spec(tm, D_MODEL, GATE1_BLOCK), _full_spec((2, D_MODEL))],
        out_specs=[row, row, row, row, _full_spec((2, D_MODEL))],
        out_shape=[jax.ShapeDtypeStruct((t, D_MODEL), BF16)] * 4 + [jax.ShapeDtypeStruct((2, D_MODEL), F32)],
        compiler_params=_cparams(("arbitrary",)),
    )(dh_b, w_out, ru, pu, proj, proj, bias)


def _half_scale(acc):
    return (FFN_RES_WEIGHT * acc,)


def _normed(h, g):
    return h * lax.rsqrt(jnp.mean(h * h, axis=-1, keepdims=True) + NORM_EPS) * g


def _residual_half_norm(acc, res, g):
    h = res + FFN_RES_WEIGHT * acc
    return h, _normed(h, g)


FF_TILE = D_FF // 2
DW_TILE = 256
SAVED_FF_DTYPE = BF16
FF_CHUNKS = ((0, 512), (512, 1024), (1024, FF_TILE))


def _ffn_in(name, nrm, w_in, tm=512):
    t = nrm.shape[0]
    nj = D_FF // FF_TILE

    def body(n_ref, wg_ref, wu_ref, a_ref, mid_ref):
        nv = n_ref[...]
        for c0, c1 in FF_CHUNKS:
            gate = _dot(nv, wg_ref[:, c0:c1])
            up = _dot(nv, wu_ref[:, c0:c1])
            s = _sigmoid(gate)
            silu = gate * s
            a_ref[0, :, c0:c1] = (FFN_RES_WEIGHT * up * (s * (1.0 + gate * (1.0 - s)))).astype(a_ref.dtype)
            a_ref[1, :, c0:c1] = (FFN_RES_WEIGHT * silu).astype(a_ref.dtype)
            mid_ref[:, c0:c1] = (silu * up).astype(BF16)

    return pl.pallas_call(
        body, name=name, grid=(nj, t // tm),
        in_specs=[pl.BlockSpec((tm, D_MODEL), lambda j, i: (i, 0)),
                  pl.BlockSpec((D_MODEL, FF_TILE), lambda j, i: (0, j)),
                  pl.BlockSpec((D_MODEL, FF_TILE), lambda j, i: (0, j + nj))],
        out_specs=[pl.BlockSpec((2, tm, FF_TILE), lambda j, i: (0, i, j)), pl.BlockSpec((tm, FF_TILE), lambda j, i: (i, j))],
        out_shape=[jax.ShapeDtypeStruct((2, t, D_FF), SAVED_FF_DTYPE), jax.ShapeDtypeStruct((t, D_FF), BF16)],
        compiler_params=_cparams(("parallel", "parallel")),
    )(nrm, w_in, w_in)


def _ffn_dact(name, dout_b, w_out, a, tm=512):
    t = dout_b.shape[0]

    def body(d_ref, w_ref, a_ref, da_ref):
        dv = d_ref[...]
        for c0, c1 in FF_CHUNKS:
            dm = _dot(dv, w_ref[c0:c1, :], "nt")
            da_ref[0, :, c0:c1] = (dm * a_ref[0, :, c0:c1].astype(F32)).astype(BF16)
            da_ref[1, :, c0:c1] = (dm * a_ref[1, :, c0:c1].astype(F32)).astype(BF16)

    blk = pl.BlockSpec((2, tm, FF_TILE), lambda j, i: (0, i, j))
    return pl.pallas_call(
        body, name=name, grid=(D_FF // FF_TILE, t // tm),
        in_specs=[pl.BlockSpec((tm, D_MODEL), lambda j, i: (i, 0)), pl.BlockSpec((FF_TILE, D_MODEL), lambda j, i: (j, 0)), blk],
        out_specs=blk,
        out_shape=jax.ShapeDtypeStruct((2, t, D_FF), BF16),
        compiler_params=_cparams(("parallel", "parallel")),
    )(dout_b, w_out, a)


def _ffn_fwd(tag, h, nrm, get_w_in, get_w_out, finish):
    t = h.shape[0]
    w_in = get_w_in(nrm)
    a, mid = _ffn_in(f"{tag}_in", nrm, w_in, tm=min(1024, t))
    w_out = get_w_out(mid)
    return finish(mid, w_out), (nrm, a, mid, w_in, w_out)


def _ffn_bwd(tag, h, g, saved, dout, dout_b, on_grads, flush):
    t = h.shape[0]
    nrm, a, mid, w_in, w_out = saved
    d_w_out = _matmul(f"{tag}_dwout", mid, dout_b, "tn", D_FF, D_MODEL, t, DW_TILE, D_MODEL, t, [BF16], epilogue=_half_scale,
                      resident="b")
    da = _ffn_dact(f"{tag}_dact", dout_b, w_out, a, tm=min(1024, t))
    nj = D_FF // DW_TILE
    d_w_in = _dw_resident(f"{tag}_dwin", nrm, [da], [pl.BlockSpec((None, t, DW_TILE), lambda s: (s // nj, 0, s % nj))],
                          2 * nj, None, DW_TILE)
    tie = on_grads({f"{tag}_w_in": d_w_in, f"{tag}_w_out": d_w_out})
    tm = min(256, t)
    dh, dh_b, dg = _proj_norm_bwd(f"{tag}_dn", [da], [pl.BlockSpec((2, tm, D_FF), lambda i: (0, i, 0))],
                                  ((0, 0, 0, D_FF), (0, 1, D_FF, 2 * D_FF)), w_in, h, g if tie is None else g + tie, dout, tm)
    return dh, dh_b, dg, flush(dh)


def _dw_resident(name, u, pieces, piece_specs, n_tiles, which_piece, tn):
    t = u.shape[0]
    npc = len(pieces)

    def body(*refs):
        u_ref, p_refs, o_ref, ut_ref = refs[0], refs[1:1 + npc], refs[1 + npc], refs[2 + npc]
        s = pl.program_id(0)

        @pl.when(s == 0)
        def _():
            ut_ref[...] = u_ref[...].T

        if npc == 1:
            o_ref[...] = _dot(ut_ref[...], p_refs[0][...]).astype(BF16)
        for which in range(npc if npc > 1 else 0):
            @pl.when(which_piece(s) == which)
            def _(which=which):
                o_ref[...] = _dot(ut_ref[...], p_refs[which][...]).astype(BF16)

    return pl.pallas_call(
        body, name=name, grid=(n_tiles,),
        in_specs=[pl.BlockSpec((t, D_MODEL), lambda s: (0, 0), pipeline_mode=pl.Buffered(1))] + list(piece_specs),
        out_specs=pl.BlockSpec((D_MODEL, tn), lambda s: (0, s)),
        out_shape=jax.ShapeDtypeStruct((D_MODEL, n_tiles * tn), BF16),
        scratch_shapes=[pltpu.VMEM((D_MODEL, t), BF16)],
        compiler_params=_cparams(("arbitrary",)),
    )(u, *pieces)


def _mix_dwin(name, u, pieces, tn=256):
    t = u.shape[0]
    nj = D_MODEL // tn
    specs = [pl.BlockSpec((t, tn), lambda s, k=k: (0, jnp.clip(s - k * nj, 0, nj - 1))) for k in range(len(pieces))]
    return _dw_resident(name, u, pieces, specs, len(pieces) * nj, lambda s: s // nj, tn)


def _local_step(x, target, vec, get_w, relay_w, on_grads, flush):
    t = x.shape[0]
    cos, sin = _rope_tables(t)
    rtab = _retention_tables()
    ptab = _pool_tables()
    w = {}

    def getter(group, name):
        def get(after):
            if name not in w:
                w.update(get_w(group, after))
            return w[name]
        return get

    nrm1 = _rmsnorm_fwd("ffn1_norm", x, vec["norm_ffn1"])
    def out_and_norm(mid, w_out):
        return _matmul("ffn1_out", mid, w_out, "nn", t, D_MODEL, D_FF, 1024, D_MODEL, D_FF, [F32, BF16],
                       extras=(x,), consts=(vec["norm_mix"],), epilogue=_residual_half_norm, resident="b")

    (h1, u), s1 = _ffn_fwd("ffn1", x, nrm1, getter(0, "ffn1_w_in"), getter(1, "ffn1_w_out"), out_and_norm)
    w.update(get_w(2, u))
    relay_w(3, w["w_in"])
    proj = _matmul("mix_in", u, w["w_in"], "nn", t, IN_WIDTH, D_MODEL, 4096, 1024, D_MODEL, [BF16], n_outer=True)
    w.update(get_w(3, proj))
    o, ret, states = _retention_fwd("retention", proj, cos, sin, rtab)
    pm, mixed, po = _pool_fwd("pool", proj, w["pool_w"], vec["pool_scale"], ptab)
    relay_w(4, po)
    merged, ru, pu, h2, nrm2 = _merge_fwd("merge", ret, po, w["w_ret_up"], w["w_pool_up"], w["w_out"], proj, w["gate_bias"],
                                          h1, vec["norm_ffn2"], tm=min(512, t))
    def out_and_loss(mid, w_out):
        return _out_loss_and_grad("ffn2_out_loss", mid, w_out, h2, vec["norm_final"], target, tm=min(512, t))

    (dh3, dh3_b, dg_final, loss), s2 = _ffn_fwd("ffn2", h2, nrm2, getter(4, "ffn2_w_in"), getter(4, "ffn2_w_out"), out_and_loss)

    def tied(v, tie):
        return v if tie is None else v + tie

    dh2, dh2_b, dg_ffn2, tie = _ffn_bwd("ffn2", h2, vec["norm_ffn2"], s2, dh3, dh3_b, on_grads, flush)
    def square_dw(name, act, grad):
        return _matmul(name, act, grad, "tn", D_MODEL, D_MODEL, t, D_MODEL, D_MODEL, 1024, [BF16])

    d_w_out = square_dw("mix_dwout", merged, dh2_b)
    dru, dpu, dg0, dg1, d_bias = _merge_bwd("merge_bwd", dh2_b, w["w_out"], ru, pu, proj, tied(w["gate_bias"], tie))
    d_w_ru = square_dw("mix_dwru", ret, dru)
    d_w_pu = square_dw("mix_dwpu", po, dpu)
    dp, d_pool_w, d_scale = _pool_bwd("pool_bwd", dpu, w["w_pool_up"], pm, mixed, w["pool_w"], vec["pool_scale"], ptab)
    dq, dk, dv, dgr = _retention_bwd("retention_bwd", dru, w["w_ret_up"], o, proj, cos, sin, states, rtab)
    dproj = [dq, dk, dv, dgr, dp, dg0, dg1]
    d_w_in = _mix_dwin("mix_dwin", u, dproj)
    tie = on_grads(dict(w_in=d_w_in, pool_w=d_pool_w.astype(BF16), w_ret_up=d_w_ru, w_pool_up=d_w_pu, w_out=d_w_out))
    tm = min(256, t)
    dh1, dh1_b, dg_mix = _proj_norm_bwd("mix_du", dproj, [_row_spec(tm, D_MODEL)] * len(dproj),
                                        [(k, None, k * D_MODEL, (k + 1) * D_MODEL) for k in range(len(dproj))],
                                        w["w_in"], h1, tied(vec["norm_mix"], tie), dh2, tm)
    tie = flush(dh1)
    dx, _, dg_ffn1, _ = _ffn_bwd("ffn1", x, tied(vec["norm_ffn1"], tie), s1, dh1, dh1_b, on_grads, flush)

    small = dict(norm_ffn1=dg_ffn1, norm_mix=dg_mix, gate_bias=d_bias, pool_scale=d_scale, norm_ffn2=dg_ffn2,
                 norm_final=dg_final)
    return loss[0, 0], dx, small


BIG = ("ffn1_w_in", "ffn1_w_out", "w_in", "pool_w", "w_ret_up", "w_pool_up", "w_out", "ffn2_w_in", "ffn2_w_out")
KIND = dict(ffn1_w_in="col", ffn1_w_out="row", w_in="col", pool_w="pool", w_ret_up="row", w_pool_up="row", w_out="row",
            ffn2_w_in="col", ffn2_w_out="row", gate_bias="col")
ANY = pl.BlockSpec(memory_space=pl.ANY)


def _place():
    x, y, c = lax.axis_index("x"), lax.axis_index("y"), lax.axis_index("c")
    chips = [(1 - x, y), (x, 1 - y), (1 - x, 1 - y)]
    return x, y, c, chips


def _full_view_shape(kind, local_shape):
    if kind == "col":
        return (2, local_shape[0] // 2, N_CHIPS * local_shape[1])
    if kind == "row":
        return (N_CHIPS, 2, local_shape[0] // 2, local_shape[1])
    return (GROUPS, N_CHIPS, 2, local_shape[1] // 2, local_shape[2])


def _local_view(kind, arr):
    if kind == "pool":
        return arr.reshape(GROUPS, 2, arr.shape[1] // 2, arr.shape[2])
    return arr.reshape(2, arr.shape[0] // 2, arr.shape[1])


def _blk(kind, ref, s, c):
    if kind == "col":
        cs = ref.shape[2] // N_CHIPS
        return ref.at[c, :, pl.ds(pl.multiple_of(s * cs, 128), cs)]
    if kind == "row":
        return ref.at[s, c]
    return ref.at[:, s, c]


def _half(kind, ref, c):
    return ref.at[:, c] if kind == "pool" else ref.at[c]


def _shard(kind, ref, s):
    if kind == "col":
        cs = ref.shape[2] // N_CHIPS
        return ref.at[:, :, pl.ds(pl.multiple_of(s * cs, 128), cs)]
    if kind == "row":
        return ref.at[s]
    return ref.at[:, s]


HBM = pl.BlockSpec(memory_space=pltpu.HBM)
SEM = pl.BlockSpec(memory_space=pltpu.SEMAPHORE)
EFFECT = pltpu.SideEffectType.DATAFLOW_SIDE_EFFECTING
WEIGHT_GROUPS = (("gate_bias", "ffn1_w_in"), ("ffn1_w_out",), ("w_in",), ("pool_w", "w_ret_up", "w_pool_up", "w_out"),
                 ("ffn2_w_in", "ffn2_w_out"))
GRAD_GROUPS = (("ffn2_w_in", "ffn2_w_out"), ("w_in", "pool_w", "w_ret_up", "w_pool_up", "w_out"), ("ffn1_w_in", "ffn1_w_out"))


def _hbm(a):
    return pltpu.with_memory_space_constraint(a, pltpu.HBM)


def _natural(kind, o):
    if kind == "col":
        return o.reshape(o.shape[0] * o.shape[1], o.shape[2])
    if kind == "row":
        return o.reshape(-1, o.shape[3])
    return o.reshape(GROUPS, -1, o.shape[4])


def _ici_copy(kind, loc, full, j, chips, s, c, send_sem, recv_sem):
    px, py = chips[j]
    return (pltpu.make_async_remote_copy(src_ref=_half(kind, loc, c), dst_ref=_blk(kind, full, s, c), send_sem=send_sem,
                                         recv_sem=recv_sem, device_id=(px, py, c), device_id_type=MESH),
            pltpu.make_async_remote_copy(src_ref=_half(kind, loc, c), dst_ref=_blk(kind, full, 2 * px + py, c), send_sem=send_sem,
                                         recv_sem=recv_sem, device_id=(px, py, c), device_id_type=MESH))


def _gather_start(tag, group_ids, shards):
    grps = [WEIGHT_GROUPS[g] for g in group_ids]
    names = [nm for grp in grps for nm in grp]
    kinds = [KIND[nm] for nm in names]
    n, ng = len(names), len(grps)
    locs = [_hbm(_local_view(KIND[nm], shards[nm])) for nm in names]
    lands = [_hbm(lax.empty(_full_view_shape(KIND[nm], shards[nm].shape), shards[nm].dtype)) for nm in names]
    first = np.cumsum([0] + [len(grp) for grp in grps])

    def body(*refs):
        loc, full = refs[:n], refs[n:2 * n]
        send_sems, recv_sems = refs[2 * n:2 * n + ng], refs[2 * n + ng:2 * n + 2 * ng]
        token = refs[-1]
        x, y, c, chips = _place()
        s = 2 * x + y
        for g in range(ng):
            for a in range(first[g], first[g + 1]):
                for j in range(3):
                    k = 3 * (a - first[g]) + j
                    _ici_copy(kinds[a], loc[a], full[a], j, chips, s, c, send_sems[g].at[k], recv_sems[g].at[k])[0].start()
        token[...] = jnp.zeros(token.shape, F32)

    sem_shapes = [pltpu.SemaphoreType.DMA((3 * len(grp),)) for grp in grps]
    outs = pl.pallas_call(
        body, name=f"gather_start_{tag}",
        in_specs=[HBM] * (2 * n),
        out_specs=[SEM] * (2 * ng) + [HBM] * (2 * n) + [pl.BlockSpec(memory_space=pltpu.VMEM)],
        out_shape=sem_shapes + sem_shapes + [pltpu.HBM(a.shape, a.dtype) for a in locs + lands] + [jax.ShapeDtypeStruct((8, 128), F32)],
        input_output_aliases={i: 2 * ng + i for i in range(2 * n)},
        compiler_params=pltpu.CompilerParams(has_side_effects=EFFECT),
    )(*locs, *lands)
    send_sems, recv_sems = outs[:ng], outs[ng:2 * ng]
    locs_t, lands_t = outs[2 * ng:2 * ng + n], outs[2 * ng + n:2 * ng + 2 * n]
    groups = {}
    for k, g in enumerate(group_ids):
        sl = slice(first[k], first[k + 1])
        groups[g] = (send_sems[k], recv_sems[k], list(locs_t[sl]), list(lands_t[sl]))
    return groups, outs[-1]


def _forward_copies(kinds, loc, full, send_sems, recv_sems):
    x, y, c, chips = _place()
    s = 2 * x + y

    def remote(a, k, src, dst):
        return pltpu.make_async_remote_copy(src_ref=src, dst_ref=dst, send_sem=send_sems.at[4 * a + k],
                                            recv_sem=recv_sems.at[4 * a + k], device_id=(x, y, 1 - c), device_id_type=MESH)

    sends, arrivals = [], []
    for a, kind in enumerate(kinds):
        for j, (px, py) in enumerate(chips):
            theirs, from_sib = _blk(kind, full[a], 2 * px + py, c), _blk(kind, full[a], 2 * px + py, 1 - c)
            sends.append(remote(a, j, theirs, theirs))
            arrivals.append(remote(a, j, from_sib, from_sib))
        own = _shard(kind, full[a], s)
        sends.append(remote(a, 3, loc[a], own))
        arrivals.append(remote(a, 3, own, own))
    return sends, arrivals


def _gather_relay(g, group, after):
    names = WEIGHT_GROUPS[g]
    kinds = [KIND[nm] for nm in names]
    m = len(names)
    ici_send, ici_recv, locs, lands = group

    def body(*refs):
        loc, full = refs[:m], refs[m:2 * m]
        ici_s, ici_r = refs[2 * m], refs[2 * m + 1]
        d2d_s, d2d_r = refs[2 * m + 2 + len(after)], refs[2 * m + 3 + len(after)]
        x, y, c, chips = _place()
        for a in range(m):
            for j in range(3):
                sent, landed = _ici_copy(kinds[a], loc[a], full[a], j, chips, 2 * x + y, c, ici_s.at[3 * a + j], ici_r.at[3 * a + j])
                sent.wait_send()
                landed.wait_recv()
        for cp in _forward_copies(kinds, loc, full, d2d_s, d2d_r)[0]:
            cp.start()

    sem_shape = pltpu.SemaphoreType.DMA((4 * m,))
    outs = pl.pallas_call(
        body, name=f"gather_relay_{g}",
        in_specs=[HBM] * (2 * m) + [SEM, SEM] + [ANY] * len(after), out_specs=[SEM, SEM] + [HBM] * (2 * m),
        out_shape=[sem_shape, sem_shape] + [pltpu.HBM(a.shape, a.dtype) for a in locs + lands],
        input_output_aliases={i: 2 + i for i in range(2 * m)},
        compiler_params=pltpu.CompilerParams(has_side_effects=EFFECT),
    )(*locs, *lands, ici_send, ici_recv, *after)
    return outs[0], outs[1], list(outs[2:2 + m]), list(outs[2 + m:2 + 2 * m])


def _gather_land(g, state, after):
    names = WEIGHT_GROUPS[g]
    kinds = [KIND[nm] for nm in names]
    m = len(names)
    d2d_send, d2d_recv, locs, lands = state

    def body(*refs):
        sends, arrivals = _forward_copies(kinds, refs[:m], refs[m:2 * m], refs[2 * m], refs[2 * m + 1])
        for cp in sends:
            cp.wait_send()
        for cp in arrivals:
            cp.wait_recv()

    outs = pl.pallas_call(
        body, name=f"gather_land_{g}",
        in_specs=[HBM] * (2 * m) + [SEM, SEM] + [ANY] * len(after), out_specs=[HBM] * (2 * m),
        out_shape=[pltpu.HBM(a.shape, a.dtype) for a in locs + lands],
        input_output_aliases={i: i for i in range(2 * m)},
        compiler_params=pltpu.CompilerParams(has_side_effects=EFFECT),
    )(*locs, *lands, d2d_send, d2d_recv, *after)
    return {nm: _natural(k, o) for nm, k, o in zip(names, kinds, outs[m:])}


def _gather_finish(g, group, after):
    names = WEIGHT_GROUPS[g]
    kinds = [KIND[nm] for nm in names]
    m = len(names)
    send_sem, recv_sem, locs, lands = group

    def wait_body(*refs):
        loc, full = refs[:m], refs[m:2 * m]
        send_sems, recv_sems = refs[2 * m], refs[2 * m + 1]
        x, y, c, chips = _place()
        s = 2 * x + y
        for a in range(m):
            for j in range(3):
                k = 3 * a + j
                sent, landed = _ici_copy(kinds[a], loc[a], full[a], j, chips, s, c, send_sems.at[k], recv_sems.at[k])
                sent.wait_send()
                landed.wait_recv()

    outs = pl.pallas_call(
        wait_body, name=f"gather_wait_{g}",
        in_specs=[HBM] * (2 * m) + [SEM, SEM] + [ANY] * len(after), out_specs=[HBM] * (2 * m),
        out_shape=[pltpu.HBM(a.shape, a.dtype) for a in locs + lands],
        input_output_aliases={i: i for i in range(2 * m)},
        compiler_params=pltpu.CompilerParams(has_side_effects=EFFECT),
    )(*locs, *lands, send_sem, recv_sem, *after)
    locs, lands = outs[:m], outs[m:]

    def forward_body(*refs):
        sends, arrivals = _forward_copies(kinds, refs[:m], refs[2 * m:3 * m], *refs[3 * m:])
        for cp in sends:
            cp.start()
        for cp in arrivals:
            cp.wait_recv()
        for cp in sends:
            cp.wait_send()

    outs = pl.pallas_call(
        forward_body, name=f"gather_forward_{g}",
        in_specs=[ANY] * (2 * m), out_specs=[ANY] * m,
        out_shape=[jax.ShapeDtypeStruct(a.shape, a.dtype) for a in lands],
        input_output_aliases={m + i: i for i in range(m)},
        scratch_shapes=[pltpu.SemaphoreType.DMA((4 * m,)), pltpu.SemaphoreType.DMA((4 * m,))],
    )(*locs, *lands)
    return {nm: _natural(k, o) for nm, k, o in zip(names, kinds, outs)}


def _grad_view(kind, g):
    if kind == "col":
        return g.reshape(2, g.shape[0] // 2, g.shape[1])
    if kind == "row":
        return g.reshape(N_CHIPS, 2, g.shape[0] // (2 * N_CHIPS), g.shape[1])
    return g.reshape(GROUPS, N_CHIPS, 2, g.shape[1] // (2 * N_CHIPS), g.shape[2])


def _pair_copies(kinds, g, got, send_sems, recv_sems):
    x, y, c, _ = _place()

    def other_half(kind, ref):
        if kind == "col":
            return ref.at[1 - c]
        if kind == "row":
            return ref.at[:, 1 - c]
        return ref.at[:, :, 1 - c]

    return [pltpu.make_async_remote_copy(src_ref=other_half(kinds[a], g[a]), dst_ref=got[a], send_sem=send_sems.at[a],
                                         recv_sem=recv_sems.at[a], device_id=(x, y, 1 - c), device_id_type=MESH)
            for a in range(len(kinds))]


def _pair_exchange_start(tag, names, views):
    kinds = [KIND[nm] for nm in names]
    n = len(names)

    def got_shape(kind, v):
        if kind == "col":
            return v.shape[1:]
        if kind == "row":
            return (v.shape[0],) + v.shape[2:]
        return v.shape[:2] + v.shape[3:]

    srcs = [_hbm(views[nm]) for nm in names]
    lands = [_hbm(lax.empty(got_shape(k, views[nm]), BF16)) for nm, k in zip(names, kinds)]

    def body(*refs):
        g, got = refs[:n], refs[n:2 * n]
        for cp in _pair_copies(kinds, g, got, refs[2 * n], refs[2 * n + 1]):
            cp.start()
        refs[-1][...] = jnp.zeros(refs[-1].shape, F32)

    sem_shape = pltpu.SemaphoreType.DMA((n,))
    outs = pl.pallas_call(
        body, name=f"grad_pair_exchange_start_{tag}",
        in_specs=[HBM] * (2 * n),
        out_specs=[SEM, SEM] + [HBM] * (2 * n) + [pl.BlockSpec(memory_space=pltpu.VMEM)],
        out_shape=[sem_shape, sem_shape] + [pltpu.HBM(a.shape, a.dtype) for a in srcs + lands] + [jax.ShapeDtypeStruct((8, 128), F32)],
        input_output_aliases={i: 2 + i for i in range(2 * n)},
        compiler_params=pltpu.CompilerParams(has_side_effects=EFFECT),
    )(*srcs, *lands)
    return (outs[0], outs[1], list(outs[2:2 + n]), list(outs[2 + n:2 + 2 * n])), outs[-1]


def _pair_exchange_wait(tag, names, state, after):
    kinds = [KIND[nm] for nm in names]
    n = len(names)
    send_sem, recv_sem, srcs, lands = state

    def body(*refs):
        g, got = refs[:n], refs[n:2 * n]
        for cp in _pair_copies(kinds, g, got, refs[2 * n], refs[2 * n + 1]):
            cp.wait_send()
            cp.wait_recv()

    outs = pl.pallas_call(
        body, name=f"grad_pair_exchange_wait_{tag}",
        in_specs=[HBM] * (2 * n) + [SEM, SEM, ANY], out_specs=[HBM] * (2 * n),
        out_shape=[pltpu.HBM(a.shape, a.dtype) for a in srcs + lands],
        input_output_aliases={i: i for i in range(2 * n)},
        compiler_params=pltpu.CompilerParams(has_side_effects=EFFECT),
    )(*srcs, *lands, send_sem, recv_sem, after)
    return dict(zip(names, outs[:n])), dict(zip(names, outs[n:]))


def _pair_sum(name, kind, view, got, c_arr):
    if kind == "col":
        _, rows, cols = view.shape
        tr = 128
        grid = (rows // tr,)
        v_spec = pl.BlockSpec((None, tr, cols), lambda i, c: (c[0], i, 0))
        g_spec = pl.BlockSpec((tr, cols), lambda i, c: (i, 0))
    elif kind == "row":
        _, _, rows, cols = view.shape
        grid = (N_CHIPS,)
        v_spec = pl.BlockSpec((None, None, rows, cols), lambda i, c: (i, c[0], 0, 0))
        g_spec = pl.BlockSpec((None, rows, cols), lambda i, c: (i, 0, 0))
    else:
        _, _, _, rows, cols = view.shape
        grid = (GROUPS,)
        v_spec = pl.BlockSpec((None, N_CHIPS, None, rows, cols), lambda i, c: (i, 0, c[0], 0, 0))
        g_spec = pl.BlockSpec((None, N_CHIPS, rows, cols), lambda i, c: (i, 0, 0, 0))

    def body(c_ref, v_ref, g_ref, o_ref):
        o_ref[...] = (v_ref[...].astype(F32) + g_ref[...].astype(F32)).astype(BF16)

    return pl.pallas_call(
        body, name=name,
        grid_spec=pltpu.PrefetchScalarGridSpec(num_scalar_prefetch=1, grid=grid, in_specs=[v_spec, g_spec], out_specs=g_spec),
        out_shape=jax.ShapeDtypeStruct(got.shape, BF16),
        compiler_params=_cparams(("parallel",)),
    )(c_arr, view, got)


def _piece(kind, ref, s):
    if kind == "col":
        cs = ref.shape[1] // N_CHIPS
        return ref.at[:, pl.ds(pl.multiple_of(s * cs, 128), cs)]
    if kind == "row":
        return ref.at[s]
    return ref.at[:, s]


def _piece_shape(kind, shape):
    if kind == "col":
        return (shape[0], shape[1] // N_CHIPS)
    if kind == "row":
        return shape[1:]
    return (shape[0],) + shape[2:]


def _shard_copies(kinds, p, got, send_sems, recv_sems):
    x, y, c, chips = _place()
    return [pltpu.make_async_remote_copy(src_ref=_piece(kinds[a], p[a], 2 * px + py), dst_ref=got[a].at[j],
                                         send_sem=send_sems.at[3 * a + j], recv_sem=recv_sems.at[3 * a + j],
                                         device_id=(px, py, c), device_id_type=MESH)
            for a in range(len(kinds)) for j, (px, py) in enumerate(chips)]


def _shard_exchange_start(g, names, psums):
    kinds = [KIND[nm] for nm in names]
    n = len(names)
    srcs = [_hbm(psums[nm]) for nm in names]
    lands = [_hbm(lax.empty((3,) + _piece_shape(k, psums[nm].shape), BF16)) for nm, k in zip(names, kinds)]

    def body(*refs):
        p, got = refs[:n], refs[n:2 * n]
        send_sems, recv_sems = refs[2 * n], refs[2 * n + 1]
        token = refs[-1]
        for cp in _shard_copies(kinds, p, got, send_sems, recv_sems):
            cp.start()
        token[...] = jnp.zeros(token.shape, F32)

    sem_shape = pltpu.SemaphoreType.DMA((3 * n,))
    outs = pl.pallas_call(
        body, name=f"grad_shard_exchange_start_{g}",
        in_specs=[HBM] * (2 * n),
        out_specs=[SEM, SEM] + [HBM] * (2 * n) + [pl.BlockSpec(memory_space=pltpu.VMEM)],
        out_shape=[sem_shape, sem_shape] + [pltpu.HBM(a.shape, a.dtype) for a in srcs + lands] + [jax.ShapeDtypeStruct((8, 128), F32)],
        input_output_aliases={i: 2 + i for i in range(2 * n)},
        compiler_params=pltpu.CompilerParams(has_side_effects=EFFECT),
    )(*srcs, *lands)
    return (outs[0], outs[1], list(outs[2:2 + n]), list(outs[2 + n:2 + 2 * n])), outs[-1]


def _shard_exchange_wait(g, names, state, after):
    kinds = [KIND[nm] for nm in names]
    n = len(names)
    send_sem, recv_sem, srcs, lands = state

    def body(*refs):
        p, got = refs[:n], refs[n:2 * n]
        for cp in _shard_copies(kinds, p, got, refs[2 * n], refs[2 * n + 1]):
            cp.wait_send()
            cp.wait_recv()

    outs = pl.pallas_call(
        body, name=f"grad_shard_exchange_wait_{g}",
        in_specs=[HBM] * (2 * n) + [SEM, SEM] + [ANY] * len(after), out_specs=[HBM] * (2 * n),
        out_shape=[pltpu.HBM(a.shape, a.dtype) for a in srcs + lands],
        input_output_aliases={i: i for i in range(2 * n)},
        compiler_params=pltpu.CompilerParams(has_side_effects=EFFECT),
    )(*srcs, *lands, send_sem, recv_sem, *after)
    return dict(zip(names, outs[:n])), dict(zip(names, outs[n:]))


def _shard_sum(name, kind, psum, got, sc_arr):
    if kind == "col":
        rows, cols = psum.shape
        cs = cols // N_CHIPS
        tr = 128
        grid = (rows // tr,)
        p_spec = pl.BlockSpec((tr, cs), lambda i, sc: (i, sc[0]))
        g_spec = pl.BlockSpec((3, tr, cs), lambda i, sc: (0, i, 0))
        o_spec = pl.BlockSpec((None, tr, cs), lambda i, sc: (sc[1], i, 0))
        out_shape = (2, rows, cs)
    elif kind == "row":
        _, rows, cols = psum.shape
        grid = (1,)
        p_spec = pl.BlockSpec((None, rows, cols), lambda i, sc: (sc[0], 0, 0))
        g_spec = pl.BlockSpec((3, rows, cols), lambda i, sc: (0, 0, 0))
        o_spec = pl.BlockSpec((None, rows, cols), lambda i, sc: (sc[1], 0, 0))
        out_shape = (2, rows, cols)
    else:
        _, _, rows, cols = psum.shape
        grid = (1,)
        p_spec = pl.BlockSpec((GROUPS, None, rows, cols), lambda i, sc: (0, sc[0], 0, 0))
        g_spec = pl.BlockSpec((3, GROUPS, rows, cols), lambda i, sc: (0, 0, 0, 0))
        o_spec = pl.BlockSpec((GROUPS, None, rows, cols), lambda i, sc: (0, sc[1], 0, 0))
        out_shape = (GROUPS, 2, rows, cols)

    def body(sc_ref, p_ref, g_ref, o_ref):
        o_ref[...] = ((p_ref[...].astype(F32) + g_ref[0].astype(F32)) + g_ref[1].astype(F32)) + g_ref[2].astype(F32)

    return pl.pallas_call(
        body, name=name,
        grid_spec=pltpu.PrefetchScalarGridSpec(num_scalar_prefetch=1, grid=grid, in_specs=[p_spec, g_spec], out_specs=o_spec),
        out_shape=jax.ShapeDtypeStruct(out_shape, F32),
        compiler_params=_cparams(("parallel",)),
    )(sc_arr, psum, got)


def _half_copies(kinds, bufs, send_sems, recv_sems):
    x, y, c, _ = _place()

    def remote(a, half):
        part = _half(kinds[a], bufs[a], half)
        return pltpu.make_async_remote_copy(src_ref=part, dst_ref=part, send_sem=send_sems.at[a], recv_sem=recv_sems.at[a],
                                            device_id=(x, y, 1 - c), device_id_type=MESH)

    return [remote(a, c) for a in range(len(kinds))], [remote(a, 1 - c) for a in range(len(kinds))]


def _half_exchange_start(tag, names, bufs):
    kinds = [KIND[nm] for nm in names]
    n = len(names)
    arrs = [_hbm(bufs[nm]) for nm in names]

    def body(*refs):
        for cp in _half_copies(kinds, refs[:n], refs[n], refs[n + 1])[0]:
            cp.start()
        refs[-1][...] = jnp.zeros(refs[-1].shape, F32)

    sem_shape = pltpu.SemaphoreType.DMA((n,))
    outs = pl.pallas_call(
        body, name=f"grad_half_exchange_start_{tag}",
        in_specs=[HBM] * n,
        out_specs=[SEM, SEM] + [HBM] * n + [pl.BlockSpec(memory_space=pltpu.VMEM)],
        out_shape=[sem_shape, sem_shape] + [pltpu.HBM(a.shape, a.dtype) for a in arrs] + [jax.ShapeDtypeStruct((8, 128), F32)],
        input_output_aliases={i: 2 + i for i in range(n)},
        compiler_params=pltpu.CompilerParams(has_side_effects=EFFECT),
    )(*arrs)
    return (outs[0], outs[1], list(outs[2:2 + n])), outs[-1]


def _half_exchange_wait(tag, names, state, after):
    kinds = [KIND[nm] for nm in names]
    n = len(names)
    send_sem, recv_sem, arrs = state

    def body(*refs):
        sends, arrivals = _half_copies(kinds, refs[:n], refs[n], refs[n + 1])
        for cp in sends:
            cp.wait_send()
        for cp in arrivals:
            cp.wait_recv()

    outs = pl.pallas_call(
        body, name=f"grad_half_exchange_wait_{tag}",
        in_specs=[HBM] * n + [SEM, SEM] + [ANY] * len(after), out_specs=[HBM] * n,
        out_shape=[pltpu.HBM(a.shape, a.dtype) for a in arrs],
        input_output_aliases={i: i for i in range(n)},
        compiler_params=pltpu.CompilerParams(has_side_effects=EFFECT),
    )(*arrs, send_sem, recv_sem, *after)
    return dict(zip(names, outs))


N_DEV = 8
SMALL_ROWS = 8


def _all_reduce_small(name, v):
    def body(v_ref, o_ref, token, buf, send_sems, recv_sems):
        token[...] = jnp.zeros(token.shape, F32)
        x, y, c, _ = _place()
        me = 4 * x + 2 * y + c
        buf[me] = v_ref[...]
        cps = []
        for r in range(1, N_DEV):
            to = (x ^ (r >> 2), y ^ ((r >> 1) & 1), c ^ (r & 1))
            cp = pltpu.make_async_remote_copy(src_ref=v_ref, dst_ref=buf.at[me], send_sem=send_sems.at[r - 1],
                                              recv_sem=recv_sems.at[r - 1], device_id=to, device_id_type=MESH)
            cp.start()
            cps.append(cp)
        for r in range(1, N_DEV):
            pltpu.make_async_remote_copy(src_ref=v_ref, dst_ref=buf.at[me ^ r], send_sem=send_sems.at[r - 1],
                                         recv_sem=recv_sems.at[r - 1], device_id=(x, y, c), device_id_type=MESH).wait_recv()
        for cp in cps:
            cp.wait_send()
        acc = buf[0]
        for d in range(1, N_DEV):
            acc = acc + buf[d]
        o_ref[...] = acc

    vm = pl.BlockSpec(memory_space=pltpu.VMEM)
    return pl.pallas_call(
        body, name=name, in_specs=[vm], out_specs=[vm, vm],
        out_shape=[jax.ShapeDtypeStruct((SMALL_ROWS, D_MODEL), F32), jax.ShapeDtypeStruct((8, 128), F32)],
        scratch_shapes=[pltpu.VMEM((N_DEV, SMALL_ROWS, D_MODEL), F32), pltpu.SemaphoreType.DMA((N_DEV - 1,)),
                        pltpu.SemaphoreType.DMA((N_DEV - 1,))],
    )(v)


def _adamw(name, w, g, m, v, with_grad=False):
    rows, cols = w.shape
    tr = next((c for c in (256, 176, 128, 64, 32, 8) if rows % c == 0), rows)
    spec = pl.BlockSpec((tr, cols), lambda i: (i, 0))

    def body(w_ref, g_ref, m_ref, v_ref, d_ref, mo_ref, vo_ref, *go_ref):
        gv = g_ref[...]
        if with_grad:
            go_ref[0][...] = gv
        m_new = ADAM_B1 * m_ref[...] + (1.0 - ADAM_B1) * gv
        v_new = ADAM_B2 * v_ref[...] + (1.0 - ADAM_B2) * jnp.square(gv)
        m_hat = m_new / (1.0 - ADAM_B1 ** ADAM_STEP)
        v_hat = v_new / (1.0 - ADAM_B2 ** ADAM_STEP)
        d_ref[...] = -ADAM_LR * (m_hat / (jnp.sqrt(v_hat) + ADAM_EPS) + ADAM_WD * w_ref[...])
        mo_ref[...] = m_new
        vo_ref[...] = v_new

    return pl.pallas_call(
        body, name=name, grid=(rows // tr,),
        in_specs=[spec] * 4, out_specs=[spec] * (4 if with_grad else 3),
        out_shape=[jax.ShapeDtypeStruct((rows, cols), F32)] * (4 if with_grad else 3),
        compiler_params=_cparams(("parallel",)),
    )(w, g, m, v)


WEIGHTS = ("norm_ffn1", "ffn1_w_in", "ffn1_w_out", "norm_mix", "w_in", "gate_bias", "pool_w", "pool_scale", "w_ret_up",
           "w_pool_up", "w_out", "norm_ffn2", "ffn2_w_in", "ffn2_w_out", "norm_final")
SMALL_ROW = dict(norm_ffn1=0, norm_mix=1, gate_bias=2, pool_scale=4, norm_ffn2=5, norm_final=6)


def _as2d(a):
    return a.reshape(-1, a.shape[-1])


def kernel(x, norm_ffn1, ffn1_w_in, ffn1_w_out, norm_mix, w_in, gate_bias, pool_w, pool_scale, w_ret_up, w_pool_up, w_out, norm_ffn2, ffn2_w_in, ffn2_w_out, norm_final, loss_target, m_norm_ffn1, m_ffn1_w_in, m_ffn1_w_out, m_norm_mix, m_w_in, m_gate_bias, m_pool_w, m_pool_scale, m_w_ret_up, m_w_pool_up, m_w_out, m_norm_ffn2, m_ffn2_w_in, m_ffn2_w_out, m_norm_final, v_norm_ffn1, v_ffn1_w_in, v_ffn1_w_out, v_norm_mix, v_w_in, v_gate_bias, v_pool_w, v_pool_scale, v_w_ret_up, v_w_pool_up, v_w_out, v_norm_ffn2, v_ffn2_w_in, v_ffn2_w_out, v_norm_final):
    wt = dict(norm_ffn1=norm_ffn1, ffn1_w_in=ffn1_w_in, ffn1_w_out=ffn1_w_out, norm_mix=norm_mix, w_in=w_in, gate_bias=gate_bias,
              pool_w=pool_w, pool_scale=pool_scale, w_ret_up=w_ret_up, w_pool_up=w_pool_up, w_out=w_out, norm_ffn2=norm_ffn2,
              ffn2_w_in=ffn2_w_in, ffn2_w_out=ffn2_w_out, norm_final=norm_final)
    mom = dict(norm_ffn1=m_norm_ffn1, ffn1_w_in=m_ffn1_w_in, ffn1_w_out=m_ffn1_w_out, norm_mix=m_norm_mix, w_in=m_w_in,
               gate_bias=m_gate_bias, pool_w=m_pool_w, pool_scale=m_pool_scale, w_ret_up=m_w_ret_up, w_pool_up=m_w_pool_up,
               w_out=m_w_out, norm_ffn2=m_norm_ffn2, ffn2_w_in=m_ffn2_w_in, ffn2_w_out=m_ffn2_w_out, norm_final=m_norm_final)
    var = dict(norm_ffn1=v_norm_ffn1, ffn1_w_in=v_ffn1_w_in, ffn1_w_out=v_ffn1_w_out, norm_mix=v_norm_mix, w_in=v_w_in,
               gate_bias=v_gate_bias, pool_w=v_pool_w, pool_scale=v_pool_scale, w_ret_up=v_w_ret_up, w_pool_up=v_w_pool_up,
               w_out=v_w_out, norm_ffn2=v_norm_ffn2, ffn2_w_in=v_ffn2_w_in, ffn2_w_out=v_ffn2_w_out, norm_final=v_norm_final)

    ax, ay, ac = lax.axis_index("x"), lax.axis_index("y"), lax.axis_index("c")
    chip = 2 * ax + ay
    c_arr = jnp.reshape(ac, (1,)).astype(jnp.int32)
    sc_arr = jnp.stack([chip, ac]).astype(jnp.int32)
    bias_cols = gate_bias.shape[-1]

    first = {"gate_bias": gate_bias[0], "ffn1_w_in": ffn1_w_in[0].astype(BF16)}
    gather_groups, token = _gather_start("first", [0], first)
    rest, rest_token = _gather_start("rest", [1, 2, 3, 4],
                                     {nm: wt[nm][0].astype(BF16) + token[0, 0].astype(BF16) for nm in BIG if nm not in first})
    gather_groups.update(rest)
    vec = dict(norm_ffn1=norm_ffn1, norm_mix=norm_mix, norm_ffn2=norm_ffn2, pool_scale=pool_scale,
               norm_final=norm_final.reshape(1, D_MODEL))

    relayed = {}

    def relay_w(g, after):
        relayed[g] = _gather_relay(g, gather_groups[g], (after,))

    def get_w(g, after):
        if g in relayed:
            return _gather_land(g, relayed[g], (after,))
        return _gather_finish(g, gather_groups[g], (after, rest_token) if g == 0 else (after,))

    pairs, pending = [], []

    def on_grads(gr):
        g = len(pairs)
        names = GRAD_GROUPS[g]
        assert set(names) == set(gr), (names, list(gr))
        state, token = _pair_exchange_start(g, names, {nm: _grad_view(KIND[nm], gr[nm]) for nm in names})
        pairs.append(state)
        return token[0:1, 0:1]

    def flush(after):
        g = len(pending)
        names = GRAD_GROUPS[g]
        views, from_sib = _pair_exchange_wait(g, names, pairs[g], after)
        psums = {nm: _pair_sum(f"pair_sum_{nm}", KIND[nm], views[nm], from_sib[nm], c_arr) for nm in names}
        state, token = _shard_exchange_start(g, names, psums)
        pending.append(state)
        tokens.append(token)
        return token[0:1, 0:1]

    tokens = []
    loss_local, dx, small = _local_step(x[0], loss_target[0], vec, get_w, relay_w, on_grads, flush)

    grads, delta, new_m, new_v = {}, {}, {}, {}

    def adamw(nm):
        shape = wt[nm].shape
        outs = _adamw(f"adamw_{nm}", _as2d(wt[nm]), _as2d(grads[nm]), _as2d(mom[nm]), _as2d(var[nm]), with_grad=nm in BIG)
        delta[nm], new_m[nm], new_v[nm] = (o.reshape(shape) for o in outs[:3])
        if nm in BIG:
            grads[nm] = outs[3].reshape(shape)
        return outs[0]

    def reduce_start(g, after):
        names = GRAD_GROUPS[g]
        psums, from_chips = _shard_exchange_wait(g, names, pending[g], after)
        bufs = {nm: _shard_sum(f"shard_sum_{nm}", KIND[nm], psums[nm], from_chips[nm], sc_arr) for nm in names}
        return _half_exchange_start(g, names, bufs)

    def reduce_finish(g, state, after):
        names = GRAD_GROUPS[g]
        reduced = _half_exchange_wait(g, names, state, after)
        for nm in names:
            grads[nm] = reduced[nm].reshape(wt[nm].shape)
        return tuple(adamw(nm) for nm in names)

    swap0, token = reduce_start(0, (tokens[-1],))
    swap1, token = reduce_start(1, (token,))
    done = reduce_finish(0, swap0, (token,))
    done = reduce_finish(1, swap1, done)
    swap2, token = reduce_start(2, done)
    packed = jnp.concatenate([small["norm_ffn1"], small["norm_mix"], small["gate_bias"], small["pool_scale"],
                              small["norm_ffn2"], small["norm_final"], jnp.broadcast_to(loss_local, (1, D_MODEL))], axis=0)
    small_sum, _ = _all_reduce_small("reduce_small_grads", packed + token[0, 0])
    loss = small_sum[SMALL_ROWS - 1, 0]
    for nm in ("norm_ffn1", "norm_mix", "pool_scale", "norm_ffn2"):
        grads[nm] = small_sum[SMALL_ROW[nm]][None, :]
    grads["norm_final"] = small_sum[SMALL_ROW["norm_final"]]
    grads["gate_bias"] = lax.dynamic_slice(small_sum, (SMALL_ROW["gate_bias"], chip * bias_cols), (2, bias_cols))[None]
    reduce_finish(2, swap2, (small_sum,))
    for nm in WEIGHTS:
        if nm not in delta:
            adamw(nm)

    return (loss, dx[None], *[grads[nm] for nm in WEIGHTS], *[delta[nm] for nm in WEIGHTS],
            *[new_m[nm] for nm in WEIGHTS], *[new_v[nm] for nm in WEIGHTS])
```

```python
import numpy as np
import jax
import jax.numpy as jnp
from jax import lax
from jax.experimental import pallas as pl
from jax.experimental.pallas import tpu as pltpu

F32 = jnp.float32
BF16 = jnp.bfloat16
MESH = pl.DeviceIdType.MESH

D_MODEL = 1024
D_FF = 2816
HEADS = 4
HEAD_DIM = 256
GROUPS = 4
GROUP_DIM = 256
POOL_WINDOWS = (2, 4, 8, 16)
IN_WIDTH = 7 * D_MODEL
ROPE_BASE = 10000.0
NORM_EPS = 1e-6
FFN_RES_WEIGHT = 0.5
ADAM_LR, ADAM_B1, ADAM_B2, ADAM_EPS, ADAM_WD, ADAM_STEP = 0.001, 0.9, 0.999, 1e-08, 0.01, 10

N_CHIPS = 4
RET_BLOCK = 256
V7X_VMEM_LIMIT = 48 * 1024 * 1024


def _cparams(sem):
    return pltpu.CompilerParams(dimension_semantics=sem, vmem_limit_bytes=V7X_VMEM_LIMIT)


def _sigmoid(x):
    return jax.nn.sigmoid(x)


_DIMS = {"nn": (((1,), (0,)), ((), ())), "nt": (((1,), (1,)), ((), ())), "tn": (((0,), (0,)), ((), ()))}


def _matmul(name, a, b, mode, m, n, k, tm, tn, tk, out_dtypes, extras=(), consts=(), epilogue=None, resident=None,
            n_outer=False):
    tm, tn, tk = min(tm, m), min(tn, n), min(tk, k)
    gi, gj, gk = m // tm, n // tn, k // tk
    assert gi * tm == m and gj * tn == n and gk * tk == k, (name, m, n, k, tm, tn, tk)
    once = dict(pipeline_mode=pl.Buffered(1))

    def spec(shape, index, **kw):
        return pl.BlockSpec(shape, (lambda j, i, kk: index(i, j, kk)) if n_outer else index, **kw)

    kw = once if resident == "a" else {}
    a_spec = (spec((tk, tm), lambda i, j, kk: (kk, i), **kw) if mode == "tn" else spec((tm, tk), lambda i, j, kk: (i, kk), **kw))
    kw = once if resident == "b" else {}
    b_spec = (spec((tn, tk), lambda i, j, kk: (j, kk), **kw) if mode == "nt" else spec((tk, tn), lambda i, j, kk: (kk, j), **kw))
    n_ex, n_out = len(extras) + len(consts), len(out_dtypes)
    dims = _DIMS[mode]

    def body(a_ref, b_ref, *rest):
        ex_refs, out_refs = rest[:n_ex], rest[n_ex:n_ex + n_out]

        def finish(acc):
            outs = (acc,) if epilogue is None else epilogue(acc, *[e[...] for e in ex_refs])
            for o_ref, o in zip(out_refs, outs):
                o_ref[...] = o.astype(o_ref.dtype)

        prod = lax.dot_general(a_ref[...], b_ref[...], dims, preferred_element_type=F32)
        if gk == 1:
            finish(prod)
        else:
            acc_ref = rest[n_ex + n_out]
            kk = pl.program_id(2)

            @pl.when(kk == 0)
            def _():
                acc_ref[...] = prod

            @pl.when(kk > 0)
            def _():
                acc_ref[...] += prod

            @pl.when(kk == gk - 1)
            def _():
                finish(acc_ref[...])

    o_spec = spec((tm, tn), lambda i, j, kk: (i, j))
    outs = pl.pallas_call(
        body, name=name, grid=(gj, gi, gk) if n_outer else (gi, gj, gk),
        in_specs=[a_spec, b_spec] + [o_spec] * len(extras) + [spec((1, tn), lambda i, j, kk: (0, j))] * len(consts),
        out_specs=[o_spec] * n_out,
        out_shape=[jax.ShapeDtypeStruct((m, n), dt) for dt in out_dtypes],
        scratch_shapes=[pltpu.VMEM((tm, tn), F32)] if gk > 1 else [],
        compiler_params=_cparams(("parallel", "parallel", "arbitrary")),
    )(a, b, *extras, *consts)
    return outs[0] if n_out == 1 else outs


def _row_spec(tm, width, col_block=0):
    return pl.BlockSpec((tm, width), lambda i: (i, col_block))


def _full_spec(shape):
    return pl.BlockSpec(shape, lambda *_: (0,) * len(shape))


def _rmsnorm_fwd(name, h, g, tm=512):
    t = h.shape[0]

    def body(h_ref, g_ref, o_ref):
        x = h_ref[...]
        r = lax.rsqrt(jnp.mean(x * x, axis=-1, keepdims=True) + NORM_EPS)
        o_ref[...] = (x * r * g_ref[...]).astype(BF16)

    return pl.pallas_call(
        body, name=name, grid=(t // tm,),
        in_specs=[_row_spec(tm, D_MODEL), _full_spec((1, D_MODEL))],
        out_specs=_row_spec(tm, D_MODEL),
        out_shape=jax.ShapeDtypeStruct((t, D_MODEL), BF16),
        compiler_params=_cparams(("parallel",)),
    )(h, g)


def _proj_norm_bwd(name, a_list, a_specs, parts, w, h, g, dres, tm):
    t = h.shape[0]
    na = len(a_list)

    def body(*refs):
        a_refs = refs[:na]
        w_ref, h_ref, g_ref, dres_ref, dh_ref, dhb_ref, dg_ref = refs[na:]
        i = pl.program_id(0)
        dn_v = None
        for which, lead, k0, k1 in parts:
            a_ref = a_refs[which]
            term = _dot(a_ref[...] if lead is None else a_ref[lead], w_ref[:, k0:k1], "nt")
            dn_v = term if dn_v is None else dn_v + term
        x = h_ref[...]
        r = lax.rsqrt(jnp.mean(x * x, axis=-1, keepdims=True) + NORM_EPS)
        xh = x * r
        dxh = dn_v * g_ref[...]
        dh = dres_ref[...] + r * (dxh - xh * jnp.mean(dxh * xh, axis=-1, keepdims=True))
        dh_ref[...] = dh
        dhb_ref[...] = dh.astype(BF16)
        part = jnp.sum(dn_v * xh, axis=0, keepdims=True)

        @pl.when(i == 0)
        def _():
            dg_ref[...] = part

        @pl.when(i > 0)
        def _():
            dg_ref[...] += part

    row = _row_spec(tm, D_MODEL)
    return pl.pallas_call(
        body, name=name, grid=(t // tm,),
        in_specs=list(a_specs) + [pl.BlockSpec(w.shape, lambda i: (0, 0), pipeline_mode=pl.Buffered(1)), row,
                                  _full_spec((1, D_MODEL)), row],
        out_specs=[row, row, _full_spec((1, D_MODEL))],
        out_shape=[jax.ShapeDtypeStruct((t, D_MODEL), F32), jax.ShapeDtypeStruct((t, D_MODEL), BF16),
                   jax.ShapeDtypeStruct((1, D_MODEL), F32)],
        compiler_params=_cparams(("arbitrary",)),
    )(*a_list, w, h, g, dres)


def _out_loss_and_grad(name, mid, w_out, h, g, target, tm=512):
    t = h.shape[0]

    def body(m_ref, w_ref, h_ref, g_ref, t_ref, dh_ref, dhb_ref, dg_ref, loss_ref):
        i = pl.program_id(0)
        x = h_ref[...] + FFN_RES_WEIGHT * _dot(m_ref[...], w_ref[...])
        gv = g_ref[...]
        r = lax.rsqrt(jnp.mean(x * x, axis=-1, keepdims=True) + NORM_EPS)
        xh = x * r
        err = xh * gv - t_ref[...]
        row = jnp.mean(err * err, axis=-1, keepdims=True)
        part_loss = 0.5 * jnp.sum(row, axis=0, keepdims=True)
        dy = err * (1.0 / D_MODEL)
        dxh = dy * gv
        dh = r * (dxh - xh * jnp.mean(dxh * xh, axis=-1, keepdims=True))
        dh_ref[...] = dh
        dhb_ref[...] = dh.astype(BF16)
        part = jnp.sum(dy * xh, axis=0, keepdims=True)

        @pl.when(i == 0)
        def _():
            dg_ref[...] = part
            loss_ref[...] = jnp.zeros(loss_ref.shape, F32) + part_loss

        @pl.when(i > 0)
        def _():
            dg_ref[...] += part
            loss_ref[...] += part_loss

    return pl.pallas_call(
        body, name=name, grid=(t // tm,),
        in_specs=[_row_spec(tm, D_FF), pl.BlockSpec((D_FF, D_MODEL), lambda i: (0, 0), pipeline_mode=pl.Buffered(1)),
                  _row_spec(tm, D_MODEL), _full_spec((1, D_MODEL)), _row_spec(tm, D_MODEL)],
        out_specs=[_row_spec(tm, D_MODEL), _row_spec(tm, D_MODEL), _full_spec((1, D_MODEL)), _full_spec((8, 128))],
        out_shape=[jax.ShapeDtypeStruct((t, D_MODEL), F32), jax.ShapeDtypeStruct((t, D_MODEL), BF16),
                   jax.ShapeDtypeStruct((1, D_MODEL), F32), jax.ShapeDtypeStruct((8, 128), F32)],
        compiler_params=_cparams(("arbitrary",)),
    )(mid, w_out, h, g, target)


def _rope_tables(t):
    half = HEAD_DIM // 2
    inv_freq = np.float32(ROPE_BASE) ** (-np.arange(half, dtype=np.float32) / np.float32(half))
    ang = (np.arange(t, dtype=np.float32)[:, None] * inv_freq[None, :].astype(np.float32)).astype(np.float32)
    return jnp.asarray(np.cos(ang.astype(np.float64)).astype(np.float32)), jnp.asarray(np.sin(ang.astype(np.float64)).astype(np.float32))


ROPE_HALF = HEAD_DIM // 2
K_SCALE = HEAD_DIM ** -0.5


def _rotate(ref, rows, hh, c, s, scale=None):
    lo, mid, hi = hh * HEAD_DIM, hh * HEAD_DIM + ROPE_HALF, (hh + 1) * HEAD_DIM
    x1, x2 = ref[rows, lo:mid].astype(F32), ref[rows, mid:hi].astype(F32)
    y = jnp.concatenate([x1 * c - x2 * s, x1 * s + x2 * c], axis=1)
    return y if scale is None else y * scale


def _unrotate_into(ref, rows, hh, dy, c, s, scale=None):
    lo, mid, hi = hh * HEAD_DIM, hh * HEAD_DIM + ROPE_HALF, (hh + 1) * HEAD_DIM
    y1, y2 = dy[:, :ROPE_HALF], dy[:, ROPE_HALF:]
    d1, d2 = y1 * c + y2 * s, y2 * c - y1 * s
    if scale is not None:
        d1, d2 = d1 * scale, d2 * scale
    ref[rows, lo:mid] = d1.astype(ref.dtype)
    ref[rows, mid:hi] = d2.astype(ref.dtype)


def _retention_tables():
    b, chunk = RET_BLOCK, 64
    gamma = 1.0 - 2.0 ** (-5.0 - np.arange(HEADS, dtype=np.float64))
    log_g = np.log(gamma)[:, None, None]
    i = np.arange(b)[:, None]
    j = np.arange(b)[None, :]
    same = (i // chunk) == (j // chunk)
    earlier = (j // chunk) < (i // chunk)
    expo = np.where(same, np.abs(i - j), np.where(earlier, i - j, 0)).astype(np.float64)
    mask = np.where(same | earlier, 1.0, 0.0)
    dmat = np.exp(log_g * expo[None]) * mask[None]
    qd = np.exp(log_g[:, :, 0] * (np.arange(b)[None, :] + 1.0))
    kd = np.exp(log_g[:, :, 0] * (b - 1.0 - np.arange(b)[None, :]))
    cd = np.exp(log_g[:, :, 0] * b) * np.ones((1, HEAD_DIM))
    as32 = lambda v: jnp.asarray(v.astype(np.float32))
    return (as32(dmat), as32(np.swapaxes(dmat, 1, 2)), as32(qd[:, :, None]), as32(kd[:, :, None]), as32(cd[:, None, :]))


def _dot(a, b, mode="nn"):
    return lax.dot_general(a, b, _DIMS[mode], preferred_element_type=F32)


GRET_BLOCK = 3


RET_SUBS = 2
RET_STEP = RET_SUBS * RET_BLOCK


def _head_specs(steps, rev=False):
    pos = (lambda n: steps - 1 - n) if rev else (lambda n: n)
    tok = pl.BlockSpec((RET_STEP, D_MODEL), lambda n: (pos(n), 0))
    blk = [pl.BlockSpec((RET_STEP, D_MODEL), lambda n, b=b: (pos(n), b)) for b in range(GRET_BLOCK + 1)]
    rope = pl.BlockSpec((RET_STEP, ROPE_HALF), lambda n: (pos(n), 0))
    tab = _full_spec((HEADS, RET_BLOCK, RET_BLOCK))
    col = _full_spec((HEADS, RET_BLOCK, 1))
    rowv = _full_spec((HEADS, 1, HEAD_DIM))
    st = pl.BlockSpec((HEADS, RET_SUBS, HEAD_DIM, HEAD_DIM), lambda n: (0, pos(n), 0, 0))
    return tok, blk, rope, tab, col, rowv, st


def _retention_fwd(name, proj, cos, sin, tables):
    t = proj.shape[0]
    nb, steps = t // RET_BLOCK, t // RET_STEP
    dmat, _, qd, kd, cd = tables
    tok, blk, rope, tab, col, rowv, st = _head_specs(steps)

    def body(q_ref, k_ref, v_ref, g_ref, c_ref, s_ref, d_ref, qd_ref, kd_ref, cd_ref, o_ref, ret_ref, st_ref, state):
        n = pl.program_id(0)

        @pl.when(n == 0)
        def _():
            state[...] = jnp.zeros(state.shape, F32)

        for sub in range(RET_SUBS):
            rows = slice(sub * RET_BLOCK, (sub + 1) * RET_BLOCK)
            cs, sn = c_ref[rows, :], s_ref[rows, :]
            for hh in range(HEADS):
                sl = slice(hh * HEAD_DIM, (hh + 1) * HEAD_DIM)
                q, k = _rotate(q_ref, rows, hh, cs, sn), _rotate(k_ref, rows, hh, cs, sn, K_SCALE)
                v = v_ref[rows, sl].astype(BF16)
                s = _dot(q.astype(BF16), k.astype(BF16), "nt") * d_ref[hh]
                stb = state[hh].astype(BF16)
                st_ref[hh, sub] = stb
                o = _dot(s.astype(BF16), v) + _dot((q * qd_ref[hh]).astype(BF16), stb)
                o_ref[rows, sl] = o
                rn = o * lax.rsqrt(jnp.mean(o * o, axis=-1, keepdims=True) + NORM_EPS)
                g = g_ref[rows, sl].astype(F32)
                ret_ref[rows, sl] = (rn * (g * _sigmoid(g))).astype(BF16)
                state[hh] = state[hh] * cd_ref[hh] + _dot((k * kd_ref[hh]).astype(BF16), v, "tn")

    return pl.pallas_call(
        body, name=name, grid=(steps,),
        in_specs=blk + [rope, rope, tab, col, col, rowv],
        out_specs=[tok, tok, st],
        out_shape=[jax.ShapeDtypeStruct((t, D_MODEL), F32), jax.ShapeDtypeStruct((t, D_MODEL), BF16),
                   jax.ShapeDtypeStruct((HEADS, nb, HEAD_DIM, HEAD_DIM), BF16)],
        scratch_shapes=[pltpu.VMEM((HEADS, HEAD_DIM, HEAD_DIM), F32)],
        compiler_params=_cparams(("arbitrary",)),
    )(proj, proj, proj, proj, cos, sin, dmat, qd, kd, cd)


def _retention_bwd(name, dru, w_ru, o, proj, cos, sin, states, tables):
    t = proj.shape[0]
    steps = t // RET_STEP
    dmat, dmat_t, qd, kd, cd = tables
    tok, blk, rope, tab, col, rowv, st = _head_specs(steps, rev=True)

    def body(dru_ref, wru_ref, o_ref, q_ref, k_ref, v_ref, g_ref, c_ref, s_ref, st_ref, d_ref, dt_ref, qd_ref, kd_ref, cd_ref,
             dq_ref, dk_ref, dv_ref, dg_ref, gstate):
        n = pl.program_id(0)

        @pl.when(n == 0)
        def _():
            gstate[...] = jnp.zeros(gstate.shape, F32)

        for sub in reversed(range(RET_SUBS)):
            rows = slice(sub * RET_BLOCK, (sub + 1) * RET_BLOCK)
            cs, sn = c_ref[rows, :], s_ref[rows, :]
            dret = _dot(dru_ref[rows, :], wru_ref[...], "nt")
            for hh in range(HEADS):
                sl = slice(hh * HEAD_DIM, (hh + 1) * HEAD_DIM)
                o_v, g, dr = o_ref[rows, sl], g_ref[rows, sl].astype(F32), dret[:, sl]
                sg = _sigmoid(g)
                r = lax.rsqrt(jnp.mean(o_v * o_v, axis=-1, keepdims=True) + NORM_EPS)
                rn = o_v * r
                d_rn = dr * (g * sg)
                dg_ref[rows, sl] = (dr * rn * (sg * (1.0 + g * (1.0 - sg)))).astype(BF16)
                d_o = r * (d_rn - rn * jnp.mean(d_rn * rn, axis=-1, keepdims=True))
                dob = d_o.astype(BF16)

                q, k = _rotate(q_ref, rows, hh, cs, sn), _rotate(k_ref, rows, hh, cs, sn, K_SCALE)
                v = v_ref[rows, sl].astype(BF16)
                qb, kb = q.astype(BF16), k.astype(BF16)
                qdv, kdv = qd_ref[hh], kd_ref[hh]
                s_t = (_dot(kb, qb, "nt") * dt_ref[hh]).astype(BF16)
                p_t = (_dot(v, dob, "nt") * dt_ref[hh]).astype(BF16)
                p = (_dot(dob, v, "nt") * d_ref[hh]).astype(BF16)
                stb = st_ref[hh, sub]
                gb = gstate[hh].astype(BF16)
                _unrotate_into(dq_ref, rows, hh, _dot(p, kb) + _dot(dob, stb, "nt") * qdv, cs, sn)
                _unrotate_into(dk_ref, rows, hh, _dot(p_t, qb) + _dot(v, gb, "nt") * kdv, cs, sn, K_SCALE)
                dv_ref[rows, sl] = (_dot(s_t, dob) + _dot((k * kdv).astype(BF16), gb)).astype(BF16)
                gstate[hh] = gstate[hh] * cd_ref[hh] + _dot((q * qdv).astype(BF16), dob, "tn")

    return pl.pallas_call(
        body, name=name, grid=(steps,),
        in_specs=[tok, pl.BlockSpec((D_MODEL, D_MODEL), lambda n: (0, 0), pipeline_mode=pl.Buffered(1)), tok] + blk
                 + [rope, rope, st, tab, tab, col, col, rowv],
        out_specs=[tok, tok, tok, tok],
        out_shape=[jax.ShapeDtypeStruct((t, D_MODEL), BF16)] * 4,
        scratch_shapes=[pltpu.VMEM((HEADS, HEAD_DIM, HEAD_DIM), F32)],
        compiler_params=_cparams(("arbitrary",)),
    )(dru, w_ru, o, proj, proj, proj, proj, cos, sin, states, dmat, dmat_t, qd, kd, cd)


POOL_TILE = 256
POOL_SUBS = 4
POOL_STEP = POOL_SUBS * POOL_TILE


def _pool_tables():
    b = POOL_TILE
    tt = np.arange(b)[:, None]
    jj = np.arange(b)[None, :]
    cur, prev = [], []
    for w in POOL_WINDOWS:
        cur.append(((tt - jj >= 0) & (tt - jj <= w - 1)).astype(np.float32))
        prev.append((tt - (jj - b) <= w - 1).astype(np.float32))
    cur, prev = np.stack(cur), np.stack(prev)
    as16 = lambda v: jnp.asarray(v, dtype=BF16)
    return as16(cur), as16(prev), as16(np.swapaxes(cur, 1, 2)), as16(np.swapaxes(prev, 1, 2))


def _split2(x):
    hi = x.astype(BF16)
    return hi, (x - hi.astype(F32)).astype(BF16)


POOL_BLOCK = 4


def _pool_count(n, window):
    tpos = n * POOL_TILE + lax.broadcasted_iota(jnp.int32, (POOL_TILE, 1), 0)
    return jnp.minimum(tpos + 1, window).astype(F32)


def _pool_fwd(name, proj, pool_w, scale, tables):
    t = proj.shape[0]
    steps = t // POOL_STEP
    mc, mp, _, _ = tables
    tab = _full_spec((GROUPS, POOL_TILE, POOL_TILE))
    row = _row_spec(POOL_STEP, D_MODEL)

    def body(pc_ref, pp_ref, mc_ref, mp_ref, w_ref, sc_ref, pm_ref, mix_ref, po_ref):
        n = pl.program_id(0)
        for sub in range(POOL_SUBS):
            rows = slice(sub * POOL_TILE, (sub + 1) * POOL_TILE)
            tile = n * POOL_SUBS + sub
            for g, window in enumerate(POOL_WINDOWS):
                sl = slice(g * GROUP_DIM, (g + 1) * GROUP_DIM)
                p = pc_ref[rows, sl]
                if sub == 0:
                    before = jnp.where(n > 0, _dot(mp_ref[g], pp_ref[:, sl]), 0.0)
                else:
                    before = _dot(mp_ref[g], pc_ref[(sub - 1) * POOL_TILE:sub * POOL_TILE, sl])
                pm = ((_dot(mc_ref[g], p) + before) / _pool_count(tile, window) - p.astype(F32)).astype(BF16)
                pm_ref[rows, sl] = pm
                mixed = _dot(pm, w_ref[g])
                mix_ref[rows, sl] = mixed
                po_ref[rows, sl] = (mixed * sc_ref[:, sl]).astype(BF16)

    return pl.pallas_call(
        body, name=name, grid=(steps,),
        in_specs=[_row_spec(POOL_STEP, D_MODEL, POOL_BLOCK),
                  pl.BlockSpec((POOL_TILE, D_MODEL), lambda n: (jnp.maximum(n * POOL_SUBS - 1, 0), POOL_BLOCK)),
                  tab, tab, _full_spec((GROUPS, GROUP_DIM, GROUP_DIM)), _full_spec((1, D_MODEL))],
        out_specs=[row] * 3,
        out_shape=[jax.ShapeDtypeStruct((t, D_MODEL), BF16), jax.ShapeDtypeStruct((t, D_MODEL), F32),
                   jax.ShapeDtypeStruct((t, D_MODEL), BF16)],
        compiler_params=_cparams(("parallel",)),
    )(proj, proj, mc, mp, pool_w, scale)


def _pool_bwd(name, dpu, w_pu, pm, mixed, pool_w, scale, tables):
    t = dpu.shape[0]
    steps = t // POOL_STEP
    _, _, mct, mpt = tables
    cur = pl.BlockSpec((POOL_STEP, D_MODEL), lambda n: (steps - 1 - n, 0))
    tab = _full_spec((GROUPS, POOL_TILE, POOL_TILE))
    wspec = _full_spec((GROUPS, GROUP_DIM, GROUP_DIM))
    sspec = _full_spec((1, D_MODEL))

    def body(dpu_ref, wpu_ref, pm_ref, mix_ref, mct_ref, mpt_ref, w_ref, sc_ref, dp_ref, dw_ref, ds_ref, later):
        n = pl.program_id(0)

        @pl.when(n == 0)
        def _():
            dw_ref[...] = jnp.zeros(dw_ref.shape, F32)
            ds_ref[...] = jnp.zeros(ds_ref.shape, F32)
            later[...] = jnp.zeros(later.shape, F32)

        for sub in reversed(range(POOL_SUBS)):
            rows = slice(sub * POOL_TILE, (sub + 1) * POOL_TILE)
            tile = (steps - 1 - n) * POOL_SUBS + sub
            dpo = _dot(dpu_ref[rows, :], wpu_ref[...], "nt")
            for g, window in enumerate(POOL_WINDOWS):
                sl = slice(g * GROUP_DIM, (g + 1) * GROUP_DIM)
                dc, sc = dpo[:, sl], sc_ref[:, sl]
                dmix = (dc * sc).astype(BF16)
                dpm = _dot(dmix, w_ref[g], "nt")
                e = dpm / _pool_count(tile, window)
                e_hi, e_lo = _split2(e)
                f_hi, f_lo = _split2(later[g])
                mctv, mptv = mct_ref[g], mpt_ref[g]
                back = _dot(mctv, e_hi) + _dot(mctv, e_lo)
                after = _dot(mptv, f_hi) + _dot(mptv, f_lo)
                dp_ref[rows, sl] = (back + after - dpm).astype(BF16)
                later[g] = e
                dw_ref[g] += _dot(pm_ref[rows, sl], dmix, "tn")
                ds_ref[:, sl] += jnp.sum(dc * mix_ref[rows, sl], axis=0, keepdims=True)

    return pl.pallas_call(
        body, name=name, grid=(steps,),
        in_specs=[cur, pl.BlockSpec((D_MODEL, D_MODEL), lambda n: (0, 0), pipeline_mode=pl.Buffered(1)), cur, cur, tab, tab,
                  wspec, sspec],
        out_specs=[cur, wspec, sspec],
        out_shape=[jax.ShapeDtypeStruct((t, D_MODEL), BF16), jax.ShapeDtypeStruct((GROUPS, GROUP_DIM, GROUP_DIM), F32),
                   jax.ShapeDtypeStruct((1, D_MODEL), F32)],
        scratch_shapes=[pltpu.VMEM((GROUPS, POOL_TILE, GROUP_DIM), F32)],
        compiler_params=_cparams(("arbitrary",)),
    )(dpu, w_pu, pm, mixed, mct, mpt, pool_w, scale)


GATE0_BLOCK, GATE1_BLOCK = 5, 6


def _merge_fwd(name, ret, po, w_ru, w_pu, w_out, proj, bias, h, next_g, tm=512):
    t = ret.shape[0]

    def body(r_ref, p_ref, wr_ref, wp_ref, wo_ref, g0_ref, g1_ref, b_ref, h_ref, ng_ref, m_ref, ru_ref, pu_ref, ho_ref, n_ref):
        ru = _dot(r_ref[...], wr_ref[...])
        pu = _dot(p_ref[...], wp_ref[...])
        ru_ref[...] = ru
        pu_ref[...] = pu
        merged = (_sigmoid(g0_ref[...].astype(F32) + b_ref[0:1, :]) * ru
                  + _sigmoid(g1_ref[...].astype(F32) + b_ref[1:2, :]) * pu).astype(BF16)
        m_ref[...] = merged
        h_new = h_ref[...] + _dot(merged, wo_ref[...])
        ho_ref[...] = h_new
        n_ref[...] = _normed(h_new, ng_ref[...]).astype(BF16)

    row = _row_spec(tm, D_MODEL)
    wspec = pl.BlockSpec((D_MODEL, D_MODEL), lambda i: (0, 0), pipeline_mode=pl.Buffered(1))
    return pl.pallas_call(
        body, name=name, grid=(t // tm,),
        in_specs=[row, row, wspec, wspec, wspec, _row_spec(tm, D_MODEL, GATE0_BLOCK), _row_spec(tm, D_MODEL, GATE1_BLOCK),
                  _full_spec((2, D_MODEL)), row, _full_spec((1, D_MODEL))],
        out_specs=[row] * 5,
        out_shape=[jax.ShapeDtypeStruct((t, D_MODEL), BF16), jax.ShapeDtypeStruct((t, D_MODEL), F32),
                   jax.ShapeDtypeStruct((t, D_MODEL), F32), jax.ShapeDtypeStruct((t, D_MODEL), F32),
                   jax.ShapeDtypeStruct((t, D_MODEL), BF16)],
        compiler_params=_cparams(("parallel",)),
    )(ret, po, w_ru, w_pu, w_out, proj, proj, bias, h, next_g)


def _merge_bwd(name, dh_b, w_out, ru, pu, proj, bias, tm=512):
    t = dh_b.shape[0]

    def body(dh_ref, wo_ref, ru_ref, pu_ref, g0_ref, g1_ref, b_ref, dru_ref, dpu_ref, dg0_ref, dg1_ref, db_ref):
        i = pl.program_id(0)
        d = _dot(dh_ref[...], wo_ref[...], "nt")
        s0 = _sigmoid(g0_ref[...].astype(F32) + b_ref[0:1, :])
        s1 = _sigmoid(g1_ref[...].astype(F32) + b_ref[1:2, :])
        dru_ref[...] = (d * s0).astype(BF16)
        dpu_ref[...] = (d * s1).astype(BF16)
        dg0 = d * ru_ref[...] * (s0 * (1.0 - s0))
        dg1 = d * pu_ref[...] * (s1 * (1.0 - s1))
        dg0_ref[...] = dg0.astype(BF16)
        dg1_ref[...] = dg1.astype(BF16)
        part0 = jnp.sum(dg0, axis=0, keepdims=True)
        part1 = jnp.sum(dg1, axis=0, keepdims=True)

        @pl.when(i == 0)
        def _():
            db_ref[0:1, :] = part0
            db_ref[1:2, :] = part1

        @pl.when(i > 0)
        def _():
            db_ref[0:1, :] += part0
            db_ref[1:2, :] += part1

    row = _row_spec(tm, D_MODEL)
    return pl.pallas_call(
        body, name=name, grid=(t // tm,),
        in_specs=[row, pl.BlockSpec((D_MODEL, D_MODEL), lambda i: (0, 0), pipeline_mode=pl.Buffered(1)), row, row,
                  _row_spec(tm, D_MODEL, GATE0_BLOCK), _row_spec(tm, D_MODEL, GATE1_BLOCK), _full_spec((2, D_MODEL))],
        out_specs=[row, row, row, row, _full_spec((2, D_MODEL))],
        out_shape=[jax.ShapeDtypeStruct((t, D_MODEL), BF16)] * 4 + [jax.ShapeDtypeStruct((2, D_MODEL), F32)],
        compiler_params=_cparams(("arbitrary",)),
    )(dh_b, w_out, ru, pu, proj, proj, bias)


def _half_scale(acc):
    return (FFN_RES_WEIGHT * acc,)


def _normed(h, g):
    return h * lax.rsqrt(jnp.mean(h * h, axis=-1, keepdims=True) + NORM_EPS) * g


def _residual_half_norm(acc, res, g):
    h = res + FFN_RES_WEIGHT * acc
    return h, _normed(h, g)


FF_TILE = D_FF // 2
DW_TILE = 256
SAVED_FF_DTYPE = BF16
FF_CHUNKS = ((0, 512), (512, 1024), (1024, FF_TILE))


def _ffn_in(name, nrm, w_in, tm=512):
    t = nrm.shape[0]
    nj = D_FF // FF_TILE

    def body(n_ref, wg_ref, wu_ref, a_ref, mid_ref):
        nv = n_ref[...]
        for c0, c1 in FF_CHUNKS:
            gate = _dot(nv, wg_ref[:, c0:c1])
            up = _dot(nv, wu_ref[:, c0:c1])
            s = _sigmoid(gate)
            silu = gate * s
            a_ref[0, :, c0:c1] = (FFN_RES_WEIGHT * up * (s * (1.0 + gate * (1.0 - s)))).astype(a_ref.dtype)
            a_ref[1, :, c0:c1] = (FFN_RES_WEIGHT * silu).astype(a_ref.dtype)
            mid_ref[:, c0:c1] = (silu * up).astype(BF16)

    return pl.pallas_call(
        body, name=name, grid=(nj, t // tm),
        in_specs=[pl.BlockSpec((tm, D_MODEL), lambda j, i: (i, 0)),
                  pl.BlockSpec((D_MODEL, FF_TILE), lambda j, i: (0, j)),
                  pl.BlockSpec((D_MODEL, FF_TILE), lambda j, i: (0, j + nj))],
        out_specs=[pl.BlockSpec((2, tm, FF_TILE), lambda j, i: (0, i, j)), pl.BlockSpec((tm, FF_TILE), lambda j, i: (i, j))],
        out_shape=[jax.ShapeDtypeStruct((2, t, D_FF), SAVED_FF_DTYPE), jax.ShapeDtypeStruct((t, D_FF), BF16)],
        compiler_params=_cparams(("parallel", "parallel")),
    )(nrm, w_in, w_in)


def _ffn_dact(name, dout_b, w_out, a, tm=512):
    t = dout_b.shape[0]

    def body(d_ref, w_ref, a_ref, da_ref):
        dv = d_ref[...]
        for c0, c1 in FF_CHUNKS:
            dm = _dot(dv, w_ref[c0:c1, :], "nt")
            da_ref[0, :, c0:c1] = (dm * a_ref[0, :, c0:c1].astype(F32)).astype(BF16)
            da_ref[1, :, c0:c1] = (dm * a_ref[1, :, c0:c1].astype(F32)).astype(BF16)

    blk = pl.BlockSpec((2, tm, FF_TILE), lambda j, i: (0, i, j))
    return pl.pallas_call(
        body, name=name, grid=(D_FF // FF_TILE, t // tm),
        in_specs=[pl.BlockSpec((tm, D_MODEL), lambda j, i: (i, 0)), pl.BlockSpec((FF_TILE, D_MODEL), lambda j, i: (j, 0)), blk],
        out_specs=blk,
        out_shape=jax.ShapeDtypeStruct((2, t, D_FF), BF16),
        compiler_params=_cparams(("parallel", "parallel")),
    )(dout_b, w_out, a)


def _ffn_fwd(tag, h, nrm, get_w_in, get_w_out, finish):
    t = h.shape[0]
    w_in = get_w_in(nrm)
    a, mid = _ffn_in(f"{tag}_in", nrm, w_in, tm=min(1024, t))
    w_out = get_w_out(mid)
    return finish(mid, w_out), (nrm, a, mid, w_in, w_out)


def _ffn_bwd(tag, h, g, saved, dout, dout_b, on_grads, flush):
    t = h.shape[0]
    nrm, a, mid, w_in, w_out = saved
    d_w_out = _matmul(f"{tag}_dwout", mid, dout_b, "tn", D_FF, D_MODEL, t, DW_TILE, D_MODEL, t, [BF16], epilogue=_half_scale,
                      resident="b")
    da = _ffn_dact(f"{tag}_dact", dout_b, w_out, a, tm=min(1024, t))
    nj = D_FF // DW_TILE
    d_w_in = _dw_resident(f"{tag}_dwin", nrm, [da], [pl.BlockSpec((None, t, DW_TILE), lambda s: (s // nj, 0, s % nj))],
                          2 * nj, None, DW_TILE)
    tie = on_grads({f"{tag}_w_in": d_w_in, f"{tag}_w_out": d_w_out})
    tm = min(256, t)
    dh, dh_b, dg = _proj_norm_bwd(f"{tag}_dn", [da], [pl.BlockSpec((2, tm, D_FF), lambda i: (0, i, 0))],
                                  ((0, 0, 0, D_FF), (0, 1, D_FF, 2 * D_FF)), w_in, h, g if tie is None else g + tie, dout, tm)
    return dh, dh_b, dg, flush(dh)


def _dw_resident(name, u, pieces, piece_specs, n_tiles, which_piece, tn):
    t = u.shape[0]
    npc = len(pieces)

    def body(*refs):
        u_ref, p_refs, o_ref, ut_ref = refs[0], refs[1:1 + npc], refs[1 + npc], refs[2 + npc]
        s = pl.program_id(0)

        @pl.when(s == 0)
        def _():
            ut_ref[...] = u_ref[...].T

        if npc == 1:
            o_ref[...] = _dot(ut_ref[...], p_refs[0][...]).astype(BF16)
        for which in range(npc if npc > 1 else 0):
            @pl.when(which_piece(s) == which)
            def _(which=which):
                o_ref[...] = _dot(ut_ref[...], p_refs[which][...]).astype(BF16)

    return pl.pallas_call(
        body, name=name, grid=(n_tiles,),
        in_specs=[pl.BlockSpec((t, D_MODEL), lambda s: (0, 0), pipeline_mode=pl.Buffered(1))] + list(piece_specs),
        out_specs=pl.BlockSpec((D_MODEL, tn), lambda s: (0, s)),
        out_shape=jax.ShapeDtypeStruct((D_MODEL, n_tiles * tn), BF16),
        scratch_shapes=[pltpu.VMEM((D_MODEL, t), BF16)],
        compiler_params=_cparams(("arbitrary",)),
    )(u, *pieces)


def _mix_dwin(name, u, pieces, tn=256):
    t = u.shape[0]
    nj = D_MODEL // tn
    specs = [pl.BlockSpec((t, tn), lambda s, k=k: (0, jnp.clip(s - k * nj, 0, nj - 1))) for k in range(len(pieces))]
    return _dw_resident(name, u, pieces, specs, len(pieces) * nj, lambda s: s // nj, tn)


def _local_step(x, target, vec, get_w, relay_w, on_grads, flush):
    t = x.shape[0]
    cos, sin = _rope_tables(t)
    rtab = _retention_tables()
    ptab = _pool_tables()
    w = {}

    def getter(group, name):
        def get(after):
            if name not in w:
                w.update(get_w(group, after))
            return w[name]
        return get

    nrm1 = _rmsnorm_fwd("ffn1_norm", x, vec["norm_ffn1"])
    def out_and_norm(mid, w_out):
        return _matmul("ffn1_out", mid, w_out, "nn", t, D_MODEL, D_FF, 512, D_MODEL, D_FF, [F32, BF16],
                       extras=(x,), consts=(vec["norm_mix"],), epilogue=_residual_half_norm)

    (h1, u), s1 = _ffn_fwd("ffn1", x, nrm1, getter(0, "ffn1_w_in"), getter(1, "ffn1_w_out"), out_and_norm)
    w.update(get_w(2, u))
    relay_w(3, w["w_in"])
    proj = _matmul("mix_in", u, w["w_in"], "nn", t, IN_WIDTH, D_MODEL, 2048, 1024, D_MODEL, [BF16], n_outer=True)
    w.update(get_w(3, proj))
    o, ret, states = _retention_fwd("retention", proj, cos, sin, rtab)
    pm, mixed, po = _pool_fwd("pool", proj, w["pool_w"], vec["pool_scale"], ptab)
    relay_w(4, po)
    merged, ru, pu, h2, nrm2 = _merge_fwd("merge", ret, po, w["w_ret_up"], w["w_pool_up"], w["w_out"], proj, w["gate_bias"],
                                          h1, vec["norm_ffn2"], tm=min(512, t))
    def out_and_loss(mid, w_out):
        return _out_loss_and_grad("ffn2_out_loss", mid, w_out, h2, vec["norm_final"], target, tm=min(512, t))

    (dh3, dh3_b, dg_final, loss), s2 = _ffn_fwd("ffn2", h2, nrm2, getter(4, "ffn2_w_in"), getter(4, "ffn2_w_out"), out_and_loss)

    def tied(v, tie):
        return v if tie is None else v + tie

    dh2, dh2_b, dg_ffn2, tie = _ffn_bwd("ffn2", h2, vec["norm_ffn2"], s2, dh3, dh3_b, on_grads, flush)
    def square_dw(name, act, grad):
        return _matmul(name, act, grad, "tn", D_MODEL, D_MODEL, t, D_MODEL, D_MODEL, 1024, [BF16])

    d_w_out = square_dw("mix_dwout", merged, dh2_b)
    dru, dpu, dg0, dg1, d_bias = _merge_bwd("merge_bwd", dh2_b, w["w_out"], ru, pu, proj, tied(w["gate_bias"], tie))
    d_w_ru = square_dw("mix_dwru", ret, dru)
    d_w_pu = square_dw("mix_dwpu", po, dpu)
    dp, d_pool_w, d_scale = _pool_bwd("pool_bwd", dpu, w["w_pool_up"], pm, mixed, w["pool_w"], vec["pool_scale"], ptab)
    dq, dk, dv, dgr = _retention_bwd("retention_bwd", dru, w["w_ret_up"], o, proj, cos, sin, states, rtab)
    dproj = [dq, dk, dv, dgr, dp, dg0, dg1]
    d_w_in = _mix_dwin("mix_dwin", u, dproj)
    tie = on_grads(dict(w_in=d_w_in, pool_w=d_pool_w.astype(BF16), w_ret_up=d_w_ru, w_pool_up=d_w_pu, w_out=d_w_out))
    tm = min(256, t)
    dh1, dh1_b, dg_mix = _proj_norm_bwd("mix_du", dproj, [_row_spec(tm, D_MODEL)] * len(dproj),
                                        [(k, None, k * D_MODEL, (k + 1) * D_MODEL) for k in range(len(dproj))],
                                        w["w_in"], h1, tied(vec["norm_mix"], tie), dh2, tm)
    tie = flush(dh1)
    dx, _, dg_ffn1, _ = _ffn_bwd("ffn1", x, tied(vec["norm_ffn1"], tie), s1, dh1, dh1_b, on_grads, flush)

    small = dict(norm_ffn1=dg_ffn1, norm_mix=dg_mix, gate_bias=d_bias, pool_scale=d_scale, norm_ffn2=dg_ffn2,
                 norm_final=dg_final)
    return loss[0, 0], dx, small


BIG = ("ffn1_w_in", "ffn1_w_out", "w_in", "pool_w", "w_ret_up", "w_pool_up", "w_out", "ffn2_w_in", "ffn2_w_out")
KIND = dict(ffn1_w_in="col", ffn1_w_out="row", w_in="col", pool_w="pool", w_ret_up="row", w_pool_up="row", w_out="row",
            ffn2_w_in="col", ffn2_w_out="row", gate_bias="col")
ANY = pl.BlockSpec(memory_space=pl.ANY)


def _place():
    x, y, c = lax.axis_index("x"), lax.axis_index("y"), lax.axis_index("c")
    chips = [(1 - x, y), (x, 1 - y), (1 - x, 1 - y)]
    return x, y, c, chips


def _full_view_shape(kind, local_shape):
    if kind == "col":
        return (2, local_shape[0] // 2, N_CHIPS * local_shape[1])
    if kind == "row":
        return (N_CHIPS, 2, local_shape[0] // 2, local_shape[1])
    return (GROUPS, N_CHIPS, 2, local_shape[1] // 2, local_shape[2])


def _local_view(kind, arr):
    if kind == "pool":
        return arr.reshape(GROUPS, 2, arr.shape[1] // 2, arr.shape[2])
    return arr.reshape(2, arr.shape[0] // 2, arr.shape[1])


def _blk(kind, ref, s, c):
    if kind == "col":
        cs = ref.shape[2] // N_CHIPS
        return ref.at[c, :, pl.ds(pl.multiple_of(s * cs, 128), cs)]
    if kind == "row":
        return ref.at[s, c]
    return ref.at[:, s, c]


def _half(kind, ref, c):
    return ref.at[:, c] if kind == "pool" else ref.at[c]


def _shard(kind, ref, s):
    if kind == "col":
        cs = ref.shape[2] // N_CHIPS
        return ref.at[:, :, pl.ds(pl.multiple_of(s * cs, 128), cs)]
    if kind == "row":
        return ref.at[s]
    return ref.at[:, s]


HBM = pl.BlockSpec(memory_space=pltpu.HBM)
SEM = pl.BlockSpec(memory_space=pltpu.SEMAPHORE)
EFFECT = pltpu.SideEffectType.DATAFLOW_SIDE_EFFECTING
WEIGHT_GROUPS = (("gate_bias", "ffn1_w_in"), ("ffn1_w_out",), ("w_in",), ("pool_w", "w_ret_up", "w_pool_up", "w_out"),
                 ("ffn2_w_in", "ffn2_w_out"))
GRAD_GROUPS = (("ffn2_w_in", "ffn2_w_out"), ("w_in", "pool_w", "w_ret_up", "w_pool_up", "w_out"), ("ffn1_w_in", "ffn1_w_out"))


def _hbm(a):
    return pltpu.with_memory_space_constraint(a, pltpu.HBM)


def _natural(kind, o):
    if kind == "col":
        return o.reshape(o.shape[0] * o.shape[1], o.shape[2])
    if kind == "row":
        return o.reshape(-1, o.shape[3])
    return o.reshape(GROUPS, -1, o.shape[4])


def _ici_copy(kind, loc, full, j, chips, s, c, send_sem, recv_sem):
    px, py = chips[j]
    return (pltpu.make_async_remote_copy(src_ref=_half(kind, loc, c), dst_ref=_blk(kind, full, s, c), send_sem=send_sem,
                                         recv_sem=recv_sem, device_id=(px, py, c), device_id_type=MESH),
            pltpu.make_async_remote_copy(src_ref=_half(kind, loc, c), dst_ref=_blk(kind, full, 2 * px + py, c), send_sem=send_sem,
                                         recv_sem=recv_sem, device_id=(px, py, c), device_id_type=MESH))


def _first_leg(kind, loc, full, a, place, send_sems, recv_sems):
    x, y, c, chips = place
    s = 2 * x + y
    pairs = [_ici_copy(kind, loc, full, j, chips, s, c, send_sems.at[4 * a + j], recv_sems.at[4 * a + j]) for j in range(3)]
    own = pltpu.make_async_remote_copy(src_ref=loc, dst_ref=_shard(kind, full, s), send_sem=send_sems.at[4 * a + 3],
                                       recv_sem=recv_sems.at[4 * a + 3], device_id=(x, y, 1 - c), device_id_type=MESH)
    return pairs + [(own, own)]


def _gather_start(tag, group_ids, shards):
    grps = [WEIGHT_GROUPS[g] for g in group_ids]
    names = [nm for grp in grps for nm in grp]
    kinds = [KIND[nm] for nm in names]
    n, ng = len(names), len(grps)
    locs = [_hbm(_local_view(KIND[nm], shards[nm])) for nm in names]
    lands = [_hbm(lax.empty(_full_view_shape(KIND[nm], shards[nm].shape), shards[nm].dtype)) for nm in names]
    first = np.cumsum([0] + [len(grp) for grp in grps])

    def body(*refs):
        loc, full = refs[:n], refs[n:2 * n]
        send_sems, recv_sems = refs[2 * n:2 * n + ng], refs[2 * n + ng:2 * n + 2 * ng]
        token = refs[-1]
        place = _place()
        for g in range(ng):
            for a in range(first[g], first[g + 1]):
                for cp in _first_leg(kinds[a], loc[a], full[a], a - first[g], place, send_sems[g], recv_sems[g]):
                    cp[0].start()
        token[...] = jnp.zeros(token.shape, F32)

    sem_shapes = [pltpu.SemaphoreType.DMA((4 * len(grp),)) for grp in grps]
    outs = pl.pallas_call(
        body, name=f"gather_start_{tag}",
        in_specs=[HBM] * (2 * n),
        out_specs=[SEM] * (2 * ng) + [HBM] * (2 * n) + [pl.BlockSpec(memory_space=pltpu.VMEM)],
        out_shape=sem_shapes + sem_shapes + [pltpu.HBM(a.shape, a.dtype) for a in locs + lands] + [jax.ShapeDtypeStruct((8, 128), F32)],
        input_output_aliases={i: 2 * ng + i for i in range(2 * n)},
        compiler_params=pltpu.CompilerParams(has_side_effects=EFFECT),
    )(*locs, *lands)
    send_sems, recv_sems = outs[:ng], outs[ng:2 * ng]
    locs_t, lands_t = outs[2 * ng:2 * ng + n], outs[2 * ng + n:2 * ng + 2 * n]
    groups = {}
    for k, g in enumerate(group_ids):
        sl = slice(first[k], first[k + 1])
        groups[g] = (send_sems[k], recv_sems[k], list(locs_t[sl]), list(lands_t[sl]))
    return groups, outs[-1]


def _forward_copies(kinds, loc, full, send_sems, recv_sems):
    x, y, c, chips = _place()

    def remote(a, k, part):
        return pltpu.make_async_remote_copy(src_ref=part, dst_ref=part, send_sem=send_sems.at[3 * a + k],
                                            recv_sem=recv_sems.at[3 * a + k], device_id=(x, y, 1 - c), device_id_type=MESH)

    sends, arrivals = [], []
    for a, kind in enumerate(kinds):
        for j, (px, py) in enumerate(chips):
            sends.append(remote(a, j, _blk(kind, full[a], 2 * px + py, c)))
            arrivals.append(remote(a, j, _blk(kind, full[a], 2 * px + py, 1 - c)))
    return sends, arrivals


def _gather_relay(g, group, after):
    names = WEIGHT_GROUPS[g]
    kinds = [KIND[nm] for nm in names]
    m = len(names)
    ici_send, ici_recv, locs, lands = group

    def body(*refs):
        loc, full = refs[:m], refs[m:2 * m]
        ici_s, ici_r = refs[2 * m], refs[2 * m + 1]
        d2d_s, d2d_r = refs[2 * m + 2 + len(after)], refs[2 * m + 3 + len(after)]
        place = _place()
        for a in range(m):
            for sent, landed in _first_leg(kinds[a], loc[a], full[a], a, place, ici_s, ici_r):
                sent.wait_send()
                landed.wait_recv()
        for cp in _forward_copies(kinds, loc, full, d2d_s, d2d_r)[0]:
            cp.start()

    sem_shape = pltpu.SemaphoreType.DMA((3 * m,))
    outs = pl.pallas_call(
        body, name=f"gather_relay_{g}",
        in_specs=[HBM] * (2 * m) + [SEM, SEM] + [ANY] * len(after), out_specs=[SEM, SEM] + [HBM] * (2 * m),
        out_shape=[sem_shape, sem_shape] + [pltpu.HBM(a.shape, a.dtype) for a in locs + lands],
        input_output_aliases={i: 2 + i for i in range(2 * m)},
        compiler_params=pltpu.CompilerParams(has_side_effects=EFFECT),
    )(*locs, *lands, ici_send, ici_recv, *after)
    return outs[0], outs[1], list(outs[2:2 + m]), list(outs[2 + m:2 + 2 * m])


def _gather_land(g, state, after):
    names = WEIGHT_GROUPS[g]
    kinds = [KIND[nm] for nm in names]
    m = len(names)
    d2d_send, d2d_recv, locs, lands = state

    def body(*refs):
        sends, arrivals = _forward_copies(kinds, refs[:m], refs[m:2 * m], refs[2 * m], refs[2 * m + 1])
        for cp in sends:
            cp.wait_send()
        for cp in arrivals:
            cp.wait_recv()

    outs = pl.pallas_call(
        body, name=f"gather_land_{g}",
        in_specs=[HBM] * (2 * m) + [SEM, SEM] + [ANY] * len(after), out_specs=[HBM] * (2 * m),
        out_shape=[pltpu.HBM(a.shape, a.dtype) for a in locs + lands],
        input_output_aliases={i: i for i in range(2 * m)},
        compiler_params=pltpu.CompilerParams(has_side_effects=EFFECT),
    )(*locs, *lands, d2d_send, d2d_recv, *after)
    return {nm: _natural(k, o) for nm, k, o in zip(names, kinds, outs[m:])}


def _gather_finish(g, group, after):
    names = WEIGHT_GROUPS[g]
    kinds = [KIND[nm] for nm in names]
    m = len(names)
    send_sem, recv_sem, locs, lands = group

    def wait_body(*refs):
        loc, full = refs[:m], refs[m:2 * m]
        send_sems, recv_sems = refs[2 * m], refs[2 * m + 1]
        place = _place()
        for a in range(m):
            for sent, landed in _first_leg(kinds[a], loc[a], full[a], a, place, send_sems, recv_sems):
                sent.wait_send()
                landed.wait_recv()

    outs = pl.pallas_call(
        wait_body, name=f"gather_wait_{g}",
        in_specs=[HBM] * (2 * m) + [SEM, SEM] + [ANY] * len(after), out_specs=[HBM] * (2 * m),
        out_shape=[pltpu.HBM(a.shape, a.dtype) for a in locs + lands],
        input_output_aliases={i: i for i in range(2 * m)},
        compiler_params=pltpu.CompilerParams(has_side_effects=EFFECT),
    )(*locs, *lands, send_sem, recv_sem, *after)
    locs, lands = outs[:m], outs[m:]

    def forward_body(*refs):
        sends, arrivals = _forward_copies(kinds, refs[:m], refs[2 * m:3 * m], *refs[3 * m:])
        for cp in sends:
            cp.start()
        for cp in arrivals:
            cp.wait_recv()
        for cp in sends:
            cp.wait_send()

    outs = pl.pallas_call(
        forward_body, name=f"gather_forward_{g}",
        in_specs=[ANY] * (2 * m), out_specs=[ANY] * m,
        out_shape=[jax.ShapeDtypeStruct(a.shape, a.dtype) for a in lands],
        input_output_aliases={m + i: i for i in range(m)},
        scratch_shapes=[pltpu.SemaphoreType.DMA((3 * m,)), pltpu.SemaphoreType.DMA((3 * m,))],
    )(*locs, *lands)
    return {nm: _natural(k, o) for nm, k, o in zip(names, kinds, outs)}


def _grad_view(kind, g):
    if kind == "col":
        return g.reshape(2, g.shape[0] // 2, g.shape[1])
    if kind == "row":
        return g.reshape(N_CHIPS, 2, g.shape[0] // (2 * N_CHIPS), g.shape[1])
    return g.reshape(GROUPS, N_CHIPS, 2, g.shape[1] // (2 * N_CHIPS), g.shape[2])


def _pair_copies(kinds, g, got, send_sems, recv_sems):
    x, y, c, _ = _place()

    def other_half(kind, ref):
        if kind == "col":
            return ref.at[1 - c]
        if kind == "row":
            return ref.at[:, 1 - c]
        return ref.at[:, :, 1 - c]

    return [pltpu.make_async_remote_copy(src_ref=other_half(kinds[a], g[a]), dst_ref=got[a], send_sem=send_sems.at[a],
                                         recv_sem=recv_sems.at[a], device_id=(x, y, 1 - c), device_id_type=MESH)
            for a in range(len(kinds))]


def _pair_exchange_start(tag, names, views):
    kinds = [KIND[nm] for nm in names]
    n = len(names)

    def got_shape(kind, v):
        if kind == "col":
            return v.shape[1:]
        if kind == "row":
            return (v.shape[0],) + v.shape[2:]
        return v.shape[:2] + v.shape[3:]

    srcs = [_hbm(views[nm]) for nm in names]
    lands = [_hbm(lax.empty(got_shape(k, views[nm]), BF16)) for nm, k in zip(names, kinds)]

    def body(*refs):
        g, got = refs[:n], refs[n:2 * n]
        for cp in _pair_copies(kinds, g, got, refs[2 * n], refs[2 * n + 1]):
            cp.start()
        refs[-1][...] = jnp.zeros(refs[-1].shape, F32)

    sem_shape = pltpu.SemaphoreType.DMA((n,))
    outs = pl.pallas_call(
        body, name=f"grad_pair_exchange_start_{tag}",
        in_specs=[HBM] * (2 * n),
        out_specs=[SEM, SEM] + [HBM] * (2 * n) + [pl.BlockSpec(memory_space=pltpu.VMEM)],
        out_shape=[sem_shape, sem_shape] + [pltpu.HBM(a.shape, a.dtype) for a in srcs + lands] + [jax.ShapeDtypeStruct((8, 128), F32)],
        input_output_aliases={i: 2 + i for i in range(2 * n)},
        compiler_params=pltpu.CompilerParams(has_side_effects=EFFECT),
    )(*srcs, *lands)
    return (outs[0], outs[1], list(outs[2:2 + n]), list(outs[2 + n:2 + 2 * n])), outs[-1]


def _pair_exchange_wait(tag, names, state, after):
    kinds = [KIND[nm] for nm in names]
    n = len(names)
    send_sem, recv_sem, srcs, lands = state

    def body(*refs):
        g, got = refs[:n], refs[n:2 * n]
        for cp in _pair_copies(kinds, g, got, refs[2 * n], refs[2 * n + 1]):
            cp.wait_send()
            cp.wait_recv()

    outs = pl.pallas_call(
        body, name=f"grad_pair_exchange_wait_{tag}",
        in_specs=[HBM] * (2 * n) + [SEM, SEM, ANY], out_specs=[HBM] * (2 * n),
        out_shape=[pltpu.HBM(a.shape, a.dtype) for a in srcs + lands],
        input_output_aliases={i: i for i in range(2 * n)},
        compiler_params=pltpu.CompilerParams(has_side_effects=EFFECT),
    )(*srcs, *lands, send_sem, recv_sem, after)
    return dict(zip(names, outs[:n])), dict(zip(names, outs[n:]))


def _pair_sum(name, kind, view, got, c_arr):
    if kind == "col":
        _, rows, cols = view.shape
        tr = 128
        grid = (rows // tr,)
        v_spec = pl.BlockSpec((None, tr, cols), lambda i, c: (c[0], i, 0))
        g_spec = pl.BlockSpec((tr, cols), lambda i, c: (i, 0))
    elif kind == "row":
        _, _, rows, cols = view.shape
        grid = (N_CHIPS,)
        v_spec = pl.BlockSpec((None, None, rows, cols), lambda i, c: (i, c[0], 0, 0))
        g_spec = pl.BlockSpec((None, rows, cols), lambda i, c: (i, 0, 0))
    else:
        _, _, _, rows, cols = view.shape
        grid = (GROUPS,)
        v_spec = pl.BlockSpec((None, N_CHIPS, None, rows, cols), lambda i, c: (i, 0, c[0], 0, 0))
        g_spec = pl.BlockSpec((None, N_CHIPS, rows, cols), lambda i, c: (i, 0, 0, 0))

    def body(c_ref, v_ref, g_ref, o_ref):
        o_ref[...] = (v_ref[...].astype(F32) + g_ref[...].astype(F32)).astype(BF16)

    return pl.pallas_call(
        body, name=name,
        grid_spec=pltpu.PrefetchScalarGridSpec(num_scalar_prefetch=1, grid=grid, in_specs=[v_spec, g_spec], out_specs=g_spec),
        out_shape=jax.ShapeDtypeStruct(got.shape, BF16),
        compiler_params=_cparams(("parallel",)),
    )(c_arr, view, got)


def _piece(kind, ref, s):
    if kind == "col":
        cs = ref.shape[1] // N_CHIPS
        return ref.at[:, pl.ds(pl.multiple_of(s * cs, 128), cs)]
    if kind == "row":
        return ref.at[s]
    return ref.at[:, s]


def _piece_shape(kind, shape):
    if kind == "col":
        return (shape[0], shape[1] // N_CHIPS)
    if kind == "row":
        return shape[1:]
    return (shape[0],) + shape[2:]


def _shard_copies(kinds, p, got, send_sems, recv_sems):
    x, y, c, chips = _place()
    return [pltpu.make_async_remote_copy(src_ref=_piece(kinds[a], p[a], 2 * px + py), dst_ref=got[a].at[j],
                                         send_sem=send_sems.at[3 * a + j], recv_sem=recv_sems.at[3 * a + j],
                                         device_id=(px, py, c), device_id_type=MESH)
            for a in range(len(kinds)) for j, (px, py) in enumerate(chips)]


def _shard_exchange_start(g, names, psums):
    kinds = [KIND[nm] for nm in names]
    n = len(names)
    srcs = [_hbm(psums[nm]) for nm in names]
    lands = [_hbm(lax.empty((3,) + _piece_shape(k, psums[nm].shape), BF16)) for nm, k in zip(names, kinds)]

    def body(*refs):
        p, got = refs[:n], refs[n:2 * n]
        send_sems, recv_sems = refs[2 * n], refs[2 * n + 1]
        token = refs[-1]
        for cp in _shard_copies(kinds, p, got, send_sems, recv_sems):
            cp.start()
        token[...] = jnp.zeros(token.shape, F32)

    sem_shape = pltpu.SemaphoreType.DMA((3 * n,))
    outs = pl.pallas_call(
        body, name=f"grad_shard_exchange_start_{g}",
        in_specs=[HBM] * (2 * n),
        out_specs=[SEM, SEM] + [HBM] * (2 * n) + [pl.BlockSpec(memory_space=pltpu.VMEM)],
        out_shape=[sem_shape, sem_shape] + [pltpu.HBM(a.shape, a.dtype) for a in srcs + lands] + [jax.ShapeDtypeStruct((8, 128), F32)],
        input_output_aliases={i: 2 + i for i in range(2 * n)},
        compiler_params=pltpu.CompilerParams(has_side_effects=EFFECT),
    )(*srcs, *lands)
    return (outs[0], outs[1], list(outs[2:2 + n]), list(outs[2 + n:2 + 2 * n])), outs[-1]


def _shard_exchange_wait(g, names, state, after):
    kinds = [KIND[nm] for nm in names]
    n = len(names)
    send_sem, recv_sem, srcs, lands = state

    def body(*refs):
        p, got = refs[:n], refs[n:2 * n]
        for cp in _shard_copies(kinds, p, got, refs[2 * n], refs[2 * n + 1]):
            cp.wait_send()
            cp.wait_recv()

    outs = pl.pallas_call(
        body, name=f"grad_shard_exchange_wait_{g}",
        in_specs=[HBM] * (2 * n) + [SEM, SEM] + [ANY] * len(after), out_specs=[HBM] * (2 * n),
        out_shape=[pltpu.HBM(a.shape, a.dtype) for a in srcs + lands],
        input_output_aliases={i: i for i in range(2 * n)},
        compiler_params=pltpu.CompilerParams(has_side_effects=EFFECT),
    )(*srcs, *lands, send_sem, recv_sem, *after)
    return dict(zip(names, outs[:n])), dict(zip(names, outs[n:]))


def _shard_sum(name, kind, psum, got, sc_arr):
    if kind == "col":
        rows, cols = psum.shape
        cs = cols // N_CHIPS
        tr = 128
        grid = (rows // tr,)
        p_spec = pl.BlockSpec((tr, cs), lambda i, sc: (i, sc[0]))
        g_spec = pl.BlockSpec((3, tr, cs), lambda i, sc: (0, i, 0))
        o_spec = pl.BlockSpec((None, tr, cs), lambda i, sc: (sc[1], i, 0))
        out_shape = (2, rows, cs)
    elif kind == "row":
        _, rows, cols = psum.shape
        grid = (1,)
        p_spec = pl.BlockSpec((None, rows, cols), lambda i, sc: (sc[0], 0, 0))
        g_spec = pl.BlockSpec((3, rows, cols), lambda i, sc: (0, 0, 0))
        o_spec = pl.BlockSpec((None, rows, cols), lambda i, sc: (sc[1], 0, 0))
        out_shape = (2, rows, cols)
    else:
        _, _, rows, cols = psum.shape
        grid = (1,)
        p_spec = pl.BlockSpec((GROUPS, None, rows, cols), lambda i, sc: (0, sc[0], 0, 0))
        g_spec = pl.BlockSpec((3, GROUPS, rows, cols), lambda i, sc: (0, 0, 0, 0))
        o_spec = pl.BlockSpec((GROUPS, None, rows, cols), lambda i, sc: (0, sc[1], 0, 0))
        out_shape = (GROUPS, 2, rows, cols)

    def body(sc_ref, p_ref, g_ref, o_ref):
        o_ref[...] = ((p_ref[...].astype(F32) + g_ref[0].astype(F32)) + g_ref[1].astype(F32)) + g_ref[2].astype(F32)

    return pl.pallas_call(
        body, name=name,
        grid_spec=pltpu.PrefetchScalarGridSpec(num_scalar_prefetch=1, grid=grid, in_specs=[p_spec, g_spec], out_specs=o_spec),
        out_shape=jax.ShapeDtypeStruct(out_shape, F32),
        compiler_params=_cparams(("parallel",)),
    )(sc_arr, psum, got)


def _half_copies(kinds, bufs, send_sems, recv_sems):
    x, y, c, _ = _place()

    def remote(a, half):
        part = _half(kinds[a], bufs[a], half)
        return pltpu.make_async_remote_copy(src_ref=part, dst_ref=part, send_sem=send_sems.at[a], recv_sem=recv_sems.at[a],
                                            device_id=(x, y, 1 - c), device_id_type=MESH)

    return [remote(a, c) for a in range(len(kinds))], [remote(a, 1 - c) for a in range(len(kinds))]


def _half_exchange_start(tag, names, bufs):
    kinds = [KIND[nm] for nm in names]
    n = len(names)
    arrs = [_hbm(bufs[nm]) for nm in names]

    def body(*refs):
        for cp in _half_copies(kinds, refs[:n], refs[n], refs[n + 1])[0]:
            cp.start()
        refs[-1][...] = jnp.zeros(refs[-1].shape, F32)

    sem_shape = pltpu.SemaphoreType.DMA((n,))
    outs = pl.pallas_call(
        body, name=f"grad_half_exchange_start_{tag}",
        in_specs=[HBM] * n,
        out_specs=[SEM, SEM] + [HBM] * n + [pl.BlockSpec(memory_space=pltpu.VMEM)],
        out_shape=[sem_shape, sem_shape] + [pltpu.HBM(a.shape, a.dtype) for a in arrs] + [jax.ShapeDtypeStruct((8, 128), F32)],
        input_output_aliases={i: 2 + i for i in range(n)},
        compiler_params=pltpu.CompilerParams(has_side_effects=EFFECT),
    )(*arrs)
    return (outs[0], outs[1], list(outs[2:2 + n])), outs[-1]


def _half_exchange_wait(tag, names, state, after):
    kinds = [KIND[nm] for nm in names]
    n = len(names)
    send_sem, recv_sem, arrs = state

    def body(*refs):
        sends, arrivals = _half_copies(kinds, refs[:n], refs[n], refs[n + 1])
        for cp in sends:
            cp.wait_send()
        for cp in arrivals:
            cp.wait_recv()

    outs = pl.pallas_call(
        body, name=f"grad_half_exchange_wait_{tag}",
        in_specs=[HBM] * n + [SEM, SEM] + [ANY] * len(after), out_specs=[HBM] * n,
        out_shape=[pltpu.HBM(a.shape, a.dtype) for a in arrs],
        input_output_aliases={i: i for i in range(n)},
        compiler_params=pltpu.CompilerParams(has_side_effects=EFFECT),
    )(*arrs, send_sem, recv_sem, *after)
    return dict(zip(names, outs))


N_DEV = 8
SMALL_ROWS = 8


def _all_reduce_small(name, v):
    def body(v_ref, o_ref, token, buf, send_sems, recv_sems):
        token[...] = jnp.zeros(token.shape, F32)
        x, y, c, _ = _place()
        me = 4 * x + 2 * y + c
        buf[me] = v_ref[...]
        cps = []
        for r in range(1, N_DEV):
            to = (x ^ (r >> 2), y ^ ((r >> 1) & 1), c ^ (r & 1))
            cp = pltpu.make_async_remote_copy(src_ref=v_ref, dst_ref=buf.at[me], send_sem=send_sems.at[r - 1],
                                              recv_sem=recv_sems.at[r - 1], device_id=to, device_id_type=MESH)
            cp.start()
            cps.append(cp)
        for r in range(1, N_DEV):
            pltpu.make_async_remote_copy(src_ref=v_ref, dst_ref=buf.at[me ^ r], send_sem=send_sems.at[r - 1],
                                         recv_sem=recv_sems.at[r - 1], device_id=(x, y, c), device_id_type=MESH).wait_recv()
        for cp in cps:
            cp.wait_send()
        acc = buf[0]
        for d in range(1, N_DEV):
            acc = acc + buf[d]
        o_ref[...] = acc

    vm = pl.BlockSpec(memory_space=pltpu.VMEM)
    return pl.pallas_call(
        body, name=name, in_specs=[vm], out_specs=[vm, vm],
        out_shape=[jax.ShapeDtypeStruct((SMALL_ROWS, D_MODEL), F32), jax.ShapeDtypeStruct((8, 128), F32)],
        scratch_shapes=[pltpu.VMEM((N_DEV, SMALL_ROWS, D_MODEL), F32), pltpu.SemaphoreType.DMA((N_DEV - 1,)),
                        pltpu.SemaphoreType.DMA((N_DEV - 1,))],
    )(v)


def _adamw(name, w, g, m, v, with_grad=False):
    rows, cols = w.shape
    tr = next((c for c in (256, 176, 128, 64, 32, 8) if rows % c == 0), rows)
    spec = pl.BlockSpec((tr, cols), lambda i: (i, 0))

    def body(w_ref, g_ref, m_ref, v_ref, d_ref, mo_ref, vo_ref, *go_ref):
        gv = g_ref[...]
        if with_grad:
            go_ref[0][...] = gv
        m_new = ADAM_B1 * m_ref[...] + (1.0 - ADAM_B1) * gv
        v_new = ADAM_B2 * v_ref[...] + (1.0 - ADAM_B2) * jnp.square(gv)
        m_hat = m_new / (1.0 - ADAM_B1 ** ADAM_STEP)
        v_hat = v_new / (1.0 - ADAM_B2 ** ADAM_STEP)
        d_ref[...] = -ADAM_LR * (m_hat / (jnp.sqrt(v_hat) + ADAM_EPS) + ADAM_WD * w_ref[...])
        mo_ref[...] = m_new
        vo_ref[...] = v_new

    return pl.pallas_call(
        body, name=name, grid=(rows // tr,),
        in_specs=[spec] * 4, out_specs=[spec] * (4 if with_grad else 3),
        out_shape=[jax.ShapeDtypeStruct((rows, cols), F32)] * (4 if with_grad else 3),
        compiler_params=_cparams(("parallel",)),
    )(w, g, m, v)


WEIGHTS = ("norm_ffn1", "ffn1_w_in", "ffn1_w_out", "norm_mix", "w_in", "gate_bias", "pool_w", "pool_scale", "w_ret_up",
           "w_pool_up", "w_out", "norm_ffn2", "ffn2_w_in", "ffn2_w_out", "norm_final")
SMALL_ROW = dict(norm_ffn1=0, norm_mix=1, gate_bias=2, pool_scale=4, norm_ffn2=5, norm_final=6)


def _as2d(a):
    return a.reshape(-1, a.shape[-1])


def kernel(x, norm_ffn1, ffn1_w_in, ffn1_w_out, norm_mix, w_in, gate_bias, pool_w, pool_scale, w_ret_up, w_pool_up, w_out, norm_ffn2, ffn2_w_in, ffn2_w_out, norm_final, loss_target, m_norm_ffn1, m_ffn1_w_in, m_ffn1_w_out, m_norm_mix, m_w_in, m_gate_bias, m_pool_w, m_pool_scale, m_w_ret_up, m_w_pool_up, m_w_out, m_norm_ffn2, m_ffn2_w_in, m_ffn2_w_out, m_norm_final, v_norm_ffn1, v_ffn1_w_in, v_ffn1_w_out, v_norm_mix, v_w_in, v_gate_bias, v_pool_w, v_pool_scale, v_w_ret_up, v_w_pool_up, v_w_out, v_norm_ffn2, v_ffn2_w_in, v_ffn2_w_out, v_norm_final):
    wt = dict(norm_ffn1=norm_ffn1, ffn1_w_in=ffn1_w_in, ffn1_w_out=ffn1_w_out, norm_mix=norm_mix, w_in=w_in, gate_bias=gate_bias,
              pool_w=pool_w, pool_scale=pool_scale, w_ret_up=w_ret_up, w_pool_up=w_pool_up, w_out=w_out, norm_ffn2=norm_ffn2,
              ffn2_w_in=ffn2_w_in, ffn2_w_out=ffn2_w_out, norm_final=norm_final)
    mom = dict(norm_ffn1=m_norm_ffn1, ffn1_w_in=m_ffn1_w_in, ffn1_w_out=m_ffn1_w_out, norm_mix=m_norm_mix, w_in=m_w_in,
               gate_bias=m_gate_bias, pool_w=m_pool_w, pool_scale=m_pool_scale, w_ret_up=m_w_ret_up, w_pool_up=m_w_pool_up,
               w_out=m_w_out, norm_ffn2=m_norm_ffn2, ffn2_w_in=m_ffn2_w_in, ffn2_w_out=m_ffn2_w_out, norm_final=m_norm_final)
    var = dict(norm_ffn1=v_norm_ffn1, ffn1_w_in=v_ffn1_w_in, ffn1_w_out=v_ffn1_w_out, norm_mix=v_norm_mix, w_in=v_w_in,
               gate_bias=v_gate_bias, pool_w=v_pool_w, pool_scale=v_pool_scale, w_ret_up=v_w_ret_up, w_pool_up=v_w_pool_up,
               w_out=v_w_out, norm_ffn2=v_norm_ffn2, ffn2_w_in=v_ffn2_w_in, ffn2_w_out=v_ffn2_w_out, norm_final=v_norm_final)

    ax, ay, ac = lax.axis_index("x"), lax.axis_index("y"), lax.axis_index("c")
    chip = 2 * ax + ay
    c_arr = jnp.reshape(ac, (1,)).astype(jnp.int32)
    sc_arr = jnp.stack([chip, ac]).astype(jnp.int32)
    bias_cols = gate_bias.shape[-1]

    first = {"gate_bias": gate_bias[0], "ffn1_w_in": ffn1_w_in[0].astype(BF16)}
    gather_groups, token = _gather_start("first", [0], first)
    rest, rest_token = _gather_start("rest", [1, 2, 3, 4],
                                     {nm: wt[nm][0].astype(BF16) + token[0, 0].astype(BF16) for nm in BIG if nm not in first})
    gather_groups.update(rest)
    vec = dict(norm_ffn1=norm_ffn1, norm_mix=norm_mix, norm_ffn2=norm_ffn2, pool_scale=pool_scale,
               norm_final=norm_final.reshape(1, D_MODEL))

    relayed = {}

    def relay_w(g, after):
        relayed[g] = _gather_relay(g, gather_groups[g], (after,))

    def get_w(g, after):
        if g in relayed:
            return _gather_land(g, relayed[g], (after,))
        return _gather_finish(g, gather_groups[g], (after, rest_token) if g == 0 else (after,))

    pairs, pending = [], []

    def on_grads(gr):
        g = len(pairs)
        names = GRAD_GROUPS[g]
        assert set(names) == set(gr), (names, list(gr))
        state, token = _pair_exchange_start(g, names, {nm: _grad_view(KIND[nm], gr[nm]) for nm in names})
        pairs.append(state)
        return token[0:1, 0:1]

    def flush(after):
        g = len(pending)
        names = GRAD_GROUPS[g]
        views, from_sib = _pair_exchange_wait(g, names, pairs[g], after)
        psums = {nm: _pair_sum(f"pair_sum_{nm}", KIND[nm], views[nm], from_sib[nm], c_arr) for nm in names}
        state, token = _shard_exchange_start(g, names, psums)
        pending.append(state)
        tokens.append(token)
        return token[0:1, 0:1]

    tokens = []
    loss_local, dx, small = _local_step(x[0], loss_target[0], vec, get_w, relay_w, on_grads, flush)

    grads, delta, new_m, new_v = {}, {}, {}, {}

    def adamw(nm):
        shape = wt[nm].shape
        outs = _adamw(f"adamw_{nm}", _as2d(wt[nm]), _as2d(grads[nm]), _as2d(mom[nm]), _as2d(var[nm]), with_grad=nm in BIG)
        delta[nm], new_m[nm], new_v[nm] = (o.reshape(shape) for o in outs[:3])
        if nm in BIG:
            grads[nm] = outs[3].reshape(shape)
        return outs[0]

    def reduce_start(g, after):
        names = GRAD_GROUPS[g]
        psums, from_chips = _shard_exchange_wait(g, names, pending[g], after)
        bufs = {nm: _shard_sum(f"shard_sum_{nm}", KIND[nm], psums[nm], from_chips[nm], sc_arr) for nm in names}
        return _half_exchange_start(g, names, bufs)

    def reduce_finish(g, state, after):
        names = GRAD_GROUPS[g]
        reduced = _half_exchange_wait(g, names, state, after)
        for nm in names:
            grads[nm] = reduced[nm].reshape(wt[nm].shape)
        return tuple(adamw(nm) for nm in names)

    swap0, token = reduce_start(0, (tokens[-1],))
    swap1, token = reduce_start(1, (token,))
    done = reduce_finish(0, swap0, (token,))
    done = reduce_finish(1, swap1, done)
    swap2, token = reduce_start(2, done)
    packed = jnp.concatenate([small["norm_ffn1"], small["norm_mix"], small["gate_bias"], small["pool_scale"],
                              small["norm_ffn2"], small["norm_final"], jnp.broadcast_to(loss_local, (1, D_MODEL))], axis=0)
    small_sum, _ = _all_reduce_small("reduce_small_grads", packed + token[0, 0])
    loss = small_sum[SMALL_ROWS - 1, 0]
    for nm in ("norm_ffn1", "norm_mix", "pool_scale", "norm_ffn2"):
        grads[nm] = small_sum[SMALL_ROW[nm]][None, :]
    grads["norm_final"] = small_sum[SMALL_ROW["norm_final"]]
    grads["gate_bias"] = lax.dynamic_slice(small_sum, (SMALL_ROW["gate_bias"], chip * bias_cols), (2, bias_cols))[None]
    reduce_finish(2, swap2, (small_sum,))
    for nm in WEIGHTS:
        if nm not in delta:
            adamw(nm)

    return (loss, dx[None], *[grads[nm] for nm in WEIGHTS], *[delta[nm] for nm in WEIGHTS],
            *[new_m[nm] for nm in WEIGHTS], *[new_v[nm] for nm in WEIGHTS])
```

```python
import numpy as np
import jax
import jax.numpy as jnp
from jax import lax
from jax.experimental import pallas as pl
from jax.experimental.pallas import tpu as pltpu

F32 = jnp.float32
BF16 = jnp.bfloat16
MESH = pl.DeviceIdType.MESH

D_MODEL = 1024
D_FF = 2816
HEADS = 4
HEAD_DIM = 256
GROUPS = 4
GROUP_DIM = 256
POOL_WINDOWS = (2, 4, 8, 16)
IN_WIDTH = 7 * D_MODEL
ROPE_BASE = 10000.0
NORM_EPS = 1e-6
FFN_RES_WEIGHT = 0.5
ADAM_LR, ADAM_B1, ADAM_B2, ADAM_EPS, ADAM_WD, ADAM_STEP = 0.001, 0.9, 0.999, 1e-08, 0.01, 10

N_CHIPS = 4
RET_BLOCK = 256
V7X_VMEM_LIMIT = 48 * 1024 * 1024


def _cparams(sem):
    return pltpu.CompilerParams(dimension_semantics=sem, vmem_limit_bytes=V7X_VMEM_LIMIT)


def _sigmoid(x):
    return jax.nn.sigmoid(x)


_DIMS = {"nn": (((1,), (0,)), ((), ())), "nt": (((1,), (1,)), ((), ())), "tn": (((0,), (0,)), ((), ()))}


def _matmul(name, a, b, mode, m, n, k, tm, tn, tk, out_dtypes, extras=(), consts=(), epilogue=None, resident=None,
            n_outer=False):
    tm, tn, tk = min(tm, m), min(tn, n), min(tk, k)
    gi, gj, gk = m // tm, n // tn, k // tk
    assert gi * tm == m and gj * tn == n and gk * tk == k, (name, m, n, k, tm, tn, tk)
    once = dict(pipeline_mode=pl.Buffered(1))

    def spec(shape, index, **kw):
        return pl.BlockSpec(shape, (lambda j, i, kk: index(i, j, kk)) if n_outer else index, **kw)

    kw = once if resident == "a" else {}
    a_spec = (spec((tk, tm), lambda i, j, kk: (kk, i), **kw) if mode == "tn" else spec((tm, tk), lambda i, j, kk: (i, kk), **kw))
    kw = once if resident == "b" else {}
    b_spec = (spec((tn, tk), lambda i, j, kk: (j, kk), **kw) if mode == "nt" else spec((tk, tn), lambda i, j, kk: (kk, j), **kw))
    n_ex, n_out = len(extras) + len(consts), len(out_dtypes)
    dims = _DIMS[mode]

    def body(a_ref, b_ref, *rest):
        ex_refs, out_refs = rest[:n_ex], rest[n_ex:n_ex + n_out]

        def finish(acc):
            outs = (acc,) if epilogue is None else epilogue(acc, *[e[...] for e in ex_refs])
            for o_ref, o in zip(out_refs, outs):
                o_ref[...] = o.astype(o_ref.dtype)

        prod = lax.dot_general(a_ref[...], b_ref[...], dims, preferred_element_type=F32)
        if gk == 1:
            finish(prod)
        else:
            acc_ref = rest[n_ex + n_out]
            kk = pl.program_id(2)

            @pl.when(kk == 0)
            def _():
                acc_ref[...] = prod

            @pl.when(kk > 0)
            def _():
                acc_ref[...] += prod

            @pl.when(kk == gk - 1)
            def _():
                finish(acc_ref[...])

    o_spec = spec((tm, tn), lambda i, j, kk: (i, j))
    outs = pl.pallas_call(
        body, name=name, grid=(gj, gi, gk) if n_outer else (gi, gj, gk),
        in_specs=[a_spec, b_spec] + [o_spec] * len(extras) + [spec((1, tn), lambda i, j, kk: (0, j))] * len(consts),
        out_specs=[o_spec] * n_out,
        out_shape=[jax.ShapeDtypeStruct((m, n), dt) for dt in out_dtypes],
        scratch_shapes=[pltpu.VMEM((tm, tn), F32)] if gk > 1 else [],
        compiler_params=_cparams(("parallel", "parallel", "arbitrary")),
    )(a, b, *extras, *consts)
    return outs[0] if n_out == 1 else outs


def _row_spec(tm, width, col_block=0):
    return pl.BlockSpec((tm, width), lambda i: (i, col_block))


def _full_spec(shape):
    return pl.BlockSpec(shape, lambda *_: (0,) * len(shape))


def _rmsnorm_fwd(name, h, g, tm=512):
    t = h.shape[0]

    def body(h_ref, g_ref, o_ref):
        x = h_ref[...]
        r = lax.rsqrt(jnp.mean(x * x, axis=-1, keepdims=True) + NORM_EPS)
        o_ref[...] = (x * r * g_ref[...]).astype(BF16)

    return pl.pallas_call(
        body, name=name, grid=(t // tm,),
        in_specs=[_row_spec(tm, D_MODEL), _full_spec((1, D_MODEL))],
        out_specs=_row_spec(tm, D_MODEL),
        out_shape=jax.ShapeDtypeStruct((t, D_MODEL), BF16),
        compiler_params=_cparams(("parallel",)),
    )(h, g)


def _proj_norm_bwd(name, a_list, a_specs, parts, w, h, g, dres, tm):
    t = h.shape[0]
    na = len(a_list)

    def body(*refs):
        a_refs = refs[:na]
        w_ref, h_ref, g_ref, dres_ref, dh_ref, dhb_ref, dg_ref = refs[na:]
        i = pl.program_id(0)
        dn_v = None
        for which, lead, k0, k1 in parts:
            a_ref = a_refs[which]
            term = _dot(a_ref[...] if lead is None else a_ref[lead], w_ref[:, k0:k1], "nt")
            dn_v = term if dn_v is None else dn_v + term
        x = h_ref[...]
        r = lax.rsqrt(jnp.mean(x * x, axis=-1, keepdims=True) + NORM_EPS)
        xh = x * r
        dxh = dn_v * g_ref[...]
        dh = dres_ref[...] + r * (dxh - xh * jnp.mean(dxh * xh, axis=-1, keepdims=True))
        dh_ref[...] = dh
        dhb_ref[...] = dh.astype(BF16)
        part = jnp.sum(dn_v * xh, axis=0, keepdims=True)

        @pl.when(i == 0)
        def _():
            dg_ref[...] = part

        @pl.when(i > 0)
        def _():
            dg_ref[...] += part

    row = _row_spec(tm, D_MODEL)
    return pl.pallas_call(
        body, name=name, grid=(t // tm,),
        in_specs=list(a_specs) + [pl.BlockSpec(w.shape, lambda i: (0, 0), pipeline_mode=pl.Buffered(1)), row,
                                  _full_spec((1, D_MODEL)), row],
        out_specs=[row, row, _full_spec((1, D_MODEL))],
        out_shape=[jax.ShapeDtypeStruct((t, D_MODEL), F32), jax.ShapeDtypeStruct((t, D_MODEL), BF16),
                   jax.ShapeDtypeStruct((1, D_MODEL), F32)],
        compiler_params=_cparams(("arbitrary",)),
    )(*a_list, w, h, g, dres)


def _out_loss_and_grad(name, mid, w_out, h, g, target, tm=512):
    t = h.shape[0]

    def body(m_ref, w_ref, h_ref, g_ref, t_ref, dh_ref, dhb_ref, dg_ref, loss_ref):
        i = pl.program_id(0)
        x = h_ref[...] + FFN_RES_WEIGHT * _dot(m_ref[...], w_ref[...])
        gv = g_ref[...]
        r = lax.rsqrt(jnp.mean(x * x, axis=-1, keepdims=True) + NORM_EPS)
        xh = x * r
        err = xh * gv - t_ref[...]
        row = jnp.mean(err * err, axis=-1, keepdims=True)
        part_loss = 0.5 * jnp.sum(row, axis=0, keepdims=True)
        dy = err * (1.0 / D_MODEL)
        dxh = dy * gv
        dh = r * (dxh - xh * jnp.mean(dxh * xh, axis=-1, keepdims=True))
        dh_ref[...] = dh
        dhb_ref[...] = dh.astype(BF16)
        part = jnp.sum(dy * xh, axis=0, keepdims=True)

        @pl.when(i == 0)
        def _():
            dg_ref[...] = part
            loss_ref[...] = jnp.zeros(loss_ref.shape, F32) + part_loss

        @pl.when(i > 0)
        def _():
            dg_ref[...] += part
            loss_ref[...] += part_loss

    return pl.pallas_call(
        body, name=name, grid=(t // tm,),
        in_specs=[_row_spec(tm, D_FF), pl.BlockSpec((D_FF, D_MODEL), lambda i: (0, 0), pipeline_mode=pl.Buffered(1)),
                  _row_spec(tm, D_MODEL), _full_spec((1, D_MODEL)), _row_spec(tm, D_MODEL)],
        out_specs=[_row_spec(tm, D_MODEL), _row_spec(tm, D_MODEL), _full_spec((1, D_MODEL)), _full_spec((8, 128))],
        out_shape=[jax.ShapeDtypeStruct((t, D_MODEL), F32), jax.ShapeDtypeStruct((t, D_MODEL), BF16),
                   jax.ShapeDtypeStruct((1, D_MODEL), F32), jax.ShapeDtypeStruct((8, 128), F32)],
        compiler_params=_cparams(("arbitrary",)),
    )(mid, w_out, h, g, target)


def _rope_tables(t):
    half = HEAD_DIM // 2
    inv_freq = np.float32(ROPE_BASE) ** (-np.arange(half, dtype=np.float32) / np.float32(half))
    ang = (np.arange(t, dtype=np.float32)[:, None] * inv_freq[None, :].astype(np.float32)).astype(np.float32)
    return jnp.asarray(np.cos(ang.astype(np.float64)).astype(np.float32)), jnp.asarray(np.sin(ang.astype(np.float64)).astype(np.float32))


ROPE_HALF = HEAD_DIM // 2
K_SCALE = HEAD_DIM ** -0.5


def _rotate(ref, rows, hh, c, s, scale=None):
    lo, mid, hi = hh * HEAD_DIM, hh * HEAD_DIM + ROPE_HALF, (hh + 1) * HEAD_DIM
    x1, x2 = ref[rows, lo:mid].astype(F32), ref[rows, mid:hi].astype(F32)
    y = jnp.concatenate([x1 * c - x2 * s, x1 * s + x2 * c], axis=1)
    return y if scale is None else y * scale


def _unrotate_into(ref, rows, hh, dy, c, s, scale=None):
    lo, mid, hi = hh * HEAD_DIM, hh * HEAD_DIM + ROPE_HALF, (hh + 1) * HEAD_DIM
    y1, y2 = dy[:, :ROPE_HALF], dy[:, ROPE_HALF:]
    d1, d2 = y1 * c + y2 * s, y2 * c - y1 * s
    if scale is not None:
        d1, d2 = d1 * scale, d2 * scale
    ref[rows, lo:mid] = d1.astype(ref.dtype)
    ref[rows, mid:hi] = d2.astype(ref.dtype)


def _retention_tables():
    b, chunk = RET_BLOCK, 64
    gamma = 1.0 - 2.0 ** (-5.0 - np.arange(HEADS, dtype=np.float64))
    log_g = np.log(gamma)[:, None, None]
    i = np.arange(b)[:, None]
    j = np.arange(b)[None, :]
    same = (i // chunk) == (j // chunk)
    earlier = (j // chunk) < (i // chunk)
    expo = np.where(same, np.abs(i - j), np.where(earlier, i - j, 0)).astype(np.float64)
    mask = np.where(same | earlier, 1.0, 0.0)
    dmat = np.exp(log_g * expo[None]) * mask[None]
    qd = np.exp(log_g[:, :, 0] * (np.arange(b)[None, :] + 1.0))
    kd = np.exp(log_g[:, :, 0] * (b - 1.0 - np.arange(b)[None, :]))
    cd = np.exp(log_g[:, :, 0] * b) * np.ones((1, HEAD_DIM))
    as32 = lambda v: jnp.asarray(v.astype(np.float32))
    return (as32(dmat), as32(np.swapaxes(dmat, 1, 2)), as32(qd[:, :, None]), as32(kd[:, :, None]), as32(cd[:, None, :]))


def _dot(a, b, mode="nn"):
    return lax.dot_general(a, b, _DIMS[mode], preferred_element_type=F32)


GRET_BLOCK = 3


RET_SUBS = 2
RET_STEP = RET_SUBS * RET_BLOCK


def _head_specs(steps, rev=False):
    pos = (lambda n: steps - 1 - n) if rev else (lambda n: n)
    tok = pl.BlockSpec((RET_STEP, D_MODEL), lambda n: (pos(n), 0))
    blk = [pl.BlockSpec((RET_STEP, D_MODEL), lambda n, b=b: (pos(n), b)) for b in range(GRET_BLOCK + 1)]
    rope = pl.BlockSpec((RET_STEP, ROPE_HALF), lambda n: (pos(n), 0))
    tab = _full_spec((HEADS, RET_BLOCK, RET_BLOCK))
    col = _full_spec((HEADS, RET_BLOCK, 1))
    rowv = _full_spec((HEADS, 1, HEAD_DIM))
    st = pl.BlockSpec((HEADS, RET_SUBS, HEAD_DIM, HEAD_DIM), lambda n: (0, pos(n), 0, 0))
    return tok, blk, rope, tab, col, rowv, st


def _retention_fwd(name, proj, cos, sin, tables):
    t = proj.shape[0]
    nb, steps = t // RET_BLOCK, t // RET_STEP
    dmat, _, qd, kd, cd = tables
    tok, blk, rope, tab, col, rowv, st = _head_specs(steps)

    def body(q_ref, k_ref, v_ref, g_ref, c_ref, s_ref, d_ref, qd_ref, kd_ref, cd_ref, o_ref, ret_ref, st_ref, state):
        n = pl.program_id(0)

        @pl.when(n == 0)
        def _():
            state[...] = jnp.zeros(state.shape, F32)

        for sub in range(RET_SUBS):
            rows = slice(sub * RET_BLOCK, (sub + 1) * RET_BLOCK)
            cs, sn = c_ref[rows, :], s_ref[rows, :]
            for hh in range(HEADS):
                sl = slice(hh * HEAD_DIM, (hh + 1) * HEAD_DIM)
                q, k = _rotate(q_ref, rows, hh, cs, sn), _rotate(k_ref, rows, hh, cs, sn, K_SCALE)
                v = v_ref[rows, sl].astype(BF16)
                s = _dot(q.astype(BF16), k.astype(BF16), "nt") * d_ref[hh]
                stb = state[hh].astype(BF16)
                st_ref[hh, sub] = stb
                o = _dot(s.astype(BF16), v) + _dot((q * qd_ref[hh]).astype(BF16), stb)
                o_ref[rows, sl] = o
                rn = o * lax.rsqrt(jnp.mean(o * o, axis=-1, keepdims=True) + NORM_EPS)
                g = g_ref[rows, sl].astype(F32)
                ret_ref[rows, sl] = (rn * (g * _sigmoid(g))).astype(BF16)
                state[hh] = state[hh] * cd_ref[hh] + _dot((k * kd_ref[hh]).astype(BF16), v, "tn")

    return pl.pallas_call(
        body, name=name, grid=(steps,),
        in_specs=blk + [rope, rope, tab, col, col, rowv],
        out_specs=[tok, tok, st],
        out_shape=[jax.ShapeDtypeStruct((t, D_MODEL), F32), jax.ShapeDtypeStruct((t, D_MODEL), BF16),
                   jax.ShapeDtypeStruct((HEADS, nb, HEAD_DIM, HEAD_DIM), BF16)],
        scratch_shapes=[pltpu.VMEM((HEADS, HEAD_DIM, HEAD_DIM), F32)],
        compiler_params=_cparams(("arbitrary",)),
    )(proj, proj, proj, proj, cos, sin, dmat, qd, kd, cd)


def _retention_bwd(name, dru, w_ru, o, proj, cos, sin, states, tables):
    t = proj.shape[0]
    steps = t // RET_STEP
    dmat, dmat_t, qd, kd, cd = tables
    tok, blk, rope, tab, col, rowv, st = _head_specs(steps, rev=True)

    def body(dru_ref, wru_ref, o_ref, q_ref, k_ref, v_ref, g_ref, c_ref, s_ref, st_ref, d_ref, dt_ref, qd_ref, kd_ref, cd_ref,
             dq_ref, dk_ref, dv_ref, dg_ref, gstate):
        n = pl.program_id(0)

        @pl.when(n == 0)
        def _():
            gstate[...] = jnp.zeros(gstate.shape, F32)

        for sub in reversed(range(RET_SUBS)):
            rows = slice(sub * RET_BLOCK, (sub + 1) * RET_BLOCK)
            cs, sn = c_ref[rows, :], s_ref[rows, :]
            dret = _dot(dru_ref[rows, :], wru_ref[...], "nt")
            for hh in range(HEADS):
                sl = slice(hh * HEAD_DIM, (hh + 1) * HEAD_DIM)
                o_v, g, dr = o_ref[rows, sl], g_ref[rows, sl].astype(F32), dret[:, sl]
                sg = _sigmoid(g)
                r = lax.rsqrt(jnp.mean(o_v * o_v, axis=-1, keepdims=True) + NORM_EPS)
                rn = o_v * r
                d_rn = dr * (g * sg)
                dg_ref[rows, sl] = (dr * rn * (sg * (1.0 + g * (1.0 - sg)))).astype(BF16)
                d_o = r * (d_rn - rn * jnp.mean(d_rn * rn, axis=-1, keepdims=True))
                dob = d_o.astype(BF16)

                q, k = _rotate(q_ref, rows, hh, cs, sn), _rotate(k_ref, rows, hh, cs, sn, K_SCALE)
                v = v_ref[rows, sl].astype(BF16)
                qb, kb = q.astype(BF16), k.astype(BF16)
                qdv, kdv = qd_ref[hh], kd_ref[hh]
                s_t = (_dot(kb, qb, "nt") * dt_ref[hh]).astype(BF16)
                p_t = (_dot(v, dob, "nt") * dt_ref[hh]).astype(BF16)
                p = (_dot(dob, v, "nt") * d_ref[hh]).astype(BF16)
                stb = st_ref[hh, sub]
                gb = gstate[hh].astype(BF16)
                _unrotate_into(dq_ref, rows, hh, _dot(p, kb) + _dot(dob, stb, "nt") * qdv, cs, sn)
                _unrotate_into(dk_ref, rows, hh, _dot(p_t, qb) + _dot(v, gb, "nt") * kdv, cs, sn, K_SCALE)
                dv_ref[rows, sl] = (_dot(s_t, dob) + _dot((k * kdv).astype(BF16), gb)).astype(BF16)
                gstate[hh] = gstate[hh] * cd_ref[hh] + _dot((q * qdv).astype(BF16), dob, "tn")

    return pl.pallas_call(
        body, name=name, grid=(steps,),
        in_specs=[tok, pl.BlockSpec((D_MODEL, D_MODEL), lambda n: (0, 0), pipeline_mode=pl.Buffered(1)), tok] + blk
                 + [rope, rope, st, tab, tab, col, col, rowv],
        out_specs=[tok, tok, tok, tok],
        out_shape=[jax.ShapeDtypeStruct((t, D_MODEL), BF16)] * 4,
        scratch_shapes=[pltpu.VMEM((HEADS, HEAD_DIM, HEAD_DIM), F32)],
        compiler_params=_cparams(("arbitrary",)),
    )(dru, w_ru, o, proj, proj, proj, proj, cos, sin, states, dmat, dmat_t, qd, kd, cd)


POOL_TILE = 256
POOL_SUBS = 4
POOL_STEP = POOL_SUBS * POOL_TILE


def _pool_tables():
    b = POOL_TILE
    tt = np.arange(b)[:, None]
    jj = np.arange(b)[None, :]
    cur, prev = [], []
    for w in POOL_WINDOWS:
        cur.append(((tt - jj >= 0) & (tt - jj <= w - 1)).astype(np.float32))
        prev.append((tt - (jj - b) <= w - 1).astype(np.float32))
    cur, prev = np.stack(cur), np.stack(prev)
    as16 = lambda v: jnp.asarray(v, dtype=BF16)
    return as16(cur), as16(prev), as16(np.swapaxes(cur, 1, 2)), as16(np.swapaxes(prev, 1, 2))


def _split2(x):
    hi = x.astype(BF16)
    return hi, (x - hi.astype(F32)).astype(BF16)


POOL_BLOCK = 4


def _pool_count(n, window):
    tpos = n * POOL_TILE + lax.broadcasted_iota(jnp.int32, (POOL_TILE, 1), 0)
    return jnp.minimum(tpos + 1, window).astype(F32)


def _pool_fwd(name, proj, pool_w, scale, tables):
    t = proj.shape[0]
    steps = t // POOL_STEP
    mc, mp, _, _ = tables
    tab = _full_spec((GROUPS, POOL_TILE, POOL_TILE))
    row = _row_spec(POOL_STEP, D_MODEL)

    def body(pc_ref, pp_ref, mc_ref, mp_ref, w_ref, sc_ref, pm_ref, mix_ref, po_ref):
        n = pl.program_id(0)
        for sub in range(POOL_SUBS):
            rows = slice(sub * POOL_TILE, (sub + 1) * POOL_TILE)
            tile = n * POOL_SUBS + sub
            for g, window in enumerate(POOL_WINDOWS):
                sl = slice(g * GROUP_DIM, (g + 1) * GROUP_DIM)
                p = pc_ref[rows, sl]
                if sub == 0:
                    before = jnp.where(n > 0, _dot(mp_ref[g], pp_ref[:, sl]), 0.0)
                else:
                    before = _dot(mp_ref[g], pc_ref[(sub - 1) * POOL_TILE:sub * POOL_TILE, sl])
                pm = ((_dot(mc_ref[g], p) + before) / _pool_count(tile, window) - p.astype(F32)).astype(BF16)
                pm_ref[rows, sl] = pm
                mixed = _dot(pm, w_ref[g])
                mix_ref[rows, sl] = mixed
                po_ref[rows, sl] = (mixed * sc_ref[:, sl]).astype(BF16)

    return pl.pallas_call(
        body, name=name, grid=(steps,),
        in_specs=[_row_spec(POOL_STEP, D_MODEL, POOL_BLOCK),
                  pl.BlockSpec((POOL_TILE, D_MODEL), lambda n: (jnp.maximum(n * POOL_SUBS - 1, 0), POOL_BLOCK)),
                  tab, tab, _full_spec((GROUPS, GROUP_DIM, GROUP_DIM)), _full_spec((1, D_MODEL))],
        out_specs=[row] * 3,
        out_shape=[jax.ShapeDtypeStruct((t, D_MODEL), BF16), jax.ShapeDtypeStruct((t, D_MODEL), F32),
                   jax.ShapeDtypeStruct((t, D_MODEL), BF16)],
        compiler_params=_cparams(("parallel",)),
    )(proj, proj, mc, mp, pool_w, scale)


def _pool_bwd(name, dpu, w_pu, pm, mixed, pool_w, scale, tables):
    t = dpu.shape[0]
    steps = t // POOL_STEP
    _, _, mct, mpt = tables
    cur = pl.BlockSpec((POOL_STEP, D_MODEL), lambda n: (steps - 1 - n, 0))
    tab = _full_spec((GROUPS, POOL_TILE, POOL_TILE))
    wspec = _full_spec((GROUPS, GROUP_DIM, GROUP_DIM))
    sspec = _full_spec((1, D_MODEL))

    def body(dpu_ref, wpu_ref, pm_ref, mix_ref, mct_ref, mpt_ref, w_ref, sc_ref, dp_ref, dw_ref, ds_ref, later):
        n = pl.program_id(0)

        @pl.when(n == 0)
        def _():
            dw_ref[...] = jnp.zeros(dw_ref.shape, F32)
            ds_ref[...] = jnp.zeros(ds_ref.shape, F32)
            later[...] = jnp.zeros(later.shape, F32)

        for sub in reversed(range(POOL_SUBS)):
            rows = slice(sub * POOL_TILE, (sub + 1) * POOL_TILE)
            tile = (steps - 1 - n) * POOL_SUBS + sub
            dpo = _dot(dpu_ref[rows, :], wpu_ref[...], "nt")
            for g, window in enumerate(POOL_WINDOWS):
                sl = slice(g * GROUP_DIM, (g + 1) * GROUP_DIM)
                dc, sc = dpo[:, sl], sc_ref[:, sl]
                dmix = (dc * sc).astype(BF16)
                dpm = _dot(dmix, w_ref[g], "nt")
                e = dpm / _pool_count(tile, window)
                e_hi, e_lo = _split2(e)
                f_hi, f_lo = _split2(later[g])
                mctv, mptv = mct_ref[g], mpt_ref[g]
                back = _dot(mctv, e_hi) + _dot(mctv, e_lo)
                after = _dot(mptv, f_hi) + _dot(mptv, f_lo)
                dp_ref[rows, sl] = (back + after - dpm).astype(BF16)
                later[g] = e
                dw_ref[g] += _dot(pm_ref[rows, sl], dmix, "tn")
                ds_ref[:, sl] += jnp.sum(dc * mix_ref[rows, sl], axis=0, keepdims=True)

    return pl.pallas_call(
        body, name=name, grid=(steps,),
        in_specs=[cur, pl.BlockSpec((D_MODEL, D_MODEL), lambda n: (0, 0), pipeline_mode=pl.Buffered(1)), cur, cur, tab, tab,
                  wspec, sspec],
        out_specs=[cur, wspec, sspec],
        out_shape=[jax.ShapeDtypeStruct((t, D_MODEL), BF16), jax.ShapeDtypeStruct((GROUPS, GROUP_DIM, GROUP_DIM), F32),
                   jax.ShapeDtypeStruct((1, D_MODEL), F32)],
        scratch_shapes=[pltpu.VMEM((GROUPS, POOL_TILE, GROUP_DIM), F32)],
        compiler_params=_cparams(("arbitrary",)),
    )(dpu, w_pu, pm, mixed, mct, mpt, pool_w, scale)


GATE0_BLOCK, GATE1_BLOCK = 5, 6


def _merge_fwd(name, ret, po, w_ru, w_pu, w_out, proj, bias, h, next_g, tm=512):
    t = ret.shape[0]

    def body(r_ref, p_ref, wr_ref, wp_ref, wo_ref, g0_ref, g1_ref, b_ref, h_ref, ng_ref, m_ref, ru_ref, pu_ref, ho_ref, n_ref):
        ru = _dot(r_ref[...], wr_ref[...])
        pu = _dot(p_ref[...], wp_ref[...])
        ru_ref[...] = ru
        pu_ref[...] = pu
        merged = (_sigmoid(g0_ref[...].astype(F32) + b_ref[0:1, :]) * ru
                  + _sigmoid(g1_ref[...].astype(F32) + b_ref[1:2, :]) * pu).astype(BF16)
        m_ref[...] = merged
        h_new = h_ref[...] + _dot(merged, wo_ref[...])
        ho_ref[...] = h_new
        n_ref[...] = _normed(h_new, ng_ref[...]).astype(BF16)

    row = _row_spec(tm, D_MODEL)
    wspec = pl.BlockSpec((D_MODEL, D_MODEL), lambda i: (0, 0), pipeline_mode=pl.Buffered(1))
    return pl.pallas_call(
        body, name=name, grid=(t // tm,),
        in_specs=[row, row, wspec, wspec, wspec, _row_spec(tm, D_MODEL, GATE0_BLOCK), _row_spec(tm, D_MODEL, GATE1_BLOCK),
                  _full_spec((2, D_MODEL)), row, _full_spec((1, D_MODEL))],
        out_specs=[row] * 5,
        out_shape=[jax.ShapeDtypeStruct((t, D_MODEL), BF16), jax.ShapeDtypeStruct((t, D_MODEL), F32),
                   jax.ShapeDtypeStruct((t, D_MODEL), F32), jax.ShapeDtypeStruct((t, D_MODEL), F32),
                   jax.ShapeDtypeStruct((t, D_MODEL), BF16)],
        compiler_params=_cparams(("parallel",)),
    )(ret, po, w_ru, w_pu, w_out, proj, proj, bias, h, next_g)


def _merge_bwd(name, dh_b, w_out, ru, pu, proj, bias, tm=512):
    t = dh_b.shape[0]

    def body(dh_ref, wo_ref, ru_ref, pu_ref, g0_ref, g1_ref, b_ref, dru_ref, dpu_ref, dg0_ref, dg1_ref, db_ref):
        i = pl.program_id(0)
        d = _dot(dh_ref[...], wo_ref[...], "nt")
        s0 = _sigmoid(g0_ref[...].astype(F32) + b_ref[0:1, :])
        s1 = _sigmoid(g1_ref[...].astype(F32) + b_ref[1:2, :])
        dru_ref[...] = (d * s0).astype(BF16)
        dpu_ref[...] = (d * s1).astype(BF16)
        dg0 = d * ru_ref[...] * (s0 * (1.0 - s0))
        dg1 = d * pu_ref[...] * (s1 * (1.0 - s1))
        dg0_ref[...] = dg0.astype(BF16)
        dg1_ref[...] = dg1.astype(BF16)
        part0 = jnp.sum(dg0, axis=0, keepdims=True)
        part1 = jnp.sum(dg1, axis=0, keepdims=True)

        @pl.when(i == 0)
        def _():
            db_ref[0:1, :] = part0
            db_ref[1:2, :] = part1

        @pl.when(i > 0)
        def _():
            db_ref[0:1, :] += part0
            db_ref[1:2, :] += part1

    row = _row_spec(tm, D_MODEL)
    return pl.pallas_call(
        body, name=name, grid=(t // tm,),
        in_specs=[row, pl.BlockSpec((D_MODEL, D_MODEL), lambda i: (0, 0), pipeline_mode=pl.Buffered(1)), row, row,
                  _row_spec(tm, D_MODEL, GATE0_BLOCK), _row_spec(tm, D_MODEL, GATE1_BLOCK), _full_spec((2, D_MODEL))],
        out_specs=[row, row, row, row, _full_spec((2, D_MODEL))],
        out_shape=[jax.ShapeDtypeStruct((t, D_MODEL), BF16)] * 4 + [jax.ShapeDtypeStruct((2, D_MODEL), F32)],
        compiler_params=_cparams(("arbitrary",)),
    )(dh_b, w_out, ru, pu, proj, proj, bias)


def _half_scale(acc):
    return (FFN_RES_WEIGHT * acc,)


def _normed(h, g):
    return h * lax.rsqrt(jnp.mean(h * h, axis=-1, keepdims=True) + NORM_EPS) * g


def _residual_half_norm(acc, res, g):
    h = res + FFN_RES_WEIGHT * acc
    return h, _normed(h, g)


FF_TILE = D_FF // 2
DW_TILE = 256
SAVED_FF_DTYPE = BF16
FF_CHUNKS = ((0, 512), (512, 1024), (1024, FF_TILE))


def _ffn_in(name, nrm, w_in, tm=512):
    t = nrm.shape[0]
    nj = D_FF // FF_TILE

    def body(n_ref, wg_ref, wu_ref, a_ref, mid_ref):
        nv = n_ref[...]
        for c0, c1 in FF_CHUNKS:
            gate = _dot(nv, wg_ref[:, c0:c1])
            up = _dot(nv, wu_ref[:, c0:c1])
            s = _sigmoid(gate)
            silu = gate * s
            a_ref[0, :, c0:c1] = (FFN_RES_WEIGHT * up * (s * (1.0 + gate * (1.0 - s)))).astype(a_ref.dtype)
            a_ref[1, :, c0:c1] = (FFN_RES_WEIGHT * silu).astype(a_ref.dtype)
            mid_ref[:, c0:c1] = (silu * up).astype(BF16)

    return pl.pallas_call(
        body, name=name, grid=(nj, t // tm),
        in_specs=[pl.BlockSpec((tm, D_MODEL), lambda j, i: (i, 0)),
                  pl.BlockSpec((D_MODEL, FF_TILE), lambda j, i: (0, j)),
                  pl.BlockSpec((D_MODEL, FF_TILE), lambda j, i: (0, j + nj))],
        out_specs=[pl.BlockSpec((2, tm, FF_TILE), lambda j, i: (0, i, j)), pl.BlockSpec((tm, FF_TILE), lambda j, i: (i, j))],
        out_shape=[jax.ShapeDtypeStruct((2, t, D_FF), SAVED_FF_DTYPE), jax.ShapeDtypeStruct((t, D_FF), BF16)],
        compiler_params=_cparams(("parallel", "parallel")),
    )(nrm, w_in, w_in)


def _ffn_dact(name, dout_b, w_out, a, tm=512):
    t = dout_b.shape[0]

    def body(d_ref, w_ref, a_ref, da_ref):
        dv = d_ref[...]
        for c0, c1 in FF_CHUNKS:
            dm = _dot(dv, w_ref[c0:c1, :], "nt")
            da_ref[0, :, c0:c1] = (dm * a_ref[0, :, c0:c1].astype(F32)).astype(BF16)
            da_ref[1, :, c0:c1] = (dm * a_ref[1, :, c0:c1].astype(F32)).astype(BF16)

    blk = pl.BlockSpec((2, tm, FF_TILE), lambda j, i: (0, i, j))
    return pl.pallas_call(
        body, name=name, grid=(D_FF // FF_TILE, t // tm),
        in_specs=[pl.BlockSpec((tm, D_MODEL), lambda j, i: (i, 0)), pl.BlockSpec((FF_TILE, D_MODEL), lambda j, i: (j, 0)), blk],
        out_specs=blk,
        out_shape=jax.ShapeDtypeStruct((2, t, D_FF), BF16),
        compiler_params=_cparams(("parallel", "parallel")),
    )(dout_b, w_out, a)


def _ffn_fwd(tag, h, nrm, get_w_in, get_w_out, finish):
    t = h.shape[0]
    w_in = get_w_in(nrm)
    a, mid = _ffn_in(f"{tag}_in", nrm, w_in, tm=min(1024, t))
    w_out = get_w_out(mid)
    return finish(mid, w_out), (nrm, a, mid, w_in, w_out)


def _ffn_bwd(tag, h, g, saved, dout, dout_b, on_grads, flush):
    t = h.shape[0]
    nrm, a, mid, w_in, w_out = saved
    d_w_out = _matmul(f"{tag}_dwout", mid, dout_b, "tn", D_FF, D_MODEL, t, DW_TILE, D_MODEL, t, [BF16], epilogue=_half_scale,
                      resident="b")
    da = _ffn_dact(f"{tag}_dact", dout_b, w_out, a, tm=min(1024, t))
    nj = D_FF // DW_TILE
    d_w_in = _dw_resident(f"{tag}_dwin", nrm, [da], [pl.BlockSpec((None, t, DW_TILE), lambda s: (s // nj, 0, s % nj))],
                          2 * nj, None, DW_TILE)
    tie = on_grads({f"{tag}_w_in": d_w_in, f"{tag}_w_out": d_w_out})
    tm = min(256, t)
    dh, dh_b, dg = _proj_norm_bwd(f"{tag}_dn", [da], [pl.BlockSpec((2, tm, D_FF), lambda i: (0, i, 0))],
                                  ((0, 0, 0, D_FF), (0, 1, D_FF, 2 * D_FF)), w_in, h, g if tie is None else g + tie, dout, tm)
    return dh, dh_b, dg, flush(dh)


def _dw_resident(name, u, pieces, piece_specs, n_tiles, which_piece, tn):
    t = u.shape[0]
    npc = len(pieces)

    def body(*refs):
        u_ref, p_refs, o_ref, ut_ref = refs[0], refs[1:1 + npc], refs[1 + npc], refs[2 + npc]
        s = pl.program_id(0)

        @pl.when(s == 0)
        def _():
            ut_ref[...] = u_ref[...].T

        if npc == 1:
            o_ref[...] = _dot(ut_ref[...], p_refs[0][...]).astype(BF16)
        for which in range(npc if npc > 1 else 0):
            @pl.when(which_piece(s) == which)
            def _(which=which):
                o_ref[...] = _dot(ut_ref[...], p_refs[which][...]).astype(BF16)

    return pl.pallas_call(
        body, name=name, grid=(n_tiles,),
        in_specs=[pl.BlockSpec((t, D_MODEL), lambda s: (0, 0), pipeline_mode=pl.Buffered(1))] + list(piece_specs),
        out_specs=pl.BlockSpec((D_MODEL, tn), lambda s: (0, s)),
        out_shape=jax.ShapeDtypeStruct((D_MODEL, n_tiles * tn), BF16),
        scratch_shapes=[pltpu.VMEM((D_MODEL, t), BF16)],
        compiler_params=_cparams(("arbitrary",)),
    )(u, *pieces)


def _mix_dwin(name, u, pieces, tn=256):
    t = u.shape[0]
    nj = D_MODEL // tn
    specs = [pl.BlockSpec((t, tn), lambda s, k=k: (0, jnp.clip(s - k * nj, 0, nj - 1))) for k in range(len(pieces))]
    return _dw_resident(name, u, pieces, specs, len(pieces) * nj, lambda s: s // nj, tn)


def _local_step(x, target, vec, get_w, relay_w, on_grads, flush):
    t = x.shape[0]
    cos, sin = _rope_tables(t)
    rtab = _retention_tables()
    ptab = _pool_tables()
    w = {}

    def getter(group, name):
        def get(after):
            if name not in w:
                w.update(get_w(group, after))
            return w[name]
        return get

    nrm1 = _rmsnorm_fwd("ffn1_norm", x, vec["norm_ffn1"])
    def out_and_norm(mid, w_out):
        return _matmul("ffn1_out", mid, w_out, "nn", t, D_MODEL, D_FF, 512, D_MODEL, D_FF, [F32, BF16],
                       extras=(x,), consts=(vec["norm_mix"],), epilogue=_residual_half_norm)

    (h1, u), s1 = _ffn_fwd("ffn1", x, nrm1, getter(0, "ffn1_w_in"), getter(1, "ffn1_w_out"), out_and_norm)
    w.update(get_w(2, u))
    relay_w(3, w["w_in"])
    proj = _matmul("mix_in", u, w["w_in"], "nn", t, IN_WIDTH, D_MODEL, 2048, 1024, D_MODEL, [BF16], n_outer=True)
    w.update(get_w(3, proj))
    o, ret, states = _retention_fwd("retention", proj, cos, sin, rtab)
    pm, mixed, po = _pool_fwd("pool", proj, w["pool_w"], vec["pool_scale"], ptab)
    relay_w(4, po)
    merged, ru, pu, h2, nrm2 = _merge_fwd("merge", ret, po, w["w_ret_up"], w["w_pool_up"], w["w_out"], proj, w["gate_bias"],
                                          h1, vec["norm_ffn2"], tm=min(512, t))
    def out_and_loss(mid, w_out):
        return _out_loss_and_grad("ffn2_out_loss", mid, w_out, h2, vec["norm_final"], target, tm=min(512, t))

    (dh3, dh3_b, dg_final, loss), s2 = _ffn_fwd("ffn2", h2, nrm2, getter(4, "ffn2_w_in"), getter(4, "ffn2_w_out"), out_and_loss)

    def tied(v, tie):
        return v if tie is None else v + tie

    dh2, dh2_b, dg_ffn2, tie = _ffn_bwd("ffn2", h2, vec["norm_ffn2"], s2, dh3, dh3_b, on_grads, flush)
    def square_dw(name, act, grad):
        return _matmul(name, act, grad, "tn", D_MODEL, D_MODEL, t, D_MODEL, D_MODEL, 1024, [BF16])

    d_w_out = square_dw("mix_dwout", merged, dh2_b)
    dru, dpu, dg0, dg1, d_bias = _merge_bwd("merge_bwd", dh2_b, w["w_out"], ru, pu, proj, tied(w["gate_bias"], tie))
    d_w_ru = square_dw("mix_dwru", ret, dru)
    d_w_pu = square_dw("mix_dwpu", po, dpu)
    dp, d_pool_w, d_scale = _pool_bwd("pool_bwd", dpu, w["w_pool_up"], pm, mixed, w["pool_w"], vec["pool_scale"], ptab)
    dq, dk, dv, dgr = _retention_bwd("retention_bwd", dru, w["w_ret_up"], o, proj, cos, sin, states, rtab)
    dproj = [dq, dk, dv, dgr, dp, dg0, dg1]
    d_w_in = _mix_dwin("mix_dwin", u, dproj)
    tie = on_grads(dict(w_in=d_w_in, pool_w=d_pool_w.astype(BF16), w_ret_up=d_w_ru, w_pool_up=d_w_pu, w_out=d_w_out))
    tm = min(256, t)
    dh1, dh1_b, dg_mix = _proj_norm_bwd("mix_du", dproj, [_row_spec(tm, D_MODEL)] * len(dproj),
                                        [(k, None, k * D_MODEL, (k + 1) * D_MODEL) for k in range(len(dproj))],
                                        w["w_in"], h1, tied(vec["norm_mix"], tie), dh2, tm)
    tie = flush(dh1)
    dx, _, dg_ffn1, _ = _ffn_bwd("ffn1", x, tied(vec["norm_ffn1"], tie), s1, dh1, dh1_b, on_grads, flush)

    small = dict(norm_ffn1=dg_ffn1, norm_mix=dg_mix, gate_bias=d_bias, pool_scale=d_scale, norm_ffn2=dg_ffn2,
                 norm_final=dg_final)
    return loss[0, 0], dx, small


BIG = ("ffn1_w_in", "ffn1_w_out", "w_in", "pool_w", "w_ret_up", "w_pool_up", "w_out", "ffn2_w_in", "ffn2_w_out")
KIND = dict(ffn1_w_in="col", ffn1_w_out="row", w_in="col", pool_w="pool", w_ret_up="row", w_pool_up="row", w_out="row",
            ffn2_w_in="col", ffn2_w_out="row", gate_bias="col")
ANY = pl.BlockSpec(memory_space=pl.ANY)


def _place():
    x, y, c = lax.axis_index("x"), lax.axis_index("y"), lax.axis_index("c")
    chips = [(1 - x, y), (x, 1 - y), (1 - x, 1 - y)]
    return x, y, c, chips


def _full_view_shape(kind, local_shape):
    if kind == "col":
        return (2, local_shape[0] // 2, N_CHIPS * local_shape[1])
    if kind == "row":
        return (N_CHIPS, 2, local_shape[0] // 2, local_shape[1])
    return (GROUPS, N_CHIPS, 2, local_shape[1] // 2, local_shape[2])


def _local_view(kind, arr):
    if kind == "pool":
        return arr.reshape(GROUPS, 2, arr.shape[1] // 2, arr.shape[2])
    return arr.reshape(2, arr.shape[0] // 2, arr.shape[1])


def _blk(kind, ref, s, c):
    if kind == "col":
        cs = ref.shape[2] // N_CHIPS
        return ref.at[c, :, pl.ds(pl.multiple_of(s * cs, 128), cs)]
    if kind == "row":
        return ref.at[s, c]
    return ref.at[:, s, c]


def _half(kind, ref, c):
    return ref.at[:, c] if kind == "pool" else ref.at[c]


def _shard(kind, ref, s):
    if kind == "col":
        cs = ref.shape[2] // N_CHIPS
        return ref.at[:, :, pl.ds(pl.multiple_of(s * cs, 128), cs)]
    if kind == "row":
        return ref.at[s]
    return ref.at[:, s]


HBM = pl.BlockSpec(memory_space=pltpu.HBM)
SEM = pl.BlockSpec(memory_space=pltpu.SEMAPHORE)
EFFECT = pltpu.SideEffectType.DATAFLOW_SIDE_EFFECTING
WEIGHT_GROUPS = (("gate_bias", "ffn1_w_in"), ("ffn1_w_out",), ("w_in",), ("pool_w", "w_ret_up", "w_pool_up", "w_out"),
                 ("ffn2_w_in", "ffn2_w_out"))
GRAD_GROUPS = (("ffn2_w_in", "ffn2_w_out"), ("w_in", "pool_w", "w_ret_up", "w_pool_up", "w_out"), ("ffn1_w_in", "ffn1_w_out"))


def _hbm(a):
    return pltpu.with_memory_space_constraint(a, pltpu.HBM)


def _natural(kind, o):
    if kind == "col":
        return o.reshape(o.shape[0] * o.shape[1], o.shape[2])
    if kind == "row":
        return o.reshape(-1, o.shape[3])
    return o.reshape(GROUPS, -1, o.shape[4])


def _ici_copy(kind, loc, full, j, chips, s, c, send_sem, recv_sem):
    px, py = chips[j]
    return (pltpu.make_async_remote_copy(src_ref=_half(kind, loc, c), dst_ref=_blk(kind, full, s, c), send_sem=send_sem,
                                         recv_sem=recv_sem, device_id=(px, py, c), device_id_type=MESH),
            pltpu.make_async_remote_copy(src_ref=_half(kind, loc, c), dst_ref=_blk(kind, full, 2 * px + py, c), send_sem=send_sem,
                                         recv_sem=recv_sem, device_id=(px, py, c), device_id_type=MESH))


def _first_leg(kind, loc, full, a, place, send_sems, recv_sems):
    x, y, c, chips = place
    s = 2 * x + y
    pairs = [_ici_copy(kind, loc, full, j, chips, s, c, send_sems.at[4 * a + j], recv_sems.at[4 * a + j]) for j in range(3)]
    own = pltpu.make_async_remote_copy(src_ref=loc, dst_ref=_shard(kind, full, s), send_sem=send_sems.at[4 * a + 3],
                                       recv_sem=recv_sems.at[4 * a + 3], device_id=(x, y, 1 - c), device_id_type=MESH)
    return pairs + [(own, own)]


def _gather_start(tag, group_ids, shards):
    grps = [WEIGHT_GROUPS[g] for g in group_ids]
    names = [nm for grp in grps for nm in grp]
    kinds = [KIND[nm] for nm in names]
    n, ng = len(names), len(grps)
    locs = [_hbm(_local_view(KIND[nm], shards[nm])) for nm in names]
    lands = [_hbm(lax.empty(_full_view_shape(KIND[nm], shards[nm].shape), shards[nm].dtype)) for nm in names]
    first = np.cumsum([0] + [len(grp) for grp in grps])

    def body(*refs):
        loc, full = refs[:n], refs[n:2 * n]
        send_sems, recv_sems = refs[2 * n:2 * n + ng], refs[2 * n + ng:2 * n + 2 * ng]
        token = refs[-1]
        place = _place()
        for g in range(ng):
            for a in range(first[g], first[g + 1]):
                pairs = _first_leg(kinds[a], loc[a], full[a], a - first[g], place, send_sems[g], recv_sems[g])
                for k in (2, 0, 1, 3):
                    pairs[k][0].start()
        token[...] = jnp.zeros(token.shape, F32)

    sem_shapes = [pltpu.SemaphoreType.DMA((4 * len(grp),)) for grp in grps]
    outs = pl.pallas_call(
        body, name=f"gather_start_{tag}",
        in_specs=[HBM] * (2 * n),
        out_specs=[SEM] * (2 * ng) + [HBM] * (2 * n) + [pl.BlockSpec(memory_space=pltpu.VMEM)],
        out_shape=sem_shapes + sem_shapes + [pltpu.HBM(a.shape, a.dtype) for a in locs + lands] + [jax.ShapeDtypeStruct((8, 128), F32)],
        input_output_aliases={i: 2 * ng + i for i in range(2 * n)},
        compiler_params=pltpu.CompilerParams(has_side_effects=EFFECT),
    )(*locs, *lands)
    send_sems, recv_sems = outs[:ng], outs[ng:2 * ng]
    locs_t, lands_t = outs[2 * ng:2 * ng + n], outs[2 * ng + n:2 * ng + 2 * n]
    groups = {}
    for k, g in enumerate(group_ids):
        sl = slice(first[k], first[k + 1])
        groups[g] = (send_sems[k], recv_sems[k], list(locs_t[sl]), list(lands_t[sl]))
    return groups, outs[-1]


def _forward_copies(kinds, loc, full, send_sems, recv_sems):
    x, y, c, chips = _place()

    def remote(a, k, part):
        return pltpu.make_async_remote_copy(src_ref=part, dst_ref=part, send_sem=send_sems.at[3 * a + k],
                                            recv_sem=recv_sems.at[3 * a + k], device_id=(x, y, 1 - c), device_id_type=MESH)

    sends, arrivals = [], []
    for a, kind in enumerate(kinds):
        for j, (px, py) in enumerate(chips):
            sends.append(remote(a, j, _blk(kind, full[a], 2 * px + py, c)))
            arrivals.append(remote(a, j, _blk(kind, full[a], 2 * px + py, 1 - c)))
    return sends, arrivals


def _gather_relay(g, group, after):
    names = WEIGHT_GROUPS[g]
    kinds = [KIND[nm] for nm in names]
    m = len(names)
    ici_send, ici_recv, locs, lands = group

    def body(*refs):
        loc, full = refs[:m], refs[m:2 * m]
        ici_s, ici_r = refs[2 * m], refs[2 * m + 1]
        d2d_s, d2d_r = refs[2 * m + 2 + len(after)], refs[2 * m + 3 + len(after)]
        place = _place()
        for a in range(m):
            for sent, landed in _first_leg(kinds[a], loc[a], full[a], a, place, ici_s, ici_r):
                sent.wait_send()
                landed.wait_recv()
        for cp in _forward_copies(kinds, loc, full, d2d_s, d2d_r)[0]:
            cp.start()

    sem_shape = pltpu.SemaphoreType.DMA((3 * m,))
    outs = pl.pallas_call(
        body, name=f"gather_relay_{g}",
        in_specs=[HBM] * (2 * m) + [SEM, SEM] + [ANY] * len(after), out_specs=[SEM, SEM] + [HBM] * (2 * m),
        out_shape=[sem_shape, sem_shape] + [pltpu.HBM(a.shape, a.dtype) for a in locs + lands],
        input_output_aliases={i: 2 + i for i in range(2 * m)},
        compiler_params=pltpu.CompilerParams(has_side_effects=EFFECT),
    )(*locs, *lands, ici_send, ici_recv, *after)
    return outs[0], outs[1], list(outs[2:2 + m]), list(outs[2 + m:2 + 2 * m])


def _gather_land(g, state, after):
    names = WEIGHT_GROUPS[g]
    kinds = [KIND[nm] for nm in names]
    m = len(names)
    d2d_send, d2d_recv, locs, lands = state

    def body(*refs):
        sends, arrivals = _forward_copies(kinds, refs[:m], refs[m:2 * m], refs[2 * m], refs[2 * m + 1])
        for cp in sends:
            cp.wait_send()
        for cp in arrivals:
            cp.wait_recv()

    outs = pl.pallas_call(
        body, name=f"gather_land_{g}",
        in_specs=[HBM] * (2 * m) + [SEM, SEM] + [ANY] * len(after), out_specs=[HBM] * (2 * m),
        out_shape=[pltpu.HBM(a.shape, a.dtype) for a in locs + lands],
        input_output_aliases={i: i for i in range(2 * m)},
        compiler_params=pltpu.CompilerParams(has_side_effects=EFFECT),
    )(*locs, *lands, d2d_send, d2d_recv, *after)
    return {nm: _natural(k, o) for nm, k, o in zip(names, kinds, outs[m:])}


def _gather_finish(g, group, after):
    names = WEIGHT_GROUPS[g]
    kinds = [KIND[nm] for nm in names]
    m = len(names)
    send_sem, recv_sem, locs, lands = group

    def wait_body(*refs):
        loc, full = refs[:m], refs[m:2 * m]
        send_sems, recv_sems = refs[2 * m], refs[2 * m + 1]
        place = _place()
        for a in range(m):
            for sent, landed in _first_leg(kinds[a], loc[a], full[a], a, place, send_sems, recv_sems):
                sent.wait_send()
                landed.wait_recv()

    outs = pl.pallas_call(
        wait_body, name=f"gather_wait_{g}",
        in_specs=[HBM] * (2 * m) + [SEM, SEM] + [ANY] * len(after), out_specs=[HBM] * (2 * m),
        out_shape=[pltpu.HBM(a.shape, a.dtype) for a in locs + lands],
        input_output_aliases={i: i for i in range(2 * m)},
        compiler_params=pltpu.CompilerParams(has_side_effects=EFFECT),
    )(*locs, *lands, send_sem, recv_sem, *after)
    locs, lands = outs[:m], outs[m:]

    def forward_body(*refs):
        sends, arrivals = _forward_copies(kinds, refs[:m], refs[2 * m:3 * m], *refs[3 * m:])
        for cp in sends:
            cp.start()
        for cp in arrivals:
            cp.wait_recv()
        for cp in sends:
            cp.wait_send()

    outs = pl.pallas_call(
        forward_body, name=f"gather_forward_{g}",
        in_specs=[ANY] * (2 * m), out_specs=[ANY] * m,
        out_shape=[jax.ShapeDtypeStruct(a.shape, a.dtype) for a in lands],
        input_output_aliases={m + i: i for i in range(m)},
        scratch_shapes=[pltpu.SemaphoreType.DMA((3 * m,)), pltpu.SemaphoreType.DMA((3 * m,))],
    )(*locs, *lands)
    return {nm: _natural(k, o) for nm, k, o in zip(names, kinds, outs)}


def _grad_view(kind, g):
    if kind == "col":
        return g.reshape(2, g.shape[0] // 2, g.shape[1])
    if kind == "row":
        return g.reshape(N_CHIPS, 2, g.shape[0] // (2 * N_CHIPS), g.shape[1])
    return g.reshape(GROUPS, N_CHIPS, 2, g.shape[1] // (2 * N_CHIPS), g.shape[2])


def _pair_copies(kinds, g, got, send_sems, recv_sems):
    x, y, c, _ = _place()

    def other_half(kind, ref):
        if kind == "col":
            return ref.at[1 - c]
        if kind == "row":
            return ref.at[:, 1 - c]
        return ref.at[:, :, 1 - c]

    return [pltpu.make_async_remote_copy(src_ref=other_half(kinds[a], g[a]), dst_ref=got[a], send_sem=send_sems.at[a],
                                         recv_sem=recv_sems.at[a], device_id=(x, y, 1 - c), device_id_type=MESH)
            for a in range(len(kinds))]


def _pair_exchange_start(tag, names, views):
    kinds = [KIND[nm] for nm in names]
    n = len(names)

    def got_shape(kind, v):
        if kind == "col":
            return v.shape[1:]
        if kind == "row":
            return (v.shape[0],) + v.shape[2:]
        return v.shape[:2] + v.shape[3:]

    srcs = [_hbm(views[nm]) for nm in names]
    lands = [_hbm(lax.empty(got_shape(k, views[nm]), BF16)) for nm, k in zip(names, kinds)]

    def body(*refs):
        g, got = refs[:n], refs[n:2 * n]
        for cp in _pair_copies(kinds, g, got, refs[2 * n], refs[2 * n + 1]):
            cp.start()
        refs[-1][...] = jnp.zeros(refs[-1].shape, F32)

    sem_shape = pltpu.SemaphoreType.DMA((n,))
    outs = pl.pallas_call(
        body, name=f"grad_pair_exchange_start_{tag}",
        in_specs=[HBM] * (2 * n),
        out_specs=[SEM, SEM] + [HBM] * (2 * n) + [pl.BlockSpec(memory_space=pltpu.VMEM)],
        out_shape=[sem_shape, sem_shape] + [pltpu.HBM(a.shape, a.dtype) for a in srcs + lands] + [jax.ShapeDtypeStruct((8, 128), F32)],
        input_output_aliases={i: 2 + i for i in range(2 * n)},
        compiler_params=pltpu.CompilerParams(has_side_effects=EFFECT),
    )(*srcs, *lands)
    return (outs[0], outs[1], list(outs[2:2 + n]), list(outs[2 + n:2 + 2 * n])), outs[-1]


def _pair_exchange_wait(tag, names, state, after):
    kinds = [KIND[nm] for nm in names]
    n = len(names)
    send_sem, recv_sem, srcs, lands = state

    def body(*refs):
        g, got = refs[:n], refs[n:2 * n]
        for cp in _pair_copies(kinds, g, got, refs[2 * n], refs[2 * n + 1]):
            cp.wait_send()
            cp.wait_recv()

    outs = pl.pallas_call(
        body, name=f"grad_pair_exchange_wait_{tag}",
        in_specs=[HBM] * (2 * n) + [SEM, SEM, ANY], out_specs=[HBM] * (2 * n),
        out_shape=[pltpu.HBM(a.shape, a.dtype) for a in srcs + lands],
        input_output_aliases={i: i for i in range(2 * n)},
        compiler_params=pltpu.CompilerParams(has_side_effects=EFFECT),
    )(*srcs, *lands, send_sem, recv_sem, after)
    return dict(zip(names, outs[:n])), dict(zip(names, outs[n:]))


def _pair_sum(name, kind, view, got, c_arr):
    if kind == "col":
        _, rows, cols = view.shape
        tr = 128
        grid = (rows // tr,)
        v_spec = pl.BlockSpec((None, tr, cols), lambda i, c: (c[0], i, 0))
        g_spec = pl.BlockSpec((tr, cols), lambda i, c: (i, 0))
    elif kind == "row":
        _, _, rows, cols = view.shape
        grid = (N_CHIPS,)
        v_spec = pl.BlockSpec((None, None, rows, cols), lambda i, c: (i, c[0], 0, 0))
        g_spec = pl.BlockSpec((None, rows, cols), lambda i, c: (i, 0, 0))
    else:
        _, _, _, rows, cols = view.shape
        grid = (GROUPS,)
        v_spec = pl.BlockSpec((None, N_CHIPS, None, rows, cols), lambda i, c: (i, 0, c[0], 0, 0))
        g_spec = pl.BlockSpec((None, N_CHIPS, rows, cols), lambda i, c: (i, 0, 0, 0))

    def body(c_ref, v_ref, g_ref, o_ref):
        o_ref[...] = (v_ref[...].astype(F32) + g_ref[...].astype(F32)).astype(BF16)

    return pl.pallas_call(
        body, name=name,
        grid_spec=pltpu.PrefetchScalarGridSpec(num_scalar_prefetch=1, grid=grid, in_specs=[v_spec, g_spec], out_specs=g_spec),
        out_shape=jax.ShapeDtypeStruct(got.shape, BF16),
        compiler_params=_cparams(("parallel",)),
    )(c_arr, view, got)


def _piece(kind, ref, s):
    if kind == "col":
        cs = ref.shape[1] // N_CHIPS
        return ref.at[:, pl.ds(pl.multiple_of(s * cs, 128), cs)]
    if kind == "row":
        return ref.at[s]
    return ref.at[:, s]


def _piece_shape(kind, shape):
    if kind == "col":
        return (shape[0], shape[1] // N_CHIPS)
    if kind == "row":
        return shape[1:]
    return (shape[0],) + shape[2:]


def _shard_copies(kinds, p, got, send_sems, recv_sems):
    x, y, c, chips = _place()
    return [pltpu.make_async_remote_copy(src_ref=_piece(kinds[a], p[a], 2 * px + py), dst_ref=got[a].at[j],
                                         send_sem=send_sems.at[3 * a + j], recv_sem=recv_sems.at[3 * a + j],
                                         device_id=(px, py, c), device_id_type=MESH)
            for a in range(len(kinds)) for j, (px, py) in enumerate(chips)]


def _shard_exchange_start(g, names, psums):
    kinds = [KIND[nm] for nm in names]
    n = len(names)
    srcs = [_hbm(psums[nm]) for nm in names]
    lands = [_hbm(lax.empty((3,) + _piece_shape(k, psums[nm].shape), BF16)) for nm, k in zip(names, kinds)]

    def body(*refs):
        p, got = refs[:n], refs[n:2 * n]
        send_sems, recv_sems = refs[2 * n], refs[2 * n + 1]
        token = refs[-1]
        for cp in _shard_copies(kinds, p, got, send_sems, recv_sems):
            cp.start()
        token[...] = jnp.zeros(token.shape, F32)

    sem_shape = pltpu.SemaphoreType.DMA((3 * n,))
    outs = pl.pallas_call(
        body, name=f"grad_shard_exchange_start_{g}",
        in_specs=[HBM] * (2 * n),
        out_specs=[SEM, SEM] + [HBM] * (2 * n) + [pl.BlockSpec(memory_space=pltpu.VMEM)],
        out_shape=[sem_shape, sem_shape] + [pltpu.HBM(a.shape, a.dtype) for a in srcs + lands] + [jax.ShapeDtypeStruct((8, 128), F32)],
        input_output_aliases={i: 2 + i for i in range(2 * n)},
        compiler_params=pltpu.CompilerParams(has_side_effects=EFFECT),
    )(*srcs, *lands)
    return (outs[0], outs[1], list(outs[2:2 + n]), list(outs[2 + n:2 + 2 * n])), outs[-1]


def _shard_exchange_wait(g, names, state, after):
    kinds = [KIND[nm] for nm in names]
    n = len(names)
    send_sem, recv_sem, srcs, lands = state

    def body(*refs):
        p, got = refs[:n], refs[n:2 * n]
        for cp in _shard_copies(kinds, p, got, refs[2 * n], refs[2 * n + 1]):
            cp.wait_send()
            cp.wait_recv()

    outs = pl.pallas_call(
        body, name=f"grad_shard_exchange_wait_{g}",
        in_specs=[HBM] * (2 * n) + [SEM, SEM] + [ANY] * len(after), out_specs=[HBM] * (2 * n),
        out_shape=[pltpu.HBM(a.shape, a.dtype) for a in srcs + lands],
        input_output_aliases={i: i for i in range(2 * n)},
        compiler_params=pltpu.CompilerParams(has_side_effects=EFFECT),
    )(*srcs, *lands, send_sem, recv_sem, *after)
    return dict(zip(names, outs[:n])), dict(zip(names, outs[n:]))


def _shard_sum(name, kind, psum, got, sc_arr):
    if kind == "col":
        rows, cols = psum.shape
        cs = cols // N_CHIPS
        tr = 128
        grid = (rows // tr,)
        p_spec = pl.BlockSpec((tr, cs), lambda i, sc: (i, sc[0]))
        g_spec = pl.BlockSpec((3, tr, cs), lambda i, sc: (0, i, 0))
        o_spec = pl.BlockSpec((None, tr, cs), lambda i, sc: (sc[1], i, 0))
        out_shape = (2, rows, cs)
    elif kind == "row":
        _, rows, cols = psum.shape
        grid = (1,)
        p_spec = pl.BlockSpec((None, rows, cols), lambda i, sc: (sc[0], 0, 0))
        g_spec = pl.BlockSpec((3, rows, cols), lambda i, sc: (0, 0, 0))
        o_spec = pl.BlockSpec((None, rows, cols), lambda i, sc: (sc[1], 0, 0))
        out_shape = (2, rows, cols)
    else:
        _, _, rows, cols = psum.shape
        grid = (1,)
        p_spec = pl.BlockSpec((GROUPS, None, rows, cols), lambda i, sc: (0, sc[0], 0, 0))
        g_spec = pl.BlockSpec((3, GROUPS, rows, cols), lambda i, sc: (0, 0, 0, 0))
        o_spec = pl.BlockSpec((GROUPS, None, rows, cols), lambda i, sc: (0, sc[1], 0, 0))
        out_shape = (GROUPS, 2, rows, cols)

    def body(sc_ref, p_ref, g_ref, o_ref):
        o_ref[...] = ((p_ref[...].astype(F32) + g_ref[0].astype(F32)) + g_ref[1].astype(F32)) + g_ref[2].astype(F32)

    return pl.pallas_call(
        body, name=name,
        grid_spec=pltpu.PrefetchScalarGridSpec(num_scalar_prefetch=1, grid=grid, in_specs=[p_spec, g_spec], out_specs=o_spec),
        out_shape=jax.ShapeDtypeStruct(out_shape, F32),
        compiler_params=_cparams(("parallel",)),
    )(sc_arr, psum, got)


def _half_copies(kinds, bufs, send_sems, recv_sems):
    x, y, c, _ = _place()

    def remote(a, half):
        part = _half(kinds[a], bufs[a], half)
        return pltpu.make_async_remote_copy(src_ref=part, dst_ref=part, send_sem=send_sems.at[a], recv_sem=recv_sems.at[a],
                                            device_id=(x, y, 1 - c), device_id_type=MESH)

    return [remote(a, c) for a in range(len(kinds))], [remote(a, 1 - c) for a in range(len(kinds))]


def _half_exchange_start(tag, names, bufs):
    kinds = [KIND[nm] for nm in names]
    n = len(names)
    arrs = [_hbm(bufs[nm]) for nm in names]

    def body(*refs):
        for cp in _half_copies(kinds, refs[:n], refs[n], refs[n + 1])[0]:
            cp.start()
        refs[-1][...] = jnp.zeros(refs[-1].shape, F32)

    sem_shape = pltpu.SemaphoreType.DMA((n,))
    outs = pl.pallas_call(
        body, name=f"grad_half_exchange_start_{tag}",
        in_specs=[HBM] * n,
        out_specs=[SEM, SEM] + [HBM] * n + [pl.BlockSpec(memory_space=pltpu.VMEM)],
        out_shape=[sem_shape, sem_shape] + [pltpu.HBM(a.shape, a.dtype) for a in arrs] + [jax.ShapeDtypeStruct((8, 128), F32)],
        input_output_aliases={i: 2 + i for i in range(n)},
        compiler_params=pltpu.CompilerParams(has_side_effects=EFFECT),
    )(*arrs)
    return (outs[0], outs[1], list(outs[2:2 + n])), outs[-1]


def _half_exchange_wait(tag, names, state, after):
    kinds = [KIND[nm] for nm in names]
    n = len(names)
    send_sem, recv_sem, arrs = state

    def body(*refs):
        sends, arrivals = _half_copies(kinds, refs[:n], refs[n], refs[n + 1])
        for cp in sends:
            cp.wait_send()
        for cp in arrivals:
            cp.wait_recv()

    outs = pl.pallas_call(
        body, name=f"grad_half_exchange_wait_{tag}",
        in_specs=[HBM] * n + [SEM, SEM] + [ANY] * len(after), out_specs=[HBM] * n,
        out_shape=[pltpu.HBM(a.shape, a.dtype) for a in arrs],
        input_output_aliases={i: i for i in range(n)},
        compiler_params=pltpu.CompilerParams(has_side_effects=EFFECT),
    )(*arrs, send_sem, recv_sem, *after)
    return dict(zip(names, outs))


N_DEV = 8
SMALL_ROWS = 8


def _all_reduce_small(name, v):
    def body(v_ref, o_ref, token, buf, send_sems, recv_sems):
        token[...] = jnp.zeros(token.shape, F32)
        x, y, c, _ = _place()
        me = 4 * x + 2 * y + c
        buf[me] = v_ref[...]
        cps = []
        for r in range(1, N_DEV):
            to = (x ^ (r >> 2), y ^ ((r >> 1) & 1), c ^ (r & 1))
            cp = pltpu.make_async_remote_copy(src_ref=v_ref, dst_ref=buf.at[me], send_sem=send_sems.at[r - 1],
                                              recv_sem=recv_sems.at[r - 1], device_id=to, device_id_type=MESH)
            cp.start()
            cps.append(cp)
        for r in range(1, N_DEV):
            pltpu.make_async_remote_copy(src_ref=v_ref, dst_ref=buf.at[me ^ r], send_sem=send_sems.at[r - 1],
                                         recv_sem=recv_sems.at[r - 1], device_id=(x, y, c), device_id_type=MESH).wait_recv()
        for cp in cps:
            cp.wait_send()
        acc = buf[0]
        for d in range(1, N_DEV):
            acc = acc + buf[d]
        o_ref[...] = acc

    vm = pl.BlockSpec(memory_space=pltpu.VMEM)
    return pl.pallas_call(
        body, name=name, in_specs=[vm], out_specs=[vm, vm],
        out_shape=[jax.ShapeDtypeStruct((SMALL_ROWS, D_MODEL), F32), jax.ShapeDtypeStruct((8, 128), F32)],
        scratch_shapes=[pltpu.VMEM((N_DEV, SMALL_ROWS, D_MODEL), F32), pltpu.SemaphoreType.DMA((N_DEV - 1,)),
                        pltpu.SemaphoreType.DMA((N_DEV - 1,))],
    )(v)


def _adamw(name, w, g, m, v, with_grad=False):
    rows, cols = w.shape
    tr = next((c for c in (256, 176, 128, 64, 32, 8) if rows % c == 0), rows)
    spec = pl.BlockSpec((tr, cols), lambda i: (i, 0))

    def body(w_ref, g_ref, m_ref, v_ref, d_ref, mo_ref, vo_ref, *go_ref):
        gv = g_ref[...]
        if with_grad:
            go_ref[0][...] = gv
        m_new = ADAM_B1 * m_ref[...] + (1.0 - ADAM_B1) * gv
        v_new = ADAM_B2 * v_ref[...] + (1.0 - ADAM_B2) * jnp.square(gv)
        m_hat = m_new / (1.0 - ADAM_B1 ** ADAM_STEP)
        v_hat = v_new / (1.0 - ADAM_B2 ** ADAM_STEP)
        d_ref[...] = -ADAM_LR * (m_hat / (jnp.sqrt(v_hat) + ADAM_EPS) + ADAM_WD * w_ref[...])
        mo_ref[...] = m_new
        vo_ref[...] = v_new

    return pl.pallas_call(
        body, name=name, grid=(rows // tr,),
        in_specs=[spec] * 4, out_specs=[spec] * (4 if with_grad else 3),
        out_shape=[jax.ShapeDtypeStruct((rows, cols), F32)] * (4 if with_grad else 3),
        compiler_params=_cparams(("parallel",)),
    )(w, g, m, v)


WEIGHTS = ("norm_ffn1", "ffn1_w_in", "ffn1_w_out", "norm_mix", "w_in", "gate_bias", "pool_w", "pool_scale", "w_ret_up",
           "w_pool_up", "w_out", "norm_ffn2", "ffn2_w_in", "ffn2_w_out", "norm_final")
SMALL_ROW = dict(norm_ffn1=0, norm_mix=1, gate_bias=2, pool_scale=4, norm_ffn2=5, norm_final=6)


def _as2d(a):
    return a.reshape(-1, a.shape[-1])


def kernel(x, norm_ffn1, ffn1_w_in, ffn1_w_out, norm_mix, w_in, gate_bias, pool_w, pool_scale, w_ret_up, w_pool_up, w_out, norm_ffn2, ffn2_w_in, ffn2_w_out, norm_final, loss_target, m_norm_ffn1, m_ffn1_w_in, m_ffn1_w_out, m_norm_mix, m_w_in, m_gate_bias, m_pool_w, m_pool_scale, m_w_ret_up, m_w_pool_up, m_w_out, m_norm_ffn2, m_ffn2_w_in, m_ffn2_w_out, m_norm_final, v_norm_ffn1, v_ffn1_w_in, v_ffn1_w_out, v_norm_mix, v_w_in, v_gate_bias, v_pool_w, v_pool_scale, v_w_ret_up, v_w_pool_up, v_w_out, v_norm_ffn2, v_ffn2_w_in, v_ffn2_w_out, v_norm_final):
    wt = dict(norm_ffn1=norm_ffn1, ffn1_w_in=ffn1_w_in, ffn1_w_out=ffn1_w_out, norm_mix=norm_mix, w_in=w_in, gate_bias=gate_bias,
              pool_w=pool_w, pool_scale=pool_scale, w_ret_up=w_ret_up, w_pool_up=w_pool_up, w_out=w_out, norm_ffn2=norm_ffn2,
              ffn2_w_in=ffn2_w_in, ffn2_w_out=ffn2_w_out, norm_final=norm_final)
    mom = dict(norm_ffn1=m_norm_ffn1, ffn1_w_in=m_ffn1_w_in, ffn1_w_out=m_ffn1_w_out, norm_mix=m_norm_mix, w_in=m_w_in,
               gate_bias=m_gate_bias, pool_w=m_pool_w, pool_scale=m_pool_scale, w_ret_up=m_w_ret_up, w_pool_up=m_w_pool_up,
               w_out=m_w_out, norm_ffn2=m_norm_ffn2, ffn2_w_in=m_ffn2_w_in, ffn2_w_out=m_ffn2_w_out, norm_final=m_norm_final)
    var = dict(norm_ffn1=v_norm_ffn1, ffn1_w_in=v_ffn1_w_in, ffn1_w_out=v_ffn1_w_out, norm_mix=v_norm_mix, w_in=v_w_in,
               gate_bias=v_gate_bias, pool_w=v_pool_w, pool_scale=v_pool_scale, w_ret_up=v_w_ret_up, w_pool_up=v_w_pool_up,
               w_out=v_w_out, norm_ffn2=v_norm_ffn2, ffn2_w_in=v_ffn2_w_in, ffn2_w_out=v_ffn2_w_out, norm_final=v_norm_final)

    ax, ay, ac = lax.axis_index("x"), lax.axis_index("y"), lax.axis_index("c")
    chip = 2 * ax + ay
    c_arr = jnp.reshape(ac, (1,)).astype(jnp.int32)
    sc_arr = jnp.stack([chip, ac]).astype(jnp.int32)
    bias_cols = gate_bias.shape[-1]

    first = {"gate_bias": gate_bias[0], "ffn1_w_in": ffn1_w_in[0].astype(BF16)}
    gather_groups, token = _gather_start("first", [0], first)
    rest, rest_token = _gather_start("rest", [1, 2, 3, 4],
                                     {nm: wt[nm][0].astype(BF16) + token[0, 0].astype(BF16) for nm in BIG if nm not in first})
    gather_groups.update(rest)
    vec = dict(norm_ffn1=norm_ffn1, norm_mix=norm_mix, norm_ffn2=norm_ffn2, pool_scale=pool_scale,
               norm_final=norm_final.reshape(1, D_MODEL))

    relayed = {}

    def relay_w(g, after):
        relayed[g] = _gather_relay(g, gather_groups[g], (after,))

    def get_w(g, after):
        if g in relayed:
            return _gather_land(g, relayed[g], (after,))
        return _gather_finish(g, gather_groups[g], (after, rest_token) if g == 0 else (after,))

    pairs, pending = [], []

    def on_grads(gr):
        g = len(pairs)
        names = GRAD_GROUPS[g]
        assert set(names) == set(gr), (names, list(gr))
        state, token = _pair_exchange_start(g, names, {nm: _grad_view(KIND[nm], gr[nm]) for nm in names})
        pairs.append(state)
        return token[0:1, 0:1]

    def flush(after):
        g = len(pending)
        names = GRAD_GROUPS[g]
        views, from_sib = _pair_exchange_wait(g, names, pairs[g], after)
        psums = {nm: _pair_sum(f"pair_sum_{nm}", KIND[nm], views[nm], from_sib[nm], c_arr) for nm in names}
        state, token = _shard_exchange_start(g, names, psums)
        pending.append(state)
        tokens.append(token)
        return token[0:1, 0:1]

    tokens = []
    loss_local, dx, small = _local_step(x[0], loss_target[0], vec, get_w, relay_w, on_grads, flush)

    grads, delta, new_m, new_v = {}, {}, {}, {}

    def adamw(nm):
        shape = wt[nm].shape
        outs = _adamw(f"adamw_{nm}", _as2d(wt[nm]), _as2d(grads[nm]), _as2d(mom[nm]), _as2d(var[nm]), with_grad=nm in BIG)
        delta[nm], new_m[nm], new_v[nm] = (o.reshape(shape) for o in outs[:3])
        if nm in BIG:
            grads[nm] = outs[3].reshape(shape)
        return outs[0]

    def reduce_start(g, after):
        names = GRAD_GROUPS[g]
        psums, from_chips = _shard_exchange_wait(g, names, pending[g], after)
        bufs = {nm: _shard_sum(f"shard_sum_{nm}", KIND[nm], psums[nm], from_chips[nm], sc_arr) for nm in names}
        return _half_exchange_start(g, names, bufs)

    def reduce_finish(g, state, after):
        names = GRAD_GROUPS[g]
        reduced = _half_exchange_wait(g, names, state, after)
        for nm in names:
            grads[nm] = reduced[nm].reshape(wt[nm].shape)
        return tuple(adamw(nm) for nm in names)

    swap0, token = reduce_start(0, (tokens[-1],))
    swap1, token = reduce_start(1, (token,))
    done = reduce_finish(0, swap0, (token,))
    done = reduce_finish(1, swap1, done)
    swap2, token = reduce_start(2, done)
    packed = jnp.concatenate([small["norm_ffn1"], small["norm_mix"], small["gate_bias"], small["pool_scale"],
                              small["norm_ffn2"], small["norm_final"], jnp.broadcast_to(loss_local, (1, D_MODEL))], axis=0)
    small_sum, _ = _all_reduce_small("reduce_small_grads", packed + token[0, 0])
    loss = small_sum[SMALL_ROWS - 1, 0]
    for nm in ("norm_ffn1", "norm_mix", "pool_scale", "norm_ffn2"):
        grads[nm] = small_sum[SMALL_ROW[nm]][None, :]
    grads["norm_final"] = small_sum[SMALL_ROW["norm_final"]]
    grads["gate_bias"] = lax.dynamic_slice(small_sum, (SMALL_ROW["gate_bias"], chip * bias_cols), (2, bias_cols))[None]
    reduce_finish(2, swap2, (small_sum,))
    for nm in WEIGHTS:
        if nm not in delta:
            adamw(nm)

    return (loss, dx[None], *[grads[nm] for nm in WEIGHTS], *[delta[nm] for nm in WEIGHTS],
            *[new_m[nm] for nm in WEIGHTS], *[new_v[nm] for nm in WEIGHTS])
```
